```python
import jax, jax.numpy as jnp
from jax import lax
import numpy as np

D_MODEL = 1024
BATCH = 8
SEQ = 4096
DEPTH = 2

EPS = 1e-6
ROPE_THETA = 10000.0
Q_BLOCK = 128

LRU_WIDTH = 1024
LRU_BLOCKS = 8
LRU_BLOCK_W = LRU_WIDTH // LRU_BLOCKS
CONV_WIDTH = 4
LRU_C = 8.0

MLA_HEADS = 8
MLA_NOPE = 64
MLA_ROPE = 32
MLA_QK = MLA_NOPE + MLA_ROPE
MLA_V = 64
Q_LORA = 256
KV_LORA = 128
MLA_WIDTH = MLA_HEADS * MLA_V

DIL_GROUPS = ((128, 1), (512, 4), (2048, 16))
DIL_HEADS = 8
DIL_HD = 64
DIL_QKV = len(DIL_GROUPS) * DIL_HEADS * DIL_HD
DIL_WIDTH = DIL_HEADS * DIL_HD

N_BRANCH = 3
SPLITS = (LRU_WIDTH, LRU_WIDTH, Q_LORA, KV_LORA, MLA_ROPE, MLA_WIDTH,
          DIL_QKV, DIL_QKV, DIL_QKV, DIL_WIDTH, N_BRANCH * D_MODEL)
IN_WIDTH = (2 * LRU_WIDTH + Q_LORA + KV_LORA + MLA_ROPE + MLA_WIDTH
            + 3 * DIL_QKV + DIL_WIDTH + N_BRANCH * D_MODEL)

kernel_name = 'hybrid_rglru_mla_dilated_swa'


def rms_norm(x, g):
    xf = x.astype(jnp.float32)
    y = xf * lax.rsqrt(jnp.mean(xf * xf, axis=-1, keepdims=True) + EPS)
    return (y * g.astype(jnp.float32)).astype(x.dtype)


def rotary(x, pos):
    d = x.shape[-1]
    inv = ROPE_THETA ** (-jnp.arange(0, d, 2, dtype=jnp.float32) / d)
    ang = pos.astype(jnp.float32)[..., None] * inv
    cos = jnp.cos(ang)[:, :, None, :]
    sin = jnp.sin(ang)[:, :, None, :]
    xf = x.astype(jnp.float32)
    x1, x2 = xf[..., : d // 2], xf[..., d // 2:]
    return jnp.concatenate([x1 * cos - x2 * sin, x2 * cos + x1 * sin], axis=-1).astype(x.dtype)


def rg_lru_branch(xb, conv_w, conv_b, w_gx, b_gx, w_ga, b_ga, lam):
    B, S, _ = xb.shape
    xp = jnp.pad(xb, ((0, 0), (CONV_WIDTH - 1, 0), (0, 0)))
    xc = conv_b
    for k in range(CONV_WIDTH):
        xc = xc + xp[:, k:k + S] * conv_w[k]
    xblk = xc.reshape(B, S, LRU_BLOCKS, LRU_BLOCK_W)
    gx = jax.nn.sigmoid(jnp.einsum('bsnc,ncd->bsnd', xblk, w_gx) + b_gx).reshape(B, S, LRU_WIDTH)
    ga = jax.nn.sigmoid(jnp.einsum('bsnc,ncd->bsnd', xblk, w_ga) + b_ga).reshape(B, S, LRU_WIDTH)
    log_a = -LRU_C * ga.astype(jnp.float32) * jax.nn.softplus(-lam.astype(jnp.float32))
    a = jnp.exp(log_a)
    mult = jnp.sqrt(-jnp.expm1(2.0 * log_a))
    b = mult * (gx * xc).astype(jnp.float32)

    def combine(left, right):
        a1, b1 = left
        a2, b2 = right
        return a1 * a2, a2 * b1 + b2

    _, h = lax.associative_scan(combine, (a, b), axis=1)
    return h.astype(xb.dtype)


def causal_attention(q, k, v, scale):
    B, S, H, dk = q.shape
    nb = S // Q_BLOCK
    qb = q.reshape(B, nb, Q_BLOCK, H, dk).transpose(1, 0, 3, 2, 4)
    kt = k.transpose(0, 2, 1, 3)
    vt = v.transpose(0, 2, 1, 3)
    kpos = jnp.arange(S)

    def one_block(args):
        qi, i = args
        s = jnp.einsum('bhqd,bhkd->bhqk', qi, kt).astype(jnp.float32) * scale
        qpos = i * Q_BLOCK + jnp.arange(Q_BLOCK)
        s = jnp.where(kpos[None, :] <= qpos[:, None], s, -jnp.inf)
        p = jax.nn.softmax(s, axis=-1)
        return jnp.einsum('bhqk,bhkd->bhqd', p.astype(vt.dtype), vt)

    o = lax.map(one_block, (qb, jnp.arange(nb)))
    return o.transpose(1, 0, 3, 2, 4).reshape(B, S, H, -1)


def mla_branch(cq, ckv, kr, pos, g_cq, g_ckv, w_uq, w_ukv, g_qn, g_kn):
    B, S, _ = cq.shape
    q = (rms_norm(cq, g_cq) @ w_uq).reshape(B, S, MLA_HEADS, MLA_QK)
    kv = (rms_norm(ckv, g_ckv) @ w_ukv).reshape(B, S, MLA_HEADS, MLA_NOPE + MLA_V)
    k_nope, v = kv[..., :MLA_NOPE], kv[..., MLA_NOPE:]
    k_rope = jnp.broadcast_to(kr[:, :, None, :], (B, S, MLA_HEADS, MLA_ROPE))
    k = jnp.concatenate([k_nope, k_rope], axis=-1)
    q = rms_norm(q, g_qn)
    k = rms_norm(k, g_kn)
    q = jnp.concatenate([q[..., :MLA_NOPE], rotary(q[..., MLA_NOPE:], pos)], axis=-1)
    k = jnp.concatenate([k[..., :MLA_NOPE], rotary(k[..., MLA_NOPE:], pos)], axis=-1)
    o = causal_attention(q, k, v, MLA_QK ** -0.5)
    return o.reshape(B, S, MLA_WIDTH)


def dilated_group(q, k, v, window, dilation):
    B, S, H, d = q.shape
    nk = window // dilation
    span = dilation * nk
    s_pad = -(-S // span) * span
    M = s_pad // dilation
    nb = M // nk

    def to_strided(t):
        t = jnp.pad(t, ((0, 0), (0, s_pad - S), (0, 0), (0, 0))).reshape(B, M, dilation, H, d)
        return t.transpose(0, 2, 3, 1, 4).reshape(B, dilation, H, nb, nk, d)

    qb, kb, vb = to_strided(q), to_strided(k), to_strided(v)

    def prev(t):
        return jnp.pad(t, ((0, 0), (0, 0), (0, 0), (1, 0), (0, 0), (0, 0)))[:, :, :, :-1]

    kw = jnp.concatenate([prev(kb), kb], axis=4)
    vw = jnp.concatenate([prev(vb), vb], axis=4)
    s = jnp.einsum('brhnqd,brhnkd->brhnqk', qb, kw).astype(jnp.float32) * (DIL_HD ** -0.5)
    qi = jnp.arange(nk)[:, None]
    ki = jnp.arange(2 * nk)[None, :]
    band = (ki >= qi) & (ki <= qi + nk)
    not_first = jnp.arange(nb)[:, None, None] > 0
    mask = band[None] & (not_first | (ki >= nk)[None])
    s = jnp.where(mask, s, -jnp.inf)
    m = jnp.max(s, axis=-1, keepdims=True)
    e = jnp.exp(s - m)
    den = jnp.sum(e, axis=-1, keepdims=True)
    o = jnp.einsum('brhnqk,brhnkd->brhnqd', (e / den).astype(vw.dtype), vw)
    lse = (m + jnp.log(den))[..., 0]
    o = o.reshape(B, dilation, H, M, d).transpose(0, 3, 1, 2, 4).reshape(B, s_pad, H, d)[:, :S]
    lse = lse.reshape(B, dilation, H, M).transpose(0, 3, 1, 2).reshape(B, s_pad, H)[:, :S]
    return o, lse


def dilated_branch(q, k, v, pos, g_qn, g_kn):
    B, S, _ = q.shape
    nh = len(DIL_GROUPS) * DIL_HEADS
    q = rotary(rms_norm(q.reshape(B, S, nh, DIL_HD), g_qn), pos)
    k = rotary(rms_norm(k.reshape(B, S, nh, DIL_HD), g_kn), pos)
    v = v.reshape(B, S, nh, DIL_HD)
    outs, lses = [], []
    for gi, (window, dilation) in enumerate(DIL_GROUPS):
        sl = slice(gi * DIL_HEADS, (gi + 1) * DIL_HEADS)
        o, l = dilated_group(q[:, :, sl], k[:, :, sl], v[:, :, sl], window, dilation)
        outs.append(o)
        lses.append(l)
    wts = jax.nn.softmax(jnp.stack(lses, axis=0), axis=0)
    o = jnp.sum(wts[..., None].astype(v.dtype) * jnp.stack(outs, axis=0), axis=0)
    return o.reshape(B, S, DIL_WIDTH)


def hybrid_layer(x, pos, norm_g, w_in, conv_w, conv_b, w_gx, b_gx, w_ga, b_ga, lam, w_lru_o,
                 g_cq, g_ckv, w_uq, w_ukv, g_mqn, g_mkn, w_mla_o, g_dqn, g_dkn, w_dil_o,
                 b_merge, w_out):
    B, S, _ = x.shape
    h = rms_norm(x, norm_g)
    z = h @ w_in
    idx = np.cumsum(SPLITS)[:-1].tolist()
    (lru_x, lru_g, cq, ckv, kr, mla_g, dq, dk, dv, dil_g, merge) = jnp.split(z, idx, axis=-1)
    y_lru = rg_lru_branch(lru_x, conv_w, conv_b, w_gx, b_gx, w_ga, b_ga, lam) * jax.nn.silu(lru_g)
    y_mla = mla_branch(cq, ckv, kr, pos, g_cq, g_ckv, w_uq, w_ukv, g_mqn, g_mkn) * jax.nn.silu(mla_g)
    y_dil = dilated_branch(dq, dk, dv, pos, g_dqn, g_dkn) * jax.nn.silu(dil_g)
    gates = jax.nn.sigmoid(merge + b_merge).reshape(B, S, N_BRANCH, D_MODEL)
    merged = (gates[:, :, 0] * (y_lru @ w_lru_o)
              + gates[:, :, 1] * (y_mla @ w_mla_o)
              + gates[:, :, 2] * (y_dil @ w_dil_o))
    return x + merged @ w_out


def _fwd_setup_inputs(seed: int = 0) -> dict:
    key = jax.random.key(seed)
    ks = jax.random.split(key, 24)

    def nrm(k, shape, scale):
        return jax.random.normal(k, shape, jnp.float32) * scale

    def gain(k, shape):
        return 1.0 + 0.05 * jax.random.normal(k, shape, jnp.float32)

    x = nrm(ks[0], (BATCH, SEQ, D_MODEL), 1.0)
    offsets = jax.random.randint(ks[1], (BATCH, 1), 0, 1024, dtype=jnp.int32)
    positions = offsets + jnp.arange(SEQ, dtype=jnp.int32)[None, :]
    a0 = jax.random.uniform(ks[8], (DEPTH, LRU_WIDTH), jnp.float32, 0.9, 0.999)
    return {
        'x': x,
        'positions': positions,
        'norm_g': gain(ks[2], (DEPTH, D_MODEL)),
        'w_in': nrm(ks[3], (DEPTH, D_MODEL, IN_WIDTH), D_MODEL ** -0.5),
        'conv_w': nrm(ks[4], (DEPTH, CONV_WIDTH, LRU_WIDTH), CONV_WIDTH ** -0.5),
        'conv_b': nrm(ks[5], (DEPTH, LRU_WIDTH), 0.02),
        'w_gate_x': nrm(ks[6], (DEPTH, LRU_BLOCKS, LRU_BLOCK_W, LRU_BLOCK_W), LRU_BLOCK_W ** -0.5),
        'b_gate_x': nrm(ks[7], (DEPTH, LRU_BLOCKS, LRU_BLOCK_W), 0.1),
        'w_gate_a': nrm(ks[9], (DEPTH, LRU_BLOCKS, LRU_BLOCK_W, LRU_BLOCK_W), LRU_BLOCK_W ** -0.5),
        'b_gate_a': nrm(ks[10], (DEPTH, LRU_BLOCKS, LRU_BLOCK_W), 0.1),
        'lru_lambda': jnp.log(a0) - jnp.log1p(-a0),
        'w_lru_o': nrm(ks[11], (DEPTH, LRU_WIDTH, D_MODEL), LRU_WIDTH ** -0.5),
        'cq_norm_g': gain(ks[12], (DEPTH, Q_LORA)),
        'ckv_norm_g': gain(ks[13], (DEPTH, KV_LORA)),
        'w_uq': nrm(ks[14], (DEPTH, Q_LORA, MLA_HEADS * MLA_QK), Q_LORA ** -0.5),
        'w_ukv': nrm(ks[15], (DEPTH, KV_LORA, MLA_HEADS * (MLA_NOPE + MLA_V)), KV_LORA ** -0.5),
        'mla_q_norm_g': gain(ks[16], (DEPTH, MLA_QK)),
        'mla_k_norm_g': gain(ks[17], (DEPTH, MLA_QK)),
        'w_mla_o': nrm(ks[18], (DEPTH, MLA_WIDTH, D_MODEL), MLA_WIDTH ** -0.5),
        'dil_q_norm_g': gain(ks[19], (DEPTH, DIL_HD)),
        'dil_k_norm_g': gain(ks[20], (DEPTH, DIL_HD)),
        'w_dil_o': nrm(ks[21], (DEPTH, DIL_WIDTH, D_MODEL), DIL_WIDTH ** -0.5),
        'b_merge': nrm(ks[22], (DEPTH, N_BRANCH * D_MODEL), 0.1),
        'w_out': nrm(ks[23], (DEPTH, D_MODEL, D_MODEL), D_MODEL ** -0.5),
    }


def _fwd_reference(x, positions, norm_g, w_in, conv_w, conv_b, w_gate_x, b_gate_x, w_gate_a, b_gate_a,
              lru_lambda, w_lru_o, cq_norm_g, ckv_norm_g, w_uq, w_ukv, mla_q_norm_g, mla_k_norm_g,
              w_mla_o, dil_q_norm_g, dil_k_norm_g, w_dil_o, b_merge, w_out):
    for l in range(DEPTH):
        x = hybrid_layer(x, positions, norm_g[l], w_in[l], conv_w[l], conv_b[l],
                         w_gate_x[l], b_gate_x[l], w_gate_a[l], b_gate_a[l], lru_lambda[l],
                         w_lru_o[l], cq_norm_g[l], ckv_norm_g[l], w_uq[l], w_ukv[l],
                         mla_q_norm_g[l], mla_k_norm_g[l], w_mla_o[l], dil_q_norm_g[l],
                         dil_k_norm_g[l], w_dil_o[l], b_merge[l], w_out[l])
    return x


import jax as _jax
import jax.numpy as _jnp

TWIN_FORMAT = 'train_step'
FWD_PARAMS = ['x', 'positions', 'norm_g', 'w_in', 'conv_w', 'conv_b', 'w_gate_x', 'b_gate_x', 'w_gate_a', 'b_gate_a', 'lru_lambda', 'w_lru_o', 'cq_norm_g', 'ckv_norm_g', 'w_uq', 'w_ukv', 'mla_q_norm_g', 'mla_k_norm_g', 'w_mla_o', 'dil_q_norm_g', 'dil_k_norm_g', 'w_dil_o', 'b_merge', 'w_out']
TWIN_WEIGHTS = ['norm_g', 'w_in', 'conv_w', 'conv_b', 'w_gate_x', 'b_gate_x', 'w_gate_a', 'b_gate_a', 'lru_lambda', 'w_lru_o', 'cq_norm_g', 'ckv_norm_g', 'w_uq', 'w_ukv', 'mla_q_norm_g', 'mla_k_norm_g', 'w_mla_o', 'dil_q_norm_g', 'dil_k_norm_g', 'w_dil_o', 'b_merge', 'w_out']
TWIN_DIFF_INPUT = 'x'
TWIN_INPUTS = ['x', 'positions', 'norm_g', 'w_in', 'conv_w', 'conv_b', 'w_gate_x', 'b_gate_x', 'w_gate_a', 'b_gate_a', 'lru_lambda', 'w_lru_o', 'cq_norm_g', 'ckv_norm_g', 'w_uq', 'w_ukv', 'mla_q_norm_g', 'mla_k_norm_g', 'w_mla_o', 'dil_q_norm_g', 'dil_k_norm_g', 'w_dil_o', 'b_merge', 'w_out', 'loss_target', 'm_norm_g', 'm_w_in', 'm_conv_w', 'm_conv_b', 'm_w_gate_x', 'm_b_gate_x', 'm_w_gate_a', 'm_b_gate_a', 'm_lru_lambda', 'm_w_lru_o', 'm_cq_norm_g', 'm_ckv_norm_g', 'm_w_uq', 'm_w_ukv', 'm_mla_q_norm_g', 'm_mla_k_norm_g', 'm_w_mla_o', 'm_dil_q_norm_g', 'm_dil_k_norm_g', 'm_w_dil_o', 'm_b_merge', 'm_w_out', 'v_norm_g', 'v_w_in', 'v_conv_w', 'v_conv_b', 'v_w_gate_x', 'v_b_gate_x', 'v_w_gate_a', 'v_b_gate_a', 'v_lru_lambda', 'v_w_lru_o', 'v_cq_norm_g', 'v_ckv_norm_g', 'v_w_uq', 'v_w_ukv', 'v_mla_q_norm_g', 'v_mla_k_norm_g', 'v_w_mla_o', 'v_dil_q_norm_g', 'v_dil_k_norm_g', 'v_w_dil_o', 'v_b_merge', 'v_w_out']
TWIN_OUTPUTS = ['loss', 'grad_x', 'grad_norm_g', 'grad_w_in', 'grad_conv_w', 'grad_conv_b', 'grad_w_gate_x', 'grad_b_gate_x', 'grad_w_gate_a', 'grad_b_gate_a', 'grad_lru_lambda', 'grad_w_lru_o', 'grad_cq_norm_g', 'grad_ckv_norm_g', 'grad_w_uq', 'grad_w_ukv', 'grad_mla_q_norm_g', 'grad_mla_k_norm_g', 'grad_w_mla_o', 'grad_dil_q_norm_g', 'grad_dil_k_norm_g', 'grad_w_dil_o', 'grad_b_merge', 'grad_w_out', 'delta_norm_g', 'delta_w_in', 'delta_conv_w', 'delta_conv_b', 'delta_w_gate_x', 'delta_b_gate_x', 'delta_w_gate_a', 'delta_b_gate_a', 'delta_lru_lambda', 'delta_w_lru_o', 'delta_cq_norm_g', 'delta_ckv_norm_g', 'delta_w_uq', 'delta_w_ukv', 'delta_mla_q_norm_g', 'delta_mla_k_norm_g', 'delta_w_mla_o', 'delta_dil_q_norm_g', 'delta_dil_k_norm_g', 'delta_w_dil_o', 'delta_b_merge', 'delta_w_out', 'new_m_norm_g', 'new_m_w_in', 'new_m_conv_w', 'new_m_conv_b', 'new_m_w_gate_x', 'new_m_b_gate_x', 'new_m_w_gate_a', 'new_m_b_gate_a', 'new_m_lru_lambda', 'new_m_w_lru_o', 'new_m_cq_norm_g', 'new_m_ckv_norm_g', 'new_m_w_uq', 'new_m_w_ukv', 'new_m_mla_q_norm_g', 'new_m_mla_k_norm_g', 'new_m_w_mla_o', 'new_m_dil_q_norm_g', 'new_m_dil_k_norm_g', 'new_m_w_dil_o', 'new_m_b_merge', 'new_m_w_out', 'new_v_norm_g', 'new_v_w_in', 'new_v_conv_w', 'new_v_conv_b', 'new_v_w_gate_x', 'new_v_b_gate_x', 'new_v_w_gate_a', 'new_v_b_gate_a', 'new_v_lru_lambda', 'new_v_w_lru_o', 'new_v_cq_norm_g', 'new_v_ckv_norm_g', 'new_v_w_uq', 'new_v_w_ukv', 'new_v_mla_q_norm_g', 'new_v_mla_k_norm_g', 'new_v_w_mla_o', 'new_v_dil_q_norm_g', 'new_v_dil_k_norm_g', 'new_v_w_dil_o', 'new_v_b_merge', 'new_v_w_out']
TWIN_LEAF_KINDS = {'loss': 'loss', 'grad_x': 'grad_x', 'grad_norm_g': 'grad_w', 'grad_w_in': 'grad_w', 'grad_conv_w': 'grad_w', 'grad_conv_b': 'grad_w', 'grad_w_gate_x': 'grad_w', 'grad_b_gate_x': 'grad_w', 'grad_w_gate_a': 'grad_w', 'grad_b_gate_a': 'grad_w', 'grad_lru_lambda': 'grad_w', 'grad_w_lru_o': 'grad_w', 'grad_cq_norm_g': 'grad_w', 'grad_ckv_norm_g': 'grad_w', 'grad_w_uq': 'grad_w', 'grad_w_ukv': 'grad_w', 'grad_mla_q_norm_g': 'grad_w', 'grad_mla_k_norm_g': 'grad_w', 'grad_w_mla_o': 'grad_w', 'grad_dil_q_norm_g': 'grad_w', 'grad_dil_k_norm_g': 'grad_w', 'grad_w_dil_o': 'grad_w', 'grad_b_merge': 'grad_w', 'grad_w_out': 'grad_w', 'delta_norm_g': 'delta_w', 'delta_w_in': 'delta_w', 'delta_conv_w': 'delta_w', 'delta_conv_b': 'delta_w', 'delta_w_gate_x': 'delta_w', 'delta_b_gate_x': 'delta_w', 'delta_w_gate_a': 'delta_w', 'delta_b_gate_a': 'delta_w', 'delta_lru_lambda': 'delta_w', 'delta_w_lru_o': 'delta_w', 'delta_cq_norm_g': 'delta_w', 'delta_ckv_norm_g': 'delta_w', 'delta_w_uq': 'delta_w', 'delta_w_ukv': 'delta_w', 'delta_mla_q_norm_g': 'delta_w', 'delta_mla_k_norm_g': 'delta_w', 'delta_w_mla_o': 'delta_w', 'delta_dil_q_norm_g': 'delta_w', 'delta_dil_k_norm_g': 'delta_w', 'delta_w_dil_o': 'delta_w', 'delta_b_merge': 'delta_w', 'delta_w_out': 'delta_w', 'new_m_norm_g': 'new_m', 'new_m_w_in': 'new_m', 'new_m_conv_w': 'new_m', 'new_m_conv_b': 'new_m', 'new_m_w_gate_x': 'new_m', 'new_m_b_gate_x': 'new_m', 'new_m_w_gate_a': 'new_m', 'new_m_b_gate_a': 'new_m', 'new_m_lru_lambda': 'new_m', 'new_m_w_lru_o': 'new_m', 'new_m_cq_norm_g': 'new_m', 'new_m_ckv_norm_g': 'new_m', 'new_m_w_uq': 'new_m', 'new_m_w_ukv': 'new_m', 'new_m_mla_q_norm_g': 'new_m', 'new_m_mla_k_norm_g': 'new_m', 'new_m_w_mla_o': 'new_m', 'new_m_dil_q_norm_g': 'new_m', 'new_m_dil_k_norm_g': 'new_m', 'new_m_w_dil_o': 'new_m', 'new_m_b_merge': 'new_m', 'new_m_w_out': 'new_m', 'new_v_norm_g': 'new_v', 'new_v_w_in': 'new_v', 'new_v_conv_w': 'new_v', 'new_v_conv_b': 'new_v', 'new_v_w_gate_x': 'new_v', 'new_v_b_gate_x': 'new_v', 'new_v_w_gate_a': 'new_v', 'new_v_b_gate_a': 'new_v', 'new_v_lru_lambda': 'new_v', 'new_v_w_lru_o': 'new_v', 'new_v_cq_norm_g': 'new_v', 'new_v_ckv_norm_g': 'new_v', 'new_v_w_uq': 'new_v', 'new_v_w_ukv': 'new_v', 'new_v_mla_q_norm_g': 'new_v', 'new_v_mla_k_norm_g': 'new_v', 'new_v_w_mla_o': 'new_v', 'new_v_dil_q_norm_g': 'new_v', 'new_v_dil_k_norm_g': 'new_v', 'new_v_w_dil_o': 'new_v', 'new_v_b_merge': 'new_v', 'new_v_w_out': 'new_v'}


def _forward(args):
    return _fwd_reference(*[args[k] for k in FWD_PARAMS])


def _output_shape():
    out = _jax.eval_shape(lambda: _forward(_fwd_setup_inputs(0)))
    return out.shape, out.dtype

N_MICROBATCH = 1
ADAM_LR = 0.001
ADAM_B1 = 0.9
ADAM_B2 = 0.999
ADAM_EPS = 1e-08
ADAM_WD = 0.01
ADAM_STEP = 10
PER_EXAMPLE_BATCH_AXIS = {'x': 0, 'positions': 0, 'loss_target': 0}
SHARED_INPUTS = []
_WEIGHT_DTYPES = {'norm_g': _jnp.float32, 'w_in': _jnp.float32, 'conv_w': _jnp.float32, 'conv_b': _jnp.float32, 'w_gate_x': _jnp.float32, 'b_gate_x': _jnp.float32, 'w_gate_a': _jnp.float32, 'b_gate_a': _jnp.float32, 'lru_lambda': _jnp.float32, 'w_lru_o': _jnp.float32, 'cq_norm_g': _jnp.float32, 'ckv_norm_g': _jnp.float32, 'w_uq': _jnp.float32, 'w_ukv': _jnp.float32, 'mla_q_norm_g': _jnp.float32, 'mla_k_norm_g': _jnp.float32, 'w_mla_o': _jnp.float32, 'dil_q_norm_g': _jnp.float32, 'dil_k_norm_g': _jnp.float32, 'w_dil_o': _jnp.float32, 'b_merge': _jnp.float32, 'w_out': _jnp.float32}
MOMENT_SCALE = {'norm_g': 2.440179e+00, 'w_in': 3.193535e-02, 'conv_w': 8.294229e-01, 'conv_b': 2.126310e+00, 'w_gate_x': 1.515123e-01, 'b_gate_x': 5.263750e-01, 'w_gate_a': 7.713037e-02, 'b_gate_a': 1.033601e-01, 'lru_lambda': 1.972960e-01, 'w_lru_o': 6.831211e-02, 'cq_norm_g': 3.619170e-02, 'ckv_norm_g': 2.740137e-01, 'w_uq': 2.045230e-02, 'w_ukv': 2.796125e-02, 'mla_q_norm_g': 1.996576e-01, 'mla_k_norm_g': 2.011859e-01, 'w_mla_o': 2.116827e-02, 'dil_q_norm_g': 2.265674e-01, 'dil_k_norm_g': 2.264977e-01, 'w_dil_o': 1.725720e-02, 'b_merge': 2.106510e-01, 'w_out': 5.995729e-02}


def _to_microbatches(a, axis):
    t = _jnp.moveaxis(a, axis, 0)
    t = t.reshape((N_MICROBATCH, t.shape[0] // N_MICROBATCH) + t.shape[1:])
    return _jnp.moveaxis(t, 1, axis + 1)


def setup_inputs(seed: int = 0) -> dict:
    inp = _fwd_setup_inputs(seed)
    key = _jax.random.fold_in(_jax.random.key(seed), 7919)
    shape, _ = _output_shape()
    out = dict(inp)
    out["loss_target"] = _jax.random.normal(_jax.random.fold_in(key, 0), shape, _jnp.float32)
    for i, name in enumerate(TWIN_WEIGHTS):
        w = inp[name].astype(_jnp.float32)
        if MOMENT_SCALE is None:
            s = _jnp.sqrt(_jnp.mean(_jnp.square(w)) + 1e-30)
        else:
            s = MOMENT_SCALE[name]
        km, kv = _jax.random.split(_jax.random.fold_in(key, i + 1))
        out[name] = w
        out["m_" + name] = s * _jax.random.normal(km, w.shape, _jnp.float32)
        out["v_" + name] = (s * s) * _jax.random.uniform(kv, w.shape, _jnp.float32, 0.5, 1.5)
    if N_MICROBATCH > 1:
        for name, axis in PER_EXAMPLE_BATCH_AXIS.items():
            out[name] = _to_microbatches(out[name], axis)
    return {'x': out['x'], 'positions': out['positions'], 'norm_g': out['norm_g'], 'w_in': out['w_in'], 'conv_w': out['conv_w'], 'conv_b': out['conv_b'], 'w_gate_x': out['w_gate_x'], 'b_gate_x': out['b_gate_x'], 'w_gate_a': out['w_gate_a'], 'b_gate_a': out['b_gate_a'], 'lru_lambda': out['lru_lambda'], 'w_lru_o': out['w_lru_o'], 'cq_norm_g': out['cq_norm_g'], 'ckv_norm_g': out['ckv_norm_g'], 'w_uq': out['w_uq'], 'w_ukv': out['w_ukv'], 'mla_q_norm_g': out['mla_q_norm_g'], 'mla_k_norm_g': out['mla_k_norm_g'], 'w_mla_o': out['w_mla_o'], 'dil_q_norm_g': out['dil_q_norm_g'], 'dil_k_norm_g': out['dil_k_norm_g'], 'w_dil_o': out['w_dil_o'], 'b_merge': out['b_merge'], 'w_out': out['w_out'], 'loss_target': out['loss_target'], 'm_norm_g': out['m_norm_g'], 'm_w_in': out['m_w_in'], 'm_conv_w': out['m_conv_w'], 'm_conv_b': out['m_conv_b'], 'm_w_gate_x': out['m_w_gate_x'], 'm_b_gate_x': out['m_b_gate_x'], 'm_w_gate_a': out['m_w_gate_a'], 'm_b_gate_a': out['m_b_gate_a'], 'm_lru_lambda': out['m_lru_lambda'], 'm_w_lru_o': out['m_w_lru_o'], 'm_cq_norm_g': out['m_cq_norm_g'], 'm_ckv_norm_g': out['m_ckv_norm_g'], 'm_w_uq': out['m_w_uq'], 'm_w_ukv': out['m_w_ukv'], 'm_mla_q_norm_g': out['m_mla_q_norm_g'], 'm_mla_k_norm_g': out['m_mla_k_norm_g'], 'm_w_mla_o': out['m_w_mla_o'], 'm_dil_q_norm_g': out['m_dil_q_norm_g'], 'm_dil_k_norm_g': out['m_dil_k_norm_g'], 'm_w_dil_o': out['m_w_dil_o'], 'm_b_merge': out['m_b_merge'], 'm_w_out': out['m_w_out'], 'v_norm_g': out['v_norm_g'], 'v_w_in': out['v_w_in'], 'v_conv_w': out['v_conv_w'], 'v_conv_b': out['v_conv_b'], 'v_w_gate_x': out['v_w_gate_x'], 'v_b_gate_x': out['v_b_gate_x'], 'v_w_gate_a': out['v_w_gate_a'], 'v_b_gate_a': out['v_b_gate_a'], 'v_lru_lambda': out['v_lru_lambda'], 'v_w_lru_o': out['v_w_lru_o'], 'v_cq_norm_g': out['v_cq_norm_g'], 'v_ckv_norm_g': out['v_ckv_norm_g'], 'v_w_uq': out['v_w_uq'], 'v_w_ukv': out['v_w_ukv'], 'v_mla_q_norm_g': out['v_mla_q_norm_g'], 'v_mla_k_norm_g': out['v_mla_k_norm_g'], 'v_w_mla_o': out['v_w_mla_o'], 'v_dil_q_norm_g': out['v_dil_q_norm_g'], 'v_dil_k_norm_g': out['v_dil_k_norm_g'], 'v_w_dil_o': out['v_w_dil_o'], 'v_b_merge': out['v_b_merge'], 'v_w_out': out['v_w_out']}


def _loss(weights, diff, rest, loss_target):
    with _jax.named_scope("forward"):
        args = {**rest, TWIN_DIFF_INPUT: diff, **{k: w.astype(_WEIGHT_DTYPES[k]) for k, w in weights.items()}}
        y = _forward(args)
    with _jax.named_scope("loss_head"):
        err = _jnp.square(y.astype(_jnp.float32) - loss_target)
        return 0.5 * _jnp.sum(_jnp.mean(err, axis=-1)) if err.ndim else 0.5 * err


def _adamw(w, g, m, v):
    m = ADAM_B1 * m + (1.0 - ADAM_B1) * g
    v = ADAM_B2 * v + (1.0 - ADAM_B2) * _jnp.square(g)
    m_hat = m / (1.0 - ADAM_B1 ** ADAM_STEP)
    v_hat = v / (1.0 - ADAM_B2 ** ADAM_STEP)
    delta = -ADAM_LR * (m_hat / (_jnp.sqrt(v_hat) + ADAM_EPS) + ADAM_WD * w)
    return delta, m, v


def reference(x, positions, norm_g, w_in, conv_w, conv_b, w_gate_x, b_gate_x, w_gate_a, b_gate_a, lru_lambda, w_lru_o, cq_norm_g, ckv_norm_g, w_uq, w_ukv, mla_q_norm_g, mla_k_norm_g, w_mla_o, dil_q_norm_g, dil_k_norm_g, w_dil_o, b_merge, w_out, loss_target, m_norm_g, m_w_in, m_conv_w, m_conv_b, m_w_gate_x, m_b_gate_x, m_w_gate_a, m_b_gate_a, m_lru_lambda, m_w_lru_o, m_cq_norm_g, m_ckv_norm_g, m_w_uq, m_w_ukv, m_mla_q_norm_g, m_mla_k_norm_g, m_w_mla_o, m_dil_q_norm_g, m_dil_k_norm_g, m_w_dil_o, m_b_merge, m_w_out, v_norm_g, v_w_in, v_conv_w, v_conv_b, v_w_gate_x, v_b_gate_x, v_w_gate_a, v_b_gate_a, v_lru_lambda, v_w_lru_o, v_cq_norm_g, v_ckv_norm_g, v_w_uq, v_w_ukv, v_mla_q_norm_g, v_mla_k_norm_g, v_w_mla_o, v_dil_q_norm_g, v_dil_k_norm_g, v_w_dil_o, v_b_merge, v_w_out):
    given = dict(x=x, positions=positions, norm_g=norm_g, w_in=w_in, conv_w=conv_w, conv_b=conv_b, w_gate_x=w_gate_x, b_gate_x=b_gate_x, w_gate_a=w_gate_a, b_gate_a=b_gate_a, lru_lambda=lru_lambda, w_lru_o=w_lru_o, cq_norm_g=cq_norm_g, ckv_norm_g=ckv_norm_g, w_uq=w_uq, w_ukv=w_ukv, mla_q_norm_g=mla_q_norm_g, mla_k_norm_g=mla_k_norm_g, w_mla_o=w_mla_o, dil_q_norm_g=dil_q_norm_g, dil_k_norm_g=dil_k_norm_g, w_dil_o=w_dil_o, b_merge=b_merge, w_out=w_out, loss_target=loss_target, m_norm_g=m_norm_g, m_w_in=m_w_in, m_conv_w=m_conv_w, m_conv_b=m_conv_b, m_w_gate_x=m_w_gate_x, m_b_gate_x=m_b_gate_x, m_w_gate_a=m_w_gate_a, m_b_gate_a=m_b_gate_a, m_lru_lambda=m_lru_lambda, m_w_lru_o=m_w_lru_o, m_cq_norm_g=m_cq_norm_g, m_ckv_norm_g=m_ckv_norm_g, m_w_uq=m_w_uq, m_w_ukv=m_w_ukv, m_mla_q_norm_g=m_mla_q_norm_g, m_mla_k_norm_g=m_mla_k_norm_g, m_w_mla_o=m_w_mla_o, m_dil_q_norm_g=m_dil_q_norm_g, m_dil_k_norm_g=m_dil_k_norm_g, m_w_dil_o=m_w_dil_o, m_b_merge=m_b_merge, m_w_out=m_w_out, v_norm_g=v_norm_g, v_w_in=v_w_in, v_conv_w=v_conv_w, v_conv_b=v_conv_b, v_w_gate_x=v_w_gate_x, v_b_gate_x=v_b_gate_x, v_w_gate_a=v_w_gate_a, v_b_gate_a=v_b_gate_a, v_lru_lambda=v_lru_lambda, v_w_lru_o=v_w_lru_o, v_cq_norm_g=v_cq_norm_g, v_ckv_norm_g=v_ckv_norm_g, v_w_uq=v_w_uq, v_w_ukv=v_w_ukv, v_mla_q_norm_g=v_mla_q_norm_g, v_mla_k_norm_g=v_mla_k_norm_g, v_w_mla_o=v_w_mla_o, v_dil_q_norm_g=v_dil_q_norm_g, v_dil_k_norm_g=v_dil_k_norm_g, v_w_dil_o=v_w_dil_o, v_b_merge=v_b_merge, v_w_out=v_w_out)
    weights = {n: given[n] for n in TWIN_WEIGHTS}
    shared = {n: given[n] for n in SHARED_INPUTS}
    per_example = {n: given[n] for n in ['x', 'positions']}
    grad_fn = _jax.value_and_grad(_loss, argnums=(0, 1))

    def one_microbatch(ex, loss_target):
        ex = dict(ex)
        diff = ex.pop(TWIN_DIFF_INPUT)
        return grad_fn(weights, diff, {**shared, **ex}, loss_target)

    if N_MICROBATCH == 1:
        loss, (grad_w, grad_x) = one_microbatch(per_example, given["loss_target"])
    else:
        def body(carry, xs):
            loss_sum, grad_sum = carry
            l_k, (gw_k, gx_k) = one_microbatch(xs[0], xs[1])
            with _jax.named_scope("update"):
                return (loss_sum + l_k, _jax.tree.map(_jnp.add, grad_sum, gw_k)), gx_k

        init = (_jnp.zeros((), _jnp.float32), _jax.tree.map(_jnp.zeros_like, weights))
        (loss, grad_w), grad_x = _jax.lax.scan(body, init, (per_example, given["loss_target"]))
    with _jax.named_scope("update"):
        delta_w, new_m, new_v = {}, {}, {}
        for n in TWIN_WEIGHTS:
            delta_w[n], new_m[n], new_v[n] = _adamw(weights[n], grad_w[n], given["m_" + n], given["v_" + n])
    return (loss, grad_x, *[grad_w[n] for n in TWIN_WEIGHTS], *[delta_w[n] for n in TWIN_WEIGHTS],
            *[new_m[n] for n in TWIN_WEIGHTS], *[new_v[n] for n in TWIN_WEIGHTS])
```

```python
import functools

import jax
import jax.numpy as jnp
from jax import lax
from jax.experimental import pallas as pl
from jax.experimental.pallas import tpu as pltpu

F32 = jnp.float32
BF16 = jnp.bfloat16
MXU_DTYPE = jnp.bfloat16

D_MODEL = 1024
EPS = 1e-6
ROPE_THETA = 10000.0
LRU_BLOCKS = 8
LRU_C = 8.0
MLA_HEADS = 8
MLA_NOPE = 64
MLA_ROPE = 32
MLA_QK = 96
DIL_GROUPS = ((128, 1), (512, 4), (2048, 16))
DIL_HD = 64
DIL_NK = 128
IN_WIDTH = 11168
ADAM_LR, ADAM_B1, ADAM_B2, ADAM_EPS, ADAM_WD, ADAM_STEP = 0.001, 0.9, 0.999, 1e-08, 0.01, 10

LANES = 128
G_LRU, G_MLA, G_DIL, G_MRG = 0, 2048, 3072, 8192
IN_PAD = 11264
KR_OFF = 2432

NN = (((1,), (0,)), ((), ()))
NT = (((1,), (1,)), ((), ()))
TN = (((0,), (0,)), ((), ()))
NEG = -1e30
MESH = pl.DeviceIdType.MESH
VMEM_LIMIT = 48 * 2**20


def _cparams(sem):
    return pltpu.CompilerParams(dimension_semantics=sem, vmem_limit_bytes=VMEM_LIMIT)


def _dot(a, b, dims):
    return lax.dot_general(a.astype(MXU_DTYPE), b.astype(MXU_DTYPE), dims, preferred_element_type=F32)


@jax.custom_vjp
def mm(a, w):
    return _dot(a, w, NN)


def _mm_fwd(a, w):
    return _dot(a, w, NN), (a, w)


def _mm_bwd(res, g):
    a, w = res
    return _dot(g, w, NT), _dot(a, g, TN)


mm.defvjp(_mm_fwd, _mm_bwd)


def _seg64_matrix():
    r = lax.broadcasted_iota(jnp.int32, (LANES, LANES), 0) < DIL_HD
    c = lax.broadcasted_iota(jnp.int32, (LANES, LANES), 1) < DIL_HD
    return (r == c).astype(BF16)


def _seg_sum_impl(x):
    b = _seg64_matrix()
    hi = x.astype(BF16)
    r1 = x - hi.astype(F32)
    mid = r1.astype(BF16)
    lo = (r1 - mid.astype(F32)).astype(BF16)
    dot = lambda u: lax.dot_general(u, b, NN, preferred_element_type=F32)
    return dot(hi) + dot(mid) + dot(lo)


@jax.custom_vjp
def seg_sum(x):
    return _seg_sum_impl(x)


seg_sum.defvjp(lambda x: (_seg_sum_impl(x), None), lambda _, g: (_seg_sum_impl(g),))


def _lroll_impl(x, s):
    return pltpu.roll(x, s % LANES, 1)


@functools.partial(jax.custom_vjp, nondiff_argnums=(1,))
def lroll(x, s):
    return _lroll_impl(x, s)


lroll.defvjp(lambda x, s: (_lroll_impl(x, s), None), lambda s, _, g: (_lroll_impl(g, -s),))


class _Ops:
    def __init__(self, diff):
        self.mm = mm if diff else (lambda a, w: _dot(a, w, NN))
        self.seg_sum = seg_sum if diff else _seg_sum_impl
        self.lroll = lroll if diff else _lroll_impl


PLAIN, DIFF = _Ops(False), _Ops(True)


def rms(x, g, n):
    ms = jnp.sum(x * x, axis=-1, keepdims=True) * (1.0 / n)
    return x * lax.rsqrt(ms + EPS) * g


def rope(ops, x, c, s1, s2, half):
    return x * c + ops.lroll(x, -half) * s1 + ops.lroll(x, half) * s2


def _sigmoid(x):
    return 1.0 / (1.0 + jnp.exp(-x))


def _silu_and_grad(g):
    sg = _sigmoid(g)
    return g * sg, sg * (1.0 + g * (1.0 - sg))


def _softplus(x):
    return jnp.maximum(x, 0.0) + jnp.log(1.0 + jnp.exp(-jnp.abs(x)))


def _expm1(y):
    series = y * (1.0 + y * (0.5 + y * (1.0 / 6.0 + y * (1.0 / 24.0 + y * (1.0 / 120.0)))))
    return jnp.where(y > -0.05, series, jnp.exp(jnp.minimum(y, -0.05)) - 1.0)


def _mm_call(name, a, b, *, mode, m, n, k, a_blk, b_blk, out_dtype, add, tm, tn, tk):
    nk = k // tk
    assert m % tm == 0 and n % tn == 0 and k % tk == 0, (name, m, n, k, tm, tn, tk)
    dims = {"nn": NN, "nt": NT, "tn": TN}[mode]

    def body(*refs):
        if add is None:
            a_ref, b_ref, o_ref, *scr = refs
            add_ref = None
        else:
            a_ref, b_ref, add_ref, o_ref, *scr = refs
        part = _dot(a_ref[...], b_ref[...], dims)

        def finish(acc):
            if add_ref is not None:
                acc = acc + add_ref[...]
            o_ref[...] = acc.astype(o_ref.dtype)

        if nk == 1:
            finish(part)
        else:
            (acc_ref,) = scr
            kk = pl.program_id(2)

            @pl.when(kk == 0)
            def _():
                acc_ref[...] = part

            @pl.when(kk > 0)
            def _():
                acc_ref[...] += part

            @pl.when(kk == nk - 1)
            def _():
                finish(acc_ref[...])

    in_specs = [a_blk, b_blk]
    args = [a, b]
    if add is not None:
        in_specs.append(pl.BlockSpec((tm, tn), lambda j, i, kk: (i, j)))
        args.append(add)
    return pl.pallas_call(
        body,
        name=name,
        grid=(n // tn, m // tm, nk),
        in_specs=in_specs,
        out_specs=pl.BlockSpec((tm, tn), lambda j, i, kk: (i, j)),
        out_shape=jax.ShapeDtypeStruct((m, n), out_dtype),
        scratch_shapes=[] if nk == 1 else [pltpu.VMEM((tm, tn), F32)],
        compiler_params=_cparams(("parallel", "parallel", "arbitrary")),
    )(*args)


def mm_nn(name, a, b, *, n=None, n_off=0, out_dtype=F32, add=None, tm=512, tn=1024):
    m, k = a.shape
    n = b.shape[1] if n is None else n
    tn, tk = min(tn, n), min(k, 1024)
    ob = n_off // tn
    assert n_off % tn == 0
    return _mm_call(name, a, b, mode="nn", m=m, n=n, k=k, out_dtype=out_dtype, add=add, tm=tm, tn=tn, tk=tk,
                    a_blk=pl.BlockSpec((tm, tk), lambda j, i, kk: (i, kk)),
                    b_blk=pl.BlockSpec((tk, tn), lambda j, i, kk: (kk, j + ob)))


def mm_nt(name, a, b, *, out_dtype=F32, tm=512, tn=1024):
    m, k = a.shape
    n = b.shape[0]
    tn, tk = min(tn, n), min(k, 1024)
    return _mm_call(name, a, b, mode="nt", m=m, n=n, k=k, out_dtype=out_dtype, add=None, tm=tm, tn=tn, tk=tk,
                    a_blk=pl.BlockSpec((tm, tk), lambda j, i, kk: (i, kk)),
                    b_blk=pl.BlockSpec((tn, tk), lambda j, i, kk: (j, kk)))


def mm_tn(name, a, b, *, tn=512, tk=512):
    t, m = a.shape
    n = b.shape[1]
    tm, tn = min(m, 1024), min(tn, n)
    return _mm_call(name, a, b, mode="tn", m=m, n=n, k=t, out_dtype=F32, add=None, tm=tm, tn=tn, tk=tk,
                    a_blk=pl.BlockSpec((tk, tm), lambda j, i, kk: (kk, i)),
                    b_blk=pl.BlockSpec((tk, tn), lambda j, i, kk: (kk, j)))


def rowwise(name, fn, t, *, tm=256, ncol=1, ins=(), outs=(), accs=()):
    n_in, n_out, n_acc = len(ins), len(outs), len(accs)

    def zero_when(ref, cond):
        @pl.when(cond)
        def _():
            ref[...] = jnp.zeros(ref.shape, ref.dtype)

    def body(*refs):
        in_refs, out_refs, acc_refs = refs[:n_in], refs[n_in:n_in + n_out], refs[n_in + n_out:]
        j, i = pl.program_id(0), pl.program_id(1)
        for ref, (_, _, _, cd) in zip(acc_refs, accs):
            zero_when(ref, (i == 0) if cd else ((i == 0) & (j == 0)))
        fn(in_refs, out_refs, acc_refs)

    def in_spec(arr, w, base, cd, rd):
        rows = tm if rd else arr.shape[0]
        return pl.BlockSpec((rows, w), lambda j, i: (i if rd else 0, base + (j if cd else 0)))

    in_specs = [in_spec(*e) for e in ins]
    out_specs = [pl.BlockSpec((tm, w), (lambda j, i, cd=cd: (i, j if cd else 0))) for (_, w, _, cd) in outs]
    out_specs += [pl.BlockSpec((r, w), (lambda j, i, cd=cd: (0, j if cd else 0))) for (r, _, w, cd) in accs]
    out_shape = [jax.ShapeDtypeStruct((t, c), dt) for (c, _, dt, _) in outs]
    out_shape += [jax.ShapeDtypeStruct((r, c), F32) for (r, c, _, _) in accs]
    res = pl.pallas_call(
        body,
        name=name,
        grid=(ncol, t // tm),
        in_specs=in_specs,
        out_specs=out_specs,
        out_shape=out_shape,
        compiler_params=_cparams(("arbitrary", "arbitrary") if accs else ("parallel", "parallel")),
    )(*[e[0] for e in ins])
    return res


def _row(arr, w=None, base=0, cd=False):
    return (arr, arr.shape[1] if w is None else w, base, cd, True)


def _const(arr, w=None, base=0, cd=False):
    return (arr, arr.shape[1] if w is None else w, base, cd, False)


def rope_tables(positions):
    t = positions.shape[0]

    def fn(ins, outs, _):
        pos = ins[0][...].astype(F32)
        lane = lax.broadcasted_iota(jnp.int32, (1, LANES), 1)
        log_theta = jnp.log(jnp.float32(ROPE_THETA))
        jm = lane - MLA_NOPE
        idx = jnp.clip(jnp.where(jm < 16, jm, jm - 16), 0, 15).astype(F32)
        ang = pos * jnp.exp(-(idx * (2.0 / MLA_ROPE)) * log_theta)
        cos, sin = jnp.cos(ang), jnp.sin(ang)
        in_rope = (lane >= MLA_NOPE) & (lane < MLA_QK)
        outs[0][:, 0:128] = jnp.where(lane < MLA_NOPE, 1.0, jnp.where(in_rope, cos, 0.0))
        outs[0][:, 128:256] = jnp.where(in_rope & (jm < 16), -sin, 0.0)
        outs[0][:, 256:384] = jnp.where(in_rope & (jm >= 16), sin, 0.0)
        jd = lane & (DIL_HD - 1)
        idx = (jd & 31).astype(F32)
        ang = pos * jnp.exp(-(idx * (2.0 / DIL_HD)) * log_theta)
        cos, sin = jnp.cos(ang), jnp.sin(ang)
        outs[0][:, 384:512] = cos
        outs[0][:, 512:640] = jnp.where(jd < 32, -sin, 0.0)
        outs[0][:, 640:768] = jnp.where(jd >= 32, sin, 0.0)

    return rowwise("rope_tables", fn, t, ins=[_row(positions)], outs=[(768, 768, F32, False)])[0]


def rmsnorm_fwd(x, g):
    def fn(ins, outs, _):
        outs[0][...] = rms(ins[0][...], ins[1][...], D_MODEL).astype(MXU_DTYPE)

    return rowwise("rmsnorm_fwd", fn, x.shape[0], ins=[_row(x), _const(g)], outs=[(D_MODEL, D_MODEL, MXU_DTYPE, False)])[0]


def rmsnorm_bwd(x, dh, dres, g):
    def fn(ins, outs, accs):
        _, vjp = jax.vjp(lambda xv, gv: rms(xv, gv, D_MODEL), ins[0][...], ins[3][...])
        dx, dg = vjp(ins[1][...])
        outs[0][...] = ins[2][...] + dx
        accs[0][...] += dg

    return rowwise("rmsnorm_bwd", fn, x.shape[0], ins=[_row(x), _row(dh), _row(dres), _const(g)],
                   outs=[(D_MODEL, D_MODEL, F32, False)], accs=[(1, D_MODEL, D_MODEL, False)])


def loss_head(y, target):
    def fn(ins, outs, accs):
        err = ins[0][...] - ins[1][...]
        outs[0][...] = err * (1.0 / D_MODEL)
        accs[0][...] += jnp.sum(err * err, axis=0, keepdims=True)
        accs[1][...] = jnp.broadcast_to(jnp.sum(accs[0][...], keepdims=True), (1, LANES))

    dy, _, tot = rowwise("loss_head", fn, y.shape[0], ins=[_row(y), _row(target)], outs=[(D_MODEL, D_MODEL, F32, False)],
                         accs=[(1, D_MODEL, D_MODEL, False), (1, LANES, LANES, False)])
    return dy, tot[0, 0] * (0.5 / D_MODEL)


def _shift_rows(v, d, fill, reverse):
    tb = v.shape[0]
    rows = lax.broadcasted_iota(jnp.int32, v.shape, 0)
    if not reverse:
        return jnp.where(rows >= d, pltpu.roll(v, d, 0), fill)
    return jnp.where(rows < tb - d, pltpu.roll(v, tb - d, 0), fill)


def _scan_tile(a, b, reverse):
    d = 1
    while d < a.shape[0]:
        b = b + a * _shift_rows(b, d, 0.0, reverse)
        a = a * _shift_rows(a, d, 1.0, reverse)
        d *= 2
    return a, b


def _lru_gates(ops, xc, wgx, bgx, wga, bga, lam):
    gx = _sigmoid(ops.mm(xc, wgx) + bgx)
    ga = _sigmoid(ops.mm(xc, wga) + bga)
    log_a = -LRU_C * ga * _softplus(-lam)
    a = jnp.exp(log_a)
    mult = jnp.sqrt(-_expm1(2.0 * log_a))
    return a, mult * (gx * xc)


def _shifted_inputs(x, halo, tb):
    rows = lax.broadcasted_iota(jnp.int32, x.shape, 0)
    pad = jnp.zeros((tb - 8, LANES), F32)
    out = []
    for d in (3, 2, 1):
        head = jnp.concatenate([pltpu.roll(halo, d, 0), pad], axis=0)
        out.append(jnp.where(rows >= d, pltpu.roll(x, d, 0), head))
    return out + [x]


def _lru_specs(nt, tb, reverse):
    hb = tb // 8
    tt = (lambda t: nt - 1 - t) if reverse else (lambda t: t)
    blk = lambda off: pl.BlockSpec((tb, LANES), lambda n, t: (tt(t), n + off))
    halo = lambda off: pl.BlockSpec((8, LANES), lambda n, t: (jnp.maximum(tt(t) * hb - 1, 0), n + off))
    chan = lambda r: pl.BlockSpec((r, LANES), lambda n, t: (0, n))
    wblk = pl.BlockSpec((None, LANES, LANES), lambda n, t: (n, 0, 0))
    bblk = pl.BlockSpec((None, 1, LANES), lambda n, t: (n, 0, 0))
    return blk, halo, chan, wblk, bblk


def lru_fwd(z, conv_w, conv_b, wgx, bgx, wga, bga, lam, *, tb=256):
    t = z.shape[0]
    nt = t // tb
    blk, halo, chan, wblk, bblk = _lru_specs(nt, tb, False)

    def body(x_ref, xh_ref, g_ref, cw_ref, cb_ref, wgx_ref, bgx_ref, wga_ref, bga_ref, lam_ref, h_ref, y_ref, carry_ref):
        ti = pl.program_id(1)

        @pl.when(ti == 0)
        def _():
            carry_ref[...] = jnp.zeros((8, LANES), F32)

        x = x_ref[...]
        hal = jnp.where(ti > 0, xh_ref[...], 0.0)
        xs = _shifted_inputs(x, hal, tb)
        xc = cb_ref[...] + sum(xs[k] * cw_ref[k:k + 1, :] for k in range(4))
        a, b = _lru_gates(PLAIN, xc, wgx_ref[...], bgx_ref[...], wga_ref[...], bga_ref[...], lam_ref[...])
        acum, h0 = _scan_tile(a, b, False)
        h = h0 + acum * carry_ref[7:8, :]
        carry_ref[...] = h[tb - 8:tb, :]
        h_ref[...] = h
        y_ref[...] = (h * _silu_and_grad(g_ref[...])[0]).astype(MXU_DTYPE)

    return pl.pallas_call(
        body,
        name="lru_fwd",
        grid=(LRU_BLOCKS, nt),
        in_specs=[blk(0), halo(0), blk(LRU_BLOCKS), chan(4), chan(1), wblk, bblk, wblk, bblk, chan(1)],
        out_specs=[blk(0), blk(0)],
        out_shape=[jax.ShapeDtypeStruct((t, 1024), F32), jax.ShapeDtypeStruct((t, 1024), MXU_DTYPE)],
        scratch_shapes=[pltpu.VMEM((8, LANES), F32)],
        compiler_params=_cparams(("parallel", "arbitrary")),
    )(z, z, z, conv_w, conv_b, wgx, bgx, wga, bga, lam)


def lru_bwd(z, h, dy, conv_w, conv_b, wgx, bgx, wga, bga, lam, *, tb=256):
    t = z.shape[0]
    nt = t // tb
    blk, halo, chan, wblk, bblk = _lru_specs(nt, tb, True)

    def body(x_ref, xh_ref, g_ref, h_ref, hh_ref, dy_ref, cw_ref, cb_ref, wgx_ref, bgx_ref, wga_ref, bga_ref, lam_ref,
             dzx_ref, dzg_ref, dwgx_ref, dbgx_ref, dwga_ref, dbga_ref, dlam_ref, dcw_ref, dcb_ref,
             gcar_ref, acar_ref, xcar_ref):
        ti = pl.program_id(1)
        has_earlier = ti < nt - 1

        @pl.when(ti == 0)
        def _():
            for ref in (dwgx_ref, dbgx_ref, dwga_ref, dbga_ref, dlam_ref, dcw_ref, dcb_ref, gcar_ref, acar_ref, xcar_ref):
                ref[...] = jnp.zeros(ref.shape, F32)

        rows = lax.broadcasted_iota(jnp.int32, (tb, LANES), 0)
        x = x_ref[...]
        hal = jnp.where(has_earlier, xh_ref[...], 0.0)
        xs = _shifted_inputs(x, hal, tb)
        xc = cb_ref[...] + sum(xs[k] * cw_ref[k:k + 1, :] for k in range(4))
        (a, _), vjp = jax.vjp(functools.partial(_lru_gates, DIFF), xc, wgx_ref[...], bgx_ref[...], wga_ref[...],
                              bga_ref[...], lam_ref[...])
        g, h, dyv = g_ref[...], h_ref[...], dy_ref[...]
        silu, dsilu = _silu_and_grad(g)
        dzg_ref[...] = (dyv * h * dsilu).astype(MXU_DTYPE)
        a_next = jnp.where(rows < tb - 1, pltpu.roll(a, tb - 1, 0), acar_ref[0:1, :])
        acum, g0 = _scan_tile(a_next, dyv * silu, True)
        gt = g0 + acum * gcar_ref[0:1, :]
        h_prev = jnp.where(rows >= 1, pltpu.roll(h, 1, 0), jnp.where(has_earlier, hh_ref[7:8, :], 0.0))
        dxc, dwgx, dbgx, dwga, dbga, dlam = vjp((gt * h_prev, gt))
        later = xcar_ref[...]
        gcar_ref[...] = gt[0:8, :]
        acar_ref[...] = a[0:8, :]
        xcar_ref[...] = dxc[0:8, :]
        dx = dxc * cw_ref[3:4, :]
        pad = jnp.zeros((tb - 8, LANES), F32)
        for d in (1, 2, 3):
            tail = jnp.concatenate([pad, pltpu.roll(later, 8 - d, 0)], axis=0)
            up = jnp.where(rows < tb - d, pltpu.roll(dxc, tb - d, 0), tail)
            dx = dx + up * cw_ref[3 - d:4 - d, :]
        dzx_ref[...] = dx.astype(MXU_DTYPE)
        for k in range(4):
            dcw_ref[k:k + 1, :] += jnp.sum(dxc * xs[k], axis=0, keepdims=True)
        dcb_ref[...] += jnp.sum(dxc, axis=0, keepdims=True)
        dwgx_ref[...] += dwgx
        dbgx_ref[...] += dbgx
        dwga_ref[...] += dwga
        dbga_ref[...] += dbga
        dlam_ref[...] += dlam

    sds = jax.ShapeDtypeStruct
    return pl.pallas_call(
        body,
        name="lru_bwd",
        grid=(LRU_BLOCKS, nt),
        in_specs=[blk(0), halo(0), blk(LRU_BLOCKS), blk(0), halo(0), blk(0), chan(4), chan(1), wblk, bblk, wblk, bblk, chan(1)],
        out_specs=[blk(0), blk(0), wblk, bblk, wblk, bblk, chan(1), chan(4), chan(1)],
        out_shape=[sds((t, 1024), MXU_DTYPE), sds((t, 1024), MXU_DTYPE), sds(wgx.shape, F32), sds(bgx.shape, F32), sds(wga.shape, F32),
                   sds(bga.shape, F32), sds((1, 1024), F32), sds((4, 1024), F32), sds((1, 1024), F32)],
        scratch_shapes=[pltpu.VMEM((8, LANES), F32)] * 3,
        compiler_params=_cparams(("parallel", "arbitrary")),
    )(z, z, z, h, h, dy, conv_w, conv_b, wgx, bgx, wga, bga, lam)


def _mla_prep_tile(ops, cq, ckv, krp, c, s1, s2, g_cq, g_ckv, wq, wk, wv, gq, gk):
    cqn = rms(cq, g_cq, 256)
    ckvn = rms(ckv, g_ckv, 128)
    v = ops.mm(ckvn, wv)
    qs, ks = [], []
    for hd in range(MLA_HEADS):
        q = rms(ops.mm(cqn, wq[hd]), gq, MLA_QK)
        k = rms(ops.mm(ckvn, wk[hd]) + krp, gk, MLA_QK)
        qs.append(rope(ops, q, c, s1, s2, 16) * (MLA_QK ** -0.5))
        ks.append(rope(ops, k, c, s1, s2, 16))
    return tuple(qs), tuple(ks), v


def _mla_prep_args(ins):
    z_cq, z_ckv, z_kr, tc, ts1, ts2, g_cq, g_ckv, wq, wk, wv, gq, gk = ins[:13]
    heads = lambda w: tuple(w[:, LANES * hd:LANES * (hd + 1)] for hd in range(MLA_HEADS))
    return (z_cq[...], z_ckv[...], z_kr[...], tc[...], ts1[...], ts2[...], g_cq[...], g_ckv[...], heads(wq), heads(wk),
            wv[...], gq[...], gk[...])


def _mla_prep_ins(z, tabs, w):
    return [_row(z, 256, 0), _row(z, 128, 2), _row(z, 128, 3), _row(tabs, 128, 0), _row(tabs, 128, 1), _row(tabs, 128, 2),
            _const(w["cq_norm_g"]), _const(w["ckv_norm_g"]), _const(w["wq"]), _const(w["wk"]), _const(w["wv"]),
            _const(w["gq"]), _const(w["gk"])]


def mla_prep_fwd(z, tabs, w):
    def fn(ins, outs, _):
        qs, ks, v = _mla_prep_tile(PLAIN, *_mla_prep_args(ins))
        for hd in range(MLA_HEADS):
            outs[0][:, LANES * hd:LANES * (hd + 1)] = qs[hd].astype(MXU_DTYPE)
            outs[1][:, LANES * hd:LANES * (hd + 1)] = ks[hd].astype(MXU_DTYPE)
        outs[2][...] = v.astype(MXU_DTYPE)

    return rowwise("mla_prep_fwd", fn, z.shape[0], ins=_mla_prep_ins(z, tabs, w),
                   outs=[(1024, 1024, MXU_DTYPE, False), (1024, 1024, MXU_DTYPE, False), (512, 512, MXU_DTYPE, False)])


def mla_prep_bwd(z, tabs, w, dq, dk, dv):
    def fn(ins, outs, accs):
        args = _mla_prep_args(ins)
        _, vjp = jax.vjp(functools.partial(_mla_prep_tile, DIFF), *args)
        heads = lambda ref: tuple(ref[:, LANES * hd:LANES * (hd + 1)] for hd in range(MLA_HEADS))
        dcq, dckv, dkr, _, _, _, dg_cq, dg_ckv, dwq, dwk, dwv, dgq, dgk = vjp((heads(ins[13]), heads(ins[14]), ins[15][...]))
        lane = lax.broadcasted_iota(jnp.int32, (1, LANES), 1)
        outs[0][:, 0:256] = dcq.astype(MXU_DTYPE)
        outs[0][:, 256:384] = dckv.astype(MXU_DTYPE)
        outs[0][:, 384:512] = jnp.where((lane >= MLA_NOPE) & (lane < MLA_QK), dkr, 0.0).astype(MXU_DTYPE)
        accs[0][...] += dg_cq
        accs[1][...] += dg_ckv
        for hd in range(MLA_HEADS):
            accs[2][:, LANES * hd:LANES * (hd + 1)] += dwq[hd]
            accs[3][:, LANES * hd:LANES * (hd + 1)] += dwk[hd]
        accs[4][...] += dwv
        accs[5][...] += dgq
        accs[6][...] += dgk

    return rowwise("mla_prep_bwd", fn, z.shape[0], ins=_mla_prep_ins(z, tabs, w) + [_row(dq), _row(dk), _row(dv)],
                   outs=[(512, 512, MXU_DTYPE, False)],
                   accs=[(1, 256, 256, False), (1, 128, 128, False), (256, 1024, 1024, False), (128, 1024, 1024, False),
                         (128, 512, 512, False), (1, 128, 128, False), (1, 128, 128, False)])


def _head_masks():
    lane = lax.broadcasted_iota(jnp.int32, (1, LANES), 1)
    return (lane < DIL_HD, lane >= DIL_HD)


def _row_scalar(tile, mask):
    return jnp.max(jnp.where(mask, tile, -jnp.inf), axis=-1, keepdims=True)


def mla_attn_fwd(q, k, v, z, *, tq=256):
    t = q.shape[0]
    nq = t // tq

    def body(q_ref, k_ref, v_ref, g_ref, o_ref, y_ref, lse_ref, m_scr, l_scr, acc_scr):
        i, j = pl.program_id(1), pl.program_id(2)
        masks = _head_masks()

        @pl.when(j == 0)
        def _():
            m_scr[...] = jnp.full(m_scr.shape, NEG, F32)
            l_scr[...] = jnp.zeros(l_scr.shape, F32)
            acc_scr[...] = jnp.zeros(acc_scr.shape, F32)

        @pl.when(j <= i)
        def _():
            row = i * tq + lax.broadcasted_iota(jnp.int32, (tq, tq), 0)
            col = j * tq + lax.broadcasted_iota(jnp.int32, (tq, tq), 1)
            vv = v_ref[...]
            for hh in range(2):
                s = _dot(q_ref[:, LANES * hh:LANES * (hh + 1)], k_ref[:, LANES * hh:LANES * (hh + 1)], NT)
                s = jnp.where(col <= row, s, NEG)
                m_prev = m_scr[hh]
                m_new = jnp.maximum(m_prev, jnp.max(s, axis=1, keepdims=True))
                alpha = jnp.exp(m_prev - m_new)
                p = jnp.exp(s - jnp.concatenate([m_new] * (tq // LANES), axis=1))
                l_scr[hh] = alpha * l_scr[hh] + jnp.sum(p, axis=1, keepdims=True)
                acc_scr[hh] = alpha * acc_scr[hh] + _dot(p, jnp.where(masks[hh], vv, jnp.zeros_like(vv)), NN)
                m_scr[hh] = m_new

        @pl.when(j == i)
        def _():
            o = acc_scr[0] / l_scr[0] + acc_scr[1] / l_scr[1]
            o_ref[...] = o
            y_ref[...] = (o * _silu_and_grad(g_ref[...])[0]).astype(MXU_DTYPE)
            lse_ref[...] = jnp.where(masks[0], m_scr[0] + jnp.log(l_scr[0]), m_scr[1] + jnp.log(l_scr[1]))

    kv = lambda w: pl.BlockSpec((tq, w), lambda p, i, j: (jnp.minimum(j, i), p))
    qo = lambda w, off=0: pl.BlockSpec((tq, w), lambda p, i, j: (i, p + off))
    sds = jax.ShapeDtypeStruct
    return pl.pallas_call(
        body,
        name="mla_attn_fwd",
        grid=(4, nq, nq),
        in_specs=[qo(256), kv(256), kv(LANES), qo(LANES, 4)],
        out_specs=[qo(LANES), qo(LANES), qo(LANES)],
        out_shape=[sds((t, 512), F32), sds((t, 512), MXU_DTYPE), sds((t, 512), F32)],
        scratch_shapes=[pltpu.VMEM((2, tq, LANES), F32)] * 3,
        compiler_params=_cparams(("parallel", "parallel", "arbitrary")),
    )(q, k, v, z)


def _mla_bwd_block(q_ref, k_ref, v_ref, do_ref, lse_ref, d_ref, hh, masks, causal):
    qh, kh = q_ref[:, LANES * hh:LANES * (hh + 1)], k_ref[:, LANES * hh:LANES * (hh + 1)]
    s = _dot(qh, kh, NT)
    p = jnp.exp(jnp.where(causal, s - _row_scalar(lse_ref[...], masks[hh]), NEG))
    doh = jnp.where(masks[hh], do_ref[...], 0.0)
    dp = _dot(doh, v_ref[...], NT)
    ds = p * (dp - _row_scalar(d_ref[...], masks[hh]))
    return qh, kh, p, doh, ds


def mla_attn_bwd_dq(q, k, v, do, lse, dd, *, tq=256):
    t = q.shape[0]
    nq = t // tq

    def body(q_ref, k_ref, v_ref, do_ref, lse_ref, d_ref, dq_ref, acc_scr):
        i, j = pl.program_id(1), pl.program_id(2)
        masks = _head_masks()

        @pl.when(j == 0)
        def _():
            acc_scr[...] = jnp.zeros(acc_scr.shape, F32)

        @pl.when(j <= i)
        def _():
            row = i * tq + lax.broadcasted_iota(jnp.int32, (tq, tq), 0)
            col = j * tq + lax.broadcasted_iota(jnp.int32, (tq, tq), 1)
            for hh in range(2):
                _, kh, _, _, ds = _mla_bwd_block(q_ref, k_ref, v_ref, do_ref, lse_ref, d_ref, hh, masks, col <= row)
                acc_scr[:, LANES * hh:LANES * (hh + 1)] += _dot(ds, kh, NN)

        @pl.when(j == i)
        def _():
            dq_ref[...] = acc_scr[...]

    kv = lambda w: pl.BlockSpec((tq, w), lambda p, i, j: (jnp.minimum(j, i), p))
    qo = lambda w: pl.BlockSpec((tq, w), lambda p, i, j: (i, p))
    return pl.pallas_call(
        body,
        name="mla_attn_bwd_dq",
        grid=(4, nq, nq),
        in_specs=[qo(256), kv(256), kv(LANES), qo(LANES), qo(LANES), qo(LANES)],
        out_specs=qo(256),
        out_shape=jax.ShapeDtypeStruct((t, 1024), F32),
        scratch_shapes=[pltpu.VMEM((tq, 256), F32)],
        compiler_params=_cparams(("parallel", "parallel", "arbitrary")),
    )(q, k, v, do, lse, dd)


def mla_attn_bwd_dkv(q, k, v, do, lse, dd, *, tq=256):
    t = q.shape[0]
    nq = t // tq

    def body(q_ref, k_ref, v_ref, do_ref, lse_ref, d_ref, dk_ref, dv_ref, dk_scr, dv_scr):
        j, i = pl.program_id(1), pl.program_id(2)
        masks = _head_masks()

        @pl.when(i == 0)
        def _():
            dk_scr[...] = jnp.zeros(dk_scr.shape, F32)
            dv_scr[...] = jnp.zeros(dv_scr.shape, F32)

        @pl.when(i >= j)
        def _():
            row = i * tq + lax.broadcasted_iota(jnp.int32, (tq, tq), 0)
            col = j * tq + lax.broadcasted_iota(jnp.int32, (tq, tq), 1)
            for hh in range(2):
                qh, _, p, doh, ds = _mla_bwd_block(q_ref, k_ref, v_ref, do_ref, lse_ref, d_ref, hh, masks, col <= row)
                dv_scr[...] += _dot(p, doh, TN)
                dk_scr[:, LANES * hh:LANES * (hh + 1)] += _dot(ds, qh, TN)

        @pl.when(i == nq - 1)
        def _():
            dk_ref[...] = dk_scr[...]
            dv_ref[...] = dv_scr[...]

    kv = lambda w: pl.BlockSpec((tq, w), lambda p, j, i: (j, p))
    qo = lambda w: pl.BlockSpec((tq, w), lambda p, j, i: (jnp.maximum(i, j), p))
    sds = jax.ShapeDtypeStruct
    return pl.pallas_call(
        body,
        name="mla_attn_bwd_dkv",
        grid=(4, nq, nq),
        in_specs=[qo(256), kv(256), kv(LANES), qo(LANES), qo(LANES), qo(LANES)],
        out_specs=[kv(256), kv(LANES)],
        out_shape=[sds((t, 1024), F32), sds((t, 512), F32)],
        scratch_shapes=[pltpu.VMEM((tq, 256), F32), pltpu.VMEM((tq, LANES), F32)],
        compiler_params=_cparams(("parallel", "parallel", "arbitrary")),
    )(q, k, v, do, lse, dd)


def gate_bwd(name, dy, o, z, g_base):
    def fn(ins, outs, _):
        dyv, ov = ins[0][...], ins[1][...]
        silu, dsilu = _silu_and_grad(ins[2][...])
        do = dyv * silu
        outs[0][...] = do
        outs[1][...] = _seg_sum_impl(do * ov)
        outs[2][...] = (dyv * ov * dsilu).astype(MXU_DTYPE)

    return rowwise(name, fn, dy.shape[0], ncol=4, ins=[_row(dy, LANES, 0, True), _row(o, LANES, 0, True), _row(z, LANES, g_base, True)],
                   outs=[(512, LANES, F32, True), (512, LANES, F32, True), (512, LANES, MXU_DTYPE, True)])


def _dil_prep_tile(ops, x, g2, c, s1, s2, scale):
    ms = ops.seg_sum(x * x) * (1.0 / DIL_HD)
    return rope(ops, x * lax.rsqrt(ms + EPS) * g2, c, s1, s2, 32) * scale


DIL_SCALE = DIL_HD ** -0.5


def _dil_prep_ins(z, tabs, gq2, gk2):
    return [_row(z, LANES, 0, True), _row(z, LANES, 12, True), _row(tabs, 128, 3), _row(tabs, 128, 4), _row(tabs, 128, 5),
            _const(gq2), _const(gk2)]


def dil_prep_fwd(z, tabs, gq2, gk2):
    def fn(ins, outs, _):
        c, s1, s2 = ins[2][...], ins[3][...], ins[4][...]
        outs[0][...] = _dil_prep_tile(PLAIN, ins[0][...], ins[5][...], c, s1, s2, DIL_SCALE).astype(MXU_DTYPE)
        outs[1][...] = _dil_prep_tile(PLAIN, ins[1][...], ins[6][...], c, s1, s2, 1.0).astype(MXU_DTYPE)

    return rowwise("dil_prep_fwd", fn, z.shape[0], ncol=12, ins=_dil_prep_ins(z, tabs, gq2, gk2),
                   outs=[(1536, LANES, MXU_DTYPE, True), (1536, LANES, MXU_DTYPE, True)])


def dil_prep_bwd(z, tabs, gq2, gk2, dq, dk):
    def fn(ins, outs, accs):
        c, s1, s2 = ins[2][...], ins[3][...], ins[4][...]
        for idx, scale in ((0, DIL_SCALE), (1, 1.0)):
            _, vjp = jax.vjp(lambda xv, gv: _dil_prep_tile(DIFF, xv, gv, c, s1, s2, scale), ins[idx][...], ins[5 + idx][...])
            dx, dg = vjp(ins[7 + idx][...])
            outs[idx][...] = dx.astype(MXU_DTYPE)
            accs[idx][...] += dg + pltpu.roll(dg, DIL_HD, 1)

    return rowwise("dil_prep_bwd", fn, z.shape[0], ncol=12,
                   ins=_dil_prep_ins(z, tabs, gq2, gk2) + [_row(dq, LANES, 0, True), _row(dk, LANES, 0, True)],
                   outs=[(1536, LANES, MXU_DTYPE, True), (1536, LANES, MXU_DTYPE, True)],
                   accs=[(1, LANES, LANES, False), (1, LANES, LANES, False)])


def _band_masks(n):
    qi = lax.broadcasted_iota(jnp.int32, (DIL_NK, DIL_NK), 0)
    ki = lax.broadcasted_iota(jnp.int32, (DIL_NK, DIL_NK), 1)
    return (ki >= qi), (ki <= qi)


def dil_attn_fwd(name, qv, kv, vv, nb, dil):
    def body(q_ref, kp_ref, kc_ref, vp_ref, vc_ref, o_ref, lse_ref):
        n = pl.program_id(0)
        mprev, mcur = _band_masks(n)
        mprev = mprev & (n > 0)
        q, kp, kc = q_ref[...], kp_ref[...], kc_ref[...]
        vp, vc = vp_ref[...], vc_ref[...]
        o = jnp.zeros((DIL_NK, LANES), F32)
        lse = jnp.zeros((DIL_NK, LANES), F32)
        for mh in _head_masks():
            qh = jnp.where(mh, q, jnp.zeros_like(q))
            sp = jnp.where(mprev, _dot(qh, kp, NT), NEG)
            sc = jnp.where(mcur, _dot(qh, kc, NT), NEG)
            m = jnp.maximum(jnp.max(sp, axis=1, keepdims=True), jnp.max(sc, axis=1, keepdims=True))
            ep, ec = jnp.exp(sp - m), jnp.exp(sc - m)
            den = jnp.sum(ep, axis=1, keepdims=True) + jnp.sum(ec, axis=1, keepdims=True)
            oh = _dot(ep, jnp.where(mh, vp, 0.0), NN) + _dot(ec, jnp.where(mh, vc, 0.0), NN)
            o = o + oh / den
            lse = jnp.where(mh, m + jnp.log(den), lse)
        o_ref[...] = o
        lse_ref[...] = lse

    cur = pl.BlockSpec((DIL_NK, LANES), lambda n, r, p: (n, r * 4 + p))
    prev = pl.BlockSpec((DIL_NK, LANES), lambda n, r, p: (jnp.maximum(n - 1, 0), r * 4 + p))
    sds = jax.ShapeDtypeStruct
    return pl.pallas_call(
        body,
        name=name,
        grid=(nb, dil, 4),
        in_specs=[cur, prev, cur, prev, cur],
        out_specs=[cur, cur],
        out_shape=[sds(qv.shape, F32), sds(qv.shape, F32)],
        compiler_params=_cparams(("parallel", "parallel", "parallel")),
    )(qv, kv, kv, vv, vv)


def dil_attn_bwd(name, qv, kv, vv, dov, lsev, ddv, nb, dil):
    def body(qc_ref, qn_ref, doc_ref, don_ref, lc_ref, ln_ref, dc_ref, dn_ref, kp_ref, kc_ref, vp_ref, vc_ref,
             dq_ref, dk_ref, dv_ref):
        n = pl.program_id(0)
        mprev, mcur = _band_masks(n)
        mnext = mprev & (n < nb - 1)
        mprev = mprev & (n > 0)
        kp, kc, vp, vc = kp_ref[...], kc_ref[...], vp_ref[...], vc_ref[...]
        dq = jnp.zeros((DIL_NK, LANES), F32)
        dk = jnp.zeros((DIL_NK, LANES), F32)
        dv = jnp.zeros((DIL_NK, LANES), F32)
        for mh in _head_masks():
            zq = jnp.zeros_like(qc_ref[...])
            qh, doh = jnp.where(mh, qc_ref[...], zq), jnp.where(mh, doc_ref[...], 0.0)
            lse_h, d_h = _row_scalar(lc_ref[...], mh), _row_scalar(dc_ref[...], mh)
            pp = jnp.exp(jnp.where(mprev, _dot(qh, kp, NT) - lse_h, NEG))
            pc = jnp.exp(jnp.where(mcur, _dot(qh, kc, NT) - lse_h, NEG))
            dsp = pp * (_dot(doh, vp, NT) - d_h)
            dsc = pc * (_dot(doh, vc, NT) - d_h)
            dq = dq + _dot(dsp, jnp.where(mh, kp, jnp.zeros_like(kp)), NN) + _dot(dsc, jnp.where(mh, kc, jnp.zeros_like(kc)), NN)
            dv = dv + _dot(pc, doh, TN)
            dk = dk + _dot(dsc, qh, TN)
            qnh, donh = jnp.where(mh, qn_ref[...], zq), jnp.where(mh, don_ref[...], 0.0)
            lse_n, d_n = _row_scalar(ln_ref[...], mh), _row_scalar(dn_ref[...], mh)
            pn = jnp.exp(jnp.where(mnext, _dot(qnh, kc, NT) - lse_n, NEG))
            dsn = pn * (_dot(donh, vc, NT) - d_n)
            dv = dv + _dot(pn, donh, TN)
            dk = dk + _dot(dsn, qnh, TN)
        dq_ref[...] = dq
        dk_ref[...] = dk
        dv_ref[...] = dv

    cur = pl.BlockSpec((DIL_NK, LANES), lambda n, r, p: (n, r * 4 + p))
    prev = pl.BlockSpec((DIL_NK, LANES), lambda n, r, p: (jnp.maximum(n - 1, 0), r * 4 + p))
    nxt = pl.BlockSpec((DIL_NK, LANES), lambda n, r, p: (jnp.minimum(n + 1, nb - 1), r * 4 + p))
    sds = jax.ShapeDtypeStruct
    return pl.pallas_call(
        body,
        name=name,
        grid=(nb, dil, 4),
        in_specs=[cur, nxt, cur, nxt, cur, nxt, cur, nxt, prev, cur, prev, cur],
        out_specs=[cur, cur, cur],
        out_shape=[sds(qv.shape, F32)] * 3,
        compiler_params=_cparams(("parallel", "parallel", "parallel")),
    )(qv, qv, dov, dov, lsev, lsev, ddv, ddv, kv, kv, vv, vv)


def dil_combine(os_, lses, z):
    def fn(ins, outs, _):
        l0, l1, l2 = ins[1][...], ins[3][...], ins[5][...]
        mx = jnp.maximum(jnp.maximum(l0, l1), l2)
        w0, w1, w2 = jnp.exp(l0 - mx), jnp.exp(l1 - mx), jnp.exp(l2 - mx)
        tot = w0 + w1 + w2
        o = (w0 * ins[0][...] + w1 * ins[2][...] + w2 * ins[4][...]) / tot
        outs[0][...] = (o * _silu_and_grad(ins[6][...])[0]).astype(MXU_DTYPE)
        outs[1][...] = o
        outs[2][...] = mx + jnp.log(tot)

    ins = []
    for o, l in zip(os_, lses):
        ins += [_row(o, LANES, 0, True), _row(l, LANES, 0, True)]
    ins.append(_row(z, LANES, 36, True))
    return rowwise("dil_combine", fn, z.shape[0], ncol=4, ins=ins,
                   outs=[(512, LANES, MXU_DTYPE, True), (512, LANES, F32, True), (512, LANES, F32, True)])


def _merge_tile(p0, p1, p2, z0, z1, z2, b0, b1, b2):
    return _sigmoid(z0 + b0) * p0 + _sigmoid(z1 + b1) * p1 + _sigmoid(z2 + b2) * p2


def _merge_ins(ps, z, b):
    w = 256
    return ([_row(p, w, 0, True) for p in ps] + [_row(z, w, 4 * i, True) for i in range(3)]
            + [_const(b, w, 4 * i, True) for i in range(3)])


def merge_fwd(ps, z, b):
    def fn(ins, outs, _):
        outs[0][...] = _merge_tile(*[r[...] for r in ins]).astype(MXU_DTYPE)

    return rowwise("merge_fwd", fn, z.shape[0], ncol=4, ins=_merge_ins(ps, z, b), outs=[(1024, 256, MXU_DTYPE, True)])[0]


def merge_bwd(dm, ps, z, b):
    def fn(ins, outs, accs):
        _, vjp = jax.vjp(_merge_tile, *[r[...] for r in ins[:9]])
        grads = vjp(ins[9][...])
        for i in range(3):
            outs[i][...] = grads[i].astype(MXU_DTYPE)
            outs[3 + i][...] = grads[3 + i].astype(MXU_DTYPE)
            accs[i][...] += grads[6 + i]

    return rowwise("merge_bwd", fn, z.shape[0], ncol=4, ins=_merge_ins(ps, z, b) + [_row(dm, 256, 0, True)],
                   outs=[(1024, 256, MXU_DTYPE, True)] * 6, accs=[(1, 1024, 256, True)] * 3)


def _row_tile(r, c):
    if r * c * 4 <= 2**21:
        return r
    for tr in (512, 256, 128, 64, 32, 16, 8):
        if r % tr == 0 and tr * c * 4 <= 2**21:
            return tr
    raise ValueError((r, c))


def adamw(name, w, g, m, v):
    shape = w.shape
    c = shape[-1]
    r = w.size // c
    tr = _row_tile(r, c)
    c1, c2 = 1.0 - ADAM_B1 ** ADAM_STEP, 1.0 - ADAM_B2 ** ADAM_STEP

    def body(w_ref, g_ref, m_ref, v_ref, d_ref, mo_ref, vo_ref):
        gv = g_ref[...]
        mn = ADAM_B1 * m_ref[...] + (1.0 - ADAM_B1) * gv
        vn = ADAM_B2 * v_ref[...] + (1.0 - ADAM_B2) * (gv * gv)
        d_ref[...] = -ADAM_LR * ((mn / c1) / (jnp.sqrt(vn / c2) + ADAM_EPS) + ADAM_WD * w_ref[...])
        mo_ref[...] = mn
        vo_ref[...] = vn

    spec = pl.BlockSpec((tr, c), lambda i: (i, 0))
    outs = pl.pallas_call(
        body,
        name=name,
        grid=(r // tr,),
        in_specs=[spec] * 4,
        out_specs=[spec] * 3,
        out_shape=[jax.ShapeDtypeStruct((r, c), F32)] * 3,
        compiler_params=_cparams(("parallel",)),
    )(*[a.reshape(r, c) for a in (w, g, m, v)])
    return [o.reshape(shape) for o in outs]


def sum_pair(name, g, la, c_idx):
    _, _, r, c = g.shape
    tr = _row_tile(r, c)

    def body(c_ref, g_ref, la_ref, o_ref):
        o_ref[...] = g_ref[...] + la_ref[...]

    return pl.pallas_call(
        body,
        name=name,
        grid_spec=pltpu.PrefetchScalarGridSpec(
            num_scalar_prefetch=1,
            grid=(4, r // tr),
            in_specs=[pl.BlockSpec((None, None, tr, c), lambda s, i, cr: (s, cr[0], i, 0)),
                      pl.BlockSpec((None, tr, c), lambda s, i, cr: (s, i, 0))],
            out_specs=pl.BlockSpec((None, tr, c), lambda s, i, cr: (s, i, 0)),
        ),
        out_shape=jax.ShapeDtypeStruct((4, r, c), F32),
        compiler_params=_cparams(("parallel", "parallel")),
    )(c_idx, g, la)


def sum_chips(name, pc, lb, s_idx):
    _, r, c = pc.shape
    tr = _row_tile(r, c)

    def body(s_ref, p_ref, l0_ref, l1_ref, l2_ref, o_ref):
        o_ref[...] = ((p_ref[...] + l0_ref[...]) + l1_ref[...]) + l2_ref[...]

    lspec = lambda k: pl.BlockSpec((None, tr, c), lambda i, sr: (k, i, 0))
    return pl.pallas_call(
        body,
        name=name,
        grid_spec=pltpu.PrefetchScalarGridSpec(
            num_scalar_prefetch=1,
            grid=(r // tr,),
            in_specs=[pl.BlockSpec((None, tr, c), lambda i, sr: (sr[0], i, 0)), lspec(0), lspec(1), lspec(2)],
            out_specs=pl.BlockSpec((tr, c), lambda i, sr: (i, 0)),
        ),
        out_shape=jax.ShapeDtypeStruct((r, c), F32),
        compiler_params=_cparams(("parallel",)),
    )(s_idx, pc, lb, lb, lb)


def _place():
    x, y, c = lax.axis_index("x"), lax.axis_index("y"), lax.axis_index("c")
    chips = [(1 - x, y), (x, 1 - y), (1 - x, 1 - y)]
    return x, y, c, chips


HBM_SPEC = pl.BlockSpec(memory_space=pltpu.HBM)


def _comm_call(name, body, ins, out_shapes, n_sem, n_local):
    return pl.pallas_call(
        body,
        name=name,
        in_specs=[HBM_SPEC] * len(ins),
        out_specs=[HBM_SPEC] * len(out_shapes),
        out_shape=out_shapes,
        scratch_shapes=[pltpu.SemaphoreType.DMA((n_sem,)), pltpu.SemaphoreType.DMA((n_sem,)),
                        pltpu.SemaphoreType.DMA((max(n_local, 1),))],
    )(*ins)


def allgather_weights(ws):
    n = len(ws)

    def body(*refs):
        ins, outs = refs[:n], refs[n:2 * n]
        send, recv, lsem = refs[2 * n:]
        x, y, c, chips = _place()
        s = 2 * x + y
        sib = (x, y, 1 - c)

        def rc(a, k, src, dst, dev):
            return pltpu.make_async_remote_copy(src_ref=src, dst_ref=dst, send_sem=send.at[6 * a + k], recv_sem=recv.at[6 * a + k],
                                                device_id=dev, device_id_type=MESH)

        local = [pltpu.make_async_copy(ins[a], outs[a].at[s], lsem.at[a]) for a in range(n)]
        for cp in local:
            cp.start()
        sends = []
        for j, chip in enumerate(chips):
            for a in range(n):
                sends.append(rc(a, j, ins[a].at[c], outs[a].at[s, c], (*chip, c)))
                sends[-1].start()
        for j, (cx, cy) in enumerate(chips):
            sj = 2 * cx + cy
            for a in range(n):
                landed = outs[a].at[sj, c]
                rc(a, j, landed, landed, sib).wait_recv()
                sends.append(rc(a, 3 + j, landed, landed, sib))
                sends[-1].start()
        for j, (cx, cy) in enumerate(chips):
            sj = 2 * cx + cy
            for a in range(n):
                other = outs[a].at[sj, 1 - c]
                rc(a, 3 + j, other, other, sib).wait_recv()
        for cp in sends:
            cp.wait_send()
        for cp in local:
            cp.wait()

    out_shapes = [jax.ShapeDtypeStruct((4,) + w.shape, w.dtype) for w in ws]
    return _comm_call("allgather_weights", body, ws, out_shapes, 6 * n, n)


def exchange_sibling(gs):
    n = len(gs)

    def body(*refs):
        ins, outs = refs[:n], refs[n:2 * n]
        send, recv, _ = refs[2 * n:]
        x, y, c, _ = _place()
        cps = []
        for a in range(n):
            for s in range(4):
                cps.append(pltpu.make_async_remote_copy(src_ref=ins[a].at[s, 1 - c], dst_ref=outs[a].at[s],
                                                        send_sem=send.at[4 * a + s], recv_sem=recv.at[4 * a + s],
                                                        device_id=(x, y, 1 - c), device_id_type=MESH))
                cps[-1].start()
        for cp in cps:
            cp.wait()

    out_shapes = [jax.ShapeDtypeStruct((4,) + g.shape[2:], g.dtype) for g in gs]
    return _comm_call("exchange_sibling", body, gs, out_shapes, 4 * n, 0)


def exchange_chips(ps):
    n = len(ps)

    def body(*refs):
        ins, outs = refs[:n], refs[n:2 * n]
        send, recv, _ = refs[2 * n:]
        x, y, c, chips = _place()
        cps = []
        for j, (cx, cy) in enumerate(chips):
            for a in range(n):
                cps.append(pltpu.make_async_remote_copy(src_ref=ins[a].at[2 * cx + cy], dst_ref=outs[a].at[j],
                                                        send_sem=send.at[3 * a + j], recv_sem=recv.at[3 * a + j],
                                                        device_id=(cx, cy, c), device_id_type=MESH))
                cps[-1].start()
        for cp in cps:
            cp.wait()

    out_shapes = [jax.ShapeDtypeStruct((3,) + p.shape[1:], p.dtype) for p in ps]
    return _comm_call("exchange_chips", body, ps, out_shapes, 3 * n, 0)


def exchange_final(rs, small):
    n = len(rs)

    def body(*refs):
        ins, small_ref = refs[:n], refs[n]
        outs, small_out = refs[n + 1:2 * n + 1], refs[2 * n + 1]
        send, recv, lsem = refs[2 * n + 2:]
        x, y, c, _ = _place()
        s = 2 * x + y
        cps, local = [], []
        for a in range(n):
            local.append(pltpu.make_async_copy(ins[a], outs[a].at[c], lsem.at[a]))
            local[-1].start()
            cps.append(pltpu.make_async_remote_copy(src_ref=ins[a], dst_ref=outs[a].at[c], send_sem=send.at[a], recv_sem=recv.at[a],
                                                    device_id=(x, y, 1 - c), device_id_type=MESH))
            cps[-1].start()
        local.append(pltpu.make_async_copy(small_ref, small_out.at[s, c], lsem.at[n]))
        local[-1].start()
        k = n
        for fx in (0, 1):
            for fy in (0, 1):
                for fc in (0, 1):
                    if fx or fy or fc:
                        dev = (x ^ fx, y ^ fy, c ^ fc)
                        cps.append(pltpu.make_async_remote_copy(src_ref=small_ref, dst_ref=small_out.at[s, c],
                                                                send_sem=send.at[k], recv_sem=recv.at[k],
                                                                device_id=dev, device_id_type=MESH))
                        cps[-1].start()
                        k += 1
        for cp in cps:
            cp.wait()
        for cp in local:
            cp.wait()

    out_shapes = [jax.ShapeDtypeStruct((2,) + r.shape, r.dtype) for r in rs]
    out_shapes.append(jax.ShapeDtypeStruct((4, 2) + small.shape, small.dtype))
    res = _comm_call("exchange_final", body, list(rs) + [small], out_shapes, n + 7, n + 1)
    return res[:n], res[n]


def _pad_in_cols(w):
    z = lambda n: jnp.zeros(w.shape[:-1] + (n,), w.dtype)
    return jnp.concatenate([w[..., :KR_OFF], z(64), w[..., KR_OFF:KR_OFF + MLA_ROPE], z(32), w[..., KR_OFF + MLA_ROPE:]], axis=-1)


def _unpad_in_cols(g):
    return jnp.concatenate([g[..., :KR_OFF], g[..., KR_OFF + 64:KR_OFF + 96], g[..., KR_OFF + 128:]], axis=-1)


def _pad_heads(w, real):
    k = w.shape[0]
    return jnp.pad(w.reshape(k, MLA_HEADS, real), ((0, 0), (0, 0), (0, LANES - real))).reshape(k, MLA_HEADS * LANES)


def _pad_gain(g, real):
    return jnp.pad(g.reshape(1, real), ((0, 0), (0, LANES - real)))


def layer_weights(l, full):
    w_ukv = full["w_ukv"][l].reshape(128, MLA_HEADS, 2, 64)
    two = lambda g: jnp.concatenate([g, g]).reshape(1, LANES)
    return dict(
        norm_g=full["norm_g"][l].reshape(1, -1), w_in=full["w_in"][l], conv_w=full["conv_w"][l], conv_b=full["conv_b"][l].reshape(1, -1),
        wgx=full["w_gate_x"][l], bgx=full["b_gate_x"][l].reshape(LRU_BLOCKS, 1, LANES),
        wga=full["w_gate_a"][l], bga=full["b_gate_a"][l].reshape(LRU_BLOCKS, 1, LANES),
        lam=full["lru_lambda"][l].reshape(1, -1), w_lru_o=full["w_lru_o"][l],
        cq_norm_g=full["cq_norm_g"][l].reshape(1, -1), ckv_norm_g=full["ckv_norm_g"][l].reshape(1, -1),
        wq=_pad_heads(full["w_uq"][l], MLA_QK), wk=_pad_heads(w_ukv[:, :, 0].reshape(128, 512), 64),
        wv=w_ukv[:, :, 1].reshape(128, 512),
        gq=_pad_gain(full["mla_q_norm_g"][l], MLA_QK), gk=_pad_gain(full["mla_k_norm_g"][l], MLA_QK),
        w_mla_o=full["w_mla_o"][l], gq2=two(full["dil_q_norm_g"][l]), gk2=two(full["dil_k_norm_g"][l]),
        w_dil_o=full["w_dil_o"][l], b_merge=full["b_merge"][l].reshape(1, -1), w_out=full["w_out"][l],
    )


def _strided(a, dil):
    return a.reshape(a.shape[0] // dil, dil * a.shape[1])


def layer_fwd(x, w, tabs):
    t = x.shape[0]
    h = rmsnorm_fwd(x, w["norm_g"])
    z_lru = mm_nn("in_proj_lru", h, w["w_in"], n=2048, n_off=G_LRU)
    z_mla = mm_nn("in_proj_mla", h, w["w_in"], n=1024, n_off=G_MLA)
    z_dil = mm_nn("in_proj_dil", h, w["w_in"], n=5120, n_off=G_DIL)
    z_mrg = mm_nn("in_proj_mrg", h, w["w_in"], n=3072, n_off=G_MRG)
    hs, y_lru = lru_fwd(z_lru, w["conv_w"], w["conv_b"], w["wgx"], w["bgx"], w["wga"], w["bga"], w["lam"])
    qm, km, vm = mla_prep_fwd(z_mla, tabs, w)
    o_mla, y_mla, lse_mla = mla_attn_fwd(qm, km, vm, z_mla)
    qd, kd = dil_prep_fwd(z_dil, tabs, w["gq2"], w["gk2"])
    os_, lses = [], []
    for gi, (window, dil) in enumerate(DIL_GROUPS):
        cols = slice(512 * gi, 512 * (gi + 1))
        vg = z_dil[:, 3072 + 512 * gi:3072 + 512 * (gi + 1)]
        o, lse = dil_attn_fwd(f"dil_attn_fwd_{dil}", _strided(qd[:, cols], dil), _strided(kd[:, cols], dil), _strided(vg, dil),
                              t // window, dil)
        os_.append(o.reshape(t, 512))
        lses.append(lse.reshape(t, 512))
    y_dil, o_dil, lse_dil = dil_combine(os_, lses, z_dil)
    ps = [mm_nn("proj_lru", y_lru, w["w_lru_o"]), mm_nn("proj_mla", y_mla, w["w_mla_o"]), mm_nn("proj_dil", y_dil, w["w_dil_o"])]
    merged = merge_fwd(ps, z_mrg, w["b_merge"])
    out = mm_nn("out_proj", merged, w["w_out"], add=x)
    res = dict(x=x, h=h, z_lru=z_lru, z_mla=z_mla, z_dil=z_dil, z_mrg=z_mrg, hs=hs, y_lru=y_lru, qm=qm, km=km, vm=vm, o_mla=o_mla,
               y_mla=y_mla, lse_mla=lse_mla, qd=qd, kd=kd, y_dil=y_dil, o_dil=o_dil, lse_dil=lse_dil, ps=ps, merged=merged)
    return out, res


def layer_bwd(dout, r, w, tabs):
    t = dout.shape[0]
    g = {}
    dmerged = mm_nt("out_proj_dx", dout, w["w_out"])
    g["w_out"] = mm_tn("out_proj_dw", r["merged"], dout)
    dp0, dp1, dp2, dzm0, dzm1, dzm2, db0, db1, db2 = merge_bwd(dmerged, r["ps"], r["z_mrg"], w["b_merge"])
    g["b_merge"] = jnp.concatenate([db0, db1, db2], axis=1).reshape(-1)
    dy_lru = mm_nt("proj_lru_dx", dp0, w["w_lru_o"])
    dy_mla = mm_nt("proj_mla_dx", dp1, w["w_mla_o"])
    dy_dil = mm_nt("proj_dil_dx", dp2, w["w_dil_o"])
    g["w_lru_o"] = mm_tn("proj_lru_dw", r["y_lru"], dp0)
    g["w_mla_o"] = mm_tn("proj_mla_dw", r["y_mla"], dp1)
    g["w_dil_o"] = mm_tn("proj_dil_dw", r["y_dil"], dp2)
    dzx, dzg_lru, dwgx, dbgx, dwga, dbga, dlam, dcw, dcb = lru_bwd(r["z_lru"], r["hs"], dy_lru, w["conv_w"], w["conv_b"], w["wgx"],
                                                                    w["bgx"], w["wga"], w["bga"], w["lam"])
    g.update(w_gate_x=dwgx, b_gate_x=dbgx.reshape(LRU_BLOCKS, LANES), w_gate_a=dwga, b_gate_a=dbga.reshape(LRU_BLOCKS, LANES),
             lru_lambda=dlam.reshape(-1), conv_w=dcw, conv_b=dcb.reshape(-1))
    do_m, dd_m, dzg_mla = gate_bwd("mla_gate_bwd", dy_mla, r["o_mla"], r["z_mla"], 4)
    dq_m = mla_attn_bwd_dq(r["qm"], r["km"], r["vm"], do_m, r["lse_mla"], dd_m)
    dk_m, dv_m = mla_attn_bwd_dkv(r["qm"], r["km"], r["vm"], do_m, r["lse_mla"], dd_m)
    dz_mla3, dg_cq, dg_ckv, dwq, dwk, dwv, dgq, dgk = mla_prep_bwd(r["z_mla"], tabs, w, dq_m, dk_m, dv_m)
    g.update(cq_norm_g=dg_cq.reshape(-1), ckv_norm_g=dg_ckv.reshape(-1), mla_q_norm_g=dgq[0, :MLA_QK], mla_k_norm_g=dgk[0, :MLA_QK])
    g["w_uq"] = dwq.reshape(256, MLA_HEADS, LANES)[:, :, :MLA_QK].reshape(256, MLA_HEADS * MLA_QK)
    g["w_ukv"] = jnp.concatenate([dwk.reshape(128, MLA_HEADS, LANES)[:, :, :64], dwv.reshape(128, MLA_HEADS, 64)], axis=2).reshape(128, 1024)
    do_d, dd_d, dzg_dil = gate_bwd("dil_gate_bwd", dy_dil, r["o_dil"], r["z_dil"], 36)
    dqs, dks, dvs = [], [], []
    for gi, (window, dil) in enumerate(DIL_GROUPS):
        cols = slice(512 * gi, 512 * (gi + 1))
        vg = r["z_dil"][:, 3072 + 512 * gi:3072 + 512 * (gi + 1)]
        dq, dk, dv = dil_attn_bwd(f"dil_attn_bwd_{dil}", _strided(r["qd"][:, cols], dil), _strided(r["kd"][:, cols], dil),
                                  _strided(vg, dil), _strided(do_d, dil), _strided(r["lse_dil"], dil), _strided(dd_d, dil),
                                  t // window, dil)
        dqs.append(dq.reshape(t, 512))
        dks.append(dk.reshape(t, 512))
        dvs.append(dv.reshape(t, 512).astype(MXU_DTYPE))
    dzq, dzk, dgq2, dgk2 = dil_prep_bwd(r["z_dil"], tabs, w["gq2"], w["gk2"], jnp.concatenate(dqs, axis=1), jnp.concatenate(dks, axis=1))
    g.update(dil_q_norm_g=dgq2[0, :DIL_HD], dil_k_norm_g=dgk2[0, :DIL_HD])
    dz = jnp.concatenate([dzx, dzg_lru, dz_mla3, dzg_mla, dzq, dzk] + dvs + [dzg_dil, dzm0, dzm1, dzm2], axis=1)
    dh = mm_nt("in_proj_dx", dz, w["w_in"])
    g["w_in"] = _unpad_in_cols(mm_tn("in_proj_dw", r["h"], dz))
    dx, dng = rmsnorm_bwd(r["x"], dh, dout, w["norm_g"])
    g["norm_g"] = dng.reshape(-1)
    return dx, g


def local_step(x, positions, target, full):
    tabs = rope_tables(positions.reshape(-1, 1))
    ws, ress = [], []
    for l in range(2):
        ws.append(layer_weights(l, full))
        x, res = layer_fwd(x, ws[l], tabs)
        ress.append(res)
    dy, loss = loss_head(x, target)
    grads = [None, None]
    for l in (1, 0):
        dy, grads[l] = layer_bwd(dy, ress[l], ws[l], tabs)
    return loss, dy, {k: jnp.stack([grads[0][k], grads[1][k]]) for k in grads[0]}


WEIGHTS = ["norm_g", "w_in", "conv_w", "conv_b", "w_gate_x", "b_gate_x", "w_gate_a", "b_gate_a", "lru_lambda", "w_lru_o", "cq_norm_g",
           "ckv_norm_g", "w_uq", "w_ukv", "mla_q_norm_g", "mla_k_norm_g", "w_mla_o", "dil_q_norm_g", "dil_k_norm_g", "w_dil_o", "b_merge",
           "w_out"]
SHARDED = {"w_in": 2, "conv_w": 2, "w_lru_o": 1, "w_uq": 2, "w_ukv": 2, "w_mla_o": 2, "w_dil_o": 2, "w_out": 1}
REPLICATED = [n for n in WEIGHTS if n not in SHARDED]
SMALL_ROWS = 72


def _shard_view(a):
    return a


def kernel(x, positions, norm_g, w_in, conv_w, conv_b, w_gate_x, b_gate_x, w_gate_a, b_gate_a, lru_lambda, w_lru_o, cq_norm_g, ckv_norm_g, w_uq, w_ukv, mla_q_norm_g, mla_k_norm_g, w_mla_o, dil_q_norm_g, dil_k_norm_g, w_dil_o, b_merge, w_out, loss_target, m_norm_g, m_w_in, m_conv_w, m_conv_b, m_w_gate_x, m_b_gate_x, m_w_gate_a, m_b_gate_a, m_lru_lambda, m_w_lru_o, m_cq_norm_g, m_ckv_norm_g, m_w_uq, m_w_ukv, m_mla_q_norm_g, m_mla_k_norm_g, m_w_mla_o, m_dil_q_norm_g, m_dil_k_norm_g, m_w_dil_o, m_b_merge, m_w_out, v_norm_g, v_w_in, v_conv_w, v_conv_b, v_w_gate_x, v_b_gate_x, v_w_gate_a, v_b_gate_a, v_lru_lambda, v_w_lru_o, v_cq_norm_g, v_ckv_norm_g, v_w_uq, v_w_ukv, v_mla_q_norm_g, v_mla_k_norm_g, v_w_mla_o, v_dil_q_norm_g, v_dil_k_norm_g, v_w_dil_o, v_b_merge, v_w_out):
    args = locals()
    w = {n: args[n] for n in WEIGHTS}
    m = {n: args["m_" + n] for n in WEIGHTS}
    v = {n: args["v_" + n] for n in WEIGHTS}
    c_idx = lax.axis_index("c").astype(jnp.int32).reshape(1)
    s_idx = (2 * lax.axis_index("x") + lax.axis_index("y")).astype(jnp.int32).reshape(1)

    names = list(SHARDED)
    wire = [w[n] if n == "conv_w" else w[n].astype(BF16) for n in names]
    gathered = allgather_weights(wire)
    full = {n: w[n] for n in REPLICATED}
    for n, ga in zip(names, gathered):
        full[n] = jnp.concatenate([ga[s] for s in range(4)], axis=SHARDED[n])
    full["w_in"] = _pad_in_cols(full["w_in"])

    loss, grad_x, grads = local_step(x[0], positions[0], loss_target[0], full)
    loss = lax.psum(loss, ("x", "y", "c"))

    flat = jnp.concatenate([grads[n].reshape(-1) for n in REPLICATED])
    small = jnp.pad(flat, (0, 8 * SMALL_ROWS * 1024 - flat.size)).reshape(4, 2, SMALL_ROWS, 1024)
    gs = []
    for n in names:
        parts = jnp.stack(jnp.split(grads[n], 4, axis=SHARDED[n]))
        gs.append(parts.reshape(4, 2, -1, parts.shape[-1]))
    gs.append(small)
    tags = names + ["small"]
    la = exchange_sibling(gs)
    pcs = [sum_pair(f"sum_pair_{t}", g_, l_, c_idx) for t, g_, l_ in zip(tags, gs, la)]
    lb = exchange_chips(pcs)
    rs = [sum_chips(f"sum_chips_{t}", p_, l_, s_idx) for t, p_, l_ in zip(tags, pcs, lb)]
    reduced, small_all = exchange_final(rs[:-1], rs[-1])
    g_local = {n: r.reshape(w[n].shape) for n, r in zip(names, reduced)}
    flat = small_all.reshape(-1)
    off = 0
    for n in REPLICATED:
        g_local[n] = flat[off:off + w[n].size].reshape(w[n].shape)
        off += w[n].size

    delta, new_m, new_v = {}, {}, {}
    for n in WEIGHTS:
        delta[n], new_m[n], new_v[n] = adamw(f"adamw_{n}", w[n], g_local[n], m[n], v[n])
    return (loss, grad_x[None], *[g_local[n] for n in WEIGHTS], *[delta[n] for n in WEIGHTS], *[new_m[n] for n in WEIGHTS],
            *[new_v[n] for n in WEIGHTS])
```

```python
import functools

import jax
import jax.numpy as jnp
from jax import lax
from jax.experimental import pallas as pl
from jax.experimental.pallas import tpu as pltpu

F32 = jnp.float32
BF16 = jnp.bfloat16
MXU_DTYPE = jnp.bfloat16

D_MODEL = 1024
EPS = 1e-6
ROPE_THETA = 10000.0
LRU_BLOCKS = 8
LRU_C = 8.0
MLA_HEADS = 8
MLA_NOPE = 64
MLA_ROPE = 32
MLA_QK = 96
DIL_GROUPS = ((128, 1), (512, 4), (2048, 16))
DIL_HD = 64
DIL_NK = 128
IN_WIDTH = 11168
ADAM_LR, ADAM_B1, ADAM_B2, ADAM_EPS, ADAM_WD, ADAM_STEP = 0.001, 0.9, 0.999, 1e-08, 0.01, 10

LANES = 128
G_LRU, G_MLA, G_DIL, G_MRG = 0, 2048, 3072, 8192
IN_PAD = 11264
KR_OFF = 2432

NN = (((1,), (0,)), ((), ()))
NT = (((1,), (1,)), ((), ()))
TN = (((0,), (0,)), ((), ()))
NEG = -1e30
MESH = pl.DeviceIdType.MESH
VMEM_LIMIT = 48 * 2**20


def _cparams(sem):
    return pltpu.CompilerParams(dimension_semantics=sem, vmem_limit_bytes=VMEM_LIMIT)


def _dot(a, b, dims):
    return lax.dot_general(a.astype(MXU_DTYPE), b.astype(MXU_DTYPE), dims, preferred_element_type=F32)


@jax.custom_vjp
def mm(a, w):
    return _dot(a, w, NN)


def _mm_fwd(a, w):
    return _dot(a, w, NN), (a, w)


def _mm_bwd(res, g):
    a, w = res
    return _dot(g, w, NT), _dot(a, g, TN)


mm.defvjp(_mm_fwd, _mm_bwd)


def _seg64_matrix():
    r = lax.broadcasted_iota(jnp.int32, (LANES, LANES), 0) < DIL_HD
    c = lax.broadcasted_iota(jnp.int32, (LANES, LANES), 1) < DIL_HD
    return (r == c).astype(BF16)


def _seg_sum_impl(x):
    b = _seg64_matrix()
    hi = x.astype(BF16)
    r1 = x - hi.astype(F32)
    mid = r1.astype(BF16)
    lo = (r1 - mid.astype(F32)).astype(BF16)
    dot = lambda u: lax.dot_general(u, b, NN, preferred_element_type=F32)
    return dot(hi) + dot(mid) + dot(lo)


@jax.custom_vjp
def seg_sum(x):
    return _seg_sum_impl(x)


seg_sum.defvjp(lambda x: (_seg_sum_impl(x), None), lambda _, g: (_seg_sum_impl(g),))


def _lroll_impl(x, s):
    return pltpu.roll(x, s % LANES, 1)


@functools.partial(jax.custom_vjp, nondiff_argnums=(1,))
def lroll(x, s):
    return _lroll_impl(x, s)


lroll.defvjp(lambda x, s: (_lroll_impl(x, s), None), lambda s, _, g: (_lroll_impl(g, -s),))


class _Ops:
    def __init__(self, diff):
        self.mm = mm if diff else (lambda a, w: _dot(a, w, NN))
        self.seg_sum = seg_sum if diff else _seg_sum_impl
        self.lroll = lroll if diff else _lroll_impl


PLAIN, DIFF = _Ops(False), _Ops(True)


def rms(x, g, n):
    ms = jnp.sum(x * x, axis=-1, keepdims=True) * (1.0 / n)
    return x * lax.rsqrt(ms + EPS) * g


def rope(ops, x, c, s1, s2, half):
    return x * c + ops.lroll(x, -half) * s1 + ops.lroll(x, half) * s2


def _sigmoid(x):
    return 1.0 / (1.0 + jnp.exp(-x))


def _silu_and_grad(g):
    sg = _sigmoid(g)
    return g * sg, sg * (1.0 + g * (1.0 - sg))


def _softplus(x):
    return jnp.maximum(x, 0.0) + jnp.log(1.0 + jnp.exp(-jnp.abs(x)))


def _expm1(y):
    series = y * (1.0 + y * (0.5 + y * (1.0 / 6.0 + y * (1.0 / 24.0 + y * (1.0 / 120.0)))))
    return jnp.where(y > -0.05, series, jnp.exp(jnp.minimum(y, -0.05)) - 1.0)


def _mm_call(name, a, b, *, mode, m, n, k, a_blk, b_blk, out_dtype, add, tm, tn, tk):
    nk = k // tk
    assert m % tm == 0 and n % tn == 0 and k % tk == 0, (name, m, n, k, tm, tn, tk)
    dims = {"nn": NN, "nt": NT, "tn": TN}[mode]

    def body(*refs):
        if add is None:
            a_ref, b_ref, o_ref, *scr = refs
            add_ref = None
        else:
            a_ref, b_ref, add_ref, o_ref, *scr = refs
        part = _dot(a_ref[...], b_ref[...], dims)

        def finish(acc):
            if add_ref is not None:
                acc = acc + add_ref[...]
            o_ref[...] = acc.astype(o_ref.dtype)

        if nk == 1:
            finish(part)
        else:
            (acc_ref,) = scr
            kk = pl.program_id(2)

            @pl.when(kk == 0)
            def _():
                acc_ref[...] = part

            @pl.when(kk > 0)
            def _():
                acc_ref[...] += part

            @pl.when(kk == nk - 1)
            def _():
                finish(acc_ref[...])

    in_specs = [a_blk, b_blk]
    args = [a, b]
    if add is not None:
        in_specs.append(pl.BlockSpec((tm, tn), lambda j, i, kk: (i, j)))
        args.append(add)
    return pl.pallas_call(
        body,
        name=name,
        grid=(n // tn, m // tm, nk),
        in_specs=in_specs,
        out_specs=pl.BlockSpec((tm, tn), lambda j, i, kk: (i, j)),
        out_shape=jax.ShapeDtypeStruct((m, n), out_dtype),
        scratch_shapes=[] if nk == 1 else [pltpu.VMEM((tm, tn), F32)],
        compiler_params=_cparams(("parallel", "parallel", "arbitrary")),
    )(*args)


def mm_nn(name, a, b, *, n=None, n_off=0, out_dtype=F32, add=None, tm=512, tn=1024):
    m, k = a.shape
    n = b.shape[1] if n is None else n
    tn, tk = min(tn, n), min(k, 1024)
    ob = n_off // tn
    assert n_off % tn == 0
    return _mm_call(name, a, b, mode="nn", m=m, n=n, k=k, out_dtype=out_dtype, add=add, tm=tm, tn=tn, tk=tk,
                    a_blk=pl.BlockSpec((tm, tk), lambda j, i, kk: (i, kk)),
                    b_blk=pl.BlockSpec((tk, tn), lambda j, i, kk: (kk, j + ob)))


def mm_nt(name, a, b, *, out_dtype=F32, tm=512, tn=1024):
    m, k = a.shape
    n = b.shape[0]
    tn, tk = min(tn, n), min(k, 1024)
    return _mm_call(name, a, b, mode="nt", m=m, n=n, k=k, out_dtype=out_dtype, add=None, tm=tm, tn=tn, tk=tk,
                    a_blk=pl.BlockSpec((tm, tk), lambda j, i, kk: (i, kk)),
                    b_blk=pl.BlockSpec((tn, tk), lambda j, i, kk: (j, kk)))


def mm_tn(name, a, b, *, tn=512, tk=512):
    t, m = a.shape
    n = b.shape[1]
    tm, tn = min(m, 1024), min(tn, n)
    return _mm_call(name, a, b, mode="tn", m=m, n=n, k=t, out_dtype=F32, add=None, tm=tm, tn=tn, tk=tk,
                    a_blk=pl.BlockSpec((tk, tm), lambda j, i, kk: (kk, i)),
                    b_blk=pl.BlockSpec((tk, tn), lambda j, i, kk: (kk, j)))


def rowwise(name, fn, t, *, tm=256, ncol=1, ins=(), outs=(), touts=(), accs=()):
    n_in, n_out, n_acc = len(ins), len(outs) + len(touts), len(accs)

    def zero_when(ref, cond):
        @pl.when(cond)
        def _():
            ref[...] = jnp.zeros(ref.shape, ref.dtype)

    def body(*refs):
        in_refs, out_refs, acc_refs = refs[:n_in], refs[n_in:n_in + n_out], refs[n_in + n_out:]
        j, i = pl.program_id(0), pl.program_id(1)
        for ref, (_, _, _, cd) in zip(acc_refs, accs):
            zero_when(ref, (i == 0) if cd else ((i == 0) & (j == 0)))
        fn(in_refs, out_refs, acc_refs)

    def in_spec(arr, w, base, cd, rd):
        rows = tm if rd else arr.shape[0]
        return pl.BlockSpec((rows, w), lambda j, i: (i if rd else 0, base + (j if cd else 0)))

    in_specs = [in_spec(*e) for e in ins]
    out_specs = [pl.BlockSpec((tm, w), (lambda j, i, cd=cd: (i, j if cd else 0))) for (_, w, _, cd) in outs]
    out_specs += [pl.BlockSpec((r, tm), lambda j, i: (0, i)) for (r, _) in touts]
    out_specs += [pl.BlockSpec((r, w), (lambda j, i, cd=cd: (0, j if cd else 0))) for (r, _, w, cd) in accs]
    out_shape = [jax.ShapeDtypeStruct((t, c), dt) for (c, _, dt, _) in outs]
    out_shape += [jax.ShapeDtypeStruct((r, t), dt) for (r, dt) in touts]
    out_shape += [jax.ShapeDtypeStruct((r, c), F32) for (r, c, _, _) in accs]
    res = pl.pallas_call(
        body,
        name=name,
        grid=(ncol, t // tm),
        in_specs=in_specs,
        out_specs=out_specs,
        out_shape=out_shape,
        compiler_params=_cparams(("arbitrary", "arbitrary") if accs else ("parallel", "parallel")),
    )(*[e[0] for e in ins])
    return res


def _row(arr, w=None, base=0, cd=False):
    return (arr, arr.shape[1] if w is None else w, base, cd, True)


def _const(arr, w=None, base=0, cd=False):
    return (arr, arr.shape[1] if w is None else w, base, cd, False)


def rope_tables(positions):
    t = positions.shape[0]

    def fn(ins, outs, _):
        pos = ins[0][...].astype(F32)
        lane = lax.broadcasted_iota(jnp.int32, (1, LANES), 1)
        log_theta = jnp.log(jnp.float32(ROPE_THETA))
        jm = lane - MLA_NOPE
        idx = jnp.clip(jnp.where(jm < 16, jm, jm - 16), 0, 15).astype(F32)
        ang = pos * jnp.exp(-(idx * (2.0 / MLA_ROPE)) * log_theta)
        cos, sin = jnp.cos(ang), jnp.sin(ang)
        in_rope = (lane >= MLA_NOPE) & (lane < MLA_QK)
        outs[0][:, 0:128] = jnp.where(lane < MLA_NOPE, 1.0, jnp.where(in_rope, cos, 0.0))
        outs[0][:, 128:256] = jnp.where(in_rope & (jm < 16), -sin, 0.0)
        outs[0][:, 256:384] = jnp.where(in_rope & (jm >= 16), sin, 0.0)
        jd = lane & (DIL_HD - 1)
        idx = (jd & 31).astype(F32)
        ang = pos * jnp.exp(-(idx * (2.0 / DIL_HD)) * log_theta)
        cos, sin = jnp.cos(ang), jnp.sin(ang)
        outs[0][:, 384:512] = cos
        outs[0][:, 512:640] = jnp.where(jd < 32, -sin, 0.0)
        outs[0][:, 640:768] = jnp.where(jd >= 32, sin, 0.0)

    return rowwise("rope_tables", fn, t, ins=[_row(positions)], outs=[(768, 768, F32, False)])[0]


def rmsnorm_fwd(x, g):
    def fn(ins, outs, _):
        outs[0][...] = rms(ins[0][...], ins[1][...], D_MODEL).astype(MXU_DTYPE)

    return rowwise("rmsnorm_fwd", fn, x.shape[0], ins=[_row(x), _const(g)], outs=[(D_MODEL, D_MODEL, MXU_DTYPE, False)])[0]


def rmsnorm_bwd(x, dh, dres, g):
    def fn(ins, outs, accs):
        _, vjp = jax.vjp(lambda xv, gv: rms(xv, gv, D_MODEL), ins[0][...], ins[3][...])
        dx, dg = vjp(ins[1][...])
        outs[0][...] = ins[2][...] + dx
        accs[0][...] += dg

    return rowwise("rmsnorm_bwd", fn, x.shape[0], ins=[_row(x), _row(dh), _row(dres), _const(g)],
                   outs=[(D_MODEL, D_MODEL, F32, False)], accs=[(1, D_MODEL, D_MODEL, False)])


def loss_head(y, target):
    def fn(ins, outs, accs):
        err = ins[0][...] - ins[1][...]
        outs[0][...] = err * (1.0 / D_MODEL)
        accs[0][...] += jnp.sum(err * err, axis=0, keepdims=True)
        accs[1][...] = jnp.broadcast_to(jnp.sum(accs[0][...], keepdims=True), (1, LANES))

    dy, _, tot = rowwise("loss_head", fn, y.shape[0], ins=[_row(y), _row(target)], outs=[(D_MODEL, D_MODEL, F32, False)],
                         accs=[(1, D_MODEL, D_MODEL, False), (1, LANES, LANES, False)])
    return dy, tot[0, 0] * (0.5 / D_MODEL)


def _shift_rows(v, d, fill, reverse):
    tb = v.shape[0]
    rows = lax.broadcasted_iota(jnp.int32, v.shape, 0)
    if not reverse:
        return jnp.where(rows >= d, pltpu.roll(v, d, 0), fill)
    return jnp.where(rows < tb - d, pltpu.roll(v, tb - d, 0), fill)


def _scan_tile(a, b, reverse):
    d = 1
    while d < a.shape[0]:
        b = b + a * _shift_rows(b, d, 0.0, reverse)
        a = a * _shift_rows(a, d, 1.0, reverse)
        d *= 2
    return a, b


def _lru_gates(ops, xc, wgx, bgx, wga, bga, lam):
    gx = _sigmoid(ops.mm(xc, wgx) + bgx)
    ga = _sigmoid(ops.mm(xc, wga) + bga)
    log_a = -LRU_C * ga * _softplus(-lam)
    a = jnp.exp(log_a)
    mult = jnp.sqrt(-_expm1(2.0 * log_a))
    return a, mult * (gx * xc)


def _shifted_inputs(x, halo, tb):
    rows = lax.broadcasted_iota(jnp.int32, x.shape, 0)
    pad = jnp.zeros((tb - 8, LANES), F32)
    out = []
    for d in (3, 2, 1):
        head = jnp.concatenate([pltpu.roll(halo, d, 0), pad], axis=0)
        out.append(jnp.where(rows >= d, pltpu.roll(x, d, 0), head))
    return out + [x]


def _lru_specs(nt, tb, reverse):
    hb = tb // 8
    tt = (lambda t: nt - 1 - t) if reverse else (lambda t: t)
    blk = lambda off: pl.BlockSpec((tb, LANES), lambda n, t: (tt(t), n + off))
    halo = lambda off: pl.BlockSpec((8, LANES), lambda n, t: (jnp.maximum(tt(t) * hb - 1, 0), n + off))
    chan = lambda r: pl.BlockSpec((r, LANES), lambda n, t: (0, n))
    wblk = pl.BlockSpec((None, LANES, LANES), lambda n, t: (n, 0, 0))
    bblk = pl.BlockSpec((None, 1, LANES), lambda n, t: (n, 0, 0))
    return blk, halo, chan, wblk, bblk


def lru_fwd(z, conv_w, conv_b, wgx, bgx, wga, bga, lam, *, tb=256):
    t = z.shape[0]
    nt = t // tb
    blk, halo, chan, wblk, bblk = _lru_specs(nt, tb, False)

    def body(x_ref, xh_ref, g_ref, cw_ref, cb_ref, wgx_ref, bgx_ref, wga_ref, bga_ref, lam_ref, h_ref, y_ref, carry_ref):
        ti = pl.program_id(1)

        @pl.when(ti == 0)
        def _():
            carry_ref[...] = jnp.zeros((8, LANES), F32)

        x = x_ref[...]
        hal = jnp.where(ti > 0, xh_ref[...], 0.0)
        xs = _shifted_inputs(x, hal, tb)
        xc = cb_ref[...] + sum(xs[k] * cw_ref[k:k + 1, :] for k in range(4))
        a, b = _lru_gates(PLAIN, xc, wgx_ref[...], bgx_ref[...], wga_ref[...], bga_ref[...], lam_ref[...])
        acum, h0 = _scan_tile(a, b, False)
        h = h0 + acum * carry_ref[7:8, :]
        carry_ref[...] = h[tb - 8:tb, :]
        h_ref[...] = h
        y_ref[...] = (h * _silu_and_grad(g_ref[...])[0]).astype(MXU_DTYPE)

    return pl.pallas_call(
        body,
        name="lru_fwd",
        grid=(LRU_BLOCKS, nt),
        in_specs=[blk(0), halo(0), blk(LRU_BLOCKS), chan(4), chan(1), wblk, bblk, wblk, bblk, chan(1)],
        out_specs=[blk(0), blk(0)],
        out_shape=[jax.ShapeDtypeStruct((t, 1024), F32), jax.ShapeDtypeStruct((t, 1024), MXU_DTYPE)],
        scratch_shapes=[pltpu.VMEM((8, LANES), F32)],
        compiler_params=_cparams(("parallel", "arbitrary")),
    )(z, z, z, conv_w, conv_b, wgx, bgx, wga, bga, lam)


def lru_bwd(z, h, dy, conv_w, conv_b, wgx, bgx, wga, bga, lam, *, tb=256):
    t = z.shape[0]
    nt = t // tb
    blk, halo, chan, wblk, bblk = _lru_specs(nt, tb, True)

    def body(x_ref, xh_ref, g_ref, h_ref, hh_ref, dy_ref, cw_ref, cb_ref, wgx_ref, bgx_ref, wga_ref, bga_ref, lam_ref,
             dzx_ref, dzg_ref, dwgx_ref, dbgx_ref, dwga_ref, dbga_ref, dlam_ref, dcw_ref, dcb_ref,
             gcar_ref, acar_ref, xcar_ref):
        ti = pl.program_id(1)
        has_earlier = ti < nt - 1

        @pl.when(ti == 0)
        def _():
            for ref in (dwgx_ref, dbgx_ref, dwga_ref, dbga_ref, dlam_ref, dcw_ref, dcb_ref, gcar_ref, acar_ref, xcar_ref):
                ref[...] = jnp.zeros(ref.shape, F32)

        rows = lax.broadcasted_iota(jnp.int32, (tb, LANES), 0)
        x = x_ref[...]
        hal = jnp.where(has_earlier, xh_ref[...], 0.0)
        xs = _shifted_inputs(x, hal, tb)
        xc = cb_ref[...] + sum(xs[k] * cw_ref[k:k + 1, :] for k in range(4))
        (a, _), vjp = jax.vjp(functools.partial(_lru_gates, DIFF), xc, wgx_ref[...], bgx_ref[...], wga_ref[...],
                              bga_ref[...], lam_ref[...])
        g, h, dyv = g_ref[...], h_ref[...], dy_ref[...]
        silu, dsilu = _silu_and_grad(g)
        dzg_ref[...] = (dyv * h * dsilu).astype(MXU_DTYPE)
        a_next = jnp.where(rows < tb - 1, pltpu.roll(a, tb - 1, 0), acar_ref[0:1, :])
        acum, g0 = _scan_tile(a_next, dyv * silu, True)
        gt = g0 + acum * gcar_ref[0:1, :]
        h_prev = jnp.where(rows >= 1, pltpu.roll(h, 1, 0), jnp.where(has_earlier, hh_ref[7:8, :], 0.0))
        dxc, dwgx, dbgx, dwga, dbga, dlam = vjp((gt * h_prev, gt))
        later = xcar_ref[...]
        gcar_ref[...] = gt[0:8, :]
        acar_ref[...] = a[0:8, :]
        xcar_ref[...] = dxc[0:8, :]
        dx = dxc * cw_ref[3:4, :]
        pad = jnp.zeros((tb - 8, LANES), F32)
        for d in (1, 2, 3):
            tail = jnp.concatenate([pad, pltpu.roll(later, 8 - d, 0)], axis=0)
            up = jnp.where(rows < tb - d, pltpu.roll(dxc, tb - d, 0), tail)
            dx = dx + up * cw_ref[3 - d:4 - d, :]
        dzx_ref[...] = dx.astype(MXU_DTYPE)
        for k in range(4):
            dcw_ref[k:k + 1, :] += jnp.sum(dxc * xs[k], axis=0, keepdims=True)
        dcb_ref[...] += jnp.sum(dxc, axis=0, keepdims=True)
        dwgx_ref[...] += dwgx
        dbgx_ref[...] += dbgx
        dwga_ref[...] += dwga
        dbga_ref[...] += dbga
        dlam_ref[...] += dlam

    sds = jax.ShapeDtypeStruct
    return pl.pallas_call(
        body,
        name="lru_bwd",
        grid=(LRU_BLOCKS, nt),
        in_specs=[blk(0), halo(0), blk(LRU_BLOCKS), blk(0), halo(0), blk(0), chan(4), chan(1), wblk, bblk, wblk, bblk, chan(1)],
        out_specs=[blk(0), blk(0), wblk, bblk, wblk, bblk, chan(1), chan(4), chan(1)],
        out_shape=[sds((t, 1024), MXU_DTYPE), sds((t, 1024), MXU_DTYPE), sds(wgx.shape, F32), sds(bgx.shape, F32), sds(wga.shape, F32),
                   sds(bga.shape, F32), sds((1, 1024), F32), sds((4, 1024), F32), sds((1, 1024), F32)],
        scratch_shapes=[pltpu.VMEM((8, LANES), F32)] * 3,
        compiler_params=_cparams(("parallel", "arbitrary")),
    )(z, z, z, h, h, dy, conv_w, conv_b, wgx, bgx, wga, bga, lam)


def _mla_prep_tile(ops, cq, ckv, krp, c, s1, s2, g_cq, g_ckv, wq, wk, wv, gq, gk):
    cqn = rms(cq, g_cq, 256)
    ckvn = rms(ckv, g_ckv, 128)
    v = ops.mm(ckvn, wv)
    qs, ks = [], []
    for hd in range(MLA_HEADS):
        q = rms(ops.mm(cqn, wq[hd]), gq, MLA_QK)
        k = rms(ops.mm(ckvn, wk[hd]) + krp, gk, MLA_QK)
        qs.append(rope(ops, q, c, s1, s2, 16) * (MLA_QK ** -0.5))
        ks.append(rope(ops, k, c, s1, s2, 16))
    return tuple(qs), tuple(ks), v


def _mla_prep_args(ins):
    z_cq, z_ckv, z_kr, tc, ts1, ts2, g_cq, g_ckv, wq, wk, wv, gq, gk = ins[:13]
    heads = lambda w: tuple(w[:, LANES * hd:LANES * (hd + 1)] for hd in range(MLA_HEADS))
    return (z_cq[...], z_ckv[...], z_kr[...], tc[...], ts1[...], ts2[...], g_cq[...], g_ckv[...], heads(wq), heads(wk),
            wv[...], gq[...], gk[...])


def _mla_prep_ins(z, tabs, w):
    return [_row(z, 256, 0), _row(z, 128, 2), _row(z, 128, 3), _row(tabs, 128, 0), _row(tabs, 128, 1), _row(tabs, 128, 2),
            _const(w["cq_norm_g"]), _const(w["ckv_norm_g"]), _const(w["wq"]), _const(w["wk"]), _const(w["wv"]),
            _const(w["gq"]), _const(w["gk"])]


def mla_prep_fwd(z, tabs, w):
    def fn(ins, outs, _):
        qs, ks, v = _mla_prep_tile(PLAIN, *_mla_prep_args(ins))
        for hd in range(MLA_HEADS):
            outs[0][:, LANES * hd:LANES * (hd + 1)] = qs[hd].astype(MXU_DTYPE)
            outs[1][:, LANES * hd:LANES * (hd + 1)] = ks[hd].astype(MXU_DTYPE)
        outs[2][...] = v.astype(MXU_DTYPE)
        outs[3][...] = v.T.astype(MXU_DTYPE)

    return rowwise("mla_prep_fwd", fn, z.shape[0], ins=_mla_prep_ins(z, tabs, w),
                   outs=[(1024, 1024, MXU_DTYPE, False), (1024, 1024, MXU_DTYPE, False), (512, 512, MXU_DTYPE, False)],
                   touts=[(512, MXU_DTYPE)])


def mla_prep_bwd(z, tabs, w, dq, dk, dv):
    def fn(ins, outs, accs):
        args = _mla_prep_args(ins)
        _, vjp = jax.vjp(functools.partial(_mla_prep_tile, DIFF), *args)
        heads = lambda ref: tuple(ref[:, LANES * hd:LANES * (hd + 1)] for hd in range(MLA_HEADS))
        dcq, dckv, dkr, _, _, _, dg_cq, dg_ckv, dwq, dwk, dwv, dgq, dgk = vjp((heads(ins[13]), heads(ins[14]), ins[15][...]))
        lane = lax.broadcasted_iota(jnp.int32, (1, LANES), 1)
        outs[0][:, 0:256] = dcq.astype(MXU_DTYPE)
        outs[0][:, 256:384] = dckv.astype(MXU_DTYPE)
        outs[0][:, 384:512] = jnp.where((lane >= MLA_NOPE) & (lane < MLA_QK), dkr, 0.0).astype(MXU_DTYPE)
        accs[0][...] += dg_cq
        accs[1][...] += dg_ckv
        for hd in range(MLA_HEADS):
            accs[2][:, LANES * hd:LANES * (hd + 1)] += dwq[hd]
            accs[3][:, LANES * hd:LANES * (hd + 1)] += dwk[hd]
        accs[4][...] += dwv
        accs[5][...] += dgq
        accs[6][...] += dgk

    return rowwise("mla_prep_bwd", fn, z.shape[0], ins=_mla_prep_ins(z, tabs, w) + [_row(dq), _row(dk), _row(dv)],
                   outs=[(512, 512, MXU_DTYPE, False)],
                   accs=[(1, 256, 256, False), (1, 128, 128, False), (256, 1024, 1024, False), (128, 1024, 1024, False),
                         (128, 512, 512, False), (1, 128, 128, False), (1, 128, 128, False)])


def _head_masks():
    lane = lax.broadcasted_iota(jnp.int32, (1, LANES), 1)
    return (lane < DIL_HD, lane >= DIL_HD)


def _row_scalar(tile, mask):
    return jnp.max(jnp.where(mask, tile, -jnp.inf), axis=-1, keepdims=True)


def _causal_tiles(nq, by_key):
    pairs = [(i, j) for i in range(nq) for j in range(i + 1)]
    if by_key:
        pairs.sort(key=lambda ij: (ij[1], ij[0]))
    return (jnp.asarray([ij[0] for ij in pairs], jnp.int32), jnp.asarray([ij[1] for ij in pairs], jnp.int32))


def mla_attn_fwd(q, k, vt, z, *, tq=256):
    t = q.shape[0]
    nq = t // tq
    it, jt = _causal_tiles(nq, False)

    def body(it_ref, jt_ref, q_ref, k_ref, vt_ref, g_ref, o_ref, y_ref, lse_ref, m_scr, l_scr, acc_scr):
        step = pl.program_id(1)
        i, j = it_ref[step], jt_ref[step]

        @pl.when(j == 0)
        def _():
            m_scr[...] = jnp.full(m_scr.shape, NEG, F32)
            l_scr[...] = jnp.zeros(l_scr.shape, F32)
            acc_scr[...] = jnp.zeros(acc_scr.shape, F32)

        def update(diagonal):
            for hh in range(2):
                lanes, rows = slice(LANES * hh, LANES * (hh + 1)), slice(64 * hh, 64 * (hh + 1))
                st = _dot(k_ref[:, lanes], q_ref[:, lanes], NT)
                if diagonal:
                    key = lax.broadcasted_iota(jnp.int32, (tq, tq), 0)
                    qry = lax.broadcasted_iota(jnp.int32, (tq, tq), 1)
                    st = jnp.where(key <= qry, st, NEG)
                m_prev = m_scr[hh:hh + 1, :]
                m_new = jnp.maximum(m_prev, jnp.max(st, axis=0, keepdims=True))
                alpha = jnp.exp(m_prev - m_new)
                pt = jnp.exp(st - m_new)
                l_scr[hh:hh + 1, :] = alpha * l_scr[hh:hh + 1, :] + jnp.sum(pt, axis=0, keepdims=True)
                m_scr[hh:hh + 1, :] = m_new
                acc_scr[rows, :] = alpha * acc_scr[rows, :] + _dot(vt_ref[rows, :], pt, NN)

        @pl.when(j < i)
        def _():
            update(False)

        @pl.when(j == i)
        def _():
            update(True)
            lse_ref[...] = jnp.zeros(lse_ref.shape, F32)
            for hh in range(2):
                rows = slice(64 * hh, 64 * (hh + 1))
                acc_scr[rows, :] = acc_scr[rows, :] / l_scr[hh:hh + 1, :]
                lse_ref[hh:hh + 1, :] = m_scr[hh:hh + 1, :] + jnp.log(l_scr[hh:hh + 1, :])
            o = acc_scr[...].T
            o_ref[...] = o
            y_ref[...] = (o * _silu_and_grad(g_ref[...])[0]).astype(MXU_DTYPE)

    qo = lambda w, off=0: pl.BlockSpec((tq, w), lambda p, s, it_, jt_: (it_[s], p + off))
    sds = jax.ShapeDtypeStruct
    return pl.pallas_call(
        body,
        name="mla_attn_fwd",
        grid_spec=pltpu.PrefetchScalarGridSpec(
            num_scalar_prefetch=2,
            grid=(4, it.shape[0]),
            in_specs=[qo(256), pl.BlockSpec((tq, 256), lambda p, s, it_, jt_: (jt_[s], p)),
                      pl.BlockSpec((LANES, tq), lambda p, s, it_, jt_: (p, jt_[s])), qo(LANES, 4)],
            out_specs=[qo(LANES), qo(LANES), pl.BlockSpec((None, 8, tq), lambda p, s, it_, jt_: (p, 0, it_[s]))],
            scratch_shapes=[pltpu.VMEM((8, tq), F32), pltpu.VMEM((8, tq), F32), pltpu.VMEM((LANES, tq), F32)],
        ),
        out_shape=[sds((t, 512), F32), sds((t, 512), MXU_DTYPE), sds((4, 8, t), F32)],
        compiler_params=_cparams(("parallel", "arbitrary")),
    )(it, jt, q, k, vt, z)


def mla_attn_bwd(q, k, v, do, lse, dd, *, tq=256):
    t = q.shape[0]
    nq = t // tq
    it, jt = _causal_tiles(nq, True)

    def body(it_ref, jt_ref, q_ref, k_ref, v_ref, do_ref, lse_ref, d_ref, dq_ref, dk_ref, dv_ref, dk_scr, dv_scr):
        step = pl.program_id(1)
        i, j = it_ref[step], jt_ref[step]
        masks = _head_masks()

        @pl.when(step == 0)
        def _():
            dq_ref[...] = jnp.zeros(dq_ref.shape, F32)

        @pl.when(i == j)
        def _():
            dk_scr[...] = jnp.zeros(dk_scr.shape, F32)
            dv_scr[...] = jnp.zeros(dv_scr.shape, F32)

        def update(diagonal):
            qrows = pl.ds(pl.multiple_of(i * tq, tq), tq)
            dov = do_ref[...]
            for hh in range(2):
                lanes = slice(LANES * hh, LANES * (hh + 1))
                qh, kh = q_ref[:, lanes], k_ref[:, lanes]
                st = _dot(kh, qh, NT) - lse_ref[hh:hh + 1, :]
                if diagonal:
                    key = lax.broadcasted_iota(jnp.int32, (tq, tq), 0)
                    qry = lax.broadcasted_iota(jnp.int32, (tq, tq), 1)
                    st = jnp.where(key <= qry, st, NEG)
                pt = jnp.exp(st)
                doh = jnp.where(masks[hh], dov, jnp.zeros_like(dov))
                dst = pt * (_dot(v_ref[...], doh, NT) - d_ref[hh:hh + 1, :])
                dv_scr[...] += _dot(pt, doh, NN)
                dk_scr[:, lanes] += _dot(dst, qh, NN)
                dq_ref[qrows, lanes] += _dot(dst, kh, TN)

        @pl.when(j < i)
        def _():
            update(False)

        @pl.when(j == i)
        def _():
            update(True)

        @pl.when(i == nq - 1)
        def _():
            dk_ref[...] = dk_scr[...]
            dv_ref[...] = dv_scr[...]

    qo = lambda w: pl.BlockSpec((tq, w), lambda p, s, it_, jt_: (it_[s], p))
    kv = lambda w: pl.BlockSpec((tq, w), lambda p, s, it_, jt_: (jt_[s], p))
    stat = pl.BlockSpec((None, 8, tq), lambda p, s, it_, jt_: (p, 0, it_[s]))
    sds = jax.ShapeDtypeStruct
    return pl.pallas_call(
        body,
        name="mla_attn_bwd",
        grid_spec=pltpu.PrefetchScalarGridSpec(
            num_scalar_prefetch=2,
            grid=(4, it.shape[0]),
            in_specs=[qo(256), kv(256), kv(LANES), qo(LANES), stat, stat],
            out_specs=[pl.BlockSpec((t, 256), lambda p, s, it_, jt_: (0, p)), kv(256), kv(LANES)],
            scratch_shapes=[pltpu.VMEM((tq, 256), F32), pltpu.VMEM((tq, LANES), F32)],
        ),
        out_shape=[sds((t, 1024), F32), sds((t, 1024), F32), sds((t, 512), F32)],
        compiler_params=_cparams(("parallel", "arbitrary")),
    )(it, jt, q, k, v, do, lse, dd)


def mla_gate_bwd(dy, o, z, *, tm=256):
    t = dy.shape[0]

    def body(dy_ref, o_ref, g_ref, do_ref, dzg_ref, dd_ref):
        dyv, ov = dy_ref[...], o_ref[...]
        silu, dsilu = _silu_and_grad(g_ref[...])
        do = dyv * silu
        do_ref[...] = do.astype(MXU_DTYPE)
        dzg_ref[...] = (dyv * ov * dsilu).astype(MXU_DTYPE)
        prod = do * ov
        row = lax.broadcasted_iota(jnp.int32, (8, LANES), 0)
        lane = lax.broadcasted_iota(jnp.int32, (8, LANES), 1)
        pick = (((row == 0) & (lane < DIL_HD)) | ((row == 1) & (lane >= DIL_HD))).astype(BF16)
        hi = prod.astype(BF16)
        r1 = prod - hi.astype(F32)
        mid = r1.astype(BF16)
        lo = (r1 - mid.astype(F32)).astype(BF16)
        dot = lambda u: lax.dot_general(pick, u, NT, preferred_element_type=F32)
        dd_ref[...] = dot(hi) + dot(mid) + dot(lo)

    blk = lambda off=0: pl.BlockSpec((tm, LANES), lambda p, i: (i, p + off))
    sds = jax.ShapeDtypeStruct
    return pl.pallas_call(
        body,
        name="mla_gate_bwd",
        grid=(4, t // tm),
        in_specs=[blk(), blk(), blk(4)],
        out_specs=[blk(), blk(), pl.BlockSpec((None, 8, tm), lambda p, i: (p, 0, i))],
        out_shape=[sds((t, 512), MXU_DTYPE), sds((t, 512), MXU_DTYPE), sds((4, 8, t), F32)],
        compiler_params=_cparams(("parallel", "parallel")),
    )(dy, o, z)


def gate_bwd(name, dy, o, z, g_base):
    def fn(ins, outs, _):
        dyv, ov = ins[0][...], ins[1][...]
        silu, dsilu = _silu_and_grad(ins[2][...])
        do = dyv * silu
        outs[0][...] = do
        outs[1][...] = _seg_sum_impl(do * ov)
        outs[2][...] = (dyv * ov * dsilu).astype(MXU_DTYPE)

    return rowwise(name, fn, dy.shape[0], ncol=4, ins=[_row(dy, LANES, 0, True), _row(o, LANES, 0, True), _row(z, LANES, g_base, True)],
                   outs=[(512, LANES, F32, True), (512, LANES, F32, True), (512, LANES, MXU_DTYPE, True)])


def _dil_prep_tile(ops, x, g2, c, s1, s2, scale):
    ms = ops.seg_sum(x * x) * (1.0 / DIL_HD)
    return rope(ops, x * lax.rsqrt(ms + EPS) * g2, c, s1, s2, 32) * scale


DIL_SCALE = DIL_HD ** -0.5


def _dil_prep_ins(z, tabs, gq2, gk2):
    return [_row(z, LANES, 0, True), _row(z, LANES, 12, True), _row(tabs, 128, 3), _row(tabs, 128, 4), _row(tabs, 128, 5),
            _const(gq2), _const(gk2)]


def dil_prep_fwd(z, tabs, gq2, gk2):
    def fn(ins, outs, _):
        c, s1, s2 = ins[2][...], ins[3][...], ins[4][...]
        outs[0][...] = _dil_prep_tile(PLAIN, ins[0][...], ins[5][...], c, s1, s2, DIL_SCALE).astype(MXU_DTYPE)
        outs[1][...] = _dil_prep_tile(PLAIN, ins[1][...], ins[6][...], c, s1, s2, 1.0).astype(MXU_DTYPE)

    return rowwise("dil_prep_fwd", fn, z.shape[0], ncol=12, ins=_dil_prep_ins(z, tabs, gq2, gk2),
                   outs=[(1536, LANES, MXU_DTYPE, True), (1536, LANES, MXU_DTYPE, True)])


def dil_prep_bwd(z, tabs, gq2, gk2, dq, dk):
    def fn(ins, outs, accs):
        c, s1, s2 = ins[2][...], ins[3][...], ins[4][...]
        for idx, scale in ((0, DIL_SCALE), (1, 1.0)):
            _, vjp = jax.vjp(lambda xv, gv: _dil_prep_tile(DIFF, xv, gv, c, s1, s2, scale), ins[idx][...], ins[5 + idx][...])
            dx, dg = vjp(ins[7 + idx][...])
            outs[idx][...] = dx.astype(MXU_DTYPE)
            accs[idx][...] += dg + pltpu.roll(dg, DIL_HD, 1)

    return rowwise("dil_prep_bwd", fn, z.shape[0], ncol=12,
                   ins=_dil_prep_ins(z, tabs, gq2, gk2) + [_row(dq, LANES, 0, True), _row(dk, LANES, 0, True)],
                   outs=[(1536, LANES, MXU_DTYPE, True), (1536, LANES, MXU_DTYPE, True)],
                   accs=[(1, LANES, LANES, False), (1, LANES, LANES, False)])


def _band_masks(n):
    qi = lax.broadcasted_iota(jnp.int32, (DIL_NK, DIL_NK), 0)
    ki = lax.broadcasted_iota(jnp.int32, (DIL_NK, DIL_NK), 1)
    return (ki >= qi), (ki <= qi)


def dil_attn_fwd(name, qv, kv, vv, nb, dil):
    def body(q_ref, kp_ref, kc_ref, vp_ref, vc_ref, o_ref, lse_ref):
        n = pl.program_id(0)
        mprev, mcur = _band_masks(n)
        mprev = mprev & (n > 0)
        for pair in range(4):
            lanes = slice(LANES * pair, LANES * (pair + 1))
            q, kp, kc = q_ref[:, lanes], kp_ref[:, lanes], kc_ref[:, lanes]
            vp, vc = vp_ref[:, lanes], vc_ref[:, lanes]
            o = jnp.zeros((DIL_NK, LANES), F32)
            lse = jnp.zeros((DIL_NK, LANES), F32)
            for mh in _head_masks():
                qh = jnp.where(mh, q, jnp.zeros_like(q))
                sp = jnp.where(mprev, _dot(qh, kp, NT), NEG)
                sc = jnp.where(mcur, _dot(qh, kc, NT), NEG)
                m = jnp.maximum(jnp.max(sp, axis=1, keepdims=True), jnp.max(sc, axis=1, keepdims=True))
                ep, ec = jnp.exp(sp - m), jnp.exp(sc - m)
                den = jnp.sum(ep, axis=1, keepdims=True) + jnp.sum(ec, axis=1, keepdims=True)
                oh = _dot(ep, jnp.where(mh, vp, 0.0), NN) + _dot(ec, jnp.where(mh, vc, 0.0), NN)
                o = o + oh / den
                lse = jnp.where(mh, m + jnp.log(den), lse)
            o_ref[:, lanes] = o
            lse_ref[:, lanes] = lse

    cur = pl.BlockSpec((DIL_NK, 512), lambda n, r: (n, r))
    prev = pl.BlockSpec((DIL_NK, 512), lambda n, r: (jnp.maximum(n - 1, 0), r))
    sds = jax.ShapeDtypeStruct
    return pl.pallas_call(
        body,
        name=name,
        grid=(nb, dil),
        in_specs=[cur, prev, cur, prev, cur],
        out_specs=[cur, cur],
        out_shape=[sds(qv.shape, F32), sds(qv.shape, F32)],
        compiler_params=_cparams(("parallel", "parallel")),
    )(qv, kv, kv, vv, vv)


def dil_attn_bwd(name, qv, kv, vv, dov, lsev, ddv, nb, dil):
    def body(qc_ref, qn_ref, doc_ref, don_ref, lc_ref, ln_ref, dc_ref, dn_ref, kp_ref, kc_ref, vp_ref, vc_ref,
             dq_ref, dk_ref, dv_ref):
        n = pl.program_id(0)
        mprev, mcur = _band_masks(n)
        mnext = mprev & (n < nb - 1)
        mprev = mprev & (n > 0)
        for pair in range(4):
            lanes = slice(LANES * pair, LANES * (pair + 1))
            kp, kc, vp, vc = kp_ref[:, lanes], kc_ref[:, lanes], vp_ref[:, lanes], vc_ref[:, lanes]
            qc, qn, doc, don = qc_ref[:, lanes], qn_ref[:, lanes], doc_ref[:, lanes], don_ref[:, lanes]
            lc, ln, dc, dn = lc_ref[:, lanes], ln_ref[:, lanes], dc_ref[:, lanes], dn_ref[:, lanes]
            dq = jnp.zeros((DIL_NK, LANES), F32)
            dk = jnp.zeros((DIL_NK, LANES), F32)
            dv = jnp.zeros((DIL_NK, LANES), F32)
            for mh in _head_masks():
                zq = jnp.zeros_like(qc)
                qh, doh = jnp.where(mh, qc, zq), jnp.where(mh, doc, 0.0)
                lse_h, d_h = _row_scalar(lc, mh), _row_scalar(dc, mh)
                pp = jnp.exp(jnp.where(mprev, _dot(qh, kp, NT) - lse_h, NEG))
                pc = jnp.exp(jnp.where(mcur, _dot(qh, kc, NT) - lse_h, NEG))
                dsp = pp * (_dot(doh, vp, NT) - d_h)
                dsc = pc * (_dot(doh, vc, NT) - d_h)
                dq = dq + _dot(dsp, jnp.where(mh, kp, jnp.zeros_like(kp)), NN) + _dot(dsc, jnp.where(mh, kc, jnp.zeros_like(kc)), NN)
                dv = dv + _dot(pc, doh, TN)
                dk = dk + _dot(dsc, qh, TN)
                qnh, donh = jnp.where(mh, qn, zq), jnp.where(mh, don, 0.0)
                lse_n, d_n = _row_scalar(ln, mh), _row_scalar(dn, mh)
                pn = jnp.exp(jnp.where(mnext, _dot(qnh, kc, NT) - lse_n, NEG))
                dsn = pn * (_dot(donh, vc, NT) - d_n)
                dv = dv + _dot(pn, donh, TN)
                dk = dk + _dot(dsn, qnh, TN)
            dq_ref[:, lanes] = dq
            dk_ref[:, lanes] = dk
            dv_ref[:, lanes] = dv

    cur = pl.BlockSpec((DIL_NK, 512), lambda n, r: (n, r))
    prev = pl.BlockSpec((DIL_NK, 512), lambda n, r: (jnp.maximum(n - 1, 0), r))
    nxt = pl.BlockSpec((DIL_NK, 512), lambda n, r: (jnp.minimum(n + 1, nb - 1), r))
    sds = jax.ShapeDtypeStruct
    return pl.pallas_call(
        body,
        name=name,
        grid=(nb, dil),
        in_specs=[cur, nxt, cur, nxt, cur, nxt, cur, nxt, prev, cur, prev, cur],
        out_specs=[cur, cur, cur],
        out_shape=[sds(qv.shape, F32)] * 3,
        compiler_params=_cparams(("parallel", "parallel")),
    )(qv, qv, dov, dov, lsev, lsev, ddv, ddv, kv, kv, vv, vv)


def dil_combine(os_, lses, z):
    def fn(ins, outs, _):
        l0, l1, l2 = ins[1][...], ins[3][...], ins[5][...]
        mx = jnp.maximum(jnp.maximum(l0, l1), l2)
        w0, w1, w2 = jnp.exp(l0 - mx), jnp.exp(l1 - mx), jnp.exp(l2 - mx)
        tot = w0 + w1 + w2
        o = (w0 * ins[0][...] + w1 * ins[2][...] + w2 * ins[4][...]) / tot
        outs[0][...] = (o * _silu_and_grad(ins[6][...])[0]).astype(MXU_DTYPE)
        outs[1][...] = o
        outs[2][...] = mx + jnp.log(tot)

    ins = []
    for o, l in zip(os_, lses):
        ins += [_row(o, LANES, 0, True), _row(l, LANES, 0, True)]
    ins.append(_row(z, LANES, 36, True))
    return rowwise("dil_combine", fn, z.shape[0], ncol=4, ins=ins,
                   outs=[(512, LANES, MXU_DTYPE, True), (512, LANES, F32, True), (512, LANES, F32, True)])


def _merge_tile(p0, p1, p2, z0, z1, z2, b0, b1, b2):
    return _sigmoid(z0 + b0) * p0 + _sigmoid(z1 + b1) * p1 + _sigmoid(z2 + b2) * p2


def _merge_ins(ps, z, b):
    w = 256
    return ([_row(p, w, 0, True) for p in ps] + [_row(z, w, 4 * i, True) for i in range(3)]
            + [_const(b, w, 4 * i, True) for i in range(3)])


def merge_fwd(ps, z, b):
    def fn(ins, outs, _):
        outs[0][...] = _merge_tile(*[r[...] for r in ins]).astype(MXU_DTYPE)

    return rowwise("merge_fwd", fn, z.shape[0], ncol=4, ins=_merge_ins(ps, z, b), outs=[(1024, 256, MXU_DTYPE, True)])[0]


def merge_bwd(dm, ps, z, b):
    def fn(ins, outs, accs):
        _, vjp = jax.vjp(_merge_tile, *[r[...] for r in ins[:9]])
        grads = vjp(ins[9][...])
        for i in range(3):
            outs[i][...] = grads[i].astype(MXU_DTYPE)
            outs[3 + i][...] = grads[3 + i].astype(MXU_DTYPE)
            accs[i][...] += grads[6 + i]

    return rowwise("merge_bwd", fn, z.shape[0], ncol=4, ins=_merge_ins(ps, z, b) + [_row(dm, 256, 0, True)],
                   outs=[(1024, 256, MXU_DTYPE, True)] * 6, accs=[(1, 1024, 256, True)] * 3)


def _row_tile(r, c):
    if r * c * 4 <= 2**21:
        return r
    for tr in (512, 256, 128, 64, 32, 16, 8):
        if r % tr == 0 and tr * c * 4 <= 2**21:
            return tr
    raise ValueError((r, c))


def adamw(name, w, g, m, v):
    shape = w.shape
    c = shape[-1]
    r = w.size // c
    tr = _row_tile(r, c)
    c1, c2 = 1.0 - ADAM_B1 ** ADAM_STEP, 1.0 - ADAM_B2 ** ADAM_STEP

    def body(w_ref, g_ref, m_ref, v_ref, d_ref, mo_ref, vo_ref):
        gv = g_ref[...]
        mn = ADAM_B1 * m_ref[...] + (1.0 - ADAM_B1) * gv
        vn = ADAM_B2 * v_ref[...] + (1.0 - ADAM_B2) * (gv * gv)
        d_ref[...] = -ADAM_LR * ((mn / c1) / (jnp.sqrt(vn / c2) + ADAM_EPS) + ADAM_WD * w_ref[...])
        mo_ref[...] = mn
        vo_ref[...] = vn

    spec = pl.BlockSpec((tr, c), lambda i: (i, 0))
    outs = pl.pallas_call(
        body,
        name=name,
        grid=(r // tr,),
        in_specs=[spec] * 4,
        out_specs=[spec] * 3,
        out_shape=[jax.ShapeDtypeStruct((r, c), F32)] * 3,
        compiler_params=_cparams(("parallel",)),
    )(*[a.reshape(r, c) for a in (w, g, m, v)])
    return [o.reshape(shape) for o in outs]


def adamw_pair(name, w, rh, lc, m, v, c_idx):
    _, r, c = w.shape
    tr = _row_tile(r, c)
    c1, c2 = 1.0 - ADAM_B1 ** ADAM_STEP, 1.0 - ADAM_B2 ** ADAM_STEP

    def body(c_ref, w_ref, rh_ref, lc_ref, m_ref, v_ref, g_ref, d_ref, mo_ref, vo_ref):
        gv = jnp.where(pl.program_id(0) == c_ref[0], rh_ref[...], lc_ref[...])
        mn = ADAM_B1 * m_ref[...] + (1.0 - ADAM_B1) * gv
        vn = ADAM_B2 * v_ref[...] + (1.0 - ADAM_B2) * (gv * gv)
        g_ref[...] = gv
        d_ref[...] = -ADAM_LR * ((mn / c1) / (jnp.sqrt(vn / c2) + ADAM_EPS) + ADAM_WD * w_ref[...])
        mo_ref[...] = mn
        vo_ref[...] = vn

    full = pl.BlockSpec((None, tr, c), lambda h, i, cr: (h, i, 0))
    half = pl.BlockSpec((tr, c), lambda h, i, cr: (i, 0))
    return pl.pallas_call(
        body,
        name=name,
        grid_spec=pltpu.PrefetchScalarGridSpec(
            num_scalar_prefetch=1,
            grid=(2, r // tr),
            in_specs=[full, half, half, full, full],
            out_specs=[full] * 4,
        ),
        out_shape=[jax.ShapeDtypeStruct((2, r, c), F32)] * 4,
        compiler_params=_cparams(("parallel", "parallel")),
    )(c_idx, w, rh, lc, m, v)


def sum_pair(name, g, la, c_idx, out_dtype):
    _, _, r, c = g.shape
    tr = _row_tile(r, c)

    def body(c_ref, g_ref, la_ref, o_ref):
        o_ref[...] = (g_ref[...] + la_ref[...]).astype(o_ref.dtype)

    return pl.pallas_call(
        body,
        name=name,
        grid_spec=pltpu.PrefetchScalarGridSpec(
            num_scalar_prefetch=1,
            grid=(4, r // tr),
            in_specs=[pl.BlockSpec((None, None, tr, c), lambda s, i, cr: (s, cr[0], i, 0)),
                      pl.BlockSpec((None, tr, c), lambda s, i, cr: (s, i, 0))],
            out_specs=pl.BlockSpec((None, tr, c), lambda s, i, cr: (s, i, 0)),
        ),
        out_shape=jax.ShapeDtypeStruct((4, r, c), out_dtype),
        compiler_params=_cparams(("parallel", "parallel")),
    )(c_idx, g, la)


def sum_chips(name, g, la, lb, cs_idx):
    _, _, r, c = g.shape
    tr = _row_tile(r, c)

    def body(cs_ref, g_ref, la_ref, l0_ref, l1_ref, l2_ref, o_ref):
        own = g_ref[...] + la_ref[...]
        o_ref[...] = ((own + l0_ref[...].astype(F32)) + l1_ref[...].astype(F32)) + l2_ref[...].astype(F32)

    lspec = lambda k: pl.BlockSpec((None, tr, c), lambda i, cs: (k, i, 0))
    return pl.pallas_call(
        body,
        name=name,
        grid_spec=pltpu.PrefetchScalarGridSpec(
            num_scalar_prefetch=1,
            grid=(r // tr,),
            in_specs=[pl.BlockSpec((None, None, tr, c), lambda i, cs: (cs[1], cs[0], i, 0)),
                      pl.BlockSpec((None, tr, c), lambda i, cs: (cs[1], i, 0)), lspec(0), lspec(1), lspec(2)],
            out_specs=pl.BlockSpec((tr, c), lambda i, cs: (i, 0)),
        ),
        out_shape=jax.ShapeDtypeStruct((r, c), F32),
        compiler_params=_cparams(("parallel",)),
    )(cs_idx, g, la, lb, lb, lb)


def _place():
    x, y, c = lax.axis_index("x"), lax.axis_index("y"), lax.axis_index("c")
    chips = [(1 - x, y), (x, 1 - y), (1 - x, 1 - y)]
    return x, y, c, chips


HBM_SPEC = pl.BlockSpec(memory_space=pltpu.HBM)


def _comm_call(name, body, ins, out_shapes, n_sem, n_local):
    return pl.pallas_call(
        body,
        name=name,
        in_specs=[HBM_SPEC] * len(ins),
        out_specs=[HBM_SPEC] * len(out_shapes),
        out_shape=out_shapes,
        scratch_shapes=[pltpu.SemaphoreType.DMA((n_sem,)), pltpu.SemaphoreType.DMA((n_sem,)),
                        pltpu.SemaphoreType.DMA((max(n_local, 1),))],
    )(*ins)


def allgather_weights(ws):
    n = len(ws)

    def body(*refs):
        ins, outs = refs[:n], refs[n:2 * n]
        send, recv, _ = refs[2 * n:]
        x, y, c, chips = _place()
        s = 2 * x + y
        sib = (x, y, 1 - c)

        def rc(a, k, src, dst, dev):
            return pltpu.make_async_remote_copy(src_ref=src, dst_ref=dst, send_sem=send.at[6 * a + k], recv_sem=recv.at[6 * a + k],
                                                device_id=dev, device_id_type=MESH)

        sends = []
        for j, chip in enumerate(chips):
            for a in range(n):
                sends.append(rc(a, j, ins[a].at[c], outs[a].at[s, c], (*chip, c)))
                sends[-1].start()
        for j, (cx, cy) in enumerate(chips):
            sj = 2 * cx + cy
            for a in range(n):
                landed = outs[a].at[sj, c]
                rc(a, j, landed, landed, sib).wait_recv()
                sends.append(rc(a, 3 + j, landed, landed, sib))
                sends[-1].start()
        for j, (cx, cy) in enumerate(chips):
            sj = 2 * cx + cy
            for a in range(n):
                other = outs[a].at[sj, 1 - c]
                rc(a, 3 + j, other, other, sib).wait_recv()
        for cp in sends:
            cp.wait_send()

    out_shapes = [jax.ShapeDtypeStruct((4,) + w.shape, w.dtype) for w in ws]
    return _comm_call("allgather_weights", body, ws, out_shapes, 6 * n, 0)


def exchange_sibling(gs):
    n = len(gs)

    def body(*refs):
        ins, outs = refs[:n], refs[n:2 * n]
        send, recv, _ = refs[2 * n:]
        x, y, c, _ = _place()
        cps = []
        for a in range(n):
            for s in range(4):
                cps.append(pltpu.make_async_remote_copy(src_ref=ins[a].at[s, 1 - c], dst_ref=outs[a].at[s],
                                                        send_sem=send.at[4 * a + s], recv_sem=recv.at[4 * a + s],
                                                        device_id=(x, y, 1 - c), device_id_type=MESH))
                cps[-1].start()
        for cp in cps:
            cp.wait()

    out_shapes = [jax.ShapeDtypeStruct((4,) + g.shape[2:], g.dtype) for g in gs]
    return _comm_call("exchange_sibling", body, gs, out_shapes, 4 * n, 0)


def exchange_chips(ps):
    n = len(ps)

    def body(*refs):
        ins, outs = refs[:n], refs[n:2 * n]
        send, recv, _ = refs[2 * n:]
        x, y, c, chips = _place()
        cps = []
        for j, (cx, cy) in enumerate(chips):
            for a in range(n):
                cps.append(pltpu.make_async_remote_copy(src_ref=ins[a].at[2 * cx + cy], dst_ref=outs[a].at[j],
                                                        send_sem=send.at[3 * a + j], recv_sem=recv.at[3 * a + j],
                                                        device_id=(cx, cy, c), device_id_type=MESH))
                cps[-1].start()
        for cp in cps:
            cp.wait()

    out_shapes = [jax.ShapeDtypeStruct((3,) + p.shape[1:], p.dtype) for p in ps]
    return _comm_call("exchange_chips", body, ps, out_shapes, 3 * n, 0)


def exchange_final(rs, small):
    n = len(rs)

    def body(*refs):
        ins, small_ref = refs[:n], refs[n]
        outs, small_out = refs[n + 1:2 * n + 1], refs[2 * n + 1]
        send, recv, lsem = refs[2 * n + 2:]
        x, y, c, _ = _place()
        s = 2 * x + y
        cps, local = [], []
        for a in range(n):
            cps.append(pltpu.make_async_remote_copy(src_ref=ins[a], dst_ref=outs[a], send_sem=send.at[a], recv_sem=recv.at[a],
                                                    device_id=(x, y, 1 - c), device_id_type=MESH))
            cps[-1].start()
        local.append(pltpu.make_async_copy(small_ref, small_out.at[s, c], lsem.at[0]))
        local[-1].start()
        k = n
        for fx in (0, 1):
            for fy in (0, 1):
                for fc in (0, 1):
                    if fx or fy or fc:
                        dev = (x ^ fx, y ^ fy, c ^ fc)
                        cps.append(pltpu.make_async_remote_copy(src_ref=small_ref, dst_ref=small_out.at[s, c],
                                                                send_sem=send.at[k], recv_sem=recv.at[k],
                                                                device_id=dev, device_id_type=MESH))
                        cps[-1].start()
                        k += 1
        for cp in cps:
            cp.wait()
        for cp in local:
            cp.wait()

    out_shapes = [jax.ShapeDtypeStruct(r.shape, r.dtype) for r in rs]
    out_shapes.append(jax.ShapeDtypeStruct((4, 2) + small.shape, small.dtype))
    res = _comm_call("exchange_final", body, list(rs) + [small], out_shapes, n + 7, 1)
    return res[:n], res[n]


def _pad_in_cols(w):
    z = lambda n: jnp.zeros(w.shape[:-1] + (n,), w.dtype)
    return jnp.concatenate([w[..., :KR_OFF], z(64), w[..., KR_OFF:KR_OFF + MLA_ROPE], z(32), w[..., KR_OFF + MLA_ROPE:]], axis=-1)


def _unpad_in_cols(g):
    return jnp.concatenate([g[..., :KR_OFF], g[..., KR_OFF + 64:KR_OFF + 96], g[..., KR_OFF + 128:]], axis=-1)


def _pad_heads(w, real):
    k = w.shape[0]
    return jnp.pad(w.reshape(k, MLA_HEADS, real), ((0, 0), (0, 0), (0, LANES - real))).reshape(k, MLA_HEADS * LANES)


def _pad_gain(g, real):
    return jnp.pad(g.reshape(1, real), ((0, 0), (0, LANES - real)))


def layer_weights(l, full):
    w_ukv = full["w_ukv"][l].reshape(128, MLA_HEADS, 2, 64)
    two = lambda g: jnp.concatenate([g, g]).reshape(1, LANES)
    return dict(
        norm_g=full["norm_g"][l].reshape(1, -1), w_in=full["w_in"][l], conv_w=full["conv_w"][l], conv_b=full["conv_b"][l].reshape(1, -1),
        wgx=full["w_gate_x"][l], bgx=full["b_gate_x"][l].reshape(LRU_BLOCKS, 1, LANES),
        wga=full["w_gate_a"][l], bga=full["b_gate_a"][l].reshape(LRU_BLOCKS, 1, LANES),
        lam=full["lru_lambda"][l].reshape(1, -1), w_lru_o=full["w_lru_o"][l],
        cq_norm_g=full["cq_norm_g"][l].reshape(1, -1), ckv_norm_g=full["ckv_norm_g"][l].reshape(1, -1),
        wq=_pad_heads(full["w_uq"][l], MLA_QK), wk=_pad_heads(w_ukv[:, :, 0].reshape(128, 512), 64),
        wv=w_ukv[:, :, 1].reshape(128, 512),
        gq=_pad_gain(full["mla_q_norm_g"][l], MLA_QK), gk=_pad_gain(full["mla_k_norm_g"][l], MLA_QK),
        w_mla_o=full["w_mla_o"][l], gq2=two(full["dil_q_norm_g"][l]), gk2=two(full["dil_k_norm_g"][l]),
        w_dil_o=full["w_dil_o"][l], b_merge=full["b_merge"][l].reshape(1, -1), w_out=full["w_out"][l],
    )


def _strided(a, dil):
    return a.reshape(a.shape[0] // dil, dil * a.shape[1])


def layer_fwd(x, w, tabs):
    t = x.shape[0]
    h = rmsnorm_fwd(x, w["norm_g"])
    z_lru = mm_nn("in_proj_lru", h, w["w_in"], n=2048, n_off=G_LRU)
    z_mla = mm_nn("in_proj_mla", h, w["w_in"], n=1024, n_off=G_MLA)
    z_dil = mm_nn("in_proj_dil", h, w["w_in"], n=5120, n_off=G_DIL)
    z_mrg = mm_nn("in_proj_mrg", h, w["w_in"], n=3072, n_off=G_MRG)
    hs, y_lru = lru_fwd(z_lru, w["conv_w"], w["conv_b"], w["wgx"], w["bgx"], w["wga"], w["bga"], w["lam"])
    qm, km, vm, vtm = mla_prep_fwd(z_mla, tabs, w)
    o_mla, y_mla, lse_mla = mla_attn_fwd(qm, km, vtm, z_mla)
    qd, kd = dil_prep_fwd(z_dil, tabs, w["gq2"], w["gk2"])
    os_, lses = [], []
    for gi, (window, dil) in enumerate(DIL_GROUPS):
        cols = slice(512 * gi, 512 * (gi + 1))
        vg = z_dil[:, 3072 + 512 * gi:3072 + 512 * (gi + 1)]
        o, lse = dil_attn_fwd(f"dil_attn_fwd_{dil}", _strided(qd[:, cols], dil), _strided(kd[:, cols], dil), _strided(vg, dil),
                              t // window, dil)
        os_.append(o.reshape(t, 512))
        lses.append(lse.reshape(t, 512))
    y_dil, o_dil, lse_dil = dil_combine(os_, lses, z_dil)
    ps = [mm_nn("proj_lru", y_lru, w["w_lru_o"]), mm_nn("proj_mla", y_mla, w["w_mla_o"]), mm_nn("proj_dil", y_dil, w["w_dil_o"])]
    merged = merge_fwd(ps, z_mrg, w["b_merge"])
    out = mm_nn("out_proj", merged, w["w_out"], add=x)
    res = dict(x=x, h=h, z_lru=z_lru, z_mla=z_mla, z_dil=z_dil, z_mrg=z_mrg, hs=hs, y_lru=y_lru, qm=qm, km=km, vm=vm, o_mla=o_mla,
               y_mla=y_mla, lse_mla=lse_mla, qd=qd, kd=kd, y_dil=y_dil, o_dil=o_dil, lse_dil=lse_dil, ps=ps, merged=merged)
    return out, res


def layer_bwd(dout, r, w, tabs):
    t = dout.shape[0]
    g = {}
    dmerged = mm_nt("out_proj_dx", dout, w["w_out"])
    g["w_out"] = mm_tn("out_proj_dw", r["merged"], dout)
    dp0, dp1, dp2, dzm0, dzm1, dzm2, db0, db1, db2 = merge_bwd(dmerged, r["ps"], r["z_mrg"], w["b_merge"])
    g["b_merge"] = jnp.concatenate([db0, db1, db2], axis=1).reshape(-1)
    dy_lru = mm_nt("proj_lru_dx", dp0, w["w_lru_o"])
    dy_mla = mm_nt("proj_mla_dx", dp1, w["w_mla_o"])
    dy_dil = mm_nt("proj_dil_dx", dp2, w["w_dil_o"])
    g["w_lru_o"] = mm_tn("proj_lru_dw", r["y_lru"], dp0)
    g["w_mla_o"] = mm_tn("proj_mla_dw", r["y_mla"], dp1)
    g["w_dil_o"] = mm_tn("proj_dil_dw", r["y_dil"], dp2)
    dzx, dzg_lru, dwgx, dbgx, dwga, dbga, dlam, dcw, dcb = lru_bwd(r["z_lru"], r["hs"], dy_lru, w["conv_w"], w["conv_b"], w["wgx"],
                                                                    w["bgx"], w["wga"], w["bga"], w["lam"])
    g.update(w_gate_x=dwgx, b_gate_x=dbgx.reshape(LRU_BLOCKS, LANES), w_gate_a=dwga, b_gate_a=dbga.reshape(LRU_BLOCKS, LANES),
             lru_lambda=dlam.reshape(-1), conv_w=dcw, conv_b=dcb.reshape(-1))
    do_m, dzg_mla, dd_m = mla_gate_bwd(dy_mla, r["o_mla"], r["z_mla"])
    dq_m, dk_m, dv_m = mla_attn_bwd(r["qm"], r["km"], r["vm"], do_m, r["lse_mla"], dd_m)
    dz_mla3, dg_cq, dg_ckv, dwq, dwk, dwv, dgq, dgk = mla_prep_bwd(r["z_mla"], tabs, w, dq_m, dk_m, dv_m)
    g.update(cq_norm_g=dg_cq.reshape(-1), ckv_norm_g=dg_ckv.reshape(-1), mla_q_norm_g=dgq[0, :MLA_QK], mla_k_norm_g=dgk[0, :MLA_QK])
    g["w_uq"] = dwq.reshape(256, MLA_HEADS, LANES)[:, :, :MLA_QK].reshape(256, MLA_HEADS * MLA_QK)
    g["w_ukv"] = jnp.concatenate([dwk.reshape(128, MLA_HEADS, LANES)[:, :, :64], dwv.reshape(128, MLA_HEADS, 64)], axis=2).reshape(128, 1024)
    do_d, dd_d, dzg_dil = gate_bwd("dil_gate_bwd", dy_dil, r["o_dil"], r["z_dil"], 36)
    dqs, dks, dvs = [], [], []
    for gi, (window, dil) in enumerate(DIL_GROUPS):
        cols = slice(512 * gi, 512 * (gi + 1))
        vg = r["z_dil"][:, 3072 + 512 * gi:3072 + 512 * (gi + 1)]
        dq, dk, dv = dil_attn_bwd(f"dil_attn_bwd_{dil}", _strided(r["qd"][:, cols], dil), _strided(r["kd"][:, cols], dil),
                                  _strided(vg, dil), _strided(do_d, dil), _strided(r["lse_dil"], dil), _strided(dd_d, dil),
                                  t // window, dil)
        dqs.append(dq.reshape(t, 512))
        dks.append(dk.reshape(t, 512))
        dvs.append(dv.reshape(t, 512).astype(MXU_DTYPE))
    dzq, dzk, dgq2, dgk2 = dil_prep_bwd(r["z_dil"], tabs, w["gq2"], w["gk2"], jnp.concatenate(dqs, axis=1), jnp.concatenate(dks, axis=1))
    g.update(dil_q_norm_g=dgq2[0, :DIL_HD], dil_k_norm_g=dgk2[0, :DIL_HD])
    dz = jnp.concatenate([dzx, dzg_lru, dz_mla3, dzg_mla, dzq, dzk] + dvs + [dzg_dil, dzm0, dzm1, dzm2], axis=1)
    dh = mm_nt("in_proj_dx", dz, w["w_in"])
    g["w_in"] = _unpad_in_cols(mm_tn("in_proj_dw", r["h"], dz))
    dx, dng = rmsnorm_bwd(r["x"], dh, dout, w["norm_g"])
    g["norm_g"] = dng.reshape(-1)
    return dx, g


def local_step(x, positions, target, full):
    tabs = rope_tables(positions.reshape(-1, 1))
    ws, ress = [], []
    for l in range(2):
        ws.append(layer_weights(l, full))
        x, res = layer_fwd(x, ws[l], tabs)
        ress.append(res)
    dy, loss = loss_head(x, target)
    grads = [None, None]
    for l in (1, 0):
        dy, grads[l] = layer_bwd(dy, ress[l], ws[l], tabs)
    return loss, dy, {k: jnp.stack([grads[0][k], grads[1][k]]) for k in grads[0]}


WEIGHTS = ["norm_g", "w_in", "conv_w", "conv_b", "w_gate_x", "b_gate_x", "w_gate_a", "b_gate_a", "lru_lambda", "w_lru_o", "cq_norm_g",
           "ckv_norm_g", "w_uq", "w_ukv", "mla_q_norm_g", "mla_k_norm_g", "w_mla_o", "dil_q_norm_g", "dil_k_norm_g", "w_dil_o", "b_merge",
           "w_out"]
SHARDED = {"w_in": 2, "conv_w": 2, "w_lru_o": 1, "w_uq": 2, "w_ukv": 2, "w_mla_o": 2, "w_dil_o": 2, "w_out": 1}
REPLICATED = [n for n in WEIGHTS if n not in SHARDED]
SMALL_ROWS = 72


def kernel(x, positions, norm_g, w_in, conv_w, conv_b, w_gate_x, b_gate_x, w_gate_a, b_gate_a, lru_lambda, w_lru_o, cq_norm_g, ckv_norm_g, w_uq, w_ukv, mla_q_norm_g, mla_k_norm_g, w_mla_o, dil_q_norm_g, dil_k_norm_g, w_dil_o, b_merge, w_out, loss_target, m_norm_g, m_w_in, m_conv_w, m_conv_b, m_w_gate_x, m_b_gate_x, m_w_gate_a, m_b_gate_a, m_lru_lambda, m_w_lru_o, m_cq_norm_g, m_ckv_norm_g, m_w_uq, m_w_ukv, m_mla_q_norm_g, m_mla_k_norm_g, m_w_mla_o, m_dil_q_norm_g, m_dil_k_norm_g, m_w_dil_o, m_b_merge, m_w_out, v_norm_g, v_w_in, v_conv_w, v_conv_b, v_w_gate_x, v_b_gate_x, v_w_gate_a, v_b_gate_a, v_lru_lambda, v_w_lru_o, v_cq_norm_g, v_ckv_norm_g, v_w_uq, v_w_ukv, v_mla_q_norm_g, v_mla_k_norm_g, v_w_mla_o, v_dil_q_norm_g, v_dil_k_norm_g, v_w_dil_o, v_b_merge, v_w_out):
    args = locals()
    w = {n: args[n] for n in WEIGHTS}
    m = {n: args["m_" + n] for n in WEIGHTS}
    v = {n: args["v_" + n] for n in WEIGHTS}
    my_c = lax.axis_index("c").astype(jnp.int32)
    my_s = (2 * lax.axis_index("x") + lax.axis_index("y")).astype(jnp.int32)
    c_idx = my_c.reshape(1)
    cs_idx = jnp.stack([my_c, my_s])

    names = list(SHARDED)
    wire = [w[n] if n == "conv_w" else w[n].astype(BF16) for n in names]
    gathered = allgather_weights(wire)
    full = {n: w[n] for n in REPLICATED}
    for n, ga, own in zip(names, gathered, wire):
        full[n] = jnp.concatenate([jnp.where(my_s == s, own, ga[s]) for s in range(4)], axis=SHARDED[n])
    full["w_in"] = _pad_in_cols(full["w_in"])

    loss, grad_x, grads = local_step(x[0], positions[0], loss_target[0], full)
    loss = lax.psum(loss, ("x", "y", "c"))

    flat = jnp.concatenate([grads[n].reshape(-1) for n in REPLICATED])
    small = jnp.pad(flat, (0, 8 * SMALL_ROWS * 1024 - flat.size)).reshape(4, 2, SMALL_ROWS, 1024)
    gs = []
    for n in names:
        parts = jnp.stack(jnp.split(grads[n], 4, axis=SHARDED[n]))
        gs.append(parts.reshape(4, 2, -1, parts.shape[-1]))
    gs.append(small)
    tags = names + ["small"]
    la = exchange_sibling(gs)
    wire_dt = [BF16 if g_.size >= 2**20 else F32 for g_ in gs]
    pcs = [sum_pair(f"sum_pair_{t}", g_, l_, c_idx, dt) for t, g_, l_, dt in zip(tags, gs, la, wire_dt)]
    lb = exchange_chips(pcs)
    rs = [sum_chips(f"sum_chips_{t}", g_, l_, b_, cs_idx) for t, g_, l_, b_ in zip(tags, gs, la, lb)]
    sib_half, small_all = exchange_final(rs[:-1], rs[-1])

    g_local, delta, new_m, new_v = {}, {}, {}, {}
    for n, rh, lc in zip(names, rs, sib_half):
        shp = w[n].shape
        as3 = lambda a: a.reshape((2,) + rh.shape)
        outs = adamw_pair(f"adamw_{n}", as3(w[n]), rh, lc, as3(m[n]), as3(v[n]), c_idx)
        g_local[n], delta[n], new_m[n], new_v[n] = [o.reshape(shp) for o in outs]
    flat = small_all.reshape(-1)
    off = 0
    for n in REPLICATED:
        g_local[n] = flat[off:off + w[n].size].reshape(w[n].shape)
        off += w[n].size
        delta[n], new_m[n], new_v[n] = adamw(f"adamw_{n}", w[n], g_local[n], m[n], v[n])
    return (loss, grad_x[None], *[g_local[n] for n in WEIGHTS], *[delta[n] for n in WEIGHTS], *[new_m[n] for n in WEIGHTS],
            *[new_v[n] for n in WEIGHTS])
```

```python
import functools

import jax
import jax.numpy as jnp
from jax import lax
from jax.experimental import pallas as pl
from jax.experimental.pallas import tpu as pltpu

F32 = jnp.float32
BF16 = jnp.bfloat16
MXU_DTYPE = jnp.bfloat16

D_MODEL = 1024
EPS = 1e-6
ROPE_THETA = 10000.0
LRU_BLOCKS = 8
LRU_C = 8.0
MLA_HEADS = 8
MLA_NOPE = 64
MLA_ROPE = 32
MLA_QK = 96
DIL_GROUPS = ((128, 1), (512, 4), (2048, 16))
DIL_HD = 64
DIL_NK = 128
MLA_HPS = 4
IN_WIDTH = 11168
ADAM_LR, ADAM_B1, ADAM_B2, ADAM_EPS, ADAM_WD, ADAM_STEP = 0.001, 0.9, 0.999, 1e-08, 0.01, 10

LANES = 128
G_LRU, G_MLA, G_DIL, G_MRG = 0, 2048, 3072, 8192
IN_PAD = 11264
KR_OFF = 2432

NN = (((1,), (0,)), ((), ()))
NT = (((1,), (1,)), ((), ()))
TN = (((0,), (0,)), ((), ()))
NEG = -1e30
MESH = pl.DeviceIdType.MESH
VMEM_LIMIT = 48 * 2**20


def _cparams(sem):
    return pltpu.CompilerParams(dimension_semantics=sem, vmem_limit_bytes=VMEM_LIMIT)


def _dot(a, b, dims):
    return lax.dot_general(a.astype(MXU_DTYPE), b.astype(MXU_DTYPE), dims, preferred_element_type=F32)


@jax.custom_vjp
def mm(a, w):
    return _dot(a, w, NN)


def _mm_fwd(a, w):
    return _dot(a, w, NN), (a, w)


def _mm_bwd(res, g):
    a, w = res
    return _dot(g, w, NT), _dot(a, g, TN)


mm.defvjp(_mm_fwd, _mm_bwd)


def _seg64_matrix():
    r = lax.broadcasted_iota(jnp.int32, (LANES, LANES), 0) < DIL_HD
    c = lax.broadcasted_iota(jnp.int32, (LANES, LANES), 1) < DIL_HD
    return (r == c).astype(BF16)


def _seg_sum_impl(x):
    b = _seg64_matrix()
    hi = x.astype(BF16)
    r1 = x - hi.astype(F32)
    mid = r1.astype(BF16)
    lo = (r1 - mid.astype(F32)).astype(BF16)
    dot = lambda u: lax.dot_general(u, b, NN, preferred_element_type=F32)
    return dot(hi) + dot(mid) + dot(lo)


@jax.custom_vjp
def seg_sum(x):
    return _seg_sum_impl(x)


seg_sum.defvjp(lambda x: (_seg_sum_impl(x), None), lambda _, g: (_seg_sum_impl(g),))


def _lroll_impl(x, s):
    return pltpu.roll(x, s % LANES, 1)


@functools.partial(jax.custom_vjp, nondiff_argnums=(1,))
def lroll(x, s):
    return _lroll_impl(x, s)


lroll.defvjp(lambda x, s: (_lroll_impl(x, s), None), lambda s, _, g: (_lroll_impl(g, -s),))


class _Ops:
    def __init__(self, diff):
        self.mm = mm if diff else (lambda a, w: _dot(a, w, NN))
        self.seg_sum = seg_sum if diff else _seg_sum_impl
        self.lroll = lroll if diff else _lroll_impl


PLAIN, DIFF = _Ops(False), _Ops(True)


def rms(x, g, n):
    ms = jnp.sum(x * x, axis=-1, keepdims=True) * (1.0 / n)
    return x * lax.rsqrt(ms + EPS) * g


def rope(ops, x, c, s1, s2, half):
    return x * c + ops.lroll(x, -half) * s1 + ops.lroll(x, half) * s2


def _sigmoid(x):
    return 1.0 / (1.0 + jnp.exp(-x))


def _silu_and_grad(g):
    sg = _sigmoid(g)
    return g * sg, sg * (1.0 + g * (1.0 - sg))


def _softplus(x):
    return jnp.maximum(x, 0.0) + jnp.log(1.0 + jnp.exp(-jnp.abs(x)))


def _expm1(y):
    series = y * (1.0 + y * (0.5 + y * (1.0 / 6.0 + y * (1.0 / 24.0 + y * (1.0 / 120.0)))))
    return jnp.where(y > -0.05, series, jnp.exp(jnp.minimum(y, -0.05)) - 1.0)


def _mm_call(name, a, b, *, mode, m, n, k, a_blk, b_blk, out_dtype, add, tm, tn, tk):
    nk = k // tk
    assert m % tm == 0 and n % tn == 0 and k % tk == 0, (name, m, n, k, tm, tn, tk)
    dims = {"nn": NN, "nt": NT, "tn": TN}[mode]

    def body(*refs):
        if add is None:
            a_ref, b_ref, o_ref, *scr = refs
            add_ref = None
        else:
            a_ref, b_ref, add_ref, o_ref, *scr = refs
        part = _dot(a_ref[...], b_ref[...], dims)

        def finish(acc):
            if add_ref is not None:
                acc = acc + add_ref[...]
            o_ref[...] = acc.astype(o_ref.dtype)

        if nk == 1:
            finish(part)
        else:
            (acc_ref,) = scr
            kk = pl.program_id(2)

            @pl.when(kk == 0)
            def _():
                acc_ref[...] = part

            @pl.when(kk > 0)
            def _():
                acc_ref[...] += part

            @pl.when(kk == nk - 1)
            def _():
                finish(acc_ref[...])

    in_specs = [a_blk, b_blk]
    args = [a, b]
    if add is not None:
        in_specs.append(pl.BlockSpec((tm, tn), lambda j, i, kk: (i, j)))
        args.append(add)
    return pl.pallas_call(
        body,
        name=name,
        grid=(n // tn, m // tm, nk),
        in_specs=in_specs,
        out_specs=pl.BlockSpec((tm, tn), lambda j, i, kk: (i, j)),
        out_shape=jax.ShapeDtypeStruct((m, n), out_dtype),
        scratch_shapes=[] if nk == 1 else [pltpu.VMEM((tm, tn), F32)],
        compiler_params=_cparams(("parallel", "parallel", "arbitrary")),
    )(*args)


def mm_nn(name, a, b, *, n=None, n_off=0, out_dtype=F32, add=None, tm=512, tn=1024, tk=1024):
    m, k = a.shape
    n = b.shape[1] if n is None else n
    tm, tn, tk = min(tm, m), min(tn, n), min(k, tk)
    ob = n_off // tn
    assert n_off % tn == 0
    return _mm_call(name, a, b, mode="nn", m=m, n=n, k=k, out_dtype=out_dtype, add=add, tm=tm, tn=tn, tk=tk,
                    a_blk=pl.BlockSpec((tm, tk), lambda j, i, kk: (i, kk)),
                    b_blk=pl.BlockSpec((tk, tn), lambda j, i, kk: (kk, j + ob)))


def mm_nt(name, a, b, *, out_dtype=F32, tm=512, tn=1024):
    m, k = a.shape
    n = b.shape[0]
    tn, tk = min(tn, n), min(k, 1024)
    return _mm_call(name, a, b, mode="nt", m=m, n=n, k=k, out_dtype=out_dtype, add=None, tm=tm, tn=tn, tk=tk,
                    a_blk=pl.BlockSpec((tm, tk), lambda j, i, kk: (i, kk)),
                    b_blk=pl.BlockSpec((tn, tk), lambda j, i, kk: (j, kk)))


DW_TILES = dict(tm=1024, tn=512, tk=4096)


def rowwise(name, fn, t, *, tm=256, ncol=1, ins=(), outs=(), touts=(), accs=()):
    n_in, n_out, n_acc = len(ins), len(outs) + len(touts), len(accs)

    def zero_when(ref, cond):
        @pl.when(cond)
        def _():
            ref[...] = jnp.zeros(ref.shape, ref.dtype)

    def body(*refs):
        in_refs, out_refs, acc_refs = refs[:n_in], refs[n_in:n_in + n_out], refs[n_in + n_out:]
        j, i = pl.program_id(0), pl.program_id(1)
        for ref, (_, _, _, cd) in zip(acc_refs, accs):
            zero_when(ref, (i == 0) if cd else ((i == 0) & (j == 0)))
        fn(in_refs, out_refs, acc_refs)

    def in_spec(arr, w, base, cd, rd):
        rows = tm if rd else arr.shape[0]
        return pl.BlockSpec((rows, w), lambda j, i: (i if rd else 0, base + (j if cd else 0)))

    in_specs = [in_spec(*e) for e in ins]
    out_specs = [pl.BlockSpec((tm, w), (lambda j, i, cd=cd: (i, j if cd else 0))) for (_, w, _, cd) in outs]
    out_specs += [pl.BlockSpec((w, tm), (lambda j, i, cd=cd: (j if cd else 0, i))) for (_, w, _, cd) in touts]
    out_specs += [pl.BlockSpec((r, w), (lambda j, i, cd=cd: (0, j if cd else 0))) for (r, _, w, cd) in accs]
    out_shape = [jax.ShapeDtypeStruct((t, c), dt) for (c, _, dt, _) in outs]
    out_shape += [jax.ShapeDtypeStruct((r, t), dt) for (r, _, dt, _) in touts]
    out_shape += [jax.ShapeDtypeStruct((r, c), F32) for (r, c, _, _) in accs]
    res = pl.pallas_call(
        body,
        name=name,
        grid=(ncol, t // tm),
        in_specs=in_specs,
        out_specs=out_specs,
        out_shape=out_shape,
        compiler_params=_cparams(("arbitrary", "arbitrary") if accs else ("parallel", "parallel")),
    )(*[e[0] for e in ins])
    return res


def _row(arr, w=None, base=0, cd=False):
    return (arr, arr.shape[1] if w is None else w, base, cd, True)


def _const(arr, w=None, base=0, cd=False):
    return (arr, arr.shape[1] if w is None else w, base, cd, False)


def rope_tables(positions):
    t = positions.shape[0]

    def fn(ins, outs, _):
        pos = ins[0][...].astype(F32)
        lane = lax.broadcasted_iota(jnp.int32, (1, LANES), 1)
        log_theta = jnp.log(jnp.float32(ROPE_THETA))
        jm = lane - MLA_NOPE
        idx = jnp.clip(jnp.where(jm < 16, jm, jm - 16), 0, 15).astype(F32)
        ang = pos * jnp.exp(-(idx * (2.0 / MLA_ROPE)) * log_theta)
        cos, sin = jnp.cos(ang), jnp.sin(ang)
        in_rope = (lane >= MLA_NOPE) & (lane < MLA_QK)
        outs[0][:, 0:128] = jnp.where(lane < MLA_NOPE, 1.0, jnp.where(in_rope, cos, 0.0))
        outs[0][:, 128:256] = jnp.where(in_rope & (jm < 16), -sin, 0.0)
        outs[0][:, 256:384] = jnp.where(in_rope & (jm >= 16), sin, 0.0)
        jd = lane & (DIL_HD - 1)
        idx = (jd & 31).astype(F32)
        ang = pos * jnp.exp(-(idx * (2.0 / DIL_HD)) * log_theta)
        cos, sin = jnp.cos(ang), jnp.sin(ang)
        outs[0][:, 384:512] = cos
        outs[0][:, 512:640] = jnp.where(jd < 32, -sin, 0.0)
        outs[0][:, 640:768] = jnp.where(jd >= 32, sin, 0.0)

    return rowwise("rope_tables", fn, t, ins=[_row(positions)], outs=[(768, 768, F32, False)])[0]


def rmsnorm_fwd(x, g):
    def fn(ins, outs, _):
        h = rms(ins[0][...], ins[1][...], D_MODEL)
        outs[0][...] = h.astype(MXU_DTYPE)
        outs[1][...] = h.T.astype(MXU_DTYPE)

    return rowwise("rmsnorm_fwd", fn, x.shape[0], ins=[_row(x), _const(g)], outs=[(D_MODEL, D_MODEL, MXU_DTYPE, False)],
                   touts=[(D_MODEL, D_MODEL, MXU_DTYPE, False)])


def rmsnorm_bwd(x, dh, dres, g):
    def fn(ins, outs, accs):
        _, vjp = jax.vjp(lambda xv, gv: rms(xv, gv, D_MODEL), ins[0][...], ins[3][...])
        dx, dg = vjp(ins[1][...])
        outs[0][...] = ins[2][...] + dx
        accs[0][...] += dg

    return rowwise("rmsnorm_bwd", fn, x.shape[0], ins=[_row(x), _row(dh), _row(dres), _const(g)],
                   outs=[(D_MODEL, D_MODEL, F32, False)], accs=[(1, D_MODEL, D_MODEL, False)])


def loss_head(y, target):
    def fn(ins, outs, accs):
        err = ins[0][...] - ins[1][...]
        outs[0][...] = err * (1.0 / D_MODEL)
        accs[0][...] += jnp.sum(err * err, axis=0, keepdims=True)
        accs[1][...] = jnp.broadcast_to(jnp.sum(accs[0][...], keepdims=True), (1, LANES))

    dy, _, tot = rowwise("loss_head", fn, y.shape[0], ins=[_row(y), _row(target)], outs=[(D_MODEL, D_MODEL, F32, False)],
                         accs=[(1, D_MODEL, D_MODEL, False), (1, LANES, LANES, False)])
    return dy, tot[0, 0] * (0.5 / D_MODEL)


def _shift_rows(v, d, fill, reverse):
    tb = v.shape[0]
    rows = lax.broadcasted_iota(jnp.int32, v.shape, 0)
    if not reverse:
        return jnp.where(rows >= d, pltpu.roll(v, d, 0), fill)
    return jnp.where(rows < tb - d, pltpu.roll(v, tb - d, 0), fill)


def _scan_tile(a, b, reverse):
    d = 1
    while d < a.shape[0]:
        b = b + a * _shift_rows(b, d, 0.0, reverse)
        a = a * _shift_rows(a, d, 1.0, reverse)
        d *= 2
    return a, b


def _lru_gates(ops, xc, wgx, bgx, wga, bga, lam):
    gx = _sigmoid(ops.mm(xc, wgx) + bgx)
    ga = _sigmoid(ops.mm(xc, wga) + bga)
    log_a = -LRU_C * ga * _softplus(-lam)
    a = jnp.exp(log_a)
    mult = jnp.sqrt(-_expm1(2.0 * log_a))
    return a, mult * (gx * xc)


def _shifted_inputs(x, halo, tb):
    rows = lax.broadcasted_iota(jnp.int32, x.shape, 0)
    pad = jnp.zeros((tb - 8, LANES), F32)
    out = []
    for d in (3, 2, 1):
        head = jnp.concatenate([pltpu.roll(halo, d, 0), pad], axis=0)
        out.append(jnp.where(rows >= d, pltpu.roll(x, d, 0), head))
    return out + [x]


def _lru_specs(nt, tb, reverse):
    hb = tb // 8
    tt = (lambda t: nt - 1 - t) if reverse else (lambda t: t)
    blk = lambda off: pl.BlockSpec((tb, LANES), lambda n, t: (tt(t), n + off))
    halo = lambda off: pl.BlockSpec((8, LANES), lambda n, t: (jnp.maximum(tt(t) * hb - 1, 0), n + off))
    chan = lambda r: pl.BlockSpec((r, LANES), lambda n, t: (0, n))
    wblk = pl.BlockSpec((None, LANES, LANES), lambda n, t: (n, 0, 0))
    bblk = pl.BlockSpec((None, 1, LANES), lambda n, t: (n, 0, 0))
    return blk, halo, chan, wblk, bblk


def lru_fwd(z, conv_w, conv_b, wgx, bgx, wga, bga, lam, *, tb=256):
    t = z.shape[0]
    nt = t // tb
    blk, halo, chan, wblk, bblk = _lru_specs(nt, tb, False)

    def body(x_ref, xh_ref, g_ref, cw_ref, cb_ref, wgx_ref, bgx_ref, wga_ref, bga_ref, lam_ref, h_ref, y_ref, yt_ref, carry_ref):
        ti = pl.program_id(1)

        @pl.when(ti == 0)
        def _():
            carry_ref[...] = jnp.zeros((8, LANES), F32)

        x = x_ref[...]
        hal = jnp.where(ti > 0, xh_ref[...], 0.0)
        xs = _shifted_inputs(x, hal, tb)
        xc = cb_ref[...] + sum(xs[k] * cw_ref[k:k + 1, :] for k in range(4))
        a, b = _lru_gates(PLAIN, xc, wgx_ref[...], bgx_ref[...], wga_ref[...], bga_ref[...], lam_ref[...])
        acum, h0 = _scan_tile(a, b, False)
        h = h0 + acum * carry_ref[7:8, :]
        carry_ref[...] = h[tb - 8:tb, :]
        h_ref[...] = h
        y = h * _silu_and_grad(g_ref[...])[0]
        y_ref[...] = y.astype(MXU_DTYPE)
        yt_ref[...] = y.T.astype(MXU_DTYPE)

    return pl.pallas_call(
        body,
        name="lru_fwd",
        grid=(LRU_BLOCKS, nt),
        in_specs=[blk(0), halo(0), blk(LRU_BLOCKS), chan(4), chan(1), wblk, bblk, wblk, bblk, chan(1)],
        out_specs=[blk(0), blk(0), pl.BlockSpec((LANES, tb), lambda n, t_: (n, t_))],
        out_shape=[jax.ShapeDtypeStruct((t, 1024), F32), jax.ShapeDtypeStruct((t, 1024), MXU_DTYPE),
                   jax.ShapeDtypeStruct((1024, t), MXU_DTYPE)],
        scratch_shapes=[pltpu.VMEM((8, LANES), F32)],
        compiler_params=_cparams(("parallel", "arbitrary")),
    )(z, z, z, conv_w, conv_b, wgx, bgx, wga, bga, lam)


def lru_bwd(z, h, dy, conv_w, conv_b, wgx, bgx, wga, bga, lam, *, tb=256):
    t = z.shape[0]
    nt = t // tb
    blk, halo, chan, wblk, bblk = _lru_specs(nt, tb, True)

    def body(x_ref, xh_ref, g_ref, h_ref, hh_ref, dy_ref, cw_ref, cb_ref, wgx_ref, bgx_ref, wga_ref, bga_ref, lam_ref,
             dzx_ref, dzg_ref, dwgx_ref, dbgx_ref, dwga_ref, dbga_ref, dlam_ref, dcw_ref, dcb_ref,
             gcar_ref, acar_ref, xcar_ref):
        ti = pl.program_id(1)
        has_earlier = ti < nt - 1

        @pl.when(ti == 0)
        def _():
            for ref in (dwgx_ref, dbgx_ref, dwga_ref, dbga_ref, dlam_ref, dcw_ref, dcb_ref, gcar_ref, acar_ref, xcar_ref):
                ref[...] = jnp.zeros(ref.shape, F32)

        rows = lax.broadcasted_iota(jnp.int32, (tb, LANES), 0)
        x = x_ref[...]
        hal = jnp.where(has_earlier, xh_ref[...], 0.0)
        xs = _shifted_inputs(x, hal, tb)
        xc = cb_ref[...] + sum(xs[k] * cw_ref[k:k + 1, :] for k in range(4))
        (a, _), vjp = jax.vjp(functools.partial(_lru_gates, DIFF), xc, wgx_ref[...], bgx_ref[...], wga_ref[...],
                              bga_ref[...], lam_ref[...])
        g, h, dyv = g_ref[...], h_ref[...], dy_ref[...]
        silu, dsilu = _silu_and_grad(g)
        dzg_ref[...] = (dyv * h * dsilu).astype(MXU_DTYPE)
        a_next = jnp.where(rows < tb - 1, pltpu.roll(a, tb - 1, 0), acar_ref[0:1, :])
        acum, g0 = _scan_tile(a_next, dyv * silu, True)
        gt = g0 + acum * gcar_ref[0:1, :]
        h_prev = jnp.where(rows >= 1, pltpu.roll(h, 1, 0), jnp.where(has_earlier, hh_ref[7:8, :], 0.0))
        dxc, dwgx, dbgx, dwga, dbga, dlam = vjp((gt * h_prev, gt))
        later = xcar_ref[...]
        gcar_ref[...] = gt[0:8, :]
        acar_ref[...] = a[0:8, :]
        xcar_ref[...] = dxc[0:8, :]
        dx = dxc * cw_ref[3:4, :]
        pad = jnp.zeros((tb - 8, LANES), F32)
        for d in (1, 2, 3):
            tail = jnp.concatenate([pad, pltpu.roll(later, 8 - d, 0)], axis=0)
            up = jnp.where(rows < tb - d, pltpu.roll(dxc, tb - d, 0), tail)
            dx = dx + up * cw_ref[3 - d:4 - d, :]
        dzx_ref[...] = dx.astype(MXU_DTYPE)
        for k in range(4):
            dcw_ref[k:k + 1, :] += jnp.sum(dxc * xs[k], axis=0, keepdims=True)
        dcb_ref[...] += jnp.sum(dxc, axis=0, keepdims=True)
        dwgx_ref[...] += dwgx
        dbgx_ref[...] += dbgx
        dwga_ref[...] += dwga
        dbga_ref[...] += dbga
        dlam_ref[...] += dlam

    sds = jax.ShapeDtypeStruct
    return pl.pallas_call(
        body,
        name="lru_bwd",
        grid=(LRU_BLOCKS, nt),
        in_specs=[blk(0), halo(0), blk(LRU_BLOCKS), blk(0), halo(0), blk(0), chan(4), chan(1), wblk, bblk, wblk, bblk, chan(1)],
        out_specs=[blk(0), blk(0), wblk, bblk, wblk, bblk, chan(1), chan(4), chan(1)],
        out_shape=[sds((t, 1024), MXU_DTYPE), sds((t, 1024), MXU_DTYPE), sds(wgx.shape, F32), sds(bgx.shape, F32), sds(wga.shape, F32),
                   sds(bga.shape, F32), sds((1, 1024), F32), sds((4, 1024), F32), sds((1, 1024), F32)],
        scratch_shapes=[pltpu.VMEM((8, LANES), F32)] * 3,
        compiler_params=_cparams(("parallel", "arbitrary")),
    )(z, z, z, h, h, dy, conv_w, conv_b, wgx, bgx, wga, bga, lam)


def _mla_prep_tile(ops, cq, ckv, krp, c, s1, s2, g_cq, g_ckv, wq, wk, wv, gq, gk):
    cqn = rms(cq, g_cq, 256)
    ckvn = rms(ckv, g_ckv, 128)
    v = ops.mm(ckvn, wv)
    qs, ks = [], []
    for hd in range(MLA_HEADS):
        q = rms(ops.mm(cqn, wq[hd]), gq, MLA_QK)
        k = rms(ops.mm(ckvn, wk[hd]) + krp, gk, MLA_QK)
        qs.append(rope(ops, q, c, s1, s2, 16) * (MLA_QK ** -0.5))
        ks.append(rope(ops, k, c, s1, s2, 16))
    return tuple(qs), tuple(ks), v


def _mla_prep_args(ins):
    z_cq, z_ckv, z_kr, tc, ts1, ts2, g_cq, g_ckv, wq, wk, wv, gq, gk = ins[:13]
    heads = lambda w: tuple(w[:, LANES * hd:LANES * (hd + 1)] for hd in range(MLA_HEADS))
    return (z_cq[...], z_ckv[...], z_kr[...], tc[...], ts1[...], ts2[...], g_cq[...], g_ckv[...], heads(wq), heads(wk),
            wv[...], gq[...], gk[...])


def _mla_prep_ins(z, tabs, w):
    return [_row(z, 256, 0), _row(z, 128, 2), _row(z, 128, 3), _row(tabs, 128, 0), _row(tabs, 128, 1), _row(tabs, 128, 2),
            _const(w["cq_norm_g"]), _const(w["ckv_norm_g"]), _const(w["wq"]), _const(w["wk"]), _const(w["wv"]),
            _const(w["gq"]), _const(w["gk"])]


def mla_prep_fwd(z, tabs, w):
    def fn(ins, outs, _):
        qs, ks, v = _mla_prep_tile(PLAIN, *_mla_prep_args(ins))
        for hd in range(MLA_HEADS):
            outs[0][:, LANES * hd:LANES * (hd + 1)] = qs[hd].astype(MXU_DTYPE)
            outs[1][:, LANES * hd:LANES * (hd + 1)] = ks[hd].astype(MXU_DTYPE)
        outs[2][...] = v.astype(MXU_DTYPE)
        outs[3][...] = v.T.astype(MXU_DTYPE)

    return rowwise("mla_prep_fwd", fn, z.shape[0], ins=_mla_prep_ins(z, tabs, w),
                   outs=[(1024, 1024, MXU_DTYPE, False), (1024, 1024, MXU_DTYPE, False), (512, 512, MXU_DTYPE, False)],
                   touts=[(512, 512, MXU_DTYPE, False)])


def mla_prep_bwd(z, tabs, w, dq, dk, dv):
    def fn(ins, outs, accs):
        args = _mla_prep_args(ins)
        _, vjp = jax.vjp(functools.partial(_mla_prep_tile, DIFF), *args)
        heads = lambda ref: tuple(ref[:, LANES * hd:LANES * (hd + 1)] for hd in range(MLA_HEADS))
        dcq, dckv, dkr, _, _, _, dg_cq, dg_ckv, dwq, dwk, dwv, dgq, dgk = vjp((heads(ins[13]), heads(ins[14]), ins[15][...]))
        lane = lax.broadcasted_iota(jnp.int32, (1, LANES), 1)
        outs[0][:, 0:256] = dcq.astype(MXU_DTYPE)
        outs[0][:, 256:384] = dckv.astype(MXU_DTYPE)
        outs[0][:, 384:512] = jnp.where((lane >= MLA_NOPE) & (lane < MLA_QK), dkr, 0.0).astype(MXU_DTYPE)
        accs[0][...] += dg_cq
        accs[1][...] += dg_ckv
        for hd in range(MLA_HEADS):
            accs[2][:, LANES * hd:LANES * (hd + 1)] += dwq[hd]
            accs[3][:, LANES * hd:LANES * (hd + 1)] += dwk[hd]
        accs[4][...] += dwv
        accs[5][...] += dgq
        accs[6][...] += dgk

    return rowwise("mla_prep_bwd", fn, z.shape[0], ins=_mla_prep_ins(z, tabs, w) + [_row(dq), _row(dk), _row(dv)],
                   outs=[(512, 512, MXU_DTYPE, False)],
                   accs=[(1, 256, 256, False), (1, 128, 128, False), (256, 1024, 1024, False), (128, 1024, 1024, False),
                         (128, 512, 512, False), (1, 128, 128, False), (1, 128, 128, False)])


def _head_masks():
    lane = lax.broadcasted_iota(jnp.int32, (1, LANES), 1)
    return (lane < DIL_HD, lane >= DIL_HD)


def _row_scalar(tile, mask):
    return jnp.max(jnp.where(mask, tile, -jnp.inf), axis=-1, keepdims=True)


def _causal_tiles(nq, by_key):
    pairs = [(i, j) for i in range(nq) for j in range(i + 1)]
    if by_key:
        pairs.sort(key=lambda ij: (ij[1], ij[0]))
    return (jnp.asarray([ij[0] for ij in pairs], jnp.int32), jnp.asarray([ij[1] for ij in pairs], jnp.int32))


def mla_attn_fwd(q, k, vt, z, *, tq=256):
    t = q.shape[0]
    nq = t // tq
    it, jt = _causal_tiles(nq, False)
    hps, wq, wv = MLA_HPS, LANES * MLA_HPS, 64 * MLA_HPS

    def body(it_ref, jt_ref, q_ref, k_ref, vt_ref, g_ref, o_ref, y_ref, yt_ref, lse_ref, m_scr, l_scr, acc_scr):
        step = pl.program_id(1)
        i, j = it_ref[step], jt_ref[step]

        @pl.when(j == 0)
        def _():
            m_scr[...] = jnp.full(m_scr.shape, NEG, F32)
            l_scr[...] = jnp.zeros(l_scr.shape, F32)
            acc_scr[...] = jnp.zeros(acc_scr.shape, F32)

        def update(diagonal):
            heads = range(hps)
            lanes = [slice(LANES * hh, LANES * (hh + 1)) for hh in heads]
            rows = [slice(64 * hh, 64 * (hh + 1)) for hh in heads]
            sts = [_dot(k_ref[:, lanes[hh]], q_ref[:, lanes[hh]], NT) for hh in heads]
            m_prev = [m_scr[hh:hh + 1, :] for hh in heads]
            l_prev = [l_scr[hh:hh + 1, :] for hh in heads]
            acc_prev = [acc_scr[rows[hh], :] for hh in heads]
            m_new, l_new, acc_new = [], [], []
            for hh in heads:
                st = sts[hh]
                if diagonal:
                    key = lax.broadcasted_iota(jnp.int32, (tq, tq), 0)
                    qry = lax.broadcasted_iota(jnp.int32, (tq, tq), 1)
                    st = jnp.where(key <= qry, st, NEG)
                m_new.append(jnp.maximum(m_prev[hh], jnp.max(st, axis=0, keepdims=True)))
                alpha = jnp.exp(m_prev[hh] - m_new[hh])
                pt = jnp.exp(st - m_new[hh])
                l_new.append(alpha * l_prev[hh] + jnp.sum(pt, axis=0, keepdims=True))
                acc_new.append(alpha * acc_prev[hh] + _dot(vt_ref[rows[hh], :], pt, NN))
            for hh in heads:
                m_scr[hh:hh + 1, :] = m_new[hh]
                l_scr[hh:hh + 1, :] = l_new[hh]
                acc_scr[rows[hh], :] = acc_new[hh]

        @pl.when(j < i)
        def _():
            update(False)

        @pl.when(j == i)
        def _():
            update(True)
            lse_ref[...] = jnp.zeros(lse_ref.shape, F32)
            for hh in range(hps):
                rows = slice(64 * hh, 64 * (hh + 1))
                acc_scr[rows, :] = acc_scr[rows, :] / l_scr[hh:hh + 1, :]
                lse_ref[hh:hh + 1, :] = m_scr[hh:hh + 1, :] + jnp.log(l_scr[hh:hh + 1, :])
            o = acc_scr[...].T
            o_ref[...] = o
            y = o * _silu_and_grad(g_ref[...])[0]
            y_ref[...] = y.astype(MXU_DTYPE)
            yt_ref[...] = y.T.astype(MXU_DTYPE)

    qo = lambda w, off=0: pl.BlockSpec((tq, w), lambda p, s, it_, jt_: (it_[s], p + off))
    sds = jax.ShapeDtypeStruct
    return pl.pallas_call(
        body,
        name="mla_attn_fwd",
        grid_spec=pltpu.PrefetchScalarGridSpec(
            num_scalar_prefetch=2,
            grid=(MLA_HEADS // hps, it.shape[0]),
            in_specs=[qo(wq), pl.BlockSpec((tq, wq), lambda p, s, it_, jt_: (jt_[s], p)),
                      pl.BlockSpec((wv, tq), lambda p, s, it_, jt_: (p, jt_[s])), qo(wv, 512 // wv)],
            out_specs=[qo(wv), qo(wv), pl.BlockSpec((wv, tq), lambda p, s, it_, jt_: (p, it_[s])),
                       pl.BlockSpec((None, 8, tq), lambda p, s, it_, jt_: (p, 0, it_[s]))],
            scratch_shapes=[pltpu.VMEM((8, tq), F32), pltpu.VMEM((8, tq), F32), pltpu.VMEM((wv, tq), F32)],
        ),
        out_shape=[sds((t, 512), F32), sds((t, 512), MXU_DTYPE), sds((512, t), MXU_DTYPE), sds((MLA_HEADS // hps, 8, t), F32)],
        compiler_params=_cparams(("parallel", "arbitrary")),
    )(it, jt, q, k, vt, z)


def mla_attn_bwd(q, k, v, do, lse, dd, *, tq=256):
    t = q.shape[0]
    nq = t // tq
    it, jt = _causal_tiles(nq, True)
    hps, wq, wv = MLA_HPS, LANES * MLA_HPS, 64 * MLA_HPS

    def body(it_ref, jt_ref, q_ref, k_ref, v_ref, do_ref, lse_ref, d_ref, dq_ref, dk_ref, dv_ref, dk_scr, dv_scr):
        step = pl.program_id(1)
        i, j = it_ref[step], jt_ref[step]
        masks = _head_masks()

        @pl.when(step == 0)
        def _():
            dq_ref[...] = jnp.zeros(dq_ref.shape, F32)

        @pl.when(i == j)
        def _():
            dk_scr[...] = jnp.zeros(dk_scr.shape, F32)
            dv_scr[...] = jnp.zeros(dv_scr.shape, F32)

        def update(diagonal):
            qrows = pl.ds(pl.multiple_of(i * tq, tq), tq)
            heads = range(hps)
            lanes = [slice(LANES * hh, LANES * (hh + 1)) for hh in heads]
            pair = [slice(LANES * (hh // 2), LANES * (hh // 2 + 1)) for hh in heads]
            qh = [q_ref[:, lanes[hh]] for hh in heads]
            kh = [k_ref[:, lanes[hh]] for hh in heads]
            doh = []
            for hh in heads:
                dov = do_ref[:, pair[hh]]
                doh.append(jnp.where(masks[hh % 2], dov, jnp.zeros_like(dov)))
            sts = [_dot(kh[hh], qh[hh], NT) for hh in heads]
            dps = [_dot(v_ref[:, pair[hh]], doh[hh], NT) for hh in heads]
            lse = [lse_ref[hh:hh + 1, :] for hh in heads]
            ddv = [d_ref[hh:hh + 1, :] for hh in heads]
            dk_new = [dk_scr[:, lanes[hh]] for hh in heads]
            dq_new = [dq_ref[qrows, lanes[hh]] for hh in heads]
            dv_new = [dv_scr[:, pair[2 * pp]] for pp in range(hps // 2)]
            for hh in heads:
                st = sts[hh] - lse[hh]
                if diagonal:
                    key = lax.broadcasted_iota(jnp.int32, (tq, tq), 0)
                    qry = lax.broadcasted_iota(jnp.int32, (tq, tq), 1)
                    st = jnp.where(key <= qry, st, NEG)
                pt = jnp.exp(st)
                dst = pt * (dps[hh] - ddv[hh])
                dv_new[hh // 2] = dv_new[hh // 2] + _dot(pt, doh[hh], NN)
                dk_new[hh] = dk_new[hh] + _dot(dst, qh[hh], NN)
                dq_new[hh] = dq_new[hh] + _dot(dst, kh[hh], TN)
            for hh in heads:
                dk_scr[:, lanes[hh]] = dk_new[hh]
                dq_ref[qrows, lanes[hh]] = dq_new[hh]
            for pp in range(hps // 2):
                dv_scr[:, pair[2 * pp]] = dv_new[pp]

        @pl.when(j < i)
        def _():
            update(False)

        @pl.when(j == i)
        def _():
            update(True)

        @pl.when(i == nq - 1)
        def _():
            dk_ref[...] = dk_scr[...]
            dv_ref[...] = dv_scr[...]

    qo = lambda w: pl.BlockSpec((tq, w), lambda p, s, it_, jt_: (it_[s], p))
    kv = lambda w: pl.BlockSpec((tq, w), lambda p, s, it_, jt_: (jt_[s], p))
    stat = pl.BlockSpec((None, 8, tq), lambda p, s, it_, jt_: (p, 0, it_[s]))
    sds = jax.ShapeDtypeStruct
    return pl.pallas_call(
        body,
        name="mla_attn_bwd",
        grid_spec=pltpu.PrefetchScalarGridSpec(
            num_scalar_prefetch=2,
            grid=(MLA_HEADS // hps, it.shape[0]),
            in_specs=[qo(wq), kv(wq), kv(wv), qo(wv), stat, stat],
            out_specs=[pl.BlockSpec((t, wq), lambda p, s, it_, jt_: (0, p)), kv(wq), kv(wv)],
            scratch_shapes=[pltpu.VMEM((tq, wq), F32), pltpu.VMEM((tq, wv), F32)],
        ),
        out_shape=[sds((t, 1024), F32), sds((t, 1024), F32), sds((t, 512), F32)],
        compiler_params=_cparams(("parallel", "arbitrary")),
    )(it, jt, q, k, v, do, lse, dd)


def mla_gate_bwd(dy, o, z, *, tm=256):
    t = dy.shape[0]
    wv = 64 * MLA_HPS

    def body(dy_ref, o_ref, g_ref, do_ref, dzg_ref, dd_ref):
        dyv, ov = dy_ref[...], o_ref[...]
        silu, dsilu = _silu_and_grad(g_ref[...])
        do = dyv * silu
        do_ref[...] = do.astype(MXU_DTYPE)
        dzg_ref[...] = (dyv * ov * dsilu).astype(MXU_DTYPE)
        prod = do * ov
        row = lax.broadcasted_iota(jnp.int32, (8, wv), 0)
        lane = lax.broadcasted_iota(jnp.int32, (8, wv), 1)
        pick = ((lane >= row * DIL_HD) & (lane < (row + 1) * DIL_HD)).astype(BF16)
        hi = prod.astype(BF16)
        r1 = prod - hi.astype(F32)
        mid = r1.astype(BF16)
        lo = (r1 - mid.astype(F32)).astype(BF16)
        dot = lambda u: lax.dot_general(pick, u, NT, preferred_element_type=F32)
        dd_ref[...] = dot(hi) + dot(mid) + dot(lo)

    blk = lambda off=0: pl.BlockSpec((tm, wv), lambda p, i: (i, p + off))
    sds = jax.ShapeDtypeStruct
    return pl.pallas_call(
        body,
        name="mla_gate_bwd",
        grid=(MLA_HEADS // MLA_HPS, t // tm),
        in_specs=[blk(), blk(), blk(512 // wv)],
        out_specs=[blk(), blk(), pl.BlockSpec((None, 8, tm), lambda p, i: (p, 0, i))],
        out_shape=[sds((t, 512), MXU_DTYPE), sds((t, 512), MXU_DTYPE), sds((MLA_HEADS // MLA_HPS, 8, t), F32)],
        compiler_params=_cparams(("parallel", "parallel")),
    )(dy, o, z)


def gate_bwd(name, dy, o, z, g_base):
    def fn(ins, outs, _):
        dyv, ov = ins[0][...], ins[1][...]
        silu, dsilu = _silu_and_grad(ins[2][...])
        do = dyv * silu
        outs[0][...] = do
        outs[1][...] = _seg_sum_impl(do * ov)
        outs[2][...] = (dyv * ov * dsilu).astype(MXU_DTYPE)

    return rowwise(name, fn, dy.shape[0], ncol=4, ins=[_row(dy, LANES, 0, True), _row(o, LANES, 0, True), _row(z, LANES, g_base, True)],
                   outs=[(512, LANES, F32, True), (512, LANES, F32, True), (512, LANES, MXU_DTYPE, True)])


def _dil_prep_tile(ops, x, g2, c, s1, s2, scale):
    ms = ops.seg_sum(x * x) * (1.0 / DIL_HD)
    return rope(ops, x * lax.rsqrt(ms + EPS) * g2, c, s1, s2, 32) * scale


DIL_SCALE = DIL_HD ** -0.5


def _dil_prep_ins(z, tabs, gq2, gk2):
    return [_row(z, LANES, 0, True), _row(z, LANES, 12, True), _row(tabs, 128, 3), _row(tabs, 128, 4), _row(tabs, 128, 5),
            _const(gq2), _const(gk2)]


def dil_prep_fwd(z, tabs, gq2, gk2):
    def fn(ins, outs, _):
        c, s1, s2 = ins[2][...], ins[3][...], ins[4][...]
        outs[0][...] = _dil_prep_tile(PLAIN, ins[0][...], ins[5][...], c, s1, s2, DIL_SCALE).astype(MXU_DTYPE)
        outs[1][...] = _dil_prep_tile(PLAIN, ins[1][...], ins[6][...], c, s1, s2, 1.0).astype(MXU_DTYPE)

    return rowwise("dil_prep_fwd", fn, z.shape[0], ncol=12, ins=_dil_prep_ins(z, tabs, gq2, gk2),
                   outs=[(1536, LANES, MXU_DTYPE, True), (1536, LANES, MXU_DTYPE, True)])


def dil_prep_bwd(z, tabs, gq2, gk2, dq, dk):
    def fn(ins, outs, accs):
        c, s1, s2 = ins[2][...], ins[3][...], ins[4][...]
        for idx, scale in ((0, DIL_SCALE), (1, 1.0)):
            _, vjp = jax.vjp(lambda xv, gv: _dil_prep_tile(DIFF, xv, gv, c, s1, s2, scale), ins[idx][...], ins[5 + idx][...])
            dx, dg = vjp(ins[7 + idx][...])
            outs[idx][...] = dx.astype(MXU_DTYPE)
            accs[idx][...] += dg + pltpu.roll(dg, DIL_HD, 1)

    return rowwise("dil_prep_bwd", fn, z.shape[0], ncol=12,
                   ins=_dil_prep_ins(z, tabs, gq2, gk2) + [_row(dq, LANES, 0, True), _row(dk, LANES, 0, True)],
                   outs=[(1536, LANES, MXU_DTYPE, True), (1536, LANES, MXU_DTYPE, True)],
                   accs=[(1, LANES, LANES, False), (1, LANES, LANES, False)])


def _band_masks(n):
    qi = lax.broadcasted_iota(jnp.int32, (DIL_NK, DIL_NK), 0)
    ki = lax.broadcasted_iota(jnp.int32, (DIL_NK, DIL_NK), 1)
    return (ki >= qi), (ki <= qi)


def dil_attn_fwd(name, qv, kv, vv, nb, dil):
    def body(q_ref, kp_ref, kc_ref, vp_ref, vc_ref, o_ref, lse_ref):
        n = pl.program_id(0)
        mprev, mcur = _band_masks(n)
        mprev = mprev & (n > 0)
        for pair in range(4):
            lanes = slice(LANES * pair, LANES * (pair + 1))
            q, kp, kc = q_ref[:, lanes], kp_ref[:, lanes], kc_ref[:, lanes]
            vp, vc = vp_ref[:, lanes], vc_ref[:, lanes]
            o = jnp.zeros((DIL_NK, LANES), F32)
            lse = jnp.zeros((DIL_NK, LANES), F32)
            for mh in _head_masks():
                qh = jnp.where(mh, q, jnp.zeros_like(q))
                sp = jnp.where(mprev, _dot(qh, kp, NT), NEG)
                sc = jnp.where(mcur, _dot(qh, kc, NT), NEG)
                m = jnp.maximum(jnp.max(sp, axis=1, keepdims=True), jnp.max(sc, axis=1, keepdims=True))
                ep, ec = jnp.exp(sp - m), jnp.exp(sc - m)
                den = jnp.sum(ep, axis=1, keepdims=True) + jnp.sum(ec, axis=1, keepdims=True)
                oh = _dot(ep, jnp.where(mh, vp, 0.0), NN) + _dot(ec, jnp.where(mh, vc, 0.0), NN)
                o = o + oh / den
                lse = jnp.where(mh, m + jnp.log(den), lse)
            o_ref[:, lanes] = o
            lse_ref[:, lanes] = lse

    cur = pl.BlockSpec((DIL_NK, 512), lambda n, r: (n, r))
    prev = pl.BlockSpec((DIL_NK, 512), lambda n, r: (jnp.maximum(n - 1, 0), r))
    sds = jax.ShapeDtypeStruct
    return pl.pallas_call(
        body,
        name=name,
        grid=(nb, dil),
        in_specs=[cur, prev, cur, prev, cur],
        out_specs=[cur, cur],
        out_shape=[sds(qv.shape, F32), sds(qv.shape, F32)],
        compiler_params=_cparams(("parallel", "parallel")),
    )(qv, kv, kv, vv, vv)


def dil_attn_bwd(name, qv, kv, vv, dov, lsev, ddv, nb, dil):
    def body(qc_ref, qn_ref, doc_ref, don_ref, lc_ref, ln_ref, dc_ref, dn_ref, kp_ref, kc_ref, vp_ref, vc_ref,
             dq_ref, dk_ref, dv_ref):
        n = pl.program_id(0)
        mprev, mcur = _band_masks(n)
        mnext = mprev & (n < nb - 1)
        mprev = mprev & (n > 0)
        for pair in range(4):
            lanes = slice(LANES * pair, LANES * (pair + 1))
            kp, kc, vp, vc = kp_ref[:, lanes], kc_ref[:, lanes], vp_ref[:, lanes], vc_ref[:, lanes]
            qc, qn, doc, don = qc_ref[:, lanes], qn_ref[:, lanes], doc_ref[:, lanes], don_ref[:, lanes]
            lc, ln, dc, dn = lc_ref[:, lanes], ln_ref[:, lanes], dc_ref[:, lanes], dn_ref[:, lanes]
            dq = jnp.zeros((DIL_NK, LANES), F32)
            dk = jnp.zeros((DIL_NK, LANES), F32)
            dv = jnp.zeros((DIL_NK, LANES), F32)
            for mh in _head_masks():
                zq = jnp.zeros_like(qc)
                qh, doh = jnp.where(mh, qc, zq), jnp.where(mh, doc, 0.0)
                lse_h, d_h = _row_scalar(lc, mh), _row_scalar(dc, mh)
                pp = jnp.exp(jnp.where(mprev, _dot(qh, kp, NT) - lse_h, NEG))
                pc = jnp.exp(jnp.where(mcur, _dot(qh, kc, NT) - lse_h, NEG))
                dsp = pp * (_dot(doh, vp, NT) - d_h)
                dsc = pc * (_dot(doh, vc, NT) - d_h)
                dq = dq + _dot(dsp, jnp.where(mh, kp, jnp.zeros_like(kp)), NN) + _dot(dsc, jnp.where(mh, kc, jnp.zeros_like(kc)), NN)
                dv = dv + _dot(pc, doh, TN)
                dk = dk + _dot(dsc, qh, TN)
                qnh, donh = jnp.where(mh, qn, zq), jnp.where(mh, don, 0.0)
                lse_n, d_n = _row_scalar(ln, mh), _row_scalar(dn, mh)
                pn = jnp.exp(jnp.where(mnext, _dot(qnh, kc, NT) - lse_n, NEG))
                dsn = pn * (_dot(donh, vc, NT) - d_n)
                dv = dv + _dot(pn, donh, TN)
                dk = dk + _dot(dsn, qnh, TN)
            dq_ref[:, lanes] = dq
            dk_ref[:, lanes] = dk
            dv_ref[:, lanes] = dv

    cur = pl.BlockSpec((DIL_NK, 512), lambda n, r: (n, r))
    prev = pl.BlockSpec((DIL_NK, 512), lambda n, r: (jnp.maximum(n - 1, 0), r))
    nxt = pl.BlockSpec((DIL_NK, 512), lambda n, r: (jnp.minimum(n + 1, nb - 1), r))
    sds = jax.ShapeDtypeStruct
    return pl.pallas_call(
        body,
        name=name,
        grid=(nb, dil),
        in_specs=[cur, nxt, cur, nxt, cur, nxt, cur, nxt, prev, cur, prev, cur],
        out_specs=[cur, cur, cur],
        out_shape=[sds(qv.shape, F32)] * 3,
        compiler_params=_cparams(("parallel", "parallel")),
    )(qv, qv, dov, dov, lsev, lsev, ddv, ddv, kv, kv, vv, vv)


def dil_combine(os_, lses, z):
    def fn(ins, outs, _):
        l0, l1, l2 = ins[1][...], ins[3][...], ins[5][...]
        mx = jnp.maximum(jnp.maximum(l0, l1), l2)
        w0, w1, w2 = jnp.exp(l0 - mx), jnp.exp(l1 - mx), jnp.exp(l2 - mx)
        tot = w0 + w1 + w2
        o = (w0 * ins[0][...] + w1 * ins[2][...] + w2 * ins[4][...]) / tot
        y = o * _silu_and_grad(ins[6][...])[0]
        outs[0][...] = y.astype(MXU_DTYPE)
        outs[1][...] = o
        outs[2][...] = mx + jnp.log(tot)
        outs[3][...] = y.T.astype(MXU_DTYPE)

    ins = []
    for o, l in zip(os_, lses):
        ins += [_row(o, LANES, 0, True), _row(l, LANES, 0, True)]
    ins.append(_row(z, LANES, 36, True))
    return rowwise("dil_combine", fn, z.shape[0], ncol=4, ins=ins,
                   outs=[(512, LANES, MXU_DTYPE, True), (512, LANES, F32, True), (512, LANES, F32, True)],
                   touts=[(512, LANES, MXU_DTYPE, True)])


def _merge_tile(p0, p1, p2, z0, z1, z2, b0, b1, b2):
    return _sigmoid(z0 + b0) * p0 + _sigmoid(z1 + b1) * p1 + _sigmoid(z2 + b2) * p2


def _merge_ins(ps, z, b):
    w = 256
    return ([_row(p, w, 0, True) for p in ps] + [_row(z, w, 4 * i, True) for i in range(3)]
            + [_const(b, w, 4 * i, True) for i in range(3)])


def merge_fwd(ps, z, b):
    def fn(ins, outs, _):
        merged = _merge_tile(*[r[...] for r in ins])
        outs[0][...] = merged.astype(MXU_DTYPE)
        outs[1][...] = merged.T.astype(MXU_DTYPE)

    return rowwise("merge_fwd", fn, z.shape[0], ncol=4, ins=_merge_ins(ps, z, b), outs=[(1024, 256, MXU_DTYPE, True)],
                   touts=[(1024, 256, MXU_DTYPE, True)])


def merge_bwd(dm, ps, z, b):
    def fn(ins, outs, accs):
        _, vjp = jax.vjp(_merge_tile, *[r[...] for r in ins[:9]])
        grads = vjp(ins[9][...])
        for i in range(3):
            outs[i][...] = grads[i].astype(MXU_DTYPE)
            outs[3 + i][...] = grads[3 + i].astype(MXU_DTYPE)
            accs[i][...] += grads[6 + i]

    return rowwise("merge_bwd", fn, z.shape[0], ncol=4, ins=_merge_ins(ps, z, b) + [_row(dm, 256, 0, True)],
                   outs=[(1024, 256, MXU_DTYPE, True)] * 6, accs=[(1, 1024, 256, True)] * 3)


def _row_tile(r, c):
    if r * c * 4 <= 2**21:
        return r
    for tr in (512, 256, 128, 64, 32, 16, 8):
        if r % tr == 0 and tr * c * 4 <= 2**21:
            return tr
    raise ValueError((r, c))


def adamw(name, w, g, m, v):
    shape = w.shape
    c = shape[-1]
    r = w.size // c
    tr = _row_tile(r, c)
    c1, c2 = 1.0 - ADAM_B1 ** ADAM_STEP, 1.0 - ADAM_B2 ** ADAM_STEP

    def body(w_ref, g_ref, m_ref, v_ref, d_ref, mo_ref, vo_ref):
        gv = g_ref[...]
        mn = ADAM_B1 * m_ref[...] + (1.0 - ADAM_B1) * gv
        vn = ADAM_B2 * v_ref[...] + (1.0 - ADAM_B2) * (gv * gv)
        d_ref[...] = -ADAM_LR * ((mn / c1) / (jnp.sqrt(vn / c2) + ADAM_EPS) + ADAM_WD * w_ref[...])
        mo_ref[...] = mn
        vo_ref[...] = vn

    spec = pl.BlockSpec((tr, c), lambda i: (i, 0))
    outs = pl.pallas_call(
        body,
        name=name,
        grid=(r // tr,),
        in_specs=[spec] * 4,
        out_specs=[spec] * 3,
        out_shape=[jax.ShapeDtypeStruct((r, c), F32)] * 3,
        compiler_params=_cparams(("parallel",)),
    )(*[a.reshape(r, c) for a in (w, g, m, v)])
    return [o.reshape(shape) for o in outs]


def adamw_pair(name, w, rh, lc, m, v, c_idx):
    _, r, c = w.shape
    tr = _row_tile(r, c)
    c1, c2 = 1.0 - ADAM_B1 ** ADAM_STEP, 1.0 - ADAM_B2 ** ADAM_STEP

    def body(c_ref, w_ref, rh_ref, lc_ref, m_ref, v_ref, g_ref, d_ref, mo_ref, vo_ref):
        gv = jnp.where(pl.program_id(0) == c_ref[0], rh_ref[...], lc_ref[...])
        mn = ADAM_B1 * m_ref[...] + (1.0 - ADAM_B1) * gv
        vn = ADAM_B2 * v_ref[...] + (1.0 - ADAM_B2) * (gv * gv)
        g_ref[...] = gv
        d_ref[...] = -ADAM_LR * ((mn / c1) / (jnp.sqrt(vn / c2) + ADAM_EPS) + ADAM_WD * w_ref[...])
        mo_ref[...] = mn
        vo_ref[...] = vn

    full = pl.BlockSpec((None, tr, c), lambda h, i, cr: (h, i, 0))
    half = pl.BlockSpec((tr, c), lambda h, i, cr: (i, 0))
    return pl.pallas_call(
        body,
        name=name,
        grid_spec=pltpu.PrefetchScalarGridSpec(
            num_scalar_prefetch=1,
            grid=(2, r // tr),
            in_specs=[full, half, half, full, full],
            out_specs=[full] * 4,
        ),
        out_shape=[jax.ShapeDtypeStruct((2, r, c), F32)] * 4,
        compiler_params=_cparams(("parallel", "parallel")),
    )(c_idx, w, rh, lc, m, v)


def sum_pair(name, g, la, c_idx, out_dtype):
    _, _, r, c = g.shape
    tr = _row_tile(r, c)

    def body(c_ref, g_ref, la_ref, o_ref):
        o_ref[...] = (g_ref[...] + la_ref[...]).astype(o_ref.dtype)

    return pl.pallas_call(
        body,
        name=name,
        grid_spec=pltpu.PrefetchScalarGridSpec(
            num_scalar_prefetch=1,
            grid=(4, r // tr),
            in_specs=[pl.BlockSpec((None, None, tr, c), lambda s, i, cr: (s, cr[0], i, 0)),
                      pl.BlockSpec((None, tr, c), lambda s, i, cr: (s, i, 0))],
            out_specs=pl.BlockSpec((None, tr, c), lambda s, i, cr: (s, i, 0)),
        ),
        out_shape=jax.ShapeDtypeStruct((4, r, c), out_dtype),
        compiler_params=_cparams(("parallel", "parallel")),
    )(c_idx, g, la)


def sum_chips(name, g, la, lb, cs_idx):
    _, _, r, c = g.shape
    tr = _row_tile(r, c)

    def body(cs_ref, g_ref, la_ref, l0_ref, l1_ref, l2_ref, o_ref):
        own = g_ref[...] + la_ref[...]
        o_ref[...] = ((own + l0_ref[...].astype(F32)) + l1_ref[...].astype(F32)) + l2_ref[...].astype(F32)

    lspec = lambda k: pl.BlockSpec((None, tr, c), lambda i, cs: (k, i, 0))
    return pl.pallas_call(
        body,
        name=name,
        grid_spec=pltpu.PrefetchScalarGridSpec(
            num_scalar_prefetch=1,
            grid=(r // tr,),
            in_specs=[pl.BlockSpec((None, None, tr, c), lambda i, cs: (cs[1], cs[0], i, 0)),
                      pl.BlockSpec((None, tr, c), lambda i, cs: (cs[1], i, 0)), lspec(0), lspec(1), lspec(2)],
            out_specs=pl.BlockSpec((tr, c), lambda i, cs: (i, 0)),
        ),
        out_shape=jax.ShapeDtypeStruct((r, c), F32),
        compiler_params=_cparams(("parallel",)),
    )(cs_idx, g, la, lb, lb, lb)


def _place():
    x, y, c = lax.axis_index("x"), lax.axis_index("y"), lax.axis_index("c")
    chips = [(1 - x, y), (x, 1 - y), (1 - x, 1 - y)]
    return x, y, c, chips


HBM_SPEC = pl.BlockSpec(memory_space=pltpu.HBM)


def _comm_call(name, body, ins, out_shapes, n_sem, n_local):
    return pl.pallas_call(
        body,
        name=name,
        in_specs=[HBM_SPEC] * len(ins),
        out_specs=[HBM_SPEC] * len(out_shapes),
        out_shape=out_shapes,
        scratch_shapes=[pltpu.SemaphoreType.DMA((n_sem,)), pltpu.SemaphoreType.DMA((n_sem,)),
                        pltpu.SemaphoreType.DMA((max(n_local, 1),))],
    )(*ins)


def allgather_weights(ws):
    n = len(ws)

    def body(*refs):
        ins, outs = refs[:n], refs[n:2 * n]
        send, recv, _ = refs[2 * n:]
        x, y, c, chips = _place()
        s = 2 * x + y
        sib = (x, y, 1 - c)

        def rc(a, k, src, dst, dev):
            return pltpu.make_async_remote_copy(src_ref=src, dst_ref=dst, send_sem=send.at[6 * a + k], recv_sem=recv.at[6 * a + k],
                                                device_id=dev, device_id_type=MESH)

        sends = []
        for j, chip in enumerate(chips):
            for a in range(n):
                sends.append(rc(a, j, ins[a].at[c], outs[a].at[s, c], (*chip, c)))
                sends[-1].start()
        for j, (cx, cy) in enumerate(chips):
            sj = 2 * cx + cy
            for a in range(n):
                landed = outs[a].at[sj, c]
                rc(a, j, landed, landed, sib).wait_recv()
                sends.append(rc(a, 3 + j, landed, landed, sib))
                sends[-1].start()
        for j, (cx, cy) in enumerate(chips):
            sj = 2 * cx + cy
            for a in range(n):
                other = outs[a].at[sj, 1 - c]
                rc(a, 3 + j, other, other, sib).wait_recv()
        for cp in sends:
            cp.wait_send()

    out_shapes = [jax.ShapeDtypeStruct((4,) + w.shape, w.dtype) for w in ws]
    return _comm_call("allgather_weights", body, ws, out_shapes, 6 * n, 0)


def exchange_sibling(gs):
    n = len(gs)

    def body(*refs):
        ins, outs = refs[:n], refs[n:2 * n]
        send, recv, _ = refs[2 * n:]
        x, y, c, _ = _place()
        cps = []
        for a in range(n):
            for s in range(4):
                cps.append(pltpu.make_async_remote_copy(src_ref=ins[a].at[s, 1 - c], dst_ref=outs[a].at[s],
                                                        send_sem=send.at[4 * a + s], recv_sem=recv.at[4 * a + s],
                                                        device_id=(x, y, 1 - c), device_id_type=MESH))
                cps[-1].start()
        for cp in cps:
            cp.wait()

    out_shapes = [jax.ShapeDtypeStruct((4,) + g.shape[2:], g.dtype) for g in gs]
    return _comm_call("exchange_sibling", body, gs, out_shapes, 4 * n, 0)


def exchange_chips(ps):
    n = len(ps)

    def body(*refs):
        ins, outs = refs[:n], refs[n:2 * n]
        send, recv, _ = refs[2 * n:]
        x, y, c, chips = _place()
        cps = []
        for j, (cx, cy) in enumerate(chips):
            for a in range(n):
                cps.append(pltpu.make_async_remote_copy(src_ref=ins[a].at[2 * cx + cy], dst_ref=outs[a].at[j],
                                                        send_sem=send.at[3 * a + j], recv_sem=recv.at[3 * a + j],
                                                        device_id=(cx, cy, c), device_id_type=MESH))
                cps[-1].start()
        for cp in cps:
            cp.wait()

    out_shapes = [jax.ShapeDtypeStruct((3,) + p.shape[1:], p.dtype) for p in ps]
    return _comm_call("exchange_chips", body, ps, out_shapes, 3 * n, 0)


def exchange_final(rs, small):
    n = len(rs)

    def body(*refs):
        ins, small_ref = refs[:n], refs[n]
        outs, small_out = refs[n + 1:2 * n + 1], refs[2 * n + 1]
        send, recv, lsem = refs[2 * n + 2:]
        x, y, c, _ = _place()
        s = 2 * x + y
        cps, local = [], []
        for a in range(n):
            cps.append(pltpu.make_async_remote_copy(src_ref=ins[a], dst_ref=outs[a], send_sem=send.at[a], recv_sem=recv.at[a],
                                                    device_id=(x, y, 1 - c), device_id_type=MESH))
            cps[-1].start()
        local.append(pltpu.make_async_copy(small_ref, small_out.at[s, c], lsem.at[0]))
        local[-1].start()
        k = n
        for fx in (0, 1):
            for fy in (0, 1):
                for fc in (0, 1):
                    if fx or fy or fc:
                        dev = (x ^ fx, y ^ fy, c ^ fc)
                        cps.append(pltpu.make_async_remote_copy(src_ref=small_ref, dst_ref=small_out.at[s, c],
                                                                send_sem=send.at[k], recv_sem=recv.at[k],
                                                                device_id=dev, device_id_type=MESH))
                        cps[-1].start()
                        k += 1
        for cp in cps:
            cp.wait()
        for cp in local:
            cp.wait()

    out_shapes = [jax.ShapeDtypeStruct(r.shape, r.dtype) for r in rs]
    out_shapes.append(jax.ShapeDtypeStruct((4, 2) + small.shape, small.dtype))
    res = _comm_call("exchange_final", body, list(rs) + [small], out_shapes, n + 7, 1)
    return res[:n], res[n]


def _pad_in_cols(w):
    z = lambda n: jnp.zeros(w.shape[:-1] + (n,), w.dtype)
    return jnp.concatenate([w[..., :KR_OFF], z(64), w[..., KR_OFF:KR_OFF + MLA_ROPE], z(32), w[..., KR_OFF + MLA_ROPE:]], axis=-1)


def _unpad_in_cols(g):
    return jnp.concatenate([g[..., :KR_OFF], g[..., KR_OFF + 64:KR_OFF + 96], g[..., KR_OFF + 128:]], axis=-1)


def _pad_heads(w, real):
    k = w.shape[0]
    return jnp.pad(w.reshape(k, MLA_HEADS, real), ((0, 0), (0, 0), (0, LANES - real))).reshape(k, MLA_HEADS * LANES)


def _pad_gain(g, real):
    return jnp.pad(g.reshape(1, real), ((0, 0), (0, LANES - real)))


def layer_weights(l, full):
    w_ukv = full["w_ukv"][l].reshape(128, MLA_HEADS, 2, 64)
    two = lambda g: jnp.concatenate([g, g]).reshape(1, LANES)
    return dict(
        norm_g=full["norm_g"][l].reshape(1, -1), w_in=full["w_in"][l], conv_w=full["conv_w"][l], conv_b=full["conv_b"][l].reshape(1, -1),
        wgx=full["w_gate_x"][l], bgx=full["b_gate_x"][l].reshape(LRU_BLOCKS, 1, LANES),
        wga=full["w_gate_a"][l], bga=full["b_gate_a"][l].reshape(LRU_BLOCKS, 1, LANES),
        lam=full["lru_lambda"][l].reshape(1, -1), w_lru_o=full["w_lru_o"][l],
        cq_norm_g=full["cq_norm_g"][l].reshape(1, -1), ckv_norm_g=full["ckv_norm_g"][l].reshape(1, -1),
        wq=_pad_heads(full["w_uq"][l], MLA_QK), wk=_pad_heads(w_ukv[:, :, 0].reshape(128, 512), 64),
        wv=w_ukv[:, :, 1].reshape(128, 512),
        gq=_pad_gain(full["mla_q_norm_g"][l], MLA_QK), gk=_pad_gain(full["mla_k_norm_g"][l], MLA_QK),
        w_mla_o=full["w_mla_o"][l], gq2=two(full["dil_q_norm_g"][l]), gk2=two(full["dil_k_norm_g"][l]),
        w_dil_o=full["w_dil_o"][l], b_merge=full["b_merge"][l].reshape(1, -1), w_out=full["w_out"][l],
    )


def _strided(a, dil):
    return a.reshape(a.shape[0] // dil, dil * a.shape[1])


def layer_fwd(x, w, tabs):
    t = x.shape[0]
    h, ht = rmsnorm_fwd(x, w["norm_g"])
    z_lru = mm_nn("in_proj_lru", h, w["w_in"], n=2048, n_off=G_LRU)
    z_mla = mm_nn("in_proj_mla", h, w["w_in"], n=1024, n_off=G_MLA)
    z_dil = mm_nn("in_proj_dil", h, w["w_in"], n=5120, n_off=G_DIL)
    z_mrg = mm_nn("in_proj_mrg", h, w["w_in"], n=3072, n_off=G_MRG)
    hs, y_lru, yt_lru = lru_fwd(z_lru, w["conv_w"], w["conv_b"], w["wgx"], w["bgx"], w["wga"], w["bga"], w["lam"])
    qm, km, vm, vtm = mla_prep_fwd(z_mla, tabs, w)
    o_mla, y_mla, yt_mla, lse_mla = mla_attn_fwd(qm, km, vtm, z_mla)
    qd, kd = dil_prep_fwd(z_dil, tabs, w["gq2"], w["gk2"])
    os_, lses = [], []
    for gi, (window, dil) in enumerate(DIL_GROUPS):
        cols = slice(512 * gi, 512 * (gi + 1))
        vg = z_dil[:, 3072 + 512 * gi:3072 + 512 * (gi + 1)]
        o, lse = dil_attn_fwd(f"dil_attn_fwd_{dil}", _strided(qd[:, cols], dil), _strided(kd[:, cols], dil), _strided(vg, dil),
                              t // window, dil)
        os_.append(o.reshape(t, 512))
        lses.append(lse.reshape(t, 512))
    y_dil, o_dil, lse_dil, yt_dil = dil_combine(os_, lses, z_dil)
    ps = [mm_nn("proj_lru", y_lru, w["w_lru_o"]), mm_nn("proj_mla", y_mla, w["w_mla_o"]), mm_nn("proj_dil", y_dil, w["w_dil_o"])]
    merged, merged_t = merge_fwd(ps, z_mrg, w["b_merge"])
    out = mm_nn("out_proj", merged, w["w_out"], add=x)
    res = dict(x=x, ht=ht, z_lru=z_lru, z_mla=z_mla, z_dil=z_dil, z_mrg=z_mrg, hs=hs, yt_lru=yt_lru, qm=qm, km=km, vm=vm, o_mla=o_mla,
               yt_mla=yt_mla, lse_mla=lse_mla, qd=qd, kd=kd, yt_dil=yt_dil, o_dil=o_dil, lse_dil=lse_dil, ps=ps, merged_t=merged_t)
    return out, res


def layer_bwd(dout, r, w, tabs):
    t = dout.shape[0]
    g = {}
    dmerged = mm_nt("out_proj_dx", dout, w["w_out"])
    g["w_out"] = mm_nn("out_proj_dw", r["merged_t"], dout, tm=1024, tn=512, tk=2048)
    dp0, dp1, dp2, dzm0, dzm1, dzm2, db0, db1, db2 = merge_bwd(dmerged, r["ps"], r["z_mrg"], w["b_merge"])
    g["b_merge"] = jnp.concatenate([db0, db1, db2], axis=1).reshape(-1)
    dy_lru = mm_nt("proj_lru_dx", dp0, w["w_lru_o"])
    dy_mla = mm_nt("proj_mla_dx", dp1, w["w_mla_o"])
    dy_dil = mm_nt("proj_dil_dx", dp2, w["w_dil_o"])
    g["w_lru_o"] = mm_nn("proj_lru_dw", r["yt_lru"], dp0, **DW_TILES)
    g["w_mla_o"] = mm_nn("proj_mla_dw", r["yt_mla"], dp1, **DW_TILES)
    g["w_dil_o"] = mm_nn("proj_dil_dw", r["yt_dil"], dp2, **DW_TILES)
    dzx, dzg_lru, dwgx, dbgx, dwga, dbga, dlam, dcw, dcb = lru_bwd(r["z_lru"], r["hs"], dy_lru, w["conv_w"], w["conv_b"], w["wgx"],
                                                                    w["bgx"], w["wga"], w["bga"], w["lam"])
    g.update(w_gate_x=dwgx, b_gate_x=dbgx.reshape(LRU_BLOCKS, LANES), w_gate_a=dwga, b_gate_a=dbga.reshape(LRU_BLOCKS, LANES),
             lru_lambda=dlam.reshape(-1), conv_w=dcw, conv_b=dcb.reshape(-1))
    do_m, dzg_mla, dd_m = mla_gate_bwd(dy_mla, r["o_mla"], r["z_mla"])
    dq_m, dk_m, dv_m = mla_attn_bwd(r["qm"], r["km"], r["vm"], do_m, r["lse_mla"], dd_m)
    dz_mla3, dg_cq, dg_ckv, dwq, dwk, dwv, dgq, dgk = mla_prep_bwd(r["z_mla"], tabs, w, dq_m, dk_m, dv_m)
    g.update(cq_norm_g=dg_cq.reshape(-1), ckv_norm_g=dg_ckv.reshape(-1), mla_q_norm_g=dgq[0, :MLA_QK], mla_k_norm_g=dgk[0, :MLA_QK])
    g["w_uq"] = dwq.reshape(256, MLA_HEADS, LANES)[:, :, :MLA_QK].reshape(256, MLA_HEADS * MLA_QK)
    g["w_ukv"] = jnp.concatenate([dwk.reshape(128, MLA_HEADS, LANES)[:, :, :64], dwv.reshape(128, MLA_HEADS, 64)], axis=2).reshape(128, 1024)
    do_d, dd_d, dzg_dil = gate_bwd("dil_gate_bwd", dy_dil, r["o_dil"], r["z_dil"], 36)
    dqs, dks, dvs = [], [], []
    for gi, (window, dil) in enumerate(DIL_GROUPS):
        cols = slice(512 * gi, 512 * (gi + 1))
        vg = r["z_dil"][:, 3072 + 512 * gi:3072 + 512 * (gi + 1)]
        dq, dk, dv = dil_attn_bwd(f"dil_attn_bwd_{dil}", _strided(r["qd"][:, cols], dil), _strided(r["kd"][:, cols], dil),
                                  _strided(vg, dil), _strided(do_d, dil), _strided(r["lse_dil"], dil), _strided(dd_d, dil),
                                  t // window, dil)
        dqs.append(dq.reshape(t, 512))
        dks.append(dk.reshape(t, 512))
        dvs.append(dv.reshape(t, 512).astype(MXU_DTYPE))
    dzq, dzk, dgq2, dgk2 = dil_prep_bwd(r["z_dil"], tabs, w["gq2"], w["gk2"], jnp.concatenate(dqs, axis=1), jnp.concatenate(dks, axis=1))
    g.update(dil_q_norm_g=dgq2[0, :DIL_HD], dil_k_norm_g=dgk2[0, :DIL_HD])
    dz = jnp.concatenate([dzx, dzg_lru, dz_mla3, dzg_mla, dzq, dzk] + dvs + [dzg_dil, dzm0, dzm1, dzm2], axis=1)
    dh = mm_nt("in_proj_dx", dz, w["w_in"])
    g["w_in"] = _unpad_in_cols(mm_nn("in_proj_dw", r["ht"], dz, **DW_TILES))
    dx, dng = rmsnorm_bwd(r["x"], dh, dout, w["norm_g"])
    g["norm_g"] = dng.reshape(-1)
    return dx, g


def local_step(x, positions, target, full):
    tabs = rope_tables(positions.reshape(-1, 1))
    ws, ress = [], []
    for l in range(2):
        ws.append(layer_weights(l, full))
        x, res = layer_fwd(x, ws[l], tabs)
        ress.append(res)
    dy, loss = loss_head(x, target)
    grads = [None, None]
    for l in (1, 0):
        dy, grads[l] = layer_bwd(dy, ress[l], ws[l], tabs)
    return loss, dy, {k: jnp.stack([grads[0][k], grads[1][k]]) for k in grads[0]}


WEIGHTS = ["norm_g", "w_in", "conv_w", "conv_b", "w_gate_x", "b_gate_x", "w_gate_a", "b_gate_a", "lru_lambda", "w_lru_o", "cq_norm_g",
           "ckv_norm_g", "w_uq", "w_ukv", "mla_q_norm_g", "mla_k_norm_g", "w_mla_o", "dil_q_norm_g", "dil_k_norm_g", "w_dil_o", "b_merge",
           "w_out"]
SHARDED = {"w_in": 2, "conv_w": 2, "w_lru_o": 1, "w_uq": 2, "w_ukv": 2, "w_mla_o": 2, "w_dil_o": 2, "w_out": 1}
REPLICATED = [n for n in WEIGHTS if n not in SHARDED]
SMALL_ROWS = 72


def kernel(x, positions, norm_g, w_in, conv_w, conv_b, w_gate_x, b_gate_x, w_gate_a, b_gate_a, lru_lambda, w_lru_o, cq_norm_g, ckv_norm_g, w_uq, w_ukv, mla_q_norm_g, mla_k_norm_g, w_mla_o, dil_q_norm_g, dil_k_norm_g, w_dil_o, b_merge, w_out, loss_target, m_norm_g, m_w_in, m_conv_w, m_conv_b, m_w_gate_x, m_b_gate_x, m_w_gate_a, m_b_gate_a, m_lru_lambda, m_w_lru_o, m_cq_norm_g, m_ckv_norm_g, m_w_uq, m_w_ukv, m_mla_q_norm_g, m_mla_k_norm_g, m_w_mla_o, m_dil_q_norm_g, m_dil_k_norm_g, m_w_dil_o, m_b_merge, m_w_out, v_norm_g, v_w_in, v_conv_w, v_conv_b, v_w_gate_x, v_b_gate_x, v_w_gate_a, v_b_gate_a, v_lru_lambda, v_w_lru_o, v_cq_norm_g, v_ckv_norm_g, v_w_uq, v_w_ukv, v_mla_q_norm_g, v_mla_k_norm_g, v_w_mla_o, v_dil_q_norm_g, v_dil_k_norm_g, v_w_dil_o, v_b_merge, v_w_out):
    args = locals()
    w = {n: args[n] for n in WEIGHTS}
    m = {n: args["m_" + n] for n in WEIGHTS}
    v = {n: args["v_" + n] for n in WEIGHTS}
    my_c = lax.axis_index("c").astype(jnp.int32)
    my_s = (2 * lax.axis_index("x") + lax.axis_index("y")).astype(jnp.int32)
    c_idx = my_c.reshape(1)
    cs_idx = jnp.stack([my_c, my_s])

    names = list(SHARDED)
    wire = [w[n] if n == "conv_w" else w[n].astype(BF16) for n in names]
    gathered = allgather_weights(wire)
    full = {n: w[n] for n in REPLICATED}
    for n, ga, own in zip(names, gathered, wire):
        full[n] = jnp.concatenate([jnp.where(my_s == s, own, ga[s]) for s in range(4)], axis=SHARDED[n])
    full["w_in"] = _pad_in_cols(full["w_in"])

    loss, grad_x, grads = local_step(x[0], positions[0], loss_target[0], full)
    loss = lax.psum(loss, ("x", "y", "c"))

    flat = jnp.concatenate([grads[n].reshape(-1) for n in REPLICATED])
    small = jnp.pad(flat, (0, 8 * SMALL_ROWS * 1024 - flat.size)).reshape(4, 2, SMALL_ROWS, 1024)
    gs = []
    for n in names:
        parts = jnp.stack(jnp.split(grads[n], 4, axis=SHARDED[n]))
        gs.append(parts.reshape(4, 2, -1, parts.shape[-1]))
    gs.append(small)
    tags = names + ["small"]
    la = exchange_sibling(gs)
    wire_dt = [BF16 if g_.size >= 2**20 else F32 for g_ in gs]
    pcs = [sum_pair(f"sum_pair_{t}", g_, l_, c_idx, dt) for t, g_, l_, dt in zip(tags, gs, la, wire_dt)]
    lb = exchange_chips(pcs)
    rs = [sum_chips(f"sum_chips_{t}", g_, l_, b_, cs_idx) for t, g_, l_, b_ in zip(tags, gs, la, lb)]
    sib_half, small_all = exchange_final(rs[:-1], rs[-1])

    g_local, delta, new_m, new_v = {}, {}, {}, {}
    for n, rh, lc in zip(names, rs, sib_half):
        shp = w[n].shape
        as3 = lambda a: a.reshape((2,) + rh.shape)
        outs = adamw_pair(f"adamw_{n}", as3(w[n]), rh, lc, as3(m[n]), as3(v[n]), c_idx)
        g_local[n], delta[n], new_m[n], new_v[n] = [o.reshape(shp) for o in outs]
    flat = small_all.reshape(-1)
    off = 0
    for n in REPLICATED:
        g_local[n] = flat[off:off + w[n].size].reshape(w[n].shape)
        off += w[n].size
        delta[n], new_m[n], new_v[n] = adamw(f"adamw_{n}", w[n], g_local[n], m[n], v[n])
    return (loss, grad_x[None], *[g_local[n] for n in WEIGHTS], *[delta[n] for n in WEIGHTS], *[new_m[n] for n in WEIGHTS],
            *[new_v[n] for n in WEIGHTS])
```

```python
import functools

import jax
import jax.numpy as jnp
from jax import lax
from jax.experimental import pallas as pl
from jax.experimental.pallas import tpu as pltpu

F32 = jnp.float32
BF16 = jnp.bfloat16
MXU_DTYPE = jnp.bfloat16

D_MODEL = 1024
EPS = 1e-6
ROPE_THETA = 10000.0
LRU_BLOCKS = 8
LRU_C = 8.0
MLA_HEADS = 8
MLA_NOPE = 64
MLA_ROPE = 32
MLA_QK = 96
DIL_GROUPS = ((128, 1), (512, 4), (2048, 16))
DIL_HD = 64
DIL_NK = 128
MLA_HPS = 4
IN_WIDTH = 11168
ADAM_LR, ADAM_B1, ADAM_B2, ADAM_EPS, ADAM_WD, ADAM_STEP = 0.001, 0.9, 0.999, 1e-08, 0.01, 10

LANES = 128
G_LRU, G_MLA, G_DIL, G_MRG = 0, 2048, 3072, 8192
IN_PAD = 11264
KR_OFF = 2432

NN = (((1,), (0,)), ((), ()))
NT = (((1,), (1,)), ((), ()))
TN = (((0,), (0,)), ((), ()))
NEG = -1e30
MESH = pl.DeviceIdType.MESH
VMEM_LIMIT = 48 * 2**20


def _cparams(sem):
    return pltpu.CompilerParams(dimension_semantics=sem, vmem_limit_bytes=VMEM_LIMIT)


def _dot(a, b, dims):
    return lax.dot_general(a.astype(MXU_DTYPE), b.astype(MXU_DTYPE), dims, preferred_element_type=F32)


@jax.custom_vjp
def mm(a, w):
    return _dot(a, w, NN)


def _mm_fwd(a, w):
    return _dot(a, w, NN), (a, w)


def _mm_bwd(res, g):
    a, w = res
    return _dot(g, w, NT), _dot(a, g, TN)


mm.defvjp(_mm_fwd, _mm_bwd)


def _seg64_matrix():
    r = lax.broadcasted_iota(jnp.int32, (LANES, LANES), 0) < DIL_HD
    c = lax.broadcasted_iota(jnp.int32, (LANES, LANES), 1) < DIL_HD
    return (r == c).astype(BF16)


def _seg_sum_impl(x):
    b = _seg64_matrix()
    hi = x.astype(BF16)
    r1 = x - hi.astype(F32)
    mid = r1.astype(BF16)
    lo = (r1 - mid.astype(F32)).astype(BF16)
    dot = lambda u: lax.dot_general(u, b, NN, preferred_element_type=F32)
    return dot(hi) + dot(mid) + dot(lo)


@jax.custom_vjp
def seg_sum(x):
    return _seg_sum_impl(x)


seg_sum.defvjp(lambda x: (_seg_sum_impl(x), None), lambda _, g: (_seg_sum_impl(g),))


def _lroll_impl(x, s):
    return pltpu.roll(x, s % LANES, 1)


@functools.partial(jax.custom_vjp, nondiff_argnums=(1,))
def lroll(x, s):
    return _lroll_impl(x, s)


lroll.defvjp(lambda x, s: (_lroll_impl(x, s), None), lambda s, _, g: (_lroll_impl(g, -s),))


class _Ops:
    def __init__(self, diff):
        self.mm = mm if diff else (lambda a, w: _dot(a, w, NN))
        self.seg_sum = seg_sum if diff else _seg_sum_impl
        self.lroll = lroll if diff else _lroll_impl


PLAIN, DIFF = _Ops(False), _Ops(True)


def rms(x, g, n):
    ms = jnp.sum(x * x, axis=-1, keepdims=True) * (1.0 / n)
    return x * lax.rsqrt(ms + EPS) * g


def rope(ops, x, c, s1, s2, half):
    return x * c + ops.lroll(x, -half) * s1 + ops.lroll(x, half) * s2


def _sigmoid(x):
    return 1.0 / (1.0 + jnp.exp(-x))


def _silu_and_grad(g):
    sg = _sigmoid(g)
    return g * sg, sg * (1.0 + g * (1.0 - sg))


def _softplus(x):
    return jnp.maximum(x, 0.0) + jnp.log(1.0 + jnp.exp(-jnp.abs(x)))


def _expm1(y):
    series = y * (1.0 + y * (0.5 + y * (1.0 / 6.0 + y * (1.0 / 24.0 + y * (1.0 / 120.0)))))
    return jnp.where(y > -0.05, series, jnp.exp(jnp.minimum(y, -0.05)) - 1.0)


def _mm_call(name, a, b, *, mode, m, n, k, a_blk, b_blk, out_dtype, add, tm, tn, tk):
    nk = k // tk
    assert m % tm == 0 and n % tn == 0 and k % tk == 0, (name, m, n, k, tm, tn, tk)
    dims = {"nn": NN, "nt": NT, "tn": TN}[mode]

    def body(*refs):
        if add is None:
            a_ref, b_ref, o_ref, *scr = refs
            add_ref = None
        else:
            a_ref, b_ref, add_ref, o_ref, *scr = refs
        part = _dot(a_ref[...], b_ref[...], dims)

        def finish(acc):
            if add_ref is not None:
                acc = acc + add_ref[...]
            o_ref[...] = acc.astype(o_ref.dtype)

        if nk == 1:
            finish(part)
        else:
            (acc_ref,) = scr
            kk = pl.program_id(2)

            @pl.when(kk == 0)
            def _():
                acc_ref[...] = part

            @pl.when(kk > 0)
            def _():
                acc_ref[...] += part

            @pl.when(kk == nk - 1)
            def _():
                finish(acc_ref[...])

    in_specs = [a_blk, b_blk]
    args = [a, b]
    if add is not None:
        in_specs.append(pl.BlockSpec((tm, tn), lambda j, i, kk: (i, j)))
        args.append(add)
    return pl.pallas_call(
        body,
        name=name,
        grid=(n // tn, m // tm, nk),
        in_specs=in_specs,
        out_specs=pl.BlockSpec((tm, tn), lambda j, i, kk: (i, j)),
        out_shape=jax.ShapeDtypeStruct((m, n), out_dtype),
        scratch_shapes=[] if nk == 1 else [pltpu.VMEM((tm, tn), F32)],
        compiler_params=_cparams(("parallel", "parallel", "arbitrary")),
    )(*args)


def mm_nn(name, a, b, *, n=None, n_off=0, out_dtype=F32, add=None, tm=512, tn=1024, tk=1024):
    m, k = a.shape
    n = b.shape[1] if n is None else n
    tm, tn, tk = min(tm, m), min(tn, n), min(k, tk)
    ob = n_off // tn
    assert n_off % tn == 0
    return _mm_call(name, a, b, mode="nn", m=m, n=n, k=k, out_dtype=out_dtype, add=add, tm=tm, tn=tn, tk=tk,
                    a_blk=pl.BlockSpec((tm, tk), lambda j, i, kk: (i, kk)),
                    b_blk=pl.BlockSpec((tk, tn), lambda j, i, kk: (kk, j + ob)))


def mm_nt(name, a, b, *, out_dtype=F32, tm=512, tn=1024):
    m, k = a.shape
    n = b.shape[0]
    tn, tk = min(tn, n), min(k, 1024)
    return _mm_call(name, a, b, mode="nt", m=m, n=n, k=k, out_dtype=out_dtype, add=None, tm=tm, tn=tn, tk=tk,
                    a_blk=pl.BlockSpec((tm, tk), lambda j, i, kk: (i, kk)),
                    b_blk=pl.BlockSpec((tn, tk), lambda j, i, kk: (j, kk)))


DW_TILES = dict(tm=1024, tn=512, tk=4096)


def rowwise(name, fn, t, *, tm=256, ncol=1, ins=(), outs=(), touts=(), accs=()):
    n_in, n_out, n_acc = len(ins), len(outs) + len(touts), len(accs)

    def zero_when(ref, cond):
        @pl.when(cond)
        def _():
            ref[...] = jnp.zeros(ref.shape, ref.dtype)

    def body(*refs):
        in_refs, out_refs, acc_refs = refs[:n_in], refs[n_in:n_in + n_out], refs[n_in + n_out:]
        j, i = pl.program_id(0), pl.program_id(1)
        for ref, (_, _, _, cd) in zip(acc_refs, accs):
            zero_when(ref, (i == 0) if cd else ((i == 0) & (j == 0)))
        fn(in_refs, out_refs, acc_refs)

    def in_spec(arr, w, base, cd, rd):
        rows = tm if rd else arr.shape[0]
        return pl.BlockSpec((rows, w), lambda j, i: (i if rd else 0, base + (j if cd else 0)))

    in_specs = [in_spec(*e) for e in ins]
    out_specs = [pl.BlockSpec((tm, w), (lambda j, i, cd=cd: (i, j if cd else 0))) for (_, w, _, cd) in outs]
    out_specs += [pl.BlockSpec((w, tm), (lambda j, i, cd=cd: (j if cd else 0, i))) for (_, w, _, cd) in touts]
    out_specs += [pl.BlockSpec((r, w), (lambda j, i, cd=cd: (0, j if cd else 0))) for (r, _, w, cd) in accs]
    out_shape = [jax.ShapeDtypeStruct((t, c), dt) for (c, _, dt, _) in outs]
    out_shape += [jax.ShapeDtypeStruct((r, t), dt) for (r, _, dt, _) in touts]
    out_shape += [jax.ShapeDtypeStruct((r, c), F32) for (r, c, _, _) in accs]
    res = pl.pallas_call(
        body,
        name=name,
        grid=(ncol, t // tm),
        in_specs=in_specs,
        out_specs=out_specs,
        out_shape=out_shape,
        compiler_params=_cparams(("arbitrary", "arbitrary") if accs else ("parallel", "parallel")),
    )(*[e[0] for e in ins])
    return res


def _row(arr, w=None, base=0, cd=False):
    return (arr, arr.shape[1] if w is None else w, base, cd, True)


def _const(arr, w=None, base=0, cd=False):
    return (arr, arr.shape[1] if w is None else w, base, cd, False)


def rope_tables(positions):
    t = positions.shape[0]

    def fn(ins, outs, _):
        pos = ins[0][...].astype(F32)
        lane = lax.broadcasted_iota(jnp.int32, (1, LANES), 1)
        log_theta = jnp.log(jnp.float32(ROPE_THETA))
        jm = lane - MLA_NOPE
        idx = jnp.clip(jnp.where(jm < 16, jm, jm - 16), 0, 15).astype(F32)
        ang = pos * jnp.exp(-(idx * (2.0 / MLA_ROPE)) * log_theta)
        cos, sin = jnp.cos(ang), jnp.sin(ang)
        in_rope = (lane >= MLA_NOPE) & (lane < MLA_QK)
        outs[0][:, 0:128] = jnp.where(lane < MLA_NOPE, 1.0, jnp.where(in_rope, cos, 0.0))
        outs[0][:, 128:256] = jnp.where(in_rope & (jm < 16), -sin, 0.0)
        outs[0][:, 256:384] = jnp.where(in_rope & (jm >= 16), sin, 0.0)
        jd = lane & (DIL_HD - 1)
        idx = (jd & 31).astype(F32)
        ang = pos * jnp.exp(-(idx * (2.0 / DIL_HD)) * log_theta)
        cos, sin = jnp.cos(ang), jnp.sin(ang)
        outs[0][:, 384:512] = cos
        outs[0][:, 512:640] = jnp.where(jd < 32, -sin, 0.0)
        outs[0][:, 640:768] = jnp.where(jd >= 32, sin, 0.0)

    return rowwise("rope_tables", fn, t, ins=[_row(positions)], outs=[(768, 768, F32, False)])[0]


def rmsnorm_fwd(x, g):
    def fn(ins, outs, _):
        h = rms(ins[0][...], ins[1][...], D_MODEL)
        outs[0][...] = h.astype(MXU_DTYPE)
        outs[1][...] = h.T.astype(MXU_DTYPE)

    return rowwise("rmsnorm_fwd", fn, x.shape[0], ins=[_row(x), _const(g)], outs=[(D_MODEL, D_MODEL, MXU_DTYPE, False)],
                   touts=[(D_MODEL, D_MODEL, MXU_DTYPE, False)])


def rmsnorm_bwd(x, dh, dres, g):
    def fn(ins, outs, accs):
        _, vjp = jax.vjp(lambda xv, gv: rms(xv, gv, D_MODEL), ins[0][...], ins[3][...])
        dx, dg = vjp(ins[1][...])
        outs[0][...] = ins[2][...] + dx
        accs[0][...] += dg

    return rowwise("rmsnorm_bwd", fn, x.shape[0], ins=[_row(x), _row(dh), _row(dres), _const(g)],
                   outs=[(D_MODEL, D_MODEL, F32, False)], accs=[(1, D_MODEL, D_MODEL, False)])


def loss_head(y, target):
    def fn(ins, outs, accs):
        err = ins[0][...] - ins[1][...]
        outs[0][...] = err * (1.0 / D_MODEL)
        accs[0][...] += jnp.sum(err * err, axis=0, keepdims=True)
        accs[1][...] = jnp.broadcast_to(jnp.sum(accs[0][...], keepdims=True), (1, LANES))

    dy, _, tot = rowwise("loss_head", fn, y.shape[0], ins=[_row(y), _row(target)], outs=[(D_MODEL, D_MODEL, F32, False)],
                         accs=[(1, D_MODEL, D_MODEL, False), (1, LANES, LANES, False)])
    return dy, tot[0, 0] * (0.5 / D_MODEL)


def _shift_rows(v, d, fill, reverse):
    tb = v.shape[0]
    rows = lax.broadcasted_iota(jnp.int32, v.shape, 0)
    if not reverse:
        return jnp.where(rows >= d, pltpu.roll(v, d, 0), fill)
    return jnp.where(rows < tb - d, pltpu.roll(v, tb - d, 0), fill)


def _scan_tile(a, b, reverse):
    d = 1
    while d < a.shape[0]:
        b = b + a * _shift_rows(b, d, 0.0, reverse)
        a = a * _shift_rows(a, d, 1.0, reverse)
        d *= 2
    return a, b


def _lru_gates(ops, xc, wgx, bgx, wga, bga, lam):
    gx = _sigmoid(ops.mm(xc, wgx) + bgx)
    ga = _sigmoid(ops.mm(xc, wga) + bga)
    log_a = -LRU_C * ga * _softplus(-lam)
    a = jnp.exp(log_a)
    mult = jnp.sqrt(-_expm1(2.0 * log_a))
    return a, mult * (gx * xc)


def _shifted_inputs(x, halo, tb):
    rows = lax.broadcasted_iota(jnp.int32, x.shape, 0)
    pad = jnp.zeros((tb - 8, LANES), F32)
    out = []
    for d in (3, 2, 1):
        head = jnp.concatenate([pltpu.roll(halo, d, 0), pad], axis=0)
        out.append(jnp.where(rows >= d, pltpu.roll(x, d, 0), head))
    return out + [x]


def _lru_specs(nt, tb, reverse):
    hb = tb // 8
    tt = (lambda t: nt - 1 - t) if reverse else (lambda t: t)
    blk = lambda off: pl.BlockSpec((tb, LANES), lambda n, t: (tt(t), n + off))
    halo = lambda off: pl.BlockSpec((8, LANES), lambda n, t: (jnp.maximum(tt(t) * hb - 1, 0), n + off))
    chan = lambda r: pl.BlockSpec((r, LANES), lambda n, t: (0, n))
    wblk = pl.BlockSpec((None, LANES, LANES), lambda n, t: (n, 0, 0))
    bblk = pl.BlockSpec((None, 1, LANES), lambda n, t: (n, 0, 0))
    return blk, halo, chan, wblk, bblk


def lru_fwd(z, conv_w, conv_b, wgx, bgx, wga, bga, lam, *, tb=256):
    t = z.shape[0]
    nt = t // tb
    blk, halo, chan, wblk, bblk = _lru_specs(nt, tb, False)

    def body(x_ref, xh_ref, g_ref, cw_ref, cb_ref, wgx_ref, bgx_ref, wga_ref, bga_ref, lam_ref, h_ref, y_ref, yt_ref, carry_ref):
        ti = pl.program_id(1)

        @pl.when(ti == 0)
        def _():
            carry_ref[...] = jnp.zeros((8, LANES), F32)

        x = x_ref[...]
        hal = jnp.where(ti > 0, xh_ref[...], 0.0)
        xs = _shifted_inputs(x, hal, tb)
        xc = cb_ref[...] + sum(xs[k] * cw_ref[k:k + 1, :] for k in range(4))
        a, b = _lru_gates(PLAIN, xc, wgx_ref[...], bgx_ref[...], wga_ref[...], bga_ref[...], lam_ref[...])
        acum, h0 = _scan_tile(a, b, False)
        h = h0 + acum * carry_ref[7:8, :]
        carry_ref[...] = h[tb - 8:tb, :]
        h_ref[...] = h
        y = h * _silu_and_grad(g_ref[...])[0]
        y_ref[...] = y.astype(MXU_DTYPE)
        yt_ref[...] = y.T.astype(MXU_DTYPE)

    return pl.pallas_call(
        body,
        name="lru_fwd",
        grid=(LRU_BLOCKS, nt),
        in_specs=[blk(0), halo(0), blk(LRU_BLOCKS), chan(4), chan(1), wblk, bblk, wblk, bblk, chan(1)],
        out_specs=[blk(0), blk(0), pl.BlockSpec((LANES, tb), lambda n, t_: (n, t_))],
        out_shape=[jax.ShapeDtypeStruct((t, 1024), F32), jax.ShapeDtypeStruct((t, 1024), MXU_DTYPE),
                   jax.ShapeDtypeStruct((1024, t), MXU_DTYPE)],
        scratch_shapes=[pltpu.VMEM((8, LANES), F32)],
        compiler_params=_cparams(("parallel", "arbitrary")),
    )(z, z, z, conv_w, conv_b, wgx, bgx, wga, bga, lam)


def lru_bwd(z, h, dy, conv_w, conv_b, wgx, bgx, wga, bga, lam, *, tb=256):
    t = z.shape[0]
    nt = t // tb
    blk, halo, chan, wblk, bblk = _lru_specs(nt, tb, True)

    def body(x_ref, xh_ref, g_ref, h_ref, hh_ref, dy_ref, cw_ref, cb_ref, wgx_ref, bgx_ref, wga_ref, bga_ref, lam_ref,
             dzx_ref, dzg_ref, dwgx_ref, dbgx_ref, dwga_ref, dbga_ref, dlam_ref, dcw_ref, dcb_ref,
             gcar_ref, acar_ref, xcar_ref):
        ti = pl.program_id(1)
        has_earlier = ti < nt - 1

        @pl.when(ti == 0)
        def _():
            for ref in (dwgx_ref, dbgx_ref, dwga_ref, dbga_ref, dlam_ref, dcw_ref, dcb_ref, gcar_ref, acar_ref, xcar_ref):
                ref[...] = jnp.zeros(ref.shape, F32)

        rows = lax.broadcasted_iota(jnp.int32, (tb, LANES), 0)
        x = x_ref[...]
        hal = jnp.where(has_earlier, xh_ref[...], 0.0)
        xs = _shifted_inputs(x, hal, tb)
        xc = cb_ref[...] + sum(xs[k] * cw_ref[k:k + 1, :] for k in range(4))
        (a, _), vjp = jax.vjp(functools.partial(_lru_gates, DIFF), xc, wgx_ref[...], bgx_ref[...], wga_ref[...],
                              bga_ref[...], lam_ref[...])
        g, h, dyv = g_ref[...], h_ref[...], dy_ref[...]
        silu, dsilu = _silu_and_grad(g)
        dzg_ref[...] = (dyv * h * dsilu).astype(MXU_DTYPE)
        a_next = jnp.where(rows < tb - 1, pltpu.roll(a, tb - 1, 0), acar_ref[0:1, :])
        acum, g0 = _scan_tile(a_next, dyv * silu, True)
        gt = g0 + acum * gcar_ref[0:1, :]
        h_prev = jnp.where(rows >= 1, pltpu.roll(h, 1, 0), jnp.where(has_earlier, hh_ref[7:8, :], 0.0))
        dxc, dwgx, dbgx, dwga, dbga, dlam = vjp((gt * h_prev, gt))
        later = xcar_ref[...]
        gcar_ref[...] = gt[0:8, :]
        acar_ref[...] = a[0:8, :]
        xcar_ref[...] = dxc[0:8, :]
        dx = dxc * cw_ref[3:4, :]
        pad = jnp.zeros((tb - 8, LANES), F32)
        for d in (1, 2, 3):
            tail = jnp.concatenate([pad, pltpu.roll(later, 8 - d, 0)], axis=0)
            up = jnp.where(rows < tb - d, pltpu.roll(dxc, tb - d, 0), tail)
            dx = dx + up * cw_ref[3 - d:4 - d, :]
        dzx_ref[...] = dx.astype(MXU_DTYPE)
        for k in range(4):
            dcw_ref[k:k + 1, :] += jnp.sum(dxc * xs[k], axis=0, keepdims=True)
        dcb_ref[...] += jnp.sum(dxc, axis=0, keepdims=True)
        dwgx_ref[...] += dwgx
        dbgx_ref[...] += dbgx
        dwga_ref[...] += dwga
        dbga_ref[...] += dbga
        dlam_ref[...] += dlam

    sds = jax.ShapeDtypeStruct
    return pl.pallas_call(
        body,
        name="lru_bwd",
        grid=(LRU_BLOCKS, nt),
        in_specs=[blk(0), halo(0), blk(LRU_BLOCKS), blk(0), halo(0), blk(0), chan(4), chan(1), wblk, bblk, wblk, bblk, chan(1)],
        out_specs=[blk(0), blk(0), wblk, bblk, wblk, bblk, chan(1), chan(4), chan(1)],
        out_shape=[sds((t, 1024), MXU_DTYPE), sds((t, 1024), MXU_DTYPE), sds(wgx.shape, F32), sds(bgx.shape, F32), sds(wga.shape, F32),
                   sds(bga.shape, F32), sds((1, 1024), F32), sds((4, 1024), F32), sds((1, 1024), F32)],
        scratch_shapes=[pltpu.VMEM((8, LANES), F32)] * 3,
        compiler_params=_cparams(("parallel", "arbitrary")),
    )(z, z, z, h, h, dy, conv_w, conv_b, wgx, bgx, wga, bga, lam)


def _mla_prep_tile(ops, cq, ckv, krp, c, s1, s2, g_cq, g_ckv, wq, wk, wv, gq, gk):
    cqn = rms(cq, g_cq, 256)
    ckvn = rms(ckv, g_ckv, 128)
    v = ops.mm(ckvn, wv)
    qs, ks = [], []
    for hd in range(MLA_HEADS):
        q = rms(ops.mm(cqn, wq[hd]), gq, MLA_QK)
        k = rms(ops.mm(ckvn, wk[hd]) + krp, gk, MLA_QK)
        qs.append(rope(ops, q, c, s1, s2, 16) * (MLA_QK ** -0.5))
        ks.append(rope(ops, k, c, s1, s2, 16))
    return tuple(qs), tuple(ks), v


def _mla_prep_args(ins):
    z_cq, z_ckv, z_kr, tc, ts1, ts2, g_cq, g_ckv, wq, wk, wv, gq, gk = ins[:13]
    heads = lambda w: tuple(w[:, LANES * hd:LANES * (hd + 1)] for hd in range(MLA_HEADS))
    return (z_cq[...], z_ckv[...], z_kr[...], tc[...], ts1[...], ts2[...], g_cq[...], g_ckv[...], heads(wq), heads(wk),
            wv[...], gq[...], gk[...])


def _mla_prep_ins(z, tabs, w):
    return [_row(z, 256, 0), _row(z, 128, 2), _row(z, 128, 3), _row(tabs, 128, 0), _row(tabs, 128, 1), _row(tabs, 128, 2),
            _const(w["cq_norm_g"]), _const(w["ckv_norm_g"]), _const(w["wq"]), _const(w["wk"]), _const(w["wv"]),
            _const(w["gq"]), _const(w["gk"])]


def mla_prep_fwd(z, tabs, w):
    def fn(ins, outs, _):
        qs, ks, v = _mla_prep_tile(PLAIN, *_mla_prep_args(ins))
        for hd in range(MLA_HEADS):
            outs[0][:, LANES * hd:LANES * (hd + 1)] = qs[hd].astype(MXU_DTYPE)
            outs[1][:, LANES * hd:LANES * (hd + 1)] = ks[hd].astype(MXU_DTYPE)
        outs[2][...] = v.astype(MXU_DTYPE)
        outs[3][...] = v.T.astype(MXU_DTYPE)

    return rowwise("mla_prep_fwd", fn, z.shape[0], ins=_mla_prep_ins(z, tabs, w),
                   outs=[(1024, 1024, MXU_DTYPE, False), (1024, 1024, MXU_DTYPE, False), (512, 512, MXU_DTYPE, False)],
                   touts=[(512, 512, MXU_DTYPE, False)])


def mla_prep_bwd(z, tabs, w, dq, dk, dv):
    def fn(ins, outs, accs):
        args = _mla_prep_args(ins)
        _, vjp = jax.vjp(functools.partial(_mla_prep_tile, DIFF), *args)
        heads = lambda ref: tuple(ref[:, LANES * hd:LANES * (hd + 1)] for hd in range(MLA_HEADS))
        dcq, dckv, dkr, _, _, _, dg_cq, dg_ckv, dwq, dwk, dwv, dgq, dgk = vjp((heads(ins[13]), heads(ins[14]), ins[15][...]))
        lane = lax.broadcasted_iota(jnp.int32, (1, LANES), 1)
        outs[0][:, 0:256] = dcq.astype(MXU_DTYPE)
        outs[0][:, 256:384] = dckv.astype(MXU_DTYPE)
        outs[0][:, 384:512] = jnp.where((lane >= MLA_NOPE) & (lane < MLA_QK), dkr, 0.0).astype(MXU_DTYPE)
        accs[0][...] += dg_cq
        accs[1][...] += dg_ckv
        for hd in range(MLA_HEADS):
            accs[2][:, LANES * hd:LANES * (hd + 1)] += dwq[hd]
            accs[3][:, LANES * hd:LANES * (hd + 1)] += dwk[hd]
        accs[4][...] += dwv
        accs[5][...] += dgq
        accs[6][...] += dgk

    return rowwise("mla_prep_bwd", fn, z.shape[0], ins=_mla_prep_ins(z, tabs, w) + [_row(dq), _row(dk), _row(dv)],
                   outs=[(512, 512, MXU_DTYPE, False)],
                   accs=[(1, 256, 256, False), (1, 128, 128, False), (256, 1024, 1024, False), (128, 1024, 1024, False),
                         (128, 512, 512, False), (1, 128, 128, False), (1, 128, 128, False)])


def _head_masks():
    lane = lax.broadcasted_iota(jnp.int32, (1, LANES), 1)
    return (lane < DIL_HD, lane >= DIL_HD)


def _row_scalar(tile, mask):
    return jnp.max(jnp.where(mask, tile, -jnp.inf), axis=-1, keepdims=True)


def _causal_tiles(nq, by_key):
    pairs = [(i, j) for i in range(nq) for j in range(i + 1)]
    if by_key:
        pairs.sort(key=lambda ij: (ij[1], ij[0]))
    return (jnp.asarray([ij[0] for ij in pairs], jnp.int32), jnp.asarray([ij[1] for ij in pairs], jnp.int32))


def mla_attn_fwd(q, k, vt, z, *, tq=256):
    t = q.shape[0]
    nq = t // tq
    it, jt = _causal_tiles(nq, False)
    hps, wq, wv = MLA_HPS, LANES * MLA_HPS, 64 * MLA_HPS

    def body(it_ref, jt_ref, q_ref, k_ref, vt_ref, g_ref, o_ref, y_ref, yt_ref, lse_ref, m_scr, l_scr, acc_scr):
        step = pl.program_id(1)
        i, j = it_ref[step], jt_ref[step]

        @pl.when(j == 0)
        def _():
            m_scr[...] = jnp.full(m_scr.shape, NEG, F32)
            l_scr[...] = jnp.zeros(l_scr.shape, F32)
            acc_scr[...] = jnp.zeros(acc_scr.shape, F32)

        def update(diagonal):
            heads = range(hps)
            lanes = [slice(LANES * hh, LANES * (hh + 1)) for hh in heads]
            rows = [slice(64 * hh, 64 * (hh + 1)) for hh in heads]
            sts = [_dot(k_ref[:, lanes[hh]], q_ref[:, lanes[hh]], NT) for hh in heads]
            m_prev = [m_scr[hh:hh + 1, :] for hh in heads]
            l_prev = [l_scr[hh:hh + 1, :] for hh in heads]
            acc_prev = [acc_scr[rows[hh], :] for hh in heads]
            m_new, l_new, acc_new = [], [], []
            for hh in heads:
                st = sts[hh]
                if diagonal:
                    key = lax.broadcasted_iota(jnp.int32, (tq, tq), 0)
                    qry = lax.broadcasted_iota(jnp.int32, (tq, tq), 1)
                    st = jnp.where(key <= qry, st, NEG)
                m_new.append(jnp.maximum(m_prev[hh], jnp.max(st, axis=0, keepdims=True)))
                alpha = jnp.exp(m_prev[hh] - m_new[hh])
                pt = jnp.exp(st - m_new[hh])
                l_new.append(alpha * l_prev[hh] + jnp.sum(pt, axis=0, keepdims=True))
                acc_new.append(alpha * acc_prev[hh] + _dot(vt_ref[rows[hh], :], pt, NN))
            for hh in heads:
                m_scr[hh:hh + 1, :] = m_new[hh]
                l_scr[hh:hh + 1, :] = l_new[hh]
                acc_scr[rows[hh], :] = acc_new[hh]

        @pl.when(j < i)
        def _():
            update(False)

        @pl.when(j == i)
        def _():
            update(True)
            lse_ref[...] = jnp.zeros(lse_ref.shape, F32)
            for hh in range(hps):
                rows = slice(64 * hh, 64 * (hh + 1))
                acc_scr[rows, :] = acc_scr[rows, :] / l_scr[hh:hh + 1, :]
                lse_ref[hh:hh + 1, :] = m_scr[hh:hh + 1, :] + jnp.log(l_scr[hh:hh + 1, :])
            o = acc_scr[...].T
            o_ref[...] = o
            y = o * _silu_and_grad(g_ref[...])[0]
            y_ref[...] = y.astype(MXU_DTYPE)
            yt_ref[...] = y.T.astype(MXU_DTYPE)

    qo = lambda w, off=0: pl.BlockSpec((tq, w), lambda p, s, it_, jt_: (it_[s], p + off))
    sds = jax.ShapeDtypeStruct
    return pl.pallas_call(
        body,
        name="mla_attn_fwd",
        grid_spec=pltpu.PrefetchScalarGridSpec(
            num_scalar_prefetch=2,
            grid=(MLA_HEADS // hps, it.shape[0]),
            in_specs=[qo(wq), pl.BlockSpec((tq, wq), lambda p, s, it_, jt_: (jt_[s], p)),
                      pl.BlockSpec((wv, tq), lambda p, s, it_, jt_: (p, jt_[s])), qo(wv, 512 // wv)],
            out_specs=[qo(wv), qo(wv), pl.BlockSpec((wv, tq), lambda p, s, it_, jt_: (p, it_[s])),
                       pl.BlockSpec((None, 8, tq), lambda p, s, it_, jt_: (p, 0, it_[s]))],
            scratch_shapes=[pltpu.VMEM((8, tq), F32), pltpu.VMEM((8, tq), F32), pltpu.VMEM((wv, tq), F32)],
        ),
        out_shape=[sds((t, 512), F32), sds((t, 512), MXU_DTYPE), sds((512, t), MXU_DTYPE), sds((MLA_HEADS // hps, 8, t), F32)],
        compiler_params=_cparams(("parallel", "arbitrary")),
    )(it, jt, q, k, vt, z)


def mla_attn_bwd(q, k, v, do, lse, dd, *, tq=256):
    t = q.shape[0]
    nq = t // tq
    it, jt = _causal_tiles(nq, True)
    hps, wq, wv = MLA_HPS, LANES * MLA_HPS, 64 * MLA_HPS

    def body(it_ref, jt_ref, q_ref, k_ref, v_ref, do_ref, lse_ref, d_ref, dq_ref, dk_ref, dv_ref, dk_scr, dv_scr):
        step = pl.program_id(1)
        i, j = it_ref[step], jt_ref[step]
        masks = _head_masks()

        @pl.when(step == 0)
        def _():
            dq_ref[...] = jnp.zeros(dq_ref.shape, F32)

        @pl.when(i == j)
        def _():
            dk_scr[...] = jnp.zeros(dk_scr.shape, F32)
            dv_scr[...] = jnp.zeros(dv_scr.shape, F32)

        def update(diagonal):
            qrows = pl.ds(pl.multiple_of(i * tq, tq), tq)
            heads = range(hps)
            lanes = [slice(LANES * hh, LANES * (hh + 1)) for hh in heads]
            pair = [slice(LANES * (hh // 2), LANES * (hh // 2 + 1)) for hh in heads]
            qh = [q_ref[:, lanes[hh]] for hh in heads]
            kh = [k_ref[:, lanes[hh]] for hh in heads]
            doh = []
            for hh in heads:
                dov = do_ref[:, pair[hh]]
                doh.append(jnp.where(masks[hh % 2], dov, jnp.zeros_like(dov)))
            sts = [_dot(kh[hh], qh[hh], NT) for hh in heads]
            dps = [_dot(v_ref[:, pair[hh]], doh[hh], NT) for hh in heads]
            lse = [lse_ref[hh:hh + 1, :] for hh in heads]
            ddv = [d_ref[hh:hh + 1, :] for hh in heads]
            dk_new = [dk_scr[:, lanes[hh]] for hh in heads]
            dq_new = [dq_ref[qrows, lanes[hh]] for hh in heads]
            dv_new = [dv_scr[:, pair[2 * pp]] for pp in range(hps // 2)]
            for hh in heads:
                st = sts[hh] - lse[hh]
                if diagonal:
                    key = lax.broadcasted_iota(jnp.int32, (tq, tq), 0)
                    qry = lax.broadcasted_iota(jnp.int32, (tq, tq), 1)
                    st = jnp.where(key <= qry, st, NEG)
                pt = jnp.exp(st)
                dst = pt * (dps[hh] - ddv[hh])
                dv_new[hh // 2] = dv_new[hh // 2] + _dot(pt, doh[hh], NN)
                dk_new[hh] = dk_new[hh] + _dot(dst, qh[hh], NN)
                dq_new[hh] = dq_new[hh] + _dot(dst, kh[hh], TN)
            for hh in heads:
                dk_scr[:, lanes[hh]] = dk_new[hh]
                dq_ref[qrows, lanes[hh]] = dq_new[hh]
            for pp in range(hps // 2):
                dv_scr[:, pair[2 * pp]] = dv_new[pp]

        @pl.when(j < i)
        def _():
            update(False)

        @pl.when(j == i)
        def _():
            update(True)

        @pl.when(i == nq - 1)
        def _():
            dk_ref[...] = dk_scr[...]
            dv_ref[...] = dv_scr[...]

    qo = lambda w: pl.BlockSpec((tq, w), lambda p, s, it_, jt_: (it_[s], p))
    kv = lambda w: pl.BlockSpec((tq, w), lambda p, s, it_, jt_: (jt_[s], p))
    stat = pl.BlockSpec((None, 8, tq), lambda p, s, it_, jt_: (p, 0, it_[s]))
    sds = jax.ShapeDtypeStruct
    return pl.pallas_call(
        body,
        name="mla_attn_bwd",
        grid_spec=pltpu.PrefetchScalarGridSpec(
            num_scalar_prefetch=2,
            grid=(MLA_HEADS // hps, it.shape[0]),
            in_specs=[qo(wq), kv(wq), kv(wv), qo(wv), stat, stat],
            out_specs=[pl.BlockSpec((t, wq), lambda p, s, it_, jt_: (0, p)), kv(wq), kv(wv)],
            scratch_shapes=[pltpu.VMEM((tq, wq), F32), pltpu.VMEM((tq, wv), F32)],
        ),
        out_shape=[sds((t, 1024), F32), sds((t, 1024), F32), sds((t, 512), F32)],
        compiler_params=_cparams(("parallel", "arbitrary")),
    )(it, jt, q, k, v, do, lse, dd)


def mla_gate_bwd(dy, o, z, *, tm=256):
    t = dy.shape[0]
    wv = 64 * MLA_HPS

    def body(dy_ref, o_ref, g_ref, do_ref, dzg_ref, dd_ref):
        dyv, ov = dy_ref[...], o_ref[...]
        silu, dsilu = _silu_and_grad(g_ref[...])
        do = dyv * silu
        do_ref[...] = do.astype(MXU_DTYPE)
        dzg_ref[...] = (dyv * ov * dsilu).astype(MXU_DTYPE)
        prod = do * ov
        row = lax.broadcasted_iota(jnp.int32, (8, wv), 0)
        lane = lax.broadcasted_iota(jnp.int32, (8, wv), 1)
        pick = ((lane >= row * DIL_HD) & (lane < (row + 1) * DIL_HD)).astype(BF16)
        hi = prod.astype(BF16)
        r1 = prod - hi.astype(F32)
        mid = r1.astype(BF16)
        lo = (r1 - mid.astype(F32)).astype(BF16)
        dot = lambda u: lax.dot_general(pick, u, NT, preferred_element_type=F32)
        dd_ref[...] = dot(hi) + dot(mid) + dot(lo)

    blk = lambda off=0: pl.BlockSpec((tm, wv), lambda p, i: (i, p + off))
    sds = jax.ShapeDtypeStruct
    return pl.pallas_call(
        body,
        name="mla_gate_bwd",
        grid=(MLA_HEADS // MLA_HPS, t // tm),
        in_specs=[blk(), blk(), blk(512 // wv)],
        out_specs=[blk(), blk(), pl.BlockSpec((None, 8, tm), lambda p, i: (p, 0, i))],
        out_shape=[sds((t, 512), MXU_DTYPE), sds((t, 512), MXU_DTYPE), sds((MLA_HEADS // MLA_HPS, 8, t), F32)],
        compiler_params=_cparams(("parallel", "parallel")),
    )(dy, o, z)


SPAN = 2048
DIL_SCALE = DIL_HD ** -0.5


def _rm_src(u, window, dil):
    nn, r = divmod(u, dil)
    return pl.ds(nn * window + r, DIL_NK, stride=dil) if dil > 1 else pl.ds(u * DIL_NK, DIL_NK)


def _rm_dst(u):
    return pl.ds(u * DIL_NK, DIL_NK)


def _dil_prep_tile(ops, x, g2, c, s1, s2, scale):
    ms = ops.seg_sum(x * x) * (1.0 / DIL_HD)
    return rope(ops, x * lax.rsqrt(ms + EPS) * g2, c, s1, s2, 32) * scale


def _span_blk(base):
    return pl.BlockSpec((SPAN, LANES), lambda s, p: (s, base + p))


def _const_blk(shape):
    return pl.BlockSpec(shape, lambda s, p: (0,) * len(shape))


_DIL_TABLE_SPECS = [pl.BlockSpec((SPAN, LANES), (lambda s, p, blk=blk: (s, blk))) for blk in (3, 4, 5)]


def dil_prep_fwd(z, tabs, gq2, gk2, gi):
    window, dil = DIL_GROUPS[gi]
    t = z.shape[0]

    def body(q_ref, k_ref, v_ref, c_ref, s1_ref, s2_ref, gq_ref, gk_ref, qo_ref, ko_ref, vo_ref):
        for u in range(SPAN // DIL_NK):
            src, dst = _rm_src(u, window, dil), _rm_dst(u)
            c, s1, s2 = c_ref[src, :], s1_ref[src, :], s2_ref[src, :]
            qo_ref[dst, :] = _dil_prep_tile(PLAIN, q_ref[src, :], gq_ref[...], c, s1, s2, DIL_SCALE).astype(MXU_DTYPE)
            ko_ref[dst, :] = _dil_prep_tile(PLAIN, k_ref[src, :], gk_ref[...], c, s1, s2, 1.0).astype(MXU_DTYPE)
            vo_ref[dst, :] = v_ref[src, :].astype(MXU_DTYPE)

    return pl.pallas_call(
        body,
        name=f"dil_prep_fwd_{dil}",
        grid=(t // SPAN, 4),
        in_specs=[_span_blk(4 * gi), _span_blk(12 + 4 * gi), _span_blk(24 + 4 * gi)] + _DIL_TABLE_SPECS
        + [_const_blk((1, LANES)), _const_blk((1, LANES))],
        out_specs=[_span_blk(0)] * 3,
        out_shape=[jax.ShapeDtypeStruct((t, 512), MXU_DTYPE)] * 3,
        compiler_params=_cparams(("parallel", "parallel")),
    )(z, z, z, tabs, tabs, tabs, gq2, gk2)


def dil_prep_bwd(z, tabs, gq2, gk2, dq, dk, dv, gi):
    window, dil = DIL_GROUPS[gi]
    t = z.shape[0]

    def body(q_ref, k_ref, c_ref, s1_ref, s2_ref, gq_ref, gk_ref, dq_ref, dk_ref, dv_ref, dzq_ref, dzk_ref, dzv_ref, dgq_ref,
             dgk_ref, sq, sk, sv):
        @pl.when((pl.program_id(0) == 0) & (pl.program_id(1) == 0))
        def _():
            dgq_ref[...] = jnp.zeros((1, LANES), F32)
            dgk_ref[...] = jnp.zeros((1, LANES), F32)

        dgs = [jnp.zeros((1, LANES), F32), jnp.zeros((1, LANES), F32)]
        for u in range(SPAN // DIL_NK):
            src, dst = _rm_src(u, window, dil), _rm_dst(u)
            c, s1, s2 = c_ref[src, :], s1_ref[src, :], s2_ref[src, :]
            for idx, (x_ref, g_ref, ct_ref, scr, scale) in enumerate(((q_ref, gq_ref, dq_ref, sq, DIL_SCALE),
                                                                      (k_ref, gk_ref, dk_ref, sk, 1.0))):
                _, vjp = jax.vjp(lambda xv, gv, sc=scale: _dil_prep_tile(DIFF, xv, gv, c, s1, s2, sc), x_ref[src, :], g_ref[...])
                dx, dg = vjp(ct_ref[dst, :])
                scr[src, :] = dx
                dgs[idx] = dgs[idx] + dg
            sv[src, :] = dv_ref[dst, :]
        dgq_ref[...] += dgs[0] + pltpu.roll(dgs[0], DIL_HD, 1)
        dgk_ref[...] += dgs[1] + pltpu.roll(dgs[1], DIL_HD, 1)
        for c0 in range(0, SPAN, 256):
            rows = slice(c0, c0 + 256)
            dzq_ref[rows, :] = sq[rows, :].astype(MXU_DTYPE)
            dzk_ref[rows, :] = sk[rows, :].astype(MXU_DTYPE)
            dzv_ref[rows, :] = sv[rows, :].astype(MXU_DTYPE)

    sds = jax.ShapeDtypeStruct
    return pl.pallas_call(
        body,
        name=f"dil_prep_bwd_{dil}",
        grid=(t // SPAN, 4),
        in_specs=[_span_blk(4 * gi), _span_blk(12 + 4 * gi)] + _DIL_TABLE_SPECS
        + [_const_blk((1, LANES)), _const_blk((1, LANES)), _span_blk(0), _span_blk(0), _span_blk(0)],
        out_specs=[_span_blk(0)] * 3 + [_const_blk((1, LANES))] * 2,
        out_shape=[sds((t, 512), MXU_DTYPE)] * 3 + [sds((1, LANES), F32)] * 2,
        scratch_shapes=[pltpu.VMEM((SPAN, LANES), F32)] * 3,
        compiler_params=_cparams(("arbitrary", "arbitrary")),
    )(z, z, tabs, tabs, tabs, gq2, gk2, dq, dk, dv)


def _band_masks():
    qi = lax.broadcasted_iota(jnp.int32, (DIL_NK, DIL_NK), 0)
    ki = lax.broadcasted_iota(jnp.int32, (DIL_NK, DIL_NK), 1)
    return (ki >= qi), (ki <= qi)


def _pair_heads():
    return [(pair, hh) for pair in range(4) for hh in range(2)]


def _pair_lanes(pair):
    return slice(LANES * pair, LANES * (pair + 1))


def _zero_other_head(mask, x):
    return jnp.where(mask, x, jnp.zeros_like(x))


def dil_attn_fwd(name, q, k, v, dil):
    t = q.shape[0]

    def body(q_ref, kp_ref, kc_ref, vp_ref, vc_ref, o_ref, lse_ref):
        b = pl.program_id(0)
        mprev, mcur = _band_masks()
        mprev = mprev & (b >= dil)
        hm = _head_masks()
        heads = _pair_heads()
        qh = [_zero_other_head(hm[hh], q_ref[:, _pair_lanes(pair)]) for pair, hh in heads]
        sps = [_dot(qh[i], kp_ref[:, _pair_lanes(pair)], NT) for i, (pair, _) in enumerate(heads)]
        scs = [_dot(qh[i], kc_ref[:, _pair_lanes(pair)], NT) for i, (pair, _) in enumerate(heads)]
        o = [jnp.zeros((DIL_NK, LANES), F32) for _ in range(4)]
        lse = [jnp.zeros((DIL_NK, LANES), F32) for _ in range(4)]
        for i, (pair, hh) in enumerate(heads):
            sp, sc = jnp.where(mprev, sps[i], NEG), jnp.where(mcur, scs[i], NEG)
            m = jnp.maximum(jnp.max(sp, axis=1, keepdims=True), jnp.max(sc, axis=1, keepdims=True))
            ep, ec = jnp.exp(sp - m), jnp.exp(sc - m)
            den = jnp.sum(ep, axis=1, keepdims=True) + jnp.sum(ec, axis=1, keepdims=True)
            lanes = _pair_lanes(pair)
            oh = _dot(ep, _zero_other_head(hm[hh], vp_ref[:, lanes]), NN) + _dot(ec, _zero_other_head(hm[hh], vc_ref[:, lanes]), NN)
            o[pair] = o[pair] + oh / den
            lse[pair] = jnp.where(hm[hh], m + jnp.log(den), lse[pair])
        for pair in range(4):
            o_ref[:, _pair_lanes(pair)] = o[pair]
            lse_ref[:, _pair_lanes(pair)] = lse[pair]

    cur = pl.BlockSpec((DIL_NK, 512), lambda b: (b, 0))
    prev = pl.BlockSpec((DIL_NK, 512), lambda b: (jnp.maximum(b - dil, 0), 0))
    sds = jax.ShapeDtypeStruct
    return pl.pallas_call(
        body,
        name=name,
        grid=(t // DIL_NK,),
        in_specs=[cur, prev, cur, prev, cur],
        out_specs=[cur, cur],
        out_shape=[sds((t, 512), F32), sds((t, 512), F32)],
        compiler_params=_cparams(("parallel",)),
    )(q, k, k, v, v)


def dil_attn_bwd(name, q, k, v, do, lse, dd, dil):
    t = q.shape[0]
    nblk = t // DIL_NK

    def body(qc_ref, qn_ref, doc_ref, don_ref, lc_ref, ln_ref, dc_ref, dn_ref, kp_ref, kc_ref, vp_ref, vc_ref,
             dq_ref, dk_ref, dv_ref):
        b = pl.program_id(0)
        mprev, mcur = _band_masks()
        mnext = mprev & (b + dil < nblk)
        mprev = mprev & (b >= dil)
        hm = _head_masks()
        heads = _pair_heads()
        ln_ = [_pair_lanes(pair) for pair, _ in heads]
        qh = [_zero_other_head(hm[hh], qc_ref[:, ln_[i]]) for i, (_, hh) in enumerate(heads)]
        qnh = [_zero_other_head(hm[hh], qn_ref[:, ln_[i]]) for i, (_, hh) in enumerate(heads)]
        doh = [_zero_other_head(hm[hh], doc_ref[:, ln_[i]]) for i, (_, hh) in enumerate(heads)]
        donh = [_zero_other_head(hm[hh], don_ref[:, ln_[i]]) for i, (_, hh) in enumerate(heads)]
        idx = range(len(heads))
        s_p = [_dot(qh[i], kp_ref[:, ln_[i]], NT) for i in idx]
        s_c = [_dot(qh[i], kc_ref[:, ln_[i]], NT) for i in idx]
        s_n = [_dot(qnh[i], kc_ref[:, ln_[i]], NT) for i in idx]
        dp_p = [_dot(doh[i], vp_ref[:, ln_[i]], NT) for i in idx]
        dp_c = [_dot(doh[i], vc_ref[:, ln_[i]], NT) for i in idx]
        dp_n = [_dot(donh[i], vc_ref[:, ln_[i]], NT) for i in idx]
        dq = [jnp.zeros((DIL_NK, LANES), F32) for _ in range(4)]
        dk = [jnp.zeros((DIL_NK, LANES), F32) for _ in range(4)]
        dv = [jnp.zeros((DIL_NK, LANES), F32) for _ in range(4)]
        for i, (pair, hh) in enumerate(heads):
            lse_h, d_h = _row_scalar(lc_ref[:, ln_[i]], hm[hh]), _row_scalar(dc_ref[:, ln_[i]], hm[hh])
            lse_n, d_n = _row_scalar(ln_ref[:, ln_[i]], hm[hh]), _row_scalar(dn_ref[:, ln_[i]], hm[hh])
            pp = jnp.exp(jnp.where(mprev, s_p[i] - lse_h, NEG))
            pc = jnp.exp(jnp.where(mcur, s_c[i] - lse_h, NEG))
            pn = jnp.exp(jnp.where(mnext, s_n[i] - lse_n, NEG))
            dsp, dsc, dsn = pp * (dp_p[i] - d_h), pc * (dp_c[i] - d_h), pn * (dp_n[i] - d_n)
            dq[pair] = (dq[pair] + _dot(dsp, _zero_other_head(hm[hh], kp_ref[:, ln_[i]]), NN)
                        + _dot(dsc, _zero_other_head(hm[hh], kc_ref[:, ln_[i]]), NN))
            dv[pair] = dv[pair] + _dot(pc, doh[i], TN) + _dot(pn, donh[i], TN)
            dk[pair] = dk[pair] + _dot(dsc, qh[i], TN) + _dot(dsn, qnh[i], TN)
        for pair in range(4):
            dq_ref[:, _pair_lanes(pair)] = dq[pair]
            dk_ref[:, _pair_lanes(pair)] = dk[pair]
            dv_ref[:, _pair_lanes(pair)] = dv[pair]

    cur = pl.BlockSpec((DIL_NK, 512), lambda b: (b, 0))
    prev = pl.BlockSpec((DIL_NK, 512), lambda b: (jnp.maximum(b - dil, 0), 0))
    nxt = pl.BlockSpec((DIL_NK, 512), lambda b: (jnp.minimum(b + dil, nblk - 1), 0))
    sds = jax.ShapeDtypeStruct
    return pl.pallas_call(
        body,
        name=name,
        grid=(nblk,),
        in_specs=[cur, nxt, cur, nxt, cur, nxt, cur, nxt, prev, cur, prev, cur],
        out_specs=[cur, cur, cur],
        out_shape=[sds((t, 512), F32)] * 3,
        compiler_params=_cparams(("parallel",)),
    )(q, q, do, do, lse, lse, dd, dd, k, k, v, v)


def dil_combine(os_, lses, z):
    t = z.shape[0]

    def body(o1_ref, l1_ref, o2_ref, l2_ref, o3_ref, l3_ref, g_ref, y_ref, o_ref, lse_ref, yt_ref, so2, sl2, so3, sl3):
        for (window, dil), o_in, l_in, so, sl in ((DIL_GROUPS[1], o2_ref, l2_ref, so2, sl2), (DIL_GROUPS[2], o3_ref, l3_ref, so3, sl3)):
            for u in range(SPAN // DIL_NK):
                src, dst = _rm_src(u, window, dil), _rm_dst(u)
                so[src, :] = o_in[dst, :]
                sl[src, :] = l_in[dst, :]
        for c0 in range(0, SPAN, 256):
            rows = slice(c0, c0 + 256)
            la, lb, lc = l1_ref[rows, :], sl2[rows, :], sl3[rows, :]
            mx = jnp.maximum(jnp.maximum(la, lb), lc)
            wa, wb, wc = jnp.exp(la - mx), jnp.exp(lb - mx), jnp.exp(lc - mx)
            tot = wa + wb + wc
            o = (wa * o1_ref[rows, :] + wb * so2[rows, :] + wc * so3[rows, :]) / tot
            y = o * _silu_and_grad(g_ref[rows, :])[0]
            y_ref[rows, :] = y.astype(MXU_DTYPE)
            o_ref[rows, :] = o
            lse_ref[rows, :] = mx + jnp.log(tot)
            yt_ref[:, rows] = y.T.astype(MXU_DTYPE)

    sds = jax.ShapeDtypeStruct
    return pl.pallas_call(
        body,
        name="dil_combine",
        grid=(t // SPAN, 4),
        in_specs=[_span_blk(0)] * 6 + [_span_blk(36)],
        out_specs=[_span_blk(0)] * 3 + [pl.BlockSpec((LANES, SPAN), lambda s, p: (p, s))],
        out_shape=[sds((t, 512), MXU_DTYPE), sds((t, 512), F32), sds((t, 512), F32), sds((512, t), MXU_DTYPE)],
        scratch_shapes=[pltpu.VMEM((SPAN, LANES), F32)] * 4,
        compiler_params=_cparams(("parallel", "parallel")),
    )(os_[0], lses[0], os_[1], lses[1], os_[2], lses[2], z)


def dil_gate_bwd(dy, o, z, lse):
    t = dy.shape[0]

    def body(dy_ref, o_ref, g_ref, lse_ref, dzg_ref, do1_ref, dd1_ref, do2_ref, dd2_ref, l2_ref, do3_ref, dd3_ref, l3_ref, do_scr):
        for c0 in range(0, SPAN, 256):
            rows = slice(c0, c0 + 256)
            dyv, ov = dy_ref[rows, :], o_ref[rows, :]
            silu, dsilu = _silu_and_grad(g_ref[rows, :])
            do = dyv * silu
            do_scr[rows, :] = do
            do1_ref[rows, :] = do.astype(MXU_DTYPE)
            dd1_ref[rows, :] = _seg_sum_impl(do * ov)
            dzg_ref[rows, :] = (dyv * ov * dsilu).astype(MXU_DTYPE)
        for (window, dil), do_o, dd_o, l_o in ((DIL_GROUPS[1], do2_ref, dd2_ref, l2_ref), (DIL_GROUPS[2], do3_ref, dd3_ref, l3_ref)):
            for u in range(SPAN // DIL_NK):
                src, dst = _rm_src(u, window, dil), _rm_dst(u)
                do_o[dst, :] = do_scr[src, :].astype(MXU_DTYPE)
                dd_o[dst, :] = dd1_ref[src, :]
                l_o[dst, :] = lse_ref[src, :]

    sds = jax.ShapeDtypeStruct
    f32, mxu = sds((t, 512), F32), sds((t, 512), MXU_DTYPE)
    return pl.pallas_call(
        body,
        name="dil_gate_bwd",
        grid=(t // SPAN, 4),
        in_specs=[_span_blk(0), _span_blk(0), _span_blk(36), _span_blk(0)],
        out_specs=[_span_blk(0)] * 9,
        out_shape=[mxu, mxu, f32, mxu, f32, f32, mxu, f32, f32],
        scratch_shapes=[pltpu.VMEM((SPAN, LANES), F32)],
        compiler_params=_cparams(("parallel", "parallel")),
    )(dy, o, z, lse)


def _merge_tile(p0, p1, p2, z0, z1, z2, b0, b1, b2):
    return _sigmoid(z0 + b0) * p0 + _sigmoid(z1 + b1) * p1 + _sigmoid(z2 + b2) * p2


def _merge_ins(ps, z, b):
    w = 256
    return ([_row(p, w, 0, True) for p in ps] + [_row(z, w, 4 * i, True) for i in range(3)]
            + [_const(b, w, 4 * i, True) for i in range(3)])


def merge_fwd(ps, z, b):
    def fn(ins, outs, _):
        merged = _merge_tile(*[r[...] for r in ins])
        outs[0][...] = merged.astype(MXU_DTYPE)
        outs[1][...] = merged.T.astype(MXU_DTYPE)

    return rowwise("merge_fwd", fn, z.shape[0], ncol=4, ins=_merge_ins(ps, z, b), outs=[(1024, 256, MXU_DTYPE, True)],
                   touts=[(1024, 256, MXU_DTYPE, True)])


def merge_bwd(dm, ps, z, b):
    def fn(ins, outs, accs):
        _, vjp = jax.vjp(_merge_tile, *[r[...] for r in ins[:9]])
        grads = vjp(ins[9][...])
        for i in range(3):
            outs[i][...] = grads[i].astype(MXU_DTYPE)
            outs[3 + i][...] = grads[3 + i].astype(MXU_DTYPE)
            accs[i][...] += grads[6 + i]

    return rowwise("merge_bwd", fn, z.shape[0], ncol=4, ins=_merge_ins(ps, z, b) + [_row(dm, 256, 0, True)],
                   outs=[(1024, 256, MXU_DTYPE, True)] * 6, accs=[(1, 1024, 256, True)] * 3)


def _row_tile(r, c):
    if r * c * 4 <= 2**21:
        return r
    for tr in (512, 256, 128, 64, 32, 16, 8):
        if r % tr == 0 and tr * c * 4 <= 2**21:
            return tr
    raise ValueError((r, c))


def adamw(name, w, g, m, v):
    shape = w.shape
    c = shape[-1]
    r = w.size // c
    tr = _row_tile(r, c)
    c1, c2 = 1.0 - ADAM_B1 ** ADAM_STEP, 1.0 - ADAM_B2 ** ADAM_STEP

    def body(w_ref, g_ref, m_ref, v_ref, d_ref, mo_ref, vo_ref):
        gv = g_ref[...]
        mn = ADAM_B1 * m_ref[...] + (1.0 - ADAM_B1) * gv
        vn = ADAM_B2 * v_ref[...] + (1.0 - ADAM_B2) * (gv * gv)
        d_ref[...] = -ADAM_LR * ((mn / c1) / (jnp.sqrt(vn / c2) + ADAM_EPS) + ADAM_WD * w_ref[...])
        mo_ref[...] = mn
        vo_ref[...] = vn

    spec = pl.BlockSpec((tr, c), lambda i: (i, 0))
    outs = pl.pallas_call(
        body,
        name=name,
        grid=(r // tr,),
        in_specs=[spec] * 4,
        out_specs=[spec] * 3,
        out_shape=[jax.ShapeDtypeStruct((r, c), F32)] * 3,
        compiler_params=_cparams(("parallel",)),
    )(*[a.reshape(r, c) for a in (w, g, m, v)])
    return [o.reshape(shape) for o in outs]


def adamw_pair(name, w, rh, lc, m, v, c_idx):
    _, r, c = w.shape
    tr = _row_tile(r, c)
    c1, c2 = 1.0 - ADAM_B1 ** ADAM_STEP, 1.0 - ADAM_B2 ** ADAM_STEP

    def body(c_ref, w_ref, rh_ref, lc_ref, m_ref, v_ref, g_ref, d_ref, mo_ref, vo_ref):
        gv = jnp.where(pl.program_id(0) == c_ref[0], rh_ref[...], lc_ref[...])
        mn = ADAM_B1 * m_ref[...] + (1.0 - ADAM_B1) * gv
        vn = ADAM_B2 * v_ref[...] + (1.0 - ADAM_B2) * (gv * gv)
        g_ref[...] = gv
        d_ref[...] = -ADAM_LR * ((mn / c1) / (jnp.sqrt(vn / c2) + ADAM_EPS) + ADAM_WD * w_ref[...])
        mo_ref[...] = mn
        vo_ref[...] = vn

    full = pl.BlockSpec((None, tr, c), lambda h, i, cr: (h, i, 0))
    half = pl.BlockSpec((tr, c), lambda h, i, cr: (i, 0))
    return pl.pallas_call(
        body,
        name=name,
        grid_spec=pltpu.PrefetchScalarGridSpec(
            num_scalar_prefetch=1,
            grid=(2, r // tr),
            in_specs=[full, half, half, full, full],
            out_specs=[full] * 4,
        ),
        out_shape=[jax.ShapeDtypeStruct((2, r, c), F32)] * 4,
        compiler_params=_cparams(("parallel", "parallel")),
    )(c_idx, w, rh, lc, m, v)


def sum_pair(name, g, la, c_idx, out_dtype):
    _, _, r, c = g.shape
    tr = _row_tile(r, c)

    def body(c_ref, g_ref, la_ref, o_ref):
        o_ref[...] = (g_ref[...] + la_ref[...]).astype(o_ref.dtype)

    return pl.pallas_call(
        body,
        name=name,
        grid_spec=pltpu.PrefetchScalarGridSpec(
            num_scalar_prefetch=1,
            grid=(4, r // tr),
            in_specs=[pl.BlockSpec((None, None, tr, c), lambda s, i, cr: (s, cr[0], i, 0)),
                      pl.BlockSpec((None, tr, c), lambda s, i, cr: (s, i, 0))],
            out_specs=pl.BlockSpec((None, tr, c), lambda s, i, cr: (s, i, 0)),
        ),
        out_shape=jax.ShapeDtypeStruct((4, r, c), out_dtype),
        compiler_params=_cparams(("parallel", "parallel")),
    )(c_idx, g, la)


def sum_chips(name, g, la, lb, cs_idx):
    _, _, r, c = g.shape
    tr = _row_tile(r, c)

    def body(cs_ref, g_ref, la_ref, l0_ref, l1_ref, l2_ref, o_ref):
        own = g_ref[...] + la_ref[...]
        o_ref[...] = ((own + l0_ref[...].astype(F32)) + l1_ref[...].astype(F32)) + l2_ref[...].astype(F32)

    lspec = lambda k: pl.BlockSpec((None, tr, c), lambda i, cs: (k, i, 0))
    return pl.pallas_call(
        body,
        name=name,
        grid_spec=pltpu.PrefetchScalarGridSpec(
            num_scalar_prefetch=1,
            grid=(r // tr,),
            in_specs=[pl.BlockSpec((None, None, tr, c), lambda i, cs: (cs[1], cs[0], i, 0)),
                      pl.BlockSpec((None, tr, c), lambda i, cs: (cs[1], i, 0)), lspec(0), lspec(1), lspec(2)],
            out_specs=pl.BlockSpec((tr, c), lambda i, cs: (i, 0)),
        ),
        out_shape=jax.ShapeDtypeStruct((r, c), F32),
        compiler_params=_cparams(("parallel",)),
    )(cs_idx, g, la, lb, lb, lb)


def _place():
    x, y, c = lax.axis_index("x"), lax.axis_index("y"), lax.axis_index("c")
    chips = [(1 - x, y), (x, 1 - y), (1 - x, 1 - y)]
    return x, y, c, chips


HBM_SPEC = pl.BlockSpec(memory_space=pltpu.HBM)


def _comm_call(name, body, ins, out_shapes, n_sem, n_local):
    return pl.pallas_call(
        body,
        name=name,
        in_specs=[HBM_SPEC] * len(ins),
        out_specs=[HBM_SPEC] * len(out_shapes),
        out_shape=out_shapes,
        scratch_shapes=[pltpu.SemaphoreType.DMA((n_sem,)), pltpu.SemaphoreType.DMA((n_sem,)),
                        pltpu.SemaphoreType.DMA((max(n_local, 1),))],
    )(*ins)


def allgather_weights(ws):
    n = len(ws)

    def body(*refs):
        ins, outs = refs[:n], refs[n:2 * n]
        send, recv, _ = refs[2 * n:]
        x, y, c, chips = _place()
        s = 2 * x + y
        sib = (x, y, 1 - c)

        def rc(a, k, src, dst, dev):
            return pltpu.make_async_remote_copy(src_ref=src, dst_ref=dst, send_sem=send.at[6 * a + k], recv_sem=recv.at[6 * a + k],
                                                device_id=dev, device_id_type=MESH)

        sends = []
        for j, chip in enumerate(chips):
            for a in range(n):
                sends.append(rc(a, j, ins[a].at[c], outs[a].at[s, c], (*chip, c)))
                sends[-1].start()
        for j, (cx, cy) in enumerate(chips):
            sj = 2 * cx + cy
            for a in range(n):
                landed = outs[a].at[sj, c]
                rc(a, j, landed, landed, sib).wait_recv()
                sends.append(rc(a, 3 + j, landed, landed, sib))
                sends[-1].start()
        for j, (cx, cy) in enumerate(chips):
            sj = 2 * cx + cy
            for a in range(n):
                other = outs[a].at[sj, 1 - c]
                rc(a, 3 + j, other, other, sib).wait_recv()
        for cp in sends:
            cp.wait_send()

    out_shapes = [jax.ShapeDtypeStruct((4,) + w.shape, w.dtype) for w in ws]
    return _comm_call("allgather_weights", body, ws, out_shapes, 6 * n, 0)


def exchange_sibling(gs):
    n = len(gs)

    def body(*refs):
        ins, outs = refs[:n], refs[n:2 * n]
        send, recv, _ = refs[2 * n:]
        x, y, c, _ = _place()
        cps = []
        for a in range(n):
            for s in range(4):
                cps.append(pltpu.make_async_remote_copy(src_ref=ins[a].at[s, 1 - c], dst_ref=outs[a].at[s],
                                                        send_sem=send.at[4 * a + s], recv_sem=recv.at[4 * a + s],
                                                        device_id=(x, y, 1 - c), device_id_type=MESH))
                cps[-1].start()
        for cp in cps:
            cp.wait()

    out_shapes = [jax.ShapeDtypeStruct((4,) + g.shape[2:], g.dtype) for g in gs]
    return _comm_call("exchange_sibling", body, gs, out_shapes, 4 * n, 0)


def exchange_chips(ps):
    n = len(ps)

    def body(*refs):
        ins, outs = refs[:n], refs[n:2 * n]
        send, recv, _ = refs[2 * n:]
        x, y, c, chips = _place()
        cps = []
        for j, (cx, cy) in enumerate(chips):
            for a in range(n):
                cps.append(pltpu.make_async_remote_copy(src_ref=ins[a].at[2 * cx + cy], dst_ref=outs[a].at[j],
                                                        send_sem=send.at[3 * a + j], recv_sem=recv.at[3 * a + j],
                                                        device_id=(cx, cy, c), device_id_type=MESH))
                cps[-1].start()
        for cp in cps:
            cp.wait()

    out_shapes = [jax.ShapeDtypeStruct((3,) + p.shape[1:], p.dtype) for p in ps]
    return _comm_call("exchange_chips", body, ps, out_shapes, 3 * n, 0)


def exchange_final(rs, small):
    n = len(rs)

    def body(*refs):
        ins, small_ref = refs[:n], refs[n]
        outs, small_out = refs[n + 1:2 * n + 1], refs[2 * n + 1]
        send, recv, lsem = refs[2 * n + 2:]
        x, y, c, _ = _place()
        s = 2 * x + y
        cps, local = [], []
        for a in range(n):
            cps.append(pltpu.make_async_remote_copy(src_ref=ins[a], dst_ref=outs[a], send_sem=send.at[a], recv_sem=recv.at[a],
                                                    device_id=(x, y, 1 - c), device_id_type=MESH))
            cps[-1].start()
        local.append(pltpu.make_async_copy(small_ref, small_out.at[s, c], lsem.at[0]))
        local[-1].start()
        k = n
        for fx in (0, 1):
            for fy in (0, 1):
                for fc in (0, 1):
                    if fx or fy or fc:
                        dev = (x ^ fx, y ^ fy, c ^ fc)
                        cps.append(pltpu.make_async_remote_copy(src_ref=small_ref, dst_ref=small_out.at[s, c],
                                                                send_sem=send.at[k], recv_sem=recv.at[k],
                                                                device_id=dev, device_id_type=MESH))
                        cps[-1].start()
                        k += 1
        for cp in cps:
            cp.wait()
        for cp in local:
            cp.wait()

    out_shapes = [jax.ShapeDtypeStruct(r.shape, r.dtype) for r in rs]
    out_shapes.append(jax.ShapeDtypeStruct((4, 2) + small.shape, small.dtype))
    res = _comm_call("exchange_final", body, list(rs) + [small], out_shapes, n + 7, 1)
    return res[:n], res[n]


def _pad_in_cols(w):
    z = lambda n: jnp.zeros(w.shape[:-1] + (n,), w.dtype)
    return jnp.concatenate([w[..., :KR_OFF], z(64), w[..., KR_OFF:KR_OFF + MLA_ROPE], z(32), w[..., KR_OFF + MLA_ROPE:]], axis=-1)


def _unpad_in_cols(g):
    return jnp.concatenate([g[..., :KR_OFF], g[..., KR_OFF + 64:KR_OFF + 96], g[..., KR_OFF + 128:]], axis=-1)


def _pad_heads(w, real):
    k = w.shape[0]
    return jnp.pad(w.reshape(k, MLA_HEADS, real), ((0, 0), (0, 0), (0, LANES - real))).reshape(k, MLA_HEADS * LANES)


def _pad_gain(g, real):
    return jnp.pad(g.reshape(1, real), ((0, 0), (0, LANES - real)))


def layer_weights(l, full):
    w_ukv = full["w_ukv"][l].reshape(128, MLA_HEADS, 2, 64)
    two = lambda g: jnp.concatenate([g, g]).reshape(1, LANES)
    return dict(
        norm_g=full["norm_g"][l].reshape(1, -1), w_in=full["w_in"][l], conv_w=full["conv_w"][l], conv_b=full["conv_b"][l].reshape(1, -1),
        wgx=full["w_gate_x"][l], bgx=full["b_gate_x"][l].reshape(LRU_BLOCKS, 1, LANES),
        wga=full["w_gate_a"][l], bga=full["b_gate_a"][l].reshape(LRU_BLOCKS, 1, LANES),
        lam=full["lru_lambda"][l].reshape(1, -1), w_lru_o=full["w_lru_o"][l],
        cq_norm_g=full["cq_norm_g"][l].reshape(1, -1), ckv_norm_g=full["ckv_norm_g"][l].reshape(1, -1),
        wq=_pad_heads(full["w_uq"][l], MLA_QK), wk=_pad_heads(w_ukv[:, :, 0].reshape(128, 512), 64),
        wv=w_ukv[:, :, 1].reshape(128, 512),
        gq=_pad_gain(full["mla_q_norm_g"][l], MLA_QK), gk=_pad_gain(full["mla_k_norm_g"][l], MLA_QK),
        w_mla_o=full["w_mla_o"][l], gq2=two(full["dil_q_norm_g"][l]), gk2=two(full["dil_k_norm_g"][l]),
        w_dil_o=full["w_dil_o"][l], b_merge=full["b_merge"][l].reshape(1, -1), w_out=full["w_out"][l],
    )


def layer_fwd(x, w, tabs):
    t = x.shape[0]
    h, ht = rmsnorm_fwd(x, w["norm_g"])
    z_lru = mm_nn("in_proj_lru", h, w["w_in"], n=2048, n_off=G_LRU)
    z_mla = mm_nn("in_proj_mla", h, w["w_in"], n=1024, n_off=G_MLA)
    z_dil = mm_nn("in_proj_dil", h, w["w_in"], n=5120, n_off=G_DIL)
    z_mrg = mm_nn("in_proj_mrg", h, w["w_in"], n=3072, n_off=G_MRG)
    hs, y_lru, yt_lru = lru_fwd(z_lru, w["conv_w"], w["conv_b"], w["wgx"], w["bgx"], w["wga"], w["bga"], w["lam"])
    qm, km, vm, vtm = mla_prep_fwd(z_mla, tabs, w)
    o_mla, y_mla, yt_mla, lse_mla = mla_attn_fwd(qm, km, vtm, z_mla)
    os_, lses, dil_rm = [], [], []
    for gi, (_, dil) in enumerate(DIL_GROUPS):
        qkv = dil_prep_fwd(z_dil, tabs, w["gq2"], w["gk2"], gi)
        o, lse = dil_attn_fwd(f"dil_attn_fwd_{dil}", *qkv, dil)
        os_.append(o)
        lses.append(lse)
        dil_rm.append(qkv)
    y_dil, o_dil, lse_dil, yt_dil = dil_combine(os_, lses, z_dil)
    ps = [mm_nn("proj_lru", y_lru, w["w_lru_o"]), mm_nn("proj_mla", y_mla, w["w_mla_o"]), mm_nn("proj_dil", y_dil, w["w_dil_o"])]
    merged, merged_t = merge_fwd(ps, z_mrg, w["b_merge"])
    out = mm_nn("out_proj", merged, w["w_out"], add=x)
    res = dict(x=x, ht=ht, z_lru=z_lru, z_mla=z_mla, z_dil=z_dil, z_mrg=z_mrg, hs=hs, yt_lru=yt_lru, qm=qm, km=km, vm=vm, o_mla=o_mla,
               yt_mla=yt_mla, lse_mla=lse_mla, dil_rm=dil_rm, yt_dil=yt_dil, o_dil=o_dil, lse_dil=lse_dil, ps=ps, merged_t=merged_t)
    return out, res


def layer_bwd(dout, r, w, tabs):
    t = dout.shape[0]
    g = {}
    dmerged = mm_nt("out_proj_dx", dout, w["w_out"])
    g["w_out"] = mm_nn("out_proj_dw", r["merged_t"], dout, tm=1024, tn=512, tk=2048)
    dp0, dp1, dp2, dzm0, dzm1, dzm2, db0, db1, db2 = merge_bwd(dmerged, r["ps"], r["z_mrg"], w["b_merge"])
    g["b_merge"] = jnp.concatenate([db0, db1, db2], axis=1).reshape(-1)
    dy_lru = mm_nt("proj_lru_dx", dp0, w["w_lru_o"])
    dy_mla = mm_nt("proj_mla_dx", dp1, w["w_mla_o"])
    dy_dil = mm_nt("proj_dil_dx", dp2, w["w_dil_o"])
    g["w_lru_o"] = mm_nn("proj_lru_dw", r["yt_lru"], dp0, **DW_TILES)
    g["w_mla_o"] = mm_nn("proj_mla_dw", r["yt_mla"], dp1, **DW_TILES)
    g["w_dil_o"] = mm_nn("proj_dil_dw", r["yt_dil"], dp2, **DW_TILES)
    dzx, dzg_lru, dwgx, dbgx, dwga, dbga, dlam, dcw, dcb = lru_bwd(r["z_lru"], r["hs"], dy_lru, w["conv_w"], w["conv_b"], w["wgx"],
                                                                    w["bgx"], w["wga"], w["bga"], w["lam"])
    g.update(w_gate_x=dwgx, b_gate_x=dbgx.reshape(LRU_BLOCKS, LANES), w_gate_a=dwga, b_gate_a=dbga.reshape(LRU_BLOCKS, LANES),
             lru_lambda=dlam.reshape(-1), conv_w=dcw, conv_b=dcb.reshape(-1))
    do_m, dzg_mla, dd_m = mla_gate_bwd(dy_mla, r["o_mla"], r["z_mla"])
    dq_m, dk_m, dv_m = mla_attn_bwd(r["qm"], r["km"], r["vm"], do_m, r["lse_mla"], dd_m)
    dz_mla3, dg_cq, dg_ckv, dwq, dwk, dwv, dgq, dgk = mla_prep_bwd(r["z_mla"], tabs, w, dq_m, dk_m, dv_m)
    g.update(cq_norm_g=dg_cq.reshape(-1), ckv_norm_g=dg_ckv.reshape(-1), mla_q_norm_g=dgq[0, :MLA_QK], mla_k_norm_g=dgk[0, :MLA_QK])
    g["w_uq"] = dwq.reshape(256, MLA_HEADS, LANES)[:, :, :MLA_QK].reshape(256, MLA_HEADS * MLA_QK)
    g["w_ukv"] = jnp.concatenate([dwk.reshape(128, MLA_HEADS, LANES)[:, :, :64], dwv.reshape(128, MLA_HEADS, 64)], axis=2).reshape(128, 1024)
    dzg_dil, do1, dd1, do2, dd2, l2, do3, dd3, l3 = dil_gate_bwd(dy_dil, r["o_dil"], r["z_dil"], r["lse_dil"])
    stats = [(do1, r["lse_dil"], dd1), (do2, l2, dd2), (do3, l3, dd3)]
    dzq, dzk, dzv, dgq2, dgk2 = [], [], [], [], []
    for gi, (_, dil) in enumerate(DIL_GROUPS):
        dq, dk, dv = dil_attn_bwd(f"dil_attn_bwd_{dil}", *r["dil_rm"][gi], *stats[gi], dil)
        parts = dil_prep_bwd(r["z_dil"], tabs, w["gq2"], w["gk2"], dq, dk, dv, gi)
        for acc, part in zip((dzq, dzk, dzv, dgq2, dgk2), parts):
            acc.append(part)
    g.update(dil_q_norm_g=sum(dgq2)[0, :DIL_HD], dil_k_norm_g=sum(dgk2)[0, :DIL_HD])
    dz = jnp.concatenate([dzx, dzg_lru, dz_mla3, dzg_mla] + dzq + dzk + dzv + [dzg_dil, dzm0, dzm1, dzm2], axis=1)
    dh = mm_nt("in_proj_dx", dz, w["w_in"])
    g["w_in"] = _unpad_in_cols(mm_nn("in_proj_dw", r["ht"], dz, **DW_TILES))
    dx, dng = rmsnorm_bwd(r["x"], dh, dout, w["norm_g"])
    g["norm_g"] = dng.reshape(-1)
    return dx, g


def local_step(x, positions, target, full):
    tabs = rope_tables(positions.reshape(-1, 1))
    ws, ress = [], []
    for l in range(2):
        ws.append(layer_weights(l, full))
        x, res = layer_fwd(x, ws[l], tabs)
        ress.append(res)
    dy, loss = loss_head(x, target)
    grads = [None, None]
    for l in (1, 0):
        dy, grads[l] = layer_bwd(dy, ress[l], ws[l], tabs)
    return loss, dy, {k: jnp.stack([grads[0][k], grads[1][k]]) for k in grads[0]}


WEIGHTS = ["norm_g", "w_in", "conv_w", "conv_b", "w_gate_x", "b_gate_x", "w_gate_a", "b_gate_a", "lru_lambda", "w_lru_o", "cq_norm_g",
           "ckv_norm_g", "w_uq", "w_ukv", "mla_q_norm_g", "mla_k_norm_g", "w_mla_o", "dil_q_norm_g", "dil_k_norm_g", "w_dil_o", "b_merge",
           "w_out"]
SHARDED = {"w_in": 2, "conv_w": 2, "w_lru_o": 1, "w_uq": 2, "w_ukv": 2, "w_mla_o": 2, "w_dil_o": 2, "w_out": 1}
REPLICATED = [n for n in WEIGHTS if n not in SHARDED]
SMALL_ROWS = 72


def kernel(x, positions, norm_g, w_in, conv_w, conv_b, w_gate_x, b_gate_x, w_gate_a, b_gate_a, lru_lambda, w_lru_o, cq_norm_g, ckv_norm_g, w_uq, w_ukv, mla_q_norm_g, mla_k_norm_g, w_mla_o, dil_q_norm_g, dil_k_norm_g, w_dil_o, b_merge, w_out, loss_target, m_norm_g, m_w_in, m_conv_w, m_conv_b, m_w_gate_x, m_b_gate_x, m_w_gate_a, m_b_gate_a, m_lru_lambda, m_w_lru_o, m_cq_norm_g, m_ckv_norm_g, m_w_uq, m_w_ukv, m_mla_q_norm_g, m_mla_k_norm_g, m_w_mla_o, m_dil_q_norm_g, m_dil_k_norm_g, m_w_dil_o, m_b_merge, m_w_out, v_norm_g, v_w_in, v_conv_w, v_conv_b, v_w_gate_x, v_b_gate_x, v_w_gate_a, v_b_gate_a, v_lru_lambda, v_w_lru_o, v_cq_norm_g, v_ckv_norm_g, v_w_uq, v_w_ukv, v_mla_q_norm_g, v_mla_k_norm_g, v_w_mla_o, v_dil_q_norm_g, v_dil_k_norm_g, v_w_dil_o, v_b_merge, v_w_out):
    args = locals()
    w = {n: args[n] for n in WEIGHTS}
    m = {n: args["m_" + n] for n in WEIGHTS}
    v = {n: args["v_" + n] for n in WEIGHTS}
    my_c = lax.axis_index("c").astype(jnp.int32)
    my_s = (2 * lax.axis_index("x") + lax.axis_index("y")).astype(jnp.int32)
    c_idx = my_c.reshape(1)
    cs_idx = jnp.stack([my_c, my_s])

    names = list(SHARDED)
    wire = [w[n] if n == "conv_w" else w[n].astype(BF16) for n in names]
    gathered = allgather_weights(wire)
    full = {n: w[n] for n in REPLICATED}
    for n, ga, own in zip(names, gathered, wire):
        full[n] = jnp.concatenate([jnp.where(my_s == s, own, ga[s]) for s in range(4)], axis=SHARDED[n])
    full["w_in"] = _pad_in_cols(full["w_in"])

    loss, grad_x, grads = local_step(x[0], positions[0], loss_target[0], full)
    loss = lax.psum(loss, ("x", "y", "c"))

    flat = jnp.concatenate([grads[n].reshape(-1) for n in REPLICATED])
    small = jnp.pad(flat, (0, 8 * SMALL_ROWS * 1024 - flat.size)).reshape(4, 2, SMALL_ROWS, 1024)
    gs = []
    for n in names:
        parts = jnp.stack(jnp.split(grads[n], 4, axis=SHARDED[n]))
        gs.append(parts.reshape(4, 2, -1, parts.shape[-1]))
    gs.append(small)
    tags = names + ["small"]
    la = exchange_sibling(gs)
    wire_dt = [BF16 if g_.size >= 2**20 else F32 for g_ in gs]
    pcs = [sum_pair(f"sum_pair_{t}", g_, l_, c_idx, dt) for t, g_, l_, dt in zip(tags, gs, la, wire_dt)]
    lb = exchange_chips(pcs)
    rs = [sum_chips(f"sum_chips_{t}", g_, l_, b_, cs_idx) for t, g_, l_, b_ in zip(tags, gs, la, lb)]
    sib_half, small_all = exchange_final(rs[:-1], rs[-1])

    g_local, delta, new_m, new_v = {}, {}, {}, {}
    for n, rh, lc in zip(names, rs, sib_half):
        shp = w[n].shape
        as3 = lambda a: a.reshape((2,) + rh.shape)
        outs = adamw_pair(f"adamw_{n}", as3(w[n]), rh, lc, as3(m[n]), as3(v[n]), c_idx)
        g_local[n], delta[n], new_m[n], new_v[n] = [o.reshape(shp) for o in outs]
    flat = small_all.reshape(-1)
    off = 0
    for n in REPLICATED:
        g_local[n] = flat[off:off + w[n].size].reshape(w[n].shape)
        off += w[n].size
        delta[n], new_m[n], new_v[n] = adamw(f"adamw_{n}", w[n], g_local[n], m[n], v[n])
    return (loss, grad_x[None], *[g_local[n] for n in WEIGHTS], *[delta[n] for n in WEIGHTS], *[new_m[n] for n in WEIGHTS],
            *[new_v[n] for n in WEIGHTS])
```

```python
import functools

import jax
import jax.numpy as jnp
from jax import lax
from jax.experimental import pallas as pl
from jax.experimental.pallas import tpu as pltpu

F32 = jnp.float32
BF16 = jnp.bfloat16
MXU_DTYPE = jnp.bfloat16

D_MODEL = 1024
EPS = 1e-6
ROPE_THETA = 10000.0
LRU_BLOCKS = 8
LRU_C = 8.0
MLA_HEADS = 8
MLA_NOPE = 64
MLA_ROPE = 32
MLA_QK = 96
DIL_GROUPS = ((128, 1), (512, 4), (2048, 16))
DIL_HD = 64
DIL_NK = 128
MLA_HPS = 8
IN_WIDTH = 11168
ADAM_LR, ADAM_B1, ADAM_B2, ADAM_EPS, ADAM_WD, ADAM_STEP = 0.001, 0.9, 0.999, 1e-08, 0.01, 10

LANES = 128
G_LRU, G_MLA, G_DIL, G_MRG = 0, 2048, 3072, 8192
IN_PAD = 11264
KR_OFF = 2432

NN = (((1,), (0,)), ((), ()))
NT = (((1,), (1,)), ((), ()))
TN = (((0,), (0,)), ((), ()))
NEG = -1e30
MESH = pl.DeviceIdType.MESH
VMEM_LIMIT = 48 * 2**20


def _cparams(sem):
    return pltpu.CompilerParams(dimension_semantics=sem, vmem_limit_bytes=VMEM_LIMIT)


def _dot(a, b, dims):
    return lax.dot_general(a.astype(MXU_DTYPE), b.astype(MXU_DTYPE), dims, preferred_element_type=F32)


@jax.custom_vjp
def mm(a, w):
    return _dot(a, w, NN)


def _mm_fwd(a, w):
    return _dot(a, w, NN), (a, w)


def _mm_bwd(res, g):
    a, w = res
    return _dot(g, w, NT), _dot(a, g, TN)


mm.defvjp(_mm_fwd, _mm_bwd)


def _seg64_matrix():
    r = lax.broadcasted_iota(jnp.int32, (LANES, LANES), 0) < DIL_HD
    c = lax.broadcasted_iota(jnp.int32, (LANES, LANES), 1) < DIL_HD
    return (r == c).astype(BF16)


def _seg_sum_impl(x):
    b = _seg64_matrix()
    hi = x.astype(BF16)
    r1 = x - hi.astype(F32)
    mid = r1.astype(BF16)
    lo = (r1 - mid.astype(F32)).astype(BF16)
    dot = lambda u: lax.dot_general(u, b, NN, preferred_element_type=F32)
    return dot(hi) + dot(mid) + dot(lo)


@jax.custom_vjp
def seg_sum(x):
    return _seg_sum_impl(x)


seg_sum.defvjp(lambda x: (_seg_sum_impl(x), None), lambda _, g: (_seg_sum_impl(g),))


def _lroll_impl(x, s):
    return pltpu.roll(x, s % LANES, 1)


@functools.partial(jax.custom_vjp, nondiff_argnums=(1,))
def lroll(x, s):
    return _lroll_impl(x, s)


lroll.defvjp(lambda x, s: (_lroll_impl(x, s), None), lambda s, _, g: (_lroll_impl(g, -s),))


class _Ops:
    def __init__(self, diff):
        self.mm = mm if diff else (lambda a, w: _dot(a, w, NN))
        self.seg_sum = seg_sum if diff else _seg_sum_impl
        self.lroll = lroll if diff else _lroll_impl


PLAIN, DIFF = _Ops(False), _Ops(True)


def rms(x, g, n):
    ms = jnp.sum(x * x, axis=-1, keepdims=True) * (1.0 / n)
    return x * lax.rsqrt(ms + EPS) * g


def rope(ops, x, c, s1, s2, half):
    return x * c + ops.lroll(x, -half) * s1 + ops.lroll(x, half) * s2


def _sigmoid(x):
    return 1.0 / (1.0 + jnp.exp(-x))


def _silu_and_grad(g):
    sg = _sigmoid(g)
    return g * sg, sg * (1.0 + g * (1.0 - sg))


def _softplus(x):
    return jnp.maximum(x, 0.0) + jnp.log(1.0 + jnp.exp(-jnp.abs(x)))


def _expm1(y):
    series = y * (1.0 + y * (0.5 + y * (1.0 / 6.0 + y * (1.0 / 24.0 + y * (1.0 / 120.0)))))
    return jnp.where(y > -0.05, series, jnp.exp(jnp.minimum(y, -0.05)) - 1.0)


def _mm_call(name, a, b, *, mode, m, n, k, a_blk, b_blk, out_dtype, add, tm, tn, tk):
    nk = k // tk
    assert m % tm == 0 and n % tn == 0 and k % tk == 0, (name, m, n, k, tm, tn, tk)
    dims = {"nn": NN, "nt": NT, "tn": TN}[mode]

    def body(*refs):
        if add is None:
            a_ref, b_ref, o_ref, *scr = refs
            add_ref = None
        else:
            a_ref, b_ref, add_ref, o_ref, *scr = refs
        part = _dot(a_ref[...], b_ref[...], dims)

        def finish(acc):
            if add_ref is not None:
                acc = acc + add_ref[...]
            o_ref[...] = acc.astype(o_ref.dtype)

        if nk == 1:
            finish(part)
        else:
            (acc_ref,) = scr
            kk = pl.program_id(2)

            @pl.when(kk == 0)
            def _():
                acc_ref[...] = part

            @pl.when(kk > 0)
            def _():
                acc_ref[...] += part

            @pl.when(kk == nk - 1)
            def _():
                finish(acc_ref[...])

    in_specs = [a_blk, b_blk]
    args = [a, b]
    if add is not None:
        in_specs.append(pl.BlockSpec((tm, tn), lambda j, i, kk: (i, j)))
        args.append(add)
    return pl.pallas_call(
        body,
        name=name,
        grid=(n // tn, m // tm, nk),
        in_specs=in_specs,
        out_specs=pl.BlockSpec((tm, tn), lambda j, i, kk: (i, j)),
        out_shape=jax.ShapeDtypeStruct((m, n), out_dtype),
        scratch_shapes=[] if nk == 1 else [pltpu.VMEM((tm, tn), F32)],
        compiler_params=_cparams(("parallel", "parallel", "arbitrary")),
    )(*args)


def mm_nn(name, a, b, *, n=None, n_off=0, out_dtype=F32, add=None, tm=512, tn=1024, tk=1024):
    m, k = a.shape
    n = b.shape[1] if n is None else n
    tm, tn, tk = min(tm, m), min(tn, n), min(k, tk)
    ob = n_off // tn
    assert n_off % tn == 0
    return _mm_call(name, a, b, mode="nn", m=m, n=n, k=k, out_dtype=out_dtype, add=add, tm=tm, tn=tn, tk=tk,
                    a_blk=pl.BlockSpec((tm, tk), lambda j, i, kk: (i, kk)),
                    b_blk=pl.BlockSpec((tk, tn), lambda j, i, kk: (kk, j + ob)))


def mm_nt(name, a, b, *, out_dtype=F32, tm=512, tn=1024):
    m, k = a.shape
    n = b.shape[0]
    tn, tk = min(tn, n), min(k, 1024)
    return _mm_call(name, a, b, mode="nt", m=m, n=n, k=k, out_dtype=out_dtype, add=None, tm=tm, tn=tn, tk=tk,
                    a_blk=pl.BlockSpec((tm, tk), lambda j, i, kk: (i, kk)),
                    b_blk=pl.BlockSpec((tn, tk), lambda j, i, kk: (j, kk)))


DW_TILES = dict(tm=1024, tn=512, tk=4096)


def rowwise(name, fn, t, *, tm=256, ncol=1, ins=(), outs=(), touts=(), accs=()):
    n_in, n_out, n_acc = len(ins), len(outs) + len(touts), len(accs)

    def zero_when(ref, cond):
        @pl.when(cond)
        def _():
            ref[...] = jnp.zeros(ref.shape, ref.dtype)

    def body(*refs):
        in_refs, out_refs, acc_refs = refs[:n_in], refs[n_in:n_in + n_out], refs[n_in + n_out:]
        j, i = pl.program_id(0), pl.program_id(1)
        for ref, (_, _, _, cd) in zip(acc_refs, accs):
            zero_when(ref, (i == 0) if cd else ((i == 0) & (j == 0)))
        fn(in_refs, out_refs, acc_refs)

    def in_spec(arr, w, base, cd, rd):
        rows = tm if rd else arr.shape[0]
        return pl.BlockSpec((rows, w), lambda j, i: (i if rd else 0, base + (j if cd else 0)))

    in_specs = [in_spec(*e) for e in ins]
    out_specs = [pl.BlockSpec((tm, w), (lambda j, i, cd=cd: (i, j if cd else 0))) for (_, w, _, cd) in outs]
    out_specs += [pl.BlockSpec((w, tm), (lambda j, i, cd=cd: (j if cd else 0, i))) for (_, w, _, cd) in touts]
    out_specs += [pl.BlockSpec((r, w), (lambda j, i, cd=cd: (0, j if cd else 0))) for (r, _, w, cd) in accs]
    out_shape = [jax.ShapeDtypeStruct((t, c), dt) for (c, _, dt, _) in outs]
    out_shape += [jax.ShapeDtypeStruct((r, t), dt) for (r, _, dt, _) in touts]
    out_shape += [jax.ShapeDtypeStruct((r, c), F32) for (r, c, _, _) in accs]
    res = pl.pallas_call(
        body,
        name=name,
        grid=(ncol, t // tm),
        in_specs=in_specs,
        out_specs=out_specs,
        out_shape=out_shape,
        compiler_params=_cparams(("arbitrary", "arbitrary") if accs else ("parallel", "parallel")),
    )(*[e[0] for e in ins])
    return res


def _row(arr, w=None, base=0, cd=False):
    return (arr, arr.shape[1] if w is None else w, base, cd, True)


def _const(arr, w=None, base=0, cd=False):
    return (arr, arr.shape[1] if w is None else w, base, cd, False)


def rope_tables(positions):
    t = positions.shape[0]

    def fn(ins, outs, _):
        pos = ins[0][...].astype(F32)
        lane = lax.broadcasted_iota(jnp.int32, (1, LANES), 1)
        log_theta = jnp.log(jnp.float32(ROPE_THETA))
        jm = lane - MLA_NOPE
        idx = jnp.clip(jnp.where(jm < 16, jm, jm - 16), 0, 15).astype(F32)
        ang = pos * jnp.exp(-(idx * (2.0 / MLA_ROPE)) * log_theta)
        cos, sin = jnp.cos(ang), jnp.sin(ang)
        in_rope = (lane >= MLA_NOPE) & (lane < MLA_QK)
        outs[0][:, 0:128] = jnp.where(lane < MLA_NOPE, 1.0, jnp.where(in_rope, cos, 0.0))
        outs[0][:, 128:256] = jnp.where(in_rope & (jm < 16), -sin, 0.0)
        outs[0][:, 256:384] = jnp.where(in_rope & (jm >= 16), sin, 0.0)
        jd = lane & (DIL_HD - 1)
        idx = (jd & 31).astype(F32)
        ang = pos * jnp.exp(-(idx * (2.0 / DIL_HD)) * log_theta)
        cos, sin = jnp.cos(ang), jnp.sin(ang)
        outs[0][:, 384:512] = cos
        outs[0][:, 512:640] = jnp.where(jd < 32, -sin, 0.0)
        outs[0][:, 640:768] = jnp.where(jd >= 32, sin, 0.0)

    return rowwise("rope_tables", fn, t, ins=[_row(positions)], outs=[(768, 768, F32, False)])[0]


def rmsnorm_fwd(x, g):
    def fn(ins, outs, _):
        h = rms(ins[0][...], ins[1][...], D_MODEL)
        outs[0][...] = h.astype(MXU_DTYPE)
        outs[1][...] = h.T.astype(MXU_DTYPE)

    return rowwise("rmsnorm_fwd", fn, x.shape[0], ins=[_row(x), _const(g)], outs=[(D_MODEL, D_MODEL, MXU_DTYPE, False)],
                   touts=[(D_MODEL, D_MODEL, MXU_DTYPE, False)])


def rmsnorm_bwd(x, dh, dres, g):
    def fn(ins, outs, accs):
        _, vjp = jax.vjp(lambda xv, gv: rms(xv, gv, D_MODEL), ins[0][...], ins[3][...])
        dx, dg = vjp(ins[1][...])
        outs[0][...] = ins[2][...] + dx
        accs[0][...] += dg

    return rowwise("rmsnorm_bwd", fn, x.shape[0], ins=[_row(x), _row(dh), _row(dres), _const(g)],
                   outs=[(D_MODEL, D_MODEL, F32, False)], accs=[(1, D_MODEL, D_MODEL, False)])


def loss_head(y, target):
    def fn(ins, outs, accs):
        err = ins[0][...] - ins[1][...]
        outs[0][...] = err * (1.0 / D_MODEL)
        accs[0][...] += jnp.sum(err * err, axis=0, keepdims=True)
        accs[1][...] = jnp.broadcast_to(jnp.sum(accs[0][...], keepdims=True), (1, LANES))

    dy, _, tot = rowwise("loss_head", fn, y.shape[0], ins=[_row(y), _row(target)], outs=[(D_MODEL, D_MODEL, F32, False)],
                         accs=[(1, D_MODEL, D_MODEL, False), (1, LANES, LANES, False)])
    return dy, tot[0, 0] * (0.5 / D_MODEL)


def _shift_rows(v, d, fill, reverse):
    tb = v.shape[0]
    if d % 8 == 0:
        pad = jnp.full((d, v.shape[1]), fill, v.dtype)
        return jnp.concatenate([v[d:], pad] if reverse else [pad, v[:tb - d]], axis=0)
    rows = lax.broadcasted_iota(jnp.int32, v.shape, 0)
    if not reverse:
        return jnp.where(rows >= d, pltpu.roll(v, d, 0), fill)
    return jnp.where(rows < tb - d, pltpu.roll(v, tb - d, 0), fill)


def _scan_tile(a, b, reverse):
    d = 1
    while d < a.shape[0]:
        b = b + a * _shift_rows(b, d, 0.0, reverse)
        a = a * _shift_rows(a, d, 1.0, reverse)
        d *= 2
    return a, b


def _lru_gates(ops, xc, wgx, bgx, wga, bga, lam):
    gx = _sigmoid(ops.mm(xc, wgx) + bgx)
    ga = _sigmoid(ops.mm(xc, wga) + bga)
    log_a = -LRU_C * ga * _softplus(-lam)
    a = jnp.exp(log_a)
    mult = jnp.sqrt(-_expm1(2.0 * log_a))
    return a, mult * (gx * xc)


def _shifted_inputs(x, halo, tb):
    rows = lax.broadcasted_iota(jnp.int32, x.shape, 0)
    pad = jnp.zeros((tb - 8, LANES), F32)
    out = []
    for d in (3, 2, 1):
        head = jnp.concatenate([pltpu.roll(halo, d, 0), pad], axis=0)
        out.append(jnp.where(rows >= d, pltpu.roll(x, d, 0), head))
    return out + [x]


def _lru_specs(nt, tb, reverse):
    hb = tb // 8
    tt = (lambda t: nt - 1 - t) if reverse else (lambda t: t)
    blk = lambda off: pl.BlockSpec((tb, LANES), lambda n, t: (tt(t), n + off))
    halo = lambda off: pl.BlockSpec((8, LANES), lambda n, t: (jnp.maximum(tt(t) * hb - 1, 0), n + off))
    chan = lambda r: pl.BlockSpec((r, LANES), lambda n, t: (0, n))
    wblk = pl.BlockSpec((None, LANES, LANES), lambda n, t: (n, 0, 0))
    bblk = pl.BlockSpec((None, 1, LANES), lambda n, t: (n, 0, 0))
    return blk, halo, chan, wblk, bblk


def lru_fwd(z, conv_w, conv_b, wgx, bgx, wga, bga, lam, *, tb=256):
    t = z.shape[0]
    nt = t // tb
    blk, halo, chan, wblk, bblk = _lru_specs(nt, tb, False)

    def body(x_ref, xh_ref, g_ref, cw_ref, cb_ref, wgx_ref, bgx_ref, wga_ref, bga_ref, lam_ref, h_ref, y_ref, yt_ref, carry_ref):
        ti = pl.program_id(1)

        @pl.when(ti == 0)
        def _():
            carry_ref[...] = jnp.zeros((8, LANES), F32)

        x = x_ref[...]
        hal = jnp.where(ti > 0, xh_ref[...], 0.0)
        xs = _shifted_inputs(x, hal, tb)
        xc = cb_ref[...] + sum(xs[k] * cw_ref[k:k + 1, :] for k in range(4))
        a, b = _lru_gates(PLAIN, xc, wgx_ref[...], bgx_ref[...], wga_ref[...], bga_ref[...], lam_ref[...])
        acum, h0 = _scan_tile(a, b, False)
        h = h0 + acum * carry_ref[7:8, :]
        carry_ref[...] = h[tb - 8:tb, :]
        h_ref[...] = h
        y = h * _silu_and_grad(g_ref[...])[0]
        y_ref[...] = y.astype(MXU_DTYPE)
        yt_ref[...] = y.T.astype(MXU_DTYPE)

    return pl.pallas_call(
        body,
        name="lru_fwd",
        grid=(LRU_BLOCKS, nt),
        in_specs=[blk(0), halo(0), blk(LRU_BLOCKS), chan(4), chan(1), wblk, bblk, wblk, bblk, chan(1)],
        out_specs=[blk(0), blk(0), pl.BlockSpec((LANES, tb), lambda n, t_: (n, t_))],
        out_shape=[jax.ShapeDtypeStruct((t, 1024), F32), jax.ShapeDtypeStruct((t, 1024), MXU_DTYPE),
                   jax.ShapeDtypeStruct((1024, t), MXU_DTYPE)],
        scratch_shapes=[pltpu.VMEM((8, LANES), F32)],
        compiler_params=_cparams(("parallel", "arbitrary")),
    )(z, z, z, conv_w, conv_b, wgx, bgx, wga, bga, lam)


def lru_bwd(z, h, dy, conv_w, conv_b, wgx, bgx, wga, bga, lam, *, tb=256):
    t = z.shape[0]
    nt = t // tb
    blk, halo, chan, wblk, bblk = _lru_specs(nt, tb, True)

    def body(x_ref, xh_ref, g_ref, h_ref, hh_ref, dy_ref, cw_ref, cb_ref, wgx_ref, bgx_ref, wga_ref, bga_ref, lam_ref,
             dzx_ref, dzg_ref, dwgx_ref, dbgx_ref, dwga_ref, dbga_ref, dlam_ref, dcw_ref, dcb_ref,
             gcar_ref, acar_ref, xcar_ref):
        ti = pl.program_id(1)
        has_earlier = ti < nt - 1

        @pl.when(ti == 0)
        def _():
            for ref in (dwgx_ref, dbgx_ref, dwga_ref, dbga_ref, dlam_ref, dcw_ref, dcb_ref, gcar_ref, acar_ref, xcar_ref):
                ref[...] = jnp.zeros(ref.shape, F32)

        rows = lax.broadcasted_iota(jnp.int32, (tb, LANES), 0)
        x = x_ref[...]
        hal = jnp.where(has_earlier, xh_ref[...], 0.0)
        xs = _shifted_inputs(x, hal, tb)
        xc = cb_ref[...] + sum(xs[k] * cw_ref[k:k + 1, :] for k in range(4))
        (a, _), vjp = jax.vjp(functools.partial(_lru_gates, DIFF), xc, wgx_ref[...], bgx_ref[...], wga_ref[...],
                              bga_ref[...], lam_ref[...])
        g, h, dyv = g_ref[...], h_ref[...], dy_ref[...]
        silu, dsilu = _silu_and_grad(g)
        dzg_ref[...] = (dyv * h * dsilu).astype(MXU_DTYPE)
        a_next = jnp.where(rows < tb - 1, pltpu.roll(a, tb - 1, 0), acar_ref[0:1, :])
        acum, g0 = _scan_tile(a_next, dyv * silu, True)
        gt = g0 + acum * gcar_ref[0:1, :]
        h_prev = jnp.where(rows >= 1, pltpu.roll(h, 1, 0), jnp.where(has_earlier, hh_ref[7:8, :], 0.0))
        dxc, dwgx, dbgx, dwga, dbga, dlam = vjp((gt * h_prev, gt))
        later = xcar_ref[...]
        gcar_ref[...] = gt[0:8, :]
        acar_ref[...] = a[0:8, :]
        xcar_ref[...] = dxc[0:8, :]
        dx = dxc * cw_ref[3:4, :]
        pad = jnp.zeros((tb - 8, LANES), F32)
        for d in (1, 2, 3):
            tail = jnp.concatenate([pad, pltpu.roll(later, 8 - d, 0)], axis=0)
            up = jnp.where(rows < tb - d, pltpu.roll(dxc, tb - d, 0), tail)
            dx = dx + up * cw_ref[3 - d:4 - d, :]
        dzx_ref[...] = dx.astype(MXU_DTYPE)
        for k in range(4):
            dcw_ref[k:k + 1, :] += jnp.sum(dxc * xs[k], axis=0, keepdims=True)
        dcb_ref[...] += jnp.sum(dxc, axis=0, keepdims=True)
        dwgx_ref[...] += dwgx
        dbgx_ref[...] += dbgx
        dwga_ref[...] += dwga
        dbga_ref[...] += dbga
        dlam_ref[...] += dlam

    sds = jax.ShapeDtypeStruct
    return pl.pallas_call(
        body,
        name="lru_bwd",
        grid=(LRU_BLOCKS, nt),
        in_specs=[blk(0), halo(0), blk(LRU_BLOCKS), blk(0), halo(0), blk(0), chan(4), chan(1), wblk, bblk, wblk, bblk, chan(1)],
        out_specs=[blk(0), blk(0), wblk, bblk, wblk, bblk, chan(1), chan(4), chan(1)],
        out_shape=[sds((t, 1024), MXU_DTYPE), sds((t, 1024), MXU_DTYPE), sds(wgx.shape, F32), sds(bgx.shape, F32), sds(wga.shape, F32),
                   sds(bga.shape, F32), sds((1, 1024), F32), sds((4, 1024), F32), sds((1, 1024), F32)],
        scratch_shapes=[pltpu.VMEM((8, LANES), F32)] * 3,
        compiler_params=_cparams(("parallel", "arbitrary")),
    )(z, z, z, h, h, dy, conv_w, conv_b, wgx, bgx, wga, bga, lam)


def _mla_prep_tile(ops, cq, ckv, krp, c, s1, s2, g_cq, g_ckv, wq, wk, wv, gq, gk):
    cqn = rms(cq, g_cq, 256)
    ckvn = rms(ckv, g_ckv, 128)
    v = ops.mm(ckvn, wv)
    qs, ks = [], []
    for hd in range(MLA_HEADS):
        q = rms(ops.mm(cqn, wq[hd]), gq, MLA_QK)
        k = rms(ops.mm(ckvn, wk[hd]) + krp, gk, MLA_QK)
        qs.append(rope(ops, q, c, s1, s2, 16) * (MLA_QK ** -0.5))
        ks.append(rope(ops, k, c, s1, s2, 16))
    return tuple(qs), tuple(ks), v


def _mla_prep_args(ins):
    z_cq, z_ckv, z_kr, tc, ts1, ts2, g_cq, g_ckv, wq, wk, wv, gq, gk = ins[:13]
    heads = lambda w: tuple(w[:, LANES * hd:LANES * (hd + 1)] for hd in range(MLA_HEADS))
    return (z_cq[...], z_ckv[...], z_kr[...], tc[...], ts1[...], ts2[...], g_cq[...], g_ckv[...], heads(wq), heads(wk),
            wv[...], gq[...], gk[...])


def _mla_prep_ins(z, tabs, w):
    return [_row(z, 256, 0), _row(z, 128, 2), _row(z, 128, 3), _row(tabs, 128, 0), _row(tabs, 128, 1), _row(tabs, 128, 2),
            _const(w["cq_norm_g"]), _const(w["ckv_norm_g"]), _const(w["wq"]), _const(w["wk"]), _const(w["wv"]),
            _const(w["gq"]), _const(w["gk"])]


def mla_prep_fwd(z, tabs, w):
    def fn(ins, outs, _):
        qs, ks, v = _mla_prep_tile(PLAIN, *_mla_prep_args(ins))
        for hd in range(MLA_HEADS):
            outs[0][:, LANES * hd:LANES * (hd + 1)] = qs[hd].astype(MXU_DTYPE)
            outs[1][:, LANES * hd:LANES * (hd + 1)] = ks[hd].astype(MXU_DTYPE)
        outs[2][...] = v.astype(MXU_DTYPE)
        outs[3][...] = v.T.astype(MXU_DTYPE)

    return rowwise("mla_prep_fwd", fn, z.shape[0], ins=_mla_prep_ins(z, tabs, w),
                   outs=[(1024, 1024, MXU_DTYPE, False), (1024, 1024, MXU_DTYPE, False), (512, 512, MXU_DTYPE, False)],
                   touts=[(512, 512, MXU_DTYPE, False)])


def mla_prep_bwd(z, tabs, w, dq, dk, dv):
    def fn(ins, outs, accs):
        args = _mla_prep_args(ins)
        _, vjp = jax.vjp(functools.partial(_mla_prep_tile, DIFF), *args)
        heads = lambda ref: tuple(ref[:, LANES * hd:LANES * (hd + 1)] for hd in range(MLA_HEADS))
        dcq, dckv, dkr, _, _, _, dg_cq, dg_ckv, dwq, dwk, dwv, dgq, dgk = vjp((heads(ins[13]), heads(ins[14]), ins[15][...]))
        lane = lax.broadcasted_iota(jnp.int32, (1, LANES), 1)
        outs[0][:, 0:256] = dcq.astype(MXU_DTYPE)
        outs[0][:, 256:384] = dckv.astype(MXU_DTYPE)
        outs[0][:, 384:512] = jnp.where((lane >= MLA_NOPE) & (lane < MLA_QK), dkr, 0.0).astype(MXU_DTYPE)
        accs[0][...] += dg_cq
        accs[1][...] += dg_ckv
        for hd in range(MLA_HEADS):
            accs[2][:, LANES * hd:LANES * (hd + 1)] += dwq[hd]
            accs[3][:, LANES * hd:LANES * (hd + 1)] += dwk[hd]
        accs[4][...] += dwv
        accs[5][...] += dgq
        accs[6][...] += dgk

    return rowwise("mla_prep_bwd", fn, z.shape[0], ins=_mla_prep_ins(z, tabs, w) + [_row(dq), _row(dk), _row(dv)],
                   outs=[(512, 512, MXU_DTYPE, False)],
                   accs=[(1, 256, 256, False), (1, 128, 128, False), (256, 1024, 1024, False), (128, 1024, 1024, False),
                         (128, 512, 512, False), (1, 128, 128, False), (1, 128, 128, False)])


def _head_masks():
    lane = lax.broadcasted_iota(jnp.int32, (1, LANES), 1)
    return (lane < DIL_HD, lane >= DIL_HD)


def _row_scalar(tile, mask):
    return jnp.max(jnp.where(mask, tile, -jnp.inf), axis=-1, keepdims=True)


def _causal_tiles(nq, by_key):
    pairs = [(i, j) for i in range(nq) for j in range(i + 1)]
    if by_key:
        pairs.sort(key=lambda ij: (ij[1], ij[0]))
    return (jnp.asarray([ij[0] for ij in pairs], jnp.int32), jnp.asarray([ij[1] for ij in pairs], jnp.int32))


def mla_attn_fwd(q, k, vt, z, *, tq=256):
    t = q.shape[0]
    nq = t // tq
    it, jt = _causal_tiles(nq, False)
    hps, wq, wv = MLA_HPS, LANES * MLA_HPS, 64 * MLA_HPS

    def body(it_ref, jt_ref, q_ref, k_ref, vt_ref, g_ref, o_ref, y_ref, yt_ref, lse_ref, m_scr, l_scr, acc_scr):
        step = pl.program_id(1)
        i, j = it_ref[step], jt_ref[step]

        @pl.when(j == 0)
        def _():
            m_scr[...] = jnp.full(m_scr.shape, NEG, F32)
            l_scr[...] = jnp.zeros(l_scr.shape, F32)
            acc_scr[...] = jnp.zeros(acc_scr.shape, F32)

        def update(diagonal):
            heads = range(hps)
            lanes = [slice(LANES * hh, LANES * (hh + 1)) for hh in heads]
            rows = [slice(64 * hh, 64 * (hh + 1)) for hh in heads]
            sts = [_dot(k_ref[:, lanes[hh]], q_ref[:, lanes[hh]], NT) for hh in heads]
            m_prev = [m_scr[hh:hh + 1, :] for hh in heads]
            l_prev = [l_scr[hh:hh + 1, :] for hh in heads]
            acc_prev = [acc_scr[rows[hh], :] for hh in heads]
            m_new, l_new, acc_new = [], [], []
            for hh in heads:
                st = sts[hh]
                if diagonal:
                    key = lax.broadcasted_iota(jnp.int32, (tq, tq), 0)
                    qry = lax.broadcasted_iota(jnp.int32, (tq, tq), 1)
                    st = jnp.where(key <= qry, st, NEG)
                m_new.append(jnp.maximum(m_prev[hh], jnp.max(st, axis=0, keepdims=True)))
                alpha = jnp.exp(m_prev[hh] - m_new[hh])
                pt = jnp.exp(st - m_new[hh])
                l_new.append(alpha * l_prev[hh] + jnp.sum(pt, axis=0, keepdims=True))
                acc_new.append(alpha * acc_prev[hh] + _dot(vt_ref[rows[hh], :], pt, NN))
            for hh in heads:
                m_scr[hh:hh + 1, :] = m_new[hh]
                l_scr[hh:hh + 1, :] = l_new[hh]
                acc_scr[rows[hh], :] = acc_new[hh]

        @pl.when(j < i)
        def _():
            update(False)

        @pl.when(j == i)
        def _():
            update(True)
            lse_ref[...] = jnp.zeros(lse_ref.shape, F32)
            for hh in range(hps):
                rows = slice(64 * hh, 64 * (hh + 1))
                acc_scr[rows, :] = acc_scr[rows, :] / l_scr[hh:hh + 1, :]
                lse_ref[hh:hh + 1, :] = m_scr[hh:hh + 1, :] + jnp.log(l_scr[hh:hh + 1, :])
            o = acc_scr[...].T
            o_ref[...] = o
            y = o * _silu_and_grad(g_ref[...])[0]
            y_ref[...] = y.astype(MXU_DTYPE)
            yt_ref[...] = y.T.astype(MXU_DTYPE)

    qo = lambda w, off=0: pl.BlockSpec((tq, w), lambda p, s, it_, jt_: (it_[s], p + off))
    sds = jax.ShapeDtypeStruct
    return pl.pallas_call(
        body,
        name="mla_attn_fwd",
        grid_spec=pltpu.PrefetchScalarGridSpec(
            num_scalar_prefetch=2,
            grid=(MLA_HEADS // hps, it.shape[0]),
            in_specs=[qo(wq), pl.BlockSpec((tq, wq), lambda p, s, it_, jt_: (jt_[s], p)),
                      pl.BlockSpec((wv, tq), lambda p, s, it_, jt_: (p, jt_[s])), qo(wv, 512 // wv)],
            out_specs=[qo(wv), qo(wv), pl.BlockSpec((wv, tq), lambda p, s, it_, jt_: (p, it_[s])),
                       pl.BlockSpec((None, 8, tq), lambda p, s, it_, jt_: (p, 0, it_[s]))],
            scratch_shapes=[pltpu.VMEM((8, tq), F32), pltpu.VMEM((8, tq), F32), pltpu.VMEM((wv, tq), F32)],
        ),
        out_shape=[sds((t, 512), F32), sds((t, 512), MXU_DTYPE), sds((512, t), MXU_DTYPE), sds((MLA_HEADS // hps, 8, t), F32)],
        compiler_params=_cparams(("parallel", "arbitrary")),
    )(it, jt, q, k, vt, z)


def mla_attn_bwd(q, k, v, do, lse, dd, *, tq=256):
    t = q.shape[0]
    nq = t // tq
    it, jt = _causal_tiles(nq, True)
    hps, wq, wv = MLA_HPS, LANES * MLA_HPS, 64 * MLA_HPS

    def body(it_ref, jt_ref, q_ref, k_ref, v_ref, do_ref, lse_ref, d_ref, dq_ref, dk_ref, dv_ref, dk_scr, dv_scr):
        step = pl.program_id(1)
        i, j = it_ref[step], jt_ref[step]
        masks = _head_masks()

        @pl.when(step == 0)
        def _():
            dq_ref[...] = jnp.zeros(dq_ref.shape, F32)

        @pl.when(i == j)
        def _():
            dk_scr[...] = jnp.zeros(dk_scr.shape, F32)
            dv_scr[...] = jnp.zeros(dv_scr.shape, F32)

        def update(diagonal):
            qrows = pl.ds(pl.multiple_of(i * tq, tq), tq)
            heads = range(hps)
            lanes = [slice(LANES * hh, LANES * (hh + 1)) for hh in heads]
            pair = [slice(LANES * (hh // 2), LANES * (hh // 2 + 1)) for hh in heads]
            qh = [q_ref[:, lanes[hh]] for hh in heads]
            kh = [k_ref[:, lanes[hh]] for hh in heads]
            doh = []
            for hh in heads:
                dov = do_ref[:, pair[hh]]
                doh.append(jnp.where(masks[hh % 2], dov, jnp.zeros_like(dov)))
            sts = [_dot(kh[hh], qh[hh], NT) for hh in heads]
            dps = [_dot(v_ref[:, pair[hh]], doh[hh], NT) for hh in heads]
            lse = [lse_ref[hh:hh + 1, :] for hh in heads]
            ddv = [d_ref[hh:hh + 1, :] for hh in heads]
            dk_new = [dk_scr[:, lanes[hh]] for hh in heads]
            dq_new = [dq_ref[qrows, lanes[hh]] for hh in heads]
            dv_new = [dv_scr[:, pair[2 * pp]] for pp in range(hps // 2)]
            for hh in heads:
                st = sts[hh] - lse[hh]
                if diagonal:
                    key = lax.broadcasted_iota(jnp.int32, (tq, tq), 0)
                    qry = lax.broadcasted_iota(jnp.int32, (tq, tq), 1)
                    st = jnp.where(key <= qry, st, NEG)
                pt = jnp.exp(st)
                dst = pt * (dps[hh] - ddv[hh])
                dv_new[hh // 2] = dv_new[hh // 2] + _dot(pt, doh[hh], NN)
                dk_new[hh] = dk_new[hh] + _dot(dst, qh[hh], NN)
                dq_new[hh] = dq_new[hh] + _dot(dst, kh[hh], TN)
            for hh in heads:
                dk_scr[:, lanes[hh]] = dk_new[hh]
                dq_ref[qrows, lanes[hh]] = dq_new[hh]
            for pp in range(hps // 2):
                dv_scr[:, pair[2 * pp]] = dv_new[pp]

        @pl.when(j < i)
        def _():
            update(False)

        @pl.when(j == i)
        def _():
            update(True)

        @pl.when(i == nq - 1)
        def _():
            dk_ref[...] = dk_scr[...]
            dv_ref[...] = dv_scr[...]

    qo = lambda w: pl.BlockSpec((tq, w), lambda p, s, it_, jt_: (it_[s], p))
    kv = lambda w: pl.BlockSpec((tq, w), lambda p, s, it_, jt_: (jt_[s], p))
    stat = pl.BlockSpec((None, 8, tq), lambda p, s, it_, jt_: (p, 0, it_[s]))
    sds = jax.ShapeDtypeStruct
    return pl.pallas_call(
        body,
        name="mla_attn_bwd",
        grid_spec=pltpu.PrefetchScalarGridSpec(
            num_scalar_prefetch=2,
            grid=(MLA_HEADS // hps, it.shape[0]),
            in_specs=[qo(wq), kv(wq), kv(wv), qo(wv), stat, stat],
            out_specs=[pl.BlockSpec((t, wq), lambda p, s, it_, jt_: (0, p)), kv(wq), kv(wv)],
            scratch_shapes=[pltpu.VMEM((tq, wq), F32), pltpu.VMEM((tq, wv), F32)],
        ),
        out_shape=[sds((t, 1024), F32), sds((t, 1024), F32), sds((t, 512), F32)],
        compiler_params=_cparams(("parallel", "arbitrary")),
    )(it, jt, q, k, v, do, lse, dd)


def mla_gate_bwd(dy, o, z, *, tm=256):
    t = dy.shape[0]
    wv = 64 * MLA_HPS

    def body(dy_ref, o_ref, g_ref, do_ref, dzg_ref, dd_ref):
        dyv, ov = dy_ref[...], o_ref[...]
        silu, dsilu = _silu_and_grad(g_ref[...])
        do = dyv * silu
        do_ref[...] = do.astype(MXU_DTYPE)
        dzg_ref[...] = (dyv * ov * dsilu).astype(MXU_DTYPE)
        prod = do * ov
        row = lax.broadcasted_iota(jnp.int32, (8, wv), 0)
        lane = lax.broadcasted_iota(jnp.int32, (8, wv), 1)
        pick = ((lane >= row * DIL_HD) & (lane < (row + 1) * DIL_HD)).astype(BF16)
        hi = prod.astype(BF16)
        r1 = prod - hi.astype(F32)
        mid = r1.astype(BF16)
        lo = (r1 - mid.astype(F32)).astype(BF16)
        dot = lambda u: lax.dot_general(pick, u, NT, preferred_element_type=F32)
        dd_ref[...] = dot(hi) + dot(mid) + dot(lo)

    blk = lambda off=0: pl.BlockSpec((tm, wv), lambda p, i: (i, p + off))
    sds = jax.ShapeDtypeStruct
    return pl.pallas_call(
        body,
        name="mla_gate_bwd",
        grid=(MLA_HEADS // MLA_HPS, t // tm),
        in_specs=[blk(), blk(), blk(512 // wv)],
        out_specs=[blk(), blk(), pl.BlockSpec((None, 8, tm), lambda p, i: (p, 0, i))],
        out_shape=[sds((t, 512), MXU_DTYPE), sds((t, 512), MXU_DTYPE), sds((MLA_HEADS // MLA_HPS, 8, t), F32)],
        compiler_params=_cparams(("parallel", "parallel")),
    )(dy, o, z)


SPAN = 2048
DIL_SCALE = DIL_HD ** -0.5


def _rm_src(u, window, dil):
    nn, r = divmod(u, dil)
    return pl.ds(nn * window + r, DIL_NK, stride=dil) if dil > 1 else pl.ds(u * DIL_NK, DIL_NK)


def _rm_dst(u):
    return pl.ds(u * DIL_NK, DIL_NK)


def _dil_prep_tile(ops, x, g2, c, s1, s2, scale):
    ms = ops.seg_sum(x * x) * (1.0 / DIL_HD)
    return rope(ops, x * lax.rsqrt(ms + EPS) * g2, c, s1, s2, 32) * scale


def _span_blk(base):
    return pl.BlockSpec((SPAN, LANES), lambda s, p: (s, base + p))


def _const_blk(shape):
    return pl.BlockSpec(shape, lambda s, p: (0,) * len(shape))


_DIL_TABLE_SPECS = [pl.BlockSpec((SPAN, LANES), (lambda s, p, blk=blk: (s, blk))) for blk in (3, 4, 5)]


def dil_prep_fwd(z, tabs, gq2, gk2, gi):
    window, dil = DIL_GROUPS[gi]
    t = z.shape[0]

    def body(q_ref, k_ref, v_ref, c_ref, s1_ref, s2_ref, gq_ref, gk_ref, qo_ref, ko_ref, vo_ref):
        for u in range(SPAN // DIL_NK):
            src, dst = _rm_src(u, window, dil), _rm_dst(u)
            c, s1, s2 = c_ref[src, :], s1_ref[src, :], s2_ref[src, :]
            qo_ref[dst, :] = _dil_prep_tile(PLAIN, q_ref[src, :], gq_ref[...], c, s1, s2, DIL_SCALE).astype(MXU_DTYPE)
            ko_ref[dst, :] = _dil_prep_tile(PLAIN, k_ref[src, :], gk_ref[...], c, s1, s2, 1.0).astype(MXU_DTYPE)
            vo_ref[dst, :] = v_ref[src, :].astype(MXU_DTYPE)

    return pl.pallas_call(
        body,
        name=f"dil_prep_fwd_{dil}",
        grid=(t // SPAN, 4),
        in_specs=[_span_blk(4 * gi), _span_blk(12 + 4 * gi), _span_blk(24 + 4 * gi)] + _DIL_TABLE_SPECS
        + [_const_blk((1, LANES)), _const_blk((1, LANES))],
        out_specs=[_span_blk(0)] * 3,
        out_shape=[jax.ShapeDtypeStruct((t, 512), MXU_DTYPE)] * 3,
        compiler_params=_cparams(("parallel", "parallel")),
    )(z, z, z, tabs, tabs, tabs, gq2, gk2)


def dil_prep_bwd(z, tabs, gq2, gk2, dq, dk, dv, gi):
    window, dil = DIL_GROUPS[gi]
    t = z.shape[0]

    def body(q_ref, k_ref, c_ref, s1_ref, s2_ref, gq_ref, gk_ref, dq_ref, dk_ref, dv_ref, dzq_ref, dzk_ref, dzv_ref, dgq_ref,
             dgk_ref, sq, sk, sv):
        @pl.when((pl.program_id(0) == 0) & (pl.program_id(1) == 0))
        def _():
            dgq_ref[...] = jnp.zeros((1, LANES), F32)
            dgk_ref[...] = jnp.zeros((1, LANES), F32)

        dgs = [jnp.zeros((1, LANES), F32), jnp.zeros((1, LANES), F32)]
        for u in range(SPAN // DIL_NK):
            src, dst = _rm_src(u, window, dil), _rm_dst(u)
            c, s1, s2 = c_ref[src, :], s1_ref[src, :], s2_ref[src, :]
            for idx, (x_ref, g_ref, ct_ref, scr, scale) in enumerate(((q_ref, gq_ref, dq_ref, sq, DIL_SCALE),
                                                                      (k_ref, gk_ref, dk_ref, sk, 1.0))):
                _, vjp = jax.vjp(lambda xv, gv, sc=scale: _dil_prep_tile(DIFF, xv, gv, c, s1, s2, sc), x_ref[src, :], g_ref[...])
                dx, dg = vjp(ct_ref[dst, :])
                scr[src, :] = dx
                dgs[idx] = dgs[idx] + dg
            sv[src, :] = dv_ref[dst, :]
        dgq_ref[...] += dgs[0] + pltpu.roll(dgs[0], DIL_HD, 1)
        dgk_ref[...] += dgs[1] + pltpu.roll(dgs[1], DIL_HD, 1)
        for c0 in range(0, SPAN, 256):
            rows = slice(c0, c0 + 256)
            dzq_ref[rows, :] = sq[rows, :].astype(MXU_DTYPE)
            dzk_ref[rows, :] = sk[rows, :].astype(MXU_DTYPE)
            dzv_ref[rows, :] = sv[rows, :].astype(MXU_DTYPE)

    sds = jax.ShapeDtypeStruct
    return pl.pallas_call(
        body,
        name=f"dil_prep_bwd_{dil}",
        grid=(t // SPAN, 4),
        in_specs=[_span_blk(4 * gi), _span_blk(12 + 4 * gi)] + _DIL_TABLE_SPECS
        + [_const_blk((1, LANES)), _const_blk((1, LANES)), _span_blk(0), _span_blk(0), _span_blk(0)],
        out_specs=[_span_blk(0)] * 3 + [_const_blk((1, LANES))] * 2,
        out_shape=[sds((t, 512), MXU_DTYPE)] * 3 + [sds((1, LANES), F32)] * 2,
        scratch_shapes=[pltpu.VMEM((SPAN, LANES), F32)] * 3,
        compiler_params=_cparams(("arbitrary", "arbitrary")),
    )(z, z, tabs, tabs, tabs, gq2, gk2, dq, dk, dv)


def _band_masks():
    qi = lax.broadcasted_iota(jnp.int32, (DIL_NK, DIL_NK), 0)
    ki = lax.broadcasted_iota(jnp.int32, (DIL_NK, DIL_NK), 1)
    return (ki >= qi), (ki <= qi)


def _pair_heads():
    return [(pair, hh) for pair in range(4) for hh in range(2)]


def _pair_lanes(pair):
    return slice(LANES * pair, LANES * (pair + 1))


def _zero_other_head(mask, x):
    return jnp.where(mask, x, jnp.zeros_like(x))


def dil_attn_fwd(name, q, k, v, dil):
    t = q.shape[0]

    def body(q_ref, kp_ref, kc_ref, vp_ref, vc_ref, o_ref, lse_ref):
        b = pl.program_id(0)
        mprev, mcur = _band_masks()
        mprev = mprev & (b >= dil)
        hm = _head_masks()
        heads = _pair_heads()
        qh = [_zero_other_head(hm[hh], q_ref[:, _pair_lanes(pair)]) for pair, hh in heads]
        sps = [_dot(qh[i], kp_ref[:, _pair_lanes(pair)], NT) for i, (pair, _) in enumerate(heads)]
        scs = [_dot(qh[i], kc_ref[:, _pair_lanes(pair)], NT) for i, (pair, _) in enumerate(heads)]
        o = [jnp.zeros((DIL_NK, LANES), F32) for _ in range(4)]
        lse = [jnp.zeros((DIL_NK, LANES), F32) for _ in range(4)]
        for i, (pair, hh) in enumerate(heads):
            sp, sc = jnp.where(mprev, sps[i], NEG), jnp.where(mcur, scs[i], NEG)
            m = jnp.maximum(jnp.max(sp, axis=1, keepdims=True), jnp.max(sc, axis=1, keepdims=True))
            ep, ec = jnp.exp(sp - m), jnp.exp(sc - m)
            den = jnp.sum(ep, axis=1, keepdims=True) + jnp.sum(ec, axis=1, keepdims=True)
            lanes = _pair_lanes(pair)
            oh = _dot(ep, _zero_other_head(hm[hh], vp_ref[:, lanes]), NN) + _dot(ec, _zero_other_head(hm[hh], vc_ref[:, lanes]), NN)
            o[pair] = o[pair] + oh / den
            lse[pair] = jnp.where(hm[hh], m + jnp.log(den), lse[pair])
        for pair in range(4):
            o_ref[:, _pair_lanes(pair)] = o[pair]
            lse_ref[:, _pair_lanes(pair)] = lse[pair]

    cur = pl.BlockSpec((DIL_NK, 512), lambda b: (b, 0))
    prev = pl.BlockSpec((DIL_NK, 512), lambda b: (jnp.maximum(b - dil, 0), 0))
    sds = jax.ShapeDtypeStruct
    return pl.pallas_call(
        body,
        name=name,
        grid=(t // DIL_NK,),
        in_specs=[cur, prev, cur, prev, cur],
        out_specs=[cur, cur],
        out_shape=[sds((t, 512), F32), sds((t, 512), F32)],
        compiler_params=_cparams(("parallel",)),
    )(q, k, k, v, v)


def dil_attn_bwd(name, q, k, v, do, lse, dd, dil):
    t = q.shape[0]
    nblk = t // DIL_NK

    def body(qc_ref, qn_ref, doc_ref, don_ref, lc_ref, ln_ref, dc_ref, dn_ref, kp_ref, kc_ref, vp_ref, vc_ref,
             dq_ref, dk_ref, dv_ref):
        b = pl.program_id(0)
        mprev, mcur = _band_masks()
        mnext = mprev & (b + dil < nblk)
        mprev = mprev & (b >= dil)
        hm = _head_masks()
        heads = _pair_heads()
        ln_ = [_pair_lanes(pair) for pair, _ in heads]
        qh = [_zero_other_head(hm[hh], qc_ref[:, ln_[i]]) for i, (_, hh) in enumerate(heads)]
        qnh = [_zero_other_head(hm[hh], qn_ref[:, ln_[i]]) for i, (_, hh) in enumerate(heads)]
        doh = [_zero_other_head(hm[hh], doc_ref[:, ln_[i]]) for i, (_, hh) in enumerate(heads)]
        donh = [_zero_other_head(hm[hh], don_ref[:, ln_[i]]) for i, (_, hh) in enumerate(heads)]
        idx = range(len(heads))
        s_p = [_dot(qh[i], kp_ref[:, ln_[i]], NT) for i in idx]
        s_c = [_dot(qh[i], kc_ref[:, ln_[i]], NT) for i in idx]
        s_n = [_dot(qnh[i], kc_ref[:, ln_[i]], NT) for i in idx]
        dp_p = [_dot(doh[i], vp_ref[:, ln_[i]], NT) for i in idx]
        dp_c = [_dot(doh[i], vc_ref[:, ln_[i]], NT) for i in idx]
        dp_n = [_dot(donh[i], vc_ref[:, ln_[i]], NT) for i in idx]
        dq = [jnp.zeros((DIL_NK, LANES), F32) for _ in range(4)]
        dk = [jnp.zeros((DIL_NK, LANES), F32) for _ in range(4)]
        dv = [jnp.zeros((DIL_NK, LANES), F32) for _ in range(4)]
        for i, (pair, hh) in enumerate(heads):
            lse_h, d_h = _row_scalar(lc_ref[:, ln_[i]], hm[hh]), _row_scalar(dc_ref[:, ln_[i]], hm[hh])
            lse_n, d_n = _row_scalar(ln_ref[:, ln_[i]], hm[hh]), _row_scalar(dn_ref[:, ln_[i]], hm[hh])
            pp = jnp.exp(jnp.where(mprev, s_p[i] - lse_h, NEG))
            pc = jnp.exp(jnp.where(mcur, s_c[i] - lse_h, NEG))
            pn = jnp.exp(jnp.where(mnext, s_n[i] - lse_n, NEG))
            dsp, dsc, dsn = pp * (dp_p[i] - d_h), pc * (dp_c[i] - d_h), pn * (dp_n[i] - d_n)
            dq[pair] = (dq[pair] + _dot(dsp, _zero_other_head(hm[hh], kp_ref[:, ln_[i]]), NN)
                        + _dot(dsc, _zero_other_head(hm[hh], kc_ref[:, ln_[i]]), NN))
            dv[pair] = dv[pair] + _dot(pc, doh[i], TN) + _dot(pn, donh[i], TN)
            dk[pair] = dk[pair] + _dot(dsc, qh[i], TN) + _dot(dsn, qnh[i], TN)
        for pair in range(4):
            dq_ref[:, _pair_lanes(pair)] = dq[pair]
            dk_ref[:, _pair_lanes(pair)] = dk[pair]
            dv_ref[:, _pair_lanes(pair)] = dv[pair]

    cur = pl.BlockSpec((DIL_NK, 512), lambda b: (b, 0))
    prev = pl.BlockSpec((DIL_NK, 512), lambda b: (jnp.maximum(b - dil, 0), 0))
    nxt = pl.BlockSpec((DIL_NK, 512), lambda b: (jnp.minimum(b + dil, nblk - 1), 0))
    sds = jax.ShapeDtypeStruct
    return pl.pallas_call(
        body,
        name=name,
        grid=(nblk,),
        in_specs=[cur, nxt, cur, nxt, cur, nxt, cur, nxt, prev, cur, prev, cur],
        out_specs=[cur, cur, cur],
        out_shape=[sds((t, 512), F32)] * 3,
        compiler_params=_cparams(("parallel",)),
    )(q, q, do, do, lse, lse, dd, dd, k, k, v, v)


def dil_combine(os_, lses, z):
    t = z.shape[0]

    def body(o1_ref, l1_ref, o2_ref, l2_ref, o3_ref, l3_ref, g_ref, y_ref, o_ref, lse_ref, yt_ref, so2, sl2, so3, sl3):
        for (window, dil), o_in, l_in, so, sl in ((DIL_GROUPS[1], o2_ref, l2_ref, so2, sl2), (DIL_GROUPS[2], o3_ref, l3_ref, so3, sl3)):
            for u in range(SPAN // DIL_NK):
                src, dst = _rm_src(u, window, dil), _rm_dst(u)
                so[src, :] = o_in[dst, :]
                sl[src, :] = l_in[dst, :]
        for c0 in range(0, SPAN, 256):
            rows = slice(c0, c0 + 256)
            la, lb, lc = l1_ref[rows, :], sl2[rows, :], sl3[rows, :]
            mx = jnp.maximum(jnp.maximum(la, lb), lc)
            wa, wb, wc = jnp.exp(la - mx), jnp.exp(lb - mx), jnp.exp(lc - mx)
            tot = wa + wb + wc
            o = (wa * o1_ref[rows, :] + wb * so2[rows, :] + wc * so3[rows, :]) / tot
            y = o * _silu_and_grad(g_ref[rows, :])[0]
            y_ref[rows, :] = y.astype(MXU_DTYPE)
            o_ref[rows, :] = o
            lse_ref[rows, :] = mx + jnp.log(tot)
            yt_ref[:, rows] = y.T.astype(MXU_DTYPE)

    sds = jax.ShapeDtypeStruct
    return pl.pallas_call(
        body,
        name="dil_combine",
        grid=(t // SPAN, 4),
        in_specs=[_span_blk(0)] * 6 + [_span_blk(36)],
        out_specs=[_span_blk(0)] * 3 + [pl.BlockSpec((LANES, SPAN), lambda s, p: (p, s))],
        out_shape=[sds((t, 512), MXU_DTYPE), sds((t, 512), F32), sds((t, 512), F32), sds((512, t), MXU_DTYPE)],
        scratch_shapes=[pltpu.VMEM((SPAN, LANES), F32)] * 4,
        compiler_params=_cparams(("parallel", "parallel")),
    )(os_[0], lses[0], os_[1], lses[1], os_[2], lses[2], z)


def dil_gate_bwd(dy, o, z, lse):
    t = dy.shape[0]

    def body(dy_ref, o_ref, g_ref, lse_ref, dzg_ref, do1_ref, dd1_ref, do2_ref, dd2_ref, l2_ref, do3_ref, dd3_ref, l3_ref, do_scr):
        for c0 in range(0, SPAN, 256):
            rows = slice(c0, c0 + 256)
            dyv, ov = dy_ref[rows, :], o_ref[rows, :]
            silu, dsilu = _silu_and_grad(g_ref[rows, :])
            do = dyv * silu
            do_scr[rows, :] = do
            do1_ref[rows, :] = do.astype(MXU_DTYPE)
            dd1_ref[rows, :] = _seg_sum_impl(do * ov)
            dzg_ref[rows, :] = (dyv * ov * dsilu).astype(MXU_DTYPE)
        for (window, dil), do_o, dd_o, l_o in ((DIL_GROUPS[1], do2_ref, dd2_ref, l2_ref), (DIL_GROUPS[2], do3_ref, dd3_ref, l3_ref)):
            for u in range(SPAN // DIL_NK):
                src, dst = _rm_src(u, window, dil), _rm_dst(u)
                do_o[dst, :] = do_scr[src, :].astype(MXU_DTYPE)
                dd_o[dst, :] = dd1_ref[src, :]
                l_o[dst, :] = lse_ref[src, :]

    sds = jax.ShapeDtypeStruct
    f32, mxu = sds((t, 512), F32), sds((t, 512), MXU_DTYPE)
    return pl.pallas_call(
        body,
        name="dil_gate_bwd",
        grid=(t // SPAN, 4),
        in_specs=[_span_blk(0), _span_blk(0), _span_blk(36), _span_blk(0)],
        out_specs=[_span_blk(0)] * 9,
        out_shape=[mxu, mxu, f32, mxu, f32, f32, mxu, f32, f32],
        scratch_shapes=[pltpu.VMEM((SPAN, LANES), F32)],
        compiler_params=_cparams(("parallel", "parallel")),
    )(dy, o, z, lse)


def _merge_tile(p0, p1, p2, z0, z1, z2, b0, b1, b2):
    return _sigmoid(z0 + b0) * p0 + _sigmoid(z1 + b1) * p1 + _sigmoid(z2 + b2) * p2


def _merge_ins(ps, z, b):
    w = 256
    return ([_row(p, w, 0, True) for p in ps] + [_row(z, w, 4 * i, True) for i in range(3)]
            + [_const(b, w, 4 * i, True) for i in range(3)])


def merge_fwd(ps, z, b):
    def fn(ins, outs, _):
        merged = _merge_tile(*[r[...] for r in ins])
        outs[0][...] = merged.astype(MXU_DTYPE)
        outs[1][...] = merged.T.astype(MXU_DTYPE)

    return rowwise("merge_fwd", fn, z.shape[0], ncol=4, ins=_merge_ins(ps, z, b), outs=[(1024, 256, MXU_DTYPE, True)],
                   touts=[(1024, 256, MXU_DTYPE, True)])


def merge_bwd(dm, ps, z, b):
    def fn(ins, outs, accs):
        _, vjp = jax.vjp(_merge_tile, *[r[...] for r in ins[:9]])
        grads = vjp(ins[9][...])
        for i in range(3):
            outs[i][...] = grads[i].astype(MXU_DTYPE)
            outs[3 + i][...] = grads[3 + i].astype(MXU_DTYPE)
            accs[i][...] += grads[6 + i]

    return rowwise("merge_bwd", fn, z.shape[0], ncol=4, ins=_merge_ins(ps, z, b) + [_row(dm, 256, 0, True)],
                   outs=[(1024, 256, MXU_DTYPE, True)] * 6, accs=[(1, 1024, 256, True)] * 3)


EW_BLOCK_BYTES = 2**21


def _tile2d(r, c):
    if r * c * 4 <= EW_BLOCK_BYTES:
        return r, c
    for tr in (512, 256, 128, 64, 32):
        if r % tr == 0 and tr * c * 4 <= EW_BLOCK_BYTES:
            return tr, c
    for tc in (1024, 512, 256, 128):
        if c % tc == 0 and r * tc * 4 <= EW_BLOCK_BYTES:
            return r, tc
    raise ValueError((r, c))


def adamw(name, w, g, m, v):
    shape = w.shape
    c = shape[-1]
    r = w.size // c
    tr, tc = _tile2d(r, c)
    assert tc == c
    c1, c2 = 1.0 - ADAM_B1 ** ADAM_STEP, 1.0 - ADAM_B2 ** ADAM_STEP

    def body(w_ref, g_ref, m_ref, v_ref, d_ref, mo_ref, vo_ref):
        gv = g_ref[...]
        mn = ADAM_B1 * m_ref[...] + (1.0 - ADAM_B1) * gv
        vn = ADAM_B2 * v_ref[...] + (1.0 - ADAM_B2) * (gv * gv)
        d_ref[...] = -ADAM_LR * ((mn / c1) / (jnp.sqrt(vn / c2) + ADAM_EPS) + ADAM_WD * w_ref[...])
        mo_ref[...] = mn
        vo_ref[...] = vn

    spec = pl.BlockSpec((tr, c), lambda i: (i, 0))
    outs = pl.pallas_call(
        body,
        name=name,
        grid=(r // tr,),
        in_specs=[spec] * 4,
        out_specs=[spec] * 3,
        out_shape=[jax.ShapeDtypeStruct((r, c), F32)] * 3,
        compiler_params=_cparams(("parallel",)),
    )(*[a.reshape(r, c) for a in (w, g, m, v)])
    return [o.reshape(shape) for o in outs]


def adamw_pair(name, w, rh, lc, m, v, c_idx):
    _, r, c = w.shape
    tr, tc = _tile2d(r, c)
    c1, c2 = 1.0 - ADAM_B1 ** ADAM_STEP, 1.0 - ADAM_B2 ** ADAM_STEP

    def body(c_ref, w_ref, rh_ref, lc_ref, m_ref, v_ref, g_ref, d_ref, mo_ref, vo_ref):
        gv = jnp.where(pl.program_id(0) == c_ref[0], rh_ref[...], lc_ref[...])
        mn = ADAM_B1 * m_ref[...] + (1.0 - ADAM_B1) * gv
        vn = ADAM_B2 * v_ref[...] + (1.0 - ADAM_B2) * (gv * gv)
        g_ref[...] = gv
        d_ref[...] = -ADAM_LR * ((mn / c1) / (jnp.sqrt(vn / c2) + ADAM_EPS) + ADAM_WD * w_ref[...])
        mo_ref[...] = mn
        vo_ref[...] = vn

    full = pl.BlockSpec((None, tr, tc), lambda h, i, j, cr: (h, i, j))
    half = pl.BlockSpec((tr, tc), lambda h, i, j, cr: (i, j))
    return pl.pallas_call(
        body,
        name=name,
        grid_spec=pltpu.PrefetchScalarGridSpec(
            num_scalar_prefetch=1,
            grid=(2, r // tr, c // tc),
            in_specs=[full, half, half, full, full],
            out_specs=[full] * 4,
        ),
        out_shape=[jax.ShapeDtypeStruct((2, r, c), F32)] * 4,
        compiler_params=_cparams(("parallel", "parallel", "parallel")),
    )(c_idx, w, rh, lc, m, v)


def sum_pair(name, g, la, c_idx, out_dtype):
    _, _, r, c = g.shape
    tr, tc = _tile2d(r, c)

    def body(c_ref, g_ref, la_ref, o_ref):
        o_ref[...] = (g_ref[...] + la_ref[...]).astype(o_ref.dtype)

    return pl.pallas_call(
        body,
        name=name,
        grid_spec=pltpu.PrefetchScalarGridSpec(
            num_scalar_prefetch=1,
            grid=(4, r // tr, c // tc),
            in_specs=[pl.BlockSpec((None, None, tr, tc), lambda s, i, j, cr: (s, cr[0], i, j)),
                      pl.BlockSpec((None, tr, tc), lambda s, i, j, cr: (s, i, j))],
            out_specs=pl.BlockSpec((None, tr, tc), lambda s, i, j, cr: (s, i, j)),
        ),
        out_shape=jax.ShapeDtypeStruct((4, r, c), out_dtype),
        compiler_params=_cparams(("parallel", "parallel", "parallel")),
    )(c_idx, g, la)


def sum_chips(name, g, la, lb, cs_idx):
    _, _, r, c = g.shape
    tr, tc = _tile2d(r, c)

    def body(cs_ref, g_ref, la_ref, l0_ref, l1_ref, l2_ref, o_ref):
        own = g_ref[...] + la_ref[...]
        o_ref[...] = ((own + l0_ref[...].astype(F32)) + l1_ref[...].astype(F32)) + l2_ref[...].astype(F32)

    lspec = lambda k: pl.BlockSpec((None, tr, tc), lambda i, j, cs: (k, i, j))
    return pl.pallas_call(
        body,
        name=name,
        grid_spec=pltpu.PrefetchScalarGridSpec(
            num_scalar_prefetch=1,
            grid=(r // tr, c // tc),
            in_specs=[pl.BlockSpec((None, None, tr, tc), lambda i, j, cs: (cs[1], cs[0], i, j)),
                      pl.BlockSpec((None, tr, tc), lambda i, j, cs: (cs[1], i, j)), lspec(0), lspec(1), lspec(2)],
            out_specs=pl.BlockSpec((tr, tc), lambda i, j, cs: (i, j)),
        ),
        out_shape=jax.ShapeDtypeStruct((r, c), F32),
        compiler_params=_cparams(("parallel", "parallel")),
    )(cs_idx, g, la, lb, lb, lb)


def _place():
    x, y, c = lax.axis_index("x"), lax.axis_index("y"), lax.axis_index("c")
    chips = [(1 - x, y), (x, 1 - y), (1 - x, 1 - y)]
    return x, y, c, chips


HBM_SPEC = pl.BlockSpec(memory_space=pltpu.HBM)


def _comm_call(name, body, ins, out_shapes, n_sem, n_local):
    return pl.pallas_call(
        body,
        name=name,
        in_specs=[HBM_SPEC] * len(ins),
        out_specs=[HBM_SPEC] * len(out_shapes),
        out_shape=out_shapes,
        scratch_shapes=[pltpu.SemaphoreType.DMA((n_sem,)), pltpu.SemaphoreType.DMA((n_sem,)),
                        pltpu.SemaphoreType.DMA((max(n_local, 1),))],
    )(*ins)


def allgather_weights(ws):
    n = len(ws)

    def body(*refs):
        ins, outs = refs[:n], refs[n:2 * n]
        send, recv, _ = refs[2 * n:]
        x, y, c, chips = _place()
        s = 2 * x + y
        sib = (x, y, 1 - c)

        def rc(a, k, src, dst, dev):
            return pltpu.make_async_remote_copy(src_ref=src, dst_ref=dst, send_sem=send.at[6 * a + k], recv_sem=recv.at[6 * a + k],
                                                device_id=dev, device_id_type=MESH)

        sends = []
        for j, chip in enumerate(chips):
            for a in range(n):
                sends.append(rc(a, j, ins[a].at[c], outs[a].at[s, c], (*chip, c)))
                sends[-1].start()
        for j, (cx, cy) in enumerate(chips):
            sj = 2 * cx + cy
            for a in range(n):
                landed = outs[a].at[sj, c]
                rc(a, j, landed, landed, sib).wait_recv()
                sends.append(rc(a, 3 + j, landed, landed, sib))
                sends[-1].start()
        for j, (cx, cy) in enumerate(chips):
            sj = 2 * cx + cy
            for a in range(n):
                other = outs[a].at[sj, 1 - c]
                rc(a, 3 + j, other, other, sib).wait_recv()
        for cp in sends:
            cp.wait_send()

    out_shapes = [jax.ShapeDtypeStruct((4,) + w.shape, w.dtype) for w in ws]
    return _comm_call("allgather_weights", body, ws, out_shapes, 6 * n, 0)


def exchange_sibling(gs):
    n = len(gs)

    def body(*refs):
        ins, outs = refs[:n], refs[n:2 * n]
        send, recv, _ = refs[2 * n:]
        x, y, c, _ = _place()
        cps = []
        for a in range(n):
            for s in range(4):
                cps.append(pltpu.make_async_remote_copy(src_ref=ins[a].at[s, 1 - c], dst_ref=outs[a].at[s],
                                                        send_sem=send.at[4 * a + s], recv_sem=recv.at[4 * a + s],
                                                        device_id=(x, y, 1 - c), device_id_type=MESH))
                cps[-1].start()
        for cp in cps:
            cp.wait()

    out_shapes = [jax.ShapeDtypeStruct((4,) + g.shape[2:], g.dtype) for g in gs]
    return _comm_call("exchange_sibling", body, gs, out_shapes, 4 * n, 0)


def exchange_chips(ps):
    n = len(ps)

    def body(*refs):
        ins, outs = refs[:n], refs[n:2 * n]
        send, recv, _ = refs[2 * n:]
        x, y, c, chips = _place()
        cps = []
        for j, (cx, cy) in enumerate(chips):
            for a in range(n):
                cps.append(pltpu.make_async_remote_copy(src_ref=ins[a].at[2 * cx + cy], dst_ref=outs[a].at[j],
                                                        send_sem=send.at[3 * a + j], recv_sem=recv.at[3 * a + j],
                                                        device_id=(cx, cy, c), device_id_type=MESH))
                cps[-1].start()
        for cp in cps:
            cp.wait()

    out_shapes = [jax.ShapeDtypeStruct((3,) + p.shape[1:], p.dtype) for p in ps]
    return _comm_call("exchange_chips", body, ps, out_shapes, 3 * n, 0)


def exchange_final(rs, small):
    n = len(rs)

    def body(*refs):
        ins, small_ref = refs[:n], refs[n]
        outs, small_out = refs[n + 1:2 * n + 1], refs[2 * n + 1]
        send, recv, lsem = refs[2 * n + 2:]
        x, y, c, _ = _place()
        s = 2 * x + y
        cps, local = [], []
        for a in range(n):
            cps.append(pltpu.make_async_remote_copy(src_ref=ins[a], dst_ref=outs[a], send_sem=send.at[a], recv_sem=recv.at[a],
                                                    device_id=(x, y, 1 - c), device_id_type=MESH))
            cps[-1].start()
        local.append(pltpu.make_async_copy(small_ref, small_out.at[s, c], lsem.at[0]))
        local[-1].start()
        k = n
        for fx in (0, 1):
            for fy in (0, 1):
                for fc in (0, 1):
                    if fx or fy or fc:
                        dev = (x ^ fx, y ^ fy, c ^ fc)
                        cps.append(pltpu.make_async_remote_copy(src_ref=small_ref, dst_ref=small_out.at[s, c],
                                                                send_sem=send.at[k], recv_sem=recv.at[k],
                                                                device_id=dev, device_id_type=MESH))
                        cps[-1].start()
                        k += 1
        for cp in cps:
            cp.wait()
        for cp in local:
            cp.wait()

    out_shapes = [jax.ShapeDtypeStruct(r.shape, r.dtype) for r in rs]
    out_shapes.append(jax.ShapeDtypeStruct((4, 2) + small.shape, small.dtype))
    res = _comm_call("exchange_final", body, list(rs) + [small], out_shapes, n + 7, 1)
    return res[:n], res[n]


def _pad_in_cols(w):
    z = lambda n: jnp.zeros(w.shape[:-1] + (n,), w.dtype)
    return jnp.concatenate([w[..., :KR_OFF], z(64), w[..., KR_OFF:KR_OFF + MLA_ROPE], z(32), w[..., KR_OFF + MLA_ROPE:]], axis=-1)


def _unpad_in_cols(g):
    return jnp.concatenate([g[..., :KR_OFF], g[..., KR_OFF + 64:KR_OFF + 96], g[..., KR_OFF + 128:]], axis=-1)


def _pad_heads(w, real):
    k = w.shape[0]
    return jnp.pad(w.reshape(k, MLA_HEADS, real), ((0, 0), (0, 0), (0, LANES - real))).reshape(k, MLA_HEADS * LANES)


def _pad_gain(g, real):
    return jnp.pad(g.reshape(1, real), ((0, 0), (0, LANES - real)))


def layer_weights(full):
    w_ukv = full["w_ukv"].reshape(128, MLA_HEADS, 2, 64)
    two = lambda g: jnp.concatenate([g, g]).reshape(1, LANES)
    return dict(
        norm_g=full["norm_g"].reshape(1, -1), w_in=full["w_in"], conv_w=full["conv_w"], conv_b=full["conv_b"].reshape(1, -1),
        wgx=full["w_gate_x"], bgx=full["b_gate_x"].reshape(LRU_BLOCKS, 1, LANES),
        wga=full["w_gate_a"], bga=full["b_gate_a"].reshape(LRU_BLOCKS, 1, LANES),
        lam=full["lru_lambda"].reshape(1, -1), w_lru_o=full["w_lru_o"],
        cq_norm_g=full["cq_norm_g"].reshape(1, -1), ckv_norm_g=full["ckv_norm_g"].reshape(1, -1),
        wq=_pad_heads(full["w_uq"], MLA_QK), wk=_pad_heads(w_ukv[:, :, 0].reshape(128, 512), 64),
        wv=w_ukv[:, :, 1].reshape(128, 512),
        gq=_pad_gain(full["mla_q_norm_g"], MLA_QK), gk=_pad_gain(full["mla_k_norm_g"], MLA_QK),
        w_mla_o=full["w_mla_o"], gq2=two(full["dil_q_norm_g"]), gk2=two(full["dil_k_norm_g"]),
        w_dil_o=full["w_dil_o"], b_merge=full["b_merge"].reshape(1, -1), w_out=full["w_out"],
    )


def layer_fwd(x, w, tabs):
    t = x.shape[0]
    h, ht = rmsnorm_fwd(x, w["norm_g"])
    z_lru = mm_nn("in_proj_lru", h, w["w_in"], n=2048, n_off=G_LRU)
    z_mla = mm_nn("in_proj_mla", h, w["w_in"], n=1024, n_off=G_MLA)
    z_dil = mm_nn("in_proj_dil", h, w["w_in"], n=5120, n_off=G_DIL)
    z_mrg = mm_nn("in_proj_mrg", h, w["w_in"], n=3072, n_off=G_MRG)
    hs, y_lru, yt_lru = lru_fwd(z_lru, w["conv_w"], w["conv_b"], w["wgx"], w["bgx"], w["wga"], w["bga"], w["lam"])
    qm, km, vm, vtm = mla_prep_fwd(z_mla, tabs, w)
    o_mla, y_mla, yt_mla, lse_mla = mla_attn_fwd(qm, km, vtm, z_mla)
    os_, lses, dil_rm = [], [], []
    for gi, (_, dil) in enumerate(DIL_GROUPS):
        qkv = dil_prep_fwd(z_dil, tabs, w["gq2"], w["gk2"], gi)
        o, lse = dil_attn_fwd(f"dil_attn_fwd_{dil}", *qkv, dil)
        os_.append(o)
        lses.append(lse)
        dil_rm.append(qkv)
    y_dil, o_dil, lse_dil, yt_dil = dil_combine(os_, lses, z_dil)
    ps = [mm_nn("proj_lru", y_lru, w["w_lru_o"]), mm_nn("proj_mla", y_mla, w["w_mla_o"]), mm_nn("proj_dil", y_dil, w["w_dil_o"])]
    merged, merged_t = merge_fwd(ps, z_mrg, w["b_merge"])
    out = mm_nn("out_proj", merged, w["w_out"], add=x)
    res = dict(x=x, ht=ht, z_lru=z_lru, z_mla=z_mla, z_dil=z_dil, z_mrg=z_mrg, hs=hs, yt_lru=yt_lru, qm=qm, km=km, vm=vm, o_mla=o_mla,
               yt_mla=yt_mla, lse_mla=lse_mla, dil_rm=dil_rm, yt_dil=yt_dil, o_dil=o_dil, lse_dil=lse_dil, ps=ps, merged_t=merged_t)
    return out, res


def layer_bwd(dout, r, w, tabs):
    t = dout.shape[0]
    g = {}
    dmerged = mm_nt("out_proj_dx", dout, w["w_out"])
    g["w_out"] = mm_nn("out_proj_dw", r["merged_t"], dout, tm=1024, tn=512, tk=2048)
    dp0, dp1, dp2, dzm0, dzm1, dzm2, db0, db1, db2 = merge_bwd(dmerged, r["ps"], r["z_mrg"], w["b_merge"])
    g["b_merge"] = jnp.concatenate([db0, db1, db2], axis=1).reshape(-1)
    dy_lru = mm_nt("proj_lru_dx", dp0, w["w_lru_o"])
    dy_mla = mm_nt("proj_mla_dx", dp1, w["w_mla_o"])
    dy_dil = mm_nt("proj_dil_dx", dp2, w["w_dil_o"])
    g["w_lru_o"] = mm_nn("proj_lru_dw", r["yt_lru"], dp0, **DW_TILES)
    g["w_mla_o"] = mm_nn("proj_mla_dw", r["yt_mla"], dp1, **DW_TILES)
    g["w_dil_o"] = mm_nn("proj_dil_dw", r["yt_dil"], dp2, **DW_TILES)
    dzx, dzg_lru, dwgx, dbgx, dwga, dbga, dlam, dcw, dcb = lru_bwd(r["z_lru"], r["hs"], dy_lru, w["conv_w"], w["conv_b"], w["wgx"],
                                                                    w["bgx"], w["wga"], w["bga"], w["lam"])
    g.update(w_gate_x=dwgx, b_gate_x=dbgx.reshape(LRU_BLOCKS, LANES), w_gate_a=dwga, b_gate_a=dbga.reshape(LRU_BLOCKS, LANES),
             lru_lambda=dlam.reshape(-1), conv_w=dcw, conv_b=dcb.reshape(-1))
    do_m, dzg_mla, dd_m = mla_gate_bwd(dy_mla, r["o_mla"], r["z_mla"])
    dq_m, dk_m, dv_m = mla_attn_bwd(r["qm"], r["km"], r["vm"], do_m, r["lse_mla"], dd_m)
    dz_mla3, dg_cq, dg_ckv, dwq, dwk, dwv, dgq, dgk = mla_prep_bwd(r["z_mla"], tabs, w, dq_m, dk_m, dv_m)
    g.update(cq_norm_g=dg_cq.reshape(-1), ckv_norm_g=dg_ckv.reshape(-1), mla_q_norm_g=dgq[0, :MLA_QK], mla_k_norm_g=dgk[0, :MLA_QK])
    g["w_uq"] = dwq.reshape(256, MLA_HEADS, LANES)[:, :, :MLA_QK].reshape(256, MLA_HEADS * MLA_QK)
    g["w_ukv"] = jnp.concatenate([dwk.reshape(128, MLA_HEADS, LANES)[:, :, :64], dwv.reshape(128, MLA_HEADS, 64)], axis=2).reshape(128, 1024)
    dzg_dil, do1, dd1, do2, dd2, l2, do3, dd3, l3 = dil_gate_bwd(dy_dil, r["o_dil"], r["z_dil"], r["lse_dil"])
    stats = [(do1, r["lse_dil"], dd1), (do2, l2, dd2), (do3, l3, dd3)]
    dzq, dzk, dzv, dgq2, dgk2 = [], [], [], [], []
    for gi, (_, dil) in enumerate(DIL_GROUPS):
        dq, dk, dv = dil_attn_bwd(f"dil_attn_bwd_{dil}", *r["dil_rm"][gi], *stats[gi], dil)
        parts = dil_prep_bwd(r["z_dil"], tabs, w["gq2"], w["gk2"], dq, dk, dv, gi)
        for acc, part in zip((dzq, dzk, dzv, dgq2, dgk2), parts):
            acc.append(part)
    g.update(dil_q_norm_g=sum(dgq2)[0, :DIL_HD], dil_k_norm_g=sum(dgk2)[0, :DIL_HD])
    dz = jnp.concatenate([dzx, dzg_lru, dz_mla3, dzg_mla] + dzq + dzk + dzv + [dzg_dil, dzm0, dzm1, dzm2], axis=1)
    dh = mm_nt("in_proj_dx", dz, w["w_in"])
    g["w_in"] = _unpad_in_cols(mm_nn("in_proj_dw", r["ht"], dz, **DW_TILES))
    dx, dng = rmsnorm_bwd(r["x"], dh, dout, w["norm_g"])
    g["norm_g"] = dng.reshape(-1)
    return dx, g


def local_step(x, positions, target, fulls):
    tabs = rope_tables(positions.reshape(-1, 1))
    ws, ress = [], []
    for l in range(2):
        ws.append(layer_weights(fulls[l]))
        x, res = layer_fwd(x, ws[l], tabs)
        ress.append(res)
    dy, loss = loss_head(x, target)
    grads = [None, None]
    for l in (1, 0):
        dy, grads[l] = layer_bwd(dy, ress[l], ws[l], tabs)
    return loss, dy, grads


WEIGHTS = ["norm_g", "w_in", "conv_w", "conv_b", "w_gate_x", "b_gate_x", "w_gate_a", "b_gate_a", "lru_lambda", "w_lru_o", "cq_norm_g",
           "ckv_norm_g", "w_uq", "w_ukv", "mla_q_norm_g", "mla_k_norm_g", "w_mla_o", "dil_q_norm_g", "dil_k_norm_g", "w_dil_o", "b_merge",
           "w_out"]
SHARDED = {"w_in": 2, "conv_w": 2, "w_lru_o": 1, "w_uq": 2, "w_ukv": 2, "w_mla_o": 2, "w_dil_o": 2, "w_out": 1}
REPLICATED = [n for n in WEIGHTS if n not in SHARDED]
SMALL_ROWS = 72


def kernel(x, positions, norm_g, w_in, conv_w, conv_b, w_gate_x, b_gate_x, w_gate_a, b_gate_a, lru_lambda, w_lru_o, cq_norm_g, ckv_norm_g, w_uq, w_ukv, mla_q_norm_g, mla_k_norm_g, w_mla_o, dil_q_norm_g, dil_k_norm_g, w_dil_o, b_merge, w_out, loss_target, m_norm_g, m_w_in, m_conv_w, m_conv_b, m_w_gate_x, m_b_gate_x, m_w_gate_a, m_b_gate_a, m_lru_lambda, m_w_lru_o, m_cq_norm_g, m_ckv_norm_g, m_w_uq, m_w_ukv, m_mla_q_norm_g, m_mla_k_norm_g, m_w_mla_o, m_dil_q_norm_g, m_dil_k_norm_g, m_w_dil_o, m_b_merge, m_w_out, v_norm_g, v_w_in, v_conv_w, v_conv_b, v_w_gate_x, v_b_gate_x, v_w_gate_a, v_b_gate_a, v_lru_lambda, v_w_lru_o, v_cq_norm_g, v_ckv_norm_g, v_w_uq, v_w_ukv, v_mla_q_norm_g, v_mla_k_norm_g, v_w_mla_o, v_dil_q_norm_g, v_dil_k_norm_g, v_w_dil_o, v_b_merge, v_w_out):
    args = locals()
    w = {n: args[n] for n in WEIGHTS}
    m = {n: args["m_" + n] for n in WEIGHTS}
    v = {n: args["v_" + n] for n in WEIGHTS}
    my_c = lax.axis_index("c").astype(jnp.int32)
    my_s = (2 * lax.axis_index("x") + lax.axis_index("y")).astype(jnp.int32)
    c_idx = my_c.reshape(1)
    cs_idx = jnp.stack([my_c, my_s])

    names = list(SHARDED)
    wire = [w[n] if n == "conv_w" else w[n].astype(BF16) for n in names]
    gathered = allgather_weights(wire)
    fulls = [{n: w[n][l] for n in REPLICATED} for l in range(2)]
    for n, ga, own in zip(names, gathered, wire):
        for l in range(2):
            fulls[l][n] = jnp.concatenate([jnp.where(my_s == s, own[l], ga[s, l]) for s in range(4)], axis=SHARDED[n] - 1)
    for l in range(2):
        fulls[l]["w_in"] = _pad_in_cols(fulls[l]["w_in"])

    loss, grad_x, grads = local_step(x[0], positions[0], loss_target[0], fulls)
    loss = lax.psum(loss, ("x", "y", "c"))

    flat = jnp.concatenate([jnp.stack([grads[0][n], grads[1][n]]).reshape(-1) for n in REPLICATED])
    small = jnp.pad(flat, (0, 8 * SMALL_ROWS * 1024 - flat.size)).reshape(4, 2, SMALL_ROWS, 1024)
    gs = []
    for n in names:
        per_layer = []
        for l in range(2):
            g_ = grads[l][n]
            if n == "w_in":
                per_layer.append(g_.reshape(D_MODEL, 4, IN_WIDTH // 4).transpose(1, 2, 0))
            else:
                parts = jnp.stack(jnp.split(g_, 4, axis=SHARDED[n] - 1))
                per_layer.append(parts.reshape(4, -1, parts.shape[-1]))
        gs.append(jnp.stack(per_layer, axis=1))
    gs.append(small)
    tags = names + ["small"]
    la = exchange_sibling(gs)
    wire_dt = [BF16 if g_.size >= 2**20 else F32 for g_ in gs]
    pcs = [sum_pair(f"sum_pair_{t}", g_, l_, c_idx, dt) for t, g_, l_, dt in zip(tags, gs, la, wire_dt)]
    lb = exchange_chips(pcs)
    rs = [sum_chips(f"sum_chips_{t}", g_, l_, b_, cs_idx) for t, g_, l_, b_ in zip(tags, gs, la, lb)]
    sib_half, small_all = exchange_final(rs[:-1], rs[-1])

    g_local, delta, new_m, new_v = {}, {}, {}, {}
    for n, rh, lc in zip(names, rs, sib_half):
        shp = w[n].shape
        if n == "w_in":
            as3 = lambda a: a.transpose(0, 2, 1)
            back = lambda o: o.transpose(0, 2, 1)
        else:
            as3 = lambda a: a.reshape((2,) + rh.shape)
            back = lambda o: o.reshape(shp)
        outs = adamw_pair(f"adamw_{n}", as3(w[n]), rh, lc, as3(m[n]), as3(v[n]), c_idx)
        g_local[n], delta[n], new_m[n], new_v[n] = [back(o) for o in outs]
    flat = small_all.reshape(-1)
    off = 0
    for n in REPLICATED:
        g_local[n] = flat[off:off + w[n].size].reshape(w[n].shape)
        off += w[n].size
        delta[n], new_m[n], new_v[n] = adamw(f"adamw_{n}", w[n], g_local[n], m[n], v[n])
    return (loss, grad_x[None], *[g_local[n] for n in WEIGHTS], *[delta[n] for n in WEIGHTS], *[new_m[n] for n in WEIGHTS],
            *[new_v[n] for n in WEIGHTS])
```

```python
import functools

import jax
import jax.numpy as jnp
from jax import lax
from jax.experimental import pallas as pl
from jax.experimental.pallas import tpu as pltpu

F32 = jnp.float32
BF16 = jnp.bfloat16
MXU_DTYPE = jnp.bfloat16

D_MODEL = 1024
EPS = 1e-6
ROPE_THETA = 10000.0
LRU_BLOCKS = 8
LRU_C = 8.0
MLA_HEADS = 8
MLA_NOPE = 64
MLA_ROPE = 32
MLA_QK = 96
DIL_GROUPS = ((128, 1), (512, 4), (2048, 16))
DIL_HD = 64
DIL_NK = 128
MLA_HPS = 8
IN_WIDTH = 11168
ADAM_LR, ADAM_B1, ADAM_B2, ADAM_EPS, ADAM_WD, ADAM_STEP = 0.001, 0.9, 0.999, 1e-08, 0.01, 10

LANES = 128
G_LRU, G_MLA, G_DIL, G_MRG = 0, 2048, 3072, 8192
IN_PAD = 11264
KR_OFF = 2432

NN = (((1,), (0,)), ((), ()))
NT = (((1,), (1,)), ((), ()))
TN = (((0,), (0,)), ((), ()))
NEG = -1e30
MESH = pl.DeviceIdType.MESH
VMEM_LIMIT = 48 * 2**20


def _cparams(sem):
    return pltpu.CompilerParams(dimension_semantics=sem, vmem_limit_bytes=VMEM_LIMIT)


def _dot(a, b, dims):
    return lax.dot_general(a.astype(MXU_DTYPE), b.astype(MXU_DTYPE), dims, preferred_element_type=F32)


@jax.custom_vjp
def mm(a, w):
    return _dot(a, w, NN)


def _mm_fwd(a, w):
    return _dot(a, w, NN), (a, w)


def _mm_bwd(res, g):
    a, w = res
    return _dot(g, w, NT), _dot(a, g, TN)


mm.defvjp(_mm_fwd, _mm_bwd)


def _seg64_matrix():
    r = lax.broadcasted_iota(jnp.int32, (LANES, LANES), 0) < DIL_HD
    c = lax.broadcasted_iota(jnp.int32, (LANES, LANES), 1) < DIL_HD
    return (r == c).astype(BF16)


def _seg_sum_impl(x):
    b = _seg64_matrix()
    hi = x.astype(BF16)
    r1 = x - hi.astype(F32)
    mid = r1.astype(BF16)
    lo = (r1 - mid.astype(F32)).astype(BF16)
    dot = lambda u: lax.dot_general(u, b, NN, preferred_element_type=F32)
    return dot(hi) + dot(mid) + dot(lo)


@jax.custom_vjp
def seg_sum(x):
    return _seg_sum_impl(x)


seg_sum.defvjp(lambda x: (_seg_sum_impl(x), None), lambda _, g: (_seg_sum_impl(g),))


def _lroll_impl(x, s):
    return pltpu.roll(x, s % LANES, 1)


@functools.partial(jax.custom_vjp, nondiff_argnums=(1,))
def lroll(x, s):
    return _lroll_impl(x, s)


lroll.defvjp(lambda x, s: (_lroll_impl(x, s), None), lambda s, _, g: (_lroll_impl(g, -s),))


class _Ops:
    def __init__(self, diff):
        self.mm = mm if diff else (lambda a, w: _dot(a, w, NN))
        self.seg_sum = seg_sum if diff else _seg_sum_impl
        self.lroll = lroll if diff else _lroll_impl


PLAIN, DIFF = _Ops(False), _Ops(True)


def rms(x, g, n):
    ms = jnp.sum(x * x, axis=-1, keepdims=True) * (1.0 / n)
    return x * lax.rsqrt(ms + EPS) * g


def rope(ops, x, c, s1, s2, half):
    return x * c + ops.lroll(x, -half) * s1 + ops.lroll(x, half) * s2


def _sigmoid(x):
    return 1.0 / (1.0 + jnp.exp(-x))


def _silu_and_grad(g):
    sg = _sigmoid(g)
    return g * sg, sg * (1.0 + g * (1.0 - sg))


def _softplus(x):
    return jnp.maximum(x, 0.0) + jnp.log(1.0 + jnp.exp(-jnp.abs(x)))


def _expm1(y):
    series = y * (1.0 + y * (0.5 + y * (1.0 / 6.0 + y * (1.0 / 24.0 + y * (1.0 / 120.0)))))
    return jnp.where(y > -0.05, series, jnp.exp(jnp.minimum(y, -0.05)) - 1.0)


def _mm_call(name, a, b, *, mode, m, n, k, a_blk, b_blk, out_dtype, add, tm, tn, tk):
    nk = k // tk
    assert m % tm == 0 and n % tn == 0 and k % tk == 0, (name, m, n, k, tm, tn, tk)
    dims = {"nn": NN, "nt": NT, "tn": TN}[mode]

    def body(*refs):
        if add is None:
            a_ref, b_ref, o_ref, *scr = refs
            add_ref = None
        else:
            a_ref, b_ref, add_ref, o_ref, *scr = refs
        part = _dot(a_ref[...], b_ref[...], dims)

        def finish(acc):
            if add_ref is not None:
                acc = acc + add_ref[...]
            o_ref[...] = acc.astype(o_ref.dtype)

        if nk == 1:
            finish(part)
        else:
            (acc_ref,) = scr
            kk = pl.program_id(2)

            @pl.when(kk == 0)
            def _():
                acc_ref[...] = part

            @pl.when(kk > 0)
            def _():
                acc_ref[...] += part

            @pl.when(kk == nk - 1)
            def _():
                finish(acc_ref[...])

    in_specs = [a_blk, b_blk]
    args = [a, b]
    if add is not None:
        in_specs.append(pl.BlockSpec((tm, tn), lambda j, i, kk: (i, j)))
        args.append(add)
    return pl.pallas_call(
        body,
        name=name,
        grid=(n // tn, m // tm, nk),
        in_specs=in_specs,
        out_specs=pl.BlockSpec((tm, tn), lambda j, i, kk: (i, j)),
        out_shape=jax.ShapeDtypeStruct((m, n), out_dtype),
        scratch_shapes=[] if nk == 1 else [pltpu.VMEM((tm, tn), F32)],
        compiler_params=_cparams(("parallel", "parallel", "arbitrary")),
    )(*args)


def mm_nn(name, a, b, *, n=None, n_off=0, out_dtype=F32, add=None, tm=512, tn=1024, tk=1024):
    m, k = a.shape
    n = b.shape[1] if n is None else n
    tm, tn, tk = min(tm, m), min(tn, n), min(k, tk)
    ob = n_off // tn
    assert n_off % tn == 0
    return _mm_call(name, a, b, mode="nn", m=m, n=n, k=k, out_dtype=out_dtype, add=add, tm=tm, tn=tn, tk=tk,
                    a_blk=pl.BlockSpec((tm, tk), lambda j, i, kk: (i, kk)),
                    b_blk=pl.BlockSpec((tk, tn), lambda j, i, kk: (kk, j + ob)))


def mm_nt(name, a, b, *, out_dtype=F32, tm=512, tn=1024, tk=1024):
    m, k = a.shape
    n = b.shape[0]
    tn, tk = min(tn, n), min(k, tk)
    return _mm_call(name, a, b, mode="nt", m=m, n=n, k=k, out_dtype=out_dtype, add=None, tm=tm, tn=tn, tk=tk,
                    a_blk=pl.BlockSpec((tm, tk), lambda j, i, kk: (i, kk)),
                    b_blk=pl.BlockSpec((tn, tk), lambda j, i, kk: (j, kk)))


DW_TILES = dict(tm=1024, tn=512, tk=4096)


def rowwise(name, fn, t, *, tm=256, ncol=1, ins=(), outs=(), touts=(), accs=()):
    n_in, n_out, n_acc = len(ins), len(outs) + len(touts), len(accs)

    def zero_when(ref, cond):
        @pl.when(cond)
        def _():
            ref[...] = jnp.zeros(ref.shape, ref.dtype)

    def body(*refs):
        in_refs, out_refs, acc_refs = refs[:n_in], refs[n_in:n_in + n_out], refs[n_in + n_out:]
        j, i = pl.program_id(0), pl.program_id(1)
        for ref, (_, _, _, cd) in zip(acc_refs, accs):
            zero_when(ref, (i == 0) if cd else ((i == 0) & (j == 0)))
        fn(in_refs, out_refs, acc_refs)

    def in_spec(arr, w, base, cd, rd):
        rows = tm if rd else arr.shape[0]
        return pl.BlockSpec((rows, w), lambda j, i: (i if rd else 0, base + (j if cd else 0)))

    in_specs = [in_spec(*e) for e in ins]
    out_specs = [pl.BlockSpec((tm, w), (lambda j, i, cd=cd: (i, j if cd else 0))) for (_, w, _, cd) in outs]
    out_specs += [pl.BlockSpec((w, tm), (lambda j, i, cd=cd: (j if cd else 0, i))) for (_, w, _, cd) in touts]
    out_specs += [pl.BlockSpec((r, w), (lambda j, i, cd=cd: (0, j if cd else 0))) for (r, _, w, cd) in accs]
    out_shape = [jax.ShapeDtypeStruct((t, c), dt) for (c, _, dt, _) in outs]
    out_shape += [jax.ShapeDtypeStruct((r, t), dt) for (r, _, dt, _) in touts]
    out_shape += [jax.ShapeDtypeStruct((r, c), F32) for (r, c, _, _) in accs]
    res = pl.pallas_call(
        body,
        name=name,
        grid=(ncol, t // tm),
        in_specs=in_specs,
        out_specs=out_specs,
        out_shape=out_shape,
        compiler_params=_cparams(("arbitrary", "arbitrary") if accs else ("parallel", "parallel")),
    )(*[e[0] for e in ins])
    return res


def _row(arr, w=None, base=0, cd=False):
    return (arr, arr.shape[1] if w is None else w, base, cd, True)


def _const(arr, w=None, base=0, cd=False):
    return (arr, arr.shape[1] if w is None else w, base, cd, False)


def rope_tables(positions):
    t = positions.shape[0]

    def fn(ins, outs, _):
        pos = ins[0][...].astype(F32)
        lane = lax.broadcasted_iota(jnp.int32, (1, LANES), 1)
        log_theta = jnp.log(jnp.float32(ROPE_THETA))
        jm = lane - MLA_NOPE
        idx = jnp.clip(jnp.where(jm < 16, jm, jm - 16), 0, 15).astype(F32)
        ang = pos * jnp.exp(-(idx * (2.0 / MLA_ROPE)) * log_theta)
        cos, sin = jnp.cos(ang), jnp.sin(ang)
        in_rope = (lane >= MLA_NOPE) & (lane < MLA_QK)
        outs[0][:, 0:128] = jnp.where(lane < MLA_NOPE, 1.0, jnp.where(in_rope, cos, 0.0))
        outs[0][:, 128:256] = jnp.where(in_rope & (jm < 16), -sin, 0.0)
        outs[0][:, 256:384] = jnp.where(in_rope & (jm >= 16), sin, 0.0)
        jd = lane & (DIL_HD - 1)
        idx = (jd & 31).astype(F32)
        ang = pos * jnp.exp(-(idx * (2.0 / DIL_HD)) * log_theta)
        cos, sin = jnp.cos(ang), jnp.sin(ang)
        outs[0][:, 384:512] = cos
        outs[0][:, 512:640] = jnp.where(jd < 32, -sin, 0.0)
        outs[0][:, 640:768] = jnp.where(jd >= 32, sin, 0.0)

    return rowwise("rope_tables", fn, t, ins=[_row(positions)], outs=[(768, 768, F32, False)])[0]


def rmsnorm_fwd(x, g):
    def fn(ins, outs, _):
        h = rms(ins[0][...], ins[1][...], D_MODEL)
        outs[0][...] = h.astype(MXU_DTYPE)
        outs[1][...] = h.T.astype(MXU_DTYPE)

    return rowwise("rmsnorm_fwd", fn, x.shape[0], ins=[_row(x), _const(g)], outs=[(D_MODEL, D_MODEL, MXU_DTYPE, False)],
                   touts=[(D_MODEL, D_MODEL, MXU_DTYPE, False)])


def rmsnorm_bwd(x, dh, dres, g):
    def fn(ins, outs, accs):
        _, vjp = jax.vjp(lambda xv, gv: rms(xv, gv, D_MODEL), ins[0][...], ins[3][...])
        dx, dg = vjp(ins[1][...])
        outs[0][...] = ins[2][...] + dx
        accs[0][...] += dg

    return rowwise("rmsnorm_bwd", fn, x.shape[0], ins=[_row(x), _row(dh), _row(dres), _const(g)],
                   outs=[(D_MODEL, D_MODEL, F32, False)], accs=[(1, D_MODEL, D_MODEL, False)])


def loss_head(y, target):
    def fn(ins, outs, accs):
        err = ins[0][...] - ins[1][...]
        outs[0][...] = err * (1.0 / D_MODEL)
        accs[0][...] += jnp.sum(err * err, axis=0, keepdims=True)
        accs[1][...] = jnp.broadcast_to(jnp.sum(accs[0][...], keepdims=True), (1, LANES))

    dy, _, tot = rowwise("loss_head", fn, y.shape[0], ins=[_row(y), _row(target)], outs=[(D_MODEL, D_MODEL, F32, False)],
                         accs=[(1, D_MODEL, D_MODEL, False), (1, LANES, LANES, False)])
    return dy, tot[0, 0] * (0.5 / D_MODEL)


def _shift_rows(v, d, fill, reverse):
    tb = v.shape[0]
    if d % 8 == 0:
        pad = jnp.full((d, v.shape[1]), fill, v.dtype)
        return jnp.concatenate([v[d:], pad] if reverse else [pad, v[:tb - d]], axis=0)
    rows = lax.broadcasted_iota(jnp.int32, v.shape, 0)
    if not reverse:
        return jnp.where(rows >= d, pltpu.roll(v, d, 0), fill)
    return jnp.where(rows < tb - d, pltpu.roll(v, tb - d, 0), fill)


def _scan_tile(a, b, reverse):
    d = 1
    while d < a.shape[0]:
        b = b + a * _shift_rows(b, d, 0.0, reverse)
        a = a * _shift_rows(a, d, 1.0, reverse)
        d *= 2
    return a, b


def _lru_gates(ops, xc, wgx, bgx, wga, bga, lam):
    gx = _sigmoid(ops.mm(xc, wgx) + bgx)
    ga = _sigmoid(ops.mm(xc, wga) + bga)
    log_a = -LRU_C * ga * _softplus(-lam)
    a = jnp.exp(log_a)
    mult = jnp.sqrt(-_expm1(2.0 * log_a))
    return a, mult * (gx * xc)


def _shifted_inputs(x, halo, tb):
    rows = lax.broadcasted_iota(jnp.int32, x.shape, 0)
    pad = jnp.zeros((tb - 8, LANES), F32)
    out = []
    for d in (3, 2, 1):
        head = jnp.concatenate([pltpu.roll(halo, d, 0), pad], axis=0)
        out.append(jnp.where(rows >= d, pltpu.roll(x, d, 0), head))
    return out + [x]


def _lru_specs(nt, tb, reverse):
    hb = tb // 8
    tt = (lambda t: nt - 1 - t) if reverse else (lambda t: t)
    blk = lambda off: pl.BlockSpec((tb, LANES), lambda n, t: (tt(t), n + off))
    halo = lambda off: pl.BlockSpec((8, LANES), lambda n, t: (jnp.maximum(tt(t) * hb - 1, 0), n + off))
    chan = lambda r: pl.BlockSpec((r, LANES), lambda n, t: (0, n))
    wblk = pl.BlockSpec((None, LANES, LANES), lambda n, t: (n, 0, 0))
    bblk = pl.BlockSpec((None, 1, LANES), lambda n, t: (n, 0, 0))
    return blk, halo, chan, wblk, bblk


def lru_fwd(z, conv_w, conv_b, wgx, bgx, wga, bga, lam, *, tb=256):
    t = z.shape[0]
    nt = t // tb
    blk, halo, chan, wblk, bblk = _lru_specs(nt, tb, False)

    def body(x_ref, xh_ref, g_ref, cw_ref, cb_ref, wgx_ref, bgx_ref, wga_ref, bga_ref, lam_ref, h_ref, y_ref, yt_ref, carry_ref):
        ti = pl.program_id(1)

        @pl.when(ti == 0)
        def _():
            carry_ref[...] = jnp.zeros((8, LANES), F32)

        x = x_ref[...]
        hal = jnp.where(ti > 0, xh_ref[...], 0.0)
        xs = _shifted_inputs(x, hal, tb)
        xc = cb_ref[...] + sum(xs[k] * cw_ref[k:k + 1, :] for k in range(4))
        a, b = _lru_gates(PLAIN, xc, wgx_ref[...], bgx_ref[...], wga_ref[...], bga_ref[...], lam_ref[...])
        acum, h0 = _scan_tile(a, b, False)
        h = h0 + acum * carry_ref[7:8, :]
        carry_ref[...] = h[tb - 8:tb, :]
        h_ref[...] = h
        y = h * _silu_and_grad(g_ref[...])[0]
        y_ref[...] = y.astype(MXU_DTYPE)
        yt_ref[...] = y.T.astype(MXU_DTYPE)

    return pl.pallas_call(
        body,
        name="lru_fwd",
        grid=(LRU_BLOCKS, nt),
        in_specs=[blk(0), halo(0), blk(LRU_BLOCKS), chan(4), chan(1), wblk, bblk, wblk, bblk, chan(1)],
        out_specs=[blk(0), blk(0), pl.BlockSpec((LANES, tb), lambda n, t_: (n, t_))],
        out_shape=[jax.ShapeDtypeStruct((t, 1024), F32), jax.ShapeDtypeStruct((t, 1024), MXU_DTYPE),
                   jax.ShapeDtypeStruct((1024, t), MXU_DTYPE)],
        scratch_shapes=[pltpu.VMEM((8, LANES), F32)],
        compiler_params=_cparams(("parallel", "arbitrary")),
    )(z, z, z, conv_w, conv_b, wgx, bgx, wga, bga, lam)


def lru_bwd(z, h, dy, conv_w, conv_b, wgx, bgx, wga, bga, lam, *, tb=256):
    t = z.shape[0]
    nt = t // tb
    blk, halo, chan, wblk, bblk = _lru_specs(nt, tb, True)

    def body(x_ref, xh_ref, g_ref, h_ref, hh_ref, dy_ref, cw_ref, cb_ref, wgx_ref, bgx_ref, wga_ref, bga_ref, lam_ref,
             dzx_ref, dzg_ref, dwgx_ref, dbgx_ref, dwga_ref, dbga_ref, dlam_ref, dcw_ref, dcb_ref,
             gcar_ref, acar_ref, xcar_ref):
        ti = pl.program_id(1)
        has_earlier = ti < nt - 1

        @pl.when(ti == 0)
        def _():
            for ref in (dwgx_ref, dbgx_ref, dwga_ref, dbga_ref, dlam_ref, dcw_ref, dcb_ref, gcar_ref, acar_ref, xcar_ref):
                ref[...] = jnp.zeros(ref.shape, F32)

        rows = lax.broadcasted_iota(jnp.int32, (tb, LANES), 0)
        x = x_ref[...]
        hal = jnp.where(has_earlier, xh_ref[...], 0.0)
        xs = _shifted_inputs(x, hal, tb)
        xc = cb_ref[...] + sum(xs[k] * cw_ref[k:k + 1, :] for k in range(4))
        (a, _), vjp = jax.vjp(functools.partial(_lru_gates, DIFF), xc, wgx_ref[...], bgx_ref[...], wga_ref[...],
                              bga_ref[...], lam_ref[...])
        g, h, dyv = g_ref[...], h_ref[...], dy_ref[...]
        silu, dsilu = _silu_and_grad(g)
        dzg_ref[...] = (dyv * h * dsilu).astype(MXU_DTYPE)
        a_next = jnp.where(rows < tb - 1, pltpu.roll(a, tb - 1, 0), acar_ref[0:1, :])
        acum, g0 = _scan_tile(a_next, dyv * silu, True)
        gt = g0 + acum * gcar_ref[0:1, :]
        h_prev = jnp.where(rows >= 1, pltpu.roll(h, 1, 0), jnp.where(has_earlier, hh_ref[7:8, :], 0.0))
        dxc, dwgx, dbgx, dwga, dbga, dlam = vjp((gt * h_prev, gt))
        later = xcar_ref[...]
        gcar_ref[...] = gt[0:8, :]
        acar_ref[...] = a[0:8, :]
        xcar_ref[...] = dxc[0:8, :]
        dx = dxc * cw_ref[3:4, :]
        pad = jnp.zeros((tb - 8, LANES), F32)
        for d in (1, 2, 3):
            tail = jnp.concatenate([pad, pltpu.roll(later, 8 - d, 0)], axis=0)
            up = jnp.where(rows < tb - d, pltpu.roll(dxc, tb - d, 0), tail)
            dx = dx + up * cw_ref[3 - d:4 - d, :]
        dzx_ref[...] = dx.astype(MXU_DTYPE)
        for k in range(4):
            dcw_ref[k:k + 1, :] += jnp.sum(dxc * xs[k], axis=0, keepdims=True)
        dcb_ref[...] += jnp.sum(dxc, axis=0, keepdims=True)
        dwgx_ref[...] += dwgx
        dbgx_ref[...] += dbgx
        dwga_ref[...] += dwga
        dbga_ref[...] += dbga
        dlam_ref[...] += dlam

    sds = jax.ShapeDtypeStruct
    return pl.pallas_call(
        body,
        name="lru_bwd",
        grid=(LRU_BLOCKS, nt),
        in_specs=[blk(0), halo(0), blk(LRU_BLOCKS), blk(0), halo(0), blk(0), chan(4), chan(1), wblk, bblk, wblk, bblk, chan(1)],
        out_specs=[blk(0), blk(0), wblk, bblk, wblk, bblk, chan(1), chan(4), chan(1)],
        out_shape=[sds((t, 1024), MXU_DTYPE), sds((t, 1024), MXU_DTYPE), sds(wgx.shape, F32), sds(bgx.shape, F32), sds(wga.shape, F32),
                   sds(bga.shape, F32), sds((1, 1024), F32), sds((4, 1024), F32), sds((1, 1024), F32)],
        scratch_shapes=[pltpu.VMEM((8, LANES), F32)] * 3,
        compiler_params=_cparams(("parallel", "arbitrary")),
    )(z, z, z, h, h, dy, conv_w, conv_b, wgx, bgx, wga, bga, lam)


def _mla_prep_tile(ops, cq, ckv, krp, c, s1, s2, g_cq, g_ckv, wq, wk, wv, gq, gk):
    cqn = rms(cq, g_cq, 256)
    ckvn = rms(ckv, g_ckv, 128)
    v = ops.mm(ckvn, wv)
    qs, ks = [], []
    for hd in range(MLA_HEADS):
        q = rms(ops.mm(cqn, wq[hd]), gq, MLA_QK)
        k = rms(ops.mm(ckvn, wk[hd]) + krp, gk, MLA_QK)
        qs.append(rope(ops, q, c, s1, s2, 16) * (MLA_QK ** -0.5))
        ks.append(rope(ops, k, c, s1, s2, 16))
    return tuple(qs), tuple(ks), v


def _mla_prep_args(ins):
    z_cq, z_ckv, z_kr, tc, ts1, ts2, g_cq, g_ckv, wq, wk, wv, gq, gk = ins[:13]
    heads = lambda w: tuple(w[:, LANES * hd:LANES * (hd + 1)] for hd in range(MLA_HEADS))
    return (z_cq[...], z_ckv[...], z_kr[...], tc[...], ts1[...], ts2[...], g_cq[...], g_ckv[...], heads(wq), heads(wk),
            wv[...], gq[...], gk[...])


def _mla_prep_ins(z, tabs, w):
    return [_row(z, 256, 0), _row(z, 128, 2), _row(z, 128, 3), _row(tabs, 128, 0), _row(tabs, 128, 1), _row(tabs, 128, 2),
            _const(w["cq_norm_g"]), _const(w["ckv_norm_g"]), _const(w["wq"]), _const(w["wk"]), _const(w["wv"]),
            _const(w["gq"]), _const(w["gk"])]


def mla_prep_fwd(z, tabs, w):
    def fn(ins, outs, _):
        qs, ks, v = _mla_prep_tile(PLAIN, *_mla_prep_args(ins))
        for hd in range(MLA_HEADS):
            outs[0][:, LANES * hd:LANES * (hd + 1)] = qs[hd].astype(MXU_DTYPE)
            outs[1][:, LANES * hd:LANES * (hd + 1)] = ks[hd].astype(MXU_DTYPE)
        outs[2][...] = v.astype(MXU_DTYPE)
        outs[3][...] = v.T.astype(MXU_DTYPE)

    return rowwise("mla_prep_fwd", fn, z.shape[0], ins=_mla_prep_ins(z, tabs, w),
                   outs=[(1024, 1024, MXU_DTYPE, False), (1024, 1024, MXU_DTYPE, False), (512, 512, MXU_DTYPE, False)],
                   touts=[(512, 512, MXU_DTYPE, False)])


def mla_prep_bwd(z, tabs, w, dq, dk, dv):
    def fn(ins, outs, accs):
        args = _mla_prep_args(ins)
        _, vjp = jax.vjp(functools.partial(_mla_prep_tile, DIFF), *args)
        heads = lambda ref: tuple(ref[:, LANES * hd:LANES * (hd + 1)] for hd in range(MLA_HEADS))
        dcq, dckv, dkr, _, _, _, dg_cq, dg_ckv, dwq, dwk, dwv, dgq, dgk = vjp((heads(ins[13]), heads(ins[14]), ins[15][...]))
        lane = lax.broadcasted_iota(jnp.int32, (1, LANES), 1)
        outs[0][:, 0:256] = dcq.astype(MXU_DTYPE)
        outs[0][:, 256:384] = dckv.astype(MXU_DTYPE)
        outs[0][:, 384:512] = jnp.where((lane >= MLA_NOPE) & (lane < MLA_QK), dkr, 0.0).astype(MXU_DTYPE)
        accs[0][...] += dg_cq
        accs[1][...] += dg_ckv
        for hd in range(MLA_HEADS):
            accs[2][:, LANES * hd:LANES * (hd + 1)] += dwq[hd]
            accs[3][:, LANES * hd:LANES * (hd + 1)] += dwk[hd]
        accs[4][...] += dwv
        accs[5][...] += dgq
        accs[6][...] += dgk

    return rowwise("mla_prep_bwd", fn, z.shape[0], ins=_mla_prep_ins(z, tabs, w) + [_row(dq), _row(dk), _row(dv)],
                   outs=[(512, 512, MXU_DTYPE, False)],
                   accs=[(1, 256, 256, False), (1, 128, 128, False), (256, 1024, 1024, False), (128, 1024, 1024, False),
                         (128, 512, 512, False), (1, 128, 128, False), (1, 128, 128, False)])


def _head_masks():
    lane = lax.broadcasted_iota(jnp.int32, (1, LANES), 1)
    return (lane < DIL_HD, lane >= DIL_HD)


def _row_scalar(tile, mask):
    return jnp.max(jnp.where(mask, tile, -jnp.inf), axis=-1, keepdims=True)


def _causal_tiles(nq, by_key):
    pairs = [(i, j) for i in range(nq) for j in range(i + 1)]
    if by_key:
        pairs.sort(key=lambda ij: (ij[1], ij[0]))
    return (jnp.asarray([ij[0] for ij in pairs], jnp.int32), jnp.asarray([ij[1] for ij in pairs], jnp.int32))


def mla_attn_fwd(q, k, vt, z, *, tq=256, gather=None):
    t = q.shape[0]
    nq = t // tq
    it, jt = _causal_tiles(nq, False)
    hps, wq, wv = MLA_HPS, LANES * MLA_HPS, 64 * MLA_HPS
    ws, layer = gather if gather is not None else ([], 0)
    ng = len(ws)
    ngrp, nsteps = MLA_HEADS // hps, int(it.shape[0])

    def body(it_ref, jt_ref, q_ref, k_ref, vt_ref, g_ref, *rest):
        w_refs, rest = rest[:ng], rest[ng:]
        o_ref, y_ref, yt_ref, lse_ref = rest[:4]
        ga_refs, rest = rest[4:4 + ng], rest[4 + ng:]
        m_scr, l_scr, acc_scr = rest[:3]
        step = pl.program_id(1)
        i, j = it_ref[step], jt_ref[step]
        if ng:
            gather_start, gather_finish = _layer_gather(w_refs, ga_refs, rest[3], rest[4], layer)

            @pl.when((pl.program_id(0) == 0) & (step == 0))
            def _():
                gather_start()

        @pl.when(j == 0)
        def _():
            m_scr[...] = jnp.full(m_scr.shape, NEG, F32)
            l_scr[...] = jnp.zeros(l_scr.shape, F32)
            acc_scr[...] = jnp.zeros(acc_scr.shape, F32)

        def update(diagonal):
            heads = range(hps)
            lanes = [slice(LANES * hh, LANES * (hh + 1)) for hh in heads]
            rows = [slice(64 * hh, 64 * (hh + 1)) for hh in heads]
            sts = [_dot(k_ref[:, lanes[hh]], q_ref[:, lanes[hh]], NT) for hh in heads]
            m_prev = [m_scr[hh:hh + 1, :] for hh in heads]
            l_prev = [l_scr[hh:hh + 1, :] for hh in heads]
            acc_prev = [acc_scr[rows[hh], :] for hh in heads]
            m_new, l_new, acc_new = [], [], []
            for hh in heads:
                st = sts[hh]
                if diagonal:
                    key = lax.broadcasted_iota(jnp.int32, (tq, tq), 0)
                    qry = lax.broadcasted_iota(jnp.int32, (tq, tq), 1)
                    st = jnp.where(key <= qry, st, NEG)
                m_new.append(jnp.maximum(m_prev[hh], jnp.max(st, axis=0, keepdims=True)))
                alpha = jnp.exp(m_prev[hh] - m_new[hh])
                pt = jnp.exp(st - m_new[hh])
                l_new.append(alpha * l_prev[hh] + jnp.sum(pt, axis=0, keepdims=True))
                acc_new.append(alpha * acc_prev[hh] + _dot(vt_ref[rows[hh], :], pt, NN))
            for hh in heads:
                m_scr[hh:hh + 1, :] = m_new[hh]
                l_scr[hh:hh + 1, :] = l_new[hh]
                acc_scr[rows[hh], :] = acc_new[hh]

        @pl.when(j < i)
        def _():
            update(False)

        @pl.when(j == i)
        def _():
            update(True)
            lse_ref[...] = jnp.zeros(lse_ref.shape, F32)
            for hh in range(hps):
                rows = slice(64 * hh, 64 * (hh + 1))
                acc_scr[rows, :] = acc_scr[rows, :] / l_scr[hh:hh + 1, :]
                lse_ref[hh:hh + 1, :] = m_scr[hh:hh + 1, :] + jnp.log(l_scr[hh:hh + 1, :])
            o = acc_scr[...].T
            o_ref[...] = o
            y = o * _silu_and_grad(g_ref[...])[0]
            y_ref[...] = y.astype(MXU_DTYPE)
            yt_ref[...] = y.T.astype(MXU_DTYPE)

        if ng:
            @pl.when((pl.program_id(0) == ngrp - 1) & (step == nsteps - 1))
            def _():
                gather_finish()

    qo = lambda w, off=0: pl.BlockSpec((tq, w), lambda p, s, it_, jt_: (it_[s], p + off))
    sds = jax.ShapeDtypeStruct
    comm_scratch = [pltpu.SemaphoreType.DMA((6 * ng,)), pltpu.SemaphoreType.DMA((6 * ng,))] if ng else []
    return pl.pallas_call(
        body,
        name="mla_attn_fwd_gather" if ng else "mla_attn_fwd",
        grid_spec=pltpu.PrefetchScalarGridSpec(
            num_scalar_prefetch=2,
            grid=(ngrp, nsteps),
            in_specs=[qo(wq), pl.BlockSpec((tq, wq), lambda p, s, it_, jt_: (jt_[s], p)),
                      pl.BlockSpec((wv, tq), lambda p, s, it_, jt_: (p, jt_[s])), qo(wv, 512 // wv)] + [HBM_SPEC] * ng,
            out_specs=[qo(wv), qo(wv), pl.BlockSpec((wv, tq), lambda p, s, it_, jt_: (p, it_[s])),
                       pl.BlockSpec((None, 8, tq), lambda p, s, it_, jt_: (p, 0, it_[s]))] + [HBM_SPEC] * ng,
            scratch_shapes=[pltpu.VMEM((8, tq), F32), pltpu.VMEM((8, tq), F32), pltpu.VMEM((wv, tq), F32)] + comm_scratch,
        ),
        out_shape=[sds((t, 512), F32), sds((t, 512), MXU_DTYPE), sds((512, t), MXU_DTYPE), sds((ngrp, 8, t), F32)]
        + _gather_shapes(ws),
        compiler_params=_cparams(("arbitrary", "arbitrary") if ng else ("parallel", "arbitrary")),
    )(it, jt, q, k, vt, z, *ws)


def mla_attn_bwd(q, k, v, do, lse, dd, *, tq=256):
    t = q.shape[0]
    nq = t // tq
    it, jt = _causal_tiles(nq, True)
    hps, wq, wv = MLA_HPS, LANES * MLA_HPS, 64 * MLA_HPS

    def body(it_ref, jt_ref, q_ref, k_ref, v_ref, do_ref, lse_ref, d_ref, dq_ref, dk_ref, dv_ref, dk_scr, dv_scr):
        step = pl.program_id(1)
        i, j = it_ref[step], jt_ref[step]
        masks = _head_masks()

        @pl.when(step == 0)
        def _():
            dq_ref[...] = jnp.zeros(dq_ref.shape, F32)

        @pl.when(i == j)
        def _():
            dk_scr[...] = jnp.zeros(dk_scr.shape, F32)
            dv_scr[...] = jnp.zeros(dv_scr.shape, F32)

        def update(diagonal):
            qrows = pl.ds(pl.multiple_of(i * tq, tq), tq)
            heads = range(hps)
            lanes = [slice(LANES * hh, LANES * (hh + 1)) for hh in heads]
            pair = [slice(LANES * (hh // 2), LANES * (hh // 2 + 1)) for hh in heads]
            qh = [q_ref[:, lanes[hh]] for hh in heads]
            kh = [k_ref[:, lanes[hh]] for hh in heads]
            doh = []
            for hh in heads:
                dov = do_ref[:, pair[hh]]
                doh.append(jnp.where(masks[hh % 2], dov, jnp.zeros_like(dov)))
            sts = [_dot(kh[hh], qh[hh], NT) for hh in heads]
            dps = [_dot(v_ref[:, pair[hh]], doh[hh], NT) for hh in heads]
            lse = [lse_ref[hh:hh + 1, :] for hh in heads]
            ddv = [d_ref[hh:hh + 1, :] for hh in heads]
            dk_new = [dk_scr[:, lanes[hh]] for hh in heads]
            dq_new = [dq_ref[qrows, lanes[hh]] for hh in heads]
            dv_new = [dv_scr[:, pair[2 * pp]] for pp in range(hps // 2)]
            for hh in heads:
                st = sts[hh] - lse[hh]
                if diagonal:
                    key = lax.broadcasted_iota(jnp.int32, (tq, tq), 0)
                    qry = lax.broadcasted_iota(jnp.int32, (tq, tq), 1)
                    st = jnp.where(key <= qry, st, NEG)
                pt = jnp.exp(st)
                dst = pt * (dps[hh] - ddv[hh])
                dv_new[hh // 2] = dv_new[hh // 2] + _dot(pt, doh[hh], NN)
                dk_new[hh] = dk_new[hh] + _dot(dst, qh[hh], NN)
                dq_new[hh] = dq_new[hh] + _dot(dst, kh[hh], TN)
            for hh in heads:
                dk_scr[:, lanes[hh]] = dk_new[hh]
                dq_ref[qrows, lanes[hh]] = dq_new[hh]
            for pp in range(hps // 2):
                dv_scr[:, pair[2 * pp]] = dv_new[pp]

        @pl.when(j < i)
        def _():
            update(False)

        @pl.when(j == i)
        def _():
            update(True)

        @pl.when(i == nq - 1)
        def _():
            dk_ref[...] = dk_scr[...]
            dv_ref[...] = dv_scr[...]

    qo = lambda w: pl.BlockSpec((tq, w), lambda p, s, it_, jt_: (it_[s], p))
    kv = lambda w: pl.BlockSpec((tq, w), lambda p, s, it_, jt_: (jt_[s], p))
    stat = pl.BlockSpec((None, 8, tq), lambda p, s, it_, jt_: (p, 0, it_[s]))
    sds = jax.ShapeDtypeStruct
    return pl.pallas_call(
        body,
        name="mla_attn_bwd",
        grid_spec=pltpu.PrefetchScalarGridSpec(
            num_scalar_prefetch=2,
            grid=(MLA_HEADS // hps, it.shape[0]),
            in_specs=[qo(wq), kv(wq), kv(wv), qo(wv), stat, stat],
            out_specs=[pl.BlockSpec((t, wq), lambda p, s, it_, jt_: (0, p)), kv(wq), kv(wv)],
            scratch_shapes=[pltpu.VMEM((tq, wq), F32), pltpu.VMEM((tq, wv), F32)],
        ),
        out_shape=[sds((t, 1024), F32), sds((t, 1024), F32), sds((t, 512), F32)],
        compiler_params=_cparams(("parallel", "arbitrary")),
    )(it, jt, q, k, v, do, lse, dd)


def mla_gate_bwd(dy, o, z, *, tm=256):
    t = dy.shape[0]
    wv = 64 * MLA_HPS

    def body(dy_ref, o_ref, g_ref, do_ref, dzg_ref, dd_ref):
        dyv, ov = dy_ref[...], o_ref[...]
        silu, dsilu = _silu_and_grad(g_ref[...])
        do = dyv * silu
        do_ref[...] = do.astype(MXU_DTYPE)
        dzg_ref[...] = (dyv * ov * dsilu).astype(MXU_DTYPE)
        prod = do * ov
        row = lax.broadcasted_iota(jnp.int32, (8, wv), 0)
        lane = lax.broadcasted_iota(jnp.int32, (8, wv), 1)
        pick = ((lane >= row * DIL_HD) & (lane < (row + 1) * DIL_HD)).astype(BF16)
        hi = prod.astype(BF16)
        r1 = prod - hi.astype(F32)
        mid = r1.astype(BF16)
        lo = (r1 - mid.astype(F32)).astype(BF16)
        dot = lambda u: lax.dot_general(pick, u, NT, preferred_element_type=F32)
        dd_ref[...] = dot(hi) + dot(mid) + dot(lo)

    blk = lambda off=0: pl.BlockSpec((tm, wv), lambda p, i: (i, p + off))
    sds = jax.ShapeDtypeStruct
    return pl.pallas_call(
        body,
        name="mla_gate_bwd",
        grid=(MLA_HEADS // MLA_HPS, t // tm),
        in_specs=[blk(), blk(), blk(512 // wv)],
        out_specs=[blk(), blk(), pl.BlockSpec((None, 8, tm), lambda p, i: (p, 0, i))],
        out_shape=[sds((t, 512), MXU_DTYPE), sds((t, 512), MXU_DTYPE), sds((MLA_HEADS // MLA_HPS, 8, t), F32)],
        compiler_params=_cparams(("parallel", "parallel")),
    )(dy, o, z)


SPAN = 2048
DIL_SCALE = DIL_HD ** -0.5


def _rm_src(u, window, dil):
    nn, r = divmod(u, dil)
    return pl.ds(nn * window + r, DIL_NK, stride=dil) if dil > 1 else pl.ds(u * DIL_NK, DIL_NK)


def _rm_dst(u):
    return pl.ds(u * DIL_NK, DIL_NK)


def _dil_prep_tile(ops, x, g2, c, s1, s2, scale):
    ms = ops.seg_sum(x * x) * (1.0 / DIL_HD)
    return rope(ops, x * lax.rsqrt(ms + EPS) * g2, c, s1, s2, 32) * scale


def _span_blk(base):
    return pl.BlockSpec((SPAN, LANES), lambda s, p: (s, base + p))


def _const_blk(shape):
    return pl.BlockSpec(shape, lambda s, p: (0,) * len(shape))


_DIL_TABLE_SPECS = [pl.BlockSpec((SPAN, LANES), (lambda s, p, blk=blk: (s, blk))) for blk in (3, 4, 5)]


def dil_prep_fwd(z, tabs, gq2, gk2, gi):
    window, dil = DIL_GROUPS[gi]
    t = z.shape[0]

    def body(q_ref, k_ref, v_ref, c_ref, s1_ref, s2_ref, gq_ref, gk_ref, qo_ref, ko_ref, vo_ref):
        for u in range(SPAN // DIL_NK):
            src, dst = _rm_src(u, window, dil), _rm_dst(u)
            c, s1, s2 = c_ref[src, :], s1_ref[src, :], s2_ref[src, :]
            qo_ref[dst, :] = _dil_prep_tile(PLAIN, q_ref[src, :], gq_ref[...], c, s1, s2, DIL_SCALE).astype(MXU_DTYPE)
            ko_ref[dst, :] = _dil_prep_tile(PLAIN, k_ref[src, :], gk_ref[...], c, s1, s2, 1.0).astype(MXU_DTYPE)
            vo_ref[dst, :] = v_ref[src, :].astype(MXU_DTYPE)

    return pl.pallas_call(
        body,
        name=f"dil_prep_fwd_{dil}",
        grid=(t // SPAN, 4),
        in_specs=[_span_blk(4 * gi), _span_blk(12 + 4 * gi), _span_blk(24 + 4 * gi)] + _DIL_TABLE_SPECS
        + [_const_blk((1, LANES)), _const_blk((1, LANES))],
        out_specs=[_span_blk(0)] * 3,
        out_shape=[jax.ShapeDtypeStruct((t, 512), MXU_DTYPE)] * 3,
        compiler_params=_cparams(("parallel", "parallel")),
    )(z, z, z, tabs, tabs, tabs, gq2, gk2)


def dil_prep_bwd(z, tabs, gq2, gk2, dq, dk, dv, gi):
    window, dil = DIL_GROUPS[gi]
    t = z.shape[0]

    def body(q_ref, k_ref, c_ref, s1_ref, s2_ref, gq_ref, gk_ref, dq_ref, dk_ref, dv_ref, dzq_ref, dzk_ref, dzv_ref, dgq_ref,
             dgk_ref, sq, sk, sv):
        @pl.when((pl.program_id(0) == 0) & (pl.program_id(1) == 0))
        def _():
            dgq_ref[...] = jnp.zeros((1, LANES), F32)
            dgk_ref[...] = jnp.zeros((1, LANES), F32)

        dgs = [jnp.zeros((1, LANES), F32), jnp.zeros((1, LANES), F32)]
        for u in range(SPAN // DIL_NK):
            src, dst = _rm_src(u, window, dil), _rm_dst(u)
            c, s1, s2 = c_ref[src, :], s1_ref[src, :], s2_ref[src, :]
            for idx, (x_ref, g_ref, ct_ref, scr, scale) in enumerate(((q_ref, gq_ref, dq_ref, sq, DIL_SCALE),
                                                                      (k_ref, gk_ref, dk_ref, sk, 1.0))):
                _, vjp = jax.vjp(lambda xv, gv, sc=scale: _dil_prep_tile(DIFF, xv, gv, c, s1, s2, sc), x_ref[src, :], g_ref[...])
                dx, dg = vjp(ct_ref[dst, :])
                scr[src, :] = dx
                dgs[idx] = dgs[idx] + dg
            sv[src, :] = dv_ref[dst, :]
        dgq_ref[...] += dgs[0] + pltpu.roll(dgs[0], DIL_HD, 1)
        dgk_ref[...] += dgs[1] + pltpu.roll(dgs[1], DIL_HD, 1)
        for c0 in range(0, SPAN, 256):
            rows = slice(c0, c0 + 256)
            dzq_ref[rows, :] = sq[rows, :].astype(MXU_DTYPE)
            dzk_ref[rows, :] = sk[rows, :].astype(MXU_DTYPE)
            dzv_ref[rows, :] = sv[rows, :].astype(MXU_DTYPE)

    sds = jax.ShapeDtypeStruct
    return pl.pallas_call(
        body,
        name=f"dil_prep_bwd_{dil}",
        grid=(t // SPAN, 4),
        in_specs=[_span_blk(4 * gi), _span_blk(12 + 4 * gi)] + _DIL_TABLE_SPECS
        + [_const_blk((1, LANES)), _const_blk((1, LANES)), _span_blk(0), _span_blk(0), _span_blk(0)],
        out_specs=[_span_blk(0)] * 3 + [_const_blk((1, LANES))] * 2,
        out_shape=[sds((t, 512), MXU_DTYPE)] * 3 + [sds((1, LANES), F32)] * 2,
        scratch_shapes=[pltpu.VMEM((SPAN, LANES), F32)] * 3,
        compiler_params=_cparams(("arbitrary", "arbitrary")),
    )(z, z, tabs, tabs, tabs, gq2, gk2, dq, dk, dv)


def _band_masks():
    qi = lax.broadcasted_iota(jnp.int32, (DIL_NK, DIL_NK), 0)
    ki = lax.broadcasted_iota(jnp.int32, (DIL_NK, DIL_NK), 1)
    return (ki >= qi), (ki <= qi)


def _pair_heads():
    return [(pair, hh) for pair in range(4) for hh in range(2)]


def _pair_lanes(pair):
    return slice(LANES * pair, LANES * (pair + 1))


def _zero_other_head(mask, x):
    return jnp.where(mask, x, jnp.zeros_like(x))


def dil_attn_fwd(name, q, k, v, dil):
    t = q.shape[0]

    def body(q_ref, kp_ref, kc_ref, vp_ref, vc_ref, o_ref, lse_ref):
        b = pl.program_id(0)
        mprev, mcur = _band_masks()
        mprev = mprev & (b >= dil)
        hm = _head_masks()
        heads = _pair_heads()
        qh = [_zero_other_head(hm[hh], q_ref[:, _pair_lanes(pair)]) for pair, hh in heads]
        sps = [_dot(qh[i], kp_ref[:, _pair_lanes(pair)], NT) for i, (pair, _) in enumerate(heads)]
        scs = [_dot(qh[i], kc_ref[:, _pair_lanes(pair)], NT) for i, (pair, _) in enumerate(heads)]
        o = [jnp.zeros((DIL_NK, LANES), F32) for _ in range(4)]
        lse = [jnp.zeros((DIL_NK, LANES), F32) for _ in range(4)]
        for i, (pair, hh) in enumerate(heads):
            sp, sc = jnp.where(mprev, sps[i], NEG), jnp.where(mcur, scs[i], NEG)
            m = jnp.maximum(jnp.max(sp, axis=1, keepdims=True), jnp.max(sc, axis=1, keepdims=True))
            ep, ec = jnp.exp(sp - m), jnp.exp(sc - m)
            den = jnp.sum(ep, axis=1, keepdims=True) + jnp.sum(ec, axis=1, keepdims=True)
            lanes = _pair_lanes(pair)
            oh = _dot(ep, _zero_other_head(hm[hh], vp_ref[:, lanes]), NN) + _dot(ec, _zero_other_head(hm[hh], vc_ref[:, lanes]), NN)
            o[pair] = o[pair] + oh / den
            lse[pair] = jnp.where(hm[hh], m + jnp.log(den), lse[pair])
        for pair in range(4):
            o_ref[:, _pair_lanes(pair)] = o[pair]
            lse_ref[:, _pair_lanes(pair)] = lse[pair]

    cur = pl.BlockSpec((DIL_NK, 512), lambda b: (b, 0))
    prev = pl.BlockSpec((DIL_NK, 512), lambda b: (jnp.maximum(b - dil, 0), 0))
    sds = jax.ShapeDtypeStruct
    return pl.pallas_call(
        body,
        name=name,
        grid=(t // DIL_NK,),
        in_specs=[cur, prev, cur, prev, cur],
        out_specs=[cur, cur],
        out_shape=[sds((t, 512), F32), sds((t, 512), F32)],
        compiler_params=_cparams(("parallel",)),
    )(q, k, k, v, v)


def dil_attn_bwd(name, q, k, v, do, lse, dd, dil):
    t = q.shape[0]
    nblk = t // DIL_NK

    def body(qc_ref, qn_ref, doc_ref, don_ref, lc_ref, ln_ref, dc_ref, dn_ref, kp_ref, kc_ref, vp_ref, vc_ref,
             dq_ref, dk_ref, dv_ref):
        b = pl.program_id(0)
        mprev, mcur = _band_masks()
        mnext = mprev & (b + dil < nblk)
        mprev = mprev & (b >= dil)
        hm = _head_masks()
        heads = _pair_heads()
        ln_ = [_pair_lanes(pair) for pair, _ in heads]
        qh = [_zero_other_head(hm[hh], qc_ref[:, ln_[i]]) for i, (_, hh) in enumerate(heads)]
        qnh = [_zero_other_head(hm[hh], qn_ref[:, ln_[i]]) for i, (_, hh) in enumerate(heads)]
        doh = [_zero_other_head(hm[hh], doc_ref[:, ln_[i]]) for i, (_, hh) in enumerate(heads)]
        donh = [_zero_other_head(hm[hh], don_ref[:, ln_[i]]) for i, (_, hh) in enumerate(heads)]
        idx = range(len(heads))
        s_p = [_dot(qh[i], kp_ref[:, ln_[i]], NT) for i in idx]
        s_c = [_dot(qh[i], kc_ref[:, ln_[i]], NT) for i in idx]
        s_n = [_dot(qnh[i], kc_ref[:, ln_[i]], NT) for i in idx]
        dp_p = [_dot(doh[i], vp_ref[:, ln_[i]], NT) for i in idx]
        dp_c = [_dot(doh[i], vc_ref[:, ln_[i]], NT) for i in idx]
        dp_n = [_dot(donh[i], vc_ref[:, ln_[i]], NT) for i in idx]
        dq = [jnp.zeros((DIL_NK, LANES), F32) for _ in range(4)]
        dk = [jnp.zeros((DIL_NK, LANES), F32) for _ in range(4)]
        dv = [jnp.zeros((DIL_NK, LANES), F32) for _ in range(4)]
        for i, (pair, hh) in enumerate(heads):
            lse_h, d_h = _row_scalar(lc_ref[:, ln_[i]], hm[hh]), _row_scalar(dc_ref[:, ln_[i]], hm[hh])
            lse_n, d_n = _row_scalar(ln_ref[:, ln_[i]], hm[hh]), _row_scalar(dn_ref[:, ln_[i]], hm[hh])
            pp = jnp.exp(jnp.where(mprev, s_p[i] - lse_h, NEG))
            pc = jnp.exp(jnp.where(mcur, s_c[i] - lse_h, NEG))
            pn = jnp.exp(jnp.where(mnext, s_n[i] - lse_n, NEG))
            dsp, dsc, dsn = pp * (dp_p[i] - d_h), pc * (dp_c[i] - d_h), pn * (dp_n[i] - d_n)
            dq[pair] = (dq[pair] + _dot(dsp, _zero_other_head(hm[hh], kp_ref[:, ln_[i]]), NN)
                        + _dot(dsc, _zero_other_head(hm[hh], kc_ref[:, ln_[i]]), NN))
            dv[pair] = dv[pair] + _dot(pc, doh[i], TN) + _dot(pn, donh[i], TN)
            dk[pair] = dk[pair] + _dot(dsc, qh[i], TN) + _dot(dsn, qnh[i], TN)
        for pair in range(4):
            dq_ref[:, _pair_lanes(pair)] = dq[pair]
            dk_ref[:, _pair_lanes(pair)] = dk[pair]
            dv_ref[:, _pair_lanes(pair)] = dv[pair]

    cur = pl.BlockSpec((DIL_NK, 512), lambda b: (b, 0))
    prev = pl.BlockSpec((DIL_NK, 512), lambda b: (jnp.maximum(b - dil, 0), 0))
    nxt = pl.BlockSpec((DIL_NK, 512), lambda b: (jnp.minimum(b + dil, nblk - 1), 0))
    sds = jax.ShapeDtypeStruct
    return pl.pallas_call(
        body,
        name=name,
        grid=(nblk,),
        in_specs=[cur, nxt, cur, nxt, cur, nxt, cur, nxt, prev, cur, prev, cur],
        out_specs=[cur, cur, cur],
        out_shape=[sds((t, 512), F32)] * 3,
        compiler_params=_cparams(("parallel",)),
    )(q, q, do, do, lse, lse, dd, dd, k, k, v, v)


def dil_combine(os_, lses, z):
    t = z.shape[0]

    def body(o1_ref, l1_ref, o2_ref, l2_ref, o3_ref, l3_ref, g_ref, y_ref, o_ref, lse_ref, yt_ref, so2, sl2, so3, sl3):
        for (window, dil), o_in, l_in, so, sl in ((DIL_GROUPS[1], o2_ref, l2_ref, so2, sl2), (DIL_GROUPS[2], o3_ref, l3_ref, so3, sl3)):
            for u in range(SPAN // DIL_NK):
                src, dst = _rm_src(u, window, dil), _rm_dst(u)
                so[src, :] = o_in[dst, :]
                sl[src, :] = l_in[dst, :]
        for c0 in range(0, SPAN, 256):
            rows = slice(c0, c0 + 256)
            la, lb, lc = l1_ref[rows, :], sl2[rows, :], sl3[rows, :]
            mx = jnp.maximum(jnp.maximum(la, lb), lc)
            wa, wb, wc = jnp.exp(la - mx), jnp.exp(lb - mx), jnp.exp(lc - mx)
            tot = wa + wb + wc
            o = (wa * o1_ref[rows, :] + wb * so2[rows, :] + wc * so3[rows, :]) / tot
            y = o * _silu_and_grad(g_ref[rows, :])[0]
            y_ref[rows, :] = y.astype(MXU_DTYPE)
            o_ref[rows, :] = o
            lse_ref[rows, :] = mx + jnp.log(tot)
            yt_ref[:, rows] = y.T.astype(MXU_DTYPE)

    sds = jax.ShapeDtypeStruct
    return pl.pallas_call(
        body,
        name="dil_combine",
        grid=(t // SPAN, 4),
        in_specs=[_span_blk(0)] * 6 + [_span_blk(36)],
        out_specs=[_span_blk(0)] * 3 + [pl.BlockSpec((LANES, SPAN), lambda s, p: (p, s))],
        out_shape=[sds((t, 512), MXU_DTYPE), sds((t, 512), F32), sds((t, 512), F32), sds((512, t), MXU_DTYPE)],
        scratch_shapes=[pltpu.VMEM((SPAN, LANES), F32)] * 4,
        compiler_params=_cparams(("parallel", "parallel")),
    )(os_[0], lses[0], os_[1], lses[1], os_[2], lses[2], z)


def dil_gate_bwd(dy, o, z, lse):
    t = dy.shape[0]

    def body(dy_ref, o_ref, g_ref, lse_ref, dzg_ref, do1_ref, dd1_ref, do2_ref, dd2_ref, l2_ref, do3_ref, dd3_ref, l3_ref, do_scr):
        for c0 in range(0, SPAN, 256):
            rows = slice(c0, c0 + 256)
            dyv, ov = dy_ref[rows, :], o_ref[rows, :]
            silu, dsilu = _silu_and_grad(g_ref[rows, :])
            do = dyv * silu
            do_scr[rows, :] = do
            do1_ref[rows, :] = do.astype(MXU_DTYPE)
            dd1_ref[rows, :] = _seg_sum_impl(do * ov)
            dzg_ref[rows, :] = (dyv * ov * dsilu).astype(MXU_DTYPE)
        for (window, dil), do_o, dd_o, l_o in ((DIL_GROUPS[1], do2_ref, dd2_ref, l2_ref), (DIL_GROUPS[2], do3_ref, dd3_ref, l3_ref)):
            for u in range(SPAN // DIL_NK):
                src, dst = _rm_src(u, window, dil), _rm_dst(u)
                do_o[dst, :] = do_scr[src, :].astype(MXU_DTYPE)
                dd_o[dst, :] = dd1_ref[src, :]
                l_o[dst, :] = lse_ref[src, :]

    sds = jax.ShapeDtypeStruct
    f32, mxu = sds((t, 512), F32), sds((t, 512), MXU_DTYPE)
    return pl.pallas_call(
        body,
        name="dil_gate_bwd",
        grid=(t // SPAN, 4),
        in_specs=[_span_blk(0), _span_blk(0), _span_blk(36), _span_blk(0)],
        out_specs=[_span_blk(0)] * 9,
        out_shape=[mxu, mxu, f32, mxu, f32, f32, mxu, f32, f32],
        scratch_shapes=[pltpu.VMEM((SPAN, LANES), F32)],
        compiler_params=_cparams(("parallel", "parallel")),
    )(dy, o, z, lse)


def _merge_tile(p0, p1, p2, z0, z1, z2, b0, b1, b2):
    return _sigmoid(z0 + b0) * p0 + _sigmoid(z1 + b1) * p1 + _sigmoid(z2 + b2) * p2


def _merge_ins(ps, z, b):
    w = 256
    return ([_row(p, w, 0, True) for p in ps] + [_row(z, w, 4 * i, True) for i in range(3)]
            + [_const(b, w, 4 * i, True) for i in range(3)])


def merge_fwd(ps, z, b):
    def fn(ins, outs, _):
        merged = _merge_tile(*[r[...] for r in ins])
        outs[0][...] = merged.astype(MXU_DTYPE)
        outs[1][...] = merged.T.astype(MXU_DTYPE)

    return rowwise("merge_fwd", fn, z.shape[0], ncol=4, ins=_merge_ins(ps, z, b), outs=[(1024, 256, MXU_DTYPE, True)],
                   touts=[(1024, 256, MXU_DTYPE, True)])


def merge_bwd(dm, ps, z, b):
    def fn(ins, outs, accs):
        _, vjp = jax.vjp(_merge_tile, *[r[...] for r in ins[:9]])
        grads = vjp(ins[9][...])
        for i in range(3):
            outs[i][...] = grads[i].astype(MXU_DTYPE)
            outs[3 + i][...] = grads[3 + i].astype(MXU_DTYPE)
            accs[i][...] += grads[6 + i]

    return rowwise("merge_bwd", fn, z.shape[0], ncol=4, ins=_merge_ins(ps, z, b) + [_row(dm, 256, 0, True)],
                   outs=[(1024, 256, MXU_DTYPE, True)] * 6, accs=[(1, 1024, 256, True)] * 3)


EW_BLOCK_BYTES = 2**21


def _tile2d(r, c):
    if r * c * 4 <= EW_BLOCK_BYTES:
        return r, c
    for tr in (512, 256, 128, 64, 32):
        if r % tr == 0 and tr * c * 4 <= EW_BLOCK_BYTES:
            return tr, c
    for tc in (1024, 512, 256, 128):
        if c % tc == 0 and r * tc * 4 <= EW_BLOCK_BYTES:
            return r, tc
    raise ValueError((r, c))


def adamw(name, w, g, m, v):
    shape = w.shape
    c = shape[-1]
    r = w.size // c
    tr, tc = _tile2d(r, c)
    assert tc == c
    c1, c2 = 1.0 - ADAM_B1 ** ADAM_STEP, 1.0 - ADAM_B2 ** ADAM_STEP

    def body(w_ref, g_ref, m_ref, v_ref, d_ref, mo_ref, vo_ref):
        gv = g_ref[...]
        mn = ADAM_B1 * m_ref[...] + (1.0 - ADAM_B1) * gv
        vn = ADAM_B2 * v_ref[...] + (1.0 - ADAM_B2) * (gv * gv)
        d_ref[...] = -ADAM_LR * ((mn / c1) / (jnp.sqrt(vn / c2) + ADAM_EPS) + ADAM_WD * w_ref[...])
        mo_ref[...] = mn
        vo_ref[...] = vn

    spec = pl.BlockSpec((tr, c), lambda i: (i, 0))
    outs = pl.pallas_call(
        body,
        name=name,
        grid=(r // tr,),
        in_specs=[spec] * 4,
        out_specs=[spec] * 3,
        out_shape=[jax.ShapeDtypeStruct((r, c), F32)] * 3,
        compiler_params=_cparams(("parallel",)),
    )(*[a.reshape(r, c) for a in (w, g, m, v)])
    return [o.reshape(shape) for o in outs]


def adamw_pair(name, w, rh, lc, m, v, c_idx):
    _, r, c = w.shape
    tr, tc = _tile2d(r, c)
    c1, c2 = 1.0 - ADAM_B1 ** ADAM_STEP, 1.0 - ADAM_B2 ** ADAM_STEP

    def body(c_ref, w_ref, rh_ref, lc_ref, m_ref, v_ref, g_ref, d_ref, mo_ref, vo_ref):
        gv = jnp.where(pl.program_id(0) == c_ref[0], rh_ref[...], lc_ref[...])
        mn = ADAM_B1 * m_ref[...] + (1.0 - ADAM_B1) * gv
        vn = ADAM_B2 * v_ref[...] + (1.0 - ADAM_B2) * (gv * gv)
        g_ref[...] = gv
        d_ref[...] = -ADAM_LR * ((mn / c1) / (jnp.sqrt(vn / c2) + ADAM_EPS) + ADAM_WD * w_ref[...])
        mo_ref[...] = mn
        vo_ref[...] = vn

    full = pl.BlockSpec((None, tr, tc), lambda h, i, j, cr: (h, i, j))
    half = pl.BlockSpec((tr, tc), lambda h, i, j, cr: (i, j))
    return pl.pallas_call(
        body,
        name=name,
        grid_spec=pltpu.PrefetchScalarGridSpec(
            num_scalar_prefetch=1,
            grid=(2, r // tr, c // tc),
            in_specs=[full, half, half, full, full],
            out_specs=[full] * 4,
        ),
        out_shape=[jax.ShapeDtypeStruct((2, r, c), F32)] * 4,
        compiler_params=_cparams(("parallel", "parallel", "parallel")),
    )(c_idx, w, rh, lc, m, v)


def sum_pair(name, g, la, c_idx, out_dtype):
    _, _, r, c = g.shape
    tr, tc = _tile2d(r, c)

    def body(c_ref, g_ref, la_ref, o_ref):
        o_ref[...] = (g_ref[...] + la_ref[...]).astype(o_ref.dtype)

    return pl.pallas_call(
        body,
        name=name,
        grid_spec=pltpu.PrefetchScalarGridSpec(
            num_scalar_prefetch=1,
            grid=(4, r // tr, c // tc),
            in_specs=[pl.BlockSpec((None, None, tr, tc), lambda s, i, j, cr: (s, cr[0], i, j)),
                      pl.BlockSpec((None, tr, tc), lambda s, i, j, cr: (s, i, j))],
            out_specs=pl.BlockSpec((None, tr, tc), lambda s, i, j, cr: (s, i, j)),
        ),
        out_shape=jax.ShapeDtypeStruct((4, r, c), out_dtype),
        compiler_params=_cparams(("parallel", "parallel", "parallel")),
    )(c_idx, g, la)


def sum_chips(name, g, la, lb, cs_idx):
    _, _, r, c = g.shape
    tr, tc = _tile2d(r, c)

    def body(cs_ref, g_ref, la_ref, l0_ref, l1_ref, l2_ref, o_ref):
        own = g_ref[...] + la_ref[...]
        o_ref[...] = ((own + l0_ref[...].astype(F32)) + l1_ref[...].astype(F32)) + l2_ref[...].astype(F32)

    lspec = lambda k: pl.BlockSpec((None, tr, tc), lambda i, j, cs: (k, i, j))
    return pl.pallas_call(
        body,
        name=name,
        grid_spec=pltpu.PrefetchScalarGridSpec(
            num_scalar_prefetch=1,
            grid=(r // tr, c // tc),
            in_specs=[pl.BlockSpec((None, None, tr, tc), lambda i, j, cs: (cs[1], cs[0], i, j)),
                      pl.BlockSpec((None, tr, tc), lambda i, j, cs: (cs[1], i, j)), lspec(0), lspec(1), lspec(2)],
            out_specs=pl.BlockSpec((tr, tc), lambda i, j, cs: (i, j)),
        ),
        out_shape=jax.ShapeDtypeStruct((r, c), F32),
        compiler_params=_cparams(("parallel", "parallel")),
    )(cs_idx, g, la, lb, lb, lb)


def _place():
    x, y, c = lax.axis_index("x"), lax.axis_index("y"), lax.axis_index("c")
    chips = [(1 - x, y), (x, 1 - y), (1 - x, 1 - y)]
    return x, y, c, chips


HBM_SPEC = pl.BlockSpec(memory_space=pltpu.HBM)


def _comm_call(name, body, ins, out_shapes, n_sem, n_local):
    return pl.pallas_call(
        body,
        name=name,
        in_specs=[HBM_SPEC] * len(ins),
        out_specs=[HBM_SPEC] * len(out_shapes),
        out_shape=out_shapes,
        scratch_shapes=[pltpu.SemaphoreType.DMA((n_sem,)), pltpu.SemaphoreType.DMA((n_sem,)),
                        pltpu.SemaphoreType.DMA((max(n_local, 1),))],
    )(*ins)


def _layer_gather(ins, outs, send, recv, layer):
    n = len(ins)
    x, y, c, chips = _place()
    s = 2 * x + y
    sib = (x, y, 1 - c)
    active = c == layer

    def rc(a, k, src, dst, dev):
        return pltpu.make_async_remote_copy(src_ref=src, dst_ref=dst, send_sem=send.at[6 * a + k], recv_sem=recv.at[6 * a + k],
                                            device_id=dev, device_id_type=MESH)

    def first_hop():
        return [rc(a, j, ins[a], outs[a].at[s], (*chip, c)) for j, chip in enumerate(chips) for a in range(n)]

    def start():
        @pl.when(active)
        def _():
            for cp in first_hop():
                cp.start()

    def finish():
        @pl.when(active)
        def _():
            forwards = []
            for j, (cx, cy) in enumerate(chips):
                for a in range(n):
                    landed = outs[a].at[2 * cx + cy]
                    rc(a, j, landed, landed, sib).wait_recv()
                    forwards.append(rc(a, 3 + j, landed, landed, sib))
                    forwards[-1].start()
            for cp in first_hop() + forwards:
                cp.wait_send()

        @pl.when(jnp.logical_not(active))
        def _():
            for j, (cx, cy) in enumerate(chips):
                for a in range(n):
                    other = outs[a].at[2 * cx + cy]
                    rc(a, 3 + j, other, other, sib).wait_recv()

    return start, finish


def _gather_shapes(ws):
    return [jax.ShapeDtypeStruct((4,) + w.shape, w.dtype) for w in ws]


def allgather_layer(ws, layer):
    n = len(ws)

    def body(*refs):
        send, recv, _ = refs[2 * n:]
        start, finish = _layer_gather(refs[:n], refs[n:2 * n], send, recv, layer)
        start()
        finish()

    return _comm_call(f"allgather_layer{layer}", body, ws, _gather_shapes(ws), 6 * n, 0)


def exchange_sibling(gs):
    n = len(gs)

    def body(*refs):
        ins, outs = refs[:n], refs[n:2 * n]
        send, recv, _ = refs[2 * n:]
        x, y, c, _ = _place()
        cps = []
        for a in range(n):
            for s in range(4):
                cps.append(pltpu.make_async_remote_copy(src_ref=ins[a].at[s, 1 - c], dst_ref=outs[a].at[s],
                                                        send_sem=send.at[4 * a + s], recv_sem=recv.at[4 * a + s],
                                                        device_id=(x, y, 1 - c), device_id_type=MESH))
                cps[-1].start()
        for cp in cps:
            cp.wait()

    out_shapes = [jax.ShapeDtypeStruct((4,) + g.shape[2:], g.dtype) for g in gs]
    return _comm_call("exchange_sibling", body, gs, out_shapes, 4 * n, 0)


def exchange_chips(ps):
    n = len(ps)

    def body(*refs):
        ins, outs = refs[:n], refs[n:2 * n]
        send, recv, _ = refs[2 * n:]
        x, y, c, chips = _place()
        cps = []
        for j, (cx, cy) in enumerate(chips):
            for a in range(n):
                cps.append(pltpu.make_async_remote_copy(src_ref=ins[a].at[2 * cx + cy], dst_ref=outs[a].at[j],
                                                        send_sem=send.at[3 * a + j], recv_sem=recv.at[3 * a + j],
                                                        device_id=(cx, cy, c), device_id_type=MESH))
                cps[-1].start()
        for cp in cps:
            cp.wait()

    out_shapes = [jax.ShapeDtypeStruct((3,) + p.shape[1:], p.dtype) for p in ps]
    return _comm_call("exchange_chips", body, ps, out_shapes, 3 * n, 0)


def exchange_final(rs, small):
    n = len(rs)

    def body(*refs):
        ins, small_ref = refs[:n], refs[n]
        outs, small_out = refs[n + 1:2 * n + 1], refs[2 * n + 1]
        send, recv, lsem = refs[2 * n + 2:]
        x, y, c, _ = _place()
        s = 2 * x + y
        cps, local = [], []
        for a in range(n):
            cps.append(pltpu.make_async_remote_copy(src_ref=ins[a], dst_ref=outs[a], send_sem=send.at[a], recv_sem=recv.at[a],
                                                    device_id=(x, y, 1 - c), device_id_type=MESH))
            cps[-1].start()
        local.append(pltpu.make_async_copy(small_ref, small_out.at[s, c], lsem.at[0]))
        local[-1].start()
        k = n
        for fx in (0, 1):
            for fy in (0, 1):
                for fc in (0, 1):
                    if fx or fy or fc:
                        dev = (x ^ fx, y ^ fy, c ^ fc)
                        cps.append(pltpu.make_async_remote_copy(src_ref=small_ref, dst_ref=small_out.at[s, c],
                                                                send_sem=send.at[k], recv_sem=recv.at[k],
                                                                device_id=dev, device_id_type=MESH))
                        cps[-1].start()
                        k += 1
        for cp in cps:
            cp.wait()
        for cp in local:
            cp.wait()

    out_shapes = [jax.ShapeDtypeStruct(r.shape, r.dtype) for r in rs]
    out_shapes.append(jax.ShapeDtypeStruct((4, 2) + small.shape, small.dtype))
    res = _comm_call("exchange_final", body, list(rs) + [small], out_shapes, n + 7, 1)
    return res[:n], res[n]


def _pad_in_cols(w):
    z = lambda n: jnp.zeros(w.shape[:-1] + (n,), w.dtype)
    return jnp.concatenate([w[..., :KR_OFF], z(64), w[..., KR_OFF:KR_OFF + MLA_ROPE], z(32), w[..., KR_OFF + MLA_ROPE:]], axis=-1)


def _unpad_in_cols(g):
    return jnp.concatenate([g[..., :KR_OFF], g[..., KR_OFF + 64:KR_OFF + 96], g[..., KR_OFF + 128:]], axis=-1)


def _pad_heads(w, real):
    k = w.shape[0]
    return jnp.pad(w.reshape(k, MLA_HEADS, real), ((0, 0), (0, 0), (0, LANES - real))).reshape(k, MLA_HEADS * LANES)


def _pad_gain(g, real):
    return jnp.pad(g.reshape(1, real), ((0, 0), (0, LANES - real)))


def layer_weights(full):
    w_ukv = full["w_ukv"].reshape(128, MLA_HEADS, 2, 64)
    two = lambda g: jnp.concatenate([g, g]).reshape(1, LANES)
    return dict(
        norm_g=full["norm_g"].reshape(1, -1), w_in=full["w_in"], conv_w=full["conv_w"], conv_b=full["conv_b"].reshape(1, -1),
        wgx=full["w_gate_x"], bgx=full["b_gate_x"].reshape(LRU_BLOCKS, 1, LANES),
        wga=full["w_gate_a"], bga=full["b_gate_a"].reshape(LRU_BLOCKS, 1, LANES),
        lam=full["lru_lambda"].reshape(1, -1), w_lru_o=full["w_lru_o"],
        cq_norm_g=full["cq_norm_g"].reshape(1, -1), ckv_norm_g=full["ckv_norm_g"].reshape(1, -1),
        wq=_pad_heads(full["w_uq"], MLA_QK), wk=_pad_heads(w_ukv[:, :, 0].reshape(128, 512), 64),
        wv=w_ukv[:, :, 1].reshape(128, 512),
        gq=_pad_gain(full["mla_q_norm_g"], MLA_QK), gk=_pad_gain(full["mla_k_norm_g"], MLA_QK),
        w_mla_o=full["w_mla_o"], gq2=two(full["dil_q_norm_g"]), gk2=two(full["dil_k_norm_g"]),
        w_dil_o=full["w_dil_o"], b_merge=full["b_merge"].reshape(1, -1), w_out=full["w_out"],
    )


def layer_fwd(x, w, tabs, gather=None):
    h, ht = rmsnorm_fwd(x, w["norm_g"])
    z_lru = mm_nn("in_proj_lru", h, w["w_in"], n=2048, n_off=G_LRU)
    z_mla = mm_nn("in_proj_mla", h, w["w_in"], n=1024, n_off=G_MLA)
    z_dil = mm_nn("in_proj_dil", h, w["w_in"], n=5120, n_off=G_DIL)
    z_mrg = mm_nn("in_proj_mrg", h, w["w_in"], n=3072, n_off=G_MRG)
    hs, y_lru, yt_lru = lru_fwd(z_lru, w["conv_w"], w["conv_b"], w["wgx"], w["bgx"], w["wga"], w["bga"], w["lam"])
    qm, km, vm, vtm = mla_prep_fwd(z_mla, tabs, w)
    o_mla, y_mla, yt_mla, lse_mla, *gathered = mla_attn_fwd(qm, km, vtm, z_mla, gather=gather)
    os_, lses, dil_rm = [], [], []
    for gi, (_, dil) in enumerate(DIL_GROUPS):
        qkv = dil_prep_fwd(z_dil, tabs, w["gq2"], w["gk2"], gi)
        o, lse = dil_attn_fwd(f"dil_attn_fwd_{dil}", *qkv, dil)
        os_.append(o)
        lses.append(lse)
        dil_rm.append(qkv)
    y_dil, o_dil, lse_dil, yt_dil = dil_combine(os_, lses, z_dil)
    ps = [mm_nn("proj_lru", y_lru, w["w_lru_o"]), mm_nn("proj_mla", y_mla, w["w_mla_o"]), mm_nn("proj_dil", y_dil, w["w_dil_o"])]
    merged, merged_t = merge_fwd(ps, z_mrg, w["b_merge"])
    out = mm_nn("out_proj", merged, w["w_out"], add=x)
    res = dict(x=x, ht=ht, z_lru=z_lru, z_mla=z_mla, z_dil=z_dil, z_mrg=z_mrg, hs=hs, yt_lru=yt_lru, qm=qm, km=km, vm=vm, o_mla=o_mla,
               yt_mla=yt_mla, lse_mla=lse_mla, dil_rm=dil_rm, yt_dil=yt_dil, o_dil=o_dil, lse_dil=lse_dil, ps=ps, merged_t=merged_t)
    return out, res, gathered


def layer_bwd(dout, r, w, tabs):
    t = dout.shape[0]
    g = {}
    dmerged = mm_nt("out_proj_dx", dout, w["w_out"])
    g["w_out"] = mm_nn("out_proj_dw", r["merged_t"], dout, tm=1024, tn=512, tk=2048)
    dp0, dp1, dp2, dzm0, dzm1, dzm2, db0, db1, db2 = merge_bwd(dmerged, r["ps"], r["z_mrg"], w["b_merge"])
    g["b_merge"] = jnp.concatenate([db0, db1, db2], axis=1).reshape(-1)
    dy_lru = mm_nt("proj_lru_dx", dp0, w["w_lru_o"])
    dy_mla = mm_nt("proj_mla_dx", dp1, w["w_mla_o"])
    dy_dil = mm_nt("proj_dil_dx", dp2, w["w_dil_o"])
    g["w_lru_o"] = mm_nn("proj_lru_dw", r["yt_lru"], dp0, **DW_TILES)
    g["w_mla_o"] = mm_nn("proj_mla_dw", r["yt_mla"], dp1, **DW_TILES)
    g["w_dil_o"] = mm_nn("proj_dil_dw", r["yt_dil"], dp2, **DW_TILES)
    dzx, dzg_lru, dwgx, dbgx, dwga, dbga, dlam, dcw, dcb = lru_bwd(r["z_lru"], r["hs"], dy_lru, w["conv_w"], w["conv_b"], w["wgx"],
                                                                    w["bgx"], w["wga"], w["bga"], w["lam"])
    g.update(w_gate_x=dwgx, b_gate_x=dbgx.reshape(LRU_BLOCKS, LANES), w_gate_a=dwga, b_gate_a=dbga.reshape(LRU_BLOCKS, LANES),
             lru_lambda=dlam.reshape(-1), conv_w=dcw, conv_b=dcb.reshape(-1))
    do_m, dzg_mla, dd_m = mla_gate_bwd(dy_mla, r["o_mla"], r["z_mla"])
    dq_m, dk_m, dv_m = mla_attn_bwd(r["qm"], r["km"], r["vm"], do_m, r["lse_mla"], dd_m)
    dz_mla3, dg_cq, dg_ckv, dwq, dwk, dwv, dgq, dgk = mla_prep_bwd(r["z_mla"], tabs, w, dq_m, dk_m, dv_m)
    g.update(cq_norm_g=dg_cq.reshape(-1), ckv_norm_g=dg_ckv.reshape(-1), mla_q_norm_g=dgq[0, :MLA_QK], mla_k_norm_g=dgk[0, :MLA_QK])
    g["w_uq"] = dwq.reshape(256, MLA_HEADS, LANES)[:, :, :MLA_QK].reshape(256, MLA_HEADS * MLA_QK)
    g["w_ukv"] = jnp.concatenate([dwk.reshape(128, MLA_HEADS, LANES)[:, :, :64], dwv.reshape(128, MLA_HEADS, 64)], axis=2).reshape(128, 1024)
    dzg_dil, do1, dd1, do2, dd2, l2, do3, dd3, l3 = dil_gate_bwd(dy_dil, r["o_dil"], r["z_dil"], r["lse_dil"])
    stats = [(do1, r["lse_dil"], dd1), (do2, l2, dd2), (do3, l3, dd3)]
    dzq, dzk, dzv, dgq2, dgk2 = [], [], [], [], []
    for gi, (_, dil) in enumerate(DIL_GROUPS):
        dq, dk, dv = dil_attn_bwd(f"dil_attn_bwd_{dil}", *r["dil_rm"][gi], *stats[gi], dil)
        parts = dil_prep_bwd(r["z_dil"], tabs, w["gq2"], w["gk2"], dq, dk, dv, gi)
        for acc, part in zip((dzq, dzk, dzv, dgq2, dgk2), parts):
            acc.append(part)
    g.update(dil_q_norm_g=sum(dgq2)[0, :DIL_HD], dil_k_norm_g=sum(dgk2)[0, :DIL_HD])
    dz = jnp.concatenate([dzx, dzg_lru, dz_mla3, dzg_mla] + dzq + dzk + dzv + [dzg_dil, dzm0, dzm1, dzm2], axis=1)
    dh = mm_nt("in_proj_dx", dz, w["w_in"], tm=1024, tn=1024, tk=IN_PAD // 4)
    g["w_in"] = _unpad_in_cols(mm_nn("in_proj_dw", r["ht"], dz, **DW_TILES))
    dx, dng = rmsnorm_bwd(r["x"], dh, dout, w["norm_g"])
    g["norm_g"] = dng.reshape(-1)
    return dx, g


def local_step(x, positions, target, full0, full1=None, gather1=None):
    tabs = rope_tables(positions.reshape(-1, 1))
    ws, ress = [], []
    for l in range(2):
        if l == 0:
            ws.append(layer_weights(full0))
            x, res, gathered = layer_fwd(x, ws[0], tabs, gather=None if gather1 is None else (gather1[0], 1))
        else:
            ws.append(layer_weights(full1 if gather1 is None else gather1[1](gathered)))
            x, res, _ = layer_fwd(x, ws[1], tabs)
        ress.append(res)
    dy, loss = loss_head(x, target)
    grads = [None, None]
    for l in (1, 0):
        dy, grads[l] = layer_bwd(dy, ress[l], ws[l], tabs)
    return loss, dy, grads


WEIGHTS = ["norm_g", "w_in", "conv_w", "conv_b", "w_gate_x", "b_gate_x", "w_gate_a", "b_gate_a", "lru_lambda", "w_lru_o", "cq_norm_g",
           "ckv_norm_g", "w_uq", "w_ukv", "mla_q_norm_g", "mla_k_norm_g", "w_mla_o", "dil_q_norm_g", "dil_k_norm_g", "w_dil_o", "b_merge",
           "w_out"]
SHARDED = {"w_in": 2, "conv_w": 2, "w_lru_o": 1, "w_uq": 2, "w_ukv": 2, "w_mla_o": 2, "w_dil_o": 2, "w_out": 1}
REPLICATED = [n for n in WEIGHTS if n not in SHARDED]
SMALL_ROWS = 72


def kernel(x, positions, norm_g, w_in, conv_w, conv_b, w_gate_x, b_gate_x, w_gate_a, b_gate_a, lru_lambda, w_lru_o, cq_norm_g, ckv_norm_g, w_uq, w_ukv, mla_q_norm_g, mla_k_norm_g, w_mla_o, dil_q_norm_g, dil_k_norm_g, w_dil_o, b_merge, w_out, loss_target, m_norm_g, m_w_in, m_conv_w, m_conv_b, m_w_gate_x, m_b_gate_x, m_w_gate_a, m_b_gate_a, m_lru_lambda, m_w_lru_o, m_cq_norm_g, m_ckv_norm_g, m_w_uq, m_w_ukv, m_mla_q_norm_g, m_mla_k_norm_g, m_w_mla_o, m_dil_q_norm_g, m_dil_k_norm_g, m_w_dil_o, m_b_merge, m_w_out, v_norm_g, v_w_in, v_conv_w, v_conv_b, v_w_gate_x, v_b_gate_x, v_w_gate_a, v_b_gate_a, v_lru_lambda, v_w_lru_o, v_cq_norm_g, v_ckv_norm_g, v_w_uq, v_w_ukv, v_mla_q_norm_g, v_mla_k_norm_g, v_w_mla_o, v_dil_q_norm_g, v_dil_k_norm_g, v_w_dil_o, v_b_merge, v_w_out):
    args = locals()
    w = {n: args[n] for n in WEIGHTS}
    m = {n: args["m_" + n] for n in WEIGHTS}
    v = {n: args["v_" + n] for n in WEIGHTS}
    my_c = lax.axis_index("c").astype(jnp.int32)
    my_s = (2 * lax.axis_index("x") + lax.axis_index("y")).astype(jnp.int32)
    c_idx = my_c.reshape(1)
    cs_idx = jnp.stack([my_c, my_s])

    names = list(SHARDED)
    wire = [[w[n][l] if n == "conv_w" else w[n][l].astype(BF16) for n in names] for l in range(2)]

    def assemble(l, gathered):
        full = {n: w[n][l] for n in REPLICATED}
        for n, ga, own in zip(names, gathered, wire[l]):
            full[n] = jnp.concatenate([jnp.where(my_s == s, own, ga[s]) for s in range(4)], axis=SHARDED[n] - 1)
        full["w_in"] = _pad_in_cols(full["w_in"])
        return full

    full0 = assemble(0, allgather_layer(wire[0], 0))

    loss, grad_x, grads = local_step(x[0], positions[0], loss_target[0], full0,
                                     gather1=(wire[1], functools.partial(assemble, 1)))
    loss = lax.psum(loss, ("x", "y", "c"))

    flat = jnp.concatenate([jnp.stack([grads[0][n], grads[1][n]]).reshape(-1) for n in REPLICATED])
    small = jnp.pad(flat, (0, 8 * SMALL_ROWS * 1024 - flat.size)).reshape(4, 2, SMALL_ROWS, 1024)
    gs = []
    for n in names:
        per_layer = []
        for l in range(2):
            g_ = grads[l][n]
            if n == "w_in":
                per_layer.append(g_.reshape(D_MODEL, 4, IN_WIDTH // 4).transpose(1, 2, 0))
            else:
                parts = jnp.stack(jnp.split(g_, 4, axis=SHARDED[n] - 1))
                per_layer.append(parts.reshape(4, -1, parts.shape[-1]))
        gs.append(jnp.stack(per_layer, axis=1))
    gs.append(small)
    tags = names + ["small"]
    la = exchange_sibling(gs)
    wire_dt = [BF16 if g_.size >= 2**20 else F32 for g_ in gs]
    pcs = [sum_pair(f"sum_pair_{t}", g_, l_, c_idx, dt) for t, g_, l_, dt in zip(tags, gs, la, wire_dt)]
    lb = exchange_chips(pcs)
    rs = [sum_chips(f"sum_chips_{t}", g_, l_, b_, cs_idx) for t, g_, l_, b_ in zip(tags, gs, la, lb)]
    sib_half, small_all = exchange_final(rs[:-1], rs[-1])

    g_local, delta, new_m, new_v = {}, {}, {}, {}
    for n, rh, lc in zip(names, rs, sib_half):
        shp = w[n].shape
        if n == "w_in":
            as3 = lambda a: a.transpose(0, 2, 1)
            back = lambda o: o.transpose(0, 2, 1)
        else:
            as3 = lambda a: a.reshape((2,) + rh.shape)
            back = lambda o: o.reshape(shp)
        outs = adamw_pair(f"adamw_{n}", as3(w[n]), rh, lc, as3(m[n]), as3(v[n]), c_idx)
        g_local[n], delta[n], new_m[n], new_v[n] = [back(o) for o in outs]
    flat = small_all.reshape(-1)
    off = 0
    for n in REPLICATED:
        g_local[n] = flat[off:off + w[n].size].reshape(w[n].shape)
        off += w[n].size
        delta[n], new_m[n], new_v[n] = adamw(f"adamw_{n}", w[n], g_local[n], m[n], v[n])
    return (loss, grad_x[None], *[g_local[n] for n in WEIGHTS], *[delta[n] for n in WEIGHTS], *[new_m[n] for n in WEIGHTS],
            *[new_v[n] for n in WEIGHTS])
```

```python
import functools

import jax
import jax.numpy as jnp
from jax import lax
from jax.experimental import pallas as pl
from jax.experimental.pallas import tpu as pltpu

F32 = jnp.float32
BF16 = jnp.bfloat16
MXU_DTYPE = jnp.bfloat16

D_MODEL = 1024
EPS = 1e-6
ROPE_THETA = 10000.0
LRU_BLOCKS = 8
LRU_C = 8.0
MLA_HEADS = 8
MLA_NOPE = 64
MLA_ROPE = 32
MLA_QK = 96
DIL_GROUPS = ((128, 1), (512, 4), (2048, 16))
DIL_HD = 64
DIL_NK = 128
MLA_HPS = 8
IN_WIDTH = 11168
ADAM_LR, ADAM_B1, ADAM_B2, ADAM_EPS, ADAM_WD, ADAM_STEP = 0.001, 0.9, 0.999, 1e-08, 0.01, 10

LANES = 128
G_LRU, G_MLA, G_DIL, G_MRG = 0, 2048, 3072, 8192
IN_PAD = 11264
KR_OFF = 2432

NN = (((1,), (0,)), ((), ()))
NT = (((1,), (1,)), ((), ()))
TN = (((0,), (0,)), ((), ()))
NEG = -1e30
MESH = pl.DeviceIdType.MESH
VMEM_LIMIT = 48 * 2**20


def _cparams(sem):
    return pltpu.CompilerParams(dimension_semantics=sem, vmem_limit_bytes=VMEM_LIMIT)


def _dot(a, b, dims):
    return lax.dot_general(a.astype(MXU_DTYPE), b.astype(MXU_DTYPE), dims, preferred_element_type=F32)


@jax.custom_vjp
def mm(a, w):
    return _dot(a, w, NN)


def _mm_fwd(a, w):
    return _dot(a, w, NN), (a, w)


def _mm_bwd(res, g):
    a, w = res
    return _dot(g, w, NT), _dot(a, g, TN)


mm.defvjp(_mm_fwd, _mm_bwd)


def _seg64_matrix():
    r = lax.broadcasted_iota(jnp.int32, (LANES, LANES), 0) < DIL_HD
    c = lax.broadcasted_iota(jnp.int32, (LANES, LANES), 1) < DIL_HD
    return (r == c).astype(BF16)


def _seg_sum_impl(x):
    b = _seg64_matrix()
    hi = x.astype(BF16)
    r1 = x - hi.astype(F32)
    mid = r1.astype(BF16)
    lo = (r1 - mid.astype(F32)).astype(BF16)
    dot = lambda u: lax.dot_general(u, b, NN, preferred_element_type=F32)
    return dot(hi) + dot(mid) + dot(lo)


@jax.custom_vjp
def seg_sum(x):
    return _seg_sum_impl(x)


seg_sum.defvjp(lambda x: (_seg_sum_impl(x), None), lambda _, g: (_seg_sum_impl(g),))


def _lroll_impl(x, s):
    return pltpu.roll(x, s % LANES, 1)


@functools.partial(jax.custom_vjp, nondiff_argnums=(1,))
def lroll(x, s):
    return _lroll_impl(x, s)


lroll.defvjp(lambda x, s: (_lroll_impl(x, s), None), lambda s, _, g: (_lroll_impl(g, -s),))


class _Ops:
    def __init__(self, diff):
        self.mm = mm if diff else (lambda a, w: _dot(a, w, NN))
        self.seg_sum = seg_sum if diff else _seg_sum_impl
        self.lroll = lroll if diff else _lroll_impl


PLAIN, DIFF = _Ops(False), _Ops(True)


def rms(x, g, n):
    ms = jnp.sum(x * x, axis=-1, keepdims=True) * (1.0 / n)
    return x * lax.rsqrt(ms + EPS) * g


def rope(ops, x, c, s1, s2, half):
    return x * c + ops.lroll(x, -half) * s1 + ops.lroll(x, half) * s2


def _sigmoid(x):
    return 1.0 / (1.0 + jnp.exp(-x))


def _silu_and_grad(g):
    sg = _sigmoid(g)
    return g * sg, sg * (1.0 + g * (1.0 - sg))


def _softplus(x):
    return jnp.maximum(x, 0.0) + jnp.log(1.0 + jnp.exp(-jnp.abs(x)))


def _expm1(y):
    series = y * (1.0 + y * (0.5 + y * (1.0 / 6.0 + y * (1.0 / 24.0 + y * (1.0 / 120.0)))))
    return jnp.where(y > -0.05, series, jnp.exp(jnp.minimum(y, -0.05)) - 1.0)


def _mm_call(name, a, b, *, mode, m, n, k, a_blk, b_blk, out_dtype, add, tm, tn, tk):
    nk = k // tk
    assert m % tm == 0 and n % tn == 0 and k % tk == 0, (name, m, n, k, tm, tn, tk)
    dims = {"nn": NN, "nt": NT, "tn": TN}[mode]

    def body(*refs):
        if add is None:
            a_ref, b_ref, o_ref, *scr = refs
            add_ref = None
        else:
            a_ref, b_ref, add_ref, o_ref, *scr = refs
        part = _dot(a_ref[...], b_ref[...], dims)

        def finish(acc):
            if add_ref is not None:
                acc = acc + add_ref[...]
            o_ref[...] = acc.astype(o_ref.dtype)

        if nk == 1:
            finish(part)
        else:
            (acc_ref,) = scr
            kk = pl.program_id(2)

            @pl.when(kk == 0)
            def _():
                acc_ref[...] = part

            @pl.when(kk > 0)
            def _():
                acc_ref[...] += part

            @pl.when(kk == nk - 1)
            def _():
                finish(acc_ref[...])

    in_specs = [a_blk, b_blk]
    args = [a, b]
    if add is not None:
        in_specs.append(pl.BlockSpec((tm, tn), lambda j, i, kk: (i, j)))
        args.append(add)
    return pl.pallas_call(
        body,
        name=name,
        grid=(n // tn, m // tm, nk),
        in_specs=in_specs,
        out_specs=pl.BlockSpec((tm, tn), lambda j, i, kk: (i, j)),
        out_shape=jax.ShapeDtypeStruct((m, n), out_dtype),
        scratch_shapes=[] if nk == 1 else [pltpu.VMEM((tm, tn), F32)],
        compiler_params=_cparams(("parallel", "parallel", "arbitrary")),
    )(*args)


def mm_nn(name, a, b, *, n=None, n_off=0, out_dtype=F32, add=None, tm=512, tn=1024, tk=1024):
    m, k = a.shape
    n = b.shape[1] if n is None else n
    tm, tn, tk = min(tm, m), min(tn, n), min(k, tk)
    ob = n_off // tn
    assert n_off % tn == 0
    return _mm_call(name, a, b, mode="nn", m=m, n=n, k=k, out_dtype=out_dtype, add=add, tm=tm, tn=tn, tk=tk,
                    a_blk=pl.BlockSpec((tm, tk), lambda j, i, kk: (i, kk)),
                    b_blk=pl.BlockSpec((tk, tn), lambda j, i, kk: (kk, j + ob)))


def mm_nt(name, a, b, *, out_dtype=F32, tm=512, tn=1024, tk=1024):
    m, k = a.shape
    n = b.shape[0]
    tn, tk = min(tn, n), min(k, tk)
    return _mm_call(name, a, b, mode="nt", m=m, n=n, k=k, out_dtype=out_dtype, add=None, tm=tm, tn=tn, tk=tk,
                    a_blk=pl.BlockSpec((tm, tk), lambda j, i, kk: (i, kk)),
                    b_blk=pl.BlockSpec((tn, tk), lambda j, i, kk: (j, kk)))


DW_TILES = dict(tm=1024, tn=512, tk=4096)


def rowwise(name, fn, t, *, tm=256, ncol=1, ins=(), outs=(), touts=(), accs=()):
    n_in, n_out, n_acc = len(ins), len(outs) + len(touts), len(accs)

    def zero_when(ref, cond):
        @pl.when(cond)
        def _():
            ref[...] = jnp.zeros(ref.shape, ref.dtype)

    def body(*refs):
        in_refs, out_refs, acc_refs = refs[:n_in], refs[n_in:n_in + n_out], refs[n_in + n_out:]
        j, i = pl.program_id(0), pl.program_id(1)
        for ref, (_, _, _, cd) in zip(acc_refs, accs):
            zero_when(ref, (i == 0) if cd else ((i == 0) & (j == 0)))
        fn(in_refs, out_refs, acc_refs)

    def in_spec(arr, w, base, cd, rd):
        rows = tm if rd else arr.shape[0]
        return pl.BlockSpec((rows, w), lambda j, i: (i if rd else 0, base + (j if cd else 0)))

    in_specs = [in_spec(*e) for e in ins]
    out_specs = [pl.BlockSpec((tm, w), (lambda j, i, cd=cd: (i, j if cd else 0))) for (_, w, _, cd) in outs]
    out_specs += [pl.BlockSpec((w, tm), (lambda j, i, cd=cd: (j if cd else 0, i))) for (_, w, _, cd) in touts]
    out_specs += [pl.BlockSpec((r, w), (lambda j, i, cd=cd: (0, j if cd else 0))) for (r, _, w, cd) in accs]
    out_shape = [jax.ShapeDtypeStruct((t, c), dt) for (c, _, dt, _) in outs]
    out_shape += [jax.ShapeDtypeStruct((r, t), dt) for (r, _, dt, _) in touts]
    out_shape += [jax.ShapeDtypeStruct((r, c), F32) for (r, c, _, _) in accs]
    res = pl.pallas_call(
        body,
        name=name,
        grid=(ncol, t // tm),
        in_specs=in_specs,
        out_specs=out_specs,
        out_shape=out_shape,
        compiler_params=_cparams(("arbitrary", "arbitrary") if accs else ("parallel", "parallel")),
    )(*[e[0] for e in ins])
    return res


def _row(arr, w=None, base=0, cd=False):
    return (arr, arr.shape[1] if w is None else w, base, cd, True)


def _const(arr, w=None, base=0, cd=False):
    return (arr, arr.shape[1] if w is None else w, base, cd, False)


def rope_tables(positions):
    t = positions.shape[0]

    def fn(ins, outs, _):
        pos = ins[0][...].astype(F32)
        lane = lax.broadcasted_iota(jnp.int32, (1, LANES), 1)
        log_theta = jnp.log(jnp.float32(ROPE_THETA))
        jm = lane - MLA_NOPE
        idx = jnp.clip(jnp.where(jm < 16, jm, jm - 16), 0, 15).astype(F32)
        ang = pos * jnp.exp(-(idx * (2.0 / MLA_ROPE)) * log_theta)
        cos, sin = jnp.cos(ang), jnp.sin(ang)
        in_rope = (lane >= MLA_NOPE) & (lane < MLA_QK)
        outs[0][:, 0:128] = jnp.where(lane < MLA_NOPE, 1.0, jnp.where(in_rope, cos, 0.0))
        outs[0][:, 128:256] = jnp.where(in_rope & (jm < 16), -sin, 0.0)
        outs[0][:, 256:384] = jnp.where(in_rope & (jm >= 16), sin, 0.0)
        jd = lane & (DIL_HD - 1)
        idx = (jd & 31).astype(F32)
        ang = pos * jnp.exp(-(idx * (2.0 / DIL_HD)) * log_theta)
        cos, sin = jnp.cos(ang), jnp.sin(ang)
        outs[0][:, 384:512] = cos
        outs[0][:, 512:640] = jnp.where(jd < 32, -sin, 0.0)
        outs[0][:, 640:768] = jnp.where(jd >= 32, sin, 0.0)

    return rowwise("rope_tables", fn, t, ins=[_row(positions)], outs=[(768, 768, F32, False)])[0]


def rmsnorm_fwd(x, g):
    def fn(ins, outs, _):
        h = rms(ins[0][...], ins[1][...], D_MODEL)
        outs[0][...] = h.astype(MXU_DTYPE)
        outs[1][...] = h.T.astype(MXU_DTYPE)

    return rowwise("rmsnorm_fwd", fn, x.shape[0], ins=[_row(x), _const(g)], outs=[(D_MODEL, D_MODEL, MXU_DTYPE, False)],
                   touts=[(D_MODEL, D_MODEL, MXU_DTYPE, False)])


def rmsnorm_bwd(x, dh, dres, g):
    def fn(ins, outs, accs):
        _, vjp = jax.vjp(lambda xv, gv: rms(xv, gv, D_MODEL), ins[0][...], ins[3][...])
        dx, dg = vjp(ins[1][...])
        outs[0][...] = ins[2][...] + dx
        accs[0][...] += dg

    return rowwise("rmsnorm_bwd", fn, x.shape[0], ins=[_row(x), _row(dh), _row(dres), _const(g)],
                   outs=[(D_MODEL, D_MODEL, F32, False)], accs=[(1, D_MODEL, D_MODEL, False)])


def loss_head(y, target):
    def fn(ins, outs, accs):
        err = ins[0][...] - ins[1][...]
        outs[0][...] = err * (1.0 / D_MODEL)
        accs[0][...] += jnp.sum(err * err, axis=0, keepdims=True)
        accs[1][...] = jnp.broadcast_to(jnp.sum(accs[0][...], keepdims=True), (1, LANES))

    dy, _, tot = rowwise("loss_head", fn, y.shape[0], ins=[_row(y), _row(target)], outs=[(D_MODEL, D_MODEL, F32, False)],
                         accs=[(1, D_MODEL, D_MODEL, False), (1, LANES, LANES, False)])
    return dy, tot[0, 0] * (0.5 / D_MODEL)


def _shift_rows(v, d, fill, reverse):
    tb = v.shape[0]
    if d % 8 == 0:
        pad = jnp.full((d, v.shape[1]), fill, v.dtype)
        return jnp.concatenate([v[d:], pad] if reverse else [pad, v[:tb - d]], axis=0)
    rows = lax.broadcasted_iota(jnp.int32, v.shape, 0)
    if not reverse:
        return jnp.where(rows >= d, pltpu.roll(v, d, 0), fill)
    return jnp.where(rows < tb - d, pltpu.roll(v, tb - d, 0), fill)


def _scan_tile(a, b, reverse):
    d = 1
    while d < a.shape[0]:
        b = b + a * _shift_rows(b, d, 0.0, reverse)
        a = a * _shift_rows(a, d, 1.0, reverse)
        d *= 2
    return a, b


def _lru_gates(ops, xc, wgx, bgx, wga, bga, lam):
    gx = _sigmoid(ops.mm(xc, wgx) + bgx)
    ga = _sigmoid(ops.mm(xc, wga) + bga)
    log_a = -LRU_C * ga * _softplus(-lam)
    a = jnp.exp(log_a)
    mult = jnp.sqrt(-_expm1(2.0 * log_a))
    return a, mult * (gx * xc)


def _shifted_inputs(x, halo, tb):
    rows = lax.broadcasted_iota(jnp.int32, x.shape, 0)
    pad = jnp.zeros((tb - 8, LANES), F32)
    out = []
    for d in (3, 2, 1):
        head = jnp.concatenate([pltpu.roll(halo, d, 0), pad], axis=0)
        out.append(jnp.where(rows >= d, pltpu.roll(x, d, 0), head))
    return out + [x]


def _lru_specs(nt, tb, reverse):
    hb = tb // 8
    tt = (lambda t: nt - 1 - t) if reverse else (lambda t: t)
    blk = lambda off: pl.BlockSpec((tb, LANES), lambda n, t: (tt(t), n + off))
    halo = lambda off: pl.BlockSpec((8, LANES), lambda n, t: (jnp.maximum(tt(t) * hb - 1, 0), n + off))
    chan = lambda r: pl.BlockSpec((r, LANES), lambda n, t: (0, n))
    wblk = pl.BlockSpec((None, LANES, LANES), lambda n, t: (n, 0, 0))
    bblk = pl.BlockSpec((None, 1, LANES), lambda n, t: (n, 0, 0))
    return blk, halo, chan, wblk, bblk


def lru_fwd(z, conv_w, conv_b, wgx, bgx, wga, bga, lam, *, tb=256):
    t = z.shape[0]
    nt = t // tb
    blk, halo, chan, wblk, bblk = _lru_specs(nt, tb, False)

    def body(x_ref, xh_ref, g_ref, cw_ref, cb_ref, wgx_ref, bgx_ref, wga_ref, bga_ref, lam_ref, h_ref, y_ref, yt_ref, carry_ref):
        ti = pl.program_id(1)

        @pl.when(ti == 0)
        def _():
            carry_ref[...] = jnp.zeros((8, LANES), F32)

        x = x_ref[...]
        hal = jnp.where(ti > 0, xh_ref[...], 0.0)
        xs = _shifted_inputs(x, hal, tb)
        xc = cb_ref[...] + sum(xs[k] * cw_ref[k:k + 1, :] for k in range(4))
        a, b = _lru_gates(PLAIN, xc, wgx_ref[...], bgx_ref[...], wga_ref[...], bga_ref[...], lam_ref[...])
        acum, h0 = _scan_tile(a, b, False)
        h = h0 + acum * carry_ref[7:8, :]
        carry_ref[...] = h[tb - 8:tb, :]
        h_ref[...] = h
        y = h * _silu_and_grad(g_ref[...])[0]
        y_ref[...] = y.astype(MXU_DTYPE)
        yt_ref[...] = y.T.astype(MXU_DTYPE)

    return pl.pallas_call(
        body,
        name="lru_fwd",
        grid=(LRU_BLOCKS, nt),
        in_specs=[blk(0), halo(0), blk(LRU_BLOCKS), chan(4), chan(1), wblk, bblk, wblk, bblk, chan(1)],
        out_specs=[blk(0), blk(0), pl.BlockSpec((LANES, tb), lambda n, t_: (n, t_))],
        out_shape=[jax.ShapeDtypeStruct((t, 1024), F32), jax.ShapeDtypeStruct((t, 1024), MXU_DTYPE),
                   jax.ShapeDtypeStruct((1024, t), MXU_DTYPE)],
        scratch_shapes=[pltpu.VMEM((8, LANES), F32)],
        compiler_params=_cparams(("parallel", "arbitrary")),
    )(z, z, z, conv_w, conv_b, wgx, bgx, wga, bga, lam)


def lru_bwd(z, h, dy, conv_w, conv_b, wgx, bgx, wga, bga, lam, *, tb=256):
    t = z.shape[0]
    nt = t // tb
    blk, halo, chan, wblk, bblk = _lru_specs(nt, tb, True)

    def body(x_ref, xh_ref, g_ref, h_ref, hh_ref, dy_ref, cw_ref, cb_ref, wgx_ref, bgx_ref, wga_ref, bga_ref, lam_ref,
             dzx_ref, dzg_ref, dwgx_ref, dbgx_ref, dwga_ref, dbga_ref, dlam_ref, dcw_ref, dcb_ref,
             gcar_ref, acar_ref, xcar_ref):
        ti = pl.program_id(1)
        has_earlier = ti < nt - 1

        @pl.when(ti == 0)
        def _():
            for ref in (dwgx_ref, dbgx_ref, dwga_ref, dbga_ref, dlam_ref, dcw_ref, dcb_ref, gcar_ref, acar_ref, xcar_ref):
                ref[...] = jnp.zeros(ref.shape, F32)

        rows = lax.broadcasted_iota(jnp.int32, (tb, LANES), 0)
        x = x_ref[...]
        hal = jnp.where(has_earlier, xh_ref[...], 0.0)
        xs = _shifted_inputs(x, hal, tb)
        xc = cb_ref[...] + sum(xs[k] * cw_ref[k:k + 1, :] for k in range(4))
        (a, _), vjp = jax.vjp(functools.partial(_lru_gates, DIFF), xc, wgx_ref[...], bgx_ref[...], wga_ref[...],
                              bga_ref[...], lam_ref[...])
        g, h, dyv = g_ref[...], h_ref[...], dy_ref[...]
        silu, dsilu = _silu_and_grad(g)
        dzg_ref[...] = (dyv * h * dsilu).astype(MXU_DTYPE)
        a_next = jnp.where(rows < tb - 1, pltpu.roll(a, tb - 1, 0), acar_ref[0:1, :])
        acum, g0 = _scan_tile(a_next, dyv * silu, True)
        gt = g0 + acum * gcar_ref[0:1, :]
        h_prev = jnp.where(rows >= 1, pltpu.roll(h, 1, 0), jnp.where(has_earlier, hh_ref[7:8, :], 0.0))
        dxc, dwgx, dbgx, dwga, dbga, dlam = vjp((gt * h_prev, gt))
        later = xcar_ref[...]
        gcar_ref[...] = gt[0:8, :]
        acar_ref[...] = a[0:8, :]
        xcar_ref[...] = dxc[0:8, :]
        dx = dxc * cw_ref[3:4, :]
        pad = jnp.zeros((tb - 8, LANES), F32)
        for d in (1, 2, 3):
            tail = jnp.concatenate([pad, pltpu.roll(later, 8 - d, 0)], axis=0)
            up = jnp.where(rows < tb - d, pltpu.roll(dxc, tb - d, 0), tail)
            dx = dx + up * cw_ref[3 - d:4 - d, :]
        dzx_ref[...] = dx.astype(MXU_DTYPE)
        for k in range(4):
            dcw_ref[k:k + 1, :] += jnp.sum(dxc * xs[k], axis=0, keepdims=True)
        dcb_ref[...] += jnp.sum(dxc, axis=0, keepdims=True)
        dwgx_ref[...] += dwgx
        dbgx_ref[...] += dbgx
        dwga_ref[...] += dwga
        dbga_ref[...] += dbga
        dlam_ref[...] += dlam

    sds = jax.ShapeDtypeStruct
    return pl.pallas_call(
        body,
        name="lru_bwd",
        grid=(LRU_BLOCKS, nt),
        in_specs=[blk(0), halo(0), blk(LRU_BLOCKS), blk(0), halo(0), blk(0), chan(4), chan(1), wblk, bblk, wblk, bblk, chan(1)],
        out_specs=[blk(0), blk(0), wblk, bblk, wblk, bblk, chan(1), chan(4), chan(1)],
        out_shape=[sds((t, 1024), MXU_DTYPE), sds((t, 1024), MXU_DTYPE), sds(wgx.shape, F32), sds(bgx.shape, F32), sds(wga.shape, F32),
                   sds(bga.shape, F32), sds((1, 1024), F32), sds((4, 1024), F32), sds((1, 1024), F32)],
        scratch_shapes=[pltpu.VMEM((8, LANES), F32)] * 3,
        compiler_params=_cparams(("parallel", "arbitrary")),
    )(z, z, z, h, h, dy, conv_w, conv_b, wgx, bgx, wga, bga, lam)


def _mla_prep_tile(ops, cq, ckv, krp, c, s1, s2, g_cq, g_ckv, wq, wk, wv, gq, gk):
    cqn = rms(cq, g_cq, 256)
    ckvn = rms(ckv, g_ckv, 128)
    v = ops.mm(ckvn, wv)
    qs, ks = [], []
    for hd in range(MLA_HEADS):
        q = rms(ops.mm(cqn, wq[hd]), gq, MLA_QK)
        k = rms(ops.mm(ckvn, wk[hd]) + krp, gk, MLA_QK)
        qs.append(rope(ops, q, c, s1, s2, 16) * (MLA_QK ** -0.5))
        ks.append(rope(ops, k, c, s1, s2, 16))
    return tuple(qs), tuple(ks), v


def _mla_prep_args(ins):
    z_cq, z_ckv, z_kr, tc, ts1, ts2, g_cq, g_ckv, wq, wk, wv, gq, gk = ins[:13]
    heads = lambda w: tuple(w[:, LANES * hd:LANES * (hd + 1)] for hd in range(MLA_HEADS))
    return (z_cq[...], z_ckv[...], z_kr[...], tc[...], ts1[...], ts2[...], g_cq[...], g_ckv[...], heads(wq), heads(wk),
            wv[...], gq[...], gk[...])


def _mla_prep_ins(z, tabs, w):
    return [_row(z, 256, 0), _row(z, 128, 2), _row(z, 128, 3), _row(tabs, 128, 0), _row(tabs, 128, 1), _row(tabs, 128, 2),
            _const(w["cq_norm_g"]), _const(w["ckv_norm_g"]), _const(w["wq"]), _const(w["wk"]), _const(w["wv"]),
            _const(w["gq"]), _const(w["gk"])]


def mla_prep_fwd(z, tabs, w):
    def fn(ins, outs, _):
        qs, ks, v = _mla_prep_tile(PLAIN, *_mla_prep_args(ins))
        for hd in range(MLA_HEADS):
            outs[0][:, LANES * hd:LANES * (hd + 1)] = qs[hd].astype(MXU_DTYPE)
            outs[1][:, LANES * hd:LANES * (hd + 1)] = ks[hd].astype(MXU_DTYPE)
        outs[2][...] = v.astype(MXU_DTYPE)
        outs[3][...] = v.T.astype(MXU_DTYPE)

    return rowwise("mla_prep_fwd", fn, z.shape[0], ins=_mla_prep_ins(z, tabs, w),
                   outs=[(1024, 1024, MXU_DTYPE, False), (1024, 1024, MXU_DTYPE, False), (512, 512, MXU_DTYPE, False)],
                   touts=[(512, 512, MXU_DTYPE, False)])


def mla_prep_bwd(z, tabs, w, dq, dk, dv):
    def fn(ins, outs, accs):
        args = _mla_prep_args(ins)
        _, vjp = jax.vjp(functools.partial(_mla_prep_tile, DIFF), *args)
        heads = lambda ref: tuple(ref[:, LANES * hd:LANES * (hd + 1)] for hd in range(MLA_HEADS))
        dcq, dckv, dkr, _, _, _, dg_cq, dg_ckv, dwq, dwk, dwv, dgq, dgk = vjp((heads(ins[13]), heads(ins[14]), ins[15][...]))
        lane = lax.broadcasted_iota(jnp.int32, (1, LANES), 1)
        outs[0][:, 0:256] = dcq.astype(MXU_DTYPE)
        outs[0][:, 256:384] = dckv.astype(MXU_DTYPE)
        outs[0][:, 384:512] = jnp.where((lane >= MLA_NOPE) & (lane < MLA_QK), dkr, 0.0).astype(MXU_DTYPE)
        accs[0][...] += dg_cq
        accs[1][...] += dg_ckv
        for hd in range(MLA_HEADS):
            accs[2][:, LANES * hd:LANES * (hd + 1)] += dwq[hd]
            accs[3][:, LANES * hd:LANES * (hd + 1)] += dwk[hd]
        accs[4][...] += dwv
        accs[5][...] += dgq
        accs[6][...] += dgk

    return rowwise("mla_prep_bwd", fn, z.shape[0], ins=_mla_prep_ins(z, tabs, w) + [_row(dq), _row(dk), _row(dv)],
                   outs=[(512, 512, MXU_DTYPE, False)],
                   accs=[(1, 256, 256, False), (1, 128, 128, False), (256, 1024, 1024, False), (128, 1024, 1024, False),
                         (128, 512, 512, False), (1, 128, 128, False), (1, 128, 128, False)])


def _head_masks():
    lane = lax.broadcasted_iota(jnp.int32, (1, LANES), 1)
    return (lane < DIL_HD, lane >= DIL_HD)


def _row_scalar(tile, mask):
    return jnp.max(jnp.where(mask, tile, -jnp.inf), axis=-1, keepdims=True)


def _causal_tiles(nq, by_key):
    pairs = [(i, j) for i in range(nq) for j in range(i + 1)]
    if by_key:
        pairs.sort(key=lambda ij: (ij[1], ij[0]))
    return (jnp.asarray([ij[0] for ij in pairs], jnp.int32), jnp.asarray([ij[1] for ij in pairs], jnp.int32))


def mla_attn_fwd(q, k, vt, z, *, tq=256, gather=None):
    t = q.shape[0]
    nq = t // tq
    it, jt = _causal_tiles(nq, False)
    hps, wq, wv = MLA_HPS, LANES * MLA_HPS, 64 * MLA_HPS
    ws, layer = gather if gather is not None else ([], 0)
    ng = len(ws)
    ngrp, nsteps = MLA_HEADS // hps, int(it.shape[0])

    def body(it_ref, jt_ref, q_ref, k_ref, vt_ref, g_ref, *rest):
        w_refs, rest = rest[:ng], rest[ng:]
        o_ref, y_ref, yt_ref, lse_ref = rest[:4]
        ga_refs, rest = rest[4:4 + ng], rest[4 + ng:]
        m_scr, l_scr, acc_scr = rest[:3]
        step = pl.program_id(1)
        i, j = it_ref[step], jt_ref[step]
        if ng:
            gather_start, gather_finish = _layer_gather(w_refs, ga_refs, rest[3], rest[4], layer)

            @pl.when((pl.program_id(0) == 0) & (step == 0))
            def _():
                gather_start()

        @pl.when(j == 0)
        def _():
            m_scr[...] = jnp.full(m_scr.shape, NEG, F32)
            l_scr[...] = jnp.zeros(l_scr.shape, F32)
            acc_scr[...] = jnp.zeros(acc_scr.shape, F32)

        def update(diagonal):
            heads = range(hps)
            lanes = [slice(LANES * hh, LANES * (hh + 1)) for hh in heads]
            rows = [slice(64 * hh, 64 * (hh + 1)) for hh in heads]
            sts = [_dot(k_ref[:, lanes[hh]], q_ref[:, lanes[hh]], NT) for hh in heads]
            m_prev = [m_scr[hh:hh + 1, :] for hh in heads]
            l_prev = [l_scr[hh:hh + 1, :] for hh in heads]
            acc_prev = [acc_scr[rows[hh], :] for hh in heads]
            m_new, l_new, acc_new = [], [], []
            for hh in heads:
                st = sts[hh]
                if diagonal:
                    key = lax.broadcasted_iota(jnp.int32, (tq, tq), 0)
                    qry = lax.broadcasted_iota(jnp.int32, (tq, tq), 1)
                    st = jnp.where(key <= qry, st, NEG)
                m_new.append(jnp.maximum(m_prev[hh], jnp.max(st, axis=0, keepdims=True)))
                alpha = jnp.exp(m_prev[hh] - m_new[hh])
                pt = jnp.exp(st - m_new[hh])
                l_new.append(alpha * l_prev[hh] + jnp.sum(pt, axis=0, keepdims=True))
                acc_new.append(alpha * acc_prev[hh] + _dot(vt_ref[rows[hh], :], pt, NN))
            for hh in heads:
                m_scr[hh:hh + 1, :] = m_new[hh]
                l_scr[hh:hh + 1, :] = l_new[hh]
                acc_scr[rows[hh], :] = acc_new[hh]

        @pl.when(j < i)
        def _():
            update(False)

        @pl.when(j == i)
        def _():
            update(True)
            lse_ref[...] = jnp.zeros(lse_ref.shape, F32)
            for hh in range(hps):
                rows = slice(64 * hh, 64 * (hh + 1))
                acc_scr[rows, :] = acc_scr[rows, :] / l_scr[hh:hh + 1, :]
                lse_ref[hh:hh + 1, :] = m_scr[hh:hh + 1, :] + jnp.log(l_scr[hh:hh + 1, :])
            o = acc_scr[...].T
            o_ref[...] = o
            y = o * _silu_and_grad(g_ref[...])[0]
            y_ref[...] = y.astype(MXU_DTYPE)
            yt_ref[...] = y.T.astype(MXU_DTYPE)

        if ng:
            @pl.when((pl.program_id(0) == ngrp - 1) & (step == nsteps - 1))
            def _():
                gather_finish()

    qo = lambda w, off=0: pl.BlockSpec((tq, w), lambda p, s, it_, jt_: (it_[s], p + off))
    sds = jax.ShapeDtypeStruct
    comm_scratch = [pltpu.SemaphoreType.DMA((6 * ng,)), pltpu.SemaphoreType.DMA((6 * ng,))] if ng else []
    return pl.pallas_call(
        body,
        name="mla_attn_fwd_gather" if ng else "mla_attn_fwd",
        grid_spec=pltpu.PrefetchScalarGridSpec(
            num_scalar_prefetch=2,
            grid=(ngrp, nsteps),
            in_specs=[qo(wq), pl.BlockSpec((tq, wq), lambda p, s, it_, jt_: (jt_[s], p)),
                      pl.BlockSpec((wv, tq), lambda p, s, it_, jt_: (p, jt_[s])), qo(wv, 512 // wv)] + [HBM_SPEC] * ng,
            out_specs=[qo(wv), qo(wv), pl.BlockSpec((wv, tq), lambda p, s, it_, jt_: (p, it_[s])),
                       pl.BlockSpec((None, 8, tq), lambda p, s, it_, jt_: (p, 0, it_[s]))] + [HBM_SPEC] * ng,
            scratch_shapes=[pltpu.VMEM((8, tq), F32), pltpu.VMEM((8, tq), F32), pltpu.VMEM((wv, tq), F32)] + comm_scratch,
        ),
        out_shape=[sds((t, 512), F32), sds((t, 512), MXU_DTYPE), sds((512, t), MXU_DTYPE), sds((ngrp, 8, t), F32)]
        + _gather_shapes(ws),
        compiler_params=_cparams(("arbitrary", "arbitrary") if ng else ("parallel", "arbitrary")),
    )(it, jt, q, k, vt, z, *ws)


def mla_attn_bwd(q, k, v, do, lse, dd, *, tq=256, chips=None):
    t = q.shape[0]
    nq = t // tq
    it, jt = _causal_tiles(nq, True)
    hps, wq, wv = MLA_HPS, LANES * MLA_HPS, 64 * MLA_HPS
    pcs, layer = chips if chips is not None else ([], 0)
    nc = len(pcs)
    ngrp, nsteps = MLA_HEADS // hps, int(it.shape[0])

    def body(it_ref, jt_ref, q_ref, k_ref, v_ref, do_ref, lse_ref, d_ref, *rest):
        pc_refs, rest = rest[:nc], rest[nc:]
        dq_ref, dk_ref, dv_ref = rest[:3]
        lb_refs, rest = rest[3:3 + nc], rest[3 + nc:]
        dk_scr, dv_scr = rest[:2]
        step = pl.program_id(1)
        i, j = it_ref[step], jt_ref[step]
        masks = _head_masks()
        if nc:
            chips_start, chips_finish = _chips_exchange(pc_refs, lb_refs, rest[2], rest[3], layer)

            @pl.when((pl.program_id(0) == 0) & (step == 0))
            def _():
                chips_start()

        @pl.when(step == 0)
        def _():
            dq_ref[...] = jnp.zeros(dq_ref.shape, F32)

        @pl.when(i == j)
        def _():
            dk_scr[...] = jnp.zeros(dk_scr.shape, F32)
            dv_scr[...] = jnp.zeros(dv_scr.shape, F32)

        def update(diagonal):
            qrows = pl.ds(pl.multiple_of(i * tq, tq), tq)
            heads = range(hps)
            lanes = [slice(LANES * hh, LANES * (hh + 1)) for hh in heads]
            pair = [slice(LANES * (hh // 2), LANES * (hh // 2 + 1)) for hh in heads]
            qh = [q_ref[:, lanes[hh]] for hh in heads]
            kh = [k_ref[:, lanes[hh]] for hh in heads]
            doh = []
            for hh in heads:
                dov = do_ref[:, pair[hh]]
                doh.append(jnp.where(masks[hh % 2], dov, jnp.zeros_like(dov)))
            sts = [_dot(kh[hh], qh[hh], NT) for hh in heads]
            dps = [_dot(v_ref[:, pair[hh]], doh[hh], NT) for hh in heads]
            lse = [lse_ref[hh:hh + 1, :] for hh in heads]
            ddv = [d_ref[hh:hh + 1, :] for hh in heads]
            dk_new = [dk_scr[:, lanes[hh]] for hh in heads]
            dq_new = [dq_ref[qrows, lanes[hh]] for hh in heads]
            dv_new = [dv_scr[:, pair[2 * pp]] for pp in range(hps // 2)]
            for hh in heads:
                st = sts[hh] - lse[hh]
                if diagonal:
                    key = lax.broadcasted_iota(jnp.int32, (tq, tq), 0)
                    qry = lax.broadcasted_iota(jnp.int32, (tq, tq), 1)
                    st = jnp.where(key <= qry, st, NEG)
                pt = jnp.exp(st)
                dst = pt * (dps[hh] - ddv[hh])
                dv_new[hh // 2] = dv_new[hh // 2] + _dot(pt, doh[hh], NN)
                dk_new[hh] = dk_new[hh] + _dot(dst, qh[hh], NN)
                dq_new[hh] = dq_new[hh] + _dot(dst, kh[hh], TN)
            for hh in heads:
                dk_scr[:, lanes[hh]] = dk_new[hh]
                dq_ref[qrows, lanes[hh]] = dq_new[hh]
            for pp in range(hps // 2):
                dv_scr[:, pair[2 * pp]] = dv_new[pp]

        @pl.when(j < i)
        def _():
            update(False)

        @pl.when(j == i)
        def _():
            update(True)

        @pl.when(i == nq - 1)
        def _():
            dk_ref[...] = dk_scr[...]
            dv_ref[...] = dv_scr[...]

        if nc:
            @pl.when((pl.program_id(0) == ngrp - 1) & (step == nsteps - 1))
            def _():
                chips_finish()

    qo = lambda w: pl.BlockSpec((tq, w), lambda p, s, it_, jt_: (it_[s], p))
    kv = lambda w: pl.BlockSpec((tq, w), lambda p, s, it_, jt_: (jt_[s], p))
    stat = pl.BlockSpec((None, 8, tq), lambda p, s, it_, jt_: (p, 0, it_[s]))
    sds = jax.ShapeDtypeStruct
    comm_scratch = [pltpu.SemaphoreType.DMA((3 * nc,)), pltpu.SemaphoreType.DMA((3 * nc,))] if nc else []
    return pl.pallas_call(
        body,
        name="mla_attn_bwd_chips" if nc else "mla_attn_bwd",
        grid_spec=pltpu.PrefetchScalarGridSpec(
            num_scalar_prefetch=2,
            grid=(ngrp, nsteps),
            in_specs=[qo(wq), kv(wq), kv(wv), qo(wv), stat, stat] + [HBM_SPEC] * nc,
            out_specs=[pl.BlockSpec((t, wq), lambda p, s, it_, jt_: (0, p)), kv(wq), kv(wv)] + [HBM_SPEC] * nc,
            scratch_shapes=[pltpu.VMEM((tq, wq), F32), pltpu.VMEM((tq, wv), F32)] + comm_scratch,
        ),
        out_shape=[sds((t, 1024), F32), sds((t, 1024), F32), sds((t, 512), F32)] + _chips_shapes(pcs),
        compiler_params=_cparams(("arbitrary", "arbitrary") if nc else ("parallel", "arbitrary")),
    )(it, jt, q, k, v, do, lse, dd, *pcs)


def mla_gate_bwd(dy, o, z, *, tm=256):
    t = dy.shape[0]
    wv = 64 * MLA_HPS

    def body(dy_ref, o_ref, g_ref, do_ref, dzg_ref, dd_ref):
        dyv, ov = dy_ref[...], o_ref[...]
        silu, dsilu = _silu_and_grad(g_ref[...])
        do = dyv * silu
        do_ref[...] = do.astype(MXU_DTYPE)
        dzg_ref[...] = (dyv * ov * dsilu).astype(MXU_DTYPE)
        prod = do * ov
        row = lax.broadcasted_iota(jnp.int32, (8, wv), 0)
        lane = lax.broadcasted_iota(jnp.int32, (8, wv), 1)
        pick = ((lane >= row * DIL_HD) & (lane < (row + 1) * DIL_HD)).astype(BF16)
        hi = prod.astype(BF16)
        r1 = prod - hi.astype(F32)
        mid = r1.astype(BF16)
        lo = (r1 - mid.astype(F32)).astype(BF16)
        dot = lambda u: lax.dot_general(pick, u, NT, preferred_element_type=F32)
        dd_ref[...] = dot(hi) + dot(mid) + dot(lo)

    blk = lambda off=0: pl.BlockSpec((tm, wv), lambda p, i: (i, p + off))
    sds = jax.ShapeDtypeStruct
    return pl.pallas_call(
        body,
        name="mla_gate_bwd",
        grid=(MLA_HEADS // MLA_HPS, t // tm),
        in_specs=[blk(), blk(), blk(512 // wv)],
        out_specs=[blk(), blk(), pl.BlockSpec((None, 8, tm), lambda p, i: (p, 0, i))],
        out_shape=[sds((t, 512), MXU_DTYPE), sds((t, 512), MXU_DTYPE), sds((MLA_HEADS // MLA_HPS, 8, t), F32)],
        compiler_params=_cparams(("parallel", "parallel")),
    )(dy, o, z)


SPAN = 2048
DIL_SCALE = DIL_HD ** -0.5


def _rm_src(u, window, dil):
    nn, r = divmod(u, dil)
    return pl.ds(nn * window + r, DIL_NK, stride=dil) if dil > 1 else pl.ds(u * DIL_NK, DIL_NK)


def _rm_dst(u):
    return pl.ds(u * DIL_NK, DIL_NK)


def _dil_prep_tile(ops, x, g2, c, s1, s2, scale):
    ms = ops.seg_sum(x * x) * (1.0 / DIL_HD)
    return rope(ops, x * lax.rsqrt(ms + EPS) * g2, c, s1, s2, 32) * scale


def _span_blk(base):
    return pl.BlockSpec((SPAN, LANES), lambda s, p: (s, base + p))


def _const_blk(shape):
    return pl.BlockSpec(shape, lambda s, p: (0,) * len(shape))


_DIL_TABLE_SPECS = [pl.BlockSpec((SPAN, LANES), (lambda s, p, blk=blk: (s, blk))) for blk in (3, 4, 5)]


def dil_prep_fwd(z, tabs, gq2, gk2, gi):
    window, dil = DIL_GROUPS[gi]
    t = z.shape[0]

    def body(q_ref, k_ref, v_ref, c_ref, s1_ref, s2_ref, gq_ref, gk_ref, qo_ref, ko_ref, vo_ref):
        for u in range(SPAN // DIL_NK):
            src, dst = _rm_src(u, window, dil), _rm_dst(u)
            c, s1, s2 = c_ref[src, :], s1_ref[src, :], s2_ref[src, :]
            qo_ref[dst, :] = _dil_prep_tile(PLAIN, q_ref[src, :], gq_ref[...], c, s1, s2, DIL_SCALE).astype(MXU_DTYPE)
            ko_ref[dst, :] = _dil_prep_tile(PLAIN, k_ref[src, :], gk_ref[...], c, s1, s2, 1.0).astype(MXU_DTYPE)
            vo_ref[dst, :] = v_ref[src, :].astype(MXU_DTYPE)

    return pl.pallas_call(
        body,
        name=f"dil_prep_fwd_{dil}",
        grid=(t // SPAN, 4),
        in_specs=[_span_blk(4 * gi), _span_blk(12 + 4 * gi), _span_blk(24 + 4 * gi)] + _DIL_TABLE_SPECS
        + [_const_blk((1, LANES)), _const_blk((1, LANES))],
        out_specs=[_span_blk(0)] * 3,
        out_shape=[jax.ShapeDtypeStruct((t, 512), MXU_DTYPE)] * 3,
        compiler_params=_cparams(("parallel", "parallel")),
    )(z, z, z, tabs, tabs, tabs, gq2, gk2)


def dil_prep_bwd(z, tabs, gq2, gk2, dq, dk, dv, gi):
    window, dil = DIL_GROUPS[gi]
    t = z.shape[0]

    def body(q_ref, k_ref, c_ref, s1_ref, s2_ref, gq_ref, gk_ref, dq_ref, dk_ref, dv_ref, dzq_ref, dzk_ref, dzv_ref, dgq_ref,
             dgk_ref, sq, sk, sv):
        @pl.when((pl.program_id(0) == 0) & (pl.program_id(1) == 0))
        def _():
            dgq_ref[...] = jnp.zeros((1, LANES), F32)
            dgk_ref[...] = jnp.zeros((1, LANES), F32)

        dgs = [jnp.zeros((1, LANES), F32), jnp.zeros((1, LANES), F32)]
        for u in range(SPAN // DIL_NK):
            src, dst = _rm_src(u, window, dil), _rm_dst(u)
            c, s1, s2 = c_ref[src, :], s1_ref[src, :], s2_ref[src, :]
            for idx, (x_ref, g_ref, ct_ref, scr, scale) in enumerate(((q_ref, gq_ref, dq_ref, sq, DIL_SCALE),
                                                                      (k_ref, gk_ref, dk_ref, sk, 1.0))):
                _, vjp = jax.vjp(lambda xv, gv, sc=scale: _dil_prep_tile(DIFF, xv, gv, c, s1, s2, sc), x_ref[src, :], g_ref[...])
                dx, dg = vjp(ct_ref[dst, :])
                scr[src, :] = dx
                dgs[idx] = dgs[idx] + dg
            sv[src, :] = dv_ref[dst, :]
        dgq_ref[...] += dgs[0] + pltpu.roll(dgs[0], DIL_HD, 1)
        dgk_ref[...] += dgs[1] + pltpu.roll(dgs[1], DIL_HD, 1)
        for c0 in range(0, SPAN, 256):
            rows = slice(c0, c0 + 256)
            dzq_ref[rows, :] = sq[rows, :].astype(MXU_DTYPE)
            dzk_ref[rows, :] = sk[rows, :].astype(MXU_DTYPE)
            dzv_ref[rows, :] = sv[rows, :].astype(MXU_DTYPE)

    sds = jax.ShapeDtypeStruct
    return pl.pallas_call(
        body,
        name=f"dil_prep_bwd_{dil}",
        grid=(t // SPAN, 4),
        in_specs=[_span_blk(4 * gi), _span_blk(12 + 4 * gi)] + _DIL_TABLE_SPECS
        + [_const_blk((1, LANES)), _const_blk((1, LANES)), _span_blk(0), _span_blk(0), _span_blk(0)],
        out_specs=[_span_blk(0)] * 3 + [_const_blk((1, LANES))] * 2,
        out_shape=[sds((t, 512), MXU_DTYPE)] * 3 + [sds((1, LANES), F32)] * 2,
        scratch_shapes=[pltpu.VMEM((SPAN, LANES), F32)] * 3,
        compiler_params=_cparams(("arbitrary", "arbitrary")),
    )(z, z, tabs, tabs, tabs, gq2, gk2, dq, dk, dv)


def _band_masks():
    qi = lax.broadcasted_iota(jnp.int32, (DIL_NK, DIL_NK), 0)
    ki = lax.broadcasted_iota(jnp.int32, (DIL_NK, DIL_NK), 1)
    return (ki >= qi), (ki <= qi)


def _pair_heads():
    return [(pair, hh) for pair in range(4) for hh in range(2)]


def _pair_lanes(pair):
    return slice(LANES * pair, LANES * (pair + 1))


def _zero_other_head(mask, x):
    return jnp.where(mask, x, jnp.zeros_like(x))


def dil_attn_fwd(name, q, k, v, dil):
    t = q.shape[0]

    def body(q_ref, kp_ref, kc_ref, vp_ref, vc_ref, o_ref, lse_ref):
        b = pl.program_id(0)
        mprev, mcur = _band_masks()
        mprev = mprev & (b >= dil)
        hm = _head_masks()
        heads = _pair_heads()
        qh = [_zero_other_head(hm[hh], q_ref[:, _pair_lanes(pair)]) for pair, hh in heads]
        sps = [_dot(qh[i], kp_ref[:, _pair_lanes(pair)], NT) for i, (pair, _) in enumerate(heads)]
        scs = [_dot(qh[i], kc_ref[:, _pair_lanes(pair)], NT) for i, (pair, _) in enumerate(heads)]
        o = [jnp.zeros((DIL_NK, LANES), F32) for _ in range(4)]
        lse = [jnp.zeros((DIL_NK, LANES), F32) for _ in range(4)]
        for i, (pair, hh) in enumerate(heads):
            sp, sc = jnp.where(mprev, sps[i], NEG), jnp.where(mcur, scs[i], NEG)
            m = jnp.maximum(jnp.max(sp, axis=1, keepdims=True), jnp.max(sc, axis=1, keepdims=True))
            ep, ec = jnp.exp(sp - m), jnp.exp(sc - m)
            den = jnp.sum(ep, axis=1, keepdims=True) + jnp.sum(ec, axis=1, keepdims=True)
            lanes = _pair_lanes(pair)
            oh = _dot(ep, _zero_other_head(hm[hh], vp_ref[:, lanes]), NN) + _dot(ec, _zero_other_head(hm[hh], vc_ref[:, lanes]), NN)
            o[pair] = o[pair] + oh / den
            lse[pair] = jnp.where(hm[hh], m + jnp.log(den), lse[pair])
        for pair in range(4):
            o_ref[:, _pair_lanes(pair)] = o[pair]
            lse_ref[:, _pair_lanes(pair)] = lse[pair]

    cur = pl.BlockSpec((DIL_NK, 512), lambda b: (b, 0))
    prev = pl.BlockSpec((DIL_NK, 512), lambda b: (jnp.maximum(b - dil, 0), 0))
    sds = jax.ShapeDtypeStruct
    return pl.pallas_call(
        body,
        name=name,
        grid=(t // DIL_NK,),
        in_specs=[cur, prev, cur, prev, cur],
        out_specs=[cur, cur],
        out_shape=[sds((t, 512), F32), sds((t, 512), F32)],
        compiler_params=_cparams(("parallel",)),
    )(q, k, k, v, v)


def dil_attn_bwd(name, q, k, v, do, lse, dd, dil):
    t = q.shape[0]
    nblk = t // DIL_NK

    def body(qc_ref, qn_ref, doc_ref, don_ref, lc_ref, ln_ref, dc_ref, dn_ref, kp_ref, kc_ref, vp_ref, vc_ref,
             dq_ref, dk_ref, dv_ref):
        b = pl.program_id(0)
        mprev, mcur = _band_masks()
        mnext = mprev & (b + dil < nblk)
        mprev = mprev & (b >= dil)
        hm = _head_masks()
        heads = _pair_heads()
        ln_ = [_pair_lanes(pair) for pair, _ in heads]
        qh = [_zero_other_head(hm[hh], qc_ref[:, ln_[i]]) for i, (_, hh) in enumerate(heads)]
        qnh = [_zero_other_head(hm[hh], qn_ref[:, ln_[i]]) for i, (_, hh) in enumerate(heads)]
        doh = [_zero_other_head(hm[hh], doc_ref[:, ln_[i]]) for i, (_, hh) in enumerate(heads)]
        donh = [_zero_other_head(hm[hh], don_ref[:, ln_[i]]) for i, (_, hh) in enumerate(heads)]
        idx = range(len(heads))
        s_p = [_dot(qh[i], kp_ref[:, ln_[i]], NT) for i in idx]
        s_c = [_dot(qh[i], kc_ref[:, ln_[i]], NT) for i in idx]
        s_n = [_dot(qnh[i], kc_ref[:, ln_[i]], NT) for i in idx]
        dp_p = [_dot(doh[i], vp_ref[:, ln_[i]], NT) for i in idx]
        dp_c = [_dot(doh[i], vc_ref[:, ln_[i]], NT) for i in idx]
        dp_n = [_dot(donh[i], vc_ref[:, ln_[i]], NT) for i in idx]
        dq = [jnp.zeros((DIL_NK, LANES), F32) for _ in range(4)]
        dk = [jnp.zeros((DIL_NK, LANES), F32) for _ in range(4)]
        dv = [jnp.zeros((DIL_NK, LANES), F32) for _ in range(4)]
        for i, (pair, hh) in enumerate(heads):
            lse_h, d_h = _row_scalar(lc_ref[:, ln_[i]], hm[hh]), _row_scalar(dc_ref[:, ln_[i]], hm[hh])
            lse_n, d_n = _row_scalar(ln_ref[:, ln_[i]], hm[hh]), _row_scalar(dn_ref[:, ln_[i]], hm[hh])
            pp = jnp.exp(jnp.where(mprev, s_p[i] - lse_h, NEG))
            pc = jnp.exp(jnp.where(mcur, s_c[i] - lse_h, NEG))
            pn = jnp.exp(jnp.where(mnext, s_n[i] - lse_n, NEG))
            dsp, dsc, dsn = pp * (dp_p[i] - d_h), pc * (dp_c[i] - d_h), pn * (dp_n[i] - d_n)
            dq[pair] = (dq[pair] + _dot(dsp, _zero_other_head(hm[hh], kp_ref[:, ln_[i]]), NN)
                        + _dot(dsc, _zero_other_head(hm[hh], kc_ref[:, ln_[i]]), NN))
            dv[pair] = dv[pair] + _dot(pc, doh[i], TN) + _dot(pn, donh[i], TN)
            dk[pair] = dk[pair] + _dot(dsc, qh[i], TN) + _dot(dsn, qnh[i], TN)
        for pair in range(4):
            dq_ref[:, _pair_lanes(pair)] = dq[pair]
            dk_ref[:, _pair_lanes(pair)] = dk[pair]
            dv_ref[:, _pair_lanes(pair)] = dv[pair]

    cur = pl.BlockSpec((DIL_NK, 512), lambda b: (b, 0))
    prev = pl.BlockSpec((DIL_NK, 512), lambda b: (jnp.maximum(b - dil, 0), 0))
    nxt = pl.BlockSpec((DIL_NK, 512), lambda b: (jnp.minimum(b + dil, nblk - 1), 0))
    sds = jax.ShapeDtypeStruct
    return pl.pallas_call(
        body,
        name=name,
        grid=(nblk,),
        in_specs=[cur, nxt, cur, nxt, cur, nxt, cur, nxt, prev, cur, prev, cur],
        out_specs=[cur, cur, cur],
        out_shape=[sds((t, 512), F32)] * 3,
        compiler_params=_cparams(("parallel",)),
    )(q, q, do, do, lse, lse, dd, dd, k, k, v, v)


def dil_combine(os_, lses, z):
    t = z.shape[0]

    def body(o1_ref, l1_ref, o2_ref, l2_ref, o3_ref, l3_ref, g_ref, y_ref, o_ref, lse_ref, yt_ref, so2, sl2, so3, sl3):
        for (window, dil), o_in, l_in, so, sl in ((DIL_GROUPS[1], o2_ref, l2_ref, so2, sl2), (DIL_GROUPS[2], o3_ref, l3_ref, so3, sl3)):
            for u in range(SPAN // DIL_NK):
                src, dst = _rm_src(u, window, dil), _rm_dst(u)
                so[src, :] = o_in[dst, :]
                sl[src, :] = l_in[dst, :]
        for c0 in range(0, SPAN, 256):
            rows = slice(c0, c0 + 256)
            la, lb, lc = l1_ref[rows, :], sl2[rows, :], sl3[rows, :]
            mx = jnp.maximum(jnp.maximum(la, lb), lc)
            wa, wb, wc = jnp.exp(la - mx), jnp.exp(lb - mx), jnp.exp(lc - mx)
            tot = wa + wb + wc
            o = (wa * o1_ref[rows, :] + wb * so2[rows, :] + wc * so3[rows, :]) / tot
            y = o * _silu_and_grad(g_ref[rows, :])[0]
            y_ref[rows, :] = y.astype(MXU_DTYPE)
            o_ref[rows, :] = o
            lse_ref[rows, :] = mx + jnp.log(tot)
            yt_ref[:, rows] = y.T.astype(MXU_DTYPE)

    sds = jax.ShapeDtypeStruct
    return pl.pallas_call(
        body,
        name="dil_combine",
        grid=(t // SPAN, 4),
        in_specs=[_span_blk(0)] * 6 + [_span_blk(36)],
        out_specs=[_span_blk(0)] * 3 + [pl.BlockSpec((LANES, SPAN), lambda s, p: (p, s))],
        out_shape=[sds((t, 512), MXU_DTYPE), sds((t, 512), F32), sds((t, 512), F32), sds((512, t), MXU_DTYPE)],
        scratch_shapes=[pltpu.VMEM((SPAN, LANES), F32)] * 4,
        compiler_params=_cparams(("parallel", "parallel")),
    )(os_[0], lses[0], os_[1], lses[1], os_[2], lses[2], z)


def dil_gate_bwd(dy, o, z, lse):
    t = dy.shape[0]

    def body(dy_ref, o_ref, g_ref, lse_ref, dzg_ref, do1_ref, dd1_ref, do2_ref, dd2_ref, l2_ref, do3_ref, dd3_ref, l3_ref, do_scr):
        for c0 in range(0, SPAN, 256):
            rows = slice(c0, c0 + 256)
            dyv, ov = dy_ref[rows, :], o_ref[rows, :]
            silu, dsilu = _silu_and_grad(g_ref[rows, :])
            do = dyv * silu
            do_scr[rows, :] = do
            do1_ref[rows, :] = do.astype(MXU_DTYPE)
            dd1_ref[rows, :] = _seg_sum_impl(do * ov)
            dzg_ref[rows, :] = (dyv * ov * dsilu).astype(MXU_DTYPE)
        for (window, dil), do_o, dd_o, l_o in ((DIL_GROUPS[1], do2_ref, dd2_ref, l2_ref), (DIL_GROUPS[2], do3_ref, dd3_ref, l3_ref)):
            for u in range(SPAN // DIL_NK):
                src, dst = _rm_src(u, window, dil), _rm_dst(u)
                do_o[dst, :] = do_scr[src, :].astype(MXU_DTYPE)
                dd_o[dst, :] = dd1_ref[src, :]
                l_o[dst, :] = lse_ref[src, :]

    sds = jax.ShapeDtypeStruct
    f32, mxu = sds((t, 512), F32), sds((t, 512), MXU_DTYPE)
    return pl.pallas_call(
        body,
        name="dil_gate_bwd",
        grid=(t // SPAN, 4),
        in_specs=[_span_blk(0), _span_blk(0), _span_blk(36), _span_blk(0)],
        out_specs=[_span_blk(0)] * 9,
        out_shape=[mxu, mxu, f32, mxu, f32, f32, mxu, f32, f32],
        scratch_shapes=[pltpu.VMEM((SPAN, LANES), F32)],
        compiler_params=_cparams(("parallel", "parallel")),
    )(dy, o, z, lse)


def _merge_tile(p0, p1, p2, z0, z1, z2, b0, b1, b2):
    return _sigmoid(z0 + b0) * p0 + _sigmoid(z1 + b1) * p1 + _sigmoid(z2 + b2) * p2


def _merge_ins(ps, z, b):
    w = 256
    return ([_row(p, w, 0, True) for p in ps] + [_row(z, w, 4 * i, True) for i in range(3)]
            + [_const(b, w, 4 * i, True) for i in range(3)])


def merge_fwd(ps, z, b):
    def fn(ins, outs, _):
        merged = _merge_tile(*[r[...] for r in ins])
        outs[0][...] = merged.astype(MXU_DTYPE)
        outs[1][...] = merged.T.astype(MXU_DTYPE)

    return rowwise("merge_fwd", fn, z.shape[0], ncol=4, ins=_merge_ins(ps, z, b), outs=[(1024, 256, MXU_DTYPE, True)],
                   touts=[(1024, 256, MXU_DTYPE, True)])


def merge_bwd(dm, ps, z, b):
    def fn(ins, outs, accs):
        _, vjp = jax.vjp(_merge_tile, *[r[...] for r in ins[:9]])
        grads = vjp(ins[9][...])
        for i in range(3):
            outs[i][...] = grads[i].astype(MXU_DTYPE)
            outs[3 + i][...] = grads[3 + i].astype(MXU_DTYPE)
            accs[i][...] += grads[6 + i]

    return rowwise("merge_bwd", fn, z.shape[0], ncol=4, ins=_merge_ins(ps, z, b) + [_row(dm, 256, 0, True)],
                   outs=[(1024, 256, MXU_DTYPE, True)] * 6, accs=[(1, 1024, 256, True)] * 3)


EW_BLOCK_BYTES = 2**21


def _tile2d(r, c):
    if r * c * 4 <= EW_BLOCK_BYTES:
        return r, c
    for tr in (512, 256, 128, 64, 32):
        if r % tr == 0 and tr * c * 4 <= EW_BLOCK_BYTES:
            return tr, c
    for tc in (1024, 512, 256, 128):
        if c % tc == 0 and r * tc * 4 <= EW_BLOCK_BYTES:
            return r, tc
    raise ValueError((r, c))


def adamw(name, w, g, m, v):
    shape = w.shape
    c = shape[-1]
    r = w.size // c
    tr, tc = _tile2d(r, c)
    assert tc == c
    c1, c2 = 1.0 - ADAM_B1 ** ADAM_STEP, 1.0 - ADAM_B2 ** ADAM_STEP

    def body(w_ref, g_ref, m_ref, v_ref, d_ref, mo_ref, vo_ref):
        gv = g_ref[...]
        mn = ADAM_B1 * m_ref[...] + (1.0 - ADAM_B1) * gv
        vn = ADAM_B2 * v_ref[...] + (1.0 - ADAM_B2) * (gv * gv)
        d_ref[...] = -ADAM_LR * ((mn / c1) / (jnp.sqrt(vn / c2) + ADAM_EPS) + ADAM_WD * w_ref[...])
        mo_ref[...] = mn
        vo_ref[...] = vn

    spec = pl.BlockSpec((tr, c), lambda i: (i, 0))
    outs = pl.pallas_call(
        body,
        name=name,
        grid=(r // tr,),
        in_specs=[spec] * 4,
        out_specs=[spec] * 3,
        out_shape=[jax.ShapeDtypeStruct((r, c), F32)] * 3,
        compiler_params=_cparams(("parallel",)),
    )(*[a.reshape(r, c) for a in (w, g, m, v)])
    return [o.reshape(shape) for o in outs]


def adamw_pair(name, w, reduced, received, m, v, c_idx):
    _, r, c = w.shape
    tr, tc = _tile2d(r, c)
    c1, c2 = 1.0 - ADAM_B1 ** ADAM_STEP, 1.0 - ADAM_B2 ** ADAM_STEP

    def body(c_ref, w_ref, r0_ref, x0_ref, r1_ref, x1_ref, m_ref, v_ref, g_ref, d_ref, mo_ref, vo_ref):
        mine = c_ref[0]
        gv = jnp.where(pl.program_id(0) == 0, jnp.where(mine == 0, r0_ref[...], x0_ref[...]),
                       jnp.where(mine == 1, r1_ref[...], x1_ref[...]))
        mn = ADAM_B1 * m_ref[...] + (1.0 - ADAM_B1) * gv
        vn = ADAM_B2 * v_ref[...] + (1.0 - ADAM_B2) * (gv * gv)
        g_ref[...] = gv
        d_ref[...] = -ADAM_LR * ((mn / c1) / (jnp.sqrt(vn / c2) + ADAM_EPS) + ADAM_WD * w_ref[...])
        mo_ref[...] = mn
        vo_ref[...] = vn

    full = pl.BlockSpec((None, tr, tc), lambda h, i, j, cr: (h, i, j))
    half = pl.BlockSpec((tr, tc), lambda h, i, j, cr: (i, j))
    return pl.pallas_call(
        body,
        name=name,
        grid_spec=pltpu.PrefetchScalarGridSpec(
            num_scalar_prefetch=1,
            grid=(2, r // tr, c // tc),
            in_specs=[full, half, half, half, half, full, full],
            out_specs=[full] * 4,
        ),
        out_shape=[jax.ShapeDtypeStruct((2, r, c), F32)] * 4,
        compiler_params=_cparams(("parallel", "parallel", "parallel")),
    )(c_idx, w, reduced[0], received[0], reduced[1], received[1], m, v)


def sum_pair(name, g, la, out_dtype):
    _, r, c = g.shape
    tr, tc = _tile2d(r, c)

    def body(g_ref, la_ref, o_ref):
        o_ref[...] = (g_ref[...] + la_ref[...]).astype(o_ref.dtype)

    spec = pl.BlockSpec((None, tr, tc), lambda s, i, j: (s, i, j))
    return pl.pallas_call(
        body,
        name=name,
        grid=(4, r // tr, c // tc),
        in_specs=[spec, spec],
        out_specs=spec,
        out_shape=jax.ShapeDtypeStruct((4, r, c), out_dtype),
        compiler_params=_cparams(("parallel", "parallel", "parallel")),
    )(g, la)


def sum_chips(name, g, la, lb, s_idx):
    _, r, c = g.shape
    tr, tc = _tile2d(r, c)

    def body(s_ref, g_ref, la_ref, l0_ref, l1_ref, l2_ref, o_ref):
        own = g_ref[...] + la_ref[...]
        o_ref[...] = ((own + l0_ref[...].astype(F32)) + l1_ref[...].astype(F32)) + l2_ref[...].astype(F32)

    own_spec = pl.BlockSpec((None, tr, tc), lambda i, j, sr: (sr[0], i, j))
    lspec = lambda k: pl.BlockSpec((None, tr, tc), lambda i, j, sr: (k, i, j))
    return pl.pallas_call(
        body,
        name=name,
        grid_spec=pltpu.PrefetchScalarGridSpec(
            num_scalar_prefetch=1,
            grid=(r // tr, c // tc),
            in_specs=[own_spec, own_spec, lspec(0), lspec(1), lspec(2)],
            out_specs=pl.BlockSpec((tr, tc), lambda i, j, sr: (i, j)),
        ),
        out_shape=jax.ShapeDtypeStruct((r, c), F32),
        compiler_params=_cparams(("parallel", "parallel")),
    )(s_idx, g, la, lb, lb, lb)


def _place():
    x, y, c = lax.axis_index("x"), lax.axis_index("y"), lax.axis_index("c")
    chips = [(1 - x, y), (x, 1 - y), (1 - x, 1 - y)]
    return x, y, c, chips


HBM_SPEC = pl.BlockSpec(memory_space=pltpu.HBM)


def _comm_call(name, body, ins, out_shapes, n_sem, n_local):
    return pl.pallas_call(
        body,
        name=name,
        in_specs=[HBM_SPEC] * len(ins),
        out_specs=[HBM_SPEC] * len(out_shapes),
        out_shape=out_shapes,
        scratch_shapes=[pltpu.SemaphoreType.DMA((n_sem,)), pltpu.SemaphoreType.DMA((n_sem,)),
                        pltpu.SemaphoreType.DMA((max(n_local, 1),))],
    )(*ins)


def _layer_gather(ins, outs, send, recv, layer):
    n = len(ins)
    x, y, c, chips = _place()
    s = 2 * x + y
    sib = (x, y, 1 - c)
    active = c == layer

    def rc(a, k, src, dst, dev):
        return pltpu.make_async_remote_copy(src_ref=src, dst_ref=dst, send_sem=send.at[6 * a + k], recv_sem=recv.at[6 * a + k],
                                            device_id=dev, device_id_type=MESH)

    def first_hop():
        return [rc(a, j, ins[a], outs[a].at[s], (*chip, c)) for j, chip in enumerate(chips) for a in range(n)]

    def start():
        @pl.when(active)
        def _():
            for cp in first_hop():
                cp.start()

    def finish():
        @pl.when(active)
        def _():
            forwards = []
            for j, (cx, cy) in enumerate(chips):
                for a in range(n):
                    landed = outs[a].at[2 * cx + cy]
                    rc(a, j, landed, landed, sib).wait_recv()
                    forwards.append(rc(a, 3 + j, landed, landed, sib))
                    forwards[-1].start()
            for cp in first_hop() + forwards:
                cp.wait_send()

        @pl.when(jnp.logical_not(active))
        def _():
            for j, (cx, cy) in enumerate(chips):
                for a in range(n):
                    other = outs[a].at[2 * cx + cy]
                    rc(a, 3 + j, other, other, sib).wait_recv()

    return start, finish


def _gather_shapes(ws):
    return [jax.ShapeDtypeStruct((4,) + w.shape, w.dtype) for w in ws]


def allgather_layer(ws, layer):
    n = len(ws)

    def body(*refs):
        send, recv, _ = refs[2 * n:]
        start, finish = _layer_gather(refs[:n], refs[n:2 * n], send, recv, layer)
        start()
        finish()

    return _comm_call(f"allgather_layer{layer}", body, ws, _gather_shapes(ws), 6 * n, 0)


def exchange_sibling(gs, layer):
    n = len(gs)

    def body(*refs):
        ins, outs = refs[:n], refs[n:2 * n]
        send, recv, _ = refs[2 * n:]
        x, y, c, _ = _place()
        cps = [pltpu.make_async_remote_copy(src_ref=ins[a], dst_ref=outs[a], send_sem=send.at[a], recv_sem=recv.at[a],
                                            device_id=(x, y, 1 - c), device_id_type=MESH) for a in range(n)]

        @pl.when(c != layer)
        def _():
            for cp in cps:
                cp.start()
            for cp in cps:
                cp.wait_send()

        @pl.when(c == layer)
        def _():
            for cp in cps:
                cp.wait_recv()

    return _comm_call(f"exchange_sibling{layer}", body, gs, [jax.ShapeDtypeStruct(g.shape, g.dtype) for g in gs], n, 0)


def _chips_exchange(ins, outs, send, recv, layer):
    n = len(ins)
    _, _, c, chips = _place()

    def copies():
        return [pltpu.make_async_remote_copy(src_ref=ins[a].at[2 * cx + cy], dst_ref=outs[a].at[j], send_sem=send.at[3 * a + j],
                                             recv_sem=recv.at[3 * a + j], device_id=(cx, cy, c), device_id_type=MESH)
                for j, (cx, cy) in enumerate(chips) for a in range(n)]

    def start():
        @pl.when(c == layer)
        def _():
            for cp in copies():
                cp.start()

    def finish():
        @pl.when(c == layer)
        def _():
            for cp in copies():
                cp.wait()

    return start, finish


def _chips_shapes(ps):
    return [jax.ShapeDtypeStruct((3,) + p.shape[1:], p.dtype) for p in ps]


def exchange_chips(ps, layer):
    n = len(ps)

    def body(*refs):
        send, recv, _ = refs[2 * n:]
        start, finish = _chips_exchange(refs[:n], refs[n:2 * n], send, recv, layer)
        start()
        finish()

    return _comm_call(f"exchange_chips{layer}", body, ps, _chips_shapes(ps), 3 * n, 0)


def exchange_final(rs, layer, small=None):
    n = len(rs)
    ins_all = list(rs) + ([small] if small is not None else [])

    def body(*refs):
        ni = len(ins_all)
        ins, outs = refs[:ni], refs[ni:2 * ni]
        send, recv, lsem = refs[2 * ni:]
        x, y, c, _ = _place()
        s = 2 * x + y
        to_sib = [pltpu.make_async_remote_copy(src_ref=ins[a], dst_ref=outs[a], send_sem=send.at[a], recv_sem=recv.at[a],
                                               device_id=(x, y, 1 - c), device_id_type=MESH) for a in range(n)]
        rel = [(fx, fy) for fx in (0, 1) for fy in (0, 1)]

        def piece(fx, fy, cc, dst_slot):
            k = n + 2 * (2 * fx + fy) + cc
            return pltpu.make_async_remote_copy(src_ref=ins[n], dst_ref=outs[n].at[dst_slot], send_sem=send.at[k], recv_sem=recv.at[k],
                                                device_id=(x ^ fx, y ^ fy, cc), device_id_type=MESH)

        @pl.when(c == layer)
        def _():
            for cp in to_sib:
                cp.start()
            if small is not None:
                own = pltpu.make_async_copy(ins[n], outs[n].at[s], lsem.at[0])
                own.start()
                sends = [piece(fx, fy, cc, s) for fx, fy in rel for cc in (0, 1) if (fx, fy) != (0, 0) or cc != layer]
                for cp in sends:
                    cp.start()
                for fx, fy in rel[1:]:
                    piece(fx, fy, layer, 2 * (x ^ fx) + (y ^ fy)).wait_recv()
                for cp in sends:
                    cp.wait_send()
                own.wait()
            for cp in to_sib:
                cp.wait_send()

        @pl.when(c != layer)
        def _():
            for cp in to_sib:
                cp.wait_recv()
            if small is not None:
                for fx, fy in rel:
                    piece(fx, fy, 1 - layer, 2 * (x ^ fx) + (y ^ fy)).wait_recv()

    out_shapes = [jax.ShapeDtypeStruct(r.shape, r.dtype) for r in rs]
    if small is not None:
        out_shapes.append(jax.ShapeDtypeStruct((4,) + small.shape, small.dtype))
    res = _comm_call(f"exchange_final{layer}", body, ins_all, out_shapes, n + 8, 1)
    return (res[:n], res[n]) if small is not None else (res[:n], None)


def _pad_in_cols(w):
    z = lambda n: jnp.zeros(w.shape[:-1] + (n,), w.dtype)
    return jnp.concatenate([w[..., :KR_OFF], z(64), w[..., KR_OFF:KR_OFF + MLA_ROPE], z(32), w[..., KR_OFF + MLA_ROPE:]], axis=-1)


def _unpad_in_cols(g):
    return jnp.concatenate([g[..., :KR_OFF], g[..., KR_OFF + 64:KR_OFF + 96], g[..., KR_OFF + 128:]], axis=-1)


def _pad_heads(w, real):
    k = w.shape[0]
    return jnp.pad(w.reshape(k, MLA_HEADS, real), ((0, 0), (0, 0), (0, LANES - real))).reshape(k, MLA_HEADS * LANES)


def _pad_gain(g, real):
    return jnp.pad(g.reshape(1, real), ((0, 0), (0, LANES - real)))


def layer_weights(full):
    w_ukv = full["w_ukv"].reshape(128, MLA_HEADS, 2, 64)
    two = lambda g: jnp.concatenate([g, g]).reshape(1, LANES)
    return dict(
        norm_g=full["norm_g"].reshape(1, -1), w_in=full["w_in"], conv_w=full["conv_w"], conv_b=full["conv_b"].reshape(1, -1),
        wgx=full["w_gate_x"], bgx=full["b_gate_x"].reshape(LRU_BLOCKS, 1, LANES),
        wga=full["w_gate_a"], bga=full["b_gate_a"].reshape(LRU_BLOCKS, 1, LANES),
        lam=full["lru_lambda"].reshape(1, -1), w_lru_o=full["w_lru_o"],
        cq_norm_g=full["cq_norm_g"].reshape(1, -1), ckv_norm_g=full["ckv_norm_g"].reshape(1, -1),
        wq=_pad_heads(full["w_uq"], MLA_QK), wk=_pad_heads(w_ukv[:, :, 0].reshape(128, 512), 64),
        wv=w_ukv[:, :, 1].reshape(128, 512),
        gq=_pad_gain(full["mla_q_norm_g"], MLA_QK), gk=_pad_gain(full["mla_k_norm_g"], MLA_QK),
        w_mla_o=full["w_mla_o"], gq2=two(full["dil_q_norm_g"]), gk2=two(full["dil_k_norm_g"]),
        w_dil_o=full["w_dil_o"], b_merge=full["b_merge"].reshape(1, -1), w_out=full["w_out"],
    )


def layer_fwd(x, w, tabs, gather=None):
    h, ht = rmsnorm_fwd(x, w["norm_g"])
    z_lru = mm_nn("in_proj_lru", h, w["w_in"], n=2048, n_off=G_LRU)
    z_mla = mm_nn("in_proj_mla", h, w["w_in"], n=1024, n_off=G_MLA)
    z_dil = mm_nn("in_proj_dil", h, w["w_in"], n=5120, n_off=G_DIL)
    z_mrg = mm_nn("in_proj_mrg", h, w["w_in"], n=3072, n_off=G_MRG)
    hs, y_lru, yt_lru = lru_fwd(z_lru, w["conv_w"], w["conv_b"], w["wgx"], w["bgx"], w["wga"], w["bga"], w["lam"])
    qm, km, vm, vtm = mla_prep_fwd(z_mla, tabs, w)
    o_mla, y_mla, yt_mla, lse_mla, *gathered = mla_attn_fwd(qm, km, vtm, z_mla, gather=gather)
    os_, lses, dil_rm = [], [], []
    for gi, (_, dil) in enumerate(DIL_GROUPS):
        qkv = dil_prep_fwd(z_dil, tabs, w["gq2"], w["gk2"], gi)
        o, lse = dil_attn_fwd(f"dil_attn_fwd_{dil}", *qkv, dil)
        os_.append(o)
        lses.append(lse)
        dil_rm.append(qkv)
    y_dil, o_dil, lse_dil, yt_dil = dil_combine(os_, lses, z_dil)
    ps = [mm_nn("proj_lru", y_lru, w["w_lru_o"]), mm_nn("proj_mla", y_mla, w["w_mla_o"]), mm_nn("proj_dil", y_dil, w["w_dil_o"])]
    merged, merged_t = merge_fwd(ps, z_mrg, w["b_merge"])
    out = mm_nn("out_proj", merged, w["w_out"], add=x)
    res = dict(x=x, ht=ht, z_lru=z_lru, z_mla=z_mla, z_dil=z_dil, z_mrg=z_mrg, hs=hs, yt_lru=yt_lru, qm=qm, km=km, vm=vm, o_mla=o_mla,
               yt_mla=yt_mla, lse_mla=lse_mla, dil_rm=dil_rm, yt_dil=yt_dil, o_dil=o_dil, lse_dil=lse_dil, ps=ps, merged_t=merged_t)
    return out, res, gathered


def layer_bwd(dout, r, w, tabs, chips=None):
    g = {}
    dmerged = mm_nt("out_proj_dx", dout, w["w_out"])
    g["w_out"] = mm_nn("out_proj_dw", r["merged_t"], dout, tm=1024, tn=512, tk=2048)
    dp0, dp1, dp2, dzm0, dzm1, dzm2, db0, db1, db2 = merge_bwd(dmerged, r["ps"], r["z_mrg"], w["b_merge"])
    g["b_merge"] = jnp.concatenate([db0, db1, db2], axis=1).reshape(-1)
    dy_lru = mm_nt("proj_lru_dx", dp0, w["w_lru_o"])
    dy_mla = mm_nt("proj_mla_dx", dp1, w["w_mla_o"])
    dy_dil = mm_nt("proj_dil_dx", dp2, w["w_dil_o"])
    g["w_lru_o"] = mm_nn("proj_lru_dw", r["yt_lru"], dp0, **DW_TILES)
    g["w_mla_o"] = mm_nn("proj_mla_dw", r["yt_mla"], dp1, **DW_TILES)
    g["w_dil_o"] = mm_nn("proj_dil_dw", r["yt_dil"], dp2, **DW_TILES)
    dzx, dzg_lru, dwgx, dbgx, dwga, dbga, dlam, dcw, dcb = lru_bwd(r["z_lru"], r["hs"], dy_lru, w["conv_w"], w["conv_b"], w["wgx"],
                                                                    w["bgx"], w["wga"], w["bga"], w["lam"])
    g.update(w_gate_x=dwgx, b_gate_x=dbgx.reshape(LRU_BLOCKS, LANES), w_gate_a=dwga, b_gate_a=dbga.reshape(LRU_BLOCKS, LANES),
             lru_lambda=dlam.reshape(-1), conv_w=dcw, conv_b=dcb.reshape(-1))
    do_m, dzg_mla, dd_m = mla_gate_bwd(dy_mla, r["o_mla"], r["z_mla"])
    dq_m, dk_m, dv_m, *landed = mla_attn_bwd(r["qm"], r["km"], r["vm"], do_m, r["lse_mla"], dd_m, chips=chips)
    dz_mla3, dg_cq, dg_ckv, dwq, dwk, dwv, dgq, dgk = mla_prep_bwd(r["z_mla"], tabs, w, dq_m, dk_m, dv_m)
    g.update(cq_norm_g=dg_cq.reshape(-1), ckv_norm_g=dg_ckv.reshape(-1), mla_q_norm_g=dgq[0, :MLA_QK], mla_k_norm_g=dgk[0, :MLA_QK])
    g["w_uq"] = dwq.reshape(256, MLA_HEADS, LANES)[:, :, :MLA_QK].reshape(256, MLA_HEADS * MLA_QK)
    g["w_ukv"] = jnp.concatenate([dwk.reshape(128, MLA_HEADS, LANES)[:, :, :64], dwv.reshape(128, MLA_HEADS, 64)], axis=2).reshape(128, 1024)
    dzg_dil, do1, dd1, do2, dd2, l2, do3, dd3, l3 = dil_gate_bwd(dy_dil, r["o_dil"], r["z_dil"], r["lse_dil"])
    stats = [(do1, r["lse_dil"], dd1), (do2, l2, dd2), (do3, l3, dd3)]
    dzq, dzk, dzv, dgq2, dgk2 = [], [], [], [], []
    for gi, (_, dil) in enumerate(DIL_GROUPS):
        dq, dk, dv = dil_attn_bwd(f"dil_attn_bwd_{dil}", *r["dil_rm"][gi], *stats[gi], dil)
        parts = dil_prep_bwd(r["z_dil"], tabs, w["gq2"], w["gk2"], dq, dk, dv, gi)
        for acc, part in zip((dzq, dzk, dzv, dgq2, dgk2), parts):
            acc.append(part)
    g.update(dil_q_norm_g=sum(dgq2)[0, :DIL_HD], dil_k_norm_g=sum(dgk2)[0, :DIL_HD])
    dz = jnp.concatenate([dzx, dzg_lru, dz_mla3, dzg_mla] + dzq + dzk + dzv + [dzg_dil, dzm0, dzm1, dzm2], axis=1)
    dh = mm_nt("in_proj_dx", dz, w["w_in"], tm=1024, tn=1024, tk=IN_PAD // 4)
    g["w_in"] = _unpad_in_cols(mm_nn("in_proj_dw", r["ht"], dz, **DW_TILES))
    dx, dng = rmsnorm_bwd(r["x"], dh, dout, w["norm_g"])
    g["norm_g"] = dng.reshape(-1)
    return dx, g, landed


def local_step(x, positions, target, full0, full1=None, gather1=None, reduce1=None):
    tabs = rope_tables(positions.reshape(-1, 1))
    ws, ress = [], []
    for l in range(2):
        if l == 0:
            ws.append(layer_weights(full0))
            x, res, gathered = layer_fwd(x, ws[0], tabs, gather=None if gather1 is None else (gather1[0], 1))
        else:
            ws.append(layer_weights(full1 if gather1 is None else gather1[1](gathered)))
            x, res, _ = layer_fwd(x, ws[1], tabs)
        ress.append(res)
    dy, loss = loss_head(x, target)
    dy, grads1, _ = layer_bwd(dy, ress[1], ws[1], tabs)
    chips1 = None if reduce1 is None else (reduce1(grads1), 1)
    dy, grads0, landed1 = layer_bwd(dy, ress[0], ws[0], tabs, chips=chips1)
    return loss, dy, [grads0, grads1], landed1


WEIGHTS = ["norm_g", "w_in", "conv_w", "conv_b", "w_gate_x", "b_gate_x", "w_gate_a", "b_gate_a", "lru_lambda", "w_lru_o", "cq_norm_g",
           "ckv_norm_g", "w_uq", "w_ukv", "mla_q_norm_g", "mla_k_norm_g", "w_mla_o", "dil_q_norm_g", "dil_k_norm_g", "w_dil_o", "b_merge",
           "w_out"]
SHARDED = {"w_in": 2, "conv_w": 2, "w_lru_o": 1, "w_uq": 2, "w_ukv": 2, "w_mla_o": 2, "w_dil_o": 2, "w_out": 1}
REPLICATED = [n for n in WEIGHTS if n not in SHARDED]
SMALL_ROWS = 144


def kernel(x, positions, norm_g, w_in, conv_w, conv_b, w_gate_x, b_gate_x, w_gate_a, b_gate_a, lru_lambda, w_lru_o, cq_norm_g, ckv_norm_g, w_uq, w_ukv, mla_q_norm_g, mla_k_norm_g, w_mla_o, dil_q_norm_g, dil_k_norm_g, w_dil_o, b_merge, w_out, loss_target, m_norm_g, m_w_in, m_conv_w, m_conv_b, m_w_gate_x, m_b_gate_x, m_w_gate_a, m_b_gate_a, m_lru_lambda, m_w_lru_o, m_cq_norm_g, m_ckv_norm_g, m_w_uq, m_w_ukv, m_mla_q_norm_g, m_mla_k_norm_g, m_w_mla_o, m_dil_q_norm_g, m_dil_k_norm_g, m_w_dil_o, m_b_merge, m_w_out, v_norm_g, v_w_in, v_conv_w, v_conv_b, v_w_gate_x, v_b_gate_x, v_w_gate_a, v_b_gate_a, v_lru_lambda, v_w_lru_o, v_cq_norm_g, v_ckv_norm_g, v_w_uq, v_w_ukv, v_mla_q_norm_g, v_mla_k_norm_g, v_w_mla_o, v_dil_q_norm_g, v_dil_k_norm_g, v_w_dil_o, v_b_merge, v_w_out):
    args = locals()
    w = {n: args[n] for n in WEIGHTS}
    m = {n: args["m_" + n] for n in WEIGHTS}
    v = {n: args["v_" + n] for n in WEIGHTS}
    my_c = lax.axis_index("c").astype(jnp.int32)
    my_s = (2 * lax.axis_index("x") + lax.axis_index("y")).astype(jnp.int32)
    c_idx = my_c.reshape(1)
    s_idx = my_s.reshape(1)

    names = list(SHARDED)
    wire = [[w[n][l] if n == "conv_w" else w[n][l].astype(BF16) for n in names] for l in range(2)]

    def assemble(l, gathered):
        full = {n: w[n][l] for n in REPLICATED}
        for n, ga, own in zip(names, gathered, wire[l]):
            full[n] = jnp.concatenate([jnp.where(my_s == s, own, ga[s]) for s in range(4)], axis=SHARDED[n] - 1)
        full["w_in"] = _pad_in_cols(full["w_in"])
        return full

    full0 = assemble(0, allgather_layer(wire[0], 0))

    def shards_of(grads_l):
        gs = []
        for n in names:
            g_ = grads_l[n]
            if n == "w_in":
                gs.append(g_.reshape(D_MODEL, 4, IN_WIDTH // 4).transpose(1, 2, 0))
            else:
                parts = jnp.stack(jnp.split(g_, 4, axis=SHARDED[n] - 1))
                gs.append(parts.reshape(4, -1, parts.shape[-1]))
        return gs

    def to_partials(l, gs, tags):
        la = exchange_sibling(gs, l)
        dts = [BF16 if g_.size >= 2**19 and t != "small" else F32 for g_, t in zip(gs, tags)]
        return la, [sum_pair(f"sum_pair{l}_{t}", g_, l_, dt) for t, g_, l_, dt in zip(tags, gs, la, dts)]

    def finish(l, gs, la, lb, tags, with_small):
        rs = [sum_chips(f"sum_chips{l}_{t}", g_, l_, b_, s_idx) for t, g_, l_, b_ in zip(tags, gs, la, lb)]
        received, small_all = exchange_final(rs[:len(names)], l, small=rs[-1] if with_small else None)
        return rs[:len(names)], received, small_all

    stash = {}

    def reduce1(grads1):
        stash["gs"] = shards_of(grads1)
        stash["la"], pcs = to_partials(1, stash["gs"], names)
        return pcs

    loss, grad_x, grads, lb1 = local_step(x[0], positions[0], loss_target[0], full0,
                                          gather1=(wire[1], functools.partial(assemble, 1)), reduce1=reduce1)
    loss = lax.psum(loss, ("x", "y", "c"))
    reduced1, received1, _ = finish(1, stash["gs"], stash["la"], lb1, names, False)

    flat = jnp.concatenate([jnp.stack([grads[0][n], grads[1][n]]).reshape(-1) for n in REPLICATED])
    small = jnp.pad(flat, (0, 4 * SMALL_ROWS * 1024 - flat.size)).reshape(4, SMALL_ROWS, 1024)
    gs0, tags0 = shards_of(grads[0]) + [small], names + ["small"]
    la0, pcs0 = to_partials(0, gs0, tags0)
    reduced0, received0, small_all = finish(0, gs0, la0, exchange_chips(pcs0, 0), tags0, True)

    g_local, delta, new_m, new_v = {}, {}, {}, {}
    for i, n in enumerate(names):
        shp = w[n].shape
        if n == "w_in":
            as3 = lambda a: a.transpose(0, 2, 1)
            back = lambda o: o.transpose(0, 2, 1)
        else:
            as3 = lambda a, rc=reduced0[i].shape: a.reshape((2,) + rc)
            back = lambda o, shp=shp: o.reshape(shp)
        outs = adamw_pair(f"adamw_{n}", as3(w[n]), (reduced0[i], reduced1[i]), (received0[i], received1[i]), as3(m[n]), as3(v[n]),
                          c_idx)
        g_local[n], delta[n], new_m[n], new_v[n] = [back(o) for o in outs]
    flat = small_all.reshape(-1)
    off = 0
    for n in REPLICATED:
        g_local[n] = flat[off:off + w[n].size].reshape(w[n].shape)
        off += w[n].size
        delta[n], new_m[n], new_v[n] = adamw(f"adamw_{n}", w[n], g_local[n], m[n], v[n])
    return (loss, grad_x[None], *[g_local[n] for n in WEIGHTS], *[delta[n] for n in WEIGHTS], *[new_m[n] for n in WEIGHTS],
            *[new_v[n] for n in WEIGHTS])
```

```python
import functools

import jax
import jax.numpy as jnp
from jax import lax
from jax.experimental import pallas as pl
from jax.experimental.pallas import tpu as pltpu

F32 = jnp.float32
BF16 = jnp.bfloat16
MXU_DTYPE = jnp.bfloat16

D_MODEL = 1024
EPS = 1e-6
ROPE_THETA = 10000.0
LRU_BLOCKS = 8
LRU_C = 8.0
MLA_HEADS = 8
MLA_NOPE = 64
MLA_ROPE = 32
MLA_QK = 96
DIL_GROUPS = ((128, 1), (512, 4), (2048, 16))
DIL_HD = 64
DIL_NK = 128
MLA_HPS = 8
IN_WIDTH = 11168
ADAM_LR, ADAM_B1, ADAM_B2, ADAM_EPS, ADAM_WD, ADAM_STEP = 0.001, 0.9, 0.999, 1e-08, 0.01, 10

LANES = 128
G_LRU, G_MLA, G_DIL, G_MRG = 0, 2048, 3072, 8192
IN_PAD = 11264
KR_OFF = 2432

NN = (((1,), (0,)), ((), ()))
NT = (((1,), (1,)), ((), ()))
TN = (((0,), (0,)), ((), ()))
NEG = -1e30
MESH = pl.DeviceIdType.MESH
VMEM_LIMIT = 48 * 2**20


def _cparams(sem):
    return pltpu.CompilerParams(dimension_semantics=sem, vmem_limit_bytes=VMEM_LIMIT)


def _dot(a, b, dims):
    return lax.dot_general(a.astype(MXU_DTYPE), b.astype(MXU_DTYPE), dims, preferred_element_type=F32)


@jax.custom_vjp
def mm(a, w):
    return _dot(a, w, NN)


def _mm_fwd(a, w):
    return _dot(a, w, NN), (a, w)


def _mm_bwd(res, g):
    a, w = res
    return _dot(g, w, NT), _dot(a, g, TN)


mm.defvjp(_mm_fwd, _mm_bwd)


def _seg64_matrix():
    r = lax.broadcasted_iota(jnp.int32, (LANES, LANES), 0) < DIL_HD
    c = lax.broadcasted_iota(jnp.int32, (LANES, LANES), 1) < DIL_HD
    return (r == c).astype(BF16)


def _seg_sum_impl(x):
    b = _seg64_matrix()
    hi = x.astype(BF16)
    r1 = x - hi.astype(F32)
    mid = r1.astype(BF16)
    lo = (r1 - mid.astype(F32)).astype(BF16)
    dot = lambda u: lax.dot_general(u, b, NN, preferred_element_type=F32)
    return dot(hi) + dot(mid) + dot(lo)


@jax.custom_vjp
def seg_sum(x):
    return _seg_sum_impl(x)


seg_sum.defvjp(lambda x: (_seg_sum_impl(x), None), lambda _, g: (_seg_sum_impl(g),))


def _lroll_impl(x, s):
    return pltpu.roll(x, s % LANES, 1)


@functools.partial(jax.custom_vjp, nondiff_argnums=(1,))
def lroll(x, s):
    return _lroll_impl(x, s)


lroll.defvjp(lambda x, s: (_lroll_impl(x, s), None), lambda s, _, g: (_lroll_impl(g, -s),))


class _Ops:
    def __init__(self, diff):
        self.mm = mm if diff else (lambda a, w: _dot(a, w, NN))
        self.seg_sum = seg_sum if diff else _seg_sum_impl
        self.lroll = lroll if diff else _lroll_impl


PLAIN, DIFF = _Ops(False), _Ops(True)


def rms(x, g, n):
    ms = jnp.sum(x * x, axis=-1, keepdims=True) * (1.0 / n)
    return x * lax.rsqrt(ms + EPS) * g


def rope(ops, x, c, s1, s2, half):
    return x * c + ops.lroll(x, -half) * s1 + ops.lroll(x, half) * s2


def _sigmoid(x):
    return 1.0 / (1.0 + jnp.exp(-x))


def _silu_and_grad(g):
    sg = _sigmoid(g)
    return g * sg, sg * (1.0 + g * (1.0 - sg))


def _softplus(x):
    return jnp.maximum(x, 0.0) + jnp.log(1.0 + jnp.exp(-jnp.abs(x)))


def _expm1(y):
    series = y * (1.0 + y * (0.5 + y * (1.0 / 6.0 + y * (1.0 / 24.0 + y * (1.0 / 120.0)))))
    return jnp.where(y > -0.05, series, jnp.exp(jnp.minimum(y, -0.05)) - 1.0)


def _host(body, n_in, n_out, n_scr, hook, grid):
    if hook is None:
        return body, [], [], [], [], []
    ins, out_shapes, n_sem, make = hook
    ni, no = len(ins), len(out_shapes)
    hbm = pl.BlockSpec(memory_space=pltpu.HBM)

    def at(corner):
        cond = None
        for axis, size in enumerate(grid):
            here = pl.program_id(axis) == (size - 1 if corner else 0)
            cond = here if cond is None else cond & here
        return cond

    def hosted(*refs):
        a, refs = refs[:n_in], refs[n_in:]
        xi, refs = refs[:ni], refs[ni:]
        o, refs = refs[:n_out], refs[n_out:]
        xo, refs = refs[:no], refs[no:]
        scr, (send, recv) = refs[:n_scr], refs[n_scr:]
        start, finish = make(xi, xo, send, recv)
        pl.when(at(False))(start)
        body(*a, *o, *scr)
        pl.when(at(True))(finish)

    sems = [pltpu.SemaphoreType.DMA((n_sem,)), pltpu.SemaphoreType.DMA((n_sem,))]
    return hosted, list(ins), [hbm] * ni, [hbm] * no, list(out_shapes), sems


def _mm_call(name, a, b, *, mode, m, n, k, a_blk, b_blk, out_dtype, add, tm, tn, tk, comm=None):
    nk = k // tk
    assert m % tm == 0 and n % tn == 0 and k % tk == 0, (name, m, n, k, tm, tn, tk)
    dims = {"nn": NN, "nt": NT, "tn": TN}[mode]

    def body(*refs):
        if add is None:
            a_ref, b_ref, o_ref, *scr = refs
            add_ref = None
        else:
            a_ref, b_ref, add_ref, o_ref, *scr = refs
        part = _dot(a_ref[...], b_ref[...], dims)

        def finish(acc):
            if add_ref is not None:
                acc = acc + add_ref[...]
            o_ref[...] = acc.astype(o_ref.dtype)

        if nk == 1:
            finish(part)
        else:
            (acc_ref,) = scr
            kk = pl.program_id(2)

            @pl.when(kk == 0)
            def _():
                acc_ref[...] = part

            @pl.when(kk > 0)
            def _():
                acc_ref[...] += part

            @pl.when(kk == nk - 1)
            def _():
                finish(acc_ref[...])

    in_specs = [a_blk, b_blk]
    args = [a, b]
    if add is not None:
        in_specs.append(pl.BlockSpec((tm, tn), lambda j, i, kk: (i, j)))
        args.append(add)
    grid = (n // tn, m // tm, nk)
    scratch = [] if nk == 1 else [pltpu.VMEM((tm, tn), F32)]
    body, x_in, x_in_specs, x_out_specs, x_out_shapes, sems = _host(body, len(args), 1, len(scratch), comm, grid)
    res = pl.pallas_call(
        body,
        name=name,
        grid=grid,
        in_specs=in_specs + x_in_specs,
        out_specs=[pl.BlockSpec((tm, tn), lambda j, i, kk: (i, j))] + x_out_specs,
        out_shape=[jax.ShapeDtypeStruct((m, n), out_dtype)] + x_out_shapes,
        scratch_shapes=scratch + sems,
        compiler_params=_cparams(("arbitrary",) * 3 if comm is not None else ("parallel", "parallel", "arbitrary")),
    )(*args, *x_in)
    return res[0] if comm is None else res


def mm_nn(name, a, b, *, n=None, n_off=0, out_dtype=F32, add=None, tm=512, tn=1024, tk=1024):
    m, k = a.shape
    n = b.shape[1] if n is None else n
    tm, tn, tk = min(tm, m), min(tn, n), min(k, tk)
    ob = n_off // tn
    assert n_off % tn == 0
    return _mm_call(name, a, b, mode="nn", m=m, n=n, k=k, out_dtype=out_dtype, add=add, tm=tm, tn=tn, tk=tk,
                    a_blk=pl.BlockSpec((tm, tk), lambda j, i, kk: (i, kk)),
                    b_blk=pl.BlockSpec((tk, tn), lambda j, i, kk: (kk, j + ob)))


def mm_nt(name, a, b, *, out_dtype=F32, tm=512, tn=1024, tk=1024, comm=None):
    m, k = a.shape
    n = b.shape[0]
    tn, tk = min(tn, n), min(k, tk)
    return _mm_call(name, a, b, mode="nt", m=m, n=n, k=k, out_dtype=out_dtype, add=None, tm=tm, tn=tn, tk=tk, comm=comm,
                    a_blk=pl.BlockSpec((tm, tk), lambda j, i, kk: (i, kk)),
                    b_blk=pl.BlockSpec((tn, tk), lambda j, i, kk: (j, kk)))


DW_TILES = dict(tm=1024, tn=512, tk=4096)


def rowwise(name, fn, t, *, tm=256, ncol=1, ins=(), outs=(), touts=(), accs=()):
    n_in, n_out, n_acc = len(ins), len(outs) + len(touts), len(accs)

    def zero_when(ref, cond):
        @pl.when(cond)
        def _():
            ref[...] = jnp.zeros(ref.shape, ref.dtype)

    def body(*refs):
        in_refs, out_refs, acc_refs = refs[:n_in], refs[n_in:n_in + n_out], refs[n_in + n_out:]
        j, i = pl.program_id(0), pl.program_id(1)
        for ref, (_, _, _, cd) in zip(acc_refs, accs):
            zero_when(ref, (i == 0) if cd else ((i == 0) & (j == 0)))
        fn(in_refs, out_refs, acc_refs)

    def in_spec(arr, w, base, cd, rd):
        rows = tm if rd else arr.shape[0]
        return pl.BlockSpec((rows, w), lambda j, i: (i if rd else 0, base + (j if cd else 0)))

    in_specs = [in_spec(*e) for e in ins]
    out_specs = [pl.BlockSpec((tm, w), (lambda j, i, cd=cd: (i, j if cd else 0))) for (_, w, _, cd) in outs]
    out_specs += [pl.BlockSpec((w, tm), (lambda j, i, cd=cd: (j if cd else 0, i))) for (_, w, _, cd) in touts]
    out_specs += [pl.BlockSpec((r, w), (lambda j, i, cd=cd: (0, j if cd else 0))) for (r, _, w, cd) in accs]
    out_shape = [jax.ShapeDtypeStruct((t, c), dt) for (c, _, dt, _) in outs]
    out_shape += [jax.ShapeDtypeStruct((r, t), dt) for (r, _, dt, _) in touts]
    out_shape += [jax.ShapeDtypeStruct((r, c), F32) for (r, c, _, _) in accs]
    res = pl.pallas_call(
        body,
        name=name,
        grid=(ncol, t // tm),
        in_specs=in_specs,
        out_specs=out_specs,
        out_shape=out_shape,
        compiler_params=_cparams(("arbitrary", "arbitrary") if accs else ("parallel", "parallel")),
    )(*[e[0] for e in ins])
    return res


def _row(arr, w=None, base=0, cd=False):
    return (arr, arr.shape[1] if w is None else w, base, cd, True)


def _const(arr, w=None, base=0, cd=False):
    return (arr, arr.shape[1] if w is None else w, base, cd, False)


def rope_tables(positions):
    t = positions.shape[0]

    def fn(ins, outs, _):
        pos = ins[0][...].astype(F32)
        lane = lax.broadcasted_iota(jnp.int32, (1, LANES), 1)
        log_theta = jnp.log(jnp.float32(ROPE_THETA))
        jm = lane - MLA_NOPE
        idx = jnp.clip(jnp.where(jm < 16, jm, jm - 16), 0, 15).astype(F32)
        ang = pos * jnp.exp(-(idx * (2.0 / MLA_ROPE)) * log_theta)
        cos, sin = jnp.cos(ang), jnp.sin(ang)
        in_rope = (lane >= MLA_NOPE) & (lane < MLA_QK)
        outs[0][:, 0:128] = jnp.where(lane < MLA_NOPE, 1.0, jnp.where(in_rope, cos, 0.0))
        outs[0][:, 128:256] = jnp.where(in_rope & (jm < 16), -sin, 0.0)
        outs[0][:, 256:384] = jnp.where(in_rope & (jm >= 16), sin, 0.0)
        jd = lane & (DIL_HD - 1)
        idx = (jd & 31).astype(F32)
        ang = pos * jnp.exp(-(idx * (2.0 / DIL_HD)) * log_theta)
        cos, sin = jnp.cos(ang), jnp.sin(ang)
        outs[0][:, 384:512] = cos
        outs[0][:, 512:640] = jnp.where(jd < 32, -sin, 0.0)
        outs[0][:, 640:768] = jnp.where(jd >= 32, sin, 0.0)

    return rowwise("rope_tables", fn, t, ins=[_row(positions)], outs=[(768, 768, F32, False)])[0]


def rmsnorm_fwd(x, g):
    def fn(ins, outs, _):
        h = rms(ins[0][...], ins[1][...], D_MODEL)
        outs[0][...] = h.astype(MXU_DTYPE)
        outs[1][...] = h.T.astype(MXU_DTYPE)

    return rowwise("rmsnorm_fwd", fn, x.shape[0], ins=[_row(x), _const(g)], outs=[(D_MODEL, D_MODEL, MXU_DTYPE, False)],
                   touts=[(D_MODEL, D_MODEL, MXU_DTYPE, False)])


def rmsnorm_bwd(x, dh, dres, g):
    def fn(ins, outs, accs):
        _, vjp = jax.vjp(lambda xv, gv: rms(xv, gv, D_MODEL), ins[0][...], ins[3][...])
        dx, dg = vjp(ins[1][...])
        outs[0][...] = ins[2][...] + dx
        accs[0][...] += dg

    return rowwise("rmsnorm_bwd", fn, x.shape[0], ins=[_row(x), _row(dh), _row(dres), _const(g)],
                   outs=[(D_MODEL, D_MODEL, F32, False)], accs=[(1, D_MODEL, D_MODEL, False)])


def loss_head(y, target):
    def fn(ins, outs, accs):
        err = ins[0][...] - ins[1][...]
        outs[0][...] = err * (1.0 / D_MODEL)
        accs[0][...] += jnp.sum(err * err, axis=0, keepdims=True)
        accs[1][...] = jnp.broadcast_to(jnp.sum(accs[0][...], keepdims=True), (1, LANES))

    dy, _, tot = rowwise("loss_head", fn, y.shape[0], ins=[_row(y), _row(target)], outs=[(D_MODEL, D_MODEL, F32, False)],
                         accs=[(1, D_MODEL, D_MODEL, False), (1, LANES, LANES, False)])
    return dy, tot[0, 0] * (0.5 / D_MODEL)


def _shift_rows(v, d, fill, reverse):
    tb = v.shape[0]
    if d % 8 == 0:
        pad = jnp.full((d, v.shape[1]), fill, v.dtype)
        return jnp.concatenate([v[d:], pad] if reverse else [pad, v[:tb - d]], axis=0)
    rows = lax.broadcasted_iota(jnp.int32, v.shape, 0)
    if not reverse:
        return jnp.where(rows >= d, pltpu.roll(v, d, 0), fill)
    return jnp.where(rows < tb - d, pltpu.roll(v, tb - d, 0), fill)


def _scan_tile(a, b, reverse):
    d = 1
    while d < a.shape[0]:
        b = b + a * _shift_rows(b, d, 0.0, reverse)
        a = a * _shift_rows(a, d, 1.0, reverse)
        d *= 2
    return a, b


def _lru_gates(ops, xc, wgx, bgx, wga, bga, lam):
    gx = _sigmoid(ops.mm(xc, wgx) + bgx)
    ga = _sigmoid(ops.mm(xc, wga) + bga)
    log_a = -LRU_C * ga * _softplus(-lam)
    a = jnp.exp(log_a)
    mult = jnp.sqrt(-_expm1(2.0 * log_a))
    return a, mult * (gx * xc)


def _shifted_inputs(x, halo, tb):
    rows = lax.broadcasted_iota(jnp.int32, x.shape, 0)
    pad = jnp.zeros((tb - 8, LANES), F32)
    out = []
    for d in (3, 2, 1):
        head = jnp.concatenate([pltpu.roll(halo, d, 0), pad], axis=0)
        out.append(jnp.where(rows >= d, pltpu.roll(x, d, 0), head))
    return out + [x]


def _lru_specs(nt, tb, reverse):
    hb = tb // 8
    tt = (lambda t: nt - 1 - t) if reverse else (lambda t: t)
    blk = lambda off: pl.BlockSpec((tb, LANES), lambda n, t: (tt(t), n + off))
    halo = lambda off: pl.BlockSpec((8, LANES), lambda n, t: (jnp.maximum(tt(t) * hb - 1, 0), n + off))
    chan = lambda r: pl.BlockSpec((r, LANES), lambda n, t: (0, n))
    wblk = pl.BlockSpec((None, LANES, LANES), lambda n, t: (n, 0, 0))
    bblk = pl.BlockSpec((None, 1, LANES), lambda n, t: (n, 0, 0))
    return blk, halo, chan, wblk, bblk


def lru_fwd(z, conv_w, conv_b, wgx, bgx, wga, bga, lam, *, tb=256):
    t = z.shape[0]
    nt = t // tb
    blk, halo, chan, wblk, bblk = _lru_specs(nt, tb, False)

    def body(x_ref, xh_ref, g_ref, cw_ref, cb_ref, wgx_ref, bgx_ref, wga_ref, bga_ref, lam_ref, h_ref, y_ref, yt_ref, carry_ref):
        ti = pl.program_id(1)

        @pl.when(ti == 0)
        def _():
            carry_ref[...] = jnp.zeros((8, LANES), F32)

        x = x_ref[...]
        hal = jnp.where(ti > 0, xh_ref[...], 0.0)
        xs = _shifted_inputs(x, hal, tb)
        xc = cb_ref[...] + sum(xs[k] * cw_ref[k:k + 1, :] for k in range(4))
        a, b = _lru_gates(PLAIN, xc, wgx_ref[...], bgx_ref[...], wga_ref[...], bga_ref[...], lam_ref[...])
        acum, h0 = _scan_tile(a, b, False)
        h = h0 + acum * carry_ref[7:8, :]
        carry_ref[...] = h[tb - 8:tb, :]
        h_ref[...] = h
        y = h * _silu_and_grad(g_ref[...])[0]
        y_ref[...] = y.astype(MXU_DTYPE)
        yt_ref[...] = y.T.astype(MXU_DTYPE)

    return pl.pallas_call(
        body,
        name="lru_fwd",
        grid=(LRU_BLOCKS, nt),
        in_specs=[blk(0), halo(0), blk(LRU_BLOCKS), chan(4), chan(1), wblk, bblk, wblk, bblk, chan(1)],
        out_specs=[blk(0), blk(0), pl.BlockSpec((LANES, tb), lambda n, t_: (n, t_))],
        out_shape=[jax.ShapeDtypeStruct((t, 1024), F32), jax.ShapeDtypeStruct((t, 1024), MXU_DTYPE),
                   jax.ShapeDtypeStruct((1024, t), MXU_DTYPE)],
        scratch_shapes=[pltpu.VMEM((8, LANES), F32)],
        compiler_params=_cparams(("parallel", "arbitrary")),
    )(z, z, z, conv_w, conv_b, wgx, bgx, wga, bga, lam)


def lru_bwd(z, h, dy, conv_w, conv_b, wgx, bgx, wga, bga, lam, *, tb=256, comm=None):
    t = z.shape[0]
    nt = t // tb
    blk, halo, chan, wblk, bblk = _lru_specs(nt, tb, True)

    def body(x_ref, xh_ref, g_ref, h_ref, hh_ref, dy_ref, cw_ref, cb_ref, wgx_ref, bgx_ref, wga_ref, bga_ref, lam_ref,
             dzx_ref, dzg_ref, dwgx_ref, dbgx_ref, dwga_ref, dbga_ref, dlam_ref, dcw_ref, dcb_ref,
             gcar_ref, acar_ref, xcar_ref):
        ti = pl.program_id(1)
        has_earlier = ti < nt - 1

        @pl.when(ti == 0)
        def _():
            for ref in (dwgx_ref, dbgx_ref, dwga_ref, dbga_ref, dlam_ref, dcw_ref, dcb_ref, gcar_ref, acar_ref, xcar_ref):
                ref[...] = jnp.zeros(ref.shape, F32)

        rows = lax.broadcasted_iota(jnp.int32, (tb, LANES), 0)
        x = x_ref[...]
        hal = jnp.where(has_earlier, xh_ref[...], 0.0)
        xs = _shifted_inputs(x, hal, tb)
        xc = cb_ref[...] + sum(xs[k] * cw_ref[k:k + 1, :] for k in range(4))
        (a, _), vjp = jax.vjp(functools.partial(_lru_gates, DIFF), xc, wgx_ref[...], bgx_ref[...], wga_ref[...],
                              bga_ref[...], lam_ref[...])
        g, h, dyv = g_ref[...], h_ref[...], dy_ref[...]
        silu, dsilu = _silu_and_grad(g)
        dzg_ref[...] = (dyv * h * dsilu).astype(MXU_DTYPE)
        a_next = jnp.where(rows < tb - 1, pltpu.roll(a, tb - 1, 0), acar_ref[0:1, :])
        acum, g0 = _scan_tile(a_next, dyv * silu, True)
        gt = g0 + acum * gcar_ref[0:1, :]
        h_prev = jnp.where(rows >= 1, pltpu.roll(h, 1, 0), jnp.where(has_earlier, hh_ref[7:8, :], 0.0))
        dxc, dwgx, dbgx, dwga, dbga, dlam = vjp((gt * h_prev, gt))
        later = xcar_ref[...]
        gcar_ref[...] = gt[0:8, :]
        acar_ref[...] = a[0:8, :]
        xcar_ref[...] = dxc[0:8, :]
        dx = dxc * cw_ref[3:4, :]
        pad = jnp.zeros((tb - 8, LANES), F32)
        for d in (1, 2, 3):
            tail = jnp.concatenate([pad, pltpu.roll(later, 8 - d, 0)], axis=0)
            up = jnp.where(rows < tb - d, pltpu.roll(dxc, tb - d, 0), tail)
            dx = dx + up * cw_ref[3 - d:4 - d, :]
        dzx_ref[...] = dx.astype(MXU_DTYPE)
        for k in range(4):
            dcw_ref[k:k + 1, :] += jnp.sum(dxc * xs[k], axis=0, keepdims=True)
        dcb_ref[...] += jnp.sum(dxc, axis=0, keepdims=True)
        dwgx_ref[...] += dwgx
        dbgx_ref[...] += dbgx
        dwga_ref[...] += dwga
        dbga_ref[...] += dbga
        dlam_ref[...] += dlam

    sds = jax.ShapeDtypeStruct
    grid = (LRU_BLOCKS, nt)
    body, x_in, x_in_specs, x_out_specs, x_out_shapes, sems = _host(body, 13, 9, 3, comm, grid)
    return pl.pallas_call(
        body,
        name="lru_bwd" if comm is None else "lru_bwd_comm",
        grid=grid,
        in_specs=[blk(0), halo(0), blk(LRU_BLOCKS), blk(0), halo(0), blk(0), chan(4), chan(1), wblk, bblk, wblk, bblk, chan(1)]
        + x_in_specs,
        out_specs=[blk(0), blk(0), wblk, bblk, wblk, bblk, chan(1), chan(4), chan(1)] + x_out_specs,
        out_shape=[sds((t, 1024), MXU_DTYPE), sds((t, 1024), MXU_DTYPE), sds(wgx.shape, F32), sds(bgx.shape, F32), sds(wga.shape, F32),
                   sds(bga.shape, F32), sds((1, 1024), F32), sds((4, 1024), F32), sds((1, 1024), F32)] + x_out_shapes,
        scratch_shapes=[pltpu.VMEM((8, LANES), F32)] * 3 + sems,
        compiler_params=_cparams(("arbitrary", "arbitrary") if comm is not None else ("parallel", "arbitrary")),
    )(z, z, z, h, h, dy, conv_w, conv_b, wgx, bgx, wga, bga, lam, *x_in)


def _mla_prep_tile(ops, cq, ckv, krp, c, s1, s2, g_cq, g_ckv, wq, wk, wv, gq, gk):
    cqn = rms(cq, g_cq, 256)
    ckvn = rms(ckv, g_ckv, 128)
    v = ops.mm(ckvn, wv)
    qs, ks = [], []
    for hd in range(MLA_HEADS):
        q = rms(ops.mm(cqn, wq[hd]), gq, MLA_QK)
        k = rms(ops.mm(ckvn, wk[hd]) + krp, gk, MLA_QK)
        qs.append(rope(ops, q, c, s1, s2, 16) * (MLA_QK ** -0.5))
        ks.append(rope(ops, k, c, s1, s2, 16))
    return tuple(qs), tuple(ks), v


def _mla_prep_args(ins):
    z_cq, z_ckv, z_kr, tc, ts1, ts2, g_cq, g_ckv, wq, wk, wv, gq, gk = ins[:13]
    heads = lambda w: tuple(w[:, LANES * hd:LANES * (hd + 1)] for hd in range(MLA_HEADS))
    return (z_cq[...], z_ckv[...], z_kr[...], tc[...], ts1[...], ts2[...], g_cq[...], g_ckv[...], heads(wq), heads(wk),
            wv[...], gq[...], gk[...])


def _mla_prep_ins(z, tabs, w):
    return [_row(z, 256, 0), _row(z, 128, 2), _row(z, 128, 3), _row(tabs, 128, 0), _row(tabs, 128, 1), _row(tabs, 128, 2),
            _const(w["cq_norm_g"]), _const(w["ckv_norm_g"]), _const(w["wq"]), _const(w["wk"]), _const(w["wv"]),
            _const(w["gq"]), _const(w["gk"])]


def mla_prep_fwd(z, tabs, w):
    def fn(ins, outs, _):
        qs, ks, v = _mla_prep_tile(PLAIN, *_mla_prep_args(ins))
        for hd in range(MLA_HEADS):
            outs[0][:, LANES * hd:LANES * (hd + 1)] = qs[hd].astype(MXU_DTYPE)
            outs[1][:, LANES * hd:LANES * (hd + 1)] = ks[hd].astype(MXU_DTYPE)
        outs[2][...] = v.astype(MXU_DTYPE)
        outs[3][...] = v.T.astype(MXU_DTYPE)

    return rowwise("mla_prep_fwd", fn, z.shape[0], ins=_mla_prep_ins(z, tabs, w),
                   outs=[(1024, 1024, MXU_DTYPE, False), (1024, 1024, MXU_DTYPE, False), (512, 512, MXU_DTYPE, False)],
                   touts=[(512, 512, MXU_DTYPE, False)])


def mla_prep_bwd(z, tabs, w, dq, dk, dv):
    def fn(ins, outs, accs):
        args = _mla_prep_args(ins)
        _, vjp = jax.vjp(functools.partial(_mla_prep_tile, DIFF), *args)
        heads = lambda ref: tuple(ref[:, LANES * hd:LANES * (hd + 1)] for hd in range(MLA_HEADS))
        dcq, dckv, dkr, _, _, _, dg_cq, dg_ckv, dwq, dwk, dwv, dgq, dgk = vjp((heads(ins[13]), heads(ins[14]), ins[15][...]))
        lane = lax.broadcasted_iota(jnp.int32, (1, LANES), 1)
        outs[0][:, 0:256] = dcq.astype(MXU_DTYPE)
        outs[0][:, 256:384] = dckv.astype(MXU_DTYPE)
        outs[0][:, 384:512] = jnp.where((lane >= MLA_NOPE) & (lane < MLA_QK), dkr, 0.0).astype(MXU_DTYPE)
        accs[0][...] += dg_cq
        accs[1][...] += dg_ckv
        for hd in range(MLA_HEADS):
            accs[2][:, LANES * hd:LANES * (hd + 1)] += dwq[hd]
            accs[3][:, LANES * hd:LANES * (hd + 1)] += dwk[hd]
        accs[4][...] += dwv
        accs[5][...] += dgq
        accs[6][...] += dgk

    return rowwise("mla_prep_bwd", fn, z.shape[0], ins=_mla_prep_ins(z, tabs, w) + [_row(dq), _row(dk), _row(dv)],
                   outs=[(512, 512, MXU_DTYPE, False)],
                   accs=[(1, 256, 256, False), (1, 128, 128, False), (256, 1024, 1024, False), (128, 1024, 1024, False),
                         (128, 512, 512, False), (1, 128, 128, False), (1, 128, 128, False)])


def _head_masks():
    lane = lax.broadcasted_iota(jnp.int32, (1, LANES), 1)
    return (lane < DIL_HD, lane >= DIL_HD)


def _row_scalar(tile, mask):
    return jnp.max(jnp.where(mask, tile, -jnp.inf), axis=-1, keepdims=True)


def _causal_tiles(nq, by_key):
    pairs = [(i, j) for i in range(nq) for j in range(i + 1)]
    if by_key:
        pairs.sort(key=lambda ij: (ij[1], ij[0]))
    return (jnp.asarray([ij[0] for ij in pairs], jnp.int32), jnp.asarray([ij[1] for ij in pairs], jnp.int32))


def mla_attn_fwd(q, k, vt, z, *, tq=256, gather=None):
    t = q.shape[0]
    nq = t // tq
    it, jt = _causal_tiles(nq, False)
    hps, wq, wv = MLA_HPS, LANES * MLA_HPS, 64 * MLA_HPS
    ws, layer = gather if gather is not None else ([], 0)
    ng = len(ws)
    ngrp, nsteps = MLA_HEADS // hps, int(it.shape[0])

    def body(it_ref, jt_ref, q_ref, k_ref, vt_ref, g_ref, *rest):
        w_refs, rest = rest[:ng], rest[ng:]
        o_ref, y_ref, yt_ref, lse_ref = rest[:4]
        ga_refs, rest = rest[4:4 + ng], rest[4 + ng:]
        m_scr, l_scr, acc_scr = rest[:3]
        step = pl.program_id(1)
        i, j = it_ref[step], jt_ref[step]
        if ng:
            gather_start, gather_finish = _layer_gather(w_refs, ga_refs, rest[3], rest[4], layer)

            @pl.when((pl.program_id(0) == 0) & (step == 0))
            def _():
                gather_start()

        @pl.when(j == 0)
        def _():
            m_scr[...] = jnp.full(m_scr.shape, NEG, F32)
            l_scr[...] = jnp.zeros(l_scr.shape, F32)
            acc_scr[...] = jnp.zeros(acc_scr.shape, F32)

        def update(diagonal):
            heads = range(hps)
            lanes = [slice(LANES * hh, LANES * (hh + 1)) for hh in heads]
            rows = [slice(64 * hh, 64 * (hh + 1)) for hh in heads]
            sts = [_dot(k_ref[:, lanes[hh]], q_ref[:, lanes[hh]], NT) for hh in heads]
            m_prev = [m_scr[hh:hh + 1, :] for hh in heads]
            l_prev = [l_scr[hh:hh + 1, :] for hh in heads]
            acc_prev = [acc_scr[rows[hh], :] for hh in heads]
            m_new, l_new, acc_new = [], [], []
            for hh in heads:
                st = sts[hh]
                if diagonal:
                    key = lax.broadcasted_iota(jnp.int32, (tq, tq), 0)
                    qry = lax.broadcasted_iota(jnp.int32, (tq, tq), 1)
                    st = jnp.where(key <= qry, st, NEG)
                m_new.append(jnp.maximum(m_prev[hh], jnp.max(st, axis=0, keepdims=True)))
                alpha = jnp.exp(m_prev[hh] - m_new[hh])
                pt = jnp.exp(st - m_new[hh])
                l_new.append(alpha * l_prev[hh] + jnp.sum(pt, axis=0, keepdims=True))
                acc_new.append(alpha * acc_prev[hh] + _dot(vt_ref[rows[hh], :], pt, NN))
            for hh in heads:
                m_scr[hh:hh + 1, :] = m_new[hh]
                l_scr[hh:hh + 1, :] = l_new[hh]
                acc_scr[rows[hh], :] = acc_new[hh]

        @pl.when(j < i)
        def _():
            update(False)

        @pl.when(j == i)
        def _():
            update(True)
            lse_ref[...] = jnp.zeros(lse_ref.shape, F32)
            for hh in range(hps):
                rows = slice(64 * hh, 64 * (hh + 1))
                acc_scr[rows, :] = acc_scr[rows, :] / l_scr[hh:hh + 1, :]
                lse_ref[hh:hh + 1, :] = m_scr[hh:hh + 1, :] + jnp.log(l_scr[hh:hh + 1, :])
            o = acc_scr[...].T
            o_ref[...] = o
            y = o * _silu_and_grad(g_ref[...])[0]
            y_ref[...] = y.astype(MXU_DTYPE)
            yt_ref[...] = y.T.astype(MXU_DTYPE)

        if ng:
            @pl.when((pl.program_id(0) == ngrp - 1) & (step == nsteps - 1))
            def _():
                gather_finish()

    qo = lambda w, off=0: pl.BlockSpec((tq, w), lambda p, s, it_, jt_: (it_[s], p + off))
    sds = jax.ShapeDtypeStruct
    comm_scratch = [pltpu.SemaphoreType.DMA((6 * ng,)), pltpu.SemaphoreType.DMA((6 * ng,))] if ng else []
    return pl.pallas_call(
        body,
        name="mla_attn_fwd_gather" if ng else "mla_attn_fwd",
        grid_spec=pltpu.PrefetchScalarGridSpec(
            num_scalar_prefetch=2,
            grid=(ngrp, nsteps),
            in_specs=[qo(wq), pl.BlockSpec((tq, wq), lambda p, s, it_, jt_: (jt_[s], p)),
                      pl.BlockSpec((wv, tq), lambda p, s, it_, jt_: (p, jt_[s])), qo(wv, 512 // wv)] + [HBM_SPEC] * ng,
            out_specs=[qo(wv), qo(wv), pl.BlockSpec((wv, tq), lambda p, s, it_, jt_: (p, it_[s])),
                       pl.BlockSpec((None, 8, tq), lambda p, s, it_, jt_: (p, 0, it_[s]))] + [HBM_SPEC] * ng,
            scratch_shapes=[pltpu.VMEM((8, tq), F32), pltpu.VMEM((8, tq), F32), pltpu.VMEM((wv, tq), F32)] + comm_scratch,
        ),
        out_shape=[sds((t, 512), F32), sds((t, 512), MXU_DTYPE), sds((512, t), MXU_DTYPE), sds((ngrp, 8, t), F32)]
        + _gather_shapes(ws),
        compiler_params=_cparams(("arbitrary", "arbitrary") if ng else ("parallel", "arbitrary")),
    )(it, jt, q, k, vt, z, *ws)


def mla_attn_bwd(q, k, v, do, lse, dd, *, tq=256, chips=None):
    t = q.shape[0]
    nq = t // tq
    it, jt = _causal_tiles(nq, True)
    hps, wq, wv = MLA_HPS, LANES * MLA_HPS, 64 * MLA_HPS
    pcs, layer = chips if chips is not None else ([], 0)
    nc = len(pcs)
    ngrp, nsteps = MLA_HEADS // hps, int(it.shape[0])

    def body(it_ref, jt_ref, q_ref, k_ref, v_ref, do_ref, lse_ref, d_ref, *rest):
        pc_refs, rest = rest[:nc], rest[nc:]
        dq_ref, dk_ref, dv_ref = rest[:3]
        lb_refs, rest = rest[3:3 + nc], rest[3 + nc:]
        dk_scr, dv_scr = rest[:2]
        step = pl.program_id(1)
        i, j = it_ref[step], jt_ref[step]
        masks = _head_masks()
        if nc:
            chips_start, chips_finish = _chips_exchange(pc_refs, lb_refs, rest[2], rest[3], layer)

            @pl.when((pl.program_id(0) == 0) & (step == 0))
            def _():
                chips_start()

        @pl.when(step == 0)
        def _():
            dq_ref[...] = jnp.zeros(dq_ref.shape, F32)

        @pl.when(i == j)
        def _():
            dk_scr[...] = jnp.zeros(dk_scr.shape, F32)
            dv_scr[...] = jnp.zeros(dv_scr.shape, F32)

        def update(diagonal):
            qrows = pl.ds(pl.multiple_of(i * tq, tq), tq)
            heads = range(hps)
            lanes = [slice(LANES * hh, LANES * (hh + 1)) for hh in heads]
            pair = [slice(LANES * (hh // 2), LANES * (hh // 2 + 1)) for hh in heads]
            qh = [q_ref[:, lanes[hh]] for hh in heads]
            kh = [k_ref[:, lanes[hh]] for hh in heads]
            doh = []
            for hh in heads:
                dov = do_ref[:, pair[hh]]
                doh.append(jnp.where(masks[hh % 2], dov, jnp.zeros_like(dov)))
            sts = [_dot(kh[hh], qh[hh], NT) for hh in heads]
            dps = [_dot(v_ref[:, pair[hh]], doh[hh], NT) for hh in heads]
            lse = [lse_ref[hh:hh + 1, :] for hh in heads]
            ddv = [d_ref[hh:hh + 1, :] for hh in heads]
            dk_new = [dk_scr[:, lanes[hh]] for hh in heads]
            dq_new = [dq_ref[qrows, lanes[hh]] for hh in heads]
            dv_new = [dv_scr[:, pair[2 * pp]] for pp in range(hps // 2)]
            for hh in heads:
                st = sts[hh] - lse[hh]
                if diagonal:
                    key = lax.broadcasted_iota(jnp.int32, (tq, tq), 0)
                    qry = lax.broadcasted_iota(jnp.int32, (tq, tq), 1)
                    st = jnp.where(key <= qry, st, NEG)
                pt = jnp.exp(st)
                dst = pt * (dps[hh] - ddv[hh])
                dv_new[hh // 2] = dv_new[hh // 2] + _dot(pt, doh[hh], NN)
                dk_new[hh] = dk_new[hh] + _dot(dst, qh[hh], NN)
                dq_new[hh] = dq_new[hh] + _dot(dst, kh[hh], TN)
            for hh in heads:
                dk_scr[:, lanes[hh]] = dk_new[hh]
                dq_ref[qrows, lanes[hh]] = dq_new[hh]
            for pp in range(hps // 2):
                dv_scr[:, pair[2 * pp]] = dv_new[pp]

        @pl.when(j < i)
        def _():
            update(False)

        @pl.when(j == i)
        def _():
            update(True)

        @pl.when(i == nq - 1)
        def _():
            dk_ref[...] = dk_scr[...]
            dv_ref[...] = dv_scr[...]

        if nc:
            @pl.when((pl.program_id(0) == ngrp - 1) & (step == nsteps - 1))
            def _():
                chips_finish()

    qo = lambda w: pl.BlockSpec((tq, w), lambda p, s, it_, jt_: (it_[s], p))
    kv = lambda w: pl.BlockSpec((tq, w), lambda p, s, it_, jt_: (jt_[s], p))
    stat = pl.BlockSpec((None, 8, tq), lambda p, s, it_, jt_: (p, 0, it_[s]))
    sds = jax.ShapeDtypeStruct
    comm_scratch = [pltpu.SemaphoreType.DMA((3 * nc,)), pltpu.SemaphoreType.DMA((3 * nc,))] if nc else []
    return pl.pallas_call(
        body,
        name="mla_attn_bwd_chips" if nc else "mla_attn_bwd",
        grid_spec=pltpu.PrefetchScalarGridSpec(
            num_scalar_prefetch=2,
            grid=(ngrp, nsteps),
            in_specs=[qo(wq), kv(wq), kv(wv), qo(wv), stat, stat] + [HBM_SPEC] * nc,
            out_specs=[pl.BlockSpec((t, wq), lambda p, s, it_, jt_: (0, p)), kv(wq), kv(wv)] + [HBM_SPEC] * nc,
            scratch_shapes=[pltpu.VMEM((tq, wq), F32), pltpu.VMEM((tq, wv), F32)] + comm_scratch,
        ),
        out_shape=[sds((t, 1024), F32), sds((t, 1024), F32), sds((t, 512), F32)] + _chips_shapes(pcs),
        compiler_params=_cparams(("arbitrary", "arbitrary") if nc else ("parallel", "arbitrary")),
    )(it, jt, q, k, v, do, lse, dd, *pcs)


def mla_gate_bwd(dy, o, z, *, tm=256):
    t = dy.shape[0]
    wv = 64 * MLA_HPS

    def body(dy_ref, o_ref, g_ref, do_ref, dzg_ref, dd_ref):
        dyv, ov = dy_ref[...], o_ref[...]
        silu, dsilu = _silu_and_grad(g_ref[...])
        do = dyv * silu
        do_ref[...] = do.astype(MXU_DTYPE)
        dzg_ref[...] = (dyv * ov * dsilu).astype(MXU_DTYPE)
        prod = do * ov
        row = lax.broadcasted_iota(jnp.int32, (8, wv), 0)
        lane = lax.broadcasted_iota(jnp.int32, (8, wv), 1)
        pick = ((lane >= row * DIL_HD) & (lane < (row + 1) * DIL_HD)).astype(BF16)
        hi = prod.astype(BF16)
        r1 = prod - hi.astype(F32)
        mid = r1.astype(BF16)
        lo = (r1 - mid.astype(F32)).astype(BF16)
        dot = lambda u: lax.dot_general(pick, u, NT, preferred_element_type=F32)
        dd_ref[...] = dot(hi) + dot(mid) + dot(lo)

    blk = lambda off=0: pl.BlockSpec((tm, wv), lambda p, i: (i, p + off))
    sds = jax.ShapeDtypeStruct
    return pl.pallas_call(
        body,
        name="mla_gate_bwd",
        grid=(MLA_HEADS // MLA_HPS, t // tm),
        in_specs=[blk(), blk(), blk(512 // wv)],
        out_specs=[blk(), blk(), pl.BlockSpec((None, 8, tm), lambda p, i: (p, 0, i))],
        out_shape=[sds((t, 512), MXU_DTYPE), sds((t, 512), MXU_DTYPE), sds((MLA_HEADS // MLA_HPS, 8, t), F32)],
        compiler_params=_cparams(("parallel", "parallel")),
    )(dy, o, z)


SPAN = 2048
DIL_SCALE = DIL_HD ** -0.5


def _rm_src(u, window, dil):
    nn, r = divmod(u, dil)
    return pl.ds(nn * window + r, DIL_NK, stride=dil) if dil > 1 else pl.ds(u * DIL_NK, DIL_NK)


def _rm_dst(u):
    return pl.ds(u * DIL_NK, DIL_NK)


def _dil_prep_tile(ops, x, g2, c, s1, s2, scale):
    ms = ops.seg_sum(x * x) * (1.0 / DIL_HD)
    return rope(ops, x * lax.rsqrt(ms + EPS) * g2, c, s1, s2, 32) * scale


def _span_blk(base):
    return pl.BlockSpec((SPAN, LANES), lambda s, p: (s, base + p))


def _const_blk(shape):
    return pl.BlockSpec(shape, lambda s, p: (0,) * len(shape))


_DIL_TABLE_SPECS = [pl.BlockSpec((SPAN, LANES), (lambda s, p, blk=blk: (s, blk))) for blk in (3, 4, 5)]


def dil_prep_fwd(z, tabs, gq2, gk2, gi):
    window, dil = DIL_GROUPS[gi]
    t = z.shape[0]

    def body(q_ref, k_ref, v_ref, c_ref, s1_ref, s2_ref, gq_ref, gk_ref, qo_ref, ko_ref, vo_ref):
        for u in range(SPAN // DIL_NK):
            src, dst = _rm_src(u, window, dil), _rm_dst(u)
            c, s1, s2 = c_ref[src, :], s1_ref[src, :], s2_ref[src, :]
            qo_ref[dst, :] = _dil_prep_tile(PLAIN, q_ref[src, :], gq_ref[...], c, s1, s2, DIL_SCALE).astype(MXU_DTYPE)
            ko_ref[dst, :] = _dil_prep_tile(PLAIN, k_ref[src, :], gk_ref[...], c, s1, s2, 1.0).astype(MXU_DTYPE)
            vo_ref[dst, :] = v_ref[src, :].astype(MXU_DTYPE)

    return pl.pallas_call(
        body,
        name=f"dil_prep_fwd_{dil}",
        grid=(t // SPAN, 4),
        in_specs=[_span_blk(4 * gi), _span_blk(12 + 4 * gi), _span_blk(24 + 4 * gi)] + _DIL_TABLE_SPECS
        + [_const_blk((1, LANES)), _const_blk((1, LANES))],
        out_specs=[_span_blk(0)] * 3,
        out_shape=[jax.ShapeDtypeStruct((t, 512), MXU_DTYPE)] * 3,
        compiler_params=_cparams(("parallel", "parallel")),
    )(z, z, z, tabs, tabs, tabs, gq2, gk2)


def dil_prep_bwd(z, tabs, gq2, gk2, dq, dk, dv, gi):
    window, dil = DIL_GROUPS[gi]
    t = z.shape[0]

    def body(q_ref, k_ref, c_ref, s1_ref, s2_ref, gq_ref, gk_ref, dq_ref, dk_ref, dv_ref, dzq_ref, dzk_ref, dzv_ref, dgq_ref,
             dgk_ref, sq, sk, sv):
        @pl.when((pl.program_id(0) == 0) & (pl.program_id(1) == 0))
        def _():
            dgq_ref[...] = jnp.zeros((1, LANES), F32)
            dgk_ref[...] = jnp.zeros((1, LANES), F32)

        dgs = [jnp.zeros((1, LANES), F32), jnp.zeros((1, LANES), F32)]
        for u in range(SPAN // DIL_NK):
            src, dst = _rm_src(u, window, dil), _rm_dst(u)
            c, s1, s2 = c_ref[src, :], s1_ref[src, :], s2_ref[src, :]
            for idx, (x_ref, g_ref, ct_ref, scr, scale) in enumerate(((q_ref, gq_ref, dq_ref, sq, DIL_SCALE),
                                                                      (k_ref, gk_ref, dk_ref, sk, 1.0))):
                _, vjp = jax.vjp(lambda xv, gv, sc=scale: _dil_prep_tile(DIFF, xv, gv, c, s1, s2, sc), x_ref[src, :], g_ref[...])
                dx, dg = vjp(ct_ref[dst, :])
                scr[src, :] = dx
                dgs[idx] = dgs[idx] + dg
            sv[src, :] = dv_ref[dst, :]
        dgq_ref[...] += dgs[0] + pltpu.roll(dgs[0], DIL_HD, 1)
        dgk_ref[...] += dgs[1] + pltpu.roll(dgs[1], DIL_HD, 1)
        for c0 in range(0, SPAN, 256):
            rows = slice(c0, c0 + 256)
            dzq_ref[rows, :] = sq[rows, :].astype(MXU_DTYPE)
            dzk_ref[rows, :] = sk[rows, :].astype(MXU_DTYPE)
            dzv_ref[rows, :] = sv[rows, :].astype(MXU_DTYPE)

    sds = jax.ShapeDtypeStruct
    return pl.pallas_call(
        body,
        name=f"dil_prep_bwd_{dil}",
        grid=(t // SPAN, 4),
        in_specs=[_span_blk(4 * gi), _span_blk(12 + 4 * gi)] + _DIL_TABLE_SPECS
        + [_const_blk((1, LANES)), _const_blk((1, LANES)), _span_blk(0), _span_blk(0), _span_blk(0)],
        out_specs=[_span_blk(0)] * 3 + [_const_blk((1, LANES))] * 2,
        out_shape=[sds((t, 512), MXU_DTYPE)] * 3 + [sds((1, LANES), F32)] * 2,
        scratch_shapes=[pltpu.VMEM((SPAN, LANES), F32)] * 3,
        compiler_params=_cparams(("arbitrary", "arbitrary")),
    )(z, z, tabs, tabs, tabs, gq2, gk2, dq, dk, dv)


def _band_masks():
    qi = lax.broadcasted_iota(jnp.int32, (DIL_NK, DIL_NK), 0)
    ki = lax.broadcasted_iota(jnp.int32, (DIL_NK, DIL_NK), 1)
    return (ki >= qi), (ki <= qi)


def _pair_heads():
    return [(pair, hh) for pair in range(4) for hh in range(2)]


def _pair_lanes(pair):
    return slice(LANES * pair, LANES * (pair + 1))


def _zero_other_head(mask, x):
    return jnp.where(mask, x, jnp.zeros_like(x))


def dil_attn_fwd(name, q, k, v, dil):
    t = q.shape[0]

    def body(q_ref, kp_ref, kc_ref, vp_ref, vc_ref, o_ref, lse_ref):
        b = pl.program_id(0)
        mprev, mcur = _band_masks()
        mprev = mprev & (b >= dil)
        hm = _head_masks()
        heads = _pair_heads()
        qh = [_zero_other_head(hm[hh], q_ref[:, _pair_lanes(pair)]) for pair, hh in heads]
        sps = [_dot(qh[i], kp_ref[:, _pair_lanes(pair)], NT) for i, (pair, _) in enumerate(heads)]
        scs = [_dot(qh[i], kc_ref[:, _pair_lanes(pair)], NT) for i, (pair, _) in enumerate(heads)]
        o = [jnp.zeros((DIL_NK, LANES), F32) for _ in range(4)]
        lse = [jnp.zeros((DIL_NK, LANES), F32) for _ in range(4)]
        for i, (pair, hh) in enumerate(heads):
            sp, sc = jnp.where(mprev, sps[i], NEG), jnp.where(mcur, scs[i], NEG)
            m = jnp.maximum(jnp.max(sp, axis=1, keepdims=True), jnp.max(sc, axis=1, keepdims=True))
            ep, ec = jnp.exp(sp - m), jnp.exp(sc - m)
            den = jnp.sum(ep, axis=1, keepdims=True) + jnp.sum(ec, axis=1, keepdims=True)
            lanes = _pair_lanes(pair)
            oh = _dot(ep, _zero_other_head(hm[hh], vp_ref[:, lanes]), NN) + _dot(ec, _zero_other_head(hm[hh], vc_ref[:, lanes]), NN)
            o[pair] = o[pair] + oh / den
            lse[pair] = jnp.where(hm[hh], m + jnp.log(den), lse[pair])
        for pair in range(4):
            o_ref[:, _pair_lanes(pair)] = o[pair]
            lse_ref[:, _pair_lanes(pair)] = lse[pair]

    cur = pl.BlockSpec((DIL_NK, 512), lambda b: (b, 0))
    prev = pl.BlockSpec((DIL_NK, 512), lambda b: (jnp.maximum(b - dil, 0), 0))
    sds = jax.ShapeDtypeStruct
    return pl.pallas_call(
        body,
        name=name,
        grid=(t // DIL_NK,),
        in_specs=[cur, prev, cur, prev, cur],
        out_specs=[cur, cur],
        out_shape=[sds((t, 512), F32), sds((t, 512), F32)],
        compiler_params=_cparams(("parallel",)),
    )(q, k, k, v, v)


def dil_attn_bwd(name, q, k, v, do, lse, dd, dil):
    t = q.shape[0]
    nblk = t // DIL_NK

    def body(qc_ref, qn_ref, doc_ref, don_ref, lc_ref, ln_ref, dc_ref, dn_ref, kp_ref, kc_ref, vp_ref, vc_ref,
             dq_ref, dk_ref, dv_ref):
        b = pl.program_id(0)
        mprev, mcur = _band_masks()
        mnext = mprev & (b + dil < nblk)
        mprev = mprev & (b >= dil)
        hm = _head_masks()
        heads = _pair_heads()
        ln_ = [_pair_lanes(pair) for pair, _ in heads]
        qh = [_zero_other_head(hm[hh], qc_ref[:, ln_[i]]) for i, (_, hh) in enumerate(heads)]
        qnh = [_zero_other_head(hm[hh], qn_ref[:, ln_[i]]) for i, (_, hh) in enumerate(heads)]
        doh = [_zero_other_head(hm[hh], doc_ref[:, ln_[i]]) for i, (_, hh) in enumerate(heads)]
        donh = [_zero_other_head(hm[hh], don_ref[:, ln_[i]]) for i, (_, hh) in enumerate(heads)]
        idx = range(len(heads))
        s_p = [_dot(qh[i], kp_ref[:, ln_[i]], NT) for i in idx]
        s_c = [_dot(qh[i], kc_ref[:, ln_[i]], NT) for i in idx]
        s_n = [_dot(qnh[i], kc_ref[:, ln_[i]], NT) for i in idx]
        dp_p = [_dot(doh[i], vp_ref[:, ln_[i]], NT) for i in idx]
        dp_c = [_dot(doh[i], vc_ref[:, ln_[i]], NT) for i in idx]
        dp_n = [_dot(donh[i], vc_ref[:, ln_[i]], NT) for i in idx]
        dq = [jnp.zeros((DIL_NK, LANES), F32) for _ in range(4)]
        dk = [jnp.zeros((DIL_NK, LANES), F32) for _ in range(4)]
        dv = [jnp.zeros((DIL_NK, LANES), F32) for _ in range(4)]
        for i, (pair, hh) in enumerate(heads):
            lse_h, d_h = _row_scalar(lc_ref[:, ln_[i]], hm[hh]), _row_scalar(dc_ref[:, ln_[i]], hm[hh])
            lse_n, d_n = _row_scalar(ln_ref[:, ln_[i]], hm[hh]), _row_scalar(dn_ref[:, ln_[i]], hm[hh])
            pp = jnp.exp(jnp.where(mprev, s_p[i] - lse_h, NEG))
            pc = jnp.exp(jnp.where(mcur, s_c[i] - lse_h, NEG))
            pn = jnp.exp(jnp.where(mnext, s_n[i] - lse_n, NEG))
            dsp, dsc, dsn = pp * (dp_p[i] - d_h), pc * (dp_c[i] - d_h), pn * (dp_n[i] - d_n)
            dq[pair] = (dq[pair] + _dot(dsp, _zero_other_head(hm[hh], kp_ref[:, ln_[i]]), NN)
                        + _dot(dsc, _zero_other_head(hm[hh], kc_ref[:, ln_[i]]), NN))
            dv[pair] = dv[pair] + _dot(pc, doh[i], TN) + _dot(pn, donh[i], TN)
            dk[pair] = dk[pair] + _dot(dsc, qh[i], TN) + _dot(dsn, qnh[i], TN)
        for pair in range(4):
            dq_ref[:, _pair_lanes(pair)] = dq[pair]
            dk_ref[:, _pair_lanes(pair)] = dk[pair]
            dv_ref[:, _pair_lanes(pair)] = dv[pair]

    cur = pl.BlockSpec((DIL_NK, 512), lambda b: (b, 0))
    prev = pl.BlockSpec((DIL_NK, 512), lambda b: (jnp.maximum(b - dil, 0), 0))
    nxt = pl.BlockSpec((DIL_NK, 512), lambda b: (jnp.minimum(b + dil, nblk - 1), 0))
    sds = jax.ShapeDtypeStruct
    return pl.pallas_call(
        body,
        name=name,
        grid=(nblk,),
        in_specs=[cur, nxt, cur, nxt, cur, nxt, cur, nxt, prev, cur, prev, cur],
        out_specs=[cur, cur, cur],
        out_shape=[sds((t, 512), F32)] * 3,
        compiler_params=_cparams(("parallel",)),
    )(q, q, do, do, lse, lse, dd, dd, k, k, v, v)


def dil_combine(os_, lses, z):
    t = z.shape[0]

    def body(o1_ref, l1_ref, o2_ref, l2_ref, o3_ref, l3_ref, g_ref, y_ref, o_ref, lse_ref, yt_ref, so2, sl2, so3, sl3):
        for (window, dil), o_in, l_in, so, sl in ((DIL_GROUPS[1], o2_ref, l2_ref, so2, sl2), (DIL_GROUPS[2], o3_ref, l3_ref, so3, sl3)):
            for u in range(SPAN // DIL_NK):
                src, dst = _rm_src(u, window, dil), _rm_dst(u)
                so[src, :] = o_in[dst, :]
                sl[src, :] = l_in[dst, :]
        for c0 in range(0, SPAN, 256):
            rows = slice(c0, c0 + 256)
            la, lb, lc = l1_ref[rows, :], sl2[rows, :], sl3[rows, :]
            mx = jnp.maximum(jnp.maximum(la, lb), lc)
            wa, wb, wc = jnp.exp(la - mx), jnp.exp(lb - mx), jnp.exp(lc - mx)
            tot = wa + wb + wc
            o = (wa * o1_ref[rows, :] + wb * so2[rows, :] + wc * so3[rows, :]) / tot
            y = o * _silu_and_grad(g_ref[rows, :])[0]
            y_ref[rows, :] = y.astype(MXU_DTYPE)
            o_ref[rows, :] = o
            lse_ref[rows, :] = mx + jnp.log(tot)
            yt_ref[:, rows] = y.T.astype(MXU_DTYPE)

    sds = jax.ShapeDtypeStruct
    return pl.pallas_call(
        body,
        name="dil_combine",
        grid=(t // SPAN, 4),
        in_specs=[_span_blk(0)] * 6 + [_span_blk(36)],
        out_specs=[_span_blk(0)] * 3 + [pl.BlockSpec((LANES, SPAN), lambda s, p: (p, s))],
        out_shape=[sds((t, 512), MXU_DTYPE), sds((t, 512), F32), sds((t, 512), F32), sds((512, t), MXU_DTYPE)],
        scratch_shapes=[pltpu.VMEM((SPAN, LANES), F32)] * 4,
        compiler_params=_cparams(("parallel", "parallel")),
    )(os_[0], lses[0], os_[1], lses[1], os_[2], lses[2], z)


def dil_gate_bwd(dy, o, z, lse):
    t = dy.shape[0]

    def body(dy_ref, o_ref, g_ref, lse_ref, dzg_ref, do1_ref, dd1_ref, do2_ref, dd2_ref, l2_ref, do3_ref, dd3_ref, l3_ref, do_scr):
        for c0 in range(0, SPAN, 256):
            rows = slice(c0, c0 + 256)
            dyv, ov = dy_ref[rows, :], o_ref[rows, :]
            silu, dsilu = _silu_and_grad(g_ref[rows, :])
            do = dyv * silu
            do_scr[rows, :] = do
            do1_ref[rows, :] = do.astype(MXU_DTYPE)
            dd1_ref[rows, :] = _seg_sum_impl(do * ov)
            dzg_ref[rows, :] = (dyv * ov * dsilu).astype(MXU_DTYPE)
        for (window, dil), do_o, dd_o, l_o in ((DIL_GROUPS[1], do2_ref, dd2_ref, l2_ref), (DIL_GROUPS[2], do3_ref, dd3_ref, l3_ref)):
            for u in range(SPAN // DIL_NK):
                src, dst = _rm_src(u, window, dil), _rm_dst(u)
                do_o[dst, :] = do_scr[src, :].astype(MXU_DTYPE)
                dd_o[dst, :] = dd1_ref[src, :]
                l_o[dst, :] = lse_ref[src, :]

    sds = jax.ShapeDtypeStruct
    f32, mxu = sds((t, 512), F32), sds((t, 512), MXU_DTYPE)
    return pl.pallas_call(
        body,
        name="dil_gate_bwd",
        grid=(t // SPAN, 4),
        in_specs=[_span_blk(0), _span_blk(0), _span_blk(36), _span_blk(0)],
        out_specs=[_span_blk(0)] * 9,
        out_shape=[mxu, mxu, f32, mxu, f32, f32, mxu, f32, f32],
        scratch_shapes=[pltpu.VMEM((SPAN, LANES), F32)],
        compiler_params=_cparams(("parallel", "parallel")),
    )(dy, o, z, lse)


def _merge_tile(p0, p1, p2, z0, z1, z2, b0, b1, b2):
    return _sigmoid(z0 + b0) * p0 + _sigmoid(z1 + b1) * p1 + _sigmoid(z2 + b2) * p2


def _merge_ins(ps, z, b):
    w = 256
    return ([_row(p, w, 0, True) for p in ps] + [_row(z, w, 4 * i, True) for i in range(3)]
            + [_const(b, w, 4 * i, True) for i in range(3)])


def merge_fwd(ps, z, b):
    def fn(ins, outs, _):
        merged = _merge_tile(*[r[...] for r in ins])
        outs[0][...] = merged.astype(MXU_DTYPE)
        outs[1][...] = merged.T.astype(MXU_DTYPE)

    return rowwise("merge_fwd", fn, z.shape[0], ncol=4, ins=_merge_ins(ps, z, b), outs=[(1024, 256, MXU_DTYPE, True)],
                   touts=[(1024, 256, MXU_DTYPE, True)])


def merge_bwd(dm, ps, z, b):
    def fn(ins, outs, accs):
        _, vjp = jax.vjp(_merge_tile, *[r[...] for r in ins[:9]])
        grads = vjp(ins[9][...])
        for i in range(3):
            outs[i][...] = grads[i].astype(MXU_DTYPE)
            outs[3 + i][...] = grads[3 + i].astype(MXU_DTYPE)
            accs[i][...] += grads[6 + i]

    return rowwise("merge_bwd", fn, z.shape[0], ncol=4, ins=_merge_ins(ps, z, b) + [_row(dm, 256, 0, True)],
                   outs=[(1024, 256, MXU_DTYPE, True)] * 6, accs=[(1, 1024, 256, True)] * 3)


EW_BLOCK_BYTES = 2**21


def _tile2d(r, c):
    if r * c * 4 <= EW_BLOCK_BYTES:
        return r, c
    for tr in (512, 256, 128, 64, 32):
        if r % tr == 0 and tr * c * 4 <= EW_BLOCK_BYTES:
            return tr, c
    for tc in (1024, 512, 256, 128):
        if c % tc == 0 and r * tc * 4 <= EW_BLOCK_BYTES:
            return r, tc
    raise ValueError((r, c))


def adamw(name, w, g, m, v):
    shape = w.shape
    c = shape[-1]
    r = w.size // c
    tr, tc = _tile2d(r, c)
    assert tc == c
    c1, c2 = 1.0 - ADAM_B1 ** ADAM_STEP, 1.0 - ADAM_B2 ** ADAM_STEP

    def body(w_ref, g_ref, m_ref, v_ref, d_ref, mo_ref, vo_ref):
        gv = g_ref[...]
        mn = ADAM_B1 * m_ref[...] + (1.0 - ADAM_B1) * gv
        vn = ADAM_B2 * v_ref[...] + (1.0 - ADAM_B2) * (gv * gv)
        d_ref[...] = -ADAM_LR * ((mn / c1) / (jnp.sqrt(vn / c2) + ADAM_EPS) + ADAM_WD * w_ref[...])
        mo_ref[...] = mn
        vo_ref[...] = vn

    spec = pl.BlockSpec((tr, c), lambda i: (i, 0))
    outs = pl.pallas_call(
        body,
        name=name,
        grid=(r // tr,),
        in_specs=[spec] * 4,
        out_specs=[spec] * 3,
        out_shape=[jax.ShapeDtypeStruct((r, c), F32)] * 3,
        compiler_params=_cparams(("parallel",)),
    )(*[a.reshape(r, c) for a in (w, g, m, v)])
    return [o.reshape(shape) for o in outs]


def adamw_pair(name, w, reduced, received, m, v, c_idx):
    _, r, c = w.shape
    tr, tc = _tile2d(r, c)
    c1, c2 = 1.0 - ADAM_B1 ** ADAM_STEP, 1.0 - ADAM_B2 ** ADAM_STEP

    def body(c_ref, w_ref, r0_ref, x0_ref, r1_ref, x1_ref, m_ref, v_ref, g_ref, d_ref, mo_ref, vo_ref):
        mine = c_ref[0]
        gv = jnp.where(pl.program_id(0) == 0, jnp.where(mine == 0, r0_ref[...], x0_ref[...]),
                       jnp.where(mine == 1, r1_ref[...], x1_ref[...]))
        mn = ADAM_B1 * m_ref[...] + (1.0 - ADAM_B1) * gv
        vn = ADAM_B2 * v_ref[...] + (1.0 - ADAM_B2) * (gv * gv)
        g_ref[...] = gv
        d_ref[...] = -ADAM_LR * ((mn / c1) / (jnp.sqrt(vn / c2) + ADAM_EPS) + ADAM_WD * w_ref[...])
        mo_ref[...] = mn
        vo_ref[...] = vn

    full = pl.BlockSpec((None, tr, tc), lambda h, i, j, cr: (h, i, j))
    half = pl.BlockSpec((tr, tc), lambda h, i, j, cr: (i, j))
    return pl.pallas_call(
        body,
        name=name,
        grid_spec=pltpu.PrefetchScalarGridSpec(
            num_scalar_prefetch=1,
            grid=(2, r // tr, c // tc),
            in_specs=[full, half, half, half, half, full, full],
            out_specs=[full] * 4,
        ),
        out_shape=[jax.ShapeDtypeStruct((2, r, c), F32)] * 4,
        compiler_params=_cparams(("parallel", "parallel", "parallel")),
    )(c_idx, w, reduced[0], received[0], reduced[1], received[1], m, v)


def sum_pair(name, g, la, out_dtype):
    _, r, c = g.shape
    tr, tc = _tile2d(r, c)

    def body(g_ref, la_ref, o_ref):
        o_ref[...] = (g_ref[...] + la_ref[...]).astype(o_ref.dtype)

    spec = pl.BlockSpec((None, tr, tc), lambda s, i, j: (s, i, j))
    return pl.pallas_call(
        body,
        name=name,
        grid=(4, r // tr, c // tc),
        in_specs=[spec, spec],
        out_specs=spec,
        out_shape=jax.ShapeDtypeStruct((4, r, c), out_dtype),
        compiler_params=_cparams(("parallel", "parallel", "parallel")),
    )(g, la)


def sum_chips(name, g, la, lb, s_idx):
    _, r, c = g.shape
    tr, tc = _tile2d(r, c)

    def body(s_ref, g_ref, la_ref, l0_ref, l1_ref, l2_ref, o_ref):
        own = g_ref[...] + la_ref[...]
        o_ref[...] = ((own + l0_ref[...].astype(F32)) + l1_ref[...].astype(F32)) + l2_ref[...].astype(F32)

    own_spec = pl.BlockSpec((None, tr, tc), lambda i, j, sr: (sr[0], i, j))
    lspec = lambda k: pl.BlockSpec((None, tr, tc), lambda i, j, sr: (k, i, j))
    return pl.pallas_call(
        body,
        name=name,
        grid_spec=pltpu.PrefetchScalarGridSpec(
            num_scalar_prefetch=1,
            grid=(r // tr, c // tc),
            in_specs=[own_spec, own_spec, lspec(0), lspec(1), lspec(2)],
            out_specs=pl.BlockSpec((tr, tc), lambda i, j, sr: (i, j)),
        ),
        out_shape=jax.ShapeDtypeStruct((r, c), F32),
        compiler_params=_cparams(("parallel", "parallel")),
    )(s_idx, g, la, lb, lb, lb)


def _place():
    x, y, c = lax.axis_index("x"), lax.axis_index("y"), lax.axis_index("c")
    chips = [(1 - x, y), (x, 1 - y), (1 - x, 1 - y)]
    return x, y, c, chips


HBM_SPEC = pl.BlockSpec(memory_space=pltpu.HBM)


def _comm_call(name, body, ins, out_shapes, n_sem, n_local):
    return pl.pallas_call(
        body,
        name=name,
        in_specs=[HBM_SPEC] * len(ins),
        out_specs=[HBM_SPEC] * len(out_shapes),
        out_shape=out_shapes,
        scratch_shapes=[pltpu.SemaphoreType.DMA((n_sem,)), pltpu.SemaphoreType.DMA((n_sem,)),
                        pltpu.SemaphoreType.DMA((max(n_local, 1),))],
    )(*ins)


def _layer_gather(ins, outs, send, recv, layer):
    n = len(ins)
    x, y, c, chips = _place()
    s = 2 * x + y
    sib = (x, y, 1 - c)
    active = c == layer

    def rc(a, k, src, dst, dev):
        return pltpu.make_async_remote_copy(src_ref=src, dst_ref=dst, send_sem=send.at[6 * a + k], recv_sem=recv.at[6 * a + k],
                                            device_id=dev, device_id_type=MESH)

    def first_hop():
        return [rc(a, j, ins[a], outs[a].at[s], (*chip, c)) for j, chip in enumerate(chips) for a in range(n)]

    def start():
        @pl.when(active)
        def _():
            for cp in first_hop():
                cp.start()

    def finish():
        @pl.when(active)
        def _():
            forwards = []
            for j, (cx, cy) in enumerate(chips):
                for a in range(n):
                    landed = outs[a].at[2 * cx + cy]
                    rc(a, j, landed, landed, sib).wait_recv()
                    forwards.append(rc(a, 3 + j, landed, landed, sib))
                    forwards[-1].start()
            for cp in first_hop() + forwards:
                cp.wait_send()

        @pl.when(jnp.logical_not(active))
        def _():
            for j, (cx, cy) in enumerate(chips):
                for a in range(n):
                    other = outs[a].at[2 * cx + cy]
                    rc(a, 3 + j, other, other, sib).wait_recv()

    return start, finish


def _gather_shapes(ws):
    return [jax.ShapeDtypeStruct((4,) + w.shape, w.dtype) for w in ws]


def allgather_layer(ws, layer):
    n = len(ws)

    def body(*refs):
        send, recv, _ = refs[2 * n:]
        start, finish = _layer_gather(refs[:n], refs[n:2 * n], send, recv, layer)
        start()
        finish()

    return _comm_call(f"allgather_layer{layer}", body, ws, _gather_shapes(ws), 6 * n, 0)


def exchange_sibling(gs, layer, tag=""):
    n = len(gs)

    def body(*refs):
        send, recv, _ = refs[2 * n:]
        start, finish = _sibling_exchange(refs[:n], refs[n:2 * n], send, recv, layer)
        start()
        finish()

    return _comm_call(f"exchange_sibling{layer}{tag}", body, gs, [jax.ShapeDtypeStruct(g.shape, g.dtype) for g in gs], n, 0)


def _sibling_exchange(ins, outs, send, recv, layer):
    x, y, c, _ = _place()

    def copies():
        return [pltpu.make_async_remote_copy(src_ref=ins[a], dst_ref=outs[a], send_sem=send.at[a], recv_sem=recv.at[a],
                                             device_id=(x, y, 1 - c), device_id_type=MESH) for a in range(len(ins))]

    def start():
        @pl.when(c != layer)
        def _():
            for cp in copies():
                cp.start()

    def finish():
        @pl.when(c != layer)
        def _():
            for cp in copies():
                cp.wait_send()

        @pl.when(c == layer)
        def _():
            for cp in copies():
                cp.wait_recv()

    return start, finish


def sibling_hook(gs, layer):
    return (gs, [jax.ShapeDtypeStruct(g.shape, g.dtype) for g in gs], len(gs),
            lambda i, o, s, r: _sibling_exchange(i, o, s, r, layer))


def chips_hook(ps, layer):
    return (ps, _chips_shapes(ps), 3 * len(ps), lambda i, o, s, r: _chips_exchange(i, o, s, r, layer))


def _chips_exchange(ins, outs, send, recv, layer):
    n = len(ins)
    _, _, c, chips = _place()

    def copies():
        return [pltpu.make_async_remote_copy(src_ref=ins[a].at[2 * cx + cy], dst_ref=outs[a].at[j], send_sem=send.at[3 * a + j],
                                             recv_sem=recv.at[3 * a + j], device_id=(cx, cy, c), device_id_type=MESH)
                for j, (cx, cy) in enumerate(chips) for a in range(n)]

    def start():
        @pl.when(c == layer)
        def _():
            for cp in copies():
                cp.start()

    def finish():
        @pl.when(c == layer)
        def _():
            for cp in copies():
                cp.wait()

    return start, finish


def _chips_shapes(ps):
    return [jax.ShapeDtypeStruct((3,) + p.shape[1:], p.dtype) for p in ps]


def exchange_chips(ps, layer, tag=""):
    n = len(ps)

    def body(*refs):
        send, recv, _ = refs[2 * n:]
        start, finish = _chips_exchange(refs[:n], refs[n:2 * n], send, recv, layer)
        start()
        finish()

    return _comm_call(f"exchange_chips{layer}{tag}", body, ps, _chips_shapes(ps), 3 * n, 0)


def exchange_final(rs, layer, small=None):
    n = len(rs)
    ins_all = list(rs) + ([small] if small is not None else [])

    def body(*refs):
        ni = len(ins_all)
        ins, outs = refs[:ni], refs[ni:2 * ni]
        send, recv, lsem = refs[2 * ni:]
        x, y, c, _ = _place()
        s = 2 * x + y
        to_sib = [pltpu.make_async_remote_copy(src_ref=ins[a], dst_ref=outs[a], send_sem=send.at[a], recv_sem=recv.at[a],
                                               device_id=(x, y, 1 - c), device_id_type=MESH) for a in range(n)]
        rel = [(fx, fy) for fx in (0, 1) for fy in (0, 1)]

        def piece(fx, fy, cc, dst_slot):
            k = n + 2 * (2 * fx + fy) + cc
            return pltpu.make_async_remote_copy(src_ref=ins[n], dst_ref=outs[n].at[dst_slot], send_sem=send.at[k], recv_sem=recv.at[k],
                                                device_id=(x ^ fx, y ^ fy, cc), device_id_type=MESH)

        @pl.when(c == layer)
        def _():
            for cp in to_sib:
                cp.start()
            if small is not None:
                own = pltpu.make_async_copy(ins[n], outs[n].at[s], lsem.at[0])
                own.start()
                sends = [piece(fx, fy, cc, s) for fx, fy in rel for cc in (0, 1) if (fx, fy) != (0, 0) or cc != layer]
                for cp in sends:
                    cp.start()
                for fx, fy in rel[1:]:
                    piece(fx, fy, layer, 2 * (x ^ fx) + (y ^ fy)).wait_recv()
                for cp in sends:
                    cp.wait_send()
                own.wait()
            for cp in to_sib:
                cp.wait_send()

        @pl.when(c != layer)
        def _():
            for cp in to_sib:
                cp.wait_recv()
            if small is not None:
                for fx, fy in rel:
                    piece(fx, fy, 1 - layer, 2 * (x ^ fx) + (y ^ fy)).wait_recv()

    out_shapes = [jax.ShapeDtypeStruct(r.shape, r.dtype) for r in rs]
    if small is not None:
        out_shapes.append(jax.ShapeDtypeStruct((4,) + small.shape, small.dtype))
    res = _comm_call(f"exchange_final{layer}", body, ins_all, out_shapes, n + 8, 1)
    return (res[:n], res[n]) if small is not None else (res[:n], None)


def _pad_in_cols(w):
    z = lambda n: jnp.zeros(w.shape[:-1] + (n,), w.dtype)
    return jnp.concatenate([w[..., :KR_OFF], z(64), w[..., KR_OFF:KR_OFF + MLA_ROPE], z(32), w[..., KR_OFF + MLA_ROPE:]], axis=-1)


def _unpad_in_cols(g):
    return jnp.concatenate([g[..., :KR_OFF], g[..., KR_OFF + 64:KR_OFF + 96], g[..., KR_OFF + 128:]], axis=-1)


def _pad_heads(w, real):
    k = w.shape[0]
    return jnp.pad(w.reshape(k, MLA_HEADS, real), ((0, 0), (0, 0), (0, LANES - real))).reshape(k, MLA_HEADS * LANES)


def _pad_gain(g, real):
    return jnp.pad(g.reshape(1, real), ((0, 0), (0, LANES - real)))


def layer_weights(full):
    w_ukv = full["w_ukv"].reshape(128, MLA_HEADS, 2, 64)
    two = lambda g: jnp.concatenate([g, g]).reshape(1, LANES)
    return dict(
        norm_g=full["norm_g"].reshape(1, -1), w_in=full["w_in"], conv_w=full["conv_w"], conv_b=full["conv_b"].reshape(1, -1),
        wgx=full["w_gate_x"], bgx=full["b_gate_x"].reshape(LRU_BLOCKS, 1, LANES),
        wga=full["w_gate_a"], bga=full["b_gate_a"].reshape(LRU_BLOCKS, 1, LANES),
        lam=full["lru_lambda"].reshape(1, -1), w_lru_o=full["w_lru_o"],
        cq_norm_g=full["cq_norm_g"].reshape(1, -1), ckv_norm_g=full["ckv_norm_g"].reshape(1, -1),
        wq=_pad_heads(full["w_uq"], MLA_QK), wk=_pad_heads(w_ukv[:, :, 0].reshape(128, 512), 64),
        wv=w_ukv[:, :, 1].reshape(128, 512),
        gq=_pad_gain(full["mla_q_norm_g"], MLA_QK), gk=_pad_gain(full["mla_k_norm_g"], MLA_QK),
        w_mla_o=full["w_mla_o"], gq2=two(full["dil_q_norm_g"]), gk2=two(full["dil_k_norm_g"]),
        w_dil_o=full["w_dil_o"], b_merge=full["b_merge"].reshape(1, -1), w_out=full["w_out"],
    )


def layer_fwd(x, w, tabs, gather=None):
    h, ht = rmsnorm_fwd(x, w["norm_g"])
    z_lru = mm_nn("in_proj_lru", h, w["w_in"], n=2048, n_off=G_LRU)
    z_mla = mm_nn("in_proj_mla", h, w["w_in"], n=1024, n_off=G_MLA)
    z_dil = mm_nn("in_proj_dil", h, w["w_in"], n=5120, n_off=G_DIL)
    z_mrg = mm_nn("in_proj_mrg", h, w["w_in"], n=3072, n_off=G_MRG)
    hs, y_lru, yt_lru = lru_fwd(z_lru, w["conv_w"], w["conv_b"], w["wgx"], w["bgx"], w["wga"], w["bga"], w["lam"])
    qm, km, vm, vtm = mla_prep_fwd(z_mla, tabs, w)
    o_mla, y_mla, yt_mla, lse_mla, *gathered = mla_attn_fwd(qm, km, vtm, z_mla, gather=gather)
    os_, lses, dil_rm = [], [], []
    for gi, (_, dil) in enumerate(DIL_GROUPS):
        qkv = dil_prep_fwd(z_dil, tabs, w["gq2"], w["gk2"], gi)
        o, lse = dil_attn_fwd(f"dil_attn_fwd_{dil}", *qkv, dil)
        os_.append(o)
        lses.append(lse)
        dil_rm.append(qkv)
    y_dil, o_dil, lse_dil, yt_dil = dil_combine(os_, lses, z_dil)
    ps = [mm_nn("proj_lru", y_lru, w["w_lru_o"]), mm_nn("proj_mla", y_mla, w["w_mla_o"]), mm_nn("proj_dil", y_dil, w["w_dil_o"])]
    merged, merged_t = merge_fwd(ps, z_mrg, w["b_merge"])
    out = mm_nn("out_proj", merged, w["w_out"], add=x)
    res = dict(x=x, ht=ht, z_lru=z_lru, z_mla=z_mla, z_dil=z_dil, z_mrg=z_mrg, hs=hs, yt_lru=yt_lru, qm=qm, km=km, vm=vm, o_mla=o_mla,
               yt_mla=yt_mla, lse_mla=lse_mla, dil_rm=dil_rm, yt_dil=yt_dil, o_dil=o_dil, lse_dil=lse_dil, ps=ps, merged_t=merged_t)
    return out, res, gathered


def layer_bwd(dout, r, w, tabs, prev=None, own=None):
    g = {}
    dmerged = mm_nt("out_proj_dx", dout, w["w_out"])
    g["w_out"] = mm_nn("out_proj_dw", r["merged_t"], dout, tm=1024, tn=512, tk=2048)
    dp0, dp1, dp2, dzm0, dzm1, dzm2, db0, db1, db2 = merge_bwd(dmerged, r["ps"], r["z_mrg"], w["b_merge"])
    g["b_merge"] = jnp.concatenate([db0, db1, db2], axis=1).reshape(-1)
    dy_lru = mm_nt("proj_lru_dx", dp0, w["w_lru_o"])
    dy_mla = mm_nt("proj_mla_dx", dp1, w["w_mla_o"])
    dy_dil = mm_nt("proj_dil_dx", dp2, w["w_dil_o"])
    g["w_lru_o"] = mm_nn("proj_lru_dw", r["yt_lru"], dp0, **DW_TILES)
    g["w_mla_o"] = mm_nn("proj_mla_dw", r["yt_mla"], dp1, **DW_TILES)
    g["w_dil_o"] = mm_nn("proj_dil_dw", r["yt_dil"], dp2, **DW_TILES)
    dzx, dzg_lru, dwgx, dbgx, dwga, dbga, dlam, dcw, dcb, *la_prev = lru_bwd(
        r["z_lru"], r["hs"], dy_lru, w["conv_w"], w["conv_b"], w["wgx"], w["bgx"], w["wga"], w["bga"], w["lam"],
        comm=None if prev is None else sibling_hook(prev[0], prev[1]))
    chips = None if prev is None else (prev[2](la_prev), prev[1])
    g.update(w_gate_x=dwgx, b_gate_x=dbgx.reshape(LRU_BLOCKS, LANES), w_gate_a=dwga, b_gate_a=dbga.reshape(LRU_BLOCKS, LANES),
             lru_lambda=dlam.reshape(-1), conv_w=dcw, conv_b=dcb.reshape(-1))
    do_m, dzg_mla, dd_m = mla_gate_bwd(dy_mla, r["o_mla"], r["z_mla"])
    dq_m, dk_m, dv_m, *lb_prev = mla_attn_bwd(r["qm"], r["km"], r["vm"], do_m, r["lse_mla"], dd_m, chips=chips)
    dz_mla3, dg_cq, dg_ckv, dwq, dwk, dwv, dgq, dgk = mla_prep_bwd(r["z_mla"], tabs, w, dq_m, dk_m, dv_m)
    g.update(cq_norm_g=dg_cq.reshape(-1), ckv_norm_g=dg_ckv.reshape(-1), mla_q_norm_g=dgq[0, :MLA_QK], mla_k_norm_g=dgk[0, :MLA_QK])
    g["w_uq"] = dwq.reshape(256, MLA_HEADS, LANES)[:, :, :MLA_QK].reshape(256, MLA_HEADS * MLA_QK)
    g["w_ukv"] = jnp.concatenate([dwk.reshape(128, MLA_HEADS, LANES)[:, :, :64], dwv.reshape(128, MLA_HEADS, 64)], axis=2).reshape(128, 1024)
    dzg_dil, do1, dd1, do2, dd2, l2, do3, dd3, l3 = dil_gate_bwd(dy_dil, r["o_dil"], r["z_dil"], r["lse_dil"])
    stats = [(do1, r["lse_dil"], dd1), (do2, l2, dd2), (do3, l3, dd3)]
    dzq, dzk, dzv, dgq2, dgk2 = [], [], [], [], []
    for gi, (_, dil) in enumerate(DIL_GROUPS):
        dq, dk, dv = dil_attn_bwd(f"dil_attn_bwd_{dil}", *r["dil_rm"][gi], *stats[gi], dil)
        parts = dil_prep_bwd(r["z_dil"], tabs, w["gq2"], w["gk2"], dq, dk, dv, gi)
        for acc, part in zip((dzq, dzk, dzv, dgq2, dgk2), parts):
            acc.append(part)
    g.update(dil_q_norm_g=sum(dgq2)[0, :DIL_HD], dil_k_norm_g=sum(dgk2)[0, :DIL_HD])
    dz = jnp.concatenate([dzx, dzg_lru, dz_mla3, dzg_mla] + dzq + dzk + dzv + [dzg_dil, dzm0, dzm1, dzm2], axis=1)
    g["w_in"] = _unpad_in_cols(mm_nn("in_proj_dw", r["ht"], dz, **DW_TILES))
    own_hook = None if own is None else chips_hook(*own(g))
    out = mm_nt("in_proj_dx", dz, w["w_in"], tm=1024, tn=1024, tk=IN_PAD // 4, comm=own_hook)
    dh, lb_own = (out, []) if own_hook is None else (out[0], out[1:])
    dx, dng = rmsnorm_bwd(r["x"], dh, dout, w["norm_g"])
    g["norm_g"] = dng.reshape(-1)
    return dx, g, dict(la_prev=la_prev, lb_prev=lb_prev, lb_own=lb_own)


def local_step(x, positions, target, full0, full1=None, gather1=None, reduce=None):
    tabs = rope_tables(positions.reshape(-1, 1))
    ws, ress = [], []
    for l in range(2):
        if l == 0:
            ws.append(layer_weights(full0))
            x, res, gathered = layer_fwd(x, ws[0], tabs, gather=None if gather1 is None else (gather1[0], 1))
        else:
            ws.append(layer_weights(full1 if gather1 is None else gather1[1](gathered)))
            x, res, _ = layer_fwd(x, ws[1], tabs)
        ress.append(res)
    dy, loss = loss_head(x, target)
    dy, grads1, _ = layer_bwd(dy, ress[1], ws[1], tabs)
    prev1, own0 = (None, None) if reduce is None else (reduce[0](grads1), reduce[1])
    dy, grads0, landed = layer_bwd(dy, ress[0], ws[0], tabs, prev=prev1, own=own0)
    return loss, dy, [grads0, grads1], landed


WEIGHTS = ["norm_g", "w_in", "conv_w", "conv_b", "w_gate_x", "b_gate_x", "w_gate_a", "b_gate_a", "lru_lambda", "w_lru_o", "cq_norm_g",
           "ckv_norm_g", "w_uq", "w_ukv", "mla_q_norm_g", "mla_k_norm_g", "w_mla_o", "dil_q_norm_g", "dil_k_norm_g", "w_dil_o", "b_merge",
           "w_out"]
SHARDED = {"w_in": 2, "conv_w": 2, "w_lru_o": 1, "w_uq": 2, "w_ukv": 2, "w_mla_o": 2, "w_dil_o": 2, "w_out": 1}
REPLICATED = [n for n in WEIGHTS if n not in SHARDED]
SMALL_ROWS = 144


def kernel(x, positions, norm_g, w_in, conv_w, conv_b, w_gate_x, b_gate_x, w_gate_a, b_gate_a, lru_lambda, w_lru_o, cq_norm_g, ckv_norm_g, w_uq, w_ukv, mla_q_norm_g, mla_k_norm_g, w_mla_o, dil_q_norm_g, dil_k_norm_g, w_dil_o, b_merge, w_out, loss_target, m_norm_g, m_w_in, m_conv_w, m_conv_b, m_w_gate_x, m_b_gate_x, m_w_gate_a, m_b_gate_a, m_lru_lambda, m_w_lru_o, m_cq_norm_g, m_ckv_norm_g, m_w_uq, m_w_ukv, m_mla_q_norm_g, m_mla_k_norm_g, m_w_mla_o, m_dil_q_norm_g, m_dil_k_norm_g, m_w_dil_o, m_b_merge, m_w_out, v_norm_g, v_w_in, v_conv_w, v_conv_b, v_w_gate_x, v_b_gate_x, v_w_gate_a, v_b_gate_a, v_lru_lambda, v_w_lru_o, v_cq_norm_g, v_ckv_norm_g, v_w_uq, v_w_ukv, v_mla_q_norm_g, v_mla_k_norm_g, v_w_mla_o, v_dil_q_norm_g, v_dil_k_norm_g, v_w_dil_o, v_b_merge, v_w_out):
    args = locals()
    w = {n: args[n] for n in WEIGHTS}
    m = {n: args["m_" + n] for n in WEIGHTS}
    v = {n: args["v_" + n] for n in WEIGHTS}
    my_c = lax.axis_index("c").astype(jnp.int32)
    my_s = (2 * lax.axis_index("x") + lax.axis_index("y")).astype(jnp.int32)
    c_idx = my_c.reshape(1)
    s_idx = my_s.reshape(1)

    names = list(SHARDED)
    wire = [[w[n][l] if n == "conv_w" else w[n][l].astype(BF16) for n in names] for l in range(2)]

    def assemble(l, gathered):
        full = {n: w[n][l] for n in REPLICATED}
        for n, ga, own in zip(names, gathered, wire[l]):
            full[n] = jnp.concatenate([jnp.where(my_s == s, own, ga[s]) for s in range(4)], axis=SHARDED[n] - 1)
        full["w_in"] = _pad_in_cols(full["w_in"])
        return full

    full0 = assemble(0, allgather_layer(wire[0], 0))

    def shards_of(grads_l):
        gs = []
        for n in names:
            g_ = grads_l[n]
            if n == "w_in":
                gs.append(g_.reshape(D_MODEL, 4, IN_WIDTH // 4).transpose(1, 2, 0))
            else:
                parts = jnp.stack(jnp.split(g_, 4, axis=SHARDED[n] - 1))
                gs.append(parts.reshape(4, -1, parts.shape[-1]))
        return gs

    def pair_sums(l, gs, la, tags):
        dts = [BF16 if g_.size >= 2**19 and t != "small" else F32 for g_, t in zip(gs, tags)]
        return [sum_pair(f"sum_pair{l}_{t}", g_, l_, dt) for t, g_, l_, dt in zip(tags, gs, la, dts)]

    def chip_sums(l, gs, la, lb, tags):
        return [sum_chips(f"sum_chips{l}_{t}", g_, l_, b_, s_idx) for t, g_, l_, b_ in zip(tags, gs, la, lb)]

    stash = {}

    def prev1(grads1):
        stash["gs1"] = shards_of(grads1)
        return stash["gs1"], 1, lambda la: pair_sums(1, stash["gs1"], la, names)

    def own0(grads0):
        stash["gs0"] = shards_of(grads0)
        stash["la0"] = exchange_sibling(stash["gs0"], 0)
        return pair_sums(0, stash["gs0"], stash["la0"], names), 0

    loss, grad_x, grads, landed = local_step(x[0], positions[0], loss_target[0], full0,
                                             gather1=(wire[1], functools.partial(assemble, 1)), reduce=(prev1, own0))
    loss = lax.psum(loss, ("x", "y", "c"))
    reduced1 = chip_sums(1, stash["gs1"], landed["la_prev"], landed["lb_prev"], names)
    received1, _ = exchange_final(reduced1, 1)
    reduced0 = chip_sums(0, stash["gs0"], stash["la0"], landed["lb_own"], names)

    flat = jnp.concatenate([jnp.stack([grads[0][n], grads[1][n]]).reshape(-1) for n in REPLICATED])
    small = [jnp.pad(flat, (0, 4 * SMALL_ROWS * 1024 - flat.size)).reshape(4, SMALL_ROWS, 1024)]
    la_s = exchange_sibling(small, 0, tag="_small")
    lb_s = exchange_chips(pair_sums(0, small, la_s, ["small"]), 0, tag="_small")
    received0, small_all = exchange_final(reduced0, 0, small=chip_sums(0, small, la_s, lb_s, ["small"])[0])

    g_local, delta, new_m, new_v = {}, {}, {}, {}
    for i, n in enumerate(names):
        shp = w[n].shape
        if n == "w_in":
            as3 = lambda a: a.transpose(0, 2, 1)
            back = lambda o: o.transpose(0, 2, 1)
        else:
            as3 = lambda a, rc=reduced0[i].shape: a.reshape((2,) + rc)
            back = lambda o, shp=shp: o.reshape(shp)
        outs = adamw_pair(f"adamw_{n}", as3(w[n]), (reduced0[i], reduced1[i]), (received0[i], received1[i]), as3(m[n]), as3(v[n]),
                          c_idx)
        g_local[n], delta[n], new_m[n], new_v[n] = [back(o) for o in outs]
    flat = small_all.reshape(-1)
    off = 0
    for n in REPLICATED:
        g_local[n] = flat[off:off + w[n].size].reshape(w[n].shape)
        off += w[n].size
        delta[n], new_m[n], new_v[n] = adamw(f"adamw_{n}", w[n], g_local[n], m[n], v[n])
    return (loss, grad_x[None], *[g_local[n] for n in WEIGHTS], *[delta[n] for n in WEIGHTS], *[new_m[n] for n in WEIGHTS],
            *[new_v[n] for n in WEIGHTS])
```

```python
import functools

import jax
import jax.numpy as jnp
from jax import lax
from jax.experimental import pallas as pl
from jax.experimental.pallas import tpu as pltpu

F32 = jnp.float32
BF16 = jnp.bfloat16
MXU_DTYPE = jnp.bfloat16

D_MODEL = 1024
EPS = 1e-6
ROPE_THETA = 10000.0
LRU_BLOCKS = 8
LRU_C = 8.0
MLA_HEADS = 8
MLA_NOPE = 64
MLA_ROPE = 32
MLA_QK = 96
DIL_GROUPS = ((128, 1), (512, 4), (2048, 16))
DIL_HD = 64
DIL_NK = 128
MLA_HPS = 8
IN_WIDTH = 11168
ADAM_LR, ADAM_B1, ADAM_B2, ADAM_EPS, ADAM_WD, ADAM_STEP = 0.001, 0.9, 0.999, 1e-08, 0.01, 10

LANES = 128
G_LRU, G_MLA, G_DIL, G_MRG = 0, 2048, 3072, 8192
IN_PAD = 11264
KR_OFF = 2432

NN = (((1,), (0,)), ((), ()))
NT = (((1,), (1,)), ((), ()))
TN = (((0,), (0,)), ((), ()))
NEG = -1e30
MESH = pl.DeviceIdType.MESH
VMEM_LIMIT = 48 * 2**20


def _cparams(sem):
    return pltpu.CompilerParams(dimension_semantics=sem, vmem_limit_bytes=VMEM_LIMIT)


def _dot(a, b, dims):
    return lax.dot_general(a.astype(MXU_DTYPE), b.astype(MXU_DTYPE), dims, preferred_element_type=F32)


@jax.custom_vjp
def mm(a, w):
    return _dot(a, w, NN)


def _mm_fwd(a, w):
    return _dot(a, w, NN), (a, w)


def _mm_bwd(res, g):
    a, w = res
    return _dot(g, w, NT), _dot(a, g, TN)


mm.defvjp(_mm_fwd, _mm_bwd)


def _seg64_matrix():
    r = lax.broadcasted_iota(jnp.int32, (LANES, LANES), 0) < DIL_HD
    c = lax.broadcasted_iota(jnp.int32, (LANES, LANES), 1) < DIL_HD
    return (r == c).astype(BF16)


def _seg_sum_impl(x):
    b = _seg64_matrix()
    hi = x.astype(BF16)
    r1 = x - hi.astype(F32)
    mid = r1.astype(BF16)
    lo = (r1 - mid.astype(F32)).astype(BF16)
    dot = lambda u: lax.dot_general(u, b, NN, preferred_element_type=F32)
    return dot(hi) + dot(mid) + dot(lo)


@jax.custom_vjp
def seg_sum(x):
    return _seg_sum_impl(x)


seg_sum.defvjp(lambda x: (_seg_sum_impl(x), None), lambda _, g: (_seg_sum_impl(g),))


def _lroll_impl(x, s):
    return pltpu.roll(x, s % LANES, 1)


@functools.partial(jax.custom_vjp, nondiff_argnums=(1,))
def lroll(x, s):
    return _lroll_impl(x, s)


lroll.defvjp(lambda x, s: (_lroll_impl(x, s), None), lambda s, _, g: (_lroll_impl(g, -s),))


class _Ops:
    def __init__(self, diff):
        self.mm = mm if diff else (lambda a, w: _dot(a, w, NN))
        self.seg_sum = seg_sum if diff else _seg_sum_impl
        self.lroll = lroll if diff else _lroll_impl


PLAIN, DIFF = _Ops(False), _Ops(True)


def rms(x, g, n):
    ms = jnp.sum(x * x, axis=-1, keepdims=True) * (1.0 / n)
    return x * lax.rsqrt(ms + EPS) * g


def rope(ops, x, c, s1, s2, half):
    return x * c + ops.lroll(x, -half) * s1 + ops.lroll(x, half) * s2


def _sigmoid(x):
    return 1.0 / (1.0 + jnp.exp(-x))


def _silu_and_grad(g):
    sg = _sigmoid(g)
    return g * sg, sg * (1.0 + g * (1.0 - sg))


def _softplus(x):
    return jnp.maximum(x, 0.0) + jnp.log(1.0 + jnp.exp(-jnp.abs(x)))


def _expm1(y):
    series = y * (1.0 + y * (0.5 + y * (1.0 / 6.0 + y * (1.0 / 24.0 + y * (1.0 / 120.0)))))
    return jnp.where(y > -0.05, series, jnp.exp(jnp.minimum(y, -0.05)) - 1.0)


def _host(body, n_in, n_out, n_scr, hook, grid):
    if hook is None:
        return body, [], [], [], [], []
    ins, out_shapes, n_sem, make = hook
    ni, no = len(ins), len(out_shapes)
    hbm = pl.BlockSpec(memory_space=pltpu.HBM)

    def at(corner):
        cond = None
        for axis, size in enumerate(grid):
            here = pl.program_id(axis) == (size - 1 if corner else 0)
            cond = here if cond is None else cond & here
        return cond

    def hosted(*refs):
        a, refs = refs[:n_in], refs[n_in:]
        xi, refs = refs[:ni], refs[ni:]
        o, refs = refs[:n_out], refs[n_out:]
        xo, refs = refs[:no], refs[no:]
        scr, (send, recv) = refs[:n_scr], refs[n_scr:]
        start, finish = make(xi, xo, send, recv)
        pl.when(at(False))(start)
        body(*a, *o, *scr)
        pl.when(at(True))(finish)

    sems = [pltpu.SemaphoreType.DMA((n_sem,)), pltpu.SemaphoreType.DMA((n_sem,))]
    return hosted, list(ins), [hbm] * ni, [hbm] * no, list(out_shapes), sems


def _mm_call(name, a, b, *, mode, m, n, k, a_blk, b_blk, out_dtype, add, tm, tn, tk, comm=None):
    nk = k // tk
    assert m % tm == 0 and n % tn == 0 and k % tk == 0, (name, m, n, k, tm, tn, tk)
    dims = {"nn": NN, "nt": NT, "tn": TN}[mode]

    def body(*refs):
        if add is None:
            a_ref, b_ref, o_ref, *scr = refs
            add_ref = None
        else:
            a_ref, b_ref, add_ref, o_ref, *scr = refs
        part = _dot(a_ref[...], b_ref[...], dims)

        def finish(acc):
            if add_ref is not None:
                acc = acc + add_ref[...]
            o_ref[...] = acc.astype(o_ref.dtype)

        if nk == 1:
            finish(part)
        else:
            (acc_ref,) = scr
            kk = pl.program_id(2)

            @pl.when(kk == 0)
            def _():
                acc_ref[...] = part

            @pl.when(kk > 0)
            def _():
                acc_ref[...] += part

            @pl.when(kk == nk - 1)
            def _():
                finish(acc_ref[...])

    in_specs = [a_blk, b_blk]
    args = [a, b]
    if add is not None:
        in_specs.append(pl.BlockSpec((tm, tn), lambda j, i, kk: (i, j)))
        args.append(add)
    grid = (n // tn, m // tm, nk)
    scratch = [] if nk == 1 else [pltpu.VMEM((tm, tn), F32)]
    body, x_in, x_in_specs, x_out_specs, x_out_shapes, sems = _host(body, len(args), 1, len(scratch), comm, grid)
    res = pl.pallas_call(
        body,
        name=name,
        grid=grid,
        in_specs=in_specs + x_in_specs,
        out_specs=[pl.BlockSpec((tm, tn), lambda j, i, kk: (i, j))] + x_out_specs,
        out_shape=[jax.ShapeDtypeStruct((m, n), out_dtype)] + x_out_shapes,
        scratch_shapes=scratch + sems,
        compiler_params=_cparams(("arbitrary",) * 3 if comm is not None else ("parallel", "parallel", "arbitrary")),
    )(*args, *x_in)
    return res[0] if comm is None else res


def mm_nn(name, a, b, *, n=None, n_off=0, out_dtype=F32, add=None, tm=512, tn=1024, tk=1024):
    m, k = a.shape
    n = b.shape[1] if n is None else n
    tm, tn, tk = min(tm, m), min(tn, n), min(k, tk)
    ob = n_off // tn
    assert n_off % tn == 0
    return _mm_call(name, a, b, mode="nn", m=m, n=n, k=k, out_dtype=out_dtype, add=add, tm=tm, tn=tn, tk=tk,
                    a_blk=pl.BlockSpec((tm, tk), lambda j, i, kk: (i, kk)),
                    b_blk=pl.BlockSpec((tk, tn), lambda j, i, kk: (kk, j + ob)))


def mm_nt(name, a, b, *, out_dtype=F32, tm=512, tn=1024, tk=1024, comm=None):
    m, k = a.shape
    n = b.shape[0]
    tn, tk = min(tn, n), min(k, tk)
    return _mm_call(name, a, b, mode="nt", m=m, n=n, k=k, out_dtype=out_dtype, add=None, tm=tm, tn=tn, tk=tk, comm=comm,
                    a_blk=pl.BlockSpec((tm, tk), lambda j, i, kk: (i, kk)),
                    b_blk=pl.BlockSpec((tn, tk), lambda j, i, kk: (j, kk)))


DW_TILES = dict(tm=1024, tn=512, tk=4096)


def rowwise(name, fn, t, *, tm=256, ncol=1, ins=(), outs=(), touts=(), accs=()):
    n_in, n_out, n_acc = len(ins), len(outs) + len(touts), len(accs)

    def zero_when(ref, cond):
        @pl.when(cond)
        def _():
            ref[...] = jnp.zeros(ref.shape, ref.dtype)

    def body(*refs):
        in_refs, out_refs, acc_refs = refs[:n_in], refs[n_in:n_in + n_out], refs[n_in + n_out:]
        j, i = pl.program_id(0), pl.program_id(1)
        for ref, (_, _, _, cd) in zip(acc_refs, accs):
            zero_when(ref, (i == 0) if cd else ((i == 0) & (j == 0)))
        fn(in_refs, out_refs, acc_refs)

    def in_spec(arr, w, base, cd, rd):
        rows = tm if rd else arr.shape[0]
        return pl.BlockSpec((rows, w), lambda j, i: (i if rd else 0, base + (j if cd else 0)))

    in_specs = [in_spec(*e) for e in ins]
    out_specs = [pl.BlockSpec((tm, w), (lambda j, i, cd=cd: (i, j if cd else 0))) for (_, w, _, cd) in outs]
    out_specs += [pl.BlockSpec((w, tm), (lambda j, i, cd=cd: (j if cd else 0, i))) for (_, w, _, cd) in touts]
    out_specs += [pl.BlockSpec((r, w), (lambda j, i, cd=cd: (0, j if cd else 0))) for (r, _, w, cd) in accs]
    out_shape = [jax.ShapeDtypeStruct((t, c), dt) for (c, _, dt, _) in outs]
    out_shape += [jax.ShapeDtypeStruct((r, t), dt) for (r, _, dt, _) in touts]
    out_shape += [jax.ShapeDtypeStruct((r, c), F32) for (r, c, _, _) in accs]
    res = pl.pallas_call(
        body,
        name=name,
        grid=(ncol, t // tm),
        in_specs=in_specs,
        out_specs=out_specs,
        out_shape=out_shape,
        compiler_params=_cparams(("arbitrary", "arbitrary") if accs else ("parallel", "parallel")),
    )(*[e[0] for e in ins])
    return res


def _row(arr, w=None, base=0, cd=False):
    return (arr, arr.shape[1] if w is None else w, base, cd, True)


def _const(arr, w=None, base=0, cd=False):
    return (arr, arr.shape[1] if w is None else w, base, cd, False)


def rope_tables(positions):
    t = positions.shape[0]

    def fn(ins, outs, _):
        pos = ins[0][...].astype(F32)
        lane = lax.broadcasted_iota(jnp.int32, (1, LANES), 1)
        log_theta = jnp.log(jnp.float32(ROPE_THETA))
        jm = lane - MLA_NOPE
        idx = jnp.clip(jnp.where(jm < 16, jm, jm - 16), 0, 15).astype(F32)
        ang = pos * jnp.exp(-(idx * (2.0 / MLA_ROPE)) * log_theta)
        cos, sin = jnp.cos(ang), jnp.sin(ang)
        in_rope = (lane >= MLA_NOPE) & (lane < MLA_QK)
        outs[0][:, 0:128] = jnp.where(lane < MLA_NOPE, 1.0, jnp.where(in_rope, cos, 0.0))
        outs[0][:, 128:256] = jnp.where(in_rope & (jm < 16), -sin, 0.0)
        outs[0][:, 256:384] = jnp.where(in_rope & (jm >= 16), sin, 0.0)
        jd = lane & (DIL_HD - 1)
        idx = (jd & 31).astype(F32)
        ang = pos * jnp.exp(-(idx * (2.0 / DIL_HD)) * log_theta)
        cos, sin = jnp.cos(ang), jnp.sin(ang)
        outs[0][:, 384:512] = cos
        outs[0][:, 512:640] = jnp.where(jd < 32, -sin, 0.0)
        outs[0][:, 640:768] = jnp.where(jd >= 32, sin, 0.0)

    return rowwise("rope_tables", fn, t, ins=[_row(positions)], outs=[(768, 768, F32, False)])[0]


def rmsnorm_fwd(x, g):
    def fn(ins, outs, _):
        h = rms(ins[0][...], ins[1][...], D_MODEL)
        outs[0][...] = h.astype(MXU_DTYPE)
        outs[1][...] = h.T.astype(MXU_DTYPE)

    return rowwise("rmsnorm_fwd", fn, x.shape[0], ins=[_row(x), _const(g)], outs=[(D_MODEL, D_MODEL, MXU_DTYPE, False)],
                   touts=[(D_MODEL, D_MODEL, MXU_DTYPE, False)])


def rmsnorm_bwd(x, dh, dres, g):
    def fn(ins, outs, accs):
        _, vjp = jax.vjp(lambda xv, gv: rms(xv, gv, D_MODEL), ins[0][...], ins[3][...])
        dx, dg = vjp(ins[1][...])
        outs[0][...] = ins[2][...] + dx
        accs[0][...] += dg

    return rowwise("rmsnorm_bwd", fn, x.shape[0], ins=[_row(x), _row(dh), _row(dres), _const(g)],
                   outs=[(D_MODEL, D_MODEL, F32, False)], accs=[(1, D_MODEL, D_MODEL, False)])


def loss_head(y, target):
    def fn(ins, outs, accs):
        err = ins[0][...] - ins[1][...]
        outs[0][...] = err * (1.0 / D_MODEL)
        accs[0][...] += jnp.sum(err * err, axis=0, keepdims=True)
        accs[1][...] = jnp.broadcast_to(jnp.sum(accs[0][...], keepdims=True), (1, LANES))

    dy, _, tot = rowwise("loss_head", fn, y.shape[0], ins=[_row(y), _row(target)], outs=[(D_MODEL, D_MODEL, F32, False)],
                         accs=[(1, D_MODEL, D_MODEL, False), (1, LANES, LANES, False)])
    return dy, tot[0, 0] * (0.5 / D_MODEL)


def _shift_rows(v, d, fill, reverse):
    tb = v.shape[0]
    if d % 8 == 0:
        pad = jnp.full((d, v.shape[1]), fill, v.dtype)
        return jnp.concatenate([v[d:], pad] if reverse else [pad, v[:tb - d]], axis=0)
    rows = lax.broadcasted_iota(jnp.int32, v.shape, 0)
    if not reverse:
        return jnp.where(rows >= d, pltpu.roll(v, d, 0), fill)
    return jnp.where(rows < tb - d, pltpu.roll(v, tb - d, 0), fill)


def _scan_tile(a, b, reverse):
    d = 1
    while d < a.shape[0]:
        b = b + a * _shift_rows(b, d, 0.0, reverse)
        a = a * _shift_rows(a, d, 1.0, reverse)
        d *= 2
    return a, b


def _lru_gates(ops, xc, wgx, bgx, wga, bga, lam):
    gx = _sigmoid(ops.mm(xc, wgx) + bgx)
    ga = _sigmoid(ops.mm(xc, wga) + bga)
    log_a = -LRU_C * ga * _softplus(-lam)
    a = jnp.exp(log_a)
    mult = jnp.sqrt(-_expm1(2.0 * log_a))
    return a, mult * (gx * xc)


def _shifted_inputs(x, halo, tb):
    rows = lax.broadcasted_iota(jnp.int32, x.shape, 0)
    pad = jnp.zeros((tb - 8, LANES), F32)
    out = []
    for d in (3, 2, 1):
        head = jnp.concatenate([pltpu.roll(halo, d, 0), pad], axis=0)
        out.append(jnp.where(rows >= d, pltpu.roll(x, d, 0), head))
    return out + [x]


def _lru_specs(nt, tb, reverse):
    hb = tb // 8
    tt = (lambda t: nt - 1 - t) if reverse else (lambda t: t)
    blk = lambda off: pl.BlockSpec((tb, LANES), lambda n, t: (tt(t), n + off))
    halo = lambda off: pl.BlockSpec((8, LANES), lambda n, t: (jnp.maximum(tt(t) * hb - 1, 0), n + off))
    chan = lambda r: pl.BlockSpec((r, LANES), lambda n, t: (0, n))
    wblk = pl.BlockSpec((None, LANES, LANES), lambda n, t: (n, 0, 0))
    bblk = pl.BlockSpec((None, 1, LANES), lambda n, t: (n, 0, 0))
    return blk, halo, chan, wblk, bblk


def lru_fwd(z, conv_w, conv_b, wgx, bgx, wga, bga, lam, *, tb=256, comm=None):
    t = z.shape[0]
    nt = t // tb
    blk, halo, chan, wblk, bblk = _lru_specs(nt, tb, False)

    def body(x_ref, xh_ref, g_ref, cw_ref, cb_ref, wgx_ref, bgx_ref, wga_ref, bga_ref, lam_ref, h_ref, y_ref, yt_ref, carry_ref):
        ti = pl.program_id(1)

        @pl.when(ti == 0)
        def _():
            carry_ref[...] = jnp.zeros((8, LANES), F32)

        x = x_ref[...]
        hal = jnp.where(ti > 0, xh_ref[...], 0.0)
        xs = _shifted_inputs(x, hal, tb)
        xc = cb_ref[...] + sum(xs[k] * cw_ref[k:k + 1, :] for k in range(4))
        a, b = _lru_gates(PLAIN, xc, wgx_ref[...], bgx_ref[...], wga_ref[...], bga_ref[...], lam_ref[...])
        acum, h0 = _scan_tile(a, b, False)
        h = h0 + acum * carry_ref[7:8, :]
        carry_ref[...] = h[tb - 8:tb, :]
        h_ref[...] = h
        y = h * _silu_and_grad(g_ref[...])[0]
        y_ref[...] = y.astype(MXU_DTYPE)
        yt_ref[...] = y.T.astype(MXU_DTYPE)

    grid = (LRU_BLOCKS, nt)
    body, x_in, x_in_specs, x_out_specs, x_out_shapes, sems = _host(body, 10, 3, 1, comm, grid)
    return pl.pallas_call(
        body,
        name="lru_fwd" if comm is None else "lru_fwd_comm",
        grid=grid,
        in_specs=[blk(0), halo(0), blk(LRU_BLOCKS), chan(4), chan(1), wblk, bblk, wblk, bblk, chan(1)] + x_in_specs,
        out_specs=[blk(0), blk(0), pl.BlockSpec((LANES, tb), lambda n, t_: (n, t_))] + x_out_specs,
        out_shape=[jax.ShapeDtypeStruct((t, 1024), F32), jax.ShapeDtypeStruct((t, 1024), MXU_DTYPE),
                   jax.ShapeDtypeStruct((1024, t), MXU_DTYPE)] + x_out_shapes,
        scratch_shapes=[pltpu.VMEM((8, LANES), F32)] + sems,
        compiler_params=_cparams(("arbitrary", "arbitrary") if comm is not None else ("parallel", "arbitrary")),
    )(z, z, z, conv_w, conv_b, wgx, bgx, wga, bga, lam, *x_in)


def lru_bwd(z, h, dy, conv_w, conv_b, wgx, bgx, wga, bga, lam, *, tb=256, comm=None):
    t = z.shape[0]
    nt = t // tb
    blk, halo, chan, wblk, bblk = _lru_specs(nt, tb, True)

    def body(x_ref, xh_ref, g_ref, h_ref, hh_ref, dy_ref, cw_ref, cb_ref, wgx_ref, bgx_ref, wga_ref, bga_ref, lam_ref,
             dzx_ref, dzg_ref, dwgx_ref, dbgx_ref, dwga_ref, dbga_ref, dlam_ref, dcw_ref, dcb_ref,
             gcar_ref, acar_ref, xcar_ref):
        ti = pl.program_id(1)
        has_earlier = ti < nt - 1

        @pl.when(ti == 0)
        def _():
            for ref in (dwgx_ref, dbgx_ref, dwga_ref, dbga_ref, dlam_ref, dcw_ref, dcb_ref, gcar_ref, acar_ref, xcar_ref):
                ref[...] = jnp.zeros(ref.shape, F32)

        rows = lax.broadcasted_iota(jnp.int32, (tb, LANES), 0)
        x = x_ref[...]
        hal = jnp.where(has_earlier, xh_ref[...], 0.0)
        xs = _shifted_inputs(x, hal, tb)
        xc = cb_ref[...] + sum(xs[k] * cw_ref[k:k + 1, :] for k in range(4))
        (a, _), vjp = jax.vjp(functools.partial(_lru_gates, DIFF), xc, wgx_ref[...], bgx_ref[...], wga_ref[...],
                              bga_ref[...], lam_ref[...])
        g, h, dyv = g_ref[...], h_ref[...], dy_ref[...]
        silu, dsilu = _silu_and_grad(g)
        dzg_ref[...] = (dyv * h * dsilu).astype(MXU_DTYPE)
        a_next = jnp.where(rows < tb - 1, pltpu.roll(a, tb - 1, 0), acar_ref[0:1, :])
        acum, g0 = _scan_tile(a_next, dyv * silu, True)
        gt = g0 + acum * gcar_ref[0:1, :]
        h_prev = jnp.where(rows >= 1, pltpu.roll(h, 1, 0), jnp.where(has_earlier, hh_ref[7:8, :], 0.0))
        dxc, dwgx, dbgx, dwga, dbga, dlam = vjp((gt * h_prev, gt))
        later = xcar_ref[...]
        gcar_ref[...] = gt[0:8, :]
        acar_ref[...] = a[0:8, :]
        xcar_ref[...] = dxc[0:8, :]
        dx = dxc * cw_ref[3:4, :]
        pad = jnp.zeros((tb - 8, LANES), F32)
        for d in (1, 2, 3):
            tail = jnp.concatenate([pad, pltpu.roll(later, 8 - d, 0)], axis=0)
            up = jnp.where(rows < tb - d, pltpu.roll(dxc, tb - d, 0), tail)
            dx = dx + up * cw_ref[3 - d:4 - d, :]
        dzx_ref[...] = dx.astype(MXU_DTYPE)
        for k in range(4):
            dcw_ref[k:k + 1, :] += jnp.sum(dxc * xs[k], axis=0, keepdims=True)
        dcb_ref[...] += jnp.sum(dxc, axis=0, keepdims=True)
        dwgx_ref[...] += dwgx
        dbgx_ref[...] += dbgx
        dwga_ref[...] += dwga
        dbga_ref[...] += dbga
        dlam_ref[...] += dlam

    sds = jax.ShapeDtypeStruct
    grid = (LRU_BLOCKS, nt)
    body, x_in, x_in_specs, x_out_specs, x_out_shapes, sems = _host(body, 13, 9, 3, comm, grid)
    return pl.pallas_call(
        body,
        name="lru_bwd" if comm is None else "lru_bwd_comm",
        grid=grid,
        in_specs=[blk(0), halo(0), blk(LRU_BLOCKS), blk(0), halo(0), blk(0), chan(4), chan(1), wblk, bblk, wblk, bblk, chan(1)]
        + x_in_specs,
        out_specs=[blk(0), blk(0), wblk, bblk, wblk, bblk, chan(1), chan(4), chan(1)] + x_out_specs,
        out_shape=[sds((t, 1024), MXU_DTYPE), sds((t, 1024), MXU_DTYPE), sds(wgx.shape, F32), sds(bgx.shape, F32), sds(wga.shape, F32),
                   sds(bga.shape, F32), sds((1, 1024), F32), sds((4, 1024), F32), sds((1, 1024), F32)] + x_out_shapes,
        scratch_shapes=[pltpu.VMEM((8, LANES), F32)] * 3 + sems,
        compiler_params=_cparams(("arbitrary", "arbitrary") if comm is not None else ("parallel", "arbitrary")),
    )(z, z, z, h, h, dy, conv_w, conv_b, wgx, bgx, wga, bga, lam, *x_in)


def _mla_prep_tile(ops, cq, ckv, krp, c, s1, s2, g_cq, g_ckv, wq, wk, wv, gq, gk):
    cqn = rms(cq, g_cq, 256)
    ckvn = rms(ckv, g_ckv, 128)
    v = ops.mm(ckvn, wv)
    qs, ks = [], []
    for hd in range(MLA_HEADS):
        q = rms(ops.mm(cqn, wq[hd]), gq, MLA_QK)
        k = rms(ops.mm(ckvn, wk[hd]) + krp, gk, MLA_QK)
        qs.append(rope(ops, q, c, s1, s2, 16) * (MLA_QK ** -0.5))
        ks.append(rope(ops, k, c, s1, s2, 16))
    return tuple(qs), tuple(ks), v


def _mla_prep_args(ins):
    z_cq, z_ckv, z_kr, tc, ts1, ts2, g_cq, g_ckv, wq, wk, wv, gq, gk = ins[:13]
    heads = lambda w: tuple(w[:, LANES * hd:LANES * (hd + 1)] for hd in range(MLA_HEADS))
    return (z_cq[...], z_ckv[...], z_kr[...], tc[...], ts1[...], ts2[...], g_cq[...], g_ckv[...], heads(wq), heads(wk),
            wv[...], gq[...], gk[...])


def _mla_prep_ins(z, tabs, w):
    return [_row(z, 256, 0), _row(z, 128, 2), _row(z, 128, 3), _row(tabs, 128, 0), _row(tabs, 128, 1), _row(tabs, 128, 2),
            _const(w["cq_norm_g"]), _const(w["ckv_norm_g"]), _const(w["wq"]), _const(w["wk"]), _const(w["wv"]),
            _const(w["gq"]), _const(w["gk"])]


def mla_prep_fwd(z, tabs, w):
    def fn(ins, outs, _):
        qs, ks, v = _mla_prep_tile(PLAIN, *_mla_prep_args(ins))
        for hd in range(MLA_HEADS):
            outs[0][:, LANES * hd:LANES * (hd + 1)] = qs[hd].astype(MXU_DTYPE)
            outs[1][:, LANES * hd:LANES * (hd + 1)] = ks[hd].astype(MXU_DTYPE)
        outs[2][...] = v.astype(MXU_DTYPE)
        outs[3][...] = v.T.astype(MXU_DTYPE)

    return rowwise("mla_prep_fwd", fn, z.shape[0], ins=_mla_prep_ins(z, tabs, w),
                   outs=[(1024, 1024, MXU_DTYPE, False), (1024, 1024, MXU_DTYPE, False), (512, 512, MXU_DTYPE, False)],
                   touts=[(512, 512, MXU_DTYPE, False)])


def mla_prep_bwd(z, tabs, w, dq, dk, dv):
    def fn(ins, outs, accs):
        args = _mla_prep_args(ins)
        _, vjp = jax.vjp(functools.partial(_mla_prep_tile, DIFF), *args)
        heads = lambda ref: tuple(ref[:, LANES * hd:LANES * (hd + 1)] for hd in range(MLA_HEADS))
        dcq, dckv, dkr, _, _, _, dg_cq, dg_ckv, dwq, dwk, dwv, dgq, dgk = vjp((heads(ins[13]), heads(ins[14]), ins[15][...]))
        lane = lax.broadcasted_iota(jnp.int32, (1, LANES), 1)
        outs[0][:, 0:256] = dcq.astype(MXU_DTYPE)
        outs[0][:, 256:384] = dckv.astype(MXU_DTYPE)
        outs[0][:, 384:512] = jnp.where((lane >= MLA_NOPE) & (lane < MLA_QK), dkr, 0.0).astype(MXU_DTYPE)
        accs[0][...] += dg_cq
        accs[1][...] += dg_ckv
        for hd in range(MLA_HEADS):
            accs[2][:, LANES * hd:LANES * (hd + 1)] += dwq[hd]
            accs[3][:, LANES * hd:LANES * (hd + 1)] += dwk[hd]
        accs[4][...] += dwv
        accs[5][...] += dgq
        accs[6][...] += dgk

    return rowwise("mla_prep_bwd", fn, z.shape[0], ins=_mla_prep_ins(z, tabs, w) + [_row(dq), _row(dk), _row(dv)],
                   outs=[(512, 512, MXU_DTYPE, False)],
                   accs=[(1, 256, 256, False), (1, 128, 128, False), (256, 1024, 1024, False), (128, 1024, 1024, False),
                         (128, 512, 512, False), (1, 128, 128, False), (1, 128, 128, False)])


def _head_masks():
    lane = lax.broadcasted_iota(jnp.int32, (1, LANES), 1)
    return (lane < DIL_HD, lane >= DIL_HD)


def _row_scalar(tile, mask):
    return jnp.max(jnp.where(mask, tile, -jnp.inf), axis=-1, keepdims=True)


def _causal_tiles(nq, by_key):
    pairs = [(i, j) for i in range(nq) for j in range(i + 1)]
    if by_key:
        pairs.sort(key=lambda ij: (ij[1], ij[0]))
    return (jnp.asarray([ij[0] for ij in pairs], jnp.int32), jnp.asarray([ij[1] for ij in pairs], jnp.int32))


def mla_attn_fwd(q, k, vt, z, *, tq=256, gather=None):
    t = q.shape[0]
    nq = t // tq
    it, jt = _causal_tiles(nq, False)
    hps, wq, wv = MLA_HPS, LANES * MLA_HPS, 64 * MLA_HPS
    ws, layer = gather if gather is not None else ([], 0)
    ng = len(ws)
    ngrp, nsteps = MLA_HEADS // hps, int(it.shape[0])

    def body(it_ref, jt_ref, q_ref, k_ref, vt_ref, g_ref, *rest):
        w_refs, rest = rest[:ng], rest[ng:]
        o_ref, y_ref, yt_ref, lse_ref = rest[:4]
        ga_refs, rest = rest[4:4 + ng], rest[4 + ng:]
        m_scr, l_scr, acc_scr = rest[:3]
        step = pl.program_id(1)
        i, j = it_ref[step], jt_ref[step]
        if ng:
            gather_start, gather_finish = _layer_gather(w_refs, ga_refs, rest[3], rest[4], layer)

            @pl.when((pl.program_id(0) == 0) & (step == 0))
            def _():
                gather_start()

        @pl.when(j == 0)
        def _():
            m_scr[...] = jnp.full(m_scr.shape, NEG, F32)
            l_scr[...] = jnp.zeros(l_scr.shape, F32)
            acc_scr[...] = jnp.zeros(acc_scr.shape, F32)

        def update(diagonal):
            heads = range(hps)
            lanes = [slice(LANES * hh, LANES * (hh + 1)) for hh in heads]
            rows = [slice(64 * hh, 64 * (hh + 1)) for hh in heads]
            sts = [_dot(k_ref[:, lanes[hh]], q_ref[:, lanes[hh]], NT) for hh in heads]
            m_prev = [m_scr[hh:hh + 1, :] for hh in heads]
            l_prev = [l_scr[hh:hh + 1, :] for hh in heads]
            acc_prev = [acc_scr[rows[hh], :] for hh in heads]
            m_new, l_new, acc_new = [], [], []
            for hh in heads:
                st = sts[hh]
                if diagonal:
                    key = lax.broadcasted_iota(jnp.int32, (tq, tq), 0)
                    qry = lax.broadcasted_iota(jnp.int32, (tq, tq), 1)
                    st = jnp.where(key <= qry, st, NEG)
                m_new.append(jnp.maximum(m_prev[hh], jnp.max(st, axis=0, keepdims=True)))
                alpha = jnp.exp(m_prev[hh] - m_new[hh])
                pt = jnp.exp(st - m_new[hh])
                l_new.append(alpha * l_prev[hh] + jnp.sum(pt, axis=0, keepdims=True))
                acc_new.append(alpha * acc_prev[hh] + _dot(vt_ref[rows[hh], :], pt, NN))
            for hh in heads:
                m_scr[hh:hh + 1, :] = m_new[hh]
                l_scr[hh:hh + 1, :] = l_new[hh]
                acc_scr[rows[hh], :] = acc_new[hh]

        @pl.when(j < i)
        def _():
            update(False)

        @pl.when(j == i)
        def _():
            update(True)
            lse_ref[...] = jnp.zeros(lse_ref.shape, F32)
            for hh in range(hps):
                rows = slice(64 * hh, 64 * (hh + 1))
                acc_scr[rows, :] = acc_scr[rows, :] / l_scr[hh:hh + 1, :]
                lse_ref[hh:hh + 1, :] = m_scr[hh:hh + 1, :] + jnp.log(l_scr[hh:hh + 1, :])
            o = acc_scr[...].T
            o_ref[...] = o
            y = o * _silu_and_grad(g_ref[...])[0]
            y_ref[...] = y.astype(MXU_DTYPE)
            yt_ref[...] = y.T.astype(MXU_DTYPE)

        if ng:
            @pl.when((pl.program_id(0) == ngrp - 1) & (step == nsteps - 1))
            def _():
                gather_finish()

    qo = lambda w, off=0: pl.BlockSpec((tq, w), lambda p, s, it_, jt_: (it_[s], p + off))
    sds = jax.ShapeDtypeStruct
    comm_scratch = [pltpu.SemaphoreType.DMA((6 * ng,)), pltpu.SemaphoreType.DMA((6 * ng,))] if ng else []
    return pl.pallas_call(
        body,
        name="mla_attn_fwd_gather" if ng else "mla_attn_fwd",
        grid_spec=pltpu.PrefetchScalarGridSpec(
            num_scalar_prefetch=2,
            grid=(ngrp, nsteps),
            in_specs=[qo(wq), pl.BlockSpec((tq, wq), lambda p, s, it_, jt_: (jt_[s], p)),
                      pl.BlockSpec((wv, tq), lambda p, s, it_, jt_: (p, jt_[s])), qo(wv, 512 // wv)] + [HBM_SPEC] * ng,
            out_specs=[qo(wv), qo(wv), pl.BlockSpec((wv, tq), lambda p, s, it_, jt_: (p, it_[s])),
                       pl.BlockSpec((None, 8, tq), lambda p, s, it_, jt_: (p, 0, it_[s]))] + [HBM_SPEC] * ng,
            scratch_shapes=[pltpu.VMEM((8, tq), F32), pltpu.VMEM((8, tq), F32), pltpu.VMEM((wv, tq), F32)] + comm_scratch,
        ),
        out_shape=[sds((t, 512), F32), sds((t, 512), MXU_DTYPE), sds((512, t), MXU_DTYPE), sds((ngrp, 8, t), F32)]
        + _gather_shapes(ws),
        compiler_params=_cparams(("arbitrary", "arbitrary") if ng else ("parallel", "arbitrary")),
    )(it, jt, q, k, vt, z, *ws)


def mla_attn_bwd(q, k, v, do, lse, dd, *, tq=256, chips=None):
    t = q.shape[0]
    nq = t // tq
    it, jt = _causal_tiles(nq, True)
    hps, wq, wv = MLA_HPS, LANES * MLA_HPS, 64 * MLA_HPS
    pcs, layer = chips if chips is not None else ([], 0)
    nc = len(pcs)
    ngrp, nsteps = MLA_HEADS // hps, int(it.shape[0])

    def body(it_ref, jt_ref, q_ref, k_ref, v_ref, do_ref, lse_ref, d_ref, *rest):
        pc_refs, rest = rest[:nc], rest[nc:]
        dq_ref, dk_ref, dv_ref = rest[:3]
        lb_refs, rest = rest[3:3 + nc], rest[3 + nc:]
        dk_scr, dv_scr = rest[:2]
        step = pl.program_id(1)
        i, j = it_ref[step], jt_ref[step]
        masks = _head_masks()
        if nc:
            chips_start, chips_finish = _chips_exchange(pc_refs, lb_refs, rest[2], rest[3], layer)

            @pl.when((pl.program_id(0) == 0) & (step == 0))
            def _():
                chips_start()

        @pl.when(step == 0)
        def _():
            dq_ref[...] = jnp.zeros(dq_ref.shape, F32)

        @pl.when(i == j)
        def _():
            dk_scr[...] = jnp.zeros(dk_scr.shape, F32)
            dv_scr[...] = jnp.zeros(dv_scr.shape, F32)

        def update(diagonal):
            qrows = pl.ds(pl.multiple_of(i * tq, tq), tq)
            heads = range(hps)
            lanes = [slice(LANES * hh, LANES * (hh + 1)) for hh in heads]
            pair = [slice(LANES * (hh // 2), LANES * (hh // 2 + 1)) for hh in heads]
            qh = [q_ref[:, lanes[hh]] for hh in heads]
            kh = [k_ref[:, lanes[hh]] for hh in heads]
            doh = []
            for hh in heads:
                dov = do_ref[:, pair[hh]]
                doh.append(jnp.where(masks[hh % 2], dov, jnp.zeros_like(dov)))
            sts = [_dot(kh[hh], qh[hh], NT) for hh in heads]
            dps = [_dot(v_ref[:, pair[hh]], doh[hh], NT) for hh in heads]
            lse = [lse_ref[hh:hh + 1, :] for hh in heads]
            ddv = [d_ref[hh:hh + 1, :] for hh in heads]
            dk_new = [dk_scr[:, lanes[hh]] for hh in heads]
            dq_new = [dq_ref[qrows, lanes[hh]] for hh in heads]
            dv_new = [dv_scr[:, pair[2 * pp]] for pp in range(hps // 2)]
            for hh in heads:
                st = sts[hh] - lse[hh]
                if diagonal:
                    key = lax.broadcasted_iota(jnp.int32, (tq, tq), 0)
                    qry = lax.broadcasted_iota(jnp.int32, (tq, tq), 1)
                    st = jnp.where(key <= qry, st, NEG)
                pt = jnp.exp(st)
                dst = pt * (dps[hh] - ddv[hh])
                dv_new[hh // 2] = dv_new[hh // 2] + _dot(pt, doh[hh], NN)
                dk_new[hh] = dk_new[hh] + _dot(dst, qh[hh], NN)
                dq_new[hh] = dq_new[hh] + _dot(dst, kh[hh], TN)
            for hh in heads:
                dk_scr[:, lanes[hh]] = dk_new[hh]
                dq_ref[qrows, lanes[hh]] = dq_new[hh]
            for pp in range(hps // 2):
                dv_scr[:, pair[2 * pp]] = dv_new[pp]

        @pl.when(j < i)
        def _():
            update(False)

        @pl.when(j == i)
        def _():
            update(True)

        @pl.when(i == nq - 1)
        def _():
            dk_ref[...] = dk_scr[...]
            dv_ref[...] = dv_scr[...]

        if nc:
            @pl.when((pl.program_id(0) == ngrp - 1) & (step == nsteps - 1))
            def _():
                chips_finish()

    qo = lambda w: pl.BlockSpec((tq, w), lambda p, s, it_, jt_: (it_[s], p))
    kv = lambda w: pl.BlockSpec((tq, w), lambda p, s, it_, jt_: (jt_[s], p))
    stat = pl.BlockSpec((None, 8, tq), lambda p, s, it_, jt_: (p, 0, it_[s]))
    sds = jax.ShapeDtypeStruct
    comm_scratch = [pltpu.SemaphoreType.DMA((3 * nc,)), pltpu.SemaphoreType.DMA((3 * nc,))] if nc else []
    return pl.pallas_call(
        body,
        name="mla_attn_bwd_chips" if nc else "mla_attn_bwd",
        grid_spec=pltpu.PrefetchScalarGridSpec(
            num_scalar_prefetch=2,
            grid=(ngrp, nsteps),
            in_specs=[qo(wq), kv(wq), kv(wv), qo(wv), stat, stat] + [HBM_SPEC] * nc,
            out_specs=[pl.BlockSpec((t, wq), lambda p, s, it_, jt_: (0, p)), kv(wq), kv(wv)] + [HBM_SPEC] * nc,
            scratch_shapes=[pltpu.VMEM((tq, wq), F32), pltpu.VMEM((tq, wv), F32)] + comm_scratch,
        ),
        out_shape=[sds((t, 1024), F32), sds((t, 1024), F32), sds((t, 512), F32)] + _chips_shapes(pcs),
        compiler_params=_cparams(("arbitrary", "arbitrary") if nc else ("parallel", "arbitrary")),
    )(it, jt, q, k, v, do, lse, dd, *pcs)


def mla_gate_bwd(dy, o, z, *, tm=256):
    t = dy.shape[0]
    wv = 64 * MLA_HPS

    def body(dy_ref, o_ref, g_ref, do_ref, dzg_ref, dd_ref):
        dyv, ov = dy_ref[...], o_ref[...]
        silu, dsilu = _silu_and_grad(g_ref[...])
        do = dyv * silu
        do_ref[...] = do.astype(MXU_DTYPE)
        dzg_ref[...] = (dyv * ov * dsilu).astype(MXU_DTYPE)
        prod = do * ov
        row = lax.broadcasted_iota(jnp.int32, (8, wv), 0)
        lane = lax.broadcasted_iota(jnp.int32, (8, wv), 1)
        pick = ((lane >= row * DIL_HD) & (lane < (row + 1) * DIL_HD)).astype(BF16)
        hi = prod.astype(BF16)
        r1 = prod - hi.astype(F32)
        mid = r1.astype(BF16)
        lo = (r1 - mid.astype(F32)).astype(BF16)
        dot = lambda u: lax.dot_general(pick, u, NT, preferred_element_type=F32)
        dd_ref[...] = dot(hi) + dot(mid) + dot(lo)

    blk = lambda off=0: pl.BlockSpec((tm, wv), lambda p, i: (i, p + off))
    sds = jax.ShapeDtypeStruct
    return pl.pallas_call(
        body,
        name="mla_gate_bwd",
        grid=(MLA_HEADS // MLA_HPS, t // tm),
        in_specs=[blk(), blk(), blk(512 // wv)],
        out_specs=[blk(), blk(), pl.BlockSpec((None, 8, tm), lambda p, i: (p, 0, i))],
        out_shape=[sds((t, 512), MXU_DTYPE), sds((t, 512), MXU_DTYPE), sds((MLA_HEADS // MLA_HPS, 8, t), F32)],
        compiler_params=_cparams(("parallel", "parallel")),
    )(dy, o, z)


SPAN = 2048
DIL_SCALE = DIL_HD ** -0.5


def _rm_src(u, window, dil):
    nn, r = divmod(u, dil)
    return pl.ds(nn * window + r, DIL_NK, stride=dil) if dil > 1 else pl.ds(u * DIL_NK, DIL_NK)


def _rm_dst(u):
    return pl.ds(u * DIL_NK, DIL_NK)


def _dil_prep_tile(ops, x, g2, c, s1, s2, scale):
    ms = ops.seg_sum(x * x) * (1.0 / DIL_HD)
    return rope(ops, x * lax.rsqrt(ms + EPS) * g2, c, s1, s2, 32) * scale


def _span_blk(base):
    return pl.BlockSpec((SPAN, LANES), lambda s, p: (s, base + p))


def _const_blk(shape):
    return pl.BlockSpec(shape, lambda s, p: (0,) * len(shape))


_DIL_TABLE_SPECS = [pl.BlockSpec((SPAN, LANES), (lambda s, p, blk=blk: (s, blk))) for blk in (3, 4, 5)]


def dil_prep_fwd(z, tabs, gq2, gk2, gi):
    window, dil = DIL_GROUPS[gi]
    t = z.shape[0]

    def body(q_ref, k_ref, v_ref, c_ref, s1_ref, s2_ref, gq_ref, gk_ref, qo_ref, ko_ref, vo_ref):
        for u in range(SPAN // DIL_NK):
            src, dst = _rm_src(u, window, dil), _rm_dst(u)
            c, s1, s2 = c_ref[src, :], s1_ref[src, :], s2_ref[src, :]
            qo_ref[dst, :] = _dil_prep_tile(PLAIN, q_ref[src, :], gq_ref[...], c, s1, s2, DIL_SCALE).astype(MXU_DTYPE)
            ko_ref[dst, :] = _dil_prep_tile(PLAIN, k_ref[src, :], gk_ref[...], c, s1, s2, 1.0).astype(MXU_DTYPE)
            vo_ref[dst, :] = v_ref[src, :].astype(MXU_DTYPE)

    return pl.pallas_call(
        body,
        name=f"dil_prep_fwd_{dil}",
        grid=(t // SPAN, 4),
        in_specs=[_span_blk(4 * gi), _span_blk(12 + 4 * gi), _span_blk(24 + 4 * gi)] + _DIL_TABLE_SPECS
        + [_const_blk((1, LANES)), _const_blk((1, LANES))],
        out_specs=[_span_blk(0)] * 3,
        out_shape=[jax.ShapeDtypeStruct((t, 512), MXU_DTYPE)] * 3,
        compiler_params=_cparams(("parallel", "parallel")),
    )(z, z, z, tabs, tabs, tabs, gq2, gk2)


def dil_prep_bwd(z, tabs, gq2, gk2, dq, dk, dv, gi):
    window, dil = DIL_GROUPS[gi]
    t = z.shape[0]

    def body(q_ref, k_ref, c_ref, s1_ref, s2_ref, gq_ref, gk_ref, dq_ref, dk_ref, dv_ref, dzq_ref, dzk_ref, dzv_ref, dgq_ref,
             dgk_ref, sq, sk, sv):
        @pl.when((pl.program_id(0) == 0) & (pl.program_id(1) == 0))
        def _():
            dgq_ref[...] = jnp.zeros((1, LANES), F32)
            dgk_ref[...] = jnp.zeros((1, LANES), F32)

        dgs = [jnp.zeros((1, LANES), F32), jnp.zeros((1, LANES), F32)]
        for u in range(SPAN // DIL_NK):
            src, dst = _rm_src(u, window, dil), _rm_dst(u)
            c, s1, s2 = c_ref[src, :], s1_ref[src, :], s2_ref[src, :]
            for idx, (x_ref, g_ref, ct_ref, scr, scale) in enumerate(((q_ref, gq_ref, dq_ref, sq, DIL_SCALE),
                                                                      (k_ref, gk_ref, dk_ref, sk, 1.0))):
                _, vjp = jax.vjp(lambda xv, gv, sc=scale: _dil_prep_tile(DIFF, xv, gv, c, s1, s2, sc), x_ref[src, :], g_ref[...])
                dx, dg = vjp(ct_ref[dst, :])
                scr[src, :] = dx
                dgs[idx] = dgs[idx] + dg
            sv[src, :] = dv_ref[dst, :]
        dgq_ref[...] += dgs[0] + pltpu.roll(dgs[0], DIL_HD, 1)
        dgk_ref[...] += dgs[1] + pltpu.roll(dgs[1], DIL_HD, 1)
        for c0 in range(0, SPAN, 256):
            rows = slice(c0, c0 + 256)
            dzq_ref[rows, :] = sq[rows, :].astype(MXU_DTYPE)
            dzk_ref[rows, :] = sk[rows, :].astype(MXU_DTYPE)
            dzv_ref[rows, :] = sv[rows, :].astype(MXU_DTYPE)

    sds = jax.ShapeDtypeStruct
    return pl.pallas_call(
        body,
        name=f"dil_prep_bwd_{dil}",
        grid=(t // SPAN, 4),
        in_specs=[_span_blk(4 * gi), _span_blk(12 + 4 * gi)] + _DIL_TABLE_SPECS
        + [_const_blk((1, LANES)), _const_blk((1, LANES)), _span_blk(0), _span_blk(0), _span_blk(0)],
        out_specs=[_span_blk(0)] * 3 + [_const_blk((1, LANES))] * 2,
        out_shape=[sds((t, 512), MXU_DTYPE)] * 3 + [sds((1, LANES), F32)] * 2,
        scratch_shapes=[pltpu.VMEM((SPAN, LANES), F32)] * 3,
        compiler_params=_cparams(("arbitrary", "arbitrary")),
    )(z, z, tabs, tabs, tabs, gq2, gk2, dq, dk, dv)


def _band_masks():
    qi = lax.broadcasted_iota(jnp.int32, (DIL_NK, DIL_NK), 0)
    ki = lax.broadcasted_iota(jnp.int32, (DIL_NK, DIL_NK), 1)
    return (ki >= qi), (ki <= qi)


def _pair_heads():
    return [(pair, hh) for pair in range(4) for hh in range(2)]


def _pair_lanes(pair):
    return slice(LANES * pair, LANES * (pair + 1))


def _zero_other_head(mask, x):
    return jnp.where(mask, x, jnp.zeros_like(x))


def dil_attn_fwd(name, q, k, v, dil):
    t = q.shape[0]

    def body(q_ref, kp_ref, kc_ref, vp_ref, vc_ref, o_ref, lse_ref):
        b = pl.program_id(0)
        mprev, mcur = _band_masks()
        mprev = mprev & (b >= dil)
        hm = _head_masks()
        heads = _pair_heads()
        qh = [_zero_other_head(hm[hh], q_ref[:, _pair_lanes(pair)]) for pair, hh in heads]
        sps = [_dot(qh[i], kp_ref[:, _pair_lanes(pair)], NT) for i, (pair, _) in enumerate(heads)]
        scs = [_dot(qh[i], kc_ref[:, _pair_lanes(pair)], NT) for i, (pair, _) in enumerate(heads)]
        o = [jnp.zeros((DIL_NK, LANES), F32) for _ in range(4)]
        lse = [jnp.zeros((DIL_NK, LANES), F32) for _ in range(4)]
        for i, (pair, hh) in enumerate(heads):
            sp, sc = jnp.where(mprev, sps[i], NEG), jnp.where(mcur, scs[i], NEG)
            m = jnp.maximum(jnp.max(sp, axis=1, keepdims=True), jnp.max(sc, axis=1, keepdims=True))
            ep, ec = jnp.exp(sp - m), jnp.exp(sc - m)
            den = jnp.sum(ep, axis=1, keepdims=True) + jnp.sum(ec, axis=1, keepdims=True)
            lanes = _pair_lanes(pair)
            oh = _dot(ep, _zero_other_head(hm[hh], vp_ref[:, lanes]), NN) + _dot(ec, _zero_other_head(hm[hh], vc_ref[:, lanes]), NN)
            o[pair] = o[pair] + oh / den
            lse[pair] = jnp.where(hm[hh], m + jnp.log(den), lse[pair])
        for pair in range(4):
            o_ref[:, _pair_lanes(pair)] = o[pair]
            lse_ref[:, _pair_lanes(pair)] = lse[pair]

    cur = pl.BlockSpec((DIL_NK, 512), lambda b: (b, 0))
    prev = pl.BlockSpec((DIL_NK, 512), lambda b: (jnp.maximum(b - dil, 0), 0))
    sds = jax.ShapeDtypeStruct
    return pl.pallas_call(
        body,
        name=name,
        grid=(t // DIL_NK,),
        in_specs=[cur, prev, cur, prev, cur],
        out_specs=[cur, cur],
        out_shape=[sds((t, 512), F32), sds((t, 512), F32)],
        compiler_params=_cparams(("parallel",)),
    )(q, k, k, v, v)


def dil_attn_bwd(name, q, k, v, do, lse, dd, dil):
    t = q.shape[0]
    nblk = t // DIL_NK

    def body(q_ref, do_ref, l_ref, d_ref, kp_ref, kc_ref, vp_ref, vc_ref, dq_ref, dk_ref, dv_ref):
        b = pl.program_id(0)

        @pl.when(b == 0)
        def _():
            dk_ref[...] = jnp.zeros(dk_ref.shape, F32)
            dv_ref[...] = jnp.zeros(dv_ref.shape, F32)

        mprev, mcur = _band_masks()
        mprev = mprev & (b >= dil)
        rows_c = pl.ds(pl.multiple_of(b * DIL_NK, DIL_NK), DIL_NK)
        rows_p = pl.ds(pl.multiple_of(jnp.maximum(b - dil, 0) * DIL_NK, DIL_NK), DIL_NK)
        hm = _head_masks()
        heads = _pair_heads()
        ln_ = [_pair_lanes(pair) for pair, _ in heads]
        qh = [_zero_other_head(hm[hh], q_ref[:, ln_[i]]) for i, (_, hh) in enumerate(heads)]
        doh = [_zero_other_head(hm[hh], do_ref[:, ln_[i]]) for i, (_, hh) in enumerate(heads)]
        idx = range(len(heads))
        s_p = [_dot(qh[i], kp_ref[:, ln_[i]], NT) for i in idx]
        s_c = [_dot(qh[i], kc_ref[:, ln_[i]], NT) for i in idx]
        dp_p = [_dot(doh[i], vp_ref[:, ln_[i]], NT) for i in idx]
        dp_c = [_dot(doh[i], vc_ref[:, ln_[i]], NT) for i in idx]
        zero = lambda: [jnp.zeros((DIL_NK, LANES), F32) for _ in range(4)]
        dq, dk_p, dk_c, dv_p, dv_c = zero(), zero(), zero(), zero(), zero()
        for i, (pair, hh) in enumerate(heads):
            lse_h, d_h = _row_scalar(l_ref[:, ln_[i]], hm[hh]), _row_scalar(d_ref[:, ln_[i]], hm[hh])
            pp = jnp.exp(jnp.where(mprev, s_p[i] - lse_h, NEG))
            pc = jnp.exp(jnp.where(mcur, s_c[i] - lse_h, NEG))
            dsp, dsc = pp * (dp_p[i] - d_h), pc * (dp_c[i] - d_h)
            dq[pair] = (dq[pair] + _dot(dsp, _zero_other_head(hm[hh], kp_ref[:, ln_[i]]), NN)
                        + _dot(dsc, _zero_other_head(hm[hh], kc_ref[:, ln_[i]]), NN))
            dv_p[pair] = dv_p[pair] + _dot(pp, doh[i], TN)
            dv_c[pair] = dv_c[pair] + _dot(pc, doh[i], TN)
            dk_p[pair] = dk_p[pair] + _dot(dsp, qh[i], TN)
            dk_c[pair] = dk_c[pair] + _dot(dsc, qh[i], TN)
        for pair in range(4):
            lanes = _pair_lanes(pair)
            dq_ref[:, lanes] = dq[pair]
            dk_ref[rows_p, lanes] += dk_p[pair]
            dv_ref[rows_p, lanes] += dv_p[pair]
            dk_ref[rows_c, lanes] += dk_c[pair]
            dv_ref[rows_c, lanes] += dv_c[pair]

    cur = pl.BlockSpec((DIL_NK, 512), lambda b: (b, 0))
    prev = pl.BlockSpec((DIL_NK, 512), lambda b: (jnp.maximum(b - dil, 0), 0))
    whole = pl.BlockSpec((t, 512), lambda b: (0, 0))
    sds = jax.ShapeDtypeStruct
    return pl.pallas_call(
        body,
        name=name,
        grid=(nblk,),
        in_specs=[cur, cur, cur, cur, prev, cur, prev, cur],
        out_specs=[cur, whole, whole],
        out_shape=[sds((t, 512), F32)] * 3,
        compiler_params=_cparams(("arbitrary",)),
    )(q, do, lse, dd, k, k, v, v)


def dil_combine(os_, lses, z):
    t = z.shape[0]

    def body(o1_ref, l1_ref, o2_ref, l2_ref, o3_ref, l3_ref, g_ref, y_ref, o_ref, lse_ref, yt_ref, so2, sl2, so3, sl3):
        for (window, dil), o_in, l_in, so, sl in ((DIL_GROUPS[1], o2_ref, l2_ref, so2, sl2), (DIL_GROUPS[2], o3_ref, l3_ref, so3, sl3)):
            for u in range(SPAN // DIL_NK):
                src, dst = _rm_src(u, window, dil), _rm_dst(u)
                so[src, :] = o_in[dst, :]
                sl[src, :] = l_in[dst, :]
        for c0 in range(0, SPAN, 256):
            rows = slice(c0, c0 + 256)
            la, lb, lc = l1_ref[rows, :], sl2[rows, :], sl3[rows, :]
            mx = jnp.maximum(jnp.maximum(la, lb), lc)
            wa, wb, wc = jnp.exp(la - mx), jnp.exp(lb - mx), jnp.exp(lc - mx)
            tot = wa + wb + wc
            o = (wa * o1_ref[rows, :] + wb * so2[rows, :] + wc * so3[rows, :]) / tot
            y = o * _silu_and_grad(g_ref[rows, :])[0]
            y_ref[rows, :] = y.astype(MXU_DTYPE)
            o_ref[rows, :] = o
            lse_ref[rows, :] = mx + jnp.log(tot)
            yt_ref[:, rows] = y.T.astype(MXU_DTYPE)

    sds = jax.ShapeDtypeStruct
    return pl.pallas_call(
        body,
        name="dil_combine",
        grid=(t // SPAN, 4),
        in_specs=[_span_blk(0)] * 6 + [_span_blk(36)],
        out_specs=[_span_blk(0)] * 3 + [pl.BlockSpec((LANES, SPAN), lambda s, p: (p, s))],
        out_shape=[sds((t, 512), MXU_DTYPE), sds((t, 512), F32), sds((t, 512), F32), sds((512, t), MXU_DTYPE)],
        scratch_shapes=[pltpu.VMEM((SPAN, LANES), F32)] * 4,
        compiler_params=_cparams(("parallel", "parallel")),
    )(os_[0], lses[0], os_[1], lses[1], os_[2], lses[2], z)


def dil_gate_bwd(dy, o, z, lse):
    t = dy.shape[0]

    def body(dy_ref, o_ref, g_ref, lse_ref, dzg_ref, do1_ref, dd1_ref, do2_ref, dd2_ref, l2_ref, do3_ref, dd3_ref, l3_ref, do_scr):
        for c0 in range(0, SPAN, 256):
            rows = slice(c0, c0 + 256)
            dyv, ov = dy_ref[rows, :], o_ref[rows, :]
            silu, dsilu = _silu_and_grad(g_ref[rows, :])
            do = dyv * silu
            do_scr[rows, :] = do
            do1_ref[rows, :] = do.astype(MXU_DTYPE)
            dd1_ref[rows, :] = _seg_sum_impl(do * ov)
            dzg_ref[rows, :] = (dyv * ov * dsilu).astype(MXU_DTYPE)
        for (window, dil), do_o, dd_o, l_o in ((DIL_GROUPS[1], do2_ref, dd2_ref, l2_ref), (DIL_GROUPS[2], do3_ref, dd3_ref, l3_ref)):
            for u in range(SPAN // DIL_NK):
                src, dst = _rm_src(u, window, dil), _rm_dst(u)
                do_o[dst, :] = do_scr[src, :].astype(MXU_DTYPE)
                dd_o[dst, :] = dd1_ref[src, :]
                l_o[dst, :] = lse_ref[src, :]

    sds = jax.ShapeDtypeStruct
    f32, mxu = sds((t, 512), F32), sds((t, 512), MXU_DTYPE)
    return pl.pallas_call(
        body,
        name="dil_gate_bwd",
        grid=(t // SPAN, 4),
        in_specs=[_span_blk(0), _span_blk(0), _span_blk(36), _span_blk(0)],
        out_specs=[_span_blk(0)] * 9,
        out_shape=[mxu, mxu, f32, mxu, f32, f32, mxu, f32, f32],
        scratch_shapes=[pltpu.VMEM((SPAN, LANES), F32)],
        compiler_params=_cparams(("parallel", "parallel")),
    )(dy, o, z, lse)


def _merge_tile(p0, p1, p2, z0, z1, z2, b0, b1, b2):
    return _sigmoid(z0 + b0) * p0 + _sigmoid(z1 + b1) * p1 + _sigmoid(z2 + b2) * p2


def _merge_ins(ps, z, b):
    w = 256
    return ([_row(p, w, 0, True) for p in ps] + [_row(z, w, 4 * i, True) for i in range(3)]
            + [_const(b, w, 4 * i, True) for i in range(3)])


def merge_fwd(ps, z, b):
    def fn(ins, outs, _):
        merged = _merge_tile(*[r[...] for r in ins])
        outs[0][...] = merged.astype(MXU_DTYPE)
        outs[1][...] = merged.T.astype(MXU_DTYPE)

    return rowwise("merge_fwd", fn, z.shape[0], ncol=4, ins=_merge_ins(ps, z, b), outs=[(1024, 256, MXU_DTYPE, True)],
                   touts=[(1024, 256, MXU_DTYPE, True)])


def merge_bwd(dm, ps, z, b):
    def fn(ins, outs, accs):
        _, vjp = jax.vjp(_merge_tile, *[r[...] for r in ins[:9]])
        grads = vjp(ins[9][...])
        for i in range(3):
            outs[i][...] = grads[i].astype(MXU_DTYPE)
            outs[3 + i][...] = grads[3 + i].astype(MXU_DTYPE)
            accs[i][...] += grads[6 + i]

    return rowwise("merge_bwd", fn, z.shape[0], ncol=4, ins=_merge_ins(ps, z, b) + [_row(dm, 256, 0, True)],
                   outs=[(1024, 256, MXU_DTYPE, True)] * 6, accs=[(1, 1024, 256, True)] * 3)


EW_BLOCK_BYTES = 2**21


def _tile2d(r, c):
    if r * c * 4 <= EW_BLOCK_BYTES:
        return r, c
    for tr in (512, 256, 128, 64, 32):
        if r % tr == 0 and tr * c * 4 <= EW_BLOCK_BYTES:
            return tr, c
    for tc in (1024, 512, 256, 128):
        if c % tc == 0 and r * tc * 4 <= EW_BLOCK_BYTES:
            return r, tc
    raise ValueError((r, c))


def adamw(name, w, g, m, v):
    shape = w.shape
    c = shape[-1]
    r = w.size // c
    tr, tc = _tile2d(r, c)
    assert tc == c
    c1, c2 = 1.0 - ADAM_B1 ** ADAM_STEP, 1.0 - ADAM_B2 ** ADAM_STEP

    def body(w_ref, g_ref, m_ref, v_ref, d_ref, mo_ref, vo_ref):
        gv = g_ref[...]
        mn = ADAM_B1 * m_ref[...] + (1.0 - ADAM_B1) * gv
        vn = ADAM_B2 * v_ref[...] + (1.0 - ADAM_B2) * (gv * gv)
        d_ref[...] = -ADAM_LR * ((mn / c1) / (jnp.sqrt(vn / c2) + ADAM_EPS) + ADAM_WD * w_ref[...])
        mo_ref[...] = mn
        vo_ref[...] = vn

    spec = pl.BlockSpec((tr, c), lambda i: (i, 0))
    outs = pl.pallas_call(
        body,
        name=name,
        grid=(r // tr,),
        in_specs=[spec] * 4,
        out_specs=[spec] * 3,
        out_shape=[jax.ShapeDtypeStruct((r, c), F32)] * 3,
        compiler_params=_cparams(("parallel",)),
    )(*[a.reshape(r, c) for a in (w, g, m, v)])
    return [o.reshape(shape) for o in outs]


def adamw_pair(name, w, reduced, received, m, v, c_idx):
    _, r, c = w.shape
    tr, tc = _tile2d(r, c)
    c1, c2 = 1.0 - ADAM_B1 ** ADAM_STEP, 1.0 - ADAM_B2 ** ADAM_STEP

    def body(c_ref, w_ref, r0_ref, x0_ref, r1_ref, x1_ref, m_ref, v_ref, g_ref, d_ref, mo_ref, vo_ref):
        mine = c_ref[0]
        gv = jnp.where(pl.program_id(0) == 0, jnp.where(mine == 0, r0_ref[...], x0_ref[...]),
                       jnp.where(mine == 1, r1_ref[...], x1_ref[...]))
        mn = ADAM_B1 * m_ref[...] + (1.0 - ADAM_B1) * gv
        vn = ADAM_B2 * v_ref[...] + (1.0 - ADAM_B2) * (gv * gv)
        g_ref[...] = gv
        d_ref[...] = -ADAM_LR * ((mn / c1) / (jnp.sqrt(vn / c2) + ADAM_EPS) + ADAM_WD * w_ref[...])
        mo_ref[...] = mn
        vo_ref[...] = vn

    full = pl.BlockSpec((None, tr, tc), lambda h, i, j, cr: (h, i, j))
    half = pl.BlockSpec((tr, tc), lambda h, i, j, cr: (i, j))
    return pl.pallas_call(
        body,
        name=name,
        grid_spec=pltpu.PrefetchScalarGridSpec(
            num_scalar_prefetch=1,
            grid=(2, r // tr, c // tc),
            in_specs=[full, half, half, half, half, full, full],
            out_specs=[full] * 4,
        ),
        out_shape=[jax.ShapeDtypeStruct((2, r, c), F32)] * 4,
        compiler_params=_cparams(("parallel", "parallel", "parallel")),
    )(c_idx, w, reduced[0], received[0], reduced[1], received[1], m, v)


def sum_pair(name, g, la, out_dtype):
    _, r, c = g.shape
    tr, tc = _tile2d(r, c)

    def body(g_ref, la_ref, o_ref):
        o_ref[...] = (g_ref[...] + la_ref[...]).astype(o_ref.dtype)

    spec = pl.BlockSpec((None, tr, tc), lambda s, i, j: (s, i, j))
    return pl.pallas_call(
        body,
        name=name,
        grid=(4, r // tr, c // tc),
        in_specs=[spec, spec],
        out_specs=spec,
        out_shape=jax.ShapeDtypeStruct((4, r, c), out_dtype),
        compiler_params=_cparams(("parallel", "parallel", "parallel")),
    )(g, la)


def sum_chips(name, g, la, lb, s_idx):
    _, r, c = g.shape
    tr, tc = _tile2d(r, c)

    def body(s_ref, g_ref, la_ref, l0_ref, l1_ref, l2_ref, o_ref):
        own = g_ref[...] + la_ref[...]
        o_ref[...] = ((own + l0_ref[...].astype(F32)) + l1_ref[...].astype(F32)) + l2_ref[...].astype(F32)

    own_spec = pl.BlockSpec((None, tr, tc), lambda i, j, sr: (sr[0], i, j))
    lspec = lambda k: pl.BlockSpec((None, tr, tc), lambda i, j, sr: (k, i, j))
    return pl.pallas_call(
        body,
        name=name,
        grid_spec=pltpu.PrefetchScalarGridSpec(
            num_scalar_prefetch=1,
            grid=(r // tr, c // tc),
            in_specs=[own_spec, own_spec, lspec(0), lspec(1), lspec(2)],
            out_specs=pl.BlockSpec((tr, tc), lambda i, j, sr: (i, j)),
        ),
        out_shape=jax.ShapeDtypeStruct((r, c), F32),
        compiler_params=_cparams(("parallel", "parallel")),
    )(s_idx, g, la, lb, lb, lb)


def _place():
    x, y, c = lax.axis_index("x"), lax.axis_index("y"), lax.axis_index("c")
    chips = [(1 - x, y), (x, 1 - y), (1 - x, 1 - y)]
    return x, y, c, chips


HBM_SPEC = pl.BlockSpec(memory_space=pltpu.HBM)


def _comm_call(name, body, ins, out_shapes, n_sem, n_local):
    return pl.pallas_call(
        body,
        name=name,
        in_specs=[HBM_SPEC] * len(ins),
        out_specs=[HBM_SPEC] * len(out_shapes),
        out_shape=out_shapes,
        scratch_shapes=[pltpu.SemaphoreType.DMA((n_sem,)), pltpu.SemaphoreType.DMA((n_sem,)),
                        pltpu.SemaphoreType.DMA((max(n_local, 1),))],
    )(*ins)


def _layer_gather(ins, outs, send, recv, layer):
    n = len(ins)
    x, y, c, chips = _place()
    s = 2 * x + y
    sib = (x, y, 1 - c)
    active = c == layer

    def rc(a, k, src, dst, dev):
        return pltpu.make_async_remote_copy(src_ref=src, dst_ref=dst, send_sem=send.at[6 * a + k], recv_sem=recv.at[6 * a + k],
                                            device_id=dev, device_id_type=MESH)

    def first_hop():
        return [rc(a, j, ins[a], outs[a].at[s], (*chip, c)) for j, chip in enumerate(chips) for a in range(n)]

    def start():
        @pl.when(active)
        def _():
            for cp in first_hop():
                cp.start()

    def finish():
        @pl.when(active)
        def _():
            forwards = []
            for j, (cx, cy) in enumerate(chips):
                for a in range(n):
                    landed = outs[a].at[2 * cx + cy]
                    rc(a, j, landed, landed, sib).wait_recv()
                    forwards.append(rc(a, 3 + j, landed, landed, sib))
                    forwards[-1].start()
            for cp in first_hop() + forwards:
                cp.wait_send()

        @pl.when(jnp.logical_not(active))
        def _():
            for j, (cx, cy) in enumerate(chips):
                for a in range(n):
                    other = outs[a].at[2 * cx + cy]
                    rc(a, 3 + j, other, other, sib).wait_recv()

    return start, finish


def _gather_shapes(ws):
    return [jax.ShapeDtypeStruct((4,) + w.shape, w.dtype) for w in ws]


def gather_hook(ws, layer):
    return (ws, _gather_shapes(ws), 6 * len(ws), lambda i, o, s, r: _layer_gather(i, o, s, r, layer))


def allgather_layer(ws, layer):
    n = len(ws)

    def body(*refs):
        send, recv, _ = refs[2 * n:]
        start, finish = _layer_gather(refs[:n], refs[n:2 * n], send, recv, layer)
        start()
        finish()

    return _comm_call(f"allgather_layer{layer}", body, ws, _gather_shapes(ws), 6 * n, 0)


def exchange_sibling(gs, layer, tag=""):
    n = len(gs)

    def body(*refs):
        send, recv, _ = refs[2 * n:]
        start, finish = _sibling_exchange(refs[:n], refs[n:2 * n], send, recv, layer)
        start()
        finish()

    return _comm_call(f"exchange_sibling{layer}{tag}", body, gs, [jax.ShapeDtypeStruct(g.shape, g.dtype) for g in gs], n, 0)


def _sibling_exchange(ins, outs, send, recv, layer):
    x, y, c, _ = _place()

    def copies():
        return [pltpu.make_async_remote_copy(src_ref=ins[a], dst_ref=outs[a], send_sem=send.at[a], recv_sem=recv.at[a],
                                             device_id=(x, y, 1 - c), device_id_type=MESH) for a in range(len(ins))]

    def start():
        @pl.when(c != layer)
        def _():
            for cp in copies():
                cp.start()

    def finish():
        @pl.when(c != layer)
        def _():
            for cp in copies():
                cp.wait_send()

        @pl.when(c == layer)
        def _():
            for cp in copies():
                cp.wait_recv()

    return start, finish


def sibling_hook(gs, layer):
    return (gs, [jax.ShapeDtypeStruct(g.shape, g.dtype) for g in gs], len(gs),
            lambda i, o, s, r: _sibling_exchange(i, o, s, r, layer))


def chips_hook(ps, layer):
    return (ps, _chips_shapes(ps), 3 * len(ps), lambda i, o, s, r: _chips_exchange(i, o, s, r, layer))


def _chips_exchange(ins, outs, send, recv, layer):
    n = len(ins)
    _, _, c, chips = _place()

    def copies():
        return [pltpu.make_async_remote_copy(src_ref=ins[a].at[2 * cx + cy], dst_ref=outs[a].at[j], send_sem=send.at[3 * a + j],
                                             recv_sem=recv.at[3 * a + j], device_id=(cx, cy, c), device_id_type=MESH)
                for j, (cx, cy) in enumerate(chips) for a in range(n)]

    def start():
        @pl.when(c == layer)
        def _():
            for cp in copies():
                cp.start()

    def finish():
        @pl.when(c == layer)
        def _():
            for cp in copies():
                cp.wait()

    return start, finish


def _chips_shapes(ps):
    return [jax.ShapeDtypeStruct((3,) + p.shape[1:], p.dtype) for p in ps]


def exchange_chips(ps, layer, tag=""):
    n = len(ps)

    def body(*refs):
        send, recv, _ = refs[2 * n:]
        start, finish = _chips_exchange(refs[:n], refs[n:2 * n], send, recv, layer)
        start()
        finish()

    return _comm_call(f"exchange_chips{layer}{tag}", body, ps, _chips_shapes(ps), 3 * n, 0)


def exchange_final(rs, layer, small=None):
    n = len(rs)
    ins_all = list(rs) + ([small] if small is not None else [])

    def body(*refs):
        ni = len(ins_all)
        ins, outs = refs[:ni], refs[ni:2 * ni]
        send, recv, lsem = refs[2 * ni:]
        x, y, c, _ = _place()
        s = 2 * x + y
        to_sib = [pltpu.make_async_remote_copy(src_ref=ins[a], dst_ref=outs[a], send_sem=send.at[a], recv_sem=recv.at[a],
                                               device_id=(x, y, 1 - c), device_id_type=MESH) for a in range(n)]
        rel = [(fx, fy) for fx in (0, 1) for fy in (0, 1)]

        def piece(fx, fy, cc, dst_slot):
            k = n + 2 * (2 * fx + fy) + cc
            return pltpu.make_async_remote_copy(src_ref=ins[n], dst_ref=outs[n].at[dst_slot], send_sem=send.at[k], recv_sem=recv.at[k],
                                                device_id=(x ^ fx, y ^ fy, cc), device_id_type=MESH)

        @pl.when(c == layer)
        def _():
            for cp in to_sib:
                cp.start()
            if small is not None:
                own = pltpu.make_async_copy(ins[n], outs[n].at[s], lsem.at[0])
                own.start()
                sends = [piece(fx, fy, cc, s) for fx, fy in rel for cc in (0, 1) if (fx, fy) != (0, 0) or cc != layer]
                for cp in sends:
                    cp.start()
                for fx, fy in rel[1:]:
                    piece(fx, fy, layer, 2 * (x ^ fx) + (y ^ fy)).wait_recv()
                for cp in sends:
                    cp.wait_send()
                own.wait()
            for cp in to_sib:
                cp.wait_send()

        @pl.when(c != layer)
        def _():
            for cp in to_sib:
                cp.wait_recv()
            if small is not None:
                for fx, fy in rel:
                    piece(fx, fy, 1 - layer, 2 * (x ^ fx) + (y ^ fy)).wait_recv()

    out_shapes = [jax.ShapeDtypeStruct(r.shape, r.dtype) for r in rs]
    if small is not None:
        out_shapes.append(jax.ShapeDtypeStruct((4,) + small.shape, small.dtype))
    res = _comm_call(f"exchange_final{layer}", body, ins_all, out_shapes, n + 8, 1)
    return (res[:n], res[n]) if small is not None else (res[:n], None)


def _pad_in_cols(w):
    z = lambda n: jnp.zeros(w.shape[:-1] + (n,), w.dtype)
    return jnp.concatenate([w[..., :KR_OFF], z(64), w[..., KR_OFF:KR_OFF + MLA_ROPE], z(32), w[..., KR_OFF + MLA_ROPE:]], axis=-1)


def _unpad_in_cols(g):
    return jnp.concatenate([g[..., :KR_OFF], g[..., KR_OFF + 64:KR_OFF + 96], g[..., KR_OFF + 128:]], axis=-1)


def _pad_heads(w, real):
    k = w.shape[0]
    return jnp.pad(w.reshape(k, MLA_HEADS, real), ((0, 0), (0, 0), (0, LANES - real))).reshape(k, MLA_HEADS * LANES)


def _pad_gain(g, real):
    return jnp.pad(g.reshape(1, real), ((0, 0), (0, LANES - real)))


def layer_weights(full):
    w_ukv = full["w_ukv"].reshape(128, MLA_HEADS, 2, 64)
    two = lambda g: jnp.concatenate([g, g]).reshape(1, LANES)
    return dict(
        norm_g=full["norm_g"].reshape(1, -1), w_in=full["w_in"], conv_w=full["conv_w"], conv_b=full["conv_b"].reshape(1, -1),
        wgx=full["w_gate_x"], bgx=full["b_gate_x"].reshape(LRU_BLOCKS, 1, LANES),
        wga=full["w_gate_a"], bga=full["b_gate_a"].reshape(LRU_BLOCKS, 1, LANES),
        lam=full["lru_lambda"].reshape(1, -1), w_lru_o=full["w_lru_o"],
        cq_norm_g=full["cq_norm_g"].reshape(1, -1), ckv_norm_g=full["ckv_norm_g"].reshape(1, -1),
        wq=_pad_heads(full["w_uq"], MLA_QK), wk=_pad_heads(w_ukv[:, :, 0].reshape(128, 512), 64),
        wv=w_ukv[:, :, 1].reshape(128, 512),
        gq=_pad_gain(full["mla_q_norm_g"], MLA_QK), gk=_pad_gain(full["mla_k_norm_g"], MLA_QK),
        w_mla_o=full["w_mla_o"], gq2=two(full["dil_q_norm_g"]), gk2=two(full["dil_k_norm_g"]),
        w_dil_o=full["w_dil_o"], b_merge=full["b_merge"].reshape(1, -1), w_out=full["w_out"],
    )


def layer_fwd(x, w, tabs, gather=None):
    h, ht = rmsnorm_fwd(x, w["norm_g"])
    z_lru = mm_nn("in_proj_lru", h, w["w_in"], n=2048, n_off=G_LRU)
    z_mla = mm_nn("in_proj_mla", h, w["w_in"], n=1024, n_off=G_MLA)
    z_dil = mm_nn("in_proj_dil", h, w["w_in"], n=5120, n_off=G_DIL)
    z_mrg = mm_nn("in_proj_mrg", h, w["w_in"], n=3072, n_off=G_MRG)
    rest_hook = None if gather is None else gather_hook(gather[0][1:], gather[1])
    hs, y_lru, yt_lru, *ga_rest = lru_fwd(z_lru, w["conv_w"], w["conv_b"], w["wgx"], w["bgx"], w["wga"], w["bga"], w["lam"],
                                          comm=rest_hook)
    qm, km, vm, vtm = mla_prep_fwd(z_mla, tabs, w)
    o_mla, y_mla, yt_mla, lse_mla, *ga_first = mla_attn_fwd(qm, km, vtm, z_mla,
                                                            gather=None if gather is None else (gather[0][:1], gather[1]))
    gathered = ga_first + ga_rest
    os_, lses, dil_rm = [], [], []
    for gi, (_, dil) in enumerate(DIL_GROUPS):
        qkv = dil_prep_fwd(z_dil, tabs, w["gq2"], w["gk2"], gi)
        o, lse = dil_attn_fwd(f"dil_attn_fwd_{dil}", *qkv, dil)
        os_.append(o)
        lses.append(lse)
        dil_rm.append(qkv)
    y_dil, o_dil, lse_dil, yt_dil = dil_combine(os_, lses, z_dil)
    ps = [mm_nn("proj_lru", y_lru, w["w_lru_o"]), mm_nn("proj_mla", y_mla, w["w_mla_o"]), mm_nn("proj_dil", y_dil, w["w_dil_o"])]
    merged, merged_t = merge_fwd(ps, z_mrg, w["b_merge"])
    out = mm_nn("out_proj", merged, w["w_out"], add=x)
    res = dict(x=x, ht=ht, z_lru=z_lru, z_mla=z_mla, z_dil=z_dil, z_mrg=z_mrg, hs=hs, yt_lru=yt_lru, qm=qm, km=km, vm=vm, o_mla=o_mla,
               yt_mla=yt_mla, lse_mla=lse_mla, dil_rm=dil_rm, yt_dil=yt_dil, o_dil=o_dil, lse_dil=lse_dil, ps=ps, merged_t=merged_t)
    return out, res, gathered


def layer_bwd(dout, r, w, tabs, prev=None, own=None):
    g = {}
    dmerged = mm_nt("out_proj_dx", dout, w["w_out"])
    g["w_out"] = mm_nn("out_proj_dw", r["merged_t"], dout, tm=1024, tn=512, tk=2048)
    dp0, dp1, dp2, dzm0, dzm1, dzm2, db0, db1, db2 = merge_bwd(dmerged, r["ps"], r["z_mrg"], w["b_merge"])
    g["b_merge"] = jnp.concatenate([db0, db1, db2], axis=1).reshape(-1)
    dy_lru = mm_nt("proj_lru_dx", dp0, w["w_lru_o"])
    dy_mla = mm_nt("proj_mla_dx", dp1, w["w_mla_o"])
    dy_dil = mm_nt("proj_dil_dx", dp2, w["w_dil_o"])
    g["w_lru_o"] = mm_nn("proj_lru_dw", r["yt_lru"], dp0, **DW_TILES)
    g["w_mla_o"] = mm_nn("proj_mla_dw", r["yt_mla"], dp1, **DW_TILES)
    g["w_dil_o"] = mm_nn("proj_dil_dw", r["yt_dil"], dp2, **DW_TILES)
    dzx, dzg_lru, dwgx, dbgx, dwga, dbga, dlam, dcw, dcb, *la_prev = lru_bwd(
        r["z_lru"], r["hs"], dy_lru, w["conv_w"], w["conv_b"], w["wgx"], w["bgx"], w["wga"], w["bga"], w["lam"],
        comm=None if prev is None else sibling_hook(prev[0], prev[1]))
    chips = None if prev is None else (prev[2](la_prev), prev[1])
    g.update(w_gate_x=dwgx, b_gate_x=dbgx.reshape(LRU_BLOCKS, LANES), w_gate_a=dwga, b_gate_a=dbga.reshape(LRU_BLOCKS, LANES),
             lru_lambda=dlam.reshape(-1), conv_w=dcw, conv_b=dcb.reshape(-1))
    do_m, dzg_mla, dd_m = mla_gate_bwd(dy_mla, r["o_mla"], r["z_mla"])
    dq_m, dk_m, dv_m, *lb_prev = mla_attn_bwd(r["qm"], r["km"], r["vm"], do_m, r["lse_mla"], dd_m, chips=chips)
    dz_mla3, dg_cq, dg_ckv, dwq, dwk, dwv, dgq, dgk = mla_prep_bwd(r["z_mla"], tabs, w, dq_m, dk_m, dv_m)
    g.update(cq_norm_g=dg_cq.reshape(-1), ckv_norm_g=dg_ckv.reshape(-1), mla_q_norm_g=dgq[0, :MLA_QK], mla_k_norm_g=dgk[0, :MLA_QK])
    g["w_uq"] = dwq.reshape(256, MLA_HEADS, LANES)[:, :, :MLA_QK].reshape(256, MLA_HEADS * MLA_QK)
    g["w_ukv"] = jnp.concatenate([dwk.reshape(128, MLA_HEADS, LANES)[:, :, :64], dwv.reshape(128, MLA_HEADS, 64)], axis=2).reshape(128, 1024)
    dzg_dil, do1, dd1, do2, dd2, l2, do3, dd3, l3 = dil_gate_bwd(dy_dil, r["o_dil"], r["z_dil"], r["lse_dil"])
    stats = [(do1, r["lse_dil"], dd1), (do2, l2, dd2), (do3, l3, dd3)]
    dzq, dzk, dzv, dgq2, dgk2 = [], [], [], [], []
    for gi, (_, dil) in enumerate(DIL_GROUPS):
        dq, dk, dv = dil_attn_bwd(f"dil_attn_bwd_{dil}", *r["dil_rm"][gi], *stats[gi], dil)
        parts = dil_prep_bwd(r["z_dil"], tabs, w["gq2"], w["gk2"], dq, dk, dv, gi)
        for acc, part in zip((dzq, dzk, dzv, dgq2, dgk2), parts):
            acc.append(part)
    g.update(dil_q_norm_g=sum(dgq2)[0, :DIL_HD], dil_k_norm_g=sum(dgk2)[0, :DIL_HD])
    dz = jnp.concatenate([dzx, dzg_lru, dz_mla3, dzg_mla] + dzq + dzk + dzv + [dzg_dil, dzm0, dzm1, dzm2], axis=1)
    g["w_in"] = _unpad_in_cols(mm_nn("in_proj_dw", r["ht"], dz, **DW_TILES))
    own_hook = None if own is None else chips_hook(*own(g))
    out = mm_nt("in_proj_dx", dz, w["w_in"], tm=1024, tn=1024, tk=IN_PAD // 4, comm=own_hook)
    dh, lb_own = (out, []) if own_hook is None else (out[0], out[1:])
    dx, dng = rmsnorm_bwd(r["x"], dh, dout, w["norm_g"])
    g["norm_g"] = dng.reshape(-1)
    return dx, g, dict(la_prev=la_prev, lb_prev=lb_prev, lb_own=lb_own)


def local_step(x, positions, target, full0, full1=None, gather1=None, reduce=None):
    tabs = rope_tables(positions.reshape(-1, 1))
    ws, ress = [], []
    for l in range(2):
        if l == 0:
            ws.append(layer_weights(full0))
            x, res, gathered = layer_fwd(x, ws[0], tabs, gather=None if gather1 is None else (gather1[0], 1))
        else:
            ws.append(layer_weights(full1 if gather1 is None else gather1[1](gathered)))
            x, res, _ = layer_fwd(x, ws[1], tabs)
        ress.append(res)
    dy, loss = loss_head(x, target)
    dy, grads1, _ = layer_bwd(dy, ress[1], ws[1], tabs)
    prev1, own0 = (None, None) if reduce is None else (reduce[0](grads1), reduce[1])
    dy, grads0, landed = layer_bwd(dy, ress[0], ws[0], tabs, prev=prev1, own=own0)
    return loss, dy, [grads0, grads1], landed


WEIGHTS = ["norm_g", "w_in", "conv_w", "conv_b", "w_gate_x", "b_gate_x", "w_gate_a", "b_gate_a", "lru_lambda", "w_lru_o", "cq_norm_g",
           "ckv_norm_g", "w_uq", "w_ukv", "mla_q_norm_g", "mla_k_norm_g", "w_mla_o", "dil_q_norm_g", "dil_k_norm_g", "w_dil_o", "b_merge",
           "w_out"]
SHARDED = {"w_in": 2, "conv_w": 2, "w_lru_o": 1, "w_uq": 2, "w_ukv": 2, "w_mla_o": 2, "w_dil_o": 2, "w_out": 1}
REPLICATED = [n for n in WEIGHTS if n not in SHARDED]
SMALL_ROWS = 144


def kernel(x, positions, norm_g, w_in, conv_w, conv_b, w_gate_x, b_gate_x, w_gate_a, b_gate_a, lru_lambda, w_lru_o, cq_norm_g, ckv_norm_g, w_uq, w_ukv, mla_q_norm_g, mla_k_norm_g, w_mla_o, dil_q_norm_g, dil_k_norm_g, w_dil_o, b_merge, w_out, loss_target, m_norm_g, m_w_in, m_conv_w, m_conv_b, m_w_gate_x, m_b_gate_x, m_w_gate_a, m_b_gate_a, m_lru_lambda, m_w_lru_o, m_cq_norm_g, m_ckv_norm_g, m_w_uq, m_w_ukv, m_mla_q_norm_g, m_mla_k_norm_g, m_w_mla_o, m_dil_q_norm_g, m_dil_k_norm_g, m_w_dil_o, m_b_merge, m_w_out, v_norm_g, v_w_in, v_conv_w, v_conv_b, v_w_gate_x, v_b_gate_x, v_w_gate_a, v_b_gate_a, v_lru_lambda, v_w_lru_o, v_cq_norm_g, v_ckv_norm_g, v_w_uq, v_w_ukv, v_mla_q_norm_g, v_mla_k_norm_g, v_w_mla_o, v_dil_q_norm_g, v_dil_k_norm_g, v_w_dil_o, v_b_merge, v_w_out):
    args = locals()
    w = {n: args[n] for n in WEIGHTS}
    m = {n: args["m_" + n] for n in WEIGHTS}
    v = {n: args["v_" + n] for n in WEIGHTS}
    my_c = lax.axis_index("c").astype(jnp.int32)
    my_s = (2 * lax.axis_index("x") + lax.axis_index("y")).astype(jnp.int32)
    c_idx = my_c.reshape(1)
    s_idx = my_s.reshape(1)

    names = list(SHARDED)
    wire = [[w[n][l] if n == "conv_w" else w[n][l].astype(BF16) for n in names] for l in range(2)]

    def assemble(l, gathered):
        full = {n: w[n][l] for n in REPLICATED}
        for n, ga, own in zip(names, gathered, wire[l]):
            full[n] = jnp.concatenate([jnp.where(my_s == s, own, ga[s]) for s in range(4)], axis=SHARDED[n] - 1)
        full["w_in"] = _pad_in_cols(full["w_in"])
        return full

    full0 = assemble(0, allgather_layer(wire[0], 0))

    def shards_of(grads_l):
        gs = []
        for n in names:
            g_ = grads_l[n]
            if n == "w_in":
                gs.append(g_.reshape(D_MODEL, 4, IN_WIDTH // 4).transpose(1, 2, 0))
            else:
                parts = jnp.stack(jnp.split(g_, 4, axis=SHARDED[n] - 1))
                gs.append(parts.reshape(4, -1, parts.shape[-1]))
        return gs

    def pair_sums(l, gs, la, tags):
        dts = [BF16 if g_.size >= 2**19 and t != "small" else F32 for g_, t in zip(gs, tags)]
        return [sum_pair(f"sum_pair{l}_{t}", g_, l_, dt) for t, g_, l_, dt in zip(tags, gs, la, dts)]

    def chip_sums(l, gs, la, lb, tags):
        return [sum_chips(f"sum_chips{l}_{t}", g_, l_, b_, s_idx) for t, g_, l_, b_ in zip(tags, gs, la, lb)]

    stash = {}

    def prev1(grads1):
        stash["gs1"] = shards_of(grads1)
        return stash["gs1"], 1, lambda la: pair_sums(1, stash["gs1"], la, names)

    def own0(grads0):
        stash["gs0"] = shards_of(grads0)
        stash["la0"] = exchange_sibling(stash["gs0"], 0)
        return pair_sums(0, stash["gs0"], stash["la0"], names), 0

    loss, grad_x, grads, landed = local_step(x[0], positions[0], loss_target[0], full0,
                                             gather1=(wire[1], functools.partial(assemble, 1)), reduce=(prev1, own0))
    loss = lax.psum(loss, ("x", "y", "c"))
    reduced1 = chip_sums(1, stash["gs1"], landed["la_prev"], landed["lb_prev"], names)
    received1, _ = exchange_final(reduced1, 1)
    reduced0 = chip_sums(0, stash["gs0"], stash["la0"], landed["lb_own"], names)

    flat = jnp.concatenate([jnp.stack([grads[0][n], grads[1][n]]).reshape(-1) for n in REPLICATED])
    small = [jnp.pad(flat, (0, 4 * SMALL_ROWS * 1024 - flat.size)).reshape(4, SMALL_ROWS, 1024)]
    la_s = exchange_sibling(small, 0, tag="_small")
    lb_s = exchange_chips(pair_sums(0, small, la_s, ["small"]), 0, tag="_small")
    received0, small_all = exchange_final(reduced0, 0, small=chip_sums(0, small, la_s, lb_s, ["small"])[0])

    g_local, delta, new_m, new_v = {}, {}, {}, {}
    for i, n in enumerate(names):
        shp = w[n].shape
        if n == "w_in":
            as3 = lambda a: a.transpose(0, 2, 1)
            back = lambda o: o.transpose(0, 2, 1)
        else:
            as3 = lambda a, rc=reduced0[i].shape: a.reshape((2,) + rc)
            back = lambda o, shp=shp: o.reshape(shp)
        outs = adamw_pair(f"adamw_{n}", as3(w[n]), (reduced0[i], reduced1[i]), (received0[i], received1[i]), as3(m[n]), as3(v[n]),
                          c_idx)
        g_local[n], delta[n], new_m[n], new_v[n] = [back(o) for o in outs]
    flat = small_all.reshape(-1)
    off = 0
    for n in REPLICATED:
        g_local[n] = flat[off:off + w[n].size].reshape(w[n].shape)
        off += w[n].size
        delta[n], new_m[n], new_v[n] = adamw(f"adamw_{n}", w[n], g_local[n], m[n], v[n])
    return (loss, grad_x[None], *[g_local[n] for n in WEIGHTS], *[delta[n] for n in WEIGHTS], *[new_m[n] for n in WEIGHTS],
            *[new_v[n] for n in WEIGHTS])
```

```python
import functools

import jax
import jax.numpy as jnp
from jax import lax
from jax.experimental import pallas as pl
from jax.experimental.pallas import tpu as pltpu

F32 = jnp.float32
BF16 = jnp.bfloat16
MXU_DTYPE = jnp.bfloat16

D_MODEL = 1024
EPS = 1e-6
ROPE_THETA = 10000.0
LRU_BLOCKS = 8
LRU_C = 8.0
MLA_HEADS = 8
MLA_NOPE = 64
MLA_ROPE = 32
MLA_QK = 96
DIL_GROUPS = ((128, 1), (512, 4), (2048, 16))
DIL_HD = 64
DIL_NK = 128
MLA_HPS = 8
IN_WIDTH = 11168
ADAM_LR, ADAM_B1, ADAM_B2, ADAM_EPS, ADAM_WD, ADAM_STEP = 0.001, 0.9, 0.999, 1e-08, 0.01, 10

LANES = 128
G_LRU, G_MLA, G_DIL, G_MRG = 0, 2048, 3072, 8192
IN_PAD = 11264
KR_OFF = 2432

NN = (((1,), (0,)), ((), ()))
NT = (((1,), (1,)), ((), ()))
TN = (((0,), (0,)), ((), ()))
NEG = -1e30
MESH = pl.DeviceIdType.MESH
VMEM_LIMIT = 48 * 2**20


def _cparams(sem):
    return pltpu.CompilerParams(dimension_semantics=sem, vmem_limit_bytes=VMEM_LIMIT)


def _dot(a, b, dims):
    return lax.dot_general(a.astype(MXU_DTYPE), b.astype(MXU_DTYPE), dims, preferred_element_type=F32)


@jax.custom_vjp
def mm(a, w):
    return _dot(a, w, NN)


def _mm_fwd(a, w):
    return _dot(a, w, NN), (a, w)


def _mm_bwd(res, g):
    a, w = res
    return _dot(g, w, NT), _dot(a, g, TN)


mm.defvjp(_mm_fwd, _mm_bwd)


def _seg64_matrix():
    r = lax.broadcasted_iota(jnp.int32, (LANES, LANES), 0) < DIL_HD
    c = lax.broadcasted_iota(jnp.int32, (LANES, LANES), 1) < DIL_HD
    return (r == c).astype(BF16)


def _seg_sum_impl(x):
    b = _seg64_matrix()
    hi = x.astype(BF16)
    r1 = x - hi.astype(F32)
    mid = r1.astype(BF16)
    lo = (r1 - mid.astype(F32)).astype(BF16)
    dot = lambda u: lax.dot_general(u, b, NN, preferred_element_type=F32)
    return dot(hi) + dot(mid) + dot(lo)


@jax.custom_vjp
def seg_sum(x):
    return _seg_sum_impl(x)


seg_sum.defvjp(lambda x: (_seg_sum_impl(x), None), lambda _, g: (_seg_sum_impl(g),))


def _lroll_impl(x, s):
    return pltpu.roll(x, s % LANES, 1)


@functools.partial(jax.custom_vjp, nondiff_argnums=(1,))
def lroll(x, s):
    return _lroll_impl(x, s)


lroll.defvjp(lambda x, s: (_lroll_impl(x, s), None), lambda s, _, g: (_lroll_impl(g, -s),))


class _Ops:
    def __init__(self, diff):
        self.mm = mm if diff else (lambda a, w: _dot(a, w, NN))
        self.seg_sum = seg_sum if diff else _seg_sum_impl
        self.lroll = lroll if diff else _lroll_impl


PLAIN, DIFF = _Ops(False), _Ops(True)


def rms(x, g, n):
    ms = jnp.sum(x * x, axis=-1, keepdims=True) * (1.0 / n)
    return x * lax.rsqrt(ms + EPS) * g


def rope(ops, x, c, s1, s2, half):
    return x * c + ops.lroll(x, -half) * s1 + ops.lroll(x, half) * s2


def _sigmoid(x):
    return 1.0 / (1.0 + jnp.exp(-x))


def _silu_and_grad(g):
    sg = _sigmoid(g)
    return g * sg, sg * (1.0 + g * (1.0 - sg))


def _softplus(x):
    return jnp.maximum(x, 0.0) + jnp.log(1.0 + jnp.exp(-jnp.abs(x)))


def _expm1(y):
    series = y * (1.0 + y * (0.5 + y * (1.0 / 6.0 + y * (1.0 / 24.0 + y * (1.0 / 120.0)))))
    return jnp.where(y > -0.05, series, jnp.exp(jnp.minimum(y, -0.05)) - 1.0)


def _host(body, n_in, n_out, n_scr, hook, grid):
    if hook is None:
        return body, [], [], [], [], []
    ins, out_shapes, n_sem, make = hook
    ni, no = len(ins), len(out_shapes)
    hbm = pl.BlockSpec(memory_space=pltpu.HBM)

    def at(corner):
        cond = None
        for axis, size in enumerate(grid):
            here = pl.program_id(axis) == (size - 1 if corner else 0)
            cond = here if cond is None else cond & here
        return cond

    def hosted(*refs):
        a, refs = refs[:n_in], refs[n_in:]
        xi, refs = refs[:ni], refs[ni:]
        o, refs = refs[:n_out], refs[n_out:]
        xo, refs = refs[:no], refs[no:]
        scr, (send, recv) = refs[:n_scr], refs[n_scr:]
        start, finish = make(xi, xo, send, recv)
        pl.when(at(False))(start)
        body(*a, *o, *scr)
        pl.when(at(True))(finish)

    sems = [pltpu.SemaphoreType.DMA((n_sem,)), pltpu.SemaphoreType.DMA((n_sem,))]
    return hosted, list(ins), [hbm] * ni, [hbm] * no, list(out_shapes), sems


def _mm_call(name, a, b, *, mode, m, n, k, a_blk, b_blk, out_dtype, add, tm, tn, tk, comm=None, transpose_out=False):
    nk = k // tk
    assert m % tm == 0 and n % tn == 0 and k % tk == 0, (name, m, n, k, tm, tn, tk)
    dims = {"nn": NN, "nt": NT, "tn": TN}[mode]

    def body(*refs):
        if add is None:
            a_ref, b_ref, o_ref, *scr = refs
            add_ref = None
        else:
            a_ref, b_ref, add_ref, o_ref, *scr = refs
        part = _dot(a_ref[...], b_ref[...], dims)

        def finish(acc):
            if add_ref is not None:
                acc = acc + add_ref[...]
            o_ref[...] = (acc.T if transpose_out else acc).astype(o_ref.dtype)

        if nk == 1:
            finish(part)
        else:
            (acc_ref,) = scr
            kk = pl.program_id(2)

            @pl.when(kk == 0)
            def _():
                acc_ref[...] = part

            @pl.when(kk > 0)
            def _():
                acc_ref[...] += part

            @pl.when(kk == nk - 1)
            def _():
                finish(acc_ref[...])

    in_specs = [a_blk, b_blk]
    args = [a, b]
    if add is not None:
        in_specs.append(pl.BlockSpec((tm, tn), lambda j, i, kk: (i, j)))
        args.append(add)
    grid = (n // tn, m // tm, nk)
    scratch = [] if nk == 1 else [pltpu.VMEM((tm, tn), F32)]
    body, x_in, x_in_specs, x_out_specs, x_out_shapes, sems = _host(body, len(args), 1, len(scratch), comm, grid)
    res = pl.pallas_call(
        body,
        name=name,
        grid=grid,
        in_specs=in_specs + x_in_specs,
        out_specs=[pl.BlockSpec((tn, tm), lambda j, i, kk: (j, i)) if transpose_out
                   else pl.BlockSpec((tm, tn), lambda j, i, kk: (i, j))] + x_out_specs,
        out_shape=[jax.ShapeDtypeStruct((n, m) if transpose_out else (m, n), out_dtype)] + x_out_shapes,
        scratch_shapes=scratch + sems,
        compiler_params=_cparams(("arbitrary",) * 3 if comm is not None else ("parallel", "parallel", "arbitrary")),
    )(*args, *x_in)
    return res[0] if comm is None else res


def mm_nn(name, a, b, *, n=None, n_off=0, out_dtype=F32, add=None, tm=512, tn=1024, tk=1024, transpose_out=False):
    m, k = a.shape
    n = b.shape[1] if n is None else n
    tm, tn, tk = min(tm, m), min(tn, n), min(k, tk)
    ob = n_off // tn
    assert n_off % tn == 0
    return _mm_call(name, a, b, mode="nn", m=m, n=n, k=k, out_dtype=out_dtype, add=add, tm=tm, tn=tn, tk=tk,
                    transpose_out=transpose_out,
                    a_blk=pl.BlockSpec((tm, tk), lambda j, i, kk: (i, kk)),
                    b_blk=pl.BlockSpec((tk, tn), lambda j, i, kk: (kk, j + ob)))


def mm_nt(name, a, b, *, out_dtype=F32, tm=512, tn=1024, tk=1024, comm=None):
    m, k = a.shape
    n = b.shape[0]
    tn, tk = min(tn, n), min(k, tk)
    return _mm_call(name, a, b, mode="nt", m=m, n=n, k=k, out_dtype=out_dtype, add=None, tm=tm, tn=tn, tk=tk, comm=comm,
                    a_blk=pl.BlockSpec((tm, tk), lambda j, i, kk: (i, kk)),
                    b_blk=pl.BlockSpec((tn, tk), lambda j, i, kk: (j, kk)))


DW_TILES = dict(tm=1024, tn=512, tk=4096)


def rowwise(name, fn, t, *, tm=256, ncol=1, ins=(), outs=(), touts=(), accs=()):
    n_in, n_out, n_acc = len(ins), len(outs) + len(touts), len(accs)

    def zero_when(ref, cond):
        @pl.when(cond)
        def _():
            ref[...] = jnp.zeros(ref.shape, ref.dtype)

    def body(*refs):
        in_refs, out_refs, acc_refs = refs[:n_in], refs[n_in:n_in + n_out], refs[n_in + n_out:]
        j, i = pl.program_id(0), pl.program_id(1)
        for ref, (_, _, _, cd) in zip(acc_refs, accs):
            zero_when(ref, (i == 0) if cd else ((i == 0) & (j == 0)))
        fn(in_refs, out_refs, acc_refs)

    def in_spec(arr, w, base, cd, rd):
        rows = tm if rd else arr.shape[0]
        return pl.BlockSpec((rows, w), lambda j, i: (i if rd else 0, base + (j if cd else 0)))

    in_specs = [in_spec(*e) for e in ins]
    out_specs = [pl.BlockSpec((tm, w), (lambda j, i, cd=cd: (i, j if cd else 0))) for (_, w, _, cd) in outs]
    out_specs += [pl.BlockSpec((w, tm), (lambda j, i, cd=cd: (j if cd else 0, i))) for (_, w, _, cd) in touts]
    out_specs += [pl.BlockSpec((r, w), (lambda j, i, cd=cd: (0, j if cd else 0))) for (r, _, w, cd) in accs]
    out_shape = [jax.ShapeDtypeStruct((t, c), dt) for (c, _, dt, _) in outs]
    out_shape += [jax.ShapeDtypeStruct((r, t), dt) for (r, _, dt, _) in touts]
    out_shape += [jax.ShapeDtypeStruct((r, c), F32) for (r, c, _, _) in accs]
    res = pl.pallas_call(
        body,
        name=name,
        grid=(ncol, t // tm),
        in_specs=in_specs,
        out_specs=out_specs,
        out_shape=out_shape,
        compiler_params=_cparams(("arbitrary", "arbitrary") if accs else ("parallel", "parallel")),
    )(*[e[0] for e in ins])
    return res


def _row(arr, w=None, base=0, cd=False):
    return (arr, arr.shape[1] if w is None else w, base, cd, True)


def _const(arr, w=None, base=0, cd=False):
    return (arr, arr.shape[1] if w is None else w, base, cd, False)


def rope_tables(positions):
    t = positions.shape[0]

    def fn(ins, outs, _):
        pos = ins[0][...].astype(F32)
        lane = lax.broadcasted_iota(jnp.int32, (1, LANES), 1)
        log_theta = jnp.log(jnp.float32(ROPE_THETA))
        jm = lane - MLA_NOPE
        idx = jnp.clip(jnp.where(jm < 16, jm, jm - 16), 0, 15).astype(F32)
        ang = pos * jnp.exp(-(idx * (2.0 / MLA_ROPE)) * log_theta)
        cos, sin = jnp.cos(ang), jnp.sin(ang)
        in_rope = (lane >= MLA_NOPE) & (lane < MLA_QK)
        outs[0][:, 0:128] = jnp.where(lane < MLA_NOPE, 1.0, jnp.where(in_rope, cos, 0.0))
        outs[0][:, 128:256] = jnp.where(in_rope & (jm < 16), -sin, 0.0)
        outs[0][:, 256:384] = jnp.where(in_rope & (jm >= 16), sin, 0.0)
        jd = lane & (DIL_HD - 1)
        idx = (jd & 31).astype(F32)
        ang = pos * jnp.exp(-(idx * (2.0 / DIL_HD)) * log_theta)
        cos, sin = jnp.cos(ang), jnp.sin(ang)
        outs[0][:, 384:512] = cos
        outs[0][:, 512:640] = jnp.where(jd < 32, -sin, 0.0)
        outs[0][:, 640:768] = jnp.where(jd >= 32, sin, 0.0)

    return rowwise("rope_tables", fn, t, ins=[_row(positions)], outs=[(768, 768, F32, False)])[0]


def rmsnorm_fwd(x, g):
    def fn(ins, outs, _):
        h = rms(ins[0][...], ins[1][...], D_MODEL)
        outs[0][...] = h.astype(MXU_DTYPE)
        outs[1][...] = h.T.astype(MXU_DTYPE)

    return rowwise("rmsnorm_fwd", fn, x.shape[0], ins=[_row(x), _const(g)], outs=[(D_MODEL, D_MODEL, MXU_DTYPE, False)],
                   touts=[(D_MODEL, D_MODEL, MXU_DTYPE, False)])


def rmsnorm_bwd(x, dh, dres, g):
    def fn(ins, outs, accs):
        _, vjp = jax.vjp(lambda xv, gv: rms(xv, gv, D_MODEL), ins[0][...], ins[3][...])
        dx, dg = vjp(ins[1][...])
        outs[0][...] = ins[2][...] + dx
        accs[0][...] += dg

    return rowwise("rmsnorm_bwd", fn, x.shape[0], ins=[_row(x), _row(dh), _row(dres), _const(g)],
                   outs=[(D_MODEL, D_MODEL, F32, False)], accs=[(1, D_MODEL, D_MODEL, False)])


def loss_head(y, target):
    def fn(ins, outs, accs):
        err = ins[0][...] - ins[1][...]
        outs[0][...] = err * (1.0 / D_MODEL)
        accs[0][...] += jnp.sum(err * err, axis=0, keepdims=True)
        accs[1][...] = jnp.broadcast_to(jnp.sum(accs[0][...], keepdims=True), (1, LANES))

    dy, _, tot = rowwise("loss_head", fn, y.shape[0], ins=[_row(y), _row(target)], outs=[(D_MODEL, D_MODEL, F32, False)],
                         accs=[(1, D_MODEL, D_MODEL, False), (1, LANES, LANES, False)])
    return dy, tot[0, 0] * (0.5 / D_MODEL)


def _shift_rows(v, d, fill, reverse):
    tb = v.shape[0]
    if d % 8 == 0:
        pad = jnp.full((d, v.shape[1]), fill, v.dtype)
        return jnp.concatenate([v[d:], pad] if reverse else [pad, v[:tb - d]], axis=0)
    rows = lax.broadcasted_iota(jnp.int32, v.shape, 0)
    if not reverse:
        return jnp.where(rows >= d, pltpu.roll(v, d, 0), fill)
    return jnp.where(rows < tb - d, pltpu.roll(v, tb - d, 0), fill)


def _scan_tile(a, b, reverse):
    d = 1
    while d < a.shape[0]:
        b = b + a * _shift_rows(b, d, 0.0, reverse)
        a = a * _shift_rows(a, d, 1.0, reverse)
        d *= 2
    return a, b


def _lru_gates(ops, xc, wgx, bgx, wga, bga, lam):
    gx = _sigmoid(ops.mm(xc, wgx) + bgx)
    ga = _sigmoid(ops.mm(xc, wga) + bga)
    log_a = -LRU_C * ga * _softplus(-lam)
    a = jnp.exp(log_a)
    mult = jnp.sqrt(-_expm1(2.0 * log_a))
    return a, mult * (gx * xc)


def _shifted_inputs(x, halo, tb):
    rows = lax.broadcasted_iota(jnp.int32, x.shape, 0)
    pad = jnp.zeros((tb - 8, LANES), F32)
    out = []
    for d in (3, 2, 1):
        head = jnp.concatenate([pltpu.roll(halo, d, 0), pad], axis=0)
        out.append(jnp.where(rows >= d, pltpu.roll(x, d, 0), head))
    return out + [x]


def _lru_specs(nt, tb, reverse):
    hb = tb // 8
    tt = (lambda t: nt - 1 - t) if reverse else (lambda t: t)
    blk = lambda off: pl.BlockSpec((tb, LANES), lambda n, t: (tt(t), n + off))
    halo = lambda off: pl.BlockSpec((8, LANES), lambda n, t: (jnp.maximum(tt(t) * hb - 1, 0), n + off))
    chan = lambda r: pl.BlockSpec((r, LANES), lambda n, t: (0, n))
    wblk = pl.BlockSpec((None, LANES, LANES), lambda n, t: (n, 0, 0))
    bblk = pl.BlockSpec((None, 1, LANES), lambda n, t: (n, 0, 0))
    return blk, halo, chan, wblk, bblk


def lru_fwd(z, conv_w, conv_b, wgx, bgx, wga, bga, lam, *, tb=256, comm=None):
    t = z.shape[0]
    nt = t // tb
    blk, halo, chan, wblk, bblk = _lru_specs(nt, tb, False)

    def body(x_ref, xh_ref, g_ref, cw_ref, cb_ref, wgx_ref, bgx_ref, wga_ref, bga_ref, lam_ref, h_ref, y_ref, yt_ref, carry_ref):
        ti = pl.program_id(1)

        @pl.when(ti == 0)
        def _():
            carry_ref[...] = jnp.zeros((8, LANES), F32)

        x = x_ref[...]
        hal = jnp.where(ti > 0, xh_ref[...], 0.0)
        xs = _shifted_inputs(x, hal, tb)
        xc = cb_ref[...] + sum(xs[k] * cw_ref[k:k + 1, :] for k in range(4))
        a, b = _lru_gates(PLAIN, xc, wgx_ref[...], bgx_ref[...], wga_ref[...], bga_ref[...], lam_ref[...])
        acum, h0 = _scan_tile(a, b, False)
        h = h0 + acum * carry_ref[7:8, :]
        carry_ref[...] = h[tb - 8:tb, :]
        h_ref[...] = h
        y = h * _silu_and_grad(g_ref[...])[0]
        y_ref[...] = y.astype(MXU_DTYPE)
        yt_ref[...] = y.T.astype(MXU_DTYPE)

    grid = (LRU_BLOCKS, nt)
    body, x_in, x_in_specs, x_out_specs, x_out_shapes, sems = _host(body, 10, 3, 1, comm, grid)
    return pl.pallas_call(
        body,
        name="lru_fwd" if comm is None else "lru_fwd_comm",
        grid=grid,
        in_specs=[blk(0), halo(0), blk(LRU_BLOCKS), chan(4), chan(1), wblk, bblk, wblk, bblk, chan(1)] + x_in_specs,
        out_specs=[blk(0), blk(0), pl.BlockSpec((LANES, tb), lambda n, t_: (n, t_))] + x_out_specs,
        out_shape=[jax.ShapeDtypeStruct((t, 1024), F32), jax.ShapeDtypeStruct((t, 1024), MXU_DTYPE),
                   jax.ShapeDtypeStruct((1024, t), MXU_DTYPE)] + x_out_shapes,
        scratch_shapes=[pltpu.VMEM((8, LANES), F32)] + sems,
        compiler_params=_cparams(("arbitrary", "arbitrary") if comm is not None else ("parallel", "arbitrary")),
    )(z, z, z, conv_w, conv_b, wgx, bgx, wga, bga, lam, *x_in)


def lru_bwd(z, h, dy, conv_w, conv_b, wgx, bgx, wga, bga, lam, *, tb=256, comm=None):
    t = z.shape[0]
    nt = t // tb
    blk, halo, chan, wblk, bblk = _lru_specs(nt, tb, True)

    def body(x_ref, xh_ref, g_ref, h_ref, hh_ref, dy_ref, cw_ref, cb_ref, wgx_ref, bgx_ref, wga_ref, bga_ref, lam_ref,
             dzx_ref, dzg_ref, dwgx_ref, dbgx_ref, dwga_ref, dbga_ref, dlam_ref, dcw_ref, dcb_ref,
             gcar_ref, acar_ref, xcar_ref):
        ti = pl.program_id(1)
        has_earlier = ti < nt - 1

        @pl.when(ti == 0)
        def _():
            for ref in (dwgx_ref, dbgx_ref, dwga_ref, dbga_ref, dlam_ref, dcw_ref, dcb_ref, gcar_ref, acar_ref, xcar_ref):
                ref[...] = jnp.zeros(ref.shape, F32)

        rows = lax.broadcasted_iota(jnp.int32, (tb, LANES), 0)
        x = x_ref[...]
        hal = jnp.where(has_earlier, xh_ref[...], 0.0)
        xs = _shifted_inputs(x, hal, tb)
        xc = cb_ref[...] + sum(xs[k] * cw_ref[k:k + 1, :] for k in range(4))
        (a, _), vjp = jax.vjp(functools.partial(_lru_gates, DIFF), xc, wgx_ref[...], bgx_ref[...], wga_ref[...],
                              bga_ref[...], lam_ref[...])
        g, h, dyv = g_ref[...], h_ref[...], dy_ref[...]
        silu, dsilu = _silu_and_grad(g)
        dzg_ref[...] = (dyv * h * dsilu).astype(MXU_DTYPE)
        a_next = jnp.where(rows < tb - 1, pltpu.roll(a, tb - 1, 0), acar_ref[0:1, :])
        acum, g0 = _scan_tile(a_next, dyv * silu, True)
        gt = g0 + acum * gcar_ref[0:1, :]
        h_prev = jnp.where(rows >= 1, pltpu.roll(h, 1, 0), jnp.where(has_earlier, hh_ref[7:8, :], 0.0))
        dxc, dwgx, dbgx, dwga, dbga, dlam = vjp((gt * h_prev, gt))
        later = xcar_ref[...]
        gcar_ref[...] = gt[0:8, :]
        acar_ref[...] = a[0:8, :]
        xcar_ref[...] = dxc[0:8, :]
        dx = dxc * cw_ref[3:4, :]
        pad = jnp.zeros((tb - 8, LANES), F32)
        for d in (1, 2, 3):
            tail = jnp.concatenate([pad, pltpu.roll(later, 8 - d, 0)], axis=0)
            up = jnp.where(rows < tb - d, pltpu.roll(dxc, tb - d, 0), tail)
            dx = dx + up * cw_ref[3 - d:4 - d, :]
        dzx_ref[...] = dx.astype(MXU_DTYPE)
        for k in range(4):
            dcw_ref[k:k + 1, :] += jnp.sum(dxc * xs[k], axis=0, keepdims=True)
        dcb_ref[...] += jnp.sum(dxc, axis=0, keepdims=True)
        dwgx_ref[...] += dwgx
        dbgx_ref[...] += dbgx
        dwga_ref[...] += dwga
        dbga_ref[...] += dbga
        dlam_ref[...] += dlam

    sds = jax.ShapeDtypeStruct
    grid = (LRU_BLOCKS, nt)
    body, x_in, x_in_specs, x_out_specs, x_out_shapes, sems = _host(body, 13, 9, 3, comm, grid)
    return pl.pallas_call(
        body,
        name="lru_bwd" if comm is None else "lru_bwd_comm",
        grid=grid,
        in_specs=[blk(0), halo(0), blk(LRU_BLOCKS), blk(0), halo(0), blk(0), chan(4), chan(1), wblk, bblk, wblk, bblk, chan(1)]
        + x_in_specs,
        out_specs=[blk(0), blk(0), wblk, bblk, wblk, bblk, chan(1), chan(4), chan(1)] + x_out_specs,
        out_shape=[sds((t, 1024), MXU_DTYPE), sds((t, 1024), MXU_DTYPE), sds(wgx.shape, F32), sds(bgx.shape, F32), sds(wga.shape, F32),
                   sds(bga.shape, F32), sds((1, 1024), F32), sds((4, 1024), F32), sds((1, 1024), F32)] + x_out_shapes,
        scratch_shapes=[pltpu.VMEM((8, LANES), F32)] * 3 + sems,
        compiler_params=_cparams(("arbitrary", "arbitrary") if comm is not None else ("parallel", "arbitrary")),
    )(z, z, z, h, h, dy, conv_w, conv_b, wgx, bgx, wga, bga, lam, *x_in)


def _mla_prep_tile(ops, cq, ckv, krp, c, s1, s2, g_cq, g_ckv, wq, wk, wv, gq, gk):
    cqn = rms(cq, g_cq, 256)
    ckvn = rms(ckv, g_ckv, 128)
    v = ops.mm(ckvn, wv)
    qs, ks = [], []
    for hd in range(MLA_HEADS):
        q = rms(ops.mm(cqn, wq[hd]), gq, MLA_QK)
        k = rms(ops.mm(ckvn, wk[hd]) + krp, gk, MLA_QK)
        qs.append(rope(ops, q, c, s1, s2, 16) * (MLA_QK ** -0.5))
        ks.append(rope(ops, k, c, s1, s2, 16))
    return tuple(qs), tuple(ks), v


def _mla_prep_args(ins):
    z_cq, z_ckv, z_kr, tc, ts1, ts2, g_cq, g_ckv, wq, wk, wv, gq, gk = ins[:13]
    heads = lambda w: tuple(w[:, LANES * hd:LANES * (hd + 1)] for hd in range(MLA_HEADS))
    return (z_cq[...], z_ckv[...], z_kr[...], tc[...], ts1[...], ts2[...], g_cq[...], g_ckv[...], heads(wq), heads(wk),
            wv[...], gq[...], gk[...])


def _mla_prep_ins(z, tabs, w):
    return [_row(z, 256, 0), _row(z, 128, 2), _row(z, 128, 3), _row(tabs, 128, 0), _row(tabs, 128, 1), _row(tabs, 128, 2),
            _const(w["cq_norm_g"]), _const(w["ckv_norm_g"]), _const(w["wq"]), _const(w["wk"]), _const(w["wv"]),
            _const(w["gq"]), _const(w["gk"])]


def mla_prep_fwd(z, tabs, w):
    def fn(ins, outs, _):
        qs, ks, v = _mla_prep_tile(PLAIN, *_mla_prep_args(ins))
        for hd in range(MLA_HEADS):
            outs[0][:, LANES * hd:LANES * (hd + 1)] = qs[hd].astype(MXU_DTYPE)
            outs[1][:, LANES * hd:LANES * (hd + 1)] = ks[hd].astype(MXU_DTYPE)
        outs[2][...] = v.astype(MXU_DTYPE)
        outs[3][...] = v.T.astype(MXU_DTYPE)

    return rowwise("mla_prep_fwd", fn, z.shape[0], ins=_mla_prep_ins(z, tabs, w),
                   outs=[(1024, 1024, MXU_DTYPE, False), (1024, 1024, MXU_DTYPE, False), (512, 512, MXU_DTYPE, False)],
                   touts=[(512, 512, MXU_DTYPE, False)])


def mla_prep_bwd(z, tabs, w, dq, dk, dv):
    def fn(ins, outs, accs):
        args = _mla_prep_args(ins)
        _, vjp = jax.vjp(functools.partial(_mla_prep_tile, DIFF), *args)
        heads = lambda ref: tuple(ref[:, LANES * hd:LANES * (hd + 1)] for hd in range(MLA_HEADS))
        dcq, dckv, dkr, _, _, _, dg_cq, dg_ckv, dwq, dwk, dwv, dgq, dgk = vjp((heads(ins[13]), heads(ins[14]), ins[15][...]))
        lane = lax.broadcasted_iota(jnp.int32, (1, LANES), 1)
        outs[0][:, 0:256] = dcq.astype(MXU_DTYPE)
        outs[0][:, 256:384] = dckv.astype(MXU_DTYPE)
        outs[0][:, 384:512] = jnp.where((lane >= MLA_NOPE) & (lane < MLA_QK), dkr, 0.0).astype(MXU_DTYPE)
        accs[0][...] += dg_cq
        accs[1][...] += dg_ckv
        for hd in range(MLA_HEADS):
            accs[2][:, LANES * hd:LANES * (hd + 1)] += dwq[hd]
            accs[3][:, LANES * hd:LANES * (hd + 1)] += dwk[hd]
        accs[4][...] += dwv
        accs[5][...] += dgq
        accs[6][...] += dgk

    return rowwise("mla_prep_bwd", fn, z.shape[0], ins=_mla_prep_ins(z, tabs, w) + [_row(dq), _row(dk), _row(dv)],
                   outs=[(512, 512, MXU_DTYPE, False)],
                   accs=[(1, 256, 256, False), (1, 128, 128, False), (256, 1024, 1024, False), (128, 1024, 1024, False),
                         (128, 512, 512, False), (1, 128, 128, False), (1, 128, 128, False)])


def _head_masks():
    lane = lax.broadcasted_iota(jnp.int32, (1, LANES), 1)
    return (lane < DIL_HD, lane >= DIL_HD)


def _row_scalar(tile, mask):
    return jnp.max(jnp.where(mask, tile, -jnp.inf), axis=-1, keepdims=True)


def _causal_tiles(nq, by_key):
    pairs = [(i, j) for i in range(nq) for j in range(i + 1)]
    if by_key:
        pairs.sort(key=lambda ij: (ij[1], ij[0]))
    return (jnp.asarray([ij[0] for ij in pairs], jnp.int32), jnp.asarray([ij[1] for ij in pairs], jnp.int32))


def mla_attn_fwd(q, k, vt, z, *, tq=256, gather=None):
    t = q.shape[0]
    nq = t // tq
    it, jt = _causal_tiles(nq, False)
    hps, wq, wv = MLA_HPS, LANES * MLA_HPS, 64 * MLA_HPS
    ws, layer = gather if gather is not None else ([], 0)
    ng = len(ws)
    ngrp, nsteps = MLA_HEADS // hps, int(it.shape[0])

    def body(it_ref, jt_ref, q_ref, k_ref, vt_ref, g_ref, *rest):
        w_refs, rest = rest[:ng], rest[ng:]
        o_ref, y_ref, yt_ref, lse_ref = rest[:4]
        ga_refs, rest = rest[4:4 + ng], rest[4 + ng:]
        m_scr, l_scr, acc_scr = rest[:3]
        step = pl.program_id(1)
        i, j = it_ref[step], jt_ref[step]
        if ng:
            gather_start, gather_finish = _layer_gather(w_refs, ga_refs, rest[3], rest[4], layer)

            @pl.when((pl.program_id(0) == 0) & (step == 0))
            def _():
                gather_start()

        @pl.when(j == 0)
        def _():
            m_scr[...] = jnp.full(m_scr.shape, NEG, F32)
            l_scr[...] = jnp.zeros(l_scr.shape, F32)
            acc_scr[...] = jnp.zeros(acc_scr.shape, F32)

        def update(diagonal):
            heads = range(hps)
            lanes = [slice(LANES * hh, LANES * (hh + 1)) for hh in heads]
            rows = [slice(64 * hh, 64 * (hh + 1)) for hh in heads]
            sts = [_dot(k_ref[:, lanes[hh]], q_ref[:, lanes[hh]], NT) for hh in heads]
            m_prev = [m_scr[hh:hh + 1, :] for hh in heads]
            l_prev = [l_scr[hh:hh + 1, :] for hh in heads]
            acc_prev = [acc_scr[rows[hh], :] for hh in heads]
            m_new, l_new, acc_new = [], [], []
            for hh in heads:
                st = sts[hh]
                if diagonal:
                    key = lax.broadcasted_iota(jnp.int32, (tq, tq), 0)
                    qry = lax.broadcasted_iota(jnp.int32, (tq, tq), 1)
                    st = jnp.where(key <= qry, st, NEG)
                m_new.append(jnp.maximum(m_prev[hh], jnp.max(st, axis=0, keepdims=True)))
                alpha = jnp.exp(m_prev[hh] - m_new[hh])
                pt = jnp.exp(st - m_new[hh])
                l_new.append(alpha * l_prev[hh] + jnp.sum(pt, axis=0, keepdims=True))
                acc_new.append(alpha * acc_prev[hh] + _dot(vt_ref[rows[hh], :], pt, NN))
            for hh in heads:
                m_scr[hh:hh + 1, :] = m_new[hh]
                l_scr[hh:hh + 1, :] = l_new[hh]
                acc_scr[rows[hh], :] = acc_new[hh]

        @pl.when(j < i)
        def _():
            update(False)

        @pl.when(j == i)
        def _():
            update(True)
            lse_ref[...] = jnp.zeros(lse_ref.shape, F32)
            for hh in range(hps):
                rows = slice(64 * hh, 64 * (hh + 1))
                acc_scr[rows, :] = acc_scr[rows, :] / l_scr[hh:hh + 1, :]
                lse_ref[hh:hh + 1, :] = m_scr[hh:hh + 1, :] + jnp.log(l_scr[hh:hh + 1, :])
            o = acc_scr[...].T
            o_ref[...] = o
            y = o * _silu_and_grad(g_ref[...])[0]
            y_ref[...] = y.astype(MXU_DTYPE)
            yt_ref[...] = y.T.astype(MXU_DTYPE)

        if ng:
            @pl.when((pl.program_id(0) == ngrp - 1) & (step == nsteps - 1))
            def _():
                gather_finish()

    qo = lambda w, off=0: pl.BlockSpec((tq, w), lambda p, s, it_, jt_: (it_[s], p + off))
    sds = jax.ShapeDtypeStruct
    comm_scratch = [pltpu.SemaphoreType.DMA((6 * ng,)), pltpu.SemaphoreType.DMA((6 * ng,))] if ng else []
    return pl.pallas_call(
        body,
        name="mla_attn_fwd_gather" if ng else "mla_attn_fwd",
        grid_spec=pltpu.PrefetchScalarGridSpec(
            num_scalar_prefetch=2,
            grid=(ngrp, nsteps),
            in_specs=[qo(wq), pl.BlockSpec((tq, wq), lambda p, s, it_, jt_: (jt_[s], p)),
                      pl.BlockSpec((wv, tq), lambda p, s, it_, jt_: (p, jt_[s])), qo(wv, 512 // wv)] + [HBM_SPEC] * ng,
            out_specs=[qo(wv), qo(wv), pl.BlockSpec((wv, tq), lambda p, s, it_, jt_: (p, it_[s])),
                       pl.BlockSpec((None, 8, tq), lambda p, s, it_, jt_: (p, 0, it_[s]))] + [HBM_SPEC] * ng,
            scratch_shapes=[pltpu.VMEM((8, tq), F32), pltpu.VMEM((8, tq), F32), pltpu.VMEM((wv, tq), F32)] + comm_scratch,
        ),
        out_shape=[sds((t, 512), F32), sds((t, 512), MXU_DTYPE), sds((512, t), MXU_DTYPE), sds((ngrp, 8, t), F32)]
        + _gather_shapes(ws),
        compiler_params=_cparams(("arbitrary", "arbitrary") if ng else ("parallel", "arbitrary")),
    )(it, jt, q, k, vt, z, *ws)


def mla_attn_bwd(q, k, v, do, lse, dd, *, tq=256, chips=None):
    t = q.shape[0]
    nq = t // tq
    it, jt = _causal_tiles(nq, True)
    hps, wq, wv = MLA_HPS, LANES * MLA_HPS, 64 * MLA_HPS
    pcs, layer = chips if chips is not None else ([], 0)
    nc = len(pcs)
    ngrp, nsteps = MLA_HEADS // hps, int(it.shape[0])

    def body(it_ref, jt_ref, q_ref, k_ref, v_ref, do_ref, lse_ref, d_ref, *rest):
        pc_refs, rest = rest[:nc], rest[nc:]
        dq_ref, dk_ref, dv_ref = rest[:3]
        lb_refs, rest = rest[3:3 + nc], rest[3 + nc:]
        dk_scr, dv_scr = rest[:2]
        step = pl.program_id(1)
        i, j = it_ref[step], jt_ref[step]
        masks = _head_masks()
        if nc:
            chips_start, chips_finish = _chips_exchange(pc_refs, lb_refs, rest[2], rest[3], layer)

            @pl.when((pl.program_id(0) == 0) & (step == 0))
            def _():
                chips_start()

        @pl.when(step == 0)
        def _():
            dq_ref[...] = jnp.zeros(dq_ref.shape, F32)

        @pl.when(i == j)
        def _():
            dk_scr[...] = jnp.zeros(dk_scr.shape, F32)
            dv_scr[...] = jnp.zeros(dv_scr.shape, F32)

        def update(diagonal):
            qrows = pl.ds(pl.multiple_of(i * tq, tq), tq)
            heads = range(hps)
            lanes = [slice(LANES * hh, LANES * (hh + 1)) for hh in heads]
            pair = [slice(LANES * (hh // 2), LANES * (hh // 2 + 1)) for hh in heads]
            qh = [q_ref[:, lanes[hh]] for hh in heads]
            kh = [k_ref[:, lanes[hh]] for hh in heads]
            doh = []
            for hh in heads:
                dov = do_ref[:, pair[hh]]
                doh.append(jnp.where(masks[hh % 2], dov, jnp.zeros_like(dov)))
            sts = [_dot(kh[hh], qh[hh], NT) for hh in heads]
            dps = [_dot(v_ref[:, pair[hh]], doh[hh], NT) for hh in heads]
            lse = [lse_ref[hh:hh + 1, :] for hh in heads]
            ddv = [d_ref[hh:hh + 1, :] for hh in heads]
            dk_new = [dk_scr[:, lanes[hh]] for hh in heads]
            dq_new = [dq_ref[qrows, lanes[hh]] for hh in heads]
            dv_new = [dv_scr[:, pair[2 * pp]] for pp in range(hps // 2)]
            for hh in heads:
                st = sts[hh] - lse[hh]
                if diagonal:
                    key = lax.broadcasted_iota(jnp.int32, (tq, tq), 0)
                    qry = lax.broadcasted_iota(jnp.int32, (tq, tq), 1)
                    st = jnp.where(key <= qry, st, NEG)
                pt = jnp.exp(st)
                dst = pt * (dps[hh] - ddv[hh])
                dv_new[hh // 2] = dv_new[hh // 2] + _dot(pt, doh[hh], NN)
                dk_new[hh] = dk_new[hh] + _dot(dst, qh[hh], NN)
                dq_new[hh] = dq_new[hh] + _dot(dst, kh[hh], TN)
            for hh in heads:
                dk_scr[:, lanes[hh]] = dk_new[hh]
                dq_ref[qrows, lanes[hh]] = dq_new[hh]
            for pp in range(hps // 2):
                dv_scr[:, pair[2 * pp]] = dv_new[pp]

        @pl.when(j < i)
        def _():
            update(False)

        @pl.when(j == i)
        def _():
            update(True)

        @pl.when(i == nq - 1)
        def _():
            dk_ref[...] = dk_scr[...]
            dv_ref[...] = dv_scr[...]

        if nc:
            @pl.when((pl.program_id(0) == ngrp - 1) & (step == nsteps - 1))
            def _():
                chips_finish()

    qo = lambda w: pl.BlockSpec((tq, w), lambda p, s, it_, jt_: (it_[s], p))
    kv = lambda w: pl.BlockSpec((tq, w), lambda p, s, it_, jt_: (jt_[s], p))
    stat = pl.BlockSpec((None, 8, tq), lambda p, s, it_, jt_: (p, 0, it_[s]))
    sds = jax.ShapeDtypeStruct
    comm_scratch = [pltpu.SemaphoreType.DMA((3 * nc,)), pltpu.SemaphoreType.DMA((3 * nc,))] if nc else []
    return pl.pallas_call(
        body,
        name="mla_attn_bwd_chips" if nc else "mla_attn_bwd",
        grid_spec=pltpu.PrefetchScalarGridSpec(
            num_scalar_prefetch=2,
            grid=(ngrp, nsteps),
            in_specs=[qo(wq), kv(wq), kv(wv), qo(wv), stat, stat] + [HBM_SPEC] * nc,
            out_specs=[pl.BlockSpec((t, wq), lambda p, s, it_, jt_: (0, p)), kv(wq), kv(wv)] + [HBM_SPEC] * nc,
            scratch_shapes=[pltpu.VMEM((tq, wq), F32), pltpu.VMEM((tq, wv), F32)] + comm_scratch,
        ),
        out_shape=[sds((t, 1024), F32), sds((t, 1024), F32), sds((t, 512), F32)] + _chips_shapes(pcs),
        compiler_params=_cparams(("arbitrary", "arbitrary") if nc else ("parallel", "arbitrary")),
    )(it, jt, q, k, v, do, lse, dd, *pcs)


def mla_gate_bwd(dy, o, z, *, tm=256):
    t = dy.shape[0]
    wv = 64 * MLA_HPS

    def body(dy_ref, o_ref, g_ref, do_ref, dzg_ref, dd_ref):
        dyv, ov = dy_ref[...], o_ref[...]
        silu, dsilu = _silu_and_grad(g_ref[...])
        do = dyv * silu
        do_ref[...] = do.astype(MXU_DTYPE)
        dzg_ref[...] = (dyv * ov * dsilu).astype(MXU_DTYPE)
        prod = do * ov
        row = lax.broadcasted_iota(jnp.int32, (8, wv), 0)
        lane = lax.broadcasted_iota(jnp.int32, (8, wv), 1)
        pick = ((lane >= row * DIL_HD) & (lane < (row + 1) * DIL_HD)).astype(BF16)
        hi = prod.astype(BF16)
        r1 = prod - hi.astype(F32)
        mid = r1.astype(BF16)
        lo = (r1 - mid.astype(F32)).astype(BF16)
        dot = lambda u: lax.dot_general(pick, u, NT, preferred_element_type=F32)
        dd_ref[...] = dot(hi) + dot(mid) + dot(lo)

    blk = lambda off=0: pl.BlockSpec((tm, wv), lambda p, i: (i, p + off))
    sds = jax.ShapeDtypeStruct
    return pl.pallas_call(
        body,
        name="mla_gate_bwd",
        grid=(MLA_HEADS // MLA_HPS, t // tm),
        in_specs=[blk(), blk(), blk(512 // wv)],
        out_specs=[blk(), blk(), pl.BlockSpec((None, 8, tm), lambda p, i: (p, 0, i))],
        out_shape=[sds((t, 512), MXU_DTYPE), sds((t, 512), MXU_DTYPE), sds((MLA_HEADS // MLA_HPS, 8, t), F32)],
        compiler_params=_cparams(("parallel", "parallel")),
    )(dy, o, z)


SPAN = 2048
DIL_SCALE = DIL_HD ** -0.5


def _rm_src(u, window, dil):
    nn, r = divmod(u, dil)
    return pl.ds(nn * window + r, DIL_NK, stride=dil) if dil > 1 else pl.ds(u * DIL_NK, DIL_NK)


def _rm_dst(u):
    return pl.ds(u * DIL_NK, DIL_NK)


def _dil_prep_tile(ops, x, g2, c, s1, s2, scale):
    ms = ops.seg_sum(x * x) * (1.0 / DIL_HD)
    return rope(ops, x * lax.rsqrt(ms + EPS) * g2, c, s1, s2, 32) * scale


def _span_blk(base):
    return pl.BlockSpec((SPAN, LANES), lambda s, p: (s, base + p))


def _const_blk(shape):
    return pl.BlockSpec(shape, lambda s, p: (0,) * len(shape))


_DIL_TABLE_SPECS = [pl.BlockSpec((SPAN, LANES), (lambda s, p, blk=blk: (s, blk))) for blk in (3, 4, 5)]


def dil_prep_fwd(z, tabs, gq2, gk2, gi):
    window, dil = DIL_GROUPS[gi]
    t = z.shape[0]

    def body(q_ref, k_ref, v_ref, c_ref, s1_ref, s2_ref, gq_ref, gk_ref, qo_ref, ko_ref, vo_ref):
        for u in range(SPAN // DIL_NK):
            src, dst = _rm_src(u, window, dil), _rm_dst(u)
            c, s1, s2 = c_ref[src, :], s1_ref[src, :], s2_ref[src, :]
            qo_ref[dst, :] = _dil_prep_tile(PLAIN, q_ref[src, :], gq_ref[...], c, s1, s2, DIL_SCALE).astype(MXU_DTYPE)
            ko_ref[dst, :] = _dil_prep_tile(PLAIN, k_ref[src, :], gk_ref[...], c, s1, s2, 1.0).astype(MXU_DTYPE)
            vo_ref[dst, :] = v_ref[src, :].astype(MXU_DTYPE)

    return pl.pallas_call(
        body,
        name=f"dil_prep_fwd_{dil}",
        grid=(t // SPAN, 4),
        in_specs=[_span_blk(4 * gi), _span_blk(12 + 4 * gi), _span_blk(24 + 4 * gi)] + _DIL_TABLE_SPECS
        + [_const_blk((1, LANES)), _const_blk((1, LANES))],
        out_specs=[_span_blk(0)] * 3,
        out_shape=[jax.ShapeDtypeStruct((t, 512), MXU_DTYPE)] * 3,
        compiler_params=_cparams(("parallel", "parallel")),
    )(z, z, z, tabs, tabs, tabs, gq2, gk2)


def dil_prep_bwd(z, tabs, gq2, gk2, dq, dk, dv, gi):
    window, dil = DIL_GROUPS[gi]
    t = z.shape[0]

    def body(q_ref, k_ref, c_ref, s1_ref, s2_ref, gq_ref, gk_ref, dq_ref, dk_ref, dv_ref, dzq_ref, dzk_ref, dzv_ref, dgq_ref,
             dgk_ref, sq, sk, sv):
        @pl.when((pl.program_id(0) == 0) & (pl.program_id(1) == 0))
        def _():
            dgq_ref[...] = jnp.zeros((1, LANES), F32)
            dgk_ref[...] = jnp.zeros((1, LANES), F32)

        dgs = [jnp.zeros((1, LANES), F32), jnp.zeros((1, LANES), F32)]
        for u in range(SPAN // DIL_NK):
            src, dst = _rm_src(u, window, dil), _rm_dst(u)
            c, s1, s2 = c_ref[src, :], s1_ref[src, :], s2_ref[src, :]
            for idx, (x_ref, g_ref, ct_ref, scr, scale) in enumerate(((q_ref, gq_ref, dq_ref, sq, DIL_SCALE),
                                                                      (k_ref, gk_ref, dk_ref, sk, 1.0))):
                _, vjp = jax.vjp(lambda xv, gv, sc=scale: _dil_prep_tile(DIFF, xv, gv, c, s1, s2, sc), x_ref[src, :], g_ref[...])
                dx, dg = vjp(ct_ref[dst, :])
                scr[src, :] = dx
                dgs[idx] = dgs[idx] + dg
            sv[src, :] = dv_ref[dst, :]
        dgq_ref[...] += dgs[0] + pltpu.roll(dgs[0], DIL_HD, 1)
        dgk_ref[...] += dgs[1] + pltpu.roll(dgs[1], DIL_HD, 1)
        for c0 in range(0, SPAN, 256):
            rows = slice(c0, c0 + 256)
            dzq_ref[rows, :] = sq[rows, :].astype(MXU_DTYPE)
            dzk_ref[rows, :] = sk[rows, :].astype(MXU_DTYPE)
            dzv_ref[rows, :] = sv[rows, :].astype(MXU_DTYPE)

    sds = jax.ShapeDtypeStruct
    return pl.pallas_call(
        body,
        name=f"dil_prep_bwd_{dil}",
        grid=(t // SPAN, 4),
        in_specs=[_span_blk(4 * gi), _span_blk(12 + 4 * gi)] + _DIL_TABLE_SPECS
        + [_const_blk((1, LANES)), _const_blk((1, LANES)), _span_blk(0), _span_blk(0), _span_blk(0)],
        out_specs=[_span_blk(0)] * 3 + [_const_blk((1, LANES))] * 2,
        out_shape=[sds((t, 512), MXU_DTYPE)] * 3 + [sds((1, LANES), F32)] * 2,
        scratch_shapes=[pltpu.VMEM((SPAN, LANES), F32)] * 3,
        compiler_params=_cparams(("arbitrary", "arbitrary")),
    )(z, z, tabs, tabs, tabs, gq2, gk2, dq, dk, dv)


def _band_masks():
    qi = lax.broadcasted_iota(jnp.int32, (DIL_NK, DIL_NK), 0)
    ki = lax.broadcasted_iota(jnp.int32, (DIL_NK, DIL_NK), 1)
    return (ki >= qi), (ki <= qi)


def _pair_heads():
    return [(pair, hh) for pair in range(4) for hh in range(2)]


def _pair_lanes(pair):
    return slice(LANES * pair, LANES * (pair + 1))


def _zero_other_head(mask, x):
    return jnp.where(mask, x, jnp.zeros_like(x))


def dil_attn_fwd(name, q, k, v, dil):
    t = q.shape[0]

    def body(q_ref, kp_ref, kc_ref, vp_ref, vc_ref, o_ref, lse_ref):
        b = pl.program_id(0)
        mprev, mcur = _band_masks()
        mprev = mprev & (b >= dil)
        hm = _head_masks()
        heads = _pair_heads()
        qh = [_zero_other_head(hm[hh], q_ref[:, _pair_lanes(pair)]) for pair, hh in heads]
        sps = [_dot(qh[i], kp_ref[:, _pair_lanes(pair)], NT) for i, (pair, _) in enumerate(heads)]
        scs = [_dot(qh[i], kc_ref[:, _pair_lanes(pair)], NT) for i, (pair, _) in enumerate(heads)]
        o = [jnp.zeros((DIL_NK, LANES), F32) for _ in range(4)]
        lse = [jnp.zeros((DIL_NK, LANES), F32) for _ in range(4)]
        for i, (pair, hh) in enumerate(heads):
            sp, sc = jnp.where(mprev, sps[i], NEG), jnp.where(mcur, scs[i], NEG)
            m = jnp.maximum(jnp.max(sp, axis=1, keepdims=True), jnp.max(sc, axis=1, keepdims=True))
            ep, ec = jnp.exp(sp - m), jnp.exp(sc - m)
            den = jnp.sum(ep, axis=1, keepdims=True) + jnp.sum(ec, axis=1, keepdims=True)
            lanes = _pair_lanes(pair)
            oh = _dot(ep, _zero_other_head(hm[hh], vp_ref[:, lanes]), NN) + _dot(ec, _zero_other_head(hm[hh], vc_ref[:, lanes]), NN)
            o[pair] = o[pair] + oh / den
            lse[pair] = jnp.where(hm[hh], m + jnp.log(den), lse[pair])
        for pair in range(4):
            o_ref[:, _pair_lanes(pair)] = o[pair]
            lse_ref[:, _pair_lanes(pair)] = lse[pair]

    cur = pl.BlockSpec((DIL_NK, 512), lambda b: (b, 0))
    prev = pl.BlockSpec((DIL_NK, 512), lambda b: (jnp.maximum(b - dil, 0), 0))
    sds = jax.ShapeDtypeStruct
    return pl.pallas_call(
        body,
        name=name,
        grid=(t // DIL_NK,),
        in_specs=[cur, prev, cur, prev, cur],
        out_specs=[cur, cur],
        out_shape=[sds((t, 512), F32), sds((t, 512), F32)],
        compiler_params=_cparams(("parallel",)),
    )(q, k, k, v, v)


def dil_attn_bwd(name, q, k, v, do, lse, dd, dil):
    t = q.shape[0]
    nblk = t // DIL_NK

    def body(q_ref, do_ref, l_ref, d_ref, kp_ref, kc_ref, vp_ref, vc_ref, dq_ref, dk_ref, dv_ref):
        b = pl.program_id(0)

        @pl.when(b == 0)
        def _():
            dk_ref[...] = jnp.zeros(dk_ref.shape, F32)
            dv_ref[...] = jnp.zeros(dv_ref.shape, F32)

        mprev, mcur = _band_masks()
        mprev = mprev & (b >= dil)
        rows_c = pl.ds(pl.multiple_of(b * DIL_NK, DIL_NK), DIL_NK)
        rows_p = pl.ds(pl.multiple_of(jnp.maximum(b - dil, 0) * DIL_NK, DIL_NK), DIL_NK)
        hm = _head_masks()
        heads = _pair_heads()
        ln_ = [_pair_lanes(pair) for pair, _ in heads]
        qh = [_zero_other_head(hm[hh], q_ref[:, ln_[i]]) for i, (_, hh) in enumerate(heads)]
        doh = [_zero_other_head(hm[hh], do_ref[:, ln_[i]]) for i, (_, hh) in enumerate(heads)]
        idx = range(len(heads))
        s_p = [_dot(qh[i], kp_ref[:, ln_[i]], NT) for i in idx]
        s_c = [_dot(qh[i], kc_ref[:, ln_[i]], NT) for i in idx]
        dp_p = [_dot(doh[i], vp_ref[:, ln_[i]], NT) for i in idx]
        dp_c = [_dot(doh[i], vc_ref[:, ln_[i]], NT) for i in idx]
        zero = lambda: [jnp.zeros((DIL_NK, LANES), F32) for _ in range(4)]
        dq, dk_p, dk_c, dv_p, dv_c = zero(), zero(), zero(), zero(), zero()
        for i, (pair, hh) in enumerate(heads):
            lse_h, d_h = _row_scalar(l_ref[:, ln_[i]], hm[hh]), _row_scalar(d_ref[:, ln_[i]], hm[hh])
            pp = jnp.exp(jnp.where(mprev, s_p[i] - lse_h, NEG))
            pc = jnp.exp(jnp.where(mcur, s_c[i] - lse_h, NEG))
            dsp, dsc = pp * (dp_p[i] - d_h), pc * (dp_c[i] - d_h)
            dq[pair] = (dq[pair] + _dot(dsp, _zero_other_head(hm[hh], kp_ref[:, ln_[i]]), NN)
                        + _dot(dsc, _zero_other_head(hm[hh], kc_ref[:, ln_[i]]), NN))
            dv_p[pair] = dv_p[pair] + _dot(pp, doh[i], TN)
            dv_c[pair] = dv_c[pair] + _dot(pc, doh[i], TN)
            dk_p[pair] = dk_p[pair] + _dot(dsp, qh[i], TN)
            dk_c[pair] = dk_c[pair] + _dot(dsc, qh[i], TN)
        for pair in range(4):
            lanes = _pair_lanes(pair)
            dq_ref[:, lanes] = dq[pair]
            dk_ref[rows_p, lanes] += dk_p[pair]
            dv_ref[rows_p, lanes] += dv_p[pair]
            dk_ref[rows_c, lanes] += dk_c[pair]
            dv_ref[rows_c, lanes] += dv_c[pair]

    cur = pl.BlockSpec((DIL_NK, 512), lambda b: (b, 0))
    prev = pl.BlockSpec((DIL_NK, 512), lambda b: (jnp.maximum(b - dil, 0), 0))
    whole = pl.BlockSpec((t, 512), lambda b: (0, 0))
    sds = jax.ShapeDtypeStruct
    return pl.pallas_call(
        body,
        name=name,
        grid=(nblk,),
        in_specs=[cur, cur, cur, cur, prev, cur, prev, cur],
        out_specs=[cur, whole, whole],
        out_shape=[sds((t, 512), F32)] * 3,
        compiler_params=_cparams(("arbitrary",)),
    )(q, do, lse, dd, k, k, v, v)


def dil_combine(os_, lses, z):
    t = z.shape[0]

    def body(o1_ref, l1_ref, o2_ref, l2_ref, o3_ref, l3_ref, g_ref, y_ref, o_ref, lse_ref, yt_ref, so2, sl2, so3, sl3):
        for (window, dil), o_in, l_in, so, sl in ((DIL_GROUPS[1], o2_ref, l2_ref, so2, sl2), (DIL_GROUPS[2], o3_ref, l3_ref, so3, sl3)):
            for u in range(SPAN // DIL_NK):
                src, dst = _rm_src(u, window, dil), _rm_dst(u)
                so[src, :] = o_in[dst, :]
                sl[src, :] = l_in[dst, :]
        for c0 in range(0, SPAN, 256):
            rows = slice(c0, c0 + 256)
            la, lb, lc = l1_ref[rows, :], sl2[rows, :], sl3[rows, :]
            mx = jnp.maximum(jnp.maximum(la, lb), lc)
            wa, wb, wc = jnp.exp(la - mx), jnp.exp(lb - mx), jnp.exp(lc - mx)
            tot = wa + wb + wc
            o = (wa * o1_ref[rows, :] + wb * so2[rows, :] + wc * so3[rows, :]) / tot
            y = o * _silu_and_grad(g_ref[rows, :])[0]
            y_ref[rows, :] = y.astype(MXU_DTYPE)
            o_ref[rows, :] = o
            lse_ref[rows, :] = mx + jnp.log(tot)
            yt_ref[:, rows] = y.T.astype(MXU_DTYPE)

    sds = jax.ShapeDtypeStruct
    return pl.pallas_call(
        body,
        name="dil_combine",
        grid=(t // SPAN, 4),
        in_specs=[_span_blk(0)] * 6 + [_span_blk(36)],
        out_specs=[_span_blk(0)] * 3 + [pl.BlockSpec((LANES, SPAN), lambda s, p: (p, s))],
        out_shape=[sds((t, 512), MXU_DTYPE), sds((t, 512), F32), sds((t, 512), F32), sds((512, t), MXU_DTYPE)],
        scratch_shapes=[pltpu.VMEM((SPAN, LANES), F32)] * 4,
        compiler_params=_cparams(("parallel", "parallel")),
    )(os_[0], lses[0], os_[1], lses[1], os_[2], lses[2], z)


def dil_gate_bwd(dy, o, z, lse):
    t = dy.shape[0]

    def body(dy_ref, o_ref, g_ref, lse_ref, dzg_ref, do1_ref, dd1_ref, do2_ref, dd2_ref, l2_ref, do3_ref, dd3_ref, l3_ref, do_scr):
        for c0 in range(0, SPAN, 256):
            rows = slice(c0, c0 + 256)
            dyv, ov = dy_ref[rows, :], o_ref[rows, :]
            silu, dsilu = _silu_and_grad(g_ref[rows, :])
            do = dyv * silu
            do_scr[rows, :] = do
            do1_ref[rows, :] = do.astype(MXU_DTYPE)
            dd1_ref[rows, :] = _seg_sum_impl(do * ov)
            dzg_ref[rows, :] = (dyv * ov * dsilu).astype(MXU_DTYPE)
        for (window, dil), do_o, dd_o, l_o in ((DIL_GROUPS[1], do2_ref, dd2_ref, l2_ref), (DIL_GROUPS[2], do3_ref, dd3_ref, l3_ref)):
            for u in range(SPAN // DIL_NK):
                src, dst = _rm_src(u, window, dil), _rm_dst(u)
                do_o[dst, :] = do_scr[src, :].astype(MXU_DTYPE)
                dd_o[dst, :] = dd1_ref[src, :]
                l_o[dst, :] = lse_ref[src, :]

    sds = jax.ShapeDtypeStruct
    f32, mxu = sds((t, 512), F32), sds((t, 512), MXU_DTYPE)
    return pl.pallas_call(
        body,
        name="dil_gate_bwd",
        grid=(t // SPAN, 4),
        in_specs=[_span_blk(0), _span_blk(0), _span_blk(36), _span_blk(0)],
        out_specs=[_span_blk(0)] * 9,
        out_shape=[mxu, mxu, f32, mxu, f32, f32, mxu, f32, f32],
        scratch_shapes=[pltpu.VMEM((SPAN, LANES), F32)],
        compiler_params=_cparams(("parallel", "parallel")),
    )(dy, o, z, lse)


def _merge_tile(p0, p1, p2, z0, z1, z2, b0, b1, b2):
    return _sigmoid(z0 + b0) * p0 + _sigmoid(z1 + b1) * p1 + _sigmoid(z2 + b2) * p2


def _merge_ins(ps, z, b):
    w = 256
    return ([_row(p, w, 0, True) for p in ps] + [_row(z, w, 4 * i, True) for i in range(3)]
            + [_const(b, w, 4 * i, True) for i in range(3)])


def merge_fwd(ps, z, b):
    def fn(ins, outs, _):
        merged = _merge_tile(*[r[...] for r in ins])
        outs[0][...] = merged.astype(MXU_DTYPE)
        outs[1][...] = merged.T.astype(MXU_DTYPE)

    return rowwise("merge_fwd", fn, z.shape[0], ncol=4, ins=_merge_ins(ps, z, b), outs=[(1024, 256, MXU_DTYPE, True)],
                   touts=[(1024, 256, MXU_DTYPE, True)])


def merge_bwd(dm, ps, z, b):
    def fn(ins, outs, accs):
        _, vjp = jax.vjp(_merge_tile, *[r[...] for r in ins[:9]])
        grads = vjp(ins[9][...])
        for i in range(3):
            outs[i][...] = grads[i].astype(MXU_DTYPE)
            outs[3 + i][...] = grads[3 + i].astype(MXU_DTYPE)
            accs[i][...] += grads[6 + i]

    return rowwise("merge_bwd", fn, z.shape[0], ncol=4, ins=_merge_ins(ps, z, b) + [_row(dm, 256, 0, True)],
                   outs=[(1024, 256, MXU_DTYPE, True)] * 6, accs=[(1, 1024, 256, True)] * 3)


EW_BLOCK_BYTES = 2**21


def _tile2d(r, c):
    if r * c * 4 <= EW_BLOCK_BYTES:
        return r, c
    for tr in (512, 256, 128, 64, 32):
        if r % tr == 0 and tr * c * 4 <= EW_BLOCK_BYTES:
            return tr, c
    for tc in (1024, 512, 256, 128):
        if c % tc == 0 and r * tc * 4 <= EW_BLOCK_BYTES:
            return r, tc
    raise ValueError((r, c))


def adamw(name, w, g, m, v):
    shape = w.shape
    c = shape[-1]
    r = w.size // c
    tr, tc = _tile2d(r, c)
    assert tc == c
    c1, c2 = 1.0 - ADAM_B1 ** ADAM_STEP, 1.0 - ADAM_B2 ** ADAM_STEP

    def body(w_ref, g_ref, m_ref, v_ref, d_ref, mo_ref, vo_ref):
        gv = g_ref[...]
        mn = ADAM_B1 * m_ref[...] + (1.0 - ADAM_B1) * gv
        vn = ADAM_B2 * v_ref[...] + (1.0 - ADAM_B2) * (gv * gv)
        d_ref[...] = -ADAM_LR * ((mn / c1) / (jnp.sqrt(vn / c2) + ADAM_EPS) + ADAM_WD * w_ref[...])
        mo_ref[...] = mn
        vo_ref[...] = vn

    spec = pl.BlockSpec((tr, c), lambda i: (i, 0))
    outs = pl.pallas_call(
        body,
        name=name,
        grid=(r // tr,),
        in_specs=[spec] * 4,
        out_specs=[spec] * 3,
        out_shape=[jax.ShapeDtypeStruct((r, c), F32)] * 3,
        compiler_params=_cparams(("parallel",)),
    )(*[a.reshape(r, c) for a in (w, g, m, v)])
    return [o.reshape(shape) for o in outs]


def adamw_pair(name, w, reduced, received, m, v, c_idx):
    _, r, c = w.shape
    tr, tc = _tile2d(r, c)
    c1, c2 = 1.0 - ADAM_B1 ** ADAM_STEP, 1.0 - ADAM_B2 ** ADAM_STEP

    def body(c_ref, w_ref, r0_ref, x0_ref, r1_ref, x1_ref, m_ref, v_ref, g_ref, d_ref, mo_ref, vo_ref):
        mine = c_ref[0]
        gv = jnp.where(pl.program_id(0) == 0, jnp.where(mine == 0, r0_ref[...], x0_ref[...]),
                       jnp.where(mine == 1, r1_ref[...], x1_ref[...]))
        mn = ADAM_B1 * m_ref[...] + (1.0 - ADAM_B1) * gv
        vn = ADAM_B2 * v_ref[...] + (1.0 - ADAM_B2) * (gv * gv)
        g_ref[...] = gv
        d_ref[...] = -ADAM_LR * ((mn / c1) / (jnp.sqrt(vn / c2) + ADAM_EPS) + ADAM_WD * w_ref[...])
        mo_ref[...] = mn
        vo_ref[...] = vn

    full = pl.BlockSpec((None, tr, tc), lambda h, i, j, cr: (h, i, j))
    half = pl.BlockSpec((tr, tc), lambda h, i, j, cr: (i, j))
    return pl.pallas_call(
        body,
        name=name,
        grid_spec=pltpu.PrefetchScalarGridSpec(
            num_scalar_prefetch=1,
            grid=(2, r // tr, c // tc),
            in_specs=[full, half, half, half, half, full, full],
            out_specs=[full] * 4,
        ),
        out_shape=[jax.ShapeDtypeStruct((2, r, c), F32)] * 4,
        compiler_params=_cparams(("parallel", "parallel", "parallel")),
    )(c_idx, w, reduced[0], received[0], reduced[1], received[1], m, v)


def sum_pair(name, g, la, out_dtype):
    _, r, c = g.shape
    tr, tc = _tile2d(r, c)

    def body(g_ref, la_ref, o_ref):
        o_ref[...] = (g_ref[...] + la_ref[...]).astype(o_ref.dtype)

    spec = pl.BlockSpec((None, tr, tc), lambda s, i, j: (s, i, j))
    return pl.pallas_call(
        body,
        name=name,
        grid=(4, r // tr, c // tc),
        in_specs=[spec, spec],
        out_specs=spec,
        out_shape=jax.ShapeDtypeStruct((4, r, c), out_dtype),
        compiler_params=_cparams(("parallel", "parallel", "parallel")),
    )(g, la)


def sum_chips(name, g, la, lb, s_idx):
    _, r, c = g.shape
    tr, tc = _tile2d(r, c)

    def body(s_ref, g_ref, la_ref, l0_ref, l1_ref, l2_ref, o_ref):
        own = g_ref[...] + la_ref[...]
        o_ref[...] = ((own + l0_ref[...].astype(F32)) + l1_ref[...].astype(F32)) + l2_ref[...].astype(F32)

    own_spec = pl.BlockSpec((None, tr, tc), lambda i, j, sr: (sr[0], i, j))
    lspec = lambda k: pl.BlockSpec((None, tr, tc), lambda i, j, sr: (k, i, j))
    return pl.pallas_call(
        body,
        name=name,
        grid_spec=pltpu.PrefetchScalarGridSpec(
            num_scalar_prefetch=1,
            grid=(r // tr, c // tc),
            in_specs=[own_spec, own_spec, lspec(0), lspec(1), lspec(2)],
            out_specs=pl.BlockSpec((tr, tc), lambda i, j, sr: (i, j)),
        ),
        out_shape=jax.ShapeDtypeStruct((r, c), F32),
        compiler_params=_cparams(("parallel", "parallel")),
    )(s_idx, g, la, lb, lb, lb)


def _place():
    x, y, c = lax.axis_index("x"), lax.axis_index("y"), lax.axis_index("c")
    chips = [(1 - x, y), (x, 1 - y), (1 - x, 1 - y)]
    return x, y, c, chips


HBM_SPEC = pl.BlockSpec(memory_space=pltpu.HBM)


def _comm_call(name, body, ins, out_shapes, n_sem, n_local):
    return pl.pallas_call(
        body,
        name=name,
        in_specs=[HBM_SPEC] * len(ins),
        out_specs=[HBM_SPEC] * len(out_shapes),
        out_shape=out_shapes,
        scratch_shapes=[pltpu.SemaphoreType.DMA((n_sem,)), pltpu.SemaphoreType.DMA((n_sem,)),
                        pltpu.SemaphoreType.DMA((max(n_local, 1),))],
    )(*ins)


def _layer_gather(ins, outs, send, recv, layer):
    n = len(ins)
    x, y, c, chips = _place()
    s = 2 * x + y
    sib = (x, y, 1 - c)
    active = c == layer

    def rc(a, k, src, dst, dev):
        return pltpu.make_async_remote_copy(src_ref=src, dst_ref=dst, send_sem=send.at[6 * a + k], recv_sem=recv.at[6 * a + k],
                                            device_id=dev, device_id_type=MESH)

    def first_hop():
        return [rc(a, j, ins[a], outs[a].at[s], (*chip, c)) for j, chip in enumerate(chips) for a in range(n)]

    def start():
        @pl.when(active)
        def _():
            for cp in first_hop():
                cp.start()

    def finish():
        @pl.when(active)
        def _():
            forwards = []
            for j, (cx, cy) in enumerate(chips):
                for a in range(n):
                    landed = outs[a].at[2 * cx + cy]
                    rc(a, j, landed, landed, sib).wait_recv()
                    forwards.append(rc(a, 3 + j, landed, landed, sib))
                    forwards[-1].start()
            for cp in first_hop() + forwards:
                cp.wait_send()

        @pl.when(jnp.logical_not(active))
        def _():
            for j, (cx, cy) in enumerate(chips):
                for a in range(n):
                    other = outs[a].at[2 * cx + cy]
                    rc(a, 3 + j, other, other, sib).wait_recv()

    return start, finish


def _gather_shapes(ws):
    return [jax.ShapeDtypeStruct((4,) + w.shape, w.dtype) for w in ws]


def gather_hook(ws, layer):
    return (ws, _gather_shapes(ws), 6 * len(ws), lambda i, o, s, r: _layer_gather(i, o, s, r, layer))


def allgather_layer(ws, layer):
    n = len(ws)

    def body(*refs):
        send, recv, _ = refs[2 * n:]
        start, finish = _layer_gather(refs[:n], refs[n:2 * n], send, recv, layer)
        start()
        finish()

    return _comm_call(f"allgather_layer{layer}", body, ws, _gather_shapes(ws), 6 * n, 0)


def exchange_sibling(gs, layer, tag=""):
    n = len(gs)

    def body(*refs):
        send, recv, _ = refs[2 * n:]
        start, finish = _sibling_exchange(refs[:n], refs[n:2 * n], send, recv, layer)
        start()
        finish()

    return _comm_call(f"exchange_sibling{layer}{tag}", body, gs, [jax.ShapeDtypeStruct(g.shape, g.dtype) for g in gs], n, 0)


def _sibling_exchange(ins, outs, send, recv, layer):
    x, y, c, _ = _place()

    def copies():
        return [pltpu.make_async_remote_copy(src_ref=ins[a], dst_ref=outs[a], send_sem=send.at[a], recv_sem=recv.at[a],
                                             device_id=(x, y, 1 - c), device_id_type=MESH) for a in range(len(ins))]

    def start():
        @pl.when(c != layer)
        def _():
            for cp in copies():
                cp.start()

    def finish():
        @pl.when(c != layer)
        def _():
            for cp in copies():
                cp.wait_send()

        @pl.when(c == layer)
        def _():
            for cp in copies():
                cp.wait_recv()

    return start, finish


def sibling_hook(gs, layer):
    return (gs, [jax.ShapeDtypeStruct(g.shape, g.dtype) for g in gs], len(gs),
            lambda i, o, s, r: _sibling_exchange(i, o, s, r, layer))


def chips_hook(ps, layer):
    return (ps, _chips_shapes(ps), 3 * len(ps), lambda i, o, s, r: _chips_exchange(i, o, s, r, layer))


def _chips_exchange(ins, outs, send, recv, layer):
    n = len(ins)
    _, _, c, chips = _place()

    def copies():
        return [pltpu.make_async_remote_copy(src_ref=ins[a].at[2 * cx + cy], dst_ref=outs[a].at[j], send_sem=send.at[3 * a + j],
                                             recv_sem=recv.at[3 * a + j], device_id=(cx, cy, c), device_id_type=MESH)
                for j, (cx, cy) in enumerate(chips) for a in range(n)]

    def start():
        @pl.when(c == layer)
        def _():
            for cp in copies():
                cp.start()

    def finish():
        @pl.when(c == layer)
        def _():
            for cp in copies():
                cp.wait()

    return start, finish


def _chips_shapes(ps):
    return [jax.ShapeDtypeStruct((3,) + p.shape[1:], p.dtype) for p in ps]


def exchange_chips(ps, layer, tag=""):
    n = len(ps)

    def body(*refs):
        send, recv, _ = refs[2 * n:]
        start, finish = _chips_exchange(refs[:n], refs[n:2 * n], send, recv, layer)
        start()
        finish()

    return _comm_call(f"exchange_chips{layer}{tag}", body, ps, _chips_shapes(ps), 3 * n, 0)


def exchange_final(rs, layer, small=None):
    n = len(rs)
    ins_all = list(rs) + ([small] if small is not None else [])

    def body(*refs):
        ni = len(ins_all)
        ins, outs = refs[:ni], refs[ni:2 * ni]
        send, recv, lsem = refs[2 * ni:]
        x, y, c, _ = _place()
        s = 2 * x + y
        to_sib = [pltpu.make_async_remote_copy(src_ref=ins[a], dst_ref=outs[a], send_sem=send.at[a], recv_sem=recv.at[a],
                                               device_id=(x, y, 1 - c), device_id_type=MESH) for a in range(n)]
        rel = [(fx, fy) for fx in (0, 1) for fy in (0, 1)]

        def piece(fx, fy, cc, dst_slot):
            k = n + 2 * (2 * fx + fy) + cc
            return pltpu.make_async_remote_copy(src_ref=ins[n], dst_ref=outs[n].at[dst_slot], send_sem=send.at[k], recv_sem=recv.at[k],
                                                device_id=(x ^ fx, y ^ fy, cc), device_id_type=MESH)

        @pl.when(c == layer)
        def _():
            for cp in to_sib:
                cp.start()
            if small is not None:
                own = pltpu.make_async_copy(ins[n], outs[n].at[s], lsem.at[0])
                own.start()
                sends = [piece(fx, fy, cc, s) for fx, fy in rel for cc in (0, 1) if (fx, fy) != (0, 0) or cc != layer]
                for cp in sends:
                    cp.start()
                for fx, fy in rel[1:]:
                    piece(fx, fy, layer, 2 * (x ^ fx) + (y ^ fy)).wait_recv()
                for cp in sends:
                    cp.wait_send()
                own.wait()
            for cp in to_sib:
                cp.wait_send()

        @pl.when(c != layer)
        def _():
            for cp in to_sib:
                cp.wait_recv()
            if small is not None:
                for fx, fy in rel:
                    piece(fx, fy, 1 - layer, 2 * (x ^ fx) + (y ^ fy)).wait_recv()

    out_shapes = [jax.ShapeDtypeStruct(r.shape, r.dtype) for r in rs]
    if small is not None:
        out_shapes.append(jax.ShapeDtypeStruct((4,) + small.shape, small.dtype))
    res = _comm_call(f"exchange_final{layer}", body, ins_all, out_shapes, n + 8, 1)
    return (res[:n], res[n]) if small is not None else (res[:n], None)


def _pad_in_cols(w):
    z = lambda n: jnp.zeros(w.shape[:-1] + (n,), w.dtype)
    return jnp.concatenate([w[..., :KR_OFF], z(64), w[..., KR_OFF:KR_OFF + MLA_ROPE], z(32), w[..., KR_OFF + MLA_ROPE:]], axis=-1)


def _unpad_in_rows(gt):
    return jnp.concatenate([gt[:KR_OFF], gt[KR_OFF + 64:KR_OFF + 96], gt[KR_OFF + 128:]], axis=0)


def _pad_heads(w, real):
    k = w.shape[0]
    return jnp.pad(w.reshape(k, MLA_HEADS, real), ((0, 0), (0, 0), (0, LANES - real))).reshape(k, MLA_HEADS * LANES)


def _pad_gain(g, real):
    return jnp.pad(g.reshape(1, real), ((0, 0), (0, LANES - real)))


def layer_weights(full):
    w_ukv = full["w_ukv"].reshape(128, MLA_HEADS, 2, 64)
    two = lambda g: jnp.concatenate([g, g]).reshape(1, LANES)
    return dict(
        norm_g=full["norm_g"].reshape(1, -1), w_in=full["w_in"], conv_w=full["conv_w"], conv_b=full["conv_b"].reshape(1, -1),
        wgx=full["w_gate_x"], bgx=full["b_gate_x"].reshape(LRU_BLOCKS, 1, LANES),
        wga=full["w_gate_a"], bga=full["b_gate_a"].reshape(LRU_BLOCKS, 1, LANES),
        lam=full["lru_lambda"].reshape(1, -1), w_lru_o=full["w_lru_o"],
        cq_norm_g=full["cq_norm_g"].reshape(1, -1), ckv_norm_g=full["ckv_norm_g"].reshape(1, -1),
        wq=_pad_heads(full["w_uq"], MLA_QK), wk=_pad_heads(w_ukv[:, :, 0].reshape(128, 512), 64),
        wv=w_ukv[:, :, 1].reshape(128, 512),
        gq=_pad_gain(full["mla_q_norm_g"], MLA_QK), gk=_pad_gain(full["mla_k_norm_g"], MLA_QK),
        w_mla_o=full["w_mla_o"], gq2=two(full["dil_q_norm_g"]), gk2=two(full["dil_k_norm_g"]),
        w_dil_o=full["w_dil_o"], b_merge=full["b_merge"].reshape(1, -1), w_out=full["w_out"],
    )


def layer_fwd(x, w, tabs, gather=None):
    h, ht = rmsnorm_fwd(x, w["norm_g"])
    z_lru = mm_nn("in_proj_lru", h, w["w_in"], n=2048, n_off=G_LRU)
    z_mla = mm_nn("in_proj_mla", h, w["w_in"], n=1024, n_off=G_MLA)
    z_dil = mm_nn("in_proj_dil", h, w["w_in"], n=5120, n_off=G_DIL)
    z_mrg = mm_nn("in_proj_mrg", h, w["w_in"], n=3072, n_off=G_MRG)
    rest_hook = None if gather is None else gather_hook(gather[0][1:], gather[1])
    hs, y_lru, yt_lru, *ga_rest = lru_fwd(z_lru, w["conv_w"], w["conv_b"], w["wgx"], w["bgx"], w["wga"], w["bga"], w["lam"],
                                          comm=rest_hook)
    qm, km, vm, vtm = mla_prep_fwd(z_mla, tabs, w)
    o_mla, y_mla, yt_mla, lse_mla, *ga_first = mla_attn_fwd(qm, km, vtm, z_mla,
                                                            gather=None if gather is None else (gather[0][:1], gather[1]))
    gathered = ga_first + ga_rest
    os_, lses, dil_rm = [], [], []
    for gi, (_, dil) in enumerate(DIL_GROUPS):
        qkv = dil_prep_fwd(z_dil, tabs, w["gq2"], w["gk2"], gi)
        o, lse = dil_attn_fwd(f"dil_attn_fwd_{dil}", *qkv, dil)
        os_.append(o)
        lses.append(lse)
        dil_rm.append(qkv)
    y_dil, o_dil, lse_dil, yt_dil = dil_combine(os_, lses, z_dil)
    ps = [mm_nn("proj_lru", y_lru, w["w_lru_o"]), mm_nn("proj_mla", y_mla, w["w_mla_o"]), mm_nn("proj_dil", y_dil, w["w_dil_o"])]
    merged, merged_t = merge_fwd(ps, z_mrg, w["b_merge"])
    out = mm_nn("out_proj", merged, w["w_out"], add=x)
    res = dict(x=x, ht=ht, z_lru=z_lru, z_mla=z_mla, z_dil=z_dil, z_mrg=z_mrg, hs=hs, yt_lru=yt_lru, qm=qm, km=km, vm=vm, o_mla=o_mla,
               yt_mla=yt_mla, lse_mla=lse_mla, dil_rm=dil_rm, yt_dil=yt_dil, o_dil=o_dil, lse_dil=lse_dil, ps=ps, merged_t=merged_t)
    return out, res, gathered


def layer_bwd(dout, r, w, tabs, prev=None, own=None):
    g = {}
    dmerged = mm_nt("out_proj_dx", dout, w["w_out"])
    g["w_out"] = mm_nn("out_proj_dw", r["merged_t"], dout, tm=1024, tn=512, tk=2048)
    dp0, dp1, dp2, dzm0, dzm1, dzm2, db0, db1, db2 = merge_bwd(dmerged, r["ps"], r["z_mrg"], w["b_merge"])
    g["b_merge"] = jnp.concatenate([db0, db1, db2], axis=1).reshape(-1)
    dy_lru = mm_nt("proj_lru_dx", dp0, w["w_lru_o"])
    dy_mla = mm_nt("proj_mla_dx", dp1, w["w_mla_o"])
    dy_dil = mm_nt("proj_dil_dx", dp2, w["w_dil_o"])
    g["w_lru_o"] = mm_nn("proj_lru_dw", r["yt_lru"], dp0, **DW_TILES)
    g["w_mla_o"] = mm_nn("proj_mla_dw", r["yt_mla"], dp1, **DW_TILES)
    g["w_dil_o"] = mm_nn("proj_dil_dw", r["yt_dil"], dp2, **DW_TILES)
    dzx, dzg_lru, dwgx, dbgx, dwga, dbga, dlam, dcw, dcb, *la_prev = lru_bwd(
        r["z_lru"], r["hs"], dy_lru, w["conv_w"], w["conv_b"], w["wgx"], w["bgx"], w["wga"], w["bga"], w["lam"],
        comm=None if prev is None else sibling_hook(prev[0], prev[1]))
    chips = None if prev is None else (prev[2](la_prev), prev[1])
    g.update(w_gate_x=dwgx, b_gate_x=dbgx.reshape(LRU_BLOCKS, LANES), w_gate_a=dwga, b_gate_a=dbga.reshape(LRU_BLOCKS, LANES),
             lru_lambda=dlam.reshape(-1), conv_w=dcw, conv_b=dcb.reshape(-1))
    do_m, dzg_mla, dd_m = mla_gate_bwd(dy_mla, r["o_mla"], r["z_mla"])
    dq_m, dk_m, dv_m, *lb_prev = mla_attn_bwd(r["qm"], r["km"], r["vm"], do_m, r["lse_mla"], dd_m, chips=chips)
    dz_mla3, dg_cq, dg_ckv, dwq, dwk, dwv, dgq, dgk = mla_prep_bwd(r["z_mla"], tabs, w, dq_m, dk_m, dv_m)
    g.update(cq_norm_g=dg_cq.reshape(-1), ckv_norm_g=dg_ckv.reshape(-1), mla_q_norm_g=dgq[0, :MLA_QK], mla_k_norm_g=dgk[0, :MLA_QK])
    g["w_uq"] = dwq.reshape(256, MLA_HEADS, LANES)[:, :, :MLA_QK].reshape(256, MLA_HEADS * MLA_QK)
    g["w_ukv"] = jnp.concatenate([dwk.reshape(128, MLA_HEADS, LANES)[:, :, :64], dwv.reshape(128, MLA_HEADS, 64)], axis=2).reshape(128, 1024)
    dzg_dil, do1, dd1, do2, dd2, l2, do3, dd3, l3 = dil_gate_bwd(dy_dil, r["o_dil"], r["z_dil"], r["lse_dil"])
    stats = [(do1, r["lse_dil"], dd1), (do2, l2, dd2), (do3, l3, dd3)]
    dzq, dzk, dzv, dgq2, dgk2 = [], [], [], [], []
    for gi, (_, dil) in enumerate(DIL_GROUPS):
        dq, dk, dv = dil_attn_bwd(f"dil_attn_bwd_{dil}", *r["dil_rm"][gi], *stats[gi], dil)
        parts = dil_prep_bwd(r["z_dil"], tabs, w["gq2"], w["gk2"], dq, dk, dv, gi)
        for acc, part in zip((dzq, dzk, dzv, dgq2, dgk2), parts):
            acc.append(part)
    g.update(dil_q_norm_g=sum(dgq2)[0, :DIL_HD], dil_k_norm_g=sum(dgk2)[0, :DIL_HD])
    dz = jnp.concatenate([dzx, dzg_lru, dz_mla3, dzg_mla] + dzq + dzk + dzv + [dzg_dil, dzm0, dzm1, dzm2], axis=1)
    g["w_in"] = _unpad_in_rows(mm_nn("in_proj_dw", r["ht"], dz, transpose_out=True, **DW_TILES))
    own_hook = None if own is None else chips_hook(*own(g))
    out = mm_nt("in_proj_dx", dz, w["w_in"], tm=1024, tn=1024, tk=IN_PAD // 4, comm=own_hook)
    dh, lb_own = (out, []) if own_hook is None else (out[0], out[1:])
    dx, dng = rmsnorm_bwd(r["x"], dh, dout, w["norm_g"])
    g["norm_g"] = dng.reshape(-1)
    return dx, g, dict(la_prev=la_prev, lb_prev=lb_prev, lb_own=lb_own)


def local_step(x, positions, target, full0, full1=None, gather1=None, reduce=None):
    tabs = rope_tables(positions.reshape(-1, 1))
    ws, ress = [], []
    for l in range(2):
        if l == 0:
            ws.append(layer_weights(full0))
            x, res, gathered = layer_fwd(x, ws[0], tabs, gather=None if gather1 is None else (gather1[0], 1))
        else:
            ws.append(layer_weights(full1 if gather1 is None else gather1[1](gathered)))
            x, res, _ = layer_fwd(x, ws[1], tabs)
        ress.append(res)
    dy, loss = loss_head(x, target)
    dy, grads1, _ = layer_bwd(dy, ress[1], ws[1], tabs)
    prev1, own0 = (None, None) if reduce is None else (reduce[0](grads1), reduce[1])
    dy, grads0, landed = layer_bwd(dy, ress[0], ws[0], tabs, prev=prev1, own=own0)
    return loss, dy, [grads0, grads1], landed


WEIGHTS = ["norm_g", "w_in", "conv_w", "conv_b", "w_gate_x", "b_gate_x", "w_gate_a", "b_gate_a", "lru_lambda", "w_lru_o", "cq_norm_g",
           "ckv_norm_g", "w_uq", "w_ukv", "mla_q_norm_g", "mla_k_norm_g", "w_mla_o", "dil_q_norm_g", "dil_k_norm_g", "w_dil_o", "b_merge",
           "w_out"]
SHARDED = {"w_in": 2, "conv_w": 2, "w_lru_o": 1, "w_uq": 2, "w_ukv": 2, "w_mla_o": 2, "w_dil_o": 2, "w_out": 1}
REPLICATED = [n for n in WEIGHTS if n not in SHARDED]
SMALL_ROWS = 144


def kernel(x, positions, norm_g, w_in, conv_w, conv_b, w_gate_x, b_gate_x, w_gate_a, b_gate_a, lru_lambda, w_lru_o, cq_norm_g, ckv_norm_g, w_uq, w_ukv, mla_q_norm_g, mla_k_norm_g, w_mla_o, dil_q_norm_g, dil_k_norm_g, w_dil_o, b_merge, w_out, loss_target, m_norm_g, m_w_in, m_conv_w, m_conv_b, m_w_gate_x, m_b_gate_x, m_w_gate_a, m_b_gate_a, m_lru_lambda, m_w_lru_o, m_cq_norm_g, m_ckv_norm_g, m_w_uq, m_w_ukv, m_mla_q_norm_g, m_mla_k_norm_g, m_w_mla_o, m_dil_q_norm_g, m_dil_k_norm_g, m_w_dil_o, m_b_merge, m_w_out, v_norm_g, v_w_in, v_conv_w, v_conv_b, v_w_gate_x, v_b_gate_x, v_w_gate_a, v_b_gate_a, v_lru_lambda, v_w_lru_o, v_cq_norm_g, v_ckv_norm_g, v_w_uq, v_w_ukv, v_mla_q_norm_g, v_mla_k_norm_g, v_w_mla_o, v_dil_q_norm_g, v_dil_k_norm_g, v_w_dil_o, v_b_merge, v_w_out):
    args = locals()
    w = {n: args[n] for n in WEIGHTS}
    m = {n: args["m_" + n] for n in WEIGHTS}
    v = {n: args["v_" + n] for n in WEIGHTS}
    my_c = lax.axis_index("c").astype(jnp.int32)
    my_s = (2 * lax.axis_index("x") + lax.axis_index("y")).astype(jnp.int32)
    c_idx = my_c.reshape(1)
    s_idx = my_s.reshape(1)

    names = list(SHARDED)
    wire = [[w[n][l] if n == "conv_w" else w[n][l].astype(BF16) for n in names] for l in range(2)]

    def assemble(l, gathered):
        full = {n: w[n][l] for n in REPLICATED}
        for n, ga, own in zip(names, gathered, wire[l]):
            full[n] = jnp.concatenate([jnp.where(my_s == s, own, ga[s]) for s in range(4)], axis=SHARDED[n] - 1)
        full["w_in"] = _pad_in_cols(full["w_in"])
        return full

    full0 = assemble(0, allgather_layer(wire[0], 0))

    def shards_of(grads_l):
        gs = []
        for n in names:
            g_ = grads_l[n]
            if n == "w_in":
                gs.append(g_.reshape(4, IN_WIDTH // 4, D_MODEL))
            else:
                parts = jnp.stack(jnp.split(g_, 4, axis=SHARDED[n] - 1))
                gs.append(parts.reshape(4, -1, parts.shape[-1]))
        return gs

    def pair_sums(l, gs, la, tags):
        dts = [BF16 if g_.size >= 2**19 and t != "small" else F32 for g_, t in zip(gs, tags)]
        return [sum_pair(f"sum_pair{l}_{t}", g_, l_, dt) for t, g_, l_, dt in zip(tags, gs, la, dts)]

    def chip_sums(l, gs, la, lb, tags):
        return [sum_chips(f"sum_chips{l}_{t}", g_, l_, b_, s_idx) for t, g_, l_, b_ in zip(tags, gs, la, lb)]

    stash = {}

    def prev1(grads1):
        stash["gs1"] = shards_of(grads1)
        return stash["gs1"], 1, lambda la: pair_sums(1, stash["gs1"], la, names)

    def own0(grads0):
        stash["gs0"] = shards_of(grads0)
        stash["la0"] = exchange_sibling(stash["gs0"], 0)
        return pair_sums(0, stash["gs0"], stash["la0"], names), 0

    loss, grad_x, grads, landed = local_step(x[0], positions[0], loss_target[0], full0,
                                             gather1=(wire[1], functools.partial(assemble, 1)), reduce=(prev1, own0))
    loss = lax.psum(loss, ("x", "y", "c"))
    reduced1 = chip_sums(1, stash["gs1"], landed["la_prev"], landed["lb_prev"], names)
    received1, _ = exchange_final(reduced1, 1)
    reduced0 = chip_sums(0, stash["gs0"], stash["la0"], landed["lb_own"], names)

    flat = jnp.concatenate([jnp.stack([grads[0][n], grads[1][n]]).reshape(-1) for n in REPLICATED])
    small = [jnp.pad(flat, (0, 4 * SMALL_ROWS * 1024 - flat.size)).reshape(4, SMALL_ROWS, 1024)]
    la_s = exchange_sibling(small, 0, tag="_small")
    lb_s = exchange_chips(pair_sums(0, small, la_s, ["small"]), 0, tag="_small")
    received0, small_all = exchange_final(reduced0, 0, small=chip_sums(0, small, la_s, lb_s, ["small"])[0])

    g_local, delta, new_m, new_v = {}, {}, {}, {}
    for i, n in enumerate(names):
        shp = w[n].shape
        if n == "w_in":
            as3 = lambda a: a.transpose(0, 2, 1)
            back = lambda o: o.transpose(0, 2, 1)
        else:
            as3 = lambda a, rc=reduced0[i].shape: a.reshape((2,) + rc)
            back = lambda o, shp=shp: o.reshape(shp)
        outs = adamw_pair(f"adamw_{n}", as3(w[n]), (reduced0[i], reduced1[i]), (received0[i], received1[i]), as3(m[n]), as3(v[n]),
                          c_idx)
        g_local[n], delta[n], new_m[n], new_v[n] = [back(o) for o in outs]
    flat = small_all.reshape(-1)
    off = 0
    for n in REPLICATED:
        g_local[n] = flat[off:off + w[n].size].reshape(w[n].shape)
        off += w[n].size
        delta[n], new_m[n], new_v[n] = adamw(f"adamw_{n}", w[n], g_local[n], m[n], v[n])
    return (loss, grad_x[None], *[g_local[n] for n in WEIGHTS], *[delta[n] for n in WEIGHTS], *[new_m[n] for n in WEIGHTS],
            *[new_v[n] for n in WEIGHTS])
```

```python
import functools

import jax
import jax.numpy as jnp
from jax import lax
from jax.experimental import pallas as pl
from jax.experimental.pallas import tpu as pltpu

F32 = jnp.float32
BF16 = jnp.bfloat16
MXU_DTYPE = jnp.bfloat16

D_MODEL = 1024
EPS = 1e-6
ROPE_THETA = 10000.0
LRU_BLOCKS = 8
LRU_C = 8.0
MLA_HEADS = 8
MLA_NOPE = 64
MLA_ROPE = 32
MLA_QK = 96
DIL_GROUPS = ((128, 1), (512, 4), (2048, 16))
DIL_HD = 64
DIL_NK = 128
MLA_HPS = 8
IN_WIDTH = 11168
ADAM_LR, ADAM_B1, ADAM_B2, ADAM_EPS, ADAM_WD, ADAM_STEP = 0.001, 0.9, 0.999, 1e-08, 0.01, 10

LANES = 128
G_LRU, G_MLA, G_DIL, G_MRG = 0, 2048, 3072, 8192
IN_PAD = 11264
KR_OFF = 2432

NN = (((1,), (0,)), ((), ()))
NT = (((1,), (1,)), ((), ()))
TN = (((0,), (0,)), ((), ()))
NEG = -1e30
MESH = pl.DeviceIdType.MESH
VMEM_LIMIT = 48 * 2**20


def _cparams(sem):
    return pltpu.CompilerParams(dimension_semantics=sem, vmem_limit_bytes=VMEM_LIMIT)


def _dot(a, b, dims):
    return lax.dot_general(a.astype(MXU_DTYPE), b.astype(MXU_DTYPE), dims, preferred_element_type=F32)


@jax.custom_vjp
def mm(a, w):
    return _dot(a, w, NN)


def _mm_fwd(a, w):
    return _dot(a, w, NN), (a, w)


def _mm_bwd(res, g):
    a, w = res
    return _dot(g, w, NT), _dot(a, g, TN)


mm.defvjp(_mm_fwd, _mm_bwd)


def _seg64_matrix():
    r = lax.broadcasted_iota(jnp.int32, (LANES, LANES), 0) < DIL_HD
    c = lax.broadcasted_iota(jnp.int32, (LANES, LANES), 1) < DIL_HD
    return (r == c).astype(BF16)


def _seg_sum_impl(x):
    b = _seg64_matrix()
    hi = x.astype(BF16)
    r1 = x - hi.astype(F32)
    mid = r1.astype(BF16)
    lo = (r1 - mid.astype(F32)).astype(BF16)
    dot = lambda u: lax.dot_general(u, b, NN, preferred_element_type=F32)
    return dot(hi) + dot(mid) + dot(lo)


@jax.custom_vjp
def seg_sum(x):
    return _seg_sum_impl(x)


seg_sum.defvjp(lambda x: (_seg_sum_impl(x), None), lambda _, g: (_seg_sum_impl(g),))


def _lroll_impl(x, s):
    return pltpu.roll(x, s % LANES, 1)


@functools.partial(jax.custom_vjp, nondiff_argnums=(1,))
def lroll(x, s):
    return _lroll_impl(x, s)


lroll.defvjp(lambda x, s: (_lroll_impl(x, s), None), lambda s, _, g: (_lroll_impl(g, -s),))


class _Ops:
    def __init__(self, diff):
        self.mm = mm if diff else (lambda a, w: _dot(a, w, NN))
        self.seg_sum = seg_sum if diff else _seg_sum_impl
        self.lroll = lroll if diff else _lroll_impl


PLAIN, DIFF = _Ops(False), _Ops(True)


def rms(x, g, n):
    ms = jnp.sum(x * x, axis=-1, keepdims=True) * (1.0 / n)
    return x * lax.rsqrt(ms + EPS) * g


def rope(ops, x, c, s1, s2, half):
    return x * c + ops.lroll(x, -half) * s1 + ops.lroll(x, half) * s2


def _sigmoid(x):
    return 1.0 / (1.0 + jnp.exp(-x))


def _silu_and_grad(g):
    sg = _sigmoid(g)
    return g * sg, sg * (1.0 + g * (1.0 - sg))


def _softplus(x):
    return jnp.maximum(x, 0.0) + jnp.log(1.0 + jnp.exp(-jnp.abs(x)))


def _expm1(y):
    series = y * (1.0 + y * (0.5 + y * (1.0 / 6.0 + y * (1.0 / 24.0 + y * (1.0 / 120.0)))))
    return jnp.where(y > -0.05, series, jnp.exp(jnp.minimum(y, -0.05)) - 1.0)


def _host(body, n_in, n_out, n_scr, hook, grid):
    if hook is None:
        return body, [], [], [], [], []
    ins, out_shapes, n_sem, make = hook
    ni, no = len(ins), len(out_shapes)
    hbm = pl.BlockSpec(memory_space=pltpu.HBM)

    def at(corner):
        cond = None
        for axis, size in enumerate(grid):
            here = pl.program_id(axis) == (size - 1 if corner else 0)
            cond = here if cond is None else cond & here
        return cond

    def hosted(*refs):
        a, refs = refs[:n_in], refs[n_in:]
        xi, refs = refs[:ni], refs[ni:]
        o, refs = refs[:n_out], refs[n_out:]
        xo, refs = refs[:no], refs[no:]
        scr, (send, recv) = refs[:n_scr], refs[n_scr:]
        start, finish = make(xi, xo, send, recv)
        pl.when(at(False))(start)
        body(*a, *o, *scr)
        pl.when(at(True))(finish)

    sems = [pltpu.SemaphoreType.DMA((n_sem,)), pltpu.SemaphoreType.DMA((n_sem,))]
    return hosted, list(ins), [hbm] * ni, [hbm] * no, list(out_shapes), sems


def _mm_call(name, a, b, *, mode, m, n, k, a_blk, b_blk, out_dtype, add, tm, tn, tk, comm=None, transpose_out=False):
    nk = k // tk
    assert m % tm == 0 and n % tn == 0 and k % tk == 0, (name, m, n, k, tm, tn, tk)
    dims = {"nn": NN, "nt": NT, "tn": TN}[mode]

    def body(*refs):
        if add is None:
            a_ref, b_ref, o_ref, *scr = refs
            add_ref = None
        else:
            a_ref, b_ref, add_ref, o_ref, *scr = refs
        part = _dot(a_ref[...], b_ref[...], dims)

        def finish(acc):
            if add_ref is not None:
                acc = acc + add_ref[...]
            o_ref[...] = (acc.T if transpose_out else acc).astype(o_ref.dtype)

        if nk == 1:
            finish(part)
        else:
            (acc_ref,) = scr
            kk = pl.program_id(2)

            @pl.when(kk == 0)
            def _():
                acc_ref[...] = part

            @pl.when(kk > 0)
            def _():
                acc_ref[...] += part

            @pl.when(kk == nk - 1)
            def _():
                finish(acc_ref[...])

    in_specs = [a_blk, b_blk]
    args = [a, b]
    if add is not None:
        in_specs.append(pl.BlockSpec((tm, tn), lambda j, i, kk: (i, j)))
        args.append(add)
    grid = (n // tn, m // tm, nk)
    scratch = [] if nk == 1 else [pltpu.VMEM((tm, tn), F32)]
    body, x_in, x_in_specs, x_out_specs, x_out_shapes, sems = _host(body, len(args), 1, len(scratch), comm, grid)
    res = pl.pallas_call(
        body,
        name=name,
        grid=grid,
        in_specs=in_specs + x_in_specs,
        out_specs=[pl.BlockSpec((tn, tm), lambda j, i, kk: (j, i)) if transpose_out
                   else pl.BlockSpec((tm, tn), lambda j, i, kk: (i, j))] + x_out_specs,
        out_shape=[jax.ShapeDtypeStruct((n, m) if transpose_out else (m, n), out_dtype)] + x_out_shapes,
        scratch_shapes=scratch + sems,
        compiler_params=_cparams(("arbitrary",) * 3 if comm is not None else ("parallel", "parallel", "arbitrary")),
    )(*args, *x_in)
    return res[0] if comm is None else res


def mm_nn(name, a, b, *, n=None, n_off=0, out_dtype=F32, add=None, tm=512, tn=1024, tk=1024, transpose_out=False):
    m, k = a.shape
    n = b.shape[1] if n is None else n
    tm, tn, tk = min(tm, m), min(tn, n), min(k, tk)
    ob = n_off // tn
    assert n_off % tn == 0
    return _mm_call(name, a, b, mode="nn", m=m, n=n, k=k, out_dtype=out_dtype, add=add, tm=tm, tn=tn, tk=tk,
                    transpose_out=transpose_out,
                    a_blk=pl.BlockSpec((tm, tk), lambda j, i, kk: (i, kk)),
                    b_blk=pl.BlockSpec((tk, tn), lambda j, i, kk: (kk, j + ob)))


def mm_nt(name, a, b, *, out_dtype=F32, tm=512, tn=1024, tk=1024, comm=None):
    m, k = a.shape
    n = b.shape[0]
    tn, tk = min(tn, n), min(k, tk)
    return _mm_call(name, a, b, mode="nt", m=m, n=n, k=k, out_dtype=out_dtype, add=None, tm=tm, tn=tn, tk=tk, comm=comm,
                    a_blk=pl.BlockSpec((tm, tk), lambda j, i, kk: (i, kk)),
                    b_blk=pl.BlockSpec((tn, tk), lambda j, i, kk: (j, kk)))


DW_TILES = dict(tm=1024, tn=512, tk=4096)


def rowwise(name, fn, t, *, tm=256, ncol=1, ins=(), outs=(), touts=(), accs=()):
    n_in, n_out, n_acc = len(ins), len(outs) + len(touts), len(accs)

    def zero_when(ref, cond):
        @pl.when(cond)
        def _():
            ref[...] = jnp.zeros(ref.shape, ref.dtype)

    def body(*refs):
        in_refs, out_refs, acc_refs = refs[:n_in], refs[n_in:n_in + n_out], refs[n_in + n_out:]
        j, i = pl.program_id(0), pl.program_id(1)
        for ref, (_, _, _, cd) in zip(acc_refs, accs):
            zero_when(ref, (i == 0) if cd else ((i == 0) & (j == 0)))
        fn(in_refs, out_refs, acc_refs)

    def in_spec(arr, w, base, cd, rd):
        rows = tm if rd else arr.shape[0]
        return pl.BlockSpec((rows, w), lambda j, i: (i if rd else 0, base + (j if cd else 0)))

    in_specs = [in_spec(*e) for e in ins]
    out_specs = [pl.BlockSpec((tm, w), (lambda j, i, cd=cd: (i, j if cd else 0))) for (_, w, _, cd) in outs]
    out_specs += [pl.BlockSpec((w, tm), (lambda j, i, cd=cd: (j if cd else 0, i))) for (_, w, _, cd) in touts]
    out_specs += [pl.BlockSpec((r, w), (lambda j, i, cd=cd: (0, j if cd else 0))) for (r, _, w, cd) in accs]
    out_shape = [jax.ShapeDtypeStruct((t, c), dt) for (c, _, dt, _) in outs]
    out_shape += [jax.ShapeDtypeStruct((r, t), dt) for (r, _, dt, _) in touts]
    out_shape += [jax.ShapeDtypeStruct((r, c), F32) for (r, c, _, _) in accs]
    res = pl.pallas_call(
        body,
        name=name,
        grid=(ncol, t // tm),
        in_specs=in_specs,
        out_specs=out_specs,
        out_shape=out_shape,
        compiler_params=_cparams(("arbitrary", "arbitrary") if accs else ("parallel", "parallel")),
    )(*[e[0] for e in ins])
    return res


def _row(arr, w=None, base=0, cd=False):
    return (arr, arr.shape[1] if w is None else w, base, cd, True)


def _const(arr, w=None, base=0, cd=False):
    return (arr, arr.shape[1] if w is None else w, base, cd, False)


def rope_tables(positions):
    t = positions.shape[0]

    def fn(ins, outs, _):
        pos = ins[0][...].astype(F32)
        lane = lax.broadcasted_iota(jnp.int32, (1, LANES), 1)
        log_theta = jnp.log(jnp.float32(ROPE_THETA))
        jm = lane - MLA_NOPE
        idx = jnp.clip(jnp.where(jm < 16, jm, jm - 16), 0, 15).astype(F32)
        ang = pos * jnp.exp(-(idx * (2.0 / MLA_ROPE)) * log_theta)
        cos, sin = jnp.cos(ang), jnp.sin(ang)
        in_rope = (lane >= MLA_NOPE) & (lane < MLA_QK)
        outs[0][:, 0:128] = jnp.where(lane < MLA_NOPE, 1.0, jnp.where(in_rope, cos, 0.0))
        outs[0][:, 128:256] = jnp.where(in_rope & (jm < 16), -sin, 0.0)
        outs[0][:, 256:384] = jnp.where(in_rope & (jm >= 16), sin, 0.0)
        jd = lane & (DIL_HD - 1)
        idx = (jd & 31).astype(F32)
        ang = pos * jnp.exp(-(idx * (2.0 / DIL_HD)) * log_theta)
        cos, sin = jnp.cos(ang), jnp.sin(ang)
        outs[0][:, 384:512] = cos
        outs[0][:, 512:640] = jnp.where(jd < 32, -sin, 0.0)
        outs[0][:, 640:768] = jnp.where(jd >= 32, sin, 0.0)

    return rowwise("rope_tables", fn, t, ins=[_row(positions)], outs=[(768, 768, F32, False)])[0]


def rmsnorm_fwd(x, g):
    def fn(ins, outs, _):
        h = rms(ins[0][...], ins[1][...], D_MODEL)
        outs[0][...] = h.astype(MXU_DTYPE)
        outs[1][...] = h.T.astype(MXU_DTYPE)

    return rowwise("rmsnorm_fwd", fn, x.shape[0], ins=[_row(x), _const(g)], outs=[(D_MODEL, D_MODEL, MXU_DTYPE, False)],
                   touts=[(D_MODEL, D_MODEL, MXU_DTYPE, False)])


def rmsnorm_bwd(x, dh, dres, g):
    def fn(ins, outs, accs):
        _, vjp = jax.vjp(lambda xv, gv: rms(xv, gv, D_MODEL), ins[0][...], ins[3][...])
        dx, dg = vjp(ins[1][...])
        outs[0][...] = ins[2][...] + dx
        accs[0][...] += dg

    return rowwise("rmsnorm_bwd", fn, x.shape[0], ins=[_row(x), _row(dh), _row(dres), _const(g)],
                   outs=[(D_MODEL, D_MODEL, F32, False)], accs=[(1, D_MODEL, D_MODEL, False)])


def loss_head(y, target):
    def fn(ins, outs, accs):
        err = ins[0][...] - ins[1][...]
        outs[0][...] = err * (1.0 / D_MODEL)
        accs[0][...] += jnp.sum(err * err, axis=0, keepdims=True)
        accs[1][...] = jnp.broadcast_to(jnp.sum(accs[0][...], keepdims=True), (1, LANES))

    dy, _, tot = rowwise("loss_head", fn, y.shape[0], ins=[_row(y), _row(target)], outs=[(D_MODEL, D_MODEL, F32, False)],
                         accs=[(1, D_MODEL, D_MODEL, False), (1, LANES, LANES, False)])
    return dy, tot[0, 0] * (0.5 / D_MODEL)


def _shift_rows(v, d, fill, reverse):
    tb = v.shape[0]
    if d % 8 == 0:
        pad = jnp.full((d, v.shape[1]), fill, v.dtype)
        return jnp.concatenate([v[d:], pad] if reverse else [pad, v[:tb - d]], axis=0)
    rows = lax.broadcasted_iota(jnp.int32, v.shape, 0)
    if not reverse:
        return jnp.where(rows >= d, pltpu.roll(v, d, 0), fill)
    return jnp.where(rows < tb - d, pltpu.roll(v, tb - d, 0), fill)


def _scan_tile(a, b, reverse):
    d = 1
    while d < a.shape[0]:
        b = b + a * _shift_rows(b, d, 0.0, reverse)
        a = a * _shift_rows(a, d, 1.0, reverse)
        d *= 2
    return a, b


def _lru_gates(ops, xc, wgx, bgx, wga, bga, lam):
    gx = _sigmoid(ops.mm(xc, wgx) + bgx)
    ga = _sigmoid(ops.mm(xc, wga) + bga)
    log_a = -LRU_C * ga * _softplus(-lam)
    a = jnp.exp(log_a)
    mult = jnp.sqrt(-_expm1(2.0 * log_a))
    return a, mult * (gx * xc)


def _shifted_inputs(x, halo, tb):
    rows = lax.broadcasted_iota(jnp.int32, x.shape, 0)
    pad = jnp.zeros((tb - 8, LANES), F32)
    out = []
    for d in (3, 2, 1):
        head = jnp.concatenate([pltpu.roll(halo, d, 0), pad], axis=0)
        out.append(jnp.where(rows >= d, pltpu.roll(x, d, 0), head))
    return out + [x]


def _lru_specs(nt, tb, reverse):
    hb = tb // 8
    tt = (lambda t: nt - 1 - t) if reverse else (lambda t: t)
    blk = lambda off: pl.BlockSpec((tb, LANES), lambda n, t: (tt(t), n + off))
    halo = lambda off: pl.BlockSpec((8, LANES), lambda n, t: (jnp.maximum(tt(t) * hb - 1, 0), n + off))
    chan = lambda r: pl.BlockSpec((r, LANES), lambda n, t: (0, n))
    wblk = pl.BlockSpec((None, LANES, LANES), lambda n, t: (n, 0, 0))
    bblk = pl.BlockSpec((None, 1, LANES), lambda n, t: (n, 0, 0))
    return blk, halo, chan, wblk, bblk


def lru_fwd(z, conv_w, conv_b, wgx, bgx, wga, bga, lam, *, tb=256, comm=None):
    t = z.shape[0]
    nt = t // tb
    blk, halo, chan, wblk, bblk = _lru_specs(nt, tb, False)

    def body(x_ref, xh_ref, g_ref, cw_ref, cb_ref, wgx_ref, bgx_ref, wga_ref, bga_ref, lam_ref, h_ref, y_ref, yt_ref, carry_ref):
        ti = pl.program_id(1)

        @pl.when(ti == 0)
        def _():
            carry_ref[...] = jnp.zeros((8, LANES), F32)

        x = x_ref[...]
        hal = jnp.where(ti > 0, xh_ref[...], 0.0)
        xs = _shifted_inputs(x, hal, tb)
        xc = cb_ref[...] + sum(xs[k] * cw_ref[k:k + 1, :] for k in range(4))
        a, b = _lru_gates(PLAIN, xc, wgx_ref[...], bgx_ref[...], wga_ref[...], bga_ref[...], lam_ref[...])
        acum, h0 = _scan_tile(a, b, False)
        h = h0 + acum * carry_ref[7:8, :]
        carry_ref[...] = h[tb - 8:tb, :]
        h_ref[...] = h
        y = h * _silu_and_grad(g_ref[...])[0]
        y_ref[...] = y.astype(MXU_DTYPE)
        yt_ref[...] = y.T.astype(MXU_DTYPE)

    grid = (LRU_BLOCKS, nt)
    body, x_in, x_in_specs, x_out_specs, x_out_shapes, sems = _host(body, 10, 3, 1, comm, grid)
    return pl.pallas_call(
        body,
        name="lru_fwd" if comm is None else "lru_fwd_comm",
        grid=grid,
        in_specs=[blk(0), halo(0), blk(LRU_BLOCKS), chan(4), chan(1), wblk, bblk, wblk, bblk, chan(1)] + x_in_specs,
        out_specs=[blk(0), blk(0), pl.BlockSpec((LANES, tb), lambda n, t_: (n, t_))] + x_out_specs,
        out_shape=[jax.ShapeDtypeStruct((t, 1024), F32), jax.ShapeDtypeStruct((t, 1024), MXU_DTYPE),
                   jax.ShapeDtypeStruct((1024, t), MXU_DTYPE)] + x_out_shapes,
        scratch_shapes=[pltpu.VMEM((8, LANES), F32)] + sems,
        compiler_params=_cparams(("arbitrary", "arbitrary") if comm is not None else ("parallel", "arbitrary")),
    )(z, z, z, conv_w, conv_b, wgx, bgx, wga, bga, lam, *x_in)


def lru_bwd(z, h, dy, conv_w, conv_b, wgx, bgx, wga, bga, lam, *, tb=256, comm=None):
    t = z.shape[0]
    nt = t // tb
    blk, halo, chan, wblk, bblk = _lru_specs(nt, tb, True)

    def body(x_ref, xh_ref, g_ref, h_ref, hh_ref, dy_ref, cw_ref, cb_ref, wgx_ref, bgx_ref, wga_ref, bga_ref, lam_ref,
             dzx_ref, dzg_ref, dwgx_ref, dbgx_ref, dwga_ref, dbga_ref, dlam_ref, dcw_ref, dcb_ref,
             gcar_ref, acar_ref, xcar_ref):
        ti = pl.program_id(1)
        has_earlier = ti < nt - 1

        @pl.when(ti == 0)
        def _():
            for ref in (dwgx_ref, dbgx_ref, dwga_ref, dbga_ref, dlam_ref, dcw_ref, dcb_ref, gcar_ref, acar_ref, xcar_ref):
                ref[...] = jnp.zeros(ref.shape, F32)

        rows = lax.broadcasted_iota(jnp.int32, (tb, LANES), 0)
        x = x_ref[...]
        hal = jnp.where(has_earlier, xh_ref[...], 0.0)
        xs = _shifted_inputs(x, hal, tb)
        xc = cb_ref[...] + sum(xs[k] * cw_ref[k:k + 1, :] for k in range(4))
        (a, _), vjp = jax.vjp(functools.partial(_lru_gates, DIFF), xc, wgx_ref[...], bgx_ref[...], wga_ref[...],
                              bga_ref[...], lam_ref[...])
        g, h, dyv = g_ref[...], h_ref[...], dy_ref[...]
        silu, dsilu = _silu_and_grad(g)
        dzg_ref[...] = (dyv * h * dsilu).astype(MXU_DTYPE)
        a_next = jnp.where(rows < tb - 1, pltpu.roll(a, tb - 1, 0), acar_ref[0:1, :])
        acum, g0 = _scan_tile(a_next, dyv * silu, True)
        gt = g0 + acum * gcar_ref[0:1, :]
        h_prev = jnp.where(rows >= 1, pltpu.roll(h, 1, 0), jnp.where(has_earlier, hh_ref[7:8, :], 0.0))
        dxc, dwgx, dbgx, dwga, dbga, dlam = vjp((gt * h_prev, gt))
        later = xcar_ref[...]
        gcar_ref[...] = gt[0:8, :]
        acar_ref[...] = a[0:8, :]
        xcar_ref[...] = dxc[0:8, :]
        dx = dxc * cw_ref[3:4, :]
        pad = jnp.zeros((tb - 8, LANES), F32)
        for d in (1, 2, 3):
            tail = jnp.concatenate([pad, pltpu.roll(later, 8 - d, 0)], axis=0)
            up = jnp.where(rows < tb - d, pltpu.roll(dxc, tb - d, 0), tail)
            dx = dx + up * cw_ref[3 - d:4 - d, :]
        dzx_ref[...] = dx.astype(MXU_DTYPE)
        for k in range(4):
            dcw_ref[k:k + 1, :] += jnp.sum(dxc * xs[k], axis=0, keepdims=True)
        dcb_ref[...] += jnp.sum(dxc, axis=0, keepdims=True)
        dwgx_ref[...] += dwgx
        dbgx_ref[...] += dbgx
        dwga_ref[...] += dwga
        dbga_ref[...] += dbga
        dlam_ref[...] += dlam

    sds = jax.ShapeDtypeStruct
    grid = (LRU_BLOCKS, nt)
    body, x_in, x_in_specs, x_out_specs, x_out_shapes, sems = _host(body, 13, 9, 3, comm, grid)
    return pl.pallas_call(
        body,
        name="lru_bwd" if comm is None else "lru_bwd_comm",
        grid=grid,
        in_specs=[blk(0), halo(0), blk(LRU_BLOCKS), blk(0), halo(0), blk(0), chan(4), chan(1), wblk, bblk, wblk, bblk, chan(1)]
        + x_in_specs,
        out_specs=[blk(0), blk(0), wblk, bblk, wblk, bblk, chan(1), chan(4), chan(1)] + x_out_specs,
        out_shape=[sds((t, 1024), MXU_DTYPE), sds((t, 1024), MXU_DTYPE), sds(wgx.shape, F32), sds(bgx.shape, F32), sds(wga.shape, F32),
                   sds(bga.shape, F32), sds((1, 1024), F32), sds((4, 1024), F32), sds((1, 1024), F32)] + x_out_shapes,
        scratch_shapes=[pltpu.VMEM((8, LANES), F32)] * 3 + sems,
        compiler_params=_cparams(("arbitrary", "arbitrary") if comm is not None else ("parallel", "arbitrary")),
    )(z, z, z, h, h, dy, conv_w, conv_b, wgx, bgx, wga, bga, lam, *x_in)


def _mla_prep_tile(ops, cq, ckv, krp, c, s1, s2, g_cq, g_ckv, wq, wk, wv, gq, gk):
    cqn = rms(cq, g_cq, 256)
    ckvn = rms(ckv, g_ckv, 128)
    v = ops.mm(ckvn, wv)
    qs, ks = [], []
    for hd in range(MLA_HEADS):
        q = rms(ops.mm(cqn, wq[hd]), gq, MLA_QK)
        k = rms(ops.mm(ckvn, wk[hd]) + krp, gk, MLA_QK)
        qs.append(rope(ops, q, c, s1, s2, 16) * (MLA_QK ** -0.5))
        ks.append(rope(ops, k, c, s1, s2, 16))
    return tuple(qs), tuple(ks), v


def _mla_prep_args(ins):
    z_cq, z_ckv, z_kr, tc, ts1, ts2, g_cq, g_ckv, wq, wk, wv, gq, gk = ins[:13]
    heads = lambda w: tuple(w[:, LANES * hd:LANES * (hd + 1)] for hd in range(MLA_HEADS))
    return (z_cq[...], z_ckv[...], z_kr[...], tc[...], ts1[...], ts2[...], g_cq[...], g_ckv[...], heads(wq), heads(wk),
            wv[...], gq[...], gk[...])


def _mla_prep_ins(z, tabs, w):
    return [_row(z, 256, 0), _row(z, 128, 2), _row(z, 128, 3), _row(tabs, 128, 0), _row(tabs, 128, 1), _row(tabs, 128, 2),
            _const(w["cq_norm_g"]), _const(w["ckv_norm_g"]), _const(w["wq"]), _const(w["wk"]), _const(w["wv"]),
            _const(w["gq"]), _const(w["gk"])]


def mla_prep_fwd(z, tabs, w):
    def fn(ins, outs, _):
        qs, ks, v = _mla_prep_tile(PLAIN, *_mla_prep_args(ins))
        for hd in range(MLA_HEADS):
            outs[0][:, LANES * hd:LANES * (hd + 1)] = qs[hd].astype(MXU_DTYPE)
            outs[1][:, LANES * hd:LANES * (hd + 1)] = ks[hd].astype(MXU_DTYPE)
        outs[2][...] = v.astype(MXU_DTYPE)
        outs[3][...] = v.T.astype(MXU_DTYPE)

    return rowwise("mla_prep_fwd", fn, z.shape[0], ins=_mla_prep_ins(z, tabs, w),
                   outs=[(1024, 1024, MXU_DTYPE, False), (1024, 1024, MXU_DTYPE, False), (512, 512, MXU_DTYPE, False)],
                   touts=[(512, 512, MXU_DTYPE, False)])


def mla_prep_bwd(z, tabs, w, dq, dk, dv):
    def fn(ins, outs, accs):
        args = _mla_prep_args(ins)
        _, vjp = jax.vjp(functools.partial(_mla_prep_tile, DIFF), *args)
        heads = lambda ref: tuple(ref[:, LANES * hd:LANES * (hd + 1)] for hd in range(MLA_HEADS))
        dcq, dckv, dkr, _, _, _, dg_cq, dg_ckv, dwq, dwk, dwv, dgq, dgk = vjp((heads(ins[13]), heads(ins[14]), ins[15][...]))
        lane = lax.broadcasted_iota(jnp.int32, (1, LANES), 1)
        outs[0][:, 0:256] = dcq.astype(MXU_DTYPE)
        outs[0][:, 256:384] = dckv.astype(MXU_DTYPE)
        outs[0][:, 384:512] = jnp.where((lane >= MLA_NOPE) & (lane < MLA_QK), dkr, 0.0).astype(MXU_DTYPE)
        accs[0][...] += dg_cq
        accs[1][...] += dg_ckv
        for hd in range(MLA_HEADS):
            accs[2][:, LANES * hd:LANES * (hd + 1)] += dwq[hd]
            accs[3][:, LANES * hd:LANES * (hd + 1)] += dwk[hd]
        accs[4][...] += dwv
        accs[5][...] += dgq
        accs[6][...] += dgk

    return rowwise("mla_prep_bwd", fn, z.shape[0], ins=_mla_prep_ins(z, tabs, w) + [_row(dq), _row(dk), _row(dv)],
                   outs=[(512, 512, MXU_DTYPE, False)],
                   accs=[(1, 256, 256, False), (1, 128, 128, False), (256, 1024, 1024, False), (128, 1024, 1024, False),
                         (128, 512, 512, False), (1, 128, 128, False), (1, 128, 128, False)])


def _head_masks():
    lane = lax.broadcasted_iota(jnp.int32, (1, LANES), 1)
    return (lane < DIL_HD, lane >= DIL_HD)


def _row_scalar(tile, mask):
    return jnp.max(jnp.where(mask, tile, -jnp.inf), axis=-1, keepdims=True)


def _causal_tiles(nq, by_key):
    pairs = [(i, j) for i in range(nq) for j in range(i + 1)]
    if by_key:
        pairs.sort(key=lambda ij: (ij[1], ij[0]))
    return (jnp.asarray([ij[0] for ij in pairs], jnp.int32), jnp.asarray([ij[1] for ij in pairs], jnp.int32))


def mla_attn_fwd(q, k, vt, z, *, tq=256, gather=None):
    t = q.shape[0]
    nq = t // tq
    it, jt = _causal_tiles(nq, False)
    hps, wq, wv = MLA_HPS, LANES * MLA_HPS, 64 * MLA_HPS
    ws, layer = gather if gather is not None else ([], 0)
    ng = len(ws)
    ngrp, nsteps = MLA_HEADS // hps, int(it.shape[0])

    def body(it_ref, jt_ref, q_ref, k_ref, vt_ref, g_ref, *rest):
        w_refs, rest = rest[:ng], rest[ng:]
        o_ref, y_ref, yt_ref, lse_ref = rest[:4]
        ga_refs, rest = rest[4:4 + ng], rest[4 + ng:]
        m_scr, l_scr, acc_scr = rest[:3]
        step = pl.program_id(1)
        i, j = it_ref[step], jt_ref[step]
        if ng:
            gather_start, gather_finish = _layer_gather(w_refs, ga_refs, rest[3], rest[4], layer)

            @pl.when((pl.program_id(0) == 0) & (step == 0))
            def _():
                gather_start()

        @pl.when(j == 0)
        def _():
            m_scr[...] = jnp.full(m_scr.shape, NEG, F32)
            l_scr[...] = jnp.zeros(l_scr.shape, F32)
            acc_scr[...] = jnp.zeros(acc_scr.shape, F32)

        def update(diagonal):
            heads = range(hps)
            lanes = [slice(LANES * hh, LANES * (hh + 1)) for hh in heads]
            rows = [slice(64 * hh, 64 * (hh + 1)) for hh in heads]
            sts = [_dot(k_ref[:, lanes[hh]], q_ref[:, lanes[hh]], NT) for hh in heads]
            m_prev = [m_scr[hh:hh + 1, :] for hh in heads]
            l_prev = [l_scr[hh:hh + 1, :] for hh in heads]
            acc_prev = [acc_scr[rows[hh], :] for hh in heads]
            m_new, l_new, acc_new = [], [], []
            for hh in heads:
                st = sts[hh]
                if diagonal:
                    key = lax.broadcasted_iota(jnp.int32, (tq, tq), 0)
                    qry = lax.broadcasted_iota(jnp.int32, (tq, tq), 1)
                    st = jnp.where(key <= qry, st, NEG)
                m_new.append(jnp.maximum(m_prev[hh], jnp.max(st, axis=0, keepdims=True)))
                alpha = jnp.exp(m_prev[hh] - m_new[hh])
                pt = jnp.exp(st - m_new[hh])
                l_new.append(alpha * l_prev[hh] + jnp.sum(pt, axis=0, keepdims=True))
                acc_new.append(alpha * acc_prev[hh] + _dot(vt_ref[rows[hh], :], pt, NN))
            for hh in heads:
                m_scr[hh:hh + 1, :] = m_new[hh]
                l_scr[hh:hh + 1, :] = l_new[hh]
                acc_scr[rows[hh], :] = acc_new[hh]

        @pl.when(j < i)
        def _():
            update(False)

        @pl.when(j == i)
        def _():
            update(True)
            lse_ref[...] = jnp.zeros(lse_ref.shape, F32)
            for hh in range(hps):
                rows = slice(64 * hh, 64 * (hh + 1))
                acc_scr[rows, :] = acc_scr[rows, :] / l_scr[hh:hh + 1, :]
                lse_ref[hh:hh + 1, :] = m_scr[hh:hh + 1, :] + jnp.log(l_scr[hh:hh + 1, :])
            o = acc_scr[...].T
            o_ref[...] = o
            y = o * _silu_and_grad(g_ref[...])[0]
            y_ref[...] = y.astype(MXU_DTYPE)
            yt_ref[...] = y.T.astype(MXU_DTYPE)

        if ng:
            @pl.when((pl.program_id(0) == ngrp - 1) & (step == nsteps - 1))
            def _():
                gather_finish()

    qo = lambda w, off=0: pl.BlockSpec((tq, w), lambda p, s, it_, jt_: (it_[s], p + off))
    sds = jax.ShapeDtypeStruct
    comm_scratch = [pltpu.SemaphoreType.DMA((6 * ng,)), pltpu.SemaphoreType.DMA((6 * ng,))] if ng else []
    return pl.pallas_call(
        body,
        name="mla_attn_fwd_gather" if ng else "mla_attn_fwd",
        grid_spec=pltpu.PrefetchScalarGridSpec(
            num_scalar_prefetch=2,
            grid=(ngrp, nsteps),
            in_specs=[qo(wq), pl.BlockSpec((tq, wq), lambda p, s, it_, jt_: (jt_[s], p)),
                      pl.BlockSpec((wv, tq), lambda p, s, it_, jt_: (p, jt_[s])), qo(wv, 512 // wv)] + [HBM_SPEC] * ng,
            out_specs=[qo(wv), qo(wv), pl.BlockSpec((wv, tq), lambda p, s, it_, jt_: (p, it_[s])),
                       pl.BlockSpec((None, 8, tq), lambda p, s, it_, jt_: (p, 0, it_[s]))] + [HBM_SPEC] * ng,
            scratch_shapes=[pltpu.VMEM((8, tq), F32), pltpu.VMEM((8, tq), F32), pltpu.VMEM((wv, tq), F32)] + comm_scratch,
        ),
        out_shape=[sds((t, 512), F32), sds((t, 512), MXU_DTYPE), sds((512, t), MXU_DTYPE), sds((ngrp, 8, t), F32)]
        + _gather_shapes(ws),
        compiler_params=_cparams(("arbitrary", "arbitrary") if ng else ("parallel", "arbitrary")),
    )(it, jt, q, k, vt, z, *ws)


def mla_attn_bwd(q, k, v, do, lse, dd, *, tq=256, chips=None):
    t = q.shape[0]
    nq = t // tq
    it, jt = _causal_tiles(nq, True)
    hps, wq, wv = MLA_HPS, LANES * MLA_HPS, 64 * MLA_HPS
    pcs, layer = chips if chips is not None else ([], 0)
    nc = len(pcs)
    ngrp, nsteps = MLA_HEADS // hps, int(it.shape[0])

    def body(it_ref, jt_ref, q_ref, k_ref, v_ref, do_ref, lse_ref, d_ref, *rest):
        pc_refs, rest = rest[:nc], rest[nc:]
        dq_ref, dk_ref, dv_ref = rest[:3]
        lb_refs, rest = rest[3:3 + nc], rest[3 + nc:]
        dk_scr, dv_scr = rest[:2]
        step = pl.program_id(1)
        i, j = it_ref[step], jt_ref[step]
        masks = _head_masks()
        if nc:
            chips_start, chips_finish = _chips_exchange(pc_refs, lb_refs, rest[2], rest[3], layer)

            @pl.when((pl.program_id(0) == 0) & (step == 0))
            def _():
                chips_start()

        @pl.when(step == 0)
        def _():
            dq_ref[...] = jnp.zeros(dq_ref.shape, F32)

        @pl.when(i == j)
        def _():
            dk_scr[...] = jnp.zeros(dk_scr.shape, F32)
            dv_scr[...] = jnp.zeros(dv_scr.shape, F32)

        def update(diagonal):
            qrows = pl.ds(pl.multiple_of(i * tq, tq), tq)
            heads = range(hps)
            lanes = [slice(LANES * hh, LANES * (hh + 1)) for hh in heads]
            pair = [slice(LANES * (hh // 2), LANES * (hh // 2 + 1)) for hh in heads]
            qh = [q_ref[:, lanes[hh]] for hh in heads]
            kh = [k_ref[:, lanes[hh]] for hh in heads]
            doh = []
            for hh in heads:
                dov = do_ref[:, pair[hh]]
                doh.append(jnp.where(masks[hh % 2], dov, jnp.zeros_like(dov)))
            sts = [_dot(kh[hh], qh[hh], NT) for hh in heads]
            dps = [_dot(v_ref[:, pair[hh]], doh[hh], NT) for hh in heads]
            lse = [lse_ref[hh:hh + 1, :] for hh in heads]
            ddv = [d_ref[hh:hh + 1, :] for hh in heads]
            dk_new = [dk_scr[:, lanes[hh]] for hh in heads]
            dq_new = [dq_ref[qrows, lanes[hh]] for hh in heads]
            dv_new = [dv_scr[:, pair[2 * pp]] for pp in range(hps // 2)]
            for hh in heads:
                st = sts[hh] - lse[hh]
                if diagonal:
                    key = lax.broadcasted_iota(jnp.int32, (tq, tq), 0)
                    qry = lax.broadcasted_iota(jnp.int32, (tq, tq), 1)
                    st = jnp.where(key <= qry, st, NEG)
                pt = jnp.exp(st)
                dst = pt * (dps[hh] - ddv[hh])
                dv_new[hh // 2] = dv_new[hh // 2] + _dot(pt, doh[hh], NN)
                dk_new[hh] = dk_new[hh] + _dot(dst, qh[hh], NN)
                dq_new[hh] = dq_new[hh] + _dot(dst, kh[hh], TN)
            for hh in heads:
                dk_scr[:, lanes[hh]] = dk_new[hh]
                dq_ref[qrows, lanes[hh]] = dq_new[hh]
            for pp in range(hps // 2):
                dv_scr[:, pair[2 * pp]] = dv_new[pp]

        @pl.when(j < i)
        def _():
            update(False)

        @pl.when(j == i)
        def _():
            update(True)

        @pl.when(i == nq - 1)
        def _():
            dk_ref[...] = dk_scr[...]
            dv_ref[...] = dv_scr[...]

        if nc:
            @pl.when((pl.program_id(0) == ngrp - 1) & (step == nsteps - 1))
            def _():
                chips_finish()

    qo = lambda w: pl.BlockSpec((tq, w), lambda p, s, it_, jt_: (it_[s], p))
    kv = lambda w: pl.BlockSpec((tq, w), lambda p, s, it_, jt_: (jt_[s], p))
    stat = pl.BlockSpec((None, 8, tq), lambda p, s, it_, jt_: (p, 0, it_[s]))
    sds = jax.ShapeDtypeStruct
    comm_scratch = [pltpu.SemaphoreType.DMA((3 * nc,)), pltpu.SemaphoreType.DMA((3 * nc,))] if nc else []
    return pl.pallas_call(
        body,
        name="mla_attn_bwd_chips" if nc else "mla_attn_bwd",
        grid_spec=pltpu.PrefetchScalarGridSpec(
            num_scalar_prefetch=2,
            grid=(ngrp, nsteps),
            in_specs=[qo(wq), kv(wq), kv(wv), qo(wv), stat, stat] + [HBM_SPEC] * nc,
            out_specs=[pl.BlockSpec((t, wq), lambda p, s, it_, jt_: (0, p)), kv(wq), kv(wv)] + [HBM_SPEC] * nc,
            scratch_shapes=[pltpu.VMEM((tq, wq), F32), pltpu.VMEM((tq, wv), F32)] + comm_scratch,
        ),
        out_shape=[sds((t, 1024), F32), sds((t, 1024), F32), sds((t, 512), F32)] + _chips_shapes(pcs),
        compiler_params=_cparams(("arbitrary", "arbitrary") if nc else ("parallel", "arbitrary")),
    )(it, jt, q, k, v, do, lse, dd, *pcs)


def mla_gate_bwd(dy, o, z, *, tm=256):
    t = dy.shape[0]
    wv = 64 * MLA_HPS

    def body(dy_ref, o_ref, g_ref, do_ref, dzg_ref, dd_ref):
        dyv, ov = dy_ref[...], o_ref[...]
        silu, dsilu = _silu_and_grad(g_ref[...])
        do = dyv * silu
        do_ref[...] = do.astype(MXU_DTYPE)
        dzg_ref[...] = (dyv * ov * dsilu).astype(MXU_DTYPE)
        prod = do * ov
        row = lax.broadcasted_iota(jnp.int32, (8, wv), 0)
        lane = lax.broadcasted_iota(jnp.int32, (8, wv), 1)
        pick = ((lane >= row * DIL_HD) & (lane < (row + 1) * DIL_HD)).astype(BF16)
        hi = prod.astype(BF16)
        r1 = prod - hi.astype(F32)
        mid = r1.astype(BF16)
        lo = (r1 - mid.astype(F32)).astype(BF16)
        dot = lambda u: lax.dot_general(pick, u, NT, preferred_element_type=F32)
        dd_ref[...] = dot(hi) + dot(mid) + dot(lo)

    blk = lambda off=0: pl.BlockSpec((tm, wv), lambda p, i: (i, p + off))
    sds = jax.ShapeDtypeStruct
    return pl.pallas_call(
        body,
        name="mla_gate_bwd",
        grid=(MLA_HEADS // MLA_HPS, t // tm),
        in_specs=[blk(), blk(), blk(512 // wv)],
        out_specs=[blk(), blk(), pl.BlockSpec((None, 8, tm), lambda p, i: (p, 0, i))],
        out_shape=[sds((t, 512), MXU_DTYPE), sds((t, 512), MXU_DTYPE), sds((MLA_HEADS // MLA_HPS, 8, t), F32)],
        compiler_params=_cparams(("parallel", "parallel")),
    )(dy, o, z)


SPAN = 2048
DIL_SCALE = DIL_HD ** -0.5


def _rm_src(u, window, dil):
    nn, r = divmod(u, dil)
    return pl.ds(nn * window + r, DIL_NK, stride=dil) if dil > 1 else pl.ds(u * DIL_NK, DIL_NK)


def _rm_dst(u):
    return pl.ds(u * DIL_NK, DIL_NK)


def _dil_prep_tile(ops, x, g2, c, s1, s2, scale):
    ms = ops.seg_sum(x * x) * (1.0 / DIL_HD)
    return rope(ops, x * lax.rsqrt(ms + EPS) * g2, c, s1, s2, 32) * scale


def _span_blk(base):
    return pl.BlockSpec((SPAN, LANES), lambda s, p: (s, base + p))


def _const_blk(shape):
    return pl.BlockSpec(shape, lambda s, p: (0,) * len(shape))


_DIL_TABLE_SPECS = [pl.BlockSpec((SPAN, LANES), (lambda s, p, blk=blk: (s, blk))) for blk in (3, 4, 5)]


def dil_prep_fwd(z, tabs, gq2, gk2, gi):
    window, dil = DIL_GROUPS[gi]
    t = z.shape[0]

    def body(q_ref, k_ref, v_ref, c_ref, s1_ref, s2_ref, gq_ref, gk_ref, qo_ref, ko_ref, vo_ref):
        for u in range(SPAN // DIL_NK):
            src, dst = _rm_src(u, window, dil), _rm_dst(u)
            c, s1, s2 = c_ref[src, :], s1_ref[src, :], s2_ref[src, :]
            qo_ref[dst, :] = _dil_prep_tile(PLAIN, q_ref[src, :], gq_ref[...], c, s1, s2, DIL_SCALE).astype(MXU_DTYPE)
            ko_ref[dst, :] = _dil_prep_tile(PLAIN, k_ref[src, :], gk_ref[...], c, s1, s2, 1.0).astype(MXU_DTYPE)
            vo_ref[dst, :] = v_ref[src, :].astype(MXU_DTYPE)

    return pl.pallas_call(
        body,
        name=f"dil_prep_fwd_{dil}",
        grid=(t // SPAN, 4),
        in_specs=[_span_blk(4 * gi), _span_blk(12 + 4 * gi), _span_blk(24 + 4 * gi)] + _DIL_TABLE_SPECS
        + [_const_blk((1, LANES)), _const_blk((1, LANES))],
        out_specs=[_span_blk(0)] * 3,
        out_shape=[jax.ShapeDtypeStruct((t, 512), MXU_DTYPE)] * 3,
        compiler_params=_cparams(("parallel", "parallel")),
    )(z, z, z, tabs, tabs, tabs, gq2, gk2)


def dil_prep_bwd(z, tabs, gq2, gk2, dq, dk, dv, gi):
    window, dil = DIL_GROUPS[gi]
    t = z.shape[0]

    def body(q_ref, k_ref, c_ref, s1_ref, s2_ref, gq_ref, gk_ref, dq_ref, dk_ref, dv_ref, dzq_ref, dzk_ref, dzv_ref, dgq_ref,
             dgk_ref, sq, sk, sv):
        @pl.when((pl.program_id(0) == 0) & (pl.program_id(1) == 0))
        def _():
            dgq_ref[...] = jnp.zeros((1, LANES), F32)
            dgk_ref[...] = jnp.zeros((1, LANES), F32)

        dgs = [jnp.zeros((1, LANES), F32), jnp.zeros((1, LANES), F32)]
        for u in range(SPAN // DIL_NK):
            src, dst = _rm_src(u, window, dil), _rm_dst(u)
            c, s1, s2 = c_ref[src, :], s1_ref[src, :], s2_ref[src, :]
            for idx, (x_ref, g_ref, ct_ref, scr, scale) in enumerate(((q_ref, gq_ref, dq_ref, sq, DIL_SCALE),
                                                                      (k_ref, gk_ref, dk_ref, sk, 1.0))):
                _, vjp = jax.vjp(lambda xv, gv, sc=scale: _dil_prep_tile(DIFF, xv, gv, c, s1, s2, sc), x_ref[src, :], g_ref[...])
                dx, dg = vjp(ct_ref[dst, :])
                scr[src, :] = dx
                dgs[idx] = dgs[idx] + dg
            sv[src, :] = dv_ref[dst, :]
        dgq_ref[...] += dgs[0] + pltpu.roll(dgs[0], DIL_HD, 1)
        dgk_ref[...] += dgs[1] + pltpu.roll(dgs[1], DIL_HD, 1)
        for c0 in range(0, SPAN, 256):
            rows = slice(c0, c0 + 256)
            dzq_ref[rows, :] = sq[rows, :].astype(MXU_DTYPE)
            dzk_ref[rows, :] = sk[rows, :].astype(MXU_DTYPE)
            dzv_ref[rows, :] = sv[rows, :].astype(MXU_DTYPE)

    sds = jax.ShapeDtypeStruct
    return pl.pallas_call(
        body,
        name=f"dil_prep_bwd_{dil}",
        grid=(t // SPAN, 4),
        in_specs=[_span_blk(4 * gi), _span_blk(12 + 4 * gi)] + _DIL_TABLE_SPECS
        + [_const_blk((1, LANES)), _const_blk((1, LANES)), _span_blk(0), _span_blk(0), _span_blk(0)],
        out_specs=[_span_blk(0)] * 3 + [_const_blk((1, LANES))] * 2,
        out_shape=[sds((t, 512), MXU_DTYPE)] * 3 + [sds((1, LANES), F32)] * 2,
        scratch_shapes=[pltpu.VMEM((SPAN, LANES), F32)] * 3,
        compiler_params=_cparams(("arbitrary", "arbitrary")),
    )(z, z, tabs, tabs, tabs, gq2, gk2, dq, dk, dv)


def _band_masks():
    qi = lax.broadcasted_iota(jnp.int32, (DIL_NK, DIL_NK), 0)
    ki = lax.broadcasted_iota(jnp.int32, (DIL_NK, DIL_NK), 1)
    return (ki >= qi), (ki <= qi)


def _pair_heads():
    return [(pair, hh) for pair in range(4) for hh in range(2)]


def _pair_lanes(pair):
    return slice(LANES * pair, LANES * (pair + 1))


def _zero_other_head(mask, x):
    return jnp.where(mask, x, jnp.zeros_like(x))


def dil_attn_fwd(name, q, k, v, dil):
    t = q.shape[0]

    def body(q_ref, kp_ref, kc_ref, vp_ref, vc_ref, o_ref, lse_ref):
        b = pl.program_id(0)
        mprev, mcur = _band_masks()
        mprev = mprev & (b >= dil)
        hm = _head_masks()
        heads = _pair_heads()
        qh = [_zero_other_head(hm[hh], q_ref[:, _pair_lanes(pair)]) for pair, hh in heads]
        sps = [_dot(qh[i], kp_ref[:, _pair_lanes(pair)], NT) for i, (pair, _) in enumerate(heads)]
        scs = [_dot(qh[i], kc_ref[:, _pair_lanes(pair)], NT) for i, (pair, _) in enumerate(heads)]
        o = [jnp.zeros((DIL_NK, LANES), F32) for _ in range(4)]
        lse = [jnp.zeros((DIL_NK, LANES), F32) for _ in range(4)]
        for i, (pair, hh) in enumerate(heads):
            sp, sc = jnp.where(mprev, sps[i], NEG), jnp.where(mcur, scs[i], NEG)
            m = jnp.maximum(jnp.max(sp, axis=1, keepdims=True), jnp.max(sc, axis=1, keepdims=True))
            ep, ec = jnp.exp(sp - m), jnp.exp(sc - m)
            den = jnp.sum(ep, axis=1, keepdims=True) + jnp.sum(ec, axis=1, keepdims=True)
            lanes = _pair_lanes(pair)
            oh = _dot(ep, _zero_other_head(hm[hh], vp_ref[:, lanes]), NN) + _dot(ec, _zero_other_head(hm[hh], vc_ref[:, lanes]), NN)
            o[pair] = o[pair] + oh / den
            lse[pair] = jnp.where(hm[hh], m + jnp.log(den), lse[pair])
        for pair in range(4):
            o_ref[:, _pair_lanes(pair)] = o[pair]
            lse_ref[:, _pair_lanes(pair)] = lse[pair]

    cur = pl.BlockSpec((DIL_NK, 512), lambda b: (b, 0))
    prev = pl.BlockSpec((DIL_NK, 512), lambda b: (jnp.maximum(b - dil, 0), 0))
    sds = jax.ShapeDtypeStruct
    return pl.pallas_call(
        body,
        name=name,
        grid=(t // DIL_NK,),
        in_specs=[cur, prev, cur, prev, cur],
        out_specs=[cur, cur],
        out_shape=[sds((t, 512), F32), sds((t, 512), F32)],
        compiler_params=_cparams(("parallel",)),
    )(q, k, k, v, v)


def dil_attn_bwd(name, q, k, v, do, lse, dd, dil):
    t = q.shape[0]
    nblk = t // DIL_NK

    def body(q_ref, do_ref, l_ref, d_ref, kp_ref, kc_ref, vp_ref, vc_ref, dq_ref, dk_ref, dv_ref):
        b = pl.program_id(0)

        @pl.when(b == 0)
        def _():
            dk_ref[...] = jnp.zeros(dk_ref.shape, F32)
            dv_ref[...] = jnp.zeros(dv_ref.shape, F32)

        mprev, mcur = _band_masks()
        mprev = mprev & (b >= dil)
        rows_c = pl.ds(pl.multiple_of(b * DIL_NK, DIL_NK), DIL_NK)
        rows_p = pl.ds(pl.multiple_of(jnp.maximum(b - dil, 0) * DIL_NK, DIL_NK), DIL_NK)
        hm = _head_masks()
        heads = _pair_heads()
        ln_ = [_pair_lanes(pair) for pair, _ in heads]
        qh = [_zero_other_head(hm[hh], q_ref[:, ln_[i]]) for i, (_, hh) in enumerate(heads)]
        doh = [_zero_other_head(hm[hh], do_ref[:, ln_[i]]) for i, (_, hh) in enumerate(heads)]
        idx = range(len(heads))
        s_p = [_dot(qh[i], kp_ref[:, ln_[i]], NT) for i in idx]
        s_c = [_dot(qh[i], kc_ref[:, ln_[i]], NT) for i in idx]
        dp_p = [_dot(doh[i], vp_ref[:, ln_[i]], NT) for i in idx]
        dp_c = [_dot(doh[i], vc_ref[:, ln_[i]], NT) for i in idx]
        zero = lambda: [jnp.zeros((DIL_NK, LANES), F32) for _ in range(4)]
        dq, dk_p, dk_c, dv_p, dv_c = zero(), zero(), zero(), zero(), zero()
        for i, (pair, hh) in enumerate(heads):
            lse_h, d_h = _row_scalar(l_ref[:, ln_[i]], hm[hh]), _row_scalar(d_ref[:, ln_[i]], hm[hh])
            pp = jnp.exp(jnp.where(mprev, s_p[i] - lse_h, NEG))
            pc = jnp.exp(jnp.where(mcur, s_c[i] - lse_h, NEG))
            dsp, dsc = pp * (dp_p[i] - d_h), pc * (dp_c[i] - d_h)
            dq[pair] = (dq[pair] + _dot(dsp, _zero_other_head(hm[hh], kp_ref[:, ln_[i]]), NN)
                        + _dot(dsc, _zero_other_head(hm[hh], kc_ref[:, ln_[i]]), NN))
            dv_p[pair] = dv_p[pair] + _dot(pp, doh[i], TN)
            dv_c[pair] = dv_c[pair] + _dot(pc, doh[i], TN)
            dk_p[pair] = dk_p[pair] + _dot(dsp, qh[i], TN)
            dk_c[pair] = dk_c[pair] + _dot(dsc, qh[i], TN)
        for pair in range(4):
            lanes = _pair_lanes(pair)
            dq_ref[:, lanes] = dq[pair]
            dk_ref[rows_p, lanes] += dk_p[pair]
            dv_ref[rows_p, lanes] += dv_p[pair]
            dk_ref[rows_c, lanes] += dk_c[pair]
            dv_ref[rows_c, lanes] += dv_c[pair]

    cur = pl.BlockSpec((DIL_NK, 512), lambda b: (b, 0))
    prev = pl.BlockSpec((DIL_NK, 512), lambda b: (jnp.maximum(b - dil, 0), 0))
    whole = pl.BlockSpec((t, 512), lambda b: (0, 0))
    sds = jax.ShapeDtypeStruct
    return pl.pallas_call(
        body,
        name=name,
        grid=(nblk,),
        in_specs=[cur, cur, cur, cur, prev, cur, prev, cur],
        out_specs=[cur, whole, whole],
        out_shape=[sds((t, 512), F32)] * 3,
        compiler_params=_cparams(("arbitrary",)),
    )(q, do, lse, dd, k, k, v, v)


def dil_combine(os_, lses, z):
    t = z.shape[0]

    def body(o1_ref, l1_ref, o2_ref, l2_ref, o3_ref, l3_ref, g_ref, y_ref, o_ref, lse_ref, yt_ref, so2, sl2, so3, sl3):
        for (window, dil), o_in, l_in, so, sl in ((DIL_GROUPS[1], o2_ref, l2_ref, so2, sl2), (DIL_GROUPS[2], o3_ref, l3_ref, so3, sl3)):
            for u in range(SPAN // DIL_NK):
                src, dst = _rm_src(u, window, dil), _rm_dst(u)
                so[src, :] = o_in[dst, :]
                sl[src, :] = l_in[dst, :]
        for c0 in range(0, SPAN, 256):
            rows = slice(c0, c0 + 256)
            la, lb, lc = l1_ref[rows, :], sl2[rows, :], sl3[rows, :]
            mx = jnp.maximum(jnp.maximum(la, lb), lc)
            wa, wb, wc = jnp.exp(la - mx), jnp.exp(lb - mx), jnp.exp(lc - mx)
            tot = wa + wb + wc
            o = (wa * o1_ref[rows, :] + wb * so2[rows, :] + wc * so3[rows, :]) / tot
            y = o * _silu_and_grad(g_ref[rows, :])[0]
            y_ref[rows, :] = y.astype(MXU_DTYPE)
            o_ref[rows, :] = o
            lse_ref[rows, :] = mx + jnp.log(tot)
            yt_ref[:, rows] = y.T.astype(MXU_DTYPE)

    sds = jax.ShapeDtypeStruct
    return pl.pallas_call(
        body,
        name="dil_combine",
        grid=(t // SPAN, 4),
        in_specs=[_span_blk(0)] * 6 + [_span_blk(36)],
        out_specs=[_span_blk(0)] * 3 + [pl.BlockSpec((LANES, SPAN), lambda s, p: (p, s))],
        out_shape=[sds((t, 512), MXU_DTYPE), sds((t, 512), F32), sds((t, 512), F32), sds((512, t), MXU_DTYPE)],
        scratch_shapes=[pltpu.VMEM((SPAN, LANES), F32)] * 4,
        compiler_params=_cparams(("parallel", "parallel")),
    )(os_[0], lses[0], os_[1], lses[1], os_[2], lses[2], z)


def dil_gate_bwd(dy, o, z, lse):
    t = dy.shape[0]

    def body(dy_ref, o_ref, g_ref, lse_ref, dzg_ref, do1_ref, dd1_ref, do2_ref, dd2_ref, l2_ref, do3_ref, dd3_ref, l3_ref, do_scr):
        for c0 in range(0, SPAN, 256):
            rows = slice(c0, c0 + 256)
            dyv, ov = dy_ref[rows, :], o_ref[rows, :]
            silu, dsilu = _silu_and_grad(g_ref[rows, :])
            do = dyv * silu
            do_scr[rows, :] = do
            do1_ref[rows, :] = do.astype(MXU_DTYPE)
            dd1_ref[rows, :] = _seg_sum_impl(do * ov)
            dzg_ref[rows, :] = (dyv * ov * dsilu).astype(MXU_DTYPE)
        for (window, dil), do_o, dd_o, l_o in ((DIL_GROUPS[1], do2_ref, dd2_ref, l2_ref), (DIL_GROUPS[2], do3_ref, dd3_ref, l3_ref)):
            for u in range(SPAN // DIL_NK):
                src, dst = _rm_src(u, window, dil), _rm_dst(u)
                do_o[dst, :] = do_scr[src, :].astype(MXU_DTYPE)
                dd_o[dst, :] = dd1_ref[src, :]
                l_o[dst, :] = lse_ref[src, :]

    sds = jax.ShapeDtypeStruct
    f32, mxu = sds((t, 512), F32), sds((t, 512), MXU_DTYPE)
    return pl.pallas_call(
        body,
        name="dil_gate_bwd",
        grid=(t // SPAN, 4),
        in_specs=[_span_blk(0), _span_blk(0), _span_blk(36), _span_blk(0)],
        out_specs=[_span_blk(0)] * 9,
        out_shape=[mxu, mxu, f32, mxu, f32, f32, mxu, f32, f32],
        scratch_shapes=[pltpu.VMEM((SPAN, LANES), F32)],
        compiler_params=_cparams(("parallel", "parallel")),
    )(dy, o, z, lse)


def _merge_tile(p0, p1, p2, z0, z1, z2, b0, b1, b2):
    return _sigmoid(z0 + b0) * p0 + _sigmoid(z1 + b1) * p1 + _sigmoid(z2 + b2) * p2


def _merge_ins(ps, z, b):
    w = 256
    return ([_row(p, w, 0, True) for p in ps] + [_row(z, w, 4 * i, True) for i in range(3)]
            + [_const(b, w, 4 * i, True) for i in range(3)])


def merge_fwd(ps, z, b):
    def fn(ins, outs, _):
        merged = _merge_tile(*[r[...] for r in ins])
        outs[0][...] = merged.astype(MXU_DTYPE)
        outs[1][...] = merged.T.astype(MXU_DTYPE)

    return rowwise("merge_fwd", fn, z.shape[0], ncol=4, ins=_merge_ins(ps, z, b), outs=[(1024, 256, MXU_DTYPE, True)],
                   touts=[(1024, 256, MXU_DTYPE, True)])


def merge_bwd(dm, ps, z, b):
    def fn(ins, outs, accs):
        _, vjp = jax.vjp(_merge_tile, *[r[...] for r in ins[:9]])
        grads = vjp(ins[9][...])
        for i in range(3):
            outs[i][...] = grads[i].astype(MXU_DTYPE)
            outs[3 + i][...] = grads[3 + i].astype(MXU_DTYPE)
            accs[i][...] += grads[6 + i]

    return rowwise("merge_bwd", fn, z.shape[0], ncol=4, ins=_merge_ins(ps, z, b) + [_row(dm, 256, 0, True)],
                   outs=[(1024, 256, MXU_DTYPE, True)] * 6, accs=[(1, 1024, 256, True)] * 3)


EW_BLOCK_BYTES = 2**21


def _tile2d(r, c):
    if r * c * 4 <= EW_BLOCK_BYTES:
        return r, c
    for tr in (512, 256, 128, 64, 32):
        if r % tr == 0 and tr * c * 4 <= EW_BLOCK_BYTES:
            return tr, c
    for tc in (1024, 512, 256, 128):
        if c % tc == 0 and r * tc * 4 <= EW_BLOCK_BYTES:
            return r, tc
    raise ValueError((r, c))


def adamw(name, w, g, m, v):
    shape = w.shape
    c = shape[-1]
    r = w.size // c
    tr, tc = _tile2d(r, c)
    assert tc == c
    c1, c2 = 1.0 - ADAM_B1 ** ADAM_STEP, 1.0 - ADAM_B2 ** ADAM_STEP

    def body(w_ref, g_ref, m_ref, v_ref, d_ref, mo_ref, vo_ref):
        gv = g_ref[...]
        mn = ADAM_B1 * m_ref[...] + (1.0 - ADAM_B1) * gv
        vn = ADAM_B2 * v_ref[...] + (1.0 - ADAM_B2) * (gv * gv)
        d_ref[...] = -ADAM_LR * ((mn / c1) / (jnp.sqrt(vn / c2) + ADAM_EPS) + ADAM_WD * w_ref[...])
        mo_ref[...] = mn
        vo_ref[...] = vn

    spec = pl.BlockSpec((tr, c), lambda i: (i, 0))
    outs = pl.pallas_call(
        body,
        name=name,
        grid=(r // tr,),
        in_specs=[spec] * 4,
        out_specs=[spec] * 3,
        out_shape=[jax.ShapeDtypeStruct((r, c), F32)] * 3,
        compiler_params=_cparams(("parallel",)),
    )(*[a.reshape(r, c) for a in (w, g, m, v)])
    return [o.reshape(shape) for o in outs]


def adamw_pair(name, w, reduced, received, m, v, c_idx):
    _, r, c = w.shape
    tr, tc = _tile2d(r, c)
    c1, c2 = 1.0 - ADAM_B1 ** ADAM_STEP, 1.0 - ADAM_B2 ** ADAM_STEP

    def body(c_ref, w_ref, r0_ref, x0_ref, r1_ref, x1_ref, m_ref, v_ref, g_ref, d_ref, mo_ref, vo_ref):
        mine = c_ref[0]
        gv = jnp.where(pl.program_id(0) == 0, jnp.where(mine == 0, r0_ref[...], x0_ref[...]),
                       jnp.where(mine == 1, r1_ref[...], x1_ref[...]))
        mn = ADAM_B1 * m_ref[...] + (1.0 - ADAM_B1) * gv
        vn = ADAM_B2 * v_ref[...] + (1.0 - ADAM_B2) * (gv * gv)
        g_ref[...] = gv
        d_ref[...] = -ADAM_LR * ((mn / c1) / (jnp.sqrt(vn / c2) + ADAM_EPS) + ADAM_WD * w_ref[...])
        mo_ref[...] = mn
        vo_ref[...] = vn

    full = pl.BlockSpec((None, tr, tc), lambda h, i, j, cr: (h, i, j))
    half = pl.BlockSpec((tr, tc), lambda h, i, j, cr: (i, j))
    return pl.pallas_call(
        body,
        name=name,
        grid_spec=pltpu.PrefetchScalarGridSpec(
            num_scalar_prefetch=1,
            grid=(2, r // tr, c // tc),
            in_specs=[full, half, half, half, half, full, full],
            out_specs=[full] * 4,
        ),
        out_shape=[jax.ShapeDtypeStruct((2, r, c), F32)] * 4,
        compiler_params=_cparams(("parallel", "parallel", "parallel")),
    )(c_idx, w, reduced[0], received[0], reduced[1], received[1], m, v)


def sum_pair(name, g, la, out_dtype):
    _, r, c = g.shape
    tr, tc = _tile2d(r, c)

    def body(g_ref, la_ref, o_ref):
        o_ref[...] = (g_ref[...] + la_ref[...]).astype(o_ref.dtype)

    spec = pl.BlockSpec((None, tr, tc), lambda s, i, j: (s, i, j))
    return pl.pallas_call(
        body,
        name=name,
        grid=(4, r // tr, c // tc),
        in_specs=[spec, spec],
        out_specs=spec,
        out_shape=jax.ShapeDtypeStruct((4, r, c), out_dtype),
        compiler_params=_cparams(("parallel", "parallel", "parallel")),
    )(g, la)


def sum_chips(name, g, la, lb, s_idx):
    _, r, c = g.shape
    tr, tc = _tile2d(r, c)

    def body(s_ref, g_ref, la_ref, l0_ref, l1_ref, l2_ref, o_ref):
        own = g_ref[...] + la_ref[...]
        o_ref[...] = ((own + l0_ref[...].astype(F32)) + l1_ref[...].astype(F32)) + l2_ref[...].astype(F32)

    own_spec = pl.BlockSpec((None, tr, tc), lambda i, j, sr: (sr[0], i, j))
    lspec = lambda k: pl.BlockSpec((None, tr, tc), lambda i, j, sr: (k, i, j))
    return pl.pallas_call(
        body,
        name=name,
        grid_spec=pltpu.PrefetchScalarGridSpec(
            num_scalar_prefetch=1,
            grid=(r // tr, c // tc),
            in_specs=[own_spec, own_spec, lspec(0), lspec(1), lspec(2)],
            out_specs=pl.BlockSpec((tr, tc), lambda i, j, sr: (i, j)),
        ),
        out_shape=jax.ShapeDtypeStruct((r, c), F32),
        compiler_params=_cparams(("parallel", "parallel")),
    )(s_idx, g, la, lb, lb, lb)


def _place():
    x, y, c = lax.axis_index("x"), lax.axis_index("y"), lax.axis_index("c")
    chips = [(1 - x, y), (x, 1 - y), (1 - x, 1 - y)]
    return x, y, c, chips


HBM_SPEC = pl.BlockSpec(memory_space=pltpu.HBM)


def _comm_call(name, body, ins, out_shapes, n_sem, n_local):
    return pl.pallas_call(
        body,
        name=name,
        in_specs=[HBM_SPEC] * len(ins),
        out_specs=[HBM_SPEC] * len(out_shapes),
        out_shape=out_shapes,
        scratch_shapes=[pltpu.SemaphoreType.DMA((n_sem,)), pltpu.SemaphoreType.DMA((n_sem,)),
                        pltpu.SemaphoreType.DMA((max(n_local, 1),))],
    )(*ins)


def _layer_gather(ins, outs, send, recv, layer):
    n = len(ins)
    x, y, c, chips = _place()
    s = 2 * x + y
    sib = (x, y, 1 - c)
    active = c == layer

    def rc(a, k, src, dst, dev):
        return pltpu.make_async_remote_copy(src_ref=src, dst_ref=dst, send_sem=send.at[6 * a + k], recv_sem=recv.at[6 * a + k],
                                            device_id=dev, device_id_type=MESH)

    def first_hop():
        return [rc(a, j, ins[a], outs[a].at[s], (*chip, c)) for j, chip in enumerate(chips) for a in range(n)]

    def start():
        @pl.when(active)
        def _():
            for cp in first_hop():
                cp.start()

    def finish():
        @pl.when(active)
        def _():
            forwards = []
            for j, (cx, cy) in enumerate(chips):
                for a in range(n):
                    landed = outs[a].at[2 * cx + cy]
                    rc(a, j, landed, landed, sib).wait_recv()
                    forwards.append(rc(a, 3 + j, landed, landed, sib))
                    forwards[-1].start()
            for cp in first_hop() + forwards:
                cp.wait_send()

        @pl.when(jnp.logical_not(active))
        def _():
            for j, (cx, cy) in enumerate(chips):
                for a in range(n):
                    other = outs[a].at[2 * cx + cy]
                    rc(a, 3 + j, other, other, sib).wait_recv()

    return start, finish


def _gather_shapes(ws):
    return [jax.ShapeDtypeStruct((4,) + w.shape, w.dtype) for w in ws]


def gather_hook(ws, layer):
    return (ws, _gather_shapes(ws), 6 * len(ws), lambda i, o, s, r: _layer_gather(i, o, s, r, layer))


def allgather_layer(ws, layer):
    n = len(ws)

    def body(*refs):
        send, recv, _ = refs[2 * n:]
        start, finish = _layer_gather(refs[:n], refs[n:2 * n], send, recv, layer)
        start()
        finish()

    return _comm_call(f"allgather_layer{layer}", body, ws, _gather_shapes(ws), 6 * n, 0)


def exchange_sibling(gs, layer, tag=""):
    n = len(gs)

    def body(*refs):
        send, recv, _ = refs[2 * n:]
        start, finish = _sibling_exchange(refs[:n], refs[n:2 * n], send, recv, layer)
        start()
        finish()

    return _comm_call(f"exchange_sibling{layer}{tag}", body, gs, [jax.ShapeDtypeStruct(g.shape, g.dtype) for g in gs], n, 0)


def _sibling_exchange(ins, outs, send, recv, layer):
    x, y, c, _ = _place()

    def copies():
        return [pltpu.make_async_remote_copy(src_ref=ins[a], dst_ref=outs[a], send_sem=send.at[a], recv_sem=recv.at[a],
                                             device_id=(x, y, 1 - c), device_id_type=MESH) for a in range(len(ins))]

    def start():
        @pl.when(c != layer)
        def _():
            for cp in copies():
                cp.start()

    def finish():
        @pl.when(c != layer)
        def _():
            for cp in copies():
                cp.wait_send()

        @pl.when(c == layer)
        def _():
            for cp in copies():
                cp.wait_recv()

    return start, finish


def sibling_hook(gs, layer):
    return (gs, [jax.ShapeDtypeStruct(g.shape, g.dtype) for g in gs], len(gs),
            lambda i, o, s, r: _sibling_exchange(i, o, s, r, layer))


def chips_hook(ps, layer):
    return (ps, _chips_shapes(ps), 3 * len(ps), lambda i, o, s, r: _chips_exchange(i, o, s, r, layer))


def _chips_exchange(ins, outs, send, recv, layer):
    n = len(ins)
    _, _, c, chips = _place()

    def copies():
        return [pltpu.make_async_remote_copy(src_ref=ins[a].at[2 * cx + cy], dst_ref=outs[a].at[j], send_sem=send.at[3 * a + j],
                                             recv_sem=recv.at[3 * a + j], device_id=(cx, cy, c), device_id_type=MESH)
                for j, (cx, cy) in enumerate(chips) for a in range(n)]

    def start():
        @pl.when(c == layer)
        def _():
            for cp in copies():
                cp.start()

    def finish():
        @pl.when(c == layer)
        def _():
            for cp in copies():
                cp.wait()

    return start, finish


def _chips_shapes(ps):
    return [jax.ShapeDtypeStruct((3,) + p.shape[1:], p.dtype) for p in ps]


def exchange_chips(ps, layer, tag=""):
    n = len(ps)

    def body(*refs):
        send, recv, _ = refs[2 * n:]
        start, finish = _chips_exchange(refs[:n], refs[n:2 * n], send, recv, layer)
        start()
        finish()

    return _comm_call(f"exchange_chips{layer}{tag}", body, ps, _chips_shapes(ps), 3 * n, 0)


def exchange_final(rs, layer, small=None):
    n = len(rs)
    ins_all = list(rs) + ([small] if small is not None else [])

    def body(*refs):
        ni = len(ins_all)
        ins, outs = refs[:ni], refs[ni:2 * ni]
        send, recv, lsem = refs[2 * ni:]
        x, y, c, _ = _place()
        s = 2 * x + y
        to_sib = [pltpu.make_async_remote_copy(src_ref=ins[a], dst_ref=outs[a], send_sem=send.at[a], recv_sem=recv.at[a],
                                               device_id=(x, y, 1 - c), device_id_type=MESH) for a in range(n)]
        rel = [(fx, fy) for fx in (0, 1) for fy in (0, 1)]

        def piece(fx, fy, cc, dst_slot):
            k = n + 2 * (2 * fx + fy) + cc
            return pltpu.make_async_remote_copy(src_ref=ins[n], dst_ref=outs[n].at[dst_slot], send_sem=send.at[k], recv_sem=recv.at[k],
                                                device_id=(x ^ fx, y ^ fy, cc), device_id_type=MESH)

        @pl.when(c == layer)
        def _():
            for cp in to_sib:
                cp.start()
            if small is not None:
                own = pltpu.make_async_copy(ins[n], outs[n].at[s], lsem.at[0])
                own.start()
                sends = [piece(fx, fy, cc, s) for fx, fy in rel for cc in (0, 1) if (fx, fy) != (0, 0) or cc != layer]
                for cp in sends:
                    cp.start()
                for fx, fy in rel[1:]:
                    piece(fx, fy, layer, 2 * (x ^ fx) + (y ^ fy)).wait_recv()
                for cp in sends:
                    cp.wait_send()
                own.wait()
            for cp in to_sib:
                cp.wait_send()

        @pl.when(c != layer)
        def _():
            for cp in to_sib:
                cp.wait_recv()
            if small is not None:
                for fx, fy in rel:
                    piece(fx, fy, 1 - layer, 2 * (x ^ fx) + (y ^ fy)).wait_recv()

    out_shapes = [jax.ShapeDtypeStruct(r.shape, r.dtype) for r in rs]
    if small is not None:
        out_shapes.append(jax.ShapeDtypeStruct((4,) + small.shape, small.dtype))
    res = _comm_call(f"exchange_final{layer}", body, ins_all, out_shapes, n + 8, 1)
    return (res[:n], res[n]) if small is not None else (res[:n], None)


def _pad_in_cols(w):
    z = lambda n: jnp.zeros(w.shape[:-1] + (n,), w.dtype)
    return jnp.concatenate([w[..., :KR_OFF], z(64), w[..., KR_OFF:KR_OFF + MLA_ROPE], z(32), w[..., KR_OFF + MLA_ROPE:]], axis=-1)


def _unpad_in_rows(gt):
    return jnp.concatenate([gt[:KR_OFF], gt[KR_OFF + 64:KR_OFF + 96], gt[KR_OFF + 128:]], axis=0)


def _pad_heads(w, real):
    k = w.shape[0]
    return jnp.pad(w.reshape(k, MLA_HEADS, real), ((0, 0), (0, 0), (0, LANES - real))).reshape(k, MLA_HEADS * LANES)


def _pad_gain(g, real):
    return jnp.pad(g.reshape(1, real), ((0, 0), (0, LANES - real)))


def layer_weights(full):
    w_ukv = full["w_ukv"].reshape(128, MLA_HEADS, 2, 64)
    two = lambda g: jnp.concatenate([g, g]).reshape(1, LANES)
    return dict(
        norm_g=full["norm_g"].reshape(1, -1), w_in=full["w_in"], conv_w=full["conv_w"], conv_b=full["conv_b"].reshape(1, -1),
        wgx=full["w_gate_x"], bgx=full["b_gate_x"].reshape(LRU_BLOCKS, 1, LANES),
        wga=full["w_gate_a"], bga=full["b_gate_a"].reshape(LRU_BLOCKS, 1, LANES),
        lam=full["lru_lambda"].reshape(1, -1), w_lru_o=full["w_lru_o"],
        cq_norm_g=full["cq_norm_g"].reshape(1, -1), ckv_norm_g=full["ckv_norm_g"].reshape(1, -1),
        wq=_pad_heads(full["w_uq"], MLA_QK), wk=_pad_heads(w_ukv[:, :, 0].reshape(128, 512), 64),
        wv=w_ukv[:, :, 1].reshape(128, 512),
        gq=_pad_gain(full["mla_q_norm_g"], MLA_QK), gk=_pad_gain(full["mla_k_norm_g"], MLA_QK),
        w_mla_o=full["w_mla_o"], gq2=two(full["dil_q_norm_g"]), gk2=two(full["dil_k_norm_g"]),
        w_dil_o=full["w_dil_o"], b_merge=full["b_merge"].reshape(1, -1), w_out=full["w_out"],
    )


def layer_fwd(x, w, tabs, gather=None):
    h, ht = rmsnorm_fwd(x, w["norm_g"])
    z_lru = mm_nn("in_proj_lru", h, w["w_in"], n=2048, n_off=G_LRU)
    z_mla = mm_nn("in_proj_mla", h, w["w_in"], n=1024, n_off=G_MLA)
    z_dil = mm_nn("in_proj_dil", h, w["w_in"], n=5120, n_off=G_DIL)
    z_mrg = mm_nn("in_proj_mrg", h, w["w_in"], n=3072, n_off=G_MRG)
    rest_hook = None if gather is None else gather_hook(gather[0][1:], gather[1])
    hs, y_lru, yt_lru, *ga_rest = lru_fwd(z_lru, w["conv_w"], w["conv_b"], w["wgx"], w["bgx"], w["wga"], w["bga"], w["lam"],
                                          comm=rest_hook)
    qm, km, vm, vtm = mla_prep_fwd(z_mla, tabs, w)
    o_mla, y_mla, yt_mla, lse_mla, *ga_first = mla_attn_fwd(qm, km, vtm, z_mla,
                                                            gather=None if gather is None else (gather[0][:1], gather[1]))
    gathered = ga_first + ga_rest
    os_, lses, dil_rm = [], [], []
    for gi, (_, dil) in enumerate(DIL_GROUPS):
        qkv = dil_prep_fwd(z_dil, tabs, w["gq2"], w["gk2"], gi)
        o, lse = dil_attn_fwd(f"dil_attn_fwd_{dil}", *qkv, dil)
        os_.append(o)
        lses.append(lse)
        dil_rm.append(qkv)
    y_dil, o_dil, lse_dil, yt_dil = dil_combine(os_, lses, z_dil)
    ps = [mm_nn("proj_lru", y_lru, w["w_lru_o"]), mm_nn("proj_mla", y_mla, w["w_mla_o"]), mm_nn("proj_dil", y_dil, w["w_dil_o"])]
    merged, merged_t = merge_fwd(ps, z_mrg, w["b_merge"])
    out = mm_nn("out_proj", merged, w["w_out"], add=x)
    res = dict(x=x, ht=ht, z_lru=z_lru, z_mla=z_mla, z_dil=z_dil, z_mrg=z_mrg, hs=hs, yt_lru=yt_lru, qm=qm, km=km, vm=vm, o_mla=o_mla,
               yt_mla=yt_mla, lse_mla=lse_mla, dil_rm=dil_rm, yt_dil=yt_dil, o_dil=o_dil, lse_dil=lse_dil, ps=ps, merged_t=merged_t)
    return out, res, gathered


def layer_bwd(dout, r, w, tabs, prev=None, own=None):
    g = {}
    dmerged = mm_nt("out_proj_dx", dout, w["w_out"])
    g["w_out"] = mm_nn("out_proj_dw", r["merged_t"], dout, tm=1024, tn=512, tk=2048)
    dp0, dp1, dp2, dzm0, dzm1, dzm2, db0, db1, db2 = merge_bwd(dmerged, r["ps"], r["z_mrg"], w["b_merge"])
    g["b_merge"] = jnp.concatenate([db0, db1, db2], axis=1).reshape(-1)
    dy_lru = mm_nt("proj_lru_dx", dp0, w["w_lru_o"])
    dy_mla = mm_nt("proj_mla_dx", dp1, w["w_mla_o"])
    dy_dil = mm_nt("proj_dil_dx", dp2, w["w_dil_o"])
    g["w_lru_o"] = mm_nn("proj_lru_dw", r["yt_lru"], dp0, **DW_TILES)
    g["w_mla_o"] = mm_nn("proj_mla_dw", r["yt_mla"], dp1, **DW_TILES)
    g["w_dil_o"] = mm_nn("proj_dil_dw", r["yt_dil"], dp2, **DW_TILES)
    dzx, dzg_lru, dwgx, dbgx, dwga, dbga, dlam, dcw, dcb, *la_prev = lru_bwd(
        r["z_lru"], r["hs"], dy_lru, w["conv_w"], w["conv_b"], w["wgx"], w["bgx"], w["wga"], w["bga"], w["lam"],
        comm=None if prev is None else sibling_hook(prev[0], prev[1]))
    chips = None if prev is None else (prev[2](la_prev), prev[1])
    g.update(w_gate_x=dwgx, b_gate_x=dbgx.reshape(LRU_BLOCKS, LANES), w_gate_a=dwga, b_gate_a=dbga.reshape(LRU_BLOCKS, LANES),
             lru_lambda=dlam.reshape(-1), conv_w=dcw, conv_b=dcb.reshape(-1))
    do_m, dzg_mla, dd_m = mla_gate_bwd(dy_mla, r["o_mla"], r["z_mla"])
    dq_m, dk_m, dv_m, *lb_prev = mla_attn_bwd(r["qm"], r["km"], r["vm"], do_m, r["lse_mla"], dd_m, chips=chips)
    dz_mla3, dg_cq, dg_ckv, dwq, dwk, dwv, dgq, dgk = mla_prep_bwd(r["z_mla"], tabs, w, dq_m, dk_m, dv_m)
    g.update(cq_norm_g=dg_cq.reshape(-1), ckv_norm_g=dg_ckv.reshape(-1), mla_q_norm_g=dgq[0, :MLA_QK], mla_k_norm_g=dgk[0, :MLA_QK])
    g["w_uq"] = dwq.reshape(256, MLA_HEADS, LANES)[:, :, :MLA_QK].reshape(256, MLA_HEADS * MLA_QK)
    g["w_ukv"] = jnp.concatenate([dwk.reshape(128, MLA_HEADS, LANES)[:, :, :64], dwv.reshape(128, MLA_HEADS, 64)], axis=2).reshape(128, 1024)
    dzg_dil, do1, dd1, do2, dd2, l2, do3, dd3, l3 = dil_gate_bwd(dy_dil, r["o_dil"], r["z_dil"], r["lse_dil"])
    stats = [(do1, r["lse_dil"], dd1), (do2, l2, dd2), (do3, l3, dd3)]
    dzq, dzk, dzv, dgq2, dgk2 = [], [], [], [], []
    for gi, (_, dil) in enumerate(DIL_GROUPS):
        dq, dk, dv = dil_attn_bwd(f"dil_attn_bwd_{dil}", *r["dil_rm"][gi], *stats[gi], dil)
        parts = dil_prep_bwd(r["z_dil"], tabs, w["gq2"], w["gk2"], dq, dk, dv, gi)
        for acc, part in zip((dzq, dzk, dzv, dgq2, dgk2), parts):
            acc.append(part)
    g.update(dil_q_norm_g=sum(dgq2)[0, :DIL_HD], dil_k_norm_g=sum(dgk2)[0, :DIL_HD])
    dz = jnp.concatenate([dzx, dzg_lru, dz_mla3, dzg_mla] + dzq + dzk + dzv + [dzg_dil, dzm0, dzm1, dzm2], axis=1)
    g["w_in"] = _unpad_in_rows(mm_nn("in_proj_dw", r["ht"], dz, transpose_out=True, **DW_TILES))
    own_hook = None if own is None else chips_hook(*own(g))
    out = mm_nt("in_proj_dx", dz, w["w_in"], tm=1024, tn=1024, tk=IN_PAD // 4, comm=own_hook)
    dh, lb_own = (out, []) if own_hook is None else (out[0], out[1:])
    dx, dng = rmsnorm_bwd(r["x"], dh, dout, w["norm_g"])
    g["norm_g"] = dng.reshape(-1)
    return dx, g, dict(la_prev=la_prev, lb_prev=lb_prev, lb_own=lb_own)


def local_step(x, positions, target, full0, full1=None, gather1=None, reduce=None):
    tabs = rope_tables(positions.reshape(-1, 1))
    ws, ress = [], []
    for l in range(2):
        if l == 0:
            ws.append(layer_weights(full0))
            x, res, gathered = layer_fwd(x, ws[0], tabs, gather=None if gather1 is None else (gather1[0], 1))
        else:
            ws.append(layer_weights(full1 if gather1 is None else gather1[1](gathered)))
            x, res, _ = layer_fwd(x, ws[1], tabs)
        ress.append(res)
    dy, loss = loss_head(x, target)
    dy, grads1, _ = layer_bwd(dy, ress[1], ws[1], tabs)
    prev1, own0 = (None, None) if reduce is None else (reduce[0](grads1), reduce[1])
    dy, grads0, landed = layer_bwd(dy, ress[0], ws[0], tabs, prev=prev1, own=own0)
    return loss, dy, [grads0, grads1], landed


WEIGHTS = ["norm_g", "w_in", "conv_w", "conv_b", "w_gate_x", "b_gate_x", "w_gate_a", "b_gate_a", "lru_lambda", "w_lru_o", "cq_norm_g",
           "ckv_norm_g", "w_uq", "w_ukv", "mla_q_norm_g", "mla_k_norm_g", "w_mla_o", "dil_q_norm_g", "dil_k_norm_g", "w_dil_o", "b_merge",
           "w_out"]
SHARDED = {"w_in": 2, "conv_w": 2, "w_lru_o": 1, "w_uq": 2, "w_ukv": 2, "w_mla_o": 2, "w_dil_o": 2, "w_out": 1}
REPLICATED = [n for n in WEIGHTS if n not in SHARDED]
SMALL_ROWS = 144


def kernel(x, positions, norm_g, w_in, conv_w, conv_b, w_gate_x, b_gate_x, w_gate_a, b_gate_a, lru_lambda, w_lru_o, cq_norm_g, ckv_norm_g, w_uq, w_ukv, mla_q_norm_g, mla_k_norm_g, w_mla_o, dil_q_norm_g, dil_k_norm_g, w_dil_o, b_merge, w_out, loss_target, m_norm_g, m_w_in, m_conv_w, m_conv_b, m_w_gate_x, m_b_gate_x, m_w_gate_a, m_b_gate_a, m_lru_lambda, m_w_lru_o, m_cq_norm_g, m_ckv_norm_g, m_w_uq, m_w_ukv, m_mla_q_norm_g, m_mla_k_norm_g, m_w_mla_o, m_dil_q_norm_g, m_dil_k_norm_g, m_w_dil_o, m_b_merge, m_w_out, v_norm_g, v_w_in, v_conv_w, v_conv_b, v_w_gate_x, v_b_gate_x, v_w_gate_a, v_b_gate_a, v_lru_lambda, v_w_lru_o, v_cq_norm_g, v_ckv_norm_g, v_w_uq, v_w_ukv, v_mla_q_norm_g, v_mla_k_norm_g, v_w_mla_o, v_dil_q_norm_g, v_dil_k_norm_g, v_w_dil_o, v_b_merge, v_w_out):
    args = locals()
    w = {n: args[n] for n in WEIGHTS}
    m = {n: args["m_" + n] for n in WEIGHTS}
    v = {n: args["v_" + n] for n in WEIGHTS}
    my_c = lax.axis_index("c").astype(jnp.int32)
    my_s = (2 * lax.axis_index("x") + lax.axis_index("y")).astype(jnp.int32)
    c_idx = my_c.reshape(1)
    s_idx = my_s.reshape(1)

    names = list(SHARDED)
    wire = [[w[n][l] if n == "conv_w" else w[n][l].astype(BF16) for n in names] for l in range(2)]

    def assemble(l, gathered):
        full = {n: w[n][l] for n in REPLICATED}
        for n, ga, own in zip(names, gathered, wire[l]):
            parts = [jnp.where(my_s == s, own, ga[s]) for s in range(4)]
            if n == "w_in":
                parts = [_pad_in_cols(parts[0])] + parts[1:]
            full[n] = jnp.concatenate(parts, axis=SHARDED[n] - 1)
        return full

    full0 = assemble(0, allgather_layer(wire[0], 0))

    def shards_of(grads_l):
        gs = []
        for n in names:
            g_ = grads_l[n]
            if n == "w_in":
                gs.append(g_.reshape(4, IN_WIDTH // 4, D_MODEL))
            else:
                parts = jnp.stack(jnp.split(g_, 4, axis=SHARDED[n] - 1))
                gs.append(parts.reshape(4, -1, parts.shape[-1]))
        return gs

    def pair_sums(l, gs, la, tags):
        dts = [BF16 if g_.size >= 2**19 and t != "small" else F32 for g_, t in zip(gs, tags)]
        return [sum_pair(f"sum_pair{l}_{t}", g_, l_, dt) for t, g_, l_, dt in zip(tags, gs, la, dts)]

    def chip_sums(l, gs, la, lb, tags):
        return [sum_chips(f"sum_chips{l}_{t}", g_, l_, b_, s_idx) for t, g_, l_, b_ in zip(tags, gs, la, lb)]

    stash = {}

    def prev1(grads1):
        stash["gs1"] = shards_of(grads1)
        return stash["gs1"], 1, lambda la: pair_sums(1, stash["gs1"], la, names)

    def own0(grads0):
        stash["gs0"] = shards_of(grads0)
        stash["la0"] = exchange_sibling(stash["gs0"], 0)
        return pair_sums(0, stash["gs0"], stash["la0"], names), 0

    loss, grad_x, grads, landed = local_step(x[0], positions[0], loss_target[0], full0,
                                             gather1=(wire[1], functools.partial(assemble, 1)), reduce=(prev1, own0))
    loss = lax.psum(loss, ("x", "y", "c"))
    reduced1 = chip_sums(1, stash["gs1"], landed["la_prev"], landed["lb_prev"], names)
    received1, _ = exchange_final(reduced1, 1)
    reduced0 = chip_sums(0, stash["gs0"], stash["la0"], landed["lb_own"], names)

    flat = jnp.concatenate([jnp.stack([grads[0][n], grads[1][n]]).reshape(-1) for n in REPLICATED])
    small = [jnp.pad(flat, (0, 4 * SMALL_ROWS * 1024 - flat.size)).reshape(4, SMALL_ROWS, 1024)]
    la_s = exchange_sibling(small, 0, tag="_small")
    lb_s = exchange_chips(pair_sums(0, small, la_s, ["small"]), 0, tag="_small")
    received0, small_all = exchange_final(reduced0, 0, small=chip_sums(0, small, la_s, lb_s, ["small"])[0])

    g_local, delta, new_m, new_v = {}, {}, {}, {}
    for i, n in enumerate(names):
        shp = w[n].shape
        if n == "w_in":
            as3 = lambda a: a.transpose(0, 2, 1)
            back = lambda o: o.transpose(0, 2, 1)
        else:
            as3 = lambda a, rc=reduced0[i].shape: a.reshape((2,) + rc)
            back = lambda o, shp=shp: o.reshape(shp)
        outs = adamw_pair(f"adamw_{n}", as3(w[n]), (reduced0[i], reduced1[i]), (received0[i], received1[i]), as3(m[n]), as3(v[n]),
                          c_idx)
        g_local[n], delta[n], new_m[n], new_v[n] = [back(o) for o in outs]
    flat = small_all.reshape(-1)
    off = 0
    for n in REPLICATED:
        g_local[n] = flat[off:off + w[n].size].reshape(w[n].shape)
        off += w[n].size
        delta[n], new_m[n], new_v[n] = adamw(f"adamw_{n}", w[n], g_local[n], m[n], v[n])
    return (loss, grad_x[None], *[g_local[n] for n in WEIGHTS], *[delta[n] for n in WEIGHTS], *[new_m[n] for n in WEIGHTS],
            *[new_v[n] for n in WEIGHTS])
```

```python
import functools

import jax
import jax.numpy as jnp
from jax import lax
from jax.experimental import pallas as pl
from jax.experimental.pallas import tpu as pltpu

F32 = jnp.float32
BF16 = jnp.bfloat16
MXU_DTYPE = jnp.bfloat16

D_MODEL = 1024
EPS = 1e-6
ROPE_THETA = 10000.0
LRU_BLOCKS = 8
LRU_C = 8.0
MLA_HEADS = 8
MLA_NOPE = 64
MLA_ROPE = 32
MLA_QK = 96
DIL_GROUPS = ((128, 1), (512, 4), (2048, 16))
DIL_HD = 64
DIL_NK = 128
MLA_HPS = 8
IN_WIDTH = 11168
ADAM_LR, ADAM_B1, ADAM_B2, ADAM_EPS, ADAM_WD, ADAM_STEP = 0.001, 0.9, 0.999, 1e-08, 0.01, 10

LANES = 128
G_LRU, G_MLA, G_DIL, G_MRG = 0, 2048, 3072, 8192
IN_PAD = 11264
KR_OFF = 2432

NN = (((1,), (0,)), ((), ()))
NT = (((1,), (1,)), ((), ()))
TN = (((0,), (0,)), ((), ()))
NEG = -1e30
MESH = pl.DeviceIdType.MESH
VMEM_LIMIT = 48 * 2**20


def _cparams(sem):
    return pltpu.CompilerParams(dimension_semantics=sem, vmem_limit_bytes=VMEM_LIMIT)


def _dot(a, b, dims):
    return lax.dot_general(a.astype(MXU_DTYPE), b.astype(MXU_DTYPE), dims, preferred_element_type=F32)


@jax.custom_vjp
def mm(a, w):
    return _dot(a, w, NN)


def _mm_fwd(a, w):
    return _dot(a, w, NN), (a, w)


def _mm_bwd(res, g):
    a, w = res
    return _dot(g, w, NT), _dot(a, g, TN)


mm.defvjp(_mm_fwd, _mm_bwd)


def _seg64_matrix():
    r = lax.broadcasted_iota(jnp.int32, (LANES, LANES), 0) < DIL_HD
    c = lax.broadcasted_iota(jnp.int32, (LANES, LANES), 1) < DIL_HD
    return (r == c).astype(BF16)


def _seg_sum_impl(x):
    b = _seg64_matrix()
    hi = x.astype(BF16)
    r1 = x - hi.astype(F32)
    mid = r1.astype(BF16)
    lo = (r1 - mid.astype(F32)).astype(BF16)
    dot = lambda u: lax.dot_general(u, b, NN, preferred_element_type=F32)
    return dot(hi) + dot(mid) + dot(lo)


@jax.custom_vjp
def seg_sum(x):
    return _seg_sum_impl(x)


seg_sum.defvjp(lambda x: (_seg_sum_impl(x), None), lambda _, g: (_seg_sum_impl(g),))


def _lroll_impl(x, s):
    return pltpu.roll(x, s % LANES, 1)


@functools.partial(jax.custom_vjp, nondiff_argnums=(1,))
def lroll(x, s):
    return _lroll_impl(x, s)


lroll.defvjp(lambda x, s: (_lroll_impl(x, s), None), lambda s, _, g: (_lroll_impl(g, -s),))


class _Ops:
    def __init__(self, diff):
        self.mm = mm if diff else (lambda a, w: _dot(a, w, NN))
        self.seg_sum = seg_sum if diff else _seg_sum_impl
        self.lroll = lroll if diff else _lroll_impl


PLAIN, DIFF = _Ops(False), _Ops(True)


def rms(x, g, n):
    ms = jnp.sum(x * x, axis=-1, keepdims=True) * (1.0 / n)
    return x * lax.rsqrt(ms + EPS) * g


def rope(ops, x, c, s1, s2, half):
    return x * c + ops.lroll(x, -half) * s1 + ops.lroll(x, half) * s2


def _sigmoid(x):
    return 1.0 / (1.0 + jnp.exp(-x))


def _silu_and_grad(g):
    sg = _sigmoid(g)
    return g * sg, sg * (1.0 + g * (1.0 - sg))


def _softplus(x):
    return jnp.maximum(x, 0.0) + jnp.log(1.0 + jnp.exp(-jnp.abs(x)))


def _expm1(y):
    series = y * (1.0 + y * (0.5 + y * (1.0 / 6.0 + y * (1.0 / 24.0 + y * (1.0 / 120.0)))))
    return jnp.where(y > -0.05, series, jnp.exp(jnp.minimum(y, -0.05)) - 1.0)


def _host(body, n_in, n_out, n_scr, hook, grid):
    if hook is None:
        return body, [], [], [], [], []
    ins, out_shapes, n_sem, make = hook
    ni, no = len(ins), len(out_shapes)
    hbm = pl.BlockSpec(memory_space=pltpu.HBM)

    def at(corner):
        cond = None
        for axis, size in enumerate(grid):
            here = pl.program_id(axis) == (size - 1 if corner else 0)
            cond = here if cond is None else cond & here
        return cond

    def hosted(*refs):
        a, refs = refs[:n_in], refs[n_in:]
        xi, refs = refs[:ni], refs[ni:]
        o, refs = refs[:n_out], refs[n_out:]
        xo, refs = refs[:no], refs[no:]
        scr, (send, recv) = refs[:n_scr], refs[n_scr:]
        start, finish = make(xi, xo, send, recv)
        pl.when(at(False))(start)
        body(*a, *o, *scr)
        pl.when(at(True))(finish)

    sems = [pltpu.SemaphoreType.DMA((n_sem,)), pltpu.SemaphoreType.DMA((n_sem,))]
    return hosted, list(ins), [hbm] * ni, [hbm] * no, list(out_shapes), sems


def _mm_call(name, a, b, *, mode, m, n, k, a_blk, b_blk, out_dtype, add, tm, tn, tk, comm=None):
    nk = k // tk
    assert m % tm == 0 and n % tn == 0 and k % tk == 0, (name, m, n, k, tm, tn, tk)
    dims = {"nn": NN, "nt": NT, "tn": TN}[mode]

    def body(*refs):
        if add is None:
            a_ref, b_ref, o_ref, *scr = refs
            add_ref = None
        else:
            a_ref, b_ref, add_ref, o_ref, *scr = refs
        part = _dot(a_ref[...], b_ref[...], dims)

        def finish(acc):
            if add_ref is not None:
                acc = acc + add_ref[...]
            o_ref[...] = acc.astype(o_ref.dtype)

        if nk == 1:
            finish(part)
        else:
            (acc_ref,) = scr
            kk = pl.program_id(2)

            @pl.when(kk == 0)
            def _():
                acc_ref[...] = part

            @pl.when(kk > 0)
            def _():
                acc_ref[...] += part

            @pl.when(kk == nk - 1)
            def _():
                finish(acc_ref[...])

    in_specs = [a_blk, b_blk]
    args = [a, b]
    if add is not None:
        in_specs.append(pl.BlockSpec((tm, tn), lambda j, i, kk: (i, j)))
        args.append(add)
    grid = (n // tn, m // tm, nk)
    scratch = [] if nk == 1 else [pltpu.VMEM((tm, tn), F32)]
    body, x_in, x_in_specs, x_out_specs, x_out_shapes, sems = _host(body, len(args), 1, len(scratch), comm, grid)
    res = pl.pallas_call(
        body,
        name=name,
        grid=grid,
        in_specs=in_specs + x_in_specs,
        out_specs=[pl.BlockSpec((tm, tn), lambda j, i, kk: (i, j))] + x_out_specs,
        out_shape=[jax.ShapeDtypeStruct((m, n), out_dtype)] + x_out_shapes,
        scratch_shapes=scratch + sems,
        compiler_params=_cparams(("arbitrary",) * 3 if comm is not None else ("parallel", "parallel", "arbitrary")),
    )(*args, *x_in)
    return res[0] if comm is None else res


def mm_nn(name, a, b, *, n=None, n_off=0, out_dtype=F32, add=None, tm=512, tn=1024, tk=1024):
    m, k = a.shape
    n = b.shape[1] if n is None else n
    tm, tn, tk = min(tm, m), min(tn, n), min(k, tk)
    ob = n_off // tn
    assert n_off % tn == 0
    return _mm_call(name, a, b, mode="nn", m=m, n=n, k=k, out_dtype=out_dtype, add=add, tm=tm, tn=tn, tk=tk,
                    a_blk=pl.BlockSpec((tm, tk), lambda j, i, kk: (i, kk)),
                    b_blk=pl.BlockSpec((tk, tn), lambda j, i, kk: (kk, j + ob)))


def mm_nt(name, a, b, *, out_dtype=F32, tm=512, tn=1024, tk=1024, comm=None):
    m, k = a.shape
    n = b.shape[0]
    tn, tk = min(tn, n), min(k, tk)
    return _mm_call(name, a, b, mode="nt", m=m, n=n, k=k, out_dtype=out_dtype, add=None, tm=tm, tn=tn, tk=tk, comm=comm,
                    a_blk=pl.BlockSpec((tm, tk), lambda j, i, kk: (i, kk)),
                    b_blk=pl.BlockSpec((tn, tk), lambda j, i, kk: (j, kk)))


DW_TILES = dict(tm=1024, tn=512, tk=4096)


def in_proj_dw(ht, dz):
    t = ht.shape[1]
    tn = 512
    nchunk = IN_PAD // tn
    kr_chunk = KR_OFF // tn
    lo = KR_OFF - kr_chunk * tn

    def body(ht_ref, dz_ref, out_ref, buf, sems):
        j = pl.program_id(0)
        slot = j % 2
        buf[slot] = _dot(ht_ref[...], dz_ref[...], NN).T

        def copies(c, s):
            dst = jnp.where(c < kr_chunk, c * tn, (c - 1) * tn + lo + MLA_ROPE)
            plain = [pltpu.make_async_copy(buf.at[s], out_ref.at[pl.ds(pl.multiple_of(dst, 8), tn)], sems.at[s, 0])]
            split = [pltpu.make_async_copy(buf.at[s, 0:lo], out_ref.at[KR_OFF - lo:KR_OFF], sems.at[s, 0]),
                     pltpu.make_async_copy(buf.at[s, lo + 64:lo + 96], out_ref.at[KR_OFF:KR_OFF + MLA_ROPE], sems.at[s, 1])]
            return plain, split

        def each(c, s, action):
            plain, split = copies(c, s)

            @pl.when(c != kr_chunk)
            def _():
                for cp in plain:
                    action(cp)

            @pl.when(c == kr_chunk)
            def _():
                for cp in split:
                    action(cp)

        @pl.when(j > 0)
        def _():
            each(j - 1, 1 - slot, lambda cp: cp.wait())

        each(j, slot, lambda cp: cp.start())

        @pl.when(j == nchunk - 1)
        def _():
            each(j, slot, lambda cp: cp.wait())

    return pl.pallas_call(
        body,
        name="in_proj_dw",
        grid=(nchunk,),
        in_specs=[pl.BlockSpec((D_MODEL, t), lambda j: (0, 0)), pl.BlockSpec((t, tn), lambda j: (0, j))],
        out_specs=pl.BlockSpec(memory_space=pltpu.HBM),
        out_shape=jax.ShapeDtypeStruct((IN_WIDTH, D_MODEL), F32),
        scratch_shapes=[pltpu.VMEM((2, tn, D_MODEL), F32), pltpu.SemaphoreType.DMA((2, 2))],
        compiler_params=_cparams(("arbitrary",)),
    )(ht, dz)


def rowwise(name, fn, t, *, tm=256, ncol=1, ins=(), outs=(), touts=(), accs=()):
    n_in, n_out, n_acc = len(ins), len(outs) + len(touts), len(accs)

    def zero_when(ref, cond):
        @pl.when(cond)
        def _():
            ref[...] = jnp.zeros(ref.shape, ref.dtype)

    def body(*refs):
        in_refs, out_refs, acc_refs = refs[:n_in], refs[n_in:n_in + n_out], refs[n_in + n_out:]
        j, i = pl.program_id(0), pl.program_id(1)
        for ref, (_, _, _, cd) in zip(acc_refs, accs):
            zero_when(ref, (i == 0) if cd else ((i == 0) & (j == 0)))
        fn(in_refs, out_refs, acc_refs)

    def in_spec(arr, w, base, cd, rd):
        rows = tm if rd else arr.shape[0]
        return pl.BlockSpec((rows, w), lambda j, i: (i if rd else 0, base + (j if cd else 0)))

    in_specs = [in_spec(*e) for e in ins]
    out_specs = [pl.BlockSpec((tm, w), (lambda j, i, cd=cd: (i, j if cd else 0))) for (_, w, _, cd) in outs]
    out_specs += [pl.BlockSpec((w, tm), (lambda j, i, cd=cd: (j if cd else 0, i))) for (_, w, _, cd) in touts]
    out_specs += [pl.BlockSpec((r, w), (lambda j, i, cd=cd: (0, j if cd else 0))) for (r, _, w, cd) in accs]
    out_shape = [jax.ShapeDtypeStruct((t, c), dt) for (c, _, dt, _) in outs]
    out_shape += [jax.ShapeDtypeStruct((r, t), dt) for (r, _, dt, _) in touts]
    out_shape += [jax.ShapeDtypeStruct((r, c), F32) for (r, c, _, _) in accs]
    res = pl.pallas_call(
        body,
        name=name,
        grid=(ncol, t // tm),
        in_specs=in_specs,
        out_specs=out_specs,
        out_shape=out_shape,
        compiler_params=_cparams(("arbitrary", "arbitrary") if accs else ("parallel", "parallel")),
    )(*[e[0] for e in ins])
    return res


def _row(arr, w=None, base=0, cd=False):
    return (arr, arr.shape[1] if w is None else w, base, cd, True)


def _const(arr, w=None, base=0, cd=False):
    return (arr, arr.shape[1] if w is None else w, base, cd, False)


def rope_tables(positions):
    t = positions.shape[0]

    def fn(ins, outs, _):
        pos = ins[0][...].astype(F32)
        lane = lax.broadcasted_iota(jnp.int32, (1, LANES), 1)
        log_theta = jnp.log(jnp.float32(ROPE_THETA))
        jm = lane - MLA_NOPE
        idx = jnp.clip(jnp.where(jm < 16, jm, jm - 16), 0, 15).astype(F32)
        ang = pos * jnp.exp(-(idx * (2.0 / MLA_ROPE)) * log_theta)
        cos, sin = jnp.cos(ang), jnp.sin(ang)
        in_rope = (lane >= MLA_NOPE) & (lane < MLA_QK)
        outs[0][:, 0:128] = jnp.where(lane < MLA_NOPE, 1.0, jnp.where(in_rope, cos, 0.0))
        outs[0][:, 128:256] = jnp.where(in_rope & (jm < 16), -sin, 0.0)
        outs[0][:, 256:384] = jnp.where(in_rope & (jm >= 16), sin, 0.0)
        jd = lane & (DIL_HD - 1)
        idx = (jd & 31).astype(F32)
        ang = pos * jnp.exp(-(idx * (2.0 / DIL_HD)) * log_theta)
        cos, sin = jnp.cos(ang), jnp.sin(ang)
        outs[0][:, 384:512] = cos
        outs[0][:, 512:640] = jnp.where(jd < 32, -sin, 0.0)
        outs[0][:, 640:768] = jnp.where(jd >= 32, sin, 0.0)

    return rowwise("rope_tables", fn, t, ins=[_row(positions)], outs=[(768, 768, F32, False)])[0]


def rmsnorm_fwd(x, g):
    def fn(ins, outs, _):
        h = rms(ins[0][...], ins[1][...], D_MODEL)
        outs[0][...] = h.astype(MXU_DTYPE)
        outs[1][...] = h.T.astype(MXU_DTYPE)

    return rowwise("rmsnorm_fwd", fn, x.shape[0], ins=[_row(x), _const(g)], outs=[(D_MODEL, D_MODEL, MXU_DTYPE, False)],
                   touts=[(D_MODEL, D_MODEL, MXU_DTYPE, False)])


def rmsnorm_bwd(x, dh, dres, g):
    def fn(ins, outs, accs):
        _, vjp = jax.vjp(lambda xv, gv: rms(xv, gv, D_MODEL), ins[0][...], ins[3][...])
        dx, dg = vjp(ins[1][...])
        outs[0][...] = ins[2][...] + dx
        accs[0][...] += dg

    return rowwise("rmsnorm_bwd", fn, x.shape[0], ins=[_row(x), _row(dh), _row(dres), _const(g)],
                   outs=[(D_MODEL, D_MODEL, F32, False)], accs=[(1, D_MODEL, D_MODEL, False)])


def loss_head(y, target):
    def fn(ins, outs, accs):
        err = ins[0][...] - ins[1][...]
        outs[0][...] = err * (1.0 / D_MODEL)
        accs[0][...] += jnp.sum(err * err, axis=0, keepdims=True)
        accs[1][...] = jnp.broadcast_to(jnp.sum(accs[0][...], keepdims=True), (1, LANES))

    dy, _, tot = rowwise("loss_head", fn, y.shape[0], ins=[_row(y), _row(target)], outs=[(D_MODEL, D_MODEL, F32, False)],
                         accs=[(1, D_MODEL, D_MODEL, False), (1, LANES, LANES, False)])
    return dy, tot[0, 0] * (0.5 / D_MODEL)


def _shift_rows(v, d, fill, reverse):
    tb = v.shape[0]
    if d % 8 == 0:
        pad = jnp.full((d, v.shape[1]), fill, v.dtype)
        return jnp.concatenate([v[d:], pad] if reverse else [pad, v[:tb - d]], axis=0)
    rows = lax.broadcasted_iota(jnp.int32, v.shape, 0)
    if not reverse:
        return jnp.where(rows >= d, pltpu.roll(v, d, 0), fill)
    return jnp.where(rows < tb - d, pltpu.roll(v, tb - d, 0), fill)


def _scan_tile(a, b, reverse):
    d = 1
    while d < a.shape[0]:
        b = b + a * _shift_rows(b, d, 0.0, reverse)
        a = a * _shift_rows(a, d, 1.0, reverse)
        d *= 2
    return a, b


def _lru_gates(ops, xc, wgx, bgx, wga, bga, lam):
    gx = _sigmoid(ops.mm(xc, wgx) + bgx)
    ga = _sigmoid(ops.mm(xc, wga) + bga)
    log_a = -LRU_C * ga * _softplus(-lam)
    a = jnp.exp(log_a)
    mult = jnp.sqrt(-_expm1(2.0 * log_a))
    return a, mult * (gx * xc)


def _shifted_inputs(x, halo, tb):
    rows = lax.broadcasted_iota(jnp.int32, x.shape, 0)
    pad = jnp.zeros((tb - 8, LANES), F32)
    out = []
    for d in (3, 2, 1):
        head = jnp.concatenate([pltpu.roll(halo, d, 0), pad], axis=0)
        out.append(jnp.where(rows >= d, pltpu.roll(x, d, 0), head))
    return out + [x]


def _lru_specs(nt, tb, reverse):
    hb = tb // 8
    tt = (lambda t: nt - 1 - t) if reverse else (lambda t: t)
    blk = lambda off: pl.BlockSpec((tb, LANES), lambda n, t: (tt(t), n + off))
    halo = lambda off: pl.BlockSpec((8, LANES), lambda n, t: (jnp.maximum(tt(t) * hb - 1, 0), n + off))
    chan = lambda r: pl.BlockSpec((r, LANES), lambda n, t: (0, n))
    wblk = pl.BlockSpec((None, LANES, LANES), lambda n, t: (n, 0, 0))
    bblk = pl.BlockSpec((None, 1, LANES), lambda n, t: (n, 0, 0))
    return blk, halo, chan, wblk, bblk


def lru_fwd(z, conv_w, conv_b, wgx, bgx, wga, bga, lam, *, tb=256, comm=None):
    t = z.shape[0]
    nt = t // tb
    blk, halo, chan, wblk, bblk = _lru_specs(nt, tb, False)

    def body(x_ref, xh_ref, g_ref, cw_ref, cb_ref, wgx_ref, bgx_ref, wga_ref, bga_ref, lam_ref, h_ref, y_ref, yt_ref, carry_ref):
        ti = pl.program_id(1)

        @pl.when(ti == 0)
        def _():
            carry_ref[...] = jnp.zeros((8, LANES), F32)

        x = x_ref[...]
        hal = jnp.where(ti > 0, xh_ref[...], 0.0)
        xs = _shifted_inputs(x, hal, tb)
        xc = cb_ref[...] + sum(xs[k] * cw_ref[k:k + 1, :] for k in range(4))
        a, b = _lru_gates(PLAIN, xc, wgx_ref[...], bgx_ref[...], wga_ref[...], bga_ref[...], lam_ref[...])
        acum, h0 = _scan_tile(a, b, False)
        h = h0 + acum * carry_ref[7:8, :]
        carry_ref[...] = h[tb - 8:tb, :]
        h_ref[...] = h
        y = h * _silu_and_grad(g_ref[...])[0]
        y_ref[...] = y.astype(MXU_DTYPE)
        yt_ref[...] = y.T.astype(MXU_DTYPE)

    grid = (LRU_BLOCKS, nt)
    body, x_in, x_in_specs, x_out_specs, x_out_shapes, sems = _host(body, 10, 3, 1, comm, grid)
    return pl.pallas_call(
        body,
        name="lru_fwd" if comm is None else "lru_fwd_comm",
        grid=grid,
        in_specs=[blk(0), halo(0), blk(LRU_BLOCKS), chan(4), chan(1), wblk, bblk, wblk, bblk, chan(1)] + x_in_specs,
        out_specs=[blk(0), blk(0), pl.BlockSpec((LANES, tb), lambda n, t_: (n, t_))] + x_out_specs,
        out_shape=[jax.ShapeDtypeStruct((t, 1024), F32), jax.ShapeDtypeStruct((t, 1024), MXU_DTYPE),
                   jax.ShapeDtypeStruct((1024, t), MXU_DTYPE)] + x_out_shapes,
        scratch_shapes=[pltpu.VMEM((8, LANES), F32)] + sems,
        compiler_params=_cparams(("arbitrary", "arbitrary") if comm is not None else ("parallel", "arbitrary")),
    )(z, z, z, conv_w, conv_b, wgx, bgx, wga, bga, lam, *x_in)


def lru_bwd(z, h, dy, conv_w, conv_b, wgx, bgx, wga, bga, lam, *, tb=256, comm=None):
    t = z.shape[0]
    nt = t // tb
    blk, halo, chan, wblk, bblk = _lru_specs(nt, tb, True)

    def body(x_ref, xh_ref, g_ref, h_ref, hh_ref, dy_ref, cw_ref, cb_ref, wgx_ref, bgx_ref, wga_ref, bga_ref, lam_ref,
             dzx_ref, dzg_ref, dwgx_ref, dbgx_ref, dwga_ref, dbga_ref, dlam_ref, dcw_ref, dcb_ref,
             gcar_ref, acar_ref, xcar_ref):
        ti = pl.program_id(1)
        has_earlier = ti < nt - 1

        @pl.when(ti == 0)
        def _():
            for ref in (dwgx_ref, dbgx_ref, dwga_ref, dbga_ref, dlam_ref, dcw_ref, dcb_ref, gcar_ref, acar_ref, xcar_ref):
                ref[...] = jnp.zeros(ref.shape, F32)

        rows = lax.broadcasted_iota(jnp.int32, (tb, LANES), 0)
        x = x_ref[...]
        hal = jnp.where(has_earlier, xh_ref[...], 0.0)
        xs = _shifted_inputs(x, hal, tb)
        xc = cb_ref[...] + sum(xs[k] * cw_ref[k:k + 1, :] for k in range(4))
        (a, _), vjp = jax.vjp(functools.partial(_lru_gates, DIFF), xc, wgx_ref[...], bgx_ref[...], wga_ref[...],
                              bga_ref[...], lam_ref[...])
        g, h, dyv = g_ref[...], h_ref[...], dy_ref[...]
        silu, dsilu = _silu_and_grad(g)
        dzg_ref[...] = (dyv * h * dsilu).astype(MXU_DTYPE)
        a_next = jnp.where(rows < tb - 1, pltpu.roll(a, tb - 1, 0), acar_ref[0:1, :])
        acum, g0 = _scan_tile(a_next, dyv * silu, True)
        gt = g0 + acum * gcar_ref[0:1, :]
        h_prev = jnp.where(rows >= 1, pltpu.roll(h, 1, 0), jnp.where(has_earlier, hh_ref[7:8, :], 0.0))
        dxc, dwgx, dbgx, dwga, dbga, dlam = vjp((gt * h_prev, gt))
        later = xcar_ref[...]
        gcar_ref[...] = gt[0:8, :]
        acar_ref[...] = a[0:8, :]
        xcar_ref[...] = dxc[0:8, :]
        dx = dxc * cw_ref[3:4, :]
        pad = jnp.zeros((tb - 8, LANES), F32)
        for d in (1, 2, 3):
            tail = jnp.concatenate([pad, pltpu.roll(later, 8 - d, 0)], axis=0)
            up = jnp.where(rows < tb - d, pltpu.roll(dxc, tb - d, 0), tail)
            dx = dx + up * cw_ref[3 - d:4 - d, :]
        dzx_ref[...] = dx.astype(MXU_DTYPE)
        for k in range(4):
            dcw_ref[k:k + 1, :] += jnp.sum(dxc * xs[k], axis=0, keepdims=True)
        dcb_ref[...] += jnp.sum(dxc, axis=0, keepdims=True)
        dwgx_ref[...] += dwgx
        dbgx_ref[...] += dbgx
        dwga_ref[...] += dwga
        dbga_ref[...] += dbga
        dlam_ref[...] += dlam

    sds = jax.ShapeDtypeStruct
    grid = (LRU_BLOCKS, nt)
    body, x_in, x_in_specs, x_out_specs, x_out_shapes, sems = _host(body, 13, 9, 3, comm, grid)
    return pl.pallas_call(
        body,
        name="lru_bwd" if comm is None else "lru_bwd_comm",
        grid=grid,
        in_specs=[blk(0), halo(0), blk(LRU_BLOCKS), blk(0), halo(0), blk(0), chan(4), chan(1), wblk, bblk, wblk, bblk, chan(1)]
        + x_in_specs,
        out_specs=[blk(0), blk(0), wblk, bblk, wblk, bblk, chan(1), chan(4), chan(1)] + x_out_specs,
        out_shape=[sds((t, 1024), MXU_DTYPE), sds((t, 1024), MXU_DTYPE), sds(wgx.shape, F32), sds(bgx.shape, F32), sds(wga.shape, F32),
                   sds(bga.shape, F32), sds((1, 1024), F32), sds((4, 1024), F32), sds((1, 1024), F32)] + x_out_shapes,
        scratch_shapes=[pltpu.VMEM((8, LANES), F32)] * 3 + sems,
        compiler_params=_cparams(("arbitrary", "arbitrary") if comm is not None else ("parallel", "arbitrary")),
    )(z, z, z, h, h, dy, conv_w, conv_b, wgx, bgx, wga, bga, lam, *x_in)


def _mla_prep_tile(ops, cq, ckv, krp, c, s1, s2, g_cq, g_ckv, wq, wk, wv, gq, gk):
    cqn = rms(cq, g_cq, 256)
    ckvn = rms(ckv, g_ckv, 128)
    v = ops.mm(ckvn, wv)
    qs, ks = [], []
    for hd in range(MLA_HEADS):
        q = rms(ops.mm(cqn, wq[hd]), gq, MLA_QK)
        k = rms(ops.mm(ckvn, wk[hd]) + krp, gk, MLA_QK)
        qs.append(rope(ops, q, c, s1, s2, 16) * (MLA_QK ** -0.5))
        ks.append(rope(ops, k, c, s1, s2, 16))
    return tuple(qs), tuple(ks), v


def _mla_prep_args(ins):
    z_cq, z_ckv, z_kr, tc, ts1, ts2, g_cq, g_ckv, wq, wk, wv, gq, gk = ins[:13]
    heads = lambda w: tuple(w[:, LANES * hd:LANES * (hd + 1)] for hd in range(MLA_HEADS))
    return (z_cq[...], z_ckv[...], z_kr[...], tc[...], ts1[...], ts2[...], g_cq[...], g_ckv[...], heads(wq), heads(wk),
            wv[...], gq[...], gk[...])


def _mla_prep_ins(z, tabs, w):
    return [_row(z, 256, 0), _row(z, 128, 2), _row(z, 128, 3), _row(tabs, 128, 0), _row(tabs, 128, 1), _row(tabs, 128, 2),
            _const(w["cq_norm_g"]), _const(w["ckv_norm_g"]), _const(w["wq"]), _const(w["wk"]), _const(w["wv"]),
            _const(w["gq"]), _const(w["gk"])]


def mla_prep_fwd(z, tabs, w):
    def fn(ins, outs, _):
        qs, ks, v = _mla_prep_tile(PLAIN, *_mla_prep_args(ins))
        for hd in range(MLA_HEADS):
            outs[0][:, LANES * hd:LANES * (hd + 1)] = qs[hd].astype(MXU_DTYPE)
            outs[1][:, LANES * hd:LANES * (hd + 1)] = ks[hd].astype(MXU_DTYPE)
        outs[2][...] = v.astype(MXU_DTYPE)
        outs[3][...] = v.T.astype(MXU_DTYPE)

    return rowwise("mla_prep_fwd", fn, z.shape[0], ins=_mla_prep_ins(z, tabs, w),
                   outs=[(1024, 1024, MXU_DTYPE, False), (1024, 1024, MXU_DTYPE, False), (512, 512, MXU_DTYPE, False)],
                   touts=[(512, 512, MXU_DTYPE, False)])


def mla_prep_bwd(z, tabs, w, dq, dk, dv):
    def fn(ins, outs, accs):
        args = _mla_prep_args(ins)
        _, vjp = jax.vjp(functools.partial(_mla_prep_tile, DIFF), *args)
        heads = lambda ref: tuple(ref[:, LANES * hd:LANES * (hd + 1)] for hd in range(MLA_HEADS))
        dcq, dckv, dkr, _, _, _, dg_cq, dg_ckv, dwq, dwk, dwv, dgq, dgk = vjp((heads(ins[13]), heads(ins[14]), ins[15][...]))
        lane = lax.broadcasted_iota(jnp.int32, (1, LANES), 1)
        outs[0][:, 0:256] = dcq.astype(MXU_DTYPE)
        outs[0][:, 256:384] = dckv.astype(MXU_DTYPE)
        outs[0][:, 384:512] = jnp.where((lane >= MLA_NOPE) & (lane < MLA_QK), dkr, 0.0).astype(MXU_DTYPE)
        accs[0][...] += dg_cq
        accs[1][...] += dg_ckv
        for hd in range(MLA_HEADS):
            accs[2][:, LANES * hd:LANES * (hd + 1)] += dwq[hd]
            accs[3][:, LANES * hd:LANES * (hd + 1)] += dwk[hd]
        accs[4][...] += dwv
        accs[5][...] += dgq
        accs[6][...] += dgk

    return rowwise("mla_prep_bwd", fn, z.shape[0], ins=_mla_prep_ins(z, tabs, w) + [_row(dq), _row(dk), _row(dv)],
                   outs=[(512, 512, MXU_DTYPE, False)],
                   accs=[(1, 256, 256, False), (1, 128, 128, False), (256, 1024, 1024, False), (128, 1024, 1024, False),
                         (128, 512, 512, False), (1, 128, 128, False), (1, 128, 128, False)])


def _head_masks():
    lane = lax.broadcasted_iota(jnp.int32, (1, LANES), 1)
    return (lane < DIL_HD, lane >= DIL_HD)


def _row_scalar(tile, mask):
    return jnp.max(jnp.where(mask, tile, -jnp.inf), axis=-1, keepdims=True)


def _causal_tiles(nq, by_key):
    pairs = [(i, j) for i in range(nq) for j in range(i + 1)]
    if by_key:
        pairs.sort(key=lambda ij: (ij[1], ij[0]))
    return (jnp.asarray([ij[0] for ij in pairs], jnp.int32), jnp.asarray([ij[1] for ij in pairs], jnp.int32))


def mla_attn_fwd(q, k, vt, z, *, tq=256, gather=None):
    t = q.shape[0]
    nq = t // tq
    it, jt = _causal_tiles(nq, False)
    hps, wq, wv = MLA_HPS, LANES * MLA_HPS, 64 * MLA_HPS
    ws, layer = gather if gather is not None else ([], 0)
    ng = len(ws)
    ngrp, nsteps = MLA_HEADS // hps, int(it.shape[0])

    def body(it_ref, jt_ref, q_ref, k_ref, vt_ref, g_ref, *rest):
        w_refs, rest = rest[:ng], rest[ng:]
        o_ref, y_ref, yt_ref, lse_ref = rest[:4]
        ga_refs, rest = rest[4:4 + ng], rest[4 + ng:]
        m_scr, l_scr, acc_scr = rest[:3]
        step = pl.program_id(1)
        i, j = it_ref[step], jt_ref[step]
        if ng:
            gather_start, gather_finish = _layer_gather(w_refs, ga_refs, rest[3], rest[4], layer)

            @pl.when((pl.program_id(0) == 0) & (step == 0))
            def _():
                gather_start()

        @pl.when(j == 0)
        def _():
            m_scr[...] = jnp.full(m_scr.shape, NEG, F32)
            l_scr[...] = jnp.zeros(l_scr.shape, F32)
            acc_scr[...] = jnp.zeros(acc_scr.shape, F32)

        def update(diagonal):
            heads = range(hps)
            lanes = [slice(LANES * hh, LANES * (hh + 1)) for hh in heads]
            rows = [slice(64 * hh, 64 * (hh + 1)) for hh in heads]
            sts = [_dot(k_ref[:, lanes[hh]], q_ref[:, lanes[hh]], NT) for hh in heads]
            m_prev = [m_scr[hh:hh + 1, :] for hh in heads]
            l_prev = [l_scr[hh:hh + 1, :] for hh in heads]
            acc_prev = [acc_scr[rows[hh], :] for hh in heads]
            m_new, l_new, acc_new = [], [], []
            for hh in heads:
                st = sts[hh]
                if diagonal:
                    key = lax.broadcasted_iota(jnp.int32, (tq, tq), 0)
                    qry = lax.broadcasted_iota(jnp.int32, (tq, tq), 1)
                    st = jnp.where(key <= qry, st, NEG)
                m_new.append(jnp.maximum(m_prev[hh], jnp.max(st, axis=0, keepdims=True)))
                alpha = jnp.exp(m_prev[hh] - m_new[hh])
                pt = jnp.exp(st - m_new[hh])
                l_new.append(alpha * l_prev[hh] + jnp.sum(pt, axis=0, keepdims=True))
                acc_new.append(alpha * acc_prev[hh] + _dot(vt_ref[rows[hh], :], pt, NN))
            for hh in heads:
                m_scr[hh:hh + 1, :] = m_new[hh]
                l_scr[hh:hh + 1, :] = l_new[hh]
                acc_scr[rows[hh], :] = acc_new[hh]

        @pl.when(j < i)
        def _():
            update(False)

        @pl.when(j == i)
        def _():
            update(True)
            lse_ref[...] = jnp.zeros(lse_ref.shape, F32)
            for hh in range(hps):
                rows = slice(64 * hh, 64 * (hh + 1))
                acc_scr[rows, :] = acc_scr[rows, :] / l_scr[hh:hh + 1, :]
                lse_ref[hh:hh + 1, :] = m_scr[hh:hh + 1, :] + jnp.log(l_scr[hh:hh + 1, :])
            o = acc_scr[...].T
            o_ref[...] = o
            y = o * _silu_and_grad(g_ref[...])[0]
            y_ref[...] = y.astype(MXU_DTYPE)
            yt_ref[...] = y.T.astype(MXU_DTYPE)

        if ng:
            @pl.when((pl.program_id(0) == ngrp - 1) & (step == nsteps - 1))
            def _():
                gather_finish()

    qo = lambda w, off=0: pl.BlockSpec((tq, w), lambda p, s, it_, jt_: (it_[s], p + off))
    sds = jax.ShapeDtypeStruct
    comm_scratch = [pltpu.SemaphoreType.DMA((6 * ng,)), pltpu.SemaphoreType.DMA((6 * ng,))] if ng else []
    return pl.pallas_call(
        body,
        name="mla_attn_fwd_gather" if ng else "mla_attn_fwd",
        grid_spec=pltpu.PrefetchScalarGridSpec(
            num_scalar_prefetch=2,
            grid=(ngrp, nsteps),
            in_specs=[qo(wq), pl.BlockSpec((tq, wq), lambda p, s, it_, jt_: (jt_[s], p)),
                      pl.BlockSpec((wv, tq), lambda p, s, it_, jt_: (p, jt_[s])), qo(wv, 512 // wv)] + [HBM_SPEC] * ng,
            out_specs=[qo(wv), qo(wv), pl.BlockSpec((wv, tq), lambda p, s, it_, jt_: (p, it_[s])),
                       pl.BlockSpec((None, 8, tq), lambda p, s, it_, jt_: (p, 0, it_[s]))] + [HBM_SPEC] * ng,
            scratch_shapes=[pltpu.VMEM((8, tq), F32), pltpu.VMEM((8, tq), F32), pltpu.VMEM((wv, tq), F32)] + comm_scratch,
        ),
        out_shape=[sds((t, 512), F32), sds((t, 512), MXU_DTYPE), sds((512, t), MXU_DTYPE), sds((ngrp, 8, t), F32)]
        + _gather_shapes(ws),
        compiler_params=_cparams(("arbitrary", "arbitrary") if ng else ("parallel", "arbitrary")),
    )(it, jt, q, k, vt, z, *ws)


def mla_attn_bwd(q, k, v, do, lse, dd, *, tq=256, chips=None):
    t = q.shape[0]
    nq = t // tq
    it, jt = _causal_tiles(nq, True)
    hps, wq, wv = MLA_HPS, LANES * MLA_HPS, 64 * MLA_HPS
    pcs, layer = chips if chips is not None else ([], 0)
    nc = len(pcs)
    ngrp, nsteps = MLA_HEADS // hps, int(it.shape[0])

    def body(it_ref, jt_ref, q_ref, k_ref, v_ref, do_ref, lse_ref, d_ref, *rest):
        pc_refs, rest = rest[:nc], rest[nc:]
        dq_ref, dk_ref, dv_ref = rest[:3]
        lb_refs, rest = rest[3:3 + nc], rest[3 + nc:]
        dk_scr, dv_scr = rest[:2]
        step = pl.program_id(1)
        i, j = it_ref[step], jt_ref[step]
        masks = _head_masks()
        if nc:
            chips_start, chips_finish = _chips_exchange(pc_refs, lb_refs, rest[2], rest[3], layer)

            @pl.when((pl.program_id(0) == 0) & (step == 0))
            def _():
                chips_start()

        @pl.when(step == 0)
        def _():
            dq_ref[...] = jnp.zeros(dq_ref.shape, F32)

        @pl.when(i == j)
        def _():
            dk_scr[...] = jnp.zeros(dk_scr.shape, F32)
            dv_scr[...] = jnp.zeros(dv_scr.shape, F32)

        def update(diagonal):
            qrows = pl.ds(pl.multiple_of(i * tq, tq), tq)
            heads = range(hps)
            lanes = [slice(LANES * hh, LANES * (hh + 1)) for hh in heads]
            pair = [slice(LANES * (hh // 2), LANES * (hh // 2 + 1)) for hh in heads]
            qh = [q_ref[:, lanes[hh]] for hh in heads]
            kh = [k_ref[:, lanes[hh]] for hh in heads]
            doh = []
            for hh in heads:
                dov = do_ref[:, pair[hh]]
                doh.append(jnp.where(masks[hh % 2], dov, jnp.zeros_like(dov)))
            sts = [_dot(kh[hh], qh[hh], NT) for hh in heads]
            dps = [_dot(v_ref[:, pair[hh]], doh[hh], NT) for hh in heads]
            lse = [lse_ref[hh:hh + 1, :] for hh in heads]
            ddv = [d_ref[hh:hh + 1, :] for hh in heads]
            dk_new = [dk_scr[:, lanes[hh]] for hh in heads]
            dq_new = [dq_ref[qrows, lanes[hh]] for hh in heads]
            dv_new = [dv_scr[:, pair[2 * pp]] for pp in range(hps // 2)]
            for hh in heads:
                st = sts[hh] - lse[hh]
                if diagonal:
                    key = lax.broadcasted_iota(jnp.int32, (tq, tq), 0)
                    qry = lax.broadcasted_iota(jnp.int32, (tq, tq), 1)
                    st = jnp.where(key <= qry, st, NEG)
                pt = jnp.exp(st)
                dst = pt * (dps[hh] - ddv[hh])
                dv_new[hh // 2] = dv_new[hh // 2] + _dot(pt, doh[hh], NN)
                dk_new[hh] = dk_new[hh] + _dot(dst, qh[hh], NN)
                dq_new[hh] = dq_new[hh] + _dot(dst, kh[hh], TN)
            for hh in heads:
                dk_scr[:, lanes[hh]] = dk_new[hh]
                dq_ref[qrows, lanes[hh]] = dq_new[hh]
            for pp in range(hps // 2):
                dv_scr[:, pair[2 * pp]] = dv_new[pp]

        @pl.when(j < i)
        def _():
            update(False)

        @pl.when(j == i)
        def _():
            update(True)

        @pl.when(i == nq - 1)
        def _():
            dk_ref[...] = dk_scr[...]
            dv_ref[...] = dv_scr[...]

        if nc:
            @pl.when((pl.program_id(0) == ngrp - 1) & (step == nsteps - 1))
            def _():
                chips_finish()

    qo = lambda w: pl.BlockSpec((tq, w), lambda p, s, it_, jt_: (it_[s], p))
    kv = lambda w: pl.BlockSpec((tq, w), lambda p, s, it_, jt_: (jt_[s], p))
    stat = pl.BlockSpec((None, 8, tq), lambda p, s, it_, jt_: (p, 0, it_[s]))
    sds = jax.ShapeDtypeStruct
    comm_scratch = [pltpu.SemaphoreType.DMA((3 * nc,)), pltpu.SemaphoreType.DMA((3 * nc,))] if nc else []
    return pl.pallas_call(
        body,
        name="mla_attn_bwd_chips" if nc else "mla_attn_bwd",
        grid_spec=pltpu.PrefetchScalarGridSpec(
            num_scalar_prefetch=2,
            grid=(ngrp, nsteps),
            in_specs=[qo(wq), kv(wq), kv(wv), qo(wv), stat, stat] + [HBM_SPEC] * nc,
            out_specs=[pl.BlockSpec((t, wq), lambda p, s, it_, jt_: (0, p)), kv(wq), kv(wv)] + [HBM_SPEC] * nc,
            scratch_shapes=[pltpu.VMEM((tq, wq), F32), pltpu.VMEM((tq, wv), F32)] + comm_scratch,
        ),
        out_shape=[sds((t, 1024), F32), sds((t, 1024), F32), sds((t, 512), F32)] + _chips_shapes(pcs),
        compiler_params=_cparams(("arbitrary", "arbitrary") if nc else ("parallel", "arbitrary")),
    )(it, jt, q, k, v, do, lse, dd, *pcs)


def mla_gate_bwd(dy, o, z, *, tm=256):
    t = dy.shape[0]
    wv = 64 * MLA_HPS

    def body(dy_ref, o_ref, g_ref, do_ref, dzg_ref, dd_ref):
        dyv, ov = dy_ref[...], o_ref[...]
        silu, dsilu = _silu_and_grad(g_ref[...])
        do = dyv * silu
        do_ref[...] = do.astype(MXU_DTYPE)
        dzg_ref[...] = (dyv * ov * dsilu).astype(MXU_DTYPE)
        prod = do * ov
        row = lax.broadcasted_iota(jnp.int32, (8, wv), 0)
        lane = lax.broadcasted_iota(jnp.int32, (8, wv), 1)
        pick = ((lane >= row * DIL_HD) & (lane < (row + 1) * DIL_HD)).astype(BF16)
        hi = prod.astype(BF16)
        r1 = prod - hi.astype(F32)
        mid = r1.astype(BF16)
        lo = (r1 - mid.astype(F32)).astype(BF16)
        dot = lambda u: lax.dot_general(pick, u, NT, preferred_element_type=F32)
        dd_ref[...] = dot(hi) + dot(mid) + dot(lo)

    blk = lambda off=0: pl.BlockSpec((tm, wv), lambda p, i: (i, p + off))
    sds = jax.ShapeDtypeStruct
    return pl.pallas_call(
        body,
        name="mla_gate_bwd",
        grid=(MLA_HEADS // MLA_HPS, t // tm),
        in_specs=[blk(), blk(), blk(512 // wv)],
        out_specs=[blk(), blk(), pl.BlockSpec((None, 8, tm), lambda p, i: (p, 0, i))],
        out_shape=[sds((t, 512), MXU_DTYPE), sds((t, 512), MXU_DTYPE), sds((MLA_HEADS // MLA_HPS, 8, t), F32)],
        compiler_params=_cparams(("parallel", "parallel")),
    )(dy, o, z)


SPAN = 2048
DIL_SCALE = DIL_HD ** -0.5


def _rm_src(u, window, dil):
    nn, r = divmod(u, dil)
    return pl.ds(nn * window + r, DIL_NK, stride=dil) if dil > 1 else pl.ds(u * DIL_NK, DIL_NK)


def _rm_dst(u):
    return pl.ds(u * DIL_NK, DIL_NK)


def _dil_prep_tile(ops, x, g2, c, s1, s2, scale):
    ms = ops.seg_sum(x * x) * (1.0 / DIL_HD)
    return rope(ops, x * lax.rsqrt(ms + EPS) * g2, c, s1, s2, 32) * scale


def _span_blk(base):
    return pl.BlockSpec((SPAN, LANES), lambda s, p: (s, base + p))


def _const_blk(shape):
    return pl.BlockSpec(shape, lambda s, p: (0,) * len(shape))


_DIL_TABLE_SPECS = [pl.BlockSpec((SPAN, LANES), (lambda s, p, blk=blk: (s, blk))) for blk in (3, 4, 5)]


def dil_prep_fwd(z, tabs, gq2, gk2, gi):
    window, dil = DIL_GROUPS[gi]
    t = z.shape[0]

    def body(q_ref, k_ref, v_ref, c_ref, s1_ref, s2_ref, gq_ref, gk_ref, qo_ref, ko_ref, vo_ref):
        for u in range(SPAN // DIL_NK):
            src, dst = _rm_src(u, window, dil), _rm_dst(u)
            c, s1, s2 = c_ref[src, :], s1_ref[src, :], s2_ref[src, :]
            qo_ref[dst, :] = _dil_prep_tile(PLAIN, q_ref[src, :], gq_ref[...], c, s1, s2, DIL_SCALE).astype(MXU_DTYPE)
            ko_ref[dst, :] = _dil_prep_tile(PLAIN, k_ref[src, :], gk_ref[...], c, s1, s2, 1.0).astype(MXU_DTYPE)
            vo_ref[dst, :] = v_ref[src, :].astype(MXU_DTYPE)

    return pl.pallas_call(
        body,
        name=f"dil_prep_fwd_{dil}",
        grid=(t // SPAN, 4),
        in_specs=[_span_blk(4 * gi), _span_blk(12 + 4 * gi), _span_blk(24 + 4 * gi)] + _DIL_TABLE_SPECS
        + [_const_blk((1, LANES)), _const_blk((1, LANES))],
        out_specs=[_span_blk(0)] * 3,
        out_shape=[jax.ShapeDtypeStruct((t, 512), MXU_DTYPE)] * 3,
        compiler_params=_cparams(("parallel", "parallel")),
    )(z, z, z, tabs, tabs, tabs, gq2, gk2)


def dil_prep_bwd(z, tabs, gq2, gk2, dq, dk, dv, gi):
    window, dil = DIL_GROUPS[gi]
    t = z.shape[0]

    def body(q_ref, k_ref, c_ref, s1_ref, s2_ref, gq_ref, gk_ref, dq_ref, dk_ref, dv_ref, dzq_ref, dzk_ref, dzv_ref, dgq_ref,
             dgk_ref, sq, sk, sv):
        @pl.when((pl.program_id(0) == 0) & (pl.program_id(1) == 0))
        def _():
            dgq_ref[...] = jnp.zeros((1, LANES), F32)
            dgk_ref[...] = jnp.zeros((1, LANES), F32)

        dgs = [jnp.zeros((1, LANES), F32), jnp.zeros((1, LANES), F32)]
        for u in range(SPAN // DIL_NK):
            src, dst = _rm_src(u, window, dil), _rm_dst(u)
            c, s1, s2 = c_ref[src, :], s1_ref[src, :], s2_ref[src, :]
            for idx, (x_ref, g_ref, ct_ref, scr, scale) in enumerate(((q_ref, gq_ref, dq_ref, sq, DIL_SCALE),
                                                                      (k_ref, gk_ref, dk_ref, sk, 1.0))):
                _, vjp = jax.vjp(lambda xv, gv, sc=scale: _dil_prep_tile(DIFF, xv, gv, c, s1, s2, sc), x_ref[src, :], g_ref[...])
                dx, dg = vjp(ct_ref[dst, :])
                scr[src, :] = dx
                dgs[idx] = dgs[idx] + dg
            sv[src, :] = dv_ref[dst, :]
        dgq_ref[...] += dgs[0] + pltpu.roll(dgs[0], DIL_HD, 1)
        dgk_ref[...] += dgs[1] + pltpu.roll(dgs[1], DIL_HD, 1)
        for c0 in range(0, SPAN, 256):
            rows = slice(c0, c0 + 256)
            dzq_ref[rows, :] = sq[rows, :].astype(MXU_DTYPE)
            dzk_ref[rows, :] = sk[rows, :].astype(MXU_DTYPE)
            dzv_ref[rows, :] = sv[rows, :].astype(MXU_DTYPE)

    sds = jax.ShapeDtypeStruct
    return pl.pallas_call(
        body,
        name=f"dil_prep_bwd_{dil}",
        grid=(t // SPAN, 4),
        in_specs=[_span_blk(4 * gi), _span_blk(12 + 4 * gi)] + _DIL_TABLE_SPECS
        + [_const_blk((1, LANES)), _const_blk((1, LANES)), _span_blk(0), _span_blk(0), _span_blk(0)],
        out_specs=[_span_blk(0)] * 3 + [_const_blk((1, LANES))] * 2,
        out_shape=[sds((t, 512), MXU_DTYPE)] * 3 + [sds((1, LANES), F32)] * 2,
        scratch_shapes=[pltpu.VMEM((SPAN, LANES), F32)] * 3,
        compiler_params=_cparams(("arbitrary", "arbitrary")),
    )(z, z, tabs, tabs, tabs, gq2, gk2, dq, dk, dv)


def _band_masks():
    qi = lax.broadcasted_iota(jnp.int32, (DIL_NK, DIL_NK), 0)
    ki = lax.broadcasted_iota(jnp.int32, (DIL_NK, DIL_NK), 1)
    return (ki >= qi), (ki <= qi)


def _pair_heads():
    return [(pair, hh) for pair in range(4) for hh in range(2)]


def _pair_lanes(pair):
    return slice(LANES * pair, LANES * (pair + 1))


def _zero_other_head(mask, x):
    return jnp.where(mask, x, jnp.zeros_like(x))


def dil_attn_fwd(name, q, k, v, dil):
    t = q.shape[0]

    def body(q_ref, kp_ref, kc_ref, vp_ref, vc_ref, o_ref, lse_ref):
        b = pl.program_id(0)
        mprev, mcur = _band_masks()
        mprev = mprev & (b >= dil)
        hm = _head_masks()
        heads = _pair_heads()
        qh = [_zero_other_head(hm[hh], q_ref[:, _pair_lanes(pair)]) for pair, hh in heads]
        sps = [_dot(qh[i], kp_ref[:, _pair_lanes(pair)], NT) for i, (pair, _) in enumerate(heads)]
        scs = [_dot(qh[i], kc_ref[:, _pair_lanes(pair)], NT) for i, (pair, _) in enumerate(heads)]
        o = [jnp.zeros((DIL_NK, LANES), F32) for _ in range(4)]
        lse = [jnp.zeros((DIL_NK, LANES), F32) for _ in range(4)]
        for i, (pair, hh) in enumerate(heads):
            sp, sc = jnp.where(mprev, sps[i], NEG), jnp.where(mcur, scs[i], NEG)
            m = jnp.maximum(jnp.max(sp, axis=1, keepdims=True), jnp.max(sc, axis=1, keepdims=True))
            ep, ec = jnp.exp(sp - m), jnp.exp(sc - m)
            den = jnp.sum(ep, axis=1, keepdims=True) + jnp.sum(ec, axis=1, keepdims=True)
            lanes = _pair_lanes(pair)
            oh = _dot(ep, _zero_other_head(hm[hh], vp_ref[:, lanes]), NN) + _dot(ec, _zero_other_head(hm[hh], vc_ref[:, lanes]), NN)
            o[pair] = o[pair] + oh / den
            lse[pair] = jnp.where(hm[hh], m + jnp.log(den), lse[pair])
        for pair in range(4):
            o_ref[:, _pair_lanes(pair)] = o[pair]
            lse_ref[:, _pair_lanes(pair)] = lse[pair]

    cur = pl.BlockSpec((DIL_NK, 512), lambda b: (b, 0))
    prev = pl.BlockSpec((DIL_NK, 512), lambda b: (jnp.maximum(b - dil, 0), 0))
    sds = jax.ShapeDtypeStruct
    return pl.pallas_call(
        body,
        name=name,
        grid=(t // DIL_NK,),
        in_specs=[cur, prev, cur, prev, cur],
        out_specs=[cur, cur],
        out_shape=[sds((t, 512), F32), sds((t, 512), F32)],
        compiler_params=_cparams(("parallel",)),
    )(q, k, k, v, v)


def dil_attn_bwd(name, q, k, v, do, lse, dd, dil):
    t = q.shape[0]
    nblk = t // DIL_NK

    def body(q_ref, do_ref, l_ref, d_ref, kp_ref, kc_ref, vp_ref, vc_ref, dq_ref, dk_ref, dv_ref):
        b = pl.program_id(0)

        @pl.when(b == 0)
        def _():
            dk_ref[...] = jnp.zeros(dk_ref.shape, F32)
            dv_ref[...] = jnp.zeros(dv_ref.shape, F32)

        mprev, mcur = _band_masks()
        mprev = mprev & (b >= dil)
        rows_c = pl.ds(pl.multiple_of(b * DIL_NK, DIL_NK), DIL_NK)
        rows_p = pl.ds(pl.multiple_of(jnp.maximum(b - dil, 0) * DIL_NK, DIL_NK), DIL_NK)
        hm = _head_masks()
        heads = _pair_heads()
        ln_ = [_pair_lanes(pair) for pair, _ in heads]
        qh = [_zero_other_head(hm[hh], q_ref[:, ln_[i]]) for i, (_, hh) in enumerate(heads)]
        doh = [_zero_other_head(hm[hh], do_ref[:, ln_[i]]) for i, (_, hh) in enumerate(heads)]
        idx = range(len(heads))
        s_p = [_dot(qh[i], kp_ref[:, ln_[i]], NT) for i in idx]
        s_c = [_dot(qh[i], kc_ref[:, ln_[i]], NT) for i in idx]
        dp_p = [_dot(doh[i], vp_ref[:, ln_[i]], NT) for i in idx]
        dp_c = [_dot(doh[i], vc_ref[:, ln_[i]], NT) for i in idx]
        zero = lambda: [jnp.zeros((DIL_NK, LANES), F32) for _ in range(4)]
        dq, dk_p, dk_c, dv_p, dv_c = zero(), zero(), zero(), zero(), zero()
        for i, (pair, hh) in enumerate(heads):
            lse_h, d_h = _row_scalar(l_ref[:, ln_[i]], hm[hh]), _row_scalar(d_ref[:, ln_[i]], hm[hh])
            pp = jnp.exp(jnp.where(mprev, s_p[i] - lse_h, NEG))
            pc = jnp.exp(jnp.where(mcur, s_c[i] - lse_h, NEG))
            dsp, dsc = pp * (dp_p[i] - d_h), pc * (dp_c[i] - d_h)
            dq[pair] = (dq[pair] + _dot(dsp, _zero_other_head(hm[hh], kp_ref[:, ln_[i]]), NN)
                        + _dot(dsc, _zero_other_head(hm[hh], kc_ref[:, ln_[i]]), NN))
            dv_p[pair] = dv_p[pair] + _dot(pp, doh[i], TN)
            dv_c[pair] = dv_c[pair] + _dot(pc, doh[i], TN)
            dk_p[pair] = dk_p[pair] + _dot(dsp, qh[i], TN)
            dk_c[pair] = dk_c[pair] + _dot(dsc, qh[i], TN)
        for pair in range(4):
            lanes = _pair_lanes(pair)
            dq_ref[:, lanes] = dq[pair]
            dk_ref[rows_p, lanes] += dk_p[pair]
            dv_ref[rows_p, lanes] += dv_p[pair]
            dk_ref[rows_c, lanes] += dk_c[pair]
            dv_ref[rows_c, lanes] += dv_c[pair]

    cur = pl.BlockSpec((DIL_NK, 512), lambda b: (b, 0))
    prev = pl.BlockSpec((DIL_NK, 512), lambda b: (jnp.maximum(b - dil, 0), 0))
    whole = pl.BlockSpec((t, 512), lambda b: (0, 0))
    sds = jax.ShapeDtypeStruct
    return pl.pallas_call(
        body,
        name=name,
        grid=(nblk,),
        in_specs=[cur, cur, cur, cur, prev, cur, prev, cur],
        out_specs=[cur, whole, whole],
        out_shape=[sds((t, 512), F32)] * 3,
        compiler_params=_cparams(("arbitrary",)),
    )(q, do, lse, dd, k, k, v, v)


def dil_combine(os_, lses, z):
    t = z.shape[0]

    def body(o1_ref, l1_ref, o2_ref, l2_ref, o3_ref, l3_ref, g_ref, y_ref, o_ref, lse_ref, yt_ref, so2, sl2, so3, sl3):
        for (window, dil), o_in, l_in, so, sl in ((DIL_GROUPS[1], o2_ref, l2_ref, so2, sl2), (DIL_GROUPS[2], o3_ref, l3_ref, so3, sl3)):
            for u in range(SPAN // DIL_NK):
                src, dst = _rm_src(u, window, dil), _rm_dst(u)
                so[src, :] = o_in[dst, :]
                sl[src, :] = l_in[dst, :]
        for c0 in range(0, SPAN, 256):
            rows = slice(c0, c0 + 256)
            la, lb, lc = l1_ref[rows, :], sl2[rows, :], sl3[rows, :]
            mx = jnp.maximum(jnp.maximum(la, lb), lc)
            wa, wb, wc = jnp.exp(la - mx), jnp.exp(lb - mx), jnp.exp(lc - mx)
            tot = wa + wb + wc
            o = (wa * o1_ref[rows, :] + wb * so2[rows, :] + wc * so3[rows, :]) / tot
            y = o * _silu_and_grad(g_ref[rows, :])[0]
            y_ref[rows, :] = y.astype(MXU_DTYPE)
            o_ref[rows, :] = o
            lse_ref[rows, :] = mx + jnp.log(tot)
            yt_ref[:, rows] = y.T.astype(MXU_DTYPE)

    sds = jax.ShapeDtypeStruct
    return pl.pallas_call(
        body,
        name="dil_combine",
        grid=(t // SPAN, 4),
        in_specs=[_span_blk(0)] * 6 + [_span_blk(36)],
        out_specs=[_span_blk(0)] * 3 + [pl.BlockSpec((LANES, SPAN), lambda s, p: (p, s))],
        out_shape=[sds((t, 512), MXU_DTYPE), sds((t, 512), F32), sds((t, 512), F32), sds((512, t), MXU_DTYPE)],
        scratch_shapes=[pltpu.VMEM((SPAN, LANES), F32)] * 4,
        compiler_params=_cparams(("parallel", "parallel")),
    )(os_[0], lses[0], os_[1], lses[1], os_[2], lses[2], z)


def dil_gate_bwd(dy, o, z, lse):
    t = dy.shape[0]

    def body(dy_ref, o_ref, g_ref, lse_ref, dzg_ref, do1_ref, dd1_ref, do2_ref, dd2_ref, l2_ref, do3_ref, dd3_ref, l3_ref, do_scr):
        for c0 in range(0, SPAN, 256):
            rows = slice(c0, c0 + 256)
            dyv, ov = dy_ref[rows, :], o_ref[rows, :]
            silu, dsilu = _silu_and_grad(g_ref[rows, :])
            do = dyv * silu
            do_scr[rows, :] = do
            do1_ref[rows, :] = do.astype(MXU_DTYPE)
            dd1_ref[rows, :] = _seg_sum_impl(do * ov)
            dzg_ref[rows, :] = (dyv * ov * dsilu).astype(MXU_DTYPE)
        for (window, dil), do_o, dd_o, l_o in ((DIL_GROUPS[1], do2_ref, dd2_ref, l2_ref), (DIL_GROUPS[2], do3_ref, dd3_ref, l3_ref)):
            for u in range(SPAN // DIL_NK):
                src, dst = _rm_src(u, window, dil), _rm_dst(u)
                do_o[dst, :] = do_scr[src, :].astype(MXU_DTYPE)
                dd_o[dst, :] = dd1_ref[src, :]
                l_o[dst, :] = lse_ref[src, :]

    sds = jax.ShapeDtypeStruct
    f32, mxu = sds((t, 512), F32), sds((t, 512), MXU_DTYPE)
    return pl.pallas_call(
        body,
        name="dil_gate_bwd",
        grid=(t // SPAN, 4),
        in_specs=[_span_blk(0), _span_blk(0), _span_blk(36), _span_blk(0)],
        out_specs=[_span_blk(0)] * 9,
        out_shape=[mxu, mxu, f32, mxu, f32, f32, mxu, f32, f32],
        scratch_shapes=[pltpu.VMEM((SPAN, LANES), F32)],
        compiler_params=_cparams(("parallel", "parallel")),
    )(dy, o, z, lse)


def _merge_tile(p0, p1, p2, z0, z1, z2, b0, b1, b2):
    return _sigmoid(z0 + b0) * p0 + _sigmoid(z1 + b1) * p1 + _sigmoid(z2 + b2) * p2


def _merge_ins(ps, z, b):
    w = 256
    return ([_row(p, w, 0, True) for p in ps] + [_row(z, w, 4 * i, True) for i in range(3)]
            + [_const(b, w, 4 * i, True) for i in range(3)])


def merge_fwd(ps, z, b):
    def fn(ins, outs, _):
        merged = _merge_tile(*[r[...] for r in ins])
        outs[0][...] = merged.astype(MXU_DTYPE)
        outs[1][...] = merged.T.astype(MXU_DTYPE)

    return rowwise("merge_fwd", fn, z.shape[0], ncol=4, ins=_merge_ins(ps, z, b), outs=[(1024, 256, MXU_DTYPE, True)],
                   touts=[(1024, 256, MXU_DTYPE, True)])


def merge_bwd(dm, ps, z, b):
    def fn(ins, outs, accs):
        _, vjp = jax.vjp(_merge_tile, *[r[...] for r in ins[:9]])
        grads = vjp(ins[9][...])
        for i in range(3):
            outs[i][...] = grads[i].astype(MXU_DTYPE)
            outs[3 + i][...] = grads[3 + i].astype(MXU_DTYPE)
            accs[i][...] += grads[6 + i]

    return rowwise("merge_bwd", fn, z.shape[0], ncol=4, ins=_merge_ins(ps, z, b) + [_row(dm, 256, 0, True)],
                   outs=[(1024, 256, MXU_DTYPE, True)] * 6, accs=[(1, 1024, 256, True)] * 3)


EW_BLOCK_BYTES = 2**21


def _tile2d(r, c):
    if r * c * 4 <= EW_BLOCK_BYTES:
        return r, c
    for tr in (512, 256, 128, 64, 32):
        if r % tr == 0 and tr * c * 4 <= EW_BLOCK_BYTES:
            return tr, c
    for tc in (1024, 512, 256, 128):
        if c % tc == 0 and r * tc * 4 <= EW_BLOCK_BYTES:
            return r, tc
    raise ValueError((r, c))


def adamw(name, w, g, m, v):
    shape = w.shape
    c = shape[-1]
    r = w.size // c
    tr, tc = _tile2d(r, c)
    assert tc == c
    c1, c2 = 1.0 - ADAM_B1 ** ADAM_STEP, 1.0 - ADAM_B2 ** ADAM_STEP

    def body(w_ref, g_ref, m_ref, v_ref, d_ref, mo_ref, vo_ref):
        gv = g_ref[...]
        mn = ADAM_B1 * m_ref[...] + (1.0 - ADAM_B1) * gv
        vn = ADAM_B2 * v_ref[...] + (1.0 - ADAM_B2) * (gv * gv)
        d_ref[...] = -ADAM_LR * ((mn / c1) / (jnp.sqrt(vn / c2) + ADAM_EPS) + ADAM_WD * w_ref[...])
        mo_ref[...] = mn
        vo_ref[...] = vn

    spec = pl.BlockSpec((tr, c), lambda i: (i, 0))
    outs = pl.pallas_call(
        body,
        name=name,
        grid=(r // tr,),
        in_specs=[spec] * 4,
        out_specs=[spec] * 3,
        out_shape=[jax.ShapeDtypeStruct((r, c), F32)] * 3,
        compiler_params=_cparams(("parallel",)),
    )(*[a.reshape(r, c) for a in (w, g, m, v)])
    return [o.reshape(shape) for o in outs]


def adamw_pair(name, w, reduced, received, m, v, c_idx):
    _, r, c = w.shape
    tr, tc = _tile2d(r, c)
    c1, c2 = 1.0 - ADAM_B1 ** ADAM_STEP, 1.0 - ADAM_B2 ** ADAM_STEP

    def body(c_ref, w_ref, r0_ref, x0_ref, r1_ref, x1_ref, m_ref, v_ref, g_ref, d_ref, mo_ref, vo_ref):
        mine = c_ref[0]
        gv = jnp.where(pl.program_id(0) == 0, jnp.where(mine == 0, r0_ref[...], x0_ref[...]),
                       jnp.where(mine == 1, r1_ref[...], x1_ref[...]))
        mn = ADAM_B1 * m_ref[...] + (1.0 - ADAM_B1) * gv
        vn = ADAM_B2 * v_ref[...] + (1.0 - ADAM_B2) * (gv * gv)
        g_ref[...] = gv
        d_ref[...] = -ADAM_LR * ((mn / c1) / (jnp.sqrt(vn / c2) + ADAM_EPS) + ADAM_WD * w_ref[...])
        mo_ref[...] = mn
        vo_ref[...] = vn

    full = pl.BlockSpec((None, tr, tc), lambda h, i, j, cr: (h, i, j))
    half = pl.BlockSpec((tr, tc), lambda h, i, j, cr: (i, j))
    return pl.pallas_call(
        body,
        name=name,
        grid_spec=pltpu.PrefetchScalarGridSpec(
            num_scalar_prefetch=1,
            grid=(2, r // tr, c // tc),
            in_specs=[full, half, half, half, half, full, full],
            out_specs=[full] * 4,
        ),
        out_shape=[jax.ShapeDtypeStruct((2, r, c), F32)] * 4,
        compiler_params=_cparams(("parallel", "parallel", "parallel")),
    )(c_idx, w, reduced[0], received[0], reduced[1], received[1], m, v)


def sum_pair(name, g, la, out_dtype):
    _, r, c = g.shape
    tr, tc = _tile2d(r, c)

    def body(g_ref, la_ref, o_ref):
        o_ref[...] = (g_ref[...] + la_ref[...]).astype(o_ref.dtype)

    spec = pl.BlockSpec((None, tr, tc), lambda s, i, j: (s, i, j))
    return pl.pallas_call(
        body,
        name=name,
        grid=(4, r // tr, c // tc),
        in_specs=[spec, spec],
        out_specs=spec,
        out_shape=jax.ShapeDtypeStruct((4, r, c), out_dtype),
        compiler_params=_cparams(("parallel", "parallel", "parallel")),
    )(g, la)


def sum_chips(name, g, la, lb, s_idx):
    _, r, c = g.shape
    tr, tc = _tile2d(r, c)

    def body(s_ref, g_ref, la_ref, l0_ref, l1_ref, l2_ref, o_ref):
        own = g_ref[...] + la_ref[...]
        o_ref[...] = ((own + l0_ref[...].astype(F32)) + l1_ref[...].astype(F32)) + l2_ref[...].astype(F32)

    own_spec = pl.BlockSpec((None, tr, tc), lambda i, j, sr: (sr[0], i, j))
    lspec = lambda k: pl.BlockSpec((None, tr, tc), lambda i, j, sr: (k, i, j))
    return pl.pallas_call(
        body,
        name=name,
        grid_spec=pltpu.PrefetchScalarGridSpec(
            num_scalar_prefetch=1,
            grid=(r // tr, c // tc),
            in_specs=[own_spec, own_spec, lspec(0), lspec(1), lspec(2)],
            out_specs=pl.BlockSpec((tr, tc), lambda i, j, sr: (i, j)),
        ),
        out_shape=jax.ShapeDtypeStruct((r, c), F32),
        compiler_params=_cparams(("parallel", "parallel")),
    )(s_idx, g, la, lb, lb, lb)


def _place():
    x, y, c = lax.axis_index("x"), lax.axis_index("y"), lax.axis_index("c")
    chips = [(1 - x, y), (x, 1 - y), (1 - x, 1 - y)]
    return x, y, c, chips


HBM_SPEC = pl.BlockSpec(memory_space=pltpu.HBM)


def _comm_call(name, body, ins, out_shapes, n_sem, n_local):
    return pl.pallas_call(
        body,
        name=name,
        in_specs=[HBM_SPEC] * len(ins),
        out_specs=[HBM_SPEC] * len(out_shapes),
        out_shape=out_shapes,
        scratch_shapes=[pltpu.SemaphoreType.DMA((n_sem,)), pltpu.SemaphoreType.DMA((n_sem,)),
                        pltpu.SemaphoreType.DMA((max(n_local, 1),))],
    )(*ins)


def _layer_gather(ins, outs, send, recv, layer):
    n = len(ins)
    x, y, c, chips = _place()
    s = 2 * x + y
    sib = (x, y, 1 - c)
    active = c == layer

    def rc(a, k, src, dst, dev):
        return pltpu.make_async_remote_copy(src_ref=src, dst_ref=dst, send_sem=send.at[6 * a + k], recv_sem=recv.at[6 * a + k],
                                            device_id=dev, device_id_type=MESH)

    def first_hop():
        return [rc(a, j, ins[a], outs[a].at[s], (*chip, c)) for j, chip in enumerate(chips) for a in range(n)]

    def start():
        @pl.when(active)
        def _():
            for cp in first_hop():
                cp.start()

    def finish():
        @pl.when(active)
        def _():
            forwards = []
            for j, (cx, cy) in enumerate(chips):
                for a in range(n):
                    landed = outs[a].at[2 * cx + cy]
                    rc(a, j, landed, landed, sib).wait_recv()
                    forwards.append(rc(a, 3 + j, landed, landed, sib))
                    forwards[-1].start()
            for cp in first_hop() + forwards:
                cp.wait_send()

        @pl.when(jnp.logical_not(active))
        def _():
            for j, (cx, cy) in enumerate(chips):
                for a in range(n):
                    other = outs[a].at[2 * cx + cy]
                    rc(a, 3 + j, other, other, sib).wait_recv()

    return start, finish


def _gather_shapes(ws):
    return [jax.ShapeDtypeStruct((4,) + w.shape, w.dtype) for w in ws]


def gather_hook(ws, layer):
    return (ws, _gather_shapes(ws), 6 * len(ws), lambda i, o, s, r: _layer_gather(i, o, s, r, layer))


def allgather_layer(ws, layer):
    n = len(ws)

    def body(*refs):
        send, recv, _ = refs[2 * n:]
        start, finish = _layer_gather(refs[:n], refs[n:2 * n], send, recv, layer)
        start()
        finish()

    return _comm_call(f"allgather_layer{layer}", body, ws, _gather_shapes(ws), 6 * n, 0)


def exchange_sibling(gs, layer, tag=""):
    n = len(gs)

    def body(*refs):
        send, recv, _ = refs[2 * n:]
        start, finish = _sibling_exchange(refs[:n], refs[n:2 * n], send, recv, layer)
        start()
        finish()

    return _comm_call(f"exchange_sibling{layer}{tag}", body, gs, [jax.ShapeDtypeStruct(g.shape, g.dtype) for g in gs], n, 0)


def _sibling_exchange(ins, outs, send, recv, layer):
    x, y, c, _ = _place()

    def copies():
        return [pltpu.make_async_remote_copy(src_ref=ins[a], dst_ref=outs[a], send_sem=send.at[a], recv_sem=recv.at[a],
                                             device_id=(x, y, 1 - c), device_id_type=MESH) for a in range(len(ins))]

    def start():
        @pl.when(c != layer)
        def _():
            for cp in copies():
                cp.start()

    def finish():
        @pl.when(c != layer)
        def _():
            for cp in copies():
                cp.wait_send()

        @pl.when(c == layer)
        def _():
            for cp in copies():
                cp.wait_recv()

    return start, finish


def sibling_hook(gs, layer):
    return (gs, [jax.ShapeDtypeStruct(g.shape, g.dtype) for g in gs], len(gs),
            lambda i, o, s, r: _sibling_exchange(i, o, s, r, layer))


def chips_hook(ps, layer):
    return (ps, _chips_shapes(ps), 3 * len(ps), lambda i, o, s, r: _chips_exchange(i, o, s, r, layer))


def _chips_exchange(ins, outs, send, recv, layer):
    n = len(ins)
    _, _, c, chips = _place()

    def copies():
        return [pltpu.make_async_remote_copy(src_ref=ins[a].at[2 * cx + cy], dst_ref=outs[a].at[j], send_sem=send.at[3 * a + j],
                                             recv_sem=recv.at[3 * a + j], device_id=(cx, cy, c), device_id_type=MESH)
                for j, (cx, cy) in enumerate(chips) for a in range(n)]

    def start():
        @pl.when(c == layer)
        def _():
            for cp in copies():
                cp.start()

    def finish():
        @pl.when(c == layer)
        def _():
            for cp in copies():
                cp.wait()

    return start, finish


def _chips_shapes(ps):
    return [jax.ShapeDtypeStruct((3,) + p.shape[1:], p.dtype) for p in ps]


def exchange_chips(ps, layer, tag=""):
    n = len(ps)

    def body(*refs):
        send, recv, _ = refs[2 * n:]
        start, finish = _chips_exchange(refs[:n], refs[n:2 * n], send, recv, layer)
        start()
        finish()

    return _comm_call(f"exchange_chips{layer}{tag}", body, ps, _chips_shapes(ps), 3 * n, 0)


def exchange_final(rs, layer, small=None):
    n = len(rs)
    ins_all = list(rs) + ([small] if small is not None else [])

    def body(*refs):
        ni = len(ins_all)
        ins, outs = refs[:ni], refs[ni:2 * ni]
        send, recv, lsem = refs[2 * ni:]
        x, y, c, _ = _place()
        s = 2 * x + y
        to_sib = [pltpu.make_async_remote_copy(src_ref=ins[a], dst_ref=outs[a], send_sem=send.at[a], recv_sem=recv.at[a],
                                               device_id=(x, y, 1 - c), device_id_type=MESH) for a in range(n)]
        rel = [(fx, fy) for fx in (0, 1) for fy in (0, 1)]

        def piece(fx, fy, cc, dst_slot):
            k = n + 2 * (2 * fx + fy) + cc
            return pltpu.make_async_remote_copy(src_ref=ins[n], dst_ref=outs[n].at[dst_slot], send_sem=send.at[k], recv_sem=recv.at[k],
                                                device_id=(x ^ fx, y ^ fy, cc), device_id_type=MESH)

        @pl.when(c == layer)
        def _():
            for cp in to_sib:
                cp.start()
            if small is not None:
                own = pltpu.make_async_copy(ins[n], outs[n].at[s], lsem.at[0])
                own.start()
                sends = [piece(fx, fy, cc, s) for fx, fy in rel for cc in (0, 1) if (fx, fy) != (0, 0) or cc != layer]
                for cp in sends:
                    cp.start()
                for fx, fy in rel[1:]:
                    piece(fx, fy, layer, 2 * (x ^ fx) + (y ^ fy)).wait_recv()
                for cp in sends:
                    cp.wait_send()
                own.wait()
            for cp in to_sib:
                cp.wait_send()

        @pl.when(c != layer)
        def _():
            for cp in to_sib:
                cp.wait_recv()
            if small is not None:
                for fx, fy in rel:
                    piece(fx, fy, 1 - layer, 2 * (x ^ fx) + (y ^ fy)).wait_recv()

    out_shapes = [jax.ShapeDtypeStruct(r.shape, r.dtype) for r in rs]
    if small is not None:
        out_shapes.append(jax.ShapeDtypeStruct((4,) + small.shape, small.dtype))
    res = _comm_call(f"exchange_final{layer}", body, ins_all, out_shapes, n + 8, 1)
    return (res[:n], res[n]) if small is not None else (res[:n], None)


def _pad_in_cols(w):
    z = lambda n: jnp.zeros(w.shape[:-1] + (n,), w.dtype)
    return jnp.concatenate([w[..., :KR_OFF], z(64), w[..., KR_OFF:KR_OFF + MLA_ROPE], z(32), w[..., KR_OFF + MLA_ROPE:]], axis=-1)


def _pad_heads(w, real):
    k = w.shape[0]
    return jnp.pad(w.reshape(k, MLA_HEADS, real), ((0, 0), (0, 0), (0, LANES - real))).reshape(k, MLA_HEADS * LANES)


def _pad_gain(g, real):
    return jnp.pad(g.reshape(1, real), ((0, 0), (0, LANES - real)))


def layer_weights(full):
    w_ukv = full["w_ukv"].reshape(128, MLA_HEADS, 2, 64)
    two = lambda g: jnp.concatenate([g, g]).reshape(1, LANES)
    return dict(
        norm_g=full["norm_g"].reshape(1, -1), w_in=full["w_in"], conv_w=full["conv_w"], conv_b=full["conv_b"].reshape(1, -1),
        wgx=full["w_gate_x"], bgx=full["b_gate_x"].reshape(LRU_BLOCKS, 1, LANES),
        wga=full["w_gate_a"], bga=full["b_gate_a"].reshape(LRU_BLOCKS, 1, LANES),
        lam=full["lru_lambda"].reshape(1, -1), w_lru_o=full["w_lru_o"],
        cq_norm_g=full["cq_norm_g"].reshape(1, -1), ckv_norm_g=full["ckv_norm_g"].reshape(1, -1),
        wq=_pad_heads(full["w_uq"], MLA_QK), wk=_pad_heads(w_ukv[:, :, 0].reshape(128, 512), 64),
        wv=w_ukv[:, :, 1].reshape(128, 512),
        gq=_pad_gain(full["mla_q_norm_g"], MLA_QK), gk=_pad_gain(full["mla_k_norm_g"], MLA_QK),
        w_mla_o=full["w_mla_o"], gq2=two(full["dil_q_norm_g"]), gk2=two(full["dil_k_norm_g"]),
        w_dil_o=full["w_dil_o"], b_merge=full["b_merge"].reshape(1, -1), w_out=full["w_out"],
    )


def layer_fwd(x, w, tabs, gather=None):
    h, ht = rmsnorm_fwd(x, w["norm_g"])
    z_lru = mm_nn("in_proj_lru", h, w["w_in"], n=2048, n_off=G_LRU)
    z_mla = mm_nn("in_proj_mla", h, w["w_in"], n=1024, n_off=G_MLA)
    z_dil = mm_nn("in_proj_dil", h, w["w_in"], n=5120, n_off=G_DIL)
    z_mrg = mm_nn("in_proj_mrg", h, w["w_in"], n=3072, n_off=G_MRG)
    rest_hook = None if gather is None else gather_hook(gather[0][1:], gather[1])
    hs, y_lru, yt_lru, *ga_rest = lru_fwd(z_lru, w["conv_w"], w["conv_b"], w["wgx"], w["bgx"], w["wga"], w["bga"], w["lam"],
                                          comm=rest_hook)
    qm, km, vm, vtm = mla_prep_fwd(z_mla, tabs, w)
    o_mla, y_mla, yt_mla, lse_mla, *ga_first = mla_attn_fwd(qm, km, vtm, z_mla,
                                                            gather=None if gather is None else (gather[0][:1], gather[1]))
    gathered = ga_first + ga_rest
    os_, lses, dil_rm = [], [], []
    for gi, (_, dil) in enumerate(DIL_GROUPS):
        qkv = dil_prep_fwd(z_dil, tabs, w["gq2"], w["gk2"], gi)
        o, lse = dil_attn_fwd(f"dil_attn_fwd_{dil}", *qkv, dil)
        os_.append(o)
        lses.append(lse)
        dil_rm.append(qkv)
    y_dil, o_dil, lse_dil, yt_dil = dil_combine(os_, lses, z_dil)
    ps = [mm_nn("proj_lru", y_lru, w["w_lru_o"]), mm_nn("proj_mla", y_mla, w["w_mla_o"]), mm_nn("proj_dil", y_dil, w["w_dil_o"])]
    merged, merged_t = merge_fwd(ps, z_mrg, w["b_merge"])
    out = mm_nn("out_proj", merged, w["w_out"], add=x)
    res = dict(x=x, ht=ht, z_lru=z_lru, z_mla=z_mla, z_dil=z_dil, z_mrg=z_mrg, hs=hs, yt_lru=yt_lru, qm=qm, km=km, vm=vm, o_mla=o_mla,
               yt_mla=yt_mla, lse_mla=lse_mla, dil_rm=dil_rm, yt_dil=yt_dil, o_dil=o_dil, lse_dil=lse_dil, ps=ps, merged_t=merged_t)
    return out, res, gathered


def layer_bwd(dout, r, w, tabs, prev=None, own=None):
    g = {}
    dmerged = mm_nt("out_proj_dx", dout, w["w_out"])
    g["w_out"] = mm_nn("out_proj_dw", r["merged_t"], dout, tm=1024, tn=512, tk=2048)
    dp0, dp1, dp2, dzm0, dzm1, dzm2, db0, db1, db2 = merge_bwd(dmerged, r["ps"], r["z_mrg"], w["b_merge"])
    g["b_merge"] = jnp.concatenate([db0, db1, db2], axis=1).reshape(-1)
    dy_lru = mm_nt("proj_lru_dx", dp0, w["w_lru_o"])
    dy_mla = mm_nt("proj_mla_dx", dp1, w["w_mla_o"])
    dy_dil = mm_nt("proj_dil_dx", dp2, w["w_dil_o"])
    g["w_lru_o"] = mm_nn("proj_lru_dw", r["yt_lru"], dp0, **DW_TILES)
    g["w_mla_o"] = mm_nn("proj_mla_dw", r["yt_mla"], dp1, **DW_TILES)
    g["w_dil_o"] = mm_nn("proj_dil_dw", r["yt_dil"], dp2, **DW_TILES)
    dzx, dzg_lru, dwgx, dbgx, dwga, dbga, dlam, dcw, dcb, *la_prev = lru_bwd(
        r["z_lru"], r["hs"], dy_lru, w["conv_w"], w["conv_b"], w["wgx"], w["bgx"], w["wga"], w["bga"], w["lam"],
        comm=None if prev is None else sibling_hook(prev[0], prev[1]))
    chips = None if prev is None else (prev[2](la_prev), prev[1])
    g.update(w_gate_x=dwgx, b_gate_x=dbgx.reshape(LRU_BLOCKS, LANES), w_gate_a=dwga, b_gate_a=dbga.reshape(LRU_BLOCKS, LANES),
             lru_lambda=dlam.reshape(-1), conv_w=dcw, conv_b=dcb.reshape(-1))
    do_m, dzg_mla, dd_m = mla_gate_bwd(dy_mla, r["o_mla"], r["z_mla"])
    dq_m, dk_m, dv_m, *lb_prev = mla_attn_bwd(r["qm"], r["km"], r["vm"], do_m, r["lse_mla"], dd_m, chips=chips)
    dz_mla3, dg_cq, dg_ckv, dwq, dwk, dwv, dgq, dgk = mla_prep_bwd(r["z_mla"], tabs, w, dq_m, dk_m, dv_m)
    g.update(cq_norm_g=dg_cq.reshape(-1), ckv_norm_g=dg_ckv.reshape(-1), mla_q_norm_g=dgq[0, :MLA_QK], mla_k_norm_g=dgk[0, :MLA_QK])
    g["w_uq"] = dwq.reshape(256, MLA_HEADS, LANES)[:, :, :MLA_QK].reshape(256, MLA_HEADS * MLA_QK)
    g["w_ukv"] = jnp.concatenate([dwk.reshape(128, MLA_HEADS, LANES)[:, :, :64], dwv.reshape(128, MLA_HEADS, 64)], axis=2).reshape(128, 1024)
    dzg_dil, do1, dd1, do2, dd2, l2, do3, dd3, l3 = dil_gate_bwd(dy_dil, r["o_dil"], r["z_dil"], r["lse_dil"])
    stats = [(do1, r["lse_dil"], dd1), (do2, l2, dd2), (do3, l3, dd3)]
    dzq, dzk, dzv, dgq2, dgk2 = [], [], [], [], []
    for gi, (_, dil) in enumerate(DIL_GROUPS):
        dq, dk, dv = dil_attn_bwd(f"dil_attn_bwd_{dil}", *r["dil_rm"][gi], *stats[gi], dil)
        parts = dil_prep_bwd(r["z_dil"], tabs, w["gq2"], w["gk2"], dq, dk, dv, gi)
        for acc, part in zip((dzq, dzk, dzv, dgq2, dgk2), parts):
            acc.append(part)
    g.update(dil_q_norm_g=sum(dgq2)[0, :DIL_HD], dil_k_norm_g=sum(dgk2)[0, :DIL_HD])
    dz = jnp.concatenate([dzx, dzg_lru, dz_mla3, dzg_mla] + dzq + dzk + dzv + [dzg_dil, dzm0, dzm1, dzm2], axis=1)
    g["w_in"] = in_proj_dw(r["ht"], dz)
    own_hook = None if own is None else chips_hook(*own(g))
    out = mm_nt("in_proj_dx", dz, w["w_in"], tm=1024, tn=1024, tk=IN_PAD // 4, comm=own_hook)
    dh, lb_own = (out, []) if own_hook is None else (out[0], out[1:])
    dx, dng = rmsnorm_bwd(r["x"], dh, dout, w["norm_g"])
    g["norm_g"] = dng.reshape(-1)
    return dx, g, dict(la_prev=la_prev, lb_prev=lb_prev, lb_own=lb_own)


def local_step(x, positions, target, full0, full1=None, gather1=None, reduce=None):
    tabs = rope_tables(positions.reshape(-1, 1))
    ws, ress = [], []
    for l in range(2):
        if l == 0:
            ws.append(layer_weights(full0))
            x, res, gathered = layer_fwd(x, ws[0], tabs, gather=None if gather1 is None else (gather1[0], 1))
        else:
            ws.append(layer_weights(full1 if gather1 is None else gather1[1](gathered)))
            x, res, _ = layer_fwd(x, ws[1], tabs)
        ress.append(res)
    dy, loss = loss_head(x, target)
    dy, grads1, _ = layer_bwd(dy, ress[1], ws[1], tabs)
    prev1, own0 = (None, None) if reduce is None else (reduce[0](grads1), reduce[1])
    dy, grads0, landed = layer_bwd(dy, ress[0], ws[0], tabs, prev=prev1, own=own0)
    return loss, dy, [grads0, grads1], landed


WEIGHTS = ["norm_g", "w_in", "conv_w", "conv_b", "w_gate_x", "b_gate_x", "w_gate_a", "b_gate_a", "lru_lambda", "w_lru_o", "cq_norm_g",
           "ckv_norm_g", "w_uq", "w_ukv", "mla_q_norm_g", "mla_k_norm_g", "w_mla_o", "dil_q_norm_g", "dil_k_norm_g", "w_dil_o", "b_merge",
           "w_out"]
SHARDED = {"w_in": 2, "conv_w": 2, "w_lru_o": 1, "w_uq": 2, "w_ukv": 2, "w_mla_o": 2, "w_dil_o": 2, "w_out": 1}
REPLICATED = [n for n in WEIGHTS if n not in SHARDED]
SMALL_ROWS = 144


def kernel(x, positions, norm_g, w_in, conv_w, conv_b, w_gate_x, b_gate_x, w_gate_a, b_gate_a, lru_lambda, w_lru_o, cq_norm_g, ckv_norm_g, w_uq, w_ukv, mla_q_norm_g, mla_k_norm_g, w_mla_o, dil_q_norm_g, dil_k_norm_g, w_dil_o, b_merge, w_out, loss_target, m_norm_g, m_w_in, m_conv_w, m_conv_b, m_w_gate_x, m_b_gate_x, m_w_gate_a, m_b_gate_a, m_lru_lambda, m_w_lru_o, m_cq_norm_g, m_ckv_norm_g, m_w_uq, m_w_ukv, m_mla_q_norm_g, m_mla_k_norm_g, m_w_mla_o, m_dil_q_norm_g, m_dil_k_norm_g, m_w_dil_o, m_b_merge, m_w_out, v_norm_g, v_w_in, v_conv_w, v_conv_b, v_w_gate_x, v_b_gate_x, v_w_gate_a, v_b_gate_a, v_lru_lambda, v_w_lru_o, v_cq_norm_g, v_ckv_norm_g, v_w_uq, v_w_ukv, v_mla_q_norm_g, v_mla_k_norm_g, v_w_mla_o, v_dil_q_norm_g, v_dil_k_norm_g, v_w_dil_o, v_b_merge, v_w_out):
    args = locals()
    w = {n: args[n] for n in WEIGHTS}
    m = {n: args["m_" + n] for n in WEIGHTS}
    v = {n: args["v_" + n] for n in WEIGHTS}
    my_c = lax.axis_index("c").astype(jnp.int32)
    my_s = (2 * lax.axis_index("x") + lax.axis_index("y")).astype(jnp.int32)
    c_idx = my_c.reshape(1)
    s_idx = my_s.reshape(1)

    names = list(SHARDED)
    wire = [[w[n][l] if n == "conv_w" else w[n][l].astype(BF16) for n in names] for l in range(2)]

    def assemble(l, gathered):
        full = {n: w[n][l] for n in REPLICATED}
        for n, ga, own in zip(names, gathered, wire[l]):
            parts = [jnp.where(my_s == s, own, ga[s]) for s in range(4)]
            if n == "w_in":
                parts = [_pad_in_cols(parts[0])] + parts[1:]
            full[n] = jnp.concatenate(parts, axis=SHARDED[n] - 1)
        return full

    full0 = assemble(0, allgather_layer(wire[0], 0))

    def shards_of(grads_l):
        gs = []
        for n in names:
            g_ = grads_l[n]
            if n == "w_in":
                gs.append(g_.reshape(4, IN_WIDTH // 4, D_MODEL))
            else:
                parts = jnp.stack(jnp.split(g_, 4, axis=SHARDED[n] - 1))
                gs.append(parts.reshape(4, -1, parts.shape[-1]))
        return gs

    def pair_sums(l, gs, la, tags):
        dts = [BF16 if g_.size >= 2**19 and t != "small" else F32 for g_, t in zip(gs, tags)]
        return [sum_pair(f"sum_pair{l}_{t}", g_, l_, dt) for t, g_, l_, dt in zip(tags, gs, la, dts)]

    def chip_sums(l, gs, la, lb, tags):
        return [sum_chips(f"sum_chips{l}_{t}", g_, l_, b_, s_idx) for t, g_, l_, b_ in zip(tags, gs, la, lb)]

    stash = {}

    def prev1(grads1):
        stash["gs1"] = shards_of(grads1)
        return stash["gs1"], 1, lambda la: pair_sums(1, stash["gs1"], la, names)

    def own0(grads0):
        stash["gs0"] = shards_of(grads0)
        stash["la0"] = exchange_sibling(stash["gs0"], 0)
        return pair_sums(0, stash["gs0"], stash["la0"], names), 0

    loss, grad_x, grads, landed = local_step(x[0], positions[0], loss_target[0], full0,
                                             gather1=(wire[1], functools.partial(assemble, 1)), reduce=(prev1, own0))
    loss = lax.psum(loss, ("x", "y", "c"))
    reduced1 = chip_sums(1, stash["gs1"], landed["la_prev"], landed["lb_prev"], names)
    received1, _ = exchange_final(reduced1, 1)
    reduced0 = chip_sums(0, stash["gs0"], stash["la0"], landed["lb_own"], names)

    flat = jnp.concatenate([jnp.stack([grads[0][n], grads[1][n]]).reshape(-1) for n in REPLICATED])
    small = [jnp.pad(flat, (0, 4 * SMALL_ROWS * 1024 - flat.size)).reshape(4, SMALL_ROWS, 1024)]
    la_s = exchange_sibling(small, 0, tag="_small")
    lb_s = exchange_chips(pair_sums(0, small, la_s, ["small"]), 0, tag="_small")
    received0, small_all = exchange_final(reduced0, 0, small=chip_sums(0, small, la_s, lb_s, ["small"])[0])

    g_local, delta, new_m, new_v = {}, {}, {}, {}
    for i, n in enumerate(names):
        shp = w[n].shape
        if n == "w_in":
            as3 = lambda a: a.transpose(0, 2, 1)
            back = lambda o: o.transpose(0, 2, 1)
        else:
            as3 = lambda a, rc=reduced0[i].shape: a.reshape((2,) + rc)
            back = lambda o, shp=shp: o.reshape(shp)
        outs = adamw_pair(f"adamw_{n}", as3(w[n]), (reduced0[i], reduced1[i]), (received0[i], received1[i]), as3(m[n]), as3(v[n]),
                          c_idx)
        g_local[n], delta[n], new_m[n], new_v[n] = [back(o) for o in outs]
    flat = small_all.reshape(-1)
    off = 0
    for n in REPLICATED:
        g_local[n] = flat[off:off + w[n].size].reshape(w[n].shape)
        off += w[n].size
        delta[n], new_m[n], new_v[n] = adamw(f"adamw_{n}", w[n], g_local[n], m[n], v[n])
    return (loss, grad_x[None], *[g_local[n] for n in WEIGHTS], *[delta[n] for n in WEIGHTS], *[new_m[n] for n in WEIGHTS],
            *[new_v[n] for n in WEIGHTS])
```

```python
import functools

import jax
import jax.numpy as jnp
from jax import lax
from jax.experimental import pallas as pl
from jax.experimental.pallas import tpu as pltpu

F32 = jnp.float32
BF16 = jnp.bfloat16
MXU_DTYPE = jnp.bfloat16

D_MODEL = 1024
EPS = 1e-6
ROPE_THETA = 10000.0
LRU_BLOCKS = 8
LRU_C = 8.0
MLA_HEADS = 8
MLA_NOPE = 64
MLA_ROPE = 32
MLA_QK = 96
DIL_GROUPS = ((128, 1), (512, 4), (2048, 16))
DIL_HD = 64
DIL_NK = 128
MLA_HPS = 8
IN_WIDTH = 11168
ADAM_LR, ADAM_B1, ADAM_B2, ADAM_EPS, ADAM_WD, ADAM_STEP = 0.001, 0.9, 0.999, 1e-08, 0.01, 10

LANES = 128
G_LRU, G_MLA, G_DIL, G_MRG = 0, 2048, 3072, 8192
IN_PAD = 11264
KR_OFF = 2432

NN = (((1,), (0,)), ((), ()))
NT = (((1,), (1,)), ((), ()))
TN = (((0,), (0,)), ((), ()))
NEG = -1e30
MESH = pl.DeviceIdType.MESH
VMEM_LIMIT = 48 * 2**20


def _cparams(sem):
    return pltpu.CompilerParams(dimension_semantics=sem, vmem_limit_bytes=VMEM_LIMIT)


def _dot(a, b, dims):
    return lax.dot_general(a.astype(MXU_DTYPE), b.astype(MXU_DTYPE), dims, preferred_element_type=F32)


@jax.custom_vjp
def mm(a, w):
    return _dot(a, w, NN)


def _mm_fwd(a, w):
    return _dot(a, w, NN), (a, w)


def _mm_bwd(res, g):
    a, w = res
    return _dot(g, w, NT), _dot(a, g, TN)


mm.defvjp(_mm_fwd, _mm_bwd)


def _seg64_matrix():
    r = lax.broadcasted_iota(jnp.int32, (LANES, LANES), 0) < DIL_HD
    c = lax.broadcasted_iota(jnp.int32, (LANES, LANES), 1) < DIL_HD
    return (r == c).astype(BF16)


def _seg_sum_impl(x):
    b = _seg64_matrix()
    hi = x.astype(BF16)
    r1 = x - hi.astype(F32)
    mid = r1.astype(BF16)
    lo = (r1 - mid.astype(F32)).astype(BF16)
    dot = lambda u: lax.dot_general(u, b, NN, preferred_element_type=F32)
    return dot(hi) + dot(mid) + dot(lo)


@jax.custom_vjp
def seg_sum(x):
    return _seg_sum_impl(x)


seg_sum.defvjp(lambda x: (_seg_sum_impl(x), None), lambda _, g: (_seg_sum_impl(g),))


def _lroll_impl(x, s):
    return pltpu.roll(x, s % LANES, 1)


@functools.partial(jax.custom_vjp, nondiff_argnums=(1,))
def lroll(x, s):
    return _lroll_impl(x, s)


lroll.defvjp(lambda x, s: (_lroll_impl(x, s), None), lambda s, _, g: (_lroll_impl(g, -s),))


class _Ops:
    def __init__(self, diff):
        self.mm = mm if diff else (lambda a, w: _dot(a, w, NN))
        self.seg_sum = seg_sum if diff else _seg_sum_impl
        self.lroll = lroll if diff else _lroll_impl


PLAIN, DIFF = _Ops(False), _Ops(True)


def rms(x, g, n):
    ms = jnp.sum(x * x, axis=-1, keepdims=True) * (1.0 / n)
    return x * lax.rsqrt(ms + EPS) * g


def rope(ops, x, c, s1, s2, half):
    return x * c + ops.lroll(x, -half) * s1 + ops.lroll(x, half) * s2


def _sigmoid(x):
    return 1.0 / (1.0 + jnp.exp(-x))


def _silu_and_grad(g):
    sg = _sigmoid(g)
    return g * sg, sg * (1.0 + g * (1.0 - sg))


def _softplus(x):
    return jnp.maximum(x, 0.0) + jnp.log(1.0 + jnp.exp(-jnp.abs(x)))


def _expm1(y):
    series = y * (1.0 + y * (0.5 + y * (1.0 / 6.0 + y * (1.0 / 24.0 + y * (1.0 / 120.0)))))
    return jnp.where(y > -0.05, series, jnp.exp(jnp.minimum(y, -0.05)) - 1.0)


def _host(body, n_in, n_out, n_scr, hook, grid):
    if hook is None:
        return body, [], [], [], [], []
    ins, out_shapes, n_sem, make = hook
    ni, no = len(ins), len(out_shapes)
    hbm = pl.BlockSpec(memory_space=pltpu.HBM)

    def at(corner):
        cond = None
        for axis, size in enumerate(grid):
            here = pl.program_id(axis) == (size - 1 if corner else 0)
            cond = here if cond is None else cond & here
        return cond

    def hosted(*refs):
        a, refs = refs[:n_in], refs[n_in:]
        xi, refs = refs[:ni], refs[ni:]
        o, refs = refs[:n_out], refs[n_out:]
        xo, refs = refs[:no], refs[no:]
        scr, (send, recv) = refs[:n_scr], refs[n_scr:]
        start, finish = make(xi, xo, send, recv)
        pl.when(at(False))(start)
        body(*a, *o, *scr)
        pl.when(at(True))(finish)

    sems = [pltpu.SemaphoreType.DMA((n_sem,)), pltpu.SemaphoreType.DMA((n_sem,))]
    return hosted, list(ins), [hbm] * ni, [hbm] * no, list(out_shapes), sems


def _mm_call(name, a, b, *, mode, m, n, k, a_blk, b_blk, out_dtype, add, tm, tn, tk, comm=None):
    nk = k // tk
    assert m % tm == 0 and n % tn == 0 and k % tk == 0, (name, m, n, k, tm, tn, tk)
    dims = {"nn": NN, "nt": NT, "tn": TN}[mode]

    def body(*refs):
        if add is None:
            a_ref, b_ref, o_ref, *scr = refs
            add_ref = None
        else:
            a_ref, b_ref, add_ref, o_ref, *scr = refs
        part = _dot(a_ref[...], b_ref[...], dims)

        def finish(acc):
            if add_ref is not None:
                acc = acc + add_ref[...]
            o_ref[...] = acc.astype(o_ref.dtype)

        if nk == 1:
            finish(part)
        else:
            (acc_ref,) = scr
            kk = pl.program_id(2)

            @pl.when(kk == 0)
            def _():
                acc_ref[...] = part

            @pl.when(kk > 0)
            def _():
                acc_ref[...] += part

            @pl.when(kk == nk - 1)
            def _():
                finish(acc_ref[...])

    in_specs = [a_blk, b_blk]
    args = [a, b]
    if add is not None:
        in_specs.append(pl.BlockSpec((tm, tn), lambda j, i, kk: (i, j)))
        args.append(add)
    grid = (n // tn, m // tm, nk)
    scratch = [] if nk == 1 else [pltpu.VMEM((tm, tn), F32)]
    body, x_in, x_in_specs, x_out_specs, x_out_shapes, sems = _host(body, len(args), 1, len(scratch), comm, grid)
    res = pl.pallas_call(
        body,
        name=name,
        grid=grid,
        in_specs=in_specs + x_in_specs,
        out_specs=[pl.BlockSpec((tm, tn), lambda j, i, kk: (i, j))] + x_out_specs,
        out_shape=[jax.ShapeDtypeStruct((m, n), out_dtype)] + x_out_shapes,
        scratch_shapes=scratch + sems,
        compiler_params=_cparams(("arbitrary",) * 3 if comm is not None else ("parallel", "parallel", "arbitrary")),
    )(*args, *x_in)
    return res[0] if comm is None else res


def mm_nn(name, a, b, *, n=None, n_off=0, out_dtype=F32, add=None, tm=512, tn=1024, tk=1024):
    m, k = a.shape
    n = b.shape[1] if n is None else n
    tm, tn, tk = min(tm, m), min(tn, n), min(k, tk)
    ob = n_off // tn
    assert n_off % tn == 0
    return _mm_call(name, a, b, mode="nn", m=m, n=n, k=k, out_dtype=out_dtype, add=add, tm=tm, tn=tn, tk=tk,
                    a_blk=pl.BlockSpec((tm, tk), lambda j, i, kk: (i, kk)),
                    b_blk=pl.BlockSpec((tk, tn), lambda j, i, kk: (kk, j + ob)))


def mm_nt(name, a, b, *, out_dtype=F32, tm=512, tn=1024, tk=1024, comm=None):
    m, k = a.shape
    n = b.shape[0]
    tn, tk = min(tn, n), min(k, tk)
    return _mm_call(name, a, b, mode="nt", m=m, n=n, k=k, out_dtype=out_dtype, add=None, tm=tm, tn=tn, tk=tk, comm=comm,
                    a_blk=pl.BlockSpec((tm, tk), lambda j, i, kk: (i, kk)),
                    b_blk=pl.BlockSpec((tn, tk), lambda j, i, kk: (j, kk)))


DW_TILES = dict(tm=1024, tn=512, tk=4096)


def in_proj_dw(ht, dz):
    t = ht.shape[1]
    tn = 512
    nchunk = IN_PAD // tn
    kr_chunk = KR_OFF // tn
    lo = KR_OFF - kr_chunk * tn

    def body(ht_ref, dz_ref, out_ref, buf, sems):
        j = pl.program_id(0)
        slot = j % 2
        buf[slot] = _dot(ht_ref[...], dz_ref[...], NN).T

        def copies(c, s):
            dst = jnp.where(c < kr_chunk, c * tn, (c - 1) * tn + lo + MLA_ROPE)
            plain = [pltpu.make_async_copy(buf.at[s], out_ref.at[pl.ds(pl.multiple_of(dst, 8), tn)], sems.at[s, 0])]
            split = [pltpu.make_async_copy(buf.at[s, 0:lo], out_ref.at[KR_OFF - lo:KR_OFF], sems.at[s, 0]),
                     pltpu.make_async_copy(buf.at[s, lo + 64:lo + 96], out_ref.at[KR_OFF:KR_OFF + MLA_ROPE], sems.at[s, 1])]
            return plain, split

        def each(c, s, action):
            plain, split = copies(c, s)

            @pl.when(c != kr_chunk)
            def _():
                for cp in plain:
                    action(cp)

            @pl.when(c == kr_chunk)
            def _():
                for cp in split:
                    action(cp)

        @pl.when(j > 0)
        def _():
            each(j - 1, 1 - slot, lambda cp: cp.wait())

        each(j, slot, lambda cp: cp.start())

        @pl.when(j == nchunk - 1)
        def _():
            each(j, slot, lambda cp: cp.wait())

    return pl.pallas_call(
        body,
        name="in_proj_dw",
        grid=(nchunk,),
        in_specs=[pl.BlockSpec((D_MODEL, t), lambda j: (0, 0)), pl.BlockSpec((t, tn), lambda j: (0, j))],
        out_specs=pl.BlockSpec(memory_space=pltpu.HBM),
        out_shape=jax.ShapeDtypeStruct((IN_WIDTH, D_MODEL), F32),
        scratch_shapes=[pltpu.VMEM((2, tn, D_MODEL), F32), pltpu.SemaphoreType.DMA((2, 2))],
        compiler_params=_cparams(("arbitrary",)),
    )(ht, dz)


def rowwise(name, fn, t, *, tm=256, ncol=1, ins=(), outs=(), touts=(), accs=()):
    n_in, n_out, n_acc = len(ins), len(outs) + len(touts), len(accs)

    def zero_when(ref, cond):
        @pl.when(cond)
        def _():
            ref[...] = jnp.zeros(ref.shape, ref.dtype)

    def body(*refs):
        in_refs, out_refs, acc_refs = refs[:n_in], refs[n_in:n_in + n_out], refs[n_in + n_out:]
        j, i = pl.program_id(0), pl.program_id(1)
        for ref, (_, _, _, cd) in zip(acc_refs, accs):
            zero_when(ref, (i == 0) if cd else ((i == 0) & (j == 0)))
        fn(in_refs, out_refs, acc_refs)

    def in_spec(arr, w, base, cd, rd):
        rows = tm if rd else arr.shape[0]
        return pl.BlockSpec((rows, w), lambda j, i: (i if rd else 0, base + (j if cd else 0)))

    in_specs = [in_spec(*e) for e in ins]
    out_specs = [pl.BlockSpec((tm, w), (lambda j, i, cd=cd: (i, j if cd else 0))) for (_, w, _, cd) in outs]
    out_specs += [pl.BlockSpec((w, tm), (lambda j, i, cd=cd: (j if cd else 0, i))) for (_, w, _, cd) in touts]
    out_specs += [pl.BlockSpec((r, w), (lambda j, i, cd=cd: (0, j if cd else 0))) for (r, _, w, cd) in accs]
    out_shape = [jax.ShapeDtypeStruct((t, c), dt) for (c, _, dt, _) in outs]
    out_shape += [jax.ShapeDtypeStruct((r, t), dt) for (r, _, dt, _) in touts]
    out_shape += [jax.ShapeDtypeStruct((r, c), F32) for (r, c, _, _) in accs]
    res = pl.pallas_call(
        body,
        name=name,
        grid=(ncol, t // tm),
        in_specs=in_specs,
        out_specs=out_specs,
        out_shape=out_shape,
        compiler_params=_cparams(("arbitrary", "arbitrary") if accs else ("parallel", "parallel")),
    )(*[e[0] for e in ins])
    return res


def _row(arr, w=None, base=0, cd=False):
    return (arr, arr.shape[1] if w is None else w, base, cd, True)


def _const(arr, w=None, base=0, cd=False):
    return (arr, arr.shape[1] if w is None else w, base, cd, False)


def rope_tables(positions):
    t = positions.shape[0]

    def fn(ins, outs, _):
        pos = ins[0][...].astype(F32)
        lane = lax.broadcasted_iota(jnp.int32, (1, LANES), 1)
        log_theta = jnp.log(jnp.float32(ROPE_THETA))
        jm = lane - MLA_NOPE
        idx = jnp.clip(jnp.where(jm < 16, jm, jm - 16), 0, 15).astype(F32)
        ang = pos * jnp.exp(-(idx * (2.0 / MLA_ROPE)) * log_theta)
        cos, sin = jnp.cos(ang), jnp.sin(ang)
        in_rope = (lane >= MLA_NOPE) & (lane < MLA_QK)
        outs[0][:, 0:128] = jnp.where(lane < MLA_NOPE, 1.0, jnp.where(in_rope, cos, 0.0))
        outs[0][:, 128:256] = jnp.where(in_rope & (jm < 16), -sin, 0.0)
        outs[0][:, 256:384] = jnp.where(in_rope & (jm >= 16), sin, 0.0)
        jd = lane & (DIL_HD - 1)
        idx = (jd & 31).astype(F32)
        ang = pos * jnp.exp(-(idx * (2.0 / DIL_HD)) * log_theta)
        cos, sin = jnp.cos(ang), jnp.sin(ang)
        outs[0][:, 384:512] = cos
        outs[0][:, 512:640] = jnp.where(jd < 32, -sin, 0.0)
        outs[0][:, 640:768] = jnp.where(jd >= 32, sin, 0.0)

    return rowwise("rope_tables", fn, t, ins=[_row(positions)], outs=[(768, 768, F32, False)])[0]


def rmsnorm_fwd(x, g):
    def fn(ins, outs, _):
        h = rms(ins[0][...], ins[1][...], D_MODEL)
        outs[0][...] = h.astype(MXU_DTYPE)
        outs[1][...] = h.T.astype(MXU_DTYPE)

    return rowwise("rmsnorm_fwd", fn, x.shape[0], ins=[_row(x), _const(g)], outs=[(D_MODEL, D_MODEL, MXU_DTYPE, False)],
                   touts=[(D_MODEL, D_MODEL, MXU_DTYPE, False)])


def rmsnorm_bwd(x, dh, dres, g):
    def fn(ins, outs, accs):
        _, vjp = jax.vjp(lambda xv, gv: rms(xv, gv, D_MODEL), ins[0][...], ins[3][...])
        dx, dg = vjp(ins[1][...])
        outs[0][...] = ins[2][...] + dx
        accs[0][...] += dg

    return rowwise("rmsnorm_bwd", fn, x.shape[0], tm=512, ins=[_row(x), _row(dh), _row(dres), _const(g)],
                   outs=[(D_MODEL, D_MODEL, F32, False)], accs=[(1, D_MODEL, D_MODEL, False)])


def loss_head(y, target):
    def fn(ins, outs, accs):
        err = ins[0][...] - ins[1][...]
        outs[0][...] = err * (1.0 / D_MODEL)
        accs[0][...] += jnp.sum(err * err, axis=0, keepdims=True)
        accs[1][...] = jnp.broadcast_to(jnp.sum(accs[0][...], keepdims=True), (1, LANES))

    dy, _, tot = rowwise("loss_head", fn, y.shape[0], tm=512, ins=[_row(y), _row(target)], outs=[(D_MODEL, D_MODEL, F32, False)],
                         accs=[(1, D_MODEL, D_MODEL, False), (1, LANES, LANES, False)])
    return dy, tot[0, 0] * (0.5 / D_MODEL)


def _shift_rows(v, d, fill, reverse):
    tb = v.shape[0]
    if d % 8 == 0:
        pad = jnp.full((d, v.shape[1]), fill, v.dtype)
        return jnp.concatenate([v[d:], pad] if reverse else [pad, v[:tb - d]], axis=0)
    rows = lax.broadcasted_iota(jnp.int32, v.shape, 0)
    if not reverse:
        return jnp.where(rows >= d, pltpu.roll(v, d, 0), fill)
    return jnp.where(rows < tb - d, pltpu.roll(v, tb - d, 0), fill)


def _scan_tile(a, b, reverse):
    d = 1
    while d < a.shape[0]:
        b = b + a * _shift_rows(b, d, 0.0, reverse)
        a = a * _shift_rows(a, d, 1.0, reverse)
        d *= 2
    return a, b


def _lru_gates(ops, xc, wgx, bgx, wga, bga, lam):
    gx = _sigmoid(ops.mm(xc, wgx) + bgx)
    ga = _sigmoid(ops.mm(xc, wga) + bga)
    log_a = -LRU_C * ga * _softplus(-lam)
    a = jnp.exp(log_a)
    mult = jnp.sqrt(-_expm1(2.0 * log_a))
    return a, mult * (gx * xc)


def _shifted_inputs(x, halo, tb):
    rows = lax.broadcasted_iota(jnp.int32, x.shape, 0)
    pad = jnp.zeros((tb - 8, LANES), F32)
    out = []
    for d in (3, 2, 1):
        head = jnp.concatenate([pltpu.roll(halo, d, 0), pad], axis=0)
        out.append(jnp.where(rows >= d, pltpu.roll(x, d, 0), head))
    return out + [x]


def _lru_specs(nt, tb, reverse):
    hb = tb // 8
    tt = (lambda t: nt - 1 - t) if reverse else (lambda t: t)
    blk = lambda off: pl.BlockSpec((tb, LANES), lambda n, t: (tt(t), n + off))
    halo = lambda off: pl.BlockSpec((8, LANES), lambda n, t: (jnp.maximum(tt(t) * hb - 1, 0), n + off))
    chan = lambda r: pl.BlockSpec((r, LANES), lambda n, t: (0, n))
    wblk = pl.BlockSpec((None, LANES, LANES), lambda n, t: (n, 0, 0))
    bblk = pl.BlockSpec((None, 1, LANES), lambda n, t: (n, 0, 0))
    return blk, halo, chan, wblk, bblk


def lru_fwd(z, conv_w, conv_b, wgx, bgx, wga, bga, lam, *, tb=256, comm=None):
    t = z.shape[0]
    nt = t // tb
    blk, halo, chan, wblk, bblk = _lru_specs(nt, tb, False)

    def body(x_ref, xh_ref, g_ref, cw_ref, cb_ref, wgx_ref, bgx_ref, wga_ref, bga_ref, lam_ref, h_ref, y_ref, yt_ref, carry_ref):
        ti = pl.program_id(1)

        @pl.when(ti == 0)
        def _():
            carry_ref[...] = jnp.zeros((8, LANES), F32)

        x = x_ref[...]
        hal = jnp.where(ti > 0, xh_ref[...], 0.0)
        xs = _shifted_inputs(x, hal, tb)
        xc = cb_ref[...] + sum(xs[k] * cw_ref[k:k + 1, :] for k in range(4))
        a, b = _lru_gates(PLAIN, xc, wgx_ref[...], bgx_ref[...], wga_ref[...], bga_ref[...], lam_ref[...])
        acum, h0 = _scan_tile(a, b, False)
        h = h0 + acum * carry_ref[7:8, :]
        carry_ref[...] = h[tb - 8:tb, :]
        h_ref[...] = h
        y = h * _silu_and_grad(g_ref[...])[0]
        y_ref[...] = y.astype(MXU_DTYPE)
        yt_ref[...] = y.T.astype(MXU_DTYPE)

    grid = (LRU_BLOCKS, nt)
    body, x_in, x_in_specs, x_out_specs, x_out_shapes, sems = _host(body, 10, 3, 1, comm, grid)
    return pl.pallas_call(
        body,
        name="lru_fwd" if comm is None else "lru_fwd_comm",
        grid=grid,
        in_specs=[blk(0), halo(0), blk(LRU_BLOCKS), chan(4), chan(1), wblk, bblk, wblk, bblk, chan(1)] + x_in_specs,
        out_specs=[blk(0), blk(0), pl.BlockSpec((LANES, tb), lambda n, t_: (n, t_))] + x_out_specs,
        out_shape=[jax.ShapeDtypeStruct((t, 1024), F32), jax.ShapeDtypeStruct((t, 1024), MXU_DTYPE),
                   jax.ShapeDtypeStruct((1024, t), MXU_DTYPE)] + x_out_shapes,
        scratch_shapes=[pltpu.VMEM((8, LANES), F32)] + sems,
        compiler_params=_cparams(("arbitrary", "arbitrary") if comm is not None else ("parallel", "arbitrary")),
    )(z, z, z, conv_w, conv_b, wgx, bgx, wga, bga, lam, *x_in)


def lru_bwd(z, h, dy, conv_w, conv_b, wgx, bgx, wga, bga, lam, *, tb=256, comm=None):
    t = z.shape[0]
    nt = t // tb
    blk, halo, chan, wblk, bblk = _lru_specs(nt, tb, True)

    def body(x_ref, xh_ref, g_ref, h_ref, hh_ref, dy_ref, cw_ref, cb_ref, wgx_ref, bgx_ref, wga_ref, bga_ref, lam_ref,
             dzx_ref, dzg_ref, dwgx_ref, dbgx_ref, dwga_ref, dbga_ref, dlam_ref, dcw_ref, dcb_ref,
             gcar_ref, acar_ref, xcar_ref):
        ti = pl.program_id(1)
        has_earlier = ti < nt - 1

        @pl.when(ti == 0)
        def _():
            for ref in (dwgx_ref, dbgx_ref, dwga_ref, dbga_ref, dlam_ref, dcw_ref, dcb_ref, gcar_ref, acar_ref, xcar_ref):
                ref[...] = jnp.zeros(ref.shape, F32)

        rows = lax.broadcasted_iota(jnp.int32, (tb, LANES), 0)
        x = x_ref[...]
        hal = jnp.where(has_earlier, xh_ref[...], 0.0)
        xs = _shifted_inputs(x, hal, tb)
        xc = cb_ref[...] + sum(xs[k] * cw_ref[k:k + 1, :] for k in range(4))
        (a, _), vjp = jax.vjp(functools.partial(_lru_gates, DIFF), xc, wgx_ref[...], bgx_ref[...], wga_ref[...],
                              bga_ref[...], lam_ref[...])
        g, h, dyv = g_ref[...], h_ref[...], dy_ref[...]
        silu, dsilu = _silu_and_grad(g)
        dzg_ref[...] = (dyv * h * dsilu).astype(MXU_DTYPE)
        a_next = jnp.where(rows < tb - 1, pltpu.roll(a, tb - 1, 0), acar_ref[0:1, :])
        acum, g0 = _scan_tile(a_next, dyv * silu, True)
        gt = g0 + acum * gcar_ref[0:1, :]
        h_prev = jnp.where(rows >= 1, pltpu.roll(h, 1, 0), jnp.where(has_earlier, hh_ref[7:8, :], 0.0))
        dxc, dwgx, dbgx, dwga, dbga, dlam = vjp((gt * h_prev, gt))
        later = xcar_ref[...]
        gcar_ref[...] = gt[0:8, :]
        acar_ref[...] = a[0:8, :]
        xcar_ref[...] = dxc[0:8, :]
        dx = dxc * cw_ref[3:4, :]
        pad = jnp.zeros((tb - 8, LANES), F32)
        for d in (1, 2, 3):
            tail = jnp.concatenate([pad, pltpu.roll(later, 8 - d, 0)], axis=0)
            up = jnp.where(rows < tb - d, pltpu.roll(dxc, tb - d, 0), tail)
            dx = dx + up * cw_ref[3 - d:4 - d, :]
        dzx_ref[...] = dx.astype(MXU_DTYPE)
        for k in range(4):
            dcw_ref[k:k + 1, :] += jnp.sum(dxc * xs[k], axis=0, keepdims=True)
        dcb_ref[...] += jnp.sum(dxc, axis=0, keepdims=True)
        dwgx_ref[...] += dwgx
        dbgx_ref[...] += dbgx
        dwga_ref[...] += dwga
        dbga_ref[...] += dbga
        dlam_ref[...] += dlam

    sds = jax.ShapeDtypeStruct
    grid = (LRU_BLOCKS, nt)
    body, x_in, x_in_specs, x_out_specs, x_out_shapes, sems = _host(body, 13, 9, 3, comm, grid)
    return pl.pallas_call(
        body,
        name="lru_bwd" if comm is None else "lru_bwd_comm",
        grid=grid,
        in_specs=[blk(0), halo(0), blk(LRU_BLOCKS), blk(0), halo(0), blk(0), chan(4), chan(1), wblk, bblk, wblk, bblk, chan(1)]
        + x_in_specs,
        out_specs=[blk(0), blk(0), wblk, bblk, wblk, bblk, chan(1), chan(4), chan(1)] + x_out_specs,
        out_shape=[sds((t, 1024), MXU_DTYPE), sds((t, 1024), MXU_DTYPE), sds(wgx.shape, F32), sds(bgx.shape, F32), sds(wga.shape, F32),
                   sds(bga.shape, F32), sds((1, 1024), F32), sds((4, 1024), F32), sds((1, 1024), F32)] + x_out_shapes,
        scratch_shapes=[pltpu.VMEM((8, LANES), F32)] * 3 + sems,
        compiler_params=_cparams(("arbitrary", "arbitrary") if comm is not None else ("parallel", "arbitrary")),
    )(z, z, z, h, h, dy, conv_w, conv_b, wgx, bgx, wga, bga, lam, *x_in)


def _mla_prep_tile(ops, cq, ckv, krp, c, s1, s2, g_cq, g_ckv, wq, wk, wv, gq, gk):
    cqn = rms(cq, g_cq, 256)
    ckvn = rms(ckv, g_ckv, 128)
    v = ops.mm(ckvn, wv)
    qs, ks = [], []
    for hd in range(MLA_HEADS):
        q = rms(ops.mm(cqn, wq[hd]), gq, MLA_QK)
        k = rms(ops.mm(ckvn, wk[hd]) + krp, gk, MLA_QK)
        qs.append(rope(ops, q, c, s1, s2, 16) * (MLA_QK ** -0.5))
        ks.append(rope(ops, k, c, s1, s2, 16))
    return tuple(qs), tuple(ks), v


def _mla_prep_args(ins):
    z_cq, z_ckv, z_kr, tc, ts1, ts2, g_cq, g_ckv, wq, wk, wv, gq, gk = ins[:13]
    heads = lambda w: tuple(w[:, LANES * hd:LANES * (hd + 1)] for hd in range(MLA_HEADS))
    return (z_cq[...], z_ckv[...], z_kr[...], tc[...], ts1[...], ts2[...], g_cq[...], g_ckv[...], heads(wq), heads(wk),
            wv[...], gq[...], gk[...])


def _mla_prep_ins(z, tabs, w):
    return [_row(z, 256, 0), _row(z, 128, 2), _row(z, 128, 3), _row(tabs, 128, 0), _row(tabs, 128, 1), _row(tabs, 128, 2),
            _const(w["cq_norm_g"]), _const(w["ckv_norm_g"]), _const(w["wq"]), _const(w["wk"]), _const(w["wv"]),
            _const(w["gq"]), _const(w["gk"])]


def mla_prep_fwd(z, tabs, w):
    def fn(ins, outs, _):
        qs, ks, v = _mla_prep_tile(PLAIN, *_mla_prep_args(ins))
        for hd in range(MLA_HEADS):
            outs[0][:, LANES * hd:LANES * (hd + 1)] = qs[hd].astype(MXU_DTYPE)
            outs[1][:, LANES * hd:LANES * (hd + 1)] = ks[hd].astype(MXU_DTYPE)
        outs[2][...] = v.astype(MXU_DTYPE)
        outs[3][...] = v.T.astype(MXU_DTYPE)

    return rowwise("mla_prep_fwd", fn, z.shape[0], ins=_mla_prep_ins(z, tabs, w),
                   outs=[(1024, 1024, MXU_DTYPE, False), (1024, 1024, MXU_DTYPE, False), (512, 512, MXU_DTYPE, False)],
                   touts=[(512, 512, MXU_DTYPE, False)])


def mla_prep_bwd(z, tabs, w, dq, dk, dv):
    def fn(ins, outs, accs):
        args = _mla_prep_args(ins)
        _, vjp = jax.vjp(functools.partial(_mla_prep_tile, DIFF), *args)
        heads = lambda ref: tuple(ref[:, LANES * hd:LANES * (hd + 1)] for hd in range(MLA_HEADS))
        dcq, dckv, dkr, _, _, _, dg_cq, dg_ckv, dwq, dwk, dwv, dgq, dgk = vjp((heads(ins[13]), heads(ins[14]), ins[15][...]))
        lane = lax.broadcasted_iota(jnp.int32, (1, LANES), 1)
        outs[0][:, 0:256] = dcq.astype(MXU_DTYPE)
        outs[0][:, 256:384] = dckv.astype(MXU_DTYPE)
        outs[0][:, 384:512] = jnp.where((lane >= MLA_NOPE) & (lane < MLA_QK), dkr, 0.0).astype(MXU_DTYPE)
        accs[0][...] += dg_cq
        accs[1][...] += dg_ckv
        for hd in range(MLA_HEADS):
            accs[2][:, LANES * hd:LANES * (hd + 1)] += dwq[hd]
            accs[3][:, LANES * hd:LANES * (hd + 1)] += dwk[hd]
        accs[4][...] += dwv
        accs[5][...] += dgq
        accs[6][...] += dgk

    return rowwise("mla_prep_bwd", fn, z.shape[0], ins=_mla_prep_ins(z, tabs, w) + [_row(dq), _row(dk), _row(dv)],
                   outs=[(512, 512, MXU_DTYPE, False)],
                   accs=[(1, 256, 256, False), (1, 128, 128, False), (256, 1024, 1024, False), (128, 1024, 1024, False),
                         (128, 512, 512, False), (1, 128, 128, False), (1, 128, 128, False)])


def _head_masks():
    lane = lax.broadcasted_iota(jnp.int32, (1, LANES), 1)
    return (lane < DIL_HD, lane >= DIL_HD)


def _row_scalar(tile, mask):
    return jnp.max(jnp.where(mask, tile, -jnp.inf), axis=-1, keepdims=True)


def _causal_tiles(nq, by_key):
    pairs = [(i, j) for i in range(nq) for j in range(i + 1)]
    if by_key:
        pairs.sort(key=lambda ij: (ij[1], ij[0]))
    return (jnp.asarray([ij[0] for ij in pairs], jnp.int32), jnp.asarray([ij[1] for ij in pairs], jnp.int32))


def mla_attn_fwd(q, k, vt, z, *, tq=256, gather=None):
    t = q.shape[0]
    nq = t // tq
    it, jt = _causal_tiles(nq, False)
    hps, wq, wv = MLA_HPS, LANES * MLA_HPS, 64 * MLA_HPS
    ws, layer = gather if gather is not None else ([], 0)
    ng = len(ws)
    ngrp, nsteps = MLA_HEADS // hps, int(it.shape[0])

    def body(it_ref, jt_ref, q_ref, k_ref, vt_ref, g_ref, *rest):
        w_refs, rest = rest[:ng], rest[ng:]
        o_ref, y_ref, yt_ref, lse_ref = rest[:4]
        ga_refs, rest = rest[4:4 + ng], rest[4 + ng:]
        m_scr, l_scr, acc_scr = rest[:3]
        step = pl.program_id(1)
        i, j = it_ref[step], jt_ref[step]
        if ng:
            gather_start, gather_finish = _layer_gather(w_refs, ga_refs, rest[3], rest[4], layer)

            @pl.when((pl.program_id(0) == 0) & (step == 0))
            def _():
                gather_start()

        @pl.when(j == 0)
        def _():
            m_scr[...] = jnp.full(m_scr.shape, NEG, F32)
            l_scr[...] = jnp.zeros(l_scr.shape, F32)
            acc_scr[...] = jnp.zeros(acc_scr.shape, F32)

        def update(diagonal):
            heads = range(hps)
            lanes = [slice(LANES * hh, LANES * (hh + 1)) for hh in heads]
            rows = [slice(64 * hh, 64 * (hh + 1)) for hh in heads]
            sts = [_dot(k_ref[:, lanes[hh]], q_ref[:, lanes[hh]], NT) for hh in heads]
            m_prev = [m_scr[hh:hh + 1, :] for hh in heads]
            l_prev = [l_scr[hh:hh + 1, :] for hh in heads]
            acc_prev = [acc_scr[rows[hh], :] for hh in heads]
            m_new, l_new, acc_new = [], [], []
            for hh in heads:
                st = sts[hh]
                if diagonal:
                    key = lax.broadcasted_iota(jnp.int32, (tq, tq), 0)
                    qry = lax.broadcasted_iota(jnp.int32, (tq, tq), 1)
                    st = jnp.where(key <= qry, st, NEG)
                m_new.append(jnp.maximum(m_prev[hh], jnp.max(st, axis=0, keepdims=True)))
                alpha = jnp.exp(m_prev[hh] - m_new[hh])
                pt = jnp.exp(st - m_new[hh])
                l_new.append(alpha * l_prev[hh] + jnp.sum(pt, axis=0, keepdims=True))
                acc_new.append(alpha * acc_prev[hh] + _dot(vt_ref[rows[hh], :], pt, NN))
            for hh in heads:
                m_scr[hh:hh + 1, :] = m_new[hh]
                l_scr[hh:hh + 1, :] = l_new[hh]
                acc_scr[rows[hh], :] = acc_new[hh]

        @pl.when(j < i)
        def _():
            update(False)

        @pl.when(j == i)
        def _():
            update(True)
            lse_ref[...] = jnp.zeros(lse_ref.shape, F32)
            for hh in range(hps):
                rows = slice(64 * hh, 64 * (hh + 1))
                acc_scr[rows, :] = acc_scr[rows, :] / l_scr[hh:hh + 1, :]
                lse_ref[hh:hh + 1, :] = m_scr[hh:hh + 1, :] + jnp.log(l_scr[hh:hh + 1, :])
            o = acc_scr[...].T
            o_ref[...] = o
            y = o * _silu_and_grad(g_ref[...])[0]
            y_ref[...] = y.astype(MXU_DTYPE)
            yt_ref[...] = y.T.astype(MXU_DTYPE)

        if ng:
            @pl.when((pl.program_id(0) == ngrp - 1) & (step == nsteps - 1))
            def _():
                gather_finish()

    qo = lambda w, off=0: pl.BlockSpec((tq, w), lambda p, s, it_, jt_: (it_[s], p + off))
    sds = jax.ShapeDtypeStruct
    comm_scratch = [pltpu.SemaphoreType.DMA((6 * ng,)), pltpu.SemaphoreType.DMA((6 * ng,))] if ng else []
    return pl.pallas_call(
        body,
        name="mla_attn_fwd_gather" if ng else "mla_attn_fwd",
        grid_spec=pltpu.PrefetchScalarGridSpec(
            num_scalar_prefetch=2,
            grid=(ngrp, nsteps),
            in_specs=[qo(wq), pl.BlockSpec((tq, wq), lambda p, s, it_, jt_: (jt_[s], p)),
                      pl.BlockSpec((wv, tq), lambda p, s, it_, jt_: (p, jt_[s])), qo(wv, 512 // wv)] + [HBM_SPEC] * ng,
            out_specs=[qo(wv), qo(wv), pl.BlockSpec((wv, tq), lambda p, s, it_, jt_: (p, it_[s])),
                       pl.BlockSpec((None, 8, tq), lambda p, s, it_, jt_: (p, 0, it_[s]))] + [HBM_SPEC] * ng,
            scratch_shapes=[pltpu.VMEM((8, tq), F32), pltpu.VMEM((8, tq), F32), pltpu.VMEM((wv, tq), F32)] + comm_scratch,
        ),
        out_shape=[sds((t, 512), F32), sds((t, 512), MXU_DTYPE), sds((512, t), MXU_DTYPE), sds((ngrp, 8, t), F32)]
        + _gather_shapes(ws),
        compiler_params=_cparams(("arbitrary", "arbitrary") if ng else ("parallel", "arbitrary")),
    )(it, jt, q, k, vt, z, *ws)


def mla_attn_bwd(q, k, v, do, lse, dd, *, tq=256, chips=None):
    t = q.shape[0]
    nq = t // tq
    it, jt = _causal_tiles(nq, True)
    hps, wq, wv = MLA_HPS, LANES * MLA_HPS, 64 * MLA_HPS
    pcs, layer = chips if chips is not None else ([], 0)
    nc = len(pcs)
    ngrp, nsteps = MLA_HEADS // hps, int(it.shape[0])

    def body(it_ref, jt_ref, q_ref, k_ref, v_ref, do_ref, lse_ref, d_ref, *rest):
        pc_refs, rest = rest[:nc], rest[nc:]
        dq_ref, dk_ref, dv_ref = rest[:3]
        lb_refs, rest = rest[3:3 + nc], rest[3 + nc:]
        dk_scr, dv_scr = rest[:2]
        step = pl.program_id(1)
        i, j = it_ref[step], jt_ref[step]
        masks = _head_masks()
        if nc:
            chips_start, chips_finish = _chips_exchange(pc_refs, lb_refs, rest[2], rest[3], layer)

            @pl.when((pl.program_id(0) == 0) & (step == 0))
            def _():
                chips_start()

        @pl.when(step == 0)
        def _():
            dq_ref[...] = jnp.zeros(dq_ref.shape, F32)

        @pl.when(i == j)
        def _():
            dk_scr[...] = jnp.zeros(dk_scr.shape, F32)
            dv_scr[...] = jnp.zeros(dv_scr.shape, F32)

        def update(diagonal):
            qrows = pl.ds(pl.multiple_of(i * tq, tq), tq)
            heads = range(hps)
            lanes = [slice(LANES * hh, LANES * (hh + 1)) for hh in heads]
            pair = [slice(LANES * (hh // 2), LANES * (hh // 2 + 1)) for hh in heads]
            qh = [q_ref[:, lanes[hh]] for hh in heads]
            kh = [k_ref[:, lanes[hh]] for hh in heads]
            doh = []
            for hh in heads:
                dov = do_ref[:, pair[hh]]
                doh.append(jnp.where(masks[hh % 2], dov, jnp.zeros_like(dov)))
            sts = [_dot(kh[hh], qh[hh], NT) for hh in heads]
            dps = [_dot(v_ref[:, pair[hh]], doh[hh], NT) for hh in heads]
            lse = [lse_ref[hh:hh + 1, :] for hh in heads]
            ddv = [d_ref[hh:hh + 1, :] for hh in heads]
            dk_new = [dk_scr[:, lanes[hh]] for hh in heads]
            dq_new = [dq_ref[qrows, lanes[hh]] for hh in heads]
            dv_new = [dv_scr[:, pair[2 * pp]] for pp in range(hps // 2)]
            for hh in heads:
                st = sts[hh] - lse[hh]
                if diagonal:
                    key = lax.broadcasted_iota(jnp.int32, (tq, tq), 0)
                    qry = lax.broadcasted_iota(jnp.int32, (tq, tq), 1)
                    st = jnp.where(key <= qry, st, NEG)
                pt = jnp.exp(st)
                dst = pt * (dps[hh] - ddv[hh])
                dv_new[hh // 2] = dv_new[hh // 2] + _dot(pt, doh[hh], NN)
                dk_new[hh] = dk_new[hh] + _dot(dst, qh[hh], NN)
                dq_new[hh] = dq_new[hh] + _dot(dst, kh[hh], TN)
            for hh in heads:
                dk_scr[:, lanes[hh]] = dk_new[hh]
                dq_ref[qrows, lanes[hh]] = dq_new[hh]
            for pp in range(hps // 2):
                dv_scr[:, pair[2 * pp]] = dv_new[pp]

        @pl.when(j < i)
        def _():
            update(False)

        @pl.when(j == i)
        def _():
            update(True)

        @pl.when(i == nq - 1)
        def _():
            dk_ref[...] = dk_scr[...]
            dv_ref[...] = dv_scr[...]

        if nc:
            @pl.when((pl.program_id(0) == ngrp - 1) & (step == nsteps - 1))
            def _():
                chips_finish()

    qo = lambda w: pl.BlockSpec((tq, w), lambda p, s, it_, jt_: (it_[s], p))
    kv = lambda w: pl.BlockSpec((tq, w), lambda p, s, it_, jt_: (jt_[s], p))
    stat = pl.BlockSpec((None, 8, tq), lambda p, s, it_, jt_: (p, 0, it_[s]))
    sds = jax.ShapeDtypeStruct
    comm_scratch = [pltpu.SemaphoreType.DMA((3 * nc,)), pltpu.SemaphoreType.DMA((3 * nc,))] if nc else []
    return pl.pallas_call(
        body,
        name="mla_attn_bwd_chips" if nc else "mla_attn_bwd",
        grid_spec=pltpu.PrefetchScalarGridSpec(
            num_scalar_prefetch=2,
            grid=(ngrp, nsteps),
            in_specs=[qo(wq), kv(wq), kv(wv), qo(wv), stat, stat] + [HBM_SPEC] * nc,
            out_specs=[pl.BlockSpec((t, wq), lambda p, s, it_, jt_: (0, p)), kv(wq), kv(wv)] + [HBM_SPEC] * nc,
            scratch_shapes=[pltpu.VMEM((tq, wq), F32), pltpu.VMEM((tq, wv), F32)] + comm_scratch,
        ),
        out_shape=[sds((t, 1024), F32), sds((t, 1024), F32), sds((t, 512), F32)] + _chips_shapes(pcs),
        compiler_params=_cparams(("arbitrary", "arbitrary") if nc else ("parallel", "arbitrary")),
    )(it, jt, q, k, v, do, lse, dd, *pcs)


def mla_gate_bwd(dy, o, z, *, tm=256):
    t = dy.shape[0]
    wv = 64 * MLA_HPS

    def body(dy_ref, o_ref, g_ref, do_ref, dzg_ref, dd_ref):
        dyv, ov = dy_ref[...], o_ref[...]
        silu, dsilu = _silu_and_grad(g_ref[...])
        do = dyv * silu
        do_ref[...] = do.astype(MXU_DTYPE)
        dzg_ref[...] = (dyv * ov * dsilu).astype(MXU_DTYPE)
        prod = do * ov
        row = lax.broadcasted_iota(jnp.int32, (8, wv), 0)
        lane = lax.broadcasted_iota(jnp.int32, (8, wv), 1)
        pick = ((lane >= row * DIL_HD) & (lane < (row + 1) * DIL_HD)).astype(BF16)
        hi = prod.astype(BF16)
        r1 = prod - hi.astype(F32)
        mid = r1.astype(BF16)
        lo = (r1 - mid.astype(F32)).astype(BF16)
        dot = lambda u: lax.dot_general(pick, u, NT, preferred_element_type=F32)
        dd_ref[...] = dot(hi) + dot(mid) + dot(lo)

    blk = lambda off=0: pl.BlockSpec((tm, wv), lambda p, i: (i, p + off))
    sds = jax.ShapeDtypeStruct
    return pl.pallas_call(
        body,
        name="mla_gate_bwd",
        grid=(MLA_HEADS // MLA_HPS, t // tm),
        in_specs=[blk(), blk(), blk(512 // wv)],
        out_specs=[blk(), blk(), pl.BlockSpec((None, 8, tm), lambda p, i: (p, 0, i))],
        out_shape=[sds((t, 512), MXU_DTYPE), sds((t, 512), MXU_DTYPE), sds((MLA_HEADS // MLA_HPS, 8, t), F32)],
        compiler_params=_cparams(("parallel", "parallel")),
    )(dy, o, z)


SPAN = 2048
DIL_SCALE = DIL_HD ** -0.5


def _rm_src(u, window, dil):
    nn, r = divmod(u, dil)
    return pl.ds(nn * window + r, DIL_NK, stride=dil) if dil > 1 else pl.ds(u * DIL_NK, DIL_NK)


def _rm_dst(u):
    return pl.ds(u * DIL_NK, DIL_NK)


def _dil_prep_tile(ops, x, g2, c, s1, s2, scale):
    ms = ops.seg_sum(x * x) * (1.0 / DIL_HD)
    return rope(ops, x * lax.rsqrt(ms + EPS) * g2, c, s1, s2, 32) * scale


def _span_blk(base):
    return pl.BlockSpec((SPAN, LANES), lambda s, p: (s, base + p))


def _const_blk(shape):
    return pl.BlockSpec(shape, lambda s, p: (0,) * len(shape))


_DIL_TABLE_SPECS = [pl.BlockSpec((SPAN, LANES), (lambda s, p, blk=blk: (s, blk))) for blk in (3, 4, 5)]


def dil_prep_fwd(z, tabs, gq2, gk2, gi):
    window, dil = DIL_GROUPS[gi]
    t = z.shape[0]

    def body(q_ref, k_ref, v_ref, c_ref, s1_ref, s2_ref, gq_ref, gk_ref, qo_ref, ko_ref, vo_ref):
        for u in range(SPAN // DIL_NK):
            src, dst = _rm_src(u, window, dil), _rm_dst(u)
            c, s1, s2 = c_ref[src, :], s1_ref[src, :], s2_ref[src, :]
            qo_ref[dst, :] = _dil_prep_tile(PLAIN, q_ref[src, :], gq_ref[...], c, s1, s2, DIL_SCALE).astype(MXU_DTYPE)
            ko_ref[dst, :] = _dil_prep_tile(PLAIN, k_ref[src, :], gk_ref[...], c, s1, s2, 1.0).astype(MXU_DTYPE)
            vo_ref[dst, :] = v_ref[src, :].astype(MXU_DTYPE)

    return pl.pallas_call(
        body,
        name=f"dil_prep_fwd_{dil}",
        grid=(t // SPAN, 4),
        in_specs=[_span_blk(4 * gi), _span_blk(12 + 4 * gi), _span_blk(24 + 4 * gi)] + _DIL_TABLE_SPECS
        + [_const_blk((1, LANES)), _const_blk((1, LANES))],
        out_specs=[_span_blk(0)] * 3,
        out_shape=[jax.ShapeDtypeStruct((t, 512), MXU_DTYPE)] * 3,
        compiler_params=_cparams(("parallel", "parallel")),
    )(z, z, z, tabs, tabs, tabs, gq2, gk2)


def dil_prep_bwd(z, tabs, gq2, gk2, dq, dk, dv, gi):
    window, dil = DIL_GROUPS[gi]
    t = z.shape[0]

    def body(q_ref, k_ref, c_ref, s1_ref, s2_ref, gq_ref, gk_ref, dq_ref, dk_ref, dv_ref, dzq_ref, dzk_ref, dzv_ref, dgq_ref,
             dgk_ref, sq, sk, sv):
        @pl.when((pl.program_id(0) == 0) & (pl.program_id(1) == 0))
        def _():
            dgq_ref[...] = jnp.zeros((1, LANES), F32)
            dgk_ref[...] = jnp.zeros((1, LANES), F32)

        dgs = [jnp.zeros((1, LANES), F32), jnp.zeros((1, LANES), F32)]
        for u in range(SPAN // DIL_NK):
            src, dst = _rm_src(u, window, dil), _rm_dst(u)
            c, s1, s2 = c_ref[src, :], s1_ref[src, :], s2_ref[src, :]
            for idx, (x_ref, g_ref, ct_ref, scr, scale) in enumerate(((q_ref, gq_ref, dq_ref, sq, DIL_SCALE),
                                                                      (k_ref, gk_ref, dk_ref, sk, 1.0))):
                _, vjp = jax.vjp(lambda xv, gv, sc=scale: _dil_prep_tile(DIFF, xv, gv, c, s1, s2, sc), x_ref[src, :], g_ref[...])
                dx, dg = vjp(ct_ref[dst, :])
                scr[src, :] = dx
                dgs[idx] = dgs[idx] + dg
            sv[src, :] = dv_ref[dst, :]
        dgq_ref[...] += dgs[0] + pltpu.roll(dgs[0], DIL_HD, 1)
        dgk_ref[...] += dgs[1] + pltpu.roll(dgs[1], DIL_HD, 1)
        for c0 in range(0, SPAN, 256):
            rows = slice(c0, c0 + 256)
            dzq_ref[rows, :] = sq[rows, :].astype(MXU_DTYPE)
            dzk_ref[rows, :] = sk[rows, :].astype(MXU_DTYPE)
            dzv_ref[rows, :] = sv[rows, :].astype(MXU_DTYPE)

    sds = jax.ShapeDtypeStruct
    return pl.pallas_call(
        body,
        name=f"dil_prep_bwd_{dil}",
        grid=(t // SPAN, 4),
        in_specs=[_span_blk(4 * gi), _span_blk(12 + 4 * gi)] + _DIL_TABLE_SPECS
        + [_const_blk((1, LANES)), _const_blk((1, LANES)), _span_blk(0), _span_blk(0), _span_blk(0)],
        out_specs=[_span_blk(0)] * 3 + [_const_blk((1, LANES))] * 2,
        out_shape=[sds((t, 512), MXU_DTYPE)] * 3 + [sds((1, LANES), F32)] * 2,
        scratch_shapes=[pltpu.VMEM((SPAN, LANES), F32)] * 3,
        compiler_params=_cparams(("arbitrary", "arbitrary")),
    )(z, z, tabs, tabs, tabs, gq2, gk2, dq, dk, dv)


def _band_masks():
    qi = lax.broadcasted_iota(jnp.int32, (DIL_NK, DIL_NK), 0)
    ki = lax.broadcasted_iota(jnp.int32, (DIL_NK, DIL_NK), 1)
    return (ki >= qi), (ki <= qi)


def _pair_heads():
    return [(pair, hh) for pair in range(4) for hh in range(2)]


def _pair_lanes(pair):
    return slice(LANES * pair, LANES * (pair + 1))


def _zero_other_head(mask, x):
    return jnp.where(mask, x, jnp.zeros_like(x))


def dil_attn_fwd(name, q, k, v, dil):
    t = q.shape[0]

    def body(q_ref, kp_ref, kc_ref, vp_ref, vc_ref, o_ref, lse_ref):
        b = pl.program_id(0)
        mprev, mcur = _band_masks()
        mprev = mprev & (b >= dil)
        hm = _head_masks()
        heads = _pair_heads()
        qh = [_zero_other_head(hm[hh], q_ref[:, _pair_lanes(pair)]) for pair, hh in heads]
        sps = [_dot(qh[i], kp_ref[:, _pair_lanes(pair)], NT) for i, (pair, _) in enumerate(heads)]
        scs = [_dot(qh[i], kc_ref[:, _pair_lanes(pair)], NT) for i, (pair, _) in enumerate(heads)]
        o = [jnp.zeros((DIL_NK, LANES), F32) for _ in range(4)]
        lse = [jnp.zeros((DIL_NK, LANES), F32) for _ in range(4)]
        for i, (pair, hh) in enumerate(heads):
            sp, sc = jnp.where(mprev, sps[i], NEG), jnp.where(mcur, scs[i], NEG)
            m = jnp.maximum(jnp.max(sp, axis=1, keepdims=True), jnp.max(sc, axis=1, keepdims=True))
            ep, ec = jnp.exp(sp - m), jnp.exp(sc - m)
            den = jnp.sum(ep, axis=1, keepdims=True) + jnp.sum(ec, axis=1, keepdims=True)
            lanes = _pair_lanes(pair)
            oh = _dot(ep, _zero_other_head(hm[hh], vp_ref[:, lanes]), NN) + _dot(ec, _zero_other_head(hm[hh], vc_ref[:, lanes]), NN)
            o[pair] = o[pair] + oh / den
            lse[pair] = jnp.where(hm[hh], m + jnp.log(den), lse[pair])
        for pair in range(4):
            o_ref[:, _pair_lanes(pair)] = o[pair]
            lse_ref[:, _pair_lanes(pair)] = lse[pair]

    cur = pl.BlockSpec((DIL_NK, 512), lambda b: (b, 0))
    prev = pl.BlockSpec((DIL_NK, 512), lambda b: (jnp.maximum(b - dil, 0), 0))
    sds = jax.ShapeDtypeStruct
    return pl.pallas_call(
        body,
        name=name,
        grid=(t // DIL_NK,),
        in_specs=[cur, prev, cur, prev, cur],
        out_specs=[cur, cur],
        out_shape=[sds((t, 512), F32), sds((t, 512), F32)],
        compiler_params=_cparams(("parallel",)),
    )(q, k, k, v, v)


def dil_attn_bwd(name, q, k, v, do, lse, dd, dil):
    t = q.shape[0]
    nblk = t // DIL_NK

    def body(q_ref, do_ref, l_ref, d_ref, kp_ref, kc_ref, vp_ref, vc_ref, dq_ref, dk_ref, dv_ref):
        b = pl.program_id(0)

        @pl.when(b == 0)
        def _():
            dk_ref[...] = jnp.zeros(dk_ref.shape, F32)
            dv_ref[...] = jnp.zeros(dv_ref.shape, F32)

        mprev, mcur = _band_masks()
        mprev = mprev & (b >= dil)
        rows_c = pl.ds(pl.multiple_of(b * DIL_NK, DIL_NK), DIL_NK)
        rows_p = pl.ds(pl.multiple_of(jnp.maximum(b - dil, 0) * DIL_NK, DIL_NK), DIL_NK)
        hm = _head_masks()
        heads = _pair_heads()
        ln_ = [_pair_lanes(pair) for pair, _ in heads]
        qh = [_zero_other_head(hm[hh], q_ref[:, ln_[i]]) for i, (_, hh) in enumerate(heads)]
        doh = [_zero_other_head(hm[hh], do_ref[:, ln_[i]]) for i, (_, hh) in enumerate(heads)]
        idx = range(len(heads))
        s_p = [_dot(qh[i], kp_ref[:, ln_[i]], NT) for i in idx]
        s_c = [_dot(qh[i], kc_ref[:, ln_[i]], NT) for i in idx]
        dp_p = [_dot(doh[i], vp_ref[:, ln_[i]], NT) for i in idx]
        dp_c = [_dot(doh[i], vc_ref[:, ln_[i]], NT) for i in idx]
        zero = lambda: [jnp.zeros((DIL_NK, LANES), F32) for _ in range(4)]
        dq, dk_p, dk_c, dv_p, dv_c = zero(), zero(), zero(), zero(), zero()
        for i, (pair, hh) in enumerate(heads):
            lse_h, d_h = _row_scalar(l_ref[:, ln_[i]], hm[hh]), _row_scalar(d_ref[:, ln_[i]], hm[hh])
            pp = jnp.exp(jnp.where(mprev, s_p[i] - lse_h, NEG))
            pc = jnp.exp(jnp.where(mcur, s_c[i] - lse_h, NEG))
            dsp, dsc = pp * (dp_p[i] - d_h), pc * (dp_c[i] - d_h)
            dq[pair] = (dq[pair] + _dot(dsp, _zero_other_head(hm[hh], kp_ref[:, ln_[i]]), NN)
                        + _dot(dsc, _zero_other_head(hm[hh], kc_ref[:, ln_[i]]), NN))
            dv_p[pair] = dv_p[pair] + _dot(pp, doh[i], TN)
            dv_c[pair] = dv_c[pair] + _dot(pc, doh[i], TN)
            dk_p[pair] = dk_p[pair] + _dot(dsp, qh[i], TN)
            dk_c[pair] = dk_c[pair] + _dot(dsc, qh[i], TN)
        for pair in range(4):
            lanes = _pair_lanes(pair)
            dq_ref[:, lanes] = dq[pair]
            dk_ref[rows_p, lanes] += dk_p[pair]
            dv_ref[rows_p, lanes] += dv_p[pair]
            dk_ref[rows_c, lanes] += dk_c[pair]
            dv_ref[rows_c, lanes] += dv_c[pair]

    cur = pl.BlockSpec((DIL_NK, 512), lambda b: (b, 0))
    prev = pl.BlockSpec((DIL_NK, 512), lambda b: (jnp.maximum(b - dil, 0), 0))
    whole = pl.BlockSpec((t, 512), lambda b: (0, 0))
    sds = jax.ShapeDtypeStruct
    return pl.pallas_call(
        body,
        name=name,
        grid=(nblk,),
        in_specs=[cur, cur, cur, cur, prev, cur, prev, cur],
        out_specs=[cur, whole, whole],
        out_shape=[sds((t, 512), F32)] * 3,
        compiler_params=_cparams(("arbitrary",)),
    )(q, do, lse, dd, k, k, v, v)


def dil_combine(os_, lses, z):
    t = z.shape[0]

    def body(o1_ref, l1_ref, o2_ref, l2_ref, o3_ref, l3_ref, g_ref, y_ref, o_ref, lse_ref, yt_ref, so2, sl2, so3, sl3):
        for (window, dil), o_in, l_in, so, sl in ((DIL_GROUPS[1], o2_ref, l2_ref, so2, sl2), (DIL_GROUPS[2], o3_ref, l3_ref, so3, sl3)):
            for u in range(SPAN // DIL_NK):
                src, dst = _rm_src(u, window, dil), _rm_dst(u)
                so[src, :] = o_in[dst, :]
                sl[src, :] = l_in[dst, :]
        for c0 in range(0, SPAN, 256):
            rows = slice(c0, c0 + 256)
            la, lb, lc = l1_ref[rows, :], sl2[rows, :], sl3[rows, :]
            mx = jnp.maximum(jnp.maximum(la, lb), lc)
            wa, wb, wc = jnp.exp(la - mx), jnp.exp(lb - mx), jnp.exp(lc - mx)
            tot = wa + wb + wc
            o = (wa * o1_ref[rows, :] + wb * so2[rows, :] + wc * so3[rows, :]) / tot
            y = o * _silu_and_grad(g_ref[rows, :])[0]
            y_ref[rows, :] = y.astype(MXU_DTYPE)
            o_ref[rows, :] = o
            lse_ref[rows, :] = mx + jnp.log(tot)
            yt_ref[:, rows] = y.T.astype(MXU_DTYPE)

    sds = jax.ShapeDtypeStruct
    return pl.pallas_call(
        body,
        name="dil_combine",
        grid=(t // SPAN, 4),
        in_specs=[_span_blk(0)] * 6 + [_span_blk(36)],
        out_specs=[_span_blk(0)] * 3 + [pl.BlockSpec((LANES, SPAN), lambda s, p: (p, s))],
        out_shape=[sds((t, 512), MXU_DTYPE), sds((t, 512), F32), sds((t, 512), F32), sds((512, t), MXU_DTYPE)],
        scratch_shapes=[pltpu.VMEM((SPAN, LANES), F32)] * 4,
        compiler_params=_cparams(("parallel", "parallel")),
    )(os_[0], lses[0], os_[1], lses[1], os_[2], lses[2], z)


def dil_gate_bwd(dy, o, z, lse):
    t = dy.shape[0]

    def body(dy_ref, o_ref, g_ref, lse_ref, dzg_ref, do1_ref, dd1_ref, do2_ref, dd2_ref, l2_ref, do3_ref, dd3_ref, l3_ref, do_scr):
        for c0 in range(0, SPAN, 256):
            rows = slice(c0, c0 + 256)
            dyv, ov = dy_ref[rows, :], o_ref[rows, :]
            silu, dsilu = _silu_and_grad(g_ref[rows, :])
            do = dyv * silu
            do_scr[rows, :] = do
            do1_ref[rows, :] = do.astype(MXU_DTYPE)
            dd1_ref[rows, :] = _seg_sum_impl(do * ov)
            dzg_ref[rows, :] = (dyv * ov * dsilu).astype(MXU_DTYPE)
        for (window, dil), do_o, dd_o, l_o in ((DIL_GROUPS[1], do2_ref, dd2_ref, l2_ref), (DIL_GROUPS[2], do3_ref, dd3_ref, l3_ref)):
            for u in range(SPAN // DIL_NK):
                src, dst = _rm_src(u, window, dil), _rm_dst(u)
                do_o[dst, :] = do_scr[src, :].astype(MXU_DTYPE)
                dd_o[dst, :] = dd1_ref[src, :]
                l_o[dst, :] = lse_ref[src, :]

    sds = jax.ShapeDtypeStruct
    f32, mxu = sds((t, 512), F32), sds((t, 512), MXU_DTYPE)
    return pl.pallas_call(
        body,
        name="dil_gate_bwd",
        grid=(t // SPAN, 4),
        in_specs=[_span_blk(0), _span_blk(0), _span_blk(36), _span_blk(0)],
        out_specs=[_span_blk(0)] * 9,
        out_shape=[mxu, mxu, f32, mxu, f32, f32, mxu, f32, f32],
        scratch_shapes=[pltpu.VMEM((SPAN, LANES), F32)],
        compiler_params=_cparams(("parallel", "parallel")),
    )(dy, o, z, lse)


def _merge_tile(p0, p1, p2, z0, z1, z2, b0, b1, b2):
    return _sigmoid(z0 + b0) * p0 + _sigmoid(z1 + b1) * p1 + _sigmoid(z2 + b2) * p2


def _merge_ins(ps, z, b):
    w = 256
    return ([_row(p, w, 0, True) for p in ps] + [_row(z, w, 4 * i, True) for i in range(3)]
            + [_const(b, w, 4 * i, True) for i in range(3)])


def merge_fwd(ps, z, b):
    def fn(ins, outs, _):
        merged = _merge_tile(*[r[...] for r in ins])
        outs[0][...] = merged.astype(MXU_DTYPE)
        outs[1][...] = merged.T.astype(MXU_DTYPE)

    return rowwise("merge_fwd", fn, z.shape[0], tm=512, ncol=4, ins=_merge_ins(ps, z, b), outs=[(1024, 256, MXU_DTYPE, True)],
                   touts=[(1024, 256, MXU_DTYPE, True)])


def merge_bwd(dm, ps, z, b):
    def fn(ins, outs, accs):
        _, vjp = jax.vjp(_merge_tile, *[r[...] for r in ins[:9]])
        grads = vjp(ins[9][...])
        for i in range(3):
            outs[i][...] = grads[i].astype(MXU_DTYPE)
            outs[3 + i][...] = grads[3 + i].astype(MXU_DTYPE)
            accs[i][...] += grads[6 + i]

    return rowwise("merge_bwd", fn, z.shape[0], tm=512, ncol=4, ins=_merge_ins(ps, z, b) + [_row(dm, 256, 0, True)],
                   outs=[(1024, 256, MXU_DTYPE, True)] * 6, accs=[(1, 1024, 256, True)] * 3)


EW_BLOCK_BYTES = 2**21


def _tile2d(r, c):
    if r * c * 4 <= EW_BLOCK_BYTES:
        return r, c
    for tr in (512, 256, 128, 64, 32):
        if r % tr == 0 and tr * c * 4 <= EW_BLOCK_BYTES:
            return tr, c
    for tc in (1024, 512, 256, 128):
        if c % tc == 0 and r * tc * 4 <= EW_BLOCK_BYTES:
            return r, tc
    raise ValueError((r, c))


def adamw(name, w, g, m, v):
    shape = w.shape
    c = shape[-1]
    r = w.size // c
    tr, tc = _tile2d(r, c)
    assert tc == c
    c1, c2 = 1.0 - ADAM_B1 ** ADAM_STEP, 1.0 - ADAM_B2 ** ADAM_STEP

    def body(w_ref, g_ref, m_ref, v_ref, d_ref, mo_ref, vo_ref):
        gv = g_ref[...]
        mn = ADAM_B1 * m_ref[...] + (1.0 - ADAM_B1) * gv
        vn = ADAM_B2 * v_ref[...] + (1.0 - ADAM_B2) * (gv * gv)
        d_ref[...] = -ADAM_LR * ((mn / c1) / (jnp.sqrt(vn / c2) + ADAM_EPS) + ADAM_WD * w_ref[...])
        mo_ref[...] = mn
        vo_ref[...] = vn

    spec = pl.BlockSpec((tr, c), lambda i: (i, 0))
    outs = pl.pallas_call(
        body,
        name=name,
        grid=(r // tr,),
        in_specs=[spec] * 4,
        out_specs=[spec] * 3,
        out_shape=[jax.ShapeDtypeStruct((r, c), F32)] * 3,
        compiler_params=_cparams(("parallel",)),
    )(*[a.reshape(r, c) for a in (w, g, m, v)])
    return [o.reshape(shape) for o in outs]


def adamw_pair(name, w, reduced, received, m, v, c_idx):
    _, r, c = w.shape
    tr, tc = _tile2d(r, c)
    c1, c2 = 1.0 - ADAM_B1 ** ADAM_STEP, 1.0 - ADAM_B2 ** ADAM_STEP

    def body(c_ref, w_ref, r0_ref, x0_ref, r1_ref, x1_ref, m_ref, v_ref, g_ref, d_ref, mo_ref, vo_ref):
        mine = c_ref[0]
        gv = jnp.where(pl.program_id(0) == 0, jnp.where(mine == 0, r0_ref[...], x0_ref[...]),
                       jnp.where(mine == 1, r1_ref[...], x1_ref[...]))
        mn = ADAM_B1 * m_ref[...] + (1.0 - ADAM_B1) * gv
        vn = ADAM_B2 * v_ref[...] + (1.0 - ADAM_B2) * (gv * gv)
        g_ref[...] = gv
        d_ref[...] = -ADAM_LR * ((mn / c1) / (jnp.sqrt(vn / c2) + ADAM_EPS) + ADAM_WD * w_ref[...])
        mo_ref[...] = mn
        vo_ref[...] = vn

    full = pl.BlockSpec((None, tr, tc), lambda h, i, j, cr: (h, i, j))
    half = pl.BlockSpec((tr, tc), lambda h, i, j, cr: (i, j))
    return pl.pallas_call(
        body,
        name=name,
        grid_spec=pltpu.PrefetchScalarGridSpec(
            num_scalar_prefetch=1,
            grid=(2, r // tr, c // tc),
            in_specs=[full, half, half, half, half, full, full],
            out_specs=[full] * 4,
        ),
        out_shape=[jax.ShapeDtypeStruct((2, r, c), F32)] * 4,
        compiler_params=_cparams(("parallel", "parallel", "parallel")),
    )(c_idx, w, reduced[0], received[0], reduced[1], received[1], m, v)


def sum_pair(name, g, la, out_dtype):
    _, r, c = g.shape
    tr, tc = _tile2d(r, c)

    def body(g_ref, la_ref, o_ref):
        o_ref[...] = (g_ref[...] + la_ref[...]).astype(o_ref.dtype)

    spec = pl.BlockSpec((None, tr, tc), lambda s, i, j: (s, i, j))
    return pl.pallas_call(
        body,
        name=name,
        grid=(4, r // tr, c // tc),
        in_specs=[spec, spec],
        out_specs=spec,
        out_shape=jax.ShapeDtypeStruct((4, r, c), out_dtype),
        compiler_params=_cparams(("parallel", "parallel", "parallel")),
    )(g, la)


def sum_chips(name, g, la, lb, s_idx):
    _, r, c = g.shape
    tr, tc = _tile2d(r, c)

    def body(s_ref, g_ref, la_ref, l0_ref, l1_ref, l2_ref, o_ref):
        own = g_ref[...] + la_ref[...]
        o_ref[...] = ((own + l0_ref[...].astype(F32)) + l1_ref[...].astype(F32)) + l2_ref[...].astype(F32)

    own_spec = pl.BlockSpec((None, tr, tc), lambda i, j, sr: (sr[0], i, j))
    lspec = lambda k: pl.BlockSpec((None, tr, tc), lambda i, j, sr: (k, i, j))
    return pl.pallas_call(
        body,
        name=name,
        grid_spec=pltpu.PrefetchScalarGridSpec(
            num_scalar_prefetch=1,
            grid=(r // tr, c // tc),
            in_specs=[own_spec, own_spec, lspec(0), lspec(1), lspec(2)],
            out_specs=pl.BlockSpec((tr, tc), lambda i, j, sr: (i, j)),
        ),
        out_shape=jax.ShapeDtypeStruct((r, c), F32),
        compiler_params=_cparams(("parallel", "parallel")),
    )(s_idx, g, la, lb, lb, lb)


def _place():
    x, y, c = lax.axis_index("x"), lax.axis_index("y"), lax.axis_index("c")
    chips = [(1 - x, y), (x, 1 - y), (1 - x, 1 - y)]
    return x, y, c, chips


HBM_SPEC = pl.BlockSpec(memory_space=pltpu.HBM)


def _comm_call(name, body, ins, out_shapes, n_sem, n_local):
    return pl.pallas_call(
        body,
        name=name,
        in_specs=[HBM_SPEC] * len(ins),
        out_specs=[HBM_SPEC] * len(out_shapes),
        out_shape=out_shapes,
        scratch_shapes=[pltpu.SemaphoreType.DMA((n_sem,)), pltpu.SemaphoreType.DMA((n_sem,)),
                        pltpu.SemaphoreType.DMA((max(n_local, 1),))],
    )(*ins)


def _layer_gather(ins, outs, send, recv, layer):
    n = len(ins)
    x, y, c, chips = _place()
    s = 2 * x + y
    sib = (x, y, 1 - c)
    active = c == layer

    def rc(a, k, src, dst, dev):
        return pltpu.make_async_remote_copy(src_ref=src, dst_ref=dst, send_sem=send.at[6 * a + k], recv_sem=recv.at[6 * a + k],
                                            device_id=dev, device_id_type=MESH)

    def first_hop():
        return [rc(a, j, ins[a], outs[a].at[s], (*chip, c)) for j, chip in enumerate(chips) for a in range(n)]

    def start():
        @pl.when(active)
        def _():
            for cp in first_hop():
                cp.start()

    def finish():
        @pl.when(active)
        def _():
            forwards = []
            for j, (cx, cy) in enumerate(chips):
                for a in range(n):
                    landed = outs[a].at[2 * cx + cy]
                    rc(a, j, landed, landed, sib).wait_recv()
                    forwards.append(rc(a, 3 + j, landed, landed, sib))
                    forwards[-1].start()
            for cp in first_hop() + forwards:
                cp.wait_send()

        @pl.when(jnp.logical_not(active))
        def _():
            for j, (cx, cy) in enumerate(chips):
                for a in range(n):
                    other = outs[a].at[2 * cx + cy]
                    rc(a, 3 + j, other, other, sib).wait_recv()

    return start, finish


def _gather_shapes(ws):
    return [jax.ShapeDtypeStruct((4,) + w.shape, w.dtype) for w in ws]


def gather_hook(ws, layer):
    return (ws, _gather_shapes(ws), 6 * len(ws), lambda i, o, s, r: _layer_gather(i, o, s, r, layer))


def allgather_layer(ws, layer):
    n = len(ws)

    def body(*refs):
        send, recv, _ = refs[2 * n:]
        start, finish = _layer_gather(refs[:n], refs[n:2 * n], send, recv, layer)
        start()
        finish()

    return _comm_call(f"allgather_layer{layer}", body, ws, _gather_shapes(ws), 6 * n, 0)


def exchange_sibling(gs, layer, tag=""):
    n = len(gs)

    def body(*refs):
        send, recv, _ = refs[2 * n:]
        start, finish = _sibling_exchange(refs[:n], refs[n:2 * n], send, recv, layer)
        start()
        finish()

    return _comm_call(f"exchange_sibling{layer}{tag}", body, gs, [jax.ShapeDtypeStruct(g.shape, g.dtype) for g in gs], n, 0)


def _sibling_exchange(ins, outs, send, recv, layer):
    x, y, c, _ = _place()

    def copies():
        return [pltpu.make_async_remote_copy(src_ref=ins[a], dst_ref=outs[a], send_sem=send.at[a], recv_sem=recv.at[a],
                                             device_id=(x, y, 1 - c), device_id_type=MESH) for a in range(len(ins))]

    def start():
        @pl.when(c != layer)
        def _():
            for cp in copies():
                cp.start()

    def finish():
        @pl.when(c != layer)
        def _():
            for cp in copies():
                cp.wait_send()

        @pl.when(c == layer)
        def _():
            for cp in copies():
                cp.wait_recv()

    return start, finish


def sibling_hook(gs, layer):
    return (gs, [jax.ShapeDtypeStruct(g.shape, g.dtype) for g in gs], len(gs),
            lambda i, o, s, r: _sibling_exchange(i, o, s, r, layer))


def chips_hook(ps, layer):
    return (ps, _chips_shapes(ps), 3 * len(ps), lambda i, o, s, r: _chips_exchange(i, o, s, r, layer))


def _chips_exchange(ins, outs, send, recv, layer):
    n = len(ins)
    _, _, c, chips = _place()

    def copies():
        return [pltpu.make_async_remote_copy(src_ref=ins[a].at[2 * cx + cy], dst_ref=outs[a].at[j], send_sem=send.at[3 * a + j],
                                             recv_sem=recv.at[3 * a + j], device_id=(cx, cy, c), device_id_type=MESH)
                for j, (cx, cy) in enumerate(chips) for a in range(n)]

    def start():
        @pl.when(c == layer)
        def _():
            for cp in copies():
                cp.start()

    def finish():
        @pl.when(c == layer)
        def _():
            for cp in copies():
                cp.wait()

    return start, finish


def _chips_shapes(ps):
    return [jax.ShapeDtypeStruct((3,) + p.shape[1:], p.dtype) for p in ps]


def exchange_chips(ps, layer, tag=""):
    n = len(ps)

    def body(*refs):
        send, recv, _ = refs[2 * n:]
        start, finish = _chips_exchange(refs[:n], refs[n:2 * n], send, recv, layer)
        start()
        finish()

    return _comm_call(f"exchange_chips{layer}{tag}", body, ps, _chips_shapes(ps), 3 * n, 0)


def exchange_final(rs, layer, small=None):
    n = len(rs)
    ins_all = list(rs) + ([small] if small is not None else [])

    def body(*refs):
        ni = len(ins_all)
        ins, outs = refs[:ni], refs[ni:2 * ni]
        send, recv, lsem = refs[2 * ni:]
        x, y, c, _ = _place()
        s = 2 * x + y
        to_sib = [pltpu.make_async_remote_copy(src_ref=ins[a], dst_ref=outs[a], send_sem=send.at[a], recv_sem=recv.at[a],
                                               device_id=(x, y, 1 - c), device_id_type=MESH) for a in range(n)]
        rel = [(fx, fy) for fx in (0, 1) for fy in (0, 1)]

        def piece(fx, fy, cc, dst_slot):
            k = n + 2 * (2 * fx + fy) + cc
            return pltpu.make_async_remote_copy(src_ref=ins[n], dst_ref=outs[n].at[dst_slot], send_sem=send.at[k], recv_sem=recv.at[k],
                                                device_id=(x ^ fx, y ^ fy, cc), device_id_type=MESH)

        @pl.when(c == layer)
        def _():
            for cp in to_sib:
                cp.start()
            if small is not None:
                own = pltpu.make_async_copy(ins[n], outs[n].at[s], lsem.at[0])
                own.start()
                sends = [piece(fx, fy, cc, s) for fx, fy in rel for cc in (0, 1) if (fx, fy) != (0, 0) or cc != layer]
                for cp in sends:
                    cp.start()
                for fx, fy in rel[1:]:
                    piece(fx, fy, layer, 2 * (x ^ fx) + (y ^ fy)).wait_recv()
                for cp in sends:
                    cp.wait_send()
                own.wait()
            for cp in to_sib:
                cp.wait_send()

        @pl.when(c != layer)
        def _():
            for cp in to_sib:
                cp.wait_recv()
            if small is not None:
                for fx, fy in rel:
                    piece(fx, fy, 1 - layer, 2 * (x ^ fx) + (y ^ fy)).wait_recv()

    out_shapes = [jax.ShapeDtypeStruct(r.shape, r.dtype) for r in rs]
    if small is not None:
        out_shapes.append(jax.ShapeDtypeStruct((4,) + small.shape, small.dtype))
    res = _comm_call(f"exchange_final{layer}", body, ins_all, out_shapes, n + 8, 1)
    return (res[:n], res[n]) if small is not None else (res[:n], None)


def _pad_in_cols(w):
    z = lambda n: jnp.zeros(w.shape[:-1] + (n,), w.dtype)
    return jnp.concatenate([w[..., :KR_OFF], z(64), w[..., KR_OFF:KR_OFF + MLA_ROPE], z(32), w[..., KR_OFF + MLA_ROPE:]], axis=-1)


def _pad_heads(w, real):
    k = w.shape[0]
    return jnp.pad(w.reshape(k, MLA_HEADS, real), ((0, 0), (0, 0), (0, LANES - real))).reshape(k, MLA_HEADS * LANES)


def _pad_gain(g, real):
    return jnp.pad(g.reshape(1, real), ((0, 0), (0, LANES - real)))


def layer_weights(full):
    w_ukv = full["w_ukv"].reshape(128, MLA_HEADS, 2, 64)
    two = lambda g: jnp.concatenate([g, g]).reshape(1, LANES)
    return dict(
        norm_g=full["norm_g"].reshape(1, -1), w_in=full["w_in"], conv_w=full["conv_w"], conv_b=full["conv_b"].reshape(1, -1),
        wgx=full["w_gate_x"], bgx=full["b_gate_x"].reshape(LRU_BLOCKS, 1, LANES),
        wga=full["w_gate_a"], bga=full["b_gate_a"].reshape(LRU_BLOCKS, 1, LANES),
        lam=full["lru_lambda"].reshape(1, -1), w_lru_o=full["w_lru_o"],
        cq_norm_g=full["cq_norm_g"].reshape(1, -1), ckv_norm_g=full["ckv_norm_g"].reshape(1, -1),
        wq=_pad_heads(full["w_uq"], MLA_QK), wk=_pad_heads(w_ukv[:, :, 0].reshape(128, 512), 64),
        wv=w_ukv[:, :, 1].reshape(128, 512),
        gq=_pad_gain(full["mla_q_norm_g"], MLA_QK), gk=_pad_gain(full["mla_k_norm_g"], MLA_QK),
        w_mla_o=full["w_mla_o"], gq2=two(full["dil_q_norm_g"]), gk2=two(full["dil_k_norm_g"]),
        w_dil_o=full["w_dil_o"], b_merge=full["b_merge"].reshape(1, -1), w_out=full["w_out"],
    )


def layer_fwd(x, w, tabs, gather=None):
    h, ht = rmsnorm_fwd(x, w["norm_g"])
    z_lru = mm_nn("in_proj_lru", h, w["w_in"], n=2048, n_off=G_LRU)
    z_mla = mm_nn("in_proj_mla", h, w["w_in"], n=1024, n_off=G_MLA)
    z_dil = mm_nn("in_proj_dil", h, w["w_in"], n=5120, n_off=G_DIL)
    z_mrg = mm_nn("in_proj_mrg", h, w["w_in"], n=3072, n_off=G_MRG)
    rest_hook = None if gather is None else gather_hook(gather[0][1:], gather[1])
    hs, y_lru, yt_lru, *ga_rest = lru_fwd(z_lru, w["conv_w"], w["conv_b"], w["wgx"], w["bgx"], w["wga"], w["bga"], w["lam"],
                                          comm=rest_hook)
    qm, km, vm, vtm = mla_prep_fwd(z_mla, tabs, w)
    o_mla, y_mla, yt_mla, lse_mla, *ga_first = mla_attn_fwd(qm, km, vtm, z_mla,
                                                            gather=None if gather is None else (gather[0][:1], gather[1]))
    gathered = ga_first + ga_rest
    os_, lses, dil_rm = [], [], []
    for gi, (_, dil) in enumerate(DIL_GROUPS):
        qkv = dil_prep_fwd(z_dil, tabs, w["gq2"], w["gk2"], gi)
        o, lse = dil_attn_fwd(f"dil_attn_fwd_{dil}", *qkv, dil)
        os_.append(o)
        lses.append(lse)
        dil_rm.append(qkv)
    y_dil, o_dil, lse_dil, yt_dil = dil_combine(os_, lses, z_dil)
    ps = [mm_nn("proj_lru", y_lru, w["w_lru_o"]), mm_nn("proj_mla", y_mla, w["w_mla_o"]), mm_nn("proj_dil", y_dil, w["w_dil_o"])]
    merged, merged_t = merge_fwd(ps, z_mrg, w["b_merge"])
    out = mm_nn("out_proj", merged, w["w_out"], add=x)
    res = dict(x=x, ht=ht, z_lru=z_lru, z_mla=z_mla, z_dil=z_dil, z_mrg=z_mrg, hs=hs, yt_lru=yt_lru, qm=qm, km=km, vm=vm, o_mla=o_mla,
               yt_mla=yt_mla, lse_mla=lse_mla, dil_rm=dil_rm, yt_dil=yt_dil, o_dil=o_dil, lse_dil=lse_dil, ps=ps, merged_t=merged_t)
    return out, res, gathered


def layer_bwd(dout, r, w, tabs, prev=None, own=None):
    g = {}
    dmerged = mm_nt("out_proj_dx", dout, w["w_out"])
    g["w_out"] = mm_nn("out_proj_dw", r["merged_t"], dout, tm=1024, tn=512, tk=2048)
    dp0, dp1, dp2, dzm0, dzm1, dzm2, db0, db1, db2 = merge_bwd(dmerged, r["ps"], r["z_mrg"], w["b_merge"])
    g["b_merge"] = jnp.concatenate([db0, db1, db2], axis=1).reshape(-1)
    dy_lru = mm_nt("proj_lru_dx", dp0, w["w_lru_o"])
    dy_mla = mm_nt("proj_mla_dx", dp1, w["w_mla_o"])
    dy_dil = mm_nt("proj_dil_dx", dp2, w["w_dil_o"])
    g["w_lru_o"] = mm_nn("proj_lru_dw", r["yt_lru"], dp0, **DW_TILES)
    g["w_mla_o"] = mm_nn("proj_mla_dw", r["yt_mla"], dp1, **DW_TILES)
    g["w_dil_o"] = mm_nn("proj_dil_dw", r["yt_dil"], dp2, **DW_TILES)
    dzx, dzg_lru, dwgx, dbgx, dwga, dbga, dlam, dcw, dcb, *la_prev = lru_bwd(
        r["z_lru"], r["hs"], dy_lru, w["conv_w"], w["conv_b"], w["wgx"], w["bgx"], w["wga"], w["bga"], w["lam"],
        comm=None if prev is None else sibling_hook(prev[0], prev[1]))
    chips = None if prev is None else (prev[2](la_prev), prev[1])
    g.update(w_gate_x=dwgx, b_gate_x=dbgx.reshape(LRU_BLOCKS, LANES), w_gate_a=dwga, b_gate_a=dbga.reshape(LRU_BLOCKS, LANES),
             lru_lambda=dlam.reshape(-1), conv_w=dcw, conv_b=dcb.reshape(-1))
    do_m, dzg_mla, dd_m = mla_gate_bwd(dy_mla, r["o_mla"], r["z_mla"])
    dq_m, dk_m, dv_m, *lb_prev = mla_attn_bwd(r["qm"], r["km"], r["vm"], do_m, r["lse_mla"], dd_m, chips=chips)
    dz_mla3, dg_cq, dg_ckv, dwq, dwk, dwv, dgq, dgk = mla_prep_bwd(r["z_mla"], tabs, w, dq_m, dk_m, dv_m)
    g.update(cq_norm_g=dg_cq.reshape(-1), ckv_norm_g=dg_ckv.reshape(-1), mla_q_norm_g=dgq[0, :MLA_QK], mla_k_norm_g=dgk[0, :MLA_QK])
    g["w_uq"] = dwq.reshape(256, MLA_HEADS, LANES)[:, :, :MLA_QK].reshape(256, MLA_HEADS * MLA_QK)
    g["w_ukv"] = jnp.concatenate([dwk.reshape(128, MLA_HEADS, LANES)[:, :, :64], dwv.reshape(128, MLA_HEADS, 64)], axis=2).reshape(128, 1024)
    dzg_dil, do1, dd1, do2, dd2, l2, do3, dd3, l3 = dil_gate_bwd(dy_dil, r["o_dil"], r["z_dil"], r["lse_dil"])
    stats = [(do1, r["lse_dil"], dd1), (do2, l2, dd2), (do3, l3, dd3)]
    dzq, dzk, dzv, dgq2, dgk2 = [], [], [], [], []
    for gi, (_, dil) in enumerate(DIL_GROUPS):
        dq, dk, dv = dil_attn_bwd(f"dil_attn_bwd_{dil}", *r["dil_rm"][gi], *stats[gi], dil)
        parts = dil_prep_bwd(r["z_dil"], tabs, w["gq2"], w["gk2"], dq, dk, dv, gi)
        for acc, part in zip((dzq, dzk, dzv, dgq2, dgk2), parts):
            acc.append(part)
    g.update(dil_q_norm_g=sum(dgq2)[0, :DIL_HD], dil_k_norm_g=sum(dgk2)[0, :DIL_HD])
    dz = jnp.concatenate([dzx, dzg_lru, dz_mla3, dzg_mla] + dzq + dzk + dzv + [dzg_dil, dzm0, dzm1, dzm2], axis=1)
    g["w_in"] = in_proj_dw(r["ht"], dz)
    own_hook = None if own is None else chips_hook(*own(g))
    out = mm_nt("in_proj_dx", dz, w["w_in"], tm=1024, tn=1024, tk=IN_PAD // 4, comm=own_hook)
    dh, lb_own = (out, []) if own_hook is None else (out[0], out[1:])
    dx, dng = rmsnorm_bwd(r["x"], dh, dout, w["norm_g"])
    g["norm_g"] = dng.reshape(-1)
    return dx, g, dict(la_prev=la_prev, lb_prev=lb_prev, lb_own=lb_own)


def local_step(x, positions, target, full0, full1=None, gather1=None, reduce=None):
    tabs = rope_tables(positions.reshape(-1, 1))
    ws, ress = [], []
    for l in range(2):
        if l == 0:
            ws.append(layer_weights(full0))
            x, res, gathered = layer_fwd(x, ws[0], tabs, gather=None if gather1 is None else (gather1[0], 1))
        else:
            ws.append(layer_weights(full1 if gather1 is None else gather1[1](gathered)))
            x, res, _ = layer_fwd(x, ws[1], tabs)
        ress.append(res)
    dy, loss = loss_head(x, target)
    dy, grads1, _ = layer_bwd(dy, ress[1], ws[1], tabs)
    prev1, own0 = (None, None) if reduce is None else (reduce[0](grads1), reduce[1])
    dy, grads0, landed = layer_bwd(dy, ress[0], ws[0], tabs, prev=prev1, own=own0)
    return loss, dy, [grads0, grads1], landed


WEIGHTS = ["norm_g", "w_in", "conv_w", "conv_b", "w_gate_x", "b_gate_x", "w_gate_a", "b_gate_a", "lru_lambda", "w_lru_o", "cq_norm_g",
           "ckv_norm_g", "w_uq", "w_ukv", "mla_q_norm_g", "mla_k_norm_g", "w_mla_o", "dil_q_norm_g", "dil_k_norm_g", "w_dil_o", "b_merge",
           "w_out"]
SHARDED = {"w_in": 2, "conv_w": 2, "w_lru_o": 1, "w_uq": 2, "w_ukv": 2, "w_mla_o": 2, "w_dil_o": 2, "w_out": 1}
REPLICATED = [n for n in WEIGHTS if n not in SHARDED]
SMALL_ROWS = 144


def kernel(x, positions, norm_g, w_in, conv_w, conv_b, w_gate_x, b_gate_x, w_gate_a, b_gate_a, lru_lambda, w_lru_o, cq_norm_g, ckv_norm_g, w_uq, w_ukv, mla_q_norm_g, mla_k_norm_g, w_mla_o, dil_q_norm_g, dil_k_norm_g, w_dil_o, b_merge, w_out, loss_target, m_norm_g, m_w_in, m_conv_w, m_conv_b, m_w_gate_x, m_b_gate_x, m_w_gate_a, m_b_gate_a, m_lru_lambda, m_w_lru_o, m_cq_norm_g, m_ckv_norm_g, m_w_uq, m_w_ukv, m_mla_q_norm_g, m_mla_k_norm_g, m_w_mla_o, m_dil_q_norm_g, m_dil_k_norm_g, m_w_dil_o, m_b_merge, m_w_out, v_norm_g, v_w_in, v_conv_w, v_conv_b, v_w_gate_x, v_b_gate_x, v_w_gate_a, v_b_gate_a, v_lru_lambda, v_w_lru_o, v_cq_norm_g, v_ckv_norm_g, v_w_uq, v_w_ukv, v_mla_q_norm_g, v_mla_k_norm_g, v_w_mla_o, v_dil_q_norm_g, v_dil_k_norm_g, v_w_dil_o, v_b_merge, v_w_out):
    args = locals()
    w = {n: args[n] for n in WEIGHTS}
    m = {n: args["m_" + n] for n in WEIGHTS}
    v = {n: args["v_" + n] for n in WEIGHTS}
    my_c = lax.axis_index("c").astype(jnp.int32)
    my_s = (2 * lax.axis_index("x") + lax.axis_index("y")).astype(jnp.int32)
    c_idx = my_c.reshape(1)
    s_idx = my_s.reshape(1)

    names = list(SHARDED)
    wire = [[w[n][l] if n == "conv_w" else w[n][l].astype(BF16) for n in names] for l in range(2)]

    def assemble(l, gathered):
        full = {n: w[n][l] for n in REPLICATED}
        for n, ga, own in zip(names, gathered, wire[l]):
            parts = [jnp.where(my_s == s, own, ga[s]) for s in range(4)]
            if n == "w_in":
                parts = [_pad_in_cols(parts[0])] + parts[1:]
            full[n] = jnp.concatenate(parts, axis=SHARDED[n] - 1)
        return full

    full0 = assemble(0, allgather_layer(wire[0], 0))

    def shards_of(grads_l):
        gs = []
        for n in names:
            g_ = grads_l[n]
            if n == "w_in":
                gs.append(g_.reshape(4, IN_WIDTH // 4, D_MODEL))
            else:
                parts = jnp.stack(jnp.split(g_, 4, axis=SHARDED[n] - 1))
                gs.append(parts.reshape(4, -1, parts.shape[-1]))
        return gs

    def pair_sums(l, gs, la, tags):
        dts = [BF16 if g_.size >= 2**19 and t != "small" else F32 for g_, t in zip(gs, tags)]
        return [sum_pair(f"sum_pair{l}_{t}", g_, l_, dt) for t, g_, l_, dt in zip(tags, gs, la, dts)]

    def chip_sums(l, gs, la, lb, tags):
        return [sum_chips(f"sum_chips{l}_{t}", g_, l_, b_, s_idx) for t, g_, l_, b_ in zip(tags, gs, la, lb)]

    stash = {}

    def prev1(grads1):
        stash["gs1"] = shards_of(grads1)
        return stash["gs1"], 1, lambda la: pair_sums(1, stash["gs1"], la, names)

    def own0(grads0):
        stash["gs0"] = shards_of(grads0)
        stash["la0"] = exchange_sibling(stash["gs0"], 0)
        return pair_sums(0, stash["gs0"], stash["la0"], names), 0

    loss, grad_x, grads, landed = local_step(x[0], positions[0], loss_target[0], full0,
                                             gather1=(wire[1], functools.partial(assemble, 1)), reduce=(prev1, own0))
    loss = lax.psum(loss, ("x", "y", "c"))
    reduced1 = chip_sums(1, stash["gs1"], landed["la_prev"], landed["lb_prev"], names)
    received1, _ = exchange_final(reduced1, 1)
    reduced0 = chip_sums(0, stash["gs0"], stash["la0"], landed["lb_own"], names)

    flat = jnp.concatenate([jnp.stack([grads[0][n], grads[1][n]]).reshape(-1) for n in REPLICATED])
    small = [jnp.pad(flat, (0, 4 * SMALL_ROWS * 1024 - flat.size)).reshape(4, SMALL_ROWS, 1024)]
    la_s = exchange_sibling(small, 0, tag="_small")
    lb_s = exchange_chips(pair_sums(0, small, la_s, ["small"]), 0, tag="_small")
    received0, small_all = exchange_final(reduced0, 0, small=chip_sums(0, small, la_s, lb_s, ["small"])[0])

    g_local, delta, new_m, new_v = {}, {}, {}, {}
    for i, n in enumerate(names):
        shp = w[n].shape
        if n == "w_in":
            as3 = lambda a: a.transpose(0, 2, 1)
            back = lambda o: o.transpose(0, 2, 1)
        else:
            as3 = lambda a, rc=reduced0[i].shape: a.reshape((2,) + rc)
            back = lambda o, shp=shp: o.reshape(shp)
        outs = adamw_pair(f"adamw_{n}", as3(w[n]), (reduced0[i], reduced1[i]), (received0[i], received1[i]), as3(m[n]), as3(v[n]),
                          c_idx)
        g_local[n], delta[n], new_m[n], new_v[n] = [back(o) for o in outs]
    flat = small_all.reshape(-1)
    off = 0
    for n in REPLICATED:
        g_local[n] = flat[off:off + w[n].size].reshape(w[n].shape)
        off += w[n].size
        delta[n], new_m[n], new_v[n] = adamw(f"adamw_{n}", w[n], g_local[n], m[n], v[n])
    return (loss, grad_x[None], *[g_local[n] for n in WEIGHTS], *[delta[n] for n in WEIGHTS], *[new_m[n] for n in WEIGHTS],
            *[new_v[n] for n in WEIGHTS])
```

```python
import functools

import jax
import jax.numpy as jnp
from jax import lax
from jax.experimental import pallas as pl
from jax.experimental.pallas import tpu as pltpu

F32 = jnp.float32
BF16 = jnp.bfloat16
MXU_DTYPE = jnp.bfloat16

D_MODEL = 1024
EPS = 1e-6
ROPE_THETA = 10000.0
LRU_BLOCKS = 8
LRU_C = 8.0
MLA_HEADS = 8
MLA_NOPE = 64
MLA_ROPE = 32
MLA_QK = 96
DIL_GROUPS = ((128, 1), (512, 4), (2048, 16))
DIL_HD = 64
DIL_NK = 128
MLA_HPS = 8
IN_WIDTH = 11168
ADAM_LR, ADAM_B1, ADAM_B2, ADAM_EPS, ADAM_WD, ADAM_STEP = 0.001, 0.9, 0.999, 1e-08, 0.01, 10

LANES = 128
G_LRU, G_MLA, G_DIL, G_MRG = 0, 2048, 3072, 8192
IN_PAD = 11264
KR_OFF = 2432

NN = (((1,), (0,)), ((), ()))
NT = (((1,), (1,)), ((), ()))
TN = (((0,), (0,)), ((), ()))
NEG = -1e30
MESH = pl.DeviceIdType.MESH
VMEM_LIMIT = 48 * 2**20


def _cparams(sem):
    return pltpu.CompilerParams(dimension_semantics=sem, vmem_limit_bytes=VMEM_LIMIT)


def _dot(a, b, dims):
    return lax.dot_general(a.astype(MXU_DTYPE), b.astype(MXU_DTYPE), dims, preferred_element_type=F32)


@jax.custom_vjp
def mm(a, w):
    return _dot(a, w, NN)


def _mm_fwd(a, w):
    return _dot(a, w, NN), (a, w)


def _mm_bwd(res, g):
    a, w = res
    return _dot(g, w, NT), _dot(a, g, TN)


mm.defvjp(_mm_fwd, _mm_bwd)


def _seg64_matrix():
    r = lax.broadcasted_iota(jnp.int32, (LANES, LANES), 0) < DIL_HD
    c = lax.broadcasted_iota(jnp.int32, (LANES, LANES), 1) < DIL_HD
    return (r == c).astype(BF16)


def _seg_sum_impl(x):
    b = _seg64_matrix()
    hi = x.astype(BF16)
    r1 = x - hi.astype(F32)
    mid = r1.astype(BF16)
    lo = (r1 - mid.astype(F32)).astype(BF16)
    dot = lambda u: lax.dot_general(u, b, NN, preferred_element_type=F32)
    return dot(hi) + dot(mid) + dot(lo)


@jax.custom_vjp
def seg_sum(x):
    return _seg_sum_impl(x)


seg_sum.defvjp(lambda x: (_seg_sum_impl(x), None), lambda _, g: (_seg_sum_impl(g),))


def _lroll_impl(x, s):
    return pltpu.roll(x, s % LANES, 1)


@functools.partial(jax.custom_vjp, nondiff_argnums=(1,))
def lroll(x, s):
    return _lroll_impl(x, s)


lroll.defvjp(lambda x, s: (_lroll_impl(x, s), None), lambda s, _, g: (_lroll_impl(g, -s),))


class _Ops:
    def __init__(self, diff):
        self.mm = mm if diff else (lambda a, w: _dot(a, w, NN))
        self.seg_sum = seg_sum if diff else _seg_sum_impl
        self.lroll = lroll if diff else _lroll_impl


PLAIN, DIFF = _Ops(False), _Ops(True)


def rms(x, g, n):
    ms = jnp.sum(x * x, axis=-1, keepdims=True) * (1.0 / n)
    return x * lax.rsqrt(ms + EPS) * g


def rope(ops, x, c, s1, s2, half):
    return x * c + ops.lroll(x, -half) * s1 + ops.lroll(x, half) * s2


def _sigmoid(x):
    return 1.0 / (1.0 + jnp.exp(-x))


def _silu_and_grad(g):
    sg = _sigmoid(g)
    return g * sg, sg * (1.0 + g * (1.0 - sg))


def _softplus(x):
    return jnp.maximum(x, 0.0) + jnp.log(1.0 + jnp.exp(-jnp.abs(x)))


def _expm1(y):
    series = y * (1.0 + y * (0.5 + y * (1.0 / 6.0 + y * (1.0 / 24.0 + y * (1.0 / 120.0)))))
    return jnp.where(y > -0.05, series, jnp.exp(jnp.minimum(y, -0.05)) - 1.0)


def _host(body, n_in, n_out, n_scr, hook, grid):
    if hook is None:
        return body, [], [], [], [], []
    ins, out_shapes, n_sem, make = hook
    ni, no = len(ins), len(out_shapes)
    hbm = pl.BlockSpec(memory_space=pltpu.HBM)

    def at(corner):
        cond = None
        for axis, size in enumerate(grid):
            here = pl.program_id(axis) == (size - 1 if corner else 0)
            cond = here if cond is None else cond & here
        return cond

    def hosted(*refs):
        a, refs = refs[:n_in], refs[n_in:]
        xi, refs = refs[:ni], refs[ni:]
        o, refs = refs[:n_out], refs[n_out:]
        xo, refs = refs[:no], refs[no:]
        scr, (send, recv) = refs[:n_scr], refs[n_scr:]
        start, finish = make(xi, xo, send, recv)
        pl.when(at(False))(start)
        body(*a, *o, *scr)
        pl.when(at(True))(finish)

    sems = [pltpu.SemaphoreType.DMA((n_sem,)), pltpu.SemaphoreType.DMA((n_sem,))]
    return hosted, list(ins), [hbm] * ni, [hbm] * no, list(out_shapes), sems


def _mm_call(name, a, b, *, mode, m, n, k, a_blk, b_blk, out_dtype, add, tm, tn, tk, comm=None):
    nk = k // tk
    assert m % tm == 0 and n % tn == 0 and k % tk == 0, (name, m, n, k, tm, tn, tk)
    dims = {"nn": NN, "nt": NT, "tn": TN}[mode]

    def body(*refs):
        if add is None:
            a_ref, b_ref, o_ref, *scr = refs
            add_ref = None
        else:
            a_ref, b_ref, add_ref, o_ref, *scr = refs
        part = _dot(a_ref[...], b_ref[...], dims)

        def finish(acc):
            if add_ref is not None:
                acc = acc + add_ref[...]
            o_ref[...] = acc.astype(o_ref.dtype)

        if nk == 1:
            finish(part)
        else:
            (acc_ref,) = scr
            kk = pl.program_id(2)

            @pl.when(kk == 0)
            def _():
                acc_ref[...] = part

            @pl.when(kk > 0)
            def _():
                acc_ref[...] += part

            @pl.when(kk == nk - 1)
            def _():
                finish(acc_ref[...])

    in_specs = [a_blk, b_blk]
    args = [a, b]
    if add is not None:
        in_specs.append(pl.BlockSpec((tm, tn), lambda j, i, kk: (i, j)))
        args.append(add)
    grid = (n // tn, m // tm, nk)
    scratch = [] if nk == 1 else [pltpu.VMEM((tm, tn), F32)]
    body, x_in, x_in_specs, x_out_specs, x_out_shapes, sems = _host(body, len(args), 1, len(scratch), comm, grid)
    res = pl.pallas_call(
        body,
        name=name,
        grid=grid,
        in_specs=in_specs + x_in_specs,
        out_specs=[pl.BlockSpec((tm, tn), lambda j, i, kk: (i, j))] + x_out_specs,
        out_shape=[jax.ShapeDtypeStruct((m, n), out_dtype)] + x_out_shapes,
        scratch_shapes=scratch + sems,
        compiler_params=_cparams(("arbitrary",) * 3 if comm is not None else ("parallel", "parallel", "arbitrary")),
    )(*args, *x_in)
    return res[0] if comm is None else res


def mm_nn(name, a, b, *, n=None, n_off=0, out_dtype=F32, add=None, tm=512, tn=1024, tk=1024):
    m, k = a.shape
    n = b.shape[1] if n is None else n
    tm, tn, tk = min(tm, m), min(tn, n), min(k, tk)
    ob = n_off // tn
    assert n_off % tn == 0
    return _mm_call(name, a, b, mode="nn", m=m, n=n, k=k, out_dtype=out_dtype, add=add, tm=tm, tn=tn, tk=tk,
                    a_blk=pl.BlockSpec((tm, tk), lambda j, i, kk: (i, kk)),
                    b_blk=pl.BlockSpec((tk, tn), lambda j, i, kk: (kk, j + ob)))


def mm_nt(name, a, b, *, out_dtype=F32, tm=512, tn=1024, tk=1024, comm=None):
    m, k = a.shape
    n = b.shape[0]
    tn, tk = min(tn, n), min(k, tk)
    return _mm_call(name, a, b, mode="nt", m=m, n=n, k=k, out_dtype=out_dtype, add=None, tm=tm, tn=tn, tk=tk, comm=comm,
                    a_blk=pl.BlockSpec((tm, tk), lambda j, i, kk: (i, kk)),
                    b_blk=pl.BlockSpec((tn, tk), lambda j, i, kk: (j, kk)))


DW_TILES = dict(tm=1024, tn=512, tk=4096)


def in_proj_dw(ht, dz):
    t = ht.shape[1]
    tn = 512
    nchunk = IN_PAD // tn
    kr_chunk = KR_OFF // tn
    lo = KR_OFF - kr_chunk * tn

    def body(ht_ref, dz_ref, out_ref, buf, sems):
        j = pl.program_id(0)
        slot = j % 2
        buf[slot] = _dot(ht_ref[...], dz_ref[...], NN).T

        def copies(c, s):
            dst = jnp.where(c < kr_chunk, c * tn, (c - 1) * tn + lo + MLA_ROPE)
            plain = [pltpu.make_async_copy(buf.at[s], out_ref.at[pl.ds(pl.multiple_of(dst, 8), tn)], sems.at[s, 0])]
            split = [pltpu.make_async_copy(buf.at[s, 0:lo], out_ref.at[KR_OFF - lo:KR_OFF], sems.at[s, 0]),
                     pltpu.make_async_copy(buf.at[s, lo + 64:lo + 96], out_ref.at[KR_OFF:KR_OFF + MLA_ROPE], sems.at[s, 1])]
            return plain, split

        def each(c, s, action):
            plain, split = copies(c, s)

            @pl.when(c != kr_chunk)
            def _():
                for cp in plain:
                    action(cp)

            @pl.when(c == kr_chunk)
            def _():
                for cp in split:
                    action(cp)

        @pl.when(j > 0)
        def _():
            each(j - 1, 1 - slot, lambda cp: cp.wait())

        each(j, slot, lambda cp: cp.start())

        @pl.when(j == nchunk - 1)
        def _():
            each(j, slot, lambda cp: cp.wait())

    return pl.pallas_call(
        body,
        name="in_proj_dw",
        grid=(nchunk,),
        in_specs=[pl.BlockSpec((D_MODEL, t), lambda j: (0, 0)), pl.BlockSpec((t, tn), lambda j: (0, j))],
        out_specs=pl.BlockSpec(memory_space=pltpu.HBM),
        out_shape=jax.ShapeDtypeStruct((IN_WIDTH, D_MODEL), F32),
        scratch_shapes=[pltpu.VMEM((2, tn, D_MODEL), F32), pltpu.SemaphoreType.DMA((2, 2))],
        compiler_params=_cparams(("arbitrary",)),
    )(ht, dz)


def rowwise(name, fn, t, *, tm=256, ncol=1, ins=(), outs=(), touts=(), accs=()):
    n_in, n_out, n_acc = len(ins), len(outs) + len(touts), len(accs)

    def zero_when(ref, cond):
        @pl.when(cond)
        def _():
            ref[...] = jnp.zeros(ref.shape, ref.dtype)

    def body(*refs):
        in_refs, out_refs, acc_refs = refs[:n_in], refs[n_in:n_in + n_out], refs[n_in + n_out:]
        j, i = pl.program_id(0), pl.program_id(1)
        for ref, (_, _, _, cd) in zip(acc_refs, accs):
            zero_when(ref, (i == 0) if cd else ((i == 0) & (j == 0)))
        fn(in_refs, out_refs, acc_refs)

    def in_spec(arr, w, base, cd, rd):
        rows = tm if rd else arr.shape[0]
        return pl.BlockSpec((rows, w), lambda j, i: (i if rd else 0, base + (j if cd else 0)))

    in_specs = [in_spec(*e) for e in ins]
    out_specs = [pl.BlockSpec((tm, w), (lambda j, i, cd=cd: (i, j if cd else 0))) for (_, w, _, cd) in outs]
    out_specs += [pl.BlockSpec((w, tm), (lambda j, i, cd=cd: (j if cd else 0, i))) for (_, w, _, cd) in touts]
    out_specs += [pl.BlockSpec((r, w), (lambda j, i, cd=cd: (0, j if cd else 0))) for (r, _, w, cd) in accs]
    out_shape = [jax.ShapeDtypeStruct((t, c), dt) for (c, _, dt, _) in outs]
    out_shape += [jax.ShapeDtypeStruct((r, t), dt) for (r, _, dt, _) in touts]
    out_shape += [jax.ShapeDtypeStruct((r, c), F32) for (r, c, _, _) in accs]
    res = pl.pallas_call(
        body,
        name=name,
        grid=(ncol, t // tm),
        in_specs=in_specs,
        out_specs=out_specs,
        out_shape=out_shape,
        compiler_params=_cparams(("arbitrary", "arbitrary") if accs else ("parallel", "parallel")),
    )(*[e[0] for e in ins])
    return res


def _row(arr, w=None, base=0, cd=False):
    return (arr, arr.shape[1] if w is None else w, base, cd, True)


def _const(arr, w=None, base=0, cd=False):
    return (arr, arr.shape[1] if w is None else w, base, cd, False)


def rope_tables(positions):
    t = positions.shape[0]

    def fn(ins, outs, _):
        pos = ins[0][...].astype(F32)
        lane = lax.broadcasted_iota(jnp.int32, (1, LANES), 1)
        log_theta = jnp.log(jnp.float32(ROPE_THETA))
        jm = lane - MLA_NOPE
        idx = jnp.clip(jnp.where(jm < 16, jm, jm - 16), 0, 15).astype(F32)
        ang = pos * jnp.exp(-(idx * (2.0 / MLA_ROPE)) * log_theta)
        cos, sin = jnp.cos(ang), jnp.sin(ang)
        in_rope = (lane >= MLA_NOPE) & (lane < MLA_QK)
        outs[0][:, 0:128] = jnp.where(lane < MLA_NOPE, 1.0, jnp.where(in_rope, cos, 0.0))
        outs[0][:, 128:256] = jnp.where(in_rope & (jm < 16), -sin, 0.0)
        outs[0][:, 256:384] = jnp.where(in_rope & (jm >= 16), sin, 0.0)
        jd = lane & (DIL_HD - 1)
        idx = (jd & 31).astype(F32)
        ang = pos * jnp.exp(-(idx * (2.0 / DIL_HD)) * log_theta)
        cos, sin = jnp.cos(ang), jnp.sin(ang)
        outs[0][:, 384:512] = cos
        outs[0][:, 512:640] = jnp.where(jd < 32, -sin, 0.0)
        outs[0][:, 640:768] = jnp.where(jd >= 32, sin, 0.0)

    return rowwise("rope_tables", fn, t, ins=[_row(positions)], outs=[(768, 768, F32, False)])[0]


def rmsnorm_fwd(x, g):
    def fn(ins, outs, _):
        h = rms(ins[0][...], ins[1][...], D_MODEL)
        outs[0][...] = h.astype(MXU_DTYPE)
        outs[1][...] = h.T.astype(MXU_DTYPE)

    return rowwise("rmsnorm_fwd", fn, x.shape[0], tm=512, ins=[_row(x), _const(g)], outs=[(D_MODEL, D_MODEL, MXU_DTYPE, False)],
                   touts=[(D_MODEL, D_MODEL, MXU_DTYPE, False)])


def rmsnorm_bwd(x, dh, dres, g):
    def fn(ins, outs, accs):
        _, vjp = jax.vjp(lambda xv, gv: rms(xv, gv, D_MODEL), ins[0][...], ins[3][...])
        dx, dg = vjp(ins[1][...])
        outs[0][...] = ins[2][...] + dx
        accs[0][...] += dg

    return rowwise("rmsnorm_bwd", fn, x.shape[0], tm=512, ins=[_row(x), _row(dh), _row(dres), _const(g)],
                   outs=[(D_MODEL, D_MODEL, F32, False)], accs=[(1, D_MODEL, D_MODEL, False)])


def loss_head(y, target):
    def fn(ins, outs, accs):
        err = ins[0][...] - ins[1][...]
        outs[0][...] = err * (1.0 / D_MODEL)
        accs[0][...] += jnp.sum(err * err, axis=0, keepdims=True)
        accs[1][...] = jnp.broadcast_to(jnp.sum(accs[0][...], keepdims=True), (1, LANES))

    dy, _, tot = rowwise("loss_head", fn, y.shape[0], tm=512, ins=[_row(y), _row(target)], outs=[(D_MODEL, D_MODEL, F32, False)],
                         accs=[(1, D_MODEL, D_MODEL, False), (1, LANES, LANES, False)])
    return dy, tot[0, 0] * (0.5 / D_MODEL)


def _shift_rows(v, d, fill, reverse):
    tb = v.shape[0]
    if d % 8 == 0:
        pad = jnp.full((d, v.shape[1]), fill, v.dtype)
        return jnp.concatenate([v[d:], pad] if reverse else [pad, v[:tb - d]], axis=0)
    rows = lax.broadcasted_iota(jnp.int32, v.shape, 0)
    if not reverse:
        return jnp.where(rows >= d, pltpu.roll(v, d, 0), fill)
    return jnp.where(rows < tb - d, pltpu.roll(v, tb - d, 0), fill)


def _scan_tile(a, b, reverse):
    d = 1
    while d < a.shape[0]:
        b = b + a * _shift_rows(b, d, 0.0, reverse)
        a = a * _shift_rows(a, d, 1.0, reverse)
        d *= 2
    return a, b


def _lru_gates(ops, xc, wgx, bgx, wga, bga, lam):
    gx = _sigmoid(ops.mm(xc, wgx) + bgx)
    ga = _sigmoid(ops.mm(xc, wga) + bga)
    log_a = -LRU_C * ga * _softplus(-lam)
    a = jnp.exp(log_a)
    mult = jnp.sqrt(-_expm1(2.0 * log_a))
    return a, mult * (gx * xc)


def _shifted_inputs(x, halo, tb):
    rows = lax.broadcasted_iota(jnp.int32, x.shape, 0)
    pad = jnp.zeros((tb - 8, LANES), F32)
    out = []
    for d in (3, 2, 1):
        head = jnp.concatenate([pltpu.roll(halo, d, 0), pad], axis=0)
        out.append(jnp.where(rows >= d, pltpu.roll(x, d, 0), head))
    return out + [x]


def _lru_specs(nt, tb, reverse):
    hb = tb // 8
    tt = (lambda t: nt - 1 - t) if reverse else (lambda t: t)
    blk = lambda off: pl.BlockSpec((tb, LANES), lambda n, t: (tt(t), n + off))
    halo = lambda off: pl.BlockSpec((8, LANES), lambda n, t: (jnp.maximum(tt(t) * hb - 1, 0), n + off))
    chan = lambda r: pl.BlockSpec((r, LANES), lambda n, t: (0, n))
    wblk = pl.BlockSpec((None, LANES, LANES), lambda n, t: (n, 0, 0))
    bblk = pl.BlockSpec((None, 1, LANES), lambda n, t: (n, 0, 0))
    return blk, halo, chan, wblk, bblk


def lru_fwd(z, conv_w, conv_b, wgx, bgx, wga, bga, lam, *, tb=256, comm=None):
    t = z.shape[0]
    nt = t // tb
    blk, halo, chan, wblk, bblk = _lru_specs(nt, tb, False)

    def body(x_ref, xh_ref, g_ref, cw_ref, cb_ref, wgx_ref, bgx_ref, wga_ref, bga_ref, lam_ref, h_ref, y_ref, yt_ref, carry_ref):
        ti = pl.program_id(1)

        @pl.when(ti == 0)
        def _():
            carry_ref[...] = jnp.zeros((8, LANES), F32)

        x = x_ref[...]
        hal = jnp.where(ti > 0, xh_ref[...], 0.0)
        xs = _shifted_inputs(x, hal, tb)
        xc = cb_ref[...] + sum(xs[k] * cw_ref[k:k + 1, :] for k in range(4))
        a, b = _lru_gates(PLAIN, xc, wgx_ref[...], bgx_ref[...], wga_ref[...], bga_ref[...], lam_ref[...])
        acum, h0 = _scan_tile(a, b, False)
        h = h0 + acum * carry_ref[7:8, :]
        carry_ref[...] = h[tb - 8:tb, :]
        h_ref[...] = h
        y = h * _silu_and_grad(g_ref[...])[0]
        y_ref[...] = y.astype(MXU_DTYPE)
        yt_ref[...] = y.T.astype(MXU_DTYPE)

    grid = (LRU_BLOCKS, nt)
    body, x_in, x_in_specs, x_out_specs, x_out_shapes, sems = _host(body, 10, 3, 1, comm, grid)
    return pl.pallas_call(
        body,
        name="lru_fwd" if comm is None else "lru_fwd_comm",
        grid=grid,
        in_specs=[blk(0), halo(0), blk(LRU_BLOCKS), chan(4), chan(1), wblk, bblk, wblk, bblk, chan(1)] + x_in_specs,
        out_specs=[blk(0), blk(0), pl.BlockSpec((LANES, tb), lambda n, t_: (n, t_))] + x_out_specs,
        out_shape=[jax.ShapeDtypeStruct((t, 1024), F32), jax.ShapeDtypeStruct((t, 1024), MXU_DTYPE),
                   jax.ShapeDtypeStruct((1024, t), MXU_DTYPE)] + x_out_shapes,
        scratch_shapes=[pltpu.VMEM((8, LANES), F32)] + sems,
        compiler_params=_cparams(("arbitrary", "arbitrary") if comm is not None else ("parallel", "arbitrary")),
    )(z, z, z, conv_w, conv_b, wgx, bgx, wga, bga, lam, *x_in)


def lru_bwd(z, h, dy, conv_w, conv_b, wgx, bgx, wga, bga, lam, *, tb=256, comm=None):
    t = z.shape[0]
    nt = t // tb
    blk, halo, chan, wblk, bblk = _lru_specs(nt, tb, True)

    def body(x_ref, xh_ref, g_ref, h_ref, hh_ref, dy_ref, cw_ref, cb_ref, wgx_ref, bgx_ref, wga_ref, bga_ref, lam_ref,
             dzx_ref, dzg_ref, dwgx_ref, dbgx_ref, dwga_ref, dbga_ref, dlam_ref, dcw_ref, dcb_ref,
             gcar_ref, acar_ref, xcar_ref):
        ti = pl.program_id(1)
        has_earlier = ti < nt - 1

        @pl.when(ti == 0)
        def _():
            for ref in (dwgx_ref, dbgx_ref, dwga_ref, dbga_ref, dlam_ref, dcw_ref, dcb_ref, gcar_ref, acar_ref, xcar_ref):
                ref[...] = jnp.zeros(ref.shape, F32)

        rows = lax.broadcasted_iota(jnp.int32, (tb, LANES), 0)
        x = x_ref[...]
        hal = jnp.where(has_earlier, xh_ref[...], 0.0)
        xs = _shifted_inputs(x, hal, tb)
        xc = cb_ref[...] + sum(xs[k] * cw_ref[k:k + 1, :] for k in range(4))
        (a, _), vjp = jax.vjp(functools.partial(_lru_gates, DIFF), xc, wgx_ref[...], bgx_ref[...], wga_ref[...],
                              bga_ref[...], lam_ref[...])
        g, h, dyv = g_ref[...], h_ref[...], dy_ref[...]
        silu, dsilu = _silu_and_grad(g)
        dzg_ref[...] = (dyv * h * dsilu).astype(MXU_DTYPE)
        a_next = jnp.where(rows < tb - 1, pltpu.roll(a, tb - 1, 0), acar_ref[0:1, :])
        acum, g0 = _scan_tile(a_next, dyv * silu, True)
        gt = g0 + acum * gcar_ref[0:1, :]
        h_prev = jnp.where(rows >= 1, pltpu.roll(h, 1, 0), jnp.where(has_earlier, hh_ref[7:8, :], 0.0))
        dxc, dwgx, dbgx, dwga, dbga, dlam = vjp((gt * h_prev, gt))
        later = xcar_ref[...]
        gcar_ref[...] = gt[0:8, :]
        acar_ref[...] = a[0:8, :]
        xcar_ref[...] = dxc[0:8, :]
        dx = dxc * cw_ref[3:4, :]
        pad = jnp.zeros((tb - 8, LANES), F32)
        for d in (1, 2, 3):
            tail = jnp.concatenate([pad, pltpu.roll(later, 8 - d, 0)], axis=0)
            up = jnp.where(rows < tb - d, pltpu.roll(dxc, tb - d, 0), tail)
            dx = dx + up * cw_ref[3 - d:4 - d, :]
        dzx_ref[...] = dx.astype(MXU_DTYPE)
        for k in range(4):
            dcw_ref[k:k + 1, :] += jnp.sum(dxc * xs[k], axis=0, keepdims=True)
        dcb_ref[...] += jnp.sum(dxc, axis=0, keepdims=True)
        dwgx_ref[...] += dwgx
        dbgx_ref[...] += dbgx
        dwga_ref[...] += dwga
        dbga_ref[...] += dbga
        dlam_ref[...] += dlam

    sds = jax.ShapeDtypeStruct
    grid = (LRU_BLOCKS, nt)
    body, x_in, x_in_specs, x_out_specs, x_out_shapes, sems = _host(body, 13, 9, 3, comm, grid)
    return pl.pallas_call(
        body,
        name="lru_bwd" if comm is None else "lru_bwd_comm",
        grid=grid,
        in_specs=[blk(0), halo(0), blk(LRU_BLOCKS), blk(0), halo(0), blk(0), chan(4), chan(1), wblk, bblk, wblk, bblk, chan(1)]
        + x_in_specs,
        out_specs=[blk(0), blk(0), wblk, bblk, wblk, bblk, chan(1), chan(4), chan(1)] + x_out_specs,
        out_shape=[sds((t, 1024), MXU_DTYPE), sds((t, 1024), MXU_DTYPE), sds(wgx.shape, F32), sds(bgx.shape, F32), sds(wga.shape, F32),
                   sds(bga.shape, F32), sds((1, 1024), F32), sds((4, 1024), F32), sds((1, 1024), F32)] + x_out_shapes,
        scratch_shapes=[pltpu.VMEM((8, LANES), F32)] * 3 + sems,
        compiler_params=_cparams(("arbitrary", "arbitrary") if comm is not None else ("parallel", "arbitrary")),
    )(z, z, z, h, h, dy, conv_w, conv_b, wgx, bgx, wga, bga, lam, *x_in)


def _mla_prep_tile(ops, cq, ckv, krp, c, s1, s2, g_cq, g_ckv, wq, wk, wv, gq, gk):
    cqn = rms(cq, g_cq, 256)
    ckvn = rms(ckv, g_ckv, 128)
    v = ops.mm(ckvn, wv)
    qs, ks = [], []
    for hd in range(MLA_HEADS):
        q = rms(ops.mm(cqn, wq[hd]), gq, MLA_QK)
        k = rms(ops.mm(ckvn, wk[hd]) + krp, gk, MLA_QK)
        qs.append(rope(ops, q, c, s1, s2, 16) * (MLA_QK ** -0.5))
        ks.append(rope(ops, k, c, s1, s2, 16))
    return tuple(qs), tuple(ks), v


def _mla_prep_args(ins):
    z_cq, z_ckv, z_kr, tc, ts1, ts2, g_cq, g_ckv, wq, wk, wv, gq, gk = ins[:13]
    heads = lambda w: tuple(w[:, LANES * hd:LANES * (hd + 1)] for hd in range(MLA_HEADS))
    return (z_cq[...], z_ckv[...], z_kr[...], tc[...], ts1[...], ts2[...], g_cq[...], g_ckv[...], heads(wq), heads(wk),
            wv[...], gq[...], gk[...])


def _mla_prep_ins(z, tabs, w):
    return [_row(z, 256, 0), _row(z, 128, 2), _row(z, 128, 3), _row(tabs, 128, 0), _row(tabs, 128, 1), _row(tabs, 128, 2),
            _const(w["cq_norm_g"]), _const(w["ckv_norm_g"]), _const(w["wq"]), _const(w["wk"]), _const(w["wv"]),
            _const(w["gq"]), _const(w["gk"])]


def mla_prep_fwd(z, tabs, w):
    def fn(ins, outs, _):
        qs, ks, v = _mla_prep_tile(PLAIN, *_mla_prep_args(ins))
        for hd in range(MLA_HEADS):
            outs[0][:, LANES * hd:LANES * (hd + 1)] = qs[hd].astype(MXU_DTYPE)
            outs[1][:, LANES * hd:LANES * (hd + 1)] = ks[hd].astype(MXU_DTYPE)
        outs[2][...] = v.astype(MXU_DTYPE)
        outs[3][...] = v.T.astype(MXU_DTYPE)

    return rowwise("mla_prep_fwd", fn, z.shape[0], tm=512, ins=_mla_prep_ins(z, tabs, w),
                   outs=[(1024, 1024, MXU_DTYPE, False), (1024, 1024, MXU_DTYPE, False), (512, 512, MXU_DTYPE, False)],
                   touts=[(512, 512, MXU_DTYPE, False)])


def mla_prep_bwd(z, tabs, w, dq, dk, dv):
    def fn(ins, outs, accs):
        args = _mla_prep_args(ins)
        _, vjp = jax.vjp(functools.partial(_mla_prep_tile, DIFF), *args)
        heads = lambda ref: tuple(ref[:, LANES * hd:LANES * (hd + 1)] for hd in range(MLA_HEADS))
        dcq, dckv, dkr, _, _, _, dg_cq, dg_ckv, dwq, dwk, dwv, dgq, dgk = vjp((heads(ins[13]), heads(ins[14]), ins[15][...]))
        lane = lax.broadcasted_iota(jnp.int32, (1, LANES), 1)
        outs[0][:, 0:256] = dcq.astype(MXU_DTYPE)
        outs[0][:, 256:384] = dckv.astype(MXU_DTYPE)
        outs[0][:, 384:512] = jnp.where((lane >= MLA_NOPE) & (lane < MLA_QK), dkr, 0.0).astype(MXU_DTYPE)
        accs[0][...] += dg_cq
        accs[1][...] += dg_ckv
        for hd in range(MLA_HEADS):
            accs[2][:, LANES * hd:LANES * (hd + 1)] += dwq[hd]
            accs[3][:, LANES * hd:LANES * (hd + 1)] += dwk[hd]
        accs[4][...] += dwv
        accs[5][...] += dgq
        accs[6][...] += dgk

    return rowwise("mla_prep_bwd", fn, z.shape[0], tm=512, ins=_mla_prep_ins(z, tabs, w) + [_row(dq), _row(dk), _row(dv)],
                   outs=[(512, 512, MXU_DTYPE, False)],
                   accs=[(1, 256, 256, False), (1, 128, 128, False), (256, 1024, 1024, False), (128, 1024, 1024, False),
                         (128, 512, 512, False), (1, 128, 128, False), (1, 128, 128, False)])


def _head_masks():
    lane = lax.broadcasted_iota(jnp.int32, (1, LANES), 1)
    return (lane < DIL_HD, lane >= DIL_HD)


def _row_scalar(tile, mask):
    return jnp.max(jnp.where(mask, tile, -jnp.inf), axis=-1, keepdims=True)


def _causal_tiles(nq, by_key):
    pairs = [(i, j) for i in range(nq) for j in range(i + 1)]
    if by_key:
        pairs.sort(key=lambda ij: (ij[1], ij[0]))
    return (jnp.asarray([ij[0] for ij in pairs], jnp.int32), jnp.asarray([ij[1] for ij in pairs], jnp.int32))


def mla_attn_fwd(q, k, vt, z, *, tq=256, gather=None):
    t = q.shape[0]
    nq = t // tq
    it, jt = _causal_tiles(nq, False)
    hps, wq, wv = MLA_HPS, LANES * MLA_HPS, 64 * MLA_HPS
    ws, layer = gather if gather is not None else ([], 0)
    ng = len(ws)
    ngrp, nsteps = MLA_HEADS // hps, int(it.shape[0])

    def body(it_ref, jt_ref, q_ref, k_ref, vt_ref, g_ref, *rest):
        w_refs, rest = rest[:ng], rest[ng:]
        o_ref, y_ref, yt_ref, lse_ref = rest[:4]
        ga_refs, rest = rest[4:4 + ng], rest[4 + ng:]
        m_scr, l_scr, acc_scr = rest[:3]
        step = pl.program_id(1)
        i, j = it_ref[step], jt_ref[step]
        if ng:
            gather_start, gather_finish = _layer_gather(w_refs, ga_refs, rest[3], rest[4], layer)

            @pl.when((pl.program_id(0) == 0) & (step == 0))
            def _():
                gather_start()

        @pl.when(j == 0)
        def _():
            m_scr[...] = jnp.full(m_scr.shape, NEG, F32)
            l_scr[...] = jnp.zeros(l_scr.shape, F32)
            acc_scr[...] = jnp.zeros(acc_scr.shape, F32)

        def update(diagonal):
            heads = range(hps)
            lanes = [slice(LANES * hh, LANES * (hh + 1)) for hh in heads]
            rows = [slice(64 * hh, 64 * (hh + 1)) for hh in heads]
            sts = [_dot(k_ref[:, lanes[hh]], q_ref[:, lanes[hh]], NT) for hh in heads]
            m_prev = [m_scr[hh:hh + 1, :] for hh in heads]
            l_prev = [l_scr[hh:hh + 1, :] for hh in heads]
            acc_prev = [acc_scr[rows[hh], :] for hh in heads]
            m_new, l_new, acc_new = [], [], []
            for hh in heads:
                st = sts[hh]
                if diagonal:
                    key = lax.broadcasted_iota(jnp.int32, (tq, tq), 0)
                    qry = lax.broadcasted_iota(jnp.int32, (tq, tq), 1)
                    st = jnp.where(key <= qry, st, NEG)
                m_new.append(jnp.maximum(m_prev[hh], jnp.max(st, axis=0, keepdims=True)))
                alpha = jnp.exp(m_prev[hh] - m_new[hh])
                pt = jnp.exp(st - m_new[hh])
                l_new.append(alpha * l_prev[hh] + jnp.sum(pt, axis=0, keepdims=True))
                acc_new.append(alpha * acc_prev[hh] + _dot(vt_ref[rows[hh], :], pt, NN))
            for hh in heads:
                m_scr[hh:hh + 1, :] = m_new[hh]
                l_scr[hh:hh + 1, :] = l_new[hh]
                acc_scr[rows[hh], :] = acc_new[hh]

        @pl.when(j < i)
        def _():
            update(False)

        @pl.when(j == i)
        def _():
            update(True)
            lse_ref[...] = jnp.zeros(lse_ref.shape, F32)
            for hh in range(hps):
                rows = slice(64 * hh, 64 * (hh + 1))
                acc_scr[rows, :] = acc_scr[rows, :] / l_scr[hh:hh + 1, :]
                lse_ref[hh:hh + 1, :] = m_scr[hh:hh + 1, :] + jnp.log(l_scr[hh:hh + 1, :])
            o = acc_scr[...].T
            o_ref[...] = o
            y = o * _silu_and_grad(g_ref[...])[0]
            y_ref[...] = y.astype(MXU_DTYPE)
            yt_ref[...] = y.T.astype(MXU_DTYPE)

        if ng:
            @pl.when((pl.program_id(0) == ngrp - 1) & (step == nsteps - 1))
            def _():
                gather_finish()

    qo = lambda w, off=0: pl.BlockSpec((tq, w), lambda p, s, it_, jt_: (it_[s], p + off))
    sds = jax.ShapeDtypeStruct
    comm_scratch = [pltpu.SemaphoreType.DMA((6 * ng,)), pltpu.SemaphoreType.DMA((6 * ng,))] if ng else []
    return pl.pallas_call(
        body,
        name="mla_attn_fwd_gather" if ng else "mla_attn_fwd",
        grid_spec=pltpu.PrefetchScalarGridSpec(
            num_scalar_prefetch=2,
            grid=(ngrp, nsteps),
            in_specs=[qo(wq), pl.BlockSpec((tq, wq), lambda p, s, it_, jt_: (jt_[s], p)),
                      pl.BlockSpec((wv, tq), lambda p, s, it_, jt_: (p, jt_[s])), qo(wv, 512 // wv)] + [HBM_SPEC] * ng,
            out_specs=[qo(wv), qo(wv), pl.BlockSpec((wv, tq), lambda p, s, it_, jt_: (p, it_[s])),
                       pl.BlockSpec((None, 8, tq), lambda p, s, it_, jt_: (p, 0, it_[s]))] + [HBM_SPEC] * ng,
            scratch_shapes=[pltpu.VMEM((8, tq), F32), pltpu.VMEM((8, tq), F32), pltpu.VMEM((wv, tq), F32)] + comm_scratch,
        ),
        out_shape=[sds((t, 512), F32), sds((t, 512), MXU_DTYPE), sds((512, t), MXU_DTYPE), sds((ngrp, 8, t), F32)]
        + _gather_shapes(ws),
        compiler_params=_cparams(("arbitrary", "arbitrary") if ng else ("parallel", "arbitrary")),
    )(it, jt, q, k, vt, z, *ws)


def mla_attn_bwd(q, k, v, do, lse, dd, *, tq=256, chips=None):
    t = q.shape[0]
    nq = t // tq
    it, jt = _causal_tiles(nq, True)
    hps, wq, wv = MLA_HPS, LANES * MLA_HPS, 64 * MLA_HPS
    pcs, layer = chips if chips is not None else ([], 0)
    nc = len(pcs)
    ngrp, nsteps = MLA_HEADS // hps, int(it.shape[0])

    def body(it_ref, jt_ref, q_ref, k_ref, v_ref, do_ref, lse_ref, d_ref, *rest):
        pc_refs, rest = rest[:nc], rest[nc:]
        dq_ref, dk_ref, dv_ref = rest[:3]
        lb_refs, rest = rest[3:3 + nc], rest[3 + nc:]
        dk_scr, dv_scr = rest[:2]
        step = pl.program_id(1)
        i, j = it_ref[step], jt_ref[step]
        masks = _head_masks()
        if nc:
            chips_start, chips_finish = _chips_exchange(pc_refs, lb_refs, rest[2], rest[3], layer)

            @pl.when((pl.program_id(0) == 0) & (step == 0))
            def _():
                chips_start()

        @pl.when(step == 0)
        def _():
            dq_ref[...] = jnp.zeros(dq_ref.shape, F32)

        @pl.when(i == j)
        def _():
            dk_scr[...] = jnp.zeros(dk_scr.shape, F32)
            dv_scr[...] = jnp.zeros(dv_scr.shape, F32)

        def update(diagonal):
            qrows = pl.ds(pl.multiple_of(i * tq, tq), tq)
            heads = range(hps)
            lanes = [slice(LANES * hh, LANES * (hh + 1)) for hh in heads]
            pair = [slice(LANES * (hh // 2), LANES * (hh // 2 + 1)) for hh in heads]
            qh = [q_ref[:, lanes[hh]] for hh in heads]
            kh = [k_ref[:, lanes[hh]] for hh in heads]
            doh = []
            for hh in heads:
                dov = do_ref[:, pair[hh]]
                doh.append(jnp.where(masks[hh % 2], dov, jnp.zeros_like(dov)))
            sts = [_dot(kh[hh], qh[hh], NT) for hh in heads]
            dps = [_dot(v_ref[:, pair[hh]], doh[hh], NT) for hh in heads]
            lse = [lse_ref[hh:hh + 1, :] for hh in heads]
            ddv = [d_ref[hh:hh + 1, :] for hh in heads]
            dk_new = [dk_scr[:, lanes[hh]] for hh in heads]
            dq_new = [dq_ref[qrows, lanes[hh]] for hh in heads]
            dv_new = [dv_scr[:, pair[2 * pp]] for pp in range(hps // 2)]
            for hh in heads:
                st = sts[hh] - lse[hh]
                if diagonal:
                    key = lax.broadcasted_iota(jnp.int32, (tq, tq), 0)
                    qry = lax.broadcasted_iota(jnp.int32, (tq, tq), 1)
                    st = jnp.where(key <= qry, st, NEG)
                pt = jnp.exp(st)
                dst = pt * (dps[hh] - ddv[hh])
                dv_new[hh // 2] = dv_new[hh // 2] + _dot(pt, doh[hh], NN)
                dk_new[hh] = dk_new[hh] + _dot(dst, qh[hh], NN)
                dq_new[hh] = dq_new[hh] + _dot(dst, kh[hh], TN)
            for hh in heads:
                dk_scr[:, lanes[hh]] = dk_new[hh]
                dq_ref[qrows, lanes[hh]] = dq_new[hh]
            for pp in range(hps // 2):
                dv_scr[:, pair[2 * pp]] = dv_new[pp]

        @pl.when(j < i)
        def _():
            update(False)

        @pl.when(j == i)
        def _():
            update(True)

        @pl.when(i == nq - 1)
        def _():
            dk_ref[...] = dk_scr[...]
            dv_ref[...] = dv_scr[...]

        if nc:
            @pl.when((pl.program_id(0) == ngrp - 1) & (step == nsteps - 1))
            def _():
                chips_finish()

    qo = lambda w: pl.BlockSpec((tq, w), lambda p, s, it_, jt_: (it_[s], p))
    kv = lambda w: pl.BlockSpec((tq, w), lambda p, s, it_, jt_: (jt_[s], p))
    stat = pl.BlockSpec((None, 8, tq), lambda p, s, it_, jt_: (p, 0, it_[s]))
    sds = jax.ShapeDtypeStruct
    comm_scratch = [pltpu.SemaphoreType.DMA((3 * nc,)), pltpu.SemaphoreType.DMA((3 * nc,))] if nc else []
    return pl.pallas_call(
        body,
        name="mla_attn_bwd_chips" if nc else "mla_attn_bwd",
        grid_spec=pltpu.PrefetchScalarGridSpec(
            num_scalar_prefetch=2,
            grid=(ngrp, nsteps),
            in_specs=[qo(wq), kv(wq), kv(wv), qo(wv), stat, stat] + [HBM_SPEC] * nc,
            out_specs=[pl.BlockSpec((t, wq), lambda p, s, it_, jt_: (0, p)), kv(wq), kv(wv)] + [HBM_SPEC] * nc,
            scratch_shapes=[pltpu.VMEM((tq, wq), F32), pltpu.VMEM((tq, wv), F32)] + comm_scratch,
        ),
        out_shape=[sds((t, 1024), F32), sds((t, 1024), F32), sds((t, 512), F32)] + _chips_shapes(pcs),
        compiler_params=_cparams(("arbitrary", "arbitrary") if nc else ("parallel", "arbitrary")),
    )(it, jt, q, k, v, do, lse, dd, *pcs)


def mla_gate_bwd(dy, o, z, *, tm=512):
    t = dy.shape[0]
    wv = 64 * MLA_HPS

    def body(dy_ref, o_ref, g_ref, do_ref, dzg_ref, dd_ref):
        dyv, ov = dy_ref[...], o_ref[...]
        silu, dsilu = _silu_and_grad(g_ref[...])
        do = dyv * silu
        do_ref[...] = do.astype(MXU_DTYPE)
        dzg_ref[...] = (dyv * ov * dsilu).astype(MXU_DTYPE)
        prod = do * ov
        row = lax.broadcasted_iota(jnp.int32, (8, wv), 0)
        lane = lax.broadcasted_iota(jnp.int32, (8, wv), 1)
        pick = ((lane >= row * DIL_HD) & (lane < (row + 1) * DIL_HD)).astype(BF16)
        hi = prod.astype(BF16)
        r1 = prod - hi.astype(F32)
        mid = r1.astype(BF16)
        lo = (r1 - mid.astype(F32)).astype(BF16)
        dot = lambda u: lax.dot_general(pick, u, NT, preferred_element_type=F32)
        dd_ref[...] = dot(hi) + dot(mid) + dot(lo)

    blk = lambda off=0: pl.BlockSpec((tm, wv), lambda p, i: (i, p + off))
    sds = jax.ShapeDtypeStruct
    return pl.pallas_call(
        body,
        name="mla_gate_bwd",
        grid=(MLA_HEADS // MLA_HPS, t // tm),
        in_specs=[blk(), blk(), blk(512 // wv)],
        out_specs=[blk(), blk(), pl.BlockSpec((None, 8, tm), lambda p, i: (p, 0, i))],
        out_shape=[sds((t, 512), MXU_DTYPE), sds((t, 512), MXU_DTYPE), sds((MLA_HEADS // MLA_HPS, 8, t), F32)],
        compiler_params=_cparams(("parallel", "parallel")),
    )(dy, o, z)


SPAN = 2048
DIL_SCALE = DIL_HD ** -0.5


def _rm_src(u, window, dil):
    nn, r = divmod(u, dil)
    return pl.ds(nn * window + r, DIL_NK, stride=dil) if dil > 1 else pl.ds(u * DIL_NK, DIL_NK)


def _rm_dst(u):
    return pl.ds(u * DIL_NK, DIL_NK)


def _dil_prep_tile(ops, x, g2, c, s1, s2, scale):
    ms = ops.seg_sum(x * x) * (1.0 / DIL_HD)
    return rope(ops, x * lax.rsqrt(ms + EPS) * g2, c, s1, s2, 32) * scale


def _span_blk(base):
    return pl.BlockSpec((SPAN, LANES), lambda s, p: (s, base + p))


def _const_blk(shape):
    return pl.BlockSpec(shape, lambda s, p: (0,) * len(shape))


_DIL_TABLE_SPECS = [pl.BlockSpec((SPAN, LANES), (lambda s, p, blk=blk: (s, blk))) for blk in (3, 4, 5)]


def dil_prep_fwd(z, tabs, gq2, gk2, gi):
    window, dil = DIL_GROUPS[gi]
    t = z.shape[0]

    def body(q_ref, k_ref, v_ref, c_ref, s1_ref, s2_ref, gq_ref, gk_ref, qo_ref, ko_ref, vo_ref):
        for u in range(SPAN // DIL_NK):
            src, dst = _rm_src(u, window, dil), _rm_dst(u)
            c, s1, s2 = c_ref[src, :], s1_ref[src, :], s2_ref[src, :]
            qo_ref[dst, :] = _dil_prep_tile(PLAIN, q_ref[src, :], gq_ref[...], c, s1, s2, DIL_SCALE).astype(MXU_DTYPE)
            ko_ref[dst, :] = _dil_prep_tile(PLAIN, k_ref[src, :], gk_ref[...], c, s1, s2, 1.0).astype(MXU_DTYPE)
            vo_ref[dst, :] = v_ref[src, :].astype(MXU_DTYPE)

    return pl.pallas_call(
        body,
        name=f"dil_prep_fwd_{dil}",
        grid=(t // SPAN, 4),
        in_specs=[_span_blk(4 * gi), _span_blk(12 + 4 * gi), _span_blk(24 + 4 * gi)] + _DIL_TABLE_SPECS
        + [_const_blk((1, LANES)), _const_blk((1, LANES))],
        out_specs=[_span_blk(0)] * 3,
        out_shape=[jax.ShapeDtypeStruct((t, 512), MXU_DTYPE)] * 3,
        compiler_params=_cparams(("parallel", "parallel")),
    )(z, z, z, tabs, tabs, tabs, gq2, gk2)


def dil_prep_bwd(z, tabs, gq2, gk2, dq, dk, dv, gi):
    window, dil = DIL_GROUPS[gi]
    t = z.shape[0]

    def body(q_ref, k_ref, c_ref, s1_ref, s2_ref, gq_ref, gk_ref, dq_ref, dk_ref, dv_ref, dzq_ref, dzk_ref, dzv_ref, dgq_ref,
             dgk_ref, sq, sk, sv):
        @pl.when((pl.program_id(0) == 0) & (pl.program_id(1) == 0))
        def _():
            dgq_ref[...] = jnp.zeros((1, LANES), F32)
            dgk_ref[...] = jnp.zeros((1, LANES), F32)

        dgs = [jnp.zeros((1, LANES), F32), jnp.zeros((1, LANES), F32)]
        for u in range(SPAN // DIL_NK):
            src, dst = _rm_src(u, window, dil), _rm_dst(u)
            c, s1, s2 = c_ref[src, :], s1_ref[src, :], s2_ref[src, :]
            for idx, (x_ref, g_ref, ct_ref, scr, scale) in enumerate(((q_ref, gq_ref, dq_ref, sq, DIL_SCALE),
                                                                      (k_ref, gk_ref, dk_ref, sk, 1.0))):
                _, vjp = jax.vjp(lambda xv, gv, sc=scale: _dil_prep_tile(DIFF, xv, gv, c, s1, s2, sc), x_ref[src, :], g_ref[...])
                dx, dg = vjp(ct_ref[dst, :])
                scr[src, :] = dx
                dgs[idx] = dgs[idx] + dg
            sv[src, :] = dv_ref[dst, :]
        dgq_ref[...] += dgs[0] + pltpu.roll(dgs[0], DIL_HD, 1)
        dgk_ref[...] += dgs[1] + pltpu.roll(dgs[1], DIL_HD, 1)
        for c0 in range(0, SPAN, 256):
            rows = slice(c0, c0 + 256)
            dzq_ref[rows, :] = sq[rows, :].astype(MXU_DTYPE)
            dzk_ref[rows, :] = sk[rows, :].astype(MXU_DTYPE)
            dzv_ref[rows, :] = sv[rows, :].astype(MXU_DTYPE)

    sds = jax.ShapeDtypeStruct
    return pl.pallas_call(
        body,
        name=f"dil_prep_bwd_{dil}",
        grid=(t // SPAN, 4),
        in_specs=[_span_blk(4 * gi), _span_blk(12 + 4 * gi)] + _DIL_TABLE_SPECS
        + [_const_blk((1, LANES)), _const_blk((1, LANES)), _span_blk(0), _span_blk(0), _span_blk(0)],
        out_specs=[_span_blk(0)] * 3 + [_const_blk((1, LANES))] * 2,
        out_shape=[sds((t, 512), MXU_DTYPE)] * 3 + [sds((1, LANES), F32)] * 2,
        scratch_shapes=[pltpu.VMEM((SPAN, LANES), F32)] * 3,
        compiler_params=_cparams(("arbitrary", "arbitrary")),
    )(z, z, tabs, tabs, tabs, gq2, gk2, dq, dk, dv)


def _band_masks():
    qi = lax.broadcasted_iota(jnp.int32, (DIL_NK, DIL_NK), 0)
    ki = lax.broadcasted_iota(jnp.int32, (DIL_NK, DIL_NK), 1)
    return (ki >= qi), (ki <= qi)


def _pair_heads():
    return [(pair, hh) for pair in range(4) for hh in range(2)]


def _pair_lanes(pair):
    return slice(LANES * pair, LANES * (pair + 1))


def _zero_other_head(mask, x):
    return jnp.where(mask, x, jnp.zeros_like(x))


def dil_attn_fwd(name, q, k, v, dil):
    t = q.shape[0]

    def body(q_ref, kp_ref, kc_ref, vp_ref, vc_ref, o_ref, lse_ref):
        b = pl.program_id(0)
        mprev, mcur = _band_masks()
        mprev = mprev & (b >= dil)
        hm = _head_masks()
        heads = _pair_heads()
        qh = [_zero_other_head(hm[hh], q_ref[:, _pair_lanes(pair)]) for pair, hh in heads]
        sps = [_dot(qh[i], kp_ref[:, _pair_lanes(pair)], NT) for i, (pair, _) in enumerate(heads)]
        scs = [_dot(qh[i], kc_ref[:, _pair_lanes(pair)], NT) for i, (pair, _) in enumerate(heads)]
        o = [jnp.zeros((DIL_NK, LANES), F32) for _ in range(4)]
        lse = [jnp.zeros((DIL_NK, LANES), F32) for _ in range(4)]
        for i, (pair, hh) in enumerate(heads):
            sp, sc = jnp.where(mprev, sps[i], NEG), jnp.where(mcur, scs[i], NEG)
            m = jnp.maximum(jnp.max(sp, axis=1, keepdims=True), jnp.max(sc, axis=1, keepdims=True))
            ep, ec = jnp.exp(sp - m), jnp.exp(sc - m)
            den = jnp.sum(ep, axis=1, keepdims=True) + jnp.sum(ec, axis=1, keepdims=True)
            lanes = _pair_lanes(pair)
            oh = _dot(ep, _zero_other_head(hm[hh], vp_ref[:, lanes]), NN) + _dot(ec, _zero_other_head(hm[hh], vc_ref[:, lanes]), NN)
            o[pair] = o[pair] + oh / den
            lse[pair] = jnp.where(hm[hh], m + jnp.log(den), lse[pair])
        for pair in range(4):
            o_ref[:, _pair_lanes(pair)] = o[pair]
            lse_ref[:, _pair_lanes(pair)] = lse[pair]

    cur = pl.BlockSpec((DIL_NK, 512), lambda b: (b, 0))
    prev = pl.BlockSpec((DIL_NK, 512), lambda b: (jnp.maximum(b - dil, 0), 0))
    sds = jax.ShapeDtypeStruct
    return pl.pallas_call(
        body,
        name=name,
        grid=(t // DIL_NK,),
        in_specs=[cur, prev, cur, prev, cur],
        out_specs=[cur, cur],
        out_shape=[sds((t, 512), F32), sds((t, 512), F32)],
        compiler_params=_cparams(("parallel",)),
    )(q, k, k, v, v)


def dil_attn_bwd(name, q, k, v, do, lse, dd, dil):
    t = q.shape[0]
    nblk = t // DIL_NK

    def body(q_ref, do_ref, l_ref, d_ref, kp_ref, kc_ref, vp_ref, vc_ref, dq_ref, dk_ref, dv_ref):
        b = pl.program_id(0)

        @pl.when(b == 0)
        def _():
            dk_ref[...] = jnp.zeros(dk_ref.shape, F32)
            dv_ref[...] = jnp.zeros(dv_ref.shape, F32)

        mprev, mcur = _band_masks()
        mprev = mprev & (b >= dil)
        rows_c = pl.ds(pl.multiple_of(b * DIL_NK, DIL_NK), DIL_NK)
        rows_p = pl.ds(pl.multiple_of(jnp.maximum(b - dil, 0) * DIL_NK, DIL_NK), DIL_NK)
        hm = _head_masks()
        heads = _pair_heads()
        ln_ = [_pair_lanes(pair) for pair, _ in heads]
        qh = [_zero_other_head(hm[hh], q_ref[:, ln_[i]]) for i, (_, hh) in enumerate(heads)]
        doh = [_zero_other_head(hm[hh], do_ref[:, ln_[i]]) for i, (_, hh) in enumerate(heads)]
        idx = range(len(heads))
        s_p = [_dot(qh[i], kp_ref[:, ln_[i]], NT) for i in idx]
        s_c = [_dot(qh[i], kc_ref[:, ln_[i]], NT) for i in idx]
        dp_p = [_dot(doh[i], vp_ref[:, ln_[i]], NT) for i in idx]
        dp_c = [_dot(doh[i], vc_ref[:, ln_[i]], NT) for i in idx]
        zero = lambda: [jnp.zeros((DIL_NK, LANES), F32) for _ in range(4)]
        dq, dk_p, dk_c, dv_p, dv_c = zero(), zero(), zero(), zero(), zero()
        for i, (pair, hh) in enumerate(heads):
            lse_h, d_h = _row_scalar(l_ref[:, ln_[i]], hm[hh]), _row_scalar(d_ref[:, ln_[i]], hm[hh])
            pp = jnp.exp(jnp.where(mprev, s_p[i] - lse_h, NEG))
            pc = jnp.exp(jnp.where(mcur, s_c[i] - lse_h, NEG))
            dsp, dsc = pp * (dp_p[i] - d_h), pc * (dp_c[i] - d_h)
            dq[pair] = (dq[pair] + _dot(dsp, _zero_other_head(hm[hh], kp_ref[:, ln_[i]]), NN)
                        + _dot(dsc, _zero_other_head(hm[hh], kc_ref[:, ln_[i]]), NN))
            dv_p[pair] = dv_p[pair] + _dot(pp, doh[i], TN)
            dv_c[pair] = dv_c[pair] + _dot(pc, doh[i], TN)
            dk_p[pair] = dk_p[pair] + _dot(dsp, qh[i], TN)
            dk_c[pair] = dk_c[pair] + _dot(dsc, qh[i], TN)
        for pair in range(4):
            lanes = _pair_lanes(pair)
            dq_ref[:, lanes] = dq[pair]
            dk_ref[rows_p, lanes] += dk_p[pair]
            dv_ref[rows_p, lanes] += dv_p[pair]
            dk_ref[rows_c, lanes] += dk_c[pair]
            dv_ref[rows_c, lanes] += dv_c[pair]

    cur = pl.BlockSpec((DIL_NK, 512), lambda b: (b, 0))
    prev = pl.BlockSpec((DIL_NK, 512), lambda b: (jnp.maximum(b - dil, 0), 0))
    whole = pl.BlockSpec((t, 512), lambda b: (0, 0))
    sds = jax.ShapeDtypeStruct
    return pl.pallas_call(
        body,
        name=name,
        grid=(nblk,),
        in_specs=[cur, cur, cur, cur, prev, cur, prev, cur],
        out_specs=[cur, whole, whole],
        out_shape=[sds((t, 512), F32)] * 3,
        compiler_params=_cparams(("arbitrary",)),
    )(q, do, lse, dd, k, k, v, v)


def dil_combine(os_, lses, z):
    t = z.shape[0]

    def body(o1_ref, l1_ref, o2_ref, l2_ref, o3_ref, l3_ref, g_ref, y_ref, o_ref, lse_ref, yt_ref, so2, sl2, so3, sl3):
        for (window, dil), o_in, l_in, so, sl in ((DIL_GROUPS[1], o2_ref, l2_ref, so2, sl2), (DIL_GROUPS[2], o3_ref, l3_ref, so3, sl3)):
            for u in range(SPAN // DIL_NK):
                src, dst = _rm_src(u, window, dil), _rm_dst(u)
                so[src, :] = o_in[dst, :]
                sl[src, :] = l_in[dst, :]
        for c0 in range(0, SPAN, 256):
            rows = slice(c0, c0 + 256)
            la, lb, lc = l1_ref[rows, :], sl2[rows, :], sl3[rows, :]
            mx = jnp.maximum(jnp.maximum(la, lb), lc)
            wa, wb, wc = jnp.exp(la - mx), jnp.exp(lb - mx), jnp.exp(lc - mx)
            tot = wa + wb + wc
            o = (wa * o1_ref[rows, :] + wb * so2[rows, :] + wc * so3[rows, :]) / tot
            y = o * _silu_and_grad(g_ref[rows, :])[0]
            y_ref[rows, :] = y.astype(MXU_DTYPE)
            o_ref[rows, :] = o
            lse_ref[rows, :] = mx + jnp.log(tot)
            yt_ref[:, rows] = y.T.astype(MXU_DTYPE)

    sds = jax.ShapeDtypeStruct
    return pl.pallas_call(
        body,
        name="dil_combine",
        grid=(t // SPAN, 4),
        in_specs=[_span_blk(0)] * 6 + [_span_blk(36)],
        out_specs=[_span_blk(0)] * 3 + [pl.BlockSpec((LANES, SPAN), lambda s, p: (p, s))],
        out_shape=[sds((t, 512), MXU_DTYPE), sds((t, 512), F32), sds((t, 512), F32), sds((512, t), MXU_DTYPE)],
        scratch_shapes=[pltpu.VMEM((SPAN, LANES), F32)] * 4,
        compiler_params=_cparams(("parallel", "parallel")),
    )(os_[0], lses[0], os_[1], lses[1], os_[2], lses[2], z)


def dil_gate_bwd(dy, o, z, lse):
    t = dy.shape[0]

    def body(dy_ref, o_ref, g_ref, lse_ref, dzg_ref, do1_ref, dd1_ref, do2_ref, dd2_ref, l2_ref, do3_ref, dd3_ref, l3_ref, do_scr):
        for c0 in range(0, SPAN, 256):
            rows = slice(c0, c0 + 256)
            dyv, ov = dy_ref[rows, :], o_ref[rows, :]
            silu, dsilu = _silu_and_grad(g_ref[rows, :])
            do = dyv * silu
            do_scr[rows, :] = do
            do1_ref[rows, :] = do.astype(MXU_DTYPE)
            dd1_ref[rows, :] = _seg_sum_impl(do * ov)
            dzg_ref[rows, :] = (dyv * ov * dsilu).astype(MXU_DTYPE)
        for (window, dil), do_o, dd_o, l_o in ((DIL_GROUPS[1], do2_ref, dd2_ref, l2_ref), (DIL_GROUPS[2], do3_ref, dd3_ref, l3_ref)):
            for u in range(SPAN // DIL_NK):
                src, dst = _rm_src(u, window, dil), _rm_dst(u)
                do_o[dst, :] = do_scr[src, :].astype(MXU_DTYPE)
                dd_o[dst, :] = dd1_ref[src, :]
                l_o[dst, :] = lse_ref[src, :]

    sds = jax.ShapeDtypeStruct
    f32, mxu = sds((t, 512), F32), sds((t, 512), MXU_DTYPE)
    return pl.pallas_call(
        body,
        name="dil_gate_bwd",
        grid=(t // SPAN, 4),
        in_specs=[_span_blk(0), _span_blk(0), _span_blk(36), _span_blk(0)],
        out_specs=[_span_blk(0)] * 9,
        out_shape=[mxu, mxu, f32, mxu, f32, f32, mxu, f32, f32],
        scratch_shapes=[pltpu.VMEM((SPAN, LANES), F32)],
        compiler_params=_cparams(("parallel", "parallel")),
    )(dy, o, z, lse)


def _merge_tile(p0, p1, p2, z0, z1, z2, b0, b1, b2):
    return _sigmoid(z0 + b0) * p0 + _sigmoid(z1 + b1) * p1 + _sigmoid(z2 + b2) * p2


def _merge_ins(ps, z, b):
    w = 256
    return ([_row(p, w, 0, True) for p in ps] + [_row(z, w, 4 * i, True) for i in range(3)]
            + [_const(b, w, 4 * i, True) for i in range(3)])


def merge_fwd(ps, z, b):
    def fn(ins, outs, _):
        merged = _merge_tile(*[r[...] for r in ins])
        outs[0][...] = merged.astype(MXU_DTYPE)
        outs[1][...] = merged.T.astype(MXU_DTYPE)

    return rowwise("merge_fwd", fn, z.shape[0], tm=1024, ncol=4, ins=_merge_ins(ps, z, b), outs=[(1024, 256, MXU_DTYPE, True)],
                   touts=[(1024, 256, MXU_DTYPE, True)])


def merge_bwd(dm, ps, z, b):
    def fn(ins, outs, accs):
        _, vjp = jax.vjp(_merge_tile, *[r[...] for r in ins[:9]])
        grads = vjp(ins[9][...])
        for i in range(3):
            outs[i][...] = grads[i].astype(MXU_DTYPE)
            outs[3 + i][...] = grads[3 + i].astype(MXU_DTYPE)
            accs[i][...] += grads[6 + i]

    return rowwise("merge_bwd", fn, z.shape[0], tm=1024, ncol=4, ins=_merge_ins(ps, z, b) + [_row(dm, 256, 0, True)],
                   outs=[(1024, 256, MXU_DTYPE, True)] * 6, accs=[(1, 1024, 256, True)] * 3)


EW_BLOCK_BYTES = 2**21


def _tile2d(r, c):
    if r * c * 4 <= EW_BLOCK_BYTES:
        return r, c
    for tr in (512, 256, 128, 64, 32):
        if r % tr == 0 and tr * c * 4 <= EW_BLOCK_BYTES:
            return tr, c
    for tc in (1024, 512, 256, 128):
        if c % tc == 0 and r * tc * 4 <= EW_BLOCK_BYTES:
            return r, tc
    raise ValueError((r, c))


def adamw(name, w, g, m, v):
    shape = w.shape
    c = shape[-1]
    r = w.size // c
    tr, tc = _tile2d(r, c)
    assert tc == c
    c1, c2 = 1.0 - ADAM_B1 ** ADAM_STEP, 1.0 - ADAM_B2 ** ADAM_STEP

    def body(w_ref, g_ref, m_ref, v_ref, d_ref, mo_ref, vo_ref):
        gv = g_ref[...]
        mn = ADAM_B1 * m_ref[...] + (1.0 - ADAM_B1) * gv
        vn = ADAM_B2 * v_ref[...] + (1.0 - ADAM_B2) * (gv * gv)
        d_ref[...] = -ADAM_LR * ((mn / c1) / (jnp.sqrt(vn / c2) + ADAM_EPS) + ADAM_WD * w_ref[...])
        mo_ref[...] = mn
        vo_ref[...] = vn

    spec = pl.BlockSpec((tr, c), lambda i: (i, 0))
    outs = pl.pallas_call(
        body,
        name=name,
        grid=(r // tr,),
        in_specs=[spec] * 4,
        out_specs=[spec] * 3,
        out_shape=[jax.ShapeDtypeStruct((r, c), F32)] * 3,
        compiler_params=_cparams(("parallel",)),
    )(*[a.reshape(r, c) for a in (w, g, m, v)])
    return [o.reshape(shape) for o in outs]


def adamw_pair(name, w, reduced, received, m, v, c_idx):
    _, r, c = w.shape
    tr, tc = _tile2d(r, c)
    c1, c2 = 1.0 - ADAM_B1 ** ADAM_STEP, 1.0 - ADAM_B2 ** ADAM_STEP

    def body(c_ref, w_ref, r0_ref, x0_ref, r1_ref, x1_ref, m_ref, v_ref, g_ref, d_ref, mo_ref, vo_ref):
        mine = c_ref[0]
        gv = jnp.where(pl.program_id(0) == 0, jnp.where(mine == 0, r0_ref[...], x0_ref[...]),
                       jnp.where(mine == 1, r1_ref[...], x1_ref[...]))
        mn = ADAM_B1 * m_ref[...] + (1.0 - ADAM_B1) * gv
        vn = ADAM_B2 * v_ref[...] + (1.0 - ADAM_B2) * (gv * gv)
        g_ref[...] = gv
        d_ref[...] = -ADAM_LR * ((mn / c1) / (jnp.sqrt(vn / c2) + ADAM_EPS) + ADAM_WD * w_ref[...])
        mo_ref[...] = mn
        vo_ref[...] = vn

    full = pl.BlockSpec((None, tr, tc), lambda h, i, j, cr: (h, i, j))
    half = pl.BlockSpec((tr, tc), lambda h, i, j, cr: (i, j))
    return pl.pallas_call(
        body,
        name=name,
        grid_spec=pltpu.PrefetchScalarGridSpec(
            num_scalar_prefetch=1,
            grid=(2, r // tr, c // tc),
            in_specs=[full, half, half, half, half, full, full],
            out_specs=[full] * 4,
        ),
        out_shape=[jax.ShapeDtypeStruct((2, r, c), F32)] * 4,
        compiler_params=_cparams(("parallel", "parallel", "parallel")),
    )(c_idx, w, reduced[0], received[0], reduced[1], received[1], m, v)


def sum_pair(name, g, la, out_dtype):
    _, r, c = g.shape
    tr, tc = _tile2d(r, c)

    def body(g_ref, la_ref, o_ref):
        o_ref[...] = (g_ref[...] + la_ref[...]).astype(o_ref.dtype)

    spec = pl.BlockSpec((None, tr, tc), lambda s, i, j: (s, i, j))
    return pl.pallas_call(
        body,
        name=name,
        grid=(4, r // tr, c // tc),
        in_specs=[spec, spec],
        out_specs=spec,
        out_shape=jax.ShapeDtypeStruct((4, r, c), out_dtype),
        compiler_params=_cparams(("parallel", "parallel", "parallel")),
    )(g, la)


def sum_chips(name, g, la, lb, s_idx):
    _, r, c = g.shape
    tr, tc = _tile2d(r, c)

    def body(s_ref, g_ref, la_ref, l0_ref, l1_ref, l2_ref, o_ref):
        own = g_ref[...] + la_ref[...]
        o_ref[...] = ((own + l0_ref[...].astype(F32)) + l1_ref[...].astype(F32)) + l2_ref[...].astype(F32)

    own_spec = pl.BlockSpec((None, tr, tc), lambda i, j, sr: (sr[0], i, j))
    lspec = lambda k: pl.BlockSpec((None, tr, tc), lambda i, j, sr: (k, i, j))
    return pl.pallas_call(
        body,
        name=name,
        grid_spec=pltpu.PrefetchScalarGridSpec(
            num_scalar_prefetch=1,
            grid=(r // tr, c // tc),
            in_specs=[own_spec, own_spec, lspec(0), lspec(1), lspec(2)],
            out_specs=pl.BlockSpec((tr, tc), lambda i, j, sr: (i, j)),
        ),
        out_shape=jax.ShapeDtypeStruct((r, c), F32),
        compiler_params=_cparams(("parallel", "parallel")),
    )(s_idx, g, la, lb, lb, lb)


def _place():
    x, y, c = lax.axis_index("x"), lax.axis_index("y"), lax.axis_index("c")
    chips = [(1 - x, y), (x, 1 - y), (1 - x, 1 - y)]
    return x, y, c, chips


HBM_SPEC = pl.BlockSpec(memory_space=pltpu.HBM)


def _comm_call(name, body, ins, out_shapes, n_sem, n_local):
    return pl.pallas_call(
        body,
        name=name,
        in_specs=[HBM_SPEC] * len(ins),
        out_specs=[HBM_SPEC] * len(out_shapes),
        out_shape=out_shapes,
        scratch_shapes=[pltpu.SemaphoreType.DMA((n_sem,)), pltpu.SemaphoreType.DMA((n_sem,)),
                        pltpu.SemaphoreType.DMA((max(n_local, 1),))],
    )(*ins)


def _layer_gather(ins, outs, send, recv, layer):
    n = len(ins)
    x, y, c, chips = _place()
    s = 2 * x + y
    sib = (x, y, 1 - c)
    active = c == layer

    def rc(a, k, src, dst, dev):
        return pltpu.make_async_remote_copy(src_ref=src, dst_ref=dst, send_sem=send.at[6 * a + k], recv_sem=recv.at[6 * a + k],
                                            device_id=dev, device_id_type=MESH)

    def first_hop():
        return [rc(a, j, ins[a], outs[a].at[s], (*chip, c)) for j, chip in enumerate(chips) for a in range(n)]

    def start():
        @pl.when(active)
        def _():
            for cp in first_hop():
                cp.start()

    def finish():
        @pl.when(active)
        def _():
            forwards = []
            for j, (cx, cy) in enumerate(chips):
                for a in range(n):
                    landed = outs[a].at[2 * cx + cy]
                    rc(a, j, landed, landed, sib).wait_recv()
                    forwards.append(rc(a, 3 + j, landed, landed, sib))
                    forwards[-1].start()
            for cp in first_hop() + forwards:
                cp.wait_send()

        @pl.when(jnp.logical_not(active))
        def _():
            for j, (cx, cy) in enumerate(chips):
                for a in range(n):
                    other = outs[a].at[2 * cx + cy]
                    rc(a, 3 + j, other, other, sib).wait_recv()

    return start, finish


def _gather_shapes(ws):
    return [jax.ShapeDtypeStruct((4,) + w.shape, w.dtype) for w in ws]


def gather_hook(ws, layer):
    return (ws, _gather_shapes(ws), 6 * len(ws), lambda i, o, s, r: _layer_gather(i, o, s, r, layer))


def allgather_layer(ws, layer):
    n = len(ws)

    def body(*refs):
        send, recv, _ = refs[2 * n:]
        start, finish = _layer_gather(refs[:n], refs[n:2 * n], send, recv, layer)
        start()
        finish()

    return _comm_call(f"allgather_layer{layer}", body, ws, _gather_shapes(ws), 6 * n, 0)


def exchange_sibling(gs, layer, tag=""):
    n = len(gs)

    def body(*refs):
        send, recv, _ = refs[2 * n:]
        start, finish = _sibling_exchange(refs[:n], refs[n:2 * n], send, recv, layer)
        start()
        finish()

    return _comm_call(f"exchange_sibling{layer}{tag}", body, gs, [jax.ShapeDtypeStruct(g.shape, g.dtype) for g in gs], n, 0)


def _sibling_exchange(ins, outs, send, recv, layer):
    x, y, c, _ = _place()

    def copies():
        return [pltpu.make_async_remote_copy(src_ref=ins[a], dst_ref=outs[a], send_sem=send.at[a], recv_sem=recv.at[a],
                                             device_id=(x, y, 1 - c), device_id_type=MESH) for a in range(len(ins))]

    def start():
        @pl.when(c != layer)
        def _():
            for cp in copies():
                cp.start()

    def finish():
        @pl.when(c != layer)
        def _():
            for cp in copies():
                cp.wait_send()

        @pl.when(c == layer)
        def _():
            for cp in copies():
                cp.wait_recv()

    return start, finish


def sibling_hook(gs, layer):
    return (gs, [jax.ShapeDtypeStruct(g.shape, g.dtype) for g in gs], len(gs),
            lambda i, o, s, r: _sibling_exchange(i, o, s, r, layer))


def chips_hook(ps, layer):
    return (ps, _chips_shapes(ps), 3 * len(ps), lambda i, o, s, r: _chips_exchange(i, o, s, r, layer))


def _chips_exchange(ins, outs, send, recv, layer):
    n = len(ins)
    _, _, c, chips = _place()

    def copies():
        return [pltpu.make_async_remote_copy(src_ref=ins[a].at[2 * cx + cy], dst_ref=outs[a].at[j], send_sem=send.at[3 * a + j],
                                             recv_sem=recv.at[3 * a + j], device_id=(cx, cy, c), device_id_type=MESH)
                for j, (cx, cy) in enumerate(chips) for a in range(n)]

    def start():
        @pl.when(c == layer)
        def _():
            for cp in copies():
                cp.start()

    def finish():
        @pl.when(c == layer)
        def _():
            for cp in copies():
                cp.wait()

    return start, finish


def _chips_shapes(ps):
    return [jax.ShapeDtypeStruct((3,) + p.shape[1:], p.dtype) for p in ps]


def exchange_chips(ps, layer, tag=""):
    n = len(ps)

    def body(*refs):
        send, recv, _ = refs[2 * n:]
        start, finish = _chips_exchange(refs[:n], refs[n:2 * n], send, recv, layer)
        start()
        finish()

    return _comm_call(f"exchange_chips{layer}{tag}", body, ps, _chips_shapes(ps), 3 * n, 0)


def exchange_final(rs, layer, small=None):
    n = len(rs)
    ins_all = list(rs) + ([small] if small is not None else [])

    def body(*refs):
        ni = len(ins_all)
        ins, outs = refs[:ni], refs[ni:2 * ni]
        send, recv, lsem = refs[2 * ni:]
        x, y, c, _ = _place()
        s = 2 * x + y
        to_sib = [pltpu.make_async_remote_copy(src_ref=ins[a], dst_ref=outs[a], send_sem=send.at[a], recv_sem=recv.at[a],
                                               device_id=(x, y, 1 - c), device_id_type=MESH) for a in range(n)]
        rel = [(fx, fy) for fx in (0, 1) for fy in (0, 1)]

        def piece(fx, fy, cc, dst_slot):
            k = n + 2 * (2 * fx + fy) + cc
            return pltpu.make_async_remote_copy(src_ref=ins[n], dst_ref=outs[n].at[dst_slot], send_sem=send.at[k], recv_sem=recv.at[k],
                                                device_id=(x ^ fx, y ^ fy, cc), device_id_type=MESH)

        @pl.when(c == layer)
        def _():
            for cp in to_sib:
                cp.start()
            if small is not None:
                own = pltpu.make_async_copy(ins[n], outs[n].at[s], lsem.at[0])
                own.start()
                sends = [piece(fx, fy, cc, s) for fx, fy in rel for cc in (0, 1) if (fx, fy) != (0, 0) or cc != layer]
                for cp in sends:
                    cp.start()
                for fx, fy in rel[1:]:
                    piece(fx, fy, layer, 2 * (x ^ fx) + (y ^ fy)).wait_recv()
                for cp in sends:
                    cp.wait_send()
                own.wait()
            for cp in to_sib:
                cp.wait_send()

        @pl.when(c != layer)
        def _():
            for cp in to_sib:
                cp.wait_recv()
            if small is not None:
                for fx, fy in rel:
                    piece(fx, fy, 1 - layer, 2 * (x ^ fx) + (y ^ fy)).wait_recv()

    out_shapes = [jax.ShapeDtypeStruct(r.shape, r.dtype) for r in rs]
    if small is not None:
        out_shapes.append(jax.ShapeDtypeStruct((4,) + small.shape, small.dtype))
    res = _comm_call(f"exchange_final{layer}", body, ins_all, out_shapes, n + 8, 1)
    return (res[:n], res[n]) if small is not None else (res[:n], None)


def _pad_in_cols(w):
    z = lambda n: jnp.zeros(w.shape[:-1] + (n,), w.dtype)
    return jnp.concatenate([w[..., :KR_OFF], z(64), w[..., KR_OFF:KR_OFF + MLA_ROPE], z(32), w[..., KR_OFF + MLA_ROPE:]], axis=-1)


def _pad_heads(w, real):
    k = w.shape[0]
    return jnp.pad(w.reshape(k, MLA_HEADS, real), ((0, 0), (0, 0), (0, LANES - real))).reshape(k, MLA_HEADS * LANES)


def _pad_gain(g, real):
    return jnp.pad(g.reshape(1, real), ((0, 0), (0, LANES - real)))


def layer_weights(full):
    w_ukv = full["w_ukv"].reshape(128, MLA_HEADS, 2, 64)
    two = lambda g: jnp.concatenate([g, g]).reshape(1, LANES)
    return dict(
        norm_g=full["norm_g"].reshape(1, -1), w_in=full["w_in"], conv_w=full["conv_w"], conv_b=full["conv_b"].reshape(1, -1),
        wgx=full["w_gate_x"], bgx=full["b_gate_x"].reshape(LRU_BLOCKS, 1, LANES),
        wga=full["w_gate_a"], bga=full["b_gate_a"].reshape(LRU_BLOCKS, 1, LANES),
        lam=full["lru_lambda"].reshape(1, -1), w_lru_o=full["w_lru_o"],
        cq_norm_g=full["cq_norm_g"].reshape(1, -1), ckv_norm_g=full["ckv_norm_g"].reshape(1, -1),
        wq=_pad_heads(full["w_uq"], MLA_QK), wk=_pad_heads(w_ukv[:, :, 0].reshape(128, 512), 64),
        wv=w_ukv[:, :, 1].reshape(128, 512),
        gq=_pad_gain(full["mla_q_norm_g"], MLA_QK), gk=_pad_gain(full["mla_k_norm_g"], MLA_QK),
        w_mla_o=full["w_mla_o"], gq2=two(full["dil_q_norm_g"]), gk2=two(full["dil_k_norm_g"]),
        w_dil_o=full["w_dil_o"], b_merge=full["b_merge"].reshape(1, -1), w_out=full["w_out"],
    )


def layer_fwd(x, w, tabs, gather=None):
    h, ht = rmsnorm_fwd(x, w["norm_g"])
    z_lru = mm_nn("in_proj_lru", h, w["w_in"], n=2048, n_off=G_LRU)
    z_mla = mm_nn("in_proj_mla", h, w["w_in"], n=1024, n_off=G_MLA)
    z_dil = mm_nn("in_proj_dil", h, w["w_in"], n=5120, n_off=G_DIL)
    z_mrg = mm_nn("in_proj_mrg", h, w["w_in"], n=3072, n_off=G_MRG)
    rest_hook = None if gather is None else gather_hook(gather[0][1:], gather[1])
    hs, y_lru, yt_lru, *ga_rest = lru_fwd(z_lru, w["conv_w"], w["conv_b"], w["wgx"], w["bgx"], w["wga"], w["bga"], w["lam"],
                                          comm=rest_hook)
    qm, km, vm, vtm = mla_prep_fwd(z_mla, tabs, w)
    o_mla, y_mla, yt_mla, lse_mla, *ga_first = mla_attn_fwd(qm, km, vtm, z_mla,
                                                            gather=None if gather is None else (gather[0][:1], gather[1]))
    gathered = ga_first + ga_rest
    os_, lses, dil_rm = [], [], []
    for gi, (_, dil) in enumerate(DIL_GROUPS):
        qkv = dil_prep_fwd(z_dil, tabs, w["gq2"], w["gk2"], gi)
        o, lse = dil_attn_fwd(f"dil_attn_fwd_{dil}", *qkv, dil)
        os_.append(o)
        lses.append(lse)
        dil_rm.append(qkv)
    y_dil, o_dil, lse_dil, yt_dil = dil_combine(os_, lses, z_dil)
    ps = [mm_nn("proj_lru", y_lru, w["w_lru_o"]), mm_nn("proj_mla", y_mla, w["w_mla_o"]), mm_nn("proj_dil", y_dil, w["w_dil_o"])]
    merged, merged_t = merge_fwd(ps, z_mrg, w["b_merge"])
    out = mm_nn("out_proj", merged, w["w_out"], add=x)
    res = dict(x=x, ht=ht, z_lru=z_lru, z_mla=z_mla, z_dil=z_dil, z_mrg=z_mrg, hs=hs, yt_lru=yt_lru, qm=qm, km=km, vm=vm, o_mla=o_mla,
               yt_mla=yt_mla, lse_mla=lse_mla, dil_rm=dil_rm, yt_dil=yt_dil, o_dil=o_dil, lse_dil=lse_dil, ps=ps, merged_t=merged_t)
    return out, res, gathered


def layer_bwd(dout, r, w, tabs, prev=None, own=None):
    g = {}
    dmerged = mm_nt("out_proj_dx", dout, w["w_out"])
    g["w_out"] = mm_nn("out_proj_dw", r["merged_t"], dout, tm=1024, tn=512, tk=2048)
    dp0, dp1, dp2, dzm0, dzm1, dzm2, db0, db1, db2 = merge_bwd(dmerged, r["ps"], r["z_mrg"], w["b_merge"])
    g["b_merge"] = jnp.concatenate([db0, db1, db2], axis=1).reshape(-1)
    dy_lru = mm_nt("proj_lru_dx", dp0, w["w_lru_o"])
    dy_mla = mm_nt("proj_mla_dx", dp1, w["w_mla_o"])
    dy_dil = mm_nt("proj_dil_dx", dp2, w["w_dil_o"])
    g["w_lru_o"] = mm_nn("proj_lru_dw", r["yt_lru"], dp0, **DW_TILES)
    g["w_mla_o"] = mm_nn("proj_mla_dw", r["yt_mla"], dp1, **DW_TILES)
    g["w_dil_o"] = mm_nn("proj_dil_dw", r["yt_dil"], dp2, **DW_TILES)
    dzx, dzg_lru, dwgx, dbgx, dwga, dbga, dlam, dcw, dcb, *la_prev = lru_bwd(
        r["z_lru"], r["hs"], dy_lru, w["conv_w"], w["conv_b"], w["wgx"], w["bgx"], w["wga"], w["bga"], w["lam"],
        comm=None if prev is None else sibling_hook(prev[0], prev[1]))
    chips = None if prev is None else (prev[2](la_prev), prev[1])
    g.update(w_gate_x=dwgx, b_gate_x=dbgx.reshape(LRU_BLOCKS, LANES), w_gate_a=dwga, b_gate_a=dbga.reshape(LRU_BLOCKS, LANES),
             lru_lambda=dlam.reshape(-1), conv_w=dcw, conv_b=dcb.reshape(-1))
    do_m, dzg_mla, dd_m = mla_gate_bwd(dy_mla, r["o_mla"], r["z_mla"])
    dq_m, dk_m, dv_m, *lb_prev = mla_attn_bwd(r["qm"], r["km"], r["vm"], do_m, r["lse_mla"], dd_m, chips=chips)
    dz_mla3, dg_cq, dg_ckv, dwq, dwk, dwv, dgq, dgk = mla_prep_bwd(r["z_mla"], tabs, w, dq_m, dk_m, dv_m)
    g.update(cq_norm_g=dg_cq.reshape(-1), ckv_norm_g=dg_ckv.reshape(-1), mla_q_norm_g=dgq[0, :MLA_QK], mla_k_norm_g=dgk[0, :MLA_QK])
    g["w_uq"] = dwq.reshape(256, MLA_HEADS, LANES)[:, :, :MLA_QK].reshape(256, MLA_HEADS * MLA_QK)
    g["w_ukv"] = jnp.concatenate([dwk.reshape(128, MLA_HEADS, LANES)[:, :, :64], dwv.reshape(128, MLA_HEADS, 64)], axis=2).reshape(128, 1024)
    dzg_dil, do1, dd1, do2, dd2, l2, do3, dd3, l3 = dil_gate_bwd(dy_dil, r["o_dil"], r["z_dil"], r["lse_dil"])
    stats = [(do1, r["lse_dil"], dd1), (do2, l2, dd2), (do3, l3, dd3)]
    dzq, dzk, dzv, dgq2, dgk2 = [], [], [], [], []
    for gi, (_, dil) in enumerate(DIL_GROUPS):
        dq, dk, dv = dil_attn_bwd(f"dil_attn_bwd_{dil}", *r["dil_rm"][gi], *stats[gi], dil)
        parts = dil_prep_bwd(r["z_dil"], tabs, w["gq2"], w["gk2"], dq, dk, dv, gi)
        for acc, part in zip((dzq, dzk, dzv, dgq2, dgk2), parts):
            acc.append(part)
    g.update(dil_q_norm_g=sum(dgq2)[0, :DIL_HD], dil_k_norm_g=sum(dgk2)[0, :DIL_HD])
    dz = jnp.concatenate([dzx, dzg_lru, dz_mla3, dzg_mla] + dzq + dzk + dzv + [dzg_dil, dzm0, dzm1, dzm2], axis=1)
    g["w_in"] = in_proj_dw(r["ht"], dz)
    own_hook = None if own is None else chips_hook(*own(g))
    out = mm_nt("in_proj_dx", dz, w["w_in"], tm=1024, tn=1024, tk=IN_PAD // 4, comm=own_hook)
    dh, lb_own = (out, []) if own_hook is None else (out[0], out[1:])
    dx, dng = rmsnorm_bwd(r["x"], dh, dout, w["norm_g"])
    g["norm_g"] = dng.reshape(-1)
    return dx, g, dict(la_prev=la_prev, lb_prev=lb_prev, lb_own=lb_own)


def local_step(x, positions, target, full0, full1=None, gather1=None, reduce=None):
    tabs = rope_tables(positions.reshape(-1, 1))
    ws, ress = [], []
    for l in range(2):
        if l == 0:
            ws.append(layer_weights(full0))
            x, res, gathered = layer_fwd(x, ws[0], tabs, gather=None if gather1 is None else (gather1[0], 1))
        else:
            ws.append(layer_weights(full1 if gather1 is None else gather1[1](gathered)))
            x, res, _ = layer_fwd(x, ws[1], tabs)
        ress.append(res)
    dy, loss = loss_head(x, target)
    dy, grads1, _ = layer_bwd(dy, ress[1], ws[1], tabs)
    prev1, own0 = (None, None) if reduce is None else (reduce[0](grads1), reduce[1])
    dy, grads0, landed = layer_bwd(dy, ress[0], ws[0], tabs, prev=prev1, own=own0)
    return loss, dy, [grads0, grads1], landed


WEIGHTS = ["norm_g", "w_in", "conv_w", "conv_b", "w_gate_x", "b_gate_x", "w_gate_a", "b_gate_a", "lru_lambda", "w_lru_o", "cq_norm_g",
           "ckv_norm_g", "w_uq", "w_ukv", "mla_q_norm_g", "mla_k_norm_g", "w_mla_o", "dil_q_norm_g", "dil_k_norm_g", "w_dil_o", "b_merge",
           "w_out"]
SHARDED = {"w_in": 2, "conv_w": 2, "w_lru_o": 1, "w_uq": 2, "w_ukv": 2, "w_mla_o": 2, "w_dil_o": 2, "w_out": 1}
REPLICATED = [n for n in WEIGHTS if n not in SHARDED]
SMALL_ROWS = 144


def kernel(x, positions, norm_g, w_in, conv_w, conv_b, w_gate_x, b_gate_x, w_gate_a, b_gate_a, lru_lambda, w_lru_o, cq_norm_g, ckv_norm_g, w_uq, w_ukv, mla_q_norm_g, mla_k_norm_g, w_mla_o, dil_q_norm_g, dil_k_norm_g, w_dil_o, b_merge, w_out, loss_target, m_norm_g, m_w_in, m_conv_w, m_conv_b, m_w_gate_x, m_b_gate_x, m_w_gate_a, m_b_gate_a, m_lru_lambda, m_w_lru_o, m_cq_norm_g, m_ckv_norm_g, m_w_uq, m_w_ukv, m_mla_q_norm_g, m_mla_k_norm_g, m_w_mla_o, m_dil_q_norm_g, m_dil_k_norm_g, m_w_dil_o, m_b_merge, m_w_out, v_norm_g, v_w_in, v_conv_w, v_conv_b, v_w_gate_x, v_b_gate_x, v_w_gate_a, v_b_gate_a, v_lru_lambda, v_w_lru_o, v_cq_norm_g, v_ckv_norm_g, v_w_uq, v_w_ukv, v_mla_q_norm_g, v_mla_k_norm_g, v_w_mla_o, v_dil_q_norm_g, v_dil_k_norm_g, v_w_dil_o, v_b_merge, v_w_out):
    args = locals()
    w = {n: args[n] for n in WEIGHTS}
    m = {n: args["m_" + n] for n in WEIGHTS}
    v = {n: args["v_" + n] for n in WEIGHTS}
    my_c = lax.axis_index("c").astype(jnp.int32)
    my_s = (2 * lax.axis_index("x") + lax.axis_index("y")).astype(jnp.int32)
    c_idx = my_c.reshape(1)
    s_idx = my_s.reshape(1)

    names = list(SHARDED)
    wire = [[w[n][l] if n == "conv_w" else w[n][l].astype(BF16) for n in names] for l in range(2)]

    def assemble(l, gathered):
        full = {n: w[n][l] for n in REPLICATED}
        for n, ga, own in zip(names, gathered, wire[l]):
            parts = [jnp.where(my_s == s, own, ga[s]) for s in range(4)]
            if n == "w_in":
                parts = [_pad_in_cols(parts[0])] + parts[1:]
            full[n] = jnp.concatenate(parts, axis=SHARDED[n] - 1)
        return full

    full0 = assemble(0, allgather_layer(wire[0], 0))

    def shards_of(grads_l):
        gs = []
        for n in names:
            g_ = grads_l[n]
            if n == "w_in":
                gs.append(g_.reshape(4, IN_WIDTH // 4, D_MODEL))
            else:
                parts = jnp.stack(jnp.split(g_, 4, axis=SHARDED[n] - 1))
                gs.append(parts.reshape(4, -1, parts.shape[-1]))
        return gs

    def pair_sums(l, gs, la, tags):
        dts = [BF16 if g_.size >= 2**19 and t != "small" else F32 for g_, t in zip(gs, tags)]
        return [sum_pair(f"sum_pair{l}_{t}", g_, l_, dt) for t, g_, l_, dt in zip(tags, gs, la, dts)]

    def chip_sums(l, gs, la, lb, tags):
        return [sum_chips(f"sum_chips{l}_{t}", g_, l_, b_, s_idx) for t, g_, l_, b_ in zip(tags, gs, la, lb)]

    stash = {}

    def prev1(grads1):
        stash["gs1"] = shards_of(grads1)
        return stash["gs1"], 1, lambda la: pair_sums(1, stash["gs1"], la, names)

    def own0(grads0):
        stash["gs0"] = shards_of(grads0)
        stash["la0"] = exchange_sibling(stash["gs0"], 0)
        return pair_sums(0, stash["gs0"], stash["la0"], names), 0

    loss, grad_x, grads, landed = local_step(x[0], positions[0], loss_target[0], full0,
                                             gather1=(wire[1], functools.partial(assemble, 1)), reduce=(prev1, own0))
    loss = lax.psum(loss, ("x", "y", "c"))
    reduced1 = chip_sums(1, stash["gs1"], landed["la_prev"], landed["lb_prev"], names)
    received1, _ = exchange_final(reduced1, 1)
    reduced0 = chip_sums(0, stash["gs0"], stash["la0"], landed["lb_own"], names)

    flat = jnp.concatenate([jnp.stack([grads[0][n], grads[1][n]]).reshape(-1) for n in REPLICATED])
    small = [jnp.pad(flat, (0, 4 * SMALL_ROWS * 1024 - flat.size)).reshape(4, SMALL_ROWS, 1024)]
    la_s = exchange_sibling(small, 0, tag="_small")
    lb_s = exchange_chips(pair_sums(0, small, la_s, ["small"]), 0, tag="_small")
    received0, small_all = exchange_final(reduced0, 0, small=chip_sums(0, small, la_s, lb_s, ["small"])[0])

    g_local, delta, new_m, new_v = {}, {}, {}, {}
    for i, n in enumerate(names):
        shp = w[n].shape
        if n == "w_in":
            as3 = lambda a: a.transpose(0, 2, 1)
            back = lambda o: o.transpose(0, 2, 1)
        else:
            as3 = lambda a, rc=reduced0[i].shape: a.reshape((2,) + rc)
            back = lambda o, shp=shp: o.reshape(shp)
        outs = adamw_pair(f"adamw_{n}", as3(w[n]), (reduced0[i], reduced1[i]), (received0[i], received1[i]), as3(m[n]), as3(v[n]),
                          c_idx)
        g_local[n], delta[n], new_m[n], new_v[n] = [back(o) for o in outs]
    flat = small_all.reshape(-1)
    off = 0
    for n in REPLICATED:
        g_local[n] = flat[off:off + w[n].size].reshape(w[n].shape)
        off += w[n].size
        delta[n], new_m[n], new_v[n] = adamw(f"adamw_{n}", w[n], g_local[n], m[n], v[n])
    return (loss, grad_x[None], *[g_local[n] for n in WEIGHTS], *[delta[n] for n in WEIGHTS], *[new_m[n] for n in WEIGHTS],
            *[new_v[n] for n in WEIGHTS])
```

```python
import functools

import jax
import jax.numpy as jnp
from jax import lax
from jax.experimental import pallas as pl
from jax.experimental.pallas import tpu as pltpu

F32 = jnp.float32
BF16 = jnp.bfloat16
MXU_DTYPE = jnp.bfloat16

D_MODEL = 1024
EPS = 1e-6
ROPE_THETA = 10000.0
LRU_BLOCKS = 8
LRU_C = 8.0
MLA_HEADS = 8
MLA_NOPE = 64
MLA_ROPE = 32
MLA_QK = 96
DIL_GROUPS = ((128, 1), (512, 4), (2048, 16))
DIL_HD = 64
DIL_NK = 128
MLA_HPS = 8
IN_WIDTH = 11168
ADAM_LR, ADAM_B1, ADAM_B2, ADAM_EPS, ADAM_WD, ADAM_STEP = 0.001, 0.9, 0.999, 1e-08, 0.01, 10

LANES = 128
G_LRU, G_MLA, G_DIL, G_MRG = 0, 2048, 3072, 8192
IN_PAD = 11264
KR_OFF = 2432

NN = (((1,), (0,)), ((), ()))
NT = (((1,), (1,)), ((), ()))
TN = (((0,), (0,)), ((), ()))
NEG = -1e30
MESH = pl.DeviceIdType.MESH
VMEM_LIMIT = 48 * 2**20


def _cparams(sem):
    return pltpu.CompilerParams(dimension_semantics=sem, vmem_limit_bytes=VMEM_LIMIT)


def _dot(a, b, dims):
    return lax.dot_general(a.astype(MXU_DTYPE), b.astype(MXU_DTYPE), dims, preferred_element_type=F32)


@jax.custom_vjp
def mm(a, w):
    return _dot(a, w, NN)


def _mm_fwd(a, w):
    return _dot(a, w, NN), (a, w)


def _mm_bwd(res, g):
    a, w = res
    return _dot(g, w, NT), _dot(a, g, TN)


mm.defvjp(_mm_fwd, _mm_bwd)


def _seg64_matrix():
    r = lax.broadcasted_iota(jnp.int32, (LANES, LANES), 0) < DIL_HD
    c = lax.broadcasted_iota(jnp.int32, (LANES, LANES), 1) < DIL_HD
    return (r == c).astype(BF16)


def _seg_sum_impl(x):
    b = _seg64_matrix()
    hi = x.astype(BF16)
    r1 = x - hi.astype(F32)
    mid = r1.astype(BF16)
    lo = (r1 - mid.astype(F32)).astype(BF16)
    dot = lambda u: lax.dot_general(u, b, NN, preferred_element_type=F32)
    return dot(hi) + dot(mid) + dot(lo)


@jax.custom_vjp
def seg_sum(x):
    return _seg_sum_impl(x)


seg_sum.defvjp(lambda x: (_seg_sum_impl(x), None), lambda _, g: (_seg_sum_impl(g),))


def _lroll_impl(x, s):
    return pltpu.roll(x, s % LANES, 1)


@functools.partial(jax.custom_vjp, nondiff_argnums=(1,))
def lroll(x, s):
    return _lroll_impl(x, s)


lroll.defvjp(lambda x, s: (_lroll_impl(x, s), None), lambda s, _, g: (_lroll_impl(g, -s),))


class _Ops:
    def __init__(self, diff):
        self.mm = mm if diff else (lambda a, w: _dot(a, w, NN))
        self.seg_sum = seg_sum if diff else _seg_sum_impl
        self.lroll = lroll if diff else _lroll_impl


PLAIN, DIFF = _Ops(False), _Ops(True)


def rms(x, g, n):
    ms = jnp.sum(x * x, axis=-1, keepdims=True) * (1.0 / n)
    return x * lax.rsqrt(ms + EPS) * g


def rope(ops, x, c, s1, s2, half):
    return x * c + ops.lroll(x, -half) * s1 + ops.lroll(x, half) * s2


def _sigmoid(x):
    return 1.0 / (1.0 + jnp.exp(-x))


def _silu_and_grad(g):
    sg = _sigmoid(g)
    return g * sg, sg * (1.0 + g * (1.0 - sg))


def _softplus(x):
    return jnp.maximum(x, 0.0) + jnp.log(1.0 + jnp.exp(-jnp.abs(x)))


def _expm1(y):
    series = y * (1.0 + y * (0.5 + y * (1.0 / 6.0 + y * (1.0 / 24.0 + y * (1.0 / 120.0)))))
    return jnp.where(y > -0.05, series, jnp.exp(jnp.minimum(y, -0.05)) - 1.0)


def _host(body, n_in, n_out, n_scr, hook, grid):
    if hook is None:
        return body, [], [], [], [], []
    ins, out_shapes, n_sem, make = hook
    ni, no = len(ins), len(out_shapes)
    hbm = pl.BlockSpec(memory_space=pltpu.HBM)

    def at(corner):
        cond = None
        for axis, size in enumerate(grid):
            here = pl.program_id(axis) == (size - 1 if corner else 0)
            cond = here if cond is None else cond & here
        return cond

    def hosted(*refs):
        a, refs = refs[:n_in], refs[n_in:]
        xi, refs = refs[:ni], refs[ni:]
        o, refs = refs[:n_out], refs[n_out:]
        xo, refs = refs[:no], refs[no:]
        scr, (send, recv) = refs[:n_scr], refs[n_scr:]
        start, finish = make(xi, xo, send, recv)
        pl.when(at(False))(start)
        body(*a, *o, *scr)
        pl.when(at(True))(finish)

    sems = [pltpu.SemaphoreType.DMA((n_sem,)), pltpu.SemaphoreType.DMA((n_sem,))]
    return hosted, list(ins), [hbm] * ni, [hbm] * no, list(out_shapes), sems


def _mm_call(name, a, b, *, mode, m, n, k, a_blk, b_blk, out_dtype, add, tm, tn, tk, comm=None):
    nk = k // tk
    assert m % tm == 0 and n % tn == 0 and k % tk == 0, (name, m, n, k, tm, tn, tk)
    dims = {"nn": NN, "nt": NT, "tn": TN}[mode]

    def body(*refs):
        if add is None:
            a_ref, b_ref, o_ref, *scr = refs
            add_ref = None
        else:
            a_ref, b_ref, add_ref, o_ref, *scr = refs
        part = _dot(a_ref[...], b_ref[...], dims)

        def finish(acc):
            if add_ref is not None:
                acc = acc + add_ref[...]
            o_ref[...] = acc.astype(o_ref.dtype)

        if nk == 1:
            finish(part)
        else:
            (acc_ref,) = scr
            kk = pl.program_id(2)

            @pl.when(kk == 0)
            def _():
                acc_ref[...] = part

            @pl.when(kk > 0)
            def _():
                acc_ref[...] += part

            @pl.when(kk == nk - 1)
            def _():
                finish(acc_ref[...])

    in_specs = [a_blk, b_blk]
    args = [a, b]
    if add is not None:
        in_specs.append(pl.BlockSpec((tm, tn), lambda j, i, kk: (i, j)))
        args.append(add)
    grid = (n // tn, m // tm, nk)
    scratch = [] if nk == 1 else [pltpu.VMEM((tm, tn), F32)]
    body, x_in, x_in_specs, x_out_specs, x_out_shapes, sems = _host(body, len(args), 1, len(scratch), comm, grid)
    res = pl.pallas_call(
        body,
        name=name,
        grid=grid,
        in_specs=in_specs + x_in_specs,
        out_specs=[pl.BlockSpec((tm, tn), lambda j, i, kk: (i, j))] + x_out_specs,
        out_shape=[jax.ShapeDtypeStruct((m, n), out_dtype)] + x_out_shapes,
        scratch_shapes=scratch + sems,
        compiler_params=_cparams(("arbitrary",) * 3 if comm is not None else ("parallel", "parallel", "arbitrary")),
    )(*args, *x_in)
    return res[0] if comm is None else res


def mm_nn(name, a, b, *, n=None, n_off=0, out_dtype=F32, add=None, tm=1024, tn=1024, tk=1024):
    m, k = a.shape
    n = b.shape[1] if n is None else n
    tm, tn, tk = min(tm, m), min(tn, n), min(k, tk)
    ob = n_off // tn
    assert n_off % tn == 0
    return _mm_call(name, a, b, mode="nn", m=m, n=n, k=k, out_dtype=out_dtype, add=add, tm=tm, tn=tn, tk=tk,
                    a_blk=pl.BlockSpec((tm, tk), lambda j, i, kk: (i, kk)),
                    b_blk=pl.BlockSpec((tk, tn), lambda j, i, kk: (kk, j + ob)))


def mm_nt(name, a, b, *, out_dtype=F32, tm=1024, tn=1024, tk=1024, comm=None):
    m, k = a.shape
    n = b.shape[0]
    tm, tn, tk = min(tm, m), min(tn, n), min(k, tk)
    return _mm_call(name, a, b, mode="nt", m=m, n=n, k=k, out_dtype=out_dtype, add=None, tm=tm, tn=tn, tk=tk, comm=comm,
                    a_blk=pl.BlockSpec((tm, tk), lambda j, i, kk: (i, kk)),
                    b_blk=pl.BlockSpec((tn, tk), lambda j, i, kk: (j, kk)))


DW_TILES = dict(tm=1024, tn=512, tk=4096)


def in_proj_dw(ht, dz):
    t = ht.shape[1]
    tn = 512
    nchunk = IN_PAD // tn
    kr_chunk = KR_OFF // tn
    lo = KR_OFF - kr_chunk * tn

    def body(ht_ref, dz_ref, out_ref, buf, sems):
        j = pl.program_id(0)
        slot = j % 2
        buf[slot] = _dot(ht_ref[...], dz_ref[...], NN).T

        def copies(c, s):
            dst = jnp.where(c < kr_chunk, c * tn, (c - 1) * tn + lo + MLA_ROPE)
            plain = [pltpu.make_async_copy(buf.at[s], out_ref.at[pl.ds(pl.multiple_of(dst, 8), tn)], sems.at[s, 0])]
            split = [pltpu.make_async_copy(buf.at[s, 0:lo], out_ref.at[KR_OFF - lo:KR_OFF], sems.at[s, 0]),
                     pltpu.make_async_copy(buf.at[s, lo + 64:lo + 96], out_ref.at[KR_OFF:KR_OFF + MLA_ROPE], sems.at[s, 1])]
            return plain, split

        def each(c, s, action):
            plain, split = copies(c, s)

            @pl.when(c != kr_chunk)
            def _():
                for cp in plain:
                    action(cp)

            @pl.when(c == kr_chunk)
            def _():
                for cp in split:
                    action(cp)

        @pl.when(j > 0)
        def _():
            each(j - 1, 1 - slot, lambda cp: cp.wait())

        each(j, slot, lambda cp: cp.start())

        @pl.when(j == nchunk - 1)
        def _():
            each(j, slot, lambda cp: cp.wait())

    return pl.pallas_call(
        body,
        name="in_proj_dw",
        grid=(nchunk,),
        in_specs=[pl.BlockSpec((D_MODEL, t), lambda j: (0, 0)), pl.BlockSpec((t, tn), lambda j: (0, j))],
        out_specs=pl.BlockSpec(memory_space=pltpu.HBM),
        out_shape=jax.ShapeDtypeStruct((IN_WIDTH, D_MODEL), F32),
        scratch_shapes=[pltpu.VMEM((2, tn, D_MODEL), F32), pltpu.SemaphoreType.DMA((2, 2))],
        compiler_params=_cparams(("arbitrary",)),
    )(ht, dz)


def rowwise(name, fn, t, *, tm=256, ncol=1, ins=(), outs=(), touts=(), accs=()):
    n_in, n_out, n_acc = len(ins), len(outs) + len(touts), len(accs)

    def zero_when(ref, cond):
        @pl.when(cond)
        def _():
            ref[...] = jnp.zeros(ref.shape, ref.dtype)

    def body(*refs):
        in_refs, out_refs, acc_refs = refs[:n_in], refs[n_in:n_in + n_out], refs[n_in + n_out:]
        j, i = pl.program_id(0), pl.program_id(1)
        for ref, (_, _, _, cd) in zip(acc_refs, accs):
            zero_when(ref, (i == 0) if cd else ((i == 0) & (j == 0)))
        fn(in_refs, out_refs, acc_refs)

    def in_spec(arr, w, base, cd, rd):
        rows = tm if rd else arr.shape[0]
        return pl.BlockSpec((rows, w), lambda j, i: (i if rd else 0, base + (j if cd else 0)))

    in_specs = [in_spec(*e) for e in ins]
    out_specs = [pl.BlockSpec((tm, w), (lambda j, i, cd=cd: (i, j if cd else 0))) for (_, w, _, cd) in outs]
    out_specs += [pl.BlockSpec((w, tm), (lambda j, i, cd=cd: (j if cd else 0, i))) for (_, w, _, cd) in touts]
    out_specs += [pl.BlockSpec((r, w), (lambda j, i, cd=cd: (0, j if cd else 0))) for (r, _, w, cd) in accs]
    out_shape = [jax.ShapeDtypeStruct((t, c), dt) for (c, _, dt, _) in outs]
    out_shape += [jax.ShapeDtypeStruct((r, t), dt) for (r, _, dt, _) in touts]
    out_shape += [jax.ShapeDtypeStruct((r, c), F32) for (r, c, _, _) in accs]
    res = pl.pallas_call(
        body,
        name=name,
        grid=(ncol, t // tm),
        in_specs=in_specs,
        out_specs=out_specs,
        out_shape=out_shape,
        compiler_params=_cparams(("arbitrary", "arbitrary") if accs else ("parallel", "parallel")),
    )(*[e[0] for e in ins])
    return res


def _row(arr, w=None, base=0, cd=False):
    return (arr, arr.shape[1] if w is None else w, base, cd, True)


def _const(arr, w=None, base=0, cd=False):
    return (arr, arr.shape[1] if w is None else w, base, cd, False)


def rope_tables(positions):
    t = positions.shape[0]

    def fn(ins, outs, _):
        pos = ins[0][...].astype(F32)
        lane = lax.broadcasted_iota(jnp.int32, (1, LANES), 1)
        log_theta = jnp.log(jnp.float32(ROPE_THETA))
        jm = lane - MLA_NOPE
        idx = jnp.clip(jnp.where(jm < 16, jm, jm - 16), 0, 15).astype(F32)
        ang = pos * jnp.exp(-(idx * (2.0 / MLA_ROPE)) * log_theta)
        cos, sin = jnp.cos(ang), jnp.sin(ang)
        in_rope = (lane >= MLA_NOPE) & (lane < MLA_QK)
        outs[0][:, 0:128] = jnp.where(lane < MLA_NOPE, 1.0, jnp.where(in_rope, cos, 0.0))
        outs[0][:, 128:256] = jnp.where(in_rope & (jm < 16), -sin, 0.0)
        outs[0][:, 256:384] = jnp.where(in_rope & (jm >= 16), sin, 0.0)
        jd = lane & (DIL_HD - 1)
        idx = (jd & 31).astype(F32)
        ang = pos * jnp.exp(-(idx * (2.0 / DIL_HD)) * log_theta)
        cos, sin = jnp.cos(ang), jnp.sin(ang)
        outs[0][:, 384:512] = cos
        outs[0][:, 512:640] = jnp.where(jd < 32, -sin, 0.0)
        outs[0][:, 640:768] = jnp.where(jd >= 32, sin, 0.0)

    return rowwise("rope_tables", fn, t, tm=1024, ins=[_row(positions)], outs=[(768, 768, F32, False)])[0]


def rmsnorm_fwd(x, g):
    def fn(ins, outs, _):
        h = rms(ins[0][...], ins[1][...], D_MODEL)
        outs[0][...] = h.astype(MXU_DTYPE)
        outs[1][...] = h.T.astype(MXU_DTYPE)

    return rowwise("rmsnorm_fwd", fn, x.shape[0], tm=512, ins=[_row(x), _const(g)], outs=[(D_MODEL, D_MODEL, MXU_DTYPE, False)],
                   touts=[(D_MODEL, D_MODEL, MXU_DTYPE, False)])


def rmsnorm_bwd(x, dh, dres, g):
    def fn(ins, outs, accs):
        _, vjp = jax.vjp(lambda xv, gv: rms(xv, gv, D_MODEL), ins[0][...], ins[3][...])
        dx, dg = vjp(ins[1][...])
        outs[0][...] = ins[2][...] + dx
        accs[0][...] += dg

    return rowwise("rmsnorm_bwd", fn, x.shape[0], tm=512, ins=[_row(x), _row(dh), _row(dres), _const(g)],
                   outs=[(D_MODEL, D_MODEL, F32, False)], accs=[(1, D_MODEL, D_MODEL, False)])


def loss_head(y, target):
    def fn(ins, outs, accs):
        err = ins[0][...] - ins[1][...]
        outs[0][...] = err * (1.0 / D_MODEL)
        accs[0][...] += jnp.sum(err * err, axis=0, keepdims=True)
        accs[1][...] = jnp.broadcast_to(jnp.sum(accs[0][...], keepdims=True), (1, LANES))

    dy, _, tot = rowwise("loss_head", fn, y.shape[0], tm=512, ins=[_row(y), _row(target)], outs=[(D_MODEL, D_MODEL, F32, False)],
                         accs=[(1, D_MODEL, D_MODEL, False), (1, LANES, LANES, False)])
    return dy, tot[0, 0] * (0.5 / D_MODEL)


def _shift_rows(v, d, fill, reverse):
    tb = v.shape[0]
    if d % 8 == 0:
        pad = jnp.full((d, v.shape[1]), fill, v.dtype)
        return jnp.concatenate([v[d:], pad] if reverse else [pad, v[:tb - d]], axis=0)
    rows = lax.broadcasted_iota(jnp.int32, v.shape, 0)
    if not reverse:
        return jnp.where(rows >= d, pltpu.roll(v, d, 0), fill)
    return jnp.where(rows < tb - d, pltpu.roll(v, tb - d, 0), fill)


def _scan_tile(a, b, reverse):
    d = 1
    while d < a.shape[0]:
        b = b + a * _shift_rows(b, d, 0.0, reverse)
        a = a * _shift_rows(a, d, 1.0, reverse)
        d *= 2
    return a, b


def _lru_gates(ops, xc, wgx, bgx, wga, bga, lam):
    gx = _sigmoid(ops.mm(xc, wgx) + bgx)
    ga = _sigmoid(ops.mm(xc, wga) + bga)
    log_a = -LRU_C * ga * _softplus(-lam)
    a = jnp.exp(log_a)
    mult = jnp.sqrt(-_expm1(2.0 * log_a))
    return a, mult * (gx * xc)


def _shifted_inputs(x, halo, tb):
    rows = lax.broadcasted_iota(jnp.int32, x.shape, 0)
    pad = jnp.zeros((tb - 8, LANES), F32)
    out = []
    for d in (3, 2, 1):
        head = jnp.concatenate([pltpu.roll(halo, d, 0), pad], axis=0)
        out.append(jnp.where(rows >= d, pltpu.roll(x, d, 0), head))
    return out + [x]


def _lru_specs(nt, tb, reverse):
    hb = tb // 8
    tt = (lambda t: nt - 1 - t) if reverse else (lambda t: t)
    blk = lambda off: pl.BlockSpec((tb, LANES), lambda n, t: (tt(t), n + off))
    halo = lambda off: pl.BlockSpec((8, LANES), lambda n, t: (jnp.maximum(tt(t) * hb - 1, 0), n + off))
    chan = lambda r: pl.BlockSpec((r, LANES), lambda n, t: (0, n))
    wblk = pl.BlockSpec((None, LANES, LANES), lambda n, t: (n, 0, 0))
    bblk = pl.BlockSpec((None, 1, LANES), lambda n, t: (n, 0, 0))
    return blk, halo, chan, wblk, bblk


def lru_fwd(z, conv_w, conv_b, wgx, bgx, wga, bga, lam, *, tb=256, comm=None):
    t = z.shape[0]
    nt = t // tb
    blk, halo, chan, wblk, bblk = _lru_specs(nt, tb, False)

    def body(x_ref, xh_ref, g_ref, cw_ref, cb_ref, wgx_ref, bgx_ref, wga_ref, bga_ref, lam_ref, h_ref, y_ref, yt_ref, carry_ref):
        ti = pl.program_id(1)

        @pl.when(ti == 0)
        def _():
            carry_ref[...] = jnp.zeros((8, LANES), F32)

        x = x_ref[...]
        hal = jnp.where(ti > 0, xh_ref[...], 0.0)
        xs = _shifted_inputs(x, hal, tb)
        xc = cb_ref[...] + sum(xs[k] * cw_ref[k:k + 1, :] for k in range(4))
        a, b = _lru_gates(PLAIN, xc, wgx_ref[...], bgx_ref[...], wga_ref[...], bga_ref[...], lam_ref[...])
        acum, h0 = _scan_tile(a, b, False)
        h = h0 + acum * carry_ref[7:8, :]
        carry_ref[...] = h[tb - 8:tb, :]
        h_ref[...] = h
        y = h * _silu_and_grad(g_ref[...])[0]
        y_ref[...] = y.astype(MXU_DTYPE)
        yt_ref[...] = y.T.astype(MXU_DTYPE)

    grid = (LRU_BLOCKS, nt)
    body, x_in, x_in_specs, x_out_specs, x_out_shapes, sems = _host(body, 10, 3, 1, comm, grid)
    return pl.pallas_call(
        body,
        name="lru_fwd" if comm is None else "lru_fwd_comm",
        grid=grid,
        in_specs=[blk(0), halo(0), blk(LRU_BLOCKS), chan(4), chan(1), wblk, bblk, wblk, bblk, chan(1)] + x_in_specs,
        out_specs=[blk(0), blk(0), pl.BlockSpec((LANES, tb), lambda n, t_: (n, t_))] + x_out_specs,
        out_shape=[jax.ShapeDtypeStruct((t, 1024), F32), jax.ShapeDtypeStruct((t, 1024), MXU_DTYPE),
                   jax.ShapeDtypeStruct((1024, t), MXU_DTYPE)] + x_out_shapes,
        scratch_shapes=[pltpu.VMEM((8, LANES), F32)] + sems,
        compiler_params=_cparams(("arbitrary", "arbitrary") if comm is not None else ("parallel", "arbitrary")),
    )(z, z, z, conv_w, conv_b, wgx, bgx, wga, bga, lam, *x_in)


def lru_bwd(z, h, dy, conv_w, conv_b, wgx, bgx, wga, bga, lam, *, tb=256, comm=None):
    t = z.shape[0]
    nt = t // tb
    blk, halo, chan, wblk, bblk = _lru_specs(nt, tb, True)

    def body(x_ref, xh_ref, g_ref, h_ref, hh_ref, dy_ref, cw_ref, cb_ref, wgx_ref, bgx_ref, wga_ref, bga_ref, lam_ref,
             dzx_ref, dzg_ref, dwgx_ref, dbgx_ref, dwga_ref, dbga_ref, dlam_ref, dcw_ref, dcb_ref,
             gcar_ref, acar_ref, xcar_ref):
        ti = pl.program_id(1)
        has_earlier = ti < nt - 1

        @pl.when(ti == 0)
        def _():
            for ref in (dwgx_ref, dbgx_ref, dwga_ref, dbga_ref, dlam_ref, dcw_ref, dcb_ref, gcar_ref, acar_ref, xcar_ref):
                ref[...] = jnp.zeros(ref.shape, F32)

        rows = lax.broadcasted_iota(jnp.int32, (tb, LANES), 0)
        x = x_ref[...]
        hal = jnp.where(has_earlier, xh_ref[...], 0.0)
        xs = _shifted_inputs(x, hal, tb)
        xc = cb_ref[...] + sum(xs[k] * cw_ref[k:k + 1, :] for k in range(4))
        (a, _), vjp = jax.vjp(functools.partial(_lru_gates, DIFF), xc, wgx_ref[...], bgx_ref[...], wga_ref[...],
                              bga_ref[...], lam_ref[...])
        g, h, dyv = g_ref[...], h_ref[...], dy_ref[...]
        silu, dsilu = _silu_and_grad(g)
        dzg_ref[...] = (dyv * h * dsilu).astype(MXU_DTYPE)
        a_next = jnp.where(rows < tb - 1, pltpu.roll(a, tb - 1, 0), acar_ref[0:1, :])
        acum, g0 = _scan_tile(a_next, dyv * silu, True)
        gt = g0 + acum * gcar_ref[0:1, :]
        h_prev = jnp.where(rows >= 1, pltpu.roll(h, 1, 0), jnp.where(has_earlier, hh_ref[7:8, :], 0.0))
        dxc, dwgx, dbgx, dwga, dbga, dlam = vjp((gt * h_prev, gt))
        later = xcar_ref[...]
        gcar_ref[...] = gt[0:8, :]
        acar_ref[...] = a[0:8, :]
        xcar_ref[...] = dxc[0:8, :]
        dx = dxc * cw_ref[3:4, :]
        pad = jnp.zeros((tb - 8, LANES), F32)
        for d in (1, 2, 3):
            tail = jnp.concatenate([pad, pltpu.roll(later, 8 - d, 0)], axis=0)
            up = jnp.where(rows < tb - d, pltpu.roll(dxc, tb - d, 0), tail)
            dx = dx + up * cw_ref[3 - d:4 - d, :]
        dzx_ref[...] = dx.astype(MXU_DTYPE)
        for k in range(4):
            dcw_ref[k:k + 1, :] += jnp.sum(dxc * xs[k], axis=0, keepdims=True)
        dcb_ref[...] += jnp.sum(dxc, axis=0, keepdims=True)
        dwgx_ref[...] += dwgx
        dbgx_ref[...] += dbgx
        dwga_ref[...] += dwga
        dbga_ref[...] += dbga
        dlam_ref[...] += dlam

    sds = jax.ShapeDtypeStruct
    grid = (LRU_BLOCKS, nt)
    body, x_in, x_in_specs, x_out_specs, x_out_shapes, sems = _host(body, 13, 9, 3, comm, grid)
    return pl.pallas_call(
        body,
        name="lru_bwd" if comm is None else "lru_bwd_comm",
        grid=grid,
        in_specs=[blk(0), halo(0), blk(LRU_BLOCKS), blk(0), halo(0), blk(0), chan(4), chan(1), wblk, bblk, wblk, bblk, chan(1)]
        + x_in_specs,
        out_specs=[blk(0), blk(0), wblk, bblk, wblk, bblk, chan(1), chan(4), chan(1)] + x_out_specs,
        out_shape=[sds((t, 1024), MXU_DTYPE), sds((t, 1024), MXU_DTYPE), sds(wgx.shape, F32), sds(bgx.shape, F32), sds(wga.shape, F32),
                   sds(bga.shape, F32), sds((1, 1024), F32), sds((4, 1024), F32), sds((1, 1024), F32)] + x_out_shapes,
        scratch_shapes=[pltpu.VMEM((8, LANES), F32)] * 3 + sems,
        compiler_params=_cparams(("arbitrary", "arbitrary") if comm is not None else ("parallel", "arbitrary")),
    )(z, z, z, h, h, dy, conv_w, conv_b, wgx, bgx, wga, bga, lam, *x_in)


def _mla_prep_tile(ops, cq, ckv, krp, c, s1, s2, g_cq, g_ckv, wq, wk, wv, gq, gk):
    cqn = rms(cq, g_cq, 256)
    ckvn = rms(ckv, g_ckv, 128)
    v = ops.mm(ckvn, wv)
    qs, ks = [], []
    for hd in range(MLA_HEADS):
        q = rms(ops.mm(cqn, wq[hd]), gq, MLA_QK)
        k = rms(ops.mm(ckvn, wk[hd]) + krp, gk, MLA_QK)
        qs.append(rope(ops, q, c, s1, s2, 16) * (MLA_QK ** -0.5))
        ks.append(rope(ops, k, c, s1, s2, 16))
    return tuple(qs), tuple(ks), v


def _mla_prep_args(ins):
    z_cq, z_ckv, z_kr, tc, ts1, ts2, g_cq, g_ckv, wq, wk, wv, gq, gk = ins[:13]
    heads = lambda w: tuple(w[:, LANES * hd:LANES * (hd + 1)] for hd in range(MLA_HEADS))
    return (z_cq[...], z_ckv[...], z_kr[...], tc[...], ts1[...], ts2[...], g_cq[...], g_ckv[...], heads(wq), heads(wk),
            wv[...], gq[...], gk[...])


def _mla_prep_ins(z, tabs, w):
    return [_row(z, 256, 0), _row(z, 128, 2), _row(z, 128, 3), _row(tabs, 128, 0), _row(tabs, 128, 1), _row(tabs, 128, 2),
            _const(w["cq_norm_g"]), _const(w["ckv_norm_g"]), _const(w["wq"]), _const(w["wk"]), _const(w["wv"]),
            _const(w["gq"]), _const(w["gk"])]


def mla_prep_fwd(z, tabs, w):
    def fn(ins, outs, _):
        qs, ks, v = _mla_prep_tile(PLAIN, *_mla_prep_args(ins))
        for hd in range(MLA_HEADS):
            outs[0][:, LANES * hd:LANES * (hd + 1)] = qs[hd].astype(MXU_DTYPE)
            outs[1][:, LANES * hd:LANES * (hd + 1)] = ks[hd].astype(MXU_DTYPE)
        outs[2][...] = v.astype(MXU_DTYPE)
        outs[3][...] = v.T.astype(MXU_DTYPE)

    return rowwise("mla_prep_fwd", fn, z.shape[0], tm=512, ins=_mla_prep_ins(z, tabs, w),
                   outs=[(1024, 1024, MXU_DTYPE, False), (1024, 1024, MXU_DTYPE, False), (512, 512, MXU_DTYPE, False)],
                   touts=[(512, 512, MXU_DTYPE, False)])


def mla_prep_bwd(z, tabs, w, dq, dk, dv):
    def fn(ins, outs, accs):
        args = _mla_prep_args(ins)
        _, vjp = jax.vjp(functools.partial(_mla_prep_tile, DIFF), *args)
        heads = lambda ref: tuple(ref[:, LANES * hd:LANES * (hd + 1)] for hd in range(MLA_HEADS))
        dcq, dckv, dkr, _, _, _, dg_cq, dg_ckv, dwq, dwk, dwv, dgq, dgk = vjp((heads(ins[13]), heads(ins[14]), ins[15][...]))
        lane = lax.broadcasted_iota(jnp.int32, (1, LANES), 1)
        outs[0][:, 0:256] = dcq.astype(MXU_DTYPE)
        outs[0][:, 256:384] = dckv.astype(MXU_DTYPE)
        outs[0][:, 384:512] = jnp.where((lane >= MLA_NOPE) & (lane < MLA_QK), dkr, 0.0).astype(MXU_DTYPE)
        accs[0][...] += dg_cq
        accs[1][...] += dg_ckv
        for hd in range(MLA_HEADS):
            accs[2][:, LANES * hd:LANES * (hd + 1)] += dwq[hd]
            accs[3][:, LANES * hd:LANES * (hd + 1)] += dwk[hd]
        accs[4][...] += dwv
        accs[5][...] += dgq
        accs[6][...] += dgk

    return rowwise("mla_prep_bwd", fn, z.shape[0], tm=512, ins=_mla_prep_ins(z, tabs, w) + [_row(dq), _row(dk), _row(dv)],
                   outs=[(512, 512, MXU_DTYPE, False)],
                   accs=[(1, 256, 256, False), (1, 128, 128, False), (256, 1024, 1024, False), (128, 1024, 1024, False),
                         (128, 512, 512, False), (1, 128, 128, False), (1, 128, 128, False)])


def _head_masks():
    lane = lax.broadcasted_iota(jnp.int32, (1, LANES), 1)
    return (lane < DIL_HD, lane >= DIL_HD)


def _row_scalar(tile, mask):
    return jnp.max(jnp.where(mask, tile, -jnp.inf), axis=-1, keepdims=True)


def _causal_tiles(nq, by_key):
    pairs = [(i, j) for i in range(nq) for j in range(i + 1)]
    if by_key:
        pairs.sort(key=lambda ij: (ij[1], ij[0]))
    return (jnp.asarray([ij[0] for ij in pairs], jnp.int32), jnp.asarray([ij[1] for ij in pairs], jnp.int32))


def mla_attn_fwd(q, k, vt, z, *, tq=256, gather=None):
    t = q.shape[0]
    nq = t // tq
    it, jt = _causal_tiles(nq, False)
    hps, wq, wv = MLA_HPS, LANES * MLA_HPS, 64 * MLA_HPS
    ws, layer = gather if gather is not None else ([], 0)
    ng = len(ws)
    ngrp, nsteps = MLA_HEADS // hps, int(it.shape[0])

    def body(it_ref, jt_ref, q_ref, k_ref, vt_ref, g_ref, *rest):
        w_refs, rest = rest[:ng], rest[ng:]
        o_ref, y_ref, yt_ref, lse_ref = rest[:4]
        ga_refs, rest = rest[4:4 + ng], rest[4 + ng:]
        m_scr, l_scr, acc_scr = rest[:3]
        step = pl.program_id(1)
        i, j = it_ref[step], jt_ref[step]
        if ng:
            gather_start, gather_finish = _layer_gather(w_refs, ga_refs, rest[3], rest[4], layer)

            @pl.when((pl.program_id(0) == 0) & (step == 0))
            def _():
                gather_start()

        @pl.when(j == 0)
        def _():
            m_scr[...] = jnp.full(m_scr.shape, NEG, F32)
            l_scr[...] = jnp.zeros(l_scr.shape, F32)
            acc_scr[...] = jnp.zeros(acc_scr.shape, F32)

        def update(diagonal):
            heads = range(hps)
            lanes = [slice(LANES * hh, LANES * (hh + 1)) for hh in heads]
            rows = [slice(64 * hh, 64 * (hh + 1)) for hh in heads]
            sts = [_dot(k_ref[:, lanes[hh]], q_ref[:, lanes[hh]], NT) for hh in heads]
            m_prev = [m_scr[hh:hh + 1, :] for hh in heads]
            l_prev = [l_scr[hh:hh + 1, :] for hh in heads]
            acc_prev = [acc_scr[rows[hh], :] for hh in heads]
            m_new, l_new, acc_new = [], [], []
            for hh in heads:
                st = sts[hh]
                if diagonal:
                    key = lax.broadcasted_iota(jnp.int32, (tq, tq), 0)
                    qry = lax.broadcasted_iota(jnp.int32, (tq, tq), 1)
                    st = jnp.where(key <= qry, st, NEG)
                m_new.append(jnp.maximum(m_prev[hh], jnp.max(st, axis=0, keepdims=True)))
                alpha = jnp.exp(m_prev[hh] - m_new[hh])
                pt = jnp.exp(st - m_new[hh])
                l_new.append(alpha * l_prev[hh] + jnp.sum(pt, axis=0, keepdims=True))
                acc_new.append(alpha * acc_prev[hh] + _dot(vt_ref[rows[hh], :], pt, NN))
            for hh in heads:
                m_scr[hh:hh + 1, :] = m_new[hh]
                l_scr[hh:hh + 1, :] = l_new[hh]
                acc_scr[rows[hh], :] = acc_new[hh]

        @pl.when(j < i)
        def _():
            update(False)

        @pl.when(j == i)
        def _():
            update(True)
            lse_ref[...] = jnp.zeros(lse_ref.shape, F32)
            for hh in range(hps):
                rows = slice(64 * hh, 64 * (hh + 1))
                acc_scr[rows, :] = acc_scr[rows, :] / l_scr[hh:hh + 1, :]
                lse_ref[hh:hh + 1, :] = m_scr[hh:hh + 1, :] + jnp.log(l_scr[hh:hh + 1, :])
            o = acc_scr[...].T
            o_ref[...] = o
            y = o * _silu_and_grad(g_ref[...])[0]
            y_ref[...] = y.astype(MXU_DTYPE)
            yt_ref[...] = y.T.astype(MXU_DTYPE)

        if ng:
            @pl.when((pl.program_id(0) == ngrp - 1) & (step == nsteps - 1))
            def _():
                gather_finish()

    qo = lambda w, off=0: pl.BlockSpec((tq, w), lambda p, s, it_, jt_: (it_[s], p + off))
    sds = jax.ShapeDtypeStruct
    comm_scratch = [pltpu.SemaphoreType.DMA((6 * ng,)), pltpu.SemaphoreType.DMA((6 * ng,))] if ng else []
    return pl.pallas_call(
        body,
        name="mla_attn_fwd_gather" if ng else "mla_attn_fwd",
        grid_spec=pltpu.PrefetchScalarGridSpec(
            num_scalar_prefetch=2,
            grid=(ngrp, nsteps),
            in_specs=[qo(wq), pl.BlockSpec((tq, wq), lambda p, s, it_, jt_: (jt_[s], p)),
                      pl.BlockSpec((wv, tq), lambda p, s, it_, jt_: (p, jt_[s])), qo(wv, 512 // wv)] + [HBM_SPEC] * ng,
            out_specs=[qo(wv), qo(wv), pl.BlockSpec((wv, tq), lambda p, s, it_, jt_: (p, it_[s])),
                       pl.BlockSpec((None, 8, tq), lambda p, s, it_, jt_: (p, 0, it_[s]))] + [HBM_SPEC] * ng,
            scratch_shapes=[pltpu.VMEM((8, tq), F32), pltpu.VMEM((8, tq), F32), pltpu.VMEM((wv, tq), F32)] + comm_scratch,
        ),
        out_shape=[sds((t, 512), F32), sds((t, 512), MXU_DTYPE), sds((512, t), MXU_DTYPE), sds((ngrp, 8, t), F32)]
        + _gather_shapes(ws),
        compiler_params=_cparams(("arbitrary", "arbitrary") if ng else ("parallel", "arbitrary")),
    )(it, jt, q, k, vt, z, *ws)


def mla_attn_bwd(q, k, v, do, lse, dd, *, tq=256, chips=None):
    t = q.shape[0]
    nq = t // tq
    it, jt = _causal_tiles(nq, True)
    hps, wq, wv = MLA_HPS, LANES * MLA_HPS, 64 * MLA_HPS
    pcs, layer = chips if chips is not None else ([], 0)
    nc = len(pcs)
    ngrp, nsteps = MLA_HEADS // hps, int(it.shape[0])

    def body(it_ref, jt_ref, q_ref, k_ref, v_ref, do_ref, lse_ref, d_ref, *rest):
        pc_refs, rest = rest[:nc], rest[nc:]
        dq_ref, dk_ref, dv_ref = rest[:3]
        lb_refs, rest = rest[3:3 + nc], rest[3 + nc:]
        dk_scr, dv_scr = rest[:2]
        step = pl.program_id(1)
        i, j = it_ref[step], jt_ref[step]
        masks = _head_masks()
        if nc:
            chips_start, chips_finish = _chips_exchange(pc_refs, lb_refs, rest[2], rest[3], layer)

            @pl.when((pl.program_id(0) == 0) & (step == 0))
            def _():
                chips_start()

        @pl.when(step == 0)
        def _():
            dq_ref[...] = jnp.zeros(dq_ref.shape, F32)

        @pl.when(i == j)
        def _():
            dk_scr[...] = jnp.zeros(dk_scr.shape, F32)
            dv_scr[...] = jnp.zeros(dv_scr.shape, F32)

        def update(diagonal):
            qrows = pl.ds(pl.multiple_of(i * tq, tq), tq)
            heads = range(hps)
            lanes = [slice(LANES * hh, LANES * (hh + 1)) for hh in heads]
            pair = [slice(LANES * (hh // 2), LANES * (hh // 2 + 1)) for hh in heads]
            qh = [q_ref[:, lanes[hh]] for hh in heads]
            kh = [k_ref[:, lanes[hh]] for hh in heads]
            doh = []
            for hh in heads:
                dov = do_ref[:, pair[hh]]
                doh.append(jnp.where(masks[hh % 2], dov, jnp.zeros_like(dov)))
            sts = [_dot(kh[hh], qh[hh], NT) for hh in heads]
            dps = [_dot(v_ref[:, pair[hh]], doh[hh], NT) for hh in heads]
            lse = [lse_ref[hh:hh + 1, :] for hh in heads]
            ddv = [d_ref[hh:hh + 1, :] for hh in heads]
            dk_new = [dk_scr[:, lanes[hh]] for hh in heads]
            dq_new = [dq_ref[qrows, lanes[hh]] for hh in heads]
            dv_new = [dv_scr[:, pair[2 * pp]] for pp in range(hps // 2)]
            for hh in heads:
                st = sts[hh] - lse[hh]
                if diagonal:
                    key = lax.broadcasted_iota(jnp.int32, (tq, tq), 0)
                    qry = lax.broadcasted_iota(jnp.int32, (tq, tq), 1)
                    st = jnp.where(key <= qry, st, NEG)
                pt = jnp.exp(st)
                dst = pt * (dps[hh] - ddv[hh])
                dv_new[hh // 2] = dv_new[hh // 2] + _dot(pt, doh[hh], NN)
                dk_new[hh] = dk_new[hh] + _dot(dst, qh[hh], NN)
                dq_new[hh] = dq_new[hh] + _dot(dst, kh[hh], TN)
            for hh in heads:
                dk_scr[:, lanes[hh]] = dk_new[hh]
                dq_ref[qrows, lanes[hh]] = dq_new[hh]
            for pp in range(hps // 2):
                dv_scr[:, pair[2 * pp]] = dv_new[pp]

        @pl.when(j < i)
        def _():
            update(False)

        @pl.when(j == i)
        def _():
            update(True)

        @pl.when(i == nq - 1)
        def _():
            dk_ref[...] = dk_scr[...]
            dv_ref[...] = dv_scr[...]

        if nc:
            @pl.when((pl.program_id(0) == ngrp - 1) & (step == nsteps - 1))
            def _():
                chips_finish()

    qo = lambda w: pl.BlockSpec((tq, w), lambda p, s, it_, jt_: (it_[s], p))
    kv = lambda w: pl.BlockSpec((tq, w), lambda p, s, it_, jt_: (jt_[s], p))
    stat = pl.BlockSpec((None, 8, tq), lambda p, s, it_, jt_: (p, 0, it_[s]))
    sds = jax.ShapeDtypeStruct
    comm_scratch = [pltpu.SemaphoreType.DMA((3 * nc,)), pltpu.SemaphoreType.DMA((3 * nc,))] if nc else []
    return pl.pallas_call(
        body,
        name="mla_attn_bwd_chips" if nc else "mla_attn_bwd",
        grid_spec=pltpu.PrefetchScalarGridSpec(
            num_scalar_prefetch=2,
            grid=(ngrp, nsteps),
            in_specs=[qo(wq), kv(wq), kv(wv), qo(wv), stat, stat] + [HBM_SPEC] * nc,
            out_specs=[pl.BlockSpec((t, wq), lambda p, s, it_, jt_: (0, p)), kv(wq), kv(wv)] + [HBM_SPEC] * nc,
            scratch_shapes=[pltpu.VMEM((tq, wq), F32), pltpu.VMEM((tq, wv), F32)] + comm_scratch,
        ),
        out_shape=[sds((t, 1024), F32), sds((t, 1024), F32), sds((t, 512), F32)] + _chips_shapes(pcs),
        compiler_params=_cparams(("arbitrary", "arbitrary") if nc else ("parallel", "arbitrary")),
    )(it, jt, q, k, v, do, lse, dd, *pcs)


def mla_gate_bwd(dy, o, z, *, tm=512):
    t = dy.shape[0]
    wv = 64 * MLA_HPS

    def body(dy_ref, o_ref, g_ref, do_ref, dzg_ref, dd_ref):
        dyv, ov = dy_ref[...], o_ref[...]
        silu, dsilu = _silu_and_grad(g_ref[...])
        do = dyv * silu
        do_ref[...] = do.astype(MXU_DTYPE)
        dzg_ref[...] = (dyv * ov * dsilu).astype(MXU_DTYPE)
        prod = do * ov
        row = lax.broadcasted_iota(jnp.int32, (8, wv), 0)
        lane = lax.broadcasted_iota(jnp.int32, (8, wv), 1)
        pick = ((lane >= row * DIL_HD) & (lane < (row + 1) * DIL_HD)).astype(BF16)
        hi = prod.astype(BF16)
        r1 = prod - hi.astype(F32)
        mid = r1.astype(BF16)
        lo = (r1 - mid.astype(F32)).astype(BF16)
        dot = lambda u: lax.dot_general(pick, u, NT, preferred_element_type=F32)
        dd_ref[...] = dot(hi) + dot(mid) + dot(lo)

    blk = lambda off=0: pl.BlockSpec((tm, wv), lambda p, i: (i, p + off))
    sds = jax.ShapeDtypeStruct
    return pl.pallas_call(
        body,
        name="mla_gate_bwd",
        grid=(MLA_HEADS // MLA_HPS, t // tm),
        in_specs=[blk(), blk(), blk(512 // wv)],
        out_specs=[blk(), blk(), pl.BlockSpec((None, 8, tm), lambda p, i: (p, 0, i))],
        out_shape=[sds((t, 512), MXU_DTYPE), sds((t, 512), MXU_DTYPE), sds((MLA_HEADS // MLA_HPS, 8, t), F32)],
        compiler_params=_cparams(("parallel", "parallel")),
    )(dy, o, z)


SPAN = 2048
DIL_SCALE = DIL_HD ** -0.5


def _rm_src(u, window, dil):
    nn, r = divmod(u, dil)
    return pl.ds(nn * window + r, DIL_NK, stride=dil) if dil > 1 else pl.ds(u * DIL_NK, DIL_NK)


def _rm_dst(u):
    return pl.ds(u * DIL_NK, DIL_NK)


def _dil_prep_tile(ops, x, g2, c, s1, s2, scale):
    ms = ops.seg_sum(x * x) * (1.0 / DIL_HD)
    return rope(ops, x * lax.rsqrt(ms + EPS) * g2, c, s1, s2, 32) * scale


def _span_blk(base):
    return pl.BlockSpec((SPAN, LANES), lambda s, p: (s, base + p))


def _const_blk(shape):
    return pl.BlockSpec(shape, lambda s, p: (0,) * len(shape))


_DIL_TABLE_SPECS = [pl.BlockSpec((SPAN, LANES), (lambda s, p, blk=blk: (s, blk))) for blk in (3, 4, 5)]


def dil_prep_fwd(z, tabs, gq2, gk2, gi):
    window, dil = DIL_GROUPS[gi]
    t = z.shape[0]

    def body(q_ref, k_ref, v_ref, c_ref, s1_ref, s2_ref, gq_ref, gk_ref, qo_ref, ko_ref, vo_ref):
        for u in range(SPAN // DIL_NK):
            src, dst = _rm_src(u, window, dil), _rm_dst(u)
            c, s1, s2 = c_ref[src, :], s1_ref[src, :], s2_ref[src, :]
            qo_ref[dst, :] = _dil_prep_tile(PLAIN, q_ref[src, :], gq_ref[...], c, s1, s2, DIL_SCALE).astype(MXU_DTYPE)
            ko_ref[dst, :] = _dil_prep_tile(PLAIN, k_ref[src, :], gk_ref[...], c, s1, s2, 1.0).astype(MXU_DTYPE)
            vo_ref[dst, :] = v_ref[src, :].astype(MXU_DTYPE)

    return pl.pallas_call(
        body,
        name=f"dil_prep_fwd_{dil}",
        grid=(t // SPAN, 4),
        in_specs=[_span_blk(4 * gi), _span_blk(12 + 4 * gi), _span_blk(24 + 4 * gi)] + _DIL_TABLE_SPECS
        + [_const_blk((1, LANES)), _const_blk((1, LANES))],
        out_specs=[_span_blk(0)] * 3,
        out_shape=[jax.ShapeDtypeStruct((t, 512), MXU_DTYPE)] * 3,
        compiler_params=_cparams(("parallel", "parallel")),
    )(z, z, z, tabs, tabs, tabs, gq2, gk2)


def dil_prep_bwd(z, tabs, gq2, gk2, dq, dk, dv, gi):
    window, dil = DIL_GROUPS[gi]
    t = z.shape[0]

    def body(q_ref, k_ref, c_ref, s1_ref, s2_ref, gq_ref, gk_ref, dq_ref, dk_ref, dv_ref, dzq_ref, dzk_ref, dzv_ref, dgq_ref,
             dgk_ref, sq, sk, sv):
        @pl.when((pl.program_id(0) == 0) & (pl.program_id(1) == 0))
        def _():
            dgq_ref[...] = jnp.zeros((1, LANES), F32)
            dgk_ref[...] = jnp.zeros((1, LANES), F32)

        dgs = [jnp.zeros((1, LANES), F32), jnp.zeros((1, LANES), F32)]
        for u in range(SPAN // DIL_NK):
            src, dst = _rm_src(u, window, dil), _rm_dst(u)
            c, s1, s2 = c_ref[src, :], s1_ref[src, :], s2_ref[src, :]
            for idx, (x_ref, g_ref, ct_ref, scr, scale) in enumerate(((q_ref, gq_ref, dq_ref, sq, DIL_SCALE),
                                                                      (k_ref, gk_ref, dk_ref, sk, 1.0))):
                _, vjp = jax.vjp(lambda xv, gv, sc=scale: _dil_prep_tile(DIFF, xv, gv, c, s1, s2, sc), x_ref[src, :], g_ref[...])
                dx, dg = vjp(ct_ref[dst, :])
                scr[src, :] = dx
                dgs[idx] = dgs[idx] + dg
            sv[src, :] = dv_ref[dst, :]
        dgq_ref[...] += dgs[0] + pltpu.roll(dgs[0], DIL_HD, 1)
        dgk_ref[...] += dgs[1] + pltpu.roll(dgs[1], DIL_HD, 1)
        for c0 in range(0, SPAN, 256):
            rows = slice(c0, c0 + 256)
            dzq_ref[rows, :] = sq[rows, :].astype(MXU_DTYPE)
            dzk_ref[rows, :] = sk[rows, :].astype(MXU_DTYPE)
            dzv_ref[rows, :] = sv[rows, :].astype(MXU_DTYPE)

    sds = jax.ShapeDtypeStruct
    return pl.pallas_call(
        body,
        name=f"dil_prep_bwd_{dil}",
        grid=(t // SPAN, 4),
        in_specs=[_span_blk(4 * gi), _span_blk(12 + 4 * gi)] + _DIL_TABLE_SPECS
        + [_const_blk((1, LANES)), _const_blk((1, LANES)), _span_blk(0), _span_blk(0), _span_blk(0)],
        out_specs=[_span_blk(0)] * 3 + [_const_blk((1, LANES))] * 2,
        out_shape=[sds((t, 512), MXU_DTYPE)] * 3 + [sds((1, LANES), F32)] * 2,
        scratch_shapes=[pltpu.VMEM((SPAN, LANES), F32)] * 3,
        compiler_params=_cparams(("arbitrary", "arbitrary")),
    )(z, z, tabs, tabs, tabs, gq2, gk2, dq, dk, dv)


def _band_masks():
    qi = lax.broadcasted_iota(jnp.int32, (DIL_NK, DIL_NK), 0)
    ki = lax.broadcasted_iota(jnp.int32, (DIL_NK, DIL_NK), 1)
    return (ki >= qi), (ki <= qi)


def _pair_heads():
    return [(pair, hh) for pair in range(4) for hh in range(2)]


def _pair_lanes(pair):
    return slice(LANES * pair, LANES * (pair + 1))


def _zero_other_head(mask, x):
    return jnp.where(mask, x, jnp.zeros_like(x))


def dil_attn_fwd(name, q, k, v, dil):
    t = q.shape[0]

    def body(q_ref, kp_ref, kc_ref, vp_ref, vc_ref, o_ref, lse_ref):
        b = pl.program_id(0)
        mprev, mcur = _band_masks()
        mprev = mprev & (b >= dil)
        hm = _head_masks()
        heads = _pair_heads()
        qh = [_zero_other_head(hm[hh], q_ref[:, _pair_lanes(pair)]) for pair, hh in heads]
        sps = [_dot(qh[i], kp_ref[:, _pair_lanes(pair)], NT) for i, (pair, _) in enumerate(heads)]
        scs = [_dot(qh[i], kc_ref[:, _pair_lanes(pair)], NT) for i, (pair, _) in enumerate(heads)]
        o = [jnp.zeros((DIL_NK, LANES), F32) for _ in range(4)]
        lse = [jnp.zeros((DIL_NK, LANES), F32) for _ in range(4)]
        for i, (pair, hh) in enumerate(heads):
            sp, sc = jnp.where(mprev, sps[i], NEG), jnp.where(mcur, scs[i], NEG)
            m = jnp.maximum(jnp.max(sp, axis=1, keepdims=True), jnp.max(sc, axis=1, keepdims=True))
            ep, ec = jnp.exp(sp - m), jnp.exp(sc - m)
            den = jnp.sum(ep, axis=1, keepdims=True) + jnp.sum(ec, axis=1, keepdims=True)
            lanes = _pair_lanes(pair)
            oh = _dot(ep, _zero_other_head(hm[hh], vp_ref[:, lanes]), NN) + _dot(ec, _zero_other_head(hm[hh], vc_ref[:, lanes]), NN)
            o[pair] = o[pair] + oh / den
            lse[pair] = jnp.where(hm[hh], m + jnp.log(den), lse[pair])
        for pair in range(4):
            o_ref[:, _pair_lanes(pair)] = o[pair]
            lse_ref[:, _pair_lanes(pair)] = lse[pair]

    cur = pl.BlockSpec((DIL_NK, 512), lambda b: (b, 0))
    prev = pl.BlockSpec((DIL_NK, 512), lambda b: (jnp.maximum(b - dil, 0), 0))
    sds = jax.ShapeDtypeStruct
    return pl.pallas_call(
        body,
        name=name,
        grid=(t // DIL_NK,),
        in_specs=[cur, prev, cur, prev, cur],
        out_specs=[cur, cur],
        out_shape=[sds((t, 512), F32), sds((t, 512), F32)],
        compiler_params=_cparams(("parallel",)),
    )(q, k, k, v, v)


def dil_attn_bwd(name, q, k, v, do, lse, dd, dil):
    t = q.shape[0]
    nblk = t // DIL_NK

    def body(q_ref, do_ref, l_ref, d_ref, kp_ref, kc_ref, vp_ref, vc_ref, dq_ref, dk_ref, dv_ref):
        b = pl.program_id(0)

        @pl.when(b == 0)
        def _():
            dk_ref[...] = jnp.zeros(dk_ref.shape, F32)
            dv_ref[...] = jnp.zeros(dv_ref.shape, F32)

        mprev, mcur = _band_masks()
        mprev = mprev & (b >= dil)
        rows_c = pl.ds(pl.multiple_of(b * DIL_NK, DIL_NK), DIL_NK)
        rows_p = pl.ds(pl.multiple_of(jnp.maximum(b - dil, 0) * DIL_NK, DIL_NK), DIL_NK)
        hm = _head_masks()
        heads = _pair_heads()
        ln_ = [_pair_lanes(pair) for pair, _ in heads]
        qh = [_zero_other_head(hm[hh], q_ref[:, ln_[i]]) for i, (_, hh) in enumerate(heads)]
        doh = [_zero_other_head(hm[hh], do_ref[:, ln_[i]]) for i, (_, hh) in enumerate(heads)]
        idx = range(len(heads))
        s_p = [_dot(qh[i], kp_ref[:, ln_[i]], NT) for i in idx]
        s_c = [_dot(qh[i], kc_ref[:, ln_[i]], NT) for i in idx]
        dp_p = [_dot(doh[i], vp_ref[:, ln_[i]], NT) for i in idx]
        dp_c = [_dot(doh[i], vc_ref[:, ln_[i]], NT) for i in idx]
        zero = lambda: [jnp.zeros((DIL_NK, LANES), F32) for _ in range(4)]
        dq, dk_p, dk_c, dv_p, dv_c = zero(), zero(), zero(), zero(), zero()
        for i, (pair, hh) in enumerate(heads):
            lse_h, d_h = _row_scalar(l_ref[:, ln_[i]], hm[hh]), _row_scalar(d_ref[:, ln_[i]], hm[hh])
            pp = jnp.exp(jnp.where(mprev, s_p[i] - lse_h, NEG))
            pc = jnp.exp(jnp.where(mcur, s_c[i] - lse_h, NEG))
            dsp, dsc = pp * (dp_p[i] - d_h), pc * (dp_c[i] - d_h)
            dq[pair] = (dq[pair] + _dot(dsp, _zero_other_head(hm[hh], kp_ref[:, ln_[i]]), NN)
                        + _dot(dsc, _zero_other_head(hm[hh], kc_ref[:, ln_[i]]), NN))
            dv_p[pair] = dv_p[pair] + _dot(pp, doh[i], TN)
            dv_c[pair] = dv_c[pair] + _dot(pc, doh[i], TN)
            dk_p[pair] = dk_p[pair] + _dot(dsp, qh[i], TN)
            dk_c[pair] = dk_c[pair] + _dot(dsc, qh[i], TN)
        for pair in range(4):
            lanes = _pair_lanes(pair)
            dq_ref[:, lanes] = dq[pair]
            dk_ref[rows_p, lanes] += dk_p[pair]
            dv_ref[rows_p, lanes] += dv_p[pair]
            dk_ref[rows_c, lanes] += dk_c[pair]
            dv_ref[rows_c, lanes] += dv_c[pair]

    cur = pl.BlockSpec((DIL_NK, 512), lambda b: (b, 0))
    prev = pl.BlockSpec((DIL_NK, 512), lambda b: (jnp.maximum(b - dil, 0), 0))
    whole = pl.BlockSpec((t, 512), lambda b: (0, 0))
    sds = jax.ShapeDtypeStruct
    return pl.pallas_call(
        body,
        name=name,
        grid=(nblk,),
        in_specs=[cur, cur, cur, cur, prev, cur, prev, cur],
        out_specs=[cur, whole, whole],
        out_shape=[sds((t, 512), F32)] * 3,
        compiler_params=_cparams(("arbitrary",)),
    )(q, do, lse, dd, k, k, v, v)


def dil_combine(os_, lses, z):
    t = z.shape[0]

    def body(o1_ref, l1_ref, o2_ref, l2_ref, o3_ref, l3_ref, g_ref, y_ref, o_ref, lse_ref, yt_ref, so2, sl2, so3, sl3):
        for (window, dil), o_in, l_in, so, sl in ((DIL_GROUPS[1], o2_ref, l2_ref, so2, sl2), (DIL_GROUPS[2], o3_ref, l3_ref, so3, sl3)):
            for u in range(SPAN // DIL_NK):
                src, dst = _rm_src(u, window, dil), _rm_dst(u)
                so[src, :] = o_in[dst, :]
                sl[src, :] = l_in[dst, :]
        for c0 in range(0, SPAN, 256):
            rows = slice(c0, c0 + 256)
            la, lb, lc = l1_ref[rows, :], sl2[rows, :], sl3[rows, :]
            mx = jnp.maximum(jnp.maximum(la, lb), lc)
            wa, wb, wc = jnp.exp(la - mx), jnp.exp(lb - mx), jnp.exp(lc - mx)
            tot = wa + wb + wc
            o = (wa * o1_ref[rows, :] + wb * so2[rows, :] + wc * so3[rows, :]) / tot
            y = o * _silu_and_grad(g_ref[rows, :])[0]
            y_ref[rows, :] = y.astype(MXU_DTYPE)
            o_ref[rows, :] = o
            lse_ref[rows, :] = mx + jnp.log(tot)
            yt_ref[:, rows] = y.T.astype(MXU_DTYPE)

    sds = jax.ShapeDtypeStruct
    return pl.pallas_call(
        body,
        name="dil_combine",
        grid=(t // SPAN, 4),
        in_specs=[_span_blk(0)] * 6 + [_span_blk(36)],
        out_specs=[_span_blk(0)] * 3 + [pl.BlockSpec((LANES, SPAN), lambda s, p: (p, s))],
        out_shape=[sds((t, 512), MXU_DTYPE), sds((t, 512), F32), sds((t, 512), F32), sds((512, t), MXU_DTYPE)],
        scratch_shapes=[pltpu.VMEM((SPAN, LANES), F32)] * 4,
        compiler_params=_cparams(("parallel", "parallel")),
    )(os_[0], lses[0], os_[1], lses[1], os_[2], lses[2], z)


def dil_gate_bwd(dy, o, z, lse):
    t = dy.shape[0]

    def body(dy_ref, o_ref, g_ref, lse_ref, dzg_ref, do1_ref, dd1_ref, do2_ref, dd2_ref, l2_ref, do3_ref, dd3_ref, l3_ref, do_scr):
        for c0 in range(0, SPAN, 256):
            rows = slice(c0, c0 + 256)
            dyv, ov = dy_ref[rows, :], o_ref[rows, :]
            silu, dsilu = _silu_and_grad(g_ref[rows, :])
            do = dyv * silu
            do_scr[rows, :] = do
            do1_ref[rows, :] = do.astype(MXU_DTYPE)
            dd1_ref[rows, :] = _seg_sum_impl(do * ov)
            dzg_ref[rows, :] = (dyv * ov * dsilu).astype(MXU_DTYPE)
        for (window, dil), do_o, dd_o, l_o in ((DIL_GROUPS[1], do2_ref, dd2_ref, l2_ref), (DIL_GROUPS[2], do3_ref, dd3_ref, l3_ref)):
            for u in range(SPAN // DIL_NK):
                src, dst = _rm_src(u, window, dil), _rm_dst(u)
                do_o[dst, :] = do_scr[src, :].astype(MXU_DTYPE)
                dd_o[dst, :] = dd1_ref[src, :]
                l_o[dst, :] = lse_ref[src, :]

    sds = jax.ShapeDtypeStruct
    f32, mxu = sds((t, 512), F32), sds((t, 512), MXU_DTYPE)
    return pl.pallas_call(
        body,
        name="dil_gate_bwd",
        grid=(t // SPAN, 4),
        in_specs=[_span_blk(0), _span_blk(0), _span_blk(36), _span_blk(0)],
        out_specs=[_span_blk(0)] * 9,
        out_shape=[mxu, mxu, f32, mxu, f32, f32, mxu, f32, f32],
        scratch_shapes=[pltpu.VMEM((SPAN, LANES), F32)],
        compiler_params=_cparams(("parallel", "parallel")),
    )(dy, o, z, lse)


def _merge_tile(p0, p1, p2, z0, z1, z2, b0, b1, b2):
    return _sigmoid(z0 + b0) * p0 + _sigmoid(z1 + b1) * p1 + _sigmoid(z2 + b2) * p2


def _merge_ins(ps, z, b):
    w = 256
    return ([_row(p, w, 0, True) for p in ps] + [_row(z, w, 4 * i, True) for i in range(3)]
            + [_const(b, w, 4 * i, True) for i in range(3)])


def merge_fwd(ps, z, b):
    def fn(ins, outs, _):
        merged = _merge_tile(*[r[...] for r in ins])
        outs[0][...] = merged.astype(MXU_DTYPE)
        outs[1][...] = merged.T.astype(MXU_DTYPE)

    return rowwise("merge_fwd", fn, z.shape[0], tm=1024, ncol=4, ins=_merge_ins(ps, z, b), outs=[(1024, 256, MXU_DTYPE, True)],
                   touts=[(1024, 256, MXU_DTYPE, True)])


def merge_bwd(dm, ps, z, b):
    def fn(ins, outs, accs):
        _, vjp = jax.vjp(_merge_tile, *[r[...] for r in ins[:9]])
        grads = vjp(ins[9][...])
        for i in range(3):
            outs[i][...] = grads[i].astype(MXU_DTYPE)
            outs[3 + i][...] = grads[3 + i].astype(MXU_DTYPE)
            accs[i][...] += grads[6 + i]

    return rowwise("merge_bwd", fn, z.shape[0], tm=1024, ncol=4, ins=_merge_ins(ps, z, b) + [_row(dm, 256, 0, True)],
                   outs=[(1024, 256, MXU_DTYPE, True)] * 6, accs=[(1, 1024, 256, True)] * 3)


EW_BLOCK_BYTES = 2**21


def _tile2d(r, c):
    if r * c * 4 <= EW_BLOCK_BYTES:
        return r, c
    for tr in (512, 256, 128, 64, 32):
        if r % tr == 0 and tr * c * 4 <= EW_BLOCK_BYTES:
            return tr, c
    for tc in (1024, 512, 256, 128):
        if c % tc == 0 and r * tc * 4 <= EW_BLOCK_BYTES:
            return r, tc
    raise ValueError((r, c))


def adamw(name, w, g, m, v):
    shape = w.shape
    c = shape[-1]
    r = w.size // c
    tr, tc = _tile2d(r, c)
    assert tc == c
    c1, c2 = 1.0 - ADAM_B1 ** ADAM_STEP, 1.0 - ADAM_B2 ** ADAM_STEP

    def body(w_ref, g_ref, m_ref, v_ref, d_ref, mo_ref, vo_ref):
        gv = g_ref[...]
        mn = ADAM_B1 * m_ref[...] + (1.0 - ADAM_B1) * gv
        vn = ADAM_B2 * v_ref[...] + (1.0 - ADAM_B2) * (gv * gv)
        d_ref[...] = -ADAM_LR * ((mn / c1) / (jnp.sqrt(vn / c2) + ADAM_EPS) + ADAM_WD * w_ref[...])
        mo_ref[...] = mn
        vo_ref[...] = vn

    spec = pl.BlockSpec((tr, c), lambda i: (i, 0))
    outs = pl.pallas_call(
        body,
        name=name,
        grid=(r // tr,),
        in_specs=[spec] * 4,
        out_specs=[spec] * 3,
        out_shape=[jax.ShapeDtypeStruct((r, c), F32)] * 3,
        compiler_params=_cparams(("parallel",)),
    )(*[a.reshape(r, c) for a in (w, g, m, v)])
    return [o.reshape(shape) for o in outs]


def adamw_pair(name, w, reduced, received, m, v, c_idx):
    _, r, c = w.shape
    tr, tc = _tile2d(r, c)
    c1, c2 = 1.0 - ADAM_B1 ** ADAM_STEP, 1.0 - ADAM_B2 ** ADAM_STEP

    def body(c_ref, w_ref, r0_ref, x0_ref, r1_ref, x1_ref, m_ref, v_ref, g_ref, d_ref, mo_ref, vo_ref):
        mine = c_ref[0]
        gv = jnp.where(pl.program_id(0) == 0, jnp.where(mine == 0, r0_ref[...], x0_ref[...]),
                       jnp.where(mine == 1, r1_ref[...], x1_ref[...]))
        mn = ADAM_B1 * m_ref[...] + (1.0 - ADAM_B1) * gv
        vn = ADAM_B2 * v_ref[...] + (1.0 - ADAM_B2) * (gv * gv)
        g_ref[...] = gv
        d_ref[...] = -ADAM_LR * ((mn / c1) / (jnp.sqrt(vn / c2) + ADAM_EPS) + ADAM_WD * w_ref[...])
        mo_ref[...] = mn
        vo_ref[...] = vn

    full = pl.BlockSpec((None, tr, tc), lambda h, i, j, cr: (h, i, j))
    half = pl.BlockSpec((tr, tc), lambda h, i, j, cr: (i, j))
    return pl.pallas_call(
        body,
        name=name,
        grid_spec=pltpu.PrefetchScalarGridSpec(
            num_scalar_prefetch=1,
            grid=(2, r // tr, c // tc),
            in_specs=[full, half, half, half, half, full, full],
            out_specs=[full] * 4,
        ),
        out_shape=[jax.ShapeDtypeStruct((2, r, c), F32)] * 4,
        compiler_params=_cparams(("parallel", "parallel", "parallel")),
    )(c_idx, w, reduced[0], received[0], reduced[1], received[1], m, v)


def sum_pair(name, g, la, out_dtype):
    _, r, c = g.shape
    tr, tc = _tile2d(r, c)

    def body(g_ref, la_ref, o_ref):
        o_ref[...] = (g_ref[...] + la_ref[...]).astype(o_ref.dtype)

    spec = pl.BlockSpec((None, tr, tc), lambda s, i, j: (s, i, j))
    return pl.pallas_call(
        body,
        name=name,
        grid=(4, r // tr, c // tc),
        in_specs=[spec, spec],
        out_specs=spec,
        out_shape=jax.ShapeDtypeStruct((4, r, c), out_dtype),
        compiler_params=_cparams(("parallel", "parallel", "parallel")),
    )(g, la)


def sum_chips(name, g, la, lb, s_idx):
    _, r, c = g.shape
    tr, tc = _tile2d(r, c)

    def body(s_ref, g_ref, la_ref, l0_ref, l1_ref, l2_ref, o_ref):
        own = g_ref[...] + la_ref[...]
        o_ref[...] = ((own + l0_ref[...].astype(F32)) + l1_ref[...].astype(F32)) + l2_ref[...].astype(F32)

    own_spec = pl.BlockSpec((None, tr, tc), lambda i, j, sr: (sr[0], i, j))
    lspec = lambda k: pl.BlockSpec((None, tr, tc), lambda i, j, sr: (k, i, j))
    return pl.pallas_call(
        body,
        name=name,
        grid_spec=pltpu.PrefetchScalarGridSpec(
            num_scalar_prefetch=1,
            grid=(r // tr, c // tc),
            in_specs=[own_spec, own_spec, lspec(0), lspec(1), lspec(2)],
            out_specs=pl.BlockSpec((tr, tc), lambda i, j, sr: (i, j)),
        ),
        out_shape=jax.ShapeDtypeStruct((r, c), F32),
        compiler_params=_cparams(("parallel", "parallel")),
    )(s_idx, g, la, lb, lb, lb)


def _place():
    x, y, c = lax.axis_index("x"), lax.axis_index("y"), lax.axis_index("c")
    chips = [(1 - x, y), (x, 1 - y), (1 - x, 1 - y)]
    return x, y, c, chips


HBM_SPEC = pl.BlockSpec(memory_space=pltpu.HBM)


def _comm_call(name, body, ins, out_shapes, n_sem, n_local):
    return pl.pallas_call(
        body,
        name=name,
        in_specs=[HBM_SPEC] * len(ins),
        out_specs=[HBM_SPEC] * len(out_shapes),
        out_shape=out_shapes,
        scratch_shapes=[pltpu.SemaphoreType.DMA((n_sem,)), pltpu.SemaphoreType.DMA((n_sem,)),
                        pltpu.SemaphoreType.DMA((max(n_local, 1),))],
    )(*ins)


def _layer_gather(ins, outs, send, recv, layer):
    n = len(ins)
    x, y, c, chips = _place()
    s = 2 * x + y
    sib = (x, y, 1 - c)
    active = c == layer

    def rc(a, k, src, dst, dev):
        return pltpu.make_async_remote_copy(src_ref=src, dst_ref=dst, send_sem=send.at[6 * a + k], recv_sem=recv.at[6 * a + k],
                                            device_id=dev, device_id_type=MESH)

    def first_hop():
        return [rc(a, j, ins[a], outs[a].at[s], (*chip, c)) for j, chip in enumerate(chips) for a in range(n)]

    def start():
        @pl.when(active)
        def _():
            for cp in first_hop():
                cp.start()

    def finish():
        @pl.when(active)
        def _():
            forwards = []
            for j, (cx, cy) in enumerate(chips):
                for a in range(n):
                    landed = outs[a].at[2 * cx + cy]
                    rc(a, j, landed, landed, sib).wait_recv()
                    forwards.append(rc(a, 3 + j, landed, landed, sib))
                    forwards[-1].start()
            for cp in first_hop() + forwards:
                cp.wait_send()

        @pl.when(jnp.logical_not(active))
        def _():
            for j, (cx, cy) in enumerate(chips):
                for a in range(n):
                    other = outs[a].at[2 * cx + cy]
                    rc(a, 3 + j, other, other, sib).wait_recv()

    return start, finish


def _gather_shapes(ws):
    return [jax.ShapeDtypeStruct((4,) + w.shape, w.dtype) for w in ws]


def gather_hook(ws, layer):
    return (ws, _gather_shapes(ws), 6 * len(ws), lambda i, o, s, r: _layer_gather(i, o, s, r, layer))


def allgather_layer(ws, layer):
    n = len(ws)

    def body(*refs):
        send, recv, _ = refs[2 * n:]
        start, finish = _layer_gather(refs[:n], refs[n:2 * n], send, recv, layer)
        start()
        finish()

    return _comm_call(f"allgather_layer{layer}", body, ws, _gather_shapes(ws), 6 * n, 0)


def exchange_sibling(gs, layer, tag=""):
    n = len(gs)

    def body(*refs):
        send, recv, _ = refs[2 * n:]
        start, finish = _sibling_exchange(refs[:n], refs[n:2 * n], send, recv, layer)
        start()
        finish()

    return _comm_call(f"exchange_sibling{layer}{tag}", body, gs, [jax.ShapeDtypeStruct(g.shape, g.dtype) for g in gs], n, 0)


def _sibling_exchange(ins, outs, send, recv, layer):
    x, y, c, _ = _place()

    def copies():
        return [pltpu.make_async_remote_copy(src_ref=ins[a], dst_ref=outs[a], send_sem=send.at[a], recv_sem=recv.at[a],
                                             device_id=(x, y, 1 - c), device_id_type=MESH) for a in range(len(ins))]

    def start():
        @pl.when(c != layer)
        def _():
            for cp in copies():
                cp.start()

    def finish():
        @pl.when(c != layer)
        def _():
            for cp in copies():
                cp.wait_send()

        @pl.when(c == layer)
        def _():
            for cp in copies():
                cp.wait_recv()

    return start, finish


def sibling_hook(gs, layer):
    return (gs, [jax.ShapeDtypeStruct(g.shape, g.dtype) for g in gs], len(gs),
            lambda i, o, s, r: _sibling_exchange(i, o, s, r, layer))


def chips_hook(ps, layer):
    return (ps, _chips_shapes(ps), 3 * len(ps), lambda i, o, s, r: _chips_exchange(i, o, s, r, layer))


def _chips_exchange(ins, outs, send, recv, layer):
    n = len(ins)
    _, _, c, chips = _place()

    def copies():
        return [pltpu.make_async_remote_copy(src_ref=ins[a].at[2 * cx + cy], dst_ref=outs[a].at[j], send_sem=send.at[3 * a + j],
                                             recv_sem=recv.at[3 * a + j], device_id=(cx, cy, c), device_id_type=MESH)
                for j, (cx, cy) in enumerate(chips) for a in range(n)]

    def start():
        @pl.when(c == layer)
        def _():
            for cp in copies():
                cp.start()

    def finish():
        @pl.when(c == layer)
        def _():
            for cp in copies():
                cp.wait()

    return start, finish


def _chips_shapes(ps):
    return [jax.ShapeDtypeStruct((3,) + p.shape[1:], p.dtype) for p in ps]


def exchange_chips(ps, layer, tag=""):
    n = len(ps)

    def body(*refs):
        send, recv, _ = refs[2 * n:]
        start, finish = _chips_exchange(refs[:n], refs[n:2 * n], send, recv, layer)
        start()
        finish()

    return _comm_call(f"exchange_chips{layer}{tag}", body, ps, _chips_shapes(ps), 3 * n, 0)


def exchange_final(rs, layer, small=None):
    n = len(rs)
    ins_all = list(rs) + ([small] if small is not None else [])

    def body(*refs):
        ni = len(ins_all)
        ins, outs = refs[:ni], refs[ni:2 * ni]
        send, recv, lsem = refs[2 * ni:]
        x, y, c, _ = _place()
        s = 2 * x + y
        to_sib = [pltpu.make_async_remote_copy(src_ref=ins[a], dst_ref=outs[a], send_sem=send.at[a], recv_sem=recv.at[a],
                                               device_id=(x, y, 1 - c), device_id_type=MESH) for a in range(n)]
        rel = [(fx, fy) for fx in (0, 1) for fy in (0, 1)]

        def piece(fx, fy, cc, dst_slot):
            k = n + 2 * (2 * fx + fy) + cc
            return pltpu.make_async_remote_copy(src_ref=ins[n], dst_ref=outs[n].at[dst_slot], send_sem=send.at[k], recv_sem=recv.at[k],
                                                device_id=(x ^ fx, y ^ fy, cc), device_id_type=MESH)

        @pl.when(c == layer)
        def _():
            for cp in to_sib:
                cp.start()
            if small is not None:
                own = pltpu.make_async_copy(ins[n], outs[n].at[s], lsem.at[0])
                own.start()
                sends = [piece(fx, fy, cc, s) for fx, fy in rel for cc in (0, 1) if (fx, fy) != (0, 0) or cc != layer]
                for cp in sends:
                    cp.start()
                for fx, fy in rel[1:]:
                    piece(fx, fy, layer, 2 * (x ^ fx) + (y ^ fy)).wait_recv()
                for cp in sends:
                    cp.wait_send()
                own.wait()
            for cp in to_sib:
                cp.wait_send()

        @pl.when(c != layer)
        def _():
            for cp in to_sib:
                cp.wait_recv()
            if small is not None:
                for fx, fy in rel:
                    piece(fx, fy, 1 - layer, 2 * (x ^ fx) + (y ^ fy)).wait_recv()

    out_shapes = [jax.ShapeDtypeStruct(r.shape, r.dtype) for r in rs]
    if small is not None:
        out_shapes.append(jax.ShapeDtypeStruct((4,) + small.shape, small.dtype))
    res = _comm_call(f"exchange_final{layer}", body, ins_all, out_shapes, n + 8, 1)
    return (res[:n], res[n]) if small is not None else (res[:n], None)


def _pad_in_cols(w):
    z = lambda n: jnp.zeros(w.shape[:-1] + (n,), w.dtype)
    return jnp.concatenate([w[..., :KR_OFF], z(64), w[..., KR_OFF:KR_OFF + MLA_ROPE], z(32), w[..., KR_OFF + MLA_ROPE:]], axis=-1)


def _pad_heads(w, real):
    k = w.shape[0]
    return jnp.pad(w.reshape(k, MLA_HEADS, real), ((0, 0), (0, 0), (0, LANES - real))).reshape(k, MLA_HEADS * LANES)


def _pad_gain(g, real):
    return jnp.pad(g.reshape(1, real), ((0, 0), (0, LANES - real)))


def layer_weights(full):
    w_ukv = full["w_ukv"].reshape(128, MLA_HEADS, 2, 64)
    two = lambda g: jnp.concatenate([g, g]).reshape(1, LANES)
    return dict(
        norm_g=full["norm_g"].reshape(1, -1), w_in=full["w_in"], conv_w=full["conv_w"], conv_b=full["conv_b"].reshape(1, -1),
        wgx=full["w_gate_x"], bgx=full["b_gate_x"].reshape(LRU_BLOCKS, 1, LANES),
        wga=full["w_gate_a"], bga=full["b_gate_a"].reshape(LRU_BLOCKS, 1, LANES),
        lam=full["lru_lambda"].reshape(1, -1), w_lru_o=full["w_lru_o"],
        cq_norm_g=full["cq_norm_g"].reshape(1, -1), ckv_norm_g=full["ckv_norm_g"].reshape(1, -1),
        wq=_pad_heads(full["w_uq"], MLA_QK), wk=_pad_heads(w_ukv[:, :, 0].reshape(128, 512), 64),
        wv=w_ukv[:, :, 1].reshape(128, 512),
        gq=_pad_gain(full["mla_q_norm_g"], MLA_QK), gk=_pad_gain(full["mla_k_norm_g"], MLA_QK),
        w_mla_o=full["w_mla_o"], gq2=two(full["dil_q_norm_g"]), gk2=two(full["dil_k_norm_g"]),
        w_dil_o=full["w_dil_o"], b_merge=full["b_merge"].reshape(1, -1), w_out=full["w_out"],
    )


def layer_fwd(x, w, tabs, gather=None):
    h, ht = rmsnorm_fwd(x, w["norm_g"])
    z_lru = mm_nn("in_proj_lru", h, w["w_in"], n=2048, n_off=G_LRU, tm=1024)
    z_mla = mm_nn("in_proj_mla", h, w["w_in"], n=1024, n_off=G_MLA, tm=1024)
    z_dil = mm_nn("in_proj_dil", h, w["w_in"], n=5120, n_off=G_DIL, tm=1024)
    z_mrg = mm_nn("in_proj_mrg", h, w["w_in"], n=3072, n_off=G_MRG, tm=1024)
    rest_hook = None if gather is None else gather_hook(gather[0][1:], gather[1])
    hs, y_lru, yt_lru, *ga_rest = lru_fwd(z_lru, w["conv_w"], w["conv_b"], w["wgx"], w["bgx"], w["wga"], w["bga"], w["lam"],
                                          comm=rest_hook)
    qm, km, vm, vtm = mla_prep_fwd(z_mla, tabs, w)
    o_mla, y_mla, yt_mla, lse_mla, *ga_first = mla_attn_fwd(qm, km, vtm, z_mla,
                                                            gather=None if gather is None else (gather[0][:1], gather[1]))
    gathered = ga_first + ga_rest
    os_, lses, dil_rm = [], [], []
    for gi, (_, dil) in enumerate(DIL_GROUPS):
        qkv = dil_prep_fwd(z_dil, tabs, w["gq2"], w["gk2"], gi)
        o, lse = dil_attn_fwd(f"dil_attn_fwd_{dil}", *qkv, dil)
        os_.append(o)
        lses.append(lse)
        dil_rm.append(qkv)
    y_dil, o_dil, lse_dil, yt_dil = dil_combine(os_, lses, z_dil)
    ps = [mm_nn("proj_lru", y_lru, w["w_lru_o"]), mm_nn("proj_mla", y_mla, w["w_mla_o"]), mm_nn("proj_dil", y_dil, w["w_dil_o"])]
    merged, merged_t = merge_fwd(ps, z_mrg, w["b_merge"])
    out = mm_nn("out_proj", merged, w["w_out"], add=x)
    res = dict(x=x, ht=ht, z_lru=z_lru, z_mla=z_mla, z_dil=z_dil, z_mrg=z_mrg, hs=hs, yt_lru=yt_lru, qm=qm, km=km, vm=vm, o_mla=o_mla,
               yt_mla=yt_mla, lse_mla=lse_mla, dil_rm=dil_rm, yt_dil=yt_dil, o_dil=o_dil, lse_dil=lse_dil, ps=ps, merged_t=merged_t)
    return out, res, gathered


def layer_bwd(dout, r, w, tabs, prev=None, own=None):
    g = {}
    dmerged = mm_nt("out_proj_dx", dout, w["w_out"])
    g["w_out"] = mm_nn("out_proj_dw", r["merged_t"], dout, tm=1024, tn=512, tk=2048)
    dp0, dp1, dp2, dzm0, dzm1, dzm2, db0, db1, db2 = merge_bwd(dmerged, r["ps"], r["z_mrg"], w["b_merge"])
    g["b_merge"] = jnp.concatenate([db0, db1, db2], axis=1).reshape(-1)
    dy_lru = mm_nt("proj_lru_dx", dp0, w["w_lru_o"])
    dy_mla = mm_nt("proj_mla_dx", dp1, w["w_mla_o"])
    dy_dil = mm_nt("proj_dil_dx", dp2, w["w_dil_o"])
    g["w_lru_o"] = mm_nn("proj_lru_dw", r["yt_lru"], dp0, **DW_TILES)
    g["w_mla_o"] = mm_nn("proj_mla_dw", r["yt_mla"], dp1, **DW_TILES)
    g["w_dil_o"] = mm_nn("proj_dil_dw", r["yt_dil"], dp2, **DW_TILES)
    dzx, dzg_lru, dwgx, dbgx, dwga, dbga, dlam, dcw, dcb, *la_prev = lru_bwd(
        r["z_lru"], r["hs"], dy_lru, w["conv_w"], w["conv_b"], w["wgx"], w["bgx"], w["wga"], w["bga"], w["lam"],
        comm=None if prev is None else sibling_hook(prev[0], prev[1]))
    chips = None if prev is None else (prev[2](la_prev), prev[1])
    g.update(w_gate_x=dwgx, b_gate_x=dbgx.reshape(LRU_BLOCKS, LANES), w_gate_a=dwga, b_gate_a=dbga.reshape(LRU_BLOCKS, LANES),
             lru_lambda=dlam.reshape(-1), conv_w=dcw, conv_b=dcb.reshape(-1))
    do_m, dzg_mla, dd_m = mla_gate_bwd(dy_mla, r["o_mla"], r["z_mla"])
    dq_m, dk_m, dv_m, *lb_prev = mla_attn_bwd(r["qm"], r["km"], r["vm"], do_m, r["lse_mla"], dd_m, chips=chips)
    dz_mla3, dg_cq, dg_ckv, dwq, dwk, dwv, dgq, dgk = mla_prep_bwd(r["z_mla"], tabs, w, dq_m, dk_m, dv_m)
    g.update(cq_norm_g=dg_cq.reshape(-1), ckv_norm_g=dg_ckv.reshape(-1), mla_q_norm_g=dgq[0, :MLA_QK], mla_k_norm_g=dgk[0, :MLA_QK])
    g["w_uq"] = dwq.reshape(256, MLA_HEADS, LANES)[:, :, :MLA_QK].reshape(256, MLA_HEADS * MLA_QK)
    g["w_ukv"] = jnp.concatenate([dwk.reshape(128, MLA_HEADS, LANES)[:, :, :64], dwv.reshape(128, MLA_HEADS, 64)], axis=2).reshape(128, 1024)
    dzg_dil, do1, dd1, do2, dd2, l2, do3, dd3, l3 = dil_gate_bwd(dy_dil, r["o_dil"], r["z_dil"], r["lse_dil"])
    stats = [(do1, r["lse_dil"], dd1), (do2, l2, dd2), (do3, l3, dd3)]
    dzq, dzk, dzv, dgq2, dgk2 = [], [], [], [], []
    for gi, (_, dil) in enumerate(DIL_GROUPS):
        dq, dk, dv = dil_attn_bwd(f"dil_attn_bwd_{dil}", *r["dil_rm"][gi], *stats[gi], dil)
        parts = dil_prep_bwd(r["z_dil"], tabs, w["gq2"], w["gk2"], dq, dk, dv, gi)
        for acc, part in zip((dzq, dzk, dzv, dgq2, dgk2), parts):
            acc.append(part)
    g.update(dil_q_norm_g=sum(dgq2)[0, :DIL_HD], dil_k_norm_g=sum(dgk2)[0, :DIL_HD])
    dz = jnp.concatenate([dzx, dzg_lru, dz_mla3, dzg_mla] + dzq + dzk + dzv + [dzg_dil, dzm0, dzm1, dzm2], axis=1)
    g["w_in"] = in_proj_dw(r["ht"], dz)
    own_hook = None if own is None else chips_hook(*own(g))
    out = mm_nt("in_proj_dx", dz, w["w_in"], tm=1024, tn=1024, tk=IN_PAD // 4, comm=own_hook)
    dh, lb_own = (out, []) if own_hook is None else (out[0], out[1:])
    dx, dng = rmsnorm_bwd(r["x"], dh, dout, w["norm_g"])
    g["norm_g"] = dng.reshape(-1)
    return dx, g, dict(la_prev=la_prev, lb_prev=lb_prev, lb_own=lb_own)


def local_step(x, positions, target, full0, full1=None, gather1=None, reduce=None):
    tabs = rope_tables(positions.reshape(-1, 1))
    ws, ress = [], []
    for l in range(2):
        if l == 0:
            ws.append(layer_weights(full0))
            x, res, gathered = layer_fwd(x, ws[0], tabs, gather=None if gather1 is None else (gather1[0], 1))
        else:
            ws.append(layer_weights(full1 if gather1 is None else gather1[1](gathered)))
            x, res, _ = layer_fwd(x, ws[1], tabs)
        ress.append(res)
    dy, loss = loss_head(x, target)
    dy, grads1, _ = layer_bwd(dy, ress[1], ws[1], tabs)
    prev1, own0 = (None, None) if reduce is None else (reduce[0](grads1), reduce[1])
    dy, grads0, landed = layer_bwd(dy, ress[0], ws[0], tabs, prev=prev1, own=own0)
    return loss, dy, [grads0, grads1], landed


WEIGHTS = ["norm_g", "w_in", "conv_w", "conv_b", "w_gate_x", "b_gate_x", "w_gate_a", "b_gate_a", "lru_lambda", "w_lru_o", "cq_norm_g",
           "ckv_norm_g", "w_uq", "w_ukv", "mla_q_norm_g", "mla_k_norm_g", "w_mla_o", "dil_q_norm_g", "dil_k_norm_g", "w_dil_o", "b_merge",
           "w_out"]
SHARDED = {"w_in": 2, "conv_w": 2, "w_lru_o": 1, "w_uq": 2, "w_ukv": 2, "w_mla_o": 2, "w_dil_o": 2, "w_out": 1}
REPLICATED = [n for n in WEIGHTS if n not in SHARDED]
SMALL_ROWS = 144


def kernel(x, positions, norm_g, w_in, conv_w, conv_b, w_gate_x, b_gate_x, w_gate_a, b_gate_a, lru_lambda, w_lru_o, cq_norm_g, ckv_norm_g, w_uq, w_ukv, mla_q_norm_g, mla_k_norm_g, w_mla_o, dil_q_norm_g, dil_k_norm_g, w_dil_o, b_merge, w_out, loss_target, m_norm_g, m_w_in, m_conv_w, m_conv_b, m_w_gate_x, m_b_gate_x, m_w_gate_a, m_b_gate_a, m_lru_lambda, m_w_lru_o, m_cq_norm_g, m_ckv_norm_g, m_w_uq, m_w_ukv, m_mla_q_norm_g, m_mla_k_norm_g, m_w_mla_o, m_dil_q_norm_g, m_dil_k_norm_g, m_w_dil_o, m_b_merge, m_w_out, v_norm_g, v_w_in, v_conv_w, v_conv_b, v_w_gate_x, v_b_gate_x, v_w_gate_a, v_b_gate_a, v_lru_lambda, v_w_lru_o, v_cq_norm_g, v_ckv_norm_g, v_w_uq, v_w_ukv, v_mla_q_norm_g, v_mla_k_norm_g, v_w_mla_o, v_dil_q_norm_g, v_dil_k_norm_g, v_w_dil_o, v_b_merge, v_w_out):
    args = locals()
    w = {n: args[n] for n in WEIGHTS}
    m = {n: args["m_" + n] for n in WEIGHTS}
    v = {n: args["v_" + n] for n in WEIGHTS}
    my_c = lax.axis_index("c").astype(jnp.int32)
    my_s = (2 * lax.axis_index("x") + lax.axis_index("y")).astype(jnp.int32)
    c_idx = my_c.reshape(1)
    s_idx = my_s.reshape(1)

    names = list(SHARDED)
    wire = [[w[n][l] if n == "conv_w" else w[n][l].astype(BF16) for n in names] for l in range(2)]

    def assemble(l, gathered):
        full = {n: w[n][l] for n in REPLICATED}
        for n, ga, own in zip(names, gathered, wire[l]):
            parts = [jnp.where(my_s == s, own, ga[s]) for s in range(4)]
            if n == "w_in":
                parts = [_pad_in_cols(parts[0])] + parts[1:]
            full[n] = jnp.concatenate(parts, axis=SHARDED[n] - 1)
        return full

    full0 = assemble(0, allgather_layer(wire[0], 0))

    def shards_of(grads_l):
        gs = []
        for n in names:
            g_ = grads_l[n]
            if n == "w_in":
                gs.append(g_.reshape(4, IN_WIDTH // 4, D_MODEL))
            else:
                parts = jnp.stack(jnp.split(g_, 4, axis=SHARDED[n] - 1))
                gs.append(parts.reshape(4, -1, parts.shape[-1]))
        return gs

    def pair_sums(l, gs, la, tags):
        dts = [BF16 if g_.size >= 2**19 and t != "small" else F32 for g_, t in zip(gs, tags)]
        return [sum_pair(f"sum_pair{l}_{t}", g_, l_, dt) for t, g_, l_, dt in zip(tags, gs, la, dts)]

    def chip_sums(l, gs, la, lb, tags):
        return [sum_chips(f"sum_chips{l}_{t}", g_, l_, b_, s_idx) for t, g_, l_, b_ in zip(tags, gs, la, lb)]

    stash = {}

    def prev1(grads1):
        stash["gs1"] = shards_of(grads1)
        return stash["gs1"], 1, lambda la: pair_sums(1, stash["gs1"], la, names)

    def own0(grads0):
        stash["gs0"] = shards_of(grads0)
        stash["la0"] = exchange_sibling(stash["gs0"], 0)
        return pair_sums(0, stash["gs0"], stash["la0"], names), 0

    loss, grad_x, grads, landed = local_step(x[0], positions[0], loss_target[0], full0,
                                             gather1=(wire[1], functools.partial(assemble, 1)), reduce=(prev1, own0))
    loss = lax.psum(loss, ("x", "y", "c"))
    reduced1 = chip_sums(1, stash["gs1"], landed["la_prev"], landed["lb_prev"], names)
    received1, _ = exchange_final(reduced1, 1)
    reduced0 = chip_sums(0, stash["gs0"], stash["la0"], landed["lb_own"], names)

    flat = jnp.concatenate([jnp.stack([grads[0][n], grads[1][n]]).reshape(-1) for n in REPLICATED])
    small = [jnp.pad(flat, (0, 4 * SMALL_ROWS * 1024 - flat.size)).reshape(4, SMALL_ROWS, 1024)]
    la_s = exchange_sibling(small, 0, tag="_small")
    lb_s = exchange_chips(pair_sums(0, small, la_s, ["small"]), 0, tag="_small")
    received0, small_all = exchange_final(reduced0, 0, small=chip_sums(0, small, la_s, lb_s, ["small"])[0])

    g_local, delta, new_m, new_v = {}, {}, {}, {}
    for i, n in enumerate(names):
        shp = w[n].shape
        if n == "w_in":
            as3 = lambda a: a.transpose(0, 2, 1)
            back = lambda o: o.transpose(0, 2, 1)
        else:
            as3 = lambda a, rc=reduced0[i].shape: a.reshape((2,) + rc)
            back = lambda o, shp=shp: o.reshape(shp)
        outs = adamw_pair(f"adamw_{n}", as3(w[n]), (reduced0[i], reduced1[i]), (received0[i], received1[i]), as3(m[n]), as3(v[n]),
                          c_idx)
        g_local[n], delta[n], new_m[n], new_v[n] = [back(o) for o in outs]
    flat = small_all.reshape(-1)
    off = 0
    for n in REPLICATED:
        g_local[n] = flat[off:off + w[n].size].reshape(w[n].shape)
        off += w[n].size
        delta[n], new_m[n], new_v[n] = adamw(f"adamw_{n}", w[n], g_local[n], m[n], v[n])
    return (loss, grad_x[None], *[g_local[n] for n in WEIGHTS], *[delta[n] for n in WEIGHTS], *[new_m[n] for n in WEIGHTS],
            *[new_v[n] for n in WEIGHTS])
```

```python
import functools

import jax
import jax.numpy as jnp
from jax import lax
from jax.experimental import pallas as pl
from jax.experimental.pallas import tpu as pltpu

F32 = jnp.float32
BF16 = jnp.bfloat16
MXU_DTYPE = jnp.bfloat16

D_MODEL = 1024
EPS = 1e-6
ROPE_THETA = 10000.0
LRU_BLOCKS = 8
LRU_C = 8.0
MLA_HEADS = 8
MLA_NOPE = 64
MLA_ROPE = 32
MLA_QK = 96
DIL_GROUPS = ((128, 1), (512, 4), (2048, 16))
DIL_HD = 64
DIL_NK = 128
MLA_HPS = 8
IN_WIDTH = 11168
ADAM_LR, ADAM_B1, ADAM_B2, ADAM_EPS, ADAM_WD, ADAM_STEP = 0.001, 0.9, 0.999, 1e-08, 0.01, 10

LANES = 128
G_LRU, G_MLA, G_DIL, G_MRG = 0, 2048, 3072, 8192
IN_PAD = 11264
KR_OFF = 2432

NN = (((1,), (0,)), ((), ()))
NT = (((1,), (1,)), ((), ()))
TN = (((0,), (0,)), ((), ()))
NEG = -1e30
MESH = pl.DeviceIdType.MESH
VMEM_LIMIT = 48 * 2**20


def _cparams(sem):
    return pltpu.CompilerParams(dimension_semantics=sem, vmem_limit_bytes=VMEM_LIMIT)


def _dot(a, b, dims):
    return lax.dot_general(a.astype(MXU_DTYPE), b.astype(MXU_DTYPE), dims, preferred_element_type=F32)


@jax.custom_vjp
def mm(a, w):
    return _dot(a, w, NN)


def _mm_fwd(a, w):
    return _dot(a, w, NN), (a, w)


def _mm_bwd(res, g):
    a, w = res
    return _dot(g, w, NT), _dot(a, g, TN)


mm.defvjp(_mm_fwd, _mm_bwd)


def _seg64_matrix():
    r = lax.broadcasted_iota(jnp.int32, (LANES, LANES), 0) < DIL_HD
    c = lax.broadcasted_iota(jnp.int32, (LANES, LANES), 1) < DIL_HD
    return (r == c).astype(BF16)


def _seg_sum_impl(x):
    b = _seg64_matrix()
    hi = x.astype(BF16)
    r1 = x - hi.astype(F32)
    mid = r1.astype(BF16)
    lo = (r1 - mid.astype(F32)).astype(BF16)
    dot = lambda u: lax.dot_general(u, b, NN, preferred_element_type=F32)
    return dot(hi) + dot(mid) + dot(lo)


@jax.custom_vjp
def seg_sum(x):
    return _seg_sum_impl(x)


seg_sum.defvjp(lambda x: (_seg_sum_impl(x), None), lambda _, g: (_seg_sum_impl(g),))


def _lroll_impl(x, s):
    return pltpu.roll(x, s % LANES, 1)


@functools.partial(jax.custom_vjp, nondiff_argnums=(1,))
def lroll(x, s):
    return _lroll_impl(x, s)


lroll.defvjp(lambda x, s: (_lroll_impl(x, s), None), lambda s, _, g: (_lroll_impl(g, -s),))


class _Ops:
    def __init__(self, diff):
        self.mm = mm if diff else (lambda a, w: _dot(a, w, NN))
        self.seg_sum = seg_sum if diff else _seg_sum_impl
        self.lroll = lroll if diff else _lroll_impl


PLAIN, DIFF = _Ops(False), _Ops(True)


def rms(x, g, n):
    ms = jnp.sum(x * x, axis=-1, keepdims=True) * (1.0 / n)
    return x * lax.rsqrt(ms + EPS) * g


def rope(ops, x, c, s1, s2, half):
    return x * c + ops.lroll(x, -half) * s1 + ops.lroll(x, half) * s2


def _sigmoid(x):
    return 1.0 / (1.0 + jnp.exp(-x))


def _silu_and_grad(g):
    sg = _sigmoid(g)
    return g * sg, sg * (1.0 + g * (1.0 - sg))


def _softplus(x):
    return jnp.maximum(x, 0.0) + jnp.log(1.0 + jnp.exp(-jnp.abs(x)))


def _expm1(y):
    series = y * (1.0 + y * (0.5 + y * (1.0 / 6.0 + y * (1.0 / 24.0 + y * (1.0 / 120.0)))))
    return jnp.where(y > -0.05, series, jnp.exp(jnp.minimum(y, -0.05)) - 1.0)


@jax.custom_vjp
def sigmoid_d(x):
    return _sigmoid(x)


def _sigmoid_d_fwd(x):
    s = _sigmoid(x)
    return s, s


sigmoid_d.defvjp(_sigmoid_d_fwd, lambda s, g: (g * s * (1.0 - s),))


def _decay_mult(log_a):
    return jnp.sqrt(-_expm1(2.0 * log_a))


@jax.custom_vjp
def decay_mult_d(log_a):
    return _decay_mult(log_a)


def _decay_mult_d_fwd(log_a):
    em = _expm1(2.0 * log_a)
    mult = jnp.sqrt(-em)
    return mult, (em, mult)


decay_mult_d.defvjp(_decay_mult_d_fwd, lambda res, g: (-g * (res[0] + 1.0) / res[1],))
PLAIN.sigmoid, DIFF.sigmoid = _sigmoid, sigmoid_d
PLAIN.decay_mult, DIFF.decay_mult = _decay_mult, decay_mult_d


def _host(body, n_in, n_out, n_scr, hook, grid):
    if hook is None:
        return body, [], [], [], [], []
    ins, out_shapes, n_sem, make = hook
    ni, no = len(ins), len(out_shapes)
    hbm = pl.BlockSpec(memory_space=pltpu.HBM)

    def at(corner):
        cond = None
        for axis, size in enumerate(grid):
            here = pl.program_id(axis) == (size - 1 if corner else 0)
            cond = here if cond is None else cond & here
        return cond

    def hosted(*refs):
        a, refs = refs[:n_in], refs[n_in:]
        xi, refs = refs[:ni], refs[ni:]
        o, refs = refs[:n_out], refs[n_out:]
        xo, refs = refs[:no], refs[no:]
        scr, (send, recv) = refs[:n_scr], refs[n_scr:]
        start, finish = make(xi, xo, send, recv)
        pl.when(at(False))(start)
        body(*a, *o, *scr)
        pl.when(at(True))(finish)

    sems = [pltpu.SemaphoreType.DMA((n_sem,)), pltpu.SemaphoreType.DMA((n_sem,))]
    return hosted, list(ins), [hbm] * ni, [hbm] * no, list(out_shapes), sems


def _mm_call(name, a, b, *, mode, m, n, k, a_blk, b_blk, out_dtype, add, tm, tn, tk, comm=None):
    nk = k // tk
    assert m % tm == 0 and n % tn == 0 and k % tk == 0, (name, m, n, k, tm, tn, tk)
    dims = {"nn": NN, "nt": NT, "tn": TN}[mode]

    def body(*refs):
        if add is None:
            a_ref, b_ref, o_ref, *scr = refs
            add_ref = None
        else:
            a_ref, b_ref, add_ref, o_ref, *scr = refs
        part = _dot(a_ref[...], b_ref[...], dims)

        def finish(acc):
            if add_ref is not None:
                acc = acc + add_ref[...]
            o_ref[...] = acc.astype(o_ref.dtype)

        if nk == 1:
            finish(part)
        else:
            (acc_ref,) = scr
            kk = pl.program_id(2)

            @pl.when(kk == 0)
            def _():
                acc_ref[...] = part

            @pl.when(kk > 0)
            def _():
                acc_ref[...] += part

            @pl.when(kk == nk - 1)
            def _():
                finish(acc_ref[...])

    in_specs = [a_blk, b_blk]
    args = [a, b]
    if add is not None:
        in_specs.append(pl.BlockSpec((tm, tn), lambda j, i, kk: (i, j)))
        args.append(add)
    grid = (n // tn, m // tm, nk)
    scratch = [] if nk == 1 else [pltpu.VMEM((tm, tn), F32)]
    body, x_in, x_in_specs, x_out_specs, x_out_shapes, sems = _host(body, len(args), 1, len(scratch), comm, grid)
    res = pl.pallas_call(
        body,
        name=name,
        grid=grid,
        in_specs=in_specs + x_in_specs,
        out_specs=[pl.BlockSpec((tm, tn), lambda j, i, kk: (i, j))] + x_out_specs,
        out_shape=[jax.ShapeDtypeStruct((m, n), out_dtype)] + x_out_shapes,
        scratch_shapes=scratch + sems,
        compiler_params=_cparams(("arbitrary",) * 3 if comm is not None else ("parallel", "parallel", "arbitrary")),
    )(*args, *x_in)
    return res[0] if comm is None else res


def mm_nn(name, a, b, *, n=None, n_off=0, out_dtype=F32, add=None, tm=1024, tn=1024, tk=1024):
    m, k = a.shape
    n = b.shape[1] if n is None else n
    tm, tn, tk = min(tm, m), min(tn, n), min(k, tk)
    ob = n_off // tn
    assert n_off % tn == 0
    return _mm_call(name, a, b, mode="nn", m=m, n=n, k=k, out_dtype=out_dtype, add=add, tm=tm, tn=tn, tk=tk,
                    a_blk=pl.BlockSpec((tm, tk), lambda j, i, kk: (i, kk)),
                    b_blk=pl.BlockSpec((tk, tn), lambda j, i, kk: (kk, j + ob)))


def mm_nt(name, a, b, *, out_dtype=F32, tm=1024, tn=1024, tk=1024, comm=None):
    m, k = a.shape
    n = b.shape[0]
    tm, tn, tk = min(tm, m), min(tn, n), min(k, tk)
    return _mm_call(name, a, b, mode="nt", m=m, n=n, k=k, out_dtype=out_dtype, add=None, tm=tm, tn=tn, tk=tk, comm=comm,
                    a_blk=pl.BlockSpec((tm, tk), lambda j, i, kk: (i, kk)),
                    b_blk=pl.BlockSpec((tn, tk), lambda j, i, kk: (j, kk)))


DW_TILES = dict(tm=1024, tn=512, tk=4096)


def in_proj_dw(ht, dz):
    t = ht.shape[1]
    tn = 512
    nchunk = IN_PAD // tn
    kr_chunk = KR_OFF // tn
    lo = KR_OFF - kr_chunk * tn

    def body(ht_ref, dz_ref, out_ref, buf, sems):
        j = pl.program_id(0)
        slot = j % 2
        buf[slot] = _dot(ht_ref[...], dz_ref[...], NN).T

        def copies(c, s):
            dst = jnp.where(c < kr_chunk, c * tn, (c - 1) * tn + lo + MLA_ROPE)
            plain = [pltpu.make_async_copy(buf.at[s], out_ref.at[pl.ds(pl.multiple_of(dst, 8), tn)], sems.at[s, 0])]
            split = [pltpu.make_async_copy(buf.at[s, 0:lo], out_ref.at[KR_OFF - lo:KR_OFF], sems.at[s, 0]),
                     pltpu.make_async_copy(buf.at[s, lo + 64:lo + 96], out_ref.at[KR_OFF:KR_OFF + MLA_ROPE], sems.at[s, 1])]
            return plain, split

        def each(c, s, action):
            plain, split = copies(c, s)

            @pl.when(c != kr_chunk)
            def _():
                for cp in plain:
                    action(cp)

            @pl.when(c == kr_chunk)
            def _():
                for cp in split:
                    action(cp)

        @pl.when(j > 0)
        def _():
            each(j - 1, 1 - slot, lambda cp: cp.wait())

        each(j, slot, lambda cp: cp.start())

        @pl.when(j == nchunk - 1)
        def _():
            each(j, slot, lambda cp: cp.wait())

    return pl.pallas_call(
        body,
        name="in_proj_dw",
        grid=(nchunk,),
        in_specs=[pl.BlockSpec((D_MODEL, t), lambda j: (0, 0)), pl.BlockSpec((t, tn), lambda j: (0, j))],
        out_specs=pl.BlockSpec(memory_space=pltpu.HBM),
        out_shape=jax.ShapeDtypeStruct((IN_WIDTH, D_MODEL), F32),
        scratch_shapes=[pltpu.VMEM((2, tn, D_MODEL), F32), pltpu.SemaphoreType.DMA((2, 2))],
        compiler_params=_cparams(("arbitrary",)),
    )(ht, dz)


def rowwise(name, fn, t, *, tm=256, ncol=1, ins=(), outs=(), touts=(), accs=()):
    n_in, n_out, n_acc = len(ins), len(outs) + len(touts), len(accs)

    def zero_when(ref, cond):
        @pl.when(cond)
        def _():
            ref[...] = jnp.zeros(ref.shape, ref.dtype)

    def body(*refs):
        in_refs, out_refs, acc_refs = refs[:n_in], refs[n_in:n_in + n_out], refs[n_in + n_out:]
        j, i = pl.program_id(0), pl.program_id(1)
        for ref, (_, _, _, cd) in zip(acc_refs, accs):
            zero_when(ref, (i == 0) if cd else ((i == 0) & (j == 0)))
        fn(in_refs, out_refs, acc_refs)

    def in_spec(arr, w, base, cd, rd):
        rows = tm if rd else arr.shape[0]
        return pl.BlockSpec((rows, w), lambda j, i: (i if rd else 0, base + (j if cd else 0)))

    in_specs = [in_spec(*e) for e in ins]
    out_specs = [pl.BlockSpec((tm, w), (lambda j, i, cd=cd: (i, j if cd else 0))) for (_, w, _, cd) in outs]
    out_specs += [pl.BlockSpec((w, tm), (lambda j, i, cd=cd: (j if cd else 0, i))) for (_, w, _, cd) in touts]
    out_specs += [pl.BlockSpec((r, w), (lambda j, i, cd=cd: (0, j if cd else 0))) for (r, _, w, cd) in accs]
    out_shape = [jax.ShapeDtypeStruct((t, c), dt) for (c, _, dt, _) in outs]
    out_shape += [jax.ShapeDtypeStruct((r, t), dt) for (r, _, dt, _) in touts]
    out_shape += [jax.ShapeDtypeStruct((r, c), F32) for (r, c, _, _) in accs]
    res = pl.pallas_call(
        body,
        name=name,
        grid=(ncol, t // tm),
        in_specs=in_specs,
        out_specs=out_specs,
        out_shape=out_shape,
        compiler_params=_cparams(("arbitrary", "arbitrary") if accs else ("parallel", "parallel")),
    )(*[e[0] for e in ins])
    return res


def _row(arr, w=None, base=0, cd=False):
    return (arr, arr.shape[1] if w is None else w, base, cd, True)


def _const(arr, w=None, base=0, cd=False):
    return (arr, arr.shape[1] if w is None else w, base, cd, False)


def rope_tables(positions):
    t = positions.shape[0]

    def fn(ins, outs, _):
        pos = ins[0][...].astype(F32)
        lane = lax.broadcasted_iota(jnp.int32, (1, LANES), 1)
        log_theta = jnp.log(jnp.float32(ROPE_THETA))
        jm = lane - MLA_NOPE
        idx = jnp.clip(jnp.where(jm < 16, jm, jm - 16), 0, 15).astype(F32)
        ang = pos * jnp.exp(-(idx * (2.0 / MLA_ROPE)) * log_theta)
        cos, sin = jnp.cos(ang), jnp.sin(ang)
        in_rope = (lane >= MLA_NOPE) & (lane < MLA_QK)
        outs[0][:, 0:128] = jnp.where(lane < MLA_NOPE, 1.0, jnp.where(in_rope, cos, 0.0))
        outs[0][:, 128:256] = jnp.where(in_rope & (jm < 16), -sin, 0.0)
        outs[0][:, 256:384] = jnp.where(in_rope & (jm >= 16), sin, 0.0)
        jd = lane & (DIL_HD - 1)
        idx = (jd & 31).astype(F32)
        ang = pos * jnp.exp(-(idx * (2.0 / DIL_HD)) * log_theta)
        cos, sin = jnp.cos(ang), jnp.sin(ang)
        outs[0][:, 384:512] = cos
        outs[0][:, 512:640] = jnp.where(jd < 32, -sin, 0.0)
        outs[0][:, 640:768] = jnp.where(jd >= 32, sin, 0.0)

    return rowwise("rope_tables", fn, t, tm=1024, ins=[_row(positions)], outs=[(768, 768, F32, False)])[0]


def rmsnorm_fwd(x, g):
    def fn(ins, outs, _):
        h = rms(ins[0][...], ins[1][...], D_MODEL)
        outs[0][...] = h.astype(MXU_DTYPE)
        outs[1][...] = h.T.astype(MXU_DTYPE)

    return rowwise("rmsnorm_fwd", fn, x.shape[0], tm=512, ins=[_row(x), _const(g)], outs=[(D_MODEL, D_MODEL, MXU_DTYPE, False)],
                   touts=[(D_MODEL, D_MODEL, MXU_DTYPE, False)])


def rmsnorm_bwd(x, dh, dres, g):
    def fn(ins, outs, accs):
        _, vjp = jax.vjp(lambda xv, gv: rms(xv, gv, D_MODEL), ins[0][...], ins[3][...])
        dx, dg = vjp(ins[1][...])
        outs[0][...] = ins[2][...] + dx
        accs[0][...] += dg

    return rowwise("rmsnorm_bwd", fn, x.shape[0], tm=512, ins=[_row(x), _row(dh), _row(dres), _const(g)],
                   outs=[(D_MODEL, D_MODEL, F32, False)], accs=[(1, D_MODEL, D_MODEL, False)])


def loss_head(y, target):
    def fn(ins, outs, accs):
        err = ins[0][...] - ins[1][...]
        outs[0][...] = err * (1.0 / D_MODEL)
        accs[0][...] += jnp.sum(err * err, axis=0, keepdims=True)
        accs[1][...] = jnp.broadcast_to(jnp.sum(accs[0][...], keepdims=True), (1, LANES))

    dy, _, tot = rowwise("loss_head", fn, y.shape[0], tm=512, ins=[_row(y), _row(target)], outs=[(D_MODEL, D_MODEL, F32, False)],
                         accs=[(1, D_MODEL, D_MODEL, False), (1, LANES, LANES, False)])
    return dy, tot[0, 0] * (0.5 / D_MODEL)


def _shift_rows(v, d, fill, reverse):
    tb = v.shape[0]
    if d % 8 == 0:
        pad = jnp.full((d, v.shape[1]), fill, v.dtype)
        return jnp.concatenate([v[d:], pad] if reverse else [pad, v[:tb - d]], axis=0)
    rows = lax.broadcasted_iota(jnp.int32, v.shape, 0)
    if not reverse:
        return jnp.where(rows >= d, pltpu.roll(v, d, 0), fill)
    return jnp.where(rows < tb - d, pltpu.roll(v, tb - d, 0), fill)


def _scan_tile(a, b, reverse):
    d = 1
    while d < a.shape[0]:
        b = b + a * _shift_rows(b, d, 0.0, reverse)
        a = a * _shift_rows(a, d, 1.0, reverse)
        d *= 2
    return a, b


def _lru_gates(ops, xc, wgx, bgx, wga, bga, lam):
    gx = ops.sigmoid(ops.mm(xc, wgx) + bgx)
    ga = ops.sigmoid(ops.mm(xc, wga) + bga)
    log_a = -LRU_C * ga * _softplus(-lam)
    a = jnp.exp(log_a)
    return a, ops.decay_mult(log_a) * (gx * xc)


def _shifted_inputs(x, halo, tb):
    rows = lax.broadcasted_iota(jnp.int32, x.shape, 0)
    pad = jnp.zeros((tb - 8, LANES), F32)
    out = []
    for d in (3, 2, 1):
        head = jnp.concatenate([pltpu.roll(halo, d, 0), pad], axis=0)
        out.append(jnp.where(rows >= d, pltpu.roll(x, d, 0), head))
    return out + [x]


def _lru_specs(nt, tb, reverse):
    hb = tb // 8
    tt = (lambda t: nt - 1 - t) if reverse else (lambda t: t)
    blk = lambda off: pl.BlockSpec((tb, LANES), lambda n, t: (tt(t), n + off))
    halo = lambda off: pl.BlockSpec((8, LANES), lambda n, t: (jnp.maximum(tt(t) * hb - 1, 0), n + off))
    chan = lambda r: pl.BlockSpec((r, LANES), lambda n, t: (0, n))
    wblk = pl.BlockSpec((None, LANES, LANES), lambda n, t: (n, 0, 0))
    bblk = pl.BlockSpec((None, 1, LANES), lambda n, t: (n, 0, 0))
    return blk, halo, chan, wblk, bblk


def lru_fwd(z, conv_w, conv_b, wgx, bgx, wga, bga, lam, *, tb=256, comm=None):
    t = z.shape[0]
    nt = t // tb
    blk, halo, chan, wblk, bblk = _lru_specs(nt, tb, False)

    def body(x_ref, xh_ref, g_ref, cw_ref, cb_ref, wgx_ref, bgx_ref, wga_ref, bga_ref, lam_ref, h_ref, y_ref, yt_ref, carry_ref):
        ti = pl.program_id(1)

        @pl.when(ti == 0)
        def _():
            carry_ref[...] = jnp.zeros((8, LANES), F32)

        x = x_ref[...]
        hal = jnp.where(ti > 0, xh_ref[...], 0.0)
        xs = _shifted_inputs(x, hal, tb)
        xc = cb_ref[...] + sum(xs[k] * cw_ref[k:k + 1, :] for k in range(4))
        a, b = _lru_gates(PLAIN, xc, wgx_ref[...], bgx_ref[...], wga_ref[...], bga_ref[...], lam_ref[...])
        acum, h0 = _scan_tile(a, b, False)
        h = h0 + acum * carry_ref[7:8, :]
        carry_ref[...] = h[tb - 8:tb, :]
        h_ref[...] = h
        y = h * _silu_and_grad(g_ref[...])[0]
        y_ref[...] = y.astype(MXU_DTYPE)
        yt_ref[...] = y.T.astype(MXU_DTYPE)

    grid = (LRU_BLOCKS, nt)
    body, x_in, x_in_specs, x_out_specs, x_out_shapes, sems = _host(body, 10, 3, 1, comm, grid)
    return pl.pallas_call(
        body,
        name="lru_fwd" if comm is None else "lru_fwd_comm",
        grid=grid,
        in_specs=[blk(0), halo(0), blk(LRU_BLOCKS), chan(4), chan(1), wblk, bblk, wblk, bblk, chan(1)] + x_in_specs,
        out_specs=[blk(0), blk(0), pl.BlockSpec((LANES, tb), lambda n, t_: (n, t_))] + x_out_specs,
        out_shape=[jax.ShapeDtypeStruct((t, 1024), F32), jax.ShapeDtypeStruct((t, 1024), MXU_DTYPE),
                   jax.ShapeDtypeStruct((1024, t), MXU_DTYPE)] + x_out_shapes,
        scratch_shapes=[pltpu.VMEM((8, LANES), F32)] + sems,
        compiler_params=_cparams(("arbitrary", "arbitrary") if comm is not None else ("parallel", "arbitrary")),
    )(z, z, z, conv_w, conv_b, wgx, bgx, wga, bga, lam, *x_in)


def lru_bwd(z, h, dy, conv_w, conv_b, wgx, bgx, wga, bga, lam, *, tb=256, comm=None):
    t = z.shape[0]
    nt = t // tb
    blk, halo, chan, wblk, bblk = _lru_specs(nt, tb, True)

    def body(x_ref, xh_ref, g_ref, h_ref, hh_ref, dy_ref, cw_ref, cb_ref, wgx_ref, bgx_ref, wga_ref, bga_ref, lam_ref,
             dzx_ref, dzg_ref, dwgx_ref, dbgx_ref, dwga_ref, dbga_ref, dlam_ref, dcw_ref, dcb_ref,
             gcar_ref, acar_ref, xcar_ref):
        ti = pl.program_id(1)
        has_earlier = ti < nt - 1

        @pl.when(ti == 0)
        def _():
            for ref in (dwgx_ref, dbgx_ref, dwga_ref, dbga_ref, dlam_ref, dcw_ref, dcb_ref, gcar_ref, acar_ref, xcar_ref):
                ref[...] = jnp.zeros(ref.shape, F32)

        rows = lax.broadcasted_iota(jnp.int32, (tb, LANES), 0)
        x = x_ref[...]
        hal = jnp.where(has_earlier, xh_ref[...], 0.0)
        xs = _shifted_inputs(x, hal, tb)
        xc = cb_ref[...] + sum(xs[k] * cw_ref[k:k + 1, :] for k in range(4))
        (a, _), vjp = jax.vjp(functools.partial(_lru_gates, DIFF), xc, wgx_ref[...], bgx_ref[...], wga_ref[...],
                              bga_ref[...], lam_ref[...])
        g, h, dyv = g_ref[...], h_ref[...], dy_ref[...]
        silu, dsilu = _silu_and_grad(g)
        dzg_ref[...] = (dyv * h * dsilu).astype(MXU_DTYPE)
        a_next = jnp.where(rows < tb - 1, pltpu.roll(a, tb - 1, 0), acar_ref[0:1, :])
        acum, g0 = _scan_tile(a_next, dyv * silu, True)
        gt = g0 + acum * gcar_ref[0:1, :]
        h_prev = jnp.where(rows >= 1, pltpu.roll(h, 1, 0), jnp.where(has_earlier, hh_ref[7:8, :], 0.0))
        dxc, dwgx, dbgx, dwga, dbga, dlam = vjp((gt * h_prev, gt))
        later = xcar_ref[...]
        gcar_ref[...] = gt[0:8, :]
        acar_ref[...] = a[0:8, :]
        xcar_ref[...] = dxc[0:8, :]
        dx = dxc * cw_ref[3:4, :]
        pad = jnp.zeros((tb - 8, LANES), F32)
        for d in (1, 2, 3):
            tail = jnp.concatenate([pad, pltpu.roll(later, 8 - d, 0)], axis=0)
            up = jnp.where(rows < tb - d, pltpu.roll(dxc, tb - d, 0), tail)
            dx = dx + up * cw_ref[3 - d:4 - d, :]
        dzx_ref[...] = dx.astype(MXU_DTYPE)
        for k in range(4):
            dcw_ref[k:k + 1, :] += jnp.sum(dxc * xs[k], axis=0, keepdims=True)
        dcb_ref[...] += jnp.sum(dxc, axis=0, keepdims=True)
        dwgx_ref[...] += dwgx
        dbgx_ref[...] += dbgx
        dwga_ref[...] += dwga
        dbga_ref[...] += dbga
        dlam_ref[...] += dlam

    sds = jax.ShapeDtypeStruct
    grid = (LRU_BLOCKS, nt)
    body, x_in, x_in_specs, x_out_specs, x_out_shapes, sems = _host(body, 13, 9, 3, comm, grid)
    return pl.pallas_call(
        body,
        name="lru_bwd" if comm is None else "lru_bwd_comm",
        grid=grid,
        in_specs=[blk(0), halo(0), blk(LRU_BLOCKS), blk(0), halo(0), blk(0), chan(4), chan(1), wblk, bblk, wblk, bblk, chan(1)]
        + x_in_specs,
        out_specs=[blk(0), blk(0), wblk, bblk, wblk, bblk, chan(1), chan(4), chan(1)] + x_out_specs,
        out_shape=[sds((t, 1024), MXU_DTYPE), sds((t, 1024), MXU_DTYPE), sds(wgx.shape, F32), sds(bgx.shape, F32), sds(wga.shape, F32),
                   sds(bga.shape, F32), sds((1, 1024), F32), sds((4, 1024), F32), sds((1, 1024), F32)] + x_out_shapes,
        scratch_shapes=[pltpu.VMEM((8, LANES), F32)] * 3 + sems,
        compiler_params=_cparams(("arbitrary", "arbitrary") if comm is not None else ("parallel", "arbitrary")),
    )(z, z, z, h, h, dy, conv_w, conv_b, wgx, bgx, wga, bga, lam, *x_in)


def _mla_prep_tile(ops, cq, ckv, krp, c, s1, s2, g_cq, g_ckv, wq, wk, wv, gq, gk):
    cqn = rms(cq, g_cq, 256)
    ckvn = rms(ckv, g_ckv, 128)
    v = ops.mm(ckvn, wv)
    qs, ks = [], []
    for hd in range(MLA_HEADS):
        q = rms(ops.mm(cqn, wq[hd]), gq, MLA_QK)
        k = rms(ops.mm(ckvn, wk[hd]) + krp, gk, MLA_QK)
        qs.append(rope(ops, q, c, s1, s2, 16) * (MLA_QK ** -0.5))
        ks.append(rope(ops, k, c, s1, s2, 16))
    return tuple(qs), tuple(ks), v


def _mla_prep_args(ins):
    z_cq, z_ckv, z_kr, tc, ts1, ts2, g_cq, g_ckv, wq, wk, wv, gq, gk = ins[:13]
    heads = lambda w: tuple(w[:, LANES * hd:LANES * (hd + 1)] for hd in range(MLA_HEADS))
    return (z_cq[...], z_ckv[...], z_kr[...], tc[...], ts1[...], ts2[...], g_cq[...], g_ckv[...], heads(wq), heads(wk),
            wv[...], gq[...], gk[...])


def _mla_prep_ins(z, tabs, w):
    return [_row(z, 256, 0), _row(z, 128, 2), _row(z, 128, 3), _row(tabs, 128, 0), _row(tabs, 128, 1), _row(tabs, 128, 2),
            _const(w["cq_norm_g"]), _const(w["ckv_norm_g"]), _const(w["wq"]), _const(w["wk"]), _const(w["wv"]),
            _const(w["gq"]), _const(w["gk"])]


def mla_prep_fwd(z, tabs, w):
    def fn(ins, outs, _):
        qs, ks, v = _mla_prep_tile(PLAIN, *_mla_prep_args(ins))
        for hd in range(MLA_HEADS):
            outs[0][:, LANES * hd:LANES * (hd + 1)] = qs[hd].astype(MXU_DTYPE)
            outs[1][:, LANES * hd:LANES * (hd + 1)] = ks[hd].astype(MXU_DTYPE)
        outs[2][...] = v.astype(MXU_DTYPE)
        outs[3][...] = v.T.astype(MXU_DTYPE)

    return rowwise("mla_prep_fwd", fn, z.shape[0], tm=512, ins=_mla_prep_ins(z, tabs, w),
                   outs=[(1024, 1024, MXU_DTYPE, False), (1024, 1024, MXU_DTYPE, False), (512, 512, MXU_DTYPE, False)],
                   touts=[(512, 512, MXU_DTYPE, False)])


def mla_prep_bwd(z, tabs, w, dq, dk, dv):
    def fn(ins, outs, accs):
        args = _mla_prep_args(ins)
        _, vjp = jax.vjp(functools.partial(_mla_prep_tile, DIFF), *args)
        heads = lambda ref: tuple(ref[:, LANES * hd:LANES * (hd + 1)] for hd in range(MLA_HEADS))
        dcq, dckv, dkr, _, _, _, dg_cq, dg_ckv, dwq, dwk, dwv, dgq, dgk = vjp((heads(ins[13]), heads(ins[14]), ins[15][...]))
        lane = lax.broadcasted_iota(jnp.int32, (1, LANES), 1)
        outs[0][:, 0:256] = dcq.astype(MXU_DTYPE)
        outs[0][:, 256:384] = dckv.astype(MXU_DTYPE)
        outs[0][:, 384:512] = jnp.where((lane >= MLA_NOPE) & (lane < MLA_QK), dkr, 0.0).astype(MXU_DTYPE)
        accs[0][...] += dg_cq
        accs[1][...] += dg_ckv
        for hd in range(MLA_HEADS):
            accs[2][:, LANES * hd:LANES * (hd + 1)] += dwq[hd]
            accs[3][:, LANES * hd:LANES * (hd + 1)] += dwk[hd]
        accs[4][...] += dwv
        accs[5][...] += dgq
        accs[6][...] += dgk

    return rowwise("mla_prep_bwd", fn, z.shape[0], tm=512, ins=_mla_prep_ins(z, tabs, w) + [_row(dq), _row(dk), _row(dv)],
                   outs=[(512, 512, MXU_DTYPE, False)],
                   accs=[(1, 256, 256, False), (1, 128, 128, False), (256, 1024, 1024, False), (128, 1024, 1024, False),
                         (128, 512, 512, False), (1, 128, 128, False), (1, 128, 128, False)])


def _head_masks():
    lane = lax.broadcasted_iota(jnp.int32, (1, LANES), 1)
    return (lane < DIL_HD, lane >= DIL_HD)


def _row_scalar(tile, mask):
    return jnp.max(jnp.where(mask, tile, -jnp.inf), axis=-1, keepdims=True)


def _causal_tiles(nq, by_key):
    pairs = [(i, j) for i in range(nq) for j in range(i + 1)]
    if by_key:
        pairs.sort(key=lambda ij: (ij[1], ij[0]))
    return (jnp.asarray([ij[0] for ij in pairs], jnp.int32), jnp.asarray([ij[1] for ij in pairs], jnp.int32))


def mla_attn_fwd(q, k, vt, z, *, tq=256, gather=None):
    t = q.shape[0]
    nq = t // tq
    it, jt = _causal_tiles(nq, False)
    hps, wq, wv = MLA_HPS, LANES * MLA_HPS, 64 * MLA_HPS
    ws, layer = gather if gather is not None else ([], 0)
    ng = len(ws)
    ngrp, nsteps = MLA_HEADS // hps, int(it.shape[0])

    def body(it_ref, jt_ref, q_ref, k_ref, vt_ref, g_ref, *rest):
        w_refs, rest = rest[:ng], rest[ng:]
        o_ref, y_ref, yt_ref, lse_ref = rest[:4]
        ga_refs, rest = rest[4:4 + ng], rest[4 + ng:]
        m_scr, l_scr, acc_scr = rest[:3]
        step = pl.program_id(1)
        i, j = it_ref[step], jt_ref[step]
        if ng:
            gather_start, gather_finish = _layer_gather(w_refs, ga_refs, rest[3], rest[4], layer)

            @pl.when((pl.program_id(0) == 0) & (step == 0))
            def _():
                gather_start()

        @pl.when(j == 0)
        def _():
            m_scr[...] = jnp.full(m_scr.shape, NEG, F32)
            l_scr[...] = jnp.zeros(l_scr.shape, F32)
            acc_scr[...] = jnp.zeros(acc_scr.shape, F32)

        def update(diagonal):
            heads = range(hps)
            lanes = [slice(LANES * hh, LANES * (hh + 1)) for hh in heads]
            rows = [slice(64 * hh, 64 * (hh + 1)) for hh in heads]
            sts = [_dot(k_ref[:, lanes[hh]], q_ref[:, lanes[hh]], NT) for hh in heads]
            m_prev = [m_scr[hh:hh + 1, :] for hh in heads]
            l_prev = [l_scr[hh:hh + 1, :] for hh in heads]
            acc_prev = [acc_scr[rows[hh], :] for hh in heads]
            m_new, l_new, acc_new = [], [], []
            for hh in heads:
                st = sts[hh]
                if diagonal:
                    key = lax.broadcasted_iota(jnp.int32, (tq, tq), 0)
                    qry = lax.broadcasted_iota(jnp.int32, (tq, tq), 1)
                    st = jnp.where(key <= qry, st, NEG)
                m_new.append(jnp.maximum(m_prev[hh], jnp.max(st, axis=0, keepdims=True)))
                alpha = jnp.exp(m_prev[hh] - m_new[hh])
                pt = jnp.exp(st - m_new[hh])
                l_new.append(alpha * l_prev[hh] + jnp.sum(pt, axis=0, keepdims=True))
                acc_new.append(alpha * acc_prev[hh] + _dot(vt_ref[rows[hh], :], pt, NN))
            for hh in heads:
                m_scr[hh:hh + 1, :] = m_new[hh]
                l_scr[hh:hh + 1, :] = l_new[hh]
                acc_scr[rows[hh], :] = acc_new[hh]

        @pl.when(j < i)
        def _():
            update(False)

        @pl.when(j == i)
        def _():
            update(True)
            lse_ref[...] = jnp.zeros(lse_ref.shape, F32)
            for hh in range(hps):
                rows = slice(64 * hh, 64 * (hh + 1))
                acc_scr[rows, :] = acc_scr[rows, :] / l_scr[hh:hh + 1, :]
                lse_ref[hh:hh + 1, :] = m_scr[hh:hh + 1, :] + jnp.log(l_scr[hh:hh + 1, :])
            o = acc_scr[...].T
            o_ref[...] = o
            y = o * _silu_and_grad(g_ref[...])[0]
            y_ref[...] = y.astype(MXU_DTYPE)
            yt_ref[...] = y.T.astype(MXU_DTYPE)

        if ng:
            @pl.when((pl.program_id(0) == ngrp - 1) & (step == nsteps - 1))
            def _():
                gather_finish()

    qo = lambda w, off=0: pl.BlockSpec((tq, w), lambda p, s, it_, jt_: (it_[s], p + off))
    sds = jax.ShapeDtypeStruct
    comm_scratch = [pltpu.SemaphoreType.DMA((6 * ng,)), pltpu.SemaphoreType.DMA((6 * ng,))] if ng else []
    return pl.pallas_call(
        body,
        name="mla_attn_fwd_gather" if ng else "mla_attn_fwd",
        grid_spec=pltpu.PrefetchScalarGridSpec(
            num_scalar_prefetch=2,
            grid=(ngrp, nsteps),
            in_specs=[qo(wq), pl.BlockSpec((tq, wq), lambda p, s, it_, jt_: (jt_[s], p)),
                      pl.BlockSpec((wv, tq), lambda p, s, it_, jt_: (p, jt_[s])), qo(wv, 512 // wv)] + [HBM_SPEC] * ng,
            out_specs=[qo(wv), qo(wv), pl.BlockSpec((wv, tq), lambda p, s, it_, jt_: (p, it_[s])),
                       pl.BlockSpec((None, 8, tq), lambda p, s, it_, jt_: (p, 0, it_[s]))] + [HBM_SPEC] * ng,
            scratch_shapes=[pltpu.VMEM((8, tq), F32), pltpu.VMEM((8, tq), F32), pltpu.VMEM((wv, tq), F32)] + comm_scratch,
        ),
        out_shape=[sds((t, 512), F32), sds((t, 512), MXU_DTYPE), sds((512, t), MXU_DTYPE), sds((ngrp, 8, t), F32)]
        + _gather_shapes(ws),
        compiler_params=_cparams(("arbitrary", "arbitrary") if ng else ("parallel", "arbitrary")),
    )(it, jt, q, k, vt, z, *ws)


def mla_attn_bwd(q, k, v, do, lse, dd, *, tq=256, chips=None):
    t = q.shape[0]
    nq = t // tq
    it, jt = _causal_tiles(nq, True)
    hps, wq, wv = MLA_HPS, LANES * MLA_HPS, 64 * MLA_HPS
    pcs, layer = chips if chips is not None else ([], 0)
    nc = len(pcs)
    ngrp, nsteps = MLA_HEADS // hps, int(it.shape[0])

    def body(it_ref, jt_ref, q_ref, k_ref, v_ref, do_ref, lse_ref, d_ref, *rest):
        pc_refs, rest = rest[:nc], rest[nc:]
        dq_ref, dk_ref, dv_ref = rest[:3]
        lb_refs, rest = rest[3:3 + nc], rest[3 + nc:]
        dk_scr, dv_scr = rest[:2]
        step = pl.program_id(1)
        i, j = it_ref[step], jt_ref[step]
        masks = _head_masks()
        if nc:
            chips_start, chips_finish = _chips_exchange(pc_refs, lb_refs, rest[2], rest[3], layer)

            @pl.when((pl.program_id(0) == 0) & (step == 0))
            def _():
                chips_start()

        @pl.when(step == 0)
        def _():
            dq_ref[...] = jnp.zeros(dq_ref.shape, F32)

        @pl.when(i == j)
        def _():
            dk_scr[...] = jnp.zeros(dk_scr.shape, F32)
            dv_scr[...] = jnp.zeros(dv_scr.shape, F32)

        def update(diagonal):
            qrows = pl.ds(pl.multiple_of(i * tq, tq), tq)
            heads = range(hps)
            lanes = [slice(LANES * hh, LANES * (hh + 1)) for hh in heads]
            pair = [slice(LANES * (hh // 2), LANES * (hh // 2 + 1)) for hh in heads]
            qh = [q_ref[:, lanes[hh]] for hh in heads]
            kh = [k_ref[:, lanes[hh]] for hh in heads]
            doh = []
            for hh in heads:
                dov = do_ref[:, pair[hh]]
                doh.append(jnp.where(masks[hh % 2], dov, jnp.zeros_like(dov)))
            sts = [_dot(kh[hh], qh[hh], NT) for hh in heads]
            dps = [_dot(v_ref[:, pair[hh]], doh[hh], NT) for hh in heads]
            lse = [lse_ref[hh:hh + 1, :] for hh in heads]
            ddv = [d_ref[hh:hh + 1, :] for hh in heads]
            dk_new = [dk_scr[:, lanes[hh]] for hh in heads]
            dq_new = [dq_ref[qrows, lanes[hh]] for hh in heads]
            dv_new = [dv_scr[:, pair[2 * pp]] for pp in range(hps // 2)]
            for hh in heads:
                st = sts[hh] - lse[hh]
                if diagonal:
                    key = lax.broadcasted_iota(jnp.int32, (tq, tq), 0)
                    qry = lax.broadcasted_iota(jnp.int32, (tq, tq), 1)
                    st = jnp.where(key <= qry, st, NEG)
                pt = jnp.exp(st)
                dst = pt * (dps[hh] - ddv[hh])
                dv_new[hh // 2] = dv_new[hh // 2] + _dot(pt, doh[hh], NN)
                dk_new[hh] = dk_new[hh] + _dot(dst, qh[hh], NN)
                dq_new[hh] = dq_new[hh] + _dot(dst, kh[hh], TN)
            for hh in heads:
                dk_scr[:, lanes[hh]] = dk_new[hh]
                dq_ref[qrows, lanes[hh]] = dq_new[hh]
            for pp in range(hps // 2):
                dv_scr[:, pair[2 * pp]] = dv_new[pp]

        @pl.when(j < i)
        def _():
            update(False)

        @pl.when(j == i)
        def _():
            update(True)

        @pl.when(i == nq - 1)
        def _():
            dk_ref[...] = dk_scr[...]
            dv_ref[...] = dv_scr[...]

        if nc:
            @pl.when((pl.program_id(0) == ngrp - 1) & (step == nsteps - 1))
            def _():
                chips_finish()

    qo = lambda w: pl.BlockSpec((tq, w), lambda p, s, it_, jt_: (it_[s], p))
    kv = lambda w: pl.BlockSpec((tq, w), lambda p, s, it_, jt_: (jt_[s], p))
    stat = pl.BlockSpec((None, 8, tq), lambda p, s, it_, jt_: (p, 0, it_[s]))
    sds = jax.ShapeDtypeStruct
    comm_scratch = [pltpu.SemaphoreType.DMA((3 * nc,)), pltpu.SemaphoreType.DMA((3 * nc,))] if nc else []
    return pl.pallas_call(
        body,
        name="mla_attn_bwd_chips" if nc else "mla_attn_bwd",
        grid_spec=pltpu.PrefetchScalarGridSpec(
            num_scalar_prefetch=2,
            grid=(ngrp, nsteps),
            in_specs=[qo(wq), kv(wq), kv(wv), qo(wv), stat, stat] + [HBM_SPEC] * nc,
            out_specs=[pl.BlockSpec((t, wq), lambda p, s, it_, jt_: (0, p)), kv(wq), kv(wv)] + [HBM_SPEC] * nc,
            scratch_shapes=[pltpu.VMEM((tq, wq), F32), pltpu.VMEM((tq, wv), F32)] + comm_scratch,
        ),
        out_shape=[sds((t, 1024), F32), sds((t, 1024), F32), sds((t, 512), F32)] + _chips_shapes(pcs),
        compiler_params=_cparams(("arbitrary", "arbitrary") if nc else ("parallel", "arbitrary")),
    )(it, jt, q, k, v, do, lse, dd, *pcs)


def mla_gate_bwd(dy, o, z, *, tm=512):
    t = dy.shape[0]
    wv = 64 * MLA_HPS

    def body(dy_ref, o_ref, g_ref, do_ref, dzg_ref, dd_ref):
        dyv, ov = dy_ref[...], o_ref[...]
        silu, dsilu = _silu_and_grad(g_ref[...])
        do = dyv * silu
        do_ref[...] = do.astype(MXU_DTYPE)
        dzg_ref[...] = (dyv * ov * dsilu).astype(MXU_DTYPE)
        prod = do * ov
        row = lax.broadcasted_iota(jnp.int32, (8, wv), 0)
        lane = lax.broadcasted_iota(jnp.int32, (8, wv), 1)
        pick = ((lane >= row * DIL_HD) & (lane < (row + 1) * DIL_HD)).astype(BF16)
        hi = prod.astype(BF16)
        r1 = prod - hi.astype(F32)
        mid = r1.astype(BF16)
        lo = (r1 - mid.astype(F32)).astype(BF16)
        dot = lambda u: lax.dot_general(pick, u, NT, preferred_element_type=F32)
        dd_ref[...] = dot(hi) + dot(mid) + dot(lo)

    blk = lambda off=0: pl.BlockSpec((tm, wv), lambda p, i: (i, p + off))
    sds = jax.ShapeDtypeStruct
    return pl.pallas_call(
        body,
        name="mla_gate_bwd",
        grid=(MLA_HEADS // MLA_HPS, t // tm),
        in_specs=[blk(), blk(), blk(512 // wv)],
        out_specs=[blk(), blk(), pl.BlockSpec((None, 8, tm), lambda p, i: (p, 0, i))],
        out_shape=[sds((t, 512), MXU_DTYPE), sds((t, 512), MXU_DTYPE), sds((MLA_HEADS // MLA_HPS, 8, t), F32)],
        compiler_params=_cparams(("parallel", "parallel")),
    )(dy, o, z)


SPAN = 2048
DIL_SCALE = DIL_HD ** -0.5


def _rm_src(u, window, dil):
    nn, r = divmod(u, dil)
    return pl.ds(nn * window + r, DIL_NK, stride=dil) if dil > 1 else pl.ds(u * DIL_NK, DIL_NK)


def _rm_dst(u):
    return pl.ds(u * DIL_NK, DIL_NK)


def _dil_prep_tile(ops, x, g2, c, s1, s2, scale):
    ms = ops.seg_sum(x * x) * (1.0 / DIL_HD)
    return rope(ops, x * lax.rsqrt(ms + EPS) * g2, c, s1, s2, 32) * scale


def _span_blk(base):
    return pl.BlockSpec((SPAN, LANES), lambda s, p: (s, base + p))


def _const_blk(shape):
    return pl.BlockSpec(shape, lambda s, p: (0,) * len(shape))


_DIL_TABLE_SPECS = [pl.BlockSpec((SPAN, LANES), (lambda s, p, blk=blk: (s, blk))) for blk in (3, 4, 5)]


def dil_prep_fwd(z, tabs, gq2, gk2, gi):
    window, dil = DIL_GROUPS[gi]
    t = z.shape[0]

    def body(q_ref, k_ref, v_ref, c_ref, s1_ref, s2_ref, gq_ref, gk_ref, qo_ref, ko_ref, vo_ref):
        for u in range(SPAN // DIL_NK):
            src, dst = _rm_src(u, window, dil), _rm_dst(u)
            c, s1, s2 = c_ref[src, :], s1_ref[src, :], s2_ref[src, :]
            qo_ref[dst, :] = _dil_prep_tile(PLAIN, q_ref[src, :], gq_ref[...], c, s1, s2, DIL_SCALE).astype(MXU_DTYPE)
            ko_ref[dst, :] = _dil_prep_tile(PLAIN, k_ref[src, :], gk_ref[...], c, s1, s2, 1.0).astype(MXU_DTYPE)
            vo_ref[dst, :] = v_ref[src, :].astype(MXU_DTYPE)

    return pl.pallas_call(
        body,
        name=f"dil_prep_fwd_{dil}",
        grid=(t // SPAN, 4),
        in_specs=[_span_blk(4 * gi), _span_blk(12 + 4 * gi), _span_blk(24 + 4 * gi)] + _DIL_TABLE_SPECS
        + [_const_blk((1, LANES)), _const_blk((1, LANES))],
        out_specs=[_span_blk(0)] * 3,
        out_shape=[jax.ShapeDtypeStruct((t, 512), MXU_DTYPE)] * 3,
        compiler_params=_cparams(("parallel", "parallel")),
    )(z, z, z, tabs, tabs, tabs, gq2, gk2)


def dil_prep_bwd(z, tabs, gq2, gk2, dq, dk, dv, gi):
    window, dil = DIL_GROUPS[gi]
    t = z.shape[0]

    def body(q_ref, k_ref, c_ref, s1_ref, s2_ref, gq_ref, gk_ref, dq_ref, dk_ref, dv_ref, dzq_ref, dzk_ref, dzv_ref, dgq_ref,
             dgk_ref, sq, sk, sv):
        @pl.when((pl.program_id(0) == 0) & (pl.program_id(1) == 0))
        def _():
            dgq_ref[...] = jnp.zeros((1, LANES), F32)
            dgk_ref[...] = jnp.zeros((1, LANES), F32)

        dgs = [jnp.zeros((1, LANES), F32), jnp.zeros((1, LANES), F32)]
        for u in range(SPAN // DIL_NK):
            src, dst = _rm_src(u, window, dil), _rm_dst(u)
            c, s1, s2 = c_ref[src, :], s1_ref[src, :], s2_ref[src, :]
            for idx, (x_ref, g_ref, ct_ref, scr, scale) in enumerate(((q_ref, gq_ref, dq_ref, sq, DIL_SCALE),
                                                                      (k_ref, gk_ref, dk_ref, sk, 1.0))):
                _, vjp = jax.vjp(lambda xv, gv, sc=scale: _dil_prep_tile(DIFF, xv, gv, c, s1, s2, sc), x_ref[src, :], g_ref[...])
                dx, dg = vjp(ct_ref[dst, :])
                scr[src, :] = dx
                dgs[idx] = dgs[idx] + dg
            sv[src, :] = dv_ref[dst, :]
        dgq_ref[...] += dgs[0] + pltpu.roll(dgs[0], DIL_HD, 1)
        dgk_ref[...] += dgs[1] + pltpu.roll(dgs[1], DIL_HD, 1)
        for c0 in range(0, SPAN, 256):
            rows = slice(c0, c0 + 256)
            dzq_ref[rows, :] = sq[rows, :].astype(MXU_DTYPE)
            dzk_ref[rows, :] = sk[rows, :].astype(MXU_DTYPE)
            dzv_ref[rows, :] = sv[rows, :].astype(MXU_DTYPE)

    sds = jax.ShapeDtypeStruct
    return pl.pallas_call(
        body,
        name=f"dil_prep_bwd_{dil}",
        grid=(t // SPAN, 4),
        in_specs=[_span_blk(4 * gi), _span_blk(12 + 4 * gi)] + _DIL_TABLE_SPECS
        + [_const_blk((1, LANES)), _const_blk((1, LANES)), _span_blk(0), _span_blk(0), _span_blk(0)],
        out_specs=[_span_blk(0)] * 3 + [_const_blk((1, LANES))] * 2,
        out_shape=[sds((t, 512), MXU_DTYPE)] * 3 + [sds((1, LANES), F32)] * 2,
        scratch_shapes=[pltpu.VMEM((SPAN, LANES), F32)] * 3,
        compiler_params=_cparams(("arbitrary", "arbitrary")),
    )(z, z, tabs, tabs, tabs, gq2, gk2, dq, dk, dv)


def _band_masks():
    qi = lax.broadcasted_iota(jnp.int32, (DIL_NK, DIL_NK), 0)
    ki = lax.broadcasted_iota(jnp.int32, (DIL_NK, DIL_NK), 1)
    return (ki >= qi), (ki <= qi)


def _pair_heads():
    return [(pair, hh) for pair in range(4) for hh in range(2)]


def _pair_lanes(pair):
    return slice(LANES * pair, LANES * (pair + 1))


def _zero_other_head(mask, x):
    return jnp.where(mask, x, jnp.zeros_like(x))


def dil_attn_fwd(name, q, k, v, dil):
    t = q.shape[0]

    def body(q_ref, kp_ref, kc_ref, vp_ref, vc_ref, o_ref, lse_ref):
        b = pl.program_id(0)
        mprev, mcur = _band_masks()
        mprev = mprev & (b >= dil)
        hm = _head_masks()
        heads = _pair_heads()
        qh = [_zero_other_head(hm[hh], q_ref[:, _pair_lanes(pair)]) for pair, hh in heads]
        sps = [_dot(qh[i], kp_ref[:, _pair_lanes(pair)], NT) for i, (pair, _) in enumerate(heads)]
        scs = [_dot(qh[i], kc_ref[:, _pair_lanes(pair)], NT) for i, (pair, _) in enumerate(heads)]
        o = [jnp.zeros((DIL_NK, LANES), F32) for _ in range(4)]
        lse = [jnp.zeros((DIL_NK, LANES), F32) for _ in range(4)]
        for i, (pair, hh) in enumerate(heads):
            sp, sc = jnp.where(mprev, sps[i], NEG), jnp.where(mcur, scs[i], NEG)
            m = jnp.maximum(jnp.max(sp, axis=1, keepdims=True), jnp.max(sc, axis=1, keepdims=True))
            ep, ec = jnp.exp(sp - m), jnp.exp(sc - m)
            den = jnp.sum(ep, axis=1, keepdims=True) + jnp.sum(ec, axis=1, keepdims=True)
            lanes = _pair_lanes(pair)
            oh = _dot(ep, _zero_other_head(hm[hh], vp_ref[:, lanes]), NN) + _dot(ec, _zero_other_head(hm[hh], vc_ref[:, lanes]), NN)
            o[pair] = o[pair] + oh / den
            lse[pair] = jnp.where(hm[hh], m + jnp.log(den), lse[pair])
        for pair in range(4):
            o_ref[:, _pair_lanes(pair)] = o[pair]
            lse_ref[:, _pair_lanes(pair)] = lse[pair]

    cur = pl.BlockSpec((DIL_NK, 512), lambda b: (b, 0))
    prev = pl.BlockSpec((DIL_NK, 512), lambda b: (jnp.maximum(b - dil, 0), 0))
    sds = jax.ShapeDtypeStruct
    return pl.pallas_call(
        body,
        name=name,
        grid=(t // DIL_NK,),
        in_specs=[cur, prev, cur, prev, cur],
        out_specs=[cur, cur],
        out_shape=[sds((t, 512), F32), sds((t, 512), F32)],
        compiler_params=_cparams(("parallel",)),
    )(q, k, k, v, v)


def dil_attn_bwd(name, q, k, v, do, lse, dd, dil):
    t = q.shape[0]
    nblk = t // DIL_NK

    def body(q_ref, do_ref, l_ref, d_ref, kp_ref, kc_ref, vp_ref, vc_ref, dq_ref, dk_ref, dv_ref):
        b = pl.program_id(0)

        @pl.when(b == 0)
        def _():
            dk_ref[...] = jnp.zeros(dk_ref.shape, F32)
            dv_ref[...] = jnp.zeros(dv_ref.shape, F32)

        mprev, mcur = _band_masks()
        mprev = mprev & (b >= dil)
        rows_c = pl.ds(pl.multiple_of(b * DIL_NK, DIL_NK), DIL_NK)
        rows_p = pl.ds(pl.multiple_of(jnp.maximum(b - dil, 0) * DIL_NK, DIL_NK), DIL_NK)
        hm = _head_masks()
        heads = _pair_heads()
        ln_ = [_pair_lanes(pair) for pair, _ in heads]
        qh = [_zero_other_head(hm[hh], q_ref[:, ln_[i]]) for i, (_, hh) in enumerate(heads)]
        doh = [_zero_other_head(hm[hh], do_ref[:, ln_[i]]) for i, (_, hh) in enumerate(heads)]
        idx = range(len(heads))
        s_p = [_dot(qh[i], kp_ref[:, ln_[i]], NT) for i in idx]
        s_c = [_dot(qh[i], kc_ref[:, ln_[i]], NT) for i in idx]
        dp_p = [_dot(doh[i], vp_ref[:, ln_[i]], NT) for i in idx]
        dp_c = [_dot(doh[i], vc_ref[:, ln_[i]], NT) for i in idx]
        zero = lambda: [jnp.zeros((DIL_NK, LANES), F32) for _ in range(4)]
        dq, dk_p, dk_c, dv_p, dv_c = zero(), zero(), zero(), zero(), zero()
        for i, (pair, hh) in enumerate(heads):
            lse_h, d_h = _row_scalar(l_ref[:, ln_[i]], hm[hh]), _row_scalar(d_ref[:, ln_[i]], hm[hh])
            pp = jnp.exp(jnp.where(mprev, s_p[i] - lse_h, NEG))
            pc = jnp.exp(jnp.where(mcur, s_c[i] - lse_h, NEG))
            dsp, dsc = pp * (dp_p[i] - d_h), pc * (dp_c[i] - d_h)
            dq[pair] = (dq[pair] + _dot(dsp, _zero_other_head(hm[hh], kp_ref[:, ln_[i]]), NN)
                        + _dot(dsc, _zero_other_head(hm[hh], kc_ref[:, ln_[i]]), NN))
            dv_p[pair] = dv_p[pair] + _dot(pp, doh[i], TN)
            dv_c[pair] = dv_c[pair] + _dot(pc, doh[i], TN)
            dk_p[pair] = dk_p[pair] + _dot(dsp, qh[i], TN)
            dk_c[pair] = dk_c[pair] + _dot(dsc, qh[i], TN)
        for pair in range(4):
            lanes = _pair_lanes(pair)
            dq_ref[:, lanes] = dq[pair]
            dk_ref[rows_p, lanes] += dk_p[pair]
            dv_ref[rows_p, lanes] += dv_p[pair]
            dk_ref[rows_c, lanes] += dk_c[pair]
            dv_ref[rows_c, lanes] += dv_c[pair]

    cur = pl.BlockSpec((DIL_NK, 512), lambda b: (b, 0))
    prev = pl.BlockSpec((DIL_NK, 512), lambda b: (jnp.maximum(b - dil, 0), 0))
    whole = pl.BlockSpec((t, 512), lambda b: (0, 0))
    sds = jax.ShapeDtypeStruct
    return pl.pallas_call(
        body,
        name=name,
        grid=(nblk,),
        in_specs=[cur, cur, cur, cur, prev, cur, prev, cur],
        out_specs=[cur, whole, whole],
        out_shape=[sds((t, 512), F32)] * 3,
        compiler_params=_cparams(("arbitrary",)),
    )(q, do, lse, dd, k, k, v, v)


def dil_combine(os_, lses, z):
    t = z.shape[0]

    def body(o1_ref, l1_ref, o2_ref, l2_ref, o3_ref, l3_ref, g_ref, y_ref, o_ref, lse_ref, yt_ref, so2, sl2, so3, sl3):
        for (window, dil), o_in, l_in, so, sl in ((DIL_GROUPS[1], o2_ref, l2_ref, so2, sl2), (DIL_GROUPS[2], o3_ref, l3_ref, so3, sl3)):
            for u in range(SPAN // DIL_NK):
                src, dst = _rm_src(u, window, dil), _rm_dst(u)
                so[src, :] = o_in[dst, :]
                sl[src, :] = l_in[dst, :]
        for c0 in range(0, SPAN, 256):
            rows = slice(c0, c0 + 256)
            la, lb, lc = l1_ref[rows, :], sl2[rows, :], sl3[rows, :]
            mx = jnp.maximum(jnp.maximum(la, lb), lc)
            wa, wb, wc = jnp.exp(la - mx), jnp.exp(lb - mx), jnp.exp(lc - mx)
            tot = wa + wb + wc
            o = (wa * o1_ref[rows, :] + wb * so2[rows, :] + wc * so3[rows, :]) / tot
            y = o * _silu_and_grad(g_ref[rows, :])[0]
            y_ref[rows, :] = y.astype(MXU_DTYPE)
            o_ref[rows, :] = o
            lse_ref[rows, :] = mx + jnp.log(tot)
            yt_ref[:, rows] = y.T.astype(MXU_DTYPE)

    sds = jax.ShapeDtypeStruct
    return pl.pallas_call(
        body,
        name="dil_combine",
        grid=(t // SPAN, 4),
        in_specs=[_span_blk(0)] * 6 + [_span_blk(36)],
        out_specs=[_span_blk(0)] * 3 + [pl.BlockSpec((LANES, SPAN), lambda s, p: (p, s))],
        out_shape=[sds((t, 512), MXU_DTYPE), sds((t, 512), F32), sds((t, 512), F32), sds((512, t), MXU_DTYPE)],
        scratch_shapes=[pltpu.VMEM((SPAN, LANES), F32)] * 4,
        compiler_params=_cparams(("parallel", "parallel")),
    )(os_[0], lses[0], os_[1], lses[1], os_[2], lses[2], z)


def dil_gate_bwd(dy, o, z, lse):
    t = dy.shape[0]

    def body(dy_ref, o_ref, g_ref, lse_ref, dzg_ref, do1_ref, dd1_ref, do2_ref, dd2_ref, l2_ref, do3_ref, dd3_ref, l3_ref, do_scr):
        for c0 in range(0, SPAN, 256):
            rows = slice(c0, c0 + 256)
            dyv, ov = dy_ref[rows, :], o_ref[rows, :]
            silu, dsilu = _silu_and_grad(g_ref[rows, :])
            do = dyv * silu
            do_scr[rows, :] = do
            do1_ref[rows, :] = do.astype(MXU_DTYPE)
            dd1_ref[rows, :] = _seg_sum_impl(do * ov)
            dzg_ref[rows, :] = (dyv * ov * dsilu).astype(MXU_DTYPE)
        for (window, dil), do_o, dd_o, l_o in ((DIL_GROUPS[1], do2_ref, dd2_ref, l2_ref), (DIL_GROUPS[2], do3_ref, dd3_ref, l3_ref)):
            for u in range(SPAN // DIL_NK):
                src, dst = _rm_src(u, window, dil), _rm_dst(u)
                do_o[dst, :] = do_scr[src, :].astype(MXU_DTYPE)
                dd_o[dst, :] = dd1_ref[src, :]
                l_o[dst, :] = lse_ref[src, :]

    sds = jax.ShapeDtypeStruct
    f32, mxu = sds((t, 512), F32), sds((t, 512), MXU_DTYPE)
    return pl.pallas_call(
        body,
        name="dil_gate_bwd",
        grid=(t // SPAN, 4),
        in_specs=[_span_blk(0), _span_blk(0), _span_blk(36), _span_blk(0)],
        out_specs=[_span_blk(0)] * 9,
        out_shape=[mxu, mxu, f32, mxu, f32, f32, mxu, f32, f32],
        scratch_shapes=[pltpu.VMEM((SPAN, LANES), F32)],
        compiler_params=_cparams(("parallel", "parallel")),
    )(dy, o, z, lse)


def _merge_tile(p0, p1, p2, z0, z1, z2, b0, b1, b2):
    return _sigmoid(z0 + b0) * p0 + _sigmoid(z1 + b1) * p1 + _sigmoid(z2 + b2) * p2


def _merge_ins(ps, z, b):
    w = 256
    return ([_row(p, w, 0, True) for p in ps] + [_row(z, w, 4 * i, True) for i in range(3)]
            + [_const(b, w, 4 * i, True) for i in range(3)])


def merge_fwd(ps, z, b):
    def fn(ins, outs, _):
        merged = _merge_tile(*[r[...] for r in ins])
        outs[0][...] = merged.astype(MXU_DTYPE)
        outs[1][...] = merged.T.astype(MXU_DTYPE)

    return rowwise("merge_fwd", fn, z.shape[0], tm=1024, ncol=4, ins=_merge_ins(ps, z, b), outs=[(1024, 256, MXU_DTYPE, True)],
                   touts=[(1024, 256, MXU_DTYPE, True)])


def merge_bwd(dm, ps, z, b):
    def fn(ins, outs, accs):
        _, vjp = jax.vjp(_merge_tile, *[r[...] for r in ins[:9]])
        grads = vjp(ins[9][...])
        for i in range(3):
            outs[i][...] = grads[i].astype(MXU_DTYPE)
            outs[3 + i][...] = grads[3 + i].astype(MXU_DTYPE)
            accs[i][...] += grads[6 + i]

    return rowwise("merge_bwd", fn, z.shape[0], tm=1024, ncol=4, ins=_merge_ins(ps, z, b) + [_row(dm, 256, 0, True)],
                   outs=[(1024, 256, MXU_DTYPE, True)] * 6, accs=[(1, 1024, 256, True)] * 3)


EW_BLOCK_BYTES = 2**21


def _tile2d(r, c):
    if r * c * 4 <= EW_BLOCK_BYTES:
        return r, c
    for tr in (512, 256, 128, 64, 32):
        if r % tr == 0 and tr * c * 4 <= EW_BLOCK_BYTES:
            return tr, c
    for tc in (1024, 512, 256, 128):
        if c % tc == 0 and r * tc * 4 <= EW_BLOCK_BYTES:
            return r, tc
    raise ValueError((r, c))


def adamw(name, w, g, m, v):
    shape = w.shape
    c = shape[-1]
    r = w.size // c
    tr, tc = _tile2d(r, c)
    assert tc == c
    c1, c2 = 1.0 - ADAM_B1 ** ADAM_STEP, 1.0 - ADAM_B2 ** ADAM_STEP

    def body(w_ref, g_ref, m_ref, v_ref, d_ref, mo_ref, vo_ref):
        gv = g_ref[...]
        mn = ADAM_B1 * m_ref[...] + (1.0 - ADAM_B1) * gv
        vn = ADAM_B2 * v_ref[...] + (1.0 - ADAM_B2) * (gv * gv)
        d_ref[...] = -ADAM_LR * ((mn / c1) / (jnp.sqrt(vn / c2) + ADAM_EPS) + ADAM_WD * w_ref[...])
        mo_ref[...] = mn
        vo_ref[...] = vn

    spec = pl.BlockSpec((tr, c), lambda i: (i, 0))
    outs = pl.pallas_call(
        body,
        name=name,
        grid=(r // tr,),
        in_specs=[spec] * 4,
        out_specs=[spec] * 3,
        out_shape=[jax.ShapeDtypeStruct((r, c), F32)] * 3,
        compiler_params=_cparams(("parallel",)),
    )(*[a.reshape(r, c) for a in (w, g, m, v)])
    return [o.reshape(shape) for o in outs]


def adamw_pair(name, w, reduced, received, m, v, c_idx):
    _, r, c = w.shape
    tr, tc = _tile2d(r, c)
    c1, c2 = 1.0 - ADAM_B1 ** ADAM_STEP, 1.0 - ADAM_B2 ** ADAM_STEP

    def body(c_ref, w_ref, r0_ref, x0_ref, r1_ref, x1_ref, m_ref, v_ref, g_ref, d_ref, mo_ref, vo_ref):
        mine = c_ref[0]
        gv = jnp.where(pl.program_id(0) == 0, jnp.where(mine == 0, r0_ref[...], x0_ref[...]),
                       jnp.where(mine == 1, r1_ref[...], x1_ref[...]))
        mn = ADAM_B1 * m_ref[...] + (1.0 - ADAM_B1) * gv
        vn = ADAM_B2 * v_ref[...] + (1.0 - ADAM_B2) * (gv * gv)
        g_ref[...] = gv
        d_ref[...] = -ADAM_LR * ((mn / c1) / (jnp.sqrt(vn / c2) + ADAM_EPS) + ADAM_WD * w_ref[...])
        mo_ref[...] = mn
        vo_ref[...] = vn

    full = pl.BlockSpec((None, tr, tc), lambda h, i, j, cr: (h, i, j))
    half = pl.BlockSpec((tr, tc), lambda h, i, j, cr: (i, j))
    return pl.pallas_call(
        body,
        name=name,
        grid_spec=pltpu.PrefetchScalarGridSpec(
            num_scalar_prefetch=1,
            grid=(2, r // tr, c // tc),
            in_specs=[full, half, half, half, half, full, full],
            out_specs=[full] * 4,
        ),
        out_shape=[jax.ShapeDtypeStruct((2, r, c), F32)] * 4,
        compiler_params=_cparams(("parallel", "parallel", "parallel")),
    )(c_idx, w, reduced[0], received[0], reduced[1], received[1], m, v)


def sum_pair(name, g, la, out_dtype):
    _, r, c = g.shape
    tr, tc = _tile2d(r, c)

    def body(g_ref, la_ref, o_ref):
        o_ref[...] = (g_ref[...] + la_ref[...]).astype(o_ref.dtype)

    spec = pl.BlockSpec((None, tr, tc), lambda s, i, j: (s, i, j))
    return pl.pallas_call(
        body,
        name=name,
        grid=(4, r // tr, c // tc),
        in_specs=[spec, spec],
        out_specs=spec,
        out_shape=jax.ShapeDtypeStruct((4, r, c), out_dtype),
        compiler_params=_cparams(("parallel", "parallel", "parallel")),
    )(g, la)


def sum_chips(name, g, la, lb, s_idx):
    _, r, c = g.shape
    tr, tc = _tile2d(r, c)

    def body(s_ref, g_ref, la_ref, l0_ref, l1_ref, l2_ref, o_ref):
        own = g_ref[...] + la_ref[...]
        o_ref[...] = ((own + l0_ref[...].astype(F32)) + l1_ref[...].astype(F32)) + l2_ref[...].astype(F32)

    own_spec = pl.BlockSpec((None, tr, tc), lambda i, j, sr: (sr[0], i, j))
    lspec = lambda k: pl.BlockSpec((None, tr, tc), lambda i, j, sr: (k, i, j))
    return pl.pallas_call(
        body,
        name=name,
        grid_spec=pltpu.PrefetchScalarGridSpec(
            num_scalar_prefetch=1,
            grid=(r // tr, c // tc),
            in_specs=[own_spec, own_spec, lspec(0), lspec(1), lspec(2)],
            out_specs=pl.BlockSpec((tr, tc), lambda i, j, sr: (i, j)),
        ),
        out_shape=jax.ShapeDtypeStruct((r, c), F32),
        compiler_params=_cparams(("parallel", "parallel")),
    )(s_idx, g, la, lb, lb, lb)


def _place():
    x, y, c = lax.axis_index("x"), lax.axis_index("y"), lax.axis_index("c")
    chips = [(1 - x, y), (x, 1 - y), (1 - x, 1 - y)]
    return x, y, c, chips


HBM_SPEC = pl.BlockSpec(memory_space=pltpu.HBM)


def _comm_call(name, body, ins, out_shapes, n_sem, n_local):
    return pl.pallas_call(
        body,
        name=name,
        in_specs=[HBM_SPEC] * len(ins),
        out_specs=[HBM_SPEC] * len(out_shapes),
        out_shape=out_shapes,
        scratch_shapes=[pltpu.SemaphoreType.DMA((n_sem,)), pltpu.SemaphoreType.DMA((n_sem,)),
                        pltpu.SemaphoreType.DMA((max(n_local, 1),))],
    )(*ins)


def _layer_gather(ins, outs, send, recv, layer):
    n = len(ins)
    x, y, c, chips = _place()
    s = 2 * x + y
    sib = (x, y, 1 - c)
    active = c == layer

    def rc(a, k, src, dst, dev):
        return pltpu.make_async_remote_copy(src_ref=src, dst_ref=dst, send_sem=send.at[6 * a + k], recv_sem=recv.at[6 * a + k],
                                            device_id=dev, device_id_type=MESH)

    def first_hop():
        return [rc(a, j, ins[a], outs[a].at[s], (*chip, c)) for j, chip in enumerate(chips) for a in range(n)]

    def start():
        @pl.when(active)
        def _():
            for cp in first_hop():
                cp.start()

    def finish():
        @pl.when(active)
        def _():
            forwards = []
            for j, (cx, cy) in enumerate(chips):
                for a in range(n):
                    landed = outs[a].at[2 * cx + cy]
                    rc(a, j, landed, landed, sib).wait_recv()
                    forwards.append(rc(a, 3 + j, landed, landed, sib))
                    forwards[-1].start()
            for cp in first_hop() + forwards:
                cp.wait_send()

        @pl.when(jnp.logical_not(active))
        def _():
            for j, (cx, cy) in enumerate(chips):
                for a in range(n):
                    other = outs[a].at[2 * cx + cy]
                    rc(a, 3 + j, other, other, sib).wait_recv()

    return start, finish


def _gather_shapes(ws):
    return [jax.ShapeDtypeStruct((4,) + w.shape, w.dtype) for w in ws]


def gather_hook(ws, layer):
    return (ws, _gather_shapes(ws), 6 * len(ws), lambda i, o, s, r: _layer_gather(i, o, s, r, layer))


def allgather_layer(ws, layer):
    n = len(ws)

    def body(*refs):
        send, recv, _ = refs[2 * n:]
        start, finish = _layer_gather(refs[:n], refs[n:2 * n], send, recv, layer)
        start()
        finish()

    return _comm_call(f"allgather_layer{layer}", body, ws, _gather_shapes(ws), 6 * n, 0)


def exchange_sibling(gs, layer, tag=""):
    n = len(gs)

    def body(*refs):
        send, recv, _ = refs[2 * n:]
        start, finish = _sibling_exchange(refs[:n], refs[n:2 * n], send, recv, layer)
        start()
        finish()

    return _comm_call(f"exchange_sibling{layer}{tag}", body, gs, [jax.ShapeDtypeStruct(g.shape, g.dtype) for g in gs], n, 0)


def _sibling_exchange(ins, outs, send, recv, layer):
    x, y, c, _ = _place()

    def copies():
        return [pltpu.make_async_remote_copy(src_ref=ins[a], dst_ref=outs[a], send_sem=send.at[a], recv_sem=recv.at[a],
                                             device_id=(x, y, 1 - c), device_id_type=MESH) for a in range(len(ins))]

    def start():
        @pl.when(c != layer)
        def _():
            for cp in copies():
                cp.start()

    def finish():
        @pl.when(c != layer)
        def _():
            for cp in copies():
                cp.wait_send()

        @pl.when(c == layer)
        def _():
            for cp in copies():
                cp.wait_recv()

    return start, finish


def sibling_hook(gs, layer):
    return (gs, [jax.ShapeDtypeStruct(g.shape, g.dtype) for g in gs], len(gs),
            lambda i, o, s, r: _sibling_exchange(i, o, s, r, layer))


def chips_hook(ps, layer):
    return (ps, _chips_shapes(ps), 3 * len(ps), lambda i, o, s, r: _chips_exchange(i, o, s, r, layer))


def _chips_exchange(ins, outs, send, recv, layer):
    n = len(ins)
    _, _, c, chips = _place()

    def copies():
        return [pltpu.make_async_remote_copy(src_ref=ins[a].at[2 * cx + cy], dst_ref=outs[a].at[j], send_sem=send.at[3 * a + j],
                                             recv_sem=recv.at[3 * a + j], device_id=(cx, cy, c), device_id_type=MESH)
                for j, (cx, cy) in enumerate(chips) for a in range(n)]

    def start():
        @pl.when(c == layer)
        def _():
            for cp in copies():
                cp.start()

    def finish():
        @pl.when(c == layer)
        def _():
            for cp in copies():
                cp.wait()

    return start, finish


def _chips_shapes(ps):
    return [jax.ShapeDtypeStruct((3,) + p.shape[1:], p.dtype) for p in ps]


def exchange_chips(ps, layer, tag=""):
    n = len(ps)

    def body(*refs):
        send, recv, _ = refs[2 * n:]
        start, finish = _chips_exchange(refs[:n], refs[n:2 * n], send, recv, layer)
        start()
        finish()

    return _comm_call(f"exchange_chips{layer}{tag}", body, ps, _chips_shapes(ps), 3 * n, 0)


def exchange_final(rs, layer, small=None):
    n = len(rs)
    ins_all = list(rs) + ([small] if small is not None else [])

    def body(*refs):
        ni = len(ins_all)
        ins, outs = refs[:ni], refs[ni:2 * ni]
        send, recv, lsem = refs[2 * ni:]
        x, y, c, _ = _place()
        s = 2 * x + y
        to_sib = [pltpu.make_async_remote_copy(src_ref=ins[a], dst_ref=outs[a], send_sem=send.at[a], recv_sem=recv.at[a],
                                               device_id=(x, y, 1 - c), device_id_type=MESH) for a in range(n)]
        rel = [(fx, fy) for fx in (0, 1) for fy in (0, 1)]

        def piece(fx, fy, cc, dst_slot):
            k = n + 2 * (2 * fx + fy) + cc
            return pltpu.make_async_remote_copy(src_ref=ins[n], dst_ref=outs[n].at[dst_slot], send_sem=send.at[k], recv_sem=recv.at[k],
                                                device_id=(x ^ fx, y ^ fy, cc), device_id_type=MESH)

        @pl.when(c == layer)
        def _():
            for cp in to_sib:
                cp.start()
            if small is not None:
                own = pltpu.make_async_copy(ins[n], outs[n].at[s], lsem.at[0])
                own.start()
                sends = [piece(fx, fy, cc, s) for fx, fy in rel for cc in (0, 1) if (fx, fy) != (0, 0) or cc != layer]
                for cp in sends:
                    cp.start()
                for fx, fy in rel[1:]:
                    piece(fx, fy, layer, 2 * (x ^ fx) + (y ^ fy)).wait_recv()
                for cp in sends:
                    cp.wait_send()
                own.wait()
            for cp in to_sib:
                cp.wait_send()

        @pl.when(c != layer)
        def _():
            for cp in to_sib:
                cp.wait_recv()
            if small is not None:
                for fx, fy in rel:
                    piece(fx, fy, 1 - layer, 2 * (x ^ fx) + (y ^ fy)).wait_recv()

    out_shapes = [jax.ShapeDtypeStruct(r.shape, r.dtype) for r in rs]
    if small is not None:
        out_shapes.append(jax.ShapeDtypeStruct((4,) + small.shape, small.dtype))
    res = _comm_call(f"exchange_final{layer}", body, ins_all, out_shapes, n + 8, 1)
    return (res[:n], res[n]) if small is not None else (res[:n], None)


def _pad_in_cols(w):
    z = lambda n: jnp.zeros(w.shape[:-1] + (n,), w.dtype)
    return jnp.concatenate([w[..., :KR_OFF], z(64), w[..., KR_OFF:KR_OFF + MLA_ROPE], z(32), w[..., KR_OFF + MLA_ROPE:]], axis=-1)


def _pad_heads(w, real):
    k = w.shape[0]
    return jnp.pad(w.reshape(k, MLA_HEADS, real), ((0, 0), (0, 0), (0, LANES - real))).reshape(k, MLA_HEADS * LANES)


def _pad_gain(g, real):
    return jnp.pad(g.reshape(1, real), ((0, 0), (0, LANES - real)))


def layer_weights(full):
    w_ukv = full["w_ukv"].reshape(128, MLA_HEADS, 2, 64)
    two = lambda g: jnp.concatenate([g, g]).reshape(1, LANES)
    return dict(
        norm_g=full["norm_g"].reshape(1, -1), w_in=full["w_in"], conv_w=full["conv_w"], conv_b=full["conv_b"].reshape(1, -1),
        wgx=full["w_gate_x"], bgx=full["b_gate_x"].reshape(LRU_BLOCKS, 1, LANES),
        wga=full["w_gate_a"], bga=full["b_gate_a"].reshape(LRU_BLOCKS, 1, LANES),
        lam=full["lru_lambda"].reshape(1, -1), w_lru_o=full["w_lru_o"],
        cq_norm_g=full["cq_norm_g"].reshape(1, -1), ckv_norm_g=full["ckv_norm_g"].reshape(1, -1),
        wq=_pad_heads(full["w_uq"], MLA_QK), wk=_pad_heads(w_ukv[:, :, 0].reshape(128, 512), 64),
        wv=w_ukv[:, :, 1].reshape(128, 512),
        gq=_pad_gain(full["mla_q_norm_g"], MLA_QK), gk=_pad_gain(full["mla_k_norm_g"], MLA_QK),
        w_mla_o=full["w_mla_o"], gq2=two(full["dil_q_norm_g"]), gk2=two(full["dil_k_norm_g"]),
        w_dil_o=full["w_dil_o"], b_merge=full["b_merge"].reshape(1, -1), w_out=full["w_out"],
    )


def layer_fwd(x, w, tabs, gather=None):
    h, ht = rmsnorm_fwd(x, w["norm_g"])
    z_lru = mm_nn("in_proj_lru", h, w["w_in"], n=2048, n_off=G_LRU, tm=1024)
    z_mla = mm_nn("in_proj_mla", h, w["w_in"], n=1024, n_off=G_MLA, tm=1024)
    z_dil = mm_nn("in_proj_dil", h, w["w_in"], n=5120, n_off=G_DIL, tm=1024)
    z_mrg = mm_nn("in_proj_mrg", h, w["w_in"], n=3072, n_off=G_MRG, tm=1024)
    rest_hook = None if gather is None else gather_hook(gather[0][1:], gather[1])
    hs, y_lru, yt_lru, *ga_rest = lru_fwd(z_lru, w["conv_w"], w["conv_b"], w["wgx"], w["bgx"], w["wga"], w["bga"], w["lam"],
                                          comm=rest_hook)
    qm, km, vm, vtm = mla_prep_fwd(z_mla, tabs, w)
    o_mla, y_mla, yt_mla, lse_mla, *ga_first = mla_attn_fwd(qm, km, vtm, z_mla,
                                                            gather=None if gather is None else (gather[0][:1], gather[1]))
    gathered = ga_first + ga_rest
    os_, lses, dil_rm = [], [], []
    for gi, (_, dil) in enumerate(DIL_GROUPS):
        qkv = dil_prep_fwd(z_dil, tabs, w["gq2"], w["gk2"], gi)
        o, lse = dil_attn_fwd(f"dil_attn_fwd_{dil}", *qkv, dil)
        os_.append(o)
        lses.append(lse)
        dil_rm.append(qkv)
    y_dil, o_dil, lse_dil, yt_dil = dil_combine(os_, lses, z_dil)
    ps = [mm_nn("proj_lru", y_lru, w["w_lru_o"]), mm_nn("proj_mla", y_mla, w["w_mla_o"]), mm_nn("proj_dil", y_dil, w["w_dil_o"])]
    merged, merged_t = merge_fwd(ps, z_mrg, w["b_merge"])
    out = mm_nn("out_proj", merged, w["w_out"], add=x)
    res = dict(x=x, ht=ht, z_lru=z_lru, z_mla=z_mla, z_dil=z_dil, z_mrg=z_mrg, hs=hs, yt_lru=yt_lru, qm=qm, km=km, vm=vm, o_mla=o_mla,
               yt_mla=yt_mla, lse_mla=lse_mla, dil_rm=dil_rm, yt_dil=yt_dil, o_dil=o_dil, lse_dil=lse_dil, ps=ps, merged_t=merged_t)
    return out, res, gathered


def layer_bwd(dout, r, w, tabs, prev=None, own=None):
    g = {}
    dmerged = mm_nt("out_proj_dx", dout, w["w_out"])
    g["w_out"] = mm_nn("out_proj_dw", r["merged_t"], dout, tm=1024, tn=512, tk=2048)
    dp0, dp1, dp2, dzm0, dzm1, dzm2, db0, db1, db2 = merge_bwd(dmerged, r["ps"], r["z_mrg"], w["b_merge"])
    g["b_merge"] = jnp.concatenate([db0, db1, db2], axis=1).reshape(-1)
    dy_lru = mm_nt("proj_lru_dx", dp0, w["w_lru_o"])
    dy_mla = mm_nt("proj_mla_dx", dp1, w["w_mla_o"])
    dy_dil = mm_nt("proj_dil_dx", dp2, w["w_dil_o"])
    g["w_lru_o"] = mm_nn("proj_lru_dw", r["yt_lru"], dp0, **DW_TILES)
    g["w_mla_o"] = mm_nn("proj_mla_dw", r["yt_mla"], dp1, **DW_TILES)
    g["w_dil_o"] = mm_nn("proj_dil_dw", r["yt_dil"], dp2, **DW_TILES)
    dzx, dzg_lru, dwgx, dbgx, dwga, dbga, dlam, dcw, dcb, *la_prev = lru_bwd(
        r["z_lru"], r["hs"], dy_lru, w["conv_w"], w["conv_b"], w["wgx"], w["bgx"], w["wga"], w["bga"], w["lam"],
        comm=None if prev is None else sibling_hook(prev[0], prev[1]))
    chips = None if prev is None else (prev[2](la_prev), prev[1])
    g.update(w_gate_x=dwgx, b_gate_x=dbgx.reshape(LRU_BLOCKS, LANES), w_gate_a=dwga, b_gate_a=dbga.reshape(LRU_BLOCKS, LANES),
             lru_lambda=dlam.reshape(-1), conv_w=dcw, conv_b=dcb.reshape(-1))
    do_m, dzg_mla, dd_m = mla_gate_bwd(dy_mla, r["o_mla"], r["z_mla"])
    dq_m, dk_m, dv_m, *lb_prev = mla_attn_bwd(r["qm"], r["km"], r["vm"], do_m, r["lse_mla"], dd_m, chips=chips)
    dz_mla3, dg_cq, dg_ckv, dwq, dwk, dwv, dgq, dgk = mla_prep_bwd(r["z_mla"], tabs, w, dq_m, dk_m, dv_m)
    g.update(cq_norm_g=dg_cq.reshape(-1), ckv_norm_g=dg_ckv.reshape(-1), mla_q_norm_g=dgq[0, :MLA_QK], mla_k_norm_g=dgk[0, :MLA_QK])
    g["w_uq"] = dwq.reshape(256, MLA_HEADS, LANES)[:, :, :MLA_QK].reshape(256, MLA_HEADS * MLA_QK)
    g["w_ukv"] = jnp.concatenate([dwk.reshape(128, MLA_HEADS, LANES)[:, :, :64], dwv.reshape(128, MLA_HEADS, 64)], axis=2).reshape(128, 1024)
    dzg_dil, do1, dd1, do2, dd2, l2, do3, dd3, l3 = dil_gate_bwd(dy_dil, r["o_dil"], r["z_dil"], r["lse_dil"])
    stats = [(do1, r["lse_dil"], dd1), (do2, l2, dd2), (do3, l3, dd3)]
    dzq, dzk, dzv, dgq2, dgk2 = [], [], [], [], []
    for gi, (_, dil) in enumerate(DIL_GROUPS):
        dq, dk, dv = dil_attn_bwd(f"dil_attn_bwd_{dil}", *r["dil_rm"][gi], *stats[gi], dil)
        parts = dil_prep_bwd(r["z_dil"], tabs, w["gq2"], w["gk2"], dq, dk, dv, gi)
        for acc, part in zip((dzq, dzk, dzv, dgq2, dgk2), parts):
            acc.append(part)
    g.update(dil_q_norm_g=sum(dgq2)[0, :DIL_HD], dil_k_norm_g=sum(dgk2)[0, :DIL_HD])
    dz = jnp.concatenate([dzx, dzg_lru, dz_mla3, dzg_mla] + dzq + dzk + dzv + [dzg_dil, dzm0, dzm1, dzm2], axis=1)
    g["w_in"] = in_proj_dw(r["ht"], dz)
    own_hook = None if own is None else chips_hook(*own(g))
    out = mm_nt("in_proj_dx", dz, w["w_in"], tm=1024, tn=1024, tk=IN_PAD // 4, comm=own_hook)
    dh, lb_own = (out, []) if own_hook is None else (out[0], out[1:])
    dx, dng = rmsnorm_bwd(r["x"], dh, dout, w["norm_g"])
    g["norm_g"] = dng.reshape(-1)
    return dx, g, dict(la_prev=la_prev, lb_prev=lb_prev, lb_own=lb_own)


def local_step(x, positions, target, full0, full1=None, gather1=None, reduce=None):
    tabs = rope_tables(positions.reshape(-1, 1))
    ws, ress = [], []
    for l in range(2):
        if l == 0:
            ws.append(layer_weights(full0))
            x, res, gathered = layer_fwd(x, ws[0], tabs, gather=None if gather1 is None else (gather1[0], 1))
        else:
            ws.append(layer_weights(full1 if gather1 is None else gather1[1](gathered)))
            x, res, _ = layer_fwd(x, ws[1], tabs)
        ress.append(res)
    dy, loss = loss_head(x, target)
    dy, grads1, _ = layer_bwd(dy, ress[1], ws[1], tabs)
    prev1, own0 = (None, None) if reduce is None else (reduce[0](grads1), reduce[1])
    dy, grads0, landed = layer_bwd(dy, ress[0], ws[0], tabs, prev=prev1, own=own0)
    return loss, dy, [grads0, grads1], landed


WEIGHTS = ["norm_g", "w_in", "conv_w", "conv_b", "w_gate_x", "b_gate_x", "w_gate_a", "b_gate_a", "lru_lambda", "w_lru_o", "cq_norm_g",
           "ckv_norm_g", "w_uq", "w_ukv", "mla_q_norm_g", "mla_k_norm_g", "w_mla_o", "dil_q_norm_g", "dil_k_norm_g", "w_dil_o", "b_merge",
           "w_out"]
SHARDED = {"w_in": 2, "conv_w": 2, "w_lru_o": 1, "w_uq": 2, "w_ukv": 2, "w_mla_o": 2, "w_dil_o": 2, "w_out": 1}
REPLICATED = [n for n in WEIGHTS if n not in SHARDED]
SMALL_ROWS = 144


def kernel(x, positions, norm_g, w_in, conv_w, conv_b, w_gate_x, b_gate_x, w_gate_a, b_gate_a, lru_lambda, w_lru_o, cq_norm_g, ckv_norm_g, w_uq, w_ukv, mla_q_norm_g, mla_k_norm_g, w_mla_o, dil_q_norm_g, dil_k_norm_g, w_dil_o, b_merge, w_out, loss_target, m_norm_g, m_w_in, m_conv_w, m_conv_b, m_w_gate_x, m_b_gate_x, m_w_gate_a, m_b_gate_a, m_lru_lambda, m_w_lru_o, m_cq_norm_g, m_ckv_norm_g, m_w_uq, m_w_ukv, m_mla_q_norm_g, m_mla_k_norm_g, m_w_mla_o, m_dil_q_norm_g, m_dil_k_norm_g, m_w_dil_o, m_b_merge, m_w_out, v_norm_g, v_w_in, v_conv_w, v_conv_b, v_w_gate_x, v_b_gate_x, v_w_gate_a, v_b_gate_a, v_lru_lambda, v_w_lru_o, v_cq_norm_g, v_ckv_norm_g, v_w_uq, v_w_ukv, v_mla_q_norm_g, v_mla_k_norm_g, v_w_mla_o, v_dil_q_norm_g, v_dil_k_norm_g, v_w_dil_o, v_b_merge, v_w_out):
    args = locals()
    w = {n: args[n] for n in WEIGHTS}
    m = {n: args["m_" + n] for n in WEIGHTS}
    v = {n: args["v_" + n] for n in WEIGHTS}
    my_c = lax.axis_index("c").astype(jnp.int32)
    my_s = (2 * lax.axis_index("x") + lax.axis_index("y")).astype(jnp.int32)
    c_idx = my_c.reshape(1)
    s_idx = my_s.reshape(1)

    names = list(SHARDED)
    wire = [[w[n][l] if n == "conv_w" else w[n][l].astype(BF16) for n in names] for l in range(2)]

    def assemble(l, gathered):
        full = {n: w[n][l] for n in REPLICATED}
        for n, ga, own in zip(names, gathered, wire[l]):
            parts = [jnp.where(my_s == s, own, ga[s]) for s in range(4)]
            if n == "w_in":
                parts = [_pad_in_cols(parts[0])] + parts[1:]
            full[n] = jnp.concatenate(parts, axis=SHARDED[n] - 1)
        return full

    full0 = assemble(0, allgather_layer(wire[0], 0))

    def shards_of(grads_l):
        gs = []
        for n in names:
            g_ = grads_l[n]
            if n == "w_in":
                gs.append(g_.reshape(4, IN_WIDTH // 4, D_MODEL))
            else:
                parts = jnp.stack(jnp.split(g_, 4, axis=SHARDED[n] - 1))
                gs.append(parts.reshape(4, -1, parts.shape[-1]))
        return gs

    def pair_sums(l, gs, la, tags):
        dts = [BF16 if g_.size >= 2**19 and t != "small" else F32 for g_, t in zip(gs, tags)]
        return [sum_pair(f"sum_pair{l}_{t}", g_, l_, dt) for t, g_, l_, dt in zip(tags, gs, la, dts)]

    def chip_sums(l, gs, la, lb, tags):
        return [sum_chips(f"sum_chips{l}_{t}", g_, l_, b_, s_idx) for t, g_, l_, b_ in zip(tags, gs, la, lb)]

    stash = {}

    def prev1(grads1):
        stash["gs1"] = shards_of(grads1)
        return stash["gs1"], 1, lambda la: pair_sums(1, stash["gs1"], la, names)

    def own0(grads0):
        stash["gs0"] = shards_of(grads0)
        stash["la0"] = exchange_sibling(stash["gs0"], 0)
        return pair_sums(0, stash["gs0"], stash["la0"], names), 0

    loss, grad_x, grads, landed = local_step(x[0], positions[0], loss_target[0], full0,
                                             gather1=(wire[1], functools.partial(assemble, 1)), reduce=(prev1, own0))
    loss = lax.psum(loss, ("x", "y", "c"))
    reduced1 = chip_sums(1, stash["gs1"], landed["la_prev"], landed["lb_prev"], names)
    received1, _ = exchange_final(reduced1, 1)
    reduced0 = chip_sums(0, stash["gs0"], stash["la0"], landed["lb_own"], names)

    flat = jnp.concatenate([jnp.stack([grads[0][n], grads[1][n]]).reshape(-1) for n in REPLICATED])
    small = [jnp.pad(flat, (0, 4 * SMALL_ROWS * 1024 - flat.size)).reshape(4, SMALL_ROWS, 1024)]
    la_s = exchange_sibling(small, 0, tag="_small")
    lb_s = exchange_chips(pair_sums(0, small, la_s, ["small"]), 0, tag="_small")
    received0, small_all = exchange_final(reduced0, 0, small=chip_sums(0, small, la_s, lb_s, ["small"])[0])

    g_local, delta, new_m, new_v = {}, {}, {}, {}
    for i, n in enumerate(names):
        shp = w[n].shape
        if n == "w_in":
            as3 = lambda a: a.transpose(0, 2, 1)
            back = lambda o: o.transpose(0, 2, 1)
        else:
            as3 = lambda a, rc=reduced0[i].shape: a.reshape((2,) + rc)
            back = lambda o, shp=shp: o.reshape(shp)
        outs = adamw_pair(f"adamw_{n}", as3(w[n]), (reduced0[i], reduced1[i]), (received0[i], received1[i]), as3(m[n]), as3(v[n]),
                          c_idx)
        g_local[n], delta[n], new_m[n], new_v[n] = [back(o) for o in outs]
    flat = small_all.reshape(-1)
    off = 0
    for n in REPLICATED:
        g_local[n] = flat[off:off + w[n].size].reshape(w[n].shape)
        off += w[n].size
        delta[n], new_m[n], new_v[n] = adamw(f"adamw_{n}", w[n], g_local[n], m[n], v[n])
    return (loss, grad_x[None], *[g_local[n] for n in WEIGHTS], *[delta[n] for n in WEIGHTS], *[new_m[n] for n in WEIGHTS],
            *[new_v[n] for n in WEIGHTS])
```

```python
import functools

import jax
import jax.numpy as jnp
from jax import lax
from jax.experimental import pallas as pl
from jax.experimental.pallas import tpu as pltpu

F32 = jnp.float32
BF16 = jnp.bfloat16
MXU_DTYPE = jnp.bfloat16

D_MODEL = 1024
EPS = 1e-6
ROPE_THETA = 10000.0
LRU_BLOCKS = 8
LRU_C = 8.0
MLA_HEADS = 8
MLA_NOPE = 64
MLA_ROPE = 32
MLA_QK = 96
DIL_GROUPS = ((128, 1), (512, 4), (2048, 16))
DIL_HD = 64
DIL_NK = 128
MLA_HPS = 8
IN_WIDTH = 11168
ADAM_LR, ADAM_B1, ADAM_B2, ADAM_EPS, ADAM_WD, ADAM_STEP = 0.001, 0.9, 0.999, 1e-08, 0.01, 10

LANES = 128
G_LRU, G_MLA, G_DIL, G_MRG = 0, 2048, 3072, 8192
IN_PAD = 11264
KR_OFF = 2432

NN = (((1,), (0,)), ((), ()))
NT = (((1,), (1,)), ((), ()))
TN = (((0,), (0,)), ((), ()))
NEG = -1e30
MESH = pl.DeviceIdType.MESH
VMEM_LIMIT = 48 * 2**20


def _cparams(sem):
    return pltpu.CompilerParams(dimension_semantics=sem, vmem_limit_bytes=VMEM_LIMIT)


def _dot(a, b, dims):
    return lax.dot_general(a.astype(MXU_DTYPE), b.astype(MXU_DTYPE), dims, preferred_element_type=F32)


@jax.custom_vjp
def mm(a, w):
    return _dot(a, w, NN)


def _mm_fwd(a, w):
    return _dot(a, w, NN), (a, w)


def _mm_bwd(res, g):
    a, w = res
    return _dot(g, w, NT), _dot(a, g, TN)


mm.defvjp(_mm_fwd, _mm_bwd)


def _seg64_matrix():
    r = lax.broadcasted_iota(jnp.int32, (LANES, LANES), 0) < DIL_HD
    c = lax.broadcasted_iota(jnp.int32, (LANES, LANES), 1) < DIL_HD
    return (r == c).astype(BF16)


def _seg_sum_impl(x):
    b = _seg64_matrix()
    hi = x.astype(BF16)
    r1 = x - hi.astype(F32)
    mid = r1.astype(BF16)
    lo = (r1 - mid.astype(F32)).astype(BF16)
    dot = lambda u: lax.dot_general(u, b, NN, preferred_element_type=F32)
    return dot(hi) + dot(mid) + dot(lo)


@jax.custom_vjp
def seg_sum(x):
    return _seg_sum_impl(x)


seg_sum.defvjp(lambda x: (_seg_sum_impl(x), None), lambda _, g: (_seg_sum_impl(g),))


def _lroll_impl(x, s):
    return pltpu.roll(x, s % LANES, 1)


@functools.partial(jax.custom_vjp, nondiff_argnums=(1,))
def lroll(x, s):
    return _lroll_impl(x, s)


lroll.defvjp(lambda x, s: (_lroll_impl(x, s), None), lambda s, _, g: (_lroll_impl(g, -s),))


def _split_heads_impl(x):
    return tuple(x[:, LANES * hd:LANES * (hd + 1)] for hd in range(MLA_HEADS))


@jax.custom_vjp
def split_heads(x):
    return _split_heads_impl(x)


split_heads.defvjp(lambda x: (_split_heads_impl(x), None), lambda _, gs: (jnp.concatenate(gs, axis=1),))


class _Ops:
    def __init__(self, diff):
        self.mm = mm if diff else (lambda a, w: _dot(a, w, NN))
        self.seg_sum = seg_sum if diff else _seg_sum_impl
        self.lroll = lroll if diff else _lroll_impl
        self.split_heads = split_heads if diff else _split_heads_impl


PLAIN, DIFF = _Ops(False), _Ops(True)


def rms(x, g, n):
    ms = jnp.sum(x * x, axis=-1, keepdims=True) * (1.0 / n)
    return x * lax.rsqrt(ms + EPS) * g


def rope(ops, x, c, s1, s2, half):
    return x * c + ops.lroll(x, -half) * s1 + ops.lroll(x, half) * s2


def _sigmoid(x):
    return 1.0 / (1.0 + jnp.exp(-x))


def _silu_and_grad(g):
    sg = _sigmoid(g)
    return g * sg, sg * (1.0 + g * (1.0 - sg))


def _softplus(x):
    return jnp.maximum(x, 0.0) + jnp.log(1.0 + jnp.exp(-jnp.abs(x)))


def _expm1(y):
    series = y * (1.0 + y * (0.5 + y * (1.0 / 6.0 + y * (1.0 / 24.0 + y * (1.0 / 120.0)))))
    return jnp.where(y > -0.05, series, jnp.exp(jnp.minimum(y, -0.05)) - 1.0)


@jax.custom_vjp
def sigmoid_d(x):
    return _sigmoid(x)


def _sigmoid_d_fwd(x):
    s = _sigmoid(x)
    return s, s


sigmoid_d.defvjp(_sigmoid_d_fwd, lambda s, g: (g * s * (1.0 - s),))


def _decay_mult(log_a):
    return jnp.sqrt(-_expm1(2.0 * log_a))


@jax.custom_vjp
def decay_mult_d(log_a):
    return _decay_mult(log_a)


def _decay_mult_d_fwd(log_a):
    em = _expm1(2.0 * log_a)
    mult = jnp.sqrt(-em)
    return mult, (em, mult)


decay_mult_d.defvjp(_decay_mult_d_fwd, lambda res, g: (-g * (res[0] + 1.0) / res[1],))
PLAIN.sigmoid, DIFF.sigmoid = _sigmoid, sigmoid_d
PLAIN.decay_mult, DIFF.decay_mult = _decay_mult, decay_mult_d


def _host(body, n_in, n_out, n_scr, hook, grid):
    if hook is None:
        return body, [], [], [], [], []
    ins, out_shapes, n_sem, make = hook
    ni, no = len(ins), len(out_shapes)
    hbm = pl.BlockSpec(memory_space=pltpu.HBM)

    def at(corner):
        cond = None
        for axis, size in enumerate(grid):
            here = pl.program_id(axis) == (size - 1 if corner else 0)
            cond = here if cond is None else cond & here
        return cond

    def hosted(*refs):
        a, refs = refs[:n_in], refs[n_in:]
        xi, refs = refs[:ni], refs[ni:]
        o, refs = refs[:n_out], refs[n_out:]
        xo, refs = refs[:no], refs[no:]
        scr, (send, recv) = refs[:n_scr], refs[n_scr:]
        start, finish = make(xi, xo, send, recv)
        pl.when(at(False))(start)
        body(*a, *o, *scr)
        pl.when(at(True))(finish)

    sems = [pltpu.SemaphoreType.DMA((n_sem,)), pltpu.SemaphoreType.DMA((n_sem,))]
    return hosted, list(ins), [hbm] * ni, [hbm] * no, list(out_shapes), sems


def _mm_call(name, a, b, *, mode, m, n, k, a_blk, b_blk, out_dtype, add, tm, tn, tk, comm=None):
    nk = k // tk
    assert m % tm == 0 and n % tn == 0 and k % tk == 0, (name, m, n, k, tm, tn, tk)
    dims = {"nn": NN, "nt": NT, "tn": TN}[mode]

    def body(*refs):
        if add is None:
            a_ref, b_ref, o_ref, *scr = refs
            add_ref = None
        else:
            a_ref, b_ref, add_ref, o_ref, *scr = refs
        part = _dot(a_ref[...], b_ref[...], dims)

        def finish(acc):
            if add_ref is not None:
                acc = acc + add_ref[...]
            o_ref[...] = acc.astype(o_ref.dtype)

        if nk == 1:
            finish(part)
        else:
            (acc_ref,) = scr
            kk = pl.program_id(2)

            @pl.when(kk == 0)
            def _():
                acc_ref[...] = part

            @pl.when(kk > 0)
            def _():
                acc_ref[...] += part

            @pl.when(kk == nk - 1)
            def _():
                finish(acc_ref[...])

    in_specs = [a_blk, b_blk]
    args = [a, b]
    if add is not None:
        in_specs.append(pl.BlockSpec((tm, tn), lambda j, i, kk: (i, j)))
        args.append(add)
    grid = (n // tn, m // tm, nk)
    scratch = [] if nk == 1 else [pltpu.VMEM((tm, tn), F32)]
    body, x_in, x_in_specs, x_out_specs, x_out_shapes, sems = _host(body, len(args), 1, len(scratch), comm, grid)
    res = pl.pallas_call(
        body,
        name=name,
        grid=grid,
        in_specs=in_specs + x_in_specs,
        out_specs=[pl.BlockSpec((tm, tn), lambda j, i, kk: (i, j))] + x_out_specs,
        out_shape=[jax.ShapeDtypeStruct((m, n), out_dtype)] + x_out_shapes,
        scratch_shapes=scratch + sems,
        compiler_params=_cparams(("arbitrary",) * 3 if comm is not None else ("parallel", "parallel", "arbitrary")),
    )(*args, *x_in)
    return res[0] if comm is None else res


def mm_nn(name, a, b, *, n=None, n_off=0, out_dtype=F32, add=None, tm=1024, tn=1024, tk=1024):
    m, k = a.shape
    n = b.shape[1] if n is None else n
    tm, tn, tk = min(tm, m), min(tn, n), min(k, tk)
    ob = n_off // tn
    assert n_off % tn == 0
    return _mm_call(name, a, b, mode="nn", m=m, n=n, k=k, out_dtype=out_dtype, add=add, tm=tm, tn=tn, tk=tk,
                    a_blk=pl.BlockSpec((tm, tk), lambda j, i, kk: (i, kk)),
                    b_blk=pl.BlockSpec((tk, tn), lambda j, i, kk: (kk, j + ob)))


def mm_nt(name, a, b, *, out_dtype=F32, tm=1024, tn=1024, tk=1024, comm=None):
    m, k = a.shape
    n = b.shape[0]
    tm, tn, tk = min(tm, m), min(tn, n), min(k, tk)
    return _mm_call(name, a, b, mode="nt", m=m, n=n, k=k, out_dtype=out_dtype, add=None, tm=tm, tn=tn, tk=tk, comm=comm,
                    a_blk=pl.BlockSpec((tm, tk), lambda j, i, kk: (i, kk)),
                    b_blk=pl.BlockSpec((tn, tk), lambda j, i, kk: (j, kk)))


DW_TILES = dict(tm=1024, tn=512, tk=4096)


def in_proj_dw(ht, dz):
    t = ht.shape[1]
    tn = 512
    nchunk = IN_PAD // tn
    kr_chunk = KR_OFF // tn
    lo = KR_OFF - kr_chunk * tn

    def body(ht_ref, dz_ref, out_ref, buf, sems):
        j = pl.program_id(0)
        slot = j % 2
        buf[slot] = _dot(ht_ref[...], dz_ref[...], NN).T

        def copies(c, s):
            dst = jnp.where(c < kr_chunk, c * tn, (c - 1) * tn + lo + MLA_ROPE)
            plain = [pltpu.make_async_copy(buf.at[s], out_ref.at[pl.ds(pl.multiple_of(dst, 8), tn)], sems.at[s, 0])]
            split = [pltpu.make_async_copy(buf.at[s, 0:lo], out_ref.at[KR_OFF - lo:KR_OFF], sems.at[s, 0]),
                     pltpu.make_async_copy(buf.at[s, lo + 64:lo + 96], out_ref.at[KR_OFF:KR_OFF + MLA_ROPE], sems.at[s, 1])]
            return plain, split

        def each(c, s, action):
            plain, split = copies(c, s)

            @pl.when(c != kr_chunk)
            def _():
                for cp in plain:
                    action(cp)

            @pl.when(c == kr_chunk)
            def _():
                for cp in split:
                    action(cp)

        @pl.when(j > 0)
        def _():
            each(j - 1, 1 - slot, lambda cp: cp.wait())

        each(j, slot, lambda cp: cp.start())

        @pl.when(j == nchunk - 1)
        def _():
            each(j, slot, lambda cp: cp.wait())

    return pl.pallas_call(
        body,
        name="in_proj_dw",
        grid=(nchunk,),
        in_specs=[pl.BlockSpec((D_MODEL, t), lambda j: (0, 0)), pl.BlockSpec((t, tn), lambda j: (0, j))],
        out_specs=pl.BlockSpec(memory_space=pltpu.HBM),
        out_shape=jax.ShapeDtypeStruct((IN_WIDTH, D_MODEL), F32),
        scratch_shapes=[pltpu.VMEM((2, tn, D_MODEL), F32), pltpu.SemaphoreType.DMA((2, 2))],
        compiler_params=_cparams(("arbitrary",)),
    )(ht, dz)


def rowwise(name, fn, t, *, tm=256, ncol=1, ins=(), outs=(), touts=(), accs=()):
    n_in, n_out, n_acc = len(ins), len(outs) + len(touts), len(accs)

    def zero_when(ref, cond):
        @pl.when(cond)
        def _():
            ref[...] = jnp.zeros(ref.shape, ref.dtype)

    def body(*refs):
        in_refs, out_refs, acc_refs = refs[:n_in], refs[n_in:n_in + n_out], refs[n_in + n_out:]
        j, i = pl.program_id(0), pl.program_id(1)
        for ref, (_, _, _, cd) in zip(acc_refs, accs):
            zero_when(ref, (i == 0) if cd else ((i == 0) & (j == 0)))
        fn(in_refs, out_refs, acc_refs)

    def in_spec(arr, w, base, cd, rd):
        rows = tm if rd else arr.shape[0]
        return pl.BlockSpec((rows, w), lambda j, i: (i if rd else 0, base + (j if cd else 0)))

    in_specs = [in_spec(*e) for e in ins]
    out_specs = [pl.BlockSpec((tm, w), (lambda j, i, cd=cd: (i, j if cd else 0))) for (_, w, _, cd) in outs]
    out_specs += [pl.BlockSpec((w, tm), (lambda j, i, cd=cd: (j if cd else 0, i))) for (_, w, _, cd) in touts]
    out_specs += [pl.BlockSpec((r, w), (lambda j, i, cd=cd: (0, j if cd else 0))) for (r, _, w, cd) in accs]
    out_shape = [jax.ShapeDtypeStruct((t, c), dt) for (c, _, dt, _) in outs]
    out_shape += [jax.ShapeDtypeStruct((r, t), dt) for (r, _, dt, _) in touts]
    out_shape += [jax.ShapeDtypeStruct((r, c), F32) for (r, c, _, _) in accs]
    res = pl.pallas_call(
        body,
        name=name,
        grid=(ncol, t // tm),
        in_specs=in_specs,
        out_specs=out_specs,
        out_shape=out_shape,
        compiler_params=_cparams(("arbitrary", "arbitrary") if accs else ("parallel", "parallel")),
    )(*[e[0] for e in ins])
    return res


def _row(arr, w=None, base=0, cd=False):
    return (arr, arr.shape[1] if w is None else w, base, cd, True)


def _const(arr, w=None, base=0, cd=False):
    return (arr, arr.shape[1] if w is None else w, base, cd, False)


def rope_tables(positions):
    t = positions.shape[0]

    def fn(ins, outs, _):
        pos = ins[0][...].astype(F32)
        lane = lax.broadcasted_iota(jnp.int32, (1, LANES), 1)
        log_theta = jnp.log(jnp.float32(ROPE_THETA))
        jm = lane - MLA_NOPE
        idx = jnp.clip(jnp.where(jm < 16, jm, jm - 16), 0, 15).astype(F32)
        ang = pos * jnp.exp(-(idx * (2.0 / MLA_ROPE)) * log_theta)
        cos, sin = jnp.cos(ang), jnp.sin(ang)
        in_rope = (lane >= MLA_NOPE) & (lane < MLA_QK)
        outs[0][:, 0:128] = jnp.where(lane < MLA_NOPE, 1.0, jnp.where(in_rope, cos, 0.0))
        outs[0][:, 128:256] = jnp.where(in_rope & (jm < 16), -sin, 0.0)
        outs[0][:, 256:384] = jnp.where(in_rope & (jm >= 16), sin, 0.0)
        jd = lane & (DIL_HD - 1)
        idx = (jd & 31).astype(F32)
        ang = pos * jnp.exp(-(idx * (2.0 / DIL_HD)) * log_theta)
        cos, sin = jnp.cos(ang), jnp.sin(ang)
        outs[0][:, 384:512] = cos
        outs[0][:, 512:640] = jnp.where(jd < 32, -sin, 0.0)
        outs[0][:, 640:768] = jnp.where(jd >= 32, sin, 0.0)

    return rowwise("rope_tables", fn, t, tm=1024, ins=[_row(positions)], outs=[(768, 768, F32, False)])[0]


def rmsnorm_fwd(x, g):
    def fn(ins, outs, _):
        h = rms(ins[0][...], ins[1][...], D_MODEL)
        outs[0][...] = h.astype(MXU_DTYPE)
        outs[1][...] = h.T.astype(MXU_DTYPE)

    return rowwise("rmsnorm_fwd", fn, x.shape[0], tm=512, ins=[_row(x), _const(g)], outs=[(D_MODEL, D_MODEL, MXU_DTYPE, False)],
                   touts=[(D_MODEL, D_MODEL, MXU_DTYPE, False)])


def rmsnorm_bwd(x, dh, dres, g):
    def fn(ins, outs, accs):
        _, vjp = jax.vjp(lambda xv, gv: rms(xv, gv, D_MODEL), ins[0][...], ins[3][...])
        dx, dg = vjp(ins[1][...])
        outs[0][...] = ins[2][...] + dx
        accs[0][...] += dg

    return rowwise("rmsnorm_bwd", fn, x.shape[0], tm=512, ins=[_row(x), _row(dh), _row(dres), _const(g)],
                   outs=[(D_MODEL, D_MODEL, F32, False)], accs=[(1, D_MODEL, D_MODEL, False)])


def loss_head(y, target):
    def fn(ins, outs, accs):
        err = ins[0][...] - ins[1][...]
        outs[0][...] = err * (1.0 / D_MODEL)
        accs[0][...] += jnp.sum(err * err, axis=0, keepdims=True)
        accs[1][...] = jnp.broadcast_to(jnp.sum(accs[0][...], keepdims=True), (1, LANES))

    dy, _, tot = rowwise("loss_head", fn, y.shape[0], tm=512, ins=[_row(y), _row(target)], outs=[(D_MODEL, D_MODEL, F32, False)],
                         accs=[(1, D_MODEL, D_MODEL, False), (1, LANES, LANES, False)])
    return dy, tot[0, 0] * (0.5 / D_MODEL)


def _shift_rows(v, d, fill, reverse):
    tb = v.shape[0]
    if d % 8 == 0:
        pad = jnp.full((d, v.shape[1]), fill, v.dtype)
        return jnp.concatenate([v[d:], pad] if reverse else [pad, v[:tb - d]], axis=0)
    rows = lax.broadcasted_iota(jnp.int32, v.shape, 0)
    if not reverse:
        return jnp.where(rows >= d, pltpu.roll(v, d, 0), fill)
    return jnp.where(rows < tb - d, pltpu.roll(v, tb - d, 0), fill)


def _scan_tile(a, b, reverse):
    d = 1
    while d < a.shape[0]:
        b = b + a * _shift_rows(b, d, 0.0, reverse)
        a = a * _shift_rows(a, d, 1.0, reverse)
        d *= 2
    return a, b


def _lru_gates(ops, xc, wgx, bgx, wga, bga, lam):
    gx = ops.sigmoid(ops.mm(xc, wgx) + bgx)
    ga = ops.sigmoid(ops.mm(xc, wga) + bga)
    log_a = -LRU_C * ga * _softplus(-lam)
    a = jnp.exp(log_a)
    return a, ops.decay_mult(log_a) * (gx * xc)


def _shifted_inputs(x, halo, tb):
    rows = lax.broadcasted_iota(jnp.int32, x.shape, 0)
    pad = jnp.zeros((tb - 8, LANES), F32)
    out = []
    for d in (3, 2, 1):
        head = jnp.concatenate([pltpu.roll(halo, d, 0), pad], axis=0)
        out.append(jnp.where(rows >= d, pltpu.roll(x, d, 0), head))
    return out + [x]


def _lru_specs(nt, tb, reverse):
    hb = tb // 8
    tt = (lambda t: nt - 1 - t) if reverse else (lambda t: t)
    blk = lambda off: pl.BlockSpec((tb, LANES), lambda n, t: (tt(t), n + off))
    halo = lambda off: pl.BlockSpec((8, LANES), lambda n, t: (jnp.maximum(tt(t) * hb - 1, 0), n + off))
    chan = lambda r: pl.BlockSpec((r, LANES), lambda n, t: (0, n))
    wblk = pl.BlockSpec((None, LANES, LANES), lambda n, t: (n, 0, 0))
    bblk = pl.BlockSpec((None, 1, LANES), lambda n, t: (n, 0, 0))
    return blk, halo, chan, wblk, bblk


def lru_fwd(z, conv_w, conv_b, wgx, bgx, wga, bga, lam, *, tb=256, comm=None):
    t = z.shape[0]
    nt = t // tb
    blk, halo, chan, wblk, bblk = _lru_specs(nt, tb, False)

    def body(x_ref, xh_ref, g_ref, cw_ref, cb_ref, wgx_ref, bgx_ref, wga_ref, bga_ref, lam_ref, h_ref, y_ref, yt_ref, carry_ref):
        ti = pl.program_id(1)

        @pl.when(ti == 0)
        def _():
            carry_ref[...] = jnp.zeros((8, LANES), F32)

        x = x_ref[...]
        hal = jnp.where(ti > 0, xh_ref[...], 0.0)
        xs = _shifted_inputs(x, hal, tb)
        xc = cb_ref[...] + sum(xs[k] * cw_ref[k:k + 1, :] for k in range(4))
        a, b = _lru_gates(PLAIN, xc, wgx_ref[...], bgx_ref[...], wga_ref[...], bga_ref[...], lam_ref[...])
        acum, h0 = _scan_tile(a, b, False)
        h = h0 + acum * carry_ref[7:8, :]
        carry_ref[...] = h[tb - 8:tb, :]
        h_ref[...] = h
        y = h * _silu_and_grad(g_ref[...])[0]
        y_ref[...] = y.astype(MXU_DTYPE)
        yt_ref[...] = y.T.astype(MXU_DTYPE)

    grid = (LRU_BLOCKS, nt)
    body, x_in, x_in_specs, x_out_specs, x_out_shapes, sems = _host(body, 10, 3, 1, comm, grid)
    return pl.pallas_call(
        body,
        name="lru_fwd" if comm is None else "lru_fwd_comm",
        grid=grid,
        in_specs=[blk(0), halo(0), blk(LRU_BLOCKS), chan(4), chan(1), wblk, bblk, wblk, bblk, chan(1)] + x_in_specs,
        out_specs=[blk(0), blk(0), pl.BlockSpec((LANES, tb), lambda n, t_: (n, t_))] + x_out_specs,
        out_shape=[jax.ShapeDtypeStruct((t, 1024), F32), jax.ShapeDtypeStruct((t, 1024), MXU_DTYPE),
                   jax.ShapeDtypeStruct((1024, t), MXU_DTYPE)] + x_out_shapes,
        scratch_shapes=[pltpu.VMEM((8, LANES), F32)] + sems,
        compiler_params=_cparams(("arbitrary", "arbitrary") if comm is not None else ("parallel", "arbitrary")),
    )(z, z, z, conv_w, conv_b, wgx, bgx, wga, bga, lam, *x_in)


def lru_bwd(z, h, dy, conv_w, conv_b, wgx, bgx, wga, bga, lam, *, tb=256, comm=None):
    t = z.shape[0]
    nt = t // tb
    blk, halo, chan, wblk, bblk = _lru_specs(nt, tb, True)

    def body(x_ref, xh_ref, g_ref, h_ref, hh_ref, dy_ref, cw_ref, cb_ref, wgx_ref, bgx_ref, wga_ref, bga_ref, lam_ref,
             dzx_ref, dzg_ref, dwgx_ref, dbgx_ref, dwga_ref, dbga_ref, dlam_ref, dcw_ref, dcb_ref,
             gcar_ref, acar_ref, xcar_ref):
        ti = pl.program_id(1)
        has_earlier = ti < nt - 1

        @pl.when(ti == 0)
        def _():
            for ref in (dwgx_ref, dbgx_ref, dwga_ref, dbga_ref, dlam_ref, dcw_ref, dcb_ref, gcar_ref, acar_ref, xcar_ref):
                ref[...] = jnp.zeros(ref.shape, F32)

        rows = lax.broadcasted_iota(jnp.int32, (tb, LANES), 0)
        x = x_ref[...]
        hal = jnp.where(has_earlier, xh_ref[...], 0.0)
        xs = _shifted_inputs(x, hal, tb)
        xc = cb_ref[...] + sum(xs[k] * cw_ref[k:k + 1, :] for k in range(4))
        (a, _), vjp = jax.vjp(functools.partial(_lru_gates, DIFF), xc, wgx_ref[...], bgx_ref[...], wga_ref[...],
                              bga_ref[...], lam_ref[...])
        g, h, dyv = g_ref[...], h_ref[...], dy_ref[...]
        silu, dsilu = _silu_and_grad(g)
        dzg_ref[...] = (dyv * h * dsilu).astype(MXU_DTYPE)
        a_next = jnp.where(rows < tb - 1, pltpu.roll(a, tb - 1, 0), acar_ref[0:1, :])
        acum, g0 = _scan_tile(a_next, dyv * silu, True)
        gt = g0 + acum * gcar_ref[0:1, :]
        h_prev = jnp.where(rows >= 1, pltpu.roll(h, 1, 0), jnp.where(has_earlier, hh_ref[7:8, :], 0.0))
        dxc, dwgx, dbgx, dwga, dbga, dlam = vjp((gt * h_prev, gt))
        later = xcar_ref[...]
        gcar_ref[...] = gt[0:8, :]
        acar_ref[...] = a[0:8, :]
        xcar_ref[...] = dxc[0:8, :]
        dx = dxc * cw_ref[3:4, :]
        pad = jnp.zeros((tb - 8, LANES), F32)
        for d in (1, 2, 3):
            tail = jnp.concatenate([pad, pltpu.roll(later, 8 - d, 0)], axis=0)
            up = jnp.where(rows < tb - d, pltpu.roll(dxc, tb - d, 0), tail)
            dx = dx + up * cw_ref[3 - d:4 - d, :]
        dzx_ref[...] = dx.astype(MXU_DTYPE)
        for k in range(4):
            dcw_ref[k:k + 1, :] += jnp.sum(dxc * xs[k], axis=0, keepdims=True)
        dcb_ref[...] += jnp.sum(dxc, axis=0, keepdims=True)
        dwgx_ref[...] += dwgx
        dbgx_ref[...] += dbgx
        dwga_ref[...] += dwga
        dbga_ref[...] += dbga
        dlam_ref[...] += dlam

    sds = jax.ShapeDtypeStruct
    grid = (LRU_BLOCKS, nt)
    body, x_in, x_in_specs, x_out_specs, x_out_shapes, sems = _host(body, 13, 9, 3, comm, grid)
    return pl.pallas_call(
        body,
        name="lru_bwd" if comm is None else "lru_bwd_comm",
        grid=grid,
        in_specs=[blk(0), halo(0), blk(LRU_BLOCKS), blk(0), halo(0), blk(0), chan(4), chan(1), wblk, bblk, wblk, bblk, chan(1)]
        + x_in_specs,
        out_specs=[blk(0), blk(0), wblk, bblk, wblk, bblk, chan(1), chan(4), chan(1)] + x_out_specs,
        out_shape=[sds((t, 1024), MXU_DTYPE), sds((t, 1024), MXU_DTYPE), sds(wgx.shape, F32), sds(bgx.shape, F32), sds(wga.shape, F32),
                   sds(bga.shape, F32), sds((1, 1024), F32), sds((4, 1024), F32), sds((1, 1024), F32)] + x_out_shapes,
        scratch_shapes=[pltpu.VMEM((8, LANES), F32)] * 3 + sems,
        compiler_params=_cparams(("arbitrary", "arbitrary") if comm is not None else ("parallel", "arbitrary")),
    )(z, z, z, h, h, dy, conv_w, conv_b, wgx, bgx, wga, bga, lam, *x_in)


def _mla_prep_tile(ops, cq, ckv, krp, c, s1, s2, g_cq, g_ckv, wq, wk, wv, gq, gk):
    cqn = rms(cq, g_cq, 256)
    ckvn = rms(ckv, g_ckv, 128)
    v = ops.mm(ckvn, wv)
    q_heads = ops.split_heads(ops.mm(cqn, wq))
    k_heads = ops.split_heads(ops.mm(ckvn, wk))
    qs, ks = [], []
    for hd in range(MLA_HEADS):
        q = rms(q_heads[hd], gq, MLA_QK)
        k = rms(k_heads[hd] + krp, gk, MLA_QK)
        qs.append(rope(ops, q, c, s1, s2, 16) * (MLA_QK ** -0.5))
        ks.append(rope(ops, k, c, s1, s2, 16))
    return tuple(qs), tuple(ks), v


def _mla_prep_args(ins):
    return tuple(ref[...] for ref in ins[:13])


def _mla_prep_ins(z, tabs, w):
    return [_row(z, 256, 0), _row(z, 128, 2), _row(z, 128, 3), _row(tabs, 128, 0), _row(tabs, 128, 1), _row(tabs, 128, 2),
            _const(w["cq_norm_g"]), _const(w["ckv_norm_g"]), _const(w["wq"]), _const(w["wk"]), _const(w["wv"]),
            _const(w["gq"]), _const(w["gk"])]


def mla_prep_fwd(z, tabs, w):
    def fn(ins, outs, _):
        qs, ks, v = _mla_prep_tile(PLAIN, *_mla_prep_args(ins))
        for hd in range(MLA_HEADS):
            outs[0][:, LANES * hd:LANES * (hd + 1)] = qs[hd].astype(MXU_DTYPE)
            outs[1][:, LANES * hd:LANES * (hd + 1)] = ks[hd].astype(MXU_DTYPE)
        outs[2][...] = v.astype(MXU_DTYPE)
        outs[3][...] = v.T.astype(MXU_DTYPE)

    return rowwise("mla_prep_fwd", fn, z.shape[0], tm=512, ins=_mla_prep_ins(z, tabs, w),
                   outs=[(1024, 1024, MXU_DTYPE, False), (1024, 1024, MXU_DTYPE, False), (512, 512, MXU_DTYPE, False)],
                   touts=[(512, 512, MXU_DTYPE, False)])


def mla_prep_bwd(z, tabs, w, dq, dk, dv):
    def fn(ins, outs, accs):
        args = _mla_prep_args(ins)
        _, vjp = jax.vjp(functools.partial(_mla_prep_tile, DIFF), *args)
        heads = lambda ref: tuple(ref[:, LANES * hd:LANES * (hd + 1)] for hd in range(MLA_HEADS))
        dcq, dckv, dkr, _, _, _, dg_cq, dg_ckv, dwq, dwk, dwv, dgq, dgk = vjp((heads(ins[13]), heads(ins[14]), ins[15][...]))
        lane = lax.broadcasted_iota(jnp.int32, (1, LANES), 1)
        outs[0][:, 0:256] = dcq.astype(MXU_DTYPE)
        outs[0][:, 256:384] = dckv.astype(MXU_DTYPE)
        outs[0][:, 384:512] = jnp.where((lane >= MLA_NOPE) & (lane < MLA_QK), dkr, 0.0).astype(MXU_DTYPE)
        accs[0][...] += dg_cq
        accs[1][...] += dg_ckv
        accs[2][...] += dwq
        accs[3][...] += dwk
        accs[4][...] += dwv
        accs[5][...] += dgq
        accs[6][...] += dgk

    return rowwise("mla_prep_bwd", fn, z.shape[0], tm=512, ins=_mla_prep_ins(z, tabs, w) + [_row(dq), _row(dk), _row(dv)],
                   outs=[(512, 512, MXU_DTYPE, False)],
                   accs=[(1, 256, 256, False), (1, 128, 128, False), (256, 1024, 1024, False), (128, 1024, 1024, False),
                         (128, 512, 512, False), (1, 128, 128, False), (1, 128, 128, False)])


def _head_masks():
    lane = lax.broadcasted_iota(jnp.int32, (1, LANES), 1)
    return (lane < DIL_HD, lane >= DIL_HD)


def _row_scalar(tile, mask):
    return jnp.max(jnp.where(mask, tile, -jnp.inf), axis=-1, keepdims=True)


def _causal_tiles(nq, by_key):
    pairs = [(i, j) for i in range(nq) for j in range(i + 1)]
    if by_key:
        pairs.sort(key=lambda ij: (ij[1], ij[0]))
    return (jnp.asarray([ij[0] for ij in pairs], jnp.int32), jnp.asarray([ij[1] for ij in pairs], jnp.int32))


def mla_attn_fwd(q, k, vt, z, *, tq=256, gather=None):
    t = q.shape[0]
    nq = t // tq
    it, jt = _causal_tiles(nq, False)
    hps, wq, wv = MLA_HPS, LANES * MLA_HPS, 64 * MLA_HPS
    ws, layer = gather if gather is not None else ([], 0)
    ng = len(ws)
    ngrp, nsteps = MLA_HEADS // hps, int(it.shape[0])

    def body(it_ref, jt_ref, q_ref, k_ref, vt_ref, g_ref, *rest):
        w_refs, rest = rest[:ng], rest[ng:]
        o_ref, y_ref, yt_ref, lse_ref = rest[:4]
        ga_refs, rest = rest[4:4 + ng], rest[4 + ng:]
        m_scr, l_scr, acc_scr = rest[:3]
        step = pl.program_id(1)
        i, j = it_ref[step], jt_ref[step]
        if ng:
            gather_start, gather_finish = _layer_gather(w_refs, ga_refs, rest[3], rest[4], layer)

            @pl.when((pl.program_id(0) == 0) & (step == 0))
            def _():
                gather_start()

        @pl.when(j == 0)
        def _():
            m_scr[...] = jnp.full(m_scr.shape, NEG, F32)
            l_scr[...] = jnp.zeros(l_scr.shape, F32)
            acc_scr[...] = jnp.zeros(acc_scr.shape, F32)

        def update(diagonal):
            heads = range(hps)
            lanes = [slice(LANES * hh, LANES * (hh + 1)) for hh in heads]
            rows = [slice(64 * hh, 64 * (hh + 1)) for hh in heads]
            sts = [_dot(k_ref[:, lanes[hh]], q_ref[:, lanes[hh]], NT) for hh in heads]
            m_prev = [m_scr[hh:hh + 1, :] for hh in heads]
            l_prev = [l_scr[hh:hh + 1, :] for hh in heads]
            acc_prev = [acc_scr[rows[hh], :] for hh in heads]
            m_new, l_new, acc_new = [], [], []
            for hh in heads:
                st = sts[hh]
                if diagonal:
                    key = lax.broadcasted_iota(jnp.int32, (tq, tq), 0)
                    qry = lax.broadcasted_iota(jnp.int32, (tq, tq), 1)
                    st = jnp.where(key <= qry, st, NEG)
                m_new.append(jnp.maximum(m_prev[hh], jnp.max(st, axis=0, keepdims=True)))
                alpha = jnp.exp(m_prev[hh] - m_new[hh])
                pt = jnp.exp(st - m_new[hh])
                l_new.append(alpha * l_prev[hh] + jnp.sum(pt, axis=0, keepdims=True))
                acc_new.append(alpha * acc_prev[hh] + _dot(vt_ref[rows[hh], :], pt, NN))
            for hh in heads:
                m_scr[hh:hh + 1, :] = m_new[hh]
                l_scr[hh:hh + 1, :] = l_new[hh]
                acc_scr[rows[hh], :] = acc_new[hh]

        @pl.when(j < i)
        def _():
            update(False)

        @pl.when(j == i)
        def _():
            update(True)
            lse_ref[...] = jnp.zeros(lse_ref.shape, F32)
            for hh in range(hps):
                rows = slice(64 * hh, 64 * (hh + 1))
                acc_scr[rows, :] = acc_scr[rows, :] / l_scr[hh:hh + 1, :]
                lse_ref[hh:hh + 1, :] = m_scr[hh:hh + 1, :] + jnp.log(l_scr[hh:hh + 1, :])
            o = acc_scr[...].T
            o_ref[...] = o
            y = o * _silu_and_grad(g_ref[...])[0]
            y_ref[...] = y.astype(MXU_DTYPE)
            yt_ref[...] = y.T.astype(MXU_DTYPE)

        if ng:
            @pl.when((pl.program_id(0) == ngrp - 1) & (step == nsteps - 1))
            def _():
                gather_finish()

    qo = lambda w, off=0: pl.BlockSpec((tq, w), lambda p, s, it_, jt_: (it_[s], p + off))
    sds = jax.ShapeDtypeStruct
    comm_scratch = [pltpu.SemaphoreType.DMA((6 * ng,)), pltpu.SemaphoreType.DMA((6 * ng,))] if ng else []
    return pl.pallas_call(
        body,
        name="mla_attn_fwd_gather" if ng else "mla_attn_fwd",
        grid_spec=pltpu.PrefetchScalarGridSpec(
            num_scalar_prefetch=2,
            grid=(ngrp, nsteps),
            in_specs=[qo(wq), pl.BlockSpec((tq, wq), lambda p, s, it_, jt_: (jt_[s], p)),
                      pl.BlockSpec((wv, tq), lambda p, s, it_, jt_: (p, jt_[s])), qo(wv, 512 // wv)] + [HBM_SPEC] * ng,
            out_specs=[qo(wv), qo(wv), pl.BlockSpec((wv, tq), lambda p, s, it_, jt_: (p, it_[s])),
                       pl.BlockSpec((None, 8, tq), lambda p, s, it_, jt_: (p, 0, it_[s]))] + [HBM_SPEC] * ng,
            scratch_shapes=[pltpu.VMEM((8, tq), F32), pltpu.VMEM((8, tq), F32), pltpu.VMEM((wv, tq), F32)] + comm_scratch,
        ),
        out_shape=[sds((t, 512), F32), sds((t, 512), MXU_DTYPE), sds((512, t), MXU_DTYPE), sds((ngrp, 8, t), F32)]
        + _gather_shapes(ws),
        compiler_params=_cparams(("arbitrary", "arbitrary") if ng else ("parallel", "arbitrary")),
    )(it, jt, q, k, vt, z, *ws)


def mla_attn_bwd(q, k, v, do, lse, dd, *, tq=256, chips=None):
    t = q.shape[0]
    nq = t // tq
    it, jt = _causal_tiles(nq, True)
    hps, wq, wv = MLA_HPS, LANES * MLA_HPS, 64 * MLA_HPS
    pcs, layer = chips if chips is not None else ([], 0)
    nc = len(pcs)
    ngrp, nsteps = MLA_HEADS // hps, int(it.shape[0])

    def body(it_ref, jt_ref, q_ref, k_ref, v_ref, do_ref, lse_ref, d_ref, *rest):
        pc_refs, rest = rest[:nc], rest[nc:]
        dq_ref, dk_ref, dv_ref = rest[:3]
        lb_refs, rest = rest[3:3 + nc], rest[3 + nc:]
        dk_scr, dv_scr = rest[:2]
        step = pl.program_id(1)
        i, j = it_ref[step], jt_ref[step]
        masks = _head_masks()
        if nc:
            chips_start, chips_finish = _chips_exchange(pc_refs, lb_refs, rest[2], rest[3], layer)

            @pl.when((pl.program_id(0) == 0) & (step == 0))
            def _():
                chips_start()

        @pl.when(step == 0)
        def _():
            dq_ref[...] = jnp.zeros(dq_ref.shape, F32)

        @pl.when(i == j)
        def _():
            dk_scr[...] = jnp.zeros(dk_scr.shape, F32)
            dv_scr[...] = jnp.zeros(dv_scr.shape, F32)

        def update(diagonal):
            qrows = pl.ds(pl.multiple_of(i * tq, tq), tq)
            heads = range(hps)
            lanes = [slice(LANES * hh, LANES * (hh + 1)) for hh in heads]
            pair = [slice(LANES * (hh // 2), LANES * (hh // 2 + 1)) for hh in heads]
            qh = [q_ref[:, lanes[hh]] for hh in heads]
            kh = [k_ref[:, lanes[hh]] for hh in heads]
            doh = []
            for hh in heads:
                dov = do_ref[:, pair[hh]]
                doh.append(jnp.where(masks[hh % 2], dov, jnp.zeros_like(dov)))
            sts = [_dot(kh[hh], qh[hh], NT) for hh in heads]
            dps = [_dot(v_ref[:, pair[hh]], doh[hh], NT) for hh in heads]
            lse = [lse_ref[hh:hh + 1, :] for hh in heads]
            ddv = [d_ref[hh:hh + 1, :] for hh in heads]
            dk_new = [dk_scr[:, lanes[hh]] for hh in heads]
            dq_new = [dq_ref[qrows, lanes[hh]] for hh in heads]
            dv_new = [dv_scr[:, pair[2 * pp]] for pp in range(hps // 2)]
            for hh in heads:
                st = sts[hh] - lse[hh]
                if diagonal:
                    key = lax.broadcasted_iota(jnp.int32, (tq, tq), 0)
                    qry = lax.broadcasted_iota(jnp.int32, (tq, tq), 1)
                    st = jnp.where(key <= qry, st, NEG)
                pt = jnp.exp(st)
                dst = pt * (dps[hh] - ddv[hh])
                dv_new[hh // 2] = dv_new[hh // 2] + _dot(pt, doh[hh], NN)
                dk_new[hh] = dk_new[hh] + _dot(dst, qh[hh], NN)
                dq_new[hh] = dq_new[hh] + _dot(dst, kh[hh], TN)
            for hh in heads:
                dk_scr[:, lanes[hh]] = dk_new[hh]
                dq_ref[qrows, lanes[hh]] = dq_new[hh]
            for pp in range(hps // 2):
                dv_scr[:, pair[2 * pp]] = dv_new[pp]

        @pl.when(j < i)
        def _():
            update(False)

        @pl.when(j == i)
        def _():
            update(True)

        @pl.when(i == nq - 1)
        def _():
            dk_ref[...] = dk_scr[...]
            dv_ref[...] = dv_scr[...]

        if nc:
            @pl.when((pl.program_id(0) == ngrp - 1) & (step == nsteps - 1))
            def _():
                chips_finish()

    qo = lambda w: pl.BlockSpec((tq, w), lambda p, s, it_, jt_: (it_[s], p))
    kv = lambda w: pl.BlockSpec((tq, w), lambda p, s, it_, jt_: (jt_[s], p))
    stat = pl.BlockSpec((None, 8, tq), lambda p, s, it_, jt_: (p, 0, it_[s]))
    sds = jax.ShapeDtypeStruct
    comm_scratch = [pltpu.SemaphoreType.DMA((3 * nc,)), pltpu.SemaphoreType.DMA((3 * nc,))] if nc else []
    return pl.pallas_call(
        body,
        name="mla_attn_bwd_chips" if nc else "mla_attn_bwd",
        grid_spec=pltpu.PrefetchScalarGridSpec(
            num_scalar_prefetch=2,
            grid=(ngrp, nsteps),
            in_specs=[qo(wq), kv(wq), kv(wv), qo(wv), stat, stat] + [HBM_SPEC] * nc,
            out_specs=[pl.BlockSpec((t, wq), lambda p, s, it_, jt_: (0, p)), kv(wq), kv(wv)] + [HBM_SPEC] * nc,
            scratch_shapes=[pltpu.VMEM((tq, wq), F32), pltpu.VMEM((tq, wv), F32)] + comm_scratch,
        ),
        out_shape=[sds((t, 1024), F32), sds((t, 1024), F32), sds((t, 512), F32)] + _chips_shapes(pcs),
        compiler_params=_cparams(("arbitrary", "arbitrary") if nc else ("parallel", "arbitrary")),
    )(it, jt, q, k, v, do, lse, dd, *pcs)


def mla_gate_bwd(dy, o, z, *, tm=512):
    t = dy.shape[0]
    wv = 64 * MLA_HPS

    def body(dy_ref, o_ref, g_ref, do_ref, dzg_ref, dd_ref):
        dyv, ov = dy_ref[...], o_ref[...]
        silu, dsilu = _silu_and_grad(g_ref[...])
        do = dyv * silu
        do_ref[...] = do.astype(MXU_DTYPE)
        dzg_ref[...] = (dyv * ov * dsilu).astype(MXU_DTYPE)
        prod = do * ov
        row = lax.broadcasted_iota(jnp.int32, (8, wv), 0)
        lane = lax.broadcasted_iota(jnp.int32, (8, wv), 1)
        pick = ((lane >= row * DIL_HD) & (lane < (row + 1) * DIL_HD)).astype(BF16)
        hi = prod.astype(BF16)
        r1 = prod - hi.astype(F32)
        mid = r1.astype(BF16)
        lo = (r1 - mid.astype(F32)).astype(BF16)
        dot = lambda u: lax.dot_general(pick, u, NT, preferred_element_type=F32)
        dd_ref[...] = dot(hi) + dot(mid) + dot(lo)

    blk = lambda off=0: pl.BlockSpec((tm, wv), lambda p, i: (i, p + off))
    sds = jax.ShapeDtypeStruct
    return pl.pallas_call(
        body,
        name="mla_gate_bwd",
        grid=(MLA_HEADS // MLA_HPS, t // tm),
        in_specs=[blk(), blk(), blk(512 // wv)],
        out_specs=[blk(), blk(), pl.BlockSpec((None, 8, tm), lambda p, i: (p, 0, i))],
        out_shape=[sds((t, 512), MXU_DTYPE), sds((t, 512), MXU_DTYPE), sds((MLA_HEADS // MLA_HPS, 8, t), F32)],
        compiler_params=_cparams(("parallel", "parallel")),
    )(dy, o, z)


SPAN = 2048
DIL_SCALE = DIL_HD ** -0.5


def _rm_src(u, window, dil):
    nn, r = divmod(u, dil)
    return pl.ds(nn * window + r, DIL_NK, stride=dil) if dil > 1 else pl.ds(u * DIL_NK, DIL_NK)


def _rm_dst(u):
    return pl.ds(u * DIL_NK, DIL_NK)


def _dil_prep_tile(ops, x, g2, c, s1, s2, scale):
    ms = ops.seg_sum(x * x) * (1.0 / DIL_HD)
    return rope(ops, x * lax.rsqrt(ms + EPS) * g2, c, s1, s2, 32) * scale


def _span_blk(base):
    return pl.BlockSpec((SPAN, LANES), lambda s, p: (s, base + p))


def _const_blk(shape):
    return pl.BlockSpec(shape, lambda s, p: (0,) * len(shape))


_DIL_TABLE_SPECS = [pl.BlockSpec((SPAN, LANES), (lambda s, p, blk=blk: (s, blk))) for blk in (3, 4, 5)]


def dil_prep_fwd(z, tabs, gq2, gk2, gi):
    window, dil = DIL_GROUPS[gi]
    t = z.shape[0]

    def body(q_ref, k_ref, v_ref, c_ref, s1_ref, s2_ref, gq_ref, gk_ref, qo_ref, ko_ref, vo_ref):
        for u in range(SPAN // DIL_NK):
            src, dst = _rm_src(u, window, dil), _rm_dst(u)
            c, s1, s2 = c_ref[src, :], s1_ref[src, :], s2_ref[src, :]
            qo_ref[dst, :] = _dil_prep_tile(PLAIN, q_ref[src, :], gq_ref[...], c, s1, s2, DIL_SCALE).astype(MXU_DTYPE)
            ko_ref[dst, :] = _dil_prep_tile(PLAIN, k_ref[src, :], gk_ref[...], c, s1, s2, 1.0).astype(MXU_DTYPE)
            vo_ref[dst, :] = v_ref[src, :].astype(MXU_DTYPE)

    return pl.pallas_call(
        body,
        name=f"dil_prep_fwd_{dil}",
        grid=(t // SPAN, 4),
        in_specs=[_span_blk(4 * gi), _span_blk(12 + 4 * gi), _span_blk(24 + 4 * gi)] + _DIL_TABLE_SPECS
        + [_const_blk((1, LANES)), _const_blk((1, LANES))],
        out_specs=[_span_blk(0)] * 3,
        out_shape=[jax.ShapeDtypeStruct((t, 512), MXU_DTYPE)] * 3,
        compiler_params=_cparams(("parallel", "parallel")),
    )(z, z, z, tabs, tabs, tabs, gq2, gk2)


def dil_prep_bwd(z, tabs, gq2, gk2, dq, dk, dv, gi):
    window, dil = DIL_GROUPS[gi]
    t = z.shape[0]

    def body(q_ref, k_ref, c_ref, s1_ref, s2_ref, gq_ref, gk_ref, dq_ref, dk_ref, dv_ref, dzq_ref, dzk_ref, dzv_ref, dgq_ref,
             dgk_ref, sq, sk, sv):
        @pl.when((pl.program_id(0) == 0) & (pl.program_id(1) == 0))
        def _():
            dgq_ref[...] = jnp.zeros((1, LANES), F32)
            dgk_ref[...] = jnp.zeros((1, LANES), F32)

        dgs = [jnp.zeros((1, LANES), F32), jnp.zeros((1, LANES), F32)]
        for u in range(SPAN // DIL_NK):
            src, dst = _rm_src(u, window, dil), _rm_dst(u)
            c, s1, s2 = c_ref[src, :], s1_ref[src, :], s2_ref[src, :]
            for idx, (x_ref, g_ref, ct_ref, scr, scale) in enumerate(((q_ref, gq_ref, dq_ref, sq, DIL_SCALE),
                                                                      (k_ref, gk_ref, dk_ref, sk, 1.0))):
                _, vjp = jax.vjp(lambda xv, gv, sc=scale: _dil_prep_tile(DIFF, xv, gv, c, s1, s2, sc), x_ref[src, :], g_ref[...])
                dx, dg = vjp(ct_ref[dst, :])
                scr[src, :] = dx
                dgs[idx] = dgs[idx] + dg
            sv[src, :] = dv_ref[dst, :]
        dgq_ref[...] += dgs[0] + pltpu.roll(dgs[0], DIL_HD, 1)
        dgk_ref[...] += dgs[1] + pltpu.roll(dgs[1], DIL_HD, 1)
        for c0 in range(0, SPAN, 256):
            rows = slice(c0, c0 + 256)
            dzq_ref[rows, :] = sq[rows, :].astype(MXU_DTYPE)
            dzk_ref[rows, :] = sk[rows, :].astype(MXU_DTYPE)
            dzv_ref[rows, :] = sv[rows, :].astype(MXU_DTYPE)

    sds = jax.ShapeDtypeStruct
    return pl.pallas_call(
        body,
        name=f"dil_prep_bwd_{dil}",
        grid=(t // SPAN, 4),
        in_specs=[_span_blk(4 * gi), _span_blk(12 + 4 * gi)] + _DIL_TABLE_SPECS
        + [_const_blk((1, LANES)), _const_blk((1, LANES)), _span_blk(0), _span_blk(0), _span_blk(0)],
        out_specs=[_span_blk(0)] * 3 + [_const_blk((1, LANES))] * 2,
        out_shape=[sds((t, 512), MXU_DTYPE)] * 3 + [sds((1, LANES), F32)] * 2,
        scratch_shapes=[pltpu.VMEM((SPAN, LANES), F32)] * 3,
        compiler_params=_cparams(("arbitrary", "arbitrary")),
    )(z, z, tabs, tabs, tabs, gq2, gk2, dq, dk, dv)


def _band_masks():
    qi = lax.broadcasted_iota(jnp.int32, (DIL_NK, DIL_NK), 0)
    ki = lax.broadcasted_iota(jnp.int32, (DIL_NK, DIL_NK), 1)
    return (ki >= qi), (ki <= qi)


def _pair_heads():
    return [(pair, hh) for pair in range(4) for hh in range(2)]


def _pair_lanes(pair):
    return slice(LANES * pair, LANES * (pair + 1))


def _zero_other_head(mask, x):
    return jnp.where(mask, x, jnp.zeros_like(x))


def dil_attn_fwd(name, q, k, v, dil):
    t = q.shape[0]

    def body(q_ref, kp_ref, kc_ref, vp_ref, vc_ref, o_ref, lse_ref):
        b = pl.program_id(0)
        mprev, mcur = _band_masks()
        mprev = mprev & (b >= dil)
        hm = _head_masks()
        heads = _pair_heads()
        qh = [_zero_other_head(hm[hh], q_ref[:, _pair_lanes(pair)]) for pair, hh in heads]
        sps = [_dot(qh[i], kp_ref[:, _pair_lanes(pair)], NT) for i, (pair, _) in enumerate(heads)]
        scs = [_dot(qh[i], kc_ref[:, _pair_lanes(pair)], NT) for i, (pair, _) in enumerate(heads)]
        o = [jnp.zeros((DIL_NK, LANES), F32) for _ in range(4)]
        lse = [jnp.zeros((DIL_NK, LANES), F32) for _ in range(4)]
        for i, (pair, hh) in enumerate(heads):
            sp, sc = jnp.where(mprev, sps[i], NEG), jnp.where(mcur, scs[i], NEG)
            m = jnp.maximum(jnp.max(sp, axis=1, keepdims=True), jnp.max(sc, axis=1, keepdims=True))
            ep, ec = jnp.exp(sp - m), jnp.exp(sc - m)
            den = jnp.sum(ep, axis=1, keepdims=True) + jnp.sum(ec, axis=1, keepdims=True)
            lanes = _pair_lanes(pair)
            oh = _dot(ep, _zero_other_head(hm[hh], vp_ref[:, lanes]), NN) + _dot(ec, _zero_other_head(hm[hh], vc_ref[:, lanes]), NN)
            o[pair] = o[pair] + oh / den
            lse[pair] = jnp.where(hm[hh], m + jnp.log(den), lse[pair])
        for pair in range(4):
            o_ref[:, _pair_lanes(pair)] = o[pair]
            lse_ref[:, _pair_lanes(pair)] = lse[pair]

    cur = pl.BlockSpec((DIL_NK, 512), lambda b: (b, 0))
    prev = pl.BlockSpec((DIL_NK, 512), lambda b: (jnp.maximum(b - dil, 0), 0))
    sds = jax.ShapeDtypeStruct
    return pl.pallas_call(
        body,
        name=name,
        grid=(t // DIL_NK,),
        in_specs=[cur, prev, cur, prev, cur],
        out_specs=[cur, cur],
        out_shape=[sds((t, 512), F32), sds((t, 512), F32)],
        compiler_params=_cparams(("parallel",)),
    )(q, k, k, v, v)


def dil_attn_bwd(name, q, k, v, do, lse, dd, dil):
    t = q.shape[0]
    nblk = t // DIL_NK

    def body(q_ref, do_ref, l_ref, d_ref, kp_ref, kc_ref, vp_ref, vc_ref, dq_ref, dk_ref, dv_ref):
        b = pl.program_id(0)

        @pl.when(b == 0)
        def _():
            dk_ref[...] = jnp.zeros(dk_ref.shape, F32)
            dv_ref[...] = jnp.zeros(dv_ref.shape, F32)

        mprev, mcur = _band_masks()
        mprev = mprev & (b >= dil)
        rows_c = pl.ds(pl.multiple_of(b * DIL_NK, DIL_NK), DIL_NK)
        rows_p = pl.ds(pl.multiple_of(jnp.maximum(b - dil, 0) * DIL_NK, DIL_NK), DIL_NK)
        hm = _head_masks()
        heads = _pair_heads()
        ln_ = [_pair_lanes(pair) for pair, _ in heads]
        qh = [_zero_other_head(hm[hh], q_ref[:, ln_[i]]) for i, (_, hh) in enumerate(heads)]
        doh = [_zero_other_head(hm[hh], do_ref[:, ln_[i]]) for i, (_, hh) in enumerate(heads)]
        idx = range(len(heads))
        s_p = [_dot(qh[i], kp_ref[:, ln_[i]], NT) for i in idx]
        s_c = [_dot(qh[i], kc_ref[:, ln_[i]], NT) for i in idx]
        dp_p = [_dot(doh[i], vp_ref[:, ln_[i]], NT) for i in idx]
        dp_c = [_dot(doh[i], vc_ref[:, ln_[i]], NT) for i in idx]
        zero = lambda: [jnp.zeros((DIL_NK, LANES), F32) for _ in range(4)]
        dq, dk_p, dk_c, dv_p, dv_c = zero(), zero(), zero(), zero(), zero()
        for i, (pair, hh) in enumerate(heads):
            lse_h, d_h = _row_scalar(l_ref[:, ln_[i]], hm[hh]), _row_scalar(d_ref[:, ln_[i]], hm[hh])
            pp = jnp.exp(jnp.where(mprev, s_p[i] - lse_h, NEG))
            pc = jnp.exp(jnp.where(mcur, s_c[i] - lse_h, NEG))
            dsp, dsc = pp * (dp_p[i] - d_h), pc * (dp_c[i] - d_h)
            dq[pair] = (dq[pair] + _dot(dsp, _zero_other_head(hm[hh], kp_ref[:, ln_[i]]), NN)
                        + _dot(dsc, _zero_other_head(hm[hh], kc_ref[:, ln_[i]]), NN))
            dv_p[pair] = dv_p[pair] + _dot(pp, doh[i], TN)
            dv_c[pair] = dv_c[pair] + _dot(pc, doh[i], TN)
            dk_p[pair] = dk_p[pair] + _dot(dsp, qh[i], TN)
            dk_c[pair] = dk_c[pair] + _dot(dsc, qh[i], TN)
        for pair in range(4):
            lanes = _pair_lanes(pair)
            dq_ref[:, lanes] = dq[pair]
            dk_ref[rows_p, lanes] += dk_p[pair]
            dv_ref[rows_p, lanes] += dv_p[pair]
            dk_ref[rows_c, lanes] += dk_c[pair]
            dv_ref[rows_c, lanes] += dv_c[pair]

    cur = pl.BlockSpec((DIL_NK, 512), lambda b: (b, 0))
    prev = pl.BlockSpec((DIL_NK, 512), lambda b: (jnp.maximum(b - dil, 0), 0))
    whole = pl.BlockSpec((t, 512), lambda b: (0, 0))
    sds = jax.ShapeDtypeStruct
    return pl.pallas_call(
        body,
        name=name,
        grid=(nblk,),
        in_specs=[cur, cur, cur, cur, prev, cur, prev, cur],
        out_specs=[cur, whole, whole],
        out_shape=[sds((t, 512), F32)] * 3,
        compiler_params=_cparams(("arbitrary",)),
    )(q, do, lse, dd, k, k, v, v)


def dil_combine(os_, lses, z):
    t = z.shape[0]

    def body(o1_ref, l1_ref, o2_ref, l2_ref, o3_ref, l3_ref, g_ref, y_ref, o_ref, lse_ref, yt_ref, so2, sl2, so3, sl3):
        for (window, dil), o_in, l_in, so, sl in ((DIL_GROUPS[1], o2_ref, l2_ref, so2, sl2), (DIL_GROUPS[2], o3_ref, l3_ref, so3, sl3)):
            for u in range(SPAN // DIL_NK):
                src, dst = _rm_src(u, window, dil), _rm_dst(u)
                so[src, :] = o_in[dst, :]
                sl[src, :] = l_in[dst, :]
        for c0 in range(0, SPAN, 256):
            rows = slice(c0, c0 + 256)
            la, lb, lc = l1_ref[rows, :], sl2[rows, :], sl3[rows, :]
            mx = jnp.maximum(jnp.maximum(la, lb), lc)
            wa, wb, wc = jnp.exp(la - mx), jnp.exp(lb - mx), jnp.exp(lc - mx)
            tot = wa + wb + wc
            o = (wa * o1_ref[rows, :] + wb * so2[rows, :] + wc * so3[rows, :]) / tot
            y = o * _silu_and_grad(g_ref[rows, :])[0]
            y_ref[rows, :] = y.astype(MXU_DTYPE)
            o_ref[rows, :] = o
            lse_ref[rows, :] = mx + jnp.log(tot)
            yt_ref[:, rows] = y.T.astype(MXU_DTYPE)

    sds = jax.ShapeDtypeStruct
    return pl.pallas_call(
        body,
        name="dil_combine",
        grid=(t // SPAN, 4),
        in_specs=[_span_blk(0)] * 6 + [_span_blk(36)],
        out_specs=[_span_blk(0)] * 3 + [pl.BlockSpec((LANES, SPAN), lambda s, p: (p, s))],
        out_shape=[sds((t, 512), MXU_DTYPE), sds((t, 512), F32), sds((t, 512), F32), sds((512, t), MXU_DTYPE)],
        scratch_shapes=[pltpu.VMEM((SPAN, LANES), F32)] * 4,
        compiler_params=_cparams(("parallel", "parallel")),
    )(os_[0], lses[0], os_[1], lses[1], os_[2], lses[2], z)


def dil_gate_bwd(dy, o, z, lse):
    t = dy.shape[0]

    def body(dy_ref, o_ref, g_ref, lse_ref, dzg_ref, do1_ref, dd1_ref, do2_ref, dd2_ref, l2_ref, do3_ref, dd3_ref, l3_ref, do_scr):
        for c0 in range(0, SPAN, 256):
            rows = slice(c0, c0 + 256)
            dyv, ov = dy_ref[rows, :], o_ref[rows, :]
            silu, dsilu = _silu_and_grad(g_ref[rows, :])
            do = dyv * silu
            do_scr[rows, :] = do
            do1_ref[rows, :] = do.astype(MXU_DTYPE)
            dd1_ref[rows, :] = _seg_sum_impl(do * ov)
            dzg_ref[rows, :] = (dyv * ov * dsilu).astype(MXU_DTYPE)
        for (window, dil), do_o, dd_o, l_o in ((DIL_GROUPS[1], do2_ref, dd2_ref, l2_ref), (DIL_GROUPS[2], do3_ref, dd3_ref, l3_ref)):
            for u in range(SPAN // DIL_NK):
                src, dst = _rm_src(u, window, dil), _rm_dst(u)
                do_o[dst, :] = do_scr[src, :].astype(MXU_DTYPE)
                dd_o[dst, :] = dd1_ref[src, :]
                l_o[dst, :] = lse_ref[src, :]

    sds = jax.ShapeDtypeStruct
    f32, mxu = sds((t, 512), F32), sds((t, 512), MXU_DTYPE)
    return pl.pallas_call(
        body,
        name="dil_gate_bwd",
        grid=(t // SPAN, 4),
        in_specs=[_span_blk(0), _span_blk(0), _span_blk(36), _span_blk(0)],
        out_specs=[_span_blk(0)] * 9,
        out_shape=[mxu, mxu, f32, mxu, f32, f32, mxu, f32, f32],
        scratch_shapes=[pltpu.VMEM((SPAN, LANES), F32)],
        compiler_params=_cparams(("parallel", "parallel")),
    )(dy, o, z, lse)


def _merge_tile(p0, p1, p2, z0, z1, z2, b0, b1, b2):
    return _sigmoid(z0 + b0) * p0 + _sigmoid(z1 + b1) * p1 + _sigmoid(z2 + b2) * p2


def _merge_ins(ps, z, b):
    w = 256
    return ([_row(p, w, 0, True) for p in ps] + [_row(z, w, 4 * i, True) for i in range(3)]
            + [_const(b, w, 4 * i, True) for i in range(3)])


def merge_fwd(ps, z, b):
    def fn(ins, outs, _):
        merged = _merge_tile(*[r[...] for r in ins])
        outs[0][...] = merged.astype(MXU_DTYPE)
        outs[1][...] = merged.T.astype(MXU_DTYPE)

    return rowwise("merge_fwd", fn, z.shape[0], tm=1024, ncol=4, ins=_merge_ins(ps, z, b), outs=[(1024, 256, MXU_DTYPE, True)],
                   touts=[(1024, 256, MXU_DTYPE, True)])


def merge_bwd(dm, ps, z, b):
    def fn(ins, outs, accs):
        _, vjp = jax.vjp(_merge_tile, *[r[...] for r in ins[:9]])
        grads = vjp(ins[9][...])
        for i in range(3):
            outs[i][...] = grads[i].astype(MXU_DTYPE)
            outs[3 + i][...] = grads[3 + i].astype(MXU_DTYPE)
            accs[i][...] += grads[6 + i]

    return rowwise("merge_bwd", fn, z.shape[0], tm=1024, ncol=4, ins=_merge_ins(ps, z, b) + [_row(dm, 256, 0, True)],
                   outs=[(1024, 256, MXU_DTYPE, True)] * 6, accs=[(1, 1024, 256, True)] * 3)


EW_BLOCK_BYTES = 2**21


def _tile2d(r, c):
    if r * c * 4 <= EW_BLOCK_BYTES:
        return r, c
    for tr in (512, 256, 128, 64, 32):
        if r % tr == 0 and tr * c * 4 <= EW_BLOCK_BYTES:
            return tr, c
    for tc in (1024, 512, 256, 128):
        if c % tc == 0 and r * tc * 4 <= EW_BLOCK_BYTES:
            return r, tc
    raise ValueError((r, c))


def adamw(name, w, g, m, v):
    shape = w.shape
    c = shape[-1]
    r = w.size // c
    tr, tc = _tile2d(r, c)
    assert tc == c
    c1, c2 = 1.0 - ADAM_B1 ** ADAM_STEP, 1.0 - ADAM_B2 ** ADAM_STEP

    def body(w_ref, g_ref, m_ref, v_ref, d_ref, mo_ref, vo_ref):
        gv = g_ref[...]
        mn = ADAM_B1 * m_ref[...] + (1.0 - ADAM_B1) * gv
        vn = ADAM_B2 * v_ref[...] + (1.0 - ADAM_B2) * (gv * gv)
        d_ref[...] = -ADAM_LR * ((mn / c1) / (jnp.sqrt(vn / c2) + ADAM_EPS) + ADAM_WD * w_ref[...])
        mo_ref[...] = mn
        vo_ref[...] = vn

    spec = pl.BlockSpec((tr, c), lambda i: (i, 0))
    outs = pl.pallas_call(
        body,
        name=name,
        grid=(r // tr,),
        in_specs=[spec] * 4,
        out_specs=[spec] * 3,
        out_shape=[jax.ShapeDtypeStruct((r, c), F32)] * 3,
        compiler_params=_cparams(("parallel",)),
    )(*[a.reshape(r, c) for a in (w, g, m, v)])
    return [o.reshape(shape) for o in outs]


def adamw_pair(name, w, reduced, received, m, v, c_idx):
    _, r, c = w.shape
    tr, tc = _tile2d(r, c)
    c1, c2 = 1.0 - ADAM_B1 ** ADAM_STEP, 1.0 - ADAM_B2 ** ADAM_STEP

    def body(c_ref, w_ref, r0_ref, x0_ref, r1_ref, x1_ref, m_ref, v_ref, g_ref, d_ref, mo_ref, vo_ref):
        mine = c_ref[0]
        gv = jnp.where(pl.program_id(0) == 0, jnp.where(mine == 0, r0_ref[...], x0_ref[...]),
                       jnp.where(mine == 1, r1_ref[...], x1_ref[...]))
        mn = ADAM_B1 * m_ref[...] + (1.0 - ADAM_B1) * gv
        vn = ADAM_B2 * v_ref[...] + (1.0 - ADAM_B2) * (gv * gv)
        g_ref[...] = gv
        d_ref[...] = -ADAM_LR * ((mn / c1) / (jnp.sqrt(vn / c2) + ADAM_EPS) + ADAM_WD * w_ref[...])
        mo_ref[...] = mn
        vo_ref[...] = vn

    full = pl.BlockSpec((None, tr, tc), lambda h, i, j, cr: (h, i, j))
    half = pl.BlockSpec((tr, tc), lambda h, i, j, cr: (i, j))
    return pl.pallas_call(
        body,
        name=name,
        grid_spec=pltpu.PrefetchScalarGridSpec(
            num_scalar_prefetch=1,
            grid=(2, r // tr, c // tc),
            in_specs=[full, half, half, half, half, full, full],
            out_specs=[full] * 4,
        ),
        out_shape=[jax.ShapeDtypeStruct((2, r, c), F32)] * 4,
        compiler_params=_cparams(("parallel", "parallel", "parallel")),
    )(c_idx, w, reduced[0], received[0], reduced[1], received[1], m, v)


def sum_pair(name, g, la, out_dtype):
    _, r, c = g.shape
    tr, tc = _tile2d(r, c)

    def body(g_ref, la_ref, o_ref):
        o_ref[...] = (g_ref[...] + la_ref[...]).astype(o_ref.dtype)

    spec = pl.BlockSpec((None, tr, tc), lambda s, i, j: (s, i, j))
    return pl.pallas_call(
        body,
        name=name,
        grid=(4, r // tr, c // tc),
        in_specs=[spec, spec],
        out_specs=spec,
        out_shape=jax.ShapeDtypeStruct((4, r, c), out_dtype),
        compiler_params=_cparams(("parallel", "parallel", "parallel")),
    )(g, la)


def sum_chips(name, g, la, lb, s_idx):
    _, r, c = g.shape
    tr, tc = _tile2d(r, c)

    def body(s_ref, g_ref, la_ref, l0_ref, l1_ref, l2_ref, o_ref):
        own = g_ref[...] + la_ref[...]
        o_ref[...] = ((own + l0_ref[...].astype(F32)) + l1_ref[...].astype(F32)) + l2_ref[...].astype(F32)

    own_spec = pl.BlockSpec((None, tr, tc), lambda i, j, sr: (sr[0], i, j))
    lspec = lambda k: pl.BlockSpec((None, tr, tc), lambda i, j, sr: (k, i, j))
    return pl.pallas_call(
        body,
        name=name,
        grid_spec=pltpu.PrefetchScalarGridSpec(
            num_scalar_prefetch=1,
            grid=(r // tr, c // tc),
            in_specs=[own_spec, own_spec, lspec(0), lspec(1), lspec(2)],
            out_specs=pl.BlockSpec((tr, tc), lambda i, j, sr: (i, j)),
        ),
        out_shape=jax.ShapeDtypeStruct((r, c), F32),
        compiler_params=_cparams(("parallel", "parallel")),
    )(s_idx, g, la, lb, lb, lb)


def _place():
    x, y, c = lax.axis_index("x"), lax.axis_index("y"), lax.axis_index("c")
    chips = [(1 - x, y), (x, 1 - y), (1 - x, 1 - y)]
    return x, y, c, chips


HBM_SPEC = pl.BlockSpec(memory_space=pltpu.HBM)


def _comm_call(name, body, ins, out_shapes, n_sem, n_local):
    return pl.pallas_call(
        body,
        name=name,
        in_specs=[HBM_SPEC] * len(ins),
        out_specs=[HBM_SPEC] * len(out_shapes),
        out_shape=out_shapes,
        scratch_shapes=[pltpu.SemaphoreType.DMA((n_sem,)), pltpu.SemaphoreType.DMA((n_sem,)),
                        pltpu.SemaphoreType.DMA((max(n_local, 1),))],
    )(*ins)


def _layer_gather(ins, outs, send, recv, layer):
    n = len(ins)
    x, y, c, chips = _place()
    s = 2 * x + y
    sib = (x, y, 1 - c)
    active = c == layer

    def rc(a, k, src, dst, dev):
        return pltpu.make_async_remote_copy(src_ref=src, dst_ref=dst, send_sem=send.at[6 * a + k], recv_sem=recv.at[6 * a + k],
                                            device_id=dev, device_id_type=MESH)

    def first_hop():
        return [rc(a, j, ins[a], outs[a].at[s], (*chip, c)) for j, chip in enumerate(chips) for a in range(n)]

    def start():
        @pl.when(active)
        def _():
            for cp in first_hop():
                cp.start()

    def finish():
        @pl.when(active)
        def _():
            forwards = []
            for j, (cx, cy) in enumerate(chips):
                for a in range(n):
                    landed = outs[a].at[2 * cx + cy]
                    rc(a, j, landed, landed, sib).wait_recv()
                    forwards.append(rc(a, 3 + j, landed, landed, sib))
                    forwards[-1].start()
            for cp in first_hop() + forwards:
                cp.wait_send()

        @pl.when(jnp.logical_not(active))
        def _():
            for j, (cx, cy) in enumerate(chips):
                for a in range(n):
                    other = outs[a].at[2 * cx + cy]
                    rc(a, 3 + j, other, other, sib).wait_recv()

    return start, finish


def _gather_shapes(ws):
    return [jax.ShapeDtypeStruct((4,) + w.shape, w.dtype) for w in ws]


def gather_hook(ws, layer):
    return (ws, _gather_shapes(ws), 6 * len(ws), lambda i, o, s, r: _layer_gather(i, o, s, r, layer))


def allgather_layer(ws, layer):
    n = len(ws)

    def body(*refs):
        send, recv, _ = refs[2 * n:]
        start, finish = _layer_gather(refs[:n], refs[n:2 * n], send, recv, layer)
        start()
        finish()

    return _comm_call(f"allgather_layer{layer}", body, ws, _gather_shapes(ws), 6 * n, 0)


def exchange_sibling(gs, layer, tag=""):
    n = len(gs)

    def body(*refs):
        send, recv, _ = refs[2 * n:]
        start, finish = _sibling_exchange(refs[:n], refs[n:2 * n], send, recv, layer)
        start()
        finish()

    return _comm_call(f"exchange_sibling{layer}{tag}", body, gs, [jax.ShapeDtypeStruct(g.shape, g.dtype) for g in gs], n, 0)


def _sibling_exchange(ins, outs, send, recv, layer):
    x, y, c, _ = _place()

    def copies():
        return [pltpu.make_async_remote_copy(src_ref=ins[a], dst_ref=outs[a], send_sem=send.at[a], recv_sem=recv.at[a],
                                             device_id=(x, y, 1 - c), device_id_type=MESH) for a in range(len(ins))]

    def start():
        @pl.when(c != layer)
        def _():
            for cp in copies():
                cp.start()

    def finish():
        @pl.when(c != layer)
        def _():
            for cp in copies():
                cp.wait_send()

        @pl.when(c == layer)
        def _():
            for cp in copies():
                cp.wait_recv()

    return start, finish


def sibling_hook(gs, layer):
    return (gs, [jax.ShapeDtypeStruct(g.shape, g.dtype) for g in gs], len(gs),
            lambda i, o, s, r: _sibling_exchange(i, o, s, r, layer))


def chips_hook(ps, layer):
    return (ps, _chips_shapes(ps), 3 * len(ps), lambda i, o, s, r: _chips_exchange(i, o, s, r, layer))


def _chips_exchange(ins, outs, send, recv, layer):
    n = len(ins)
    _, _, c, chips = _place()

    def copies():
        return [pltpu.make_async_remote_copy(src_ref=ins[a].at[2 * cx + cy], dst_ref=outs[a].at[j], send_sem=send.at[3 * a + j],
                                             recv_sem=recv.at[3 * a + j], device_id=(cx, cy, c), device_id_type=MESH)
                for j, (cx, cy) in enumerate(chips) for a in range(n)]

    def start():
        @pl.when(c == layer)
        def _():
            for cp in copies():
                cp.start()

    def finish():
        @pl.when(c == layer)
        def _():
            for cp in copies():
                cp.wait()

    return start, finish


def _chips_shapes(ps):
    return [jax.ShapeDtypeStruct((3,) + p.shape[1:], p.dtype) for p in ps]


def exchange_chips(ps, layer, tag=""):
    n = len(ps)

    def body(*refs):
        send, recv, _ = refs[2 * n:]
        start, finish = _chips_exchange(refs[:n], refs[n:2 * n], send, recv, layer)
        start()
        finish()

    return _comm_call(f"exchange_chips{layer}{tag}", body, ps, _chips_shapes(ps), 3 * n, 0)


def exchange_final(rs, layer, small=None):
    n = len(rs)
    ins_all = list(rs) + ([small] if small is not None else [])

    def body(*refs):
        ni = len(ins_all)
        ins, outs = refs[:ni], refs[ni:2 * ni]
        send, recv, lsem = refs[2 * ni:]
        x, y, c, _ = _place()
        s = 2 * x + y
        to_sib = [pltpu.make_async_remote_copy(src_ref=ins[a], dst_ref=outs[a], send_sem=send.at[a], recv_sem=recv.at[a],
                                               device_id=(x, y, 1 - c), device_id_type=MESH) for a in range(n)]
        rel = [(fx, fy) for fx in (0, 1) for fy in (0, 1)]

        def piece(fx, fy, cc, dst_slot):
            k = n + 2 * (2 * fx + fy) + cc
            return pltpu.make_async_remote_copy(src_ref=ins[n], dst_ref=outs[n].at[dst_slot], send_sem=send.at[k], recv_sem=recv.at[k],
                                                device_id=(x ^ fx, y ^ fy, cc), device_id_type=MESH)

        @pl.when(c == layer)
        def _():
            for cp in to_sib:
                cp.start()
            if small is not None:
                own = pltpu.make_async_copy(ins[n], outs[n].at[s], lsem.at[0])
                own.start()
                sends = [piece(fx, fy, cc, s) for fx, fy in rel for cc in (0, 1) if (fx, fy) != (0, 0) or cc != layer]
                for cp in sends:
                    cp.start()
                for fx, fy in rel[1:]:
                    piece(fx, fy, layer, 2 * (x ^ fx) + (y ^ fy)).wait_recv()
                for cp in sends:
                    cp.wait_send()
                own.wait()
            for cp in to_sib:
                cp.wait_send()

        @pl.when(c != layer)
        def _():
            for cp in to_sib:
                cp.wait_recv()
            if small is not None:
                for fx, fy in rel:
                    piece(fx, fy, 1 - layer, 2 * (x ^ fx) + (y ^ fy)).wait_recv()

    out_shapes = [jax.ShapeDtypeStruct(r.shape, r.dtype) for r in rs]
    if small is not None:
        out_shapes.append(jax.ShapeDtypeStruct((4,) + small.shape, small.dtype))
    res = _comm_call(f"exchange_final{layer}", body, ins_all, out_shapes, n + 8, 1)
    return (res[:n], res[n]) if small is not None else (res[:n], None)


def _pad_in_cols(w):
    z = lambda n: jnp.zeros(w.shape[:-1] + (n,), w.dtype)
    return jnp.concatenate([w[..., :KR_OFF], z(64), w[..., KR_OFF:KR_OFF + MLA_ROPE], z(32), w[..., KR_OFF + MLA_ROPE:]], axis=-1)


def _pad_heads(w, real):
    k = w.shape[0]
    return jnp.pad(w.reshape(k, MLA_HEADS, real), ((0, 0), (0, 0), (0, LANES - real))).reshape(k, MLA_HEADS * LANES)


def _pad_gain(g, real):
    return jnp.pad(g.reshape(1, real), ((0, 0), (0, LANES - real)))


def layer_weights(full):
    w_ukv = full["w_ukv"].reshape(128, MLA_HEADS, 2, 64)
    two = lambda g: jnp.concatenate([g, g]).reshape(1, LANES)
    return dict(
        norm_g=full["norm_g"].reshape(1, -1), w_in=full["w_in"], conv_w=full["conv_w"], conv_b=full["conv_b"].reshape(1, -1),
        wgx=full["w_gate_x"], bgx=full["b_gate_x"].reshape(LRU_BLOCKS, 1, LANES),
        wga=full["w_gate_a"], bga=full["b_gate_a"].reshape(LRU_BLOCKS, 1, LANES),
        lam=full["lru_lambda"].reshape(1, -1), w_lru_o=full["w_lru_o"],
        cq_norm_g=full["cq_norm_g"].reshape(1, -1), ckv_norm_g=full["ckv_norm_g"].reshape(1, -1),
        wq=_pad_heads(full["w_uq"], MLA_QK), wk=_pad_heads(w_ukv[:, :, 0].reshape(128, 512), 64),
        wv=w_ukv[:, :, 1].reshape(128, 512),
        gq=_pad_gain(full["mla_q_norm_g"], MLA_QK), gk=_pad_gain(full["mla_k_norm_g"], MLA_QK),
        w_mla_o=full["w_mla_o"], gq2=two(full["dil_q_norm_g"]), gk2=two(full["dil_k_norm_g"]),
        w_dil_o=full["w_dil_o"], b_merge=full["b_merge"].reshape(1, -1), w_out=full["w_out"],
    )


def layer_fwd(x, w, tabs, gather=None):
    h, ht = rmsnorm_fwd(x, w["norm_g"])
    z_lru = mm_nn("in_proj_lru", h, w["w_in"], n=2048, n_off=G_LRU, tm=1024)
    z_mla = mm_nn("in_proj_mla", h, w["w_in"], n=1024, n_off=G_MLA, tm=1024)
    z_dil = mm_nn("in_proj_dil", h, w["w_in"], n=5120, n_off=G_DIL, tm=1024)
    z_mrg = mm_nn("in_proj_mrg", h, w["w_in"], n=3072, n_off=G_MRG, tm=1024)
    rest_hook = None if gather is None else gather_hook(gather[0][1:], gather[1])
    hs, y_lru, yt_lru, *ga_rest = lru_fwd(z_lru, w["conv_w"], w["conv_b"], w["wgx"], w["bgx"], w["wga"], w["bga"], w["lam"],
                                          comm=rest_hook)
    qm, km, vm, vtm = mla_prep_fwd(z_mla, tabs, w)
    o_mla, y_mla, yt_mla, lse_mla, *ga_first = mla_attn_fwd(qm, km, vtm, z_mla,
                                                            gather=None if gather is None else (gather[0][:1], gather[1]))
    gathered = ga_first + ga_rest
    os_, lses, dil_rm = [], [], []
    for gi, (_, dil) in enumerate(DIL_GROUPS):
        qkv = dil_prep_fwd(z_dil, tabs, w["gq2"], w["gk2"], gi)
        o, lse = dil_attn_fwd(f"dil_attn_fwd_{dil}", *qkv, dil)
        os_.append(o)
        lses.append(lse)
        dil_rm.append(qkv)
    y_dil, o_dil, lse_dil, yt_dil = dil_combine(os_, lses, z_dil)
    ps = [mm_nn("proj_lru", y_lru, w["w_lru_o"]), mm_nn("proj_mla", y_mla, w["w_mla_o"]), mm_nn("proj_dil", y_dil, w["w_dil_o"])]
    merged, merged_t = merge_fwd(ps, z_mrg, w["b_merge"])
    out = mm_nn("out_proj", merged, w["w_out"], add=x)
    res = dict(x=x, ht=ht, z_lru=z_lru, z_mla=z_mla, z_dil=z_dil, z_mrg=z_mrg, hs=hs, yt_lru=yt_lru, qm=qm, km=km, vm=vm, o_mla=o_mla,
               yt_mla=yt_mla, lse_mla=lse_mla, dil_rm=dil_rm, yt_dil=yt_dil, o_dil=o_dil, lse_dil=lse_dil, ps=ps, merged_t=merged_t)
    return out, res, gathered


def layer_bwd(dout, r, w, tabs, prev=None, own=None):
    g = {}
    dmerged = mm_nt("out_proj_dx", dout, w["w_out"])
    g["w_out"] = mm_nn("out_proj_dw", r["merged_t"], dout, tm=1024, tn=512, tk=2048)
    dp0, dp1, dp2, dzm0, dzm1, dzm2, db0, db1, db2 = merge_bwd(dmerged, r["ps"], r["z_mrg"], w["b_merge"])
    g["b_merge"] = jnp.concatenate([db0, db1, db2], axis=1).reshape(-1)
    dy_lru = mm_nt("proj_lru_dx", dp0, w["w_lru_o"])
    dy_mla = mm_nt("proj_mla_dx", dp1, w["w_mla_o"])
    dy_dil = mm_nt("proj_dil_dx", dp2, w["w_dil_o"])
    g["w_lru_o"] = mm_nn("proj_lru_dw", r["yt_lru"], dp0, **DW_TILES)
    g["w_mla_o"] = mm_nn("proj_mla_dw", r["yt_mla"], dp1, **DW_TILES)
    g["w_dil_o"] = mm_nn("proj_dil_dw", r["yt_dil"], dp2, **DW_TILES)
    dzx, dzg_lru, dwgx, dbgx, dwga, dbga, dlam, dcw, dcb, *la_prev = lru_bwd(
        r["z_lru"], r["hs"], dy_lru, w["conv_w"], w["conv_b"], w["wgx"], w["bgx"], w["wga"], w["bga"], w["lam"],
        comm=None if prev is None else sibling_hook(prev[0], prev[1]))
    chips = None if prev is None else (prev[2](la_prev), prev[1])
    g.update(w_gate_x=dwgx, b_gate_x=dbgx.reshape(LRU_BLOCKS, LANES), w_gate_a=dwga, b_gate_a=dbga.reshape(LRU_BLOCKS, LANES),
             lru_lambda=dlam.reshape(-1), conv_w=dcw, conv_b=dcb.reshape(-1))
    do_m, dzg_mla, dd_m = mla_gate_bwd(dy_mla, r["o_mla"], r["z_mla"])
    dq_m, dk_m, dv_m, *lb_prev = mla_attn_bwd(r["qm"], r["km"], r["vm"], do_m, r["lse_mla"], dd_m, chips=chips)
    dz_mla3, dg_cq, dg_ckv, dwq, dwk, dwv, dgq, dgk = mla_prep_bwd(r["z_mla"], tabs, w, dq_m, dk_m, dv_m)
    g.update(cq_norm_g=dg_cq.reshape(-1), ckv_norm_g=dg_ckv.reshape(-1), mla_q_norm_g=dgq[0, :MLA_QK], mla_k_norm_g=dgk[0, :MLA_QK])
    g["w_uq"] = dwq.reshape(256, MLA_HEADS, LANES)[:, :, :MLA_QK].reshape(256, MLA_HEADS * MLA_QK)
    g["w_ukv"] = jnp.concatenate([dwk.reshape(128, MLA_HEADS, LANES)[:, :, :64], dwv.reshape(128, MLA_HEADS, 64)], axis=2).reshape(128, 1024)
    dzg_dil, do1, dd1, do2, dd2, l2, do3, dd3, l3 = dil_gate_bwd(dy_dil, r["o_dil"], r["z_dil"], r["lse_dil"])
    stats = [(do1, r["lse_dil"], dd1), (do2, l2, dd2), (do3, l3, dd3)]
    dzq, dzk, dzv, dgq2, dgk2 = [], [], [], [], []
    for gi, (_, dil) in enumerate(DIL_GROUPS):
        dq, dk, dv = dil_attn_bwd(f"dil_attn_bwd_{dil}", *r["dil_rm"][gi], *stats[gi], dil)
        parts = dil_prep_bwd(r["z_dil"], tabs, w["gq2"], w["gk2"], dq, dk, dv, gi)
        for acc, part in zip((dzq, dzk, dzv, dgq2, dgk2), parts):
            acc.append(part)
    g.update(dil_q_norm_g=sum(dgq2)[0, :DIL_HD], dil_k_norm_g=sum(dgk2)[0, :DIL_HD])
    dz = jnp.concatenate([dzx, dzg_lru, dz_mla3, dzg_mla] + dzq + dzk + dzv + [dzg_dil, dzm0, dzm1, dzm2], axis=1)
    g["w_in"] = in_proj_dw(r["ht"], dz)
    own_hook = None if own is None else chips_hook(*own(g))
    out = mm_nt("in_proj_dx", dz, w["w_in"], tm=1024, tn=1024, tk=IN_PAD // 4, comm=own_hook)
    dh, lb_own = (out, []) if own_hook is None else (out[0], out[1:])
    dx, dng = rmsnorm_bwd(r["x"], dh, dout, w["norm_g"])
    g["norm_g"] = dng.reshape(-1)
    return dx, g, dict(la_prev=la_prev, lb_prev=lb_prev, lb_own=lb_own)


def local_step(x, positions, target, full0, full1=None, gather1=None, reduce=None):
    tabs = rope_tables(positions.reshape(-1, 1))
    ws, ress = [], []
    for l in range(2):
        if l == 0:
            ws.append(layer_weights(full0))
            x, res, gathered = layer_fwd(x, ws[0], tabs, gather=None if gather1 is None else (gather1[0], 1))
        else:
            ws.append(layer_weights(full1 if gather1 is None else gather1[1](gathered)))
            x, res, _ = layer_fwd(x, ws[1], tabs)
        ress.append(res)
    dy, loss = loss_head(x, target)
    dy, grads1, _ = layer_bwd(dy, ress[1], ws[1], tabs)
    prev1, own0 = (None, None) if reduce is None else (reduce[0](grads1), reduce[1])
    dy, grads0, landed = layer_bwd(dy, ress[0], ws[0], tabs, prev=prev1, own=own0)
    return loss, dy, [grads0, grads1], landed


WEIGHTS = ["norm_g", "w_in", "conv_w", "conv_b", "w_gate_x", "b_gate_x", "w_gate_a", "b_gate_a", "lru_lambda", "w_lru_o", "cq_norm_g",
           "ckv_norm_g", "w_uq", "w_ukv", "mla_q_norm_g", "mla_k_norm_g", "w_mla_o", "dil_q_norm_g", "dil_k_norm_g", "w_dil_o", "b_merge",
           "w_out"]
SHARDED = {"w_in": 2, "conv_w": 2, "w_lru_o": 1, "w_uq": 2, "w_ukv": 2, "w_mla_o": 2, "w_dil_o": 2, "w_out": 1}
REPLICATED = [n for n in WEIGHTS if n not in SHARDED]
SMALL_ROWS = 144


def kernel(x, positions, norm_g, w_in, conv_w, conv_b, w_gate_x, b_gate_x, w_gate_a, b_gate_a, lru_lambda, w_lru_o, cq_norm_g, ckv_norm_g, w_uq, w_ukv, mla_q_norm_g, mla_k_norm_g, w_mla_o, dil_q_norm_g, dil_k_norm_g, w_dil_o, b_merge, w_out, loss_target, m_norm_g, m_w_in, m_conv_w, m_conv_b, m_w_gate_x, m_b_gate_x, m_w_gate_a, m_b_gate_a, m_lru_lambda, m_w_lru_o, m_cq_norm_g, m_ckv_norm_g, m_w_uq, m_w_ukv, m_mla_q_norm_g, m_mla_k_norm_g, m_w_mla_o, m_dil_q_norm_g, m_dil_k_norm_g, m_w_dil_o, m_b_merge, m_w_out, v_norm_g, v_w_in, v_conv_w, v_conv_b, v_w_gate_x, v_b_gate_x, v_w_gate_a, v_b_gate_a, v_lru_lambda, v_w_lru_o, v_cq_norm_g, v_ckv_norm_g, v_w_uq, v_w_ukv, v_mla_q_norm_g, v_mla_k_norm_g, v_w_mla_o, v_dil_q_norm_g, v_dil_k_norm_g, v_w_dil_o, v_b_merge, v_w_out):
    args = locals()
    w = {n: args[n] for n in WEIGHTS}
    m = {n: args["m_" + n] for n in WEIGHTS}
    v = {n: args["v_" + n] for n in WEIGHTS}
    my_c = lax.axis_index("c").astype(jnp.int32)
    my_s = (2 * lax.axis_index("x") + lax.axis_index("y")).astype(jnp.int32)
    c_idx = my_c.reshape(1)
    s_idx = my_s.reshape(1)

    names = list(SHARDED)
    wire = [[w[n][l] if n == "conv_w" else w[n][l].astype(BF16) for n in names] for l in range(2)]

    def assemble(l, gathered):
        full = {n: w[n][l] for n in REPLICATED}
        for n, ga, own in zip(names, gathered, wire[l]):
            parts = [jnp.where(my_s == s, own, ga[s]) for s in range(4)]
            if n == "w_in":
                parts = [_pad_in_cols(parts[0])] + parts[1:]
            full[n] = jnp.concatenate(parts, axis=SHARDED[n] - 1)
        return full

    full0 = assemble(0, allgather_layer(wire[0], 0))

    def shards_of(grads_l):
        gs = []
        for n in names:
            g_ = grads_l[n]
            if n == "w_in":
                gs.append(g_.reshape(4, IN_WIDTH // 4, D_MODEL))
            else:
                parts = jnp.stack(jnp.split(g_, 4, axis=SHARDED[n] - 1))
                gs.append(parts.reshape(4, -1, parts.shape[-1]))
        return gs

    def pair_sums(l, gs, la, tags):
        dts = [BF16 if g_.size >= 2**19 and t != "small" else F32 for g_, t in zip(gs, tags)]
        return [sum_pair(f"sum_pair{l}_{t}", g_, l_, dt) for t, g_, l_, dt in zip(tags, gs, la, dts)]

    def chip_sums(l, gs, la, lb, tags):
        return [sum_chips(f"sum_chips{l}_{t}", g_, l_, b_, s_idx) for t, g_, l_, b_ in zip(tags, gs, la, lb)]

    stash = {}

    def prev1(grads1):
        stash["gs1"] = shards_of(grads1)
        return stash["gs1"], 1, lambda la: pair_sums(1, stash["gs1"], la, names)

    def own0(grads0):
        stash["gs0"] = shards_of(grads0)
        stash["la0"] = exchange_sibling(stash["gs0"], 0)
        return pair_sums(0, stash["gs0"], stash["la0"], names), 0

    loss, grad_x, grads, landed = local_step(x[0], positions[0], loss_target[0], full0,
                                             gather1=(wire[1], functools.partial(assemble, 1)), reduce=(prev1, own0))
    loss = lax.psum(loss, ("x", "y", "c"))
    reduced1 = chip_sums(1, stash["gs1"], landed["la_prev"], landed["lb_prev"], names)
    received1, _ = exchange_final(reduced1, 1)
    reduced0 = chip_sums(0, stash["gs0"], stash["la0"], landed["lb_own"], names)

    flat = jnp.concatenate([jnp.stack([grads[0][n], grads[1][n]]).reshape(-1) for n in REPLICATED])
    small = [jnp.pad(flat, (0, 4 * SMALL_ROWS * 1024 - flat.size)).reshape(4, SMALL_ROWS, 1024)]
    la_s = exchange_sibling(small, 0, tag="_small")
    lb_s = exchange_chips(pair_sums(0, small, la_s, ["small"]), 0, tag="_small")
    received0, small_all = exchange_final(reduced0, 0, small=chip_sums(0, small, la_s, lb_s, ["small"])[0])

    g_local, delta, new_m, new_v = {}, {}, {}, {}
    for i, n in enumerate(names):
        shp = w[n].shape
        if n == "w_in":
            as3 = lambda a: a.transpose(0, 2, 1)
            back = lambda o: o.transpose(0, 2, 1)
        else:
            as3 = lambda a, rc=reduced0[i].shape: a.reshape((2,) + rc)
            back = lambda o, shp=shp: o.reshape(shp)
        outs = adamw_pair(f"adamw_{n}", as3(w[n]), (reduced0[i], reduced1[i]), (received0[i], received1[i]), as3(m[n]), as3(v[n]),
                          c_idx)
        g_local[n], delta[n], new_m[n], new_v[n] = [back(o) for o in outs]
    flat = small_all.reshape(-1)
    off = 0
    for n in REPLICATED:
        g_local[n] = flat[off:off + w[n].size].reshape(w[n].shape)
        off += w[n].size
        delta[n], new_m[n], new_v[n] = adamw(f"adamw_{n}", w[n], g_local[n], m[n], v[n])
    return (loss, grad_x[None], *[g_local[n] for n in WEIGHTS], *[delta[n] for n in WEIGHTS], *[new_m[n] for n in WEIGHTS],
            *[new_v[n] for n in WEIGHTS])
```

```python
import functools

import jax
import jax.numpy as jnp
from jax import lax
from jax.experimental import pallas as pl
from jax.experimental.pallas import tpu as pltpu

F32 = jnp.float32
BF16 = jnp.bfloat16
MXU_DTYPE = jnp.bfloat16

D_MODEL = 1024
EPS = 1e-6
ROPE_THETA = 10000.0
LRU_BLOCKS = 8
LRU_C = 8.0
MLA_HEADS = 8
MLA_NOPE = 64
MLA_ROPE = 32
MLA_QK = 96
DIL_GROUPS = ((128, 1), (512, 4), (2048, 16))
DIL_HD = 64
DIL_NK = 128
MLA_HPS = 8
IN_WIDTH = 11168
ADAM_LR, ADAM_B1, ADAM_B2, ADAM_EPS, ADAM_WD, ADAM_STEP = 0.001, 0.9, 0.999, 1e-08, 0.01, 10

LANES = 128
G_LRU, G_MLA, G_DIL, G_MRG = 0, 2048, 3072, 8192
IN_PAD = 11264
KR_OFF = 2432

NN = (((1,), (0,)), ((), ()))
NT = (((1,), (1,)), ((), ()))
TN = (((0,), (0,)), ((), ()))
NEG = -1e30
MESH = pl.DeviceIdType.MESH
VMEM_LIMIT = 48 * 2**20


def _cparams(sem):
    return pltpu.CompilerParams(dimension_semantics=sem, vmem_limit_bytes=VMEM_LIMIT)


def _dot(a, b, dims):
    return lax.dot_general(a.astype(MXU_DTYPE), b.astype(MXU_DTYPE), dims, preferred_element_type=F32)


@jax.custom_vjp
def mm(a, w):
    return _dot(a, w, NN)


def _mm_fwd(a, w):
    return _dot(a, w, NN), (a, w)


def _mm_bwd(res, g):
    a, w = res
    return _dot(g, w, NT), _dot(a, g, TN)


mm.defvjp(_mm_fwd, _mm_bwd)


def _seg64_matrix():
    r = lax.broadcasted_iota(jnp.int32, (LANES, LANES), 0) < DIL_HD
    c = lax.broadcasted_iota(jnp.int32, (LANES, LANES), 1) < DIL_HD
    return (r == c).astype(BF16)


def _seg_sum_impl(x):
    b = _seg64_matrix()
    hi = x.astype(BF16)
    r1 = x - hi.astype(F32)
    mid = r1.astype(BF16)
    lo = (r1 - mid.astype(F32)).astype(BF16)
    dot = lambda u: lax.dot_general(u, b, NN, preferred_element_type=F32)
    return dot(hi) + dot(mid) + dot(lo)


@jax.custom_vjp
def seg_sum(x):
    return _seg_sum_impl(x)


seg_sum.defvjp(lambda x: (_seg_sum_impl(x), None), lambda _, g: (_seg_sum_impl(g),))


def _lroll_impl(x, s):
    return pltpu.roll(x, s % LANES, 1)


@functools.partial(jax.custom_vjp, nondiff_argnums=(1,))
def lroll(x, s):
    return _lroll_impl(x, s)


lroll.defvjp(lambda x, s: (_lroll_impl(x, s), None), lambda s, _, g: (_lroll_impl(g, -s),))


class _Ops:
    def __init__(self, diff):
        self.mm = mm if diff else (lambda a, w: _dot(a, w, NN))
        self.seg_sum = seg_sum if diff else _seg_sum_impl
        self.lroll = lroll if diff else _lroll_impl


PLAIN, DIFF = _Ops(False), _Ops(True)


def rms(x, g, n):
    ms = jnp.sum(x * x, axis=-1, keepdims=True) * (1.0 / n)
    return x * lax.rsqrt(ms + EPS) * g


def rope(ops, x, c, s1, s2, half):
    return x * c + ops.lroll(x, -half) * s1 + ops.lroll(x, half) * s2


def _sigmoid(x):
    return 1.0 / (1.0 + jnp.exp(-x))


def _silu_and_grad(g):
    sg = _sigmoid(g)
    return g * sg, sg * (1.0 + g * (1.0 - sg))


def _softplus(x):
    return jnp.maximum(x, 0.0) + jnp.log(1.0 + jnp.exp(-jnp.abs(x)))


def _expm1(y):
    series = y * (1.0 + y * (0.5 + y * (1.0 / 6.0 + y * (1.0 / 24.0 + y * (1.0 / 120.0)))))
    return jnp.where(y > -0.05, series, jnp.exp(jnp.minimum(y, -0.05)) - 1.0)


@jax.custom_vjp
def sigmoid_d(x):
    return _sigmoid(x)


def _sigmoid_d_fwd(x):
    s = _sigmoid(x)
    return s, s


sigmoid_d.defvjp(_sigmoid_d_fwd, lambda s, g: (g * s * (1.0 - s),))


def _decay_mult(log_a):
    return jnp.sqrt(-_expm1(2.0 * log_a))


@jax.custom_vjp
def decay_mult_d(log_a):
    return _decay_mult(log_a)


def _decay_mult_d_fwd(log_a):
    em = _expm1(2.0 * log_a)
    mult = jnp.sqrt(-em)
    return mult, (em, mult)


decay_mult_d.defvjp(_decay_mult_d_fwd, lambda res, g: (-g * (res[0] + 1.0) / res[1],))
PLAIN.sigmoid, DIFF.sigmoid = _sigmoid, sigmoid_d
PLAIN.decay_mult, DIFF.decay_mult = _decay_mult, decay_mult_d


def _host(body, n_in, n_out, n_scr, hook, grid):
    if hook is None:
        return body, [], [], [], [], []
    ins, out_shapes, n_sem, make = hook
    ni, no = len(ins), len(out_shapes)
    hbm = pl.BlockSpec(memory_space=pltpu.HBM)

    def at(corner):
        cond = None
        for axis, size in enumerate(grid):
            here = pl.program_id(axis) == (size - 1 if corner else 0)
            cond = here if cond is None else cond & here
        return cond

    def hosted(*refs):
        a, refs = refs[:n_in], refs[n_in:]
        xi, refs = refs[:ni], refs[ni:]
        o, refs = refs[:n_out], refs[n_out:]
        xo, refs = refs[:no], refs[no:]
        scr, (send, recv) = refs[:n_scr], refs[n_scr:]
        start, finish = make(xi, xo, send, recv)
        pl.when(at(False))(start)
        body(*a, *o, *scr)
        pl.when(at(True))(finish)

    sems = [pltpu.SemaphoreType.DMA((n_sem,)), pltpu.SemaphoreType.DMA((n_sem,))]
    return hosted, list(ins), [hbm] * ni, [hbm] * no, list(out_shapes), sems


def _mm_call(name, a, b, *, mode, m, n, k, a_blk, b_blk, out_dtype, add, tm, tn, tk, comm=None):
    nk = k // tk
    assert m % tm == 0 and n % tn == 0 and k % tk == 0, (name, m, n, k, tm, tn, tk)
    dims = {"nn": NN, "nt": NT, "tn": TN}[mode]

    def body(*refs):
        if add is None:
            a_ref, b_ref, o_ref, *scr = refs
            add_ref = None
        else:
            a_ref, b_ref, add_ref, o_ref, *scr = refs
        part = _dot(a_ref[...], b_ref[...], dims)

        def finish(acc):
            if add_ref is not None:
                acc = acc + add_ref[...]
            o_ref[...] = acc.astype(o_ref.dtype)

        if nk == 1:
            finish(part)
        else:
            (acc_ref,) = scr
            kk = pl.program_id(2)

            @pl.when(kk == 0)
            def _():
                acc_ref[...] = part

            @pl.when(kk > 0)
            def _():
                acc_ref[...] += part

            @pl.when(kk == nk - 1)
            def _():
                finish(acc_ref[...])

    in_specs = [a_blk, b_blk]
    args = [a, b]
    if add is not None:
        in_specs.append(pl.BlockSpec((tm, tn), lambda j, i, kk: (i, j)))
        args.append(add)
    grid = (n // tn, m // tm, nk)
    scratch = [] if nk == 1 else [pltpu.VMEM((tm, tn), F32)]
    body, x_in, x_in_specs, x_out_specs, x_out_shapes, sems = _host(body, len(args), 1, len(scratch), comm, grid)
    res = pl.pallas_call(
        body,
        name=name,
        grid=grid,
        in_specs=in_specs + x_in_specs,
        out_specs=[pl.BlockSpec((tm, tn), lambda j, i, kk: (i, j))] + x_out_specs,
        out_shape=[jax.ShapeDtypeStruct((m, n), out_dtype)] + x_out_shapes,
        scratch_shapes=scratch + sems,
        compiler_params=_cparams(("arbitrary",) * 3 if comm is not None else ("parallel", "parallel", "arbitrary")),
    )(*args, *x_in)
    return res[0] if comm is None else res


def mm_nn(name, a, b, *, n=None, n_off=0, out_dtype=F32, add=None, tm=1024, tn=1024, tk=1024):
    m, k = a.shape
    n = b.shape[1] if n is None else n
    tm, tn, tk = min(tm, m), min(tn, n), min(k, tk)
    ob = n_off // tn
    assert n_off % tn == 0
    return _mm_call(name, a, b, mode="nn", m=m, n=n, k=k, out_dtype=out_dtype, add=add, tm=tm, tn=tn, tk=tk,
                    a_blk=pl.BlockSpec((tm, tk), lambda j, i, kk: (i, kk)),
                    b_blk=pl.BlockSpec((tk, tn), lambda j, i, kk: (kk, j + ob)))


def mm_nt(name, a, b, *, out_dtype=F32, tm=1024, tn=1024, tk=1024, comm=None):
    m, k = a.shape
    n = b.shape[0]
    tm, tn, tk = min(tm, m), min(tn, n), min(k, tk)
    return _mm_call(name, a, b, mode="nt", m=m, n=n, k=k, out_dtype=out_dtype, add=None, tm=tm, tn=tn, tk=tk, comm=comm,
                    a_blk=pl.BlockSpec((tm, tk), lambda j, i, kk: (i, kk)),
                    b_blk=pl.BlockSpec((tn, tk), lambda j, i, kk: (j, kk)))


DW_TILES = dict(tm=1024, tn=512, tk=4096)


def in_proj_dw(ht, dz):
    t = ht.shape[1]
    tn = 512
    nchunk = IN_PAD // tn
    kr_chunk = KR_OFF // tn
    lo = KR_OFF - kr_chunk * tn

    def body(ht_ref, dz_ref, out_ref, buf, sems):
        j = pl.program_id(0)
        slot = j % 2
        buf[slot] = _dot(ht_ref[...], dz_ref[...], NN).T

        def copies(c, s):
            dst = jnp.where(c < kr_chunk, c * tn, (c - 1) * tn + lo + MLA_ROPE)
            plain = [pltpu.make_async_copy(buf.at[s], out_ref.at[pl.ds(pl.multiple_of(dst, 8), tn)], sems.at[s, 0])]
            split = [pltpu.make_async_copy(buf.at[s, 0:lo], out_ref.at[KR_OFF - lo:KR_OFF], sems.at[s, 0]),
                     pltpu.make_async_copy(buf.at[s, lo + 64:lo + 96], out_ref.at[KR_OFF:KR_OFF + MLA_ROPE], sems.at[s, 1])]
            return plain, split

        def each(c, s, action):
            plain, split = copies(c, s)

            @pl.when(c != kr_chunk)
            def _():
                for cp in plain:
                    action(cp)

            @pl.when(c == kr_chunk)
            def _():
                for cp in split:
                    action(cp)

        @pl.when(j > 0)
        def _():
            each(j - 1, 1 - slot, lambda cp: cp.wait())

        each(j, slot, lambda cp: cp.start())

        @pl.when(j == nchunk - 1)
        def _():
            each(j, slot, lambda cp: cp.wait())

    return pl.pallas_call(
        body,
        name="in_proj_dw",
        grid=(nchunk,),
        in_specs=[pl.BlockSpec((D_MODEL, t), lambda j: (0, 0)), pl.BlockSpec((t, tn), lambda j: (0, j))],
        out_specs=pl.BlockSpec(memory_space=pltpu.HBM),
        out_shape=jax.ShapeDtypeStruct((IN_WIDTH, D_MODEL), F32),
        scratch_shapes=[pltpu.VMEM((2, tn, D_MODEL), F32), pltpu.SemaphoreType.DMA((2, 2))],
        compiler_params=_cparams(("arbitrary",)),
    )(ht, dz)


def rowwise(name, fn, t, *, tm=256, ncol=1, ins=(), outs=(), touts=(), accs=()):
    n_in, n_out, n_acc = len(ins), len(outs) + len(touts), len(accs)

    def zero_when(ref, cond):
        @pl.when(cond)
        def _():
            ref[...] = jnp.zeros(ref.shape, ref.dtype)

    def body(*refs):
        in_refs, out_refs, acc_refs = refs[:n_in], refs[n_in:n_in + n_out], refs[n_in + n_out:]
        j, i = pl.program_id(0), pl.program_id(1)
        for ref, (_, _, _, cd) in zip(acc_refs, accs):
            zero_when(ref, (i == 0) if cd else ((i == 0) & (j == 0)))
        fn(in_refs, out_refs, acc_refs)

    def in_spec(arr, w, base, cd, rd):
        rows = tm if rd else arr.shape[0]
        return pl.BlockSpec((rows, w), lambda j, i: (i if rd else 0, base + (j if cd else 0)))

    in_specs = [in_spec(*e) for e in ins]
    out_specs = [pl.BlockSpec((tm, w), (lambda j, i, cd=cd: (i, j if cd else 0))) for (_, w, _, cd) in outs]
    out_specs += [pl.BlockSpec((w, tm), (lambda j, i, cd=cd: (j if cd else 0, i))) for (_, w, _, cd) in touts]
    out_specs += [pl.BlockSpec((r, w), (lambda j, i, cd=cd: (0, j if cd else 0))) for (r, _, w, cd) in accs]
    out_shape = [jax.ShapeDtypeStruct((t, c), dt) for (c, _, dt, _) in outs]
    out_shape += [jax.ShapeDtypeStruct((r, t), dt) for (r, _, dt, _) in touts]
    out_shape += [jax.ShapeDtypeStruct((r, c), F32) for (r, c, _, _) in accs]
    res = pl.pallas_call(
        body,
        name=name,
        grid=(ncol, t // tm),
        in_specs=in_specs,
        out_specs=out_specs,
        out_shape=out_shape,
        compiler_params=_cparams(("arbitrary", "arbitrary") if accs else ("parallel", "parallel")),
    )(*[e[0] for e in ins])
    return res


def _row(arr, w=None, base=0, cd=False):
    return (arr, arr.shape[1] if w is None else w, base, cd, True)


def _const(arr, w=None, base=0, cd=False):
    return (arr, arr.shape[1] if w is None else w, base, cd, False)


def rope_tables(positions):
    t = positions.shape[0]

    def fn(ins, outs, _):
        pos = ins[0][...].astype(F32)
        lane = lax.broadcasted_iota(jnp.int32, (1, LANES), 1)
        log_theta = jnp.log(jnp.float32(ROPE_THETA))
        jm = lane - MLA_NOPE
        idx = jnp.clip(jnp.where(jm < 16, jm, jm - 16), 0, 15).astype(F32)
        ang = pos * jnp.exp(-(idx * (2.0 / MLA_ROPE)) * log_theta)
        cos, sin = jnp.cos(ang), jnp.sin(ang)
        in_rope = (lane >= MLA_NOPE) & (lane < MLA_QK)
        outs[0][:, 0:128] = jnp.where(lane < MLA_NOPE, 1.0, jnp.where(in_rope, cos, 0.0))
        outs[0][:, 128:256] = jnp.where(in_rope & (jm < 16), -sin, 0.0)
        outs[0][:, 256:384] = jnp.where(in_rope & (jm >= 16), sin, 0.0)
        jd = lane & (DIL_HD - 1)
        idx = (jd & 31).astype(F32)
        ang = pos * jnp.exp(-(idx * (2.0 / DIL_HD)) * log_theta)
        cos, sin = jnp.cos(ang), jnp.sin(ang)
        outs[0][:, 384:512] = cos
        outs[0][:, 512:640] = jnp.where(jd < 32, -sin, 0.0)
        outs[0][:, 640:768] = jnp.where(jd >= 32, sin, 0.0)

    return rowwise("rope_tables", fn, t, tm=1024, ins=[_row(positions)], outs=[(768, 768, F32, False)])[0]


def rmsnorm_fwd(x, g):
    def fn(ins, outs, _):
        h = rms(ins[0][...], ins[1][...], D_MODEL)
        outs[0][...] = h.astype(MXU_DTYPE)
        outs[1][...] = h.T.astype(MXU_DTYPE)

    return rowwise("rmsnorm_fwd", fn, x.shape[0], tm=512, ins=[_row(x), _const(g)], outs=[(D_MODEL, D_MODEL, MXU_DTYPE, False)],
                   touts=[(D_MODEL, D_MODEL, MXU_DTYPE, False)])


def rmsnorm_bwd(x, dh, dres, g):
    def fn(ins, outs, accs):
        _, vjp = jax.vjp(lambda xv, gv: rms(xv, gv, D_MODEL), ins[0][...], ins[3][...])
        dx, dg = vjp(ins[1][...])
        outs[0][...] = ins[2][...] + dx
        accs[0][...] += dg

    return rowwise("rmsnorm_bwd", fn, x.shape[0], tm=512, ins=[_row(x), _row(dh), _row(dres), _const(g)],
                   outs=[(D_MODEL, D_MODEL, F32, False)], accs=[(1, D_MODEL, D_MODEL, False)])


def loss_head(y, target):
    def fn(ins, outs, accs):
        err = ins[0][...] - ins[1][...]
        outs[0][...] = err * (1.0 / D_MODEL)
        accs[0][...] += jnp.sum(err * err, axis=0, keepdims=True)
        accs[1][...] = jnp.broadcast_to(jnp.sum(accs[0][...], keepdims=True), (1, LANES))

    dy, _, tot = rowwise("loss_head", fn, y.shape[0], tm=512, ins=[_row(y), _row(target)], outs=[(D_MODEL, D_MODEL, F32, False)],
                         accs=[(1, D_MODEL, D_MODEL, False), (1, LANES, LANES, False)])
    return dy, tot[0, 0] * (0.5 / D_MODEL)


def _shift_rows(v, d, fill, reverse):
    tb = v.shape[0]
    if d % 8 == 0:
        pad = jnp.full((d, v.shape[1]), fill, v.dtype)
        return jnp.concatenate([v[d:], pad] if reverse else [pad, v[:tb - d]], axis=0)
    rows = lax.broadcasted_iota(jnp.int32, v.shape, 0)
    if not reverse:
        return jnp.where(rows >= d, pltpu.roll(v, d, 0), fill)
    return jnp.where(rows < tb - d, pltpu.roll(v, tb - d, 0), fill)


def _scan_tile(a, b, reverse):
    d = 1
    while d < a.shape[0]:
        b = b + a * _shift_rows(b, d, 0.0, reverse)
        a = a * _shift_rows(a, d, 1.0, reverse)
        d *= 2
    return a, b


def _lru_gates(ops, xc, wgx, bgx, wga, bga, lam):
    gx = ops.sigmoid(ops.mm(xc, wgx) + bgx)
    ga = ops.sigmoid(ops.mm(xc, wga) + bga)
    log_a = -LRU_C * ga * _softplus(-lam)
    a = jnp.exp(log_a)
    return a, ops.decay_mult(log_a) * (gx * xc)


def _shifted_inputs(x, halo, tb):
    rows = lax.broadcasted_iota(jnp.int32, x.shape, 0)
    pad = jnp.zeros((tb - 8, LANES), F32)
    out = []
    for d in (3, 2, 1):
        head = jnp.concatenate([pltpu.roll(halo, d, 0), pad], axis=0)
        out.append(jnp.where(rows >= d, pltpu.roll(x, d, 0), head))
    return out + [x]


def _lru_specs(nt, tb, reverse):
    hb = tb // 8
    tt = (lambda t: nt - 1 - t) if reverse else (lambda t: t)
    blk = lambda off: pl.BlockSpec((tb, LANES), lambda n, t: (tt(t), n + off))
    halo = lambda off: pl.BlockSpec((8, LANES), lambda n, t: (jnp.maximum(tt(t) * hb - 1, 0), n + off))
    chan = lambda r: pl.BlockSpec((r, LANES), lambda n, t: (0, n))
    wblk = pl.BlockSpec((None, LANES, LANES), lambda n, t: (n, 0, 0))
    bblk = pl.BlockSpec((None, 1, LANES), lambda n, t: (n, 0, 0))
    return blk, halo, chan, wblk, bblk


def lru_fwd(z, conv_w, conv_b, wgx, bgx, wga, bga, lam, *, tb=256, comm=None):
    t = z.shape[0]
    nt = t // tb
    blk, halo, chan, wblk, bblk = _lru_specs(nt, tb, False)

    def body(x_ref, xh_ref, g_ref, cw_ref, cb_ref, wgx_ref, bgx_ref, wga_ref, bga_ref, lam_ref, h_ref, y_ref, yt_ref, carry_ref):
        ti = pl.program_id(1)

        @pl.when(ti == 0)
        def _():
            carry_ref[...] = jnp.zeros((8, LANES), F32)

        x = x_ref[...]
        hal = jnp.where(ti > 0, xh_ref[...], 0.0)
        xs = _shifted_inputs(x, hal, tb)
        xc = cb_ref[...] + sum(xs[k] * cw_ref[k:k + 1, :] for k in range(4))
        a, b = _lru_gates(PLAIN, xc, wgx_ref[...], bgx_ref[...], wga_ref[...], bga_ref[...], lam_ref[...])
        acum, h0 = _scan_tile(a, b, False)
        h = h0 + acum * carry_ref[7:8, :]
        carry_ref[...] = h[tb - 8:tb, :]
        h_ref[...] = h
        y = h * _silu_and_grad(g_ref[...])[0]
        y_ref[...] = y.astype(MXU_DTYPE)
        yt_ref[...] = y.T.astype(MXU_DTYPE)

    grid = (LRU_BLOCKS, nt)
    body, x_in, x_in_specs, x_out_specs, x_out_shapes, sems = _host(body, 10, 3, 1, comm, grid)
    return pl.pallas_call(
        body,
        name="lru_fwd" if comm is None else "lru_fwd_comm",
        grid=grid,
        in_specs=[blk(0), halo(0), blk(LRU_BLOCKS), chan(4), chan(1), wblk, bblk, wblk, bblk, chan(1)] + x_in_specs,
        out_specs=[blk(0), blk(0), pl.BlockSpec((LANES, tb), lambda n, t_: (n, t_))] + x_out_specs,
        out_shape=[jax.ShapeDtypeStruct((t, 1024), F32), jax.ShapeDtypeStruct((t, 1024), MXU_DTYPE),
                   jax.ShapeDtypeStruct((1024, t), MXU_DTYPE)] + x_out_shapes,
        scratch_shapes=[pltpu.VMEM((8, LANES), F32)] + sems,
        compiler_params=_cparams(("arbitrary", "arbitrary") if comm is not None else ("parallel", "arbitrary")),
    )(z, z, z, conv_w, conv_b, wgx, bgx, wga, bga, lam, *x_in)


def lru_bwd(z, h, dy, conv_w, conv_b, wgx, bgx, wga, bga, lam, *, tb=256, comm=None):
    t = z.shape[0]
    nt = t // tb
    blk, halo, chan, wblk, bblk = _lru_specs(nt, tb, True)

    def body(x_ref, xh_ref, g_ref, h_ref, hh_ref, dy_ref, cw_ref, cb_ref, wgx_ref, bgx_ref, wga_ref, bga_ref, lam_ref,
             dzx_ref, dzg_ref, dwgx_ref, dbgx_ref, dwga_ref, dbga_ref, dlam_ref, dcw_ref, dcb_ref,
             gcar_ref, acar_ref, xcar_ref):
        ti = pl.program_id(1)
        has_earlier = ti < nt - 1

        @pl.when(ti == 0)
        def _():
            for ref in (dwgx_ref, dbgx_ref, dwga_ref, dbga_ref, dlam_ref, dcw_ref, dcb_ref, gcar_ref, acar_ref, xcar_ref):
                ref[...] = jnp.zeros(ref.shape, F32)

        rows = lax.broadcasted_iota(jnp.int32, (tb, LANES), 0)
        x = x_ref[...]
        hal = jnp.where(has_earlier, xh_ref[...], 0.0)
        xs = _shifted_inputs(x, hal, tb)
        xc = cb_ref[...] + sum(xs[k] * cw_ref[k:k + 1, :] for k in range(4))
        (a, _), vjp = jax.vjp(functools.partial(_lru_gates, DIFF), xc, wgx_ref[...], bgx_ref[...], wga_ref[...],
                              bga_ref[...], lam_ref[...])
        g, h, dyv = g_ref[...], h_ref[...], dy_ref[...]
        silu, dsilu = _silu_and_grad(g)
        dzg_ref[...] = (dyv * h * dsilu).astype(MXU_DTYPE)
        a_next = jnp.where(rows < tb - 1, pltpu.roll(a, tb - 1, 0), acar_ref[0:1, :])
        acum, g0 = _scan_tile(a_next, dyv * silu, True)
        gt = g0 + acum * gcar_ref[0:1, :]
        h_prev = jnp.where(rows >= 1, pltpu.roll(h, 1, 0), jnp.where(has_earlier, hh_ref[7:8, :], 0.0))
        dxc, dwgx, dbgx, dwga, dbga, dlam = vjp((gt * h_prev, gt))
        later = xcar_ref[...]
        gcar_ref[...] = gt[0:8, :]
        acar_ref[...] = a[0:8, :]
        xcar_ref[...] = dxc[0:8, :]
        dx = dxc * cw_ref[3:4, :]
        pad = jnp.zeros((tb - 8, LANES), F32)
        for d in (1, 2, 3):
            tail = jnp.concatenate([pad, pltpu.roll(later, 8 - d, 0)], axis=0)
            up = jnp.where(rows < tb - d, pltpu.roll(dxc, tb - d, 0), tail)
            dx = dx + up * cw_ref[3 - d:4 - d, :]
        dzx_ref[...] = dx.astype(MXU_DTYPE)
        for k in range(4):
            dcw_ref[k:k + 1, :] += jnp.sum(dxc * xs[k], axis=0, keepdims=True)
        dcb_ref[...] += jnp.sum(dxc, axis=0, keepdims=True)
        dwgx_ref[...] += dwgx
        dbgx_ref[...] += dbgx
        dwga_ref[...] += dwga
        dbga_ref[...] += dbga
        dlam_ref[...] += dlam

    sds = jax.ShapeDtypeStruct
    grid = (LRU_BLOCKS, nt)
    body, x_in, x_in_specs, x_out_specs, x_out_shapes, sems = _host(body, 13, 9, 3, comm, grid)
    return pl.pallas_call(
        body,
        name="lru_bwd" if comm is None else "lru_bwd_comm",
        grid=grid,
        in_specs=[blk(0), halo(0), blk(LRU_BLOCKS), blk(0), halo(0), blk(0), chan(4), chan(1), wblk, bblk, wblk, bblk, chan(1)]
        + x_in_specs,
        out_specs=[blk(0), blk(0), wblk, bblk, wblk, bblk, chan(1), chan(4), chan(1)] + x_out_specs,
        out_shape=[sds((t, 1024), MXU_DTYPE), sds((t, 1024), MXU_DTYPE), sds(wgx.shape, F32), sds(bgx.shape, F32), sds(wga.shape, F32),
                   sds(bga.shape, F32), sds((1, 1024), F32), sds((4, 1024), F32), sds((1, 1024), F32)] + x_out_shapes,
        scratch_shapes=[pltpu.VMEM((8, LANES), F32)] * 3 + sems,
        compiler_params=_cparams(("arbitrary", "arbitrary") if comm is not None else ("parallel", "arbitrary")),
    )(z, z, z, h, h, dy, conv_w, conv_b, wgx, bgx, wga, bga, lam, *x_in)


def _mla_prep_tile(ops, cq, ckv, krp, c, s1, s2, g_cq, g_ckv, wq, wk, wv, gq, gk):
    cqn = rms(cq, g_cq, 256)
    ckvn = rms(ckv, g_ckv, 128)
    v = ops.mm(ckvn, wv)
    qs, ks = [], []
    for hd in range(MLA_HEADS):
        q = rms(ops.mm(cqn, wq[hd]), gq, MLA_QK)
        k = rms(ops.mm(ckvn, wk[hd]) + krp, gk, MLA_QK)
        qs.append(rope(ops, q, c, s1, s2, 16) * (MLA_QK ** -0.5))
        ks.append(rope(ops, k, c, s1, s2, 16))
    return tuple(qs), tuple(ks), v


def _mla_prep_args(ins):
    z_cq, z_ckv, z_kr, tc, ts1, ts2, g_cq, g_ckv, wq, wk, wv, gq, gk = ins[:13]
    heads = lambda w: tuple(w[:, LANES * hd:LANES * (hd + 1)] for hd in range(MLA_HEADS))
    return (z_cq[...], z_ckv[...], z_kr[...], tc[...], ts1[...], ts2[...], g_cq[...], g_ckv[...], heads(wq), heads(wk),
            wv[...], gq[...], gk[...])


def _mla_prep_ins(z, tabs, w):
    return [_row(z, 256, 0), _row(z, 128, 2), _row(z, 128, 3), _row(tabs, 128, 0), _row(tabs, 128, 1), _row(tabs, 128, 2),
            _const(w["cq_norm_g"]), _const(w["ckv_norm_g"]), _const(w["wq"]), _const(w["wk"]), _const(w["wv"]),
            _const(w["gq"]), _const(w["gk"])]


def mla_prep_fwd(z, tabs, w):
    def fn(ins, outs, _):
        qs, ks, v = _mla_prep_tile(PLAIN, *_mla_prep_args(ins))
        for hd in range(MLA_HEADS):
            outs[0][:, LANES * hd:LANES * (hd + 1)] = qs[hd].astype(MXU_DTYPE)
            outs[1][:, LANES * hd:LANES * (hd + 1)] = ks[hd].astype(MXU_DTYPE)
        outs[2][...] = v.astype(MXU_DTYPE)
        outs[3][...] = v.T.astype(MXU_DTYPE)

    return rowwise("mla_prep_fwd", fn, z.shape[0], tm=512, ins=_mla_prep_ins(z, tabs, w),
                   outs=[(1024, 1024, MXU_DTYPE, False), (1024, 1024, MXU_DTYPE, False), (512, 512, MXU_DTYPE, False)],
                   touts=[(512, 512, MXU_DTYPE, False)])


def mla_prep_bwd(z, tabs, w, dq, dk, dv):
    def fn(ins, outs, accs):
        args = _mla_prep_args(ins)
        _, vjp = jax.vjp(functools.partial(_mla_prep_tile, DIFF), *args)
        heads = lambda ref: tuple(ref[:, LANES * hd:LANES * (hd + 1)] for hd in range(MLA_HEADS))
        dcq, dckv, dkr, _, _, _, dg_cq, dg_ckv, dwq, dwk, dwv, dgq, dgk = vjp((heads(ins[13]), heads(ins[14]), ins[15][...]))
        lane = lax.broadcasted_iota(jnp.int32, (1, LANES), 1)
        outs[0][:, 0:256] = dcq.astype(MXU_DTYPE)
        outs[0][:, 256:384] = dckv.astype(MXU_DTYPE)
        outs[0][:, 384:512] = jnp.where((lane >= MLA_NOPE) & (lane < MLA_QK), dkr, 0.0).astype(MXU_DTYPE)
        accs[0][...] += dg_cq
        accs[1][...] += dg_ckv
        for hd in range(MLA_HEADS):
            accs[2][:, LANES * hd:LANES * (hd + 1)] += dwq[hd]
            accs[3][:, LANES * hd:LANES * (hd + 1)] += dwk[hd]
        accs[4][...] += dwv
        accs[5][...] += dgq
        accs[6][...] += dgk

    return rowwise("mla_prep_bwd", fn, z.shape[0], tm=512, ins=_mla_prep_ins(z, tabs, w) + [_row(dq), _row(dk), _row(dv)],
                   outs=[(512, 512, MXU_DTYPE, False)],
                   accs=[(1, 256, 256, False), (1, 128, 128, False), (256, 1024, 1024, False), (128, 1024, 1024, False),
                         (128, 512, 512, False), (1, 128, 128, False), (1, 128, 128, False)])


def _head_masks():
    lane = lax.broadcasted_iota(jnp.int32, (1, LANES), 1)
    return (lane < DIL_HD, lane >= DIL_HD)


def _row_scalar(tile, mask):
    return jnp.max(jnp.where(mask, tile, -jnp.inf), axis=-1, keepdims=True)


def _causal_tiles(nq, by_key):
    pairs = [(i, j) for i in range(nq) for j in range(i + 1)]
    if by_key:
        pairs.sort(key=lambda ij: (ij[1], ij[0]))
    return (jnp.asarray([ij[0] for ij in pairs], jnp.int32), jnp.asarray([ij[1] for ij in pairs], jnp.int32))


def mla_attn_fwd(q, k, vt, z, *, tq=256, gather=None):
    t = q.shape[0]
    nq = t // tq
    it, jt = _causal_tiles(nq, False)
    hps, wq, wv = MLA_HPS, LANES * MLA_HPS, 64 * MLA_HPS
    ws, layer = gather if gather is not None else ([], 0)
    ng = len(ws)
    ngrp, nsteps = MLA_HEADS // hps, int(it.shape[0])

    def body(it_ref, jt_ref, q_ref, k_ref, vt_ref, g_ref, *rest):
        w_refs, rest = rest[:ng], rest[ng:]
        o_ref, y_ref, yt_ref, lse_ref = rest[:4]
        ga_refs, rest = rest[4:4 + ng], rest[4 + ng:]
        m_scr, l_scr, acc_scr = rest[:3]
        step = pl.program_id(1)
        i, j = it_ref[step], jt_ref[step]
        if ng:
            gather_start, gather_finish = _layer_gather(w_refs, ga_refs, rest[3], rest[4], layer)

            @pl.when((pl.program_id(0) == 0) & (step == 0))
            def _():
                gather_start()

        @pl.when(j == 0)
        def _():
            m_scr[...] = jnp.full(m_scr.shape, NEG, F32)
            l_scr[...] = jnp.zeros(l_scr.shape, F32)
            acc_scr[...] = jnp.zeros(acc_scr.shape, F32)

        def update(diagonal):
            heads = range(hps)
            lanes = [slice(LANES * hh, LANES * (hh + 1)) for hh in heads]
            rows = [slice(64 * hh, 64 * (hh + 1)) for hh in heads]
            sts = [_dot(k_ref[:, lanes[hh]], q_ref[:, lanes[hh]], NT) for hh in heads]
            m_prev = [m_scr[hh:hh + 1, :] for hh in heads]
            l_prev = [l_scr[hh:hh + 1, :] for hh in heads]
            acc_prev = [acc_scr[rows[hh], :] for hh in heads]
            m_new, l_new, acc_new = [], [], []
            for hh in heads:
                st = sts[hh]
                if diagonal:
                    key = lax.broadcasted_iota(jnp.int32, (tq, tq), 0)
                    qry = lax.broadcasted_iota(jnp.int32, (tq, tq), 1)
                    st = jnp.where(key <= qry, st, NEG)
                m_new.append(jnp.maximum(m_prev[hh], jnp.max(st, axis=0, keepdims=True)))
                alpha = jnp.exp(m_prev[hh] - m_new[hh])
                pt = jnp.exp(st - m_new[hh])
                l_new.append(alpha * l_prev[hh] + jnp.sum(pt, axis=0, keepdims=True))
                acc_new.append(alpha * acc_prev[hh] + _dot(vt_ref[rows[hh], :], pt, NN))
            for hh in heads:
                m_scr[hh:hh + 1, :] = m_new[hh]
                l_scr[hh:hh + 1, :] = l_new[hh]
                acc_scr[rows[hh], :] = acc_new[hh]

        @pl.when(j < i)
        def _():
            update(False)

        @pl.when(j == i)
        def _():
            update(True)
            lse_ref[...] = jnp.zeros(lse_ref.shape, F32)
            for hh in range(hps):
                rows = slice(64 * hh, 64 * (hh + 1))
                acc_scr[rows, :] = acc_scr[rows, :] / l_scr[hh:hh + 1, :]
                lse_ref[hh:hh + 1, :] = m_scr[hh:hh + 1, :] + jnp.log(l_scr[hh:hh + 1, :])
            o = acc_scr[...].T
            o_ref[...] = o
            y = o * _silu_and_grad(g_ref[...])[0]
            y_ref[...] = y.astype(MXU_DTYPE)
            yt_ref[...] = y.T.astype(MXU_DTYPE)

        if ng:
            @pl.when((pl.program_id(0) == ngrp - 1) & (step == nsteps - 1))
            def _():
                gather_finish()

    qo = lambda w, off=0: pl.BlockSpec((tq, w), lambda p, s, it_, jt_: (it_[s], p + off))
    sds = jax.ShapeDtypeStruct
    comm_scratch = [pltpu.SemaphoreType.DMA((6 * ng,)), pltpu.SemaphoreType.DMA((6 * ng,))] if ng else []
    return pl.pallas_call(
        body,
        name="mla_attn_fwd_gather" if ng else "mla_attn_fwd",
        grid_spec=pltpu.PrefetchScalarGridSpec(
            num_scalar_prefetch=2,
            grid=(ngrp, nsteps),
            in_specs=[qo(wq), pl.BlockSpec((tq, wq), lambda p, s, it_, jt_: (jt_[s], p)),
                      pl.BlockSpec((wv, tq), lambda p, s, it_, jt_: (p, jt_[s])), qo(wv, 512 // wv)] + [HBM_SPEC] * ng,
            out_specs=[qo(wv), qo(wv), pl.BlockSpec((wv, tq), lambda p, s, it_, jt_: (p, it_[s])),
                       pl.BlockSpec((None, 8, tq), lambda p, s, it_, jt_: (p, 0, it_[s]))] + [HBM_SPEC] * ng,
            scratch_shapes=[pltpu.VMEM((8, tq), F32), pltpu.VMEM((8, tq), F32), pltpu.VMEM((wv, tq), F32)] + comm_scratch,
        ),
        out_shape=[sds((t, 512), F32), sds((t, 512), MXU_DTYPE), sds((512, t), MXU_DTYPE), sds((ngrp, 8, t), F32)]
        + _gather_shapes(ws),
        compiler_params=_cparams(("arbitrary", "arbitrary") if ng else ("parallel", "arbitrary")),
    )(it, jt, q, k, vt, z, *ws)


def mla_attn_bwd(q, k, v, do, lse, dd, *, tq=256, chips=None):
    t = q.shape[0]
    nq = t // tq
    it, jt = _causal_tiles(nq, True)
    hps, wq, wv = MLA_HPS, LANES * MLA_HPS, 64 * MLA_HPS
    pcs, layer = chips if chips is not None else ([], 0)
    nc = len(pcs)
    ngrp, nsteps = MLA_HEADS // hps, int(it.shape[0])

    def body(it_ref, jt_ref, q_ref, k_ref, v_ref, do_ref, lse_ref, d_ref, *rest):
        pc_refs, rest = rest[:nc], rest[nc:]
        dq_ref, dk_ref, dv_ref = rest[:3]
        lb_refs, rest = rest[3:3 + nc], rest[3 + nc:]
        dk_scr, dv_scr = rest[:2]
        step = pl.program_id(1)
        i, j = it_ref[step], jt_ref[step]
        masks = _head_masks()
        if nc:
            chips_start, chips_finish = _chips_exchange(pc_refs, lb_refs, rest[2], rest[3], layer)

            @pl.when((pl.program_id(0) == 0) & (step == 0))
            def _():
                chips_start()

        @pl.when(step == 0)
        def _():
            dq_ref[...] = jnp.zeros(dq_ref.shape, F32)

        @pl.when(i == j)
        def _():
            dk_scr[...] = jnp.zeros(dk_scr.shape, F32)
            dv_scr[...] = jnp.zeros(dv_scr.shape, F32)

        def update(diagonal):
            qrows = pl.ds(pl.multiple_of(i * tq, tq), tq)
            heads = range(hps)
            lanes = [slice(LANES * hh, LANES * (hh + 1)) for hh in heads]
            pair = [slice(LANES * (hh // 2), LANES * (hh // 2 + 1)) for hh in heads]
            qh = [q_ref[:, lanes[hh]] for hh in heads]
            kh = [k_ref[:, lanes[hh]] for hh in heads]
            doh = []
            for hh in heads:
                dov = do_ref[:, pair[hh]]
                doh.append(jnp.where(masks[hh % 2], dov, jnp.zeros_like(dov)))
            sts = [_dot(kh[hh], qh[hh], NT) for hh in heads]
            dps = [_dot(v_ref[:, pair[hh]], doh[hh], NT) for hh in heads]
            lse = [lse_ref[hh:hh + 1, :] for hh in heads]
            ddv = [d_ref[hh:hh + 1, :] for hh in heads]
            dk_new = [dk_scr[:, lanes[hh]] for hh in heads]
            dq_new = [dq_ref[qrows, lanes[hh]] for hh in heads]
            dv_new = [dv_scr[:, pair[2 * pp]] for pp in range(hps // 2)]
            for hh in heads:
                st = sts[hh] - lse[hh]
                if diagonal:
                    key = lax.broadcasted_iota(jnp.int32, (tq, tq), 0)
                    qry = lax.broadcasted_iota(jnp.int32, (tq, tq), 1)
                    st = jnp.where(key <= qry, st, NEG)
                pt = jnp.exp(st)
                dst = pt * (dps[hh] - ddv[hh])
                dv_new[hh // 2] = dv_new[hh // 2] + _dot(pt, doh[hh], NN)
                dk_new[hh] = dk_new[hh] + _dot(dst, qh[hh], NN)
                dq_new[hh] = dq_new[hh] + _dot(dst, kh[hh], TN)
            for hh in heads:
                dk_scr[:, lanes[hh]] = dk_new[hh]
                dq_ref[qrows, lanes[hh]] = dq_new[hh]
            for pp in range(hps // 2):
                dv_scr[:, pair[2 * pp]] = dv_new[pp]

        @pl.when(j < i)
        def _():
            update(False)

        @pl.when(j == i)
        def _():
            update(True)

        @pl.when(i == nq - 1)
        def _():
            dk_ref[...] = dk_scr[...]
            dv_ref[...] = dv_scr[...]

        if nc:
            @pl.when((pl.program_id(0) == ngrp - 1) & (step == nsteps - 1))
            def _():
                chips_finish()

    qo = lambda w: pl.BlockSpec((tq, w), lambda p, s, it_, jt_: (it_[s], p))
    kv = lambda w: pl.BlockSpec((tq, w), lambda p, s, it_, jt_: (jt_[s], p))
    stat = pl.BlockSpec((None, 8, tq), lambda p, s, it_, jt_: (p, 0, it_[s]))
    sds = jax.ShapeDtypeStruct
    comm_scratch = [pltpu.SemaphoreType.DMA((3 * nc,)), pltpu.SemaphoreType.DMA((3 * nc,))] if nc else []
    return pl.pallas_call(
        body,
        name="mla_attn_bwd_chips" if nc else "mla_attn_bwd",
        grid_spec=pltpu.PrefetchScalarGridSpec(
            num_scalar_prefetch=2,
            grid=(ngrp, nsteps),
            in_specs=[qo(wq), kv(wq), kv(wv), qo(wv), stat, stat] + [HBM_SPEC] * nc,
            out_specs=[pl.BlockSpec((t, wq), lambda p, s, it_, jt_: (0, p)), kv(wq), kv(wv)] + [HBM_SPEC] * nc,
            scratch_shapes=[pltpu.VMEM((tq, wq), F32), pltpu.VMEM((tq, wv), F32)] + comm_scratch,
        ),
        out_shape=[sds((t, 1024), F32), sds((t, 1024), F32), sds((t, 512), F32)] + _chips_shapes(pcs),
        compiler_params=_cparams(("arbitrary", "arbitrary") if nc else ("parallel", "arbitrary")),
    )(it, jt, q, k, v, do, lse, dd, *pcs)


def mla_gate_bwd(dy, o, z, *, tm=512):
    t = dy.shape[0]
    wv = 64 * MLA_HPS

    def body(dy_ref, o_ref, g_ref, do_ref, dzg_ref, dd_ref):
        dyv, ov = dy_ref[...], o_ref[...]
        silu, dsilu = _silu_and_grad(g_ref[...])
        do = dyv * silu
        do_ref[...] = do.astype(MXU_DTYPE)
        dzg_ref[...] = (dyv * ov * dsilu).astype(MXU_DTYPE)
        prod = do * ov
        row = lax.broadcasted_iota(jnp.int32, (8, wv), 0)
        lane = lax.broadcasted_iota(jnp.int32, (8, wv), 1)
        pick = ((lane >= row * DIL_HD) & (lane < (row + 1) * DIL_HD)).astype(BF16)
        hi = prod.astype(BF16)
        r1 = prod - hi.astype(F32)
        mid = r1.astype(BF16)
        lo = (r1 - mid.astype(F32)).astype(BF16)
        dot = lambda u: lax.dot_general(pick, u, NT, preferred_element_type=F32)
        dd_ref[...] = dot(hi) + dot(mid) + dot(lo)

    blk = lambda off=0: pl.BlockSpec((tm, wv), lambda p, i: (i, p + off))
    sds = jax.ShapeDtypeStruct
    return pl.pallas_call(
        body,
        name="mla_gate_bwd",
        grid=(MLA_HEADS // MLA_HPS, t // tm),
        in_specs=[blk(), blk(), blk(512 // wv)],
        out_specs=[blk(), blk(), pl.BlockSpec((None, 8, tm), lambda p, i: (p, 0, i))],
        out_shape=[sds((t, 512), MXU_DTYPE), sds((t, 512), MXU_DTYPE), sds((MLA_HEADS // MLA_HPS, 8, t), F32)],
        compiler_params=_cparams(("parallel", "parallel")),
    )(dy, o, z)


SPAN = 2048
DIL_SCALE = DIL_HD ** -0.5


def _rm_src(u, window, dil):
    nn, r = divmod(u, dil)
    return pl.ds(nn * window + r, DIL_NK, stride=dil) if dil > 1 else pl.ds(u * DIL_NK, DIL_NK)


def _rm_dst(u):
    return pl.ds(u * DIL_NK, DIL_NK)


def _dil_prep_tile(ops, x, g2, c, s1, s2, scale):
    ms = ops.seg_sum(x * x) * (1.0 / DIL_HD)
    return rope(ops, x * lax.rsqrt(ms + EPS) * g2, c, s1, s2, 32) * scale


def _span_blk(base):
    return pl.BlockSpec((SPAN, LANES), lambda s, p: (s, base + p))


def _const_blk(shape):
    return pl.BlockSpec(shape, lambda s, p: (0,) * len(shape))


_DIL_TABLE_SPECS = [pl.BlockSpec((SPAN, LANES), (lambda s, p, blk=blk: (s, blk))) for blk in (0, 1, 2)]


def dil_tables(tabs, gi):
    window, dil = DIL_GROUPS[gi]
    t = tabs.shape[0]

    def body(c_ref, s1_ref, s2_ref, o_ref):
        for u in range(SPAN // DIL_NK):
            src, dst = _rm_src(u, window, dil), _rm_dst(u)
            for i, ref in enumerate((c_ref, s1_ref, s2_ref)):
                o_ref[dst, LANES * i:LANES * (i + 1)] = ref[src, :]

    return pl.pallas_call(
        body,
        name=f"dil_tables_{dil}",
        grid=(t // SPAN,),
        in_specs=[pl.BlockSpec((SPAN, LANES), (lambda s, blk=blk: (s, blk))) for blk in (3, 4, 5)],
        out_specs=pl.BlockSpec((SPAN, 3 * LANES), lambda s: (s, 0)),
        out_shape=jax.ShapeDtypeStruct((t, 3 * LANES), F32),
        compiler_params=_cparams(("parallel",)),
    )(tabs, tabs, tabs)


def dil_prep_fwd(z, tabs, gq2, gk2, gi):
    window, dil = DIL_GROUPS[gi]
    t = z.shape[0]

    def body(q_ref, k_ref, v_ref, c_ref, s1_ref, s2_ref, gq_ref, gk_ref, qo_ref, ko_ref, vo_ref):
        for u in range(SPAN // DIL_NK):
            src, dst = _rm_src(u, window, dil), _rm_dst(u)
            c, s1, s2 = c_ref[dst, :], s1_ref[dst, :], s2_ref[dst, :]
            qo_ref[dst, :] = _dil_prep_tile(PLAIN, q_ref[src, :], gq_ref[...], c, s1, s2, DIL_SCALE).astype(MXU_DTYPE)
            ko_ref[dst, :] = _dil_prep_tile(PLAIN, k_ref[src, :], gk_ref[...], c, s1, s2, 1.0).astype(MXU_DTYPE)
            vo_ref[dst, :] = v_ref[src, :].astype(MXU_DTYPE)

    return pl.pallas_call(
        body,
        name=f"dil_prep_fwd_{dil}",
        grid=(t // SPAN, 4),
        in_specs=[_span_blk(4 * gi), _span_blk(12 + 4 * gi), _span_blk(24 + 4 * gi)] + _DIL_TABLE_SPECS
        + [_const_blk((1, LANES)), _const_blk((1, LANES))],
        out_specs=[_span_blk(0)] * 3,
        out_shape=[jax.ShapeDtypeStruct((t, 512), MXU_DTYPE)] * 3,
        compiler_params=_cparams(("parallel", "parallel")),
    )(z, z, z, tabs, tabs, tabs, gq2, gk2)


def dil_prep_bwd(z, tabs, gq2, gk2, dq, dk, dv, gi):
    window, dil = DIL_GROUPS[gi]
    t = z.shape[0]

    def body(q_ref, k_ref, c_ref, s1_ref, s2_ref, gq_ref, gk_ref, dq_ref, dk_ref, dv_ref, dzq_ref, dzk_ref, dzv_ref, dgq_ref,
             dgk_ref, sq, sk, sv):
        @pl.when((pl.program_id(0) == 0) & (pl.program_id(1) == 0))
        def _():
            dgq_ref[...] = jnp.zeros((1, LANES), F32)
            dgk_ref[...] = jnp.zeros((1, LANES), F32)

        dgs = [jnp.zeros((1, LANES), F32), jnp.zeros((1, LANES), F32)]
        for u in range(SPAN // DIL_NK):
            src, dst = _rm_src(u, window, dil), _rm_dst(u)
            c, s1, s2 = c_ref[dst, :], s1_ref[dst, :], s2_ref[dst, :]
            for idx, (x_ref, g_ref, ct_ref, scr, scale) in enumerate(((q_ref, gq_ref, dq_ref, sq, DIL_SCALE),
                                                                      (k_ref, gk_ref, dk_ref, sk, 1.0))):
                _, vjp = jax.vjp(lambda xv, gv, sc=scale: _dil_prep_tile(DIFF, xv, gv, c, s1, s2, sc), x_ref[src, :], g_ref[...])
                dx, dg = vjp(ct_ref[dst, :])
                scr[src, :] = dx
                dgs[idx] = dgs[idx] + dg
            sv[src, :] = dv_ref[dst, :]
        dgq_ref[...] += dgs[0] + pltpu.roll(dgs[0], DIL_HD, 1)
        dgk_ref[...] += dgs[1] + pltpu.roll(dgs[1], DIL_HD, 1)
        for c0 in range(0, SPAN, 256):
            rows = slice(c0, c0 + 256)
            dzq_ref[rows, :] = sq[rows, :].astype(MXU_DTYPE)
            dzk_ref[rows, :] = sk[rows, :].astype(MXU_DTYPE)
            dzv_ref[rows, :] = sv[rows, :].astype(MXU_DTYPE)

    sds = jax.ShapeDtypeStruct
    return pl.pallas_call(
        body,
        name=f"dil_prep_bwd_{dil}",
        grid=(t // SPAN, 4),
        in_specs=[_span_blk(4 * gi), _span_blk(12 + 4 * gi)] + _DIL_TABLE_SPECS
        + [_const_blk((1, LANES)), _const_blk((1, LANES)), _span_blk(0), _span_blk(0), _span_blk(0)],
        out_specs=[_span_blk(0)] * 3 + [_const_blk((1, LANES))] * 2,
        out_shape=[sds((t, 512), MXU_DTYPE)] * 3 + [sds((1, LANES), F32)] * 2,
        scratch_shapes=[pltpu.VMEM((SPAN, LANES), F32)] * 3,
        compiler_params=_cparams(("arbitrary", "arbitrary")),
    )(z, z, tabs, tabs, tabs, gq2, gk2, dq, dk, dv)


def _band_masks():
    qi = lax.broadcasted_iota(jnp.int32, (DIL_NK, DIL_NK), 0)
    ki = lax.broadcasted_iota(jnp.int32, (DIL_NK, DIL_NK), 1)
    return (ki >= qi), (ki <= qi)


def _pair_heads():
    return [(pair, hh) for pair in range(4) for hh in range(2)]


def _pair_lanes(pair):
    return slice(LANES * pair, LANES * (pair + 1))


def _zero_other_head(mask, x):
    return jnp.where(mask, x, jnp.zeros_like(x))


def dil_attn_fwd(name, q, k, v, dil):
    t = q.shape[0]

    def body(q_ref, kp_ref, kc_ref, vp_ref, vc_ref, o_ref, lse_ref):
        b = pl.program_id(0)
        mprev, mcur = _band_masks()
        mprev = mprev & (b >= dil)
        hm = _head_masks()
        heads = _pair_heads()
        qh = [_zero_other_head(hm[hh], q_ref[:, _pair_lanes(pair)]) for pair, hh in heads]
        sps = [_dot(qh[i], kp_ref[:, _pair_lanes(pair)], NT) for i, (pair, _) in enumerate(heads)]
        scs = [_dot(qh[i], kc_ref[:, _pair_lanes(pair)], NT) for i, (pair, _) in enumerate(heads)]
        o = [jnp.zeros((DIL_NK, LANES), F32) for _ in range(4)]
        lse = [jnp.zeros((DIL_NK, LANES), F32) for _ in range(4)]
        for i, (pair, hh) in enumerate(heads):
            sp, sc = jnp.where(mprev, sps[i], NEG), jnp.where(mcur, scs[i], NEG)
            m = jnp.maximum(jnp.max(sp, axis=1, keepdims=True), jnp.max(sc, axis=1, keepdims=True))
            ep, ec = jnp.exp(sp - m), jnp.exp(sc - m)
            den = jnp.sum(ep, axis=1, keepdims=True) + jnp.sum(ec, axis=1, keepdims=True)
            lanes = _pair_lanes(pair)
            oh = _dot(ep, _zero_other_head(hm[hh], vp_ref[:, lanes]), NN) + _dot(ec, _zero_other_head(hm[hh], vc_ref[:, lanes]), NN)
            o[pair] = o[pair] + oh / den
            lse[pair] = jnp.where(hm[hh], m + jnp.log(den), lse[pair])
        for pair in range(4):
            o_ref[:, _pair_lanes(pair)] = o[pair]
            lse_ref[:, _pair_lanes(pair)] = lse[pair]

    cur = pl.BlockSpec((DIL_NK, 512), lambda b: (b, 0))
    prev = pl.BlockSpec((DIL_NK, 512), lambda b: (jnp.maximum(b - dil, 0), 0))
    sds = jax.ShapeDtypeStruct
    return pl.pallas_call(
        body,
        name=name,
        grid=(t // DIL_NK,),
        in_specs=[cur, prev, cur, prev, cur],
        out_specs=[cur, cur],
        out_shape=[sds((t, 512), F32), sds((t, 512), F32)],
        compiler_params=_cparams(("parallel",)),
    )(q, k, k, v, v)


def dil_attn_bwd(name, q, k, v, do, lse, dd, dil):
    t = q.shape[0]
    nblk = t // DIL_NK

    def body(q_ref, do_ref, l_ref, d_ref, kp_ref, kc_ref, vp_ref, vc_ref, dq_ref, dk_ref, dv_ref):
        b = pl.program_id(0)

        @pl.when(b == 0)
        def _():
            dk_ref[...] = jnp.zeros(dk_ref.shape, F32)
            dv_ref[...] = jnp.zeros(dv_ref.shape, F32)

        mprev, mcur = _band_masks()
        mprev = mprev & (b >= dil)
        rows_c = pl.ds(pl.multiple_of(b * DIL_NK, DIL_NK), DIL_NK)
        rows_p = pl.ds(pl.multiple_of(jnp.maximum(b - dil, 0) * DIL_NK, DIL_NK), DIL_NK)
        hm = _head_masks()
        heads = _pair_heads()
        ln_ = [_pair_lanes(pair) for pair, _ in heads]
        qh = [_zero_other_head(hm[hh], q_ref[:, ln_[i]]) for i, (_, hh) in enumerate(heads)]
        doh = [_zero_other_head(hm[hh], do_ref[:, ln_[i]]) for i, (_, hh) in enumerate(heads)]
        idx = range(len(heads))
        s_p = [_dot(qh[i], kp_ref[:, ln_[i]], NT) for i in idx]
        s_c = [_dot(qh[i], kc_ref[:, ln_[i]], NT) for i in idx]
        dp_p = [_dot(doh[i], vp_ref[:, ln_[i]], NT) for i in idx]
        dp_c = [_dot(doh[i], vc_ref[:, ln_[i]], NT) for i in idx]
        zero = lambda: [jnp.zeros((DIL_NK, LANES), F32) for _ in range(4)]
        dq, dk_p, dk_c, dv_p, dv_c = zero(), zero(), zero(), zero(), zero()
        for i, (pair, hh) in enumerate(heads):
            lse_h, d_h = _row_scalar(l_ref[:, ln_[i]], hm[hh]), _row_scalar(d_ref[:, ln_[i]], hm[hh])
            pp = jnp.exp(jnp.where(mprev, s_p[i] - lse_h, NEG))
            pc = jnp.exp(jnp.where(mcur, s_c[i] - lse_h, NEG))
            dsp, dsc = pp * (dp_p[i] - d_h), pc * (dp_c[i] - d_h)
            dq[pair] = (dq[pair] + _dot(dsp, _zero_other_head(hm[hh], kp_ref[:, ln_[i]]), NN)
                        + _dot(dsc, _zero_other_head(hm[hh], kc_ref[:, ln_[i]]), NN))
            dv_p[pair] = dv_p[pair] + _dot(pp, doh[i], TN)
            dv_c[pair] = dv_c[pair] + _dot(pc, doh[i], TN)
            dk_p[pair] = dk_p[pair] + _dot(dsp, qh[i], TN)
            dk_c[pair] = dk_c[pair] + _dot(dsc, qh[i], TN)
        for pair in range(4):
            lanes = _pair_lanes(pair)
            dq_ref[:, lanes] = dq[pair]
            dk_ref[rows_p, lanes] += dk_p[pair]
            dv_ref[rows_p, lanes] += dv_p[pair]
            dk_ref[rows_c, lanes] += dk_c[pair]
            dv_ref[rows_c, lanes] += dv_c[pair]

    cur = pl.BlockSpec((DIL_NK, 512), lambda b: (b, 0))
    prev = pl.BlockSpec((DIL_NK, 512), lambda b: (jnp.maximum(b - dil, 0), 0))
    whole = pl.BlockSpec((t, 512), lambda b: (0, 0))
    sds = jax.ShapeDtypeStruct
    return pl.pallas_call(
        body,
        name=name,
        grid=(nblk,),
        in_specs=[cur, cur, cur, cur, prev, cur, prev, cur],
        out_specs=[cur, whole, whole],
        out_shape=[sds((t, 512), F32)] * 3,
        compiler_params=_cparams(("arbitrary",)),
    )(q, do, lse, dd, k, k, v, v)


def dil_combine(os_, lses, z):
    t = z.shape[0]

    def body(o1_ref, l1_ref, o2_ref, l2_ref, o3_ref, l3_ref, g_ref, y_ref, o_ref, lse_ref, yt_ref, so2, sl2, so3, sl3):
        for (window, dil), o_in, l_in, so, sl in ((DIL_GROUPS[1], o2_ref, l2_ref, so2, sl2), (DIL_GROUPS[2], o3_ref, l3_ref, so3, sl3)):
            for u in range(SPAN // DIL_NK):
                src, dst = _rm_src(u, window, dil), _rm_dst(u)
                so[src, :] = o_in[dst, :]
                sl[src, :] = l_in[dst, :]
        for c0 in range(0, SPAN, 256):
            rows = slice(c0, c0 + 256)
            la, lb, lc = l1_ref[rows, :], sl2[rows, :], sl3[rows, :]
            mx = jnp.maximum(jnp.maximum(la, lb), lc)
            wa, wb, wc = jnp.exp(la - mx), jnp.exp(lb - mx), jnp.exp(lc - mx)
            tot = wa + wb + wc
            o = (wa * o1_ref[rows, :] + wb * so2[rows, :] + wc * so3[rows, :]) / tot
            y = o * _silu_and_grad(g_ref[rows, :])[0]
            y_ref[rows, :] = y.astype(MXU_DTYPE)
            o_ref[rows, :] = o
            lse_ref[rows, :] = mx + jnp.log(tot)
            yt_ref[:, rows] = y.T.astype(MXU_DTYPE)

    sds = jax.ShapeDtypeStruct
    return pl.pallas_call(
        body,
        name="dil_combine",
        grid=(t // SPAN, 4),
        in_specs=[_span_blk(0)] * 6 + [_span_blk(36)],
        out_specs=[_span_blk(0)] * 3 + [pl.BlockSpec((LANES, SPAN), lambda s, p: (p, s))],
        out_shape=[sds((t, 512), MXU_DTYPE), sds((t, 512), F32), sds((t, 512), F32), sds((512, t), MXU_DTYPE)],
        scratch_shapes=[pltpu.VMEM((SPAN, LANES), F32)] * 4,
        compiler_params=_cparams(("parallel", "parallel")),
    )(os_[0], lses[0], os_[1], lses[1], os_[2], lses[2], z)


def dil_gate_bwd(dy, o, z, lse):
    t = dy.shape[0]

    def body(dy_ref, o_ref, g_ref, lse_ref, dzg_ref, do1_ref, dd1_ref, do2_ref, dd2_ref, l2_ref, do3_ref, dd3_ref, l3_ref, do_scr):
        for c0 in range(0, SPAN, 256):
            rows = slice(c0, c0 + 256)
            dyv, ov = dy_ref[rows, :], o_ref[rows, :]
            silu, dsilu = _silu_and_grad(g_ref[rows, :])
            do = dyv * silu
            do_scr[rows, :] = do
            do1_ref[rows, :] = do.astype(MXU_DTYPE)
            dd1_ref[rows, :] = _seg_sum_impl(do * ov)
            dzg_ref[rows, :] = (dyv * ov * dsilu).astype(MXU_DTYPE)
        for (window, dil), do_o, dd_o, l_o in ((DIL_GROUPS[1], do2_ref, dd2_ref, l2_ref), (DIL_GROUPS[2], do3_ref, dd3_ref, l3_ref)):
            for u in range(SPAN // DIL_NK):
                src, dst = _rm_src(u, window, dil), _rm_dst(u)
                do_o[dst, :] = do_scr[src, :].astype(MXU_DTYPE)
                dd_o[dst, :] = dd1_ref[src, :]
                l_o[dst, :] = lse_ref[src, :]

    sds = jax.ShapeDtypeStruct
    f32, mxu = sds((t, 512), F32), sds((t, 512), MXU_DTYPE)
    return pl.pallas_call(
        body,
        name="dil_gate_bwd",
        grid=(t // SPAN, 4),
        in_specs=[_span_blk(0), _span_blk(0), _span_blk(36), _span_blk(0)],
        out_specs=[_span_blk(0)] * 9,
        out_shape=[mxu, mxu, f32, mxu, f32, f32, mxu, f32, f32],
        scratch_shapes=[pltpu.VMEM((SPAN, LANES), F32)],
        compiler_params=_cparams(("parallel", "parallel")),
    )(dy, o, z, lse)


def _merge_tile(p0, p1, p2, z0, z1, z2, b0, b1, b2):
    return _sigmoid(z0 + b0) * p0 + _sigmoid(z1 + b1) * p1 + _sigmoid(z2 + b2) * p2


def _merge_ins(ps, z, b):
    w = 256
    return ([_row(p, w, 0, True) for p in ps] + [_row(z, w, 4 * i, True) for i in range(3)]
            + [_const(b, w, 4 * i, True) for i in range(3)])


def merge_fwd(ps, z, b):
    def fn(ins, outs, _):
        merged = _merge_tile(*[r[...] for r in ins])
        outs[0][...] = merged.astype(MXU_DTYPE)
        outs[1][...] = merged.T.astype(MXU_DTYPE)

    return rowwise("merge_fwd", fn, z.shape[0], tm=1024, ncol=4, ins=_merge_ins(ps, z, b), outs=[(1024, 256, MXU_DTYPE, True)],
                   touts=[(1024, 256, MXU_DTYPE, True)])


def merge_bwd(dm, ps, z, b):
    def fn(ins, outs, accs):
        _, vjp = jax.vjp(_merge_tile, *[r[...] for r in ins[:9]])
        grads = vjp(ins[9][...])
        for i in range(3):
            outs[i][...] = grads[i].astype(MXU_DTYPE)
            outs[3 + i][...] = grads[3 + i].astype(MXU_DTYPE)
            accs[i][...] += grads[6 + i]

    return rowwise("merge_bwd", fn, z.shape[0], tm=1024, ncol=4, ins=_merge_ins(ps, z, b) + [_row(dm, 256, 0, True)],
                   outs=[(1024, 256, MXU_DTYPE, True)] * 6, accs=[(1, 1024, 256, True)] * 3)


EW_BLOCK_BYTES = 2**21


def _tile2d(r, c):
    if r * c * 4 <= EW_BLOCK_BYTES:
        return r, c
    for tr in (512, 256, 128, 64, 32):
        if r % tr == 0 and tr * c * 4 <= EW_BLOCK_BYTES:
            return tr, c
    for tc in (1024, 512, 256, 128):
        if c % tc == 0 and r * tc * 4 <= EW_BLOCK_BYTES:
            return r, tc
    raise ValueError((r, c))


def adamw(name, w, g, m, v):
    shape = w.shape
    c = shape[-1]
    r = w.size // c
    tr, tc = _tile2d(r, c)
    assert tc == c
    c1, c2 = 1.0 - ADAM_B1 ** ADAM_STEP, 1.0 - ADAM_B2 ** ADAM_STEP

    def body(w_ref, g_ref, m_ref, v_ref, d_ref, mo_ref, vo_ref):
        gv = g_ref[...]
        mn = ADAM_B1 * m_ref[...] + (1.0 - ADAM_B1) * gv
        vn = ADAM_B2 * v_ref[...] + (1.0 - ADAM_B2) * (gv * gv)
        d_ref[...] = -ADAM_LR * ((mn / c1) / (jnp.sqrt(vn / c2) + ADAM_EPS) + ADAM_WD * w_ref[...])
        mo_ref[...] = mn
        vo_ref[...] = vn

    spec = pl.BlockSpec((tr, c), lambda i: (i, 0))
    outs = pl.pallas_call(
        body,
        name=name,
        grid=(r // tr,),
        in_specs=[spec] * 4,
        out_specs=[spec] * 3,
        out_shape=[jax.ShapeDtypeStruct((r, c), F32)] * 3,
        compiler_params=_cparams(("parallel",)),
    )(*[a.reshape(r, c) for a in (w, g, m, v)])
    return [o.reshape(shape) for o in outs]


def adamw_pair(name, w, reduced, received, m, v, c_idx):
    _, r, c = w.shape
    tr, tc = _tile2d(r, c)
    c1, c2 = 1.0 - ADAM_B1 ** ADAM_STEP, 1.0 - ADAM_B2 ** ADAM_STEP

    def body(c_ref, w_ref, r0_ref, x0_ref, r1_ref, x1_ref, m_ref, v_ref, g_ref, d_ref, mo_ref, vo_ref):
        mine = c_ref[0]
        gv = jnp.where(pl.program_id(0) == 0, jnp.where(mine == 0, r0_ref[...], x0_ref[...]),
                       jnp.where(mine == 1, r1_ref[...], x1_ref[...]))
        mn = ADAM_B1 * m_ref[...] + (1.0 - ADAM_B1) * gv
        vn = ADAM_B2 * v_ref[...] + (1.0 - ADAM_B2) * (gv * gv)
        g_ref[...] = gv
        d_ref[...] = -ADAM_LR * ((mn / c1) / (jnp.sqrt(vn / c2) + ADAM_EPS) + ADAM_WD * w_ref[...])
        mo_ref[...] = mn
        vo_ref[...] = vn

    full = pl.BlockSpec((None, tr, tc), lambda h, i, j, cr: (h, i, j))
    half = pl.BlockSpec((tr, tc), lambda h, i, j, cr: (i, j))
    return pl.pallas_call(
        body,
        name=name,
        grid_spec=pltpu.PrefetchScalarGridSpec(
            num_scalar_prefetch=1,
            grid=(2, r // tr, c // tc),
            in_specs=[full, half, half, half, half, full, full],
            out_specs=[full] * 4,
        ),
        out_shape=[jax.ShapeDtypeStruct((2, r, c), F32)] * 4,
        compiler_params=_cparams(("parallel", "parallel", "parallel")),
    )(c_idx, w, reduced[0], received[0], reduced[1], received[1], m, v)


def sum_pair(name, g, la, out_dtype):
    _, r, c = g.shape
    tr, tc = _tile2d(r, c)

    def body(g_ref, la_ref, o_ref):
        o_ref[...] = (g_ref[...] + la_ref[...]).astype(o_ref.dtype)

    spec = pl.BlockSpec((None, tr, tc), lambda s, i, j: (s, i, j))
    return pl.pallas_call(
        body,
        name=name,
        grid=(4, r // tr, c // tc),
        in_specs=[spec, spec],
        out_specs=spec,
        out_shape=jax.ShapeDtypeStruct((4, r, c), out_dtype),
        compiler_params=_cparams(("parallel", "parallel", "parallel")),
    )(g, la)


def sum_chips(name, g, la, lb, s_idx):
    _, r, c = g.shape
    tr, tc = _tile2d(r, c)

    def body(s_ref, g_ref, la_ref, l0_ref, l1_ref, l2_ref, o_ref):
        own = g_ref[...] + la_ref[...]
        o_ref[...] = ((own + l0_ref[...].astype(F32)) + l1_ref[...].astype(F32)) + l2_ref[...].astype(F32)

    own_spec = pl.BlockSpec((None, tr, tc), lambda i, j, sr: (sr[0], i, j))
    lspec = lambda k: pl.BlockSpec((None, tr, tc), lambda i, j, sr: (k, i, j))
    return pl.pallas_call(
        body,
        name=name,
        grid_spec=pltpu.PrefetchScalarGridSpec(
            num_scalar_prefetch=1,
            grid=(r // tr, c // tc),
            in_specs=[own_spec, own_spec, lspec(0), lspec(1), lspec(2)],
            out_specs=pl.BlockSpec((tr, tc), lambda i, j, sr: (i, j)),
        ),
        out_shape=jax.ShapeDtypeStruct((r, c), F32),
        compiler_params=_cparams(("parallel", "parallel")),
    )(s_idx, g, la, lb, lb, lb)


def _place():
    x, y, c = lax.axis_index("x"), lax.axis_index("y"), lax.axis_index("c")
    chips = [(1 - x, y), (x, 1 - y), (1 - x, 1 - y)]
    return x, y, c, chips


HBM_SPEC = pl.BlockSpec(memory_space=pltpu.HBM)


def _comm_call(name, body, ins, out_shapes, n_sem, n_local):
    return pl.pallas_call(
        body,
        name=name,
        in_specs=[HBM_SPEC] * len(ins),
        out_specs=[HBM_SPEC] * len(out_shapes),
        out_shape=out_shapes,
        scratch_shapes=[pltpu.SemaphoreType.DMA((n_sem,)), pltpu.SemaphoreType.DMA((n_sem,)),
                        pltpu.SemaphoreType.DMA((max(n_local, 1),))],
    )(*ins)


def _layer_gather(ins, outs, send, recv, layer):
    n = len(ins)
    x, y, c, chips = _place()
    s = 2 * x + y
    sib = (x, y, 1 - c)
    active = c == layer

    def rc(a, k, src, dst, dev):
        return pltpu.make_async_remote_copy(src_ref=src, dst_ref=dst, send_sem=send.at[6 * a + k], recv_sem=recv.at[6 * a + k],
                                            device_id=dev, device_id_type=MESH)

    def first_hop():
        return [rc(a, j, ins[a], outs[a].at[s], (*chip, c)) for j, chip in enumerate(chips) for a in range(n)]

    def start():
        @pl.when(active)
        def _():
            for cp in first_hop():
                cp.start()

    def finish():
        @pl.when(active)
        def _():
            forwards = []
            for j, (cx, cy) in enumerate(chips):
                for a in range(n):
                    landed = outs[a].at[2 * cx + cy]
                    rc(a, j, landed, landed, sib).wait_recv()
                    forwards.append(rc(a, 3 + j, landed, landed, sib))
                    forwards[-1].start()
            for cp in first_hop() + forwards:
                cp.wait_send()

        @pl.when(jnp.logical_not(active))
        def _():
            for j, (cx, cy) in enumerate(chips):
                for a in range(n):
                    other = outs[a].at[2 * cx + cy]
                    rc(a, 3 + j, other, other, sib).wait_recv()

    return start, finish


def _gather_shapes(ws):
    return [jax.ShapeDtypeStruct((4,) + w.shape, w.dtype) for w in ws]


def gather_hook(ws, layer):
    return (ws, _gather_shapes(ws), 6 * len(ws), lambda i, o, s, r: _layer_gather(i, o, s, r, layer))


def allgather_layer(ws, layer):
    n = len(ws)

    def body(*refs):
        send, recv, _ = refs[2 * n:]
        start, finish = _layer_gather(refs[:n], refs[n:2 * n], send, recv, layer)
        start()
        finish()

    return _comm_call(f"allgather_layer{layer}", body, ws, _gather_shapes(ws), 6 * n, 0)


def exchange_sibling(gs, layer, tag=""):
    n = len(gs)

    def body(*refs):
        send, recv, _ = refs[2 * n:]
        start, finish = _sibling_exchange(refs[:n], refs[n:2 * n], send, recv, layer)
        start()
        finish()

    return _comm_call(f"exchange_sibling{layer}{tag}", body, gs, [jax.ShapeDtypeStruct(g.shape, g.dtype) for g in gs], n, 0)


def _sibling_exchange(ins, outs, send, recv, layer):
    x, y, c, _ = _place()

    def copies():
        return [pltpu.make_async_remote_copy(src_ref=ins[a], dst_ref=outs[a], send_sem=send.at[a], recv_sem=recv.at[a],
                                             device_id=(x, y, 1 - c), device_id_type=MESH) for a in range(len(ins))]

    def start():
        @pl.when(c != layer)
        def _():
            for cp in copies():
                cp.start()

    def finish():
        @pl.when(c != layer)
        def _():
            for cp in copies():
                cp.wait_send()

        @pl.when(c == layer)
        def _():
            for cp in copies():
                cp.wait_recv()

    return start, finish


def sibling_hook(gs, layer):
    return (gs, [jax.ShapeDtypeStruct(g.shape, g.dtype) for g in gs], len(gs),
            lambda i, o, s, r: _sibling_exchange(i, o, s, r, layer))


def chips_hook(ps, layer):
    return (ps, _chips_shapes(ps), 3 * len(ps), lambda i, o, s, r: _chips_exchange(i, o, s, r, layer))


def _chips_exchange(ins, outs, send, recv, layer):
    n = len(ins)
    _, _, c, chips = _place()

    def copies():
        return [pltpu.make_async_remote_copy(src_ref=ins[a].at[2 * cx + cy], dst_ref=outs[a].at[j], send_sem=send.at[3 * a + j],
                                             recv_sem=recv.at[3 * a + j], device_id=(cx, cy, c), device_id_type=MESH)
                for j, (cx, cy) in enumerate(chips) for a in range(n)]

    def start():
        @pl.when(c == layer)
        def _():
            for cp in copies():
                cp.start()

    def finish():
        @pl.when(c == layer)
        def _():
            for cp in copies():
                cp.wait()

    return start, finish


def _chips_shapes(ps):
    return [jax.ShapeDtypeStruct((3,) + p.shape[1:], p.dtype) for p in ps]


def exchange_chips(ps, layer, tag=""):
    n = len(ps)

    def body(*refs):
        send, recv, _ = refs[2 * n:]
        start, finish = _chips_exchange(refs[:n], refs[n:2 * n], send, recv, layer)
        start()
        finish()

    return _comm_call(f"exchange_chips{layer}{tag}", body, ps, _chips_shapes(ps), 3 * n, 0)


def exchange_final(rs, layer, small=None):
    n = len(rs)
    ins_all = list(rs) + ([small] if small is not None else [])

    def body(*refs):
        ni = len(ins_all)
        ins, outs = refs[:ni], refs[ni:2 * ni]
        send, recv, lsem = refs[2 * ni:]
        x, y, c, _ = _place()
        s = 2 * x + y
        to_sib = [pltpu.make_async_remote_copy(src_ref=ins[a], dst_ref=outs[a], send_sem=send.at[a], recv_sem=recv.at[a],
                                               device_id=(x, y, 1 - c), device_id_type=MESH) for a in range(n)]
        rel = [(fx, fy) for fx in (0, 1) for fy in (0, 1)]

        def piece(fx, fy, cc, dst_slot):
            k = n + 2 * (2 * fx + fy) + cc
            return pltpu.make_async_remote_copy(src_ref=ins[n], dst_ref=outs[n].at[dst_slot], send_sem=send.at[k], recv_sem=recv.at[k],
                                                device_id=(x ^ fx, y ^ fy, cc), device_id_type=MESH)

        @pl.when(c == layer)
        def _():
            for cp in to_sib:
                cp.start()
            if small is not None:
                own = pltpu.make_async_copy(ins[n], outs[n].at[s], lsem.at[0])
                own.start()
                sends = [piece(fx, fy, cc, s) for fx, fy in rel for cc in (0, 1) if (fx, fy) != (0, 0) or cc != layer]
                for cp in sends:
                    cp.start()
                for fx, fy in rel[1:]:
                    piece(fx, fy, layer, 2 * (x ^ fx) + (y ^ fy)).wait_recv()
                for cp in sends:
                    cp.wait_send()
                own.wait()
            for cp in to_sib:
                cp.wait_send()

        @pl.when(c != layer)
        def _():
            for cp in to_sib:
                cp.wait_recv()
            if small is not None:
                for fx, fy in rel:
                    piece(fx, fy, 1 - layer, 2 * (x ^ fx) + (y ^ fy)).wait_recv()

    out_shapes = [jax.ShapeDtypeStruct(r.shape, r.dtype) for r in rs]
    if small is not None:
        out_shapes.append(jax.ShapeDtypeStruct((4,) + small.shape, small.dtype))
    res = _comm_call(f"exchange_final{layer}", body, ins_all, out_shapes, n + 8, 1)
    return (res[:n], res[n]) if small is not None else (res[:n], None)


def _pad_in_cols(w):
    z = lambda n: jnp.zeros(w.shape[:-1] + (n,), w.dtype)
    return jnp.concatenate([w[..., :KR_OFF], z(64), w[..., KR_OFF:KR_OFF + MLA_ROPE], z(32), w[..., KR_OFF + MLA_ROPE:]], axis=-1)


def _pad_heads(w, real):
    k = w.shape[0]
    return jnp.pad(w.reshape(k, MLA_HEADS, real), ((0, 0), (0, 0), (0, LANES - real))).reshape(k, MLA_HEADS * LANES)


def _pad_gain(g, real):
    return jnp.pad(g.reshape(1, real), ((0, 0), (0, LANES - real)))


def layer_weights(full):
    w_ukv = full["w_ukv"].reshape(128, MLA_HEADS, 2, 64)
    two = lambda g: jnp.concatenate([g, g]).reshape(1, LANES)
    return dict(
        norm_g=full["norm_g"].reshape(1, -1), w_in=full["w_in"], conv_w=full["conv_w"], conv_b=full["conv_b"].reshape(1, -1),
        wgx=full["w_gate_x"], bgx=full["b_gate_x"].reshape(LRU_BLOCKS, 1, LANES),
        wga=full["w_gate_a"], bga=full["b_gate_a"].reshape(LRU_BLOCKS, 1, LANES),
        lam=full["lru_lambda"].reshape(1, -1), w_lru_o=full["w_lru_o"],
        cq_norm_g=full["cq_norm_g"].reshape(1, -1), ckv_norm_g=full["ckv_norm_g"].reshape(1, -1),
        wq=_pad_heads(full["w_uq"], MLA_QK), wk=_pad_heads(w_ukv[:, :, 0].reshape(128, 512), 64),
        wv=w_ukv[:, :, 1].reshape(128, 512),
        gq=_pad_gain(full["mla_q_norm_g"], MLA_QK), gk=_pad_gain(full["mla_k_norm_g"], MLA_QK),
        w_mla_o=full["w_mla_o"], gq2=two(full["dil_q_norm_g"]), gk2=two(full["dil_k_norm_g"]),
        w_dil_o=full["w_dil_o"], b_merge=full["b_merge"].reshape(1, -1), w_out=full["w_out"],
    )


def layer_fwd(x, w, tabs, gather=None):
    tabs, dtabs = tabs
    h, ht = rmsnorm_fwd(x, w["norm_g"])
    z_lru = mm_nn("in_proj_lru", h, w["w_in"], n=2048, n_off=G_LRU, tm=1024)
    z_mla = mm_nn("in_proj_mla", h, w["w_in"], n=1024, n_off=G_MLA, tm=1024)
    z_dil = mm_nn("in_proj_dil", h, w["w_in"], n=5120, n_off=G_DIL, tm=1024)
    z_mrg = mm_nn("in_proj_mrg", h, w["w_in"], n=3072, n_off=G_MRG, tm=1024)
    rest_hook = None if gather is None else gather_hook(gather[0][1:], gather[1])
    hs, y_lru, yt_lru, *ga_rest = lru_fwd(z_lru, w["conv_w"], w["conv_b"], w["wgx"], w["bgx"], w["wga"], w["bga"], w["lam"],
                                          comm=rest_hook)
    qm, km, vm, vtm = mla_prep_fwd(z_mla, tabs, w)
    o_mla, y_mla, yt_mla, lse_mla, *ga_first = mla_attn_fwd(qm, km, vtm, z_mla,
                                                            gather=None if gather is None else (gather[0][:1], gather[1]))
    gathered = ga_first + ga_rest
    os_, lses, dil_rm = [], [], []
    for gi, (_, dil) in enumerate(DIL_GROUPS):
        qkv = dil_prep_fwd(z_dil, dtabs[gi], w["gq2"], w["gk2"], gi)
        o, lse = dil_attn_fwd(f"dil_attn_fwd_{dil}", *qkv, dil)
        os_.append(o)
        lses.append(lse)
        dil_rm.append(qkv)
    y_dil, o_dil, lse_dil, yt_dil = dil_combine(os_, lses, z_dil)
    ps = [mm_nn("proj_lru", y_lru, w["w_lru_o"]), mm_nn("proj_mla", y_mla, w["w_mla_o"]), mm_nn("proj_dil", y_dil, w["w_dil_o"])]
    merged, merged_t = merge_fwd(ps, z_mrg, w["b_merge"])
    out = mm_nn("out_proj", merged, w["w_out"], add=x)
    res = dict(x=x, ht=ht, z_lru=z_lru, z_mla=z_mla, z_dil=z_dil, z_mrg=z_mrg, hs=hs, yt_lru=yt_lru, qm=qm, km=km, vm=vm, o_mla=o_mla,
               yt_mla=yt_mla, lse_mla=lse_mla, dil_rm=dil_rm, yt_dil=yt_dil, o_dil=o_dil, lse_dil=lse_dil, ps=ps, merged_t=merged_t)
    return out, res, gathered


def layer_bwd(dout, r, w, tabs, prev=None, own=None):
    tabs, dtabs = tabs
    g = {}
    dmerged = mm_nt("out_proj_dx", dout, w["w_out"])
    g["w_out"] = mm_nn("out_proj_dw", r["merged_t"], dout, tm=1024, tn=512, tk=2048)
    dp0, dp1, dp2, dzm0, dzm1, dzm2, db0, db1, db2 = merge_bwd(dmerged, r["ps"], r["z_mrg"], w["b_merge"])
    g["b_merge"] = jnp.concatenate([db0, db1, db2], axis=1).reshape(-1)
    dy_lru = mm_nt("proj_lru_dx", dp0, w["w_lru_o"])
    dy_mla = mm_nt("proj_mla_dx", dp1, w["w_mla_o"])
    dy_dil = mm_nt("proj_dil_dx", dp2, w["w_dil_o"])
    g["w_lru_o"] = mm_nn("proj_lru_dw", r["yt_lru"], dp0, **DW_TILES)
    g["w_mla_o"] = mm_nn("proj_mla_dw", r["yt_mla"], dp1, **DW_TILES)
    g["w_dil_o"] = mm_nn("proj_dil_dw", r["yt_dil"], dp2, **DW_TILES)
    dzx, dzg_lru, dwgx, dbgx, dwga, dbga, dlam, dcw, dcb, *la_prev = lru_bwd(
        r["z_lru"], r["hs"], dy_lru, w["conv_w"], w["conv_b"], w["wgx"], w["bgx"], w["wga"], w["bga"], w["lam"],
        comm=None if prev is None else sibling_hook(prev[0], prev[1]))
    chips = None if prev is None else (prev[2](la_prev), prev[1])
    g.update(w_gate_x=dwgx, b_gate_x=dbgx.reshape(LRU_BLOCKS, LANES), w_gate_a=dwga, b_gate_a=dbga.reshape(LRU_BLOCKS, LANES),
             lru_lambda=dlam.reshape(-1), conv_w=dcw, conv_b=dcb.reshape(-1))
    do_m, dzg_mla, dd_m = mla_gate_bwd(dy_mla, r["o_mla"], r["z_mla"])
    dq_m, dk_m, dv_m, *lb_prev = mla_attn_bwd(r["qm"], r["km"], r["vm"], do_m, r["lse_mla"], dd_m, chips=chips)
    dz_mla3, dg_cq, dg_ckv, dwq, dwk, dwv, dgq, dgk = mla_prep_bwd(r["z_mla"], tabs, w, dq_m, dk_m, dv_m)
    g.update(cq_norm_g=dg_cq.reshape(-1), ckv_norm_g=dg_ckv.reshape(-1), mla_q_norm_g=dgq[0, :MLA_QK], mla_k_norm_g=dgk[0, :MLA_QK])
    g["w_uq"] = dwq.reshape(256, MLA_HEADS, LANES)[:, :, :MLA_QK].reshape(256, MLA_HEADS * MLA_QK)
    g["w_ukv"] = jnp.concatenate([dwk.reshape(128, MLA_HEADS, LANES)[:, :, :64], dwv.reshape(128, MLA_HEADS, 64)], axis=2).reshape(128, 1024)
    dzg_dil, do1, dd1, do2, dd2, l2, do3, dd3, l3 = dil_gate_bwd(dy_dil, r["o_dil"], r["z_dil"], r["lse_dil"])
    stats = [(do1, r["lse_dil"], dd1), (do2, l2, dd2), (do3, l3, dd3)]
    dzq, dzk, dzv, dgq2, dgk2 = [], [], [], [], []
    for gi, (_, dil) in enumerate(DIL_GROUPS):
        dq, dk, dv = dil_attn_bwd(f"dil_attn_bwd_{dil}", *r["dil_rm"][gi], *stats[gi], dil)
        parts = dil_prep_bwd(r["z_dil"], dtabs[gi], w["gq2"], w["gk2"], dq, dk, dv, gi)
        for acc, part in zip((dzq, dzk, dzv, dgq2, dgk2), parts):
            acc.append(part)
    g.update(dil_q_norm_g=sum(dgq2)[0, :DIL_HD], dil_k_norm_g=sum(dgk2)[0, :DIL_HD])
    dz = jnp.concatenate([dzx, dzg_lru, dz_mla3, dzg_mla] + dzq + dzk + dzv + [dzg_dil, dzm0, dzm1, dzm2], axis=1)
    g["w_in"] = in_proj_dw(r["ht"], dz)
    own_hook = None if own is None else chips_hook(*own(g))
    out = mm_nt("in_proj_dx", dz, w["w_in"], tm=1024, tn=1024, tk=IN_PAD // 4, comm=own_hook)
    dh, lb_own = (out, []) if own_hook is None else (out[0], out[1:])
    dx, dng = rmsnorm_bwd(r["x"], dh, dout, w["norm_g"])
    g["norm_g"] = dng.reshape(-1)
    return dx, g, dict(la_prev=la_prev, lb_prev=lb_prev, lb_own=lb_own)


def local_step(x, positions, target, full0, full1=None, gather1=None, reduce=None):
    tabs = rope_tables(positions.reshape(-1, 1))
    tabs = (tabs, [dil_tables(tabs, gi) for gi in range(len(DIL_GROUPS))])
    ws, ress = [], []
    for l in range(2):
        if l == 0:
            ws.append(layer_weights(full0))
            x, res, gathered = layer_fwd(x, ws[0], tabs, gather=None if gather1 is None else (gather1[0], 1))
        else:
            ws.append(layer_weights(full1 if gather1 is None else gather1[1](gathered)))
            x, res, _ = layer_fwd(x, ws[1], tabs)
        ress.append(res)
    dy, loss = loss_head(x, target)
    dy, grads1, _ = layer_bwd(dy, ress[1], ws[1], tabs)
    prev1, own0 = (None, None) if reduce is None else (reduce[0](grads1), reduce[1])
    dy, grads0, landed = layer_bwd(dy, ress[0], ws[0], tabs, prev=prev1, own=own0)
    return loss, dy, [grads0, grads1], landed


WEIGHTS = ["norm_g", "w_in", "conv_w", "conv_b", "w_gate_x", "b_gate_x", "w_gate_a", "b_gate_a", "lru_lambda", "w_lru_o", "cq_norm_g",
           "ckv_norm_g", "w_uq", "w_ukv", "mla_q_norm_g", "mla_k_norm_g", "w_mla_o", "dil_q_norm_g", "dil_k_norm_g", "w_dil_o", "b_merge",
           "w_out"]
SHARDED = {"w_in": 2, "conv_w": 2, "w_lru_o": 1, "w_uq": 2, "w_ukv": 2, "w_mla_o": 2, "w_dil_o": 2, "w_out": 1}
REPLICATED = [n for n in WEIGHTS if n not in SHARDED]
SMALL_ROWS = 144


def kernel(x, positions, norm_g, w_in, conv_w, conv_b, w_gate_x, b_gate_x, w_gate_a, b_gate_a, lru_lambda, w_lru_o, cq_norm_g, ckv_norm_g, w_uq, w_ukv, mla_q_norm_g, mla_k_norm_g, w_mla_o, dil_q_norm_g, dil_k_norm_g, w_dil_o, b_merge, w_out, loss_target, m_norm_g, m_w_in, m_conv_w, m_conv_b, m_w_gate_x, m_b_gate_x, m_w_gate_a, m_b_gate_a, m_lru_lambda, m_w_lru_o, m_cq_norm_g, m_ckv_norm_g, m_w_uq, m_w_ukv, m_mla_q_norm_g, m_mla_k_norm_g, m_w_mla_o, m_dil_q_norm_g, m_dil_k_norm_g, m_w_dil_o, m_b_merge, m_w_out, v_norm_g, v_w_in, v_conv_w, v_conv_b, v_w_gate_x, v_b_gate_x, v_w_gate_a, v_b_gate_a, v_lru_lambda, v_w_lru_o, v_cq_norm_g, v_ckv_norm_g, v_w_uq, v_w_ukv, v_mla_q_norm_g, v_mla_k_norm_g, v_w_mla_o, v_dil_q_norm_g, v_dil_k_norm_g, v_w_dil_o, v_b_merge, v_w_out):
    args = locals()
    w = {n: args[n] for n in WEIGHTS}
    m = {n: args["m_" + n] for n in WEIGHTS}
    v = {n: args["v_" + n] for n in WEIGHTS}
    my_c = lax.axis_index("c").astype(jnp.int32)
    my_s = (2 * lax.axis_index("x") + lax.axis_index("y")).astype(jnp.int32)
    c_idx = my_c.reshape(1)
    s_idx = my_s.reshape(1)

    names = list(SHARDED)
    wire = [[w[n][l] if n == "conv_w" else w[n][l].astype(BF16) for n in names] for l in range(2)]

    def assemble(l, gathered):
        full = {n: w[n][l] for n in REPLICATED}
        for n, ga, own in zip(names, gathered, wire[l]):
            parts = [jnp.where(my_s == s, own, ga[s]) for s in range(4)]
            if n == "w_in":
                parts = [_pad_in_cols(parts[0])] + parts[1:]
            full[n] = jnp.concatenate(parts, axis=SHARDED[n] - 1)
        return full

    full0 = assemble(0, allgather_layer(wire[0], 0))

    def shards_of(grads_l):
        gs = []
        for n in names:
            g_ = grads_l[n]
            if n == "w_in":
                gs.append(g_.reshape(4, IN_WIDTH // 4, D_MODEL))
            else:
                parts = jnp.stack(jnp.split(g_, 4, axis=SHARDED[n] - 1))
                gs.append(parts.reshape(4, -1, parts.shape[-1]))
        return gs

    def pair_sums(l, gs, la, tags):
        dts = [BF16 if g_.size >= 2**19 and t != "small" else F32 for g_, t in zip(gs, tags)]
        return [sum_pair(f"sum_pair{l}_{t}", g_, l_, dt) for t, g_, l_, dt in zip(tags, gs, la, dts)]

    def chip_sums(l, gs, la, lb, tags):
        return [sum_chips(f"sum_chips{l}_{t}", g_, l_, b_, s_idx) for t, g_, l_, b_ in zip(tags, gs, la, lb)]

    stash = {}

    def prev1(grads1):
        stash["gs1"] = shards_of(grads1)
        return stash["gs1"], 1, lambda la: pair_sums(1, stash["gs1"], la, names)

    def own0(grads0):
        stash["gs0"] = shards_of(grads0)
        stash["la0"] = exchange_sibling(stash["gs0"], 0)
        return pair_sums(0, stash["gs0"], stash["la0"], names), 0

    loss, grad_x, grads, landed = local_step(x[0], positions[0], loss_target[0], full0,
                                             gather1=(wire[1], functools.partial(assemble, 1)), reduce=(prev1, own0))
    loss = lax.psum(loss, ("x", "y", "c"))
    reduced1 = chip_sums(1, stash["gs1"], landed["la_prev"], landed["lb_prev"], names)
    received1, _ = exchange_final(reduced1, 1)
    reduced0 = chip_sums(0, stash["gs0"], stash["la0"], landed["lb_own"], names)

    flat = jnp.concatenate([jnp.stack([grads[0][n], grads[1][n]]).reshape(-1) for n in REPLICATED])
    small = [jnp.pad(flat, (0, 4 * SMALL_ROWS * 1024 - flat.size)).reshape(4, SMALL_ROWS, 1024)]
    la_s = exchange_sibling(small, 0, tag="_small")
    lb_s = exchange_chips(pair_sums(0, small, la_s, ["small"]), 0, tag="_small")
    received0, small_all = exchange_final(reduced0, 0, small=chip_sums(0, small, la_s, lb_s, ["small"])[0])

    g_local, delta, new_m, new_v = {}, {}, {}, {}
    for i, n in enumerate(names):
        shp = w[n].shape
        if n == "w_in":
            as3 = lambda a: a.transpose(0, 2, 1)
            back = lambda o: o.transpose(0, 2, 1)
        else:
            as3 = lambda a, rc=reduced0[i].shape: a.reshape((2,) + rc)
            back = lambda o, shp=shp: o.reshape(shp)
        outs = adamw_pair(f"adamw_{n}", as3(w[n]), (reduced0[i], reduced1[i]), (received0[i], received1[i]), as3(m[n]), as3(v[n]),
                          c_idx)
        g_local[n], delta[n], new_m[n], new_v[n] = [back(o) for o in outs]
    flat = small_all.reshape(-1)
    off = 0
    for n in REPLICATED:
        g_local[n] = flat[off:off + w[n].size].reshape(w[n].shape)
        off += w[n].size
        delta[n], new_m[n], new_v[n] = adamw(f"adamw_{n}", w[n], g_local[n], m[n], v[n])
    return (loss, grad_x[None], *[g_local[n] for n in WEIGHTS], *[delta[n] for n in WEIGHTS], *[new_m[n] for n in WEIGHTS],
            *[new_v[n] for n in WEIGHTS])
```

```python
import functools

import jax
import jax.numpy as jnp
from jax import lax
from jax.experimental import pallas as pl
from jax.experimental.pallas import tpu as pltpu

F32 = jnp.float32
BF16 = jnp.bfloat16
MXU_DTYPE = jnp.bfloat16

D_MODEL = 1024
EPS = 1e-6
ROPE_THETA = 10000.0
LRU_BLOCKS = 8
LRU_C = 8.0
MLA_HEADS = 8
MLA_NOPE = 64
MLA_ROPE = 32
MLA_QK = 96
DIL_GROUPS = ((128, 1), (512, 4), (2048, 16))
DIL_HD = 64
DIL_NK = 128
MLA_HPS = 8
IN_WIDTH = 11168
ADAM_LR, ADAM_B1, ADAM_B2, ADAM_EPS, ADAM_WD, ADAM_STEP = 0.001, 0.9, 0.999, 1e-08, 0.01, 10

LANES = 128
G_LRU, G_MLA, G_DIL, G_MRG = 0, 2048, 3072, 8192
IN_PAD = 11264
KR_OFF = 2432

NN = (((1,), (0,)), ((), ()))
NT = (((1,), (1,)), ((), ()))
TN = (((0,), (0,)), ((), ()))
NEG = -1e30
MESH = pl.DeviceIdType.MESH
VMEM_LIMIT = 48 * 2**20


def _cparams(sem):
    return pltpu.CompilerParams(dimension_semantics=sem, vmem_limit_bytes=VMEM_LIMIT)


def _dot(a, b, dims):
    return lax.dot_general(a.astype(MXU_DTYPE), b.astype(MXU_DTYPE), dims, preferred_element_type=F32)


@jax.custom_vjp
def mm(a, w):
    return _dot(a, w, NN)


def _mm_fwd(a, w):
    return _dot(a, w, NN), (a, w)


def _mm_bwd(res, g):
    a, w = res
    return _dot(g, w, NT), _dot(a, g, TN)


mm.defvjp(_mm_fwd, _mm_bwd)


def _seg64_matrix():
    r = lax.broadcasted_iota(jnp.int32, (LANES, LANES), 0) < DIL_HD
    c = lax.broadcasted_iota(jnp.int32, (LANES, LANES), 1) < DIL_HD
    return (r == c).astype(BF16)


def _seg_sum_impl(x):
    b = _seg64_matrix()
    hi = x.astype(BF16)
    r1 = x - hi.astype(F32)
    mid = r1.astype(BF16)
    lo = (r1 - mid.astype(F32)).astype(BF16)
    dot = lambda u: lax.dot_general(u, b, NN, preferred_element_type=F32)
    return dot(hi) + dot(mid) + dot(lo)


@jax.custom_vjp
def seg_sum(x):
    return _seg_sum_impl(x)


seg_sum.defvjp(lambda x: (_seg_sum_impl(x), None), lambda _, g: (_seg_sum_impl(g),))


def _lroll_impl(x, s):
    return pltpu.roll(x, s % LANES, 1)


@functools.partial(jax.custom_vjp, nondiff_argnums=(1,))
def lroll(x, s):
    return _lroll_impl(x, s)


lroll.defvjp(lambda x, s: (_lroll_impl(x, s), None), lambda s, _, g: (_lroll_impl(g, -s),))


class _Ops:
    def __init__(self, diff):
        self.mm = mm if diff else (lambda a, w: _dot(a, w, NN))
        self.seg_sum = seg_sum if diff else _seg_sum_impl
        self.lroll = lroll if diff else _lroll_impl


PLAIN, DIFF = _Ops(False), _Ops(True)


def rms(x, g, n):
    ms = jnp.sum(x * x, axis=-1, keepdims=True) * (1.0 / n)
    return x * lax.rsqrt(ms + EPS) * g


def rope(ops, x, c, s1, s2, half):
    return x * c + ops.lroll(x, -half) * s1 + ops.lroll(x, half) * s2


def _sigmoid(x):
    return 1.0 / (1.0 + jnp.exp(-x))


def _silu_and_grad(g):
    sg = _sigmoid(g)
    return g * sg, sg * (1.0 + g * (1.0 - sg))


def _softplus(x):
    return jnp.maximum(x, 0.0) + jnp.log(1.0 + jnp.exp(-jnp.abs(x)))


def _expm1(y):
    series = y * (1.0 + y * (0.5 + y * (1.0 / 6.0 + y * (1.0 / 24.0 + y * (1.0 / 120.0)))))
    return jnp.where(y > -0.05, series, jnp.exp(jnp.minimum(y, -0.05)) - 1.0)


@jax.custom_vjp
def sigmoid_d(x):
    return _sigmoid(x)


def _sigmoid_d_fwd(x):
    s = _sigmoid(x)
    return s, s


sigmoid_d.defvjp(_sigmoid_d_fwd, lambda s, g: (g * s * (1.0 - s),))


def _decay_mult(log_a):
    return jnp.sqrt(-_expm1(2.0 * log_a))


@jax.custom_vjp
def decay_mult_d(log_a):
    return _decay_mult(log_a)


def _decay_mult_d_fwd(log_a):
    em = _expm1(2.0 * log_a)
    mult = jnp.sqrt(-em)
    return mult, (em, mult)


decay_mult_d.defvjp(_decay_mult_d_fwd, lambda res, g: (-g * (res[0] + 1.0) / res[1],))
PLAIN.sigmoid, DIFF.sigmoid = _sigmoid, sigmoid_d
PLAIN.decay_mult, DIFF.decay_mult = _decay_mult, decay_mult_d


def _host(body, n_in, n_out, n_scr, hook, grid):
    if hook is None:
        return body, [], [], [], [], []
    ins, out_shapes, n_sem, make = hook
    ni, no = len(ins), len(out_shapes)
    hbm = pl.BlockSpec(memory_space=pltpu.HBM)

    def at(corner):
        cond = None
        for axis, size in enumerate(grid):
            here = pl.program_id(axis) == (size - 1 if corner else 0)
            cond = here if cond is None else cond & here
        return cond

    def hosted(*refs):
        a, refs = refs[:n_in], refs[n_in:]
        xi, refs = refs[:ni], refs[ni:]
        o, refs = refs[:n_out], refs[n_out:]
        xo, refs = refs[:no], refs[no:]
        scr, (send, recv) = refs[:n_scr], refs[n_scr:]
        start, finish = make(xi, xo, send, recv)
        pl.when(at(False))(start)
        body(*a, *o, *scr)
        pl.when(at(True))(finish)

    sems = [pltpu.SemaphoreType.DMA((n_sem,)), pltpu.SemaphoreType.DMA((n_sem,))]
    return hosted, list(ins), [hbm] * ni, [hbm] * no, list(out_shapes), sems


def _mm_call(name, a, b, *, mode, m, n, k, a_blk, b_blk, out_dtype, add, tm, tn, tk, comm=None):
    nk = k // tk
    assert m % tm == 0 and n % tn == 0 and k % tk == 0, (name, m, n, k, tm, tn, tk)
    dims = {"nn": NN, "nt": NT, "tn": TN}[mode]

    def body(*refs):
        if add is None:
            a_ref, b_ref, o_ref, *scr = refs
            add_ref = None
        else:
            a_ref, b_ref, add_ref, o_ref, *scr = refs
        part = _dot(a_ref[...], b_ref[...], dims)

        def finish(acc):
            if add_ref is not None:
                acc = acc + add_ref[...]
            o_ref[...] = acc.astype(o_ref.dtype)

        if nk == 1:
            finish(part)
        else:
            (acc_ref,) = scr
            kk = pl.program_id(2)

            @pl.when(kk == 0)
            def _():
                acc_ref[...] = part

            @pl.when(kk > 0)
            def _():
                acc_ref[...] += part

            @pl.when(kk == nk - 1)
            def _():
                finish(acc_ref[...])

    in_specs = [a_blk, b_blk]
    args = [a, b]
    if add is not None:
        in_specs.append(pl.BlockSpec((tm, tn), lambda j, i, kk: (i, j)))
        args.append(add)
    grid = (n // tn, m // tm, nk)
    scratch = [] if nk == 1 else [pltpu.VMEM((tm, tn), F32)]
    body, x_in, x_in_specs, x_out_specs, x_out_shapes, sems = _host(body, len(args), 1, len(scratch), comm, grid)
    res = pl.pallas_call(
        body,
        name=name,
        grid=grid,
        in_specs=in_specs + x_in_specs,
        out_specs=[pl.BlockSpec((tm, tn), lambda j, i, kk: (i, j))] + x_out_specs,
        out_shape=[jax.ShapeDtypeStruct((m, n), out_dtype)] + x_out_shapes,
        scratch_shapes=scratch + sems,
        compiler_params=_cparams(("arbitrary",) * 3 if comm is not None else ("parallel", "parallel", "arbitrary")),
    )(*args, *x_in)
    return res[0] if comm is None else res


def mm_nn(name, a, b, *, n=None, n_off=0, out_dtype=F32, add=None, tm=1024, tn=1024, tk=1024):
    m, k = a.shape
    n = b.shape[1] if n is None else n
    tm, tn, tk = min(tm, m), min(tn, n), min(k, tk)
    ob = n_off // tn
    assert n_off % tn == 0
    return _mm_call(name, a, b, mode="nn", m=m, n=n, k=k, out_dtype=out_dtype, add=add, tm=tm, tn=tn, tk=tk,
                    a_blk=pl.BlockSpec((tm, tk), lambda j, i, kk: (i, kk)),
                    b_blk=pl.BlockSpec((tk, tn), lambda j, i, kk: (kk, j + ob)))


def mm_nt(name, a, b, *, out_dtype=F32, tm=1024, tn=1024, tk=1024, comm=None):
    m, k = a.shape
    n = b.shape[0]
    tm, tn, tk = min(tm, m), min(tn, n), min(k, tk)
    return _mm_call(name, a, b, mode="nt", m=m, n=n, k=k, out_dtype=out_dtype, add=None, tm=tm, tn=tn, tk=tk, comm=comm,
                    a_blk=pl.BlockSpec((tm, tk), lambda j, i, kk: (i, kk)),
                    b_blk=pl.BlockSpec((tn, tk), lambda j, i, kk: (j, kk)))


DW_TILES = dict(tm=1024, tn=512, tk=4096)


def in_proj_dw(ht, dz):
    t = ht.shape[1]
    tn = 512
    nchunk = IN_PAD // tn
    kr_chunk = KR_OFF // tn
    lo = KR_OFF - kr_chunk * tn

    def body(ht_ref, dz_ref, out_ref, buf, sems):
        j = pl.program_id(0)
        slot = j % 2
        buf[slot] = _dot(ht_ref[...], dz_ref[...], NN).T

        def copies(c, s):
            dst = jnp.where(c < kr_chunk, c * tn, (c - 1) * tn + lo + MLA_ROPE)
            plain = [pltpu.make_async_copy(buf.at[s], out_ref.at[pl.ds(pl.multiple_of(dst, 8), tn)], sems.at[s, 0])]
            split = [pltpu.make_async_copy(buf.at[s, 0:lo], out_ref.at[KR_OFF - lo:KR_OFF], sems.at[s, 0]),
                     pltpu.make_async_copy(buf.at[s, lo + 64:lo + 96], out_ref.at[KR_OFF:KR_OFF + MLA_ROPE], sems.at[s, 1])]
            return plain, split

        def each(c, s, action):
            plain, split = copies(c, s)

            @pl.when(c != kr_chunk)
            def _():
                for cp in plain:
                    action(cp)

            @pl.when(c == kr_chunk)
            def _():
                for cp in split:
                    action(cp)

        @pl.when(j > 0)
        def _():
            each(j - 1, 1 - slot, lambda cp: cp.wait())

        each(j, slot, lambda cp: cp.start())

        @pl.when(j == nchunk - 1)
        def _():
            each(j, slot, lambda cp: cp.wait())

    return pl.pallas_call(
        body,
        name="in_proj_dw",
        grid=(nchunk,),
        in_specs=[pl.BlockSpec((D_MODEL, t), lambda j: (0, 0)), pl.BlockSpec((t, tn), lambda j: (0, j))],
        out_specs=pl.BlockSpec(memory_space=pltpu.HBM),
        out_shape=jax.ShapeDtypeStruct((IN_WIDTH, D_MODEL), F32),
        scratch_shapes=[pltpu.VMEM((2, tn, D_MODEL), F32), pltpu.SemaphoreType.DMA((2, 2))],
        compiler_params=_cparams(("arbitrary",)),
    )(ht, dz)


def rowwise(name, fn, t, *, tm=256, ncol=1, ins=(), outs=(), touts=(), accs=()):
    n_in, n_out, n_acc = len(ins), len(outs) + len(touts), len(accs)

    def zero_when(ref, cond):
        @pl.when(cond)
        def _():
            ref[...] = jnp.zeros(ref.shape, ref.dtype)

    def body(*refs):
        in_refs, out_refs, acc_refs = refs[:n_in], refs[n_in:n_in + n_out], refs[n_in + n_out:]
        j, i = pl.program_id(0), pl.program_id(1)
        for ref, (_, _, _, cd) in zip(acc_refs, accs):
            zero_when(ref, (i == 0) if cd else ((i == 0) & (j == 0)))
        fn(in_refs, out_refs, acc_refs)

    def in_spec(arr, w, base, cd, rd):
        rows = tm if rd else arr.shape[0]
        return pl.BlockSpec((rows, w), lambda j, i: (i if rd else 0, base + (j if cd else 0)))

    in_specs = [in_spec(*e) for e in ins]
    out_specs = [pl.BlockSpec((tm, w), (lambda j, i, cd=cd: (i, j if cd else 0))) for (_, w, _, cd) in outs]
    out_specs += [pl.BlockSpec((w, tm), (lambda j, i, cd=cd: (j if cd else 0, i))) for (_, w, _, cd) in touts]
    out_specs += [pl.BlockSpec((r, w), (lambda j, i, cd=cd: (0, j if cd else 0))) for (r, _, w, cd) in accs]
    out_shape = [jax.ShapeDtypeStruct((t, c), dt) for (c, _, dt, _) in outs]
    out_shape += [jax.ShapeDtypeStruct((r, t), dt) for (r, _, dt, _) in touts]
    out_shape += [jax.ShapeDtypeStruct((r, c), F32) for (r, c, _, _) in accs]
    res = pl.pallas_call(
        body,
        name=name,
        grid=(ncol, t // tm),
        in_specs=in_specs,
        out_specs=out_specs,
        out_shape=out_shape,
        compiler_params=_cparams(("arbitrary", "arbitrary") if accs else ("parallel", "parallel")),
    )(*[e[0] for e in ins])
    return res


def _row(arr, w=None, base=0, cd=False):
    return (arr, arr.shape[1] if w is None else w, base, cd, True)


def _const(arr, w=None, base=0, cd=False):
    return (arr, arr.shape[1] if w is None else w, base, cd, False)


def rope_tables(positions):
    t = positions.shape[0]

    def fn(ins, outs, _):
        pos = ins[0][...].astype(F32)
        lane = lax.broadcasted_iota(jnp.int32, (1, LANES), 1)
        log_theta = jnp.log(jnp.float32(ROPE_THETA))
        jm = lane - MLA_NOPE
        idx = jnp.clip(jnp.where(jm < 16, jm, jm - 16), 0, 15).astype(F32)
        ang = pos * jnp.exp(-(idx * (2.0 / MLA_ROPE)) * log_theta)
        cos, sin = jnp.cos(ang), jnp.sin(ang)
        in_rope = (lane >= MLA_NOPE) & (lane < MLA_QK)
        outs[0][:, 0:128] = jnp.where(lane < MLA_NOPE, 1.0, jnp.where(in_rope, cos, 0.0))
        outs[0][:, 128:256] = jnp.where(in_rope & (jm < 16), -sin, 0.0)
        outs[0][:, 256:384] = jnp.where(in_rope & (jm >= 16), sin, 0.0)
        jd = lane & (DIL_HD - 1)
        idx = (jd & 31).astype(F32)
        ang = pos * jnp.exp(-(idx * (2.0 / DIL_HD)) * log_theta)
        cos, sin = jnp.cos(ang), jnp.sin(ang)
        outs[0][:, 384:512] = cos
        outs[0][:, 512:640] = jnp.where(jd < 32, -sin, 0.0)
        outs[0][:, 640:768] = jnp.where(jd >= 32, sin, 0.0)

    return rowwise("rope_tables", fn, t, tm=1024, ins=[_row(positions)], outs=[(768, 768, F32, False)])[0]


def rmsnorm_fwd(x, g):
    def fn(ins, outs, _):
        h = rms(ins[0][...], ins[1][...], D_MODEL)
        outs[0][...] = h.astype(MXU_DTYPE)
        outs[1][...] = h.T.astype(MXU_DTYPE)

    return rowwise("rmsnorm_fwd", fn, x.shape[0], tm=512, ins=[_row(x), _const(g)], outs=[(D_MODEL, D_MODEL, MXU_DTYPE, False)],
                   touts=[(D_MODEL, D_MODEL, MXU_DTYPE, False)])


def rmsnorm_bwd(x, dh, dres, g):
    def fn(ins, outs, accs):
        _, vjp = jax.vjp(lambda xv, gv: rms(xv, gv, D_MODEL), ins[0][...], ins[3][...])
        dx, dg = vjp(ins[1][...])
        outs[0][...] = ins[2][...] + dx
        accs[0][...] += dg

    return rowwise("rmsnorm_bwd", fn, x.shape[0], tm=512, ins=[_row(x), _row(dh), _row(dres), _const(g)],
                   outs=[(D_MODEL, D_MODEL, F32, False)], accs=[(1, D_MODEL, D_MODEL, False)])


def loss_head(y, target):
    def fn(ins, outs, accs):
        err = ins[0][...] - ins[1][...]
        outs[0][...] = err * (1.0 / D_MODEL)
        accs[0][...] += jnp.sum(err * err, axis=0, keepdims=True)
        accs[1][...] = jnp.broadcast_to(jnp.sum(accs[0][...], keepdims=True), (1, LANES))

    dy, _, tot = rowwise("loss_head", fn, y.shape[0], tm=512, ins=[_row(y), _row(target)], outs=[(D_MODEL, D_MODEL, F32, False)],
                         accs=[(1, D_MODEL, D_MODEL, False), (1, LANES, LANES, False)])
    return dy, tot[0, 0] * (0.5 / D_MODEL)


def _shift_rows(v, d, fill, reverse):
    tb = v.shape[0]
    if d % 8 == 0:
        pad = jnp.full((d, v.shape[1]), fill, v.dtype)
        return jnp.concatenate([v[d:], pad] if reverse else [pad, v[:tb - d]], axis=0)
    rows = lax.broadcasted_iota(jnp.int32, v.shape, 0)
    if not reverse:
        return jnp.where(rows >= d, pltpu.roll(v, d, 0), fill)
    return jnp.where(rows < tb - d, pltpu.roll(v, tb - d, 0), fill)


def _scan_tile(a, b, reverse):
    d = 1
    while d < a.shape[0]:
        b = b + a * _shift_rows(b, d, 0.0, reverse)
        a = a * _shift_rows(a, d, 1.0, reverse)
        d *= 2
    return a, b


def _lru_gates(ops, xc, wgx, bgx, wga, bga, lam):
    gx = ops.sigmoid(ops.mm(xc, wgx) + bgx)
    ga = ops.sigmoid(ops.mm(xc, wga) + bga)
    log_a = -LRU_C * ga * _softplus(-lam)
    a = jnp.exp(log_a)
    return a, ops.decay_mult(log_a) * (gx * xc)


def _shifted_inputs(x, halo, tb):
    rows = lax.broadcasted_iota(jnp.int32, x.shape, 0)
    pad = jnp.zeros((tb - 8, LANES), F32)
    out = []
    for d in (3, 2, 1):
        head = jnp.concatenate([pltpu.roll(halo, d, 0), pad], axis=0)
        out.append(jnp.where(rows >= d, pltpu.roll(x, d, 0), head))
    return out + [x]


def _lru_specs(nt, tb, reverse):
    hb = tb // 8
    tt = (lambda t: nt - 1 - t) if reverse else (lambda t: t)
    blk = lambda off: pl.BlockSpec((tb, LANES), lambda n, t: (tt(t), n + off))
    halo = lambda off: pl.BlockSpec((8, LANES), lambda n, t: (jnp.maximum(tt(t) * hb - 1, 0), n + off))
    chan = lambda r: pl.BlockSpec((r, LANES), lambda n, t: (0, n))
    wblk = pl.BlockSpec((None, LANES, LANES), lambda n, t: (n, 0, 0))
    bblk = pl.BlockSpec((None, 1, LANES), lambda n, t: (n, 0, 0))
    return blk, halo, chan, wblk, bblk


def lru_fwd(z, conv_w, conv_b, wgx, bgx, wga, bga, lam, *, tb=256, comm=None):
    t = z.shape[0]
    nt = t // tb
    blk, halo, chan, wblk, bblk = _lru_specs(nt, tb, False)

    def body(x_ref, xh_ref, g_ref, cw_ref, cb_ref, wgx_ref, bgx_ref, wga_ref, bga_ref, lam_ref, h_ref, y_ref, yt_ref, carry_ref):
        ti = pl.program_id(1)

        @pl.when(ti == 0)
        def _():
            carry_ref[...] = jnp.zeros((8, LANES), F32)

        x = x_ref[...]
        hal = jnp.where(ti > 0, xh_ref[...], 0.0)
        xs = _shifted_inputs(x, hal, tb)
        xc = cb_ref[...] + sum(xs[k] * cw_ref[k:k + 1, :] for k in range(4))
        a, b = _lru_gates(PLAIN, xc, wgx_ref[...], bgx_ref[...], wga_ref[...], bga_ref[...], lam_ref[...])
        acum, h0 = _scan_tile(a, b, False)
        h = h0 + acum * carry_ref[7:8, :]
        carry_ref[...] = h[tb - 8:tb, :]
        h_ref[...] = h
        y = h * _silu_and_grad(g_ref[...])[0]
        y_ref[...] = y.astype(MXU_DTYPE)
        yt_ref[...] = y.T.astype(MXU_DTYPE)

    grid = (LRU_BLOCKS, nt)
    body, x_in, x_in_specs, x_out_specs, x_out_shapes, sems = _host(body, 10, 3, 1, comm, grid)
    return pl.pallas_call(
        body,
        name="lru_fwd" if comm is None else "lru_fwd_comm",
        grid=grid,
        in_specs=[blk(0), halo(0), blk(LRU_BLOCKS), chan(4), chan(1), wblk, bblk, wblk, bblk, chan(1)] + x_in_specs,
        out_specs=[blk(0), blk(0), pl.BlockSpec((LANES, tb), lambda n, t_: (n, t_))] + x_out_specs,
        out_shape=[jax.ShapeDtypeStruct((t, 1024), F32), jax.ShapeDtypeStruct((t, 1024), MXU_DTYPE),
                   jax.ShapeDtypeStruct((1024, t), MXU_DTYPE)] + x_out_shapes,
        scratch_shapes=[pltpu.VMEM((8, LANES), F32)] + sems,
        compiler_params=_cparams(("arbitrary", "arbitrary") if comm is not None else ("parallel", "arbitrary")),
    )(z, z, z, conv_w, conv_b, wgx, bgx, wga, bga, lam, *x_in)


def lru_bwd(z, h, dy, conv_w, conv_b, wgx, bgx, wga, bga, lam, *, tb=256, comm=None):
    t = z.shape[0]
    nt = t // tb
    blk, halo, chan, wblk, bblk = _lru_specs(nt, tb, True)

    def body(x_ref, xh_ref, g_ref, h_ref, hh_ref, dy_ref, cw_ref, cb_ref, wgx_ref, bgx_ref, wga_ref, bga_ref, lam_ref,
             dzx_ref, dzg_ref, dwgx_ref, dbgx_ref, dwga_ref, dbga_ref, dlam_ref, dcw_ref, dcb_ref,
             gcar_ref, acar_ref, xcar_ref):
        ti = pl.program_id(1)
        has_earlier = ti < nt - 1

        @pl.when(ti == 0)
        def _():
            for ref in (dwgx_ref, dbgx_ref, dwga_ref, dbga_ref, dlam_ref, dcw_ref, dcb_ref, gcar_ref, acar_ref, xcar_ref):
                ref[...] = jnp.zeros(ref.shape, F32)

        rows = lax.broadcasted_iota(jnp.int32, (tb, LANES), 0)
        x = x_ref[...]
        hal = jnp.where(has_earlier, xh_ref[...], 0.0)
        xs = _shifted_inputs(x, hal, tb)
        xc = cb_ref[...] + sum(xs[k] * cw_ref[k:k + 1, :] for k in range(4))
        (a, _), vjp = jax.vjp(functools.partial(_lru_gates, DIFF), xc, wgx_ref[...], bgx_ref[...], wga_ref[...],
                              bga_ref[...], lam_ref[...])
        g, h, dyv = g_ref[...], h_ref[...], dy_ref[...]
        silu, dsilu = _silu_and_grad(g)
        dzg_ref[...] = (dyv * h * dsilu).astype(MXU_DTYPE)
        a_next = jnp.where(rows < tb - 1, pltpu.roll(a, tb - 1, 0), acar_ref[0:1, :])
        acum, g0 = _scan_tile(a_next, dyv * silu, True)
        gt = g0 + acum * gcar_ref[0:1, :]
        h_prev = jnp.where(rows >= 1, pltpu.roll(h, 1, 0), jnp.where(has_earlier, hh_ref[7:8, :], 0.0))
        dxc, dwgx, dbgx, dwga, dbga, dlam = vjp((gt * h_prev, gt))
        later = xcar_ref[...]
        gcar_ref[...] = gt[0:8, :]
        acar_ref[...] = a[0:8, :]
        xcar_ref[...] = dxc[0:8, :]
        dx = dxc * cw_ref[3:4, :]
        pad = jnp.zeros((tb - 8, LANES), F32)
        for d in (1, 2, 3):
            tail = jnp.concatenate([pad, pltpu.roll(later, 8 - d, 0)], axis=0)
            up = jnp.where(rows < tb - d, pltpu.roll(dxc, tb - d, 0), tail)
            dx = dx + up * cw_ref[3 - d:4 - d, :]
        dzx_ref[...] = dx.astype(MXU_DTYPE)
        for k in range(4):
            dcw_ref[k:k + 1, :] += jnp.sum(dxc * xs[k], axis=0, keepdims=True)
        dcb_ref[...] += jnp.sum(dxc, axis=0, keepdims=True)
        dwgx_ref[...] += dwgx
        dbgx_ref[...] += dbgx
        dwga_ref[...] += dwga
        dbga_ref[...] += dbga
        dlam_ref[...] += dlam

    sds = jax.ShapeDtypeStruct
    grid = (LRU_BLOCKS, nt)
    body, x_in, x_in_specs, x_out_specs, x_out_shapes, sems = _host(body, 13, 9, 3, comm, grid)
    return pl.pallas_call(
        body,
        name="lru_bwd" if comm is None else "lru_bwd_comm",
        grid=grid,
        in_specs=[blk(0), halo(0), blk(LRU_BLOCKS), blk(0), halo(0), blk(0), chan(4), chan(1), wblk, bblk, wblk, bblk, chan(1)]
        + x_in_specs,
        out_specs=[blk(0), blk(0), wblk, bblk, wblk, bblk, chan(1), chan(4), chan(1)] + x_out_specs,
        out_shape=[sds((t, 1024), MXU_DTYPE), sds((t, 1024), MXU_DTYPE), sds(wgx.shape, F32), sds(bgx.shape, F32), sds(wga.shape, F32),
                   sds(bga.shape, F32), sds((1, 1024), F32), sds((4, 1024), F32), sds((1, 1024), F32)] + x_out_shapes,
        scratch_shapes=[pltpu.VMEM((8, LANES), F32)] * 3 + sems,
        compiler_params=_cparams(("arbitrary", "arbitrary") if comm is not None else ("parallel", "arbitrary")),
    )(z, z, z, h, h, dy, conv_w, conv_b, wgx, bgx, wga, bga, lam, *x_in)


def _mla_prep_tile(ops, cq, ckv, krp, c, s1, s2, g_cq, g_ckv, wq, wk, wv, gq, gk):
    cqn = rms(cq, g_cq, 256)
    ckvn = rms(ckv, g_ckv, 128)
    v = ops.mm(ckvn, wv)
    qs, ks = [], []
    for hd in range(MLA_HEADS):
        q = rms(ops.mm(cqn, wq[hd]), gq, MLA_QK)
        k = rms(ops.mm(ckvn, wk[hd]) + krp, gk, MLA_QK)
        qs.append(rope(ops, q, c, s1, s2, 16) * (MLA_QK ** -0.5))
        ks.append(rope(ops, k, c, s1, s2, 16))
    return tuple(qs), tuple(ks), v


def _mla_prep_args(ins):
    z_cq, z_ckv, z_kr, tc, ts1, ts2, g_cq, g_ckv, wq, wk, wv, gq, gk = ins[:13]
    heads = lambda w: tuple(w[:, LANES * hd:LANES * (hd + 1)] for hd in range(MLA_HEADS))
    return (z_cq[...], z_ckv[...], z_kr[...], tc[...], ts1[...], ts2[...], g_cq[...], g_ckv[...], heads(wq), heads(wk),
            wv[...], gq[...], gk[...])


def _mla_prep_ins(z, tabs, w):
    return [_row(z, 256, 0), _row(z, 128, 2), _row(z, 128, 3), _row(tabs, 128, 0), _row(tabs, 128, 1), _row(tabs, 128, 2),
            _const(w["cq_norm_g"]), _const(w["ckv_norm_g"]), _const(w["wq"]), _const(w["wk"]), _const(w["wv"]),
            _const(w["gq"]), _const(w["gk"])]


def mla_prep_fwd(z, tabs, w):
    def fn(ins, outs, _):
        qs, ks, v = _mla_prep_tile(PLAIN, *_mla_prep_args(ins))
        for hd in range(MLA_HEADS):
            outs[0][:, LANES * hd:LANES * (hd + 1)] = qs[hd].astype(MXU_DTYPE)
            outs[1][:, LANES * hd:LANES * (hd + 1)] = ks[hd].astype(MXU_DTYPE)
        outs[2][...] = v.astype(MXU_DTYPE)
        outs[3][...] = v.T.astype(MXU_DTYPE)

    return rowwise("mla_prep_fwd", fn, z.shape[0], tm=512, ins=_mla_prep_ins(z, tabs, w),
                   outs=[(1024, 1024, MXU_DTYPE, False), (1024, 1024, MXU_DTYPE, False), (512, 512, MXU_DTYPE, False)],
                   touts=[(512, 512, MXU_DTYPE, False)])


def mla_prep_bwd(z, tabs, w, dq, dk, dv):
    def fn(ins, outs, accs):
        args = _mla_prep_args(ins)
        _, vjp = jax.vjp(functools.partial(_mla_prep_tile, DIFF), *args)
        heads = lambda ref: tuple(ref[:, LANES * hd:LANES * (hd + 1)] for hd in range(MLA_HEADS))
        dcq, dckv, dkr, _, _, _, dg_cq, dg_ckv, dwq, dwk, dwv, dgq, dgk = vjp((heads(ins[13]), heads(ins[14]), ins[15][...]))
        lane = lax.broadcasted_iota(jnp.int32, (1, LANES), 1)
        outs[0][:, 0:256] = dcq.astype(MXU_DTYPE)
        outs[0][:, 256:384] = dckv.astype(MXU_DTYPE)
        outs[0][:, 384:512] = jnp.where((lane >= MLA_NOPE) & (lane < MLA_QK), dkr, 0.0).astype(MXU_DTYPE)
        accs[0][...] += dg_cq
        accs[1][...] += dg_ckv
        for hd in range(MLA_HEADS):
            accs[2][:, LANES * hd:LANES * (hd + 1)] += dwq[hd]
            accs[3][:, LANES * hd:LANES * (hd + 1)] += dwk[hd]
        accs[4][...] += dwv
        accs[5][...] += dgq
        accs[6][...] += dgk

    return rowwise("mla_prep_bwd", fn, z.shape[0], tm=512, ins=_mla_prep_ins(z, tabs, w) + [_row(dq), _row(dk), _row(dv)],
                   outs=[(512, 512, MXU_DTYPE, False)],
                   accs=[(1, 256, 256, False), (1, 128, 128, False), (256, 1024, 1024, False), (128, 1024, 1024, False),
                         (128, 512, 512, False), (1, 128, 128, False), (1, 128, 128, False)])


def _head_masks():
    lane = lax.broadcasted_iota(jnp.int32, (1, LANES), 1)
    return (lane < DIL_HD, lane >= DIL_HD)


def _row_scalar(tile, mask):
    return jnp.max(jnp.where(mask, tile, -jnp.inf), axis=-1, keepdims=True)


def _causal_tiles(nq, by_key):
    pairs = [(i, j) for i in range(nq) for j in range(i + 1)]
    if by_key:
        pairs.sort(key=lambda ij: (ij[1], ij[0]))
    return (jnp.asarray([ij[0] for ij in pairs], jnp.int32), jnp.asarray([ij[1] for ij in pairs], jnp.int32))


def mla_attn_fwd(q, k, vt, z, *, tq=256, gather=None):
    t = q.shape[0]
    nq = t // tq
    it, jt = _causal_tiles(nq, False)
    hps, wq, wv = MLA_HPS, LANES * MLA_HPS, 64 * MLA_HPS
    ws, layer = gather if gather is not None else ([], 0)
    ng = len(ws)
    ngrp, nsteps = MLA_HEADS // hps, int(it.shape[0])

    def body(it_ref, jt_ref, q_ref, k_ref, vt_ref, g_ref, *rest):
        w_refs, rest = rest[:ng], rest[ng:]
        o_ref, y_ref, yt_ref, lse_ref = rest[:4]
        ga_refs, rest = rest[4:4 + ng], rest[4 + ng:]
        m_scr, l_scr, acc_scr = rest[:3]
        step = pl.program_id(1)
        i, j = it_ref[step], jt_ref[step]
        if ng:
            gather_start, gather_forward, gather_finish = _layer_gather(w_refs, ga_refs, rest[3], rest[4], layer)

            @pl.when((pl.program_id(0) == 0) & (step == 0))
            def _():
                gather_start()

            @pl.when((pl.program_id(0) == ngrp - 1) & (step == (3 * nsteps) // 4))
            def _():
                gather_forward()

        @pl.when(j == 0)
        def _():
            m_scr[...] = jnp.full(m_scr.shape, NEG, F32)
            l_scr[...] = jnp.zeros(l_scr.shape, F32)
            acc_scr[...] = jnp.zeros(acc_scr.shape, F32)

        def update(diagonal):
            heads = range(hps)
            lanes = [slice(LANES * hh, LANES * (hh + 1)) for hh in heads]
            rows = [slice(64 * hh, 64 * (hh + 1)) for hh in heads]
            sts = [_dot(k_ref[:, lanes[hh]], q_ref[:, lanes[hh]], NT) for hh in heads]
            m_prev = [m_scr[hh:hh + 1, :] for hh in heads]
            l_prev = [l_scr[hh:hh + 1, :] for hh in heads]
            acc_prev = [acc_scr[rows[hh], :] for hh in heads]
            m_new, l_new, acc_new = [], [], []
            for hh in heads:
                st = sts[hh]
                if diagonal:
                    key = lax.broadcasted_iota(jnp.int32, (tq, tq), 0)
                    qry = lax.broadcasted_iota(jnp.int32, (tq, tq), 1)
                    st = jnp.where(key <= qry, st, NEG)
                m_new.append(jnp.maximum(m_prev[hh], jnp.max(st, axis=0, keepdims=True)))
                alpha = jnp.exp(m_prev[hh] - m_new[hh])
                pt = jnp.exp(st - m_new[hh])
                l_new.append(alpha * l_prev[hh] + jnp.sum(pt, axis=0, keepdims=True))
                acc_new.append(alpha * acc_prev[hh] + _dot(vt_ref[rows[hh], :], pt, NN))
            for hh in heads:
                m_scr[hh:hh + 1, :] = m_new[hh]
                l_scr[hh:hh + 1, :] = l_new[hh]
                acc_scr[rows[hh], :] = acc_new[hh]

        @pl.when(j < i)
        def _():
            update(False)

        @pl.when(j == i)
        def _():
            update(True)
            lse_ref[...] = jnp.zeros(lse_ref.shape, F32)
            for hh in range(hps):
                rows = slice(64 * hh, 64 * (hh + 1))
                acc_scr[rows, :] = acc_scr[rows, :] / l_scr[hh:hh + 1, :]
                lse_ref[hh:hh + 1, :] = m_scr[hh:hh + 1, :] + jnp.log(l_scr[hh:hh + 1, :])
            o = acc_scr[...].T
            o_ref[...] = o
            y = o * _silu_and_grad(g_ref[...])[0]
            y_ref[...] = y.astype(MXU_DTYPE)
            yt_ref[...] = y.T.astype(MXU_DTYPE)

        if ng:
            @pl.when((pl.program_id(0) == ngrp - 1) & (step == nsteps - 1))
            def _():
                gather_finish()

    qo = lambda w, off=0: pl.BlockSpec((tq, w), lambda p, s, it_, jt_: (it_[s], p + off))
    sds = jax.ShapeDtypeStruct
    comm_scratch = [pltpu.SemaphoreType.DMA((6 * ng,)), pltpu.SemaphoreType.DMA((6 * ng,))] if ng else []
    return pl.pallas_call(
        body,
        name="mla_attn_fwd_gather" if ng else "mla_attn_fwd",
        grid_spec=pltpu.PrefetchScalarGridSpec(
            num_scalar_prefetch=2,
            grid=(ngrp, nsteps),
            in_specs=[qo(wq), pl.BlockSpec((tq, wq), lambda p, s, it_, jt_: (jt_[s], p)),
                      pl.BlockSpec((wv, tq), lambda p, s, it_, jt_: (p, jt_[s])), qo(wv, 512 // wv)] + [HBM_SPEC] * ng,
            out_specs=[qo(wv), qo(wv), pl.BlockSpec((wv, tq), lambda p, s, it_, jt_: (p, it_[s])),
                       pl.BlockSpec((None, 8, tq), lambda p, s, it_, jt_: (p, 0, it_[s]))] + [HBM_SPEC] * ng,
            scratch_shapes=[pltpu.VMEM((8, tq), F32), pltpu.VMEM((8, tq), F32), pltpu.VMEM((wv, tq), F32)] + comm_scratch,
        ),
        out_shape=[sds((t, 512), F32), sds((t, 512), MXU_DTYPE), sds((512, t), MXU_DTYPE), sds((ngrp, 8, t), F32)]
        + _gather_shapes(ws),
        compiler_params=_cparams(("arbitrary", "arbitrary") if ng else ("parallel", "arbitrary")),
    )(it, jt, q, k, vt, z, *ws)


def mla_attn_bwd(q, k, v, do, lse, dd, *, tq=256, chips=None):
    t = q.shape[0]
    nq = t // tq
    it, jt = _causal_tiles(nq, True)
    hps, wq, wv = MLA_HPS, LANES * MLA_HPS, 64 * MLA_HPS
    pcs, layer = chips if chips is not None else ([], 0)
    nc = len(pcs)
    ngrp, nsteps = MLA_HEADS // hps, int(it.shape[0])

    def body(it_ref, jt_ref, q_ref, k_ref, v_ref, do_ref, lse_ref, d_ref, *rest):
        pc_refs, rest = rest[:nc], rest[nc:]
        dq_ref, dk_ref, dv_ref = rest[:3]
        lb_refs, rest = rest[3:3 + nc], rest[3 + nc:]
        dk_scr, dv_scr = rest[:2]
        step = pl.program_id(1)
        i, j = it_ref[step], jt_ref[step]
        masks = _head_masks()
        if nc:
            chips_start, chips_finish = _chips_exchange(pc_refs, lb_refs, rest[2], rest[3], layer)

            @pl.when((pl.program_id(0) == 0) & (step == 0))
            def _():
                chips_start()

        @pl.when(step == 0)
        def _():
            dq_ref[...] = jnp.zeros(dq_ref.shape, F32)

        @pl.when(i == j)
        def _():
            dk_scr[...] = jnp.zeros(dk_scr.shape, F32)
            dv_scr[...] = jnp.zeros(dv_scr.shape, F32)

        def update(diagonal):
            qrows = pl.ds(pl.multiple_of(i * tq, tq), tq)
            heads = range(hps)
            lanes = [slice(LANES * hh, LANES * (hh + 1)) for hh in heads]
            pair = [slice(LANES * (hh // 2), LANES * (hh // 2 + 1)) for hh in heads]
            qh = [q_ref[:, lanes[hh]] for hh in heads]
            kh = [k_ref[:, lanes[hh]] for hh in heads]
            doh = []
            for hh in heads:
                dov = do_ref[:, pair[hh]]
                doh.append(jnp.where(masks[hh % 2], dov, jnp.zeros_like(dov)))
            sts = [_dot(kh[hh], qh[hh], NT) for hh in heads]
            dps = [_dot(v_ref[:, pair[hh]], doh[hh], NT) for hh in heads]
            lse = [lse_ref[hh:hh + 1, :] for hh in heads]
            ddv = [d_ref[hh:hh + 1, :] for hh in heads]
            dk_new = [dk_scr[:, lanes[hh]] for hh in heads]
            dq_new = [dq_ref[qrows, lanes[hh]] for hh in heads]
            dv_new = [dv_scr[:, pair[2 * pp]] for pp in range(hps // 2)]
            for hh in heads:
                st = sts[hh] - lse[hh]
                if diagonal:
                    key = lax.broadcasted_iota(jnp.int32, (tq, tq), 0)
                    qry = lax.broadcasted_iota(jnp.int32, (tq, tq), 1)
                    st = jnp.where(key <= qry, st, NEG)
                pt = jnp.exp(st)
                dst = pt * (dps[hh] - ddv[hh])
                dv_new[hh // 2] = dv_new[hh // 2] + _dot(pt, doh[hh], NN)
                dk_new[hh] = dk_new[hh] + _dot(dst, qh[hh], NN)
                dq_new[hh] = dq_new[hh] + _dot(dst, kh[hh], TN)
            for hh in heads:
                dk_scr[:, lanes[hh]] = dk_new[hh]
                dq_ref[qrows, lanes[hh]] = dq_new[hh]
            for pp in range(hps // 2):
                dv_scr[:, pair[2 * pp]] = dv_new[pp]

        @pl.when(j < i)
        def _():
            update(False)

        @pl.when(j == i)
        def _():
            update(True)

        @pl.when(i == nq - 1)
        def _():
            dk_ref[...] = dk_scr[...]
            dv_ref[...] = dv_scr[...]

        if nc:
            @pl.when((pl.program_id(0) == ngrp - 1) & (step == nsteps - 1))
            def _():
                chips_finish()

    qo = lambda w: pl.BlockSpec((tq, w), lambda p, s, it_, jt_: (it_[s], p))
    kv = lambda w: pl.BlockSpec((tq, w), lambda p, s, it_, jt_: (jt_[s], p))
    stat = pl.BlockSpec((None, 8, tq), lambda p, s, it_, jt_: (p, 0, it_[s]))
    sds = jax.ShapeDtypeStruct
    comm_scratch = [pltpu.SemaphoreType.DMA((3 * nc,)), pltpu.SemaphoreType.DMA((3 * nc,))] if nc else []
    return pl.pallas_call(
        body,
        name="mla_attn_bwd_chips" if nc else "mla_attn_bwd",
        grid_spec=pltpu.PrefetchScalarGridSpec(
            num_scalar_prefetch=2,
            grid=(ngrp, nsteps),
            in_specs=[qo(wq), kv(wq), kv(wv), qo(wv), stat, stat] + [HBM_SPEC] * nc,
            out_specs=[pl.BlockSpec((t, wq), lambda p, s, it_, jt_: (0, p)), kv(wq), kv(wv)] + [HBM_SPEC] * nc,
            scratch_shapes=[pltpu.VMEM((tq, wq), F32), pltpu.VMEM((tq, wv), F32)] + comm_scratch,
        ),
        out_shape=[sds((t, 1024), F32), sds((t, 1024), F32), sds((t, 512), F32)] + _chips_shapes(pcs),
        compiler_params=_cparams(("arbitrary", "arbitrary") if nc else ("parallel", "arbitrary")),
    )(it, jt, q, k, v, do, lse, dd, *pcs)


def mla_gate_bwd(dy, o, z, *, tm=512):
    t = dy.shape[0]
    wv = 64 * MLA_HPS

    def body(dy_ref, o_ref, g_ref, do_ref, dzg_ref, dd_ref):
        dyv, ov = dy_ref[...], o_ref[...]
        silu, dsilu = _silu_and_grad(g_ref[...])
        do = dyv * silu
        do_ref[...] = do.astype(MXU_DTYPE)
        dzg_ref[...] = (dyv * ov * dsilu).astype(MXU_DTYPE)
        prod = do * ov
        row = lax.broadcasted_iota(jnp.int32, (8, wv), 0)
        lane = lax.broadcasted_iota(jnp.int32, (8, wv), 1)
        pick = ((lane >= row * DIL_HD) & (lane < (row + 1) * DIL_HD)).astype(BF16)
        hi = prod.astype(BF16)
        r1 = prod - hi.astype(F32)
        mid = r1.astype(BF16)
        lo = (r1 - mid.astype(F32)).astype(BF16)
        dot = lambda u: lax.dot_general(pick, u, NT, preferred_element_type=F32)
        dd_ref[...] = dot(hi) + dot(mid) + dot(lo)

    blk = lambda off=0: pl.BlockSpec((tm, wv), lambda p, i: (i, p + off))
    sds = jax.ShapeDtypeStruct
    return pl.pallas_call(
        body,
        name="mla_gate_bwd",
        grid=(MLA_HEADS // MLA_HPS, t // tm),
        in_specs=[blk(), blk(), blk(512 // wv)],
        out_specs=[blk(), blk(), pl.BlockSpec((None, 8, tm), lambda p, i: (p, 0, i))],
        out_shape=[sds((t, 512), MXU_DTYPE), sds((t, 512), MXU_DTYPE), sds((MLA_HEADS // MLA_HPS, 8, t), F32)],
        compiler_params=_cparams(("parallel", "parallel")),
    )(dy, o, z)


SPAN = 2048
DIL_SCALE = DIL_HD ** -0.5


def _rm_src(u, window, dil):
    nn, r = divmod(u, dil)
    return pl.ds(nn * window + r, DIL_NK, stride=dil) if dil > 1 else pl.ds(u * DIL_NK, DIL_NK)


def _rm_dst(u):
    return pl.ds(u * DIL_NK, DIL_NK)


def _dil_prep_tile(ops, x, g2, c, s1, s2, scale):
    ms = ops.seg_sum(x * x) * (1.0 / DIL_HD)
    return rope(ops, x * lax.rsqrt(ms + EPS) * g2, c, s1, s2, 32) * scale


def _span_blk(base):
    return pl.BlockSpec((SPAN, LANES), lambda s, p: (s, base + p))


def _const_blk(shape):
    return pl.BlockSpec(shape, lambda s, p: (0,) * len(shape))


_DIL_TABLE_SPECS = [pl.BlockSpec((SPAN, LANES), (lambda s, p, blk=blk: (s, blk))) for blk in (0, 1, 2)]


def dil_tables(tabs, gi):
    window, dil = DIL_GROUPS[gi]
    t = tabs.shape[0]

    def body(c_ref, s1_ref, s2_ref, o_ref):
        for u in range(SPAN // DIL_NK):
            src, dst = _rm_src(u, window, dil), _rm_dst(u)
            for i, ref in enumerate((c_ref, s1_ref, s2_ref)):
                o_ref[dst, LANES * i:LANES * (i + 1)] = ref[src, :]

    return pl.pallas_call(
        body,
        name=f"dil_tables_{dil}",
        grid=(t // SPAN,),
        in_specs=[pl.BlockSpec((SPAN, LANES), (lambda s, blk=blk: (s, blk))) for blk in (3, 4, 5)],
        out_specs=pl.BlockSpec((SPAN, 3 * LANES), lambda s: (s, 0)),
        out_shape=jax.ShapeDtypeStruct((t, 3 * LANES), F32),
        compiler_params=_cparams(("parallel",)),
    )(tabs, tabs, tabs)


def dil_prep_fwd(z, tabs, gq2, gk2, gi):
    window, dil = DIL_GROUPS[gi]
    t = z.shape[0]

    def body(q_ref, k_ref, v_ref, c_ref, s1_ref, s2_ref, gq_ref, gk_ref, qo_ref, ko_ref, vo_ref):
        for u in range(SPAN // DIL_NK):
            src, dst = _rm_src(u, window, dil), _rm_dst(u)
            c, s1, s2 = c_ref[dst, :], s1_ref[dst, :], s2_ref[dst, :]
            qo_ref[dst, :] = _dil_prep_tile(PLAIN, q_ref[src, :], gq_ref[...], c, s1, s2, DIL_SCALE).astype(MXU_DTYPE)
            ko_ref[dst, :] = _dil_prep_tile(PLAIN, k_ref[src, :], gk_ref[...], c, s1, s2, 1.0).astype(MXU_DTYPE)
            vo_ref[dst, :] = v_ref[src, :].astype(MXU_DTYPE)

    return pl.pallas_call(
        body,
        name=f"dil_prep_fwd_{dil}",
        grid=(t // SPAN, 4),
        in_specs=[_span_blk(4 * gi), _span_blk(12 + 4 * gi), _span_blk(24 + 4 * gi)] + _DIL_TABLE_SPECS
        + [_const_blk((1, LANES)), _const_blk((1, LANES))],
        out_specs=[_span_blk(0)] * 3,
        out_shape=[jax.ShapeDtypeStruct((t, 512), MXU_DTYPE)] * 3,
        compiler_params=_cparams(("parallel", "parallel")),
    )(z, z, z, tabs, tabs, tabs, gq2, gk2)


def dil_prep_bwd(z, tabs, gq2, gk2, dq, dk, dv, gi):
    window, dil = DIL_GROUPS[gi]
    t = z.shape[0]

    def body(q_ref, k_ref, c_ref, s1_ref, s2_ref, gq_ref, gk_ref, dq_ref, dk_ref, dv_ref, dzq_ref, dzk_ref, dzv_ref, dgq_ref,
             dgk_ref, sq, sk, sv):
        @pl.when((pl.program_id(0) == 0) & (pl.program_id(1) == 0))
        def _():
            dgq_ref[...] = jnp.zeros((1, LANES), F32)
            dgk_ref[...] = jnp.zeros((1, LANES), F32)

        dgs = [jnp.zeros((1, LANES), F32), jnp.zeros((1, LANES), F32)]
        for u in range(SPAN // DIL_NK):
            src, dst = _rm_src(u, window, dil), _rm_dst(u)
            c, s1, s2 = c_ref[dst, :], s1_ref[dst, :], s2_ref[dst, :]
            for idx, (x_ref, g_ref, ct_ref, scr, scale) in enumerate(((q_ref, gq_ref, dq_ref, sq, DIL_SCALE),
                                                                      (k_ref, gk_ref, dk_ref, sk, 1.0))):
                _, vjp = jax.vjp(lambda xv, gv, sc=scale: _dil_prep_tile(DIFF, xv, gv, c, s1, s2, sc), x_ref[src, :], g_ref[...])
                dx, dg = vjp(ct_ref[dst, :])
                scr[src, :] = dx
                dgs[idx] = dgs[idx] + dg
            sv[src, :] = dv_ref[dst, :]
        dgq_ref[...] += dgs[0] + pltpu.roll(dgs[0], DIL_HD, 1)
        dgk_ref[...] += dgs[1] + pltpu.roll(dgs[1], DIL_HD, 1)
        for c0 in range(0, SPAN, 256):
            rows = slice(c0, c0 + 256)
            dzq_ref[rows, :] = sq[rows, :].astype(MXU_DTYPE)
            dzk_ref[rows, :] = sk[rows, :].astype(MXU_DTYPE)
            dzv_ref[rows, :] = sv[rows, :].astype(MXU_DTYPE)

    sds = jax.ShapeDtypeStruct
    return pl.pallas_call(
        body,
        name=f"dil_prep_bwd_{dil}",
        grid=(t // SPAN, 4),
        in_specs=[_span_blk(4 * gi), _span_blk(12 + 4 * gi)] + _DIL_TABLE_SPECS
        + [_const_blk((1, LANES)), _const_blk((1, LANES)), _span_blk(0), _span_blk(0), _span_blk(0)],
        out_specs=[_span_blk(0)] * 3 + [_const_blk((1, LANES))] * 2,
        out_shape=[sds((t, 512), MXU_DTYPE)] * 3 + [sds((1, LANES), F32)] * 2,
        scratch_shapes=[pltpu.VMEM((SPAN, LANES), F32)] * 3,
        compiler_params=_cparams(("arbitrary", "arbitrary")),
    )(z, z, tabs, tabs, tabs, gq2, gk2, dq, dk, dv)


def _band_masks():
    qi = lax.broadcasted_iota(jnp.int32, (DIL_NK, DIL_NK), 0)
    ki = lax.broadcasted_iota(jnp.int32, (DIL_NK, DIL_NK), 1)
    return (ki >= qi), (ki <= qi)


def _pair_heads():
    return [(pair, hh) for pair in range(4) for hh in range(2)]


def _pair_lanes(pair):
    return slice(LANES * pair, LANES * (pair + 1))


def _zero_other_head(mask, x):
    return jnp.where(mask, x, jnp.zeros_like(x))


def dil_attn_fwd(name, q, k, v, dil):
    t = q.shape[0]

    def body(q_ref, kp_ref, kc_ref, vp_ref, vc_ref, o_ref, lse_ref):
        b = pl.program_id(0)
        mprev, mcur = _band_masks()
        mprev = mprev & (b >= dil)
        hm = _head_masks()
        heads = _pair_heads()
        qh = [_zero_other_head(hm[hh], q_ref[:, _pair_lanes(pair)]) for pair, hh in heads]
        sps = [_dot(qh[i], kp_ref[:, _pair_lanes(pair)], NT) for i, (pair, _) in enumerate(heads)]
        scs = [_dot(qh[i], kc_ref[:, _pair_lanes(pair)], NT) for i, (pair, _) in enumerate(heads)]
        o = [jnp.zeros((DIL_NK, LANES), F32) for _ in range(4)]
        lse = [jnp.zeros((DIL_NK, LANES), F32) for _ in range(4)]
        for i, (pair, hh) in enumerate(heads):
            sp, sc = jnp.where(mprev, sps[i], NEG), jnp.where(mcur, scs[i], NEG)
            m = jnp.maximum(jnp.max(sp, axis=1, keepdims=True), jnp.max(sc, axis=1, keepdims=True))
            ep, ec = jnp.exp(sp - m), jnp.exp(sc - m)
            den = jnp.sum(ep, axis=1, keepdims=True) + jnp.sum(ec, axis=1, keepdims=True)
            lanes = _pair_lanes(pair)
            oh = _dot(ep, _zero_other_head(hm[hh], vp_ref[:, lanes]), NN) + _dot(ec, _zero_other_head(hm[hh], vc_ref[:, lanes]), NN)
            o[pair] = o[pair] + oh / den
            lse[pair] = jnp.where(hm[hh], m + jnp.log(den), lse[pair])
        for pair in range(4):
            o_ref[:, _pair_lanes(pair)] = o[pair]
            lse_ref[:, _pair_lanes(pair)] = lse[pair]

    cur = pl.BlockSpec((DIL_NK, 512), lambda b: (b, 0))
    prev = pl.BlockSpec((DIL_NK, 512), lambda b: (jnp.maximum(b - dil, 0), 0))
    sds = jax.ShapeDtypeStruct
    return pl.pallas_call(
        body,
        name=name,
        grid=(t // DIL_NK,),
        in_specs=[cur, prev, cur, prev, cur],
        out_specs=[cur, cur],
        out_shape=[sds((t, 512), F32), sds((t, 512), F32)],
        compiler_params=_cparams(("parallel",)),
    )(q, k, k, v, v)


def dil_attn_bwd(name, q, k, v, do, lse, dd, dil):
    t = q.shape[0]
    nblk = t // DIL_NK

    def body(q_ref, do_ref, l_ref, d_ref, kp_ref, kc_ref, vp_ref, vc_ref, dq_ref, dk_ref, dv_ref):
        b = pl.program_id(0)

        @pl.when(b == 0)
        def _():
            dk_ref[...] = jnp.zeros(dk_ref.shape, F32)
            dv_ref[...] = jnp.zeros(dv_ref.shape, F32)

        mprev, mcur = _band_masks()
        mprev = mprev & (b >= dil)
        rows_c = pl.ds(pl.multiple_of(b * DIL_NK, DIL_NK), DIL_NK)
        rows_p = pl.ds(pl.multiple_of(jnp.maximum(b - dil, 0) * DIL_NK, DIL_NK), DIL_NK)
        hm = _head_masks()
        heads = _pair_heads()
        ln_ = [_pair_lanes(pair) for pair, _ in heads]
        qh = [_zero_other_head(hm[hh], q_ref[:, ln_[i]]) for i, (_, hh) in enumerate(heads)]
        doh = [_zero_other_head(hm[hh], do_ref[:, ln_[i]]) for i, (_, hh) in enumerate(heads)]
        idx = range(len(heads))
        s_p = [_dot(qh[i], kp_ref[:, ln_[i]], NT) for i in idx]
        s_c = [_dot(qh[i], kc_ref[:, ln_[i]], NT) for i in idx]
        dp_p = [_dot(doh[i], vp_ref[:, ln_[i]], NT) for i in idx]
        dp_c = [_dot(doh[i], vc_ref[:, ln_[i]], NT) for i in idx]
        zero = lambda: [jnp.zeros((DIL_NK, LANES), F32) for _ in range(4)]
        dq, dk_p, dk_c, dv_p, dv_c = zero(), zero(), zero(), zero(), zero()
        for i, (pair, hh) in enumerate(heads):
            lse_h, d_h = _row_scalar(l_ref[:, ln_[i]], hm[hh]), _row_scalar(d_ref[:, ln_[i]], hm[hh])
            pp = jnp.exp(jnp.where(mprev, s_p[i] - lse_h, NEG))
            pc = jnp.exp(jnp.where(mcur, s_c[i] - lse_h, NEG))
            dsp, dsc = pp * (dp_p[i] - d_h), pc * (dp_c[i] - d_h)
            dq[pair] = (dq[pair] + _dot(dsp, _zero_other_head(hm[hh], kp_ref[:, ln_[i]]), NN)
                        + _dot(dsc, _zero_other_head(hm[hh], kc_ref[:, ln_[i]]), NN))
            dv_p[pair] = dv_p[pair] + _dot(pp, doh[i], TN)
            dv_c[pair] = dv_c[pair] + _dot(pc, doh[i], TN)
            dk_p[pair] = dk_p[pair] + _dot(dsp, qh[i], TN)
            dk_c[pair] = dk_c[pair] + _dot(dsc, qh[i], TN)
        for pair in range(4):
            lanes = _pair_lanes(pair)
            dq_ref[:, lanes] = dq[pair]
            dk_ref[rows_p, lanes] += dk_p[pair]
            dv_ref[rows_p, lanes] += dv_p[pair]
            dk_ref[rows_c, lanes] += dk_c[pair]
            dv_ref[rows_c, lanes] += dv_c[pair]

    cur = pl.BlockSpec((DIL_NK, 512), lambda b: (b, 0))
    prev = pl.BlockSpec((DIL_NK, 512), lambda b: (jnp.maximum(b - dil, 0), 0))
    whole = pl.BlockSpec((t, 512), lambda b: (0, 0))
    sds = jax.ShapeDtypeStruct
    return pl.pallas_call(
        body,
        name=name,
        grid=(nblk,),
        in_specs=[cur, cur, cur, cur, prev, cur, prev, cur],
        out_specs=[cur, whole, whole],
        out_shape=[sds((t, 512), F32)] * 3,
        compiler_params=_cparams(("arbitrary",)),
    )(q, do, lse, dd, k, k, v, v)


def dil_combine(os_, lses, z):
    t = z.shape[0]

    def body(o1_ref, l1_ref, o2_ref, l2_ref, o3_ref, l3_ref, g_ref, y_ref, o_ref, lse_ref, yt_ref, so2, sl2, so3, sl3):
        for (window, dil), o_in, l_in, so, sl in ((DIL_GROUPS[1], o2_ref, l2_ref, so2, sl2), (DIL_GROUPS[2], o3_ref, l3_ref, so3, sl3)):
            for u in range(SPAN // DIL_NK):
                src, dst = _rm_src(u, window, dil), _rm_dst(u)
                so[src, :] = o_in[dst, :]
                sl[src, :] = l_in[dst, :]
        for c0 in range(0, SPAN, 256):
            rows = slice(c0, c0 + 256)
            la, lb, lc = l1_ref[rows, :], sl2[rows, :], sl3[rows, :]
            mx = jnp.maximum(jnp.maximum(la, lb), lc)
            wa, wb, wc = jnp.exp(la - mx), jnp.exp(lb - mx), jnp.exp(lc - mx)
            tot = wa + wb + wc
            o = (wa * o1_ref[rows, :] + wb * so2[rows, :] + wc * so3[rows, :]) / tot
            y = o * _silu_and_grad(g_ref[rows, :])[0]
            y_ref[rows, :] = y.astype(MXU_DTYPE)
            o_ref[rows, :] = o
            lse_ref[rows, :] = mx + jnp.log(tot)
            yt_ref[:, rows] = y.T.astype(MXU_DTYPE)

    sds = jax.ShapeDtypeStruct
    return pl.pallas_call(
        body,
        name="dil_combine",
        grid=(t // SPAN, 4),
        in_specs=[_span_blk(0)] * 6 + [_span_blk(36)],
        out_specs=[_span_blk(0)] * 3 + [pl.BlockSpec((LANES, SPAN), lambda s, p: (p, s))],
        out_shape=[sds((t, 512), MXU_DTYPE), sds((t, 512), F32), sds((t, 512), F32), sds((512, t), MXU_DTYPE)],
        scratch_shapes=[pltpu.VMEM((SPAN, LANES), F32)] * 4,
        compiler_params=_cparams(("parallel", "parallel")),
    )(os_[0], lses[0], os_[1], lses[1], os_[2], lses[2], z)


def dil_gate_bwd(dy, o, z, lse):
    t = dy.shape[0]

    def body(dy_ref, o_ref, g_ref, lse_ref, dzg_ref, do1_ref, dd1_ref, do2_ref, dd2_ref, l2_ref, do3_ref, dd3_ref, l3_ref, do_scr):
        for c0 in range(0, SPAN, 256):
            rows = slice(c0, c0 + 256)
            dyv, ov = dy_ref[rows, :], o_ref[rows, :]
            silu, dsilu = _silu_and_grad(g_ref[rows, :])
            do = dyv * silu
            do_scr[rows, :] = do
            do1_ref[rows, :] = do.astype(MXU_DTYPE)
            dd1_ref[rows, :] = _seg_sum_impl(do * ov)
            dzg_ref[rows, :] = (dyv * ov * dsilu).astype(MXU_DTYPE)
        for (window, dil), do_o, dd_o, l_o in ((DIL_GROUPS[1], do2_ref, dd2_ref, l2_ref), (DIL_GROUPS[2], do3_ref, dd3_ref, l3_ref)):
            for u in range(SPAN // DIL_NK):
                src, dst = _rm_src(u, window, dil), _rm_dst(u)
                do_o[dst, :] = do_scr[src, :].astype(MXU_DTYPE)
                dd_o[dst, :] = dd1_ref[src, :]
                l_o[dst, :] = lse_ref[src, :]

    sds = jax.ShapeDtypeStruct
    f32, mxu = sds((t, 512), F32), sds((t, 512), MXU_DTYPE)
    return pl.pallas_call(
        body,
        name="dil_gate_bwd",
        grid=(t // SPAN, 4),
        in_specs=[_span_blk(0), _span_blk(0), _span_blk(36), _span_blk(0)],
        out_specs=[_span_blk(0)] * 9,
        out_shape=[mxu, mxu, f32, mxu, f32, f32, mxu, f32, f32],
        scratch_shapes=[pltpu.VMEM((SPAN, LANES), F32)],
        compiler_params=_cparams(("parallel", "parallel")),
    )(dy, o, z, lse)


def _merge_tile(p0, p1, p2, z0, z1, z2, b0, b1, b2):
    return _sigmoid(z0 + b0) * p0 + _sigmoid(z1 + b1) * p1 + _sigmoid(z2 + b2) * p2


def _merge_ins(ps, z, b):
    w = 256
    return ([_row(p, w, 0, True) for p in ps] + [_row(z, w, 4 * i, True) for i in range(3)]
            + [_const(b, w, 4 * i, True) for i in range(3)])


def merge_fwd(ps, z, b):
    def fn(ins, outs, _):
        merged = _merge_tile(*[r[...] for r in ins])
        outs[0][...] = merged.astype(MXU_DTYPE)
        outs[1][...] = merged.T.astype(MXU_DTYPE)

    return rowwise("merge_fwd", fn, z.shape[0], tm=1024, ncol=4, ins=_merge_ins(ps, z, b), outs=[(1024, 256, MXU_DTYPE, True)],
                   touts=[(1024, 256, MXU_DTYPE, True)])


def merge_bwd(dm, ps, z, b):
    def fn(ins, outs, accs):
        _, vjp = jax.vjp(_merge_tile, *[r[...] for r in ins[:9]])
        grads = vjp(ins[9][...])
        for i in range(3):
            outs[i][...] = grads[i].astype(MXU_DTYPE)
            outs[3 + i][...] = grads[3 + i].astype(MXU_DTYPE)
            accs[i][...] += grads[6 + i]

    return rowwise("merge_bwd", fn, z.shape[0], tm=1024, ncol=4, ins=_merge_ins(ps, z, b) + [_row(dm, 256, 0, True)],
                   outs=[(1024, 256, MXU_DTYPE, True)] * 6, accs=[(1, 1024, 256, True)] * 3)


EW_BLOCK_BYTES = 2**21


def _tile2d(r, c):
    if r * c * 4 <= EW_BLOCK_BYTES:
        return r, c
    for tr in (512, 256, 128, 64, 32):
        if r % tr == 0 and tr * c * 4 <= EW_BLOCK_BYTES:
            return tr, c
    for tc in (1024, 512, 256, 128):
        if c % tc == 0 and r * tc * 4 <= EW_BLOCK_BYTES:
            return r, tc
    raise ValueError((r, c))


def adamw(name, w, g, m, v):
    shape = w.shape
    c = shape[-1]
    r = w.size // c
    tr, tc = _tile2d(r, c)
    assert tc == c
    c1, c2 = 1.0 - ADAM_B1 ** ADAM_STEP, 1.0 - ADAM_B2 ** ADAM_STEP

    def body(w_ref, g_ref, m_ref, v_ref, d_ref, mo_ref, vo_ref):
        gv = g_ref[...]
        mn = ADAM_B1 * m_ref[...] + (1.0 - ADAM_B1) * gv
        vn = ADAM_B2 * v_ref[...] + (1.0 - ADAM_B2) * (gv * gv)
        d_ref[...] = -ADAM_LR * ((mn / c1) / (jnp.sqrt(vn / c2) + ADAM_EPS) + ADAM_WD * w_ref[...])
        mo_ref[...] = mn
        vo_ref[...] = vn

    spec = pl.BlockSpec((tr, c), lambda i: (i, 0))
    outs = pl.pallas_call(
        body,
        name=name,
        grid=(r // tr,),
        in_specs=[spec] * 4,
        out_specs=[spec] * 3,
        out_shape=[jax.ShapeDtypeStruct((r, c), F32)] * 3,
        compiler_params=_cparams(("parallel",)),
    )(*[a.reshape(r, c) for a in (w, g, m, v)])
    return [o.reshape(shape) for o in outs]


def adamw_pair(name, w, reduced, received, m, v, c_idx):
    _, r, c = w.shape
    tr, tc = _tile2d(r, c)
    c1, c2 = 1.0 - ADAM_B1 ** ADAM_STEP, 1.0 - ADAM_B2 ** ADAM_STEP

    def body(c_ref, w_ref, r0_ref, x0_ref, r1_ref, x1_ref, m_ref, v_ref, g_ref, d_ref, mo_ref, vo_ref):
        mine = c_ref[0]
        gv = jnp.where(pl.program_id(0) == 0, jnp.where(mine == 0, r0_ref[...], x0_ref[...]),
                       jnp.where(mine == 1, r1_ref[...], x1_ref[...]))
        mn = ADAM_B1 * m_ref[...] + (1.0 - ADAM_B1) * gv
        vn = ADAM_B2 * v_ref[...] + (1.0 - ADAM_B2) * (gv * gv)
        g_ref[...] = gv
        d_ref[...] = -ADAM_LR * ((mn / c1) / (jnp.sqrt(vn / c2) + ADAM_EPS) + ADAM_WD * w_ref[...])
        mo_ref[...] = mn
        vo_ref[...] = vn

    full = pl.BlockSpec((None, tr, tc), lambda h, i, j, cr: (h, i, j))
    half = pl.BlockSpec((tr, tc), lambda h, i, j, cr: (i, j))
    return pl.pallas_call(
        body,
        name=name,
        grid_spec=pltpu.PrefetchScalarGridSpec(
            num_scalar_prefetch=1,
            grid=(2, r // tr, c // tc),
            in_specs=[full, half, half, half, half, full, full],
            out_specs=[full] * 4,
        ),
        out_shape=[jax.ShapeDtypeStruct((2, r, c), F32)] * 4,
        compiler_params=_cparams(("parallel", "parallel", "parallel")),
    )(c_idx, w, reduced[0], received[0], reduced[1], received[1], m, v)


def sum_pair(name, g, la, out_dtype):
    _, r, c = g.shape
    tr, tc = _tile2d(r, c)

    def body(g_ref, la_ref, o_ref):
        o_ref[...] = (g_ref[...] + la_ref[...]).astype(o_ref.dtype)

    spec = pl.BlockSpec((None, tr, tc), lambda s, i, j: (s, i, j))
    return pl.pallas_call(
        body,
        name=name,
        grid=(4, r // tr, c // tc),
        in_specs=[spec, spec],
        out_specs=spec,
        out_shape=jax.ShapeDtypeStruct((4, r, c), out_dtype),
        compiler_params=_cparams(("parallel", "parallel", "parallel")),
    )(g, la)


def sum_chips(name, g, la, lb, s_idx):
    _, r, c = g.shape
    tr, tc = _tile2d(r, c)

    def body(s_ref, g_ref, la_ref, l0_ref, l1_ref, l2_ref, o_ref):
        own = g_ref[...] + la_ref[...]
        o_ref[...] = ((own + l0_ref[...].astype(F32)) + l1_ref[...].astype(F32)) + l2_ref[...].astype(F32)

    own_spec = pl.BlockSpec((None, tr, tc), lambda i, j, sr: (sr[0], i, j))
    lspec = lambda k: pl.BlockSpec((None, tr, tc), lambda i, j, sr: (k, i, j))
    return pl.pallas_call(
        body,
        name=name,
        grid_spec=pltpu.PrefetchScalarGridSpec(
            num_scalar_prefetch=1,
            grid=(r // tr, c // tc),
            in_specs=[own_spec, own_spec, lspec(0), lspec(1), lspec(2)],
            out_specs=pl.BlockSpec((tr, tc), lambda i, j, sr: (i, j)),
        ),
        out_shape=jax.ShapeDtypeStruct((r, c), F32),
        compiler_params=_cparams(("parallel", "parallel")),
    )(s_idx, g, la, lb, lb, lb)


def _place():
    x, y, c = lax.axis_index("x"), lax.axis_index("y"), lax.axis_index("c")
    chips = [(1 - x, y), (x, 1 - y), (1 - x, 1 - y)]
    return x, y, c, chips


HBM_SPEC = pl.BlockSpec(memory_space=pltpu.HBM)


def _comm_call(name, body, ins, out_shapes, n_sem, n_local):
    return pl.pallas_call(
        body,
        name=name,
        in_specs=[HBM_SPEC] * len(ins),
        out_specs=[HBM_SPEC] * len(out_shapes),
        out_shape=out_shapes,
        scratch_shapes=[pltpu.SemaphoreType.DMA((n_sem,)), pltpu.SemaphoreType.DMA((n_sem,)),
                        pltpu.SemaphoreType.DMA((max(n_local, 1),))],
    )(*ins)


def _layer_gather(ins, outs, send, recv, layer):
    n = len(ins)
    x, y, c, chips = _place()
    s = 2 * x + y
    sib = (x, y, 1 - c)
    active = c == layer

    def rc(a, k, src, dst, dev):
        return pltpu.make_async_remote_copy(src_ref=src, dst_ref=dst, send_sem=send.at[6 * a + k], recv_sem=recv.at[6 * a + k],
                                            device_id=dev, device_id_type=MESH)

    def first_hop():
        return [rc(a, j, ins[a], outs[a].at[s], (*chip, c)) for j, chip in enumerate(chips) for a in range(n)]

    def start():
        @pl.when(active)
        def _():
            for cp in first_hop():
                cp.start()

    def second_hop():
        return [rc(a, 3 + j, outs[a].at[2 * cx + cy], outs[a].at[2 * cx + cy], sib)
                for j, (cx, cy) in enumerate(chips) for a in range(n)]

    def forward():
        @pl.when(active)
        def _():
            for j, (cx, cy) in enumerate(chips):
                for a in range(n):
                    landed = outs[a].at[2 * cx + cy]
                    rc(a, j, landed, landed, sib).wait_recv()
                    rc(a, 3 + j, landed, landed, sib).start()

    def finish():
        @pl.when(active)
        def _():
            for cp in first_hop() + second_hop():
                cp.wait_send()

        @pl.when(jnp.logical_not(active))
        def _():
            for cp in second_hop():
                cp.wait_recv()

    return start, forward, finish


def _gather_shapes(ws):
    return [jax.ShapeDtypeStruct((4,) + w.shape, w.dtype) for w in ws]


def gather_hook(ws, layer):
    def make(i, o, s, r):
        start, forward, finish = _layer_gather(i, o, s, r, layer)

        def forward_and_finish():
            forward()
            finish()

        return start, forward_and_finish

    return (ws, _gather_shapes(ws), 6 * len(ws), make)


def allgather_layer(ws, layer):
    n = len(ws)

    def body(*refs):
        send, recv, _ = refs[2 * n:]
        start, forward, finish = _layer_gather(refs[:n], refs[n:2 * n], send, recv, layer)
        start()
        forward()
        finish()

    return _comm_call(f"allgather_layer{layer}", body, ws, _gather_shapes(ws), 6 * n, 0)


def exchange_sibling(gs, layer, tag=""):
    n = len(gs)

    def body(*refs):
        send, recv, _ = refs[2 * n:]
        start, finish = _sibling_exchange(refs[:n], refs[n:2 * n], send, recv, layer)
        start()
        finish()

    return _comm_call(f"exchange_sibling{layer}{tag}", body, gs, [jax.ShapeDtypeStruct(g.shape, g.dtype) for g in gs], n, 0)


def _sibling_exchange(ins, outs, send, recv, layer):
    x, y, c, _ = _place()

    def copies():
        return [pltpu.make_async_remote_copy(src_ref=ins[a], dst_ref=outs[a], send_sem=send.at[a], recv_sem=recv.at[a],
                                             device_id=(x, y, 1 - c), device_id_type=MESH) for a in range(len(ins))]

    def start():
        @pl.when(c != layer)
        def _():
            for cp in copies():
                cp.start()

    def finish():
        @pl.when(c != layer)
        def _():
            for cp in copies():
                cp.wait_send()

        @pl.when(c == layer)
        def _():
            for cp in copies():
                cp.wait_recv()

    return start, finish


def sibling_hook(gs, layer):
    return (gs, [jax.ShapeDtypeStruct(g.shape, g.dtype) for g in gs], len(gs),
            lambda i, o, s, r: _sibling_exchange(i, o, s, r, layer))


def chips_hook(ps, layer):
    return (ps, _chips_shapes(ps), 3 * len(ps), lambda i, o, s, r: _chips_exchange(i, o, s, r, layer))


def _chips_exchange(ins, outs, send, recv, layer):
    n = len(ins)
    _, _, c, chips = _place()

    def copies():
        return [pltpu.make_async_remote_copy(src_ref=ins[a].at[2 * cx + cy], dst_ref=outs[a].at[j], send_sem=send.at[3 * a + j],
                                             recv_sem=recv.at[3 * a + j], device_id=(cx, cy, c), device_id_type=MESH)
                for j, (cx, cy) in enumerate(chips) for a in range(n)]

    def start():
        @pl.when(c == layer)
        def _():
            for cp in copies():
                cp.start()

    def finish():
        @pl.when(c == layer)
        def _():
            for cp in copies():
                cp.wait()

    return start, finish


def _chips_shapes(ps):
    return [jax.ShapeDtypeStruct((3,) + p.shape[1:], p.dtype) for p in ps]


def exchange_chips(ps, layer, tag=""):
    n = len(ps)

    def body(*refs):
        send, recv, _ = refs[2 * n:]
        start, finish = _chips_exchange(refs[:n], refs[n:2 * n], send, recv, layer)
        start()
        finish()

    return _comm_call(f"exchange_chips{layer}{tag}", body, ps, _chips_shapes(ps), 3 * n, 0)


def exchange_final(rs, layer, small=None):
    n = len(rs)
    ins_all = list(rs) + ([small] if small is not None else [])

    def body(*refs):
        ni = len(ins_all)
        ins, outs = refs[:ni], refs[ni:2 * ni]
        send, recv, lsem = refs[2 * ni:]
        x, y, c, _ = _place()
        s = 2 * x + y
        to_sib = [pltpu.make_async_remote_copy(src_ref=ins[a], dst_ref=outs[a], send_sem=send.at[a], recv_sem=recv.at[a],
                                               device_id=(x, y, 1 - c), device_id_type=MESH) for a in range(n)]
        rel = [(fx, fy) for fx in (0, 1) for fy in (0, 1)]

        def piece(fx, fy, cc, dst_slot):
            k = n + 2 * (2 * fx + fy) + cc
            return pltpu.make_async_remote_copy(src_ref=ins[n], dst_ref=outs[n].at[dst_slot], send_sem=send.at[k], recv_sem=recv.at[k],
                                                device_id=(x ^ fx, y ^ fy, cc), device_id_type=MESH)

        @pl.when(c == layer)
        def _():
            for cp in to_sib:
                cp.start()
            if small is not None:
                own = pltpu.make_async_copy(ins[n], outs[n].at[s], lsem.at[0])
                own.start()
                sends = [piece(fx, fy, cc, s) for fx, fy in rel for cc in (0, 1) if (fx, fy) != (0, 0) or cc != layer]
                for cp in sends:
                    cp.start()
                for fx, fy in rel[1:]:
                    piece(fx, fy, layer, 2 * (x ^ fx) + (y ^ fy)).wait_recv()
                for cp in sends:
                    cp.wait_send()
                own.wait()
            for cp in to_sib:
                cp.wait_send()

        @pl.when(c != layer)
        def _():
            for cp in to_sib:
                cp.wait_recv()
            if small is not None:
                for fx, fy in rel:
                    piece(fx, fy, 1 - layer, 2 * (x ^ fx) + (y ^ fy)).wait_recv()

    out_shapes = [jax.ShapeDtypeStruct(r.shape, r.dtype) for r in rs]
    if small is not None:
        out_shapes.append(jax.ShapeDtypeStruct((4,) + small.shape, small.dtype))
    res = _comm_call(f"exchange_final{layer}", body, ins_all, out_shapes, n + 8, 1)
    return (res[:n], res[n]) if small is not None else (res[:n], None)


def _pad_in_cols(w):
    z = lambda n: jnp.zeros(w.shape[:-1] + (n,), w.dtype)
    return jnp.concatenate([w[..., :KR_OFF], z(64), w[..., KR_OFF:KR_OFF + MLA_ROPE], z(32), w[..., KR_OFF + MLA_ROPE:]], axis=-1)


def _pad_heads(w, real):
    k = w.shape[0]
    return jnp.pad(w.reshape(k, MLA_HEADS, real), ((0, 0), (0, 0), (0, LANES - real))).reshape(k, MLA_HEADS * LANES)


def _pad_gain(g, real):
    return jnp.pad(g.reshape(1, real), ((0, 0), (0, LANES - real)))


def layer_weights(full):
    w_ukv = full["w_ukv"].reshape(128, MLA_HEADS, 2, 64)
    two = lambda g: jnp.concatenate([g, g]).reshape(1, LANES)
    return dict(
        norm_g=full["norm_g"].reshape(1, -1), w_in=full["w_in"], conv_w=full["conv_w"], conv_b=full["conv_b"].reshape(1, -1),
        wgx=full["w_gate_x"], bgx=full["b_gate_x"].reshape(LRU_BLOCKS, 1, LANES),
        wga=full["w_gate_a"], bga=full["b_gate_a"].reshape(LRU_BLOCKS, 1, LANES),
        lam=full["lru_lambda"].reshape(1, -1), w_lru_o=full["w_lru_o"],
        cq_norm_g=full["cq_norm_g"].reshape(1, -1), ckv_norm_g=full["ckv_norm_g"].reshape(1, -1),
        wq=_pad_heads(full["w_uq"], MLA_QK), wk=_pad_heads(w_ukv[:, :, 0].reshape(128, 512), 64),
        wv=w_ukv[:, :, 1].reshape(128, 512),
        gq=_pad_gain(full["mla_q_norm_g"], MLA_QK), gk=_pad_gain(full["mla_k_norm_g"], MLA_QK),
        w_mla_o=full["w_mla_o"], gq2=two(full["dil_q_norm_g"]), gk2=two(full["dil_k_norm_g"]),
        w_dil_o=full["w_dil_o"], b_merge=full["b_merge"].reshape(1, -1), w_out=full["w_out"],
    )


def layer_fwd(x, w, tabs, gather=None):
    tabs, dtabs = tabs
    h, ht = rmsnorm_fwd(x, w["norm_g"])
    z_lru = mm_nn("in_proj_lru", h, w["w_in"], n=2048, n_off=G_LRU, tm=1024)
    z_mla = mm_nn("in_proj_mla", h, w["w_in"], n=1024, n_off=G_MLA, tm=1024)
    z_dil = mm_nn("in_proj_dil", h, w["w_in"], n=5120, n_off=G_DIL, tm=1024)
    z_mrg = mm_nn("in_proj_mrg", h, w["w_in"], n=3072, n_off=G_MRG, tm=1024)
    rest_hook = None if gather is None else gather_hook(gather[0][1:], gather[1])
    hs, y_lru, yt_lru, *ga_rest = lru_fwd(z_lru, w["conv_w"], w["conv_b"], w["wgx"], w["bgx"], w["wga"], w["bga"], w["lam"],
                                          comm=rest_hook)
    qm, km, vm, vtm = mla_prep_fwd(z_mla, tabs, w)
    o_mla, y_mla, yt_mla, lse_mla, *ga_first = mla_attn_fwd(qm, km, vtm, z_mla,
                                                            gather=None if gather is None else (gather[0][:1], gather[1]))
    gathered = ga_first + ga_rest
    os_, lses, dil_rm = [], [], []
    for gi, (_, dil) in enumerate(DIL_GROUPS):
        qkv = dil_prep_fwd(z_dil, dtabs[gi], w["gq2"], w["gk2"], gi)
        o, lse = dil_attn_fwd(f"dil_attn_fwd_{dil}", *qkv, dil)
        os_.append(o)
        lses.append(lse)
        dil_rm.append(qkv)
    y_dil, o_dil, lse_dil, yt_dil = dil_combine(os_, lses, z_dil)
    ps = [mm_nn("proj_lru", y_lru, w["w_lru_o"]), mm_nn("proj_mla", y_mla, w["w_mla_o"]), mm_nn("proj_dil", y_dil, w["w_dil_o"])]
    merged, merged_t = merge_fwd(ps, z_mrg, w["b_merge"])
    out = mm_nn("out_proj", merged, w["w_out"], add=x)
    res = dict(x=x, ht=ht, z_lru=z_lru, z_mla=z_mla, z_dil=z_dil, z_mrg=z_mrg, hs=hs, yt_lru=yt_lru, qm=qm, km=km, vm=vm, o_mla=o_mla,
               yt_mla=yt_mla, lse_mla=lse_mla, dil_rm=dil_rm, yt_dil=yt_dil, o_dil=o_dil, lse_dil=lse_dil, ps=ps, merged_t=merged_t)
    return out, res, gathered


def layer_bwd(dout, r, w, tabs, prev=None, own=None):
    tabs, dtabs = tabs
    g = {}
    dmerged = mm_nt("out_proj_dx", dout, w["w_out"])
    g["w_out"] = mm_nn("out_proj_dw", r["merged_t"], dout, tm=1024, tn=512, tk=2048)
    dp0, dp1, dp2, dzm0, dzm1, dzm2, db0, db1, db2 = merge_bwd(dmerged, r["ps"], r["z_mrg"], w["b_merge"])
    g["b_merge"] = jnp.concatenate([db0, db1, db2], axis=1).reshape(-1)
    dy_lru = mm_nt("proj_lru_dx", dp0, w["w_lru_o"])
    dy_mla = mm_nt("proj_mla_dx", dp1, w["w_mla_o"])
    dy_dil = mm_nt("proj_dil_dx", dp2, w["w_dil_o"])
    g["w_lru_o"] = mm_nn("proj_lru_dw", r["yt_lru"], dp0, **DW_TILES)
    g["w_mla_o"] = mm_nn("proj_mla_dw", r["yt_mla"], dp1, **DW_TILES)
    g["w_dil_o"] = mm_nn("proj_dil_dw", r["yt_dil"], dp2, **DW_TILES)
    dzx, dzg_lru, dwgx, dbgx, dwga, dbga, dlam, dcw, dcb, *la_prev = lru_bwd(
        r["z_lru"], r["hs"], dy_lru, w["conv_w"], w["conv_b"], w["wgx"], w["bgx"], w["wga"], w["bga"], w["lam"],
        comm=None if prev is None else sibling_hook(prev[0], prev[1]))
    chips = None if prev is None else (prev[2](la_prev), prev[1])
    g.update(w_gate_x=dwgx, b_gate_x=dbgx.reshape(LRU_BLOCKS, LANES), w_gate_a=dwga, b_gate_a=dbga.reshape(LRU_BLOCKS, LANES),
             lru_lambda=dlam.reshape(-1), conv_w=dcw, conv_b=dcb.reshape(-1))
    do_m, dzg_mla, dd_m = mla_gate_bwd(dy_mla, r["o_mla"], r["z_mla"])
    dq_m, dk_m, dv_m, *lb_prev = mla_attn_bwd(r["qm"], r["km"], r["vm"], do_m, r["lse_mla"], dd_m, chips=chips)
    dz_mla3, dg_cq, dg_ckv, dwq, dwk, dwv, dgq, dgk = mla_prep_bwd(r["z_mla"], tabs, w, dq_m, dk_m, dv_m)
    g.update(cq_norm_g=dg_cq.reshape(-1), ckv_norm_g=dg_ckv.reshape(-1), mla_q_norm_g=dgq[0, :MLA_QK], mla_k_norm_g=dgk[0, :MLA_QK])
    g["w_uq"] = dwq.reshape(256, MLA_HEADS, LANES)[:, :, :MLA_QK].reshape(256, MLA_HEADS * MLA_QK)
    g["w_ukv"] = jnp.concatenate([dwk.reshape(128, MLA_HEADS, LANES)[:, :, :64], dwv.reshape(128, MLA_HEADS, 64)], axis=2).reshape(128, 1024)
    dzg_dil, do1, dd1, do2, dd2, l2, do3, dd3, l3 = dil_gate_bwd(dy_dil, r["o_dil"], r["z_dil"], r["lse_dil"])
    stats = [(do1, r["lse_dil"], dd1), (do2, l2, dd2), (do3, l3, dd3)]
    dzq, dzk, dzv, dgq2, dgk2 = [], [], [], [], []
    for gi, (_, dil) in enumerate(DIL_GROUPS):
        dq, dk, dv = dil_attn_bwd(f"dil_attn_bwd_{dil}", *r["dil_rm"][gi], *stats[gi], dil)
        parts = dil_prep_bwd(r["z_dil"], dtabs[gi], w["gq2"], w["gk2"], dq, dk, dv, gi)
        for acc, part in zip((dzq, dzk, dzv, dgq2, dgk2), parts):
            acc.append(part)
    g.update(dil_q_norm_g=sum(dgq2)[0, :DIL_HD], dil_k_norm_g=sum(dgk2)[0, :DIL_HD])
    dz = jnp.concatenate([dzx, dzg_lru, dz_mla3, dzg_mla] + dzq + dzk + dzv + [dzg_dil, dzm0, dzm1, dzm2], axis=1)
    g["w_in"] = in_proj_dw(r["ht"], dz)
    own_hook = None if own is None else chips_hook(*own(g))
    out = mm_nt("in_proj_dx", dz, w["w_in"], tm=1024, tn=1024, tk=IN_PAD // 4, comm=own_hook)
    dh, lb_own = (out, []) if own_hook is None else (out[0], out[1:])
    dx, dng = rmsnorm_bwd(r["x"], dh, dout, w["norm_g"])
    g["norm_g"] = dng.reshape(-1)
    return dx, g, dict(la_prev=la_prev, lb_prev=lb_prev, lb_own=lb_own)


def local_step(x, positions, target, full0, full1=None, gather1=None, reduce=None):
    tabs = rope_tables(positions.reshape(-1, 1))
    tabs = (tabs, [dil_tables(tabs, gi) for gi in range(len(DIL_GROUPS))])
    ws, ress = [], []
    for l in range(2):
        if l == 0:
            ws.append(layer_weights(full0))
            x, res, gathered = layer_fwd(x, ws[0], tabs, gather=None if gather1 is None else (gather1[0], 1))
        else:
            ws.append(layer_weights(full1 if gather1 is None else gather1[1](gathered)))
            x, res, _ = layer_fwd(x, ws[1], tabs)
        ress.append(res)
    dy, loss = loss_head(x, target)
    dy, grads1, _ = layer_bwd(dy, ress[1], ws[1], tabs)
    prev1, own0 = (None, None) if reduce is None else (reduce[0](grads1), reduce[1])
    dy, grads0, landed = layer_bwd(dy, ress[0], ws[0], tabs, prev=prev1, own=own0)
    return loss, dy, [grads0, grads1], landed


WEIGHTS = ["norm_g", "w_in", "conv_w", "conv_b", "w_gate_x", "b_gate_x", "w_gate_a", "b_gate_a", "lru_lambda", "w_lru_o", "cq_norm_g",
           "ckv_norm_g", "w_uq", "w_ukv", "mla_q_norm_g", "mla_k_norm_g", "w_mla_o", "dil_q_norm_g", "dil_k_norm_g", "w_dil_o", "b_merge",
           "w_out"]
SHARDED = {"w_in": 2, "conv_w": 2, "w_lru_o": 1, "w_uq": 2, "w_ukv": 2, "w_mla_o": 2, "w_dil_o": 2, "w_out": 1}
REPLICATED = [n for n in WEIGHTS if n not in SHARDED]
SMALL_ROWS = 144


def kernel(x, positions, norm_g, w_in, conv_w, conv_b, w_gate_x, b_gate_x, w_gate_a, b_gate_a, lru_lambda, w_lru_o, cq_norm_g, ckv_norm_g, w_uq, w_ukv, mla_q_norm_g, mla_k_norm_g, w_mla_o, dil_q_norm_g, dil_k_norm_g, w_dil_o, b_merge, w_out, loss_target, m_norm_g, m_w_in, m_conv_w, m_conv_b, m_w_gate_x, m_b_gate_x, m_w_gate_a, m_b_gate_a, m_lru_lambda, m_w_lru_o, m_cq_norm_g, m_ckv_norm_g, m_w_uq, m_w_ukv, m_mla_q_norm_g, m_mla_k_norm_g, m_w_mla_o, m_dil_q_norm_g, m_dil_k_norm_g, m_w_dil_o, m_b_merge, m_w_out, v_norm_g, v_w_in, v_conv_w, v_conv_b, v_w_gate_x, v_b_gate_x, v_w_gate_a, v_b_gate_a, v_lru_lambda, v_w_lru_o, v_cq_norm_g, v_ckv_norm_g, v_w_uq, v_w_ukv, v_mla_q_norm_g, v_mla_k_norm_g, v_w_mla_o, v_dil_q_norm_g, v_dil_k_norm_g, v_w_dil_o, v_b_merge, v_w_out):
    args = locals()
    w = {n: args[n] for n in WEIGHTS}
    m = {n: args["m_" + n] for n in WEIGHTS}
    v = {n: args["v_" + n] for n in WEIGHTS}
    my_c = lax.axis_index("c").astype(jnp.int32)
    my_s = (2 * lax.axis_index("x") + lax.axis_index("y")).astype(jnp.int32)
    c_idx = my_c.reshape(1)
    s_idx = my_s.reshape(1)

    names = list(SHARDED)
    wire = [[w[n][l] if n == "conv_w" else w[n][l].astype(BF16) for n in names] for l in range(2)]

    def assemble(l, gathered):
        full = {n: w[n][l] for n in REPLICATED}
        for n, ga, own in zip(names, gathered, wire[l]):
            parts = [jnp.where(my_s == s, own, ga[s]) for s in range(4)]
            if n == "w_in":
                parts = [_pad_in_cols(parts[0])] + parts[1:]
            full[n] = jnp.concatenate(parts, axis=SHARDED[n] - 1)
        return full

    full0 = assemble(0, allgather_layer(wire[0], 0))

    def shards_of(grads_l):
        gs = []
        for n in names:
            g_ = grads_l[n]
            if n == "w_in":
                gs.append(g_.reshape(4, IN_WIDTH // 4, D_MODEL))
            else:
                parts = jnp.stack(jnp.split(g_, 4, axis=SHARDED[n] - 1))
                gs.append(parts.reshape(4, -1, parts.shape[-1]))
        return gs

    def pair_sums(l, gs, la, tags):
        dts = [BF16 if g_.size >= 2**19 and t != "small" else F32 for g_, t in zip(gs, tags)]
        return [sum_pair(f"sum_pair{l}_{t}", g_, l_, dt) for t, g_, l_, dt in zip(tags, gs, la, dts)]

    def chip_sums(l, gs, la, lb, tags):
        return [sum_chips(f"sum_chips{l}_{t}", g_, l_, b_, s_idx) for t, g_, l_, b_ in zip(tags, gs, la, lb)]

    stash = {}

    def prev1(grads1):
        stash["gs1"] = shards_of(grads1)
        return stash["gs1"], 1, lambda la: pair_sums(1, stash["gs1"], la, names)

    def own0(grads0):
        stash["gs0"] = shards_of(grads0)
        stash["la0"] = exchange_sibling(stash["gs0"], 0)
        return pair_sums(0, stash["gs0"], stash["la0"], names), 0

    loss, grad_x, grads, landed = local_step(x[0], positions[0], loss_target[0], full0,
                                             gather1=(wire[1], functools.partial(assemble, 1)), reduce=(prev1, own0))
    loss = lax.psum(loss, ("x", "y", "c"))
    reduced1 = chip_sums(1, stash["gs1"], landed["la_prev"], landed["lb_prev"], names)
    received1, _ = exchange_final(reduced1, 1)
    reduced0 = chip_sums(0, stash["gs0"], stash["la0"], landed["lb_own"], names)

    flat = jnp.concatenate([jnp.stack([grads[0][n], grads[1][n]]).reshape(-1) for n in REPLICATED])
    small = [jnp.pad(flat, (0, 4 * SMALL_ROWS * 1024 - flat.size)).reshape(4, SMALL_ROWS, 1024)]
    la_s = exchange_sibling(small, 0, tag="_small")
    lb_s = exchange_chips(pair_sums(0, small, la_s, ["small"]), 0, tag="_small")
    received0, small_all = exchange_final(reduced0, 0, small=chip_sums(0, small, la_s, lb_s, ["small"])[0])

    g_local, delta, new_m, new_v = {}, {}, {}, {}
    for i, n in enumerate(names):
        shp = w[n].shape
        if n == "w_in":
            as3 = lambda a: a.transpose(0, 2, 1)
            back = lambda o: o.transpose(0, 2, 1)
        else:
            as3 = lambda a, rc=reduced0[i].shape: a.reshape((2,) + rc)
            back = lambda o, shp=shp: o.reshape(shp)
        outs = adamw_pair(f"adamw_{n}", as3(w[n]), (reduced0[i], reduced1[i]), (received0[i], received1[i]), as3(m[n]), as3(v[n]),
                          c_idx)
        g_local[n], delta[n], new_m[n], new_v[n] = [back(o) for o in outs]
    flat = small_all.reshape(-1)
    off = 0
    for n in REPLICATED:
        g_local[n] = flat[off:off + w[n].size].reshape(w[n].shape)
        off += w[n].size
        delta[n], new_m[n], new_v[n] = adamw(f"adamw_{n}", w[n], g_local[n], m[n], v[n])
    return (loss, grad_x[None], *[g_local[n] for n in WEIGHTS], *[delta[n] for n in WEIGHTS], *[new_m[n] for n in WEIGHTS],
            *[new_v[n] for n in WEIGHTS])
```

```python
import functools

import jax
import jax.numpy as jnp
from jax import lax
from jax.experimental import pallas as pl
from jax.experimental.pallas import tpu as pltpu

F32 = jnp.float32
BF16 = jnp.bfloat16
MXU_DTYPE = jnp.bfloat16

D_MODEL = 1024
EPS = 1e-6
ROPE_THETA = 10000.0
LRU_BLOCKS = 8
LRU_C = 8.0
MLA_HEADS = 8
MLA_NOPE = 64
MLA_ROPE = 32
MLA_QK = 96
DIL_GROUPS = ((128, 1), (512, 4), (2048, 16))
DIL_HD = 64
DIL_NK = 128
MLA_HPS = 8
IN_WIDTH = 11168
ADAM_LR, ADAM_B1, ADAM_B2, ADAM_EPS, ADAM_WD, ADAM_STEP = 0.001, 0.9, 0.999, 1e-08, 0.01, 10

LANES = 128
G_LRU, G_MLA, G_DIL, G_MRG = 0, 2048, 3072, 8192
IN_PAD = 11264
KR_OFF = 2432

NN = (((1,), (0,)), ((), ()))
NT = (((1,), (1,)), ((), ()))
TN = (((0,), (0,)), ((), ()))
NEG = -1e30
MESH = pl.DeviceIdType.MESH
VMEM_LIMIT = 48 * 2**20


def _cparams(sem):
    return pltpu.CompilerParams(dimension_semantics=sem, vmem_limit_bytes=VMEM_LIMIT)


def _dot(a, b, dims):
    return lax.dot_general(a.astype(MXU_DTYPE), b.astype(MXU_DTYPE), dims, preferred_element_type=F32)


@jax.custom_vjp
def mm(a, w):
    return _dot(a, w, NN)


def _mm_fwd(a, w):
    return _dot(a, w, NN), (a, w)


def _mm_bwd(res, g):
    a, w = res
    return _dot(g, w, NT), _dot(a, g, TN)


mm.defvjp(_mm_fwd, _mm_bwd)


def _seg64_matrix():
    r = lax.broadcasted_iota(jnp.int32, (LANES, LANES), 0) < DIL_HD
    c = lax.broadcasted_iota(jnp.int32, (LANES, LANES), 1) < DIL_HD
    return (r == c).astype(BF16)


def _seg_sum_impl(x):
    b = _seg64_matrix()
    hi = x.astype(BF16)
    r1 = x - hi.astype(F32)
    mid = r1.astype(BF16)
    lo = (r1 - mid.astype(F32)).astype(BF16)
    dot = lambda u: lax.dot_general(u, b, NN, preferred_element_type=F32)
    return dot(hi) + dot(mid) + dot(lo)


@jax.custom_vjp
def seg_sum(x):
    return _seg_sum_impl(x)


seg_sum.defvjp(lambda x: (_seg_sum_impl(x), None), lambda _, g: (_seg_sum_impl(g),))


def _lroll_impl(x, s):
    return pltpu.roll(x, s % LANES, 1)


@functools.partial(jax.custom_vjp, nondiff_argnums=(1,))
def lroll(x, s):
    return _lroll_impl(x, s)


lroll.defvjp(lambda x, s: (_lroll_impl(x, s), None), lambda s, _, g: (_lroll_impl(g, -s),))


class _Ops:
    def __init__(self, diff):
        self.mm = mm if diff else (lambda a, w: _dot(a, w, NN))
        self.seg_sum = seg_sum if diff else _seg_sum_impl
        self.lroll = lroll if diff else _lroll_impl


PLAIN, DIFF = _Ops(False), _Ops(True)


def rms(x, g, n):
    ms = jnp.sum(x * x, axis=-1, keepdims=True) * (1.0 / n)
    return x * lax.rsqrt(ms + EPS) * g


def rope(ops, x, c, s1, s2, half):
    return x * c + ops.lroll(x, -half) * s1 + ops.lroll(x, half) * s2


def _sigmoid(x):
    return 1.0 / (1.0 + jnp.exp(-x))


def _silu_and_grad(g):
    sg = _sigmoid(g)
    return g * sg, sg * (1.0 + g * (1.0 - sg))


def _softplus(x):
    return jnp.maximum(x, 0.0) + jnp.log(1.0 + jnp.exp(-jnp.abs(x)))


def _expm1(y):
    series = y * (1.0 + y * (0.5 + y * (1.0 / 6.0 + y * (1.0 / 24.0 + y * (1.0 / 120.0)))))
    return jnp.where(y > -0.05, series, jnp.exp(jnp.minimum(y, -0.05)) - 1.0)


@jax.custom_vjp
def sigmoid_d(x):
    return _sigmoid(x)


def _sigmoid_d_fwd(x):
    s = _sigmoid(x)
    return s, s


sigmoid_d.defvjp(_sigmoid_d_fwd, lambda s, g: (g * s * (1.0 - s),))


def _decay_mult(log_a):
    return jnp.sqrt(-_expm1(2.0 * log_a))


@jax.custom_vjp
def decay_mult_d(log_a):
    return _decay_mult(log_a)


def _decay_mult_d_fwd(log_a):
    em = _expm1(2.0 * log_a)
    mult = jnp.sqrt(-em)
    return mult, (em, mult)


decay_mult_d.defvjp(_decay_mult_d_fwd, lambda res, g: (-g * (res[0] + 1.0) / res[1],))
PLAIN.sigmoid, DIFF.sigmoid = _sigmoid, sigmoid_d
PLAIN.decay_mult, DIFF.decay_mult = _decay_mult, decay_mult_d


def _host(body, n_in, n_out, n_scr, hook, grid):
    if hook is None:
        return body, [], [], [], [], []
    ins, out_shapes, n_sem, make = hook
    ni, no = len(ins), len(out_shapes)
    hbm = pl.BlockSpec(memory_space=pltpu.HBM)

    def at(corner):
        cond = None
        for axis, size in enumerate(grid):
            here = pl.program_id(axis) == (size - 1 if corner else 0)
            cond = here if cond is None else cond & here
        return cond

    def hosted(*refs):
        a, refs = refs[:n_in], refs[n_in:]
        xi, refs = refs[:ni], refs[ni:]
        o, refs = refs[:n_out], refs[n_out:]
        xo, refs = refs[:no], refs[no:]
        scr, (send, recv) = refs[:n_scr], refs[n_scr:]
        start, finish = make(xi, xo, send, recv)
        pl.when(at(False))(start)
        body(*a, *o, *scr)
        pl.when(at(True))(finish)

    sems = [pltpu.SemaphoreType.DMA((n_sem,)), pltpu.SemaphoreType.DMA((n_sem,))]
    return hosted, list(ins), [hbm] * ni, [hbm] * no, list(out_shapes), sems


def _mm_call(name, a, b, *, mode, m, n, k, a_blk, b_blk, out_dtype, add, tm, tn, tk, comm=None):
    nk = k // tk
    assert m % tm == 0 and n % tn == 0 and k % tk == 0, (name, m, n, k, tm, tn, tk)
    dims = {"nn": NN, "nt": NT, "tn": TN}[mode]

    def body(*refs):
        if add is None:
            a_ref, b_ref, o_ref, *scr = refs
            add_ref = None
        else:
            a_ref, b_ref, add_ref, o_ref, *scr = refs
        part = _dot(a_ref[...], b_ref[...], dims)

        def finish(acc):
            if add_ref is not None:
                acc = acc + add_ref[...]
            o_ref[...] = acc.astype(o_ref.dtype)

        if nk == 1:
            finish(part)
        else:
            (acc_ref,) = scr
            kk = pl.program_id(2)

            @pl.when(kk == 0)
            def _():
                acc_ref[...] = part

            @pl.when(kk > 0)
            def _():
                acc_ref[...] += part

            @pl.when(kk == nk - 1)
            def _():
                finish(acc_ref[...])

    in_specs = [a_blk, b_blk]
    args = [a, b]
    if add is not None:
        in_specs.append(pl.BlockSpec((tm, tn), lambda j, i, kk: (i, j)))
        args.append(add)
    grid = (n // tn, m // tm, nk)
    scratch = [] if nk == 1 else [pltpu.VMEM((tm, tn), F32)]
    body, x_in, x_in_specs, x_out_specs, x_out_shapes, sems = _host(body, len(args), 1, len(scratch), comm, grid)
    res = pl.pallas_call(
        body,
        name=name,
        grid=grid,
        in_specs=in_specs + x_in_specs,
        out_specs=[pl.BlockSpec((tm, tn), lambda j, i, kk: (i, j))] + x_out_specs,
        out_shape=[jax.ShapeDtypeStruct((m, n), out_dtype)] + x_out_shapes,
        scratch_shapes=scratch + sems,
        compiler_params=_cparams(("arbitrary",) * 3 if comm is not None else ("parallel", "parallel", "arbitrary")),
    )(*args, *x_in)
    return res[0] if comm is None else res


def mm_nn(name, a, b, *, n=None, n_off=0, out_dtype=F32, add=None, tm=1024, tn=1024, tk=1024):
    m, k = a.shape
    n = b.shape[1] if n is None else n
    tm, tn, tk = min(tm, m), min(tn, n), min(k, tk)
    ob = n_off // tn
    assert n_off % tn == 0
    return _mm_call(name, a, b, mode="nn", m=m, n=n, k=k, out_dtype=out_dtype, add=add, tm=tm, tn=tn, tk=tk,
                    a_blk=pl.BlockSpec((tm, tk), lambda j, i, kk: (i, kk)),
                    b_blk=pl.BlockSpec((tk, tn), lambda j, i, kk: (kk, j + ob)))


def mm_nt(name, a, b, *, out_dtype=F32, tm=1024, tn=1024, tk=1024, comm=None):
    m, k = a.shape
    n = b.shape[0]
    tm, tn, tk = min(tm, m), min(tn, n), min(k, tk)
    return _mm_call(name, a, b, mode="nt", m=m, n=n, k=k, out_dtype=out_dtype, add=None, tm=tm, tn=tn, tk=tk, comm=comm,
                    a_blk=pl.BlockSpec((tm, tk), lambda j, i, kk: (i, kk)),
                    b_blk=pl.BlockSpec((tn, tk), lambda j, i, kk: (j, kk)))


DW_TILES = dict(tm=1024, tn=512, tk=4096)


def in_proj_dw(ht, dz):
    t = ht.shape[1]
    tn = 512
    nchunk = IN_PAD // tn
    kr_chunk = KR_OFF // tn
    lo = KR_OFF - kr_chunk * tn

    def body(ht_ref, dz_ref, out_ref, buf, sems):
        j = pl.program_id(0)
        slot = j % 2
        buf[slot] = _dot(ht_ref[...], dz_ref[...], NN).T

        def copies(c, s):
            dst = jnp.where(c < kr_chunk, c * tn, (c - 1) * tn + lo + MLA_ROPE)
            plain = [pltpu.make_async_copy(buf.at[s], out_ref.at[pl.ds(pl.multiple_of(dst, 8), tn)], sems.at[s, 0])]
            split = [pltpu.make_async_copy(buf.at[s, 0:lo], out_ref.at[KR_OFF - lo:KR_OFF], sems.at[s, 0]),
                     pltpu.make_async_copy(buf.at[s, lo + 64:lo + 96], out_ref.at[KR_OFF:KR_OFF + MLA_ROPE], sems.at[s, 1])]
            return plain, split

        def each(c, s, action):
            plain, split = copies(c, s)

            @pl.when(c != kr_chunk)
            def _():
                for cp in plain:
                    action(cp)

            @pl.when(c == kr_chunk)
            def _():
                for cp in split:
                    action(cp)

        @pl.when(j > 0)
        def _():
            each(j - 1, 1 - slot, lambda cp: cp.wait())

        each(j, slot, lambda cp: cp.start())

        @pl.when(j == nchunk - 1)
        def _():
            each(j, slot, lambda cp: cp.wait())

    return pl.pallas_call(
        body,
        name="in_proj_dw",
        grid=(nchunk,),
        in_specs=[pl.BlockSpec((D_MODEL, t), lambda j: (0, 0)), pl.BlockSpec((t, tn), lambda j: (0, j))],
        out_specs=pl.BlockSpec(memory_space=pltpu.HBM),
        out_shape=jax.ShapeDtypeStruct((IN_WIDTH, D_MODEL), F32),
        scratch_shapes=[pltpu.VMEM((2, tn, D_MODEL), F32), pltpu.SemaphoreType.DMA((2, 2))],
        compiler_params=_cparams(("arbitrary",)),
    )(ht, dz)


def rowwise(name, fn, t, *, tm=256, ncol=1, ins=(), outs=(), touts=(), accs=()):
    n_in, n_out, n_acc = len(ins), len(outs) + len(touts), len(accs)

    def zero_when(ref, cond):
        @pl.when(cond)
        def _():
            ref[...] = jnp.zeros(ref.shape, ref.dtype)

    def body(*refs):
        in_refs, out_refs, acc_refs = refs[:n_in], refs[n_in:n_in + n_out], refs[n_in + n_out:]
        j, i = pl.program_id(0), pl.program_id(1)
        for ref, (_, _, _, cd) in zip(acc_refs, accs):
            zero_when(ref, (i == 0) if cd else ((i == 0) & (j == 0)))
        fn(in_refs, out_refs, acc_refs)

    def in_spec(arr, w, base, cd, rd):
        rows = tm if rd else arr.shape[0]
        return pl.BlockSpec((rows, w), lambda j, i: (i if rd else 0, base + (j if cd else 0)))

    in_specs = [in_spec(*e) for e in ins]
    out_specs = [pl.BlockSpec((tm, w), (lambda j, i, cd=cd: (i, j if cd else 0))) for (_, w, _, cd) in outs]
    out_specs += [pl.BlockSpec((w, tm), (lambda j, i, cd=cd: (j if cd else 0, i))) for (_, w, _, cd) in touts]
    out_specs += [pl.BlockSpec((r, w), (lambda j, i, cd=cd: (0, j if cd else 0))) for (r, _, w, cd) in accs]
    out_shape = [jax.ShapeDtypeStruct((t, c), dt) for (c, _, dt, _) in outs]
    out_shape += [jax.ShapeDtypeStruct((r, t), dt) for (r, _, dt, _) in touts]
    out_shape += [jax.ShapeDtypeStruct((r, c), F32) for (r, c, _, _) in accs]
    res = pl.pallas_call(
        body,
        name=name,
        grid=(ncol, t // tm),
        in_specs=in_specs,
        out_specs=out_specs,
        out_shape=out_shape,
        compiler_params=_cparams(("arbitrary", "arbitrary") if accs else ("parallel", "parallel")),
    )(*[e[0] for e in ins])
    return res


def _row(arr, w=None, base=0, cd=False):
    return (arr, arr.shape[1] if w is None else w, base, cd, True)


def _const(arr, w=None, base=0, cd=False):
    return (arr, arr.shape[1] if w is None else w, base, cd, False)


def rope_tables(positions):
    t = positions.shape[0]

    def fn(ins, outs, _):
        pos = ins[0][...].astype(F32)
        lane = lax.broadcasted_iota(jnp.int32, (1, LANES), 1)
        log_theta = jnp.log(jnp.float32(ROPE_THETA))
        jm = lane - MLA_NOPE
        idx = jnp.clip(jnp.where(jm < 16, jm, jm - 16), 0, 15).astype(F32)
        ang = pos * jnp.exp(-(idx * (2.0 / MLA_ROPE)) * log_theta)
        cos, sin = jnp.cos(ang), jnp.sin(ang)
        in_rope = (lane >= MLA_NOPE) & (lane < MLA_QK)
        outs[0][:, 0:128] = jnp.where(lane < MLA_NOPE, 1.0, jnp.where(in_rope, cos, 0.0))
        outs[0][:, 128:256] = jnp.where(in_rope & (jm < 16), -sin, 0.0)
        outs[0][:, 256:384] = jnp.where(in_rope & (jm >= 16), sin, 0.0)
        jd = lane & (DIL_HD - 1)
        idx = (jd & 31).astype(F32)
        ang = pos * jnp.exp(-(idx * (2.0 / DIL_HD)) * log_theta)
        cos, sin = jnp.cos(ang), jnp.sin(ang)
        outs[0][:, 384:512] = cos
        outs[0][:, 512:640] = jnp.where(jd < 32, -sin, 0.0)
        outs[0][:, 640:768] = jnp.where(jd >= 32, sin, 0.0)

    return rowwise("rope_tables", fn, t, tm=1024, ins=[_row(positions)], outs=[(768, 768, F32, False)])[0]


def rmsnorm_fwd(x, g):
    def fn(ins, outs, _):
        h = rms(ins[0][...], ins[1][...], D_MODEL)
        outs[0][...] = h.astype(MXU_DTYPE)
        outs[1][...] = h.T.astype(MXU_DTYPE)

    return rowwise("rmsnorm_fwd", fn, x.shape[0], tm=512, ins=[_row(x), _const(g)], outs=[(D_MODEL, D_MODEL, MXU_DTYPE, False)],
                   touts=[(D_MODEL, D_MODEL, MXU_DTYPE, False)])


def rmsnorm_bwd(x, dh, dres, g):
    def fn(ins, outs, accs):
        _, vjp = jax.vjp(lambda xv, gv: rms(xv, gv, D_MODEL), ins[0][...], ins[3][...])
        dx, dg = vjp(ins[1][...])
        outs[0][...] = ins[2][...] + dx
        accs[0][...] += dg

    return rowwise("rmsnorm_bwd", fn, x.shape[0], tm=512, ins=[_row(x), _row(dh), _row(dres), _const(g)],
                   outs=[(D_MODEL, D_MODEL, F32, False)], accs=[(1, D_MODEL, D_MODEL, False)])


def loss_head(y, target):
    def fn(ins, outs, accs):
        err = ins[0][...] - ins[1][...]
        outs[0][...] = err * (1.0 / D_MODEL)
        accs[0][...] += jnp.sum(err * err, axis=0, keepdims=True)
        accs[1][...] = jnp.broadcast_to(jnp.sum(accs[0][...], keepdims=True), (1, LANES))

    dy, _, tot = rowwise("loss_head", fn, y.shape[0], tm=512, ins=[_row(y), _row(target)], outs=[(D_MODEL, D_MODEL, F32, False)],
                         accs=[(1, D_MODEL, D_MODEL, False), (1, LANES, LANES, False)])
    return dy, tot[0, 0] * (0.5 / D_MODEL)


def _shift_rows(v, d, fill, reverse):
    tb = v.shape[0]
    if d % 8 == 0:
        pad = jnp.full((d, v.shape[1]), fill, v.dtype)
        return jnp.concatenate([v[d:], pad] if reverse else [pad, v[:tb - d]], axis=0)
    rows = lax.broadcasted_iota(jnp.int32, v.shape, 0)
    if not reverse:
        return jnp.where(rows >= d, pltpu.roll(v, d, 0), fill)
    return jnp.where(rows < tb - d, pltpu.roll(v, tb - d, 0), fill)


def _scan_tile(a, b, reverse):
    d = 1
    while d < a.shape[0]:
        b = b + a * _shift_rows(b, d, 0.0, reverse)
        a = a * _shift_rows(a, d, 1.0, reverse)
        d *= 2
    return a, b


def _lru_gates(ops, xc, wgx, bgx, wga, bga, lam):
    gx = ops.sigmoid(ops.mm(xc, wgx) + bgx)
    ga = ops.sigmoid(ops.mm(xc, wga) + bga)
    log_a = -LRU_C * ga * _softplus(-lam)
    a = jnp.exp(log_a)
    return a, ops.decay_mult(log_a) * (gx * xc)


def _shifted_inputs(x, halo, tb):
    rows = lax.broadcasted_iota(jnp.int32, x.shape, 0)
    pad = jnp.zeros((tb - 8, LANES), F32)
    out = []
    for d in (3, 2, 1):
        head = jnp.concatenate([pltpu.roll(halo, d, 0), pad], axis=0)
        out.append(jnp.where(rows >= d, pltpu.roll(x, d, 0), head))
    return out + [x]


def _lru_specs(nt, tb, reverse):
    hb = tb // 8
    tt = (lambda t: nt - 1 - t) if reverse else (lambda t: t)
    blk = lambda off: pl.BlockSpec((tb, LANES), lambda n, t: (tt(t), n + off))
    halo = lambda off: pl.BlockSpec((8, LANES), lambda n, t: (jnp.maximum(tt(t) * hb - 1, 0), n + off))
    chan = lambda r: pl.BlockSpec((r, LANES), lambda n, t: (0, n))
    wblk = pl.BlockSpec((None, LANES, LANES), lambda n, t: (n, 0, 0))
    bblk = pl.BlockSpec((None, 1, LANES), lambda n, t: (n, 0, 0))
    return blk, halo, chan, wblk, bblk


def lru_fwd(z, conv_w, conv_b, wgx, bgx, wga, bga, lam, *, tb=512, comm=None):
    t = z.shape[0]
    nt = t // tb
    blk, halo, chan, wblk, bblk = _lru_specs(nt, tb, False)

    def body(x_ref, xh_ref, g_ref, cw_ref, cb_ref, wgx_ref, bgx_ref, wga_ref, bga_ref, lam_ref, h_ref, y_ref, yt_ref, carry_ref):
        ti = pl.program_id(1)

        @pl.when(ti == 0)
        def _():
            carry_ref[...] = jnp.zeros((8, LANES), F32)

        x = x_ref[...]
        hal = jnp.where(ti > 0, xh_ref[...], 0.0)
        xs = _shifted_inputs(x, hal, tb)
        xc = cb_ref[...] + sum(xs[k] * cw_ref[k:k + 1, :] for k in range(4))
        a, b = _lru_gates(PLAIN, xc, wgx_ref[...], bgx_ref[...], wga_ref[...], bga_ref[...], lam_ref[...])
        acum, h0 = _scan_tile(a, b, False)
        h = h0 + acum * carry_ref[7:8, :]
        carry_ref[...] = h[tb - 8:tb, :]
        h_ref[...] = h
        y = h * _silu_and_grad(g_ref[...])[0]
        y_ref[...] = y.astype(MXU_DTYPE)
        yt_ref[...] = y.T.astype(MXU_DTYPE)

    grid = (LRU_BLOCKS, nt)
    body, x_in, x_in_specs, x_out_specs, x_out_shapes, sems = _host(body, 10, 3, 1, comm, grid)
    return pl.pallas_call(
        body,
        name="lru_fwd" if comm is None else "lru_fwd_comm",
        grid=grid,
        in_specs=[blk(0), halo(0), blk(LRU_BLOCKS), chan(4), chan(1), wblk, bblk, wblk, bblk, chan(1)] + x_in_specs,
        out_specs=[blk(0), blk(0), pl.BlockSpec((LANES, tb), lambda n, t_: (n, t_))] + x_out_specs,
        out_shape=[jax.ShapeDtypeStruct((t, 1024), F32), jax.ShapeDtypeStruct((t, 1024), MXU_DTYPE),
                   jax.ShapeDtypeStruct((1024, t), MXU_DTYPE)] + x_out_shapes,
        scratch_shapes=[pltpu.VMEM((8, LANES), F32)] + sems,
        compiler_params=_cparams(("arbitrary", "arbitrary") if comm is not None else ("parallel", "arbitrary")),
    )(z, z, z, conv_w, conv_b, wgx, bgx, wga, bga, lam, *x_in)


def lru_bwd(z, h, dy, conv_w, conv_b, wgx, bgx, wga, bga, lam, *, tb=512, comm=None):
    t = z.shape[0]
    nt = t // tb
    blk, halo, chan, wblk, bblk = _lru_specs(nt, tb, True)

    def body(x_ref, xh_ref, g_ref, h_ref, hh_ref, dy_ref, cw_ref, cb_ref, wgx_ref, bgx_ref, wga_ref, bga_ref, lam_ref,
             dzx_ref, dzg_ref, dwgx_ref, dbgx_ref, dwga_ref, dbga_ref, dlam_ref, dcw_ref, dcb_ref,
             gcar_ref, acar_ref, xcar_ref):
        ti = pl.program_id(1)
        has_earlier = ti < nt - 1

        @pl.when(ti == 0)
        def _():
            for ref in (dwgx_ref, dbgx_ref, dwga_ref, dbga_ref, dlam_ref, dcw_ref, dcb_ref, gcar_ref, acar_ref, xcar_ref):
                ref[...] = jnp.zeros(ref.shape, F32)

        rows = lax.broadcasted_iota(jnp.int32, (tb, LANES), 0)
        x = x_ref[...]
        hal = jnp.where(has_earlier, xh_ref[...], 0.0)
        xs = _shifted_inputs(x, hal, tb)
        xc = cb_ref[...] + sum(xs[k] * cw_ref[k:k + 1, :] for k in range(4))
        (a, _), vjp = jax.vjp(functools.partial(_lru_gates, DIFF), xc, wgx_ref[...], bgx_ref[...], wga_ref[...],
                              bga_ref[...], lam_ref[...])
        g, h, dyv = g_ref[...], h_ref[...], dy_ref[...]
        silu, dsilu = _silu_and_grad(g)
        dzg_ref[...] = (dyv * h * dsilu).astype(MXU_DTYPE)
        a_next = jnp.where(rows < tb - 1, pltpu.roll(a, tb - 1, 0), acar_ref[0:1, :])
        acum, g0 = _scan_tile(a_next, dyv * silu, True)
        gt = g0 + acum * gcar_ref[0:1, :]
        h_prev = jnp.where(rows >= 1, pltpu.roll(h, 1, 0), jnp.where(has_earlier, hh_ref[7:8, :], 0.0))
        dxc, dwgx, dbgx, dwga, dbga, dlam = vjp((gt * h_prev, gt))
        later = xcar_ref[...]
        gcar_ref[...] = gt[0:8, :]
        acar_ref[...] = a[0:8, :]
        xcar_ref[...] = dxc[0:8, :]
        dx = dxc * cw_ref[3:4, :]
        pad = jnp.zeros((tb - 8, LANES), F32)
        for d in (1, 2, 3):
            tail = jnp.concatenate([pad, pltpu.roll(later, 8 - d, 0)], axis=0)
            up = jnp.where(rows < tb - d, pltpu.roll(dxc, tb - d, 0), tail)
            dx = dx + up * cw_ref[3 - d:4 - d, :]
        dzx_ref[...] = dx.astype(MXU_DTYPE)
        for k in range(4):
            dcw_ref[k:k + 1, :] += jnp.sum(dxc * xs[k], axis=0, keepdims=True)
        dcb_ref[...] += jnp.sum(dxc, axis=0, keepdims=True)
        dwgx_ref[...] += dwgx
        dbgx_ref[...] += dbgx
        dwga_ref[...] += dwga
        dbga_ref[...] += dbga
        dlam_ref[...] += dlam

    sds = jax.ShapeDtypeStruct
    grid = (LRU_BLOCKS, nt)
    body, x_in, x_in_specs, x_out_specs, x_out_shapes, sems = _host(body, 13, 9, 3, comm, grid)
    return pl.pallas_call(
        body,
        name="lru_bwd" if comm is None else "lru_bwd_comm",
        grid=grid,
        in_specs=[blk(0), halo(0), blk(LRU_BLOCKS), blk(0), halo(0), blk(0), chan(4), chan(1), wblk, bblk, wblk, bblk, chan(1)]
        + x_in_specs,
        out_specs=[blk(0), blk(0), wblk, bblk, wblk, bblk, chan(1), chan(4), chan(1)] + x_out_specs,
        out_shape=[sds((t, 1024), MXU_DTYPE), sds((t, 1024), MXU_DTYPE), sds(wgx.shape, F32), sds(bgx.shape, F32), sds(wga.shape, F32),
                   sds(bga.shape, F32), sds((1, 1024), F32), sds((4, 1024), F32), sds((1, 1024), F32)] + x_out_shapes,
        scratch_shapes=[pltpu.VMEM((8, LANES), F32)] * 3 + sems,
        compiler_params=_cparams(("arbitrary", "arbitrary") if comm is not None else ("parallel", "arbitrary")),
    )(z, z, z, h, h, dy, conv_w, conv_b, wgx, bgx, wga, bga, lam, *x_in)


def _mla_prep_tile(ops, cq, ckv, krp, c, s1, s2, g_cq, g_ckv, wq, wk, wv, gq, gk):
    cqn = rms(cq, g_cq, 256)
    ckvn = rms(ckv, g_ckv, 128)
    v = ops.mm(ckvn, wv)
    qs, ks = [], []
    for hd in range(MLA_HEADS):
        q = rms(ops.mm(cqn, wq[hd]), gq, MLA_QK)
        k = rms(ops.mm(ckvn, wk[hd]) + krp, gk, MLA_QK)
        qs.append(rope(ops, q, c, s1, s2, 16) * (MLA_QK ** -0.5))
        ks.append(rope(ops, k, c, s1, s2, 16))
    return tuple(qs), tuple(ks), v


def _mla_prep_args(ins):
    z_cq, z_ckv, z_kr, tc, ts1, ts2, g_cq, g_ckv, wq, wk, wv, gq, gk = ins[:13]
    heads = lambda w: tuple(w[:, LANES * hd:LANES * (hd + 1)] for hd in range(MLA_HEADS))
    return (z_cq[...], z_ckv[...], z_kr[...], tc[...], ts1[...], ts2[...], g_cq[...], g_ckv[...], heads(wq), heads(wk),
            wv[...], gq[...], gk[...])


def _mla_prep_ins(z, tabs, w):
    return [_row(z, 256, 0), _row(z, 128, 2), _row(z, 128, 3), _row(tabs, 128, 0), _row(tabs, 128, 1), _row(tabs, 128, 2),
            _const(w["cq_norm_g"]), _const(w["ckv_norm_g"]), _const(w["wq"]), _const(w["wk"]), _const(w["wv"]),
            _const(w["gq"]), _const(w["gk"])]


def mla_prep_fwd(z, tabs, w):
    def fn(ins, outs, _):
        qs, ks, v = _mla_prep_tile(PLAIN, *_mla_prep_args(ins))
        for hd in range(MLA_HEADS):
            outs[0][:, LANES * hd:LANES * (hd + 1)] = qs[hd].astype(MXU_DTYPE)
            outs[1][:, LANES * hd:LANES * (hd + 1)] = ks[hd].astype(MXU_DTYPE)
        outs[2][...] = v.astype(MXU_DTYPE)
        outs[3][...] = v.T.astype(MXU_DTYPE)

    return rowwise("mla_prep_fwd", fn, z.shape[0], tm=512, ins=_mla_prep_ins(z, tabs, w),
                   outs=[(1024, 1024, MXU_DTYPE, False), (1024, 1024, MXU_DTYPE, False), (512, 512, MXU_DTYPE, False)],
                   touts=[(512, 512, MXU_DTYPE, False)])


def mla_prep_bwd(z, tabs, w, dq, dk, dv):
    def fn(ins, outs, accs):
        args = _mla_prep_args(ins)
        _, vjp = jax.vjp(functools.partial(_mla_prep_tile, DIFF), *args)
        heads = lambda ref: tuple(ref[:, LANES * hd:LANES * (hd + 1)] for hd in range(MLA_HEADS))
        dcq, dckv, dkr, _, _, _, dg_cq, dg_ckv, dwq, dwk, dwv, dgq, dgk = vjp((heads(ins[13]), heads(ins[14]), ins[15][...]))
        lane = lax.broadcasted_iota(jnp.int32, (1, LANES), 1)
        outs[0][:, 0:256] = dcq.astype(MXU_DTYPE)
        outs[0][:, 256:384] = dckv.astype(MXU_DTYPE)
        outs[0][:, 384:512] = jnp.where((lane >= MLA_NOPE) & (lane < MLA_QK), dkr, 0.0).astype(MXU_DTYPE)
        accs[0][...] += dg_cq
        accs[1][...] += dg_ckv
        for hd in range(MLA_HEADS):
            accs[2][:, LANES * hd:LANES * (hd + 1)] += dwq[hd]
            accs[3][:, LANES * hd:LANES * (hd + 1)] += dwk[hd]
        accs[4][...] += dwv
        accs[5][...] += dgq
        accs[6][...] += dgk

    return rowwise("mla_prep_bwd", fn, z.shape[0], tm=512, ins=_mla_prep_ins(z, tabs, w) + [_row(dq), _row(dk), _row(dv)],
                   outs=[(512, 512, MXU_DTYPE, False)],
                   accs=[(1, 256, 256, False), (1, 128, 128, False), (256, 1024, 1024, False), (128, 1024, 1024, False),
                         (128, 512, 512, False), (1, 128, 128, False), (1, 128, 128, False)])


def _head_masks():
    lane = lax.broadcasted_iota(jnp.int32, (1, LANES), 1)
    return (lane < DIL_HD, lane >= DIL_HD)


def _row_scalar(tile, mask):
    return jnp.max(jnp.where(mask, tile, -jnp.inf), axis=-1, keepdims=True)


def _causal_tiles(nq, by_key):
    pairs = [(i, j) for i in range(nq) for j in range(i + 1)]
    if by_key:
        pairs.sort(key=lambda ij: (ij[1], ij[0]))
    return (jnp.asarray([ij[0] for ij in pairs], jnp.int32), jnp.asarray([ij[1] for ij in pairs], jnp.int32))


def mla_attn_fwd(q, k, vt, z, *, tq=256, gather=None):
    t = q.shape[0]
    nq = t // tq
    it, jt = _causal_tiles(nq, False)
    hps, wq, wv = MLA_HPS, LANES * MLA_HPS, 64 * MLA_HPS
    ws, layer = gather if gather is not None else ([], 0)
    ng = len(ws)
    ngrp, nsteps = MLA_HEADS // hps, int(it.shape[0])

    def body(it_ref, jt_ref, q_ref, k_ref, vt_ref, g_ref, *rest):
        w_refs, rest = rest[:ng], rest[ng:]
        o_ref, y_ref, yt_ref, lse_ref = rest[:4]
        ga_refs, rest = rest[4:4 + ng], rest[4 + ng:]
        m_scr, l_scr, acc_scr = rest[:3]
        step = pl.program_id(1)
        i, j = it_ref[step], jt_ref[step]
        if ng:
            gather_start, gather_finish = _layer_gather(w_refs, ga_refs, rest[3], rest[4], layer)

            @pl.when((pl.program_id(0) == 0) & (step == 0))
            def _():
                gather_start()

        @pl.when(j == 0)
        def _():
            m_scr[...] = jnp.full(m_scr.shape, NEG, F32)
            l_scr[...] = jnp.zeros(l_scr.shape, F32)
            acc_scr[...] = jnp.zeros(acc_scr.shape, F32)

        def update(diagonal):
            heads = range(hps)
            lanes = [slice(LANES * hh, LANES * (hh + 1)) for hh in heads]
            rows = [slice(64 * hh, 64 * (hh + 1)) for hh in heads]
            sts = [_dot(k_ref[:, lanes[hh]], q_ref[:, lanes[hh]], NT) for hh in heads]
            m_prev = [m_scr[hh:hh + 1, :] for hh in heads]
            l_prev = [l_scr[hh:hh + 1, :] for hh in heads]
            acc_prev = [acc_scr[rows[hh], :] for hh in heads]
            m_new, l_new, acc_new = [], [], []
            for hh in heads:
                st = sts[hh]
                if diagonal:
                    key = lax.broadcasted_iota(jnp.int32, (tq, tq), 0)
                    qry = lax.broadcasted_iota(jnp.int32, (tq, tq), 1)
                    st = jnp.where(key <= qry, st, NEG)
                m_new.append(jnp.maximum(m_prev[hh], jnp.max(st, axis=0, keepdims=True)))
                alpha = jnp.exp(m_prev[hh] - m_new[hh])
                pt = jnp.exp(st - m_new[hh])
                l_new.append(alpha * l_prev[hh] + jnp.sum(pt, axis=0, keepdims=True))
                acc_new.append(alpha * acc_prev[hh] + _dot(vt_ref[rows[hh], :], pt, NN))
            for hh in heads:
                m_scr[hh:hh + 1, :] = m_new[hh]
                l_scr[hh:hh + 1, :] = l_new[hh]
                acc_scr[rows[hh], :] = acc_new[hh]

        @pl.when(j < i)
        def _():
            update(False)

        @pl.when(j == i)
        def _():
            update(True)
            lse_ref[...] = jnp.zeros(lse_ref.shape, F32)
            for hh in range(hps):
                rows = slice(64 * hh, 64 * (hh + 1))
                acc_scr[rows, :] = acc_scr[rows, :] / l_scr[hh:hh + 1, :]
                lse_ref[hh:hh + 1, :] = m_scr[hh:hh + 1, :] + jnp.log(l_scr[hh:hh + 1, :])
            o = acc_scr[...].T
            o_ref[...] = o
            y = o * _silu_and_grad(g_ref[...])[0]
            y_ref[...] = y.astype(MXU_DTYPE)
            yt_ref[...] = y.T.astype(MXU_DTYPE)

        if ng:
            @pl.when((pl.program_id(0) == ngrp - 1) & (step == nsteps - 1))
            def _():
                gather_finish()

    qo = lambda w, off=0: pl.BlockSpec((tq, w), lambda p, s, it_, jt_: (it_[s], p + off))
    sds = jax.ShapeDtypeStruct
    comm_scratch = [pltpu.SemaphoreType.DMA((6 * ng,)), pltpu.SemaphoreType.DMA((6 * ng,))] if ng else []
    return pl.pallas_call(
        body,
        name="mla_attn_fwd_gather" if ng else "mla_attn_fwd",
        grid_spec=pltpu.PrefetchScalarGridSpec(
            num_scalar_prefetch=2,
            grid=(ngrp, nsteps),
            in_specs=[qo(wq), pl.BlockSpec((tq, wq), lambda p, s, it_, jt_: (jt_[s], p)),
                      pl.BlockSpec((wv, tq), lambda p, s, it_, jt_: (p, jt_[s])), qo(wv, 512 // wv)] + [HBM_SPEC] * ng,
            out_specs=[qo(wv), qo(wv), pl.BlockSpec((wv, tq), lambda p, s, it_, jt_: (p, it_[s])),
                       pl.BlockSpec((None, 8, tq), lambda p, s, it_, jt_: (p, 0, it_[s]))] + [HBM_SPEC] * ng,
            scratch_shapes=[pltpu.VMEM((8, tq), F32), pltpu.VMEM((8, tq), F32), pltpu.VMEM((wv, tq), F32)] + comm_scratch,
        ),
        out_shape=[sds((t, 512), F32), sds((t, 512), MXU_DTYPE), sds((512, t), MXU_DTYPE), sds((ngrp, 8, t), F32)]
        + _gather_shapes(ws),
        compiler_params=_cparams(("arbitrary", "arbitrary") if ng else ("parallel", "arbitrary")),
    )(it, jt, q, k, vt, z, *ws)


def mla_attn_bwd(q, k, v, do, lse, dd, *, tq=256, chips=None):
    t = q.shape[0]
    nq = t // tq
    it, jt = _causal_tiles(nq, True)
    hps, wq, wv = MLA_HPS, LANES * MLA_HPS, 64 * MLA_HPS
    pcs, layer = chips if chips is not None else ([], 0)
    nc = len(pcs)
    ngrp, nsteps = MLA_HEADS // hps, int(it.shape[0])

    def body(it_ref, jt_ref, q_ref, k_ref, v_ref, do_ref, lse_ref, d_ref, *rest):
        pc_refs, rest = rest[:nc], rest[nc:]
        dq_ref, dk_ref, dv_ref = rest[:3]
        lb_refs, rest = rest[3:3 + nc], rest[3 + nc:]
        dk_scr, dv_scr = rest[:2]
        step = pl.program_id(1)
        i, j = it_ref[step], jt_ref[step]
        masks = _head_masks()
        if nc:
            chips_start, chips_finish = _chips_exchange(pc_refs, lb_refs, rest[2], rest[3], layer)

            @pl.when((pl.program_id(0) == 0) & (step == 0))
            def _():
                chips_start()

        @pl.when(step == 0)
        def _():
            dq_ref[...] = jnp.zeros(dq_ref.shape, F32)

        @pl.when(i == j)
        def _():
            dk_scr[...] = jnp.zeros(dk_scr.shape, F32)
            dv_scr[...] = jnp.zeros(dv_scr.shape, F32)

        def update(diagonal):
            qrows = pl.ds(pl.multiple_of(i * tq, tq), tq)
            heads = range(hps)
            lanes = [slice(LANES * hh, LANES * (hh + 1)) for hh in heads]
            pair = [slice(LANES * (hh // 2), LANES * (hh // 2 + 1)) for hh in heads]
            qh = [q_ref[:, lanes[hh]] for hh in heads]
            kh = [k_ref[:, lanes[hh]] for hh in heads]
            doh = []
            for hh in heads:
                dov = do_ref[:, pair[hh]]
                doh.append(jnp.where(masks[hh % 2], dov, jnp.zeros_like(dov)))
            sts = [_dot(kh[hh], qh[hh], NT) for hh in heads]
            dps = [_dot(v_ref[:, pair[hh]], doh[hh], NT) for hh in heads]
            lse = [lse_ref[hh:hh + 1, :] for hh in heads]
            ddv = [d_ref[hh:hh + 1, :] for hh in heads]
            dk_new = [dk_scr[:, lanes[hh]] for hh in heads]
            dq_new = [dq_ref[qrows, lanes[hh]] for hh in heads]
            dv_new = [dv_scr[:, pair[2 * pp]] for pp in range(hps // 2)]
            for hh in heads:
                st = sts[hh] - lse[hh]
                if diagonal:
                    key = lax.broadcasted_iota(jnp.int32, (tq, tq), 0)
                    qry = lax.broadcasted_iota(jnp.int32, (tq, tq), 1)
                    st = jnp.where(key <= qry, st, NEG)
                pt = jnp.exp(st)
                dst = pt * (dps[hh] - ddv[hh])
                dv_new[hh // 2] = dv_new[hh // 2] + _dot(pt, doh[hh], NN)
                dk_new[hh] = dk_new[hh] + _dot(dst, qh[hh], NN)
                dq_new[hh] = dq_new[hh] + _dot(dst, kh[hh], TN)
            for hh in heads:
                dk_scr[:, lanes[hh]] = dk_new[hh]
                dq_ref[qrows, lanes[hh]] = dq_new[hh]
            for pp in range(hps // 2):
                dv_scr[:, pair[2 * pp]] = dv_new[pp]

        @pl.when(j < i)
        def _():
            update(False)

        @pl.when(j == i)
        def _():
            update(True)

        @pl.when(i == nq - 1)
        def _():
            dk_ref[...] = dk_scr[...]
            dv_ref[...] = dv_scr[...]

        if nc:
            @pl.when((pl.program_id(0) == ngrp - 1) & (step == nsteps - 1))
            def _():
                chips_finish()

    qo = lambda w: pl.BlockSpec((tq, w), lambda p, s, it_, jt_: (it_[s], p))
    kv = lambda w: pl.BlockSpec((tq, w), lambda p, s, it_, jt_: (jt_[s], p))
    stat = pl.BlockSpec((None, 8, tq), lambda p, s, it_, jt_: (p, 0, it_[s]))
    sds = jax.ShapeDtypeStruct
    comm_scratch = [pltpu.SemaphoreType.DMA((3 * nc,)), pltpu.SemaphoreType.DMA((3 * nc,))] if nc else []
    return pl.pallas_call(
        body,
        name="mla_attn_bwd_chips" if nc else "mla_attn_bwd",
        grid_spec=pltpu.PrefetchScalarGridSpec(
            num_scalar_prefetch=2,
            grid=(ngrp, nsteps),
            in_specs=[qo(wq), kv(wq), kv(wv), qo(wv), stat, stat] + [HBM_SPEC] * nc,
            out_specs=[pl.BlockSpec((t, wq), lambda p, s, it_, jt_: (0, p)), kv(wq), kv(wv)] + [HBM_SPEC] * nc,
            scratch_shapes=[pltpu.VMEM((tq, wq), F32), pltpu.VMEM((tq, wv), F32)] + comm_scratch,
        ),
        out_shape=[sds((t, 1024), F32), sds((t, 1024), F32), sds((t, 512), F32)] + _chips_shapes(pcs),
        compiler_params=_cparams(("arbitrary", "arbitrary") if nc else ("parallel", "arbitrary")),
    )(it, jt, q, k, v, do, lse, dd, *pcs)


def mla_gate_bwd(dy, o, z, *, tm=512):
    t = dy.shape[0]
    wv = 64 * MLA_HPS

    def body(dy_ref, o_ref, g_ref, do_ref, dzg_ref, dd_ref):
        dyv, ov = dy_ref[...], o_ref[...]
        silu, dsilu = _silu_and_grad(g_ref[...])
        do = dyv * silu
        do_ref[...] = do.astype(MXU_DTYPE)
        dzg_ref[...] = (dyv * ov * dsilu).astype(MXU_DTYPE)
        prod = do * ov
        row = lax.broadcasted_iota(jnp.int32, (8, wv), 0)
        lane = lax.broadcasted_iota(jnp.int32, (8, wv), 1)
        pick = ((lane >= row * DIL_HD) & (lane < (row + 1) * DIL_HD)).astype(BF16)
        hi = prod.astype(BF16)
        r1 = prod - hi.astype(F32)
        mid = r1.astype(BF16)
        lo = (r1 - mid.astype(F32)).astype(BF16)
        dot = lambda u: lax.dot_general(pick, u, NT, preferred_element_type=F32)
        dd_ref[...] = dot(hi) + dot(mid) + dot(lo)

    blk = lambda off=0: pl.BlockSpec((tm, wv), lambda p, i: (i, p + off))
    sds = jax.ShapeDtypeStruct
    return pl.pallas_call(
        body,
        name="mla_gate_bwd",
        grid=(MLA_HEADS // MLA_HPS, t // tm),
        in_specs=[blk(), blk(), blk(512 // wv)],
        out_specs=[blk(), blk(), pl.BlockSpec((None, 8, tm), lambda p, i: (p, 0, i))],
        out_shape=[sds((t, 512), MXU_DTYPE), sds((t, 512), MXU_DTYPE), sds((MLA_HEADS // MLA_HPS, 8, t), F32)],
        compiler_params=_cparams(("parallel", "parallel")),
    )(dy, o, z)


SPAN = 2048
DIL_SCALE = DIL_HD ** -0.5


def _rm_src(u, window, dil):
    nn, r = divmod(u, dil)
    return pl.ds(nn * window + r, DIL_NK, stride=dil) if dil > 1 else pl.ds(u * DIL_NK, DIL_NK)


def _rm_dst(u):
    return pl.ds(u * DIL_NK, DIL_NK)


def _dil_prep_tile(ops, x, g2, c, s1, s2, scale):
    ms = ops.seg_sum(x * x) * (1.0 / DIL_HD)
    return rope(ops, x * lax.rsqrt(ms + EPS) * g2, c, s1, s2, 32) * scale


def _span_blk(base):
    return pl.BlockSpec((SPAN, LANES), lambda s, p: (s, base + p))


def _const_blk(shape):
    return pl.BlockSpec(shape, lambda s, p: (0,) * len(shape))


_DIL_TABLE_SPECS = [pl.BlockSpec((SPAN, LANES), (lambda s, p, blk=blk: (s, blk))) for blk in (0, 1, 2)]


def dil_tables(tabs, gi):
    window, dil = DIL_GROUPS[gi]
    t = tabs.shape[0]

    def body(c_ref, s1_ref, s2_ref, o_ref):
        for u in range(SPAN // DIL_NK):
            src, dst = _rm_src(u, window, dil), _rm_dst(u)
            for i, ref in enumerate((c_ref, s1_ref, s2_ref)):
                o_ref[dst, LANES * i:LANES * (i + 1)] = ref[src, :]

    return pl.pallas_call(
        body,
        name=f"dil_tables_{dil}",
        grid=(t // SPAN,),
        in_specs=[pl.BlockSpec((SPAN, LANES), (lambda s, blk=blk: (s, blk))) for blk in (3, 4, 5)],
        out_specs=pl.BlockSpec((SPAN, 3 * LANES), lambda s: (s, 0)),
        out_shape=jax.ShapeDtypeStruct((t, 3 * LANES), F32),
        compiler_params=_cparams(("parallel",)),
    )(tabs, tabs, tabs)


def dil_prep_fwd(z, tabs, gq2, gk2, gi):
    window, dil = DIL_GROUPS[gi]
    t = z.shape[0]

    def body(q_ref, k_ref, v_ref, c_ref, s1_ref, s2_ref, gq_ref, gk_ref, qo_ref, ko_ref, vo_ref):
        for u in range(SPAN // DIL_NK):
            src, dst = _rm_src(u, window, dil), _rm_dst(u)
            c, s1, s2 = c_ref[dst, :], s1_ref[dst, :], s2_ref[dst, :]
            qo_ref[dst, :] = _dil_prep_tile(PLAIN, q_ref[src, :], gq_ref[...], c, s1, s2, DIL_SCALE).astype(MXU_DTYPE)
            ko_ref[dst, :] = _dil_prep_tile(PLAIN, k_ref[src, :], gk_ref[...], c, s1, s2, 1.0).astype(MXU_DTYPE)
            vo_ref[dst, :] = v_ref[src, :].astype(MXU_DTYPE)

    return pl.pallas_call(
        body,
        name=f"dil_prep_fwd_{dil}",
        grid=(t // SPAN, 4),
        in_specs=[_span_blk(4 * gi), _span_blk(12 + 4 * gi), _span_blk(24 + 4 * gi)] + _DIL_TABLE_SPECS
        + [_const_blk((1, LANES)), _const_blk((1, LANES))],
        out_specs=[_span_blk(0)] * 3,
        out_shape=[jax.ShapeDtypeStruct((t, 512), MXU_DTYPE)] * 3,
        compiler_params=_cparams(("parallel", "parallel")),
    )(z, z, z, tabs, tabs, tabs, gq2, gk2)


def dil_prep_bwd(z, tabs, gq2, gk2, dq, dk, dv, gi):
    window, dil = DIL_GROUPS[gi]
    t = z.shape[0]

    def body(q_ref, k_ref, c_ref, s1_ref, s2_ref, gq_ref, gk_ref, dq_ref, dk_ref, dv_ref, dzq_ref, dzk_ref, dzv_ref, dgq_ref,
             dgk_ref, sq, sk, sv):
        @pl.when((pl.program_id(0) == 0) & (pl.program_id(1) == 0))
        def _():
            dgq_ref[...] = jnp.zeros((1, LANES), F32)
            dgk_ref[...] = jnp.zeros((1, LANES), F32)

        dgs = [jnp.zeros((1, LANES), F32), jnp.zeros((1, LANES), F32)]
        for u in range(SPAN // DIL_NK):
            src, dst = _rm_src(u, window, dil), _rm_dst(u)
            c, s1, s2 = c_ref[dst, :], s1_ref[dst, :], s2_ref[dst, :]
            for idx, (x_ref, g_ref, ct_ref, scr, scale) in enumerate(((q_ref, gq_ref, dq_ref, sq, DIL_SCALE),
                                                                      (k_ref, gk_ref, dk_ref, sk, 1.0))):
                _, vjp = jax.vjp(lambda xv, gv, sc=scale: _dil_prep_tile(DIFF, xv, gv, c, s1, s2, sc), x_ref[src, :], g_ref[...])
                dx, dg = vjp(ct_ref[dst, :])
                scr[src, :] = dx
                dgs[idx] = dgs[idx] + dg
            sv[src, :] = dv_ref[dst, :]
        dgq_ref[...] += dgs[0] + pltpu.roll(dgs[0], DIL_HD, 1)
        dgk_ref[...] += dgs[1] + pltpu.roll(dgs[1], DIL_HD, 1)
        for c0 in range(0, SPAN, 256):
            rows = slice(c0, c0 + 256)
            dzq_ref[rows, :] = sq[rows, :].astype(MXU_DTYPE)
            dzk_ref[rows, :] = sk[rows, :].astype(MXU_DTYPE)
            dzv_ref[rows, :] = sv[rows, :].astype(MXU_DTYPE)

    sds = jax.ShapeDtypeStruct
    return pl.pallas_call(
        body,
        name=f"dil_prep_bwd_{dil}",
        grid=(t // SPAN, 4),
        in_specs=[_span_blk(4 * gi), _span_blk(12 + 4 * gi)] + _DIL_TABLE_SPECS
        + [_const_blk((1, LANES)), _const_blk((1, LANES)), _span_blk(0), _span_blk(0), _span_blk(0)],
        out_specs=[_span_blk(0)] * 3 + [_const_blk((1, LANES))] * 2,
        out_shape=[sds((t, 512), MXU_DTYPE)] * 3 + [sds((1, LANES), F32)] * 2,
        scratch_shapes=[pltpu.VMEM((SPAN, LANES), F32)] * 3,
        compiler_params=_cparams(("arbitrary", "arbitrary")),
    )(z, z, tabs, tabs, tabs, gq2, gk2, dq, dk, dv)


def _band_masks():
    qi = lax.broadcasted_iota(jnp.int32, (DIL_NK, DIL_NK), 0)
    ki = lax.broadcasted_iota(jnp.int32, (DIL_NK, DIL_NK), 1)
    return (ki >= qi), (ki <= qi)


def _pair_heads():
    return [(pair, hh) for pair in range(4) for hh in range(2)]


def _pair_lanes(pair):
    return slice(LANES * pair, LANES * (pair + 1))


def _zero_other_head(mask, x):
    return jnp.where(mask, x, jnp.zeros_like(x))


def dil_attn_fwd(name, q, k, v, dil):
    t = q.shape[0]

    def body(q_ref, kp_ref, kc_ref, vp_ref, vc_ref, o_ref, lse_ref):
        b = pl.program_id(0)
        mprev, mcur = _band_masks()
        mprev = mprev & (b >= dil)
        hm = _head_masks()
        heads = _pair_heads()
        qh = [_zero_other_head(hm[hh], q_ref[:, _pair_lanes(pair)]) for pair, hh in heads]
        sps = [_dot(qh[i], kp_ref[:, _pair_lanes(pair)], NT) for i, (pair, _) in enumerate(heads)]
        scs = [_dot(qh[i], kc_ref[:, _pair_lanes(pair)], NT) for i, (pair, _) in enumerate(heads)]
        o = [jnp.zeros((DIL_NK, LANES), F32) for _ in range(4)]
        lse = [jnp.zeros((DIL_NK, LANES), F32) for _ in range(4)]
        for i, (pair, hh) in enumerate(heads):
            sp, sc = jnp.where(mprev, sps[i], NEG), jnp.where(mcur, scs[i], NEG)
            m = jnp.maximum(jnp.max(sp, axis=1, keepdims=True), jnp.max(sc, axis=1, keepdims=True))
            ep, ec = jnp.exp(sp - m), jnp.exp(sc - m)
            den = jnp.sum(ep, axis=1, keepdims=True) + jnp.sum(ec, axis=1, keepdims=True)
            lanes = _pair_lanes(pair)
            oh = _dot(ep, _zero_other_head(hm[hh], vp_ref[:, lanes]), NN) + _dot(ec, _zero_other_head(hm[hh], vc_ref[:, lanes]), NN)
            o[pair] = o[pair] + oh / den
            lse[pair] = jnp.where(hm[hh], m + jnp.log(den), lse[pair])
        for pair in range(4):
            o_ref[:, _pair_lanes(pair)] = o[pair]
            lse_ref[:, _pair_lanes(pair)] = lse[pair]

    cur = pl.BlockSpec((DIL_NK, 512), lambda b: (b, 0))
    prev = pl.BlockSpec((DIL_NK, 512), lambda b: (jnp.maximum(b - dil, 0), 0))
    sds = jax.ShapeDtypeStruct
    return pl.pallas_call(
        body,
        name=name,
        grid=(t // DIL_NK,),
        in_specs=[cur, prev, cur, prev, cur],
        out_specs=[cur, cur],
        out_shape=[sds((t, 512), F32), sds((t, 512), F32)],
        compiler_params=_cparams(("parallel",)),
    )(q, k, k, v, v)


def dil_attn_bwd(name, q, k, v, do, lse, dd, dil):
    t = q.shape[0]
    nblk = t // DIL_NK

    def body(q_ref, do_ref, l_ref, d_ref, kp_ref, kc_ref, vp_ref, vc_ref, dq_ref, dk_ref, dv_ref):
        b = pl.program_id(0)

        @pl.when(b == 0)
        def _():
            dk_ref[...] = jnp.zeros(dk_ref.shape, F32)
            dv_ref[...] = jnp.zeros(dv_ref.shape, F32)

        mprev, mcur = _band_masks()
        mprev = mprev & (b >= dil)
        rows_c = pl.ds(pl.multiple_of(b * DIL_NK, DIL_NK), DIL_NK)
        rows_p = pl.ds(pl.multiple_of(jnp.maximum(b - dil, 0) * DIL_NK, DIL_NK), DIL_NK)
        hm = _head_masks()
        heads = _pair_heads()
        ln_ = [_pair_lanes(pair) for pair, _ in heads]
        qh = [_zero_other_head(hm[hh], q_ref[:, ln_[i]]) for i, (_, hh) in enumerate(heads)]
        doh = [_zero_other_head(hm[hh], do_ref[:, ln_[i]]) for i, (_, hh) in enumerate(heads)]
        idx = range(len(heads))
        s_p = [_dot(qh[i], kp_ref[:, ln_[i]], NT) for i in idx]
        s_c = [_dot(qh[i], kc_ref[:, ln_[i]], NT) for i in idx]
        dp_p = [_dot(doh[i], vp_ref[:, ln_[i]], NT) for i in idx]
        dp_c = [_dot(doh[i], vc_ref[:, ln_[i]], NT) for i in idx]
        zero = lambda: [jnp.zeros((DIL_NK, LANES), F32) for _ in range(4)]
        dq, dk_p, dk_c, dv_p, dv_c = zero(), zero(), zero(), zero(), zero()
        for i, (pair, hh) in enumerate(heads):
            lse_h, d_h = _row_scalar(l_ref[:, ln_[i]], hm[hh]), _row_scalar(d_ref[:, ln_[i]], hm[hh])
            pp = jnp.exp(jnp.where(mprev, s_p[i] - lse_h, NEG))
            pc = jnp.exp(jnp.where(mcur, s_c[i] - lse_h, NEG))
            dsp, dsc = pp * (dp_p[i] - d_h), pc * (dp_c[i] - d_h)
            dq[pair] = (dq[pair] + _dot(dsp, _zero_other_head(hm[hh], kp_ref[:, ln_[i]]), NN)
                        + _dot(dsc, _zero_other_head(hm[hh], kc_ref[:, ln_[i]]), NN))
            dv_p[pair] = dv_p[pair] + _dot(pp, doh[i], TN)
            dv_c[pair] = dv_c[pair] + _dot(pc, doh[i], TN)
            dk_p[pair] = dk_p[pair] + _dot(dsp, qh[i], TN)
            dk_c[pair] = dk_c[pair] + _dot(dsc, qh[i], TN)
        for pair in range(4):
            lanes = _pair_lanes(pair)
            dq_ref[:, lanes] = dq[pair]
            dk_ref[rows_p, lanes] += dk_p[pair]
            dv_ref[rows_p, lanes] += dv_p[pair]
            dk_ref[rows_c, lanes] += dk_c[pair]
            dv_ref[rows_c, lanes] += dv_c[pair]

    cur = pl.BlockSpec((DIL_NK, 512), lambda b: (b, 0))
    prev = pl.BlockSpec((DIL_NK, 512), lambda b: (jnp.maximum(b - dil, 0), 0))
    whole = pl.BlockSpec((t, 512), lambda b: (0, 0))
    sds = jax.ShapeDtypeStruct
    return pl.pallas_call(
        body,
        name=name,
        grid=(nblk,),
        in_specs=[cur, cur, cur, cur, prev, cur, prev, cur],
        out_specs=[cur, whole, whole],
        out_shape=[sds((t, 512), F32)] * 3,
        compiler_params=_cparams(("arbitrary",)),
    )(q, do, lse, dd, k, k, v, v)


def dil_combine(os_, lses, z):
    t = z.shape[0]

    def body(o1_ref, l1_ref, o2_ref, l2_ref, o3_ref, l3_ref, g_ref, y_ref, o_ref, lse_ref, yt_ref, so2, sl2, so3, sl3):
        for (window, dil), o_in, l_in, so, sl in ((DIL_GROUPS[1], o2_ref, l2_ref, so2, sl2), (DIL_GROUPS[2], o3_ref, l3_ref, so3, sl3)):
            for u in range(SPAN // DIL_NK):
                src, dst = _rm_src(u, window, dil), _rm_dst(u)
                so[src, :] = o_in[dst, :]
                sl[src, :] = l_in[dst, :]
        for c0 in range(0, SPAN, 256):
            rows = slice(c0, c0 + 256)
            la, lb, lc = l1_ref[rows, :], sl2[rows, :], sl3[rows, :]
            mx = jnp.maximum(jnp.maximum(la, lb), lc)
            wa, wb, wc = jnp.exp(la - mx), jnp.exp(lb - mx), jnp.exp(lc - mx)
            tot = wa + wb + wc
            o = (wa * o1_ref[rows, :] + wb * so2[rows, :] + wc * so3[rows, :]) / tot
            y = o * _silu_and_grad(g_ref[rows, :])[0]
            y_ref[rows, :] = y.astype(MXU_DTYPE)
            o_ref[rows, :] = o
            lse_ref[rows, :] = mx + jnp.log(tot)
            yt_ref[:, rows] = y.T.astype(MXU_DTYPE)

    sds = jax.ShapeDtypeStruct
    return pl.pallas_call(
        body,
        name="dil_combine",
        grid=(t // SPAN, 4),
        in_specs=[_span_blk(0)] * 6 + [_span_blk(36)],
        out_specs=[_span_blk(0)] * 3 + [pl.BlockSpec((LANES, SPAN), lambda s, p: (p, s))],
        out_shape=[sds((t, 512), MXU_DTYPE), sds((t, 512), F32), sds((t, 512), F32), sds((512, t), MXU_DTYPE)],
        scratch_shapes=[pltpu.VMEM((SPAN, LANES), F32)] * 4,
        compiler_params=_cparams(("parallel", "parallel")),
    )(os_[0], lses[0], os_[1], lses[1], os_[2], lses[2], z)


def dil_gate_bwd(dy, o, z, lse):
    t = dy.shape[0]

    def body(dy_ref, o_ref, g_ref, lse_ref, dzg_ref, do1_ref, dd1_ref, do2_ref, dd2_ref, l2_ref, do3_ref, dd3_ref, l3_ref, do_scr):
        for c0 in range(0, SPAN, 256):
            rows = slice(c0, c0 + 256)
            dyv, ov = dy_ref[rows, :], o_ref[rows, :]
            silu, dsilu = _silu_and_grad(g_ref[rows, :])
            do = dyv * silu
            do_scr[rows, :] = do
            do1_ref[rows, :] = do.astype(MXU_DTYPE)
            dd1_ref[rows, :] = _seg_sum_impl(do * ov)
            dzg_ref[rows, :] = (dyv * ov * dsilu).astype(MXU_DTYPE)
        for (window, dil), do_o, dd_o, l_o in ((DIL_GROUPS[1], do2_ref, dd2_ref, l2_ref), (DIL_GROUPS[2], do3_ref, dd3_ref, l3_ref)):
            for u in range(SPAN // DIL_NK):
                src, dst = _rm_src(u, window, dil), _rm_dst(u)
                do_o[dst, :] = do_scr[src, :].astype(MXU_DTYPE)
                dd_o[dst, :] = dd1_ref[src, :]
                l_o[dst, :] = lse_ref[src, :]

    sds = jax.ShapeDtypeStruct
    f32, mxu = sds((t, 512), F32), sds((t, 512), MXU_DTYPE)
    return pl.pallas_call(
        body,
        name="dil_gate_bwd",
        grid=(t // SPAN, 4),
        in_specs=[_span_blk(0), _span_blk(0), _span_blk(36), _span_blk(0)],
        out_specs=[_span_blk(0)] * 9,
        out_shape=[mxu, mxu, f32, mxu, f32, f32, mxu, f32, f32],
        scratch_shapes=[pltpu.VMEM((SPAN, LANES), F32)],
        compiler_params=_cparams(("parallel", "parallel")),
    )(dy, o, z, lse)


def _merge_tile(p0, p1, p2, z0, z1, z2, b0, b1, b2):
    return _sigmoid(z0 + b0) * p0 + _sigmoid(z1 + b1) * p1 + _sigmoid(z2 + b2) * p2


def _merge_ins(ps, z, b):
    w = 256
    return ([_row(p, w, 0, True) for p in ps] + [_row(z, w, 4 * i, True) for i in range(3)]
            + [_const(b, w, 4 * i, True) for i in range(3)])


def merge_fwd(ps, z, b):
    def fn(ins, outs, _):
        merged = _merge_tile(*[r[...] for r in ins])
        outs[0][...] = merged.astype(MXU_DTYPE)
        outs[1][...] = merged.T.astype(MXU_DTYPE)

    return rowwise("merge_fwd", fn, z.shape[0], tm=1024, ncol=4, ins=_merge_ins(ps, z, b), outs=[(1024, 256, MXU_DTYPE, True)],
                   touts=[(1024, 256, MXU_DTYPE, True)])


def merge_bwd(dm, ps, z, b):
    def fn(ins, outs, accs):
        _, vjp = jax.vjp(_merge_tile, *[r[...] for r in ins[:9]])
        grads = vjp(ins[9][...])
        for i in range(3):
            outs[i][...] = grads[i].astype(MXU_DTYPE)
            outs[3 + i][...] = grads[3 + i].astype(MXU_DTYPE)
            accs[i][...] += grads[6 + i]

    return rowwise("merge_bwd", fn, z.shape[0], tm=1024, ncol=4, ins=_merge_ins(ps, z, b) + [_row(dm, 256, 0, True)],
                   outs=[(1024, 256, MXU_DTYPE, True)] * 6, accs=[(1, 1024, 256, True)] * 3)


EW_BLOCK_BYTES = 2**21


def _tile2d(r, c):
    if r * c * 4 <= EW_BLOCK_BYTES:
        return r, c
    for tr in (512, 256, 128, 64, 32):
        if r % tr == 0 and tr * c * 4 <= EW_BLOCK_BYTES:
            return tr, c
    for tc in (1024, 512, 256, 128):
        if c % tc == 0 and r * tc * 4 <= EW_BLOCK_BYTES:
            return r, tc
    raise ValueError((r, c))


def adamw(name, w, g, m, v):
    shape = w.shape
    c = shape[-1]
    r = w.size // c
    tr, tc = _tile2d(r, c)
    assert tc == c
    c1, c2 = 1.0 - ADAM_B1 ** ADAM_STEP, 1.0 - ADAM_B2 ** ADAM_STEP

    def body(w_ref, g_ref, m_ref, v_ref, d_ref, mo_ref, vo_ref):
        gv = g_ref[...]
        mn = ADAM_B1 * m_ref[...] + (1.0 - ADAM_B1) * gv
        vn = ADAM_B2 * v_ref[...] + (1.0 - ADAM_B2) * (gv * gv)
        d_ref[...] = -ADAM_LR * ((mn / c1) / (jnp.sqrt(vn / c2) + ADAM_EPS) + ADAM_WD * w_ref[...])
        mo_ref[...] = mn
        vo_ref[...] = vn

    spec = pl.BlockSpec((tr, c), lambda i: (i, 0))
    outs = pl.pallas_call(
        body,
        name=name,
        grid=(r // tr,),
        in_specs=[spec] * 4,
        out_specs=[spec] * 3,
        out_shape=[jax.ShapeDtypeStruct((r, c), F32)] * 3,
        compiler_params=_cparams(("parallel",)),
    )(*[a.reshape(r, c) for a in (w, g, m, v)])
    return [o.reshape(shape) for o in outs]


def adamw_pair(name, w, reduced, received, m, v, c_idx):
    _, r, c = w.shape
    tr, tc = _tile2d(r, c)
    c1, c2 = 1.0 - ADAM_B1 ** ADAM_STEP, 1.0 - ADAM_B2 ** ADAM_STEP

    def body(c_ref, w_ref, r0_ref, x0_ref, r1_ref, x1_ref, m_ref, v_ref, g_ref, d_ref, mo_ref, vo_ref):
        mine = c_ref[0]
        gv = jnp.where(pl.program_id(0) == 0, jnp.where(mine == 0, r0_ref[...], x0_ref[...]),
                       jnp.where(mine == 1, r1_ref[...], x1_ref[...]))
        mn = ADAM_B1 * m_ref[...] + (1.0 - ADAM_B1) * gv
        vn = ADAM_B2 * v_ref[...] + (1.0 - ADAM_B2) * (gv * gv)
        g_ref[...] = gv
        d_ref[...] = -ADAM_LR * ((mn / c1) / (jnp.sqrt(vn / c2) + ADAM_EPS) + ADAM_WD * w_ref[...])
        mo_ref[...] = mn
        vo_ref[...] = vn

    full = pl.BlockSpec((None, tr, tc), lambda h, i, j, cr: (h, i, j))
    half = pl.BlockSpec((tr, tc), lambda h, i, j, cr: (i, j))
    return pl.pallas_call(
        body,
        name=name,
        grid_spec=pltpu.PrefetchScalarGridSpec(
            num_scalar_prefetch=1,
            grid=(2, r // tr, c // tc),
            in_specs=[full, half, half, half, half, full, full],
            out_specs=[full] * 4,
        ),
        out_shape=[jax.ShapeDtypeStruct((2, r, c), F32)] * 4,
        compiler_params=_cparams(("parallel", "parallel", "parallel")),
    )(c_idx, w, reduced[0], received[0], reduced[1], received[1], m, v)


def sum_pair(name, g, la, out_dtype):
    _, r, c = g.shape
    tr, tc = _tile2d(r, c)

    def body(g_ref, la_ref, o_ref):
        o_ref[...] = (g_ref[...] + la_ref[...]).astype(o_ref.dtype)

    spec = pl.BlockSpec((None, tr, tc), lambda s, i, j: (s, i, j))
    return pl.pallas_call(
        body,
        name=name,
        grid=(4, r // tr, c // tc),
        in_specs=[spec, spec],
        out_specs=spec,
        out_shape=jax.ShapeDtypeStruct((4, r, c), out_dtype),
        compiler_params=_cparams(("parallel", "parallel", "parallel")),
    )(g, la)


def sum_chips(name, g, la, lb, s_idx):
    _, r, c = g.shape
    tr, tc = _tile2d(r, c)

    def body(s_ref, g_ref, la_ref, l0_ref, l1_ref, l2_ref, o_ref):
        own = g_ref[...] + la_ref[...]
        o_ref[...] = ((own + l0_ref[...].astype(F32)) + l1_ref[...].astype(F32)) + l2_ref[...].astype(F32)

    own_spec = pl.BlockSpec((None, tr, tc), lambda i, j, sr: (sr[0], i, j))
    lspec = lambda k: pl.BlockSpec((None, tr, tc), lambda i, j, sr: (k, i, j))
    return pl.pallas_call(
        body,
        name=name,
        grid_spec=pltpu.PrefetchScalarGridSpec(
            num_scalar_prefetch=1,
            grid=(r // tr, c // tc),
            in_specs=[own_spec, own_spec, lspec(0), lspec(1), lspec(2)],
            out_specs=pl.BlockSpec((tr, tc), lambda i, j, sr: (i, j)),
        ),
        out_shape=jax.ShapeDtypeStruct((r, c), F32),
        compiler_params=_cparams(("parallel", "parallel")),
    )(s_idx, g, la, lb, lb, lb)


def _place():
    x, y, c = lax.axis_index("x"), lax.axis_index("y"), lax.axis_index("c")
    chips = [(1 - x, y), (x, 1 - y), (1 - x, 1 - y)]
    return x, y, c, chips


HBM_SPEC = pl.BlockSpec(memory_space=pltpu.HBM)


def _comm_call(name, body, ins, out_shapes, n_sem, n_local):
    return pl.pallas_call(
        body,
        name=name,
        in_specs=[HBM_SPEC] * len(ins),
        out_specs=[HBM_SPEC] * len(out_shapes),
        out_shape=out_shapes,
        scratch_shapes=[pltpu.SemaphoreType.DMA((n_sem,)), pltpu.SemaphoreType.DMA((n_sem,)),
                        pltpu.SemaphoreType.DMA((max(n_local, 1),))],
    )(*ins)


def _layer_gather(ins, outs, send, recv, layer):
    n = len(ins)
    x, y, c, chips = _place()
    s = 2 * x + y
    sib = (x, y, 1 - c)
    active = c == layer

    def rc(a, k, src, dst, dev):
        return pltpu.make_async_remote_copy(src_ref=src, dst_ref=dst, send_sem=send.at[6 * a + k], recv_sem=recv.at[6 * a + k],
                                            device_id=dev, device_id_type=MESH)

    def first_hop():
        return [rc(a, j, ins[a], outs[a].at[s], (*chip, c)) for j, chip in enumerate(chips) for a in range(n)]

    def start():
        @pl.when(active)
        def _():
            for cp in first_hop():
                cp.start()

    def finish():
        @pl.when(active)
        def _():
            forwards = []
            for j, (cx, cy) in enumerate(chips):
                for a in range(n):
                    landed = outs[a].at[2 * cx + cy]
                    rc(a, j, landed, landed, sib).wait_recv()
                    forwards.append(rc(a, 3 + j, landed, landed, sib))
                    forwards[-1].start()
            for cp in first_hop() + forwards:
                cp.wait_send()

        @pl.when(jnp.logical_not(active))
        def _():
            for j, (cx, cy) in enumerate(chips):
                for a in range(n):
                    other = outs[a].at[2 * cx + cy]
                    rc(a, 3 + j, other, other, sib).wait_recv()

    return start, finish


def _gather_shapes(ws):
    return [jax.ShapeDtypeStruct((4,) + w.shape, w.dtype) for w in ws]


def gather_hook(ws, layer):
    return (ws, _gather_shapes(ws), 6 * len(ws), lambda i, o, s, r: _layer_gather(i, o, s, r, layer))


def allgather_layer(ws, layer):
    n = len(ws)

    def body(*refs):
        send, recv, _ = refs[2 * n:]
        start, finish = _layer_gather(refs[:n], refs[n:2 * n], send, recv, layer)
        start()
        finish()

    return _comm_call(f"allgather_layer{layer}", body, ws, _gather_shapes(ws), 6 * n, 0)


def exchange_sibling(gs, layer, tag=""):
    n = len(gs)

    def body(*refs):
        send, recv, _ = refs[2 * n:]
        start, finish = _sibling_exchange(refs[:n], refs[n:2 * n], send, recv, layer)
        start()
        finish()

    return _comm_call(f"exchange_sibling{layer}{tag}", body, gs, [jax.ShapeDtypeStruct(g.shape, g.dtype) for g in gs], n, 0)


def _sibling_exchange(ins, outs, send, recv, layer):
    x, y, c, _ = _place()

    def copies():
        return [pltpu.make_async_remote_copy(src_ref=ins[a], dst_ref=outs[a], send_sem=send.at[a], recv_sem=recv.at[a],
                                             device_id=(x, y, 1 - c), device_id_type=MESH) for a in range(len(ins))]

    def start():
        @pl.when(c != layer)
        def _():
            for cp in copies():
                cp.start()

    def finish():
        @pl.when(c != layer)
        def _():
            for cp in copies():
                cp.wait_send()

        @pl.when(c == layer)
        def _():
            for cp in copies():
                cp.wait_recv()

    return start, finish


def sibling_hook(gs, layer):
    return (gs, [jax.ShapeDtypeStruct(g.shape, g.dtype) for g in gs], len(gs),
            lambda i, o, s, r: _sibling_exchange(i, o, s, r, layer))


def chips_hook(ps, layer):
    return (ps, _chips_shapes(ps), 3 * len(ps), lambda i, o, s, r: _chips_exchange(i, o, s, r, layer))


def _chips_exchange(ins, outs, send, recv, layer):
    n = len(ins)
    _, _, c, chips = _place()

    def copies():
        return [pltpu.make_async_remote_copy(src_ref=ins[a].at[2 * cx + cy], dst_ref=outs[a].at[j], send_sem=send.at[3 * a + j],
                                             recv_sem=recv.at[3 * a + j], device_id=(cx, cy, c), device_id_type=MESH)
                for j, (cx, cy) in enumerate(chips) for a in range(n)]

    def start():
        @pl.when(c == layer)
        def _():
            for cp in copies():
                cp.start()

    def finish():
        @pl.when(c == layer)
        def _():
            for cp in copies():
                cp.wait()

    return start, finish


def _chips_shapes(ps):
    return [jax.ShapeDtypeStruct((3,) + p.shape[1:], p.dtype) for p in ps]


def exchange_chips(ps, layer, tag=""):
    n = len(ps)

    def body(*refs):
        send, recv, _ = refs[2 * n:]
        start, finish = _chips_exchange(refs[:n], refs[n:2 * n], send, recv, layer)
        start()
        finish()

    return _comm_call(f"exchange_chips{layer}{tag}", body, ps, _chips_shapes(ps), 3 * n, 0)


def exchange_final(rs, layer, small=None):
    n = len(rs)
    ins_all = list(rs) + ([small] if small is not None else [])

    def body(*refs):
        ni = len(ins_all)
        ins, outs = refs[:ni], refs[ni:2 * ni]
        send, recv, lsem = refs[2 * ni:]
        x, y, c, _ = _place()
        s = 2 * x + y
        to_sib = [pltpu.make_async_remote_copy(src_ref=ins[a], dst_ref=outs[a], send_sem=send.at[a], recv_sem=recv.at[a],
                                               device_id=(x, y, 1 - c), device_id_type=MESH) for a in range(n)]
        rel = [(fx, fy) for fx in (0, 1) for fy in (0, 1)]

        def piece(fx, fy, cc, dst_slot):
            k = n + 2 * (2 * fx + fy) + cc
            return pltpu.make_async_remote_copy(src_ref=ins[n], dst_ref=outs[n].at[dst_slot], send_sem=send.at[k], recv_sem=recv.at[k],
                                                device_id=(x ^ fx, y ^ fy, cc), device_id_type=MESH)

        @pl.when(c == layer)
        def _():
            for cp in to_sib:
                cp.start()
            if small is not None:
                own = pltpu.make_async_copy(ins[n], outs[n].at[s], lsem.at[0])
                own.start()
                sends = [piece(fx, fy, cc, s) for fx, fy in rel for cc in (0, 1) if (fx, fy) != (0, 0) or cc != layer]
                for cp in sends:
                    cp.start()
                for fx, fy in rel[1:]:
                    piece(fx, fy, layer, 2 * (x ^ fx) + (y ^ fy)).wait_recv()
                for cp in sends:
                    cp.wait_send()
                own.wait()
            for cp in to_sib:
                cp.wait_send()

        @pl.when(c != layer)
        def _():
            for cp in to_sib:
                cp.wait_recv()
            if small is not None:
                for fx, fy in rel:
                    piece(fx, fy, 1 - layer, 2 * (x ^ fx) + (y ^ fy)).wait_recv()

    out_shapes = [jax.ShapeDtypeStruct(r.shape, r.dtype) for r in rs]
    if small is not None:
        out_shapes.append(jax.ShapeDtypeStruct((4,) + small.shape, small.dtype))
    res = _comm_call(f"exchange_final{layer}", body, ins_all, out_shapes, n + 8, 1)
    return (res[:n], res[n]) if small is not None else (res[:n], None)


def _pad_in_cols(w):
    z = lambda n: jnp.zeros(w.shape[:-1] + (n,), w.dtype)
    return jnp.concatenate([w[..., :KR_OFF], z(64), w[..., KR_OFF:KR_OFF + MLA_ROPE], z(32), w[..., KR_OFF + MLA_ROPE:]], axis=-1)


def _pad_heads(w, real):
    k = w.shape[0]
    return jnp.pad(w.reshape(k, MLA_HEADS, real), ((0, 0), (0, 0), (0, LANES - real))).reshape(k, MLA_HEADS * LANES)


def _pad_gain(g, real):
    return jnp.pad(g.reshape(1, real), ((0, 0), (0, LANES - real)))


def layer_weights(full):
    w_ukv = full["w_ukv"].reshape(128, MLA_HEADS, 2, 64)
    two = lambda g: jnp.concatenate([g, g]).reshape(1, LANES)
    return dict(
        norm_g=full["norm_g"].reshape(1, -1), w_in=full["w_in"], conv_w=full["conv_w"], conv_b=full["conv_b"].reshape(1, -1),
        wgx=full["w_gate_x"], bgx=full["b_gate_x"].reshape(LRU_BLOCKS, 1, LANES),
        wga=full["w_gate_a"], bga=full["b_gate_a"].reshape(LRU_BLOCKS, 1, LANES),
        lam=full["lru_lambda"].reshape(1, -1), w_lru_o=full["w_lru_o"],
        cq_norm_g=full["cq_norm_g"].reshape(1, -1), ckv_norm_g=full["ckv_norm_g"].reshape(1, -1),
        wq=_pad_heads(full["w_uq"], MLA_QK), wk=_pad_heads(w_ukv[:, :, 0].reshape(128, 512), 64),
        wv=w_ukv[:, :, 1].reshape(128, 512),
        gq=_pad_gain(full["mla_q_norm_g"], MLA_QK), gk=_pad_gain(full["mla_k_norm_g"], MLA_QK),
        w_mla_o=full["w_mla_o"], gq2=two(full["dil_q_norm_g"]), gk2=two(full["dil_k_norm_g"]),
        w_dil_o=full["w_dil_o"], b_merge=full["b_merge"].reshape(1, -1), w_out=full["w_out"],
    )


def layer_fwd(x, w, tabs, gather=None):
    tabs, dtabs = tabs
    h, ht = rmsnorm_fwd(x, w["norm_g"])
    z_lru = mm_nn("in_proj_lru", h, w["w_in"], n=2048, n_off=G_LRU, tm=1024)
    z_mla = mm_nn("in_proj_mla", h, w["w_in"], n=1024, n_off=G_MLA, tm=1024)
    z_dil = mm_nn("in_proj_dil", h, w["w_in"], n=5120, n_off=G_DIL, tm=1024)
    z_mrg = mm_nn("in_proj_mrg", h, w["w_in"], n=3072, n_off=G_MRG, tm=1024)
    rest_hook = None if gather is None else gather_hook(gather[0][1:], gather[1])
    hs, y_lru, yt_lru, *ga_rest = lru_fwd(z_lru, w["conv_w"], w["conv_b"], w["wgx"], w["bgx"], w["wga"], w["bga"], w["lam"],
                                          comm=rest_hook)
    qm, km, vm, vtm = mla_prep_fwd(z_mla, tabs, w)
    o_mla, y_mla, yt_mla, lse_mla, *ga_first = mla_attn_fwd(qm, km, vtm, z_mla,
                                                            gather=None if gather is None else (gather[0][:1], gather[1]))
    gathered = ga_first + ga_rest
    os_, lses, dil_rm = [], [], []
    for gi, (_, dil) in enumerate(DIL_GROUPS):
        qkv = dil_prep_fwd(z_dil, dtabs[gi], w["gq2"], w["gk2"], gi)
        o, lse = dil_attn_fwd(f"dil_attn_fwd_{dil}", *qkv, dil)
        os_.append(o)
        lses.append(lse)
        dil_rm.append(qkv)
    y_dil, o_dil, lse_dil, yt_dil = dil_combine(os_, lses, z_dil)
    ps = [mm_nn("proj_lru", y_lru, w["w_lru_o"]), mm_nn("proj_mla", y_mla, w["w_mla_o"]), mm_nn("proj_dil", y_dil, w["w_dil_o"])]
    merged, merged_t = merge_fwd(ps, z_mrg, w["b_merge"])
    out = mm_nn("out_proj", merged, w["w_out"], add=x)
    res = dict(x=x, ht=ht, z_lru=z_lru, z_mla=z_mla, z_dil=z_dil, z_mrg=z_mrg, hs=hs, yt_lru=yt_lru, qm=qm, km=km, vm=vm, o_mla=o_mla,
               yt_mla=yt_mla, lse_mla=lse_mla, dil_rm=dil_rm, yt_dil=yt_dil, o_dil=o_dil, lse_dil=lse_dil, ps=ps, merged_t=merged_t)
    return out, res, gathered


def layer_bwd(dout, r, w, tabs, prev=None, own=None):
    tabs, dtabs = tabs
    g = {}
    dmerged = mm_nt("out_proj_dx", dout, w["w_out"])
    g["w_out"] = mm_nn("out_proj_dw", r["merged_t"], dout, tm=1024, tn=512, tk=2048)
    dp0, dp1, dp2, dzm0, dzm1, dzm2, db0, db1, db2 = merge_bwd(dmerged, r["ps"], r["z_mrg"], w["b_merge"])
    g["b_merge"] = jnp.concatenate([db0, db1, db2], axis=1).reshape(-1)
    dy_lru = mm_nt("proj_lru_dx", dp0, w["w_lru_o"])
    dy_mla = mm_nt("proj_mla_dx", dp1, w["w_mla_o"])
    dy_dil = mm_nt("proj_dil_dx", dp2, w["w_dil_o"])
    g["w_lru_o"] = mm_nn("proj_lru_dw", r["yt_lru"], dp0, **DW_TILES)
    g["w_mla_o"] = mm_nn("proj_mla_dw", r["yt_mla"], dp1, **DW_TILES)
    g["w_dil_o"] = mm_nn("proj_dil_dw", r["yt_dil"], dp2, **DW_TILES)
    dzx, dzg_lru, dwgx, dbgx, dwga, dbga, dlam, dcw, dcb, *la_prev = lru_bwd(
        r["z_lru"], r["hs"], dy_lru, w["conv_w"], w["conv_b"], w["wgx"], w["bgx"], w["wga"], w["bga"], w["lam"],
        comm=None if prev is None else sibling_hook(prev[0], prev[1]))
    chips = None if prev is None else (prev[2](la_prev), prev[1])
    g.update(w_gate_x=dwgx, b_gate_x=dbgx.reshape(LRU_BLOCKS, LANES), w_gate_a=dwga, b_gate_a=dbga.reshape(LRU_BLOCKS, LANES),
             lru_lambda=dlam.reshape(-1), conv_w=dcw, conv_b=dcb.reshape(-1))
    do_m, dzg_mla, dd_m = mla_gate_bwd(dy_mla, r["o_mla"], r["z_mla"])
    dq_m, dk_m, dv_m, *lb_prev = mla_attn_bwd(r["qm"], r["km"], r["vm"], do_m, r["lse_mla"], dd_m, chips=chips)
    dz_mla3, dg_cq, dg_ckv, dwq, dwk, dwv, dgq, dgk = mla_prep_bwd(r["z_mla"], tabs, w, dq_m, dk_m, dv_m)
    g.update(cq_norm_g=dg_cq.reshape(-1), ckv_norm_g=dg_ckv.reshape(-1), mla_q_norm_g=dgq[0, :MLA_QK], mla_k_norm_g=dgk[0, :MLA_QK])
    g["w_uq"] = dwq.reshape(256, MLA_HEADS, LANES)[:, :, :MLA_QK].reshape(256, MLA_HEADS * MLA_QK)
    g["w_ukv"] = jnp.concatenate([dwk.reshape(128, MLA_HEADS, LANES)[:, :, :64], dwv.reshape(128, MLA_HEADS, 64)], axis=2).reshape(128, 1024)
    dzg_dil, do1, dd1, do2, dd2, l2, do3, dd3, l3 = dil_gate_bwd(dy_dil, r["o_dil"], r["z_dil"], r["lse_dil"])
    stats = [(do1, r["lse_dil"], dd1), (do2, l2, dd2), (do3, l3, dd3)]
    dzq, dzk, dzv, dgq2, dgk2 = [], [], [], [], []
    for gi, (_, dil) in enumerate(DIL_GROUPS):
        dq, dk, dv = dil_attn_bwd(f"dil_attn_bwd_{dil}", *r["dil_rm"][gi], *stats[gi], dil)
        parts = dil_prep_bwd(r["z_dil"], dtabs[gi], w["gq2"], w["gk2"], dq, dk, dv, gi)
        for acc, part in zip((dzq, dzk, dzv, dgq2, dgk2), parts):
            acc.append(part)
    g.update(dil_q_norm_g=sum(dgq2)[0, :DIL_HD], dil_k_norm_g=sum(dgk2)[0, :DIL_HD])
    dz = jnp.concatenate([dzx, dzg_lru, dz_mla3, dzg_mla] + dzq + dzk + dzv + [dzg_dil, dzm0, dzm1, dzm2], axis=1)
    g["w_in"] = in_proj_dw(r["ht"], dz)
    own_hook = None if own is None else chips_hook(*own(g))
    out = mm_nt("in_proj_dx", dz, w["w_in"], tm=1024, tn=1024, tk=IN_PAD // 4, comm=own_hook)
    dh, lb_own = (out, []) if own_hook is None else (out[0], out[1:])
    dx, dng = rmsnorm_bwd(r["x"], dh, dout, w["norm_g"])
    g["norm_g"] = dng.reshape(-1)
    return dx, g, dict(la_prev=la_prev, lb_prev=lb_prev, lb_own=lb_own)


def local_step(x, positions, target, full0, full1=None, gather1=None, reduce=None):
    tabs = rope_tables(positions.reshape(-1, 1))
    tabs = (tabs, [dil_tables(tabs, gi) for gi in range(len(DIL_GROUPS))])
    ws, ress = [], []
    for l in range(2):
        if l == 0:
            ws.append(layer_weights(full0))
            x, res, gathered = layer_fwd(x, ws[0], tabs, gather=None if gather1 is None else (gather1[0], 1))
        else:
            ws.append(layer_weights(full1 if gather1 is None else gather1[1](gathered)))
            x, res, _ = layer_fwd(x, ws[1], tabs)
        ress.append(res)
    dy, loss = loss_head(x, target)
    dy, grads1, _ = layer_bwd(dy, ress[1], ws[1], tabs)
    prev1, own0 = (None, None) if reduce is None else (reduce[0](grads1), reduce[1])
    dy, grads0, landed = layer_bwd(dy, ress[0], ws[0], tabs, prev=prev1, own=own0)
    return loss, dy, [grads0, grads1], landed


WEIGHTS = ["norm_g", "w_in", "conv_w", "conv_b", "w_gate_x", "b_gate_x", "w_gate_a", "b_gate_a", "lru_lambda", "w_lru_o", "cq_norm_g",
           "ckv_norm_g", "w_uq", "w_ukv", "mla_q_norm_g", "mla_k_norm_g", "w_mla_o", "dil_q_norm_g", "dil_k_norm_g", "w_dil_o", "b_merge",
           "w_out"]
SHARDED = {"w_in": 2, "conv_w": 2, "w_lru_o": 1, "w_uq": 2, "w_ukv": 2, "w_mla_o": 2, "w_dil_o": 2, "w_out": 1}
REPLICATED = [n for n in WEIGHTS if n not in SHARDED]
SMALL_ROWS = 144


def kernel(x, positions, norm_g, w_in, conv_w, conv_b, w_gate_x, b_gate_x, w_gate_a, b_gate_a, lru_lambda, w_lru_o, cq_norm_g, ckv_norm_g, w_uq, w_ukv, mla_q_norm_g, mla_k_norm_g, w_mla_o, dil_q_norm_g, dil_k_norm_g, w_dil_o, b_merge, w_out, loss_target, m_norm_g, m_w_in, m_conv_w, m_conv_b, m_w_gate_x, m_b_gate_x, m_w_gate_a, m_b_gate_a, m_lru_lambda, m_w_lru_o, m_cq_norm_g, m_ckv_norm_g, m_w_uq, m_w_ukv, m_mla_q_norm_g, m_mla_k_norm_g, m_w_mla_o, m_dil_q_norm_g, m_dil_k_norm_g, m_w_dil_o, m_b_merge, m_w_out, v_norm_g, v_w_in, v_conv_w, v_conv_b, v_w_gate_x, v_b_gate_x, v_w_gate_a, v_b_gate_a, v_lru_lambda, v_w_lru_o, v_cq_norm_g, v_ckv_norm_g, v_w_uq, v_w_ukv, v_mla_q_norm_g, v_mla_k_norm_g, v_w_mla_o, v_dil_q_norm_g, v_dil_k_norm_g, v_w_dil_o, v_b_merge, v_w_out):
    args = locals()
    w = {n: args[n] for n in WEIGHTS}
    m = {n: args["m_" + n] for n in WEIGHTS}
    v = {n: args["v_" + n] for n in WEIGHTS}
    my_c = lax.axis_index("c").astype(jnp.int32)
    my_s = (2 * lax.axis_index("x") + lax.axis_index("y")).astype(jnp.int32)
    c_idx = my_c.reshape(1)
    s_idx = my_s.reshape(1)

    names = list(SHARDED)
    wire = [[w[n][l] if n == "conv_w" else w[n][l].astype(BF16) for n in names] for l in range(2)]

    def assemble(l, gathered):
        full = {n: w[n][l] for n in REPLICATED}
        for n, ga, own in zip(names, gathered, wire[l]):
            parts = [jnp.where(my_s == s, own, ga[s]) for s in range(4)]
            if n == "w_in":
                parts = [_pad_in_cols(parts[0])] + parts[1:]
            full[n] = jnp.concatenate(parts, axis=SHARDED[n] - 1)
        return full

    full0 = assemble(0, allgather_layer(wire[0], 0))

    def shards_of(grads_l):
        gs = []
        for n in names:
            g_ = grads_l[n]
            if n == "w_in":
                gs.append(g_.reshape(4, IN_WIDTH // 4, D_MODEL))
            else:
                parts = jnp.stack(jnp.split(g_, 4, axis=SHARDED[n] - 1))
                gs.append(parts.reshape(4, -1, parts.shape[-1]))
        return gs

    def pair_sums(l, gs, la, tags):
        dts = [BF16 if g_.size >= 2**19 and t != "small" else F32 for g_, t in zip(gs, tags)]
        return [sum_pair(f"sum_pair{l}_{t}", g_, l_, dt) for t, g_, l_, dt in zip(tags, gs, la, dts)]

    def chip_sums(l, gs, la, lb, tags):
        return [sum_chips(f"sum_chips{l}_{t}", g_, l_, b_, s_idx) for t, g_, l_, b_ in zip(tags, gs, la, lb)]

    stash = {}

    def prev1(grads1):
        stash["gs1"] = shards_of(grads1)
        return stash["gs1"], 1, lambda la: pair_sums(1, stash["gs1"], la, names)

    def own0(grads0):
        stash["gs0"] = shards_of(grads0)
        stash["la0"] = exchange_sibling(stash["gs0"], 0)
        return pair_sums(0, stash["gs0"], stash["la0"], names), 0

    loss, grad_x, grads, landed = local_step(x[0], positions[0], loss_target[0], full0,
                                             gather1=(wire[1], functools.partial(assemble, 1)), reduce=(prev1, own0))
    loss = lax.psum(loss, ("x", "y", "c"))
    reduced1 = chip_sums(1, stash["gs1"], landed["la_prev"], landed["lb_prev"], names)
    received1, _ = exchange_final(reduced1, 1)
    reduced0 = chip_sums(0, stash["gs0"], stash["la0"], landed["lb_own"], names)

    flat = jnp.concatenate([jnp.stack([grads[0][n], grads[1][n]]).reshape(-1) for n in REPLICATED])
    small = [jnp.pad(flat, (0, 4 * SMALL_ROWS * 1024 - flat.size)).reshape(4, SMALL_ROWS, 1024)]
    la_s = exchange_sibling(small, 0, tag="_small")
    lb_s = exchange_chips(pair_sums(0, small, la_s, ["small"]), 0, tag="_small")
    received0, small_all = exchange_final(reduced0, 0, small=chip_sums(0, small, la_s, lb_s, ["small"])[0])

    g_local, delta, new_m, new_v = {}, {}, {}, {}
    for i, n in enumerate(names):
        shp = w[n].shape
        if n == "w_in":
            as3 = lambda a: a.transpose(0, 2, 1)
            back = lambda o: o.transpose(0, 2, 1)
        else:
            as3 = lambda a, rc=reduced0[i].shape: a.reshape((2,) + rc)
            back = lambda o, shp=shp: o.reshape(shp)
        outs = adamw_pair(f"adamw_{n}", as3(w[n]), (reduced0[i], reduced1[i]), (received0[i], received1[i]), as3(m[n]), as3(v[n]),
                          c_idx)
        g_local[n], delta[n], new_m[n], new_v[n] = [back(o) for o in outs]
    flat = small_all.reshape(-1)
    off = 0
    for n in REPLICATED:
        g_local[n] = flat[off:off + w[n].size].reshape(w[n].shape)
        off += w[n].size
        delta[n], new_m[n], new_v[n] = adamw(f"adamw_{n}", w[n], g_local[n], m[n], v[n])
    return (loss, grad_x[None], *[g_local[n] for n in WEIGHTS], *[delta[n] for n in WEIGHTS], *[new_m[n] for n in WEIGHTS],
            *[new_v[n] for n in WEIGHTS])
```

```python
import functools

import jax
import jax.numpy as jnp
from jax import lax
from jax.experimental import pallas as pl
from jax.experimental.pallas import tpu as pltpu

F32 = jnp.float32
BF16 = jnp.bfloat16
MXU_DTYPE = jnp.bfloat16

D_MODEL = 1024
EPS = 1e-6
ROPE_THETA = 10000.0
LRU_BLOCKS = 8
LRU_C = 8.0
MLA_HEADS = 8
MLA_NOPE = 64
MLA_ROPE = 32
MLA_QK = 96
DIL_GROUPS = ((128, 1), (512, 4), (2048, 16))
DIL_HD = 64
DIL_NK = 128
MLA_HPS = 8
IN_WIDTH = 11168
ADAM_LR, ADAM_B1, ADAM_B2, ADAM_EPS, ADAM_WD, ADAM_STEP = 0.001, 0.9, 0.999, 1e-08, 0.01, 10

LANES = 128
G_LRU, G_MLA, G_DIL, G_MRG = 0, 2048, 3072, 8192
IN_PAD = 11264
KR_OFF = 2432

NN = (((1,), (0,)), ((), ()))
NT = (((1,), (1,)), ((), ()))
TN = (((0,), (0,)), ((), ()))
NEG = -1e30
MESH = pl.DeviceIdType.MESH
VMEM_LIMIT = 48 * 2**20


def _cparams(sem):
    return pltpu.CompilerParams(dimension_semantics=sem, vmem_limit_bytes=VMEM_LIMIT)


def _dot(a, b, dims):
    return lax.dot_general(a.astype(MXU_DTYPE), b.astype(MXU_DTYPE), dims, preferred_element_type=F32)


@jax.custom_vjp
def mm(a, w):
    return _dot(a, w, NN)


def _mm_fwd(a, w):
    return _dot(a, w, NN), (a, w)


def _mm_bwd(res, g):
    a, w = res
    return _dot(g, w, NT), _dot(a, g, TN)


mm.defvjp(_mm_fwd, _mm_bwd)


def _seg64_matrix():
    r = lax.broadcasted_iota(jnp.int32, (LANES, LANES), 0) < DIL_HD
    c = lax.broadcasted_iota(jnp.int32, (LANES, LANES), 1) < DIL_HD
    return (r == c).astype(BF16)


def _seg_sum_impl(x):
    b = _seg64_matrix()
    hi = x.astype(BF16)
    r1 = x - hi.astype(F32)
    mid = r1.astype(BF16)
    lo = (r1 - mid.astype(F32)).astype(BF16)
    dot = lambda u: lax.dot_general(u, b, NN, preferred_element_type=F32)
    return dot(hi) + dot(mid) + dot(lo)


@jax.custom_vjp
def seg_sum(x):
    return _seg_sum_impl(x)


seg_sum.defvjp(lambda x: (_seg_sum_impl(x), None), lambda _, g: (_seg_sum_impl(g),))


def _lroll_impl(x, s):
    return pltpu.roll(x, s % LANES, 1)


@functools.partial(jax.custom_vjp, nondiff_argnums=(1,))
def lroll(x, s):
    return _lroll_impl(x, s)


lroll.defvjp(lambda x, s: (_lroll_impl(x, s), None), lambda s, _, g: (_lroll_impl(g, -s),))


class _Ops:
    def __init__(self, diff):
        self.mm = mm if diff else (lambda a, w: _dot(a, w, NN))
        self.seg_sum = seg_sum if diff else _seg_sum_impl
        self.lroll = lroll if diff else _lroll_impl


PLAIN, DIFF = _Ops(False), _Ops(True)


def rms(x, g, n):
    ms = jnp.sum(x * x, axis=-1, keepdims=True) * (1.0 / n)
    return x * lax.rsqrt(ms + EPS) * g


def rope(ops, x, c, s1, s2, half):
    return x * c + ops.lroll(x, -half) * s1 + ops.lroll(x, half) * s2


def _sigmoid(x):
    return 1.0 / (1.0 + jnp.exp(-x))


def _silu_and_grad(g):
    sg = _sigmoid(g)
    return g * sg, sg * (1.0 + g * (1.0 - sg))


def _softplus(x):
    return jnp.maximum(x, 0.0) + jnp.log(1.0 + jnp.exp(-jnp.abs(x)))


def _expm1(y):
    series = y * (1.0 + y * (0.5 + y * (1.0 / 6.0 + y * (1.0 / 24.0 + y * (1.0 / 120.0)))))
    return jnp.where(y > -0.05, series, jnp.exp(jnp.minimum(y, -0.05)) - 1.0)


@jax.custom_vjp
def sigmoid_d(x):
    return _sigmoid(x)


def _sigmoid_d_fwd(x):
    s = _sigmoid(x)
    return s, s


sigmoid_d.defvjp(_sigmoid_d_fwd, lambda s, g: (g * s * (1.0 - s),))


def _decay_mult(log_a):
    return jnp.sqrt(-_expm1(2.0 * log_a))


@jax.custom_vjp
def decay_mult_d(log_a):
    return _decay_mult(log_a)


def _decay_mult_d_fwd(log_a):
    em = _expm1(2.0 * log_a)
    mult = jnp.sqrt(-em)
    return mult, (em, mult)


decay_mult_d.defvjp(_decay_mult_d_fwd, lambda res, g: (-g * (res[0] + 1.0) / res[1],))
PLAIN.sigmoid, DIFF.sigmoid = _sigmoid, sigmoid_d
PLAIN.decay_mult, DIFF.decay_mult = _decay_mult, decay_mult_d


def _host(body, n_in, n_out, n_scr, hook, grid):
    if hook is None:
        return body, [], [], [], [], []
    ins, out_shapes, n_sem, make = hook
    ni, no = len(ins), len(out_shapes)
    hbm = pl.BlockSpec(memory_space=pltpu.HBM)

    def at(corner):
        cond = None
        for axis, size in enumerate(grid):
            here = pl.program_id(axis) == (size - 1 if corner else 0)
            cond = here if cond is None else cond & here
        return cond

    def hosted(*refs):
        a, refs = refs[:n_in], refs[n_in:]
        xi, refs = refs[:ni], refs[ni:]
        o, refs = refs[:n_out], refs[n_out:]
        xo, refs = refs[:no], refs[no:]
        scr, (send, recv) = refs[:n_scr], refs[n_scr:]
        start, finish = make(xi, xo, send, recv)
        pl.when(at(False))(start)
        body(*a, *o, *scr)
        pl.when(at(True))(finish)

    sems = [pltpu.SemaphoreType.DMA((n_sem,)), pltpu.SemaphoreType.DMA((n_sem,))]
    return hosted, list(ins), [hbm] * ni, [hbm] * no, list(out_shapes), sems


def _mm_call(name, a, b, *, mode, m, n, k, a_blk, b_blk, out_dtype, add, tm, tn, tk, comm=None):
    nk = k // tk
    assert m % tm == 0 and n % tn == 0 and k % tk == 0, (name, m, n, k, tm, tn, tk)
    dims = {"nn": NN, "nt": NT, "tn": TN}[mode]

    def body(*refs):
        if add is None:
            a_ref, b_ref, o_ref, *scr = refs
            add_ref = None
        else:
            a_ref, b_ref, add_ref, o_ref, *scr = refs
        part = _dot(a_ref[...], b_ref[...], dims)

        def finish(acc):
            if add_ref is not None:
                acc = acc + add_ref[...]
            o_ref[...] = acc.astype(o_ref.dtype)

        if nk == 1:
            finish(part)
        else:
            (acc_ref,) = scr
            kk = pl.program_id(2)

            @pl.when(kk == 0)
            def _():
                acc_ref[...] = part

            @pl.when(kk > 0)
            def _():
                acc_ref[...] += part

            @pl.when(kk == nk - 1)
            def _():
                finish(acc_ref[...])

    in_specs = [a_blk, b_blk]
    args = [a, b]
    if add is not None:
        in_specs.append(pl.BlockSpec((tm, tn), lambda j, i, kk: (i, j)))
        args.append(add)
    grid = (n // tn, m // tm, nk)
    scratch = [] if nk == 1 else [pltpu.VMEM((tm, tn), F32)]
    body, x_in, x_in_specs, x_out_specs, x_out_shapes, sems = _host(body, len(args), 1, len(scratch), comm, grid)
    res = pl.pallas_call(
        body,
        name=name,
        grid=grid,
        in_specs=in_specs + x_in_specs,
        out_specs=[pl.BlockSpec((tm, tn), lambda j, i, kk: (i, j))] + x_out_specs,
        out_shape=[jax.ShapeDtypeStruct((m, n), out_dtype)] + x_out_shapes,
        scratch_shapes=scratch + sems,
        compiler_params=_cparams(("arbitrary",) * 3 if comm is not None else ("parallel", "parallel", "arbitrary")),
    )(*args, *x_in)
    return res[0] if comm is None else res


def mm_nn(name, a, b, *, n=None, n_off=0, out_dtype=F32, add=None, tm=1024, tn=1024, tk=1024):
    m, k = a.shape
    n = b.shape[1] if n is None else n
    tm, tn, tk = min(tm, m), min(tn, n), min(k, tk)
    ob = n_off // tn
    assert n_off % tn == 0
    return _mm_call(name, a, b, mode="nn", m=m, n=n, k=k, out_dtype=out_dtype, add=add, tm=tm, tn=tn, tk=tk,
                    a_blk=pl.BlockSpec((tm, tk), lambda j, i, kk: (i, kk)),
                    b_blk=pl.BlockSpec((tk, tn), lambda j, i, kk: (kk, j + ob)))


def mm_nt(name, a, b, *, out_dtype=F32, tm=1024, tn=1024, tk=1024, comm=None):
    m, k = a.shape
    n = b.shape[0]
    tm, tn, tk = min(tm, m), min(tn, n), min(k, tk)
    return _mm_call(name, a, b, mode="nt", m=m, n=n, k=k, out_dtype=out_dtype, add=None, tm=tm, tn=tn, tk=tk, comm=comm,
                    a_blk=pl.BlockSpec((tm, tk), lambda j, i, kk: (i, kk)),
                    b_blk=pl.BlockSpec((tn, tk), lambda j, i, kk: (j, kk)))


DW_TILES = dict(tm=1024, tn=512, tk=4096)


def in_proj_dw(ht, dz):
    t = ht.shape[1]
    tn = 512
    nchunk = IN_PAD // tn
    kr_chunk = KR_OFF // tn
    lo = KR_OFF - kr_chunk * tn

    def body(ht_ref, dz_ref, out_ref, buf, sems):
        j = pl.program_id(0)
        slot = j % 2
        buf[slot] = _dot(ht_ref[...], dz_ref[...], NN).T

        def copies(c, s):
            dst = jnp.where(c < kr_chunk, c * tn, (c - 1) * tn + lo + MLA_ROPE)
            plain = [pltpu.make_async_copy(buf.at[s], out_ref.at[pl.ds(pl.multiple_of(dst, 8), tn)], sems.at[s, 0])]
            split = [pltpu.make_async_copy(buf.at[s, 0:lo], out_ref.at[KR_OFF - lo:KR_OFF], sems.at[s, 0]),
                     pltpu.make_async_copy(buf.at[s, lo + 64:lo + 96], out_ref.at[KR_OFF:KR_OFF + MLA_ROPE], sems.at[s, 1])]
            return plain, split

        def each(c, s, action):
            plain, split = copies(c, s)

            @pl.when(c != kr_chunk)
            def _():
                for cp in plain:
                    action(cp)

            @pl.when(c == kr_chunk)
            def _():
                for cp in split:
                    action(cp)

        @pl.when(j > 0)
        def _():
            each(j - 1, 1 - slot, lambda cp: cp.wait())

        each(j, slot, lambda cp: cp.start())

        @pl.when(j == nchunk - 1)
        def _():
            each(j, slot, lambda cp: cp.wait())

    return pl.pallas_call(
        body,
        name="in_proj_dw",
        grid=(nchunk,),
        in_specs=[pl.BlockSpec((D_MODEL, t), lambda j: (0, 0)), pl.BlockSpec((t, tn), lambda j: (0, j))],
        out_specs=pl.BlockSpec(memory_space=pltpu.HBM),
        out_shape=jax.ShapeDtypeStruct((IN_WIDTH, D_MODEL), F32),
        scratch_shapes=[pltpu.VMEM((2, tn, D_MODEL), F32), pltpu.SemaphoreType.DMA((2, 2))],
        compiler_params=_cparams(("arbitrary",)),
    )(ht, dz)


def rowwise(name, fn, t, *, tm=256, ncol=1, ins=(), outs=(), touts=(), accs=()):
    n_in, n_out, n_acc = len(ins), len(outs) + len(touts), len(accs)

    def zero_when(ref, cond):
        @pl.when(cond)
        def _():
            ref[...] = jnp.zeros(ref.shape, ref.dtype)

    def body(*refs):
        in_refs, out_refs, acc_refs = refs[:n_in], refs[n_in:n_in + n_out], refs[n_in + n_out:]
        j, i = pl.program_id(0), pl.program_id(1)
        for ref, (_, _, _, cd) in zip(acc_refs, accs):
            zero_when(ref, (i == 0) if cd else ((i == 0) & (j == 0)))
        fn(in_refs, out_refs, acc_refs)

    def in_spec(arr, w, base, cd, rd):
        rows = tm if rd else arr.shape[0]
        return pl.BlockSpec((rows, w), lambda j, i: (i if rd else 0, base + (j if cd else 0)))

    in_specs = [in_spec(*e) for e in ins]
    out_specs = [pl.BlockSpec((tm, w), (lambda j, i, cd=cd: (i, j if cd else 0))) for (_, w, _, cd) in outs]
    out_specs += [pl.BlockSpec((w, tm), (lambda j, i, cd=cd: (j if cd else 0, i))) for (_, w, _, cd) in touts]
    out_specs += [pl.BlockSpec((r, w), (lambda j, i, cd=cd: (0, j if cd else 0))) for (r, _, w, cd) in accs]
    out_shape = [jax.ShapeDtypeStruct((t, c), dt) for (c, _, dt, _) in outs]
    out_shape += [jax.ShapeDtypeStruct((r, t), dt) for (r, _, dt, _) in touts]
    out_shape += [jax.ShapeDtypeStruct((r, c), F32) for (r, c, _, _) in accs]
    res = pl.pallas_call(
        body,
        name=name,
        grid=(ncol, t // tm),
        in_specs=in_specs,
        out_specs=out_specs,
        out_shape=out_shape,
        compiler_params=_cparams(("arbitrary", "arbitrary") if accs else ("parallel", "parallel")),
    )(*[e[0] for e in ins])
    return res


def _row(arr, w=None, base=0, cd=False):
    return (arr, arr.shape[1] if w is None else w, base, cd, True)


def _const(arr, w=None, base=0, cd=False):
    return (arr, arr.shape[1] if w is None else w, base, cd, False)


def rope_tables(positions):
    t = positions.shape[0]

    def fn(ins, outs, _):
        pos = ins[0][...].astype(F32)
        lane = lax.broadcasted_iota(jnp.int32, (1, LANES), 1)
        log_theta = jnp.log(jnp.float32(ROPE_THETA))
        jm = lane - MLA_NOPE
        idx = jnp.clip(jnp.where(jm < 16, jm, jm - 16), 0, 15).astype(F32)
        ang = pos * jnp.exp(-(idx * (2.0 / MLA_ROPE)) * log_theta)
        cos, sin = jnp.cos(ang), jnp.sin(ang)
        in_rope = (lane >= MLA_NOPE) & (lane < MLA_QK)
        outs[0][:, 0:128] = jnp.where(lane < MLA_NOPE, 1.0, jnp.where(in_rope, cos, 0.0))
        outs[0][:, 128:256] = jnp.where(in_rope & (jm < 16), -sin, 0.0)
        outs[0][:, 256:384] = jnp.where(in_rope & (jm >= 16), sin, 0.0)
        jd = lane & (DIL_HD - 1)
        idx = (jd & 31).astype(F32)
        ang = pos * jnp.exp(-(idx * (2.0 / DIL_HD)) * log_theta)
        cos, sin = jnp.cos(ang), jnp.sin(ang)
        outs[0][:, 384:512] = cos
        outs[0][:, 512:640] = jnp.where(jd < 32, -sin, 0.0)
        outs[0][:, 640:768] = jnp.where(jd >= 32, sin, 0.0)

    return rowwise("rope_tables", fn, t, tm=1024, ins=[_row(positions)], outs=[(768, 768, F32, False)])[0]


def rmsnorm_fwd(x, g):
    def fn(ins, outs, _):
        h = rms(ins[0][...], ins[1][...], D_MODEL)
        outs[0][...] = h.astype(MXU_DTYPE)
        outs[1][...] = h.T.astype(MXU_DTYPE)

    return rowwise("rmsnorm_fwd", fn, x.shape[0], tm=512, ins=[_row(x), _const(g)], outs=[(D_MODEL, D_MODEL, MXU_DTYPE, False)],
                   touts=[(D_MODEL, D_MODEL, MXU_DTYPE, False)])


def rmsnorm_bwd(x, dh, dres, g):
    def fn(ins, outs, accs):
        _, vjp = jax.vjp(lambda xv, gv: rms(xv, gv, D_MODEL), ins[0][...], ins[3][...])
        dx, dg = vjp(ins[1][...])
        outs[0][...] = ins[2][...] + dx
        accs[0][...] += dg

    return rowwise("rmsnorm_bwd", fn, x.shape[0], tm=512, ins=[_row(x), _row(dh), _row(dres), _const(g)],
                   outs=[(D_MODEL, D_MODEL, F32, False)], accs=[(1, D_MODEL, D_MODEL, False)])


def loss_head(y, target):
    def fn(ins, outs, accs):
        err = ins[0][...] - ins[1][...]
        outs[0][...] = err * (1.0 / D_MODEL)
        accs[0][...] += jnp.sum(err * err, axis=0, keepdims=True)
        accs[1][...] = jnp.broadcast_to(jnp.sum(accs[0][...], keepdims=True), (1, LANES))

    dy, _, tot = rowwise("loss_head", fn, y.shape[0], tm=512, ins=[_row(y), _row(target)], outs=[(D_MODEL, D_MODEL, F32, False)],
                         accs=[(1, D_MODEL, D_MODEL, False), (1, LANES, LANES, False)])
    return dy, tot[0, 0] * (0.5 / D_MODEL)


def _shift_rows(v, d, fill, reverse):
    tb = v.shape[0]
    if d % 8 == 0:
        pad = jnp.full((d, v.shape[1]), fill, v.dtype)
        return jnp.concatenate([v[d:], pad] if reverse else [pad, v[:tb - d]], axis=0)
    rows = lax.broadcasted_iota(jnp.int32, v.shape, 0)
    if not reverse:
        return jnp.where(rows >= d, pltpu.roll(v, d, 0), fill)
    return jnp.where(rows < tb - d, pltpu.roll(v, tb - d, 0), fill)


def _scan_tile(a, b, reverse):
    d = 1
    while d < a.shape[0]:
        b = b + a * _shift_rows(b, d, 0.0, reverse)
        a = a * _shift_rows(a, d, 1.0, reverse)
        d *= 2
    return a, b


def _lru_gates(ops, xc, wgx, bgx, wga, bga, lam):
    gx = ops.sigmoid(ops.mm(xc, wgx) + bgx)
    ga = ops.sigmoid(ops.mm(xc, wga) + bga)
    log_a = -LRU_C * ga * _softplus(-lam)
    a = jnp.exp(log_a)
    return a, ops.decay_mult(log_a) * (gx * xc)


def _shifted_inputs(x, halo, tb):
    rows = lax.broadcasted_iota(jnp.int32, x.shape, 0)
    pad = jnp.zeros((tb - 8, LANES), F32)
    out = []
    for d in (3, 2, 1):
        head = jnp.concatenate([pltpu.roll(halo, d, 0), pad], axis=0)
        out.append(jnp.where(rows >= d, pltpu.roll(x, d, 0), head))
    return out + [x]


def _lru_specs(nt, tb, reverse):
    hb = tb // 8
    tt = (lambda t: nt - 1 - t) if reverse else (lambda t: t)
    blk = lambda off: pl.BlockSpec((tb, LANES), lambda n, t: (tt(t), n + off))
    halo = lambda off: pl.BlockSpec((8, LANES), lambda n, t: (jnp.maximum(tt(t) * hb - 1, 0), n + off))
    chan = lambda r: pl.BlockSpec((r, LANES), lambda n, t: (0, n))
    wblk = pl.BlockSpec((None, LANES, LANES), lambda n, t: (n, 0, 0))
    bblk = pl.BlockSpec((None, 1, LANES), lambda n, t: (n, 0, 0))
    return blk, halo, chan, wblk, bblk


def lru_fwd(z, conv_w, conv_b, wgx, bgx, wga, bga, lam, *, tb=512, comm=None):
    t = z.shape[0]
    nt = t // tb
    blk, halo, chan, wblk, bblk = _lru_specs(nt, tb, False)

    def body(x_ref, xh_ref, g_ref, cw_ref, cb_ref, wgx_ref, bgx_ref, wga_ref, bga_ref, lam_ref, h_ref, y_ref, yt_ref, carry_ref):
        ti = pl.program_id(1)

        @pl.when(ti == 0)
        def _():
            carry_ref[...] = jnp.zeros((8, LANES), F32)

        x = x_ref[...]
        hal = jnp.where(ti > 0, xh_ref[...], 0.0)
        xs = _shifted_inputs(x, hal, tb)
        xc = cb_ref[...] + sum(xs[k] * cw_ref[k:k + 1, :] for k in range(4))
        a, b = _lru_gates(PLAIN, xc, wgx_ref[...], bgx_ref[...], wga_ref[...], bga_ref[...], lam_ref[...])
        acum, h0 = _scan_tile(a, b, False)
        h = h0 + acum * carry_ref[7:8, :]
        carry_ref[...] = h[tb - 8:tb, :]
        h_ref[...] = h
        y = h * _silu_and_grad(g_ref[...])[0]
        y_ref[...] = y.astype(MXU_DTYPE)
        yt_ref[...] = y.T.astype(MXU_DTYPE)

    grid = (LRU_BLOCKS, nt)
    body, x_in, x_in_specs, x_out_specs, x_out_shapes, sems = _host(body, 10, 3, 1, comm, grid)
    return pl.pallas_call(
        body,
        name="lru_fwd" if comm is None else "lru_fwd_comm",
        grid=grid,
        in_specs=[blk(0), halo(0), blk(LRU_BLOCKS), chan(4), chan(1), wblk, bblk, wblk, bblk, chan(1)] + x_in_specs,
        out_specs=[blk(0), blk(0), pl.BlockSpec((LANES, tb), lambda n, t_: (n, t_))] + x_out_specs,
        out_shape=[jax.ShapeDtypeStruct((t, 1024), F32), jax.ShapeDtypeStruct((t, 1024), MXU_DTYPE),
                   jax.ShapeDtypeStruct((1024, t), MXU_DTYPE)] + x_out_shapes,
        scratch_shapes=[pltpu.VMEM((8, LANES), F32)] + sems,
        compiler_params=_cparams(("arbitrary", "arbitrary") if comm is not None else ("parallel", "arbitrary")),
    )(z, z, z, conv_w, conv_b, wgx, bgx, wga, bga, lam, *x_in)


def lru_bwd(z, h, dy, conv_w, conv_b, wgx, bgx, wga, bga, lam, *, tb=512, comm=None):
    t = z.shape[0]
    nt = t // tb
    blk, halo, chan, wblk, bblk = _lru_specs(nt, tb, True)

    def body(x_ref, xh_ref, g_ref, h_ref, hh_ref, dy_ref, cw_ref, cb_ref, wgx_ref, bgx_ref, wga_ref, bga_ref, lam_ref,
             dzx_ref, dzg_ref, dwgx_ref, dbgx_ref, dwga_ref, dbga_ref, dlam_ref, dcw_ref, dcb_ref,
             gcar_ref, acar_ref, xcar_ref):
        ti = pl.program_id(1)
        has_earlier = ti < nt - 1

        @pl.when(ti == 0)
        def _():
            for ref in (dwgx_ref, dbgx_ref, dwga_ref, dbga_ref, dlam_ref, dcw_ref, dcb_ref, gcar_ref, acar_ref, xcar_ref):
                ref[...] = jnp.zeros(ref.shape, F32)

        rows = lax.broadcasted_iota(jnp.int32, (tb, LANES), 0)
        x = x_ref[...]
        hal = jnp.where(has_earlier, xh_ref[...], 0.0)
        xs = _shifted_inputs(x, hal, tb)
        xc = cb_ref[...] + sum(xs[k] * cw_ref[k:k + 1, :] for k in range(4))
        (a, _), vjp = jax.vjp(functools.partial(_lru_gates, DIFF), xc, wgx_ref[...], bgx_ref[...], wga_ref[...],
                              bga_ref[...], lam_ref[...])
        g, h, dyv = g_ref[...], h_ref[...], dy_ref[...]
        silu, dsilu = _silu_and_grad(g)
        dzg_ref[...] = (dyv * h * dsilu).astype(MXU_DTYPE)
        a_next = jnp.where(rows < tb - 1, pltpu.roll(a, tb - 1, 0), acar_ref[0:1, :])
        acum, g0 = _scan_tile(a_next, dyv * silu, True)
        gt = g0 + acum * gcar_ref[0:1, :]
        h_prev = jnp.where(rows >= 1, pltpu.roll(h, 1, 0), jnp.where(has_earlier, hh_ref[7:8, :], 0.0))
        dxc, dwgx, dbgx, dwga, dbga, dlam = vjp((gt * h_prev, gt))
        later = xcar_ref[...]
        gcar_ref[...] = gt[0:8, :]
        acar_ref[...] = a[0:8, :]
        xcar_ref[...] = dxc[0:8, :]
        dx = dxc * cw_ref[3:4, :]
        pad = jnp.zeros((tb - 8, LANES), F32)
        for d in (1, 2, 3):
            tail = jnp.concatenate([pad, pltpu.roll(later, 8 - d, 0)], axis=0)
            up = jnp.where(rows < tb - d, pltpu.roll(dxc, tb - d, 0), tail)
            dx = dx + up * cw_ref[3 - d:4 - d, :]
        dzx_ref[...] = dx.astype(MXU_DTYPE)
        for k in range(4):
            dcw_ref[k:k + 1, :] += jnp.sum(dxc * xs[k], axis=0, keepdims=True)
        dcb_ref[...] += jnp.sum(dxc, axis=0, keepdims=True)
        dwgx_ref[...] += dwgx
        dbgx_ref[...] += dbgx
        dwga_ref[...] += dwga
        dbga_ref[...] += dbga
        dlam_ref[...] += dlam

    sds = jax.ShapeDtypeStruct
    grid = (LRU_BLOCKS, nt)
    body, x_in, x_in_specs, x_out_specs, x_out_shapes, sems = _host(body, 13, 9, 3, comm, grid)
    return pl.pallas_call(
        body,
        name="lru_bwd" if comm is None else "lru_bwd_comm",
        grid=grid,
        in_specs=[blk(0), halo(0), blk(LRU_BLOCKS), blk(0), halo(0), blk(0), chan(4), chan(1), wblk, bblk, wblk, bblk, chan(1)]
        + x_in_specs,
        out_specs=[blk(0), blk(0), wblk, bblk, wblk, bblk, chan(1), chan(4), chan(1)] + x_out_specs,
        out_shape=[sds((t, 1024), MXU_DTYPE), sds((t, 1024), MXU_DTYPE), sds(wgx.shape, F32), sds(bgx.shape, F32), sds(wga.shape, F32),
                   sds(bga.shape, F32), sds((1, 1024), F32), sds((4, 1024), F32), sds((1, 1024), F32)] + x_out_shapes,
        scratch_shapes=[pltpu.VMEM((8, LANES), F32)] * 3 + sems,
        compiler_params=_cparams(("arbitrary", "arbitrary") if comm is not None else ("parallel", "arbitrary")),
    )(z, z, z, h, h, dy, conv_w, conv_b, wgx, bgx, wga, bga, lam, *x_in)


def _mla_prep_tile(ops, cq, ckv, krp, c, s1, s2, g_cq, g_ckv, wq, wk, wv, gq, gk):
    cqn = rms(cq, g_cq, 256)
    ckvn = rms(ckv, g_ckv, 128)
    v = ops.mm(ckvn, wv)
    qs, ks = [], []
    for hd in range(MLA_HEADS):
        q = rms(ops.mm(cqn, wq[hd]), gq, MLA_QK)
        k = rms(ops.mm(ckvn, wk[hd]) + krp, gk, MLA_QK)
        qs.append(rope(ops, q, c, s1, s2, 16) * (MLA_QK ** -0.5))
        ks.append(rope(ops, k, c, s1, s2, 16))
    return tuple(qs), tuple(ks), v


def _mla_prep_args(ins):
    z_cq, z_ckv, z_kr, tc, ts1, ts2, g_cq, g_ckv, wq, wk, wv, gq, gk = ins[:13]
    heads = lambda w: tuple(w[:, LANES * hd:LANES * (hd + 1)] for hd in range(MLA_HEADS))
    return (z_cq[...], z_ckv[...], z_kr[...], tc[...], ts1[...], ts2[...], g_cq[...], g_ckv[...], heads(wq), heads(wk),
            wv[...], gq[...], gk[...])


def _mla_prep_ins(z, tabs, w):
    return [_row(z, 256, 0), _row(z, 128, 2), _row(z, 128, 3), _row(tabs, 128, 0), _row(tabs, 128, 1), _row(tabs, 128, 2),
            _const(w["cq_norm_g"]), _const(w["ckv_norm_g"]), _const(w["wq"]), _const(w["wk"]), _const(w["wv"]),
            _const(w["gq"]), _const(w["gk"])]


def mla_prep_fwd(z, tabs, w):
    def fn(ins, outs, _):
        qs, ks, v = _mla_prep_tile(PLAIN, *_mla_prep_args(ins))
        for hd in range(MLA_HEADS):
            outs[0][:, LANES * hd:LANES * (hd + 1)] = qs[hd].astype(MXU_DTYPE)
            outs[1][:, LANES * hd:LANES * (hd + 1)] = ks[hd].astype(MXU_DTYPE)
        outs[2][...] = v.astype(MXU_DTYPE)
        outs[3][...] = v.T.astype(MXU_DTYPE)

    return rowwise("mla_prep_fwd", fn, z.shape[0], tm=512, ins=_mla_prep_ins(z, tabs, w),
                   outs=[(1024, 1024, MXU_DTYPE, False), (1024, 1024, MXU_DTYPE, False), (512, 512, MXU_DTYPE, False)],
                   touts=[(512, 512, MXU_DTYPE, False)])


def mla_prep_bwd(z, tabs, w, dq, dk, dv):
    def fn(ins, outs, accs):
        args = _mla_prep_args(ins)
        _, vjp = jax.vjp(functools.partial(_mla_prep_tile, DIFF), *args)
        heads = lambda ref: tuple(ref[:, LANES * hd:LANES * (hd + 1)] for hd in range(MLA_HEADS))
        dcq, dckv, dkr, _, _, _, dg_cq, dg_ckv, dwq, dwk, dwv, dgq, dgk = vjp((heads(ins[13]), heads(ins[14]), ins[15][...]))
        lane = lax.broadcasted_iota(jnp.int32, (1, LANES), 1)
        outs[0][:, 0:256] = dcq.astype(MXU_DTYPE)
        outs[0][:, 256:384] = dckv.astype(MXU_DTYPE)
        outs[0][:, 384:512] = jnp.where((lane >= MLA_NOPE) & (lane < MLA_QK), dkr, 0.0).astype(MXU_DTYPE)
        accs[0][...] += dg_cq
        accs[1][...] += dg_ckv
        for hd in range(MLA_HEADS):
            accs[2][:, LANES * hd:LANES * (hd + 1)] += dwq[hd]
            accs[3][:, LANES * hd:LANES * (hd + 1)] += dwk[hd]
        accs[4][...] += dwv
        accs[5][...] += dgq
        accs[6][...] += dgk

    return rowwise("mla_prep_bwd", fn, z.shape[0], tm=512, ins=_mla_prep_ins(z, tabs, w) + [_row(dq), _row(dk), _row(dv)],
                   outs=[(512, 512, MXU_DTYPE, False)],
                   accs=[(1, 256, 256, False), (1, 128, 128, False), (256, 1024, 1024, False), (128, 1024, 1024, False),
                         (128, 512, 512, False), (1, 128, 128, False), (1, 128, 128, False)])


def _head_masks():
    lane = lax.broadcasted_iota(jnp.int32, (1, LANES), 1)
    return (lane < DIL_HD, lane >= DIL_HD)


def _row_scalar(tile, mask):
    return jnp.max(jnp.where(mask, tile, -jnp.inf), axis=-1, keepdims=True)


def _causal_tiles(nq, by_key):
    pairs = [(i, j) for i in range(nq) for j in range(i + 1)]
    if by_key:
        pairs.sort(key=lambda ij: (ij[1], ij[0]))
    return (jnp.asarray([ij[0] for ij in pairs], jnp.int32), jnp.asarray([ij[1] for ij in pairs], jnp.int32))


def mla_attn_fwd(q, k, vt, z, *, tq=256, gather=None):
    t = q.shape[0]
    nq = t // tq
    it, jt = _causal_tiles(nq, False)
    hps, wq, wv = MLA_HPS, LANES * MLA_HPS, 64 * MLA_HPS
    ws, layer = gather if gather is not None else ([], 0)
    ng = len(ws)
    ngrp, nsteps = MLA_HEADS // hps, int(it.shape[0])

    def body(it_ref, jt_ref, q_ref, k_ref, vt_ref, g_ref, *rest):
        w_refs, rest = rest[:ng], rest[ng:]
        o_ref, y_ref, yt_ref, lse_ref = rest[:4]
        ga_refs, rest = rest[4:4 + ng], rest[4 + ng:]
        m_scr, l_scr, acc_scr = rest[:3]
        step = pl.program_id(1)
        i, j = it_ref[step], jt_ref[step]
        if ng:
            gather_start, gather_finish = _layer_gather(w_refs, ga_refs, rest[3], rest[4], layer)

            @pl.when((pl.program_id(0) == 0) & (step == 0))
            def _():
                gather_start()

        @pl.when(j == 0)
        def _():
            m_scr[...] = jnp.full(m_scr.shape, NEG, F32)
            l_scr[...] = jnp.zeros(l_scr.shape, F32)
            acc_scr[...] = jnp.zeros(acc_scr.shape, F32)

        def update(diagonal):
            heads = range(hps)
            lanes = [slice(LANES * hh, LANES * (hh + 1)) for hh in heads]
            rows = [slice(64 * hh, 64 * (hh + 1)) for hh in heads]
            sts = [_dot(k_ref[:, lanes[hh]], q_ref[:, lanes[hh]], NT) for hh in heads]
            m_prev = [m_scr[hh:hh + 1, :] for hh in heads]
            l_prev = [l_scr[hh:hh + 1, :] for hh in heads]
            acc_prev = [acc_scr[rows[hh], :] for hh in heads]
            m_new, l_new, acc_new = [], [], []
            for hh in heads:
                st = sts[hh]
                if diagonal:
                    key = lax.broadcasted_iota(jnp.int32, (tq, tq), 0)
                    qry = lax.broadcasted_iota(jnp.int32, (tq, tq), 1)
                    st = jnp.where(key <= qry, st, NEG)
                m_new.append(jnp.maximum(m_prev[hh], jnp.max(st, axis=0, keepdims=True)))
                alpha = jnp.exp(m_prev[hh] - m_new[hh])
                pt = jnp.exp(st - m_new[hh])
                l_new.append(alpha * l_prev[hh] + jnp.sum(pt, axis=0, keepdims=True))
                acc_new.append(alpha * acc_prev[hh] + _dot(vt_ref[rows[hh], :], pt, NN))
            for hh in heads:
                m_scr[hh:hh + 1, :] = m_new[hh]
                l_scr[hh:hh + 1, :] = l_new[hh]
                acc_scr[rows[hh], :] = acc_new[hh]

        @pl.when(j < i)
        def _():
            update(False)

        @pl.when(j == i)
        def _():
            update(True)
            lse_ref[...] = jnp.zeros(lse_ref.shape, F32)
            for hh in range(hps):
                rows = slice(64 * hh, 64 * (hh + 1))
                acc_scr[rows, :] = acc_scr[rows, :] / l_scr[hh:hh + 1, :]
                lse_ref[hh:hh + 1, :] = m_scr[hh:hh + 1, :] + jnp.log(l_scr[hh:hh + 1, :])
            o = acc_scr[...].T
            o_ref[...] = o
            y = o * _silu_and_grad(g_ref[...])[0]
            y_ref[...] = y.astype(MXU_DTYPE)
            yt_ref[...] = y.T.astype(MXU_DTYPE)

        if ng:
            @pl.when((pl.program_id(0) == ngrp - 1) & (step == nsteps - 1))
            def _():
                gather_finish()

    qo = lambda w, off=0: pl.BlockSpec((tq, w), lambda p, s, it_, jt_: (it_[s], p + off))
    sds = jax.ShapeDtypeStruct
    comm_scratch = [pltpu.SemaphoreType.DMA((6 * ng,)), pltpu.SemaphoreType.DMA((6 * ng,))] if ng else []
    return pl.pallas_call(
        body,
        name="mla_attn_fwd_gather" if ng else "mla_attn_fwd",
        grid_spec=pltpu.PrefetchScalarGridSpec(
            num_scalar_prefetch=2,
            grid=(ngrp, nsteps),
            in_specs=[qo(wq), pl.BlockSpec((tq, wq), lambda p, s, it_, jt_: (jt_[s], p)),
                      pl.BlockSpec((wv, tq), lambda p, s, it_, jt_: (p, jt_[s])), qo(wv, 512 // wv)] + [HBM_SPEC] * ng,
            out_specs=[qo(wv), qo(wv), pl.BlockSpec((wv, tq), lambda p, s, it_, jt_: (p, it_[s])),
                       pl.BlockSpec((None, 8, tq), lambda p, s, it_, jt_: (p, 0, it_[s]))] + [HBM_SPEC] * ng,
            scratch_shapes=[pltpu.VMEM((8, tq), F32), pltpu.VMEM((8, tq), F32), pltpu.VMEM((wv, tq), F32)] + comm_scratch,
        ),
        out_shape=[sds((t, 512), F32), sds((t, 512), MXU_DTYPE), sds((512, t), MXU_DTYPE), sds((ngrp, 8, t), F32)]
        + _gather_shapes(ws),
        compiler_params=_cparams(("arbitrary", "arbitrary") if ng else ("parallel", "arbitrary")),
    )(it, jt, q, k, vt, z, *ws)


def mla_attn_bwd(q, k, v, do, lse, dd, *, tq=256, chips=None):
    t = q.shape[0]
    nq = t // tq
    it, jt = _causal_tiles(nq, True)
    hps, wq, wv = MLA_HPS, LANES * MLA_HPS, 64 * MLA_HPS
    pcs, layer = chips if chips is not None else ([], 0)
    nc = len(pcs)
    ngrp, nsteps = MLA_HEADS // hps, int(it.shape[0])

    def body(it_ref, jt_ref, q_ref, k_ref, v_ref, do_ref, lse_ref, d_ref, *rest):
        pc_refs, rest = rest[:nc], rest[nc:]
        dq_ref, dk_ref, dv_ref = rest[:3]
        lb_refs, rest = rest[3:3 + nc], rest[3 + nc:]
        dk_scr, dv_scr = rest[:2]
        step = pl.program_id(1)
        i, j = it_ref[step], jt_ref[step]
        masks = _head_masks()
        if nc:
            chips_start, chips_finish = _chips_exchange(pc_refs, lb_refs, rest[2], rest[3], layer)

            @pl.when((pl.program_id(0) == 0) & (step == 0))
            def _():
                chips_start()

        @pl.when(step == 0)
        def _():
            dq_ref[...] = jnp.zeros(dq_ref.shape, F32)

        @pl.when(i == j)
        def _():
            dk_scr[...] = jnp.zeros(dk_scr.shape, F32)
            dv_scr[...] = jnp.zeros(dv_scr.shape, F32)

        def update(diagonal):
            qrows = pl.ds(pl.multiple_of(i * tq, tq), tq)
            heads = range(hps)
            lanes = [slice(LANES * hh, LANES * (hh + 1)) for hh in heads]
            pair = [slice(LANES * (hh // 2), LANES * (hh // 2 + 1)) for hh in heads]
            qh = [q_ref[:, lanes[hh]] for hh in heads]
            kh = [k_ref[:, lanes[hh]] for hh in heads]
            doh = []
            for hh in heads:
                dov = do_ref[:, pair[hh]]
                doh.append(jnp.where(masks[hh % 2], dov, jnp.zeros_like(dov)))
            sts = [_dot(kh[hh], qh[hh], NT) for hh in heads]
            dps = [_dot(v_ref[:, pair[hh]], doh[hh], NT) for hh in heads]
            lse = [lse_ref[hh:hh + 1, :] for hh in heads]
            ddv = [d_ref[hh:hh + 1, :] for hh in heads]
            dk_new = [dk_scr[:, lanes[hh]] for hh in heads]
            dq_new = [dq_ref[qrows, lanes[hh]] for hh in heads]
            dv_new = [dv_scr[:, pair[2 * pp]] for pp in range(hps // 2)]
            for hh in heads:
                st = sts[hh] - lse[hh]
                if diagonal:
                    key = lax.broadcasted_iota(jnp.int32, (tq, tq), 0)
                    qry = lax.broadcasted_iota(jnp.int32, (tq, tq), 1)
                    st = jnp.where(key <= qry, st, NEG)
                pt = jnp.exp(st)
                dst = pt * (dps[hh] - ddv[hh])
                dv_new[hh // 2] = dv_new[hh // 2] + _dot(pt, doh[hh], NN)
                dk_new[hh] = dk_new[hh] + _dot(dst, qh[hh], NN)
                dq_new[hh] = dq_new[hh] + _dot(dst, kh[hh], TN)
            for hh in heads:
                dk_scr[:, lanes[hh]] = dk_new[hh]
                dq_ref[qrows, lanes[hh]] = dq_new[hh]
            for pp in range(hps // 2):
                dv_scr[:, pair[2 * pp]] = dv_new[pp]

        @pl.when(j < i)
        def _():
            update(False)

        @pl.when(j == i)
        def _():
            update(True)

        @pl.when(i == nq - 1)
        def _():
            dk_ref[...] = dk_scr[...]
            dv_ref[...] = dv_scr[...]

        if nc:
            @pl.when((pl.program_id(0) == ngrp - 1) & (step == nsteps - 1))
            def _():
                chips_finish()

    qo = lambda w: pl.BlockSpec((tq, w), lambda p, s, it_, jt_: (it_[s], p))
    kv = lambda w: pl.BlockSpec((tq, w), lambda p, s, it_, jt_: (jt_[s], p))
    stat = pl.BlockSpec((None, 8, tq), lambda p, s, it_, jt_: (p, 0, it_[s]))
    sds = jax.ShapeDtypeStruct
    comm_scratch = [pltpu.SemaphoreType.DMA((3 * nc,)), pltpu.SemaphoreType.DMA((3 * nc,))] if nc else []
    return pl.pallas_call(
        body,
        name="mla_attn_bwd_chips" if nc else "mla_attn_bwd",
        grid_spec=pltpu.PrefetchScalarGridSpec(
            num_scalar_prefetch=2,
            grid=(ngrp, nsteps),
            in_specs=[qo(wq), kv(wq), kv(wv), qo(wv), stat, stat] + [HBM_SPEC] * nc,
            out_specs=[pl.BlockSpec((t, wq), lambda p, s, it_, jt_: (0, p)), kv(wq), kv(wv)] + [HBM_SPEC] * nc,
            scratch_shapes=[pltpu.VMEM((tq, wq), F32), pltpu.VMEM((tq, wv), F32)] + comm_scratch,
        ),
        out_shape=[sds((t, 1024), F32), sds((t, 1024), F32), sds((t, 512), F32)] + _chips_shapes(pcs),
        compiler_params=_cparams(("arbitrary", "arbitrary") if nc else ("parallel", "arbitrary")),
    )(it, jt, q, k, v, do, lse, dd, *pcs)


def mla_gate_bwd(dy, o, z, *, tm=512):
    t = dy.shape[0]
    wv = 64 * MLA_HPS

    def body(dy_ref, o_ref, g_ref, do_ref, dzg_ref, dd_ref):
        dyv, ov = dy_ref[...], o_ref[...]
        silu, dsilu = _silu_and_grad(g_ref[...])
        do = dyv * silu
        do_ref[...] = do.astype(MXU_DTYPE)
        dzg_ref[...] = (dyv * ov * dsilu).astype(MXU_DTYPE)
        prod = do * ov
        row = lax.broadcasted_iota(jnp.int32, (8, wv), 0)
        lane = lax.broadcasted_iota(jnp.int32, (8, wv), 1)
        pick = ((lane >= row * DIL_HD) & (lane < (row + 1) * DIL_HD)).astype(BF16)
        hi = prod.astype(BF16)
        r1 = prod - hi.astype(F32)
        mid = r1.astype(BF16)
        lo = (r1 - mid.astype(F32)).astype(BF16)
        dot = lambda u: lax.dot_general(pick, u, NT, preferred_element_type=F32)
        dd_ref[...] = dot(hi) + dot(mid) + dot(lo)

    blk = lambda off=0: pl.BlockSpec((tm, wv), lambda p, i: (i, p + off))
    sds = jax.ShapeDtypeStruct
    return pl.pallas_call(
        body,
        name="mla_gate_bwd",
        grid=(MLA_HEADS // MLA_HPS, t // tm),
        in_specs=[blk(), blk(), blk(512 // wv)],
        out_specs=[blk(), blk(), pl.BlockSpec((None, 8, tm), lambda p, i: (p, 0, i))],
        out_shape=[sds((t, 512), MXU_DTYPE), sds((t, 512), MXU_DTYPE), sds((MLA_HEADS // MLA_HPS, 8, t), F32)],
        compiler_params=_cparams(("parallel", "parallel")),
    )(dy, o, z)


SPAN = 2048
DIL_SCALE = DIL_HD ** -0.5


def _rm_src(u, window, dil):
    nn, r = divmod(u, dil)
    return pl.ds(nn * window + r, DIL_NK, stride=dil) if dil > 1 else pl.ds(u * DIL_NK, DIL_NK)


def _rm_dst(u):
    return pl.ds(u * DIL_NK, DIL_NK)


def _dil_prep_tile(ops, x, g2, c, s1, s2, scale):
    ms = ops.seg_sum(x * x) * (1.0 / DIL_HD)
    return rope(ops, x * lax.rsqrt(ms + EPS) * g2, c, s1, s2, 32) * scale


def _span_blk(base):
    return pl.BlockSpec((SPAN, LANES), lambda s, p: (s, base + p))


def _const_blk(shape):
    return pl.BlockSpec(shape, lambda s, p: (0,) * len(shape))


_DIL_TABLE_SPECS = [pl.BlockSpec((SPAN, LANES), (lambda s, p, blk=blk: (s, blk))) for blk in (0, 1, 2)]


def dil_tables(tabs, gi):
    window, dil = DIL_GROUPS[gi]
    t = tabs.shape[0]

    def body(c_ref, s1_ref, s2_ref, o_ref):
        for u in range(SPAN // DIL_NK):
            src, dst = _rm_src(u, window, dil), _rm_dst(u)
            for i, ref in enumerate((c_ref, s1_ref, s2_ref)):
                o_ref[dst, LANES * i:LANES * (i + 1)] = ref[src, :]

    return pl.pallas_call(
        body,
        name=f"dil_tables_{dil}",
        grid=(t // SPAN,),
        in_specs=[pl.BlockSpec((SPAN, LANES), (lambda s, blk=blk: (s, blk))) for blk in (3, 4, 5)],
        out_specs=pl.BlockSpec((SPAN, 3 * LANES), lambda s: (s, 0)),
        out_shape=jax.ShapeDtypeStruct((t, 3 * LANES), F32),
        compiler_params=_cparams(("parallel",)),
    )(tabs, tabs, tabs)


def dil_prep_fwd(z, tabs, gq2, gk2, gi):
    window, dil = DIL_GROUPS[gi]
    t = z.shape[0]

    def body(q_ref, k_ref, v_ref, c_ref, s1_ref, s2_ref, gq_ref, gk_ref, qo_ref, ko_ref, vo_ref):
        for u in range(SPAN // DIL_NK):
            src, dst = _rm_src(u, window, dil), _rm_dst(u)
            c, s1, s2 = c_ref[dst, :], s1_ref[dst, :], s2_ref[dst, :]
            qo_ref[dst, :] = _dil_prep_tile(PLAIN, q_ref[src, :], gq_ref[...], c, s1, s2, DIL_SCALE).astype(MXU_DTYPE)
            ko_ref[dst, :] = _dil_prep_tile(PLAIN, k_ref[src, :], gk_ref[...], c, s1, s2, 1.0).astype(MXU_DTYPE)
            vo_ref[dst, :] = v_ref[src, :].astype(MXU_DTYPE)

    return pl.pallas_call(
        body,
        name=f"dil_prep_fwd_{dil}",
        grid=(t // SPAN, 4),
        in_specs=[_span_blk(4 * gi), _span_blk(12 + 4 * gi), _span_blk(24 + 4 * gi)] + _DIL_TABLE_SPECS
        + [_const_blk((1, LANES)), _const_blk((1, LANES))],
        out_specs=[_span_blk(0)] * 3,
        out_shape=[jax.ShapeDtypeStruct((t, 512), MXU_DTYPE)] * 3,
        compiler_params=_cparams(("parallel", "parallel")),
    )(z, z, z, tabs, tabs, tabs, gq2, gk2)


def dil_prep_bwd(z, tabs, gq2, gk2, dq, dk, dv, gi):
    window, dil = DIL_GROUPS[gi]
    t = z.shape[0]

    def body(q_ref, k_ref, c_ref, s1_ref, s2_ref, gq_ref, gk_ref, dq_ref, dk_ref, dv_ref, dzq_ref, dzk_ref, dzv_ref, dgq_ref,
             dgk_ref, sq, sk, sv):
        @pl.when((pl.program_id(0) == 0) & (pl.program_id(1) == 0))
        def _():
            dgq_ref[...] = jnp.zeros((1, LANES), F32)
            dgk_ref[...] = jnp.zeros((1, LANES), F32)

        dgs = [jnp.zeros((1, LANES), F32), jnp.zeros((1, LANES), F32)]
        for u in range(SPAN // DIL_NK):
            src, dst = _rm_src(u, window, dil), _rm_dst(u)
            c, s1, s2 = c_ref[dst, :], s1_ref[dst, :], s2_ref[dst, :]
            for idx, (x_ref, g_ref, ct_ref, scr, scale) in enumerate(((q_ref, gq_ref, dq_ref, sq, DIL_SCALE),
                                                                      (k_ref, gk_ref, dk_ref, sk, 1.0))):
                _, vjp = jax.vjp(lambda xv, gv, sc=scale: _dil_prep_tile(DIFF, xv, gv, c, s1, s2, sc), x_ref[src, :], g_ref[...])
                dx, dg = vjp(ct_ref[dst, :])
                scr[src, :] = dx
                dgs[idx] = dgs[idx] + dg
            sv[src, :] = dv_ref[dst, :]
        dgq_ref[...] += dgs[0] + pltpu.roll(dgs[0], DIL_HD, 1)
        dgk_ref[...] += dgs[1] + pltpu.roll(dgs[1], DIL_HD, 1)
        for c0 in range(0, SPAN, 256):
            rows = slice(c0, c0 + 256)
            dzq_ref[rows, :] = sq[rows, :].astype(MXU_DTYPE)
            dzk_ref[rows, :] = sk[rows, :].astype(MXU_DTYPE)
            dzv_ref[rows, :] = sv[rows, :].astype(MXU_DTYPE)

    sds = jax.ShapeDtypeStruct
    return pl.pallas_call(
        body,
        name=f"dil_prep_bwd_{dil}",
        grid=(t // SPAN, 4),
        in_specs=[_span_blk(4 * gi), _span_blk(12 + 4 * gi)] + _DIL_TABLE_SPECS
        + [_const_blk((1, LANES)), _const_blk((1, LANES)), _span_blk(0), _span_blk(0), _span_blk(0)],
        out_specs=[_span_blk(0)] * 3 + [_const_blk((1, LANES))] * 2,
        out_shape=[sds((t, 512), MXU_DTYPE)] * 3 + [sds((1, LANES), F32)] * 2,
        scratch_shapes=[pltpu.VMEM((SPAN, LANES), F32)] * 3,
        compiler_params=_cparams(("arbitrary", "arbitrary")),
    )(z, z, tabs, tabs, tabs, gq2, gk2, dq, dk, dv)


def _band_masks():
    qi = lax.broadcasted_iota(jnp.int32, (DIL_NK, DIL_NK), 0)
    ki = lax.broadcasted_iota(jnp.int32, (DIL_NK, DIL_NK), 1)
    return (ki >= qi), (ki <= qi)


def _pair_heads():
    return [(pair, hh) for pair in range(4) for hh in range(2)]


def _pair_lanes(pair):
    return slice(LANES * pair, LANES * (pair + 1))


def _zero_other_head(mask, x):
    return jnp.where(mask, x, jnp.zeros_like(x))


def dil_attn_fwd(name, q, k, v, dil):
    t = q.shape[0]

    def body(q_ref, kp_ref, kc_ref, vp_ref, vc_ref, o_ref, lse_ref):
        b = pl.program_id(0)
        mprev, mcur = _band_masks()
        mprev = mprev & (b >= dil)
        hm = _head_masks()
        heads = _pair_heads()
        qh = [_zero_other_head(hm[hh], q_ref[:, _pair_lanes(pair)]) for pair, hh in heads]
        sps = [_dot(qh[i], kp_ref[:, _pair_lanes(pair)], NT) for i, (pair, _) in enumerate(heads)]
        scs = [_dot(qh[i], kc_ref[:, _pair_lanes(pair)], NT) for i, (pair, _) in enumerate(heads)]
        o = [jnp.zeros((DIL_NK, LANES), F32) for _ in range(4)]
        lse = [jnp.zeros((DIL_NK, LANES), F32) for _ in range(4)]
        for i, (pair, hh) in enumerate(heads):
            sp, sc = jnp.where(mprev, sps[i], NEG), jnp.where(mcur, scs[i], NEG)
            m = jnp.maximum(jnp.max(sp, axis=1, keepdims=True), jnp.max(sc, axis=1, keepdims=True))
            ep, ec = jnp.exp(sp - m), jnp.exp(sc - m)
            den = jnp.sum(ep, axis=1, keepdims=True) + jnp.sum(ec, axis=1, keepdims=True)
            lanes = _pair_lanes(pair)
            oh = _dot(ep, _zero_other_head(hm[hh], vp_ref[:, lanes]), NN) + _dot(ec, _zero_other_head(hm[hh], vc_ref[:, lanes]), NN)
            o[pair] = o[pair] + oh / den
            lse[pair] = jnp.where(hm[hh], m + jnp.log(den), lse[pair])
        for pair in range(4):
            o_ref[:, _pair_lanes(pair)] = o[pair]
            lse_ref[:, _pair_lanes(pair)] = lse[pair]

    cur = pl.BlockSpec((DIL_NK, 512), lambda b: (b, 0))
    prev = pl.BlockSpec((DIL_NK, 512), lambda b: (jnp.maximum(b - dil, 0), 0))
    sds = jax.ShapeDtypeStruct
    return pl.pallas_call(
        body,
        name=name,
        grid=(t // DIL_NK,),
        in_specs=[cur, prev, cur, prev, cur],
        out_specs=[cur, cur],
        out_shape=[sds((t, 512), F32), sds((t, 512), F32)],
        compiler_params=_cparams(("parallel",)),
    )(q, k, k, v, v)


def dil_attn_bwd(name, q, k, v, do, lse, dd, dil):
    t = q.shape[0]
    nblk = t // DIL_NK

    def body(q_ref, do_ref, l_ref, d_ref, kp_ref, kc_ref, vp_ref, vc_ref, dq_ref, dk_ref, dv_ref):
        b = pl.program_id(0)

        @pl.when(b == 0)
        def _():
            dk_ref[...] = jnp.zeros(dk_ref.shape, F32)
            dv_ref[...] = jnp.zeros(dv_ref.shape, F32)

        mprev, mcur = _band_masks()
        mprev = mprev & (b >= dil)
        rows_c = pl.ds(pl.multiple_of(b * DIL_NK, DIL_NK), DIL_NK)
        rows_p = pl.ds(pl.multiple_of(jnp.maximum(b - dil, 0) * DIL_NK, DIL_NK), DIL_NK)
        hm = _head_masks()
        heads = _pair_heads()
        ln_ = [_pair_lanes(pair) for pair, _ in heads]
        qh = [_zero_other_head(hm[hh], q_ref[:, ln_[i]]) for i, (_, hh) in enumerate(heads)]
        doh = [_zero_other_head(hm[hh], do_ref[:, ln_[i]]) for i, (_, hh) in enumerate(heads)]
        idx = range(len(heads))
        s_p = [_dot(qh[i], kp_ref[:, ln_[i]], NT) for i in idx]
        s_c = [_dot(qh[i], kc_ref[:, ln_[i]], NT) for i in idx]
        dp_p = [_dot(doh[i], vp_ref[:, ln_[i]], NT) for i in idx]
        dp_c = [_dot(doh[i], vc_ref[:, ln_[i]], NT) for i in idx]
        zero = lambda: [jnp.zeros((DIL_NK, LANES), F32) for _ in range(4)]
        dq, dk_p, dk_c, dv_p, dv_c = zero(), zero(), zero(), zero(), zero()
        for i, (pair, hh) in enumerate(heads):
            lse_h, d_h = _row_scalar(l_ref[:, ln_[i]], hm[hh]), _row_scalar(d_ref[:, ln_[i]], hm[hh])
            pp = jnp.exp(jnp.where(mprev, s_p[i] - lse_h, NEG))
            pc = jnp.exp(jnp.where(mcur, s_c[i] - lse_h, NEG))
            dsp, dsc = pp * (dp_p[i] - d_h), pc * (dp_c[i] - d_h)
            dq[pair] = (dq[pair] + _dot(dsp, _zero_other_head(hm[hh], kp_ref[:, ln_[i]]), NN)
                        + _dot(dsc, _zero_other_head(hm[hh], kc_ref[:, ln_[i]]), NN))
            dv_p[pair] = dv_p[pair] + _dot(pp, doh[i], TN)
            dv_c[pair] = dv_c[pair] + _dot(pc, doh[i], TN)
            dk_p[pair] = dk_p[pair] + _dot(dsp, qh[i], TN)
            dk_c[pair] = dk_c[pair] + _dot(dsc, qh[i], TN)
        for pair in range(4):
            lanes = _pair_lanes(pair)
            dq_ref[:, lanes] = dq[pair]
            dk_ref[rows_p, lanes] += dk_p[pair]
            dv_ref[rows_p, lanes] += dv_p[pair]
            dk_ref[rows_c, lanes] += dk_c[pair]
            dv_ref[rows_c, lanes] += dv_c[pair]

    cur = pl.BlockSpec((DIL_NK, 512), lambda b: (b, 0))
    prev = pl.BlockSpec((DIL_NK, 512), lambda b: (jnp.maximum(b - dil, 0), 0))
    whole = pl.BlockSpec((t, 512), lambda b: (0, 0))
    sds = jax.ShapeDtypeStruct
    return pl.pallas_call(
        body,
        name=name,
        grid=(nblk,),
        in_specs=[cur, cur, cur, cur, prev, cur, prev, cur],
        out_specs=[cur, whole, whole],
        out_shape=[sds((t, 512), F32)] * 3,
        compiler_params=_cparams(("arbitrary",)),
    )(q, do, lse, dd, k, k, v, v)


def dil_combine(os_, lses, z):
    t = z.shape[0]

    def body(o1_ref, l1_ref, o2_ref, l2_ref, o3_ref, l3_ref, g_ref, y_ref, o_ref, lse_ref, yt_ref, so2, sl2, so3, sl3):
        for (window, dil), o_in, l_in, so, sl in ((DIL_GROUPS[1], o2_ref, l2_ref, so2, sl2), (DIL_GROUPS[2], o3_ref, l3_ref, so3, sl3)):
            for u in range(SPAN // DIL_NK):
                src, dst = _rm_src(u, window, dil), _rm_dst(u)
                so[src, :] = o_in[dst, :]
                sl[src, :] = l_in[dst, :]
        for c0 in range(0, SPAN, 256):
            rows = slice(c0, c0 + 256)
            la, lb, lc = l1_ref[rows, :], sl2[rows, :], sl3[rows, :]
            mx = jnp.maximum(jnp.maximum(la, lb), lc)
            wa, wb, wc = jnp.exp(la - mx), jnp.exp(lb - mx), jnp.exp(lc - mx)
            tot = wa + wb + wc
            o = (wa * o1_ref[rows, :] + wb * so2[rows, :] + wc * so3[rows, :]) / tot
            y = o * _silu_and_grad(g_ref[rows, :])[0]
            y_ref[rows, :] = y.astype(MXU_DTYPE)
            o_ref[rows, :] = o
            lse_ref[rows, :] = mx + jnp.log(tot)
            yt_ref[:, rows] = y.T.astype(MXU_DTYPE)

    sds = jax.ShapeDtypeStruct
    return pl.pallas_call(
        body,
        name="dil_combine",
        grid=(t // SPAN, 4),
        in_specs=[_span_blk(0)] * 6 + [_span_blk(36)],
        out_specs=[_span_blk(0)] * 3 + [pl.BlockSpec((LANES, SPAN), lambda s, p: (p, s))],
        out_shape=[sds((t, 512), MXU_DTYPE), sds((t, 512), F32), sds((t, 512), F32), sds((512, t), MXU_DTYPE)],
        scratch_shapes=[pltpu.VMEM((SPAN, LANES), F32)] * 4,
        compiler_params=_cparams(("parallel", "parallel")),
    )(os_[0], lses[0], os_[1], lses[1], os_[2], lses[2], z)


def dil_gate_bwd(dy, o, z, lse):
    t = dy.shape[0]

    def body(dy_ref, o_ref, g_ref, lse_ref, dzg_ref, do1_ref, dd1_ref, do2_ref, dd2_ref, l2_ref, do3_ref, dd3_ref, l3_ref, do_scr):
        for c0 in range(0, SPAN, 256):
            rows = slice(c0, c0 + 256)
            dyv, ov = dy_ref[rows, :], o_ref[rows, :]
            silu, dsilu = _silu_and_grad(g_ref[rows, :])
            do = dyv * silu
            do_scr[rows, :] = do
            do1_ref[rows, :] = do.astype(MXU_DTYPE)
            dd1_ref[rows, :] = _seg_sum_impl(do * ov)
            dzg_ref[rows, :] = (dyv * ov * dsilu).astype(MXU_DTYPE)
        for (window, dil), do_o, dd_o, l_o in ((DIL_GROUPS[1], do2_ref, dd2_ref, l2_ref), (DIL_GROUPS[2], do3_ref, dd3_ref, l3_ref)):
            for u in range(SPAN // DIL_NK):
                src, dst = _rm_src(u, window, dil), _rm_dst(u)
                do_o[dst, :] = do_scr[src, :].astype(MXU_DTYPE)
                dd_o[dst, :] = dd1_ref[src, :]
                l_o[dst, :] = lse_ref[src, :]

    sds = jax.ShapeDtypeStruct
    f32, mxu = sds((t, 512), F32), sds((t, 512), MXU_DTYPE)
    return pl.pallas_call(
        body,
        name="dil_gate_bwd",
        grid=(t // SPAN, 4),
        in_specs=[_span_blk(0), _span_blk(0), _span_blk(36), _span_blk(0)],
        out_specs=[_span_blk(0)] * 9,
        out_shape=[mxu, mxu, f32, mxu, f32, f32, mxu, f32, f32],
        scratch_shapes=[pltpu.VMEM((SPAN, LANES), F32)],
        compiler_params=_cparams(("parallel", "parallel")),
    )(dy, o, z, lse)


def _merge_tile(p0, p1, p2, z0, z1, z2, b0, b1, b2):
    return _sigmoid(z0 + b0) * p0 + _sigmoid(z1 + b1) * p1 + _sigmoid(z2 + b2) * p2


def _merge_ins(ps, z, b):
    w = 256
    return ([_row(p, w, 0, True) for p in ps] + [_row(z, w, 4 * i, True) for i in range(3)]
            + [_const(b, w, 4 * i, True) for i in range(3)])


def merge_fwd(ps, z, b):
    def fn(ins, outs, _):
        merged = _merge_tile(*[r[...] for r in ins])
        outs[0][...] = merged.astype(MXU_DTYPE)
        outs[1][...] = merged.T.astype(MXU_DTYPE)

    return rowwise("merge_fwd", fn, z.shape[0], tm=1024, ncol=4, ins=_merge_ins(ps, z, b), outs=[(1024, 256, MXU_DTYPE, True)],
                   touts=[(1024, 256, MXU_DTYPE, True)])


def merge_bwd(dm, ps, z, b):
    def fn(ins, outs, accs):
        _, vjp = jax.vjp(_merge_tile, *[r[...] for r in ins[:9]])
        grads = vjp(ins[9][...])
        for i in range(3):
            outs[i][...] = grads[i].astype(MXU_DTYPE)
            outs[3 + i][...] = grads[3 + i].astype(MXU_DTYPE)
            accs[i][...] += grads[6 + i]

    return rowwise("merge_bwd", fn, z.shape[0], tm=1024, ncol=4, ins=_merge_ins(ps, z, b) + [_row(dm, 256, 0, True)],
                   outs=[(1024, 256, MXU_DTYPE, True)] * 6, accs=[(1, 1024, 256, True)] * 3)


EW_BLOCK_BYTES = 2**21


def _tile2d(r, c):
    if r * c * 4 <= EW_BLOCK_BYTES:
        return r, c
    for tr in (512, 256, 128, 64, 32):
        if r % tr == 0 and tr * c * 4 <= EW_BLOCK_BYTES:
            return tr, c
    for tc in (1024, 512, 256, 128):
        if c % tc == 0 and r * tc * 4 <= EW_BLOCK_BYTES:
            return r, tc
    raise ValueError((r, c))


def adamw(name, w, g, m, v):
    shape = w.shape
    c = shape[-1]
    r = w.size // c
    tr, tc = _tile2d(r, c)
    assert tc == c
    c1, c2 = 1.0 - ADAM_B1 ** ADAM_STEP, 1.0 - ADAM_B2 ** ADAM_STEP

    def body(w_ref, g_ref, m_ref, v_ref, d_ref, mo_ref, vo_ref):
        gv = g_ref[...]
        mn = ADAM_B1 * m_ref[...] + (1.0 - ADAM_B1) * gv
        vn = ADAM_B2 * v_ref[...] + (1.0 - ADAM_B2) * (gv * gv)
        d_ref[...] = -ADAM_LR * ((mn / c1) / (jnp.sqrt(vn / c2) + ADAM_EPS) + ADAM_WD * w_ref[...])
        mo_ref[...] = mn
        vo_ref[...] = vn

    spec = pl.BlockSpec((tr, c), lambda i: (i, 0))
    outs = pl.pallas_call(
        body,
        name=name,
        grid=(r // tr,),
        in_specs=[spec] * 4,
        out_specs=[spec] * 3,
        out_shape=[jax.ShapeDtypeStruct((r, c), F32)] * 3,
        compiler_params=_cparams(("parallel",)),
    )(*[a.reshape(r, c) for a in (w, g, m, v)])
    return [o.reshape(shape) for o in outs]


def adamw_pair(name, w, reduced, received, m, v, c_idx):
    _, r, c = w.shape
    tr, tc = _tile2d(r, c)
    c1, c2 = 1.0 - ADAM_B1 ** ADAM_STEP, 1.0 - ADAM_B2 ** ADAM_STEP

    def body(c_ref, w_ref, r0_ref, x0_ref, r1_ref, x1_ref, m_ref, v_ref, g_ref, d_ref, mo_ref, vo_ref):
        mine = c_ref[0]
        gv = jnp.where(pl.program_id(0) == 0, jnp.where(mine == 0, r0_ref[...], x0_ref[...]),
                       jnp.where(mine == 1, r1_ref[...], x1_ref[...]))
        mn = ADAM_B1 * m_ref[...] + (1.0 - ADAM_B1) * gv
        vn = ADAM_B2 * v_ref[...] + (1.0 - ADAM_B2) * (gv * gv)
        g_ref[...] = gv
        d_ref[...] = -ADAM_LR * ((mn / c1) / (jnp.sqrt(vn / c2) + ADAM_EPS) + ADAM_WD * w_ref[...])
        mo_ref[...] = mn
        vo_ref[...] = vn

    full = pl.BlockSpec((None, tr, tc), lambda h, i, j, cr: (h, i, j))
    half = pl.BlockSpec((tr, tc), lambda h, i, j, cr: (i, j))
    return pl.pallas_call(
        body,
        name=name,
        grid_spec=pltpu.PrefetchScalarGridSpec(
            num_scalar_prefetch=1,
            grid=(2, r // tr, c // tc),
            in_specs=[full, half, half, half, half, full, full],
            out_specs=[full] * 4,
        ),
        out_shape=[jax.ShapeDtypeStruct((2, r, c), F32)] * 4,
        compiler_params=_cparams(("parallel", "parallel", "parallel")),
    )(c_idx, w, reduced[0], received[0], reduced[1], received[1], m, v)


def sum_pair(name, g, la, out_dtype):
    _, r, c = g.shape
    tr, tc = _tile2d(r, c)

    def body(g_ref, la_ref, o_ref):
        o_ref[...] = (g_ref[...] + la_ref[...]).astype(o_ref.dtype)

    spec = pl.BlockSpec((None, tr, tc), lambda s, i, j: (s, i, j))
    return pl.pallas_call(
        body,
        name=name,
        grid=(4, r // tr, c // tc),
        in_specs=[spec, spec],
        out_specs=spec,
        out_shape=jax.ShapeDtypeStruct((4, r, c), out_dtype),
        compiler_params=_cparams(("parallel", "parallel", "parallel")),
    )(g, la)


def sum_chips(name, g, la, lb, s_idx):
    _, r, c = g.shape
    tr, tc = _tile2d(r, c)

    def body(s_ref, g_ref, la_ref, l0_ref, l1_ref, l2_ref, o_ref):
        own = g_ref[...] + la_ref[...]
        o_ref[...] = ((own + l0_ref[...].astype(F32)) + l1_ref[...].astype(F32)) + l2_ref[...].astype(F32)

    own_spec = pl.BlockSpec((None, tr, tc), lambda i, j, sr: (sr[0], i, j))
    lspec = lambda k: pl.BlockSpec((None, tr, tc), lambda i, j, sr: (k, i, j))
    return pl.pallas_call(
        body,
        name=name,
        grid_spec=pltpu.PrefetchScalarGridSpec(
            num_scalar_prefetch=1,
            grid=(r // tr, c // tc),
            in_specs=[own_spec, own_spec, lspec(0), lspec(1), lspec(2)],
            out_specs=pl.BlockSpec((tr, tc), lambda i, j, sr: (i, j)),
        ),
        out_shape=jax.ShapeDtypeStruct((r, c), F32),
        compiler_params=_cparams(("parallel", "parallel")),
    )(s_idx, g, la, lb, lb, lb)


def _place():
    x, y, c = lax.axis_index("x"), lax.axis_index("y"), lax.axis_index("c")
    chips = [(1 - x, y), (x, 1 - y), (1 - x, 1 - y)]
    return x, y, c, chips


HBM_SPEC = pl.BlockSpec(memory_space=pltpu.HBM)


def _comm_call(name, body, ins, out_shapes, n_sem, n_local):
    return pl.pallas_call(
        body,
        name=name,
        in_specs=[HBM_SPEC] * len(ins),
        out_specs=[HBM_SPEC] * len(out_shapes),
        out_shape=out_shapes,
        scratch_shapes=[pltpu.SemaphoreType.DMA((n_sem,)), pltpu.SemaphoreType.DMA((n_sem,)),
                        pltpu.SemaphoreType.DMA((max(n_local, 1),))],
    )(*ins)


def _layer_gather(ins, outs, send, recv, layer):
    n = len(ins)
    x, y, c, chips = _place()
    s = 2 * x + y
    sib = (x, y, 1 - c)
    active = c == layer

    def rc(a, k, src, dst, dev):
        return pltpu.make_async_remote_copy(src_ref=src, dst_ref=dst, send_sem=send.at[6 * a + k], recv_sem=recv.at[6 * a + k],
                                            device_id=dev, device_id_type=MESH)

    def first_hop():
        return [rc(a, j, ins[a], outs[a].at[s], (*chip, c)) for j, chip in enumerate(chips) for a in range(n)]

    def start():
        @pl.when(active)
        def _():
            for cp in first_hop():
                cp.start()

    def finish():
        @pl.when(active)
        def _():
            forwards = []
            for j, (cx, cy) in enumerate(chips):
                for a in range(n):
                    landed = outs[a].at[2 * cx + cy]
                    rc(a, j, landed, landed, sib).wait_recv()
                    forwards.append(rc(a, 3 + j, landed, landed, sib))
                    forwards[-1].start()
            for cp in first_hop() + forwards:
                cp.wait_send()

        @pl.when(jnp.logical_not(active))
        def _():
            for j, (cx, cy) in enumerate(chips):
                for a in range(n):
                    other = outs[a].at[2 * cx + cy]
                    rc(a, 3 + j, other, other, sib).wait_recv()

    return start, finish


def _gather_shapes(ws):
    return [jax.ShapeDtypeStruct((4,) + w.shape, w.dtype) for w in ws]


def gather_hook(ws, layer):
    return (ws, _gather_shapes(ws), 6 * len(ws), lambda i, o, s, r: _layer_gather(i, o, s, r, layer))


def allgather_layer(ws, layer):
    n = len(ws)

    def body(*refs):
        send, recv, _ = refs[2 * n:]
        start, finish = _layer_gather(refs[:n], refs[n:2 * n], send, recv, layer)
        start()
        finish()

    return _comm_call(f"allgather_layer{layer}", body, ws, _gather_shapes(ws), 6 * n, 0)


def exchange_sibling(gs, layer, tag=""):
    n = len(gs)

    def body(*refs):
        send, recv, _ = refs[2 * n:]
        start, finish = _sibling_exchange(refs[:n], refs[n:2 * n], send, recv, layer)
        start()
        finish()

    return _comm_call(f"exchange_sibling{layer}{tag}", body, gs, [jax.ShapeDtypeStruct(g.shape, g.dtype) for g in gs], n, 0)


def _sibling_exchange(ins, outs, send, recv, layer):
    x, y, c, _ = _place()

    def copies():
        return [pltpu.make_async_remote_copy(src_ref=ins[a], dst_ref=outs[a], send_sem=send.at[a], recv_sem=recv.at[a],
                                             device_id=(x, y, 1 - c), device_id_type=MESH) for a in range(len(ins))]

    def start():
        @pl.when(c != layer)
        def _():
            for cp in copies():
                cp.start()

    def finish():
        @pl.when(c != layer)
        def _():
            for cp in copies():
                cp.wait_send()

        @pl.when(c == layer)
        def _():
            for cp in copies():
                cp.wait_recv()

    return start, finish


def sibling_hook(gs, layer):
    return (gs, [jax.ShapeDtypeStruct(g.shape, g.dtype) for g in gs], len(gs),
            lambda i, o, s, r: _sibling_exchange(i, o, s, r, layer))


def chips_hook(ps, layer):
    return (ps, _chips_shapes(ps), 3 * len(ps), lambda i, o, s, r: _chips_exchange(i, o, s, r, layer))


def _chips_exchange(ins, outs, send, recv, layer):
    n = len(ins)
    _, _, c, chips = _place()

    def copies():
        return [pltpu.make_async_remote_copy(src_ref=ins[a].at[2 * cx + cy], dst_ref=outs[a].at[j], send_sem=send.at[3 * a + j],
                                             recv_sem=recv.at[3 * a + j], device_id=(cx, cy, c), device_id_type=MESH)
                for j, (cx, cy) in enumerate(chips) for a in range(n)]

    def start():
        @pl.when(c == layer)
        def _():
            for cp in copies():
                cp.start()

    def finish():
        @pl.when(c == layer)
        def _():
            for cp in copies():
                cp.wait()

    return start, finish


def _chips_shapes(ps):
    return [jax.ShapeDtypeStruct((3,) + p.shape[1:], p.dtype) for p in ps]


def exchange_chips(ps, layer, tag=""):
    n = len(ps)

    def body(*refs):
        send, recv, _ = refs[2 * n:]
        start, finish = _chips_exchange(refs[:n], refs[n:2 * n], send, recv, layer)
        start()
        finish()

    return _comm_call(f"exchange_chips{layer}{tag}", body, ps, _chips_shapes(ps), 3 * n, 0)


def exchange_small(small):
    def body(in_ref, out_ref, send, recv, _):
        x, y, c, _ = _place()
        rel = [(fx, fy) for fx in (0, 1) for fy in (0, 1)]

        def cp(fx, fy, sender_core):
            k = 2 * (2 * fx + fy) + sender_core
            return pltpu.make_async_remote_copy(src_ref=in_ref.at[2 * (x ^ fx) + (y ^ fy)], dst_ref=out_ref.at[k], send_sem=send.at[k],
                                                recv_sem=recv.at[k], device_id=(x ^ fx, y ^ fy, 0), device_id_type=MESH)

        @pl.when(c == 1)
        def _():
            sends = [cp(fx, fy, 1) for fx, fy in rel]
            for s_ in sends:
                s_.start()
            for s_ in sends:
                s_.wait_send()

        @pl.when(c == 0)
        def _():
            sends = [cp(fx, fy, 0) for fx, fy in rel[1:]]
            for s_ in sends:
                s_.start()
            for fx, fy in rel:
                for sc in (0, 1):
                    if (fx, fy, sc) != (0, 0, 0):
                        cp(fx, fy, sc).wait_recv()
            for s_ in sends:
                s_.wait_send()

    return _comm_call("exchange_small", body, [small], [jax.ShapeDtypeStruct((8,) + small.shape[1:], small.dtype)], 8, 0)[0]


def small_sum(small, landed, s_idx):
    _, r, c = small.shape

    def body(s_ref, own_ref, land_ref, o_ref):
        acc = own_ref[...]
        for k in range(1, 8):
            acc = acc + land_ref[k]
        o_ref[...] = acc

    return pl.pallas_call(
        body,
        name="small_sum",
        grid_spec=pltpu.PrefetchScalarGridSpec(
            num_scalar_prefetch=1,
            grid=(1,),
            in_specs=[pl.BlockSpec((None, r, c), lambda i, sr: (sr[0], 0, 0)), pl.BlockSpec((8, r, c), lambda i, sr: (0, 0, 0))],
            out_specs=pl.BlockSpec((r, c), lambda i, sr: (0, 0)),
        ),
        out_shape=jax.ShapeDtypeStruct((r, c), F32),
        compiler_params=_cparams(("arbitrary",)),
    )(s_idx, small, landed)


def exchange_final(rs, layer, small=None):
    n = len(rs)
    ins_all = list(rs) + ([small] if small is not None else [])

    def body(*refs):
        ni = len(ins_all)
        ins, outs = refs[:ni], refs[ni:2 * ni]
        send, recv, lsem = refs[2 * ni:]
        x, y, c, _ = _place()
        s = 2 * x + y
        to_sib = [pltpu.make_async_remote_copy(src_ref=ins[a], dst_ref=outs[a], send_sem=send.at[a], recv_sem=recv.at[a],
                                               device_id=(x, y, 1 - c), device_id_type=MESH) for a in range(n)]
        rel = [(fx, fy) for fx in (0, 1) for fy in (0, 1)]

        def piece(fx, fy, cc, dst_slot):
            k = n + 2 * (2 * fx + fy) + cc
            return pltpu.make_async_remote_copy(src_ref=ins[n], dst_ref=outs[n].at[dst_slot], send_sem=send.at[k], recv_sem=recv.at[k],
                                                device_id=(x ^ fx, y ^ fy, cc), device_id_type=MESH)

        @pl.when(c == layer)
        def _():
            for cp in to_sib:
                cp.start()
            if small is not None:
                own = pltpu.make_async_copy(ins[n], outs[n].at[s], lsem.at[0])
                own.start()
                sends = [piece(fx, fy, cc, s) for fx, fy in rel for cc in (0, 1) if (fx, fy) != (0, 0) or cc != layer]
                for cp in sends:
                    cp.start()
                for fx, fy in rel[1:]:
                    piece(fx, fy, layer, 2 * (x ^ fx) + (y ^ fy)).wait_recv()
                for cp in sends:
                    cp.wait_send()
                own.wait()
            for cp in to_sib:
                cp.wait_send()

        @pl.when(c != layer)
        def _():
            for cp in to_sib:
                cp.wait_recv()
            if small is not None:
                for fx, fy in rel:
                    piece(fx, fy, 1 - layer, 2 * (x ^ fx) + (y ^ fy)).wait_recv()

    out_shapes = [jax.ShapeDtypeStruct(r.shape, r.dtype) for r in rs]
    if small is not None:
        out_shapes.append(jax.ShapeDtypeStruct((4,) + small.shape, small.dtype))
    res = _comm_call(f"exchange_final{layer}", body, ins_all, out_shapes, n + 8, 1)
    return (res[:n], res[n]) if small is not None else (res[:n], None)


def _pad_in_cols(w):
    z = lambda n: jnp.zeros(w.shape[:-1] + (n,), w.dtype)
    return jnp.concatenate([w[..., :KR_OFF], z(64), w[..., KR_OFF:KR_OFF + MLA_ROPE], z(32), w[..., KR_OFF + MLA_ROPE:]], axis=-1)


def _pad_heads(w, real):
    k = w.shape[0]
    return jnp.pad(w.reshape(k, MLA_HEADS, real), ((0, 0), (0, 0), (0, LANES - real))).reshape(k, MLA_HEADS * LANES)


def _pad_gain(g, real):
    return jnp.pad(g.reshape(1, real), ((0, 0), (0, LANES - real)))


def layer_weights(full):
    w_ukv = full["w_ukv"].reshape(128, MLA_HEADS, 2, 64)
    two = lambda g: jnp.concatenate([g, g]).reshape(1, LANES)
    return dict(
        norm_g=full["norm_g"].reshape(1, -1), w_in=full["w_in"], conv_w=full["conv_w"], conv_b=full["conv_b"].reshape(1, -1),
        wgx=full["w_gate_x"], bgx=full["b_gate_x"].reshape(LRU_BLOCKS, 1, LANES),
        wga=full["w_gate_a"], bga=full["b_gate_a"].reshape(LRU_BLOCKS, 1, LANES),
        lam=full["lru_lambda"].reshape(1, -1), w_lru_o=full["w_lru_o"],
        cq_norm_g=full["cq_norm_g"].reshape(1, -1), ckv_norm_g=full["ckv_norm_g"].reshape(1, -1),
        wq=_pad_heads(full["w_uq"], MLA_QK), wk=_pad_heads(w_ukv[:, :, 0].reshape(128, 512), 64),
        wv=w_ukv[:, :, 1].reshape(128, 512),
        gq=_pad_gain(full["mla_q_norm_g"], MLA_QK), gk=_pad_gain(full["mla_k_norm_g"], MLA_QK),
        w_mla_o=full["w_mla_o"], gq2=two(full["dil_q_norm_g"]), gk2=two(full["dil_k_norm_g"]),
        w_dil_o=full["w_dil_o"], b_merge=full["b_merge"].reshape(1, -1), w_out=full["w_out"],
    )


def layer_fwd(x, w, tabs, gather=None):
    tabs, dtabs = tabs
    h, ht = rmsnorm_fwd(x, w["norm_g"])
    z_lru = mm_nn("in_proj_lru", h, w["w_in"], n=2048, n_off=G_LRU, tm=1024)
    z_mla = mm_nn("in_proj_mla", h, w["w_in"], n=1024, n_off=G_MLA, tm=1024)
    z_dil = mm_nn("in_proj_dil", h, w["w_in"], n=5120, n_off=G_DIL, tm=1024)
    z_mrg = mm_nn("in_proj_mrg", h, w["w_in"], n=3072, n_off=G_MRG, tm=1024)
    rest_hook = None if gather is None else gather_hook(gather[0][1:], gather[1])
    hs, y_lru, yt_lru, *ga_rest = lru_fwd(z_lru, w["conv_w"], w["conv_b"], w["wgx"], w["bgx"], w["wga"], w["bga"], w["lam"],
                                          comm=rest_hook)
    qm, km, vm, vtm = mla_prep_fwd(z_mla, tabs, w)
    o_mla, y_mla, yt_mla, lse_mla, *ga_first = mla_attn_fwd(qm, km, vtm, z_mla,
                                                            gather=None if gather is None else (gather[0][:1], gather[1]))
    gathered = ga_first + ga_rest
    os_, lses, dil_rm = [], [], []
    for gi, (_, dil) in enumerate(DIL_GROUPS):
        qkv = dil_prep_fwd(z_dil, dtabs[gi], w["gq2"], w["gk2"], gi)
        o, lse = dil_attn_fwd(f"dil_attn_fwd_{dil}", *qkv, dil)
        os_.append(o)
        lses.append(lse)
        dil_rm.append(qkv)
    y_dil, o_dil, lse_dil, yt_dil = dil_combine(os_, lses, z_dil)
    ps = [mm_nn("proj_lru", y_lru, w["w_lru_o"]), mm_nn("proj_mla", y_mla, w["w_mla_o"]), mm_nn("proj_dil", y_dil, w["w_dil_o"])]
    merged, merged_t = merge_fwd(ps, z_mrg, w["b_merge"])
    out = mm_nn("out_proj", merged, w["w_out"], add=x)
    res = dict(x=x, ht=ht, z_lru=z_lru, z_mla=z_mla, z_dil=z_dil, z_mrg=z_mrg, hs=hs, yt_lru=yt_lru, qm=qm, km=km, vm=vm, o_mla=o_mla,
               yt_mla=yt_mla, lse_mla=lse_mla, dil_rm=dil_rm, yt_dil=yt_dil, o_dil=o_dil, lse_dil=lse_dil, ps=ps, merged_t=merged_t)
    return out, res, gathered


def layer_bwd(dout, r, w, tabs, prev=None, own=None):
    tabs, dtabs = tabs
    g = {}
    dmerged = mm_nt("out_proj_dx", dout, w["w_out"])
    g["w_out"] = mm_nn("out_proj_dw", r["merged_t"], dout, tm=1024, tn=512, tk=2048)
    dp0, dp1, dp2, dzm0, dzm1, dzm2, db0, db1, db2 = merge_bwd(dmerged, r["ps"], r["z_mrg"], w["b_merge"])
    g["b_merge"] = jnp.concatenate([db0, db1, db2], axis=1).reshape(-1)
    dy_lru = mm_nt("proj_lru_dx", dp0, w["w_lru_o"])
    dy_mla = mm_nt("proj_mla_dx", dp1, w["w_mla_o"])
    dy_dil = mm_nt("proj_dil_dx", dp2, w["w_dil_o"])
    g["w_lru_o"] = mm_nn("proj_lru_dw", r["yt_lru"], dp0, **DW_TILES)
    g["w_mla_o"] = mm_nn("proj_mla_dw", r["yt_mla"], dp1, **DW_TILES)
    g["w_dil_o"] = mm_nn("proj_dil_dw", r["yt_dil"], dp2, **DW_TILES)
    dzx, dzg_lru, dwgx, dbgx, dwga, dbga, dlam, dcw, dcb, *la_prev = lru_bwd(
        r["z_lru"], r["hs"], dy_lru, w["conv_w"], w["conv_b"], w["wgx"], w["bgx"], w["wga"], w["bga"], w["lam"],
        comm=None if prev is None else sibling_hook(prev[0], prev[1]))
    chips = None if prev is None else (prev[2](la_prev), prev[1])
    g.update(w_gate_x=dwgx, b_gate_x=dbgx.reshape(LRU_BLOCKS, LANES), w_gate_a=dwga, b_gate_a=dbga.reshape(LRU_BLOCKS, LANES),
             lru_lambda=dlam.reshape(-1), conv_w=dcw, conv_b=dcb.reshape(-1))
    do_m, dzg_mla, dd_m = mla_gate_bwd(dy_mla, r["o_mla"], r["z_mla"])
    dq_m, dk_m, dv_m, *lb_prev = mla_attn_bwd(r["qm"], r["km"], r["vm"], do_m, r["lse_mla"], dd_m, chips=chips)
    dz_mla3, dg_cq, dg_ckv, dwq, dwk, dwv, dgq, dgk = mla_prep_bwd(r["z_mla"], tabs, w, dq_m, dk_m, dv_m)
    g.update(cq_norm_g=dg_cq.reshape(-1), ckv_norm_g=dg_ckv.reshape(-1), mla_q_norm_g=dgq[0, :MLA_QK], mla_k_norm_g=dgk[0, :MLA_QK])
    g["w_uq"] = dwq.reshape(256, MLA_HEADS, LANES)[:, :, :MLA_QK].reshape(256, MLA_HEADS * MLA_QK)
    g["w_ukv"] = jnp.concatenate([dwk.reshape(128, MLA_HEADS, LANES)[:, :, :64], dwv.reshape(128, MLA_HEADS, 64)], axis=2).reshape(128, 1024)
    dzg_dil, do1, dd1, do2, dd2, l2, do3, dd3, l3 = dil_gate_bwd(dy_dil, r["o_dil"], r["z_dil"], r["lse_dil"])
    stats = [(do1, r["lse_dil"], dd1), (do2, l2, dd2), (do3, l3, dd3)]
    dzq, dzk, dzv, dgq2, dgk2 = [], [], [], [], []
    for gi, (_, dil) in enumerate(DIL_GROUPS):
        dq, dk, dv = dil_attn_bwd(f"dil_attn_bwd_{dil}", *r["dil_rm"][gi], *stats[gi], dil)
        parts = dil_prep_bwd(r["z_dil"], dtabs[gi], w["gq2"], w["gk2"], dq, dk, dv, gi)
        for acc, part in zip((dzq, dzk, dzv, dgq2, dgk2), parts):
            acc.append(part)
    g.update(dil_q_norm_g=sum(dgq2)[0, :DIL_HD], dil_k_norm_g=sum(dgk2)[0, :DIL_HD])
    dz = jnp.concatenate([dzx, dzg_lru, dz_mla3, dzg_mla] + dzq + dzk + dzv + [dzg_dil, dzm0, dzm1, dzm2], axis=1)
    g["w_in"] = in_proj_dw(r["ht"], dz)
    own_hook = None if own is None else chips_hook(*own(g))
    out = mm_nt("in_proj_dx", dz, w["w_in"], tm=1024, tn=1024, tk=IN_PAD // 4, comm=own_hook)
    dh, lb_own = (out, []) if own_hook is None else (out[0], out[1:])
    dx, dng = rmsnorm_bwd(r["x"], dh, dout, w["norm_g"])
    g["norm_g"] = dng.reshape(-1)
    return dx, g, dict(la_prev=la_prev, lb_prev=lb_prev, lb_own=lb_own)


def local_step(x, positions, target, full0, full1=None, gather1=None, reduce=None):
    tabs = rope_tables(positions.reshape(-1, 1))
    tabs = (tabs, [dil_tables(tabs, gi) for gi in range(len(DIL_GROUPS))])
    ws, ress = [], []
    for l in range(2):
        if l == 0:
            ws.append(layer_weights(full0))
            x, res, gathered = layer_fwd(x, ws[0], tabs, gather=None if gather1 is None else (gather1[0], 1))
        else:
            ws.append(layer_weights(full1 if gather1 is None else gather1[1](gathered)))
            x, res, _ = layer_fwd(x, ws[1], tabs)
        ress.append(res)
    dy, loss = loss_head(x, target)
    dy, grads1, _ = layer_bwd(dy, ress[1], ws[1], tabs)
    prev1, own0 = (None, None) if reduce is None else (reduce[0](grads1), reduce[1])
    dy, grads0, landed = layer_bwd(dy, ress[0], ws[0], tabs, prev=prev1, own=own0)
    return loss, dy, [grads0, grads1], landed


WEIGHTS = ["norm_g", "w_in", "conv_w", "conv_b", "w_gate_x", "b_gate_x", "w_gate_a", "b_gate_a", "lru_lambda", "w_lru_o", "cq_norm_g",
           "ckv_norm_g", "w_uq", "w_ukv", "mla_q_norm_g", "mla_k_norm_g", "w_mla_o", "dil_q_norm_g", "dil_k_norm_g", "w_dil_o", "b_merge",
           "w_out"]
SHARDED = {"w_in": 2, "conv_w": 2, "w_lru_o": 1, "w_uq": 2, "w_ukv": 2, "w_mla_o": 2, "w_dil_o": 2, "w_out": 1}
REPLICATED = [n for n in WEIGHTS if n not in SHARDED]
SMALL_ROWS = 144


def kernel(x, positions, norm_g, w_in, conv_w, conv_b, w_gate_x, b_gate_x, w_gate_a, b_gate_a, lru_lambda, w_lru_o, cq_norm_g, ckv_norm_g, w_uq, w_ukv, mla_q_norm_g, mla_k_norm_g, w_mla_o, dil_q_norm_g, dil_k_norm_g, w_dil_o, b_merge, w_out, loss_target, m_norm_g, m_w_in, m_conv_w, m_conv_b, m_w_gate_x, m_b_gate_x, m_w_gate_a, m_b_gate_a, m_lru_lambda, m_w_lru_o, m_cq_norm_g, m_ckv_norm_g, m_w_uq, m_w_ukv, m_mla_q_norm_g, m_mla_k_norm_g, m_w_mla_o, m_dil_q_norm_g, m_dil_k_norm_g, m_w_dil_o, m_b_merge, m_w_out, v_norm_g, v_w_in, v_conv_w, v_conv_b, v_w_gate_x, v_b_gate_x, v_w_gate_a, v_b_gate_a, v_lru_lambda, v_w_lru_o, v_cq_norm_g, v_ckv_norm_g, v_w_uq, v_w_ukv, v_mla_q_norm_g, v_mla_k_norm_g, v_w_mla_o, v_dil_q_norm_g, v_dil_k_norm_g, v_w_dil_o, v_b_merge, v_w_out):
    args = locals()
    w = {n: args[n] for n in WEIGHTS}
    m = {n: args["m_" + n] for n in WEIGHTS}
    v = {n: args["v_" + n] for n in WEIGHTS}
    my_c = lax.axis_index("c").astype(jnp.int32)
    my_s = (2 * lax.axis_index("x") + lax.axis_index("y")).astype(jnp.int32)
    c_idx = my_c.reshape(1)
    s_idx = my_s.reshape(1)

    names = list(SHARDED)
    wire = [[w[n][l] if n == "conv_w" else w[n][l].astype(BF16) for n in names] for l in range(2)]

    def assemble(l, gathered):
        full = {n: w[n][l] for n in REPLICATED}
        for n, ga, own in zip(names, gathered, wire[l]):
            parts = [jnp.where(my_s == s, own, ga[s]) for s in range(4)]
            if n == "w_in":
                parts = [_pad_in_cols(parts[0])] + parts[1:]
            full[n] = jnp.concatenate(parts, axis=SHARDED[n] - 1)
        return full

    full0 = assemble(0, allgather_layer(wire[0], 0))

    def shards_of(grads_l):
        gs = []
        for n in names:
            g_ = grads_l[n]
            if n == "w_in":
                gs.append(g_.reshape(4, IN_WIDTH // 4, D_MODEL))
            else:
                parts = jnp.stack(jnp.split(g_, 4, axis=SHARDED[n] - 1))
                gs.append(parts.reshape(4, -1, parts.shape[-1]))
        return gs

    def pair_sums(l, gs, la, tags):
        dts = [BF16 if g_.size >= 2**19 and t != "small" else F32 for g_, t in zip(gs, tags)]
        return [sum_pair(f"sum_pair{l}_{t}", g_, l_, dt) for t, g_, l_, dt in zip(tags, gs, la, dts)]

    def chip_sums(l, gs, la, lb, tags):
        return [sum_chips(f"sum_chips{l}_{t}", g_, l_, b_, s_idx) for t, g_, l_, b_ in zip(tags, gs, la, lb)]

    stash = {}

    def prev1(grads1):
        stash["gs1"] = shards_of(grads1)
        return stash["gs1"], 1, lambda la: pair_sums(1, stash["gs1"], la, names)

    def own0(grads0):
        stash["gs0"] = shards_of(grads0)
        stash["la0"] = exchange_sibling(stash["gs0"], 0)
        return pair_sums(0, stash["gs0"], stash["la0"], names), 0

    loss, grad_x, grads, landed = local_step(x[0], positions[0], loss_target[0], full0,
                                             gather1=(wire[1], functools.partial(assemble, 1)), reduce=(prev1, own0))
    loss = lax.psum(loss, ("x", "y", "c"))
    reduced1 = chip_sums(1, stash["gs1"], landed["la_prev"], landed["lb_prev"], names)
    received1, _ = exchange_final(reduced1, 1)
    reduced0 = chip_sums(0, stash["gs0"], stash["la0"], landed["lb_own"], names)

    flat = jnp.concatenate([jnp.stack([grads[0][n], grads[1][n]]).reshape(-1) for n in REPLICATED])
    small = jnp.pad(flat, (0, 4 * SMALL_ROWS * 1024 - flat.size)).reshape(4, SMALL_ROWS, 1024)
    small_piece = small_sum(small, exchange_small(small), s_idx)
    received0, small_all = exchange_final(reduced0, 0, small=small_piece)

    g_local, delta, new_m, new_v = {}, {}, {}, {}
    for i, n in enumerate(names):
        shp = w[n].shape
        if n == "w_in":
            as3 = lambda a: a.transpose(0, 2, 1)
            back = lambda o: o.transpose(0, 2, 1)
        else:
            as3 = lambda a, rc=reduced0[i].shape: a.reshape((2,) + rc)
            back = lambda o, shp=shp: o.reshape(shp)
        outs = adamw_pair(f"adamw_{n}", as3(w[n]), (reduced0[i], reduced1[i]), (received0[i], received1[i]), as3(m[n]), as3(v[n]),
                          c_idx)
        g_local[n], delta[n], new_m[n], new_v[n] = [back(o) for o in outs]
    flat = small_all.reshape(-1)
    off = 0
    for n in REPLICATED:
        g_local[n] = flat[off:off + w[n].size].reshape(w[n].shape)
        off += w[n].size
        delta[n], new_m[n], new_v[n] = adamw(f"adamw_{n}", w[n], g_local[n], m[n], v[n])
    return (loss, grad_x[None], *[g_local[n] for n in WEIGHTS], *[delta[n] for n in WEIGHTS], *[new_m[n] for n in WEIGHTS],
            *[new_v[n] for n in WEIGHTS])
```
